```python
import math
import jax, jax.numpy as jnp
from jax import lax
import numpy as np

D_MODEL = 1024
BATCH = 8
SEQ = 8192
DEPTH = 1

SSD_D_INNER = D_MODEL
SSD_HEADDIM = 64
SSD_N_HEADS = SSD_D_INNER // SSD_HEADDIM
SSD_N_GROUPS = 4
SSD_HPG = SSD_N_HEADS // SSD_N_GROUPS
SSD_D_STATE = 128
SSD_CONV = 4
SSD_CHUNK = 128
SSD_CONV_DIM = SSD_D_INNER + 2 * SSD_N_GROUPS * SSD_D_STATE
S5_WIDTH = D_MODEL // 2
S5_GROUP = 16
S5_N_GROUPS = S5_WIDTH // S5_GROUP
S5_STATE = 64
D_FF = 4 * D_MODEL
N_BRANCHES = 2
OFF_Z = 0
OFF_XBC = OFF_Z + SSD_D_INNER
OFF_DT = OFF_XBC + SSD_CONV_DIM
OFF_U = OFF_DT + SSD_N_HEADS
OFF_G = OFF_U + S5_WIDTH
D_IN_PROJ = OFF_G + N_BRANCHES * D_MODEL
EPS = 1e-6
DT_MIN = 1e-3
DT_MAX = 1e-1

kernel_name = "hybrid_ssd_s5_gated_block"


def rms_norm(x, g):
    xf = x.astype(jnp.float32)
    y = xf * lax.rsqrt(jnp.mean(xf * xf, axis=-1, keepdims=True) + EPS)
    return y.astype(x.dtype) * g


def causal_dwconv(u, w, b):
    y = lax.conv_general_dilated(
        u, w.astype(u.dtype)[:, None, :], window_strides=(1,),
        padding=[(SSD_CONV - 1, 0)], dimension_numbers=('NWC', 'WIO', 'NWC'),
        feature_group_count=u.shape[-1])
    return y + b


def ssd_chunked(xh, dt, a, bm, cm):
    bsz, seqlen = xh.shape[0], xh.shape[1]
    nc = seqlen // SSD_CHUNK
    q, g, r, p, n = SSD_CHUNK, SSD_N_GROUPS, SSD_HPG, SSD_HEADDIM, SSD_D_STATE
    x = xh.astype(jnp.float32).reshape(bsz, nc, q, g, r, p)
    dtc = dt.astype(jnp.float32).reshape(bsz, nc, q, g, r)
    bc = bm.astype(jnp.float32).reshape(bsz, nc, q, g, n)
    cc = cm.astype(jnp.float32).reshape(bsz, nc, q, g, n)
    la = dtc * a.astype(jnp.float32).reshape(g, r)
    la_cum = jnp.cumsum(la, axis=2)
    xdt = x * dtc[..., None]
    seg = la_cum[:, :, :, None] - la_cum[:, :, None]
    causal = jnp.tril(jnp.ones((q, q), dtype=bool))[:, :, None, None]
    decay = jnp.exp(jnp.where(causal, seg, -jnp.inf))
    cb = jnp.einsum('bctgn,bcsgn->bctsg', cc, bc)
    y_diag = jnp.einsum('bctsg,bctsgr,bcsgrp->bctgrp', cb, decay, xdt)
    decay_end = jnp.exp(la_cum[:, :, -1:] - la_cum)
    states = jnp.einsum('bcsgn,bcsgr,bcsgrp->bcgrpn', bc, decay_end, xdt)
    chunk_decay = jnp.exp(la_cum[:, :, -1])

    def step(carry, inp):
        st, dec = inp
        return carry * dec[..., None, None] + st, carry

    init = jnp.zeros((bsz, g, r, p, n), jnp.float32)
    _, prev = lax.scan(step, init, (jnp.moveaxis(states, 1, 0), jnp.moveaxis(chunk_decay, 1, 0)))
    prev = jnp.moveaxis(prev, 0, 1)
    y_off = jnp.einsum('bctgn,bcgrpn,bctgr->bctgrp', cc, prev, jnp.exp(la_cum))
    return (y_diag + y_off).reshape(bsz, seqlen, SSD_N_HEADS, p)


def s5_mixer(u, a_re, a_im, log_dt, b_re, b_im, c_re, c_im, d):
    bsz, seqlen = u.shape[0], u.shape[1]
    ug = u.astype(jnp.float32).reshape(bsz, seqlen, S5_N_GROUPS, S5_GROUP)
    dt = jnp.exp(log_dt.astype(jnp.float32))[:, None]
    a_re = a_re.astype(jnp.float32)
    a_im = a_im.astype(jnp.float32)
    mag = jnp.exp(a_re * dt)
    ab_re = mag * jnp.cos(a_im * dt)
    ab_im = mag * jnp.sin(a_im * dt)
    den = a_re * a_re + a_im * a_im
    nr = ab_re - 1.0
    ni = ab_im
    coef_re = (nr * a_re + ni * a_im) / den
    coef_im = (ni * a_re - nr * a_im) / den
    bb_re = coef_re[..., None] * b_re - coef_im[..., None] * b_im
    bb_im = coef_re[..., None] * b_im + coef_im[..., None] * b_re
    bu_re = jnp.einsum('blgk,gpk->blgp', ug, bb_re)
    bu_im = jnp.einsum('blgk,gpk->blgp', ug, bb_im)
    a_seq_re = jnp.broadcast_to(ab_re, (1, seqlen, S5_N_GROUPS, S5_STATE))
    a_seq_im = jnp.broadcast_to(ab_im, (1, seqlen, S5_N_GROUPS, S5_STATE))

    def combine(e1, e2):
        a1r, a1i, b1r, b1i = e1
        a2r, a2i, b2r, b2i = e2
        return (a2r * a1r - a2i * a1i,
                a2r * a1i + a2i * a1r,
                a2r * b1r - a2i * b1i + b2r,
                a2r * b1i + a2i * b1r + b2i)

    _, _, s_re, s_im = lax.associative_scan(combine, (a_seq_re, a_seq_im, bu_re, bu_im), axis=1)
    y = jnp.einsum('blgp,gkp->blgk', s_re, c_re) - jnp.einsum('blgp,gkp->blgk', s_im, c_im)
    return y.reshape(bsz, seqlen, S5_WIDTH) + d * u


def _fwd_setup_inputs(seed: int = 0) -> dict:
    key = jax.random.key(seed)
    ks = jax.random.split(key, 32)
    f32 = jnp.float32
    nrm = lambda k, shp, s: jax.random.normal(k, shp, f32) * s
    x = jax.random.normal(ks[0], (BATCH, SEQ, D_MODEL), f32)
    norm_mix_g = 1.0 + nrm(ks[1], (D_MODEL,), 0.02)
    w_in = nrm(ks[2], (D_MODEL, D_IN_PROJ), D_MODEL ** -0.5)
    conv_w = nrm(ks[3], (SSD_CONV, SSD_CONV_DIM), SSD_CONV ** -0.5)
    conv_b = nrm(ks[4], (SSD_CONV_DIM,), 0.02)
    dt0 = jnp.exp(jax.random.uniform(ks[5], (SSD_N_HEADS,), f32, math.log(DT_MIN), math.log(DT_MAX)))
    dt_bias = dt0 + jnp.log(-jnp.expm1(-dt0))
    a_log = jnp.log(jax.random.uniform(ks[6], (SSD_N_HEADS,), f32, 1.0, 16.0))
    d_ssd = 1.0 + nrm(ks[7], (SSD_N_HEADS,), 0.02)
    ssd_norm_g = 1.0 + nrm(ks[8], (SSD_D_INNER,), 0.02)
    s5_a_re = -0.5 + nrm(ks[9], (S5_N_GROUPS, S5_STATE), 0.01)
    s5_a_im = (math.pi * jnp.arange(S5_STATE, dtype=f32))[None, :] + nrm(ks[10], (S5_N_GROUPS, S5_STATE), 0.01)
    s5_log_dt = jax.random.uniform(ks[11], (S5_N_GROUPS,), f32, math.log(DT_MIN), math.log(DT_MAX))
    s5_b_re = nrm(ks[12], (S5_N_GROUPS, S5_STATE, S5_GROUP), (2 * S5_GROUP) ** -0.5)
    s5_b_im = nrm(ks[13], (S5_N_GROUPS, S5_STATE, S5_GROUP), (2 * S5_GROUP) ** -0.5)
    s5_c_re = nrm(ks[14], (S5_N_GROUPS, S5_GROUP, S5_STATE), (2 * S5_STATE) ** -0.5)
    s5_c_im = nrm(ks[15], (S5_N_GROUPS, S5_GROUP, S5_STATE), (2 * S5_STATE) ** -0.5)
    s5_d = nrm(ks[16], (S5_WIDTH,), 1.0)
    s5_glu_w = nrm(ks[17], (S5_WIDTH, S5_WIDTH), S5_WIDTH ** -0.5)
    s5_glu_b = nrm(ks[18], (S5_WIDTH,), 0.02)
    w_branch = jnp.concatenate([
        nrm(ks[19], (SSD_D_INNER, D_MODEL), SSD_D_INNER ** -0.5),
        nrm(ks[20], (S5_WIDTH, D_MODEL), S5_WIDTH ** -0.5)], axis=0)
    w_out = nrm(ks[21], (D_MODEL, D_MODEL), D_MODEL ** -0.5)
    norm_mlp_g = 1.0 + nrm(ks[22], (D_MODEL,), 0.02)
    w_mlp_in = nrm(ks[23], (D_MODEL, D_FF), D_MODEL ** -0.5)
    w_mlp_out = nrm(ks[24], (D_FF, D_MODEL), D_FF ** -0.5)
    norm_final_g = 1.0 + nrm(ks[25], (D_MODEL,), 0.02)
    return {"x": x, "norm_mix_g": norm_mix_g, "w_in": w_in, "conv_w": conv_w, "conv_b": conv_b,
            "dt_bias": dt_bias, "a_log": a_log, "d_ssd": d_ssd, "ssd_norm_g": ssd_norm_g,
            "s5_a_re": s5_a_re, "s5_a_im": s5_a_im, "s5_log_dt": s5_log_dt,
            "s5_b_re": s5_b_re, "s5_b_im": s5_b_im, "s5_c_re": s5_c_re, "s5_c_im": s5_c_im,
            "s5_d": s5_d, "s5_glu_w": s5_glu_w, "s5_glu_b": s5_glu_b,
            "w_branch": w_branch, "w_out": w_out, "norm_mlp_g": norm_mlp_g,
            "w_mlp_in": w_mlp_in, "w_mlp_out": w_mlp_out, "norm_final_g": norm_final_g}


def _fwd_reference(x, norm_mix_g, w_in, conv_w, conv_b, dt_bias, a_log, d_ssd, ssd_norm_g,
              s5_a_re, s5_a_im, s5_log_dt, s5_b_re, s5_b_im, s5_c_re, s5_c_im, s5_d,
              s5_glu_w, s5_glu_b, w_branch, w_out, norm_mlp_g, w_mlp_in, w_mlp_out,
              norm_final_g):
    bsz, seqlen = x.shape[0], x.shape[1]
    for _ in range(DEPTH):
        h = rms_norm(x, norm_mix_g)
        proj = h @ w_in
        z = proj[..., OFF_Z:OFF_XBC]
        xbc = proj[..., OFF_XBC:OFF_DT]
        dt_raw = proj[..., OFF_DT:OFF_U]
        u5 = proj[..., OFF_U:OFF_G]
        gates = jax.nn.sigmoid(proj[..., OFF_G:].astype(jnp.float32)).reshape(bsz, seqlen, N_BRANCHES, D_MODEL)

        xbc = jax.nn.silu(causal_dwconv(xbc, conv_w, conv_b))
        xs = xbc[..., :SSD_D_INNER].reshape(bsz, seqlen, SSD_N_HEADS, SSD_HEADDIM)
        bm = xbc[..., SSD_D_INNER:SSD_D_INNER + SSD_N_GROUPS * SSD_D_STATE].reshape(bsz, seqlen, SSD_N_GROUPS, SSD_D_STATE)
        cm = xbc[..., SSD_D_INNER + SSD_N_GROUPS * SSD_D_STATE:].reshape(bsz, seqlen, SSD_N_GROUPS, SSD_D_STATE)
        dt = jax.nn.softplus(dt_raw.astype(jnp.float32) + dt_bias)
        a = -jnp.exp(a_log.astype(jnp.float32))
        y_a = ssd_chunked(xs, dt, a, bm, cm) + d_ssd[:, None] * xs
        y_a = y_a.reshape(bsz, seqlen, SSD_D_INNER) * jax.nn.silu(z)
        y_a = rms_norm(y_a.reshape(bsz, seqlen, SSD_N_GROUPS, SSD_D_INNER // SSD_N_GROUPS), 1.0)
        y_a = y_a.reshape(bsz, seqlen, SSD_D_INNER) * ssd_norm_g

        y_b = s5_mixer(u5, s5_a_re, s5_a_im, s5_log_dt, s5_b_re, s5_b_im, s5_c_re, s5_c_im, s5_d)
        y_b = jax.nn.gelu(y_b)
        y_b = y_b * jax.nn.sigmoid(y_b @ s5_glu_w + s5_glu_b)

        p_a = y_a @ w_branch[:SSD_D_INNER]
        p_b = y_b @ w_branch[SSD_D_INNER:]
        merged = gates[:, :, 0] * p_a + gates[:, :, 1] * p_b
        x = x + (merged @ w_out).astype(x.dtype)

        h2 = rms_norm(x, norm_mlp_g)
        x = x + (jnp.square(jax.nn.relu(h2 @ w_mlp_in)) @ w_mlp_out).astype(x.dtype)
    return rms_norm(x, norm_final_g)


import jax as _jax
import jax.numpy as _jnp

TWIN_FORMAT = 'train_step'
FWD_PARAMS = ['x', 'norm_mix_g', 'w_in', 'conv_w', 'conv_b', 'dt_bias', 'a_log', 'd_ssd', 'ssd_norm_g', 's5_a_re', 's5_a_im', 's5_log_dt', 's5_b_re', 's5_b_im', 's5_c_re', 's5_c_im', 's5_d', 's5_glu_w', 's5_glu_b', 'w_branch', 'w_out', 'norm_mlp_g', 'w_mlp_in', 'w_mlp_out', 'norm_final_g']
TWIN_WEIGHTS = ['norm_mix_g', 'w_in', 'conv_w', 'conv_b', 'dt_bias', 'a_log', 'd_ssd', 'ssd_norm_g', 's5_a_re', 's5_a_im', 's5_log_dt', 's5_b_re', 's5_b_im', 's5_c_re', 's5_c_im', 's5_d', 's5_glu_w', 's5_glu_b', 'w_branch', 'w_out', 'norm_mlp_g', 'w_mlp_in', 'w_mlp_out', 'norm_final_g']
TWIN_DIFF_INPUT = 'x'
TWIN_INPUTS = ['x', 'norm_mix_g', 'w_in', 'conv_w', 'conv_b', 'dt_bias', 'a_log', 'd_ssd', 'ssd_norm_g', 's5_a_re', 's5_a_im', 's5_log_dt', 's5_b_re', 's5_b_im', 's5_c_re', 's5_c_im', 's5_d', 's5_glu_w', 's5_glu_b', 'w_branch', 'w_out', 'norm_mlp_g', 'w_mlp_in', 'w_mlp_out', 'norm_final_g', 'loss_target', 'm_norm_mix_g', 'm_w_in', 'm_conv_w', 'm_conv_b', 'm_dt_bias', 'm_a_log', 'm_d_ssd', 'm_ssd_norm_g', 'm_s5_a_re', 'm_s5_a_im', 'm_s5_log_dt', 'm_s5_b_re', 'm_s5_b_im', 'm_s5_c_re', 'm_s5_c_im', 'm_s5_d', 'm_s5_glu_w', 'm_s5_glu_b', 'm_w_branch', 'm_w_out', 'm_norm_mlp_g', 'm_w_mlp_in', 'm_w_mlp_out', 'm_norm_final_g', 'v_norm_mix_g', 'v_w_in', 'v_conv_w', 'v_conv_b', 'v_dt_bias', 'v_a_log', 'v_d_ssd', 'v_ssd_norm_g', 'v_s5_a_re', 'v_s5_a_im', 'v_s5_log_dt', 'v_s5_b_re', 'v_s5_b_im', 'v_s5_c_re', 'v_s5_c_im', 'v_s5_d', 'v_s5_glu_w', 'v_s5_glu_b', 'v_w_branch', 'v_w_out', 'v_norm_mlp_g', 'v_w_mlp_in', 'v_w_mlp_out', 'v_norm_final_g']
TWIN_OUTPUTS = ['loss', 'grad_x', 'grad_norm_mix_g', 'grad_w_in', 'grad_conv_w', 'grad_conv_b', 'grad_dt_bias', 'grad_a_log', 'grad_d_ssd', 'grad_ssd_norm_g', 'grad_s5_a_re', 'grad_s5_a_im', 'grad_s5_log_dt', 'grad_s5_b_re', 'grad_s5_b_im', 'grad_s5_c_re', 'grad_s5_c_im', 'grad_s5_d', 'grad_s5_glu_w', 'grad_s5_glu_b', 'grad_w_branch', 'grad_w_out', 'grad_norm_mlp_g', 'grad_w_mlp_in', 'grad_w_mlp_out', 'grad_norm_final_g', 'delta_norm_mix_g', 'delta_w_in', 'delta_conv_w', 'delta_conv_b', 'delta_dt_bias', 'delta_a_log', 'delta_d_ssd', 'delta_ssd_norm_g', 'delta_s5_a_re', 'delta_s5_a_im', 'delta_s5_log_dt', 'delta_s5_b_re', 'delta_s5_b_im', 'delta_s5_c_re', 'delta_s5_c_im', 'delta_s5_d', 'delta_s5_glu_w', 'delta_s5_glu_b', 'delta_w_branch', 'delta_w_out', 'delta_norm_mlp_g', 'delta_w_mlp_in', 'delta_w_mlp_out', 'delta_norm_final_g', 'new_m_norm_mix_g', 'new_m_w_in', 'new_m_conv_w', 'new_m_conv_b', 'new_m_dt_bias', 'new_m_a_log', 'new_m_d_ssd', 'new_m_ssd_norm_g', 'new_m_s5_a_re', 'new_m_s5_a_im', 'new_m_s5_log_dt', 'new_m_s5_b_re', 'new_m_s5_b_im', 'new_m_s5_c_re', 'new_m_s5_c_im', 'new_m_s5_d', 'new_m_s5_glu_w', 'new_m_s5_glu_b', 'new_m_w_branch', 'new_m_w_out', 'new_m_norm_mlp_g', 'new_m_w_mlp_in', 'new_m_w_mlp_out', 'new_m_norm_final_g', 'new_v_norm_mix_g', 'new_v_w_in', 'new_v_conv_w', 'new_v_conv_b', 'new_v_dt_bias', 'new_v_a_log', 'new_v_d_ssd', 'new_v_ssd_norm_g', 'new_v_s5_a_re', 'new_v_s5_a_im', 'new_v_s5_log_dt', 'new_v_s5_b_re', 'new_v_s5_b_im', 'new_v_s5_c_re', 'new_v_s5_c_im', 'new_v_s5_d', 'new_v_s5_glu_w', 'new_v_s5_glu_b', 'new_v_w_branch', 'new_v_w_out', 'new_v_norm_mlp_g', 'new_v_w_mlp_in', 'new_v_w_mlp_out', 'new_v_norm_final_g']
TWIN_LEAF_KINDS = {'loss': 'loss', 'grad_x': 'grad_x', 'grad_norm_mix_g': 'grad_w', 'grad_w_in': 'grad_w', 'grad_conv_w': 'grad_w', 'grad_conv_b': 'grad_w', 'grad_dt_bias': 'grad_w', 'grad_a_log': 'grad_w', 'grad_d_ssd': 'grad_w', 'grad_ssd_norm_g': 'grad_w', 'grad_s5_a_re': 'grad_w', 'grad_s5_a_im': 'grad_w', 'grad_s5_log_dt': 'grad_w', 'grad_s5_b_re': 'grad_w', 'grad_s5_b_im': 'grad_w', 'grad_s5_c_re': 'grad_w', 'grad_s5_c_im': 'grad_w', 'grad_s5_d': 'grad_w', 'grad_s5_glu_w': 'grad_w', 'grad_s5_glu_b': 'grad_w', 'grad_w_branch': 'grad_w', 'grad_w_out': 'grad_w', 'grad_norm_mlp_g': 'grad_w', 'grad_w_mlp_in': 'grad_w', 'grad_w_mlp_out': 'grad_w', 'grad_norm_final_g': 'grad_w', 'delta_norm_mix_g': 'delta_w', 'delta_w_in': 'delta_w', 'delta_conv_w': 'delta_w', 'delta_conv_b': 'delta_w', 'delta_dt_bias': 'delta_w', 'delta_a_log': 'delta_w', 'delta_d_ssd': 'delta_w', 'delta_ssd_norm_g': 'delta_w', 'delta_s5_a_re': 'delta_w', 'delta_s5_a_im': 'delta_w', 'delta_s5_log_dt': 'delta_w', 'delta_s5_b_re': 'delta_w', 'delta_s5_b_im': 'delta_w', 'delta_s5_c_re': 'delta_w', 'delta_s5_c_im': 'delta_w', 'delta_s5_d': 'delta_w', 'delta_s5_glu_w': 'delta_w', 'delta_s5_glu_b': 'delta_w', 'delta_w_branch': 'delta_w', 'delta_w_out': 'delta_w', 'delta_norm_mlp_g': 'delta_w', 'delta_w_mlp_in': 'delta_w', 'delta_w_mlp_out': 'delta_w', 'delta_norm_final_g': 'delta_w', 'new_m_norm_mix_g': 'new_m', 'new_m_w_in': 'new_m', 'new_m_conv_w': 'new_m', 'new_m_conv_b': 'new_m', 'new_m_dt_bias': 'new_m', 'new_m_a_log': 'new_m', 'new_m_d_ssd': 'new_m', 'new_m_ssd_norm_g': 'new_m', 'new_m_s5_a_re': 'new_m', 'new_m_s5_a_im': 'new_m', 'new_m_s5_log_dt': 'new_m', 'new_m_s5_b_re': 'new_m', 'new_m_s5_b_im': 'new_m', 'new_m_s5_c_re': 'new_m', 'new_m_s5_c_im': 'new_m', 'new_m_s5_d': 'new_m', 'new_m_s5_glu_w': 'new_m', 'new_m_s5_glu_b': 'new_m', 'new_m_w_branch': 'new_m', 'new_m_w_out': 'new_m', 'new_m_norm_mlp_g': 'new_m', 'new_m_w_mlp_in': 'new_m', 'new_m_w_mlp_out': 'new_m', 'new_m_norm_final_g': 'new_m', 'new_v_norm_mix_g': 'new_v', 'new_v_w_in': 'new_v', 'new_v_conv_w': 'new_v', 'new_v_conv_b': 'new_v', 'new_v_dt_bias': 'new_v', 'new_v_a_log': 'new_v', 'new_v_d_ssd': 'new_v', 'new_v_ssd_norm_g': 'new_v', 'new_v_s5_a_re': 'new_v', 'new_v_s5_a_im': 'new_v', 'new_v_s5_log_dt': 'new_v', 'new_v_s5_b_re': 'new_v', 'new_v_s5_b_im': 'new_v', 'new_v_s5_c_re': 'new_v', 'new_v_s5_c_im': 'new_v', 'new_v_s5_d': 'new_v', 'new_v_s5_glu_w': 'new_v', 'new_v_s5_glu_b': 'new_v', 'new_v_w_branch': 'new_v', 'new_v_w_out': 'new_v', 'new_v_norm_mlp_g': 'new_v', 'new_v_w_mlp_in': 'new_v', 'new_v_w_mlp_out': 'new_v', 'new_v_norm_final_g': 'new_v'}


def _forward(args):
    return _fwd_reference(*[args[k] for k in FWD_PARAMS])


def _output_shape():
    def fwd():
        inp = _fwd_setup_inputs(0)
        return _fwd_reference(*[inp[k] for k in FWD_PARAMS])
    out = _jax.eval_shape(fwd)
    return out.shape, out.dtype

N_MICROBATCH = 1
ADAM_LR = 0.001
ADAM_B1 = 0.9
ADAM_B2 = 0.999
ADAM_EPS = 1e-08
ADAM_WD = 0.01
ADAM_STEP = 10
PER_EXAMPLE_BATCH_AXIS = {'x': 0, 'loss_target': 0}
SHARED_INPUTS = []
_WEIGHT_DTYPES = {'norm_mix_g': _jnp.float32, 'w_in': _jnp.float32, 'conv_w': _jnp.float32, 'conv_b': _jnp.float32, 'dt_bias': _jnp.float32, 'a_log': _jnp.float32, 'd_ssd': _jnp.float32, 'ssd_norm_g': _jnp.float32, 's5_a_re': _jnp.float32, 's5_a_im': _jnp.float32, 's5_log_dt': _jnp.float32, 's5_b_re': _jnp.float32, 's5_b_im': _jnp.float32, 's5_c_re': _jnp.float32, 's5_c_im': _jnp.float32, 's5_d': _jnp.float32, 's5_glu_w': _jnp.float32, 's5_glu_b': _jnp.float32, 'w_branch': _jnp.float32, 'w_out': _jnp.float32, 'norm_mlp_g': _jnp.float32, 'w_mlp_in': _jnp.float32, 'w_mlp_out': _jnp.float32, 'norm_final_g': _jnp.float32}
MOMENT_SCALE = {'norm_mix_g': 2.130298e-01, 'w_in': 8.910580e-02, 'conv_w': 9.969706e-02, 'conv_b': 1.337491e-01, 'dt_bias': 4.270458e-01, 'a_log': 2.480201e-01, 'd_ssd': 4.956983e-01, 'ssd_norm_g': 1.265273e-01, 's5_a_re': 3.189943e-03, 's5_a_im': 3.430589e-03, 's5_log_dt': 2.581941e+00, 's5_b_re': 2.165404e-03, 's5_b_im': 2.150796e-03, 's5_c_re': 4.455072e-03, 's5_c_im': 4.552136e-03, 's5_d': 7.056687e-02, 's5_glu_w': 1.794128e-02, 's5_glu_b': 2.874163e-02, 'w_branch': 1.112163e-01, 'w_out': 1.408233e-01, 'norm_mlp_g': 2.114167e-01, 'w_mlp_in': 1.062856e-01, 'w_mlp_out': 2.346141e-01, 'norm_final_g': 6.464623e+01}


def _to_microbatches(a, axis):
    t = _jnp.moveaxis(a, axis, 0)
    t = t.reshape((N_MICROBATCH, t.shape[0] // N_MICROBATCH) + t.shape[1:])
    return _jnp.moveaxis(t, 1, axis + 1)


def setup_inputs(seed: int = 0) -> dict:
    inp = _fwd_setup_inputs(seed)
    key = _jax.random.fold_in(_jax.random.key(seed), 7919)
    shape, _ = _output_shape()
    out = dict(inp)
    out["loss_target"] = _jax.random.normal(_jax.random.fold_in(key, 0), shape, _jnp.float32)
    for i, name in enumerate(TWIN_WEIGHTS):
        w = inp[name].astype(_jnp.float32)
        if MOMENT_SCALE is None:
            s = _jnp.sqrt(_jnp.mean(_jnp.square(w)) + 1e-30)
        else:
            s = MOMENT_SCALE[name]
        km, kv = _jax.random.split(_jax.random.fold_in(key, i + 1))
        out[name] = w
        out["m_" + name] = s * _jax.random.normal(km, w.shape, _jnp.float32)
        out["v_" + name] = (s * s) * _jax.random.uniform(kv, w.shape, _jnp.float32, 0.5, 1.5)
    if N_MICROBATCH > 1:
        for name, axis in PER_EXAMPLE_BATCH_AXIS.items():
            out[name] = _to_microbatches(out[name], axis)
    return {'x': out['x'], 'norm_mix_g': out['norm_mix_g'], 'w_in': out['w_in'], 'conv_w': out['conv_w'], 'conv_b': out['conv_b'], 'dt_bias': out['dt_bias'], 'a_log': out['a_log'], 'd_ssd': out['d_ssd'], 'ssd_norm_g': out['ssd_norm_g'], 's5_a_re': out['s5_a_re'], 's5_a_im': out['s5_a_im'], 's5_log_dt': out['s5_log_dt'], 's5_b_re': out['s5_b_re'], 's5_b_im': out['s5_b_im'], 's5_c_re': out['s5_c_re'], 's5_c_im': out['s5_c_im'], 's5_d': out['s5_d'], 's5_glu_w': out['s5_glu_w'], 's5_glu_b': out['s5_glu_b'], 'w_branch': out['w_branch'], 'w_out': out['w_out'], 'norm_mlp_g': out['norm_mlp_g'], 'w_mlp_in': out['w_mlp_in'], 'w_mlp_out': out['w_mlp_out'], 'norm_final_g': out['norm_final_g'], 'loss_target': out['loss_target'], 'm_norm_mix_g': out['m_norm_mix_g'], 'm_w_in': out['m_w_in'], 'm_conv_w': out['m_conv_w'], 'm_conv_b': out['m_conv_b'], 'm_dt_bias': out['m_dt_bias'], 'm_a_log': out['m_a_log'], 'm_d_ssd': out['m_d_ssd'], 'm_ssd_norm_g': out['m_ssd_norm_g'], 'm_s5_a_re': out['m_s5_a_re'], 'm_s5_a_im': out['m_s5_a_im'], 'm_s5_log_dt': out['m_s5_log_dt'], 'm_s5_b_re': out['m_s5_b_re'], 'm_s5_b_im': out['m_s5_b_im'], 'm_s5_c_re': out['m_s5_c_re'], 'm_s5_c_im': out['m_s5_c_im'], 'm_s5_d': out['m_s5_d'], 'm_s5_glu_w': out['m_s5_glu_w'], 'm_s5_glu_b': out['m_s5_glu_b'], 'm_w_branch': out['m_w_branch'], 'm_w_out': out['m_w_out'], 'm_norm_mlp_g': out['m_norm_mlp_g'], 'm_w_mlp_in': out['m_w_mlp_in'], 'm_w_mlp_out': out['m_w_mlp_out'], 'm_norm_final_g': out['m_norm_final_g'], 'v_norm_mix_g': out['v_norm_mix_g'], 'v_w_in': out['v_w_in'], 'v_conv_w': out['v_conv_w'], 'v_conv_b': out['v_conv_b'], 'v_dt_bias': out['v_dt_bias'], 'v_a_log': out['v_a_log'], 'v_d_ssd': out['v_d_ssd'], 'v_ssd_norm_g': out['v_ssd_norm_g'], 'v_s5_a_re': out['v_s5_a_re'], 'v_s5_a_im': out['v_s5_a_im'], 'v_s5_log_dt': out['v_s5_log_dt'], 'v_s5_b_re': out['v_s5_b_re'], 'v_s5_b_im': out['v_s5_b_im'], 'v_s5_c_re': out['v_s5_c_re'], 'v_s5_c_im': out['v_s5_c_im'], 'v_s5_d': out['v_s5_d'], 'v_s5_glu_w': out['v_s5_glu_w'], 'v_s5_glu_b': out['v_s5_glu_b'], 'v_w_branch': out['v_w_branch'], 'v_w_out': out['v_w_out'], 'v_norm_mlp_g': out['v_norm_mlp_g'], 'v_w_mlp_in': out['v_w_mlp_in'], 'v_w_mlp_out': out['v_w_mlp_out'], 'v_norm_final_g': out['v_norm_final_g']}


def _loss(weights, diff, rest, loss_target):
    with _jax.named_scope("forward"):
        args = {**rest, TWIN_DIFF_INPUT: diff, **{k: w.astype(_WEIGHT_DTYPES[k]) for k, w in weights.items()}}
        y = _forward(args)
    with _jax.named_scope("loss_head"):
        err = _jnp.square(y.astype(_jnp.float32) - loss_target)
        return 0.5 * _jnp.sum(_jnp.mean(err, axis=-1)) if err.ndim else 0.5 * err


def _adamw(w, g, m, v):
    m = ADAM_B1 * m + (1.0 - ADAM_B1) * g
    v = ADAM_B2 * v + (1.0 - ADAM_B2) * _jnp.square(g)
    m_hat = m / (1.0 - ADAM_B1 ** ADAM_STEP)
    v_hat = v / (1.0 - ADAM_B2 ** ADAM_STEP)
    delta = -ADAM_LR * (m_hat / (_jnp.sqrt(v_hat) + ADAM_EPS) + ADAM_WD * w)
    return delta, m, v


def reference(x, norm_mix_g, w_in, conv_w, conv_b, dt_bias, a_log, d_ssd, ssd_norm_g, s5_a_re, s5_a_im, s5_log_dt, s5_b_re, s5_b_im, s5_c_re, s5_c_im, s5_d, s5_glu_w, s5_glu_b, w_branch, w_out, norm_mlp_g, w_mlp_in, w_mlp_out, norm_final_g, loss_target, m_norm_mix_g, m_w_in, m_conv_w, m_conv_b, m_dt_bias, m_a_log, m_d_ssd, m_ssd_norm_g, m_s5_a_re, m_s5_a_im, m_s5_log_dt, m_s5_b_re, m_s5_b_im, m_s5_c_re, m_s5_c_im, m_s5_d, m_s5_glu_w, m_s5_glu_b, m_w_branch, m_w_out, m_norm_mlp_g, m_w_mlp_in, m_w_mlp_out, m_norm_final_g, v_norm_mix_g, v_w_in, v_conv_w, v_conv_b, v_dt_bias, v_a_log, v_d_ssd, v_ssd_norm_g, v_s5_a_re, v_s5_a_im, v_s5_log_dt, v_s5_b_re, v_s5_b_im, v_s5_c_re, v_s5_c_im, v_s5_d, v_s5_glu_w, v_s5_glu_b, v_w_branch, v_w_out, v_norm_mlp_g, v_w_mlp_in, v_w_mlp_out, v_norm_final_g):
    given = dict(x=x, norm_mix_g=norm_mix_g, w_in=w_in, conv_w=conv_w, conv_b=conv_b, dt_bias=dt_bias, a_log=a_log, d_ssd=d_ssd, ssd_norm_g=ssd_norm_g, s5_a_re=s5_a_re, s5_a_im=s5_a_im, s5_log_dt=s5_log_dt, s5_b_re=s5_b_re, s5_b_im=s5_b_im, s5_c_re=s5_c_re, s5_c_im=s5_c_im, s5_d=s5_d, s5_glu_w=s5_glu_w, s5_glu_b=s5_glu_b, w_branch=w_branch, w_out=w_out, norm_mlp_g=norm_mlp_g, w_mlp_in=w_mlp_in, w_mlp_out=w_mlp_out, norm_final_g=norm_final_g, loss_target=loss_target, m_norm_mix_g=m_norm_mix_g, m_w_in=m_w_in, m_conv_w=m_conv_w, m_conv_b=m_conv_b, m_dt_bias=m_dt_bias, m_a_log=m_a_log, m_d_ssd=m_d_ssd, m_ssd_norm_g=m_ssd_norm_g, m_s5_a_re=m_s5_a_re, m_s5_a_im=m_s5_a_im, m_s5_log_dt=m_s5_log_dt, m_s5_b_re=m_s5_b_re, m_s5_b_im=m_s5_b_im, m_s5_c_re=m_s5_c_re, m_s5_c_im=m_s5_c_im, m_s5_d=m_s5_d, m_s5_glu_w=m_s5_glu_w, m_s5_glu_b=m_s5_glu_b, m_w_branch=m_w_branch, m_w_out=m_w_out, m_norm_mlp_g=m_norm_mlp_g, m_w_mlp_in=m_w_mlp_in, m_w_mlp_out=m_w_mlp_out, m_norm_final_g=m_norm_final_g, v_norm_mix_g=v_norm_mix_g, v_w_in=v_w_in, v_conv_w=v_conv_w, v_conv_b=v_conv_b, v_dt_bias=v_dt_bias, v_a_log=v_a_log, v_d_ssd=v_d_ssd, v_ssd_norm_g=v_ssd_norm_g, v_s5_a_re=v_s5_a_re, v_s5_a_im=v_s5_a_im, v_s5_log_dt=v_s5_log_dt, v_s5_b_re=v_s5_b_re, v_s5_b_im=v_s5_b_im, v_s5_c_re=v_s5_c_re, v_s5_c_im=v_s5_c_im, v_s5_d=v_s5_d, v_s5_glu_w=v_s5_glu_w, v_s5_glu_b=v_s5_glu_b, v_w_branch=v_w_branch, v_w_out=v_w_out, v_norm_mlp_g=v_norm_mlp_g, v_w_mlp_in=v_w_mlp_in, v_w_mlp_out=v_w_mlp_out, v_norm_final_g=v_norm_final_g)
    weights = {n: given[n] for n in TWIN_WEIGHTS}
    shared = {n: given[n] for n in SHARED_INPUTS}
    per_example = {n: given[n] for n in ['x']}
    grad_fn = _jax.value_and_grad(_loss, argnums=(0, 1))

    def one_microbatch(ex, loss_target):
        ex = dict(ex)
        diff = ex.pop(TWIN_DIFF_INPUT)
        return grad_fn(weights, diff, {**shared, **ex}, loss_target)

    if N_MICROBATCH == 1:
        loss, (grad_w, grad_x) = one_microbatch(per_example, given["loss_target"])
    else:
        def body(carry, xs):
            loss_sum, grad_sum = carry
            l_k, (gw_k, gx_k) = one_microbatch(xs[0], xs[1])
            with _jax.named_scope("update"):
                return (loss_sum + l_k, _jax.tree.map(_jnp.add, grad_sum, gw_k)), gx_k

        init = (_jnp.zeros((), _jnp.float32), _jax.tree.map(_jnp.zeros_like, weights))
        (loss, grad_w), grad_x = _jax.lax.scan(body, init, (per_example, given["loss_target"]))
    with _jax.named_scope("update"):
        delta_w, new_m, new_v = {}, {}, {}
        for n in TWIN_WEIGHTS:
            delta_w[n], new_m[n], new_v[n] = _adamw(weights[n], grad_w[n], given["m_" + n], given["v_" + n])
    return (loss, grad_x, *[grad_w[n] for n in TWIN_WEIGHTS], *[delta_w[n] for n in TWIN_WEIGHTS],
            *[new_m[n] for n in TWIN_WEIGHTS], *[new_v[n] for n in TWIN_WEIGHTS])
```

```python
import functools
import math

import jax
import jax.numpy as jnp
from jax import lax
from jax.experimental import pallas as pl
from jax.experimental.pallas import tpu as pltpu

F32 = jnp.float32
BF16 = jnp.bfloat16

D_MODEL = 1024
SSD_INNER = 1024
SSD_HEADS = 16
SSD_HEADDIM = 64
SSD_GROUPS = 4
SSD_HPG = 4
SSD_STATE = 128
SSD_CHUNK = 128
CONV_K = 4
CONV_DIM = 2048
S5_WIDTH = 512
S5_STATES = 2048
S5_BLOCKS = 4
S5_CHUNK = 128
D_FF = 4096
EPS = 1e-6
P_Z, P_XBC, P_U5, P_G, P_DT, P_END = 0, 1024, 3072, 3584, 5632, 5760
DT_PAD = 128
OFF_DT, OFF_U = 3072, 3088
D_IN_PROJ = 5648

ADAM_LR, ADAM_B1, ADAM_B2, ADAM_EPS, ADAM_WD, ADAM_STEP = 0.001, 0.9, 0.999, 1e-08, 0.01, 10

TOKEN_TILE = 256
VMEM_LIMIT = 56 * 1024 * 1024
HALO = 8


def _pc(body, **kw):
    return pl.pallas_call(body, **kw)


def _cparams(sem=None):
    return pltpu.CompilerParams(dimension_semantics=sem, vmem_limit_bytes=VMEM_LIMIT)


def _dot(a, b):
    return jnp.dot(a, b, preferred_element_type=F32)


def _dot_nt(a, b):
    return lax.dot_general(a, b, (((1,), (1,)), ((), ())), preferred_element_type=F32)


def _dot_tn(a, b):
    return lax.dot_general(a, b, (((0,), (0,)), ((), ())), preferred_element_type=F32)


def _dot_hi(a, b, dims=(((1,), (0,)), ((), ()))):
    return lax.dot_general(a, b, dims, preferred_element_type=F32, precision=lax.Precision.HIGHEST)


def _sigmoid(x):
    return 1.0 / (1.0 + jnp.exp(-x))


def _softplus(x):
    return jnp.maximum(x, 0.0) + jnp.log(1.0 + jnp.exp(-jnp.abs(x)))


_GELU_C = math.sqrt(2.0 / math.pi)


def _gelu(x):
    return 0.5 * x * (1.0 + jnp.tanh(_GELU_C * (x + 0.044715 * x * x * x)))


def _gelu_grad(x):
    t = jnp.tanh(_GELU_C * (x + 0.044715 * x * x * x))
    return 0.5 * (1.0 + t) + 0.5 * x * (1.0 - t * t) * _GELU_C * (1.0 + 3.0 * 0.044715 * x * x)


def _rms(x):
    r = lax.rsqrt(jnp.mean(x * x, axis=-1, keepdims=True) + EPS)
    return x * r, r


def _rms_bwd(xn, r, dxn):
    return r * (dxn - xn * jnp.mean(dxn * xn, axis=-1, keepdims=True))


def _row_spec(tm, width, col=0):
    return pl.BlockSpec((tm, width), lambda i: (i, col))


def _const_spec(shape):
    nd = len(shape)
    return pl.BlockSpec(shape, lambda i: (0,) * nd)


def _hbm_spec():
    return pl.BlockSpec(memory_space=pl.ANY)


def _inproj_fwd(x, g, wp):
    T = x.shape[0]
    tm = TOKEN_TILE

    def body(x_ref, g_ref, w_hbm, z_ref, xbc_ref, u5_ref, gt_ref, dt_ref, w_ref):
        @pl.when(pl.program_id(0) == 0)
        def _():
            pltpu.sync_copy(w_hbm, w_ref)

        xn, _ = _rms(x_ref[...])
        h = (xn * g_ref[...]).astype(BF16)
        z_ref[...] = _dot(h, w_ref[:, P_Z:P_XBC])
        xbc_ref[...] = _dot(h, w_ref[:, P_XBC:P_U5])
        u5_ref[...] = _dot(h, w_ref[:, P_U5:P_G])
        gt_ref[...] = _dot(h, w_ref[:, P_G:P_DT])
        dt_ref[...] = _dot(h, w_ref[:, P_DT:P_END])

    widths = (1024, 2048, 512, 2048, DT_PAD)
    return _pc(
        body, name="inproj_fwd", grid=(T // tm,),
        in_specs=[_row_spec(tm, D_MODEL), _const_spec((1, D_MODEL)), _hbm_spec()],
        out_specs=[_row_spec(tm, w) for w in widths],
        out_shape=[jax.ShapeDtypeStruct((T, w), F32) for w in widths],
        scratch_shapes=[pltpu.VMEM((D_MODEL, P_END), BF16)],
        compiler_params=_cparams(("arbitrary",)),
    )(x, g, wp)


def _inproj_bwd(x, dx1, dz, dxbc, du5, dgt, ddt, g, wp):
    T = x.shape[0]
    tm = TOKEN_TILE

    def body(x_ref, dx1_ref, dz_ref, dxbc_ref, du5_ref, dgt_ref, ddt_ref, g_ref, w_hbm, dx_ref, h_ref, dg_ref, w_ref):
        @pl.when(pl.program_id(0) == 0)
        def _():
            pltpu.sync_copy(w_hbm, w_ref)
            dg_ref[...] = jnp.zeros_like(dg_ref)

        xn, r = _rms(x_ref[...])
        gv = g_ref[...]
        h_ref[...] = (xn * gv).astype(BF16)
        dh = _dot_nt(dz_ref[...].astype(BF16), w_ref[:, P_Z:P_XBC])
        dh += _dot_nt(dxbc_ref[...].astype(BF16), w_ref[:, P_XBC:P_U5])
        dh += _dot_nt(du5_ref[...].astype(BF16), w_ref[:, P_U5:P_G])
        dh += _dot_nt(dgt_ref[...].astype(BF16), w_ref[:, P_G:P_DT])
        dh += _dot_nt(ddt_ref[...].astype(BF16), w_ref[:, P_DT:P_END])
        dg_ref[...] += jnp.sum(dh * xn, axis=0, keepdims=True)
        dx_ref[...] = dx1_ref[...] + _rms_bwd(xn, r, dh * gv)

    return _pc(
        body, name="inproj_bwd", grid=(T // tm,),
        in_specs=[_row_spec(tm, 1024), _row_spec(tm, 1024), _row_spec(tm, 1024), _row_spec(tm, 2048),
                  _row_spec(tm, 512), _row_spec(tm, 2048), _row_spec(tm, DT_PAD), _const_spec((1, 1024)), _hbm_spec()],
        out_specs=[_row_spec(tm, 1024), _row_spec(tm, 1024), _const_spec((1, 1024))],
        out_shape=[jax.ShapeDtypeStruct((T, 1024), F32), jax.ShapeDtypeStruct((T, 1024), BF16),
                   jax.ShapeDtypeStruct((1, 1024), F32)],
        scratch_shapes=[pltpu.VMEM((D_MODEL, P_END), BF16)],
        compiler_params=_cparams(("arbitrary",)),
    )(x, dx1, dz, dxbc, du5, dgt, ddt, g, wp)


def _conv_fwd(xbc_raw, dt_raw, conv_w, conv_b, dt_bias):
    T = xbc_raw.shape[0]
    tm = TOKEN_TILE

    def body(u_ref, dtr_ref, w_ref, b_ref, db_ref, act_ref, dt_ref, ext_ref):
        @pl.when(pl.program_id(0) == 0)
        def _():
            ext_ref[0:HALO, :] = jnp.zeros((HALO, CONV_DIM), F32)

        ext_ref[HALO:, :] = u_ref[...]
        y = b_ref[...] + jnp.zeros((tm, CONV_DIM), F32)
        for k in range(CONV_K):
            y += w_ref[k:k + 1, :] * ext_ref[pl.ds(HALO - (CONV_K - 1) + k, tm), :]
        act_ref[...] = y * _sigmoid(y)
        ext_ref[0:HALO, :] = u_ref[tm - HALO:tm, :]
        dt_ref[...] = _softplus(dtr_ref[...] + db_ref[...])

    return _pc(
        body, name="conv_fwd", grid=(T // tm,),
        in_specs=[_row_spec(tm, CONV_DIM), _row_spec(tm, DT_PAD), _const_spec((CONV_K, CONV_DIM)),
                  _const_spec((1, CONV_DIM)), _const_spec((1, DT_PAD))],
        out_specs=[_row_spec(tm, CONV_DIM), _row_spec(tm, DT_PAD)],
        out_shape=[jax.ShapeDtypeStruct((T, CONV_DIM), F32), jax.ShapeDtypeStruct((T, DT_PAD), F32)],
        scratch_shapes=[pltpu.VMEM((tm + HALO, CONV_DIM), F32)],
        compiler_params=_cparams(("arbitrary",)),
    )(xbc_raw, dt_raw, conv_w, conv_b, dt_bias)


def _conv_bwd(xbc_raw, dt_raw, dxs_a, dxs_b, dB, dC, ddt, conv_w, conv_b, dt_bias):
    T = xbc_raw.shape[0]
    tm = TOKEN_TILE
    n = T // tm
    hb = tm // HALO

    def rev(width):
        return pl.BlockSpec((tm, width), lambda i: (n - 1 - i, 0))

    def body(u_ref, up_ref, dtr_ref, dxa_ref, dxb_ref, dB_ref, dC_ref, ddt_ref, w_ref, b_ref, db_ref,
             du_ref, ddtr_ref, dw_ref, dcb_ref, ddb_ref, ext_ref, dye_ref):
        i = pl.program_id(0)

        @pl.when(i == 0)
        def _():
            dye_ref[tm:, :] = jnp.zeros((HALO, CONV_DIM), F32)
            dw_ref[...] = jnp.zeros_like(dw_ref)
            dcb_ref[...] = jnp.zeros_like(dcb_ref)
            ddb_ref[...] = jnp.zeros_like(ddb_ref)

        first = (i == n - 1).astype(F32)
        ext_ref[0:HALO, :] = up_ref[...] * (1.0 - first)
        ext_ref[HALO:, :] = u_ref[...]
        y = b_ref[...] + jnp.zeros((tm, CONV_DIM), F32)
        for k in range(CONV_K):
            y += w_ref[k:k + 1, :] * ext_ref[pl.ds(HALO - (CONV_K - 1) + k, tm), :]
        s = _sigmoid(y)
        dsilu = s * (1.0 + y * (1.0 - s))
        dy = jnp.concatenate([dxa_ref[...] + dxb_ref[...], dB_ref[...], dC_ref[...]], axis=1) * dsilu
        dye_ref[0:tm, :] = dy
        dcb_ref[...] += jnp.sum(dy, axis=0, keepdims=True)
        du = jnp.zeros((tm, CONV_DIM), F32)
        for k in range(CONV_K):
            dw_ref[k:k + 1, :] += jnp.sum(dy * ext_ref[pl.ds(HALO - (CONV_K - 1) + k, tm), :], axis=0, keepdims=True)
            du += w_ref[k:k + 1, :] * dye_ref[pl.ds(CONV_K - 1 - k, tm), :]
        du_ref[...] = du
        dye_ref[tm:, :] = dy[0:HALO, :]
        sg = _sigmoid(dtr_ref[...] + db_ref[...])
        ddtr = ddt_ref[...] * sg
        ddtr_ref[...] = ddtr
        ddb_ref[...] += jnp.sum(ddtr, axis=0, keepdims=True)

    prev_spec = pl.BlockSpec((HALO, CONV_DIM), lambda i: (jnp.maximum((n - 1 - i) * hb - 1, 0), 0))
    return _pc(
        body, name="conv_bwd", grid=(n,),
        in_specs=[rev(CONV_DIM), prev_spec, rev(DT_PAD), rev(1024), rev(1024), rev(512), rev(512), rev(DT_PAD),
                  _const_spec((CONV_K, CONV_DIM)), _const_spec((1, CONV_DIM)), _const_spec((1, DT_PAD))],
        out_specs=[rev(CONV_DIM), rev(DT_PAD), _const_spec((HALO, CONV_DIM)), _const_spec((1, CONV_DIM)),
                   _const_spec((1, DT_PAD))],
        out_shape=[jax.ShapeDtypeStruct((T, CONV_DIM), F32), jax.ShapeDtypeStruct((T, DT_PAD), F32),
                   jax.ShapeDtypeStruct((HALO, CONV_DIM), F32), jax.ShapeDtypeStruct((1, CONV_DIM), F32),
                   jax.ShapeDtypeStruct((1, DT_PAD), F32)],
        scratch_shapes=[pltpu.VMEM((tm + HALO, CONV_DIM), F32), pltpu.VMEM((tm + HALO, CONV_DIM), F32)],
        compiler_params=_cparams(("arbitrary",)),
    )(xbc_raw, xbc_raw, dt_raw, dxs_a, dxs_b, dB, dC, ddt, conv_w, conv_b, dt_bias)


def _ssd_chunk_common(dt_ref, alog_ref):
    q = SSD_CHUNK
    a = -jnp.exp(alog_ref[...])
    dtv = dt_ref[...]
    la = dtv * a
    row = lax.broadcasted_iota(jnp.int32, (q, q), 0)
    col = lax.broadcasted_iota(jnp.int32, (q, q), 1)
    causal = col <= row
    tri = causal.astype(F32)
    cum = _dot_hi(tri, la)
    cum_t = _dot_hi(la, tri, (((0,), (1,)), ((), ())))
    return a, dtv, causal, tri, cum, cum_t


def _ssd_fwd(xbc_act, dt, alog):
    T = xbc_act.shape[0]
    q = SSD_CHUNK
    nc = T // q

    def body(xbc_ref, dt_ref, alog_ref, y_ref, sp_ref, s_ref):
        @pl.when(pl.program_id(0) == 0)
        def _():
            s_ref[...] = jnp.zeros_like(s_ref)

        a, dtv, causal, tri, cum, cum_t = _ssd_chunk_common(dt_ref, alog_ref)
        sp_ref[0] = s_ref[...]
        for g in range(SSD_GROUPS):
            bb = xbc_ref[:, 1024 + 128 * g:1152 + 128 * g].astype(BF16)
            cb = xbc_ref[:, 1536 + 128 * g:1664 + 128 * g].astype(BF16)
            gm = _dot_nt(cb, bb)
            for r in range(SSD_HPG):
                h = SSD_HPG * g + r
                x = xbc_ref[:, 64 * h:64 * h + 64]
                cc = cum[:, h:h + 1]
                cr = cum_t[h:h + 1, :]
                cl = cum[q - 1:q, h:h + 1]
                decay = jnp.where(causal, jnp.exp(jnp.minimum(cc - cr, 0.0)), 0.0)
                xd = x * dtv[:, h:h + 1]
                sp = s_ref[h]
                y = _dot((gm * decay).astype(BF16), xd.astype(BF16))
                y += _dot_nt(cb, sp.astype(BF16)) * jnp.exp(cc)
                y_ref[:, 64 * h:64 * h + 64] = y
                st = _dot_tn((xd * jnp.exp(cl - cc)).astype(BF16), bb)
                s_ref[h] = sp * jnp.exp(cl) + st

    return _pc(
        body, name="ssd_fwd", grid=(nc,),
        in_specs=[_row_spec(q, CONV_DIM), _row_spec(q, DT_PAD), _const_spec((1, DT_PAD))],
        out_specs=[_row_spec(q, SSD_INNER),
                   pl.BlockSpec((1, SSD_HEADS, SSD_HEADDIM, SSD_STATE), lambda i: (i, 0, 0, 0))],
        out_shape=[jax.ShapeDtypeStruct((T, SSD_INNER), F32),
                   jax.ShapeDtypeStruct((nc, SSD_HEADS, SSD_HEADDIM, SSD_STATE), F32)],
        scratch_shapes=[pltpu.VMEM((SSD_HEADS, SSD_HEADDIM, SSD_STATE), F32)],
        compiler_params=_cparams(("arbitrary",)),
    )(xbc_act, dt, alog)


def _ssd_bwd(xbc_act, dt, alog, sprev, dy):
    T = xbc_act.shape[0]
    q = SSD_CHUNK
    nc = T // q

    def rev(width):
        return pl.BlockSpec((q, width), lambda i: (nc - 1 - i, 0))

    def body(xbc_ref, dt_ref, alog_ref, sp_ref, dy_ref, dxs_ref, dB_ref, dC_ref, ddt_ref, dalog_ref, ds_ref):
        i = pl.program_id(0)

        @pl.when(i == 0)
        def _():
            ds_ref[...] = jnp.zeros_like(ds_ref)
            dalog_ref[...] = jnp.zeros_like(dalog_ref)

        a, dtv, causal, tri, cum, cum_t = _ssd_chunk_common(dt_ref, alog_ref)
        lane = lax.broadcasted_iota(jnp.int32, (1, DT_PAD), 1)
        rowq = lax.broadcasted_iota(jnp.int32, (q, 1), 0)
        dcum_all = jnp.zeros((q, DT_PAD), F32)
        ddt_all = jnp.zeros((q, DT_PAD), F32)
        for g in range(SSD_GROUPS):
            bb = xbc_ref[:, 1024 + 128 * g:1152 + 128 * g].astype(BF16)
            cb = xbc_ref[:, 1536 + 128 * g:1664 + 128 * g].astype(BF16)
            gm = _dot_nt(cb, bb)
            dgm = jnp.zeros((q, q), F32)
            dbg = jnp.zeros((q, SSD_STATE), F32)
            dcg = jnp.zeros((q, SSD_STATE), F32)
            for r in range(SSD_HPG):
                h = SSD_HPG * g + r
                x = xbc_ref[:, 64 * h:64 * h + 64]
                dyh = dy_ref[:, 64 * h:64 * h + 64]
                dtc = dtv[:, h:h + 1]
                cc = cum[:, h:h + 1]
                cr = cum_t[h:h + 1, :]
                cl = cum[q - 1:q, h:h + 1]
                decay = jnp.where(causal, jnp.exp(jnp.minimum(cc - cr, 0.0)), 0.0)
                m = gm * decay
                xd = x * dtc
                ec = jnp.exp(cc)
                de = jnp.exp(cl - cc)
                cd = jnp.exp(cl)
                sp = sp_ref[0, h]
                dsn = ds_ref[h]
                spb = sp.astype(BF16)
                dsnb = dsn.astype(BF16)
                dyb = dyh.astype(BF16)
                dye = (dyh * ec).astype(BF16)
                cs = _dot_nt(cb, spb)
                dcum = jnp.sum(dyh * cs, axis=1, keepdims=True) * ec
                dcg += _dot(dye, spb)
                dsp = dsn * cd + _dot_tn(dye, cb)
                dlast = jnp.sum(dsn * sp, keepdims=True) * cd
                dbg += _dot((xd * de).astype(BF16), dsnb)
                w = _dot_nt(bb, dsnb)
                dxd = w * de
                tde = jnp.sum(w * xd, axis=1, keepdims=True) * de
                dlast += jnp.sum(tde, keepdims=True)
                dcum -= tde
                dm = _dot_nt(dyb, xd.astype(BF16))
                dxd += _dot_tn(m.astype(BF16), dyb)
                dgm += dm * decay
                e = dm * m
                dcum += jnp.sum(e, axis=1, keepdims=True) - jnp.sum(e.T, axis=1, keepdims=True)
                dcum += jnp.where(rowq == q - 1, dlast, 0.0)
                dxs_ref[:, 64 * h:64 * h + 64] = dxd * dtc
                onehot = (lane == h).astype(F32)
                dcum_all += dcum * onehot
                ddt_all += jnp.sum(dxd * x, axis=1, keepdims=True) * onehot
                ds_ref[h] = dsp
            dgb = dgm.astype(BF16)
            dC_ref[:, 128 * g:128 * g + 128] = dcg + _dot(dgb, bb)
            dB_ref[:, 128 * g:128 * g + 128] = dbg + _dot_tn(dgb, cb)
        dla = _dot_hi(tri, dcum_all, (((0,), (0,)), ((), ())))
        ddt_ref[...] = ddt_all + dla * a
        dalog_ref[...] += jnp.sum(dla * dtv, axis=0, keepdims=True)

        @pl.when(i == nc - 1)
        def _():
            dalog_ref[...] = dalog_ref[...] * a

    return _pc(
        body, name="ssd_bwd", grid=(nc,),
        in_specs=[rev(CONV_DIM), rev(DT_PAD), _const_spec((1, DT_PAD)),
                  pl.BlockSpec((1, SSD_HEADS, SSD_HEADDIM, SSD_STATE), lambda i: (nc - 1 - i, 0, 0, 0)),
                  rev(SSD_INNER)],
        out_specs=[rev(SSD_INNER), rev(512), rev(512), rev(DT_PAD), _const_spec((1, DT_PAD))],
        out_shape=[jax.ShapeDtypeStruct((T, SSD_INNER), F32), jax.ShapeDtypeStruct((T, 512), F32),
                   jax.ShapeDtypeStruct((T, 512), F32), jax.ShapeDtypeStruct((T, DT_PAD), F32),
                   jax.ShapeDtypeStruct((1, DT_PAD), F32)],
        scratch_shapes=[pltpu.VMEM((SSD_HEADS, SSD_HEADDIM, SSD_STATE), F32)],
        compiler_params=_cparams(("arbitrary",)),
    )(xbc_act, dt, alog, sprev, dy)


def _s5_disc_vals(a_re, a_im, log_dt, b_re, b_im):
    dt = jnp.exp(log_dt)
    mag = jnp.exp(a_re * dt)
    ab_re = mag * jnp.cos(a_im * dt)
    ab_im = mag * jnp.sin(a_im * dt)
    den = a_re * a_re + a_im * a_im
    nr = ab_re - 1.0
    ni = ab_im
    coef_re = (nr * a_re + ni * a_im) / den
    coef_im = (ni * a_re - nr * a_im) / den
    bb_re = coef_re * b_re - coef_im * b_im
    bb_im = coef_re * b_im + coef_im * b_re
    return ab_re, ab_im, bb_re, bb_im


def _s5_disc(a_re, a_im, log_dt, b_re, b_im):
    def body(ar, ai, ld, br, bi, o1, o2, o3, o4):
        o1[...], o2[...], o3[...], o4[...] = _s5_disc_vals(ar[...], ai[...], ld[...], br[...], bi[...])

    return _pc(
        body, name="s5_disc",
        out_shape=[jax.ShapeDtypeStruct((S5_STATES, 1), F32), jax.ShapeDtypeStruct((S5_STATES, 1), F32),
                   jax.ShapeDtypeStruct((S5_STATES, 16), F32), jax.ShapeDtypeStruct((S5_STATES, 16), F32)],
    )(a_re, a_im, log_dt, b_re, b_im)


def _s5_disc_bwd(a_re, a_im, log_dt, b_re, b_im, d_ab_re, d_ab_im, d_bb_re, d_bb_im):
    def body(ar, ai, ld, br, bi, g1, g2, g3, g4, o1, o2, o3, o4, o5):
        _, vjp = jax.vjp(_s5_disc_vals, ar[...], ai[...], ld[...], br[...], bi[...])
        d1, d2, d3, d4, d5 = vjp((g1[...], g2[...], g3[...], g4[...]))
        o1[...] = d1
        o2[...] = d2
        grp = lax.broadcasted_iota(jnp.int32, (32, S5_STATES), 0)
        st = lax.broadcasted_iota(jnp.int32, (32, S5_STATES), 1)
        sel = (st // 64 == grp).astype(F32)
        o3[...] = _dot_hi(sel, d3)
        o4[...] = d4
        o5[...] = d5

    return _pc(
        body, name="s5_disc_bwd",
        out_shape=[jax.ShapeDtypeStruct((S5_STATES, 1), F32), jax.ShapeDtypeStruct((S5_STATES, 1), F32),
                   jax.ShapeDtypeStruct((32, 1), F32),
                   jax.ShapeDtypeStruct((S5_STATES, 16), F32), jax.ShapeDtypeStruct((S5_STATES, 16), F32)],
    )(a_re, a_im, log_dt, b_re, b_im, d_ab_re, d_ab_im, d_bb_re, d_bb_im)


def _cmul_add(xr, xi, pr, pi, yr, yi):
    return xr + pr * yr - pi * yi, xi + pr * yi + pi * yr


def _powers(ar, ai, n):
    out = [(ar, ai)]
    for _ in range(n - 1):
        pr, pi = out[-1]
        out.append((pr * pr - pi * pi, 2.0 * pr * pi))
    return out


def _scan_causal(br, bi, pws, row):
    q = br.shape[0]
    k = 1
    for pr, pi in pws:
        keep = row >= k
        sr = jnp.where(keep, pltpu.roll(br, k, 0), 0.0)
        si = jnp.where(keep, pltpu.roll(bi, k, 0), 0.0)
        br, bi = _cmul_add(br, bi, pr, pi, sr, si)
        k *= 2
    assert k == q
    return br, bi


def _scan_anticausal(br, bi, pws, row):
    q = br.shape[0]
    k = 1
    for pr, pi in pws:
        keep = row < q - k
        sr = jnp.where(keep, pltpu.roll(br, q - k, 0), 0.0)
        si = jnp.where(keep, pltpu.roll(bi, q - k, 0), 0.0)
        br, bi = _cmul_add(br, bi, pr, pi, sr, si)
        k *= 2
    assert k == q
    return br, bi


_S5_LEVELS = int(math.log2(S5_CHUNK))
_BW = S5_STATES // S5_BLOCKS
_BI = S5_WIDTH // S5_BLOCKS


def _s5_fwd(u5, wb4, wc4, ab, dvec):
    T = u5.shape[0]
    q = S5_CHUNK
    nc = T // q

    def body(u_ref, wb_ref, wc_ref, ab_ref, d_ref, y_ref, sp_ref, carry_ref, pw_re, pw_im):
        i = pl.program_id(0)
        row = lax.broadcasted_iota(jnp.int32, (q, 1), 0)

        @pl.when(i == 0)
        def _():
            carry_ref[...] = jnp.zeros_like(carry_ref)
            for j in range(S5_BLOCKS):
                ar = ab_ref[0:1, _BW * j:_BW * (j + 1)]
                ai = ab_ref[1:2, _BW * j:_BW * (j + 1)]
                er = jnp.where(row == 0, ar, 0.0) + jnp.zeros((q, _BW), F32)
                ei = jnp.where(row == 0, ai, 0.0) + jnp.zeros((q, _BW), F32)
                pr, pi = _scan_causal(er, ei, _powers(ar, ai, _S5_LEVELS), row)
                pw_re[:, _BW * j:_BW * (j + 1)] = pr
                pw_im[:, _BW * j:_BW * (j + 1)] = pi

        sp_ref[0] = carry_ref[...]
        for j in range(S5_BLOCKS):
            sl = slice(_BW * j, _BW * (j + 1))
            ul = slice(_BI * j, _BI * (j + 1))
            ar = ab_ref[0:1, sl]
            ai = ab_ref[1:2, sl]
            u = u_ref[:, ul]
            bu = _dot(u.astype(BF16), wb_ref[j])
            sr, si = _scan_causal(bu[:, :_BW], bu[:, _BW:], _powers(ar, ai, _S5_LEVELS), row)
            sr, si = _cmul_add(sr, si, pw_re[:, sl], pw_im[:, sl], carry_ref[0:1, sl], carry_ref[1:2, sl])
            carry_ref[0:1, sl] = sr[q - 1:q, :]
            carry_ref[1:2, sl] = si[q - 1:q, :]
            s = jnp.concatenate([sr, si], axis=1).astype(BF16)
            y_ref[:, ul] = _dot(s, wc_ref[j]) + d_ref[:, ul] * u

    return _pc(
        body, name="s5_fwd", grid=(nc,),
        in_specs=[_row_spec(q, S5_WIDTH), _const_spec((S5_BLOCKS, _BI, 2 * _BW)), _const_spec((S5_BLOCKS, 2 * _BW, _BI)),
                  _const_spec((8, S5_STATES)), _const_spec((1, S5_WIDTH))],
        out_specs=[_row_spec(q, S5_WIDTH), pl.BlockSpec((1, 8, S5_STATES), lambda i: (i, 0, 0))],
        out_shape=[jax.ShapeDtypeStruct((T, S5_WIDTH), F32), jax.ShapeDtypeStruct((nc, 8, S5_STATES), F32)],
        scratch_shapes=[pltpu.VMEM((8, S5_STATES), F32), pltpu.VMEM((q, S5_STATES), F32), pltpu.VMEM((q, S5_STATES), F32)],
        compiler_params=_cparams(("arbitrary",)),
    )(u5, wb4, wc4, ab, dvec)


def _s5_bwd(u5, dy5, wb4, wc4, ab, dvec, sprev):
    T = u5.shape[0]
    q = S5_CHUNK
    nc = T // q

    def rev(width):
        return pl.BlockSpec((q, width), lambda i: (nc - 1 - i, 0))

    def body(u_ref, dy_ref, wb_ref, wc_ref, ab_ref, d_ref, sp_ref, du_ref, dwb_ref, dwc_ref, dab_ref, dd_ref,
             carry_ref, pw_re, pw_im, rp_re, rp_im):
        i = pl.program_id(0)
        row = lax.broadcasted_iota(jnp.int32, (q, 1), 0)

        @pl.when(i == 0)
        def _():
            carry_ref[...] = jnp.zeros_like(carry_ref)
            dwb_ref[...] = jnp.zeros_like(dwb_ref)
            dwc_ref[...] = jnp.zeros_like(dwc_ref)
            dab_ref[...] = jnp.zeros_like(dab_ref)
            dd_ref[...] = jnp.zeros_like(dd_ref)
            for j in range(S5_BLOCKS):
                sl = slice(_BW * j, _BW * (j + 1))
                ar = ab_ref[0:1, sl]
                ai = ab_ref[1:2, sl]
                zero = jnp.zeros((q, _BW), F32)
                pr, pi = _scan_causal(jnp.where(row == 0, ar, 0.0) + zero, jnp.where(row == 0, ai, 0.0) + zero,
                                      _powers(ar, ai, _S5_LEVELS), row)
                pw_re[:, sl] = pr
                pw_im[:, sl] = pi
                pr, pi = _scan_anticausal(jnp.where(row == q - 1, ar, 0.0) + zero, jnp.where(row == q - 1, -ai, 0.0) + zero,
                                          _powers(ar, -ai, _S5_LEVELS), row)
                rp_re[:, sl] = pr
                rp_im[:, sl] = pi

        for j in range(S5_BLOCKS):
            sl = slice(_BW * j, _BW * (j + 1))
            ul = slice(_BI * j, _BI * (j + 1))
            ar = ab_ref[0:1, sl]
            ai = ab_ref[1:2, sl]
            u = u_ref[:, ul]
            ub = u.astype(BF16)
            dy = dy_ref[:, ul]
            dyb = dy.astype(BF16)
            bu = _dot(ub, wb_ref[j])
            sr, si = _scan_causal(bu[:, :_BW], bu[:, _BW:], _powers(ar, ai, _S5_LEVELS), row)
            s0r = sp_ref[0, 0:1, sl]
            s0i = sp_ref[0, 1:2, sl]
            sr, si = _cmul_add(sr, si, pw_re[:, sl], pw_im[:, sl], s0r, s0i)
            ds = _dot_nt(dyb, wc_ref[j])
            lr, li = _scan_anticausal(ds[:, :_BW], ds[:, _BW:], _powers(ar, -ai, _S5_LEVELS), row)
            lr, li = _cmul_add(lr, li, rp_re[:, sl], rp_im[:, sl], carry_ref[0:1, sl], carry_ref[1:2, sl])
            carry_ref[0:1, sl] = lr[0:1, :]
            carry_ref[1:2, sl] = li[0:1, :]
            lam = jnp.concatenate([lr, li], axis=1).astype(BF16)
            du_ref[:, ul] = _dot_nt(lam, wb_ref[j]) + d_ref[:, ul] * dy
            dwb_ref[j] += _dot_tn(ub, lam)
            dwc_ref[j] += _dot_tn(jnp.concatenate([sr, si], axis=1).astype(BF16), dyb)
            keep = row >= 1
            pr = jnp.where(keep, pltpu.roll(sr, 1, 0), s0r)
            pi = jnp.where(keep, pltpu.roll(si, 1, 0), s0i)
            dab_ref[0:1, sl] += jnp.sum(lr * pr + li * pi, axis=0, keepdims=True)
            dab_ref[1:2, sl] += jnp.sum(li * pr - lr * pi, axis=0, keepdims=True)
            dd_ref[:, ul] += jnp.sum(dy * u, axis=0, keepdims=True)

    return _pc(
        body, name="s5_bwd", grid=(nc,),
        in_specs=[rev(S5_WIDTH), rev(S5_WIDTH), _const_spec((S5_BLOCKS, _BI, 2 * _BW)), _const_spec((S5_BLOCKS, 2 * _BW, _BI)),
                  _const_spec((8, S5_STATES)), _const_spec((1, S5_WIDTH)),
                  pl.BlockSpec((1, 8, S5_STATES), lambda i: (nc - 1 - i, 0, 0))],
        out_specs=[rev(S5_WIDTH), _const_spec((S5_BLOCKS, _BI, 2 * _BW)), _const_spec((S5_BLOCKS, 2 * _BW, _BI)),
                   _const_spec((8, S5_STATES)), _const_spec((1, S5_WIDTH))],
        out_shape=[jax.ShapeDtypeStruct((T, S5_WIDTH), F32), jax.ShapeDtypeStruct((S5_BLOCKS, _BI, 2 * _BW), F32),
                   jax.ShapeDtypeStruct((S5_BLOCKS, 2 * _BW, _BI), F32), jax.ShapeDtypeStruct((8, S5_STATES), F32),
                   jax.ShapeDtypeStruct((1, S5_WIDTH), F32)],
        scratch_shapes=[pltpu.VMEM((8, S5_STATES), F32)] + [pltpu.VMEM((q, S5_STATES), F32)] * 4,
        compiler_params=_cparams(("arbitrary",)),
    )(u5, dy5, wb4, wc4, ab, dvec, sprev)


def _merge_vals(ys, xs, z, y5, gates, dvec, gssd, glu_w, glu_b, wbr):
    sz = _sigmoid(z)
    qv = ys + dvec * xs
    pre = qv * (z * sz)
    yn, rs = [], []
    for gi in range(SSD_GROUPS):
        p, r = _rms(pre[:, 256 * gi:256 * (gi + 1)])
        yn.append(p)
        rs.append(r)
    yn = jnp.concatenate(yn, axis=1)
    ya = yn * gssd
    gel = _gelu(y5)
    sg = _sigmoid(_dot(gel.astype(BF16), glu_w) + glu_b)
    yb = gel * sg
    pa = _dot(ya.astype(BF16), wbr[0:SSD_INNER, :])
    pb = _dot(yb.astype(BF16), wbr[SSD_INNER:, :])
    s0 = _sigmoid(gates[:, :D_MODEL])
    s1 = _sigmoid(gates[:, D_MODEL:])
    merged = s0 * pa + s1 * pb
    return dict(sz=sz, qv=qv, yn=yn, rs=rs, ya=ya, gel=gel, sg=sg, yb=yb, pa=pa, pb=pb, s0=s0, s1=s1, merged=merged)


def _merge_specs(tm):
    acts = [_row_spec(tm, 1024), _row_spec(tm, 1024, 0), _row_spec(tm, 1024), _row_spec(tm, 512), _row_spec(tm, 2048),
            _row_spec(tm, 1024)]
    params = [_const_spec((1, 1024)), _const_spec((1, 1024)), _const_spec((512, 512)), _const_spec((1, 512)),
              _hbm_spec(), _hbm_spec()]
    return acts, params


def _merge_fwd(ys, xbc_act, z, y5, gates, x, dvec, gssd, glu_w, glu_b, wbr, wout):
    T = x.shape[0]
    tm = TOKEN_TILE
    acts, params = _merge_specs(tm)

    def body(ys_ref, xs_ref, z_ref, y5_ref, gt_ref, x_ref, dv_ref, gs_ref, gw_ref, gb_ref, wbr_hbm, wout_hbm, x1_ref,
             wbr_ref, wout_ref):
        @pl.when(pl.program_id(0) == 0)
        def _():
            pltpu.sync_copy(wbr_hbm, wbr_ref)
            pltpu.sync_copy(wout_hbm, wout_ref)

        v = _merge_vals(ys_ref[...], xs_ref[...], z_ref[...], y5_ref[...], gt_ref[...], dv_ref[...], gs_ref[...],
                        gw_ref[...], gb_ref[...], wbr_ref)
        x1_ref[...] = x_ref[...] + _dot(v["merged"].astype(BF16), wout_ref[...])

    return _pc(
        body, name="merge_fwd", grid=(T // tm,),
        in_specs=acts + params, out_specs=_row_spec(tm, 1024),
        out_shape=jax.ShapeDtypeStruct((T, 1024), F32),
        scratch_shapes=[pltpu.VMEM((1536, 1024), BF16), pltpu.VMEM((1024, 1024), BF16)],
        compiler_params=_cparams(("arbitrary",)),
    )(ys, xbc_act, z, y5, gates, x, dvec, gssd, glu_w, glu_b, wbr, wout)


def _merge_bwd(ys, xbc_act, z, y5, gates, dx1, dvec, gssd, glu_w, glu_b, wbr, wout, head_sel):
    T = dx1.shape[0]
    tm = TOKEN_TILE
    acts, params = _merge_specs(tm)

    def body(ys_ref, xs_ref, z_ref, y5_ref, gt_ref, dx1_ref, dv_ref, gs_ref, gw_ref, gb_ref, wbr_hbm, wout_hbm, hs_ref,
             dys_ref, dxs_ref, dz_ref, dy5_ref, dgt_ref, mg_ref, ya_ref, yb_ref, dpa_ref, dpb_ref, gel_ref, dpre_ref,
             ddv_ref, dgs_ref, dgb_ref, wbr_ref, wout_ref, ddacc_ref):
        i = pl.program_id(0)

        @pl.when(i == 0)
        def _():
            pltpu.sync_copy(wbr_hbm, wbr_ref)
            pltpu.sync_copy(wout_hbm, wout_ref)
            ddacc_ref[...] = jnp.zeros_like(ddacc_ref)
            dgs_ref[...] = jnp.zeros_like(dgs_ref)
            dgb_ref[...] = jnp.zeros_like(dgb_ref)

        ys, xs, z, y5, gates = ys_ref[...], xs_ref[...], z_ref[...], y5_ref[...], gt_ref[...]
        dvv, gsv, gw = dv_ref[...], gs_ref[...], gw_ref[...]
        v = _merge_vals(ys, xs, z, y5, gates, dvv, gsv, gw, gb_ref[...], wbr_ref)
        dmg = _dot_nt(dx1_ref[...].astype(BF16), wout_ref[...])
        s0, s1, pa, pb = v["s0"], v["s1"], v["pa"], v["pb"]
        dgt_ref[:, :D_MODEL] = dmg * pa * s0 * (1.0 - s0)
        dgt_ref[:, D_MODEL:] = dmg * pb * s1 * (1.0 - s1)
        dpa = (dmg * s0).astype(BF16)
        dpb = (dmg * s1).astype(BF16)
        dya = _dot_nt(dpa, wbr_ref[0:SSD_INNER, :])
        dyb = _dot_nt(dpb, wbr_ref[SSD_INNER:, :])
        gel, sg = v["gel"], v["sg"]
        dpre = (dyb * gel * sg * (1.0 - sg))
        dgb_ref[...] += jnp.sum(dpre, axis=0, keepdims=True)
        dpre_b = dpre.astype(BF16)
        dgel = dyb * sg + _dot_nt(dpre_b, gw)
        dy5_ref[...] = dgel * _gelu_grad(y5)
        yn = v["yn"]
        dgs_ref[...] += jnp.sum(dya * yn, axis=0, keepdims=True)
        dyn = dya * gsv
        dpre_a = jnp.concatenate(
            [_rms_bwd(yn[:, 256 * gi:256 * (gi + 1)], v["rs"][gi], dyn[:, 256 * gi:256 * (gi + 1)])
             for gi in range(SSD_GROUPS)], axis=1)
        sz, qv = v["sz"], v["qv"]
        dq = dpre_a * (z * sz)
        dz_ref[...] = dpre_a * qv * (sz * (1.0 + z * (1.0 - sz)))
        dys_ref[...] = dq
        dxs_ref[...] = dq * dvv
        ddacc_ref[...] += jnp.sum(dq * xs, axis=0, keepdims=True)
        mg_ref[...] = v["merged"].astype(BF16)
        ya_ref[...] = v["ya"].astype(BF16)
        yb_ref[...] = v["yb"].astype(BF16)
        dpa_ref[...] = dpa
        dpb_ref[...] = dpb
        gel_ref[...] = gel.astype(BF16)
        dpre_ref[...] = dpre_b

        @pl.when(i == pl.num_programs(0) - 1)
        def _():
            ddv_ref[...] = _dot_hi(ddacc_ref[...], hs_ref[...])

    outs = [(1024, F32), (1024, F32), (1024, F32), (512, F32), (2048, F32),
            (1024, BF16), (1024, BF16), (512, BF16), (1024, BF16), (1024, BF16), (512, BF16), (512, BF16)]
    return _pc(
        body, name="merge_bwd", grid=(T // tm,),
        in_specs=acts + params + [_const_spec((1024, DT_PAD))],
        out_specs=[_row_spec(tm, w) for w, _ in outs] + [_const_spec((1, DT_PAD)), _const_spec((1, 1024)), _const_spec((1, 512))],
        out_shape=[jax.ShapeDtypeStruct((T, w), d) for w, d in outs] + [
            jax.ShapeDtypeStruct((1, DT_PAD), F32), jax.ShapeDtypeStruct((1, 1024), F32), jax.ShapeDtypeStruct((1, 512), F32)],
        scratch_shapes=[pltpu.VMEM((1536, 1024), BF16), pltpu.VMEM((1024, 1024), BF16), pltpu.VMEM((1, 1024), F32)],
        compiler_params=_cparams(("arbitrary",)),
    )(ys, xbc_act, z, y5, gates, dx1, dvec, gssd, glu_w, glu_b, wbr, wout, head_sel)


def _mlp_fwd(x1, g, w1, w2):
    T = x1.shape[0]
    tm = TOKEN_TILE

    def body(x_ref, g_ref, w1_hbm, w2_hbm, o_ref, w1_ref, w2_ref):
        @pl.when(pl.program_id(0) == 0)
        def _():
            pltpu.sync_copy(w1_hbm, w1_ref)
            pltpu.sync_copy(w2_hbm, w2_ref)

        xv = x_ref[...]
        xn, _ = _rms(xv)
        a1 = _dot((xn * g_ref[...]).astype(BF16), w1_ref[...])
        rl = jnp.maximum(a1, 0.0)
        o_ref[...] = xv + _dot((rl * rl).astype(BF16), w2_ref[...])

    return _pc(
        body, name="mlp_fwd", grid=(T // tm,),
        in_specs=[_row_spec(tm, 1024), _const_spec((1, 1024)), _hbm_spec(), _hbm_spec()],
        out_specs=_row_spec(tm, 1024), out_shape=jax.ShapeDtypeStruct((T, 1024), F32),
        scratch_shapes=[pltpu.VMEM((D_MODEL, D_FF), BF16), pltpu.VMEM((D_FF, D_MODEL), BF16)],
        compiler_params=_cparams(("arbitrary",)),
    )(x1, g, w1, w2)


def _mlp_bwd(x1, dx2, g, w1, w2):
    T = x1.shape[0]
    tm = TOKEN_TILE

    def body(x_ref, dx2_ref, g_ref, w1_hbm, w2_hbm, dx1_ref, h_ref, act_ref, da_ref, dg_ref, w1_ref, w2_ref):
        @pl.when(pl.program_id(0) == 0)
        def _():
            pltpu.sync_copy(w1_hbm, w1_ref)
            pltpu.sync_copy(w2_hbm, w2_ref)
            dg_ref[...] = jnp.zeros_like(dg_ref)

        xn, r = _rms(x_ref[...])
        gv = g_ref[...]
        h = (xn * gv).astype(BF16)
        h_ref[...] = h
        rl = jnp.maximum(_dot(h, w1_ref[...]), 0.0)
        act_ref[...] = (rl * rl).astype(BF16)
        dx2 = dx2_ref[...]
        da = (_dot_nt(dx2.astype(BF16), w2_ref[...]) * (2.0 * rl)).astype(BF16)
        da_ref[...] = da
        dh = _dot_nt(da, w1_ref[...])
        dg_ref[...] += jnp.sum(dh * xn, axis=0, keepdims=True)
        dx1_ref[...] = dx2 + _rms_bwd(xn, r, dh * gv)

    return _pc(
        body, name="mlp_bwd", grid=(T // tm,),
        in_specs=[_row_spec(tm, 1024), _row_spec(tm, 1024), _const_spec((1, 1024)), _hbm_spec(), _hbm_spec()],
        out_specs=[_row_spec(tm, 1024), _row_spec(tm, 1024), _row_spec(tm, D_FF), _row_spec(tm, D_FF), _const_spec((1, 1024))],
        out_shape=[jax.ShapeDtypeStruct((T, 1024), F32), jax.ShapeDtypeStruct((T, 1024), BF16),
                   jax.ShapeDtypeStruct((T, D_FF), BF16), jax.ShapeDtypeStruct((T, D_FF), BF16),
                   jax.ShapeDtypeStruct((1, 1024), F32)],
        scratch_shapes=[pltpu.VMEM((D_MODEL, D_FF), BF16), pltpu.VMEM((D_FF, D_MODEL), BF16)],
        compiler_params=_cparams(("arbitrary",)),
    )(x1, dx2, g, w1, w2)


def _loss_head(x2, target, g):
    T = x2.shape[0]
    tm = TOKEN_TILE

    def body(x_ref, t_ref, g_ref, dx_ref, loss_ref, dg_ref):
        @pl.when(pl.program_id(0) == 0)
        def _():
            loss_ref[...] = jnp.zeros_like(loss_ref)
            dg_ref[...] = jnp.zeros_like(dg_ref)

        xn, r = _rms(x_ref[...])
        gv = g_ref[...]
        err = xn * gv - t_ref[...]
        loss_ref[...] += jnp.sum(err * err, axis=0, keepdims=True) * (0.5 / D_MODEL)
        dy = err * (1.0 / D_MODEL)
        dg_ref[...] += jnp.sum(dy * xn, axis=0, keepdims=True)
        dx_ref[...] = _rms_bwd(xn, r, dy * gv)

    return _pc(
        body, name="loss_head", grid=(T // tm,),
        in_specs=[_row_spec(tm, 1024), _row_spec(tm, 1024), _const_spec((1, 1024))],
        out_specs=[_row_spec(tm, 1024), _const_spec((1, 1024)), _const_spec((1, 1024))],
        out_shape=[jax.ShapeDtypeStruct((T, 1024), F32), jax.ShapeDtypeStruct((1, 1024), F32),
                   jax.ShapeDtypeStruct((1, 1024), F32)],
        compiler_params=_cparams(("arbitrary",)),
    )(x2, target, g)


WGRAD_TOKENS = 512
WGRAD_OUT_ELEMS = 2 * 1024 * 1024


def _wgrad(a, b, name):
    T, K = a.shape
    N = b.shape[1]
    tt = min(T, WGRAD_TOKENS)
    nb = min(N, max(128, WGRAD_OUT_ELEMS // K))
    assert N % nb == 0 and T % tt == 0

    def body(a_ref, b_ref, o_ref):
        @pl.when(pl.program_id(1) == 0)
        def _():
            o_ref[...] = jnp.zeros_like(o_ref)

        o_ref[...] += _dot_tn(a_ref[...].astype(BF16), b_ref[...].astype(BF16))

    return _pc(
        body, name=name, grid=(N // nb, T // tt),
        in_specs=[pl.BlockSpec((tt, K), lambda n, t: (t, 0)), pl.BlockSpec((tt, nb), lambda n, t: (t, n))],
        out_specs=pl.BlockSpec((K, nb), lambda n, t: (0, n)),
        out_shape=jax.ShapeDtypeStruct((K, N), F32),
        compiler_params=_cparams(("parallel", "arbitrary")),
    )(a, b)


def _s5_block_weights(bb_re, bb_im, c_re, c_im):
    eye = jnp.eye(8, dtype=F32)
    bre = bb_re.reshape(S5_BLOCKS, 8, 64, 16)
    bim = bb_im.reshape(S5_BLOCKS, 8, 64, 16)
    wb_re = jnp.einsum('jgpk,gh->jhkgp', bre, eye).reshape(S5_BLOCKS, _BI, _BW)
    wb_im = jnp.einsum('jgpk,gh->jhkgp', bim, eye).reshape(S5_BLOCKS, _BI, _BW)
    wb4 = jnp.concatenate([wb_re, wb_im], axis=2).astype(BF16)
    cre = c_re.reshape(S5_BLOCKS, 8, 16, 64)
    cim = c_im.reshape(S5_BLOCKS, 8, 16, 64)
    wc_re = jnp.einsum('jgkp,gh->jgphk', cre, eye).reshape(S5_BLOCKS, _BW, _BI)
    wc_im = jnp.einsum('jgkp,gh->jgphk', -cim, eye).reshape(S5_BLOCKS, _BW, _BI)
    wc4 = jnp.concatenate([wc_re, wc_im], axis=1).astype(BF16)
    return wb4, wc4


def _s5_block_grads(dwb4, dwc4):
    eye = jnp.eye(8, dtype=F32)
    dwb = dwb4.reshape(S5_BLOCKS, 8, 16, 2, 8, 64)
    dbb = jnp.einsum('jhkrgp,gh->rjgpk', dwb, eye).reshape(2, 32, 64, 16)
    dwc = dwc4.reshape(S5_BLOCKS, 2, 8, 64, 8, 16)
    dc = jnp.einsum('jrgphk,gh->rjgkp', dwc, eye).reshape(2, 32, 16, 64)
    return dbb[0], dbb[1], dc[0], -dc[1]


def _permute_w_in(w_in):
    pad = jnp.zeros((D_MODEL, DT_PAD - 16), w_in.dtype)
    return jnp.concatenate([w_in[:, :OFF_DT], w_in[:, OFF_U:], w_in[:, OFF_DT:OFF_U], pad], axis=1)


def _row(v, width=None):
    v = v.reshape(1, -1)
    if width is not None and v.shape[1] < width:
        v = jnp.concatenate([v, jnp.zeros((1, width - v.shape[1]), v.dtype)], axis=1)
    return v


def _local_step(x, target, p):
    g_mix, g_mlp, g_fin = _row(p["norm_mix_g"]), _row(p["norm_mlp_g"]), _row(p["norm_final_g"])
    conv_b = _row(p["conv_b"])
    dt_bias = _row(p["dt_bias"], DT_PAD)
    alog = _row(p["a_log"], DT_PAD)
    dvec = _row(jnp.repeat(p["d_ssd"], SSD_HEADDIM))
    gssd = _row(p["ssd_norm_g"])
    s5d = _row(p["s5_d"])
    glu_b = _row(p["s5_glu_b"])
    head_sel = (jnp.arange(SSD_INNER)[:, None] // SSD_HEADDIM == jnp.arange(DT_PAD)[None, :]).astype(F32)

    a_re = p["s5_a_re"].reshape(S5_STATES, 1)
    a_im = p["s5_a_im"].reshape(S5_STATES, 1)
    log_dt = jnp.repeat(p["s5_log_dt"], 64).reshape(S5_STATES, 1)
    b_re = p["s5_b_re"].reshape(S5_STATES, 16)
    b_im = p["s5_b_im"].reshape(S5_STATES, 16)
    ab_re, ab_im, bb_re, bb_im = _s5_disc(a_re, a_im, log_dt, b_re, b_im)
    wb4, wc4 = _s5_block_weights(bb_re, bb_im, p["s5_c_re"], p["s5_c_im"])
    ab = jnp.concatenate([ab_re.reshape(1, S5_STATES), ab_im.reshape(1, S5_STATES), jnp.zeros((6, S5_STATES), F32)], axis=0)

    wp, wbr, wout, w1, w2, glu_w = p["w_in_perm"], p["w_branch"], p["w_out"], p["w_mlp_in"], p["w_mlp_out"], p["s5_glu_w"]

    z, xbc_raw, u5, gates, dt_raw = _inproj_fwd(x, g_mix, wp)
    xbc_act, dt = _conv_fwd(xbc_raw, dt_raw, p["conv_w"], conv_b, dt_bias)
    ys, ssd_states = _ssd_fwd(xbc_act, dt, alog)
    y5, s5_states = _s5_fwd(u5, wb4, wc4, ab, s5d)
    x1 = _merge_fwd(ys, xbc_act, z, y5, gates, x, dvec, gssd, glu_w, glu_b, wbr, wout)
    x2 = _mlp_fwd(x1, g_mlp, w1, w2)
    dx2, loss_lanes, d_gfin = _loss_head(x2, target, g_fin)

    dx1, h2, act, da1, d_gmlp = _mlp_bwd(x1, dx2, g_mlp, w1, w2)
    d_w_mlp_out = _wgrad(act, dx2, "wgrad_mlp_out")
    d_w_mlp_in = _wgrad(h2, da1, "wgrad_mlp_in")
    (dys, dxs_m, dz, dy5, dgates, mg, ya, yb, dpa, dpb, gel, dpre, d_dssd, d_gssd, d_glu_b) = _merge_bwd(
        ys, xbc_act, z, y5, gates, dx1, dvec, gssd, glu_w, glu_b, wbr, wout, head_sel)
    d_w_out = _wgrad(mg, dx1, "wgrad_out")
    d_w_branch = jnp.concatenate([_wgrad(ya, dpa, "wgrad_branch_a"), _wgrad(yb, dpb, "wgrad_branch_b")], axis=0)
    d_glu_w = _wgrad(gel, dpre, "wgrad_glu")
    du5, dwb4, dwc4, dab, d_s5d = _s5_bwd(u5, dy5, wb4, wc4, ab, s5d, s5_states)
    dbb_re, dbb_im, d_c_re, d_c_im = _s5_block_grads(dwb4, dwc4)
    d_a_re, d_a_im, d_log_dt, d_b_re, d_b_im = _s5_disc_bwd(
        a_re, a_im, log_dt, b_re, b_im, dab[0].reshape(S5_STATES, 1), dab[1].reshape(S5_STATES, 1),
        dbb_re.reshape(S5_STATES, 16), dbb_im.reshape(S5_STATES, 16))
    dxs_s, dB, dC, ddt, d_alog = _ssd_bwd(xbc_act, dt, alog, ssd_states, dys)
    dxbc_raw, ddt_raw, d_conv_w, d_conv_b, d_dt_bias = _conv_bwd(
        xbc_raw, dt_raw, dxs_m, dxs_s, dB, dC, ddt, p["conv_w"], conv_b, dt_bias)
    dx, h, d_gmix = _inproj_bwd(x, dx1, dz, dxbc_raw, du5, dgates, ddt_raw, g_mix, wp)
    d_w_in = jnp.concatenate([
        _wgrad(h, dz, "wgrad_in_z"), _wgrad(h, dxbc_raw, "wgrad_in_xbc"), _wgrad(h, ddt_raw, "wgrad_in_dt")[:, :16],
        _wgrad(h, du5, "wgrad_in_u5"), _wgrad(h, dgates, "wgrad_in_gates")], axis=1)

    grads = dict(
        norm_mix_g=d_gmix.reshape(-1), w_in=d_w_in, conv_w=d_conv_w[:CONV_K], conv_b=d_conv_b.reshape(-1),
        dt_bias=d_dt_bias[0, :16], a_log=d_alog[0, :16], d_ssd=d_dssd[0, :16], ssd_norm_g=d_gssd.reshape(-1),
        s5_a_re=d_a_re.reshape(32, 64), s5_a_im=d_a_im.reshape(32, 64), s5_log_dt=d_log_dt.reshape(32),
        s5_b_re=d_b_re.reshape(32, 64, 16), s5_b_im=d_b_im.reshape(32, 64, 16), s5_c_re=d_c_re, s5_c_im=d_c_im,
        s5_d=d_s5d.reshape(-1), s5_glu_w=d_glu_w, s5_glu_b=d_glu_b.reshape(-1), w_branch=d_w_branch, w_out=d_w_out,
        norm_mlp_g=d_gmlp.reshape(-1), w_mlp_in=d_w_mlp_in, w_mlp_out=d_w_mlp_out, norm_final_g=d_gfin.reshape(-1))
    return jnp.sum(loss_lanes), dx, grads


MESH = pl.DeviceIdType.MESH
N_CHIPS = 4


def _place():
    x, y, c = lax.axis_index("x"), lax.axis_index("y"), lax.axis_index("c")
    chips = [(1 - x, y), (x, 1 - y), (1 - x, 1 - y)]
    return x, y, c, chips


def _remote(src, dst, send_sems, recv_sems, k, to):
    return pltpu.make_async_remote_copy(src_ref=src, dst_ref=dst, send_sem=send_sems.at[k], recv_sem=recv_sems.at[k],
                                        device_id=to, device_id_type=MESH)


def _allgather_chips(src, name):
    R, C = src.shape
    H = R // 2

    def body(src_ref, out_ref, send_sems, recv_sems, local_sem):
        x, y, c, chips = _place()
        own = 2 * x + y
        sib = (x, y, 1 - c)

        def half(s, hc):
            return out_ref.at[s, pl.ds(hc * H, H), :]

        mine = pltpu.make_async_copy(src_ref, out_ref.at[own], local_sem)
        mine.start()
        first = [_remote(src_ref.at[pl.ds(c * H, H), :], half(own, c), send_sems, recv_sems, j, (cx, cy, c))
                 for j, (cx, cy) in enumerate(chips)]
        for cp in first:
            cp.start()
        passed = []
        for j, (cx, cy) in enumerate(chips):
            got = half(2 * cx + cy, c)
            _remote(got, got, send_sems, recv_sems, j, (cx, cy, c)).wait_recv()
            fw = _remote(got, got, send_sems, recv_sems, 3 + j, sib)
            fw.start()
            passed.append(fw)
        for j, (cx, cy) in enumerate(chips):
            got = half(2 * cx + cy, 1 - c)
            _remote(got, got, send_sems, recv_sems, 3 + j, sib).wait_recv()
        for cp in first + passed:
            cp.wait_send()
        mine.wait()

    return _pc(
        body, name=name, in_specs=[_hbm_spec()], out_specs=_hbm_spec(),
        out_shape=jax.ShapeDtypeStruct((N_CHIPS, R, C), src.dtype),
        scratch_shapes=[pltpu.SemaphoreType.DMA((6,)), pltpu.SemaphoreType.DMA((6,)), pltpu.SemaphoreType.DMA],
    )(src)


def _pair_exchange(gpack, small):
    _, R, C = gpack.shape
    H = R // 2

    def body(g_ref, s_ref, mine_ref, sib_ref, sibs_ref, send_sems, recv_sems, local_sem):
        x, y, c, _ = _place()
        sib = (x, y, 1 - c)
        keep = pltpu.make_async_copy(g_ref.at[:, pl.ds(c * H, H), :], mine_ref, local_sem)
        keep.start()
        big = _remote(g_ref.at[:, pl.ds((1 - c) * H, H), :], sib_ref, send_sems, recv_sems, 0, sib)
        sm = _remote(s_ref, sibs_ref, send_sems, recv_sems, 1, sib)
        big.start()
        sm.start()
        big.wait()
        sm.wait()
        keep.wait()

    return _pc(
        body, name="pair_exchange", in_specs=[_hbm_spec(), _hbm_spec()], out_specs=[_hbm_spec()] * 3,
        out_shape=[jax.ShapeDtypeStruct((N_CHIPS, H, C), F32), jax.ShapeDtypeStruct((N_CHIPS, H, C), F32),
                   jax.ShapeDtypeStruct(small.shape, F32)],
        scratch_shapes=[pltpu.SemaphoreType.DMA((2,)), pltpu.SemaphoreType.DMA((2,)), pltpu.SemaphoreType.DMA],
    )(gpack, small)


PACK_BLOCK_ROWS = 4


def _pair_sum(mine, sib, small, sib_small):
    n, H, C = mine.shape
    rb = H // PACK_BLOCK_ROWS
    assert H % PACK_BLOCK_ROWS == 0 and rb % 16 == 0

    def body(a_ref, b_ref, s_ref, t_ref, pf_ref, pb_ref, ps_ref):
        p = a_ref[...] + b_ref[...]
        pf_ref[...] = p
        pb_ref[...] = p.astype(BF16)

        @pl.when((pl.program_id(0) == 0) & (pl.program_id(1) == 0))
        def _():
            ps_ref[...] = s_ref[...] + t_ref[...]

    blk = pl.BlockSpec((1, rb, C), lambda s, i: (s, i, 0))
    sm = pl.BlockSpec(small.shape, lambda s, i: (0, 0))
    return _pc(
        body, name="pair_sum", grid=(n, PACK_BLOCK_ROWS), in_specs=[blk, blk, sm, sm], out_specs=[blk, blk, sm],
        out_shape=[jax.ShapeDtypeStruct(mine.shape, F32), jax.ShapeDtypeStruct(mine.shape, BF16),
                   jax.ShapeDtypeStruct(small.shape, F32)],
        compiler_params=_cparams(("arbitrary", "arbitrary")),
    )(mine, sib, small, sib_small)


def _chip_exchange(pf, pb, psmall):
    _, H, C = pf.shape

    def body(pf_ref, pb_ref, ps_ref, own_ref, got_ref, small4_ref, send_sems, recv_sems, local_sems):
        x, y, c, chips = _place()
        own = 2 * x + y
        keep = pltpu.make_async_copy(pf_ref.at[own], own_ref, local_sems.at[0])
        keep_s = pltpu.make_async_copy(ps_ref, small4_ref.at[own], local_sems.at[1])
        keep.start()
        keep_s.start()
        copies = []
        for j, (cx, cy) in enumerate(chips):
            copies.append(_remote(pb_ref.at[2 * cx + cy], got_ref.at[j], send_sems, recv_sems, j, (cx, cy, c)))
            copies.append(_remote(ps_ref, small4_ref.at[own], send_sems, recv_sems, 3 + j, (cx, cy, c)))
        for cp in copies:
            cp.start()
        for j, (cx, cy) in enumerate(chips):
            _remote(pb_ref.at[own], got_ref.at[j], send_sems, recv_sems, j, (cx, cy, c)).wait_recv()
            _remote(ps_ref, small4_ref.at[2 * cx + cy], send_sems, recv_sems, 3 + j, (cx, cy, c)).wait_recv()
        for cp in copies:
            cp.wait_send()
        keep.wait()
        keep_s.wait()

    return _pc(
        body, name="chip_exchange", in_specs=[_hbm_spec()] * 3, out_specs=[_hbm_spec()] * 3,
        out_shape=[jax.ShapeDtypeStruct((H, C), F32), jax.ShapeDtypeStruct((3, H, C), BF16),
                   jax.ShapeDtypeStruct((N_CHIPS,) + psmall.shape, F32)],
        scratch_shapes=[pltpu.SemaphoreType.DMA((6,)), pltpu.SemaphoreType.DMA((6,)), pltpu.SemaphoreType.DMA((2,))],
    )(pf, pb, psmall)


def _chip_sum(own, got, small4):
    H, C = own.shape
    rb = H // PACK_BLOCK_ROWS

    def body(o_ref, g_ref, s_ref, tot_ref, st_ref):
        tot_ref[...] = ((o_ref[...] + g_ref[0].astype(F32)) + g_ref[1].astype(F32)) + g_ref[2].astype(F32)

        @pl.when(pl.program_id(0) == 0)
        def _():
            st_ref[...] = ((s_ref[0] + s_ref[1]) + s_ref[2]) + s_ref[3]

    return _pc(
        body, name="chip_sum", grid=(PACK_BLOCK_ROWS,),
        in_specs=[pl.BlockSpec((rb, C), lambda i: (i, 0)), pl.BlockSpec((3, rb, C), lambda i: (0, i, 0)),
                  _const_spec(small4.shape)],
        out_specs=[pl.BlockSpec((rb, C), lambda i: (i, 0)), _const_spec(small4.shape[1:])],
        out_shape=[jax.ShapeDtypeStruct((H, C), F32), jax.ShapeDtypeStruct(small4.shape[1:], F32)],
        compiler_params=_cparams(("arbitrary",)),
    )(own, got, small4)


def _half_exchange(tot):
    H, C = tot.shape

    def body(t_ref, out_ref, send_sems, recv_sems, local_sem):
        x, y, c, _ = _place()
        keep = pltpu.make_async_copy(t_ref, out_ref.at[pl.ds(c * H, H), :], local_sem)
        keep.start()
        cp = _remote(t_ref, out_ref.at[pl.ds(c * H, H), :], send_sems, recv_sems, 0, (x, y, 1 - c))
        cp.start()
        other = out_ref.at[pl.ds((1 - c) * H, H), :]
        _remote(other, other, send_sems, recv_sems, 0, (x, y, 1 - c)).wait_recv()
        cp.wait_send()
        keep.wait()

    return _pc(
        body, name="half_exchange", in_specs=[_hbm_spec()], out_specs=_hbm_spec(),
        out_shape=jax.ShapeDtypeStruct((2 * H, C), F32),
        scratch_shapes=[pltpu.SemaphoreType.DMA((1,)), pltpu.SemaphoreType.DMA((1,)), pltpu.SemaphoreType.DMA],
    )(tot)


def _adamw(w, g, m, v, name):
    R, C = w.shape
    rb = 256 if R % 256 == 0 else (128 if R % 128 == 0 else R)

    def body(w_ref, g_ref, m_ref, v_ref, d_ref, nm_ref, nv_ref):
        gv = g_ref[...]
        m2 = ADAM_B1 * m_ref[...] + (1.0 - ADAM_B1) * gv
        v2 = ADAM_B2 * v_ref[...] + (1.0 - ADAM_B2) * (gv * gv)
        m_hat = m2 / (1.0 - ADAM_B1 ** ADAM_STEP)
        v_hat = v2 / (1.0 - ADAM_B2 ** ADAM_STEP)
        d_ref[...] = -ADAM_LR * (m_hat / (jnp.sqrt(v_hat) + ADAM_EPS) + ADAM_WD * w_ref[...])
        nm_ref[...] = m2
        nv_ref[...] = v2

    spec = pl.BlockSpec((rb, C), lambda i: (i, 0))
    return _pc(
        body, name=name, grid=(R // rb,), in_specs=[spec] * 4, out_specs=[spec] * 3,
        out_shape=[jax.ShapeDtypeStruct((R, C), F32)] * 3, compiler_params=_cparams(("parallel",)),
    )(w, g, m, v)


PACK_COLS = 1024
PACK_ROWS = 4224
BIG = (("w_in", (1024, 1412), 1), ("s5_glu_w", (128, 512), 0), ("w_branch", (384, 1024), 0), ("w_out", (256, 1024), 0),
       ("w_mlp_in", (1024, 1024), 1), ("w_mlp_out", (1024, 1024), 0), ("conv_w", (4, 512), 1))
SMALL = (("norm_mix_g", (1024,)), ("conv_b", (2048,)), ("dt_bias", (16,)), ("a_log", (16,)), ("d_ssd", (16,)),
         ("ssd_norm_g", (1024,)), ("s5_a_re", (32, 64)), ("s5_a_im", (32, 64)), ("s5_log_dt", (32,)),
         ("s5_b_re", (32, 64, 16)), ("s5_b_im", (32, 64, 16)), ("s5_c_re", (32, 16, 64)), ("s5_c_im", (32, 16, 64)),
         ("s5_d", (512,)), ("s5_glu_b", (512,)), ("norm_mlp_g", (1024,)), ("norm_final_g", (1024,)))
SMALL_ROWS = 144


PART_ALIGN = 16


def _part_rows(n):
    return -(-n // PART_ALIGN) * PART_ALIGN


def _pack_rows(parts, rows, dtype):
    flat, used = [], 0
    for a in parts:
        a = a.astype(dtype).reshape(-1, PACK_COLS)
        n = a.shape[0]
        if _part_rows(n) != n:
            a = jnp.pad(a, ((0, _part_rows(n) - n), (0, 0)))
        flat.append(a)
        used += a.shape[0]
    return jnp.concatenate(flat + [jnp.zeros((rows - used, PACK_COLS), dtype)], axis=0)


def _unpack_rows(pack, shapes):
    out, r = [], 0
    for shp in shapes:
        n = math.prod(shp) // PACK_COLS
        out.append(pack[r:r + n].reshape(shp))
        r += _part_rows(n)
    return out


def _pack_small(parts):
    flat = jnp.concatenate([a.astype(F32).reshape(-1) for a in parts])
    return jnp.concatenate([flat, jnp.zeros((SMALL_ROWS * PACK_COLS - flat.shape[0],), F32)]).reshape(SMALL_ROWS, PACK_COLS)


def _unpack_small(pack):
    flat, out, r = pack.reshape(-1), {}, 0
    for name, shp in SMALL:
        n = math.prod(shp)
        out[name] = flat[r:r + n].reshape(shp)
        r += n
    return out


def _join_shards(gathered, names_shapes):
    per_chip = [_unpack_rows(gathered[s], [shp for _, shp, _ in names_shapes]) for s in range(N_CHIPS)]
    return {name: jnp.concatenate([per_chip[s][i] for s in range(N_CHIPS)], axis=axis)
            for i, (name, _, axis) in enumerate(names_shapes)}


def _split_shards(full):
    packs = []
    for s in range(N_CHIPS):
        parts = []
        for name, shp, axis in BIG:
            n = shp[axis]
            parts.append(lax.slice_in_dim(full[name], s * n, (s + 1) * n, axis=axis))
        packs.append(_pack_rows(parts, PACK_ROWS, F32))
    return jnp.stack(packs)


def kernel(x, norm_mix_g, w_in, conv_w, conv_b, dt_bias, a_log, d_ssd, ssd_norm_g, s5_a_re, s5_a_im, s5_log_dt, s5_b_re, s5_b_im, s5_c_re, s5_c_im, s5_d, s5_glu_w, s5_glu_b, w_branch, w_out, norm_mlp_g, w_mlp_in, w_mlp_out, norm_final_g, loss_target, m_norm_mix_g, m_w_in, m_conv_w, m_conv_b, m_dt_bias, m_a_log, m_d_ssd, m_ssd_norm_g, m_s5_a_re, m_s5_a_im, m_s5_log_dt, m_s5_b_re, m_s5_b_im, m_s5_c_re, m_s5_c_im, m_s5_d, m_s5_glu_w, m_s5_glu_b, m_w_branch, m_w_out, m_norm_mlp_g, m_w_mlp_in, m_w_mlp_out, m_norm_final_g, v_norm_mix_g, v_w_in, v_conv_w, v_conv_b, v_dt_bias, v_a_log, v_d_ssd, v_ssd_norm_g, v_s5_a_re, v_s5_a_im, v_s5_log_dt, v_s5_b_re, v_s5_b_im, v_s5_c_re, v_s5_c_im, v_s5_d, v_s5_glu_w, v_s5_glu_b, v_w_branch, v_w_out, v_norm_mlp_g, v_w_mlp_in, v_w_mlp_out, v_norm_final_g):
    names = ("norm_mix_g", "w_in", "conv_w", "conv_b", "dt_bias", "a_log", "d_ssd", "ssd_norm_g", "s5_a_re", "s5_a_im",
             "s5_log_dt", "s5_b_re", "s5_b_im", "s5_c_re", "s5_c_im", "s5_d", "s5_glu_w", "s5_glu_b", "w_branch", "w_out",
             "norm_mlp_g", "w_mlp_in", "w_mlp_out", "norm_final_g")
    w = dict(zip(names, (norm_mix_g, w_in, conv_w, conv_b, dt_bias, a_log, d_ssd, ssd_norm_g, s5_a_re, s5_a_im, s5_log_dt,
                         s5_b_re, s5_b_im, s5_c_re, s5_c_im, s5_d, s5_glu_w, s5_glu_b, w_branch, w_out, norm_mlp_g,
                         w_mlp_in, w_mlp_out, norm_final_g)))
    m = dict(zip(names, (m_norm_mix_g, m_w_in, m_conv_w, m_conv_b, m_dt_bias, m_a_log, m_d_ssd, m_ssd_norm_g, m_s5_a_re,
                         m_s5_a_im, m_s5_log_dt, m_s5_b_re, m_s5_b_im, m_s5_c_re, m_s5_c_im, m_s5_d, m_s5_glu_w,
                         m_s5_glu_b, m_w_branch, m_w_out, m_norm_mlp_g, m_w_mlp_in, m_w_mlp_out, m_norm_final_g)))
    v = dict(zip(names, (v_norm_mix_g, v_w_in, v_conv_w, v_conv_b, v_dt_bias, v_a_log, v_d_ssd, v_ssd_norm_g, v_s5_a_re,
                         v_s5_a_im, v_s5_log_dt, v_s5_b_re, v_s5_b_im, v_s5_c_re, v_s5_c_im, v_s5_d, v_s5_glu_w,
                         v_s5_glu_b, v_w_branch, v_w_out, v_norm_mlp_g, v_w_mlp_in, v_w_mlp_out, v_norm_final_g)))

    mats = [(n, s, a) for n, s, a in BIG if n != "conv_w"]
    wpack = _pack_rows([w[n] for n, _, _ in mats], PACK_ROWS, BF16)
    full = _join_shards(_allgather_chips(wpack, "gather_weights"), mats)
    cpack = jnp.concatenate([conv_w, jnp.zeros((12, 512), F32)], axis=0)
    conv_full = _allgather_chips(cpack, "gather_conv")[:, :CONV_K, :]
    p = {n: w[n] for n, _ in SMALL}
    p["conv_w"] = jnp.concatenate([conv_full[s] for s in range(N_CHIPS)], axis=1)
    p["w_in_perm"] = _permute_w_in(full["w_in"])
    for n in ("s5_glu_w", "w_branch", "w_out", "w_mlp_in", "w_mlp_out"):
        p[n] = full[n]

    loss_part, grad_x, g = _local_step(x[0], loss_target[0], p)
    loss = lax.psum(loss_part, ("x", "y", "c"))

    gpack = _split_shards(g)
    spack = _pack_small([g[n] for n, _ in SMALL])
    mine, sib, sib_small = _pair_exchange(gpack, spack)
    pf, pb, psmall = _pair_sum(mine, sib, spack, sib_small)
    own, got, small4 = _chip_exchange(pf, pb, psmall)
    tot, small_tot = _chip_sum(own, got, small4)
    gshard = _unpack_rows(_half_exchange(tot), [shp for _, shp, _ in BIG])
    grads = _unpack_small(small_tot)
    for (n, _, _), gs in zip(BIG, gshard):
        grads[n] = gs

    delta, new_m, new_v = {}, {}, {}
    for n, _, _ in BIG:
        delta[n], new_m[n], new_v[n] = _adamw(w[n], grads[n], m[n], v[n], "adamw_" + n)
    ds, ms, vs = _adamw(_pack_small([w[n] for n, _ in SMALL]), small_tot, _pack_small([m[n] for n, _ in SMALL]),
                        _pack_small([v[n] for n, _ in SMALL]), "adamw_small")
    delta.update(_unpack_small(ds))
    new_m.update(_unpack_small(ms))
    new_v.update(_unpack_small(vs))

    return (loss, grad_x[None], *[grads[n] for n in names], *[delta[n] for n in names],
            *[new_m[n] for n in names], *[new_v[n] for n in names])
```

```python
import functools
import math

import jax
import jax.numpy as jnp
from jax import lax
from jax.experimental import pallas as pl
from jax.experimental.pallas import tpu as pltpu

F32 = jnp.float32
BF16 = jnp.bfloat16

D_MODEL = 1024
SSD_INNER = 1024
SSD_HEADS = 16
SSD_HEADDIM = 64
SSD_GROUPS = 4
SSD_HPG = 4
SSD_STATE = 128
SSD_CHUNK = 128
CONV_K = 4
CONV_DIM = 2048
S5_WIDTH = 512
S5_STATES = 2048
S5_BLOCKS = 4
S5_CHUNK = 128
D_FF = 4096
EPS = 1e-6
P_Z, P_XBC, P_U5, P_G, P_DT, P_END = 0, 1024, 3072, 3584, 5632, 5760
DT_PAD = 128
OFF_DT, OFF_U = 3072, 3088
D_IN_PROJ = 5648

ADAM_LR, ADAM_B1, ADAM_B2, ADAM_EPS, ADAM_WD, ADAM_STEP = 0.001, 0.9, 0.999, 1e-08, 0.01, 10

TOKEN_TILE = 256
VMEM_LIMIT = 56 * 1024 * 1024
HALO = 8


def _pc(body, **kw):
    return pl.pallas_call(body, **kw)


def _cparams(sem=None):
    return pltpu.CompilerParams(dimension_semantics=sem, vmem_limit_bytes=VMEM_LIMIT)


def _dot(a, b):
    return jnp.dot(a, b, preferred_element_type=F32)


def _dot_nt(a, b):
    return lax.dot_general(a, b, (((1,), (1,)), ((), ())), preferred_element_type=F32)


def _dot_tn(a, b):
    return lax.dot_general(a, b, (((0,), (0,)), ((), ())), preferred_element_type=F32)


def _dot_hi(a, b, dims=(((1,), (0,)), ((), ()))):
    return lax.dot_general(a, b, dims, preferred_element_type=F32, precision=lax.Precision.HIGHEST)


def _sigmoid(x):
    return 1.0 / (1.0 + jnp.exp(-x))


def _softplus(x):
    return jnp.maximum(x, 0.0) + jnp.log(1.0 + jnp.exp(-jnp.abs(x)))


_GELU_C = math.sqrt(2.0 / math.pi)


def _gelu(x):
    return 0.5 * x * (1.0 + jnp.tanh(_GELU_C * (x + 0.044715 * x * x * x)))


def _gelu_grad(x):
    t = jnp.tanh(_GELU_C * (x + 0.044715 * x * x * x))
    return 0.5 * (1.0 + t) + 0.5 * x * (1.0 - t * t) * _GELU_C * (1.0 + 3.0 * 0.044715 * x * x)


def _rms(x):
    r = lax.rsqrt(jnp.mean(x * x, axis=-1, keepdims=True) + EPS)
    return x * r, r


def _rms_bwd(xn, r, dxn):
    return r * (dxn - xn * jnp.mean(dxn * xn, axis=-1, keepdims=True))


def _row_spec(tm, width, col=0):
    return pl.BlockSpec((tm, width), lambda i: (i, col))


def _const_spec(shape):
    nd = len(shape)
    return pl.BlockSpec(shape, lambda i: (0,) * nd)


def _hbm_spec():
    return pl.BlockSpec(memory_space=pl.ANY)


def _inproj_fwd(x, g, wp):
    T = x.shape[0]
    tm = TOKEN_TILE

    def body(x_ref, g_ref, w_hbm, z_ref, xbc_ref, u5_ref, gt_ref, dt_ref, w_ref):
        @pl.when(pl.program_id(0) == 0)
        def _():
            pltpu.sync_copy(w_hbm, w_ref)

        xn, _ = _rms(x_ref[...])
        h = (xn * g_ref[...]).astype(BF16)
        z_ref[...] = _dot(h, w_ref[:, P_Z:P_XBC])
        xbc_ref[...] = _dot(h, w_ref[:, P_XBC:P_U5])
        u5_ref[...] = _dot(h, w_ref[:, P_U5:P_G])
        gt_ref[...] = _dot(h, w_ref[:, P_G:P_DT])
        dt_ref[...] = _dot(h, w_ref[:, P_DT:P_END])

    widths = (1024, 2048, 512, 2048, DT_PAD)
    return _pc(
        body, name="inproj_fwd", grid=(T // tm,),
        in_specs=[_row_spec(tm, D_MODEL), _const_spec((1, D_MODEL)), _hbm_spec()],
        out_specs=[_row_spec(tm, w) for w in widths],
        out_shape=[jax.ShapeDtypeStruct((T, w), F32) for w in widths],
        scratch_shapes=[pltpu.VMEM((D_MODEL, P_END), BF16)],
        compiler_params=_cparams(("arbitrary",)),
    )(x, g, wp)


def _inproj_bwd(x, dx1, dz, dxbc, du5, dgt, ddt, g, wp):
    T = x.shape[0]
    tm = TOKEN_TILE

    def body(x_ref, dx1_ref, dz_ref, dxbc_ref, du5_ref, dgt_ref, ddt_ref, g_ref, w_hbm, dx_ref, h_ref, dg_ref, w_ref):
        @pl.when(pl.program_id(0) == 0)
        def _():
            pltpu.sync_copy(w_hbm, w_ref)
            dg_ref[...] = jnp.zeros_like(dg_ref)

        xn, r = _rms(x_ref[...])
        gv = g_ref[...]
        h_ref[...] = (xn * gv).astype(BF16)
        dh = _dot_nt(dz_ref[...].astype(BF16), w_ref[:, P_Z:P_XBC])
        dh += _dot_nt(dxbc_ref[...].astype(BF16), w_ref[:, P_XBC:P_U5])
        dh += _dot_nt(du5_ref[...].astype(BF16), w_ref[:, P_U5:P_G])
        dh += _dot_nt(dgt_ref[...].astype(BF16), w_ref[:, P_G:P_DT])
        dh += _dot_nt(ddt_ref[...].astype(BF16), w_ref[:, P_DT:P_END])
        dg_ref[...] += jnp.sum(dh * xn, axis=0, keepdims=True)
        dx_ref[...] = dx1_ref[...] + _rms_bwd(xn, r, dh * gv)

    return _pc(
        body, name="inproj_bwd", grid=(T // tm,),
        in_specs=[_row_spec(tm, 1024), _row_spec(tm, 1024), _row_spec(tm, 1024), _row_spec(tm, 2048),
                  _row_spec(tm, 512), _row_spec(tm, 2048), _row_spec(tm, DT_PAD), _const_spec((1, 1024)), _hbm_spec()],
        out_specs=[_row_spec(tm, 1024), _row_spec(tm, 1024), _const_spec((1, 1024))],
        out_shape=[jax.ShapeDtypeStruct((T, 1024), F32), jax.ShapeDtypeStruct((T, 1024), BF16),
                   jax.ShapeDtypeStruct((1, 1024), F32)],
        scratch_shapes=[pltpu.VMEM((D_MODEL, P_END), BF16)],
        compiler_params=_cparams(("arbitrary",)),
    )(x, dx1, dz, dxbc, du5, dgt, ddt, g, wp)


def _conv_fwd(xbc_raw, dt_raw, conv_w, conv_b, dt_bias):
    T = xbc_raw.shape[0]
    tm = TOKEN_TILE

    def body(u_ref, dtr_ref, w_ref, b_ref, db_ref, act_ref, dt_ref, ext_ref):
        @pl.when(pl.program_id(0) == 0)
        def _():
            ext_ref[0:HALO, :] = jnp.zeros((HALO, CONV_DIM), F32)

        ext_ref[HALO:, :] = u_ref[...]
        y = b_ref[...] + jnp.zeros((tm, CONV_DIM), F32)
        for k in range(CONV_K):
            y += w_ref[k:k + 1, :] * ext_ref[pl.ds(HALO - (CONV_K - 1) + k, tm), :]
        act_ref[...] = y * _sigmoid(y)
        ext_ref[0:HALO, :] = u_ref[tm - HALO:tm, :]
        dt_ref[...] = _softplus(dtr_ref[...] + db_ref[...])

    return _pc(
        body, name="conv_fwd", grid=(T // tm,),
        in_specs=[_row_spec(tm, CONV_DIM), _row_spec(tm, DT_PAD), _const_spec((CONV_K, CONV_DIM)),
                  _const_spec((1, CONV_DIM)), _const_spec((1, DT_PAD))],
        out_specs=[_row_spec(tm, CONV_DIM), _row_spec(tm, DT_PAD)],
        out_shape=[jax.ShapeDtypeStruct((T, CONV_DIM), F32), jax.ShapeDtypeStruct((T, DT_PAD), F32)],
        scratch_shapes=[pltpu.VMEM((tm + HALO, CONV_DIM), F32)],
        compiler_params=_cparams(("arbitrary",)),
    )(xbc_raw, dt_raw, conv_w, conv_b, dt_bias)


def _conv_bwd(xbc_raw, dt_raw, dxs_a, dxs_b, dB, dC, ddt, conv_w, conv_b, dt_bias):
    T = xbc_raw.shape[0]
    tm = TOKEN_TILE
    n = T // tm
    hb = tm // HALO

    def rev(width):
        return pl.BlockSpec((tm, width), lambda i: (n - 1 - i, 0))

    def body(u_ref, up_ref, dtr_ref, dxa_ref, dxb_ref, dB_ref, dC_ref, ddt_ref, w_ref, b_ref, db_ref,
             du_ref, ddtr_ref, dw_ref, dcb_ref, ddb_ref, ext_ref, dye_ref):
        i = pl.program_id(0)

        @pl.when(i == 0)
        def _():
            dye_ref[tm:, :] = jnp.zeros((HALO, CONV_DIM), F32)
            dw_ref[...] = jnp.zeros_like(dw_ref)
            dcb_ref[...] = jnp.zeros_like(dcb_ref)
            ddb_ref[...] = jnp.zeros_like(ddb_ref)

        first = (i == n - 1).astype(F32)
        ext_ref[0:HALO, :] = up_ref[...] * (1.0 - first)
        ext_ref[HALO:, :] = u_ref[...]
        y = b_ref[...] + jnp.zeros((tm, CONV_DIM), F32)
        for k in range(CONV_K):
            y += w_ref[k:k + 1, :] * ext_ref[pl.ds(HALO - (CONV_K - 1) + k, tm), :]
        s = _sigmoid(y)
        dsilu = s * (1.0 + y * (1.0 - s))
        dy = jnp.concatenate([dxa_ref[...] + dxb_ref[...], dB_ref[...], dC_ref[...]], axis=1) * dsilu
        dye_ref[0:tm, :] = dy
        dcb_ref[...] += jnp.sum(dy, axis=0, keepdims=True)
        du = jnp.zeros((tm, CONV_DIM), F32)
        for k in range(CONV_K):
            dw_ref[k:k + 1, :] += jnp.sum(dy * ext_ref[pl.ds(HALO - (CONV_K - 1) + k, tm), :], axis=0, keepdims=True)
            du += w_ref[k:k + 1, :] * dye_ref[pl.ds(CONV_K - 1 - k, tm), :]
        du_ref[...] = du
        dye_ref[tm:, :] = dy[0:HALO, :]
        sg = _sigmoid(dtr_ref[...] + db_ref[...])
        ddtr = ddt_ref[...] * sg
        ddtr_ref[...] = ddtr
        ddb_ref[...] += jnp.sum(ddtr, axis=0, keepdims=True)

    prev_spec = pl.BlockSpec((HALO, CONV_DIM), lambda i: (jnp.maximum((n - 1 - i) * hb - 1, 0), 0))
    return _pc(
        body, name="conv_bwd", grid=(n,),
        in_specs=[rev(CONV_DIM), prev_spec, rev(DT_PAD), rev(1024), rev(1024), rev(512), rev(512), rev(DT_PAD),
                  _const_spec((CONV_K, CONV_DIM)), _const_spec((1, CONV_DIM)), _const_spec((1, DT_PAD))],
        out_specs=[rev(CONV_DIM), rev(DT_PAD), _const_spec((HALO, CONV_DIM)), _const_spec((1, CONV_DIM)),
                   _const_spec((1, DT_PAD))],
        out_shape=[jax.ShapeDtypeStruct((T, CONV_DIM), F32), jax.ShapeDtypeStruct((T, DT_PAD), F32),
                   jax.ShapeDtypeStruct((HALO, CONV_DIM), F32), jax.ShapeDtypeStruct((1, CONV_DIM), F32),
                   jax.ShapeDtypeStruct((1, DT_PAD), F32)],
        scratch_shapes=[pltpu.VMEM((tm + HALO, CONV_DIM), F32), pltpu.VMEM((tm + HALO, CONV_DIM), F32)],
        compiler_params=_cparams(("arbitrary",)),
    )(xbc_raw, xbc_raw, dt_raw, dxs_a, dxs_b, dB, dC, ddt, conv_w, conv_b, dt_bias)


def _ssd_chunk_common(dt_ref, alog_ref):
    q = SSD_CHUNK
    a = -jnp.exp(alog_ref[...])
    dtv = dt_ref[...]
    la = dtv * a
    row = lax.broadcasted_iota(jnp.int32, (q, q), 0)
    col = lax.broadcasted_iota(jnp.int32, (q, q), 1)
    causal = col <= row
    tri = causal.astype(F32)
    cum = _dot_hi(tri, la)
    cum_t = _dot_hi(la, tri, (((0,), (1,)), ((), ())))
    return a, dtv, causal, tri, cum, cum_t


def _ssd_fwd(xbc_act, dt, alog):
    T = xbc_act.shape[0]
    q = SSD_CHUNK
    nc = T // q

    def body(xbc_ref, dt_ref, alog_ref, y_ref, sp_ref, s_ref):
        @pl.when(pl.program_id(0) == 0)
        def _():
            s_ref[...] = jnp.zeros_like(s_ref)

        a, dtv, causal, tri, cum, cum_t = _ssd_chunk_common(dt_ref, alog_ref)
        sp_ref[0] = s_ref[...]
        for g in range(SSD_GROUPS):
            bb = xbc_ref[:, 1024 + 128 * g:1152 + 128 * g].astype(BF16)
            cb = xbc_ref[:, 1536 + 128 * g:1664 + 128 * g].astype(BF16)
            gm = _dot_nt(cb, bb)
            for r in range(SSD_HPG):
                h = SSD_HPG * g + r
                x = xbc_ref[:, 64 * h:64 * h + 64]
                cc = cum[:, h:h + 1]
                cr = cum_t[h:h + 1, :]
                cl = cum[q - 1:q, h:h + 1]
                decay = jnp.where(causal, jnp.exp(jnp.minimum(cc - cr, 0.0)), 0.0)
                xd = x * dtv[:, h:h + 1]
                sp = s_ref[h]
                y = _dot((gm * decay).astype(BF16), xd.astype(BF16))
                y += _dot_nt(cb, sp.astype(BF16)) * jnp.exp(cc)
                y_ref[:, 64 * h:64 * h + 64] = y
                st = _dot_tn((xd * jnp.exp(cl - cc)).astype(BF16), bb)
                s_ref[h] = sp * jnp.exp(cl) + st

    return _pc(
        body, name="ssd_fwd", grid=(nc,),
        in_specs=[_row_spec(q, CONV_DIM), _row_spec(q, DT_PAD), _const_spec((1, DT_PAD))],
        out_specs=[_row_spec(q, SSD_INNER),
                   pl.BlockSpec((1, SSD_HEADS, SSD_HEADDIM, SSD_STATE), lambda i: (i, 0, 0, 0))],
        out_shape=[jax.ShapeDtypeStruct((T, SSD_INNER), F32),
                   jax.ShapeDtypeStruct((nc, SSD_HEADS, SSD_HEADDIM, SSD_STATE), F32)],
        scratch_shapes=[pltpu.VMEM((SSD_HEADS, SSD_HEADDIM, SSD_STATE), F32)],
        compiler_params=_cparams(("arbitrary",)),
    )(xbc_act, dt, alog)


def _ssd_bwd(xbc_act, dt, alog, sprev, dy):
    T = xbc_act.shape[0]
    q = SSD_CHUNK
    nc = T // q

    def rev(width):
        return pl.BlockSpec((q, width), lambda i: (nc - 1 - i, 0))

    def body(xbc_ref, dt_ref, alog_ref, sp_ref, dy_ref, dxs_ref, dB_ref, dC_ref, ddt_ref, dalog_ref, ds_ref):
        i = pl.program_id(0)

        @pl.when(i == 0)
        def _():
            ds_ref[...] = jnp.zeros_like(ds_ref)
            dalog_ref[...] = jnp.zeros_like(dalog_ref)

        a, dtv, causal, tri, cum, cum_t = _ssd_chunk_common(dt_ref, alog_ref)
        lane = lax.broadcasted_iota(jnp.int32, (1, DT_PAD), 1)
        rowq = lax.broadcasted_iota(jnp.int32, (q, 1), 0)
        dcum_all = jnp.zeros((q, DT_PAD), F32)
        ddt_all = jnp.zeros((q, DT_PAD), F32)
        for g in range(SSD_GROUPS):
            bb = xbc_ref[:, 1024 + 128 * g:1152 + 128 * g].astype(BF16)
            cb = xbc_ref[:, 1536 + 128 * g:1664 + 128 * g].astype(BF16)
            gm = _dot_nt(cb, bb)
            dgm = jnp.zeros((q, q), F32)
            dbg = jnp.zeros((q, SSD_STATE), F32)
            dcg = jnp.zeros((q, SSD_STATE), F32)
            for r in range(SSD_HPG):
                h = SSD_HPG * g + r
                x = xbc_ref[:, 64 * h:64 * h + 64]
                dyh = dy_ref[:, 64 * h:64 * h + 64]
                dtc = dtv[:, h:h + 1]
                cc = cum[:, h:h + 1]
                cr = cum_t[h:h + 1, :]
                cl = cum[q - 1:q, h:h + 1]
                decay = jnp.where(causal, jnp.exp(jnp.minimum(cc - cr, 0.0)), 0.0)
                m = gm * decay
                xd = x * dtc
                ec = jnp.exp(cc)
                de = jnp.exp(cl - cc)
                cd = jnp.exp(cl)
                sp = sp_ref[0, h]
                dsn = ds_ref[h]
                spb = sp.astype(BF16)
                dsnb = dsn.astype(BF16)
                dyb = dyh.astype(BF16)
                dye = (dyh * ec).astype(BF16)
                cs = _dot_nt(cb, spb)
                dcum = jnp.sum(dyh * cs, axis=1, keepdims=True) * ec
                dcg += _dot(dye, spb)
                dsp = dsn * cd + _dot_tn(dye, cb)
                dlast = jnp.sum(dsn * sp, keepdims=True) * cd
                dbg += _dot((xd * de).astype(BF16), dsnb)
                w = _dot_nt(bb, dsnb)
                dxd = w * de
                tde = jnp.sum(w * xd, axis=1, keepdims=True) * de
                dlast += jnp.sum(tde, keepdims=True)
                dcum -= tde
                dm = _dot_nt(dyb, xd.astype(BF16))
                dxd += _dot_tn(m.astype(BF16), dyb)
                dgm += dm * decay
                e = dm * m
                dcum += jnp.sum(e, axis=1, keepdims=True) - jnp.sum(e.T, axis=1, keepdims=True)
                dcum += jnp.where(rowq == q - 1, dlast, 0.0)
                dxs_ref[:, 64 * h:64 * h + 64] = dxd * dtc
                onehot = (lane == h).astype(F32)
                dcum_all += dcum * onehot
                ddt_all += jnp.sum(dxd * x, axis=1, keepdims=True) * onehot
                ds_ref[h] = dsp
            dgb = dgm.astype(BF16)
            dC_ref[:, 128 * g:128 * g + 128] = dcg + _dot(dgb, bb)
            dB_ref[:, 128 * g:128 * g + 128] = dbg + _dot_tn(dgb, cb)
        dla = _dot_hi(tri, dcum_all, (((0,), (0,)), ((), ())))
        ddt_ref[...] = ddt_all + dla * a
        dalog_ref[...] += jnp.sum(dla * dtv, axis=0, keepdims=True)

        @pl.when(i == nc - 1)
        def _():
            dalog_ref[...] = dalog_ref[...] * a

    return _pc(
        body, name="ssd_bwd", grid=(nc,),
        in_specs=[rev(CONV_DIM), rev(DT_PAD), _const_spec((1, DT_PAD)),
                  pl.BlockSpec((1, SSD_HEADS, SSD_HEADDIM, SSD_STATE), lambda i: (nc - 1 - i, 0, 0, 0)),
                  rev(SSD_INNER)],
        out_specs=[rev(SSD_INNER), rev(512), rev(512), rev(DT_PAD), _const_spec((1, DT_PAD))],
        out_shape=[jax.ShapeDtypeStruct((T, SSD_INNER), F32), jax.ShapeDtypeStruct((T, 512), F32),
                   jax.ShapeDtypeStruct((T, 512), F32), jax.ShapeDtypeStruct((T, DT_PAD), F32),
                   jax.ShapeDtypeStruct((1, DT_PAD), F32)],
        scratch_shapes=[pltpu.VMEM((SSD_HEADS, SSD_HEADDIM, SSD_STATE), F32)],
        compiler_params=_cparams(("arbitrary",)),
    )(xbc_act, dt, alog, sprev, dy)


def _s5_disc_vals(a_re, a_im, log_dt, b_re, b_im):
    dt = jnp.exp(log_dt)
    mag = jnp.exp(a_re * dt)
    ab_re = mag * jnp.cos(a_im * dt)
    ab_im = mag * jnp.sin(a_im * dt)
    den = a_re * a_re + a_im * a_im
    nr = ab_re - 1.0
    ni = ab_im
    coef_re = (nr * a_re + ni * a_im) / den
    coef_im = (ni * a_re - nr * a_im) / den
    bb_re = coef_re * b_re - coef_im * b_im
    bb_im = coef_re * b_im + coef_im * b_re
    return ab_re, ab_im, bb_re, bb_im


def _s5_disc(a_re, a_im, log_dt, b_re, b_im):
    def body(ar, ai, ld, br, bi, o1, o2, o3, o4):
        o1[...], o2[...], o3[...], o4[...] = _s5_disc_vals(ar[...], ai[...], ld[...], br[...], bi[...])

    return _pc(
        body, name="s5_disc",
        out_shape=[jax.ShapeDtypeStruct((S5_STATES, 1), F32), jax.ShapeDtypeStruct((S5_STATES, 1), F32),
                   jax.ShapeDtypeStruct((S5_STATES, 16), F32), jax.ShapeDtypeStruct((S5_STATES, 16), F32)],
    )(a_re, a_im, log_dt, b_re, b_im)


def _s5_disc_bwd(a_re, a_im, log_dt, b_re, b_im, d_ab_re, d_ab_im, d_bb_re, d_bb_im):
    def body(ar, ai, ld, br, bi, g1, g2, g3, g4, o1, o2, o3, o4, o5):
        _, vjp = jax.vjp(_s5_disc_vals, ar[...], ai[...], ld[...], br[...], bi[...])
        d1, d2, d3, d4, d5 = vjp((g1[...], g2[...], g3[...], g4[...]))
        o1[...] = d1
        o2[...] = d2
        grp = lax.broadcasted_iota(jnp.int32, (32, S5_STATES), 0)
        st = lax.broadcasted_iota(jnp.int32, (32, S5_STATES), 1)
        sel = (st // 64 == grp).astype(F32)
        o3[...] = _dot_hi(sel, d3)
        o4[...] = d4
        o5[...] = d5

    return _pc(
        body, name="s5_disc_bwd",
        out_shape=[jax.ShapeDtypeStruct((S5_STATES, 1), F32), jax.ShapeDtypeStruct((S5_STATES, 1), F32),
                   jax.ShapeDtypeStruct((32, 1), F32),
                   jax.ShapeDtypeStruct((S5_STATES, 16), F32), jax.ShapeDtypeStruct((S5_STATES, 16), F32)],
    )(a_re, a_im, log_dt, b_re, b_im, d_ab_re, d_ab_im, d_bb_re, d_bb_im)


def _cmul_add(xr, xi, pr, pi, yr, yi):
    return xr + pr * yr - pi * yi, xi + pr * yi + pi * yr


def _powers(ar, ai, n):
    out = [(ar, ai)]
    for _ in range(n - 1):
        pr, pi = out[-1]
        out.append((pr * pr - pi * pi, 2.0 * pr * pi))
    return out


def _scan_causal(br, bi, pws, row):
    q = br.shape[0]
    k = 1
    for pr, pi in pws:
        keep = row >= k
        sr = jnp.where(keep, pltpu.roll(br, k, 0), 0.0)
        si = jnp.where(keep, pltpu.roll(bi, k, 0), 0.0)
        br, bi = _cmul_add(br, bi, pr, pi, sr, si)
        k *= 2
    assert k == q
    return br, bi


def _scan_anticausal(br, bi, pws, row):
    q = br.shape[0]
    k = 1
    for pr, pi in pws:
        keep = row < q - k
        sr = jnp.where(keep, pltpu.roll(br, q - k, 0), 0.0)
        si = jnp.where(keep, pltpu.roll(bi, q - k, 0), 0.0)
        br, bi = _cmul_add(br, bi, pr, pi, sr, si)
        k *= 2
    assert k == q
    return br, bi


_S5_LEVELS = int(math.log2(S5_CHUNK))
_BW = S5_STATES // S5_BLOCKS
_BI = S5_WIDTH // S5_BLOCKS


def _s5_fwd(u5, wb4, wc4, ab, dvec):
    T = u5.shape[0]
    q = S5_CHUNK
    nc = T // q

    def body(u_ref, wb_ref, wc_ref, ab_ref, d_ref, y_ref, sp_ref, carry_ref, pw_re, pw_im):
        i = pl.program_id(0)
        row = lax.broadcasted_iota(jnp.int32, (q, 1), 0)

        @pl.when(i == 0)
        def _():
            carry_ref[...] = jnp.zeros_like(carry_ref)
            for j in range(S5_BLOCKS):
                ar = ab_ref[0:1, _BW * j:_BW * (j + 1)]
                ai = ab_ref[1:2, _BW * j:_BW * (j + 1)]
                er = jnp.where(row == 0, ar, 0.0) + jnp.zeros((q, _BW), F32)
                ei = jnp.where(row == 0, ai, 0.0) + jnp.zeros((q, _BW), F32)
                pr, pi = _scan_causal(er, ei, _powers(ar, ai, _S5_LEVELS), row)
                pw_re[:, _BW * j:_BW * (j + 1)] = pr
                pw_im[:, _BW * j:_BW * (j + 1)] = pi

        sp_ref[0] = carry_ref[...]
        for j in range(S5_BLOCKS):
            sl = slice(_BW * j, _BW * (j + 1))
            ul = slice(_BI * j, _BI * (j + 1))
            ar = ab_ref[0:1, sl]
            ai = ab_ref[1:2, sl]
            u = u_ref[:, ul]
            bu = _dot(u.astype(BF16), wb_ref[j])
            sr, si = _scan_causal(bu[:, :_BW], bu[:, _BW:], _powers(ar, ai, _S5_LEVELS), row)
            sr, si = _cmul_add(sr, si, pw_re[:, sl], pw_im[:, sl], carry_ref[0:1, sl], carry_ref[1:2, sl])
            carry_ref[0:1, sl] = sr[q - 1:q, :]
            carry_ref[1:2, sl] = si[q - 1:q, :]
            s = jnp.concatenate([sr, si], axis=1).astype(BF16)
            y_ref[:, ul] = _dot(s, wc_ref[j]) + d_ref[:, ul] * u

    return _pc(
        body, name="s5_fwd", grid=(nc,),
        in_specs=[_row_spec(q, S5_WIDTH), _const_spec((S5_BLOCKS, _BI, 2 * _BW)), _const_spec((S5_BLOCKS, 2 * _BW, _BI)),
                  _const_spec((8, S5_STATES)), _const_spec((1, S5_WIDTH))],
        out_specs=[_row_spec(q, S5_WIDTH), pl.BlockSpec((1, 8, S5_STATES), lambda i: (i, 0, 0))],
        out_shape=[jax.ShapeDtypeStruct((T, S5_WIDTH), F32), jax.ShapeDtypeStruct((nc, 8, S5_STATES), F32)],
        scratch_shapes=[pltpu.VMEM((8, S5_STATES), F32), pltpu.VMEM((q, S5_STATES), F32), pltpu.VMEM((q, S5_STATES), F32)],
        compiler_params=_cparams(("arbitrary",)),
    )(u5, wb4, wc4, ab, dvec)


def _s5_bwd(u5, dy5, wb4, wc4, ab, dvec, sprev):
    T = u5.shape[0]
    q = S5_CHUNK
    nc = T // q

    def rev(width):
        return pl.BlockSpec((q, width), lambda i: (nc - 1 - i, 0))

    def body(u_ref, dy_ref, wb_ref, wc_ref, ab_ref, d_ref, sp_ref, du_ref, dwb_ref, dwc_ref, dab_ref, dd_ref,
             carry_ref, pw_re, pw_im, rp_re, rp_im):
        i = pl.program_id(0)
        row = lax.broadcasted_iota(jnp.int32, (q, 1), 0)

        @pl.when(i == 0)
        def _():
            carry_ref[...] = jnp.zeros_like(carry_ref)
            dwb_ref[...] = jnp.zeros_like(dwb_ref)
            dwc_ref[...] = jnp.zeros_like(dwc_ref)
            dab_ref[...] = jnp.zeros_like(dab_ref)
            dd_ref[...] = jnp.zeros_like(dd_ref)
            for j in range(S5_BLOCKS):
                sl = slice(_BW * j, _BW * (j + 1))
                ar = ab_ref[0:1, sl]
                ai = ab_ref[1:2, sl]
                zero = jnp.zeros((q, _BW), F32)
                pr, pi = _scan_causal(jnp.where(row == 0, ar, 0.0) + zero, jnp.where(row == 0, ai, 0.0) + zero,
                                      _powers(ar, ai, _S5_LEVELS), row)
                pw_re[:, sl] = pr
                pw_im[:, sl] = pi
                pr, pi = _scan_anticausal(jnp.where(row == q - 1, ar, 0.0) + zero, jnp.where(row == q - 1, -ai, 0.0) + zero,
                                          _powers(ar, -ai, _S5_LEVELS), row)
                rp_re[:, sl] = pr
                rp_im[:, sl] = pi

        for j in range(S5_BLOCKS):
            sl = slice(_BW * j, _BW * (j + 1))
            ul = slice(_BI * j, _BI * (j + 1))
            ar = ab_ref[0:1, sl]
            ai = ab_ref[1:2, sl]
            u = u_ref[:, ul]
            ub = u.astype(BF16)
            dy = dy_ref[:, ul]
            dyb = dy.astype(BF16)
            bu = _dot(ub, wb_ref[j])
            sr, si = _scan_causal(bu[:, :_BW], bu[:, _BW:], _powers(ar, ai, _S5_LEVELS), row)
            s0r = sp_ref[0, 0:1, sl]
            s0i = sp_ref[0, 1:2, sl]
            sr, si = _cmul_add(sr, si, pw_re[:, sl], pw_im[:, sl], s0r, s0i)
            ds = _dot_nt(dyb, wc_ref[j])
            lr, li = _scan_anticausal(ds[:, :_BW], ds[:, _BW:], _powers(ar, -ai, _S5_LEVELS), row)
            lr, li = _cmul_add(lr, li, rp_re[:, sl], rp_im[:, sl], carry_ref[0:1, sl], carry_ref[1:2, sl])
            carry_ref[0:1, sl] = lr[0:1, :]
            carry_ref[1:2, sl] = li[0:1, :]
            lam = jnp.concatenate([lr, li], axis=1).astype(BF16)
            du_ref[:, ul] = _dot_nt(lam, wb_ref[j]) + d_ref[:, ul] * dy
            dwb_ref[j] += _dot_tn(ub, lam)
            dwc_ref[j] += _dot_tn(jnp.concatenate([sr, si], axis=1).astype(BF16), dyb)
            keep = row >= 1
            pr = jnp.where(keep, pltpu.roll(sr, 1, 0), s0r)
            pi = jnp.where(keep, pltpu.roll(si, 1, 0), s0i)
            dab_ref[0:1, sl] += jnp.sum(lr * pr + li * pi, axis=0, keepdims=True)
            dab_ref[1:2, sl] += jnp.sum(li * pr - lr * pi, axis=0, keepdims=True)
            dd_ref[:, ul] += jnp.sum(dy * u, axis=0, keepdims=True)

    return _pc(
        body, name="s5_bwd", grid=(nc,),
        in_specs=[rev(S5_WIDTH), rev(S5_WIDTH), _const_spec((S5_BLOCKS, _BI, 2 * _BW)), _const_spec((S5_BLOCKS, 2 * _BW, _BI)),
                  _const_spec((8, S5_STATES)), _const_spec((1, S5_WIDTH)),
                  pl.BlockSpec((1, 8, S5_STATES), lambda i: (nc - 1 - i, 0, 0))],
        out_specs=[rev(S5_WIDTH), _const_spec((S5_BLOCKS, _BI, 2 * _BW)), _const_spec((S5_BLOCKS, 2 * _BW, _BI)),
                   _const_spec((8, S5_STATES)), _const_spec((1, S5_WIDTH))],
        out_shape=[jax.ShapeDtypeStruct((T, S5_WIDTH), F32), jax.ShapeDtypeStruct((S5_BLOCKS, _BI, 2 * _BW), F32),
                   jax.ShapeDtypeStruct((S5_BLOCKS, 2 * _BW, _BI), F32), jax.ShapeDtypeStruct((8, S5_STATES), F32),
                   jax.ShapeDtypeStruct((1, S5_WIDTH), F32)],
        scratch_shapes=[pltpu.VMEM((8, S5_STATES), F32)] + [pltpu.VMEM((q, S5_STATES), F32)] * 4,
        compiler_params=_cparams(("arbitrary",)),
    )(u5, dy5, wb4, wc4, ab, dvec, sprev)


def _merge_vals(ys, xs, z, y5, gates, dvec, gssd, glu_w, glu_b, wbr):
    sz = _sigmoid(z)
    qv = ys + dvec * xs
    pre = qv * (z * sz)
    yn, rs = [], []
    for gi in range(SSD_GROUPS):
        p, r = _rms(pre[:, 256 * gi:256 * (gi + 1)])
        yn.append(p)
        rs.append(r)
    yn = jnp.concatenate(yn, axis=1)
    ya = yn * gssd
    gel = _gelu(y5)
    sg = _sigmoid(_dot(gel.astype(BF16), glu_w) + glu_b)
    yb = gel * sg
    pa = _dot(ya.astype(BF16), wbr[0:SSD_INNER, :])
    pb = _dot(yb.astype(BF16), wbr[SSD_INNER:, :])
    s0 = _sigmoid(gates[:, :D_MODEL])
    s1 = _sigmoid(gates[:, D_MODEL:])
    merged = s0 * pa + s1 * pb
    return dict(sz=sz, qv=qv, yn=yn, rs=rs, ya=ya, gel=gel, sg=sg, yb=yb, pa=pa, pb=pb, s0=s0, s1=s1, merged=merged)


def _merge_specs(tm):
    acts = [_row_spec(tm, 1024), _row_spec(tm, 1024, 0), _row_spec(tm, 1024), _row_spec(tm, 512), _row_spec(tm, 2048),
            _row_spec(tm, 1024)]
    params = [_const_spec((1, 1024)), _const_spec((1, 1024)), _const_spec((512, 512)), _const_spec((1, 512)),
              _hbm_spec(), _hbm_spec()]
    return acts, params


def _merge_fwd(ys, xbc_act, z, y5, gates, x, dvec, gssd, glu_w, glu_b, wbr, wout):
    T = x.shape[0]
    tm = TOKEN_TILE
    acts, params = _merge_specs(tm)

    def body(ys_ref, xs_ref, z_ref, y5_ref, gt_ref, x_ref, dv_ref, gs_ref, gw_ref, gb_ref, wbr_hbm, wout_hbm, x1_ref,
             wbr_ref, wout_ref):
        @pl.when(pl.program_id(0) == 0)
        def _():
            pltpu.sync_copy(wbr_hbm, wbr_ref)
            pltpu.sync_copy(wout_hbm, wout_ref)

        v = _merge_vals(ys_ref[...], xs_ref[...], z_ref[...], y5_ref[...], gt_ref[...], dv_ref[...], gs_ref[...],
                        gw_ref[...], gb_ref[...], wbr_ref)
        x1_ref[...] = x_ref[...] + _dot(v["merged"].astype(BF16), wout_ref[...])

    return _pc(
        body, name="merge_fwd", grid=(T // tm,),
        in_specs=acts + params, out_specs=_row_spec(tm, 1024),
        out_shape=jax.ShapeDtypeStruct((T, 1024), F32),
        scratch_shapes=[pltpu.VMEM((1536, 1024), BF16), pltpu.VMEM((1024, 1024), BF16)],
        compiler_params=_cparams(("arbitrary",)),
    )(ys, xbc_act, z, y5, gates, x, dvec, gssd, glu_w, glu_b, wbr, wout)


def _merge_bwd(ys, xbc_act, z, y5, gates, dx1, dvec, gssd, glu_w, glu_b, wbr, wout, head_sel):
    T = dx1.shape[0]
    tm = TOKEN_TILE
    acts, params = _merge_specs(tm)

    def body(ys_ref, xs_ref, z_ref, y5_ref, gt_ref, dx1_ref, dv_ref, gs_ref, gw_ref, gb_ref, wbr_hbm, wout_hbm, hs_ref,
             dys_ref, dxs_ref, dz_ref, dy5_ref, dgt_ref, mg_ref, ya_ref, yb_ref, dpa_ref, dpb_ref, gel_ref, dpre_ref,
             ddv_ref, dgs_ref, dgb_ref, wbr_ref, wout_ref, ddacc_ref):
        i = pl.program_id(0)

        @pl.when(i == 0)
        def _():
            pltpu.sync_copy(wbr_hbm, wbr_ref)
            pltpu.sync_copy(wout_hbm, wout_ref)
            ddacc_ref[...] = jnp.zeros_like(ddacc_ref)
            dgs_ref[...] = jnp.zeros_like(dgs_ref)
            dgb_ref[...] = jnp.zeros_like(dgb_ref)

        ys, xs, z, y5, gates = ys_ref[...], xs_ref[...], z_ref[...], y5_ref[...], gt_ref[...]
        dvv, gsv, gw = dv_ref[...], gs_ref[...], gw_ref[...]
        v = _merge_vals(ys, xs, z, y5, gates, dvv, gsv, gw, gb_ref[...], wbr_ref)
        dmg = _dot_nt(dx1_ref[...].astype(BF16), wout_ref[...])
        s0, s1, pa, pb = v["s0"], v["s1"], v["pa"], v["pb"]
        dgt_ref[:, :D_MODEL] = dmg * pa * s0 * (1.0 - s0)
        dgt_ref[:, D_MODEL:] = dmg * pb * s1 * (1.0 - s1)
        dpa = (dmg * s0).astype(BF16)
        dpb = (dmg * s1).astype(BF16)
        dya = _dot_nt(dpa, wbr_ref[0:SSD_INNER, :])
        dyb = _dot_nt(dpb, wbr_ref[SSD_INNER:, :])
        gel, sg = v["gel"], v["sg"]
        dpre = (dyb * gel * sg * (1.0 - sg))
        dgb_ref[...] += jnp.sum(dpre, axis=0, keepdims=True)
        dpre_b = dpre.astype(BF16)
        dgel = dyb * sg + _dot_nt(dpre_b, gw)
        dy5_ref[...] = dgel * _gelu_grad(y5)
        yn = v["yn"]
        dgs_ref[...] += jnp.sum(dya * yn, axis=0, keepdims=True)
        dyn = dya * gsv
        dpre_a = jnp.concatenate(
            [_rms_bwd(yn[:, 256 * gi:256 * (gi + 1)], v["rs"][gi], dyn[:, 256 * gi:256 * (gi + 1)])
             for gi in range(SSD_GROUPS)], axis=1)
        sz, qv = v["sz"], v["qv"]
        dq = dpre_a * (z * sz)
        dz_ref[...] = dpre_a * qv * (sz * (1.0 + z * (1.0 - sz)))
        dys_ref[...] = dq
        dxs_ref[...] = dq * dvv
        ddacc_ref[...] += jnp.sum(dq * xs, axis=0, keepdims=True)
        mg_ref[...] = v["merged"].astype(BF16)
        ya_ref[...] = v["ya"].astype(BF16)
        yb_ref[...] = v["yb"].astype(BF16)
        dpa_ref[...] = dpa
        dpb_ref[...] = dpb
        gel_ref[...] = gel.astype(BF16)
        dpre_ref[...] = dpre_b

        @pl.when(i == pl.num_programs(0) - 1)
        def _():
            ddv_ref[...] = _dot_hi(ddacc_ref[...], hs_ref[...])

    outs = [(1024, F32), (1024, F32), (1024, F32), (512, F32), (2048, F32),
            (1024, BF16), (1024, BF16), (512, BF16), (1024, BF16), (1024, BF16), (512, BF16), (512, BF16)]
    return _pc(
        body, name="merge_bwd", grid=(T // tm,),
        in_specs=acts + params + [_const_spec((1024, DT_PAD))],
        out_specs=[_row_spec(tm, w) for w, _ in outs] + [_const_spec((1, DT_PAD)), _const_spec((1, 1024)), _const_spec((1, 512))],
        out_shape=[jax.ShapeDtypeStruct((T, w), d) for w, d in outs] + [
            jax.ShapeDtypeStruct((1, DT_PAD), F32), jax.ShapeDtypeStruct((1, 1024), F32), jax.ShapeDtypeStruct((1, 512), F32)],
        scratch_shapes=[pltpu.VMEM((1536, 1024), BF16), pltpu.VMEM((1024, 1024), BF16), pltpu.VMEM((1, 1024), F32)],
        compiler_params=_cparams(("arbitrary",)),
    )(ys, xbc_act, z, y5, gates, dx1, dvec, gssd, glu_w, glu_b, wbr, wout, head_sel)


def _mlp_fwd(x1, g, w1, w2):
    T = x1.shape[0]
    tm = TOKEN_TILE

    def body(x_ref, g_ref, w1_hbm, w2_hbm, o_ref, w1_ref, w2_ref):
        @pl.when(pl.program_id(0) == 0)
        def _():
            pltpu.sync_copy(w1_hbm, w1_ref)
            pltpu.sync_copy(w2_hbm, w2_ref)

        xv = x_ref[...]
        xn, _ = _rms(xv)
        a1 = _dot((xn * g_ref[...]).astype(BF16), w1_ref[...])
        rl = jnp.maximum(a1, 0.0)
        o_ref[...] = xv + _dot((rl * rl).astype(BF16), w2_ref[...])

    return _pc(
        body, name="mlp_fwd", grid=(T // tm,),
        in_specs=[_row_spec(tm, 1024), _const_spec((1, 1024)), _hbm_spec(), _hbm_spec()],
        out_specs=_row_spec(tm, 1024), out_shape=jax.ShapeDtypeStruct((T, 1024), F32),
        scratch_shapes=[pltpu.VMEM((D_MODEL, D_FF), BF16), pltpu.VMEM((D_FF, D_MODEL), BF16)],
        compiler_params=_cparams(("arbitrary",)),
    )(x1, g, w1, w2)


def _mlp_bwd(x1, dx2, g, w1, w2):
    T = x1.shape[0]
    tm = TOKEN_TILE

    def body(x_ref, dx2_ref, g_ref, w1_hbm, w2_hbm, dx1_ref, h_ref, act_ref, da_ref, dg_ref, w1_ref, w2_ref):
        @pl.when(pl.program_id(0) == 0)
        def _():
            pltpu.sync_copy(w1_hbm, w1_ref)
            pltpu.sync_copy(w2_hbm, w2_ref)
            dg_ref[...] = jnp.zeros_like(dg_ref)

        xn, r = _rms(x_ref[...])
        gv = g_ref[...]
        h = (xn * gv).astype(BF16)
        h_ref[...] = h
        rl = jnp.maximum(_dot(h, w1_ref[...]), 0.0)
        act_ref[...] = (rl * rl).astype(BF16)
        dx2 = dx2_ref[...]
        da = (_dot_nt(dx2.astype(BF16), w2_ref[...]) * (2.0 * rl)).astype(BF16)
        da_ref[...] = da
        dh = _dot_nt(da, w1_ref[...])
        dg_ref[...] += jnp.sum(dh * xn, axis=0, keepdims=True)
        dx1_ref[...] = dx2 + _rms_bwd(xn, r, dh * gv)

    return _pc(
        body, name="mlp_bwd", grid=(T // tm,),
        in_specs=[_row_spec(tm, 1024), _row_spec(tm, 1024), _const_spec((1, 1024)), _hbm_spec(), _hbm_spec()],
        out_specs=[_row_spec(tm, 1024), _row_spec(tm, 1024), _row_spec(tm, D_FF), _row_spec(tm, D_FF), _const_spec((1, 1024))],
        out_shape=[jax.ShapeDtypeStruct((T, 1024), F32), jax.ShapeDtypeStruct((T, 1024), BF16),
                   jax.ShapeDtypeStruct((T, D_FF), BF16), jax.ShapeDtypeStruct((T, D_FF), BF16),
                   jax.ShapeDtypeStruct((1, 1024), F32)],
        scratch_shapes=[pltpu.VMEM((D_MODEL, D_FF), BF16), pltpu.VMEM((D_FF, D_MODEL), BF16)],
        compiler_params=_cparams(("arbitrary",)),
    )(x1, dx2, g, w1, w2)


def _loss_head(x2, target, g):
    T = x2.shape[0]
    tm = TOKEN_TILE

    def body(x_ref, t_ref, g_ref, dx_ref, loss_ref, dg_ref):
        @pl.when(pl.program_id(0) == 0)
        def _():
            loss_ref[...] = jnp.zeros_like(loss_ref)
            dg_ref[...] = jnp.zeros_like(dg_ref)

        xn, r = _rms(x_ref[...])
        gv = g_ref[...]
        err = xn * gv - t_ref[...]
        loss_ref[...] += jnp.sum(err * err, axis=0, keepdims=True) * (0.5 / D_MODEL)
        dy = err * (1.0 / D_MODEL)
        dg_ref[...] += jnp.sum(dy * xn, axis=0, keepdims=True)
        dx_ref[...] = _rms_bwd(xn, r, dy * gv)

    return _pc(
        body, name="loss_head", grid=(T // tm,),
        in_specs=[_row_spec(tm, 1024), _row_spec(tm, 1024), _const_spec((1, 1024))],
        out_specs=[_row_spec(tm, 1024), _const_spec((1, 1024)), _const_spec((1, 1024))],
        out_shape=[jax.ShapeDtypeStruct((T, 1024), F32), jax.ShapeDtypeStruct((1, 1024), F32),
                   jax.ShapeDtypeStruct((1, 1024), F32)],
        compiler_params=_cparams(("arbitrary",)),
    )(x2, target, g)


WGRAD_TOKENS = 512
WGRAD_OUT_ELEMS = 2 * 1024 * 1024


def _wgrad(a, b, name):
    T, K = a.shape
    N = b.shape[1]
    tt = min(T, WGRAD_TOKENS)
    nb = min(N, max(128, WGRAD_OUT_ELEMS // K))
    assert N % nb == 0 and T % tt == 0

    def body(a_ref, b_ref, o_ref):
        @pl.when(pl.program_id(1) == 0)
        def _():
            o_ref[...] = jnp.zeros_like(o_ref)

        o_ref[...] += _dot_tn(a_ref[...].astype(BF16), b_ref[...].astype(BF16))

    return _pc(
        body, name=name, grid=(N // nb, T // tt),
        in_specs=[pl.BlockSpec((tt, K), lambda n, t: (t, 0)), pl.BlockSpec((tt, nb), lambda n, t: (t, n))],
        out_specs=pl.BlockSpec((K, nb), lambda n, t: (0, n)),
        out_shape=jax.ShapeDtypeStruct((K, N), F32),
        compiler_params=_cparams(("parallel", "arbitrary")),
    )(a, b)


def _s5_block_weights(bb_re, bb_im, c_re, c_im):
    eye = jnp.eye(8, dtype=F32)
    bre = bb_re.reshape(S5_BLOCKS, 8, 64, 16)
    bim = bb_im.reshape(S5_BLOCKS, 8, 64, 16)
    wb_re = jnp.einsum('jgpk,gh->jhkgp', bre, eye).reshape(S5_BLOCKS, _BI, _BW)
    wb_im = jnp.einsum('jgpk,gh->jhkgp', bim, eye).reshape(S5_BLOCKS, _BI, _BW)
    wb4 = jnp.concatenate([wb_re, wb_im], axis=2).astype(BF16)
    cre = c_re.reshape(S5_BLOCKS, 8, 16, 64)
    cim = c_im.reshape(S5_BLOCKS, 8, 16, 64)
    wc_re = jnp.einsum('jgkp,gh->jgphk', cre, eye).reshape(S5_BLOCKS, _BW, _BI)
    wc_im = jnp.einsum('jgkp,gh->jgphk', -cim, eye).reshape(S5_BLOCKS, _BW, _BI)
    wc4 = jnp.concatenate([wc_re, wc_im], axis=1).astype(BF16)
    return wb4, wc4


def _s5_block_grads(dwb4, dwc4):
    eye = jnp.eye(8, dtype=F32)
    dwb = dwb4.reshape(S5_BLOCKS, 8, 16, 2, 8, 64)
    dbb = jnp.einsum('jhkrgp,gh->rjgpk', dwb, eye).reshape(2, 32, 64, 16)
    dwc = dwc4.reshape(S5_BLOCKS, 2, 8, 64, 8, 16)
    dc = jnp.einsum('jrgphk,gh->rjgkp', dwc, eye).reshape(2, 32, 16, 64)
    return dbb[0], dbb[1], dc[0], -dc[1]


def _permute_w_in(w_in):
    pad = jnp.zeros((D_MODEL, DT_PAD - 16), w_in.dtype)
    return jnp.concatenate([w_in[:, :OFF_DT], w_in[:, OFF_U:], w_in[:, OFF_DT:OFF_U], pad], axis=1)


def _row(v, width=None):
    v = v.reshape(1, -1)
    if width is not None and v.shape[1] < width:
        v = jnp.concatenate([v, jnp.zeros((1, width - v.shape[1]), v.dtype)], axis=1)
    return v


def _local_step(x, target, p):
    g_mix, g_mlp, g_fin = _row(p["norm_mix_g"]), _row(p["norm_mlp_g"]), _row(p["norm_final_g"])
    conv_b = _row(p["conv_b"])
    dt_bias = _row(p["dt_bias"], DT_PAD)
    alog = _row(p["a_log"], DT_PAD)
    dvec = _row(jnp.repeat(p["d_ssd"], SSD_HEADDIM))
    gssd = _row(p["ssd_norm_g"])
    s5d = _row(p["s5_d"])
    glu_b = _row(p["s5_glu_b"])
    head_sel = (jnp.arange(SSD_INNER)[:, None] // SSD_HEADDIM == jnp.arange(DT_PAD)[None, :]).astype(F32)

    a_re = p["s5_a_re"].reshape(S5_STATES, 1)
    a_im = p["s5_a_im"].reshape(S5_STATES, 1)
    log_dt = jnp.repeat(p["s5_log_dt"], 64).reshape(S5_STATES, 1)
    b_re = p["s5_b_re"].reshape(S5_STATES, 16)
    b_im = p["s5_b_im"].reshape(S5_STATES, 16)
    ab_re, ab_im, bb_re, bb_im = _s5_disc(a_re, a_im, log_dt, b_re, b_im)
    wb4, wc4 = _s5_block_weights(bb_re, bb_im, p["s5_c_re"], p["s5_c_im"])
    ab = jnp.concatenate([ab_re.reshape(1, S5_STATES), ab_im.reshape(1, S5_STATES), jnp.zeros((6, S5_STATES), F32)], axis=0)

    wp, wbr, wout, w1, w2, glu_w = p["w_in_perm"], p["w_branch"], p["w_out"], p["w_mlp_in"], p["w_mlp_out"], p["s5_glu_w"]

    z, xbc_raw, u5, gates, dt_raw = _inproj_fwd(x, g_mix, wp)
    xbc_act, dt = _conv_fwd(xbc_raw, dt_raw, p["conv_w"], conv_b, dt_bias)
    ys, ssd_states = _ssd_fwd(xbc_act, dt, alog)
    y5, s5_states = _s5_fwd(u5, wb4, wc4, ab, s5d)
    x1 = _merge_fwd(ys, xbc_act, z, y5, gates, x, dvec, gssd, glu_w, glu_b, wbr, wout)
    x2 = _mlp_fwd(x1, g_mlp, w1, w2)
    dx2, loss_lanes, d_gfin = _loss_head(x2, target, g_fin)

    dx1, h2, act, da1, d_gmlp = _mlp_bwd(x1, dx2, g_mlp, w1, w2)
    d_w_mlp_out = _wgrad(act, dx2, "wgrad_mlp_out")
    d_w_mlp_in = _wgrad(h2, da1, "wgrad_mlp_in")
    (dys, dxs_m, dz, dy5, dgates, mg, ya, yb, dpa, dpb, gel, dpre, d_dssd, d_gssd, d_glu_b) = _merge_bwd(
        ys, xbc_act, z, y5, gates, dx1, dvec, gssd, glu_w, glu_b, wbr, wout, head_sel)
    d_w_out = _wgrad(mg, dx1, "wgrad_out")
    d_w_branch = jnp.concatenate([_wgrad(ya, dpa, "wgrad_branch_a"), _wgrad(yb, dpb, "wgrad_branch_b")], axis=0)
    d_glu_w = _wgrad(gel, dpre, "wgrad_glu")
    du5, dwb4, dwc4, dab, d_s5d = _s5_bwd(u5, dy5, wb4, wc4, ab, s5d, s5_states)
    dbb_re, dbb_im, d_c_re, d_c_im = _s5_block_grads(dwb4, dwc4)
    d_a_re, d_a_im, d_log_dt, d_b_re, d_b_im = _s5_disc_bwd(
        a_re, a_im, log_dt, b_re, b_im, dab[0].reshape(S5_STATES, 1), dab[1].reshape(S5_STATES, 1),
        dbb_re.reshape(S5_STATES, 16), dbb_im.reshape(S5_STATES, 16))
    dxs_s, dB, dC, ddt, d_alog = _ssd_bwd(xbc_act, dt, alog, ssd_states, dys)
    dxbc_raw, ddt_raw, d_conv_w, d_conv_b, d_dt_bias = _conv_bwd(
        xbc_raw, dt_raw, dxs_m, dxs_s, dB, dC, ddt, p["conv_w"], conv_b, dt_bias)
    dx, h, d_gmix = _inproj_bwd(x, dx1, dz, dxbc_raw, du5, dgates, ddt_raw, g_mix, wp)
    d_w_in = jnp.concatenate([
        _wgrad(h, dz, "wgrad_in_z"), _wgrad(h, dxbc_raw, "wgrad_in_xbc"), _wgrad(h, ddt_raw, "wgrad_in_dt")[:, :16],
        _wgrad(h, du5, "wgrad_in_u5"), _wgrad(h, dgates, "wgrad_in_gates")], axis=1)

    grads = dict(
        norm_mix_g=d_gmix.reshape(-1), w_in=d_w_in, conv_w=d_conv_w[:CONV_K], conv_b=d_conv_b.reshape(-1),
        dt_bias=d_dt_bias[0, :16], a_log=d_alog[0, :16], d_ssd=d_dssd[0, :16], ssd_norm_g=d_gssd.reshape(-1),
        s5_a_re=d_a_re.reshape(32, 64), s5_a_im=d_a_im.reshape(32, 64), s5_log_dt=d_log_dt.reshape(32),
        s5_b_re=d_b_re.reshape(32, 64, 16), s5_b_im=d_b_im.reshape(32, 64, 16), s5_c_re=d_c_re, s5_c_im=d_c_im,
        s5_d=d_s5d.reshape(-1), s5_glu_w=d_glu_w, s5_glu_b=d_glu_b.reshape(-1), w_branch=d_w_branch, w_out=d_w_out,
        norm_mlp_g=d_gmlp.reshape(-1), w_mlp_in=d_w_mlp_in, w_mlp_out=d_w_mlp_out, norm_final_g=d_gfin.reshape(-1))
    return jnp.sum(loss_lanes), dx, grads


MESH = pl.DeviceIdType.MESH
N_CHIPS = 4


def _place():
    x, y, c = lax.axis_index("x"), lax.axis_index("y"), lax.axis_index("c")
    chips = [(1 - x, y), (x, 1 - y), (1 - x, 1 - y)]
    return x, y, c, chips


def _remote(src, dst, send_sems, recv_sems, k, to):
    return pltpu.make_async_remote_copy(src_ref=src, dst_ref=dst, send_sem=send_sems.at[k], recv_sem=recv_sems.at[k],
                                        device_id=to, device_id_type=MESH)


def _row_chunks(rows, k, align):
    step = rows // k
    assert rows % k == 0 and step % align == 0, (rows, k, align)
    return [(i * step, step) for i in range(k)]


ICI_CHUNKS = 4
D2D_CHUNKS = 24


def _allgather_chips(src, name, k_ici):
    R, C = src.shape
    H = R // 2
    pieces = _row_chunks(H, k_ici, 32 // src.dtype.itemsize)
    n = 3 * k_ici

    def body(src_ref, out_ref, send_sems, recv_sems, local_sem):
        x, y, c, chips = _place()
        own = 2 * x + y
        sib = (x, y, 1 - c)

        def part(s, hc, r0, nr):
            return out_ref.at[s, pl.ds(hc * H + r0, nr), :]

        mine = pltpu.make_async_copy(src_ref, out_ref.at[own], local_sem)
        mine.start()
        first = []
        for i, (r0, nr) in enumerate(pieces):
            for j, (cx, cy) in enumerate(chips):
                first.append(_remote(src_ref.at[pl.ds(c * H + r0, nr), :], part(own, c, r0, nr), send_sems, recv_sems,
                                     j * k_ici + i, (cx, cy, c)))
        for cp in first:
            cp.start()
        passed = []
        for i, (r0, nr) in enumerate(pieces):
            for j, (cx, cy) in enumerate(chips):
                got = part(2 * cx + cy, c, r0, nr)
                _remote(got, got, send_sems, recv_sems, j * k_ici + i, (cx, cy, c)).wait_recv()
                fw = _remote(got, got, send_sems, recv_sems, n + j * k_ici + i, sib)
                fw.start()
                passed.append(fw)
        for i, (r0, nr) in enumerate(pieces):
            for j, (cx, cy) in enumerate(chips):
                got = part(2 * cx + cy, 1 - c, r0, nr)
                _remote(got, got, send_sems, recv_sems, n + j * k_ici + i, sib).wait_recv()
        for cp in first + passed:
            cp.wait_send()
        mine.wait()

    return _pc(
        body, name=name, in_specs=[_hbm_spec()], out_specs=_hbm_spec(),
        out_shape=jax.ShapeDtypeStruct((N_CHIPS, R, C), src.dtype),
        scratch_shapes=[pltpu.SemaphoreType.DMA((2 * n,)), pltpu.SemaphoreType.DMA((2 * n,)), pltpu.SemaphoreType.DMA],
    )(src)


def _pair_exchange(gpack, small):
    _, R, C = gpack.shape
    H = R // 2
    pieces = _row_chunks(H, D2D_CHUNKS, 8)

    def body(g_ref, s_ref, mine_ref, sib_ref, sibs_ref, send_sems, recv_sems, local_sem):
        x, y, c, _ = _place()
        sib = (x, y, 1 - c)
        for s in range(N_CHIPS):
            for r0, nr in pieces:
                pltpu.make_async_copy(g_ref.at[s, pl.ds(c * H + r0, nr), :], mine_ref.at[s, pl.ds(r0, nr), :], local_sem).start()
                _remote(g_ref.at[s, pl.ds((1 - c) * H + r0, nr), :], sib_ref.at[s, pl.ds(r0, nr), :], send_sems, recv_sems, 0,
                        sib).start()
        sm = _remote(s_ref, sibs_ref, send_sems, recv_sems, 1, sib)
        sm.start()
        _remote(mine_ref, sib_ref, send_sems, recv_sems, 0, sib).wait()
        sm.wait()
        pltpu.make_async_copy(sib_ref, mine_ref, local_sem).wait()

    return _pc(
        body, name="pair_exchange", in_specs=[_hbm_spec(), _hbm_spec()], out_specs=[_hbm_spec()] * 3,
        out_shape=[jax.ShapeDtypeStruct((N_CHIPS, H, C), F32), jax.ShapeDtypeStruct((N_CHIPS, H, C), F32),
                   jax.ShapeDtypeStruct(small.shape, F32)],
        scratch_shapes=[pltpu.SemaphoreType.DMA((2,)), pltpu.SemaphoreType.DMA((2,)), pltpu.SemaphoreType.DMA],
    )(gpack, small)


PACK_BLOCK_ROWS = 4


def _pair_sum(mine, sib, small, sib_small):
    n, H, C = mine.shape
    rb = H // PACK_BLOCK_ROWS
    assert H % PACK_BLOCK_ROWS == 0 and rb % 16 == 0

    def body(a_ref, b_ref, s_ref, t_ref, pf_ref, pb_ref, ps_ref):
        p = a_ref[...] + b_ref[...]
        pf_ref[...] = p
        pb_ref[...] = p.astype(BF16)

        @pl.when((pl.program_id(0) == 0) & (pl.program_id(1) == 0))
        def _():
            ps_ref[...] = s_ref[...] + t_ref[...]

    blk = pl.BlockSpec((1, rb, C), lambda s, i: (s, i, 0))
    sm = pl.BlockSpec(small.shape, lambda s, i: (0, 0))
    return _pc(
        body, name="pair_sum", grid=(n, PACK_BLOCK_ROWS), in_specs=[blk, blk, sm, sm], out_specs=[blk, blk, sm],
        out_shape=[jax.ShapeDtypeStruct(mine.shape, F32), jax.ShapeDtypeStruct(mine.shape, BF16),
                   jax.ShapeDtypeStruct(small.shape, F32)],
        compiler_params=_cparams(("arbitrary", "arbitrary")),
    )(mine, sib, small, sib_small)


def _chip_exchange(pf, pb, psmall):
    _, H, C = pf.shape
    pieces = _row_chunks(H, ICI_CHUNKS, 16)

    def body(pf_ref, pb_ref, ps_ref, own_ref, got_ref, small4_ref, send_sems, recv_sems, local_sems):
        x, y, c, chips = _place()
        own = 2 * x + y
        for r0, nr in pieces:
            pltpu.make_async_copy(pf_ref.at[own, pl.ds(r0, nr), :], own_ref.at[pl.ds(r0, nr), :], local_sems.at[0]).start()
        keep_s = pltpu.make_async_copy(ps_ref, small4_ref.at[own], local_sems.at[1])
        keep_s.start()
        small = []
        for j, (cx, cy) in enumerate(chips):
            for r0, nr in pieces:
                _remote(pb_ref.at[2 * cx + cy, pl.ds(r0, nr), :], got_ref.at[j, pl.ds(r0, nr), :], send_sems, recv_sems, j,
                        (cx, cy, c)).start()
            small.append(_remote(ps_ref, small4_ref.at[own], send_sems, recv_sems, 3 + j, (cx, cy, c)))
            small[-1].start()
        for j, (cx, cy) in enumerate(chips):
            _remote(pb_ref.at[own], got_ref.at[j], send_sems, recv_sems, j, (cx, cy, c)).wait()
            _remote(ps_ref, small4_ref.at[2 * cx + cy], send_sems, recv_sems, 3 + j, (cx, cy, c)).wait_recv()
        for cp in small:
            cp.wait_send()
        pltpu.make_async_copy(pf_ref.at[own], own_ref, local_sems.at[0]).wait()
        keep_s.wait()

    return _pc(
        body, name="chip_exchange", in_specs=[_hbm_spec()] * 3, out_specs=[_hbm_spec()] * 3,
        out_shape=[jax.ShapeDtypeStruct((H, C), F32), jax.ShapeDtypeStruct((3, H, C), BF16),
                   jax.ShapeDtypeStruct((N_CHIPS,) + psmall.shape, F32)],
        scratch_shapes=[pltpu.SemaphoreType.DMA((6,)), pltpu.SemaphoreType.DMA((6,)), pltpu.SemaphoreType.DMA((2,))],
    )(pf, pb, psmall)


def _chip_sum(own, got, small4):
    H, C = own.shape
    rb = H // PACK_BLOCK_ROWS

    def body(o_ref, g_ref, s_ref, tot_ref, st_ref):
        tot_ref[...] = ((o_ref[...] + g_ref[0].astype(F32)) + g_ref[1].astype(F32)) + g_ref[2].astype(F32)

        @pl.when(pl.program_id(0) == 0)
        def _():
            st_ref[...] = ((s_ref[0] + s_ref[1]) + s_ref[2]) + s_ref[3]

    return _pc(
        body, name="chip_sum", grid=(PACK_BLOCK_ROWS,),
        in_specs=[pl.BlockSpec((rb, C), lambda i: (i, 0)), pl.BlockSpec((3, rb, C), lambda i: (0, i, 0)),
                  _const_spec(small4.shape)],
        out_specs=[pl.BlockSpec((rb, C), lambda i: (i, 0)), _const_spec(small4.shape[1:])],
        out_shape=[jax.ShapeDtypeStruct((H, C), F32), jax.ShapeDtypeStruct(small4.shape[1:], F32)],
        compiler_params=_cparams(("arbitrary",)),
    )(own, got, small4)


def _half_exchange(tot):
    H, C = tot.shape
    pieces = _row_chunks(H, D2D_CHUNKS, 8)

    def body(t_ref, out_ref, send_sems, recv_sems, local_sem):
        x, y, c, _ = _place()
        sib = (x, y, 1 - c)
        for r0, nr in pieces:
            dst = out_ref.at[pl.ds(c * H + r0, nr), :]
            pltpu.make_async_copy(t_ref.at[pl.ds(r0, nr), :], dst, local_sem).start()
            _remote(t_ref.at[pl.ds(r0, nr), :], dst, send_sems, recv_sems, 0, sib).start()
        other = out_ref.at[pl.ds((1 - c) * H, H), :]
        _remote(t_ref, other, send_sems, recv_sems, 0, sib).wait()
        pltpu.make_async_copy(t_ref, other, local_sem).wait()

    return _pc(
        body, name="half_exchange", in_specs=[_hbm_spec()], out_specs=_hbm_spec(),
        out_shape=jax.ShapeDtypeStruct((2 * H, C), F32),
        scratch_shapes=[pltpu.SemaphoreType.DMA((1,)), pltpu.SemaphoreType.DMA((1,)), pltpu.SemaphoreType.DMA],
    )(tot)


def _adamw(w, g, m, v, name):
    R, C = w.shape
    rb = 256 if R % 256 == 0 else (128 if R % 128 == 0 else R)

    def body(w_ref, g_ref, m_ref, v_ref, d_ref, nm_ref, nv_ref):
        gv = g_ref[...]
        m2 = ADAM_B1 * m_ref[...] + (1.0 - ADAM_B1) * gv
        v2 = ADAM_B2 * v_ref[...] + (1.0 - ADAM_B2) * (gv * gv)
        m_hat = m2 / (1.0 - ADAM_B1 ** ADAM_STEP)
        v_hat = v2 / (1.0 - ADAM_B2 ** ADAM_STEP)
        d_ref[...] = -ADAM_LR * (m_hat / (jnp.sqrt(v_hat) + ADAM_EPS) + ADAM_WD * w_ref[...])
        nm_ref[...] = m2
        nv_ref[...] = v2

    spec = pl.BlockSpec((rb, C), lambda i: (i, 0))
    return _pc(
        body, name=name, grid=(R // rb,), in_specs=[spec] * 4, out_specs=[spec] * 3,
        out_shape=[jax.ShapeDtypeStruct((R, C), F32)] * 3, compiler_params=_cparams(("parallel",)),
    )(w, g, m, v)


PACK_COLS = 1024
PACK_ROWS = 4224
BIG = (("w_in", (1024, 1412), 1), ("s5_glu_w", (128, 512), 0), ("w_branch", (384, 1024), 0), ("w_out", (256, 1024), 0),
       ("w_mlp_in", (1024, 1024), 1), ("w_mlp_out", (1024, 1024), 0), ("conv_w", (4, 512), 1))
SMALL = (("norm_mix_g", (1024,)), ("conv_b", (2048,)), ("dt_bias", (16,)), ("a_log", (16,)), ("d_ssd", (16,)),
         ("ssd_norm_g", (1024,)), ("s5_a_re", (32, 64)), ("s5_a_im", (32, 64)), ("s5_log_dt", (32,)),
         ("s5_b_re", (32, 64, 16)), ("s5_b_im", (32, 64, 16)), ("s5_c_re", (32, 16, 64)), ("s5_c_im", (32, 16, 64)),
         ("s5_d", (512,)), ("s5_glu_b", (512,)), ("norm_mlp_g", (1024,)), ("norm_final_g", (1024,)))
SMALL_ROWS = 144


PART_ALIGN = 16


def _part_rows(n):
    return -(-n // PART_ALIGN) * PART_ALIGN


def _pack_rows(parts, rows, dtype):
    flat, used = [], 0
    for a in parts:
        a = a.astype(dtype).reshape(-1, PACK_COLS)
        n = a.shape[0]
        if _part_rows(n) != n:
            a = jnp.pad(a, ((0, _part_rows(n) - n), (0, 0)))
        flat.append(a)
        used += a.shape[0]
    return jnp.concatenate(flat + [jnp.zeros((rows - used, PACK_COLS), dtype)], axis=0)


def _unpack_rows(pack, shapes):
    out, r = [], 0
    for shp in shapes:
        n = math.prod(shp) // PACK_COLS
        out.append(pack[r:r + n].reshape(shp))
        r += _part_rows(n)
    return out


def _pack_small(parts):
    flat = jnp.concatenate([a.astype(F32).reshape(-1) for a in parts])
    return jnp.concatenate([flat, jnp.zeros((SMALL_ROWS * PACK_COLS - flat.shape[0],), F32)]).reshape(SMALL_ROWS, PACK_COLS)


def _unpack_small(pack):
    flat, out, r = pack.reshape(-1), {}, 0
    for name, shp in SMALL:
        n = math.prod(shp)
        out[name] = flat[r:r + n].reshape(shp)
        r += n
    return out


def _join_shards(gathered, names_shapes):
    per_chip = [_unpack_rows(gathered[s], [shp for _, shp, _ in names_shapes]) for s in range(N_CHIPS)]
    return {name: jnp.concatenate([per_chip[s][i] for s in range(N_CHIPS)], axis=axis)
            for i, (name, _, axis) in enumerate(names_shapes)}


def _split_shards(full):
    packs = []
    for s in range(N_CHIPS):
        parts = []
        for name, shp, axis in BIG:
            n = shp[axis]
            parts.append(lax.slice_in_dim(full[name], s * n, (s + 1) * n, axis=axis))
        packs.append(_pack_rows(parts, PACK_ROWS, F32))
    return jnp.stack(packs)


def kernel(x, norm_mix_g, w_in, conv_w, conv_b, dt_bias, a_log, d_ssd, ssd_norm_g, s5_a_re, s5_a_im, s5_log_dt, s5_b_re, s5_b_im, s5_c_re, s5_c_im, s5_d, s5_glu_w, s5_glu_b, w_branch, w_out, norm_mlp_g, w_mlp_in, w_mlp_out, norm_final_g, loss_target, m_norm_mix_g, m_w_in, m_conv_w, m_conv_b, m_dt_bias, m_a_log, m_d_ssd, m_ssd_norm_g, m_s5_a_re, m_s5_a_im, m_s5_log_dt, m_s5_b_re, m_s5_b_im, m_s5_c_re, m_s5_c_im, m_s5_d, m_s5_glu_w, m_s5_glu_b, m_w_branch, m_w_out, m_norm_mlp_g, m_w_mlp_in, m_w_mlp_out, m_norm_final_g, v_norm_mix_g, v_w_in, v_conv_w, v_conv_b, v_dt_bias, v_a_log, v_d_ssd, v_ssd_norm_g, v_s5_a_re, v_s5_a_im, v_s5_log_dt, v_s5_b_re, v_s5_b_im, v_s5_c_re, v_s5_c_im, v_s5_d, v_s5_glu_w, v_s5_glu_b, v_w_branch, v_w_out, v_norm_mlp_g, v_w_mlp_in, v_w_mlp_out, v_norm_final_g):
    names = ("norm_mix_g", "w_in", "conv_w", "conv_b", "dt_bias", "a_log", "d_ssd", "ssd_norm_g", "s5_a_re", "s5_a_im",
             "s5_log_dt", "s5_b_re", "s5_b_im", "s5_c_re", "s5_c_im", "s5_d", "s5_glu_w", "s5_glu_b", "w_branch", "w_out",
             "norm_mlp_g", "w_mlp_in", "w_mlp_out", "norm_final_g")
    w = dict(zip(names, (norm_mix_g, w_in, conv_w, conv_b, dt_bias, a_log, d_ssd, ssd_norm_g, s5_a_re, s5_a_im, s5_log_dt,
                         s5_b_re, s5_b_im, s5_c_re, s5_c_im, s5_d, s5_glu_w, s5_glu_b, w_branch, w_out, norm_mlp_g,
                         w_mlp_in, w_mlp_out, norm_final_g)))
    m = dict(zip(names, (m_norm_mix_g, m_w_in, m_conv_w, m_conv_b, m_dt_bias, m_a_log, m_d_ssd, m_ssd_norm_g, m_s5_a_re,
                         m_s5_a_im, m_s5_log_dt, m_s5_b_re, m_s5_b_im, m_s5_c_re, m_s5_c_im, m_s5_d, m_s5_glu_w,
                         m_s5_glu_b, m_w_branch, m_w_out, m_norm_mlp_g, m_w_mlp_in, m_w_mlp_out, m_norm_final_g)))
    v = dict(zip(names, (v_norm_mix_g, v_w_in, v_conv_w, v_conv_b, v_dt_bias, v_a_log, v_d_ssd, v_ssd_norm_g, v_s5_a_re,
                         v_s5_a_im, v_s5_log_dt, v_s5_b_re, v_s5_b_im, v_s5_c_re, v_s5_c_im, v_s5_d, v_s5_glu_w,
                         v_s5_glu_b, v_w_branch, v_w_out, v_norm_mlp_g, v_w_mlp_in, v_w_mlp_out, v_norm_final_g)))

    mats = [(n, s, a) for n, s, a in BIG if n != "conv_w"]
    wpack = _pack_rows([w[n] for n, _, _ in mats], PACK_ROWS, BF16)
    full = _join_shards(_allgather_chips(wpack, "gather_weights", ICI_CHUNKS), mats)
    cpack = jnp.concatenate([conv_w, jnp.zeros((12, 512), F32)], axis=0)
    conv_full = _allgather_chips(cpack, "gather_conv", 1)[:, :CONV_K, :]
    p = {n: w[n] for n, _ in SMALL}
    p["conv_w"] = jnp.concatenate([conv_full[s] for s in range(N_CHIPS)], axis=1)
    p["w_in_perm"] = _permute_w_in(full["w_in"])
    for n in ("s5_glu_w", "w_branch", "w_out", "w_mlp_in", "w_mlp_out"):
        p[n] = full[n]

    loss_part, grad_x, g = _local_step(x[0], loss_target[0], p)
    loss = lax.psum(loss_part, ("x", "y", "c"))

    gpack = _split_shards(g)
    spack = _pack_small([g[n] for n, _ in SMALL])
    mine, sib, sib_small = _pair_exchange(gpack, spack)
    pf, pb, psmall = _pair_sum(mine, sib, spack, sib_small)
    own, got, small4 = _chip_exchange(pf, pb, psmall)
    tot, small_tot = _chip_sum(own, got, small4)
    gshard = _unpack_rows(_half_exchange(tot), [shp for _, shp, _ in BIG])
    grads = _unpack_small(small_tot)
    for (n, _, _), gs in zip(BIG, gshard):
        grads[n] = gs

    delta, new_m, new_v = {}, {}, {}
    for n, _, _ in BIG:
        delta[n], new_m[n], new_v[n] = _adamw(w[n], grads[n], m[n], v[n], "adamw_" + n)
    ds, ms, vs = _adamw(_pack_small([w[n] for n, _ in SMALL]), small_tot, _pack_small([m[n] for n, _ in SMALL]),
                        _pack_small([v[n] for n, _ in SMALL]), "adamw_small")
    delta.update(_unpack_small(ds))
    new_m.update(_unpack_small(ms))
    new_v.update(_unpack_small(vs))

    return (loss, grad_x[None], *[grads[n] for n in names], *[delta[n] for n in names],
            *[new_m[n] for n in names], *[new_v[n] for n in names])
```

```python
import functools
import math

import jax
import jax.numpy as jnp
from jax import lax
from jax.experimental import pallas as pl
from jax.experimental.pallas import tpu as pltpu

F32 = jnp.float32
BF16 = jnp.bfloat16

D_MODEL = 1024
SSD_INNER = 1024
SSD_HEADS = 16
SSD_HEADDIM = 64
SSD_GROUPS = 4
SSD_HPG = 4
SSD_STATE = 128
SSD_CHUNK = 128
CONV_K = 4
CONV_DIM = 2048
S5_WIDTH = 512
S5_STATES = 2048
S5_BLOCKS = 4
S5_CHUNK = 128
D_FF = 4096
EPS = 1e-6
P_Z, P_XBC, P_U5, P_G, P_DT, P_END = 0, 1024, 3072, 3584, 5632, 5760
DT_PAD = 128
OFF_DT, OFF_U = 3072, 3088
D_IN_PROJ = 5648

ADAM_LR, ADAM_B1, ADAM_B2, ADAM_EPS, ADAM_WD, ADAM_STEP = 0.001, 0.9, 0.999, 1e-08, 0.01, 10

TOKEN_TILE = 256
VMEM_LIMIT = 56 * 1024 * 1024
HALO = 8


def _pc(body, **kw):
    return pl.pallas_call(body, **kw)


def _cparams(sem=None):
    return pltpu.CompilerParams(dimension_semantics=sem, vmem_limit_bytes=VMEM_LIMIT)


def _dot(a, b):
    return jnp.dot(a, b, preferred_element_type=F32)


def _dot_nt(a, b):
    return lax.dot_general(a, b, (((1,), (1,)), ((), ())), preferred_element_type=F32)


def _dot_tn(a, b):
    return lax.dot_general(a, b, (((0,), (0,)), ((), ())), preferred_element_type=F32)


def _dot_hi(a, b, dims=(((1,), (0,)), ((), ()))):
    return lax.dot_general(a, b, dims, preferred_element_type=F32, precision=lax.Precision.HIGHEST)


def _sigmoid(x):
    return 1.0 / (1.0 + jnp.exp(-x))


def _softplus(x):
    return jnp.maximum(x, 0.0) + jnp.log(1.0 + jnp.exp(-jnp.abs(x)))


_GELU_C = math.sqrt(2.0 / math.pi)


def _gelu(x):
    return 0.5 * x * (1.0 + jnp.tanh(_GELU_C * (x + 0.044715 * x * x * x)))


def _gelu_grad(x):
    t = jnp.tanh(_GELU_C * (x + 0.044715 * x * x * x))
    return 0.5 * (1.0 + t) + 0.5 * x * (1.0 - t * t) * _GELU_C * (1.0 + 3.0 * 0.044715 * x * x)


def _rms(x):
    r = lax.rsqrt(jnp.mean(x * x, axis=-1, keepdims=True) + EPS)
    return x * r, r


def _rms_bwd(xn, r, dxn):
    return r * (dxn - xn * jnp.mean(dxn * xn, axis=-1, keepdims=True))


def _row_spec(tm, width, col=0):
    return pl.BlockSpec((tm, width), lambda i: (i, col))


def _const_spec(shape):
    nd = len(shape)
    return pl.BlockSpec(shape, lambda i: (0,) * nd)


def _hbm_spec():
    return pl.BlockSpec(memory_space=pl.ANY)


def _inproj_fwd(x, g, wp):
    T = x.shape[0]
    tm = TOKEN_TILE

    def body(x_ref, g_ref, w_hbm, z_ref, xbc_ref, u5_ref, gt_ref, dt_ref, w_ref):
        @pl.when(pl.program_id(0) == 0)
        def _():
            pltpu.sync_copy(w_hbm, w_ref)

        xn, _ = _rms(x_ref[...])
        h = (xn * g_ref[...]).astype(BF16)
        z_ref[...] = _dot(h, w_ref[:, P_Z:P_XBC])
        xbc_ref[...] = _dot(h, w_ref[:, P_XBC:P_U5])
        u5_ref[...] = _dot(h, w_ref[:, P_U5:P_G])
        gt_ref[...] = _dot(h, w_ref[:, P_G:P_DT])
        dt_ref[...] = _dot(h, w_ref[:, P_DT:P_END])

    widths = (1024, 2048, 512, 2048, DT_PAD)
    return _pc(
        body, name="inproj_fwd", grid=(T // tm,),
        in_specs=[_row_spec(tm, D_MODEL), _const_spec((1, D_MODEL)), _hbm_spec()],
        out_specs=[_row_spec(tm, w) for w in widths],
        out_shape=[jax.ShapeDtypeStruct((T, w), F32) for w in widths],
        scratch_shapes=[pltpu.VMEM((D_MODEL, P_END), BF16)],
        compiler_params=_cparams(("arbitrary",)),
    )(x, g, wp)


def _inproj_bwd(x, dx1, dz, dxbc, du5, dgt, ddt, g, wp):
    T = x.shape[0]
    tm = TOKEN_TILE

    def body(x_ref, dx1_ref, dz_ref, dxbc_ref, du5_ref, dgt_ref, ddt_ref, g_ref, w_hbm, dx_ref, h_ref, dg_ref, w_ref):
        @pl.when(pl.program_id(0) == 0)
        def _():
            pltpu.sync_copy(w_hbm, w_ref)
            dg_ref[...] = jnp.zeros_like(dg_ref)

        xn, r = _rms(x_ref[...])
        gv = g_ref[...]
        h_ref[...] = (xn * gv).astype(BF16)
        dh = _dot_nt(dz_ref[...].astype(BF16), w_ref[:, P_Z:P_XBC])
        dh += _dot_nt(dxbc_ref[...].astype(BF16), w_ref[:, P_XBC:P_U5])
        dh += _dot_nt(du5_ref[...].astype(BF16), w_ref[:, P_U5:P_G])
        dh += _dot_nt(dgt_ref[...].astype(BF16), w_ref[:, P_G:P_DT])
        dh += _dot_nt(ddt_ref[...].astype(BF16), w_ref[:, P_DT:P_END])
        dg_ref[...] += jnp.sum(dh * xn, axis=0, keepdims=True)
        dx_ref[...] = dx1_ref[...] + _rms_bwd(xn, r, dh * gv)

    return _pc(
        body, name="inproj_bwd", grid=(T // tm,),
        in_specs=[_row_spec(tm, 1024), _row_spec(tm, 1024), _row_spec(tm, 1024), _row_spec(tm, 2048),
                  _row_spec(tm, 512), _row_spec(tm, 2048), _row_spec(tm, DT_PAD), _const_spec((1, 1024)), _hbm_spec()],
        out_specs=[_row_spec(tm, 1024), _row_spec(tm, 1024), _const_spec((1, 1024))],
        out_shape=[jax.ShapeDtypeStruct((T, 1024), F32), jax.ShapeDtypeStruct((T, 1024), BF16),
                   jax.ShapeDtypeStruct((1, 1024), F32)],
        scratch_shapes=[pltpu.VMEM((D_MODEL, P_END), BF16)],
        compiler_params=_cparams(("arbitrary",)),
    )(x, dx1, dz, dxbc, du5, dgt, ddt, g, wp)


def _conv_fwd(xbc_raw, dt_raw, conv_w, conv_b, dt_bias):
    T = xbc_raw.shape[0]
    tm = TOKEN_TILE

    def body(u_ref, dtr_ref, w_ref, b_ref, db_ref, act_ref, dt_ref, ext_ref):
        @pl.when(pl.program_id(0) == 0)
        def _():
            ext_ref[0:HALO, :] = jnp.zeros((HALO, CONV_DIM), F32)

        ext_ref[HALO:, :] = u_ref[...]
        y = b_ref[...] + jnp.zeros((tm, CONV_DIM), F32)
        for k in range(CONV_K):
            y += w_ref[k:k + 1, :] * ext_ref[pl.ds(HALO - (CONV_K - 1) + k, tm), :]
        act_ref[...] = y * _sigmoid(y)
        ext_ref[0:HALO, :] = u_ref[tm - HALO:tm, :]
        dt_ref[...] = _softplus(dtr_ref[...] + db_ref[...])

    return _pc(
        body, name="conv_fwd", grid=(T // tm,),
        in_specs=[_row_spec(tm, CONV_DIM), _row_spec(tm, DT_PAD), _const_spec((CONV_K, CONV_DIM)),
                  _const_spec((1, CONV_DIM)), _const_spec((1, DT_PAD))],
        out_specs=[_row_spec(tm, CONV_DIM), _row_spec(tm, DT_PAD)],
        out_shape=[jax.ShapeDtypeStruct((T, CONV_DIM), F32), jax.ShapeDtypeStruct((T, DT_PAD), F32)],
        scratch_shapes=[pltpu.VMEM((tm + HALO, CONV_DIM), F32)],
        compiler_params=_cparams(("arbitrary",)),
    )(xbc_raw, dt_raw, conv_w, conv_b, dt_bias)


def _conv_bwd(xbc_raw, dt_raw, dxs_a, dxs_b, dB, dC, ddt, conv_w, conv_b, dt_bias):
    T = xbc_raw.shape[0]
    tm = TOKEN_TILE
    n = T // tm
    hb = tm // HALO

    def rev(width):
        return pl.BlockSpec((tm, width), lambda i: (n - 1 - i, 0))

    def body(u_ref, up_ref, dtr_ref, dxa_ref, dxb_ref, dB_ref, dC_ref, ddt_ref, w_ref, b_ref, db_ref,
             du_ref, ddtr_ref, dw_ref, dcb_ref, ddb_ref, ext_ref, dye_ref):
        i = pl.program_id(0)

        @pl.when(i == 0)
        def _():
            dye_ref[tm:, :] = jnp.zeros((HALO, CONV_DIM), F32)
            dw_ref[...] = jnp.zeros_like(dw_ref)
            dcb_ref[...] = jnp.zeros_like(dcb_ref)
            ddb_ref[...] = jnp.zeros_like(ddb_ref)

        first = (i == n - 1).astype(F32)
        ext_ref[0:HALO, :] = up_ref[...] * (1.0 - first)
        ext_ref[HALO:, :] = u_ref[...]
        y = b_ref[...] + jnp.zeros((tm, CONV_DIM), F32)
        for k in range(CONV_K):
            y += w_ref[k:k + 1, :] * ext_ref[pl.ds(HALO - (CONV_K - 1) + k, tm), :]
        s = _sigmoid(y)
        dsilu = s * (1.0 + y * (1.0 - s))
        dy = jnp.concatenate([dxa_ref[...] + dxb_ref[...], dB_ref[...], dC_ref[...]], axis=1) * dsilu
        dye_ref[0:tm, :] = dy
        dcb_ref[...] += jnp.sum(dy, axis=0, keepdims=True)
        du = jnp.zeros((tm, CONV_DIM), F32)
        for k in range(CONV_K):
            dw_ref[k:k + 1, :] += jnp.sum(dy * ext_ref[pl.ds(HALO - (CONV_K - 1) + k, tm), :], axis=0, keepdims=True)
            du += w_ref[k:k + 1, :] * dye_ref[pl.ds(CONV_K - 1 - k, tm), :]
        du_ref[...] = du
        dye_ref[tm:, :] = dy[0:HALO, :]
        sg = _sigmoid(dtr_ref[...] + db_ref[...])
        ddtr = ddt_ref[...] * sg
        ddtr_ref[...] = ddtr
        ddb_ref[...] += jnp.sum(ddtr, axis=0, keepdims=True)

    prev_spec = pl.BlockSpec((HALO, CONV_DIM), lambda i: (jnp.maximum((n - 1 - i) * hb - 1, 0), 0))
    return _pc(
        body, name="conv_bwd", grid=(n,),
        in_specs=[rev(CONV_DIM), prev_spec, rev(DT_PAD), rev(1024), rev(1024), rev(512), rev(512), rev(DT_PAD),
                  _const_spec((CONV_K, CONV_DIM)), _const_spec((1, CONV_DIM)), _const_spec((1, DT_PAD))],
        out_specs=[rev(CONV_DIM), rev(DT_PAD), _const_spec((HALO, CONV_DIM)), _const_spec((1, CONV_DIM)),
                   _const_spec((1, DT_PAD))],
        out_shape=[jax.ShapeDtypeStruct((T, CONV_DIM), F32), jax.ShapeDtypeStruct((T, DT_PAD), F32),
                   jax.ShapeDtypeStruct((HALO, CONV_DIM), F32), jax.ShapeDtypeStruct((1, CONV_DIM), F32),
                   jax.ShapeDtypeStruct((1, DT_PAD), F32)],
        scratch_shapes=[pltpu.VMEM((tm + HALO, CONV_DIM), F32), pltpu.VMEM((tm + HALO, CONV_DIM), F32)],
        compiler_params=_cparams(("arbitrary",)),
    )(xbc_raw, xbc_raw, dt_raw, dxs_a, dxs_b, dB, dC, ddt, conv_w, conv_b, dt_bias)


def _ssd_chunk_common(dt_ref, alog_ref):
    q = SSD_CHUNK
    a = -jnp.exp(alog_ref[...])
    dtv = dt_ref[...]
    la = dtv * a
    row = lax.broadcasted_iota(jnp.int32, (q, q), 0)
    col = lax.broadcasted_iota(jnp.int32, (q, q), 1)
    causal = col <= row
    tri = causal.astype(F32)
    cum = _dot_hi(tri, la)
    cum_t = _dot_hi(la, tri, (((0,), (1,)), ((), ())))
    return a, dtv, causal, tri, cum, cum_t


def _ssd_fwd(xbc_act, dt, alog):
    T = xbc_act.shape[0]
    q = SSD_CHUNK
    nc = T // q

    def body(xbc_ref, dt_ref, alog_ref, y_ref, sp_ref, s_ref):
        @pl.when(pl.program_id(0) == 0)
        def _():
            s_ref[...] = jnp.zeros_like(s_ref)

        a, dtv, causal, tri, cum, cum_t = _ssd_chunk_common(dt_ref, alog_ref)
        sp_ref[0] = s_ref[...]
        for g in range(SSD_GROUPS):
            bb = xbc_ref[:, 1024 + 128 * g:1152 + 128 * g].astype(BF16)
            cb = xbc_ref[:, 1536 + 128 * g:1664 + 128 * g].astype(BF16)
            gm = _dot_nt(cb, bb)
            for r in range(SSD_HPG):
                h = SSD_HPG * g + r
                x = xbc_ref[:, 64 * h:64 * h + 64]
                cc = cum[:, h:h + 1]
                cr = cum_t[h:h + 1, :]
                cl = cum[q - 1:q, h:h + 1]
                decay = jnp.where(causal, jnp.exp(jnp.minimum(cc - cr, 0.0)), 0.0)
                xd = x * dtv[:, h:h + 1]
                sp = s_ref[h]
                y = _dot((gm * decay).astype(BF16), xd.astype(BF16))
                y += _dot_nt(cb, sp.astype(BF16)) * jnp.exp(cc)
                y_ref[:, 64 * h:64 * h + 64] = y
                st = _dot_tn((xd * jnp.exp(cl - cc)).astype(BF16), bb)
                s_ref[h] = sp * jnp.exp(cl) + st

    return _pc(
        body, name="ssd_fwd", grid=(nc,),
        in_specs=[_row_spec(q, CONV_DIM), _row_spec(q, DT_PAD), _const_spec((1, DT_PAD))],
        out_specs=[_row_spec(q, SSD_INNER),
                   pl.BlockSpec((1, SSD_HEADS, SSD_HEADDIM, SSD_STATE), lambda i: (i, 0, 0, 0))],
        out_shape=[jax.ShapeDtypeStruct((T, SSD_INNER), F32),
                   jax.ShapeDtypeStruct((nc, SSD_HEADS, SSD_HEADDIM, SSD_STATE), F32)],
        scratch_shapes=[pltpu.VMEM((SSD_HEADS, SSD_HEADDIM, SSD_STATE), F32)],
        compiler_params=_cparams(("arbitrary",)),
    )(xbc_act, dt, alog)


def _ssd_bwd(xbc_act, dt, alog, sprev, dy):
    T = xbc_act.shape[0]
    q = SSD_CHUNK
    nc = T // q

    def rev(width):
        return pl.BlockSpec((q, width), lambda i: (nc - 1 - i, 0))

    def body(xbc_ref, dt_ref, alog_ref, sp_ref, dy_ref, dxs_ref, dB_ref, dC_ref, ddt_ref, dalog_ref, ds_ref):
        i = pl.program_id(0)

        @pl.when(i == 0)
        def _():
            ds_ref[...] = jnp.zeros_like(ds_ref)
            dalog_ref[...] = jnp.zeros_like(dalog_ref)

        a, dtv, causal, tri, cum, cum_t = _ssd_chunk_common(dt_ref, alog_ref)
        lane = lax.broadcasted_iota(jnp.int32, (1, DT_PAD), 1)
        rowq = lax.broadcasted_iota(jnp.int32, (q, 1), 0)
        dcum_all = jnp.zeros((q, DT_PAD), F32)
        ddt_all = jnp.zeros((q, DT_PAD), F32)
        for g in range(SSD_GROUPS):
            bb = xbc_ref[:, 1024 + 128 * g:1152 + 128 * g].astype(BF16)
            cb = xbc_ref[:, 1536 + 128 * g:1664 + 128 * g].astype(BF16)
            gm = _dot_nt(cb, bb)
            dgm = jnp.zeros((q, q), F32)
            dbg = jnp.zeros((q, SSD_STATE), F32)
            dcg = jnp.zeros((q, SSD_STATE), F32)
            for r in range(SSD_HPG):
                h = SSD_HPG * g + r
                x = xbc_ref[:, 64 * h:64 * h + 64]
                dyh = dy_ref[:, 64 * h:64 * h + 64]
                dtc = dtv[:, h:h + 1]
                cc = cum[:, h:h + 1]
                cr = cum_t[h:h + 1, :]
                cl = cum[q - 1:q, h:h + 1]
                decay = jnp.where(causal, jnp.exp(jnp.minimum(cc - cr, 0.0)), 0.0)
                m = gm * decay
                xd = x * dtc
                ec = jnp.exp(cc)
                de = jnp.exp(cl - cc)
                cd = jnp.exp(cl)
                sp = sp_ref[0, h]
                dsn = ds_ref[h]
                spb = sp.astype(BF16)
                dsnb = dsn.astype(BF16)
                dyb = dyh.astype(BF16)
                dye = (dyh * ec).astype(BF16)
                cs = _dot_nt(cb, spb)
                dcum = jnp.sum(dyh * cs, axis=1, keepdims=True) * ec
                dcg += _dot(dye, spb)
                dsp = dsn * cd + _dot_tn(dye, cb)
                dlast = jnp.sum(dsn * sp, keepdims=True) * cd
                dbg += _dot((xd * de).astype(BF16), dsnb)
                w = _dot_nt(bb, dsnb)
                dxd = w * de
                tde = jnp.sum(w * xd, axis=1, keepdims=True) * de
                dlast += jnp.sum(tde, keepdims=True)
                dcum -= tde
                dm = _dot_nt(dyb, xd.astype(BF16))
                dxd += _dot_tn(m.astype(BF16), dyb)
                dgm += dm * decay
                e = dm * m
                dcum += jnp.sum(e, axis=1, keepdims=True) - jnp.sum(e.T, axis=1, keepdims=True)
                dcum += jnp.where(rowq == q - 1, dlast, 0.0)
                dxs_ref[:, 64 * h:64 * h + 64] = dxd * dtc
                onehot = (lane == h).astype(F32)
                dcum_all += dcum * onehot
                ddt_all += jnp.sum(dxd * x, axis=1, keepdims=True) * onehot
                ds_ref[h] = dsp
            dgb = dgm.astype(BF16)
            dC_ref[:, 128 * g:128 * g + 128] = dcg + _dot(dgb, bb)
            dB_ref[:, 128 * g:128 * g + 128] = dbg + _dot_tn(dgb, cb)
        dla = _dot_hi(tri, dcum_all, (((0,), (0,)), ((), ())))
        ddt_ref[...] = ddt_all + dla * a
        dalog_ref[...] += jnp.sum(dla * dtv, axis=0, keepdims=True)

        @pl.when(i == nc - 1)
        def _():
            dalog_ref[...] = dalog_ref[...] * a

    return _pc(
        body, name="ssd_bwd", grid=(nc,),
        in_specs=[rev(CONV_DIM), rev(DT_PAD), _const_spec((1, DT_PAD)),
                  pl.BlockSpec((1, SSD_HEADS, SSD_HEADDIM, SSD_STATE), lambda i: (nc - 1 - i, 0, 0, 0)),
                  rev(SSD_INNER)],
        out_specs=[rev(SSD_INNER), rev(512), rev(512), rev(DT_PAD), _const_spec((1, DT_PAD))],
        out_shape=[jax.ShapeDtypeStruct((T, SSD_INNER), F32), jax.ShapeDtypeStruct((T, 512), F32),
                   jax.ShapeDtypeStruct((T, 512), F32), jax.ShapeDtypeStruct((T, DT_PAD), F32),
                   jax.ShapeDtypeStruct((1, DT_PAD), F32)],
        scratch_shapes=[pltpu.VMEM((SSD_HEADS, SSD_HEADDIM, SSD_STATE), F32)],
        compiler_params=_cparams(("arbitrary",)),
    )(xbc_act, dt, alog, sprev, dy)


def _s5_disc_vals(a_re, a_im, log_dt, b_re, b_im):
    dt = jnp.exp(log_dt)
    mag = jnp.exp(a_re * dt)
    ab_re = mag * jnp.cos(a_im * dt)
    ab_im = mag * jnp.sin(a_im * dt)
    den = a_re * a_re + a_im * a_im
    nr = ab_re - 1.0
    ni = ab_im
    coef_re = (nr * a_re + ni * a_im) / den
    coef_im = (ni * a_re - nr * a_im) / den
    bb_re = coef_re * b_re - coef_im * b_im
    bb_im = coef_re * b_im + coef_im * b_re
    return ab_re, ab_im, bb_re, bb_im


def _s5_disc(a_re, a_im, log_dt, b_re, b_im):
    def body(ar, ai, ld, br, bi, o1, o2, o3, o4):
        o1[...], o2[...], o3[...], o4[...] = _s5_disc_vals(ar[...], ai[...], ld[...], br[...], bi[...])

    return _pc(
        body, name="s5_disc",
        out_shape=[jax.ShapeDtypeStruct((S5_STATES, 1), F32), jax.ShapeDtypeStruct((S5_STATES, 1), F32),
                   jax.ShapeDtypeStruct((S5_STATES, 16), F32), jax.ShapeDtypeStruct((S5_STATES, 16), F32)],
    )(a_re, a_im, log_dt, b_re, b_im)


def _s5_disc_bwd(a_re, a_im, log_dt, b_re, b_im, d_ab_re, d_ab_im, d_bb_re, d_bb_im):
    def body(ar, ai, ld, br, bi, g1, g2, g3, g4, o1, o2, o3, o4, o5):
        _, vjp = jax.vjp(_s5_disc_vals, ar[...], ai[...], ld[...], br[...], bi[...])
        d1, d2, d3, d4, d5 = vjp((g1[...], g2[...], g3[...], g4[...]))
        o1[...] = d1
        o2[...] = d2
        grp = lax.broadcasted_iota(jnp.int32, (32, S5_STATES), 0)
        st = lax.broadcasted_iota(jnp.int32, (32, S5_STATES), 1)
        sel = (st // 64 == grp).astype(F32)
        o3[...] = _dot_hi(sel, d3)
        o4[...] = d4
        o5[...] = d5

    return _pc(
        body, name="s5_disc_bwd",
        out_shape=[jax.ShapeDtypeStruct((S5_STATES, 1), F32), jax.ShapeDtypeStruct((S5_STATES, 1), F32),
                   jax.ShapeDtypeStruct((32, 1), F32),
                   jax.ShapeDtypeStruct((S5_STATES, 16), F32), jax.ShapeDtypeStruct((S5_STATES, 16), F32)],
    )(a_re, a_im, log_dt, b_re, b_im, d_ab_re, d_ab_im, d_bb_re, d_bb_im)


def _cmul_add(xr, xi, pr, pi, yr, yi):
    return xr + pr * yr - pi * yi, xi + pr * yi + pi * yr


def _powers(ar, ai, n):
    out = [(ar, ai)]
    for _ in range(n - 1):
        pr, pi = out[-1]
        out.append((pr * pr - pi * pi, 2.0 * pr * pi))
    return out


def _scan_causal(br, bi, pws, row):
    q = br.shape[0]
    k = 1
    for pr, pi in pws:
        keep = row >= k
        sr = jnp.where(keep, pltpu.roll(br, k, 0), 0.0)
        si = jnp.where(keep, pltpu.roll(bi, k, 0), 0.0)
        br, bi = _cmul_add(br, bi, pr, pi, sr, si)
        k *= 2
    assert k == q
    return br, bi


def _scan_anticausal(br, bi, pws, row):
    q = br.shape[0]
    k = 1
    for pr, pi in pws:
        keep = row < q - k
        sr = jnp.where(keep, pltpu.roll(br, q - k, 0), 0.0)
        si = jnp.where(keep, pltpu.roll(bi, q - k, 0), 0.0)
        br, bi = _cmul_add(br, bi, pr, pi, sr, si)
        k *= 2
    assert k == q
    return br, bi


_S5_LEVELS = int(math.log2(S5_CHUNK))
_BW = S5_STATES // S5_BLOCKS
_BI = S5_WIDTH // S5_BLOCKS


def _s5_fwd(u5, wb4, wc4, ab, dvec):
    T = u5.shape[0]
    q = S5_CHUNK
    nc = T // q

    def body(u_ref, wb_ref, wc_ref, ab_ref, d_ref, y_ref, sp_ref, carry_ref, pw_re, pw_im):
        i = pl.program_id(0)
        row = lax.broadcasted_iota(jnp.int32, (q, 1), 0)

        @pl.when(i == 0)
        def _():
            carry_ref[...] = jnp.zeros_like(carry_ref)
            for j in range(S5_BLOCKS):
                ar = ab_ref[0:1, _BW * j:_BW * (j + 1)]
                ai = ab_ref[1:2, _BW * j:_BW * (j + 1)]
                er = jnp.where(row == 0, ar, 0.0) + jnp.zeros((q, _BW), F32)
                ei = jnp.where(row == 0, ai, 0.0) + jnp.zeros((q, _BW), F32)
                pr, pi = _scan_causal(er, ei, _powers(ar, ai, _S5_LEVELS), row)
                pw_re[:, _BW * j:_BW * (j + 1)] = pr
                pw_im[:, _BW * j:_BW * (j + 1)] = pi

        sp_ref[0] = carry_ref[...]
        for j in range(S5_BLOCKS):
            sl = slice(_BW * j, _BW * (j + 1))
            ul = slice(_BI * j, _BI * (j + 1))
            ar = ab_ref[0:1, sl]
            ai = ab_ref[1:2, sl]
            u = u_ref[:, ul]
            bu = _dot(u.astype(BF16), wb_ref[j])
            sr, si = _scan_causal(bu[:, :_BW], bu[:, _BW:], _powers(ar, ai, _S5_LEVELS), row)
            sr, si = _cmul_add(sr, si, pw_re[:, sl], pw_im[:, sl], carry_ref[0:1, sl], carry_ref[1:2, sl])
            carry_ref[0:1, sl] = sr[q - 1:q, :]
            carry_ref[1:2, sl] = si[q - 1:q, :]
            s = jnp.concatenate([sr, si], axis=1).astype(BF16)
            y_ref[:, ul] = _dot(s, wc_ref[j]) + d_ref[:, ul] * u

    return _pc(
        body, name="s5_fwd", grid=(nc,),
        in_specs=[_row_spec(q, S5_WIDTH), _const_spec((S5_BLOCKS, _BI, 2 * _BW)), _const_spec((S5_BLOCKS, 2 * _BW, _BI)),
                  _const_spec((8, S5_STATES)), _const_spec((1, S5_WIDTH))],
        out_specs=[_row_spec(q, S5_WIDTH), pl.BlockSpec((1, 8, S5_STATES), lambda i: (i, 0, 0))],
        out_shape=[jax.ShapeDtypeStruct((T, S5_WIDTH), F32), jax.ShapeDtypeStruct((nc, 8, S5_STATES), F32)],
        scratch_shapes=[pltpu.VMEM((8, S5_STATES), F32), pltpu.VMEM((q, S5_STATES), F32), pltpu.VMEM((q, S5_STATES), F32)],
        compiler_params=_cparams(("arbitrary",)),
    )(u5, wb4, wc4, ab, dvec)


def _s5_bwd(u5, dy5, wb4, wc4, ab, dvec, sprev):
    T = u5.shape[0]
    q = S5_CHUNK
    nc = T // q

    def rev(width):
        return pl.BlockSpec((q, width), lambda i: (nc - 1 - i, 0))

    def body(u_ref, dy_ref, wb_ref, wc_ref, ab_ref, d_ref, sp_ref, du_ref, dwb_ref, dwc_ref, dab_ref, dd_ref,
             carry_ref, pw_re, pw_im, rp_re, rp_im):
        i = pl.program_id(0)
        row = lax.broadcasted_iota(jnp.int32, (q, 1), 0)

        @pl.when(i == 0)
        def _():
            carry_ref[...] = jnp.zeros_like(carry_ref)
            dwb_ref[...] = jnp.zeros_like(dwb_ref)
            dwc_ref[...] = jnp.zeros_like(dwc_ref)
            dab_ref[...] = jnp.zeros_like(dab_ref)
            dd_ref[...] = jnp.zeros_like(dd_ref)
            for j in range(S5_BLOCKS):
                sl = slice(_BW * j, _BW * (j + 1))
                ar = ab_ref[0:1, sl]
                ai = ab_ref[1:2, sl]
                zero = jnp.zeros((q, _BW), F32)
                pr, pi = _scan_causal(jnp.where(row == 0, ar, 0.0) + zero, jnp.where(row == 0, ai, 0.0) + zero,
                                      _powers(ar, ai, _S5_LEVELS), row)
                pw_re[:, sl] = pr
                pw_im[:, sl] = pi
                pr, pi = _scan_anticausal(jnp.where(row == q - 1, ar, 0.0) + zero, jnp.where(row == q - 1, -ai, 0.0) + zero,
                                          _powers(ar, -ai, _S5_LEVELS), row)
                rp_re[:, sl] = pr
                rp_im[:, sl] = pi

        for j in range(S5_BLOCKS):
            sl = slice(_BW * j, _BW * (j + 1))
            ul = slice(_BI * j, _BI * (j + 1))
            ar = ab_ref[0:1, sl]
            ai = ab_ref[1:2, sl]
            u = u_ref[:, ul]
            ub = u.astype(BF16)
            dy = dy_ref[:, ul]
            dyb = dy.astype(BF16)
            bu = _dot(ub, wb_ref[j])
            sr, si = _scan_causal(bu[:, :_BW], bu[:, _BW:], _powers(ar, ai, _S5_LEVELS), row)
            s0r = sp_ref[0, 0:1, sl]
            s0i = sp_ref[0, 1:2, sl]
            sr, si = _cmul_add(sr, si, pw_re[:, sl], pw_im[:, sl], s0r, s0i)
            ds = _dot_nt(dyb, wc_ref[j])
            lr, li = _scan_anticausal(ds[:, :_BW], ds[:, _BW:], _powers(ar, -ai, _S5_LEVELS), row)
            lr, li = _cmul_add(lr, li, rp_re[:, sl], rp_im[:, sl], carry_ref[0:1, sl], carry_ref[1:2, sl])
            carry_ref[0:1, sl] = lr[0:1, :]
            carry_ref[1:2, sl] = li[0:1, :]
            lam = jnp.concatenate([lr, li], axis=1).astype(BF16)
            du_ref[:, ul] = _dot_nt(lam, wb_ref[j]) + d_ref[:, ul] * dy
            dwb_ref[j] += _dot_tn(ub, lam)
            dwc_ref[j] += _dot_tn(jnp.concatenate([sr, si], axis=1).astype(BF16), dyb)
            keep = row >= 1
            pr = jnp.where(keep, pltpu.roll(sr, 1, 0), s0r)
            pi = jnp.where(keep, pltpu.roll(si, 1, 0), s0i)
            dab_ref[0:1, sl] += jnp.sum(lr * pr + li * pi, axis=0, keepdims=True)
            dab_ref[1:2, sl] += jnp.sum(li * pr - lr * pi, axis=0, keepdims=True)
            dd_ref[:, ul] += jnp.sum(dy * u, axis=0, keepdims=True)

    return _pc(
        body, name="s5_bwd", grid=(nc,),
        in_specs=[rev(S5_WIDTH), rev(S5_WIDTH), _const_spec((S5_BLOCKS, _BI, 2 * _BW)), _const_spec((S5_BLOCKS, 2 * _BW, _BI)),
                  _const_spec((8, S5_STATES)), _const_spec((1, S5_WIDTH)),
                  pl.BlockSpec((1, 8, S5_STATES), lambda i: (nc - 1 - i, 0, 0))],
        out_specs=[rev(S5_WIDTH), _const_spec((S5_BLOCKS, _BI, 2 * _BW)), _const_spec((S5_BLOCKS, 2 * _BW, _BI)),
                   _const_spec((8, S5_STATES)), _const_spec((1, S5_WIDTH))],
        out_shape=[jax.ShapeDtypeStruct((T, S5_WIDTH), F32), jax.ShapeDtypeStruct((S5_BLOCKS, _BI, 2 * _BW), F32),
                   jax.ShapeDtypeStruct((S5_BLOCKS, 2 * _BW, _BI), F32), jax.ShapeDtypeStruct((8, S5_STATES), F32),
                   jax.ShapeDtypeStruct((1, S5_WIDTH), F32)],
        scratch_shapes=[pltpu.VMEM((8, S5_STATES), F32)] + [pltpu.VMEM((q, S5_STATES), F32)] * 4,
        compiler_params=_cparams(("arbitrary",)),
    )(u5, dy5, wb4, wc4, ab, dvec, sprev)


def _merge_vals(ys, xs, z, y5, gates, dvec, gssd, glu_w, glu_b, wbr):
    sz = _sigmoid(z)
    qv = ys + dvec * xs
    pre = qv * (z * sz)
    yn, rs = [], []
    for gi in range(SSD_GROUPS):
        p, r = _rms(pre[:, 256 * gi:256 * (gi + 1)])
        yn.append(p)
        rs.append(r)
    yn = jnp.concatenate(yn, axis=1)
    ya = yn * gssd
    gel = _gelu(y5)
    sg = _sigmoid(_dot(gel.astype(BF16), glu_w) + glu_b)
    yb = gel * sg
    pa = _dot(ya.astype(BF16), wbr[0:SSD_INNER, :])
    pb = _dot(yb.astype(BF16), wbr[SSD_INNER:, :])
    s0 = _sigmoid(gates[:, :D_MODEL])
    s1 = _sigmoid(gates[:, D_MODEL:])
    merged = s0 * pa + s1 * pb
    return dict(sz=sz, qv=qv, yn=yn, rs=rs, ya=ya, gel=gel, sg=sg, yb=yb, pa=pa, pb=pb, s0=s0, s1=s1, merged=merged)


def _merge_specs(tm):
    acts = [_row_spec(tm, 1024), _row_spec(tm, 1024, 0), _row_spec(tm, 1024), _row_spec(tm, 512), _row_spec(tm, 2048),
            _row_spec(tm, 1024)]
    params = [_const_spec((1, 1024)), _const_spec((1, 1024)), _const_spec((512, 512)), _const_spec((1, 512)),
              _hbm_spec(), _hbm_spec()]
    return acts, params


def _merge_fwd(ys, xbc_act, z, y5, gates, x, dvec, gssd, glu_w, glu_b, wbr, wout):
    T = x.shape[0]
    tm = TOKEN_TILE
    acts, params = _merge_specs(tm)

    def body(ys_ref, xs_ref, z_ref, y5_ref, gt_ref, x_ref, dv_ref, gs_ref, gw_ref, gb_ref, wbr_hbm, wout_hbm, x1_ref,
             wbr_ref, wout_ref):
        @pl.when(pl.program_id(0) == 0)
        def _():
            pltpu.sync_copy(wbr_hbm, wbr_ref)
            pltpu.sync_copy(wout_hbm, wout_ref)

        v = _merge_vals(ys_ref[...], xs_ref[...], z_ref[...], y5_ref[...], gt_ref[...], dv_ref[...], gs_ref[...],
                        gw_ref[...], gb_ref[...], wbr_ref)
        x1_ref[...] = x_ref[...] + _dot(v["merged"].astype(BF16), wout_ref[...])

    return _pc(
        body, name="merge_fwd", grid=(T // tm,),
        in_specs=acts + params, out_specs=_row_spec(tm, 1024),
        out_shape=jax.ShapeDtypeStruct((T, 1024), F32),
        scratch_shapes=[pltpu.VMEM((1536, 1024), BF16), pltpu.VMEM((1024, 1024), BF16)],
        compiler_params=_cparams(("arbitrary",)),
    )(ys, xbc_act, z, y5, gates, x, dvec, gssd, glu_w, glu_b, wbr, wout)


def _merge_bwd(ys, xbc_act, z, y5, gates, dx1, dvec, gssd, glu_w, glu_b, wbr, wout, head_sel):
    T = dx1.shape[0]
    tm = TOKEN_TILE
    acts, params = _merge_specs(tm)

    def body(ys_ref, xs_ref, z_ref, y5_ref, gt_ref, dx1_ref, dv_ref, gs_ref, gw_ref, gb_ref, wbr_hbm, wout_hbm, hs_ref,
             dys_ref, dxs_ref, dz_ref, dy5_ref, dgt_ref, mg_ref, ya_ref, yb_ref, dpa_ref, dpb_ref, gel_ref, dpre_ref,
             ddv_ref, dgs_ref, dgb_ref, wbr_ref, wout_ref, ddacc_ref):
        i = pl.program_id(0)

        @pl.when(i == 0)
        def _():
            pltpu.sync_copy(wbr_hbm, wbr_ref)
            pltpu.sync_copy(wout_hbm, wout_ref)
            ddacc_ref[...] = jnp.zeros_like(ddacc_ref)
            dgs_ref[...] = jnp.zeros_like(dgs_ref)
            dgb_ref[...] = jnp.zeros_like(dgb_ref)

        ys, xs, z, y5, gates = ys_ref[...], xs_ref[...], z_ref[...], y5_ref[...], gt_ref[...]
        dvv, gsv, gw = dv_ref[...], gs_ref[...], gw_ref[...]
        v = _merge_vals(ys, xs, z, y5, gates, dvv, gsv, gw, gb_ref[...], wbr_ref)
        dmg = _dot_nt(dx1_ref[...].astype(BF16), wout_ref[...])
        s0, s1, pa, pb = v["s0"], v["s1"], v["pa"], v["pb"]
        dgt_ref[:, :D_MODEL] = dmg * pa * s0 * (1.0 - s0)
        dgt_ref[:, D_MODEL:] = dmg * pb * s1 * (1.0 - s1)
        dpa = (dmg * s0).astype(BF16)
        dpb = (dmg * s1).astype(BF16)
        dya = _dot_nt(dpa, wbr_ref[0:SSD_INNER, :])
        dyb = _dot_nt(dpb, wbr_ref[SSD_INNER:, :])
        gel, sg = v["gel"], v["sg"]
        dpre = (dyb * gel * sg * (1.0 - sg))
        dgb_ref[...] += jnp.sum(dpre, axis=0, keepdims=True)
        dpre_b = dpre.astype(BF16)
        dgel = dyb * sg + _dot_nt(dpre_b, gw)
        dy5_ref[...] = dgel * _gelu_grad(y5)
        yn = v["yn"]
        dgs_ref[...] += jnp.sum(dya * yn, axis=0, keepdims=True)
        dyn = dya * gsv
        dpre_a = jnp.concatenate(
            [_rms_bwd(yn[:, 256 * gi:256 * (gi + 1)], v["rs"][gi], dyn[:, 256 * gi:256 * (gi + 1)])
             for gi in range(SSD_GROUPS)], axis=1)
        sz, qv = v["sz"], v["qv"]
        dq = dpre_a * (z * sz)
        dz_ref[...] = dpre_a * qv * (sz * (1.0 + z * (1.0 - sz)))
        dys_ref[...] = dq
        dxs_ref[...] = dq * dvv
        ddacc_ref[...] += jnp.sum(dq * xs, axis=0, keepdims=True)
        mg_ref[...] = v["merged"].astype(BF16)
        ya_ref[...] = v["ya"].astype(BF16)
        yb_ref[...] = v["yb"].astype(BF16)
        dpa_ref[...] = dpa
        dpb_ref[...] = dpb
        gel_ref[...] = gel.astype(BF16)
        dpre_ref[...] = dpre_b

        @pl.when(i == pl.num_programs(0) - 1)
        def _():
            ddv_ref[...] = _dot_hi(ddacc_ref[...], hs_ref[...])

    outs = [(1024, F32), (1024, F32), (1024, F32), (512, F32), (2048, F32),
            (1024, BF16), (1024, BF16), (512, BF16), (1024, BF16), (1024, BF16), (512, BF16), (512, BF16)]
    return _pc(
        body, name="merge_bwd", grid=(T // tm,),
        in_specs=acts + params + [_const_spec((1024, DT_PAD))],
        out_specs=[_row_spec(tm, w) for w, _ in outs] + [_const_spec((1, DT_PAD)), _const_spec((1, 1024)), _const_spec((1, 512))],
        out_shape=[jax.ShapeDtypeStruct((T, w), d) for w, d in outs] + [
            jax.ShapeDtypeStruct((1, DT_PAD), F32), jax.ShapeDtypeStruct((1, 1024), F32), jax.ShapeDtypeStruct((1, 512), F32)],
        scratch_shapes=[pltpu.VMEM((1536, 1024), BF16), pltpu.VMEM((1024, 1024), BF16), pltpu.VMEM((1, 1024), F32)],
        compiler_params=_cparams(("arbitrary",)),
    )(ys, xbc_act, z, y5, gates, dx1, dvec, gssd, glu_w, glu_b, wbr, wout, head_sel)


def _mlp_fwd(x1, g, w1, w2):
    T = x1.shape[0]
    tm = TOKEN_TILE

    def body(x_ref, g_ref, w1_hbm, w2_hbm, o_ref, w1_ref, w2_ref):
        @pl.when(pl.program_id(0) == 0)
        def _():
            pltpu.sync_copy(w1_hbm, w1_ref)
            pltpu.sync_copy(w2_hbm, w2_ref)

        xv = x_ref[...]
        xn, _ = _rms(xv)
        a1 = _dot((xn * g_ref[...]).astype(BF16), w1_ref[...])
        rl = jnp.maximum(a1, 0.0)
        o_ref[...] = xv + _dot((rl * rl).astype(BF16), w2_ref[...])

    return _pc(
        body, name="mlp_fwd", grid=(T // tm,),
        in_specs=[_row_spec(tm, 1024), _const_spec((1, 1024)), _hbm_spec(), _hbm_spec()],
        out_specs=_row_spec(tm, 1024), out_shape=jax.ShapeDtypeStruct((T, 1024), F32),
        scratch_shapes=[pltpu.VMEM((D_MODEL, D_FF), BF16), pltpu.VMEM((D_FF, D_MODEL), BF16)],
        compiler_params=_cparams(("arbitrary",)),
    )(x1, g, w1, w2)


def _mlp_bwd(x1, dx2, g, w1, w2):
    T = x1.shape[0]
    tm = TOKEN_TILE

    def body(x_ref, dx2_ref, g_ref, w1_hbm, w2_hbm, dx1_ref, h_ref, act_ref, da_ref, dg_ref, w1_ref, w2_ref):
        @pl.when(pl.program_id(0) == 0)
        def _():
            pltpu.sync_copy(w1_hbm, w1_ref)
            pltpu.sync_copy(w2_hbm, w2_ref)
            dg_ref[...] = jnp.zeros_like(dg_ref)

        xn, r = _rms(x_ref[...])
        gv = g_ref[...]
        h = (xn * gv).astype(BF16)
        h_ref[...] = h
        rl = jnp.maximum(_dot(h, w1_ref[...]), 0.0)
        act_ref[...] = (rl * rl).astype(BF16)
        dx2 = dx2_ref[...]
        da = (_dot_nt(dx2.astype(BF16), w2_ref[...]) * (2.0 * rl)).astype(BF16)
        da_ref[...] = da
        dh = _dot_nt(da, w1_ref[...])
        dg_ref[...] += jnp.sum(dh * xn, axis=0, keepdims=True)
        dx1_ref[...] = dx2 + _rms_bwd(xn, r, dh * gv)

    return _pc(
        body, name="mlp_bwd", grid=(T // tm,),
        in_specs=[_row_spec(tm, 1024), _row_spec(tm, 1024), _const_spec((1, 1024)), _hbm_spec(), _hbm_spec()],
        out_specs=[_row_spec(tm, 1024), _row_spec(tm, 1024), _row_spec(tm, D_FF), _row_spec(tm, D_FF), _const_spec((1, 1024))],
        out_shape=[jax.ShapeDtypeStruct((T, 1024), F32), jax.ShapeDtypeStruct((T, 1024), BF16),
                   jax.ShapeDtypeStruct((T, D_FF), BF16), jax.ShapeDtypeStruct((T, D_FF), BF16),
                   jax.ShapeDtypeStruct((1, 1024), F32)],
        scratch_shapes=[pltpu.VMEM((D_MODEL, D_FF), BF16), pltpu.VMEM((D_FF, D_MODEL), BF16)],
        compiler_params=_cparams(("arbitrary",)),
    )(x1, dx2, g, w1, w2)


def _loss_head(x2, target, g):
    T = x2.shape[0]
    tm = TOKEN_TILE

    def body(x_ref, t_ref, g_ref, dx_ref, loss_ref, dg_ref):
        @pl.when(pl.program_id(0) == 0)
        def _():
            loss_ref[...] = jnp.zeros_like(loss_ref)
            dg_ref[...] = jnp.zeros_like(dg_ref)

        xn, r = _rms(x_ref[...])
        gv = g_ref[...]
        err = xn * gv - t_ref[...]
        loss_ref[...] += jnp.sum(err * err, axis=0, keepdims=True) * (0.5 / D_MODEL)
        dy = err * (1.0 / D_MODEL)
        dg_ref[...] += jnp.sum(dy * xn, axis=0, keepdims=True)
        dx_ref[...] = _rms_bwd(xn, r, dy * gv)

    return _pc(
        body, name="loss_head", grid=(T // tm,),
        in_specs=[_row_spec(tm, 1024), _row_spec(tm, 1024), _const_spec((1, 1024))],
        out_specs=[_row_spec(tm, 1024), _const_spec((1, 1024)), _const_spec((1, 1024))],
        out_shape=[jax.ShapeDtypeStruct((T, 1024), F32), jax.ShapeDtypeStruct((1, 1024), F32),
                   jax.ShapeDtypeStruct((1, 1024), F32)],
        compiler_params=_cparams(("arbitrary",)),
    )(x2, target, g)


WGRAD_TOKENS = 512
WGRAD_OUT_ELEMS = 2 * 1024 * 1024


def _wgrad(a, b, name):
    T, K = a.shape
    N = b.shape[1]
    tt = min(T, WGRAD_TOKENS)
    nb = min(N, max(128, WGRAD_OUT_ELEMS // K))
    assert N % nb == 0 and T % tt == 0

    def body(a_ref, b_ref, o_ref):
        @pl.when(pl.program_id(1) == 0)
        def _():
            o_ref[...] = jnp.zeros_like(o_ref)

        o_ref[...] += _dot_tn(a_ref[...].astype(BF16), b_ref[...].astype(BF16))

    return _pc(
        body, name=name, grid=(N // nb, T // tt),
        in_specs=[pl.BlockSpec((tt, K), lambda n, t: (t, 0)), pl.BlockSpec((tt, nb), lambda n, t: (t, n))],
        out_specs=pl.BlockSpec((K, nb), lambda n, t: (0, n)),
        out_shape=jax.ShapeDtypeStruct((K, N), F32),
        compiler_params=_cparams(("parallel", "arbitrary")),
    )(a, b)


def _s5_block_weights(bb_re, bb_im, c_re, c_im):
    eye = jnp.eye(8, dtype=F32)
    bre = bb_re.reshape(S5_BLOCKS, 8, 64, 16)
    bim = bb_im.reshape(S5_BLOCKS, 8, 64, 16)
    wb_re = jnp.einsum('jgpk,gh->jhkgp', bre, eye).reshape(S5_BLOCKS, _BI, _BW)
    wb_im = jnp.einsum('jgpk,gh->jhkgp', bim, eye).reshape(S5_BLOCKS, _BI, _BW)
    wb4 = jnp.concatenate([wb_re, wb_im], axis=2).astype(BF16)
    cre = c_re.reshape(S5_BLOCKS, 8, 16, 64)
    cim = c_im.reshape(S5_BLOCKS, 8, 16, 64)
    wc_re = jnp.einsum('jgkp,gh->jgphk', cre, eye).reshape(S5_BLOCKS, _BW, _BI)
    wc_im = jnp.einsum('jgkp,gh->jgphk', -cim, eye).reshape(S5_BLOCKS, _BW, _BI)
    wc4 = jnp.concatenate([wc_re, wc_im], axis=1).astype(BF16)
    return wb4, wc4


def _s5_block_grads(dwb4, dwc4):
    eye = jnp.eye(8, dtype=F32)
    dwb = dwb4.reshape(S5_BLOCKS, 8, 16, 2, 8, 64)
    dbb = jnp.einsum('jhkrgp,gh->rjgpk', dwb, eye).reshape(2, 32, 64, 16)
    dwc = dwc4.reshape(S5_BLOCKS, 2, 8, 64, 8, 16)
    dc = jnp.einsum('jrgphk,gh->rjgkp', dwc, eye).reshape(2, 32, 16, 64)
    return dbb[0], dbb[1], dc[0], -dc[1]


def _permute_w_in(w_in):
    pad = jnp.zeros((D_MODEL, DT_PAD - 16), w_in.dtype)
    return jnp.concatenate([w_in[:, :OFF_DT], w_in[:, OFF_U:], w_in[:, OFF_DT:OFF_U], pad], axis=1)


def _row(v, width=None):
    v = v.reshape(1, -1)
    if width is not None and v.shape[1] < width:
        v = jnp.concatenate([v, jnp.zeros((1, width - v.shape[1]), v.dtype)], axis=1)
    return v


def _local_step(x, target, p):
    g_mix, g_mlp, g_fin = _row(p["norm_mix_g"]), _row(p["norm_mlp_g"]), _row(p["norm_final_g"])
    conv_b = _row(p["conv_b"])
    dt_bias = _row(p["dt_bias"], DT_PAD)
    alog = _row(p["a_log"], DT_PAD)
    dvec = _row(jnp.repeat(p["d_ssd"], SSD_HEADDIM))
    gssd = _row(p["ssd_norm_g"])
    s5d = _row(p["s5_d"])
    glu_b = _row(p["s5_glu_b"])
    head_sel = (jnp.arange(SSD_INNER)[:, None] // SSD_HEADDIM == jnp.arange(DT_PAD)[None, :]).astype(F32)

    a_re = p["s5_a_re"].reshape(S5_STATES, 1)
    a_im = p["s5_a_im"].reshape(S5_STATES, 1)
    log_dt = jnp.repeat(p["s5_log_dt"], 64).reshape(S5_STATES, 1)
    b_re = p["s5_b_re"].reshape(S5_STATES, 16)
    b_im = p["s5_b_im"].reshape(S5_STATES, 16)
    ab_re, ab_im, bb_re, bb_im = _s5_disc(a_re, a_im, log_dt, b_re, b_im)
    wb4, wc4 = _s5_block_weights(bb_re, bb_im, p["s5_c_re"], p["s5_c_im"])
    ab = jnp.concatenate([ab_re.reshape(1, S5_STATES), ab_im.reshape(1, S5_STATES), jnp.zeros((6, S5_STATES), F32)], axis=0)

    wp, wbr, wout, w1, w2, glu_w = p["w_in_perm"], p["w_branch"], p["w_out"], p["w_mlp_in"], p["w_mlp_out"], p["s5_glu_w"]

    z, xbc_raw, u5, gates, dt_raw = _inproj_fwd(x, g_mix, wp)
    xbc_act, dt = _conv_fwd(xbc_raw, dt_raw, p["conv_w"], conv_b, dt_bias)
    ys, ssd_states = _ssd_fwd(xbc_act, dt, alog)
    y5, s5_states = _s5_fwd(u5, wb4, wc4, ab, s5d)
    x1 = _merge_fwd(ys, xbc_act, z, y5, gates, x, dvec, gssd, glu_w, glu_b, wbr, wout)
    x2 = _mlp_fwd(x1, g_mlp, w1, w2)
    dx2, loss_lanes, d_gfin = _loss_head(x2, target, g_fin)

    dx1, h2, act, da1, d_gmlp = _mlp_bwd(x1, dx2, g_mlp, w1, w2)
    d_w_mlp_out = _wgrad(act, dx2, "wgrad_mlp_out")
    d_w_mlp_in = _wgrad(h2, da1, "wgrad_mlp_in")
    (dys, dxs_m, dz, dy5, dgates, mg, ya, yb, dpa, dpb, gel, dpre, d_dssd, d_gssd, d_glu_b) = _merge_bwd(
        ys, xbc_act, z, y5, gates, dx1, dvec, gssd, glu_w, glu_b, wbr, wout, head_sel)
    d_w_out = _wgrad(mg, dx1, "wgrad_out")
    d_w_branch = jnp.concatenate([_wgrad(ya, dpa, "wgrad_branch_a"), _wgrad(yb, dpb, "wgrad_branch_b")], axis=0)
    d_glu_w = _wgrad(gel, dpre, "wgrad_glu")
    du5, dwb4, dwc4, dab, d_s5d = _s5_bwd(u5, dy5, wb4, wc4, ab, s5d, s5_states)
    dbb_re, dbb_im, d_c_re, d_c_im = _s5_block_grads(dwb4, dwc4)
    d_a_re, d_a_im, d_log_dt, d_b_re, d_b_im = _s5_disc_bwd(
        a_re, a_im, log_dt, b_re, b_im, dab[0].reshape(S5_STATES, 1), dab[1].reshape(S5_STATES, 1),
        dbb_re.reshape(S5_STATES, 16), dbb_im.reshape(S5_STATES, 16))
    dxs_s, dB, dC, ddt, d_alog = _ssd_bwd(xbc_act, dt, alog, ssd_states, dys)
    dxbc_raw, ddt_raw, d_conv_w, d_conv_b, d_dt_bias = _conv_bwd(
        xbc_raw, dt_raw, dxs_m, dxs_s, dB, dC, ddt, p["conv_w"], conv_b, dt_bias)
    dx, h, d_gmix = _inproj_bwd(x, dx1, dz, dxbc_raw, du5, dgates, ddt_raw, g_mix, wp)
    d_w_in = jnp.concatenate([
        _wgrad(h, dz, "wgrad_in_z"), _wgrad(h, dxbc_raw, "wgrad_in_xbc"), _wgrad(h, ddt_raw, "wgrad_in_dt")[:, :16],
        _wgrad(h, du5, "wgrad_in_u5"), _wgrad(h, dgates, "wgrad_in_gates")], axis=1)

    grads = dict(
        norm_mix_g=d_gmix.reshape(-1), w_in=d_w_in, conv_w=d_conv_w[:CONV_K], conv_b=d_conv_b.reshape(-1),
        dt_bias=d_dt_bias[0, :16], a_log=d_alog[0, :16], d_ssd=d_dssd[0, :16], ssd_norm_g=d_gssd.reshape(-1),
        s5_a_re=d_a_re.reshape(32, 64), s5_a_im=d_a_im.reshape(32, 64), s5_log_dt=d_log_dt.reshape(32),
        s5_b_re=d_b_re.reshape(32, 64, 16), s5_b_im=d_b_im.reshape(32, 64, 16), s5_c_re=d_c_re, s5_c_im=d_c_im,
        s5_d=d_s5d.reshape(-1), s5_glu_w=d_glu_w, s5_glu_b=d_glu_b.reshape(-1), w_branch=d_w_branch, w_out=d_w_out,
        norm_mlp_g=d_gmlp.reshape(-1), w_mlp_in=d_w_mlp_in, w_mlp_out=d_w_mlp_out, norm_final_g=d_gfin.reshape(-1))
    return jnp.sum(loss_lanes), dx, grads


MESH = pl.DeviceIdType.MESH
N_CHIPS = 4


def _place():
    x, y, c = lax.axis_index("x"), lax.axis_index("y"), lax.axis_index("c")
    chips = [(1 - x, y), (x, 1 - y), (1 - x, 1 - y)]
    return x, y, c, chips


def _remote(src, dst, send_sems, recv_sems, k, to):
    return pltpu.make_async_remote_copy(src_ref=src, dst_ref=dst, send_sem=send_sems.at[k], recv_sem=recv_sems.at[k],
                                        device_id=to, device_id_type=MESH)


def _row_chunks(rows, k, align):
    step = rows // k
    assert rows % k == 0 and step % align == 0, (rows, k, align)
    return [(i * step, step) for i in range(k)]


ICI_CHUNKS = 4
D2D_CHUNKS = 24


def _allgather_chips(src, name, k_ici):
    R, C = src.shape
    H = R // 2
    pieces = _row_chunks(H, k_ici, 32 // src.dtype.itemsize)
    n = 3 * k_ici

    def body(src_ref, out_ref, send_sems, recv_sems):
        x, y, c, chips = _place()
        own = 2 * x + y
        sib = (x, y, 1 - c)

        def part(s, hc, r0, nr):
            return out_ref.at[s, pl.ds(hc * H + r0, nr), :]

        first = []
        for i, (r0, nr) in enumerate(pieces):
            for j, (cx, cy) in enumerate(chips):
                first.append(_remote(src_ref.at[pl.ds(c * H + r0, nr), :], part(own, c, r0, nr), send_sems, recv_sems,
                                     j * k_ici + i, (cx, cy, c)))
        for cp in first:
            cp.start()
        passed = []
        for i, (r0, nr) in enumerate(pieces):
            for j, (cx, cy) in enumerate(chips):
                got = part(2 * cx + cy, c, r0, nr)
                _remote(got, got, send_sems, recv_sems, j * k_ici + i, (cx, cy, c)).wait_recv()
                fw = _remote(got, got, send_sems, recv_sems, n + j * k_ici + i, sib)
                fw.start()
                passed.append(fw)
        for i, (r0, nr) in enumerate(pieces):
            for j, (cx, cy) in enumerate(chips):
                got = part(2 * cx + cy, 1 - c, r0, nr)
                _remote(got, got, send_sems, recv_sems, n + j * k_ici + i, sib).wait_recv()
        for cp in first + passed:
            cp.wait_send()

    return _pc(
        body, name=name, in_specs=[_hbm_spec()], out_specs=_hbm_spec(),
        out_shape=jax.ShapeDtypeStruct((N_CHIPS, R, C), src.dtype),
        scratch_shapes=[pltpu.SemaphoreType.DMA((2 * n,)), pltpu.SemaphoreType.DMA((2 * n,))],
    )(src)


def _pair_exchange(gpack, small):
    _, R, C = gpack.shape
    H = R // 2
    pieces = _row_chunks(H, D2D_CHUNKS, 8)

    def body(g_ref, s_ref, sib_ref, sibs_ref, send_sems, recv_sems):
        x, y, c, _ = _place()
        sib = (x, y, 1 - c)
        for s in range(N_CHIPS):
            for r0, nr in pieces:
                _remote(g_ref.at[s, pl.ds((1 - c) * H + r0, nr), :], sib_ref.at[s, pl.ds(r0, nr), :], send_sems, recv_sems, 0,
                        sib).start()
        sm = _remote(s_ref, sibs_ref, send_sems, recv_sems, 1, sib)
        sm.start()
        _remote(sib_ref, sib_ref, send_sems, recv_sems, 0, sib).wait()
        sm.wait()

    return _pc(
        body, name="pair_exchange", in_specs=[_hbm_spec(), _hbm_spec()], out_specs=[_hbm_spec()] * 2,
        out_shape=[jax.ShapeDtypeStruct((N_CHIPS, H, C), F32), jax.ShapeDtypeStruct(small.shape, F32)],
        scratch_shapes=[pltpu.SemaphoreType.DMA((2,)), pltpu.SemaphoreType.DMA((2,))],
    )(gpack, small)


PACK_BLOCK_ROWS = 4


def _pair_sum(mine, sib, small, sib_small):
    n, H, C = mine.shape
    rb = H // PACK_BLOCK_ROWS
    assert H % PACK_BLOCK_ROWS == 0 and rb % 16 == 0

    def body(a_ref, b_ref, s_ref, t_ref, pf_ref, pb_ref, ps_ref):
        p = a_ref[...] + b_ref[...]
        pf_ref[...] = p
        pb_ref[...] = p.astype(BF16)

        @pl.when((pl.program_id(0) == 0) & (pl.program_id(1) == 0))
        def _():
            ps_ref[...] = s_ref[...] + t_ref[...]

    blk = pl.BlockSpec((1, rb, C), lambda s, i: (s, i, 0))
    sm = pl.BlockSpec(small.shape, lambda s, i: (0, 0))
    return _pc(
        body, name="pair_sum", grid=(n, PACK_BLOCK_ROWS), in_specs=[blk, blk, sm, sm], out_specs=[blk, blk, sm],
        out_shape=[jax.ShapeDtypeStruct(mine.shape, F32), jax.ShapeDtypeStruct(mine.shape, BF16),
                   jax.ShapeDtypeStruct(small.shape, F32)],
        compiler_params=_cparams(("arbitrary", "arbitrary")),
    )(mine, sib, small, sib_small)


def _chip_exchange(pb, psmall):
    _, H, C = pb.shape
    pieces = _row_chunks(H, ICI_CHUNKS, 16)

    def body(pb_ref, ps_ref, got_ref, small4_ref, send_sems, recv_sems):
        x, y, c, chips = _place()
        own = 2 * x + y
        small = []
        for j, (cx, cy) in enumerate(chips):
            for r0, nr in pieces:
                _remote(pb_ref.at[2 * cx + cy, pl.ds(r0, nr), :], got_ref.at[j, pl.ds(r0, nr), :], send_sems, recv_sems, j,
                        (cx, cy, c)).start()
            small.append(_remote(ps_ref, small4_ref.at[own], send_sems, recv_sems, 3 + j, (cx, cy, c)))
            small[-1].start()
        for j, (cx, cy) in enumerate(chips):
            _remote(pb_ref.at[own], got_ref.at[j], send_sems, recv_sems, j, (cx, cy, c)).wait()
            _remote(ps_ref, small4_ref.at[2 * cx + cy], send_sems, recv_sems, 3 + j, (cx, cy, c)).wait_recv()
        for cp in small:
            cp.wait_send()

    return _pc(
        body, name="chip_exchange", in_specs=[_hbm_spec()] * 2, out_specs=[_hbm_spec()] * 2,
        out_shape=[jax.ShapeDtypeStruct((3, H, C), BF16), jax.ShapeDtypeStruct((N_CHIPS,) + psmall.shape, F32)],
        scratch_shapes=[pltpu.SemaphoreType.DMA((6,)), pltpu.SemaphoreType.DMA((6,))],
    )(pb, psmall)


def _chip_sum(own, got, small4):
    H, C = own.shape
    rb = H // PACK_BLOCK_ROWS

    def body(o_ref, g_ref, s_ref, tot_ref, st_ref):
        tot_ref[...] = ((o_ref[...] + g_ref[0].astype(F32)) + g_ref[1].astype(F32)) + g_ref[2].astype(F32)

        @pl.when(pl.program_id(0) == 0)
        def _():
            st_ref[...] = ((s_ref[0] + s_ref[1]) + s_ref[2]) + s_ref[3]

    return _pc(
        body, name="chip_sum", grid=(PACK_BLOCK_ROWS,),
        in_specs=[pl.BlockSpec((rb, C), lambda i: (i, 0)), pl.BlockSpec((3, rb, C), lambda i: (0, i, 0)),
                  _const_spec(small4.shape)],
        out_specs=[pl.BlockSpec((rb, C), lambda i: (i, 0)), _const_spec(small4.shape[1:])],
        out_shape=[jax.ShapeDtypeStruct((H, C), F32), jax.ShapeDtypeStruct(small4.shape[1:], F32)],
        compiler_params=_cparams(("arbitrary",)),
    )(own, got, small4)


def _half_exchange(tot):
    H, C = tot.shape
    pieces = _row_chunks(H, D2D_CHUNKS, 8)

    def body(t_ref, other_ref, send_sems, recv_sems):
        x, y, c, _ = _place()
        sib = (x, y, 1 - c)
        for r0, nr in pieces:
            _remote(t_ref.at[pl.ds(r0, nr), :], other_ref.at[pl.ds(r0, nr), :], send_sems, recv_sems, 0, sib).start()
        _remote(t_ref, other_ref, send_sems, recv_sems, 0, sib).wait()

    return _pc(
        body, name="half_exchange", in_specs=[_hbm_spec()], out_specs=_hbm_spec(),
        out_shape=jax.ShapeDtypeStruct((H, C), F32),
        scratch_shapes=[pltpu.SemaphoreType.DMA((1,)), pltpu.SemaphoreType.DMA((1,))],
    )(tot)


def _adamw(w, g, m, v, name):
    R, C = w.shape
    rb = 256 if R % 256 == 0 else (128 if R % 128 == 0 else R)

    def body(w_ref, g_ref, m_ref, v_ref, d_ref, nm_ref, nv_ref):
        gv = g_ref[...]
        m2 = ADAM_B1 * m_ref[...] + (1.0 - ADAM_B1) * gv
        v2 = ADAM_B2 * v_ref[...] + (1.0 - ADAM_B2) * (gv * gv)
        m_hat = m2 / (1.0 - ADAM_B1 ** ADAM_STEP)
        v_hat = v2 / (1.0 - ADAM_B2 ** ADAM_STEP)
        d_ref[...] = -ADAM_LR * (m_hat / (jnp.sqrt(v_hat) + ADAM_EPS) + ADAM_WD * w_ref[...])
        nm_ref[...] = m2
        nv_ref[...] = v2

    spec = pl.BlockSpec((rb, C), lambda i: (i, 0))
    return _pc(
        body, name=name, grid=(R // rb,), in_specs=[spec] * 4, out_specs=[spec] * 3,
        out_shape=[jax.ShapeDtypeStruct((R, C), F32)] * 3, compiler_params=_cparams(("parallel",)),
    )(w, g, m, v)


PACK_COLS = 1024
PACK_ROWS = 4224
BIG = (("w_in", (1024, 1412), 1), ("s5_glu_w", (128, 512), 0), ("w_branch", (384, 1024), 0), ("w_out", (256, 1024), 0),
       ("w_mlp_in", (1024, 1024), 1), ("w_mlp_out", (1024, 1024), 0), ("conv_w", (4, 512), 1))
SMALL = (("norm_mix_g", (1024,)), ("conv_b", (2048,)), ("dt_bias", (16,)), ("a_log", (16,)), ("d_ssd", (16,)),
         ("ssd_norm_g", (1024,)), ("s5_a_re", (32, 64)), ("s5_a_im", (32, 64)), ("s5_log_dt", (32,)),
         ("s5_b_re", (32, 64, 16)), ("s5_b_im", (32, 64, 16)), ("s5_c_re", (32, 16, 64)), ("s5_c_im", (32, 16, 64)),
         ("s5_d", (512,)), ("s5_glu_b", (512,)), ("norm_mlp_g", (1024,)), ("norm_final_g", (1024,)))
SMALL_ROWS = 144


PART_ALIGN = 16


def _part_rows(n):
    return -(-n // PART_ALIGN) * PART_ALIGN


def _pack_rows(parts, rows, dtype):
    flat, used = [], 0
    for a in parts:
        a = a.astype(dtype).reshape(-1, PACK_COLS)
        n = a.shape[0]
        if _part_rows(n) != n:
            a = jnp.pad(a, ((0, _part_rows(n) - n), (0, 0)))
        flat.append(a)
        used += a.shape[0]
    return jnp.concatenate(flat + [jnp.zeros((rows - used, PACK_COLS), dtype)], axis=0)


def _unpack_rows(pack, shapes):
    out, r = [], 0
    for shp in shapes:
        n = math.prod(shp) // PACK_COLS
        out.append(pack[r:r + n].reshape(shp))
        r += _part_rows(n)
    return out


def _pack_small(parts):
    flat = jnp.concatenate([a.astype(F32).reshape(-1) for a in parts])
    return jnp.concatenate([flat, jnp.zeros((SMALL_ROWS * PACK_COLS - flat.shape[0],), F32)]).reshape(SMALL_ROWS, PACK_COLS)


def _unpack_small(pack):
    flat, out, r = pack.reshape(-1), {}, 0
    for name, shp in SMALL:
        n = math.prod(shp)
        out[name] = flat[r:r + n].reshape(shp)
        r += n
    return out


def _join_shards(gathered, names_shapes):
    per_chip = [_unpack_rows(gathered[s], [shp for _, shp, _ in names_shapes]) for s in range(N_CHIPS)]
    return {name: jnp.concatenate([per_chip[s][i] for s in range(N_CHIPS)], axis=axis)
            for i, (name, _, axis) in enumerate(names_shapes)}


def _split_shards(full):
    packs = []
    for s in range(N_CHIPS):
        parts = []
        for name, shp, axis in BIG:
            n = shp[axis]
            parts.append(lax.slice_in_dim(full[name], s * n, (s + 1) * n, axis=axis))
        packs.append(_pack_rows(parts, PACK_ROWS, F32))
    return jnp.stack(packs)


def kernel(x, norm_mix_g, w_in, conv_w, conv_b, dt_bias, a_log, d_ssd, ssd_norm_g, s5_a_re, s5_a_im, s5_log_dt, s5_b_re, s5_b_im, s5_c_re, s5_c_im, s5_d, s5_glu_w, s5_glu_b, w_branch, w_out, norm_mlp_g, w_mlp_in, w_mlp_out, norm_final_g, loss_target, m_norm_mix_g, m_w_in, m_conv_w, m_conv_b, m_dt_bias, m_a_log, m_d_ssd, m_ssd_norm_g, m_s5_a_re, m_s5_a_im, m_s5_log_dt, m_s5_b_re, m_s5_b_im, m_s5_c_re, m_s5_c_im, m_s5_d, m_s5_glu_w, m_s5_glu_b, m_w_branch, m_w_out, m_norm_mlp_g, m_w_mlp_in, m_w_mlp_out, m_norm_final_g, v_norm_mix_g, v_w_in, v_conv_w, v_conv_b, v_dt_bias, v_a_log, v_d_ssd, v_ssd_norm_g, v_s5_a_re, v_s5_a_im, v_s5_log_dt, v_s5_b_re, v_s5_b_im, v_s5_c_re, v_s5_c_im, v_s5_d, v_s5_glu_w, v_s5_glu_b, v_w_branch, v_w_out, v_norm_mlp_g, v_w_mlp_in, v_w_mlp_out, v_norm_final_g):
    names = ("norm_mix_g", "w_in", "conv_w", "conv_b", "dt_bias", "a_log", "d_ssd", "ssd_norm_g", "s5_a_re", "s5_a_im",
             "s5_log_dt", "s5_b_re", "s5_b_im", "s5_c_re", "s5_c_im", "s5_d", "s5_glu_w", "s5_glu_b", "w_branch", "w_out",
             "norm_mlp_g", "w_mlp_in", "w_mlp_out", "norm_final_g")
    w = dict(zip(names, (norm_mix_g, w_in, conv_w, conv_b, dt_bias, a_log, d_ssd, ssd_norm_g, s5_a_re, s5_a_im, s5_log_dt,
                         s5_b_re, s5_b_im, s5_c_re, s5_c_im, s5_d, s5_glu_w, s5_glu_b, w_branch, w_out, norm_mlp_g,
                         w_mlp_in, w_mlp_out, norm_final_g)))
    m = dict(zip(names, (m_norm_mix_g, m_w_in, m_conv_w, m_conv_b, m_dt_bias, m_a_log, m_d_ssd, m_ssd_norm_g, m_s5_a_re,
                         m_s5_a_im, m_s5_log_dt, m_s5_b_re, m_s5_b_im, m_s5_c_re, m_s5_c_im, m_s5_d, m_s5_glu_w,
                         m_s5_glu_b, m_w_branch, m_w_out, m_norm_mlp_g, m_w_mlp_in, m_w_mlp_out, m_norm_final_g)))
    v = dict(zip(names, (v_norm_mix_g, v_w_in, v_conv_w, v_conv_b, v_dt_bias, v_a_log, v_d_ssd, v_ssd_norm_g, v_s5_a_re,
                         v_s5_a_im, v_s5_log_dt, v_s5_b_re, v_s5_b_im, v_s5_c_re, v_s5_c_im, v_s5_d, v_s5_glu_w,
                         v_s5_glu_b, v_w_branch, v_w_out, v_norm_mlp_g, v_w_mlp_in, v_w_mlp_out, v_norm_final_g)))

    mats = [(n, s, a) for n, s, a in BIG if n != "conv_w"]
    wpack = _pack_rows([w[n] for n, _, _ in mats], PACK_ROWS, BF16)
    cx, cy, cc = lax.axis_index("x"), lax.axis_index("y"), lax.axis_index("c")
    own = 2 * cx + cy
    slot = jnp.arange(N_CHIPS)[:, None, None] == own

    def with_own(gathered, mine):
        return jnp.where(slot, mine[None], gathered)

    full = _join_shards(with_own(_allgather_chips(wpack, "gather_weights", ICI_CHUNKS), wpack), mats)
    cpack = jnp.concatenate([conv_w, jnp.zeros((12, 512), F32)], axis=0)
    conv_full = with_own(_allgather_chips(cpack, "gather_conv", 1), cpack)[:, :CONV_K, :]
    p = {n: w[n] for n, _ in SMALL}
    p["conv_w"] = jnp.concatenate([conv_full[s] for s in range(N_CHIPS)], axis=1)
    p["w_in_perm"] = _permute_w_in(full["w_in"])
    for n in ("s5_glu_w", "w_branch", "w_out", "w_mlp_in", "w_mlp_out"):
        p[n] = full[n]

    loss_part, grad_x, g = _local_step(x[0], loss_target[0], p)
    loss = lax.psum(loss_part, ("x", "y", "c"))

    gpack = _split_shards(g)
    spack = _pack_small([g[n] for n, _ in SMALL])
    half_rows = PACK_ROWS // 2
    sib, sib_small = _pair_exchange(gpack, spack)
    mine = lax.dynamic_slice_in_dim(gpack, cc * half_rows, half_rows, axis=1)
    pf, pb, psmall = _pair_sum(mine, sib, spack, sib_small)
    got, small4 = _chip_exchange(pb, psmall)
    tot, small_tot = _chip_sum(lax.dynamic_index_in_dim(pf, own, 0, keepdims=False), got, with_own(small4, psmall))
    other = _half_exchange(tot)
    both = jnp.where(cc == 0, jnp.concatenate([tot, other], axis=0), jnp.concatenate([other, tot], axis=0))
    gshard = _unpack_rows(both, [shp for _, shp, _ in BIG])
    grads = _unpack_small(small_tot)
    for (n, _, _), gs in zip(BIG, gshard):
        grads[n] = gs

    delta, new_m, new_v = {}, {}, {}
    for n, _, _ in BIG:
        delta[n], new_m[n], new_v[n] = _adamw(w[n], grads[n], m[n], v[n], "adamw_" + n)
    ds, ms, vs = _adamw(_pack_small([w[n] for n, _ in SMALL]), small_tot, _pack_small([m[n] for n, _ in SMALL]),
                        _pack_small([v[n] for n, _ in SMALL]), "adamw_small")
    delta.update(_unpack_small(ds))
    new_m.update(_unpack_small(ms))
    new_v.update(_unpack_small(vs))

    return (loss, grad_x[None], *[grads[n] for n in names], *[delta[n] for n in names],
            *[new_m[n] for n in names], *[new_v[n] for n in names])
```

```python
import functools
import math

import jax
import jax.numpy as jnp
from jax import lax
from jax.experimental import pallas as pl
from jax.experimental.pallas import tpu as pltpu

F32 = jnp.float32
BF16 = jnp.bfloat16

D_MODEL = 1024
SSD_INNER = 1024
SSD_HEADS = 16
SSD_HEADDIM = 64
SSD_GROUPS = 4
SSD_HPG = 4
SSD_STATE = 128
SSD_CHUNK = 128
CONV_K = 4
CONV_DIM = 2048
S5_WIDTH = 512
S5_STATES = 2048
S5_BLOCKS = 4
S5_CHUNK = 128
D_FF = 4096
EPS = 1e-6
P_Z, P_XBC, P_U5, P_G, P_DT, P_END = 0, 1024, 3072, 3584, 5632, 5760
DT_PAD = 128
OFF_DT, OFF_U = 3072, 3088
D_IN_PROJ = 5648

ADAM_LR, ADAM_B1, ADAM_B2, ADAM_EPS, ADAM_WD, ADAM_STEP = 0.001, 0.9, 0.999, 1e-08, 0.01, 10

TOKEN_TILE = 256
VMEM_LIMIT = 56 * 1024 * 1024
HALO = 8


def _pc(body, **kw):
    return pl.pallas_call(body, **kw)


def _cparams(sem=None):
    return pltpu.CompilerParams(dimension_semantics=sem, vmem_limit_bytes=VMEM_LIMIT)


def _dot(a, b):
    return jnp.dot(a, b, preferred_element_type=F32)


def _dot_nt(a, b):
    return lax.dot_general(a, b, (((1,), (1,)), ((), ())), preferred_element_type=F32)


def _dot_tn(a, b):
    return lax.dot_general(a, b, (((0,), (0,)), ((), ())), preferred_element_type=F32)


def _dot_hi(a, b, dims=(((1,), (0,)), ((), ()))):
    return lax.dot_general(a, b, dims, preferred_element_type=F32, precision=lax.Precision.HIGHEST)


def _split_bf16(x, terms):
    out = []
    for _ in range(terms - 1):
        t = x.astype(BF16)
        out.append(t)
        x = x - t.astype(F32)
    out.append(x.astype(BF16))
    return out


def _dot_split(x, onehots, terms, dims=(((1,), (0,)), ((), ()))):
    acc = None
    for t in _split_bf16(x, terms):
        p = lax.dot_general(t, onehots, dims, preferred_element_type=F32)
        acc = p if acc is None else acc + p
    return acc


def _dot_split_rhs(onehots, x, terms, dims=(((1,), (0,)), ((), ()))):
    acc = None
    for t in _split_bf16(x, terms):
        p = lax.dot_general(onehots, t, dims, preferred_element_type=F32)
        acc = p if acc is None else acc + p
    return acc


def _sigmoid(x):
    return 1.0 / (1.0 + jnp.exp(-x))


def _softplus(x):
    return jnp.maximum(x, 0.0) + jnp.log(1.0 + jnp.exp(-jnp.abs(x)))


_GELU_C = math.sqrt(2.0 / math.pi)


def _gelu(x):
    return 0.5 * x * (1.0 + jnp.tanh(_GELU_C * (x + 0.044715 * x * x * x)))


def _gelu_grad(x):
    t = jnp.tanh(_GELU_C * (x + 0.044715 * x * x * x))
    return 0.5 * (1.0 + t) + 0.5 * x * (1.0 - t * t) * _GELU_C * (1.0 + 3.0 * 0.044715 * x * x)


def _rms(x):
    r = lax.rsqrt(jnp.mean(x * x, axis=-1, keepdims=True) + EPS)
    return x * r, r


def _rms_bwd(xn, r, dxn):
    return r * (dxn - xn * jnp.mean(dxn * xn, axis=-1, keepdims=True))


def _row_spec(tm, width, col=0):
    return pl.BlockSpec((tm, width), lambda i: (i, col))


def _const_spec(shape):
    nd = len(shape)
    return pl.BlockSpec(shape, lambda i: (0,) * nd)


def _hbm_spec():
    return pl.BlockSpec(memory_space=pl.ANY)


def _inproj_fwd(x, g, wp):
    T = x.shape[0]
    tm = TOKEN_TILE

    def body(x_ref, g_ref, w_hbm, z_ref, xbc_ref, u5_ref, gt_ref, dt_ref, w_ref):
        @pl.when(pl.program_id(0) == 0)
        def _():
            pltpu.sync_copy(w_hbm, w_ref)

        xn, _ = _rms(x_ref[...])
        h = (xn * g_ref[...]).astype(BF16)
        z_ref[...] = _dot(h, w_ref[:, P_Z:P_XBC])
        xbc_ref[...] = _dot(h, w_ref[:, P_XBC:P_U5])
        u5_ref[...] = _dot(h, w_ref[:, P_U5:P_G])
        gt_ref[...] = _dot(h, w_ref[:, P_G:P_DT])
        dt_ref[...] = _dot(h, w_ref[:, P_DT:P_END])

    widths = (1024, 2048, 512, 2048, DT_PAD)
    return _pc(
        body, name="inproj_fwd", grid=(T // tm,),
        in_specs=[_row_spec(tm, D_MODEL), _const_spec((1, D_MODEL)), _hbm_spec()],
        out_specs=[_row_spec(tm, w) for w in widths],
        out_shape=[jax.ShapeDtypeStruct((T, w), F32) for w in widths],
        scratch_shapes=[pltpu.VMEM((D_MODEL, P_END), BF16)],
        compiler_params=_cparams(("arbitrary",)),
    )(x, g, wp)


def _inproj_bwd(x, dx1, dz, dxbc, du5, dgt, ddt, g, wp):
    T = x.shape[0]
    tm = TOKEN_TILE

    def body(x_ref, dx1_ref, dz_ref, dxbc_ref, du5_ref, dgt_ref, ddt_ref, g_ref, w_hbm, dx_ref, h_ref, dg_ref, w_ref):
        @pl.when(pl.program_id(0) == 0)
        def _():
            pltpu.sync_copy(w_hbm, w_ref)
            dg_ref[...] = jnp.zeros_like(dg_ref)

        xn, r = _rms(x_ref[...])
        gv = g_ref[...]
        h_ref[...] = (xn * gv).astype(BF16)
        dh = _dot_nt(dz_ref[...].astype(BF16), w_ref[:, P_Z:P_XBC])
        dh += _dot_nt(dxbc_ref[...].astype(BF16), w_ref[:, P_XBC:P_U5])
        dh += _dot_nt(du5_ref[...].astype(BF16), w_ref[:, P_U5:P_G])
        dh += _dot_nt(dgt_ref[...].astype(BF16), w_ref[:, P_G:P_DT])
        dh += _dot_nt(ddt_ref[...].astype(BF16), w_ref[:, P_DT:P_END])
        dg_ref[...] += jnp.sum(dh * xn, axis=0, keepdims=True)
        dx_ref[...] = dx1_ref[...] + _rms_bwd(xn, r, dh * gv)

    return _pc(
        body, name="inproj_bwd", grid=(T // tm,),
        in_specs=[_row_spec(tm, 1024), _row_spec(tm, 1024), _row_spec(tm, 1024), _row_spec(tm, 2048),
                  _row_spec(tm, 512), _row_spec(tm, 2048), _row_spec(tm, DT_PAD), _const_spec((1, 1024)), _hbm_spec()],
        out_specs=[_row_spec(tm, 1024), _row_spec(tm, 1024), _const_spec((1, 1024))],
        out_shape=[jax.ShapeDtypeStruct((T, 1024), F32), jax.ShapeDtypeStruct((T, 1024), BF16),
                   jax.ShapeDtypeStruct((1, 1024), F32)],
        scratch_shapes=[pltpu.VMEM((D_MODEL, P_END), BF16)],
        compiler_params=_cparams(("arbitrary",)),
    )(x, dx1, dz, dxbc, du5, dgt, ddt, g, wp)


def _conv_fwd(xbc_raw, dt_raw, conv_w, conv_b, dt_bias):
    T = xbc_raw.shape[0]
    tm = TOKEN_TILE

    def body(u_ref, dtr_ref, w_ref, b_ref, db_ref, act_ref, dt_ref, ext_ref):
        @pl.when(pl.program_id(0) == 0)
        def _():
            ext_ref[0:HALO, :] = jnp.zeros((HALO, CONV_DIM), F32)

        ext_ref[HALO:, :] = u_ref[...]
        y = b_ref[...] + jnp.zeros((tm, CONV_DIM), F32)
        for k in range(CONV_K):
            y += w_ref[k:k + 1, :] * ext_ref[pl.ds(HALO - (CONV_K - 1) + k, tm), :]
        act_ref[...] = y * _sigmoid(y)
        ext_ref[0:HALO, :] = u_ref[tm - HALO:tm, :]
        dt_ref[...] = _softplus(dtr_ref[...] + db_ref[...])

    return _pc(
        body, name="conv_fwd", grid=(T // tm,),
        in_specs=[_row_spec(tm, CONV_DIM), _row_spec(tm, DT_PAD), _const_spec((CONV_K, CONV_DIM)),
                  _const_spec((1, CONV_DIM)), _const_spec((1, DT_PAD))],
        out_specs=[_row_spec(tm, CONV_DIM), _row_spec(tm, DT_PAD)],
        out_shape=[jax.ShapeDtypeStruct((T, CONV_DIM), F32), jax.ShapeDtypeStruct((T, DT_PAD), F32)],
        scratch_shapes=[pltpu.VMEM((tm + HALO, CONV_DIM), F32)],
        compiler_params=_cparams(("arbitrary",)),
    )(xbc_raw, dt_raw, conv_w, conv_b, dt_bias)


def _conv_bwd(xbc_raw, dt_raw, dxs_a, dxs_b, dB, dC, ddt, conv_w, conv_b, dt_bias):
    T = xbc_raw.shape[0]
    tm = TOKEN_TILE
    n = T // tm
    hb = tm // HALO

    def rev(width):
        return pl.BlockSpec((tm, width), lambda i: (n - 1 - i, 0))

    def body(u_ref, up_ref, dtr_ref, dxa_ref, dxb_ref, dB_ref, dC_ref, ddt_ref, w_ref, b_ref, db_ref,
             du_ref, ddtr_ref, dw_ref, dcb_ref, ddb_ref, ext_ref, dye_ref):
        i = pl.program_id(0)

        @pl.when(i == 0)
        def _():
            dye_ref[tm:, :] = jnp.zeros((HALO, CONV_DIM), F32)
            dw_ref[...] = jnp.zeros_like(dw_ref)
            dcb_ref[...] = jnp.zeros_like(dcb_ref)
            ddb_ref[...] = jnp.zeros_like(ddb_ref)

        first = (i == n - 1).astype(F32)
        ext_ref[0:HALO, :] = up_ref[...] * (1.0 - first)
        ext_ref[HALO:, :] = u_ref[...]
        y = b_ref[...] + jnp.zeros((tm, CONV_DIM), F32)
        for k in range(CONV_K):
            y += w_ref[k:k + 1, :] * ext_ref[pl.ds(HALO - (CONV_K - 1) + k, tm), :]
        s = _sigmoid(y)
        dsilu = s * (1.0 + y * (1.0 - s))
        dy = jnp.concatenate([dxa_ref[...] + dxb_ref[...], dB_ref[...], dC_ref[...]], axis=1) * dsilu
        dye_ref[0:tm, :] = dy
        dcb_ref[...] += jnp.sum(dy, axis=0, keepdims=True)
        du = jnp.zeros((tm, CONV_DIM), F32)
        for k in range(CONV_K):
            dw_ref[k:k + 1, :] += jnp.sum(dy * ext_ref[pl.ds(HALO - (CONV_K - 1) + k, tm), :], axis=0, keepdims=True)
            du += w_ref[k:k + 1, :] * dye_ref[pl.ds(CONV_K - 1 - k, tm), :]
        du_ref[...] = du
        dye_ref[tm:, :] = dy[0:HALO, :]
        sg = _sigmoid(dtr_ref[...] + db_ref[...])
        ddtr = ddt_ref[...] * sg
        ddtr_ref[...] = ddtr
        ddb_ref[...] += jnp.sum(ddtr, axis=0, keepdims=True)

    prev_spec = pl.BlockSpec((HALO, CONV_DIM), lambda i: (jnp.maximum((n - 1 - i) * hb - 1, 0), 0))
    return _pc(
        body, name="conv_bwd", grid=(n,),
        in_specs=[rev(CONV_DIM), prev_spec, rev(DT_PAD), rev(1024), rev(1024), rev(512), rev(512), rev(DT_PAD),
                  _const_spec((CONV_K, CONV_DIM)), _const_spec((1, CONV_DIM)), _const_spec((1, DT_PAD))],
        out_specs=[rev(CONV_DIM), rev(DT_PAD), _const_spec((HALO, CONV_DIM)), _const_spec((1, CONV_DIM)),
                   _const_spec((1, DT_PAD))],
        out_shape=[jax.ShapeDtypeStruct((T, CONV_DIM), F32), jax.ShapeDtypeStruct((T, DT_PAD), F32),
                   jax.ShapeDtypeStruct((HALO, CONV_DIM), F32), jax.ShapeDtypeStruct((1, CONV_DIM), F32),
                   jax.ShapeDtypeStruct((1, DT_PAD), F32)],
        scratch_shapes=[pltpu.VMEM((tm + HALO, CONV_DIM), F32), pltpu.VMEM((tm + HALO, CONV_DIM), F32)],
        compiler_params=_cparams(("arbitrary",)),
    )(xbc_raw, xbc_raw, dt_raw, dxs_a, dxs_b, dB, dC, ddt, conv_w, conv_b, dt_bias)


GROUP_LANES = SSD_HPG * SSD_HEADDIM


def _ssd_expanders():
    head = jnp.arange(DT_PAD)[:, None]
    to_wide = (jnp.arange(SSD_INNER)[None, :] // SSD_HEADDIM == head).astype(BF16)
    to_cols = (jnp.arange(SSD_HEADS * SSD_CHUNK)[None, :] // SSD_CHUNK == head).astype(BF16)
    return to_wide, to_wide.T, to_cols


def _ssd_prep(dt_ref, alog_ref, wide_ref, cols_ref):
    q = SSD_CHUNK
    a = -jnp.exp(alog_ref[...])
    dtv = dt_ref[...]
    la = dtv * a
    row = lax.broadcasted_iota(jnp.int32, (q, q), 0)
    col = lax.broadcasted_iota(jnp.int32, (q, q), 1)
    tri = (col <= row).astype(BF16)
    cum = _dot_split_rhs(tri, la, 3)
    cum_t = _dot_split(la, tri, 3, (((0,), (1,)), ((), ())))
    dtw = _dot_split(dtv, wide_ref[...], 2)
    cumw = _dot_split(cum, wide_ref[...], 3)
    segcol = _dot_split(cum, cols_ref[...], 3)
    return a, dtv, row, col, tri, cum_t, dtw, cumw, segcol


def _decay(segcol, cum_t, h, keep):
    return jnp.where(keep, jnp.exp(jnp.minimum(segcol[:, 128 * h:128 * h + 128] - cum_t[h:h + 1, :], 0.0)), 0.0)


def _decay_t(segcol, cum_t, h, keep_t):
    return jnp.where(keep_t, jnp.exp(jnp.minimum(cum_t[h:h + 1, :] - segcol[:, 128 * h:128 * h + 128], 0.0)), 0.0)


def _ssd_fwd(xbc_act, dt, alog):
    T = xbc_act.shape[0]
    q = SSD_CHUNK
    nc = T // q
    to_wide, _, to_cols = _ssd_expanders()

    def body(xbc_ref, dt_ref, alog_ref, wide_ref, cols_ref, y_ref, sp_ref, st_ref, xd_ref, xde_ref):
        @pl.when(pl.program_id(0) == 0)
        def _():
            st_ref[...] = jnp.zeros_like(st_ref)

        a, dtv, row, col, tri, cum_t, dtw, cumw, segcol = _ssd_prep(dt_ref, alog_ref, wide_ref, cols_ref)
        clw = cumw[q - 1:q, :]
        ecw = jnp.exp(cumw)
        xd = xbc_ref[:, 0:SSD_INNER] * dtw
        xd_ref[...] = xd.astype(BF16)
        xde_ref[...] = (xd * jnp.exp(clw - cumw)).astype(BF16)
        cdw = jnp.exp(clw)
        keep = col <= row
        sp_ref[0] = st_ref[...]
        for g in range(SSD_GROUPS):
            gl = slice(GROUP_LANES * g, GROUP_LANES * (g + 1))
            bb = xbc_ref[:, 1024 + 128 * g:1152 + 128 * g].astype(BF16)
            cb = xbc_ref[:, 1536 + 128 * g:1664 + 128 * g].astype(BF16)
            gm = _dot_nt(cb, bb)
            stp = st_ref[g]
            yoff = _dot(cb, stp.astype(BF16)) * ecw[:, gl]
            for r in range(SSD_HPG):
                h = SSD_HPG * g + r
                m = (gm * _decay(segcol, cum_t, h, keep)).astype(BF16)
                y_ref[:, 64 * h:64 * h + 64] = _dot(m, xd_ref[:, 64 * h:64 * h + 64]) + yoff[:, 64 * r:64 * r + 64]
            st_ref[g] = stp * cdw[:, gl] + _dot_tn(bb, xde_ref[:, gl])

    return _pc(
        body, name="ssd_fwd", grid=(nc,),
        in_specs=[_row_spec(q, CONV_DIM), _row_spec(q, DT_PAD), _const_spec((1, DT_PAD)),
                  _const_spec(to_wide.shape), _const_spec(to_cols.shape)],
        out_specs=[_row_spec(q, SSD_INNER),
                   pl.BlockSpec((1, SSD_GROUPS, SSD_STATE, GROUP_LANES), lambda i: (i, 0, 0, 0))],
        out_shape=[jax.ShapeDtypeStruct((T, SSD_INNER), F32),
                   jax.ShapeDtypeStruct((nc, SSD_GROUPS, SSD_STATE, GROUP_LANES), F32)],
        scratch_shapes=[pltpu.VMEM((SSD_GROUPS, SSD_STATE, GROUP_LANES), F32), pltpu.VMEM((q, SSD_INNER), BF16),
                        pltpu.VMEM((q, SSD_INNER), BF16)],
        compiler_params=_cparams(("arbitrary",)),
    )(xbc_act, dt, alog, to_wide, to_cols)


def _ssd_bwd(xbc_act, dt, alog, sprev, dy):
    T = xbc_act.shape[0]
    q = SSD_CHUNK
    nc = T // q
    to_wide, to_heads, to_cols = _ssd_expanders()

    def rev(width):
        return pl.BlockSpec((q, width), lambda i: (nc - 1 - i, 0))

    def body(xbc_ref, dt_ref, alog_ref, sp_ref, dy_ref, wide_ref, heads_ref, cols_ref,
             dxs_ref, dB_ref, dC_ref, ddt_ref, dalog_ref, ds_ref, xd_ref, dxd_ref):
        i = pl.program_id(0)

        @pl.when(i == 0)
        def _():
            ds_ref[...] = jnp.zeros_like(ds_ref)
            dalog_ref[...] = jnp.zeros_like(dalog_ref)

        a, dtv, row, col, tri, cum_t, dtw, cumw, segcol = _ssd_prep(dt_ref, alog_ref, wide_ref, cols_ref)
        clw = cumw[q - 1:q, :]
        ecw = jnp.exp(cumw)
        dew = jnp.exp(clw - cumw)
        cdw = jnp.exp(clw)
        xs = xbc_ref[:, 0:SSD_INNER]
        xd = xs * dtw
        xd_ref[...] = xd.astype(BF16)
        dyv = dy_ref[...]
        dye = (dyv * ecw).astype(BF16)
        xde = (xd * dew).astype(BF16)
        keep = col <= row
        keep_t = col >= row
        rows_k = lax.broadcasted_iota(jnp.int32, (SSD_HPG * q, DT_PAD), 0) // q
        lanes_k = lax.broadcasted_iota(jnp.int32, (SSD_HPG * q, DT_PAD), 1)
        dcw_parts = []
        dcum = jnp.zeros((q, DT_PAD), F32)
        for g in range(SSD_GROUPS):
            gl = slice(GROUP_LANES * g, GROUP_LANES * (g + 1))
            bb = xbc_ref[:, 1024 + 128 * g:1152 + 128 * g].astype(BF16)
            cb = xbc_ref[:, 1536 + 128 * g:1664 + 128 * g].astype(BF16)
            gm = _dot_nt(cb, bb)
            gmt = _dot_nt(bb, cb)
            stp = sp_ref[0, g]
            dst = ds_ref[g]
            stpb = stp.astype(BF16)
            dstb = dst.astype(BF16)
            yoff = _dot(cb, stpb) * ecw[:, gl]
            dcg = _dot_nt(dye[:, gl], stpb)
            ds_ref[g] = dst * cdw[:, gl] + _dot_tn(cb, dye[:, gl])
            dlast = jnp.sum(dst * stp, axis=0, keepdims=True) * cdw[:, gl]
            dbg = _dot_nt(xde[:, gl], dstb)
            w = _dot(bb, dstb) * dew[:, gl]
            wx = w * xd[:, gl]
            dlast = dlast + jnp.sum(wx, axis=0, keepdims=True)
            dcw_parts.append(dyv[:, gl] * yoff - wx
                             + jnp.where(lax.broadcasted_iota(jnp.int32, (q, 1), 0) == q - 1, dlast, 0.0))
            dgm = jnp.zeros((q, q), F32)
            diag = []
            for r in range(SSD_HPG):
                h = SSD_HPG * g + r
                hl = slice(64 * h, 64 * h + 64)
                dyb = dy_ref[:, hl].astype(BF16)
                xdh = xd_ref[:, hl]
                dm = _dot_nt(dyb, xdh)
                dmt = _dot_nt(xdh, dyb)
                dec = _decay(segcol, cum_t, h, keep)
                mt = gmt * _decay_t(segcol, cum_t, h, keep_t)
                dgm += dm * dec
                diag.append(dm * (gm * dec) - dmt * mt)
                dxd_ref[:, hl] = _dot(mt.astype(BF16), dyb) + w[:, 64 * r:64 * r + 64]
            onehots = (lanes_k == SSD_HPG * g + rows_k).astype(BF16)
            dcum += _dot_split(jnp.concatenate(diag, axis=1), onehots, 2)
            dgb = dgm.astype(BF16)
            dC_ref[:, 128 * g:128 * g + 128] = dcg + _dot(dgb, bb)
            dB_ref[:, 128 * g:128 * g + 128] = dbg + _dot_tn(dgb, cb)
        dxd = dxd_ref[...]
        dxs_ref[...] = dxd * dtw
        dcum += _dot_split(jnp.concatenate(dcw_parts, axis=1), heads_ref[...], 2)
        dla = _dot_split_rhs(tri, dcum, 3, (((0,), (0,)), ((), ())))
        ddt_ref[...] = _dot_split(xs * dxd, heads_ref[...], 2) + dla * a
        dalog_ref[...] += jnp.sum(dla * dtv, axis=0, keepdims=True)

        @pl.when(i == nc - 1)
        def _():
            dalog_ref[...] = dalog_ref[...] * a

    st_spec = pl.BlockSpec((1, SSD_GROUPS, SSD_STATE, GROUP_LANES), lambda i: (nc - 1 - i, 0, 0, 0))
    return _pc(
        body, name="ssd_bwd", grid=(nc,),
        in_specs=[rev(CONV_DIM), rev(DT_PAD), _const_spec((1, DT_PAD)), st_spec, rev(SSD_INNER),
                  _const_spec(to_wide.shape), _const_spec(to_heads.shape), _const_spec(to_cols.shape)],
        out_specs=[rev(SSD_INNER), rev(512), rev(512), rev(DT_PAD), _const_spec((1, DT_PAD))],
        out_shape=[jax.ShapeDtypeStruct((T, SSD_INNER), F32), jax.ShapeDtypeStruct((T, 512), F32),
                   jax.ShapeDtypeStruct((T, 512), F32), jax.ShapeDtypeStruct((T, DT_PAD), F32),
                   jax.ShapeDtypeStruct((1, DT_PAD), F32)],
        scratch_shapes=[pltpu.VMEM((SSD_GROUPS, SSD_STATE, GROUP_LANES), F32), pltpu.VMEM((q, SSD_INNER), BF16),
                        pltpu.VMEM((q, SSD_INNER), F32)],
        compiler_params=_cparams(("arbitrary",)),
    )(xbc_act, dt, alog, sprev, dy, to_wide, to_heads, to_cols)


def _s5_disc_vals(a_re, a_im, log_dt, b_re, b_im):
    dt = jnp.exp(log_dt)
    mag = jnp.exp(a_re * dt)
    ab_re = mag * jnp.cos(a_im * dt)
    ab_im = mag * jnp.sin(a_im * dt)
    den = a_re * a_re + a_im * a_im
    nr = ab_re - 1.0
    ni = ab_im
    coef_re = (nr * a_re + ni * a_im) / den
    coef_im = (ni * a_re - nr * a_im) / den
    bb_re = coef_re * b_re - coef_im * b_im
    bb_im = coef_re * b_im + coef_im * b_re
    return ab_re, ab_im, bb_re, bb_im


def _s5_disc(a_re, a_im, log_dt, b_re, b_im):
    def body(ar, ai, ld, br, bi, o1, o2, o3, o4):
        o1[...], o2[...], o3[...], o4[...] = _s5_disc_vals(ar[...], ai[...], ld[...], br[...], bi[...])

    return _pc(
        body, name="s5_disc",
        out_shape=[jax.ShapeDtypeStruct((S5_STATES, 1), F32), jax.ShapeDtypeStruct((S5_STATES, 1), F32),
                   jax.ShapeDtypeStruct((S5_STATES, 16), F32), jax.ShapeDtypeStruct((S5_STATES, 16), F32)],
    )(a_re, a_im, log_dt, b_re, b_im)


def _s5_disc_bwd(a_re, a_im, log_dt, b_re, b_im, d_ab_re, d_ab_im, d_bb_re, d_bb_im):
    def body(ar, ai, ld, br, bi, g1, g2, g3, g4, o1, o2, o3, o4, o5):
        _, vjp = jax.vjp(_s5_disc_vals, ar[...], ai[...], ld[...], br[...], bi[...])
        d1, d2, d3, d4, d5 = vjp((g1[...], g2[...], g3[...], g4[...]))
        o1[...] = d1
        o2[...] = d2
        grp = lax.broadcasted_iota(jnp.int32, (32, S5_STATES), 0)
        st = lax.broadcasted_iota(jnp.int32, (32, S5_STATES), 1)
        sel = (st // 64 == grp).astype(F32)
        o3[...] = _dot_hi(sel, d3)
        o4[...] = d4
        o5[...] = d5

    return _pc(
        body, name="s5_disc_bwd",
        out_shape=[jax.ShapeDtypeStruct((S5_STATES, 1), F32), jax.ShapeDtypeStruct((S5_STATES, 1), F32),
                   jax.ShapeDtypeStruct((32, 1), F32),
                   jax.ShapeDtypeStruct((S5_STATES, 16), F32), jax.ShapeDtypeStruct((S5_STATES, 16), F32)],
    )(a_re, a_im, log_dt, b_re, b_im, d_ab_re, d_ab_im, d_bb_re, d_bb_im)


def _cmul_add(xr, xi, pr, pi, yr, yi):
    return xr + pr * yr - pi * yi, xi + pr * yi + pi * yr


def _powers(ar, ai, n):
    out = [(ar, ai)]
    for _ in range(n - 1):
        pr, pi = out[-1]
        out.append((pr * pr - pi * pi, 2.0 * pr * pi))
    return out


def _scan_causal(br, bi, pws, row):
    q = br.shape[0]
    k = 1
    for pr, pi in pws:
        keep = row >= k
        sr = jnp.where(keep, pltpu.roll(br, k, 0), 0.0)
        si = jnp.where(keep, pltpu.roll(bi, k, 0), 0.0)
        br, bi = _cmul_add(br, bi, pr, pi, sr, si)
        k *= 2
    assert k == q
    return br, bi


def _scan_anticausal(br, bi, pws, row):
    q = br.shape[0]
    k = 1
    for pr, pi in pws:
        keep = row < q - k
        sr = jnp.where(keep, pltpu.roll(br, q - k, 0), 0.0)
        si = jnp.where(keep, pltpu.roll(bi, q - k, 0), 0.0)
        br, bi = _cmul_add(br, bi, pr, pi, sr, si)
        k *= 2
    assert k == q
    return br, bi


_S5_LEVELS = int(math.log2(S5_CHUNK))
_BW = S5_STATES // S5_BLOCKS
_BI = S5_WIDTH // S5_BLOCKS


def _s5_fwd(u5, wb4, wc4, ab, dvec):
    T = u5.shape[0]
    q = S5_CHUNK
    nc = T // q

    def body(u_ref, wb_ref, wc_ref, ab_ref, d_ref, y_ref, sp_ref, carry_ref, pw_re, pw_im):
        i = pl.program_id(0)
        row = lax.broadcasted_iota(jnp.int32, (q, 1), 0)

        @pl.when(i == 0)
        def _():
            carry_ref[...] = jnp.zeros_like(carry_ref)
            for j in range(S5_BLOCKS):
                ar = ab_ref[0:1, _BW * j:_BW * (j + 1)]
                ai = ab_ref[1:2, _BW * j:_BW * (j + 1)]
                er = jnp.where(row == 0, ar, 0.0) + jnp.zeros((q, _BW), F32)
                ei = jnp.where(row == 0, ai, 0.0) + jnp.zeros((q, _BW), F32)
                pr, pi = _scan_causal(er, ei, _powers(ar, ai, _S5_LEVELS), row)
                pw_re[:, _BW * j:_BW * (j + 1)] = pr
                pw_im[:, _BW * j:_BW * (j + 1)] = pi

        sp_ref[0] = carry_ref[...]
        for j in range(S5_BLOCKS):
            sl = slice(_BW * j, _BW * (j + 1))
            ul = slice(_BI * j, _BI * (j + 1))
            ar = ab_ref[0:1, sl]
            ai = ab_ref[1:2, sl]
            u = u_ref[:, ul]
            bu = _dot(u.astype(BF16), wb_ref[j])
            sr, si = _scan_causal(bu[:, :_BW], bu[:, _BW:], _powers(ar, ai, _S5_LEVELS), row)
            sr, si = _cmul_add(sr, si, pw_re[:, sl], pw_im[:, sl], carry_ref[0:1, sl], carry_ref[1:2, sl])
            carry_ref[0:1, sl] = sr[q - 1:q, :]
            carry_ref[1:2, sl] = si[q - 1:q, :]
            s = jnp.concatenate([sr, si], axis=1).astype(BF16)
            y_ref[:, ul] = _dot(s, wc_ref[j]) + d_ref[:, ul] * u

    return _pc(
        body, name="s5_fwd", grid=(nc,),
        in_specs=[_row_spec(q, S5_WIDTH), _const_spec((S5_BLOCKS, _BI, 2 * _BW)), _const_spec((S5_BLOCKS, 2 * _BW, _BI)),
                  _const_spec((8, S5_STATES)), _const_spec((1, S5_WIDTH))],
        out_specs=[_row_spec(q, S5_WIDTH), pl.BlockSpec((1, 8, S5_STATES), lambda i: (i, 0, 0))],
        out_shape=[jax.ShapeDtypeStruct((T, S5_WIDTH), F32), jax.ShapeDtypeStruct((nc, 8, S5_STATES), F32)],
        scratch_shapes=[pltpu.VMEM((8, S5_STATES), F32), pltpu.VMEM((q, S5_STATES), F32), pltpu.VMEM((q, S5_STATES), F32)],
        compiler_params=_cparams(("arbitrary",)),
    )(u5, wb4, wc4, ab, dvec)


def _s5_bwd(u5, dy5, wb4, wc4, ab, dvec, sprev):
    T = u5.shape[0]
    q = S5_CHUNK
    nc = T // q

    def rev(width):
        return pl.BlockSpec((q, width), lambda i: (nc - 1 - i, 0))

    def body(u_ref, dy_ref, wb_ref, wc_ref, ab_ref, d_ref, sp_ref, du_ref, dwb_ref, dwc_ref, dab_ref, dd_ref,
             carry_ref, pw_re, pw_im, rp_re, rp_im):
        i = pl.program_id(0)
        row = lax.broadcasted_iota(jnp.int32, (q, 1), 0)

        @pl.when(i == 0)
        def _():
            carry_ref[...] = jnp.zeros_like(carry_ref)
            dwb_ref[...] = jnp.zeros_like(dwb_ref)
            dwc_ref[...] = jnp.zeros_like(dwc_ref)
            dab_ref[...] = jnp.zeros_like(dab_ref)
            dd_ref[...] = jnp.zeros_like(dd_ref)
            for j in range(S5_BLOCKS):
                sl = slice(_BW * j, _BW * (j + 1))
                ar = ab_ref[0:1, sl]
                ai = ab_ref[1:2, sl]
                zero = jnp.zeros((q, _BW), F32)
                pr, pi = _scan_causal(jnp.where(row == 0, ar, 0.0) + zero, jnp.where(row == 0, ai, 0.0) + zero,
                                      _powers(ar, ai, _S5_LEVELS), row)
                pw_re[:, sl] = pr
                pw_im[:, sl] = pi
                pr, pi = _scan_anticausal(jnp.where(row == q - 1, ar, 0.0) + zero, jnp.where(row == q - 1, -ai, 0.0) + zero,
                                          _powers(ar, -ai, _S5_LEVELS), row)
                rp_re[:, sl] = pr
                rp_im[:, sl] = pi

        for j in range(S5_BLOCKS):
            sl = slice(_BW * j, _BW * (j + 1))
            ul = slice(_BI * j, _BI * (j + 1))
            ar = ab_ref[0:1, sl]
            ai = ab_ref[1:2, sl]
            u = u_ref[:, ul]
            ub = u.astype(BF16)
            dy = dy_ref[:, ul]
            dyb = dy.astype(BF16)
            bu = _dot(ub, wb_ref[j])
            sr, si = _scan_causal(bu[:, :_BW], bu[:, _BW:], _powers(ar, ai, _S5_LEVELS), row)
            s0r = sp_ref[0, 0:1, sl]
            s0i = sp_ref[0, 1:2, sl]
            sr, si = _cmul_add(sr, si, pw_re[:, sl], pw_im[:, sl], s0r, s0i)
            ds = _dot_nt(dyb, wc_ref[j])
            lr, li = _scan_anticausal(ds[:, :_BW], ds[:, _BW:], _powers(ar, -ai, _S5_LEVELS), row)
            lr, li = _cmul_add(lr, li, rp_re[:, sl], rp_im[:, sl], carry_ref[0:1, sl], carry_ref[1:2, sl])
            carry_ref[0:1, sl] = lr[0:1, :]
            carry_ref[1:2, sl] = li[0:1, :]
            lam = jnp.concatenate([lr, li], axis=1).astype(BF16)
            du_ref[:, ul] = _dot_nt(lam, wb_ref[j]) + d_ref[:, ul] * dy
            dwb_ref[j] += _dot_tn(ub, lam)
            dwc_ref[j] += _dot_tn(jnp.concatenate([sr, si], axis=1).astype(BF16), dyb)
            keep = row >= 1
            pr = jnp.where(keep, pltpu.roll(sr, 1, 0), s0r)
            pi = jnp.where(keep, pltpu.roll(si, 1, 0), s0i)
            dab_ref[0:1, sl] += jnp.sum(lr * pr + li * pi, axis=0, keepdims=True)
            dab_ref[1:2, sl] += jnp.sum(li * pr - lr * pi, axis=0, keepdims=True)
            dd_ref[:, ul] += jnp.sum(dy * u, axis=0, keepdims=True)

    return _pc(
        body, name="s5_bwd", grid=(nc,),
        in_specs=[rev(S5_WIDTH), rev(S5_WIDTH), _const_spec((S5_BLOCKS, _BI, 2 * _BW)), _const_spec((S5_BLOCKS, 2 * _BW, _BI)),
                  _const_spec((8, S5_STATES)), _const_spec((1, S5_WIDTH)),
                  pl.BlockSpec((1, 8, S5_STATES), lambda i: (nc - 1 - i, 0, 0))],
        out_specs=[rev(S5_WIDTH), _const_spec((S5_BLOCKS, _BI, 2 * _BW)), _const_spec((S5_BLOCKS, 2 * _BW, _BI)),
                   _const_spec((8, S5_STATES)), _const_spec((1, S5_WIDTH))],
        out_shape=[jax.ShapeDtypeStruct((T, S5_WIDTH), F32), jax.ShapeDtypeStruct((S5_BLOCKS, _BI, 2 * _BW), F32),
                   jax.ShapeDtypeStruct((S5_BLOCKS, 2 * _BW, _BI), F32), jax.ShapeDtypeStruct((8, S5_STATES), F32),
                   jax.ShapeDtypeStruct((1, S5_WIDTH), F32)],
        scratch_shapes=[pltpu.VMEM((8, S5_STATES), F32)] + [pltpu.VMEM((q, S5_STATES), F32)] * 4,
        compiler_params=_cparams(("arbitrary",)),
    )(u5, dy5, wb4, wc4, ab, dvec, sprev)


def _merge_vals(ys, xs, z, y5, gates, dvec, gssd, glu_w, glu_b, wbr):
    sz = _sigmoid(z)
    qv = ys + dvec * xs
    pre = qv * (z * sz)
    yn, rs = [], []
    for gi in range(SSD_GROUPS):
        p, r = _rms(pre[:, 256 * gi:256 * (gi + 1)])
        yn.append(p)
        rs.append(r)
    yn = jnp.concatenate(yn, axis=1)
    ya = yn * gssd
    gel = _gelu(y5)
    sg = _sigmoid(_dot(gel.astype(BF16), glu_w) + glu_b)
    yb = gel * sg
    pa = _dot(ya.astype(BF16), wbr[0:SSD_INNER, :])
    pb = _dot(yb.astype(BF16), wbr[SSD_INNER:, :])
    s0 = _sigmoid(gates[:, :D_MODEL])
    s1 = _sigmoid(gates[:, D_MODEL:])
    merged = s0 * pa + s1 * pb
    return dict(sz=sz, qv=qv, yn=yn, rs=rs, ya=ya, gel=gel, sg=sg, yb=yb, pa=pa, pb=pb, s0=s0, s1=s1, merged=merged)


def _merge_specs(tm):
    acts = [_row_spec(tm, 1024), _row_spec(tm, 1024, 0), _row_spec(tm, 1024), _row_spec(tm, 512), _row_spec(tm, 2048),
            _row_spec(tm, 1024)]
    params = [_const_spec((1, 1024)), _const_spec((1, 1024)), _const_spec((512, 512)), _const_spec((1, 512)),
              _hbm_spec(), _hbm_spec()]
    return acts, params


def _merge_fwd(ys, xbc_act, z, y5, gates, x, dvec, gssd, glu_w, glu_b, wbr, wout):
    T = x.shape[0]
    tm = TOKEN_TILE
    acts, params = _merge_specs(tm)

    def body(ys_ref, xs_ref, z_ref, y5_ref, gt_ref, x_ref, dv_ref, gs_ref, gw_ref, gb_ref, wbr_hbm, wout_hbm, x1_ref,
             wbr_ref, wout_ref):
        @pl.when(pl.program_id(0) == 0)
        def _():
            pltpu.sync_copy(wbr_hbm, wbr_ref)
            pltpu.sync_copy(wout_hbm, wout_ref)

        v = _merge_vals(ys_ref[...], xs_ref[...], z_ref[...], y5_ref[...], gt_ref[...], dv_ref[...], gs_ref[...],
                        gw_ref[...], gb_ref[...], wbr_ref)
        x1_ref[...] = x_ref[...] + _dot(v["merged"].astype(BF16), wout_ref[...])

    return _pc(
        body, name="merge_fwd", grid=(T // tm,),
        in_specs=acts + params, out_specs=_row_spec(tm, 1024),
        out_shape=jax.ShapeDtypeStruct((T, 1024), F32),
        scratch_shapes=[pltpu.VMEM((1536, 1024), BF16), pltpu.VMEM((1024, 1024), BF16)],
        compiler_params=_cparams(("arbitrary",)),
    )(ys, xbc_act, z, y5, gates, x, dvec, gssd, glu_w, glu_b, wbr, wout)


def _merge_bwd(ys, xbc_act, z, y5, gates, dx1, dvec, gssd, glu_w, glu_b, wbr, wout, head_sel):
    T = dx1.shape[0]
    tm = TOKEN_TILE
    acts, params = _merge_specs(tm)

    def body(ys_ref, xs_ref, z_ref, y5_ref, gt_ref, dx1_ref, dv_ref, gs_ref, gw_ref, gb_ref, wbr_hbm, wout_hbm, hs_ref,
             dys_ref, dxs_ref, dz_ref, dy5_ref, dgt_ref, mg_ref, ya_ref, yb_ref, dpa_ref, dpb_ref, gel_ref, dpre_ref,
             ddv_ref, dgs_ref, dgb_ref, wbr_ref, wout_ref, ddacc_ref):
        i = pl.program_id(0)

        @pl.when(i == 0)
        def _():
            pltpu.sync_copy(wbr_hbm, wbr_ref)
            pltpu.sync_copy(wout_hbm, wout_ref)
            ddacc_ref[...] = jnp.zeros_like(ddacc_ref)
            dgs_ref[...] = jnp.zeros_like(dgs_ref)
            dgb_ref[...] = jnp.zeros_like(dgb_ref)

        ys, xs, z, y5, gates = ys_ref[...], xs_ref[...], z_ref[...], y5_ref[...], gt_ref[...]
        dvv, gsv, gw = dv_ref[...], gs_ref[...], gw_ref[...]
        v = _merge_vals(ys, xs, z, y5, gates, dvv, gsv, gw, gb_ref[...], wbr_ref)
        dmg = _dot_nt(dx1_ref[...].astype(BF16), wout_ref[...])
        s0, s1, pa, pb = v["s0"], v["s1"], v["pa"], v["pb"]
        dgt_ref[:, :D_MODEL] = dmg * pa * s0 * (1.0 - s0)
        dgt_ref[:, D_MODEL:] = dmg * pb * s1 * (1.0 - s1)
        dpa = (dmg * s0).astype(BF16)
        dpb = (dmg * s1).astype(BF16)
        dya = _dot_nt(dpa, wbr_ref[0:SSD_INNER, :])
        dyb = _dot_nt(dpb, wbr_ref[SSD_INNER:, :])
        gel, sg = v["gel"], v["sg"]
        dpre = (dyb * gel * sg * (1.0 - sg))
        dgb_ref[...] += jnp.sum(dpre, axis=0, keepdims=True)
        dpre_b = dpre.astype(BF16)
        dgel = dyb * sg + _dot_nt(dpre_b, gw)
        dy5_ref[...] = dgel * _gelu_grad(y5)
        yn = v["yn"]
        dgs_ref[...] += jnp.sum(dya * yn, axis=0, keepdims=True)
        dyn = dya * gsv
        dpre_a = jnp.concatenate(
            [_rms_bwd(yn[:, 256 * gi:256 * (gi + 1)], v["rs"][gi], dyn[:, 256 * gi:256 * (gi + 1)])
             for gi in range(SSD_GROUPS)], axis=1)
        sz, qv = v["sz"], v["qv"]
        dq = dpre_a * (z * sz)
        dz_ref[...] = dpre_a * qv * (sz * (1.0 + z * (1.0 - sz)))
        dys_ref[...] = dq
        dxs_ref[...] = dq * dvv
        ddacc_ref[...] += jnp.sum(dq * xs, axis=0, keepdims=True)
        mg_ref[...] = v["merged"].astype(BF16)
        ya_ref[...] = v["ya"].astype(BF16)
        yb_ref[...] = v["yb"].astype(BF16)
        dpa_ref[...] = dpa
        dpb_ref[...] = dpb
        gel_ref[...] = gel.astype(BF16)
        dpre_ref[...] = dpre_b

        @pl.when(i == pl.num_programs(0) - 1)
        def _():
            ddv_ref[...] = _dot_hi(ddacc_ref[...], hs_ref[...])

    outs = [(1024, F32), (1024, F32), (1024, F32), (512, F32), (2048, F32),
            (1024, BF16), (1024, BF16), (512, BF16), (1024, BF16), (1024, BF16), (512, BF16), (512, BF16)]
    return _pc(
        body, name="merge_bwd", grid=(T // tm,),
        in_specs=acts + params + [_const_spec((1024, DT_PAD))],
        out_specs=[_row_spec(tm, w) for w, _ in outs] + [_const_spec((1, DT_PAD)), _const_spec((1, 1024)), _const_spec((1, 512))],
        out_shape=[jax.ShapeDtypeStruct((T, w), d) for w, d in outs] + [
            jax.ShapeDtypeStruct((1, DT_PAD), F32), jax.ShapeDtypeStruct((1, 1024), F32), jax.ShapeDtypeStruct((1, 512), F32)],
        scratch_shapes=[pltpu.VMEM((1536, 1024), BF16), pltpu.VMEM((1024, 1024), BF16), pltpu.VMEM((1, 1024), F32)],
        compiler_params=_cparams(("arbitrary",)),
    )(ys, xbc_act, z, y5, gates, dx1, dvec, gssd, glu_w, glu_b, wbr, wout, head_sel)


def _mlp_fwd(x1, g, w1, w2):
    T = x1.shape[0]
    tm = TOKEN_TILE

    def body(x_ref, g_ref, w1_hbm, w2_hbm, o_ref, w1_ref, w2_ref):
        @pl.when(pl.program_id(0) == 0)
        def _():
            pltpu.sync_copy(w1_hbm, w1_ref)
            pltpu.sync_copy(w2_hbm, w2_ref)

        xv = x_ref[...]
        xn, _ = _rms(xv)
        a1 = _dot((xn * g_ref[...]).astype(BF16), w1_ref[...])
        rl = jnp.maximum(a1, 0.0)
        o_ref[...] = xv + _dot((rl * rl).astype(BF16), w2_ref[...])

    return _pc(
        body, name="mlp_fwd", grid=(T // tm,),
        in_specs=[_row_spec(tm, 1024), _const_spec((1, 1024)), _hbm_spec(), _hbm_spec()],
        out_specs=_row_spec(tm, 1024), out_shape=jax.ShapeDtypeStruct((T, 1024), F32),
        scratch_shapes=[pltpu.VMEM((D_MODEL, D_FF), BF16), pltpu.VMEM((D_FF, D_MODEL), BF16)],
        compiler_params=_cparams(("arbitrary",)),
    )(x1, g, w1, w2)


def _mlp_bwd(x1, dx2, g, w1, w2):
    T = x1.shape[0]
    tm = TOKEN_TILE

    def body(x_ref, dx2_ref, g_ref, w1_hbm, w2_hbm, dx1_ref, h_ref, act_ref, da_ref, dg_ref, w1_ref, w2_ref):
        @pl.when(pl.program_id(0) == 0)
        def _():
            pltpu.sync_copy(w1_hbm, w1_ref)
            pltpu.sync_copy(w2_hbm, w2_ref)
            dg_ref[...] = jnp.zeros_like(dg_ref)

        xn, r = _rms(x_ref[...])
        gv = g_ref[...]
        h = (xn * gv).astype(BF16)
        h_ref[...] = h
        rl = jnp.maximum(_dot(h, w1_ref[...]), 0.0)
        act_ref[...] = (rl * rl).astype(BF16)
        dx2 = dx2_ref[...]
        da = (_dot_nt(dx2.astype(BF16), w2_ref[...]) * (2.0 * rl)).astype(BF16)
        da_ref[...] = da
        dh = _dot_nt(da, w1_ref[...])
        dg_ref[...] += jnp.sum(dh * xn, axis=0, keepdims=True)
        dx1_ref[...] = dx2 + _rms_bwd(xn, r, dh * gv)

    return _pc(
        body, name="mlp_bwd", grid=(T // tm,),
        in_specs=[_row_spec(tm, 1024), _row_spec(tm, 1024), _const_spec((1, 1024)), _hbm_spec(), _hbm_spec()],
        out_specs=[_row_spec(tm, 1024), _row_spec(tm, 1024), _row_spec(tm, D_FF), _row_spec(tm, D_FF), _const_spec((1, 1024))],
        out_shape=[jax.ShapeDtypeStruct((T, 1024), F32), jax.ShapeDtypeStruct((T, 1024), BF16),
                   jax.ShapeDtypeStruct((T, D_FF), BF16), jax.ShapeDtypeStruct((T, D_FF), BF16),
                   jax.ShapeDtypeStruct((1, 1024), F32)],
        scratch_shapes=[pltpu.VMEM((D_MODEL, D_FF), BF16), pltpu.VMEM((D_FF, D_MODEL), BF16)],
        compiler_params=_cparams(("arbitrary",)),
    )(x1, dx2, g, w1, w2)


def _loss_head(x2, target, g):
    T = x2.shape[0]
    tm = TOKEN_TILE

    def body(x_ref, t_ref, g_ref, dx_ref, loss_ref, dg_ref):
        @pl.when(pl.program_id(0) == 0)
        def _():
            loss_ref[...] = jnp.zeros_like(loss_ref)
            dg_ref[...] = jnp.zeros_like(dg_ref)

        xn, r = _rms(x_ref[...])
        gv = g_ref[...]
        err = xn * gv - t_ref[...]
        loss_ref[...] += jnp.sum(err * err, axis=0, keepdims=True) * (0.5 / D_MODEL)
        dy = err * (1.0 / D_MODEL)
        dg_ref[...] += jnp.sum(dy * xn, axis=0, keepdims=True)
        dx_ref[...] = _rms_bwd(xn, r, dy * gv)

    return _pc(
        body, name="loss_head", grid=(T // tm,),
        in_specs=[_row_spec(tm, 1024), _row_spec(tm, 1024), _const_spec((1, 1024))],
        out_specs=[_row_spec(tm, 1024), _const_spec((1, 1024)), _const_spec((1, 1024))],
        out_shape=[jax.ShapeDtypeStruct((T, 1024), F32), jax.ShapeDtypeStruct((1, 1024), F32),
                   jax.ShapeDtypeStruct((1, 1024), F32)],
        compiler_params=_cparams(("arbitrary",)),
    )(x2, target, g)


WGRAD_TOKENS = 512
WGRAD_OUT_ELEMS = 2 * 1024 * 1024


def _wgrad(a, b, name):
    T, K = a.shape
    N = b.shape[1]
    tt = min(T, WGRAD_TOKENS)
    nb = min(N, max(128, WGRAD_OUT_ELEMS // K))
    assert N % nb == 0 and T % tt == 0

    def body(a_ref, b_ref, o_ref):
        @pl.when(pl.program_id(1) == 0)
        def _():
            o_ref[...] = jnp.zeros_like(o_ref)

        o_ref[...] += _dot_tn(a_ref[...].astype(BF16), b_ref[...].astype(BF16))

    return _pc(
        body, name=name, grid=(N // nb, T // tt),
        in_specs=[pl.BlockSpec((tt, K), lambda n, t: (t, 0)), pl.BlockSpec((tt, nb), lambda n, t: (t, n))],
        out_specs=pl.BlockSpec((K, nb), lambda n, t: (0, n)),
        out_shape=jax.ShapeDtypeStruct((K, N), F32),
        compiler_params=_cparams(("parallel", "arbitrary")),
    )(a, b)


def _s5_block_weights(bb_re, bb_im, c_re, c_im):
    eye = jnp.eye(8, dtype=F32)
    bre = bb_re.reshape(S5_BLOCKS, 8, 64, 16)
    bim = bb_im.reshape(S5_BLOCKS, 8, 64, 16)
    wb_re = jnp.einsum('jgpk,gh->jhkgp', bre, eye).reshape(S5_BLOCKS, _BI, _BW)
    wb_im = jnp.einsum('jgpk,gh->jhkgp', bim, eye).reshape(S5_BLOCKS, _BI, _BW)
    wb4 = jnp.concatenate([wb_re, wb_im], axis=2).astype(BF16)
    cre = c_re.reshape(S5_BLOCKS, 8, 16, 64)
    cim = c_im.reshape(S5_BLOCKS, 8, 16, 64)
    wc_re = jnp.einsum('jgkp,gh->jgphk', cre, eye).reshape(S5_BLOCKS, _BW, _BI)
    wc_im = jnp.einsum('jgkp,gh->jgphk', -cim, eye).reshape(S5_BLOCKS, _BW, _BI)
    wc4 = jnp.concatenate([wc_re, wc_im], axis=1).astype(BF16)
    return wb4, wc4


def _s5_block_grads(dwb4, dwc4):
    eye = jnp.eye(8, dtype=F32)
    dwb = dwb4.reshape(S5_BLOCKS, 8, 16, 2, 8, 64)
    dbb = jnp.einsum('jhkrgp,gh->rjgpk', dwb, eye).reshape(2, 32, 64, 16)
    dwc = dwc4.reshape(S5_BLOCKS, 2, 8, 64, 8, 16)
    dc = jnp.einsum('jrgphk,gh->rjgkp', dwc, eye).reshape(2, 32, 16, 64)
    return dbb[0], dbb[1], dc[0], -dc[1]


def _permute_w_in(w_in):
    pad = jnp.zeros((D_MODEL, DT_PAD - 16), w_in.dtype)
    return jnp.concatenate([w_in[:, :OFF_DT], w_in[:, OFF_U:], w_in[:, OFF_DT:OFF_U], pad], axis=1)


def _row(v, width=None):
    v = v.reshape(1, -1)
    if width is not None and v.shape[1] < width:
        v = jnp.concatenate([v, jnp.zeros((1, width - v.shape[1]), v.dtype)], axis=1)
    return v


def _local_step(x, target, p):
    g_mix, g_mlp, g_fin = _row(p["norm_mix_g"]), _row(p["norm_mlp_g"]), _row(p["norm_final_g"])
    conv_b = _row(p["conv_b"])
    dt_bias = _row(p["dt_bias"], DT_PAD)
    alog = _row(p["a_log"], DT_PAD)
    dvec = _row(jnp.repeat(p["d_ssd"], SSD_HEADDIM))
    gssd = _row(p["ssd_norm_g"])
    s5d = _row(p["s5_d"])
    glu_b = _row(p["s5_glu_b"])
    head_sel = (jnp.arange(SSD_INNER)[:, None] // SSD_HEADDIM == jnp.arange(DT_PAD)[None, :]).astype(F32)

    a_re = p["s5_a_re"].reshape(S5_STATES, 1)
    a_im = p["s5_a_im"].reshape(S5_STATES, 1)
    log_dt = jnp.repeat(p["s5_log_dt"], 64).reshape(S5_STATES, 1)
    b_re = p["s5_b_re"].reshape(S5_STATES, 16)
    b_im = p["s5_b_im"].reshape(S5_STATES, 16)
    ab_re, ab_im, bb_re, bb_im = _s5_disc(a_re, a_im, log_dt, b_re, b_im)
    wb4, wc4 = _s5_block_weights(bb_re, bb_im, p["s5_c_re"], p["s5_c_im"])
    ab = jnp.concatenate([ab_re.reshape(1, S5_STATES), ab_im.reshape(1, S5_STATES), jnp.zeros((6, S5_STATES), F32)], axis=0)

    wp, wbr, wout, w1, w2, glu_w = p["w_in_perm"], p["w_branch"], p["w_out"], p["w_mlp_in"], p["w_mlp_out"], p["s5_glu_w"]

    z, xbc_raw, u5, gates, dt_raw = _inproj_fwd(x, g_mix, wp)
    xbc_act, dt = _conv_fwd(xbc_raw, dt_raw, p["conv_w"], conv_b, dt_bias)
    ys, ssd_states = _ssd_fwd(xbc_act, dt, alog)
    y5, s5_states = _s5_fwd(u5, wb4, wc4, ab, s5d)
    x1 = _merge_fwd(ys, xbc_act, z, y5, gates, x, dvec, gssd, glu_w, glu_b, wbr, wout)
    x2 = _mlp_fwd(x1, g_mlp, w1, w2)
    dx2, loss_lanes, d_gfin = _loss_head(x2, target, g_fin)

    dx1, h2, act, da1, d_gmlp = _mlp_bwd(x1, dx2, g_mlp, w1, w2)
    d_w_mlp_out = _wgrad(act, dx2, "wgrad_mlp_out")
    d_w_mlp_in = _wgrad(h2, da1, "wgrad_mlp_in")
    (dys, dxs_m, dz, dy5, dgates, mg, ya, yb, dpa, dpb, gel, dpre, d_dssd, d_gssd, d_glu_b) = _merge_bwd(
        ys, xbc_act, z, y5, gates, dx1, dvec, gssd, glu_w, glu_b, wbr, wout, head_sel)
    d_w_out = _wgrad(mg, dx1, "wgrad_out")
    d_w_branch = jnp.concatenate([_wgrad(ya, dpa, "wgrad_branch_a"), _wgrad(yb, dpb, "wgrad_branch_b")], axis=0)
    d_glu_w = _wgrad(gel, dpre, "wgrad_glu")
    du5, dwb4, dwc4, dab, d_s5d = _s5_bwd(u5, dy5, wb4, wc4, ab, s5d, s5_states)
    dbb_re, dbb_im, d_c_re, d_c_im = _s5_block_grads(dwb4, dwc4)
    d_a_re, d_a_im, d_log_dt, d_b_re, d_b_im = _s5_disc_bwd(
        a_re, a_im, log_dt, b_re, b_im, dab[0].reshape(S5_STATES, 1), dab[1].reshape(S5_STATES, 1),
        dbb_re.reshape(S5_STATES, 16), dbb_im.reshape(S5_STATES, 16))
    dxs_s, dB, dC, ddt, d_alog = _ssd_bwd(xbc_act, dt, alog, ssd_states, dys)
    dxbc_raw, ddt_raw, d_conv_w, d_conv_b, d_dt_bias = _conv_bwd(
        xbc_raw, dt_raw, dxs_m, dxs_s, dB, dC, ddt, p["conv_w"], conv_b, dt_bias)
    dx, h, d_gmix = _inproj_bwd(x, dx1, dz, dxbc_raw, du5, dgates, ddt_raw, g_mix, wp)
    d_w_in = jnp.concatenate([
        _wgrad(h, dz, "wgrad_in_z"), _wgrad(h, dxbc_raw, "wgrad_in_xbc"), _wgrad(h, ddt_raw, "wgrad_in_dt")[:, :16],
        _wgrad(h, du5, "wgrad_in_u5"), _wgrad(h, dgates, "wgrad_in_gates")], axis=1)

    grads = dict(
        norm_mix_g=d_gmix.reshape(-1), w_in=d_w_in, conv_w=d_conv_w[:CONV_K], conv_b=d_conv_b.reshape(-1),
        dt_bias=d_dt_bias[0, :16], a_log=d_alog[0, :16], d_ssd=d_dssd[0, :16], ssd_norm_g=d_gssd.reshape(-1),
        s5_a_re=d_a_re.reshape(32, 64), s5_a_im=d_a_im.reshape(32, 64), s5_log_dt=d_log_dt.reshape(32),
        s5_b_re=d_b_re.reshape(32, 64, 16), s5_b_im=d_b_im.reshape(32, 64, 16), s5_c_re=d_c_re, s5_c_im=d_c_im,
        s5_d=d_s5d.reshape(-1), s5_glu_w=d_glu_w, s5_glu_b=d_glu_b.reshape(-1), w_branch=d_w_branch, w_out=d_w_out,
        norm_mlp_g=d_gmlp.reshape(-1), w_mlp_in=d_w_mlp_in, w_mlp_out=d_w_mlp_out, norm_final_g=d_gfin.reshape(-1))
    return jnp.sum(loss_lanes), dx, grads


MESH = pl.DeviceIdType.MESH
N_CHIPS = 4


def _place():
    x, y, c = lax.axis_index("x"), lax.axis_index("y"), lax.axis_index("c")
    chips = [(1 - x, y), (x, 1 - y), (1 - x, 1 - y)]
    return x, y, c, chips


def _remote(src, dst, send_sems, recv_sems, k, to):
    return pltpu.make_async_remote_copy(src_ref=src, dst_ref=dst, send_sem=send_sems.at[k], recv_sem=recv_sems.at[k],
                                        device_id=to, device_id_type=MESH)


def _row_chunks(rows, k, align):
    step = rows // k
    assert rows % k == 0 and step % align == 0, (rows, k, align)
    return [(i * step, step) for i in range(k)]


ICI_CHUNKS = 4
D2D_CHUNKS = 24


def _allgather_chips(src, name, k_ici):
    R, C = src.shape
    H = R // 2
    pieces = _row_chunks(H, k_ici, 32 // src.dtype.itemsize)
    n = 3 * k_ici

    def body(src_ref, out_ref, send_sems, recv_sems):
        x, y, c, chips = _place()
        own = 2 * x + y
        sib = (x, y, 1 - c)

        def part(s, hc, r0, nr):
            return out_ref.at[s, pl.ds(hc * H + r0, nr), :]

        first = []
        for i, (r0, nr) in enumerate(pieces):
            for j, (cx, cy) in enumerate(chips):
                first.append(_remote(src_ref.at[pl.ds(c * H + r0, nr), :], part(own, c, r0, nr), send_sems, recv_sems,
                                     j * k_ici + i, (cx, cy, c)))
        for cp in first:
            cp.start()
        passed = []
        for i, (r0, nr) in enumerate(pieces):
            for j, (cx, cy) in enumerate(chips):
                got = part(2 * cx + cy, c, r0, nr)
                _remote(got, got, send_sems, recv_sems, j * k_ici + i, (cx, cy, c)).wait_recv()
                fw = _remote(got, got, send_sems, recv_sems, n + j * k_ici + i, sib)
                fw.start()
                passed.append(fw)
        for i, (r0, nr) in enumerate(pieces):
            for j, (cx, cy) in enumerate(chips):
                got = part(2 * cx + cy, 1 - c, r0, nr)
                _remote(got, got, send_sems, recv_sems, n + j * k_ici + i, sib).wait_recv()
        for cp in first + passed:
            cp.wait_send()

    return _pc(
        body, name=name, in_specs=[_hbm_spec()], out_specs=_hbm_spec(),
        out_shape=jax.ShapeDtypeStruct((N_CHIPS, R, C), src.dtype),
        scratch_shapes=[pltpu.SemaphoreType.DMA((2 * n,)), pltpu.SemaphoreType.DMA((2 * n,))],
    )(src)


def _pair_exchange(gpack, small):
    _, R, C = gpack.shape
    H = R // 2
    pieces = _row_chunks(H, D2D_CHUNKS, 8)

    def body(g_ref, s_ref, sib_ref, sibs_ref, send_sems, recv_sems):
        x, y, c, _ = _place()
        sib = (x, y, 1 - c)
        for s in range(N_CHIPS):
            for r0, nr in pieces:
                _remote(g_ref.at[s, pl.ds((1 - c) * H + r0, nr), :], sib_ref.at[s, pl.ds(r0, nr), :], send_sems, recv_sems, 0,
                        sib).start()
        sm = _remote(s_ref, sibs_ref, send_sems, recv_sems, 1, sib)
        sm.start()
        _remote(sib_ref, sib_ref, send_sems, recv_sems, 0, sib).wait()
        sm.wait()

    return _pc(
        body, name="pair_exchange", in_specs=[_hbm_spec(), _hbm_spec()], out_specs=[_hbm_spec()] * 2,
        out_shape=[jax.ShapeDtypeStruct((N_CHIPS, H, C), F32), jax.ShapeDtypeStruct(small.shape, F32)],
        scratch_shapes=[pltpu.SemaphoreType.DMA((2,)), pltpu.SemaphoreType.DMA((2,))],
    )(gpack, small)


PACK_BLOCK_ROWS = 4


def _pair_sum(mine, sib, small, sib_small):
    n, H, C = mine.shape
    rb = H // PACK_BLOCK_ROWS
    assert H % PACK_BLOCK_ROWS == 0 and rb % 16 == 0

    def body(a_ref, b_ref, s_ref, t_ref, pf_ref, pb_ref, ps_ref):
        p = a_ref[...] + b_ref[...]
        pf_ref[...] = p
        pb_ref[...] = p.astype(BF16)

        @pl.when((pl.program_id(0) == 0) & (pl.program_id(1) == 0))
        def _():
            ps_ref[...] = s_ref[...] + t_ref[...]

    blk = pl.BlockSpec((1, rb, C), lambda s, i: (s, i, 0))
    sm = pl.BlockSpec(small.shape, lambda s, i: (0, 0))
    return _pc(
        body, name="pair_sum", grid=(n, PACK_BLOCK_ROWS), in_specs=[blk, blk, sm, sm], out_specs=[blk, blk, sm],
        out_shape=[jax.ShapeDtypeStruct(mine.shape, F32), jax.ShapeDtypeStruct(mine.shape, BF16),
                   jax.ShapeDtypeStruct(small.shape, F32)],
        compiler_params=_cparams(("arbitrary", "arbitrary")),
    )(mine, sib, small, sib_small)


def _chip_exchange(pb, psmall):
    _, H, C = pb.shape
    pieces = _row_chunks(H, ICI_CHUNKS, 16)

    def body(pb_ref, ps_ref, got_ref, small4_ref, send_sems, recv_sems):
        x, y, c, chips = _place()
        own = 2 * x + y
        small = []
        for j, (cx, cy) in enumerate(chips):
            for r0, nr in pieces:
                _remote(pb_ref.at[2 * cx + cy, pl.ds(r0, nr), :], got_ref.at[j, pl.ds(r0, nr), :], send_sems, recv_sems, j,
                        (cx, cy, c)).start()
            small.append(_remote(ps_ref, small4_ref.at[own], send_sems, recv_sems, 3 + j, (cx, cy, c)))
            small[-1].start()
        for j, (cx, cy) in enumerate(chips):
            _remote(pb_ref.at[own], got_ref.at[j], send_sems, recv_sems, j, (cx, cy, c)).wait()
            _remote(ps_ref, small4_ref.at[2 * cx + cy], send_sems, recv_sems, 3 + j, (cx, cy, c)).wait_recv()
        for cp in small:
            cp.wait_send()

    return _pc(
        body, name="chip_exchange", in_specs=[_hbm_spec()] * 2, out_specs=[_hbm_spec()] * 2,
        out_shape=[jax.ShapeDtypeStruct((3, H, C), BF16), jax.ShapeDtypeStruct((N_CHIPS,) + psmall.shape, F32)],
        scratch_shapes=[pltpu.SemaphoreType.DMA((6,)), pltpu.SemaphoreType.DMA((6,))],
    )(pb, psmall)


def _chip_sum(own, got, small4):
    H, C = own.shape
    rb = H // PACK_BLOCK_ROWS

    def body(o_ref, g_ref, s_ref, tot_ref, st_ref):
        tot_ref[...] = ((o_ref[...] + g_ref[0].astype(F32)) + g_ref[1].astype(F32)) + g_ref[2].astype(F32)

        @pl.when(pl.program_id(0) == 0)
        def _():
            st_ref[...] = ((s_ref[0] + s_ref[1]) + s_ref[2]) + s_ref[3]

    return _pc(
        body, name="chip_sum", grid=(PACK_BLOCK_ROWS,),
        in_specs=[pl.BlockSpec((rb, C), lambda i: (i, 0)), pl.BlockSpec((3, rb, C), lambda i: (0, i, 0)),
                  _const_spec(small4.shape)],
        out_specs=[pl.BlockSpec((rb, C), lambda i: (i, 0)), _const_spec(small4.shape[1:])],
        out_shape=[jax.ShapeDtypeStruct((H, C), F32), jax.ShapeDtypeStruct(small4.shape[1:], F32)],
        compiler_params=_cparams(("arbitrary",)),
    )(own, got, small4)


def _half_exchange(tot):
    H, C = tot.shape
    pieces = _row_chunks(H, D2D_CHUNKS, 8)

    def body(t_ref, other_ref, send_sems, recv_sems):
        x, y, c, _ = _place()
        sib = (x, y, 1 - c)
        for r0, nr in pieces:
            _remote(t_ref.at[pl.ds(r0, nr), :], other_ref.at[pl.ds(r0, nr), :], send_sems, recv_sems, 0, sib).start()
        _remote(t_ref, other_ref, send_sems, recv_sems, 0, sib).wait()

    return _pc(
        body, name="half_exchange", in_specs=[_hbm_spec()], out_specs=_hbm_spec(),
        out_shape=jax.ShapeDtypeStruct((H, C), F32),
        scratch_shapes=[pltpu.SemaphoreType.DMA((1,)), pltpu.SemaphoreType.DMA((1,))],
    )(tot)


def _adamw(w, g, m, v, name):
    R, C = w.shape
    rb = 256 if R % 256 == 0 else (128 if R % 128 == 0 else R)

    def body(w_ref, g_ref, m_ref, v_ref, d_ref, nm_ref, nv_ref):
        gv = g_ref[...]
        m2 = ADAM_B1 * m_ref[...] + (1.0 - ADAM_B1) * gv
        v2 = ADAM_B2 * v_ref[...] + (1.0 - ADAM_B2) * (gv * gv)
        m_hat = m2 / (1.0 - ADAM_B1 ** ADAM_STEP)
        v_hat = v2 / (1.0 - ADAM_B2 ** ADAM_STEP)
        d_ref[...] = -ADAM_LR * (m_hat / (jnp.sqrt(v_hat) + ADAM_EPS) + ADAM_WD * w_ref[...])
        nm_ref[...] = m2
        nv_ref[...] = v2

    spec = pl.BlockSpec((rb, C), lambda i: (i, 0))
    return _pc(
        body, name=name, grid=(R // rb,), in_specs=[spec] * 4, out_specs=[spec] * 3,
        out_shape=[jax.ShapeDtypeStruct((R, C), F32)] * 3, compiler_params=_cparams(("parallel",)),
    )(w, g, m, v)


PACK_COLS = 1024
PACK_ROWS = 4224
BIG = (("w_in", (1024, 1412), 1), ("s5_glu_w", (128, 512), 0), ("w_branch", (384, 1024), 0), ("w_out", (256, 1024), 0),
       ("w_mlp_in", (1024, 1024), 1), ("w_mlp_out", (1024, 1024), 0), ("conv_w", (4, 512), 1))
SMALL = (("norm_mix_g", (1024,)), ("conv_b", (2048,)), ("dt_bias", (16,)), ("a_log", (16,)), ("d_ssd", (16,)),
         ("ssd_norm_g", (1024,)), ("s5_a_re", (32, 64)), ("s5_a_im", (32, 64)), ("s5_log_dt", (32,)),
         ("s5_b_re", (32, 64, 16)), ("s5_b_im", (32, 64, 16)), ("s5_c_re", (32, 16, 64)), ("s5_c_im", (32, 16, 64)),
         ("s5_d", (512,)), ("s5_glu_b", (512,)), ("norm_mlp_g", (1024,)), ("norm_final_g", (1024,)))
SMALL_ROWS = 144


PART_ALIGN = 16


def _part_rows(n):
    return -(-n // PART_ALIGN) * PART_ALIGN


def _pack_rows(parts, rows, dtype):
    flat, used = [], 0
    for a in parts:
        a = a.astype(dtype).reshape(-1, PACK_COLS)
        n = a.shape[0]
        if _part_rows(n) != n:
            a = jnp.pad(a, ((0, _part_rows(n) - n), (0, 0)))
        flat.append(a)
        used += a.shape[0]
    return jnp.concatenate(flat + [jnp.zeros((rows - used, PACK_COLS), dtype)], axis=0)


def _unpack_rows(pack, shapes):
    out, r = [], 0
    for shp in shapes:
        n = math.prod(shp) // PACK_COLS
        out.append(pack[r:r + n].reshape(shp))
        r += _part_rows(n)
    return out


def _pack_small(parts):
    flat = jnp.concatenate([a.astype(F32).reshape(-1) for a in parts])
    return jnp.concatenate([flat, jnp.zeros((SMALL_ROWS * PACK_COLS - flat.shape[0],), F32)]).reshape(SMALL_ROWS, PACK_COLS)


def _unpack_small(pack):
    flat, out, r = pack.reshape(-1), {}, 0
    for name, shp in SMALL:
        n = math.prod(shp)
        out[name] = flat[r:r + n].reshape(shp)
        r += n
    return out


def _join_shards(gathered, names_shapes):
    per_chip = [_unpack_rows(gathered[s], [shp for _, shp, _ in names_shapes]) for s in range(N_CHIPS)]
    return {name: jnp.concatenate([per_chip[s][i] for s in range(N_CHIPS)], axis=axis)
            for i, (name, _, axis) in enumerate(names_shapes)}


def _split_shards(full):
    packs = []
    for s in range(N_CHIPS):
        parts = []
        for name, shp, axis in BIG:
            n = shp[axis]
            parts.append(lax.slice_in_dim(full[name], s * n, (s + 1) * n, axis=axis))
        packs.append(_pack_rows(parts, PACK_ROWS, F32))
    return jnp.stack(packs)


def kernel(x, norm_mix_g, w_in, conv_w, conv_b, dt_bias, a_log, d_ssd, ssd_norm_g, s5_a_re, s5_a_im, s5_log_dt, s5_b_re, s5_b_im, s5_c_re, s5_c_im, s5_d, s5_glu_w, s5_glu_b, w_branch, w_out, norm_mlp_g, w_mlp_in, w_mlp_out, norm_final_g, loss_target, m_norm_mix_g, m_w_in, m_conv_w, m_conv_b, m_dt_bias, m_a_log, m_d_ssd, m_ssd_norm_g, m_s5_a_re, m_s5_a_im, m_s5_log_dt, m_s5_b_re, m_s5_b_im, m_s5_c_re, m_s5_c_im, m_s5_d, m_s5_glu_w, m_s5_glu_b, m_w_branch, m_w_out, m_norm_mlp_g, m_w_mlp_in, m_w_mlp_out, m_norm_final_g, v_norm_mix_g, v_w_in, v_conv_w, v_conv_b, v_dt_bias, v_a_log, v_d_ssd, v_ssd_norm_g, v_s5_a_re, v_s5_a_im, v_s5_log_dt, v_s5_b_re, v_s5_b_im, v_s5_c_re, v_s5_c_im, v_s5_d, v_s5_glu_w, v_s5_glu_b, v_w_branch, v_w_out, v_norm_mlp_g, v_w_mlp_in, v_w_mlp_out, v_norm_final_g):
    names = ("norm_mix_g", "w_in", "conv_w", "conv_b", "dt_bias", "a_log", "d_ssd", "ssd_norm_g", "s5_a_re", "s5_a_im",
             "s5_log_dt", "s5_b_re", "s5_b_im", "s5_c_re", "s5_c_im", "s5_d", "s5_glu_w", "s5_glu_b", "w_branch", "w_out",
             "norm_mlp_g", "w_mlp_in", "w_mlp_out", "norm_final_g")
    w = dict(zip(names, (norm_mix_g, w_in, conv_w, conv_b, dt_bias, a_log, d_ssd, ssd_norm_g, s5_a_re, s5_a_im, s5_log_dt,
                         s5_b_re, s5_b_im, s5_c_re, s5_c_im, s5_d, s5_glu_w, s5_glu_b, w_branch, w_out, norm_mlp_g,
                         w_mlp_in, w_mlp_out, norm_final_g)))
    m = dict(zip(names, (m_norm_mix_g, m_w_in, m_conv_w, m_conv_b, m_dt_bias, m_a_log, m_d_ssd, m_ssd_norm_g, m_s5_a_re,
                         m_s5_a_im, m_s5_log_dt, m_s5_b_re, m_s5_b_im, m_s5_c_re, m_s5_c_im, m_s5_d, m_s5_glu_w,
                         m_s5_glu_b, m_w_branch, m_w_out, m_norm_mlp_g, m_w_mlp_in, m_w_mlp_out, m_norm_final_g)))
    v = dict(zip(names, (v_norm_mix_g, v_w_in, v_conv_w, v_conv_b, v_dt_bias, v_a_log, v_d_ssd, v_ssd_norm_g, v_s5_a_re,
                         v_s5_a_im, v_s5_log_dt, v_s5_b_re, v_s5_b_im, v_s5_c_re, v_s5_c_im, v_s5_d, v_s5_glu_w,
                         v_s5_glu_b, v_w_branch, v_w_out, v_norm_mlp_g, v_w_mlp_in, v_w_mlp_out, v_norm_final_g)))

    mats = [(n, s, a) for n, s, a in BIG if n != "conv_w"]
    wpack = _pack_rows([w[n] for n, _, _ in mats], PACK_ROWS, BF16)
    cx, cy, cc = lax.axis_index("x"), lax.axis_index("y"), lax.axis_index("c")
    own = 2 * cx + cy
    slot = jnp.arange(N_CHIPS)[:, None, None] == own

    def with_own(gathered, mine):
        return jnp.where(slot, mine[None], gathered)

    full = _join_shards(with_own(_allgather_chips(wpack, "gather_weights", ICI_CHUNKS), wpack), mats)
    cpack = jnp.concatenate([conv_w, jnp.zeros((12, 512), F32)], axis=0)
    conv_full = with_own(_allgather_chips(cpack, "gather_conv", 1), cpack)[:, :CONV_K, :]
    p = {n: w[n] for n, _ in SMALL}
    p["conv_w"] = jnp.concatenate([conv_full[s] for s in range(N_CHIPS)], axis=1)
    p["w_in_perm"] = _permute_w_in(full["w_in"])
    for n in ("s5_glu_w", "w_branch", "w_out", "w_mlp_in", "w_mlp_out"):
        p[n] = full[n]

    loss_part, grad_x, g = _local_step(x[0], loss_target[0], p)
    loss = lax.psum(loss_part, ("x", "y", "c"))

    gpack = _split_shards(g)
    spack = _pack_small([g[n] for n, _ in SMALL])
    half_rows = PACK_ROWS // 2
    sib, sib_small = _pair_exchange(gpack, spack)
    mine = lax.dynamic_slice_in_dim(gpack, cc * half_rows, half_rows, axis=1)
    pf, pb, psmall = _pair_sum(mine, sib, spack, sib_small)
    got, small4 = _chip_exchange(pb, psmall)
    tot, small_tot = _chip_sum(lax.dynamic_index_in_dim(pf, own, 0, keepdims=False), got, with_own(small4, psmall))
    other = _half_exchange(tot)
    both = jnp.where(cc == 0, jnp.concatenate([tot, other], axis=0), jnp.concatenate([other, tot], axis=0))
    gshard = _unpack_rows(both, [shp for _, shp, _ in BIG])
    grads = _unpack_small(small_tot)
    for (n, _, _), gs in zip(BIG, gshard):
        grads[n] = gs

    delta, new_m, new_v = {}, {}, {}
    for n, _, _ in BIG:
        delta[n], new_m[n], new_v[n] = _adamw(w[n], grads[n], m[n], v[n], "adamw_" + n)
    ds, ms, vs = _adamw(_pack_small([w[n] for n, _ in SMALL]), small_tot, _pack_small([m[n] for n, _ in SMALL]),
                        _pack_small([v[n] for n, _ in SMALL]), "adamw_small")
    delta.update(_unpack_small(ds))
    new_m.update(_unpack_small(ms))
    new_v.update(_unpack_small(vs))

    return (loss, grad_x[None], *[grads[n] for n in names], *[delta[n] for n in names],
            *[new_m[n] for n in names], *[new_v[n] for n in names])
```

```python
import functools
import math

import jax
import jax.numpy as jnp
from jax import lax
from jax.experimental import pallas as pl
from jax.experimental.pallas import tpu as pltpu

F32 = jnp.float32
BF16 = jnp.bfloat16

D_MODEL = 1024
SSD_INNER = 1024
SSD_HEADS = 16
SSD_HEADDIM = 64
SSD_GROUPS = 4
SSD_HPG = 4
SSD_STATE = 128
SSD_CHUNK = 128
CONV_K = 4
CONV_DIM = 2048
S5_WIDTH = 512
S5_STATES = 2048
S5_BLOCKS = 4
S5_CHUNK = 128
D_FF = 4096
EPS = 1e-6
P_Z, P_XBC, P_U5, P_G, P_DT, P_END = 0, 1024, 3072, 3584, 5632, 5760
DT_PAD = 128
OFF_DT, OFF_U = 3072, 3088
D_IN_PROJ = 5648

ADAM_LR, ADAM_B1, ADAM_B2, ADAM_EPS, ADAM_WD, ADAM_STEP = 0.001, 0.9, 0.999, 1e-08, 0.01, 10

TOKEN_TILE = 256
VMEM_LIMIT = 56 * 1024 * 1024
HALO = 8


def _pc(body, **kw):
    return pl.pallas_call(body, **kw)


def _cparams(sem=None):
    return pltpu.CompilerParams(dimension_semantics=sem, vmem_limit_bytes=VMEM_LIMIT)


def _dot(a, b):
    return jnp.dot(a, b, preferred_element_type=F32)


def _dot_nt(a, b):
    return lax.dot_general(a, b, (((1,), (1,)), ((), ())), preferred_element_type=F32)


def _dot_tn(a, b):
    return lax.dot_general(a, b, (((0,), (0,)), ((), ())), preferred_element_type=F32)


def _dot_hi(a, b, dims=(((1,), (0,)), ((), ()))):
    return lax.dot_general(a, b, dims, preferred_element_type=F32, precision=lax.Precision.HIGHEST)


def _split_bf16(x, terms):
    out = []
    for _ in range(terms - 1):
        t = x.astype(BF16)
        out.append(t)
        x = x - t.astype(F32)
    out.append(x.astype(BF16))
    return out


def _dot_split(x, onehots, terms, dims=(((1,), (0,)), ((), ()))):
    acc = None
    for t in _split_bf16(x, terms):
        p = lax.dot_general(t, onehots, dims, preferred_element_type=F32)
        acc = p if acc is None else acc + p
    return acc


def _dot_split_rhs(onehots, x, terms, dims=(((1,), (0,)), ((), ()))):
    acc = None
    for t in _split_bf16(x, terms):
        p = lax.dot_general(onehots, t, dims, preferred_element_type=F32)
        acc = p if acc is None else acc + p
    return acc


def _sigmoid(x):
    return 1.0 / (1.0 + jnp.exp(-x))


def _softplus(x):
    return jnp.maximum(x, 0.0) + jnp.log(1.0 + jnp.exp(-jnp.abs(x)))


_GELU_C = math.sqrt(2.0 / math.pi)


def _gelu(x):
    return 0.5 * x * (1.0 + jnp.tanh(_GELU_C * (x + 0.044715 * x * x * x)))


def _gelu_grad(x):
    t = jnp.tanh(_GELU_C * (x + 0.044715 * x * x * x))
    return 0.5 * (1.0 + t) + 0.5 * x * (1.0 - t * t) * _GELU_C * (1.0 + 3.0 * 0.044715 * x * x)


def _rms(x):
    r = lax.rsqrt(jnp.mean(x * x, axis=-1, keepdims=True) + EPS)
    return x * r, r


def _rms_bwd(xn, r, dxn):
    return r * (dxn - xn * jnp.mean(dxn * xn, axis=-1, keepdims=True))


def _row_spec(tm, width, col=0):
    return pl.BlockSpec((tm, width), lambda i: (i, col))


def _const_spec(shape):
    nd = len(shape)
    return pl.BlockSpec(shape, lambda i: (0,) * nd)


def _hbm_spec():
    return pl.BlockSpec(memory_space=pl.ANY)


def _inproj_fwd(x, g, wp):
    T = x.shape[0]
    tm = TOKEN_TILE

    def body(x_ref, g_ref, w_hbm, z_ref, xbc_ref, u5_ref, gt_ref, dt_ref, w_ref):
        @pl.when(pl.program_id(0) == 0)
        def _():
            pltpu.sync_copy(w_hbm, w_ref)

        xn, _ = _rms(x_ref[...])
        h = (xn * g_ref[...]).astype(BF16)
        z_ref[...] = _dot(h, w_ref[:, P_Z:P_XBC])
        xbc_ref[...] = _dot(h, w_ref[:, P_XBC:P_U5])
        u5_ref[...] = _dot(h, w_ref[:, P_U5:P_G])
        gt_ref[...] = _dot(h, w_ref[:, P_G:P_DT])
        dt_ref[...] = _dot(h, w_ref[:, P_DT:P_END])

    widths = (1024, 2048, 512, 2048, DT_PAD)
    return _pc(
        body, name="inproj_fwd", grid=(T // tm,),
        in_specs=[_row_spec(tm, D_MODEL), _const_spec((1, D_MODEL)), _hbm_spec()],
        out_specs=[_row_spec(tm, w) for w in widths],
        out_shape=[jax.ShapeDtypeStruct((T, w), F32) for w in widths],
        scratch_shapes=[pltpu.VMEM((D_MODEL, P_END), BF16)],
        compiler_params=_cparams(("arbitrary",)),
    )(x, g, wp)


def _inproj_bwd(x, dx1, dz, dxbc, du5, dgt, ddt, g, wp):
    T = x.shape[0]
    tm = TOKEN_TILE

    def body(x_ref, dx1_ref, dz_ref, dxbc_ref, du5_ref, dgt_ref, ddt_ref, g_ref, w_hbm, dx_ref, h_ref, dg_ref, w_ref):
        @pl.when(pl.program_id(0) == 0)
        def _():
            pltpu.sync_copy(w_hbm, w_ref)
            dg_ref[...] = jnp.zeros_like(dg_ref)

        xn, r = _rms(x_ref[...])
        gv = g_ref[...]
        h_ref[...] = (xn * gv).astype(BF16)
        dh = _dot_nt(dz_ref[...].astype(BF16), w_ref[:, P_Z:P_XBC])
        dh += _dot_nt(dxbc_ref[...].astype(BF16), w_ref[:, P_XBC:P_U5])
        dh += _dot_nt(du5_ref[...].astype(BF16), w_ref[:, P_U5:P_G])
        dh += _dot_nt(dgt_ref[...].astype(BF16), w_ref[:, P_G:P_DT])
        dh += _dot_nt(ddt_ref[...].astype(BF16), w_ref[:, P_DT:P_END])
        dg_ref[...] += jnp.sum(dh * xn, axis=0, keepdims=True)
        dx_ref[...] = dx1_ref[...] + _rms_bwd(xn, r, dh * gv)

    return _pc(
        body, name="inproj_bwd", grid=(T // tm,),
        in_specs=[_row_spec(tm, 1024), _row_spec(tm, 1024), _row_spec(tm, 1024), _row_spec(tm, 2048),
                  _row_spec(tm, 512), _row_spec(tm, 2048), _row_spec(tm, DT_PAD), _const_spec((1, 1024)), _hbm_spec()],
        out_specs=[_row_spec(tm, 1024), _row_spec(tm, 1024), _const_spec((1, 1024))],
        out_shape=[jax.ShapeDtypeStruct((T, 1024), F32), jax.ShapeDtypeStruct((T, 1024), BF16),
                   jax.ShapeDtypeStruct((1, 1024), F32)],
        scratch_shapes=[pltpu.VMEM((D_MODEL, P_END), BF16)],
        compiler_params=_cparams(("arbitrary",)),
    )(x, dx1, dz, dxbc, du5, dgt, ddt, g, wp)


def _conv_fwd(xbc_raw, dt_raw, conv_w, conv_b, dt_bias):
    T = xbc_raw.shape[0]
    tm = TOKEN_TILE

    def body(u_ref, dtr_ref, w_ref, b_ref, db_ref, act_ref, dt_ref, ext_ref):
        @pl.when(pl.program_id(0) == 0)
        def _():
            ext_ref[0:HALO, :] = jnp.zeros((HALO, CONV_DIM), F32)

        ext_ref[HALO:, :] = u_ref[...]
        y = b_ref[...] + jnp.zeros((tm, CONV_DIM), F32)
        for k in range(CONV_K):
            y += w_ref[k:k + 1, :] * ext_ref[pl.ds(HALO - (CONV_K - 1) + k, tm), :]
        act_ref[...] = y * _sigmoid(y)
        ext_ref[0:HALO, :] = u_ref[tm - HALO:tm, :]
        dt_ref[...] = _softplus(dtr_ref[...] + db_ref[...])

    return _pc(
        body, name="conv_fwd", grid=(T // tm,),
        in_specs=[_row_spec(tm, CONV_DIM), _row_spec(tm, DT_PAD), _const_spec((CONV_K, CONV_DIM)),
                  _const_spec((1, CONV_DIM)), _const_spec((1, DT_PAD))],
        out_specs=[_row_spec(tm, CONV_DIM), _row_spec(tm, DT_PAD)],
        out_shape=[jax.ShapeDtypeStruct((T, CONV_DIM), F32), jax.ShapeDtypeStruct((T, DT_PAD), F32)],
        scratch_shapes=[pltpu.VMEM((tm + HALO, CONV_DIM), F32)],
        compiler_params=_cparams(("arbitrary",)),
    )(xbc_raw, dt_raw, conv_w, conv_b, dt_bias)


def _conv_bwd(xbc_raw, dt_raw, dxs_a, dxs_b, dB, dC, ddt, conv_w, conv_b, dt_bias):
    T = xbc_raw.shape[0]
    tm = TOKEN_TILE
    n = T // tm
    hb = tm // HALO

    def rev(width):
        return pl.BlockSpec((tm, width), lambda i: (n - 1 - i, 0))

    def body(u_ref, up_ref, dtr_ref, dxa_ref, dxb_ref, dB_ref, dC_ref, ddt_ref, w_ref, b_ref, db_ref,
             du_ref, ddtr_ref, dw_ref, dcb_ref, ddb_ref, ext_ref, dye_ref):
        i = pl.program_id(0)

        @pl.when(i == 0)
        def _():
            dye_ref[tm:, :] = jnp.zeros((HALO, CONV_DIM), F32)
            dw_ref[...] = jnp.zeros_like(dw_ref)
            dcb_ref[...] = jnp.zeros_like(dcb_ref)
            ddb_ref[...] = jnp.zeros_like(ddb_ref)

        first = (i == n - 1).astype(F32)
        ext_ref[0:HALO, :] = up_ref[...] * (1.0 - first)
        ext_ref[HALO:, :] = u_ref[...]
        y = b_ref[...] + jnp.zeros((tm, CONV_DIM), F32)
        for k in range(CONV_K):
            y += w_ref[k:k + 1, :] * ext_ref[pl.ds(HALO - (CONV_K - 1) + k, tm), :]
        s = _sigmoid(y)
        dsilu = s * (1.0 + y * (1.0 - s))
        dy = jnp.concatenate([dxa_ref[...] + dxb_ref[...], dB_ref[...], dC_ref[...]], axis=1) * dsilu
        dye_ref[0:tm, :] = dy
        dcb_ref[...] += jnp.sum(dy, axis=0, keepdims=True)
        du = jnp.zeros((tm, CONV_DIM), F32)
        for k in range(CONV_K):
            dw_ref[k:k + 1, :] += jnp.sum(dy * ext_ref[pl.ds(HALO - (CONV_K - 1) + k, tm), :], axis=0, keepdims=True)
            du += w_ref[k:k + 1, :] * dye_ref[pl.ds(CONV_K - 1 - k, tm), :]
        du_ref[...] = du
        dye_ref[tm:, :] = dy[0:HALO, :]
        sg = _sigmoid(dtr_ref[...] + db_ref[...])
        ddtr = ddt_ref[...] * sg
        ddtr_ref[...] = ddtr
        ddb_ref[...] += jnp.sum(ddtr, axis=0, keepdims=True)

    prev_spec = pl.BlockSpec((HALO, CONV_DIM), lambda i: (jnp.maximum((n - 1 - i) * hb - 1, 0), 0))
    return _pc(
        body, name="conv_bwd", grid=(n,),
        in_specs=[rev(CONV_DIM), prev_spec, rev(DT_PAD), rev(1024), rev(1024), rev(512), rev(512), rev(DT_PAD),
                  _const_spec((CONV_K, CONV_DIM)), _const_spec((1, CONV_DIM)), _const_spec((1, DT_PAD))],
        out_specs=[rev(CONV_DIM), rev(DT_PAD), _const_spec((HALO, CONV_DIM)), _const_spec((1, CONV_DIM)),
                   _const_spec((1, DT_PAD))],
        out_shape=[jax.ShapeDtypeStruct((T, CONV_DIM), F32), jax.ShapeDtypeStruct((T, DT_PAD), F32),
                   jax.ShapeDtypeStruct((HALO, CONV_DIM), F32), jax.ShapeDtypeStruct((1, CONV_DIM), F32),
                   jax.ShapeDtypeStruct((1, DT_PAD), F32)],
        scratch_shapes=[pltpu.VMEM((tm + HALO, CONV_DIM), F32), pltpu.VMEM((tm + HALO, CONV_DIM), F32)],
        compiler_params=_cparams(("arbitrary",)),
    )(xbc_raw, xbc_raw, dt_raw, dxs_a, dxs_b, dB, dC, ddt, conv_w, conv_b, dt_bias)


GROUP_LANES = SSD_HPG * SSD_HEADDIM


def _ssd_expanders():
    head = jnp.arange(DT_PAD)[:, None]
    to_wide = (jnp.arange(SSD_INNER)[None, :] // SSD_HEADDIM == head).astype(BF16)
    to_cols = (jnp.arange(SSD_HEADS * SSD_CHUNK)[None, :] // SSD_CHUNK == head).astype(BF16)
    return to_wide, to_wide.T, to_cols


def _ssd_prep(dt_ref, alog_ref, wide_ref, cols_ref):
    q = SSD_CHUNK
    a = -jnp.exp(alog_ref[...])
    dtv = dt_ref[...]
    la = dtv * a
    row = lax.broadcasted_iota(jnp.int32, (q, q), 0)
    col = lax.broadcasted_iota(jnp.int32, (q, q), 1)
    tri = (col <= row).astype(BF16)
    cum = _dot_split_rhs(tri, la, 3)
    cum_t = _dot_split(la, tri, 3, (((0,), (1,)), ((), ())))
    dtw = _dot_split(dtv, wide_ref[...], 2)
    cumw = _dot_split(cum, wide_ref[...], 3)
    segcol = _dot_split(cum, cols_ref[...], 3)
    return a, dtv, row, col, tri, cum_t, dtw, cumw, segcol


def _decay(segcol, cum_t, h, keep):
    return jnp.where(keep, jnp.exp(jnp.minimum(segcol[:, 128 * h:128 * h + 128] - cum_t[h:h + 1, :], 0.0)), 0.0)


def _decay_t(segcol, cum_t, h, keep_t):
    return jnp.where(keep_t, jnp.exp(jnp.minimum(cum_t[h:h + 1, :] - segcol[:, 128 * h:128 * h + 128], 0.0)), 0.0)


def _ssd_fwd(xbc_act, dt, alog):
    T = xbc_act.shape[0]
    q = SSD_CHUNK
    nc = T // q
    to_wide, _, to_cols = _ssd_expanders()

    def body(xbc_ref, dt_ref, alog_ref, wide_ref, cols_ref, y_ref, sp_ref, st_ref, xd_ref, xde_ref):
        @pl.when(pl.program_id(0) == 0)
        def _():
            st_ref[...] = jnp.zeros_like(st_ref)

        a, dtv, row, col, tri, cum_t, dtw, cumw, segcol = _ssd_prep(dt_ref, alog_ref, wide_ref, cols_ref)
        clw = cumw[q - 1:q, :]
        ecw = jnp.exp(cumw)
        xd = xbc_ref[:, 0:SSD_INNER] * dtw
        xd_ref[...] = xd.astype(BF16)
        xde_ref[...] = (xd * jnp.exp(clw - cumw)).astype(BF16)
        cdw = jnp.exp(clw)
        keep = col <= row
        sp_ref[0] = st_ref[...]
        for g in range(SSD_GROUPS):
            gl = slice(GROUP_LANES * g, GROUP_LANES * (g + 1))
            bb = xbc_ref[:, 1024 + 128 * g:1152 + 128 * g].astype(BF16)
            cb = xbc_ref[:, 1536 + 128 * g:1664 + 128 * g].astype(BF16)
            gm = _dot_nt(cb, bb)
            stp = st_ref[g]
            yoff = _dot(cb, stp.astype(BF16)) * ecw[:, gl]
            for r in range(SSD_HPG):
                h = SSD_HPG * g + r
                m = (gm * _decay(segcol, cum_t, h, keep)).astype(BF16)
                y_ref[:, 64 * h:64 * h + 64] = _dot(m, xd_ref[:, 64 * h:64 * h + 64]) + yoff[:, 64 * r:64 * r + 64]
            st_ref[g] = stp * cdw[:, gl] + _dot_tn(bb, xde_ref[:, gl])

    return _pc(
        body, name="ssd_fwd", grid=(nc,),
        in_specs=[_row_spec(q, CONV_DIM), _row_spec(q, DT_PAD), _const_spec((1, DT_PAD)),
                  _const_spec(to_wide.shape), _const_spec(to_cols.shape)],
        out_specs=[_row_spec(q, SSD_INNER),
                   pl.BlockSpec((1, SSD_GROUPS, SSD_STATE, GROUP_LANES), lambda i: (i, 0, 0, 0))],
        out_shape=[jax.ShapeDtypeStruct((T, SSD_INNER), F32),
                   jax.ShapeDtypeStruct((nc, SSD_GROUPS, SSD_STATE, GROUP_LANES), F32)],
        scratch_shapes=[pltpu.VMEM((SSD_GROUPS, SSD_STATE, GROUP_LANES), F32), pltpu.VMEM((q, SSD_INNER), BF16),
                        pltpu.VMEM((q, SSD_INNER), BF16)],
        compiler_params=_cparams(("arbitrary",)),
    )(xbc_act, dt, alog, to_wide, to_cols)


def _ssd_bwd(xbc_act, dt, alog, sprev, dy):
    T = xbc_act.shape[0]
    q = SSD_CHUNK
    nc = T // q
    to_wide, to_heads, to_cols = _ssd_expanders()

    def rev(width):
        return pl.BlockSpec((q, width), lambda i: (nc - 1 - i, 0))

    def body(xbc_ref, dt_ref, alog_ref, sp_ref, dy_ref, wide_ref, heads_ref, cols_ref,
             dxs_ref, dB_ref, dC_ref, ddt_ref, dalog_ref, ds_ref, xd_ref, dxd_ref):
        i = pl.program_id(0)

        @pl.when(i == 0)
        def _():
            ds_ref[...] = jnp.zeros_like(ds_ref)
            dalog_ref[...] = jnp.zeros_like(dalog_ref)

        a, dtv, row, col, tri, cum_t, dtw, cumw, segcol = _ssd_prep(dt_ref, alog_ref, wide_ref, cols_ref)
        clw = cumw[q - 1:q, :]
        ecw = jnp.exp(cumw)
        dew = jnp.exp(clw - cumw)
        cdw = jnp.exp(clw)
        xs = xbc_ref[:, 0:SSD_INNER]
        xd = xs * dtw
        xd_ref[...] = xd.astype(BF16)
        dyv = dy_ref[...]
        dye = (dyv * ecw).astype(BF16)
        xde = (xd * dew).astype(BF16)
        keep = col <= row
        keep_t = col >= row
        rows_k = lax.broadcasted_iota(jnp.int32, (SSD_HPG * q, DT_PAD), 0) // q
        lanes_k = lax.broadcasted_iota(jnp.int32, (SSD_HPG * q, DT_PAD), 1)
        dcw_parts = []
        dcum = jnp.zeros((q, DT_PAD), F32)
        for g in range(SSD_GROUPS):
            gl = slice(GROUP_LANES * g, GROUP_LANES * (g + 1))
            bb = xbc_ref[:, 1024 + 128 * g:1152 + 128 * g].astype(BF16)
            cb = xbc_ref[:, 1536 + 128 * g:1664 + 128 * g].astype(BF16)
            gm = _dot_nt(cb, bb)
            gmt = _dot_nt(bb, cb)
            stp = sp_ref[0, g]
            dst = ds_ref[g]
            stpb = stp.astype(BF16)
            dstb = dst.astype(BF16)
            yoff = _dot(cb, stpb) * ecw[:, gl]
            dcg = _dot_nt(dye[:, gl], stpb)
            ds_ref[g] = dst * cdw[:, gl] + _dot_tn(cb, dye[:, gl])
            dlast = jnp.sum(dst * stp, axis=0, keepdims=True) * cdw[:, gl]
            dbg = _dot_nt(xde[:, gl], dstb)
            w = _dot(bb, dstb) * dew[:, gl]
            wx = w * xd[:, gl]
            dlast = dlast + jnp.sum(wx, axis=0, keepdims=True)
            dcw_parts.append(dyv[:, gl] * yoff - wx
                             + jnp.where(lax.broadcasted_iota(jnp.int32, (q, 1), 0) == q - 1, dlast, 0.0))
            dgm = jnp.zeros((q, q), F32)
            diag = []
            for r in range(SSD_HPG):
                h = SSD_HPG * g + r
                hl = slice(64 * h, 64 * h + 64)
                dyb = dy_ref[:, hl].astype(BF16)
                xdh = xd_ref[:, hl]
                dm = _dot_nt(dyb, xdh)
                dmt = _dot_nt(xdh, dyb)
                dec = _decay(segcol, cum_t, h, keep)
                mt = gmt * _decay_t(segcol, cum_t, h, keep_t)
                dgm += dm * dec
                diag.append(dm * (gm * dec) - dmt * mt)
                dxd_ref[:, hl] = _dot(mt.astype(BF16), dyb) + w[:, 64 * r:64 * r + 64]
            onehots = (lanes_k == SSD_HPG * g + rows_k).astype(BF16)
            dcum += _dot_split(jnp.concatenate(diag, axis=1), onehots, 2)
            dgb = dgm.astype(BF16)
            dC_ref[:, 128 * g:128 * g + 128] = dcg + _dot(dgb, bb)
            dB_ref[:, 128 * g:128 * g + 128] = dbg + _dot_tn(dgb, cb)
        dxd = dxd_ref[...]
        dxs_ref[...] = dxd * dtw
        dcum += _dot_split(jnp.concatenate(dcw_parts, axis=1), heads_ref[...], 2)
        dla = _dot_split_rhs(tri, dcum, 3, (((0,), (0,)), ((), ())))
        ddt_ref[...] = _dot_split(xs * dxd, heads_ref[...], 2) + dla * a
        dalog_ref[...] += jnp.sum(dla * dtv, axis=0, keepdims=True)

        @pl.when(i == nc - 1)
        def _():
            dalog_ref[...] = dalog_ref[...] * a

    st_spec = pl.BlockSpec((1, SSD_GROUPS, SSD_STATE, GROUP_LANES), lambda i: (nc - 1 - i, 0, 0, 0))
    return _pc(
        body, name="ssd_bwd", grid=(nc,),
        in_specs=[rev(CONV_DIM), rev(DT_PAD), _const_spec((1, DT_PAD)), st_spec, rev(SSD_INNER),
                  _const_spec(to_wide.shape), _const_spec(to_heads.shape), _const_spec(to_cols.shape)],
        out_specs=[rev(SSD_INNER), rev(512), rev(512), rev(DT_PAD), _const_spec((1, DT_PAD))],
        out_shape=[jax.ShapeDtypeStruct((T, SSD_INNER), F32), jax.ShapeDtypeStruct((T, 512), F32),
                   jax.ShapeDtypeStruct((T, 512), F32), jax.ShapeDtypeStruct((T, DT_PAD), F32),
                   jax.ShapeDtypeStruct((1, DT_PAD), F32)],
        scratch_shapes=[pltpu.VMEM((SSD_GROUPS, SSD_STATE, GROUP_LANES), F32), pltpu.VMEM((q, SSD_INNER), BF16),
                        pltpu.VMEM((q, SSD_INNER), F32)],
        compiler_params=_cparams(("arbitrary",)),
    )(xbc_act, dt, alog, sprev, dy, to_wide, to_heads, to_cols)


def _s5_disc_vals(a_re, a_im, log_dt, b_re, b_im):
    dt = jnp.exp(log_dt)
    mag = jnp.exp(a_re * dt)
    ab_re = mag * jnp.cos(a_im * dt)
    ab_im = mag * jnp.sin(a_im * dt)
    den = a_re * a_re + a_im * a_im
    nr = ab_re - 1.0
    ni = ab_im
    coef_re = (nr * a_re + ni * a_im) / den
    coef_im = (ni * a_re - nr * a_im) / den
    bb_re = coef_re * b_re - coef_im * b_im
    bb_im = coef_re * b_im + coef_im * b_re
    return ab_re, ab_im, bb_re, bb_im


def _s5_disc(a_re, a_im, log_dt, b_re, b_im):
    def body(ar, ai, ld, br, bi, o1, o2, o3, o4):
        o1[...], o2[...], o3[...], o4[...] = _s5_disc_vals(ar[...], ai[...], ld[...], br[...], bi[...])

    return _pc(
        body, name="s5_disc",
        out_shape=[jax.ShapeDtypeStruct((S5_STATES, 1), F32), jax.ShapeDtypeStruct((S5_STATES, 1), F32),
                   jax.ShapeDtypeStruct((S5_STATES, 16), F32), jax.ShapeDtypeStruct((S5_STATES, 16), F32)],
    )(a_re, a_im, log_dt, b_re, b_im)


def _s5_disc_bwd(a_re, a_im, log_dt, b_re, b_im, d_ab_re, d_ab_im, d_bb_re, d_bb_im):
    def body(ar, ai, ld, br, bi, g1, g2, g3, g4, o1, o2, o3, o4, o5):
        _, vjp = jax.vjp(_s5_disc_vals, ar[...], ai[...], ld[...], br[...], bi[...])
        d1, d2, d3, d4, d5 = vjp((g1[...], g2[...], g3[...], g4[...]))
        o1[...] = d1
        o2[...] = d2
        grp = lax.broadcasted_iota(jnp.int32, (32, S5_STATES), 0)
        st = lax.broadcasted_iota(jnp.int32, (32, S5_STATES), 1)
        sel = (st // 64 == grp).astype(F32)
        o3[...] = _dot_hi(sel, d3)
        o4[...] = d4
        o5[...] = d5

    return _pc(
        body, name="s5_disc_bwd",
        out_shape=[jax.ShapeDtypeStruct((S5_STATES, 1), F32), jax.ShapeDtypeStruct((S5_STATES, 1), F32),
                   jax.ShapeDtypeStruct((32, 1), F32),
                   jax.ShapeDtypeStruct((S5_STATES, 16), F32), jax.ShapeDtypeStruct((S5_STATES, 16), F32)],
    )(a_re, a_im, log_dt, b_re, b_im, d_ab_re, d_ab_im, d_bb_re, d_bb_im)


def _cmul_add(xr, xi, pr, pi, yr, yi):
    return xr + pr * yr - pi * yi, xi + pr * yi + pi * yr


def _powers(ar, ai, n):
    out = [(ar, ai)]
    for _ in range(n - 1):
        pr, pi = out[-1]
        out.append((pr * pr - pi * pi, 2.0 * pr * pi))
    return out


_BW = S5_STATES // S5_BLOCKS
_BI = S5_WIDTH // S5_BLOCKS
SUB = 8
S5_ROWS = S5_CHUNK // SUB


def _scan8(br, bi, pws, rowin, reverse):
    k = 1
    for pr, pi in pws:
        if reverse:
            keep = rowin < SUB - k
            sr = jnp.where(keep, pltpu.roll(br, SUB - k, 0), 0.0)
            si = jnp.where(keep, pltpu.roll(bi, SUB - k, 0), 0.0)
        else:
            keep = rowin >= k
            sr = jnp.where(keep, pltpu.roll(br, k, 0), 0.0)
            si = jnp.where(keep, pltpu.roll(bi, k, 0), 0.0)
        br, bi = _cmul_add(br, bi, pr, pi, sr, si)
        k *= 2
    return br, bi


def _s5_tables(ab_ref, tab_ref, reverse):
    rowin = lax.broadcasted_iota(jnp.int32, (SUB, 1), 0)
    ar = ab_ref[0:1, :]
    ai = -ab_ref[1:2, :] if reverse else ab_ref[1:2, :]
    hit = rowin == (SUB - 1 if reverse else 0)
    zero = jnp.zeros((SUB, S5_STATES), F32)
    pr, pi = _scan8(jnp.where(hit, ar, 0.0) + zero, jnp.where(hit, ai, 0.0) + zero, _powers(ar, ai, 3), rowin, reverse)
    tab_ref[0:SUB, :] = pr
    tab_ref[SUB:2 * SUB, :] = pi


def _s5_fwd(u5, wb4, wc4, ab, dvec):
    T = u5.shape[0]
    q = S5_CHUNK
    nc = T // q

    def body(u_ref, wb_ref, wc_ref, ab_ref, d_ref, y_ref, sp_ref, carry_ref, tab_ref, sr_ref, si_ref):
        i = pl.program_id(0)
        rowin = lax.broadcasted_iota(jnp.int32, (SUB, 1), 0)

        @pl.when(i == 0)
        def _():
            carry_ref[...] = jnp.zeros_like(carry_ref)
            _s5_tables(ab_ref, tab_ref, False)

        sp_ref[0] = carry_ref[...]
        for j in range(S5_BLOCKS):
            bu = _dot(u_ref[:, _BI * j:_BI * (j + 1)].astype(BF16), wb_ref[j])
            sr_ref[:, :, _BW * j:_BW * (j + 1)] = bu[:, :_BW].reshape(S5_ROWS, SUB, _BW)
            si_ref[:, :, _BW * j:_BW * (j + 1)] = bu[:, _BW:].reshape(S5_ROWS, SUB, _BW)
        pws = _powers(ab_ref[0:1, :], ab_ref[1:2, :], 3)
        tr, ti = tab_ref[0:SUB, :], tab_ref[SUB:2 * SUB, :]
        cr, ci = carry_ref[0:1, :], carry_ref[1:2, :]
        for k in range(S5_ROWS):
            sr, si = _scan8(sr_ref[k], si_ref[k], pws, rowin, False)
            sr, si = _cmul_add(sr, si, tr, ti, cr, ci)
            sr_ref[k] = sr
            si_ref[k] = si
            cr, ci = sr[SUB - 1:SUB, :], si[SUB - 1:SUB, :]
        carry_ref[0:1, :] = cr
        carry_ref[1:2, :] = ci
        for j in range(S5_BLOCKS):
            sl = slice(_BW * j, _BW * (j + 1))
            ul = slice(_BI * j, _BI * (j + 1))
            s = jnp.concatenate([sr_ref[:, :, sl].reshape(q, _BW), si_ref[:, :, sl].reshape(q, _BW)], axis=1).astype(BF16)
            y_ref[:, ul] = _dot(s, wc_ref[j]) + d_ref[:, ul] * u_ref[:, ul]

    return _pc(
        body, name="s5_fwd", grid=(nc,),
        in_specs=[_row_spec(q, S5_WIDTH), _const_spec((S5_BLOCKS, _BI, 2 * _BW)), _const_spec((S5_BLOCKS, 2 * _BW, _BI)),
                  _const_spec((8, S5_STATES)), _const_spec((1, S5_WIDTH))],
        out_specs=[_row_spec(q, S5_WIDTH), pl.BlockSpec((1, 8, S5_STATES), lambda i: (i, 0, 0))],
        out_shape=[jax.ShapeDtypeStruct((T, S5_WIDTH), F32), jax.ShapeDtypeStruct((nc, 8, S5_STATES), F32)],
        scratch_shapes=[pltpu.VMEM((8, S5_STATES), F32), pltpu.VMEM((2 * SUB, S5_STATES), F32),
                        pltpu.VMEM((S5_ROWS, SUB, S5_STATES), F32), pltpu.VMEM((S5_ROWS, SUB, S5_STATES), F32)],
        compiler_params=_cparams(("arbitrary",)),
    )(u5, wb4, wc4, ab, dvec)


def _s5_bwd(u5, dy5, wb4, wc4, ab, dvec, sprev):
    T = u5.shape[0]
    q = S5_CHUNK
    nc = T // q

    def rev(width):
        return pl.BlockSpec((q, width), lambda i: (nc - 1 - i, 0))

    def body(u_ref, dy_ref, wb_ref, wc_ref, ab_ref, d_ref, sp_ref, du_ref, dwb_ref, dwc_ref, dab_ref, dd_ref,
             carry_ref, tab_ref, rtab_ref, sr_ref, si_ref, lr_ref, li_ref):
        i = pl.program_id(0)
        rowin = lax.broadcasted_iota(jnp.int32, (SUB, 1), 0)

        @pl.when(i == 0)
        def _():
            carry_ref[...] = jnp.zeros_like(carry_ref)
            dwb_ref[...] = jnp.zeros_like(dwb_ref)
            dwc_ref[...] = jnp.zeros_like(dwc_ref)
            dab_ref[...] = jnp.zeros_like(dab_ref)
            dd_ref[...] = jnp.zeros_like(dd_ref)
            _s5_tables(ab_ref, tab_ref, False)
            _s5_tables(ab_ref, rtab_ref, True)

        for j in range(S5_BLOCKS):
            sl = slice(_BW * j, _BW * (j + 1))
            ul = slice(_BI * j, _BI * (j + 1))
            bu = _dot(u_ref[:, ul].astype(BF16), wb_ref[j])
            sr_ref[:, :, sl] = bu[:, :_BW].reshape(S5_ROWS, SUB, _BW)
            si_ref[:, :, sl] = bu[:, _BW:].reshape(S5_ROWS, SUB, _BW)
            ds = _dot_nt(dy_ref[:, ul].astype(BF16), wc_ref[j])
            lr_ref[:, :, sl] = ds[:, :_BW].reshape(S5_ROWS, SUB, _BW)
            li_ref[:, :, sl] = ds[:, _BW:].reshape(S5_ROWS, SUB, _BW)
        ar, ai = ab_ref[0:1, :], ab_ref[1:2, :]
        pws = _powers(ar, ai, 3)
        tr, ti = tab_ref[0:SUB, :], tab_ref[SUB:2 * SUB, :]
        cr, ci = sp_ref[0, 0:1, :], sp_ref[0, 1:2, :]
        for k in range(S5_ROWS):
            sr, si = _scan8(sr_ref[k], si_ref[k], pws, rowin, False)
            sr, si = _cmul_add(sr, si, tr, ti, cr, ci)
            sr_ref[k] = sr
            si_ref[k] = si
            cr, ci = sr[SUB - 1:SUB, :], si[SUB - 1:SUB, :]
        pws = _powers(ar, -ai, 3)
        tr, ti = rtab_ref[0:SUB, :], rtab_ref[SUB:2 * SUB, :]
        cr, ci = carry_ref[0:1, :], carry_ref[1:2, :]
        acc_r = jnp.zeros((SUB, S5_STATES), F32)
        acc_i = jnp.zeros((SUB, S5_STATES), F32)
        for k in reversed(range(S5_ROWS)):
            lr, li = _scan8(lr_ref[k], li_ref[k], pws, rowin, True)
            lr, li = _cmul_add(lr, li, tr, ti, cr, ci)
            lr_ref[k] = lr
            li_ref[k] = li
            cr, ci = lr[0:1, :], li[0:1, :]
            if k > 0:
                before_r, before_i = sr_ref[k - 1, SUB - 1:SUB, :], si_ref[k - 1, SUB - 1:SUB, :]
            else:
                before_r, before_i = sp_ref[0, 0:1, :], sp_ref[0, 1:2, :]
            keep = rowin >= 1
            pr = jnp.where(keep, pltpu.roll(sr_ref[k], 1, 0), before_r)
            pi = jnp.where(keep, pltpu.roll(si_ref[k], 1, 0), before_i)
            acc_r += lr * pr + li * pi
            acc_i += li * pr - lr * pi
        carry_ref[0:1, :] = cr
        carry_ref[1:2, :] = ci
        dab_ref[0:1, :] += jnp.sum(acc_r, axis=0, keepdims=True)
        dab_ref[1:2, :] += jnp.sum(acc_i, axis=0, keepdims=True)
        for j in range(S5_BLOCKS):
            sl = slice(_BW * j, _BW * (j + 1))
            ul = slice(_BI * j, _BI * (j + 1))
            u = u_ref[:, ul]
            dy = dy_ref[:, ul]
            dyb = dy.astype(BF16)
            lam = jnp.concatenate([lr_ref[:, :, sl].reshape(q, _BW), li_ref[:, :, sl].reshape(q, _BW)], axis=1).astype(BF16)
            s = jnp.concatenate([sr_ref[:, :, sl].reshape(q, _BW), si_ref[:, :, sl].reshape(q, _BW)], axis=1).astype(BF16)
            du_ref[:, ul] = _dot_nt(lam, wb_ref[j]) + d_ref[:, ul] * dy
            dwb_ref[j] += _dot_tn(u.astype(BF16), lam)
            dwc_ref[j] += _dot_tn(s, dyb)
            dd_ref[:, ul] += jnp.sum(dy * u, axis=0, keepdims=True)

    big = pltpu.VMEM((S5_ROWS, SUB, S5_STATES), F32)
    return _pc(
        body, name="s5_bwd", grid=(nc,),
        in_specs=[rev(S5_WIDTH), rev(S5_WIDTH), _const_spec((S5_BLOCKS, _BI, 2 * _BW)), _const_spec((S5_BLOCKS, 2 * _BW, _BI)),
                  _const_spec((8, S5_STATES)), _const_spec((1, S5_WIDTH)),
                  pl.BlockSpec((1, 8, S5_STATES), lambda i: (nc - 1 - i, 0, 0))],
        out_specs=[rev(S5_WIDTH), _const_spec((S5_BLOCKS, _BI, 2 * _BW)), _const_spec((S5_BLOCKS, 2 * _BW, _BI)),
                   _const_spec((8, S5_STATES)), _const_spec((1, S5_WIDTH))],
        out_shape=[jax.ShapeDtypeStruct((T, S5_WIDTH), F32), jax.ShapeDtypeStruct((S5_BLOCKS, _BI, 2 * _BW), F32),
                   jax.ShapeDtypeStruct((S5_BLOCKS, 2 * _BW, _BI), F32), jax.ShapeDtypeStruct((8, S5_STATES), F32),
                   jax.ShapeDtypeStruct((1, S5_WIDTH), F32)],
        scratch_shapes=[pltpu.VMEM((8, S5_STATES), F32), pltpu.VMEM((2 * SUB, S5_STATES), F32),
                        pltpu.VMEM((2 * SUB, S5_STATES), F32), big, big, big, big],
        compiler_params=_cparams(("arbitrary",)),
    )(u5, dy5, wb4, wc4, ab, dvec, sprev)


def _merge_vals(ys, xs, z, y5, gates, dvec, gssd, glu_w, glu_b, wbr):
    sz = _sigmoid(z)
    qv = ys + dvec * xs
    pre = qv * (z * sz)
    yn, rs = [], []
    for gi in range(SSD_GROUPS):
        p, r = _rms(pre[:, 256 * gi:256 * (gi + 1)])
        yn.append(p)
        rs.append(r)
    yn = jnp.concatenate(yn, axis=1)
    ya = yn * gssd
    gel = _gelu(y5)
    sg = _sigmoid(_dot(gel.astype(BF16), glu_w) + glu_b)
    yb = gel * sg
    pa = _dot(ya.astype(BF16), wbr[0:SSD_INNER, :])
    pb = _dot(yb.astype(BF16), wbr[SSD_INNER:, :])
    s0 = _sigmoid(gates[:, :D_MODEL])
    s1 = _sigmoid(gates[:, D_MODEL:])
    merged = s0 * pa + s1 * pb
    return dict(sz=sz, qv=qv, yn=yn, rs=rs, ya=ya, gel=gel, sg=sg, yb=yb, pa=pa, pb=pb, s0=s0, s1=s1, merged=merged)


def _merge_specs(tm):
    acts = [_row_spec(tm, 1024), _row_spec(tm, 1024, 0), _row_spec(tm, 1024), _row_spec(tm, 512), _row_spec(tm, 2048),
            _row_spec(tm, 1024)]
    params = [_const_spec((1, 1024)), _const_spec((1, 1024)), _const_spec((512, 512)), _const_spec((1, 512)),
              _hbm_spec(), _hbm_spec()]
    return acts, params


def _merge_fwd(ys, xbc_act, z, y5, gates, x, dvec, gssd, glu_w, glu_b, wbr, wout):
    T = x.shape[0]
    tm = TOKEN_TILE
    acts, params = _merge_specs(tm)

    def body(ys_ref, xs_ref, z_ref, y5_ref, gt_ref, x_ref, dv_ref, gs_ref, gw_ref, gb_ref, wbr_hbm, wout_hbm, x1_ref,
             wbr_ref, wout_ref):
        @pl.when(pl.program_id(0) == 0)
        def _():
            pltpu.sync_copy(wbr_hbm, wbr_ref)
            pltpu.sync_copy(wout_hbm, wout_ref)

        v = _merge_vals(ys_ref[...], xs_ref[...], z_ref[...], y5_ref[...], gt_ref[...], dv_ref[...], gs_ref[...],
                        gw_ref[...], gb_ref[...], wbr_ref)
        x1_ref[...] = x_ref[...] + _dot(v["merged"].astype(BF16), wout_ref[...])

    return _pc(
        body, name="merge_fwd", grid=(T // tm,),
        in_specs=acts + params, out_specs=_row_spec(tm, 1024),
        out_shape=jax.ShapeDtypeStruct((T, 1024), F32),
        scratch_shapes=[pltpu.VMEM((1536, 1024), BF16), pltpu.VMEM((1024, 1024), BF16)],
        compiler_params=_cparams(("arbitrary",)),
    )(ys, xbc_act, z, y5, gates, x, dvec, gssd, glu_w, glu_b, wbr, wout)


def _merge_bwd(ys, xbc_act, z, y5, gates, dx1, dvec, gssd, glu_w, glu_b, wbr, wout, head_sel):
    T = dx1.shape[0]
    tm = TOKEN_TILE
    acts, params = _merge_specs(tm)

    def body(ys_ref, xs_ref, z_ref, y5_ref, gt_ref, dx1_ref, dv_ref, gs_ref, gw_ref, gb_ref, wbr_hbm, wout_hbm, hs_ref,
             dys_ref, dxs_ref, dz_ref, dy5_ref, dgt_ref, mg_ref, ya_ref, yb_ref, dpa_ref, dpb_ref, gel_ref, dpre_ref,
             ddv_ref, dgs_ref, dgb_ref, wbr_ref, wout_ref, ddacc_ref):
        i = pl.program_id(0)

        @pl.when(i == 0)
        def _():
            pltpu.sync_copy(wbr_hbm, wbr_ref)
            pltpu.sync_copy(wout_hbm, wout_ref)
            ddacc_ref[...] = jnp.zeros_like(ddacc_ref)
            dgs_ref[...] = jnp.zeros_like(dgs_ref)
            dgb_ref[...] = jnp.zeros_like(dgb_ref)

        ys, xs, z, y5, gates = ys_ref[...], xs_ref[...], z_ref[...], y5_ref[...], gt_ref[...]
        dvv, gsv, gw = dv_ref[...], gs_ref[...], gw_ref[...]
        v = _merge_vals(ys, xs, z, y5, gates, dvv, gsv, gw, gb_ref[...], wbr_ref)
        dmg = _dot_nt(dx1_ref[...].astype(BF16), wout_ref[...])
        s0, s1, pa, pb = v["s0"], v["s1"], v["pa"], v["pb"]
        dgt_ref[:, :D_MODEL] = dmg * pa * s0 * (1.0 - s0)
        dgt_ref[:, D_MODEL:] = dmg * pb * s1 * (1.0 - s1)
        dpa = (dmg * s0).astype(BF16)
        dpb = (dmg * s1).astype(BF16)
        dya = _dot_nt(dpa, wbr_ref[0:SSD_INNER, :])
        dyb = _dot_nt(dpb, wbr_ref[SSD_INNER:, :])
        gel, sg = v["gel"], v["sg"]
        dpre = (dyb * gel * sg * (1.0 - sg))
        dgb_ref[...] += jnp.sum(dpre, axis=0, keepdims=True)
        dpre_b = dpre.astype(BF16)
        dgel = dyb * sg + _dot_nt(dpre_b, gw)
        dy5_ref[...] = dgel * _gelu_grad(y5)
        yn = v["yn"]
        dgs_ref[...] += jnp.sum(dya * yn, axis=0, keepdims=True)
        dyn = dya * gsv
        dpre_a = jnp.concatenate(
            [_rms_bwd(yn[:, 256 * gi:256 * (gi + 1)], v["rs"][gi], dyn[:, 256 * gi:256 * (gi + 1)])
             for gi in range(SSD_GROUPS)], axis=1)
        sz, qv = v["sz"], v["qv"]
        dq = dpre_a * (z * sz)
        dz_ref[...] = dpre_a * qv * (sz * (1.0 + z * (1.0 - sz)))
        dys_ref[...] = dq
        dxs_ref[...] = dq * dvv
        ddacc_ref[...] += jnp.sum(dq * xs, axis=0, keepdims=True)
        mg_ref[...] = v["merged"].astype(BF16)
        ya_ref[...] = v["ya"].astype(BF16)
        yb_ref[...] = v["yb"].astype(BF16)
        dpa_ref[...] = dpa
        dpb_ref[...] = dpb
        gel_ref[...] = gel.astype(BF16)
        dpre_ref[...] = dpre_b

        @pl.when(i == pl.num_programs(0) - 1)
        def _():
            ddv_ref[...] = _dot_hi(ddacc_ref[...], hs_ref[...])

    outs = [(1024, F32), (1024, F32), (1024, F32), (512, F32), (2048, F32),
            (1024, BF16), (1024, BF16), (512, BF16), (1024, BF16), (1024, BF16), (512, BF16), (512, BF16)]
    return _pc(
        body, name="merge_bwd", grid=(T // tm,),
        in_specs=acts + params + [_const_spec((1024, DT_PAD))],
        out_specs=[_row_spec(tm, w) for w, _ in outs] + [_const_spec((1, DT_PAD)), _const_spec((1, 1024)), _const_spec((1, 512))],
        out_shape=[jax.ShapeDtypeStruct((T, w), d) for w, d in outs] + [
            jax.ShapeDtypeStruct((1, DT_PAD), F32), jax.ShapeDtypeStruct((1, 1024), F32), jax.ShapeDtypeStruct((1, 512), F32)],
        scratch_shapes=[pltpu.VMEM((1536, 1024), BF16), pltpu.VMEM((1024, 1024), BF16), pltpu.VMEM((1, 1024), F32)],
        compiler_params=_cparams(("arbitrary",)),
    )(ys, xbc_act, z, y5, gates, dx1, dvec, gssd, glu_w, glu_b, wbr, wout, head_sel)


def _mlp_fwd(x1, g, w1, w2):
    T = x1.shape[0]
    tm = TOKEN_TILE

    def body(x_ref, g_ref, w1_hbm, w2_hbm, o_ref, w1_ref, w2_ref):
        @pl.when(pl.program_id(0) == 0)
        def _():
            pltpu.sync_copy(w1_hbm, w1_ref)
            pltpu.sync_copy(w2_hbm, w2_ref)

        xv = x_ref[...]
        xn, _ = _rms(xv)
        a1 = _dot((xn * g_ref[...]).astype(BF16), w1_ref[...])
        rl = jnp.maximum(a1, 0.0)
        o_ref[...] = xv + _dot((rl * rl).astype(BF16), w2_ref[...])

    return _pc(
        body, name="mlp_fwd", grid=(T // tm,),
        in_specs=[_row_spec(tm, 1024), _const_spec((1, 1024)), _hbm_spec(), _hbm_spec()],
        out_specs=_row_spec(tm, 1024), out_shape=jax.ShapeDtypeStruct((T, 1024), F32),
        scratch_shapes=[pltpu.VMEM((D_MODEL, D_FF), BF16), pltpu.VMEM((D_FF, D_MODEL), BF16)],
        compiler_params=_cparams(("arbitrary",)),
    )(x1, g, w1, w2)


def _mlp_bwd(x1, dx2, g, w1, w2):
    T = x1.shape[0]
    tm = TOKEN_TILE

    def body(x_ref, dx2_ref, g_ref, w1_hbm, w2_hbm, dx1_ref, h_ref, act_ref, da_ref, dg_ref, w1_ref, w2_ref):
        @pl.when(pl.program_id(0) == 0)
        def _():
            pltpu.sync_copy(w1_hbm, w1_ref)
            pltpu.sync_copy(w2_hbm, w2_ref)
            dg_ref[...] = jnp.zeros_like(dg_ref)

        xn, r = _rms(x_ref[...])
        gv = g_ref[...]
        h = (xn * gv).astype(BF16)
        h_ref[...] = h
        rl = jnp.maximum(_dot(h, w1_ref[...]), 0.0)
        act_ref[...] = (rl * rl).astype(BF16)
        dx2 = dx2_ref[...]
        da = (_dot_nt(dx2.astype(BF16), w2_ref[...]) * (2.0 * rl)).astype(BF16)
        da_ref[...] = da
        dh = _dot_nt(da, w1_ref[...])
        dg_ref[...] += jnp.sum(dh * xn, axis=0, keepdims=True)
        dx1_ref[...] = dx2 + _rms_bwd(xn, r, dh * gv)

    return _pc(
        body, name="mlp_bwd", grid=(T // tm,),
        in_specs=[_row_spec(tm, 1024), _row_spec(tm, 1024), _const_spec((1, 1024)), _hbm_spec(), _hbm_spec()],
        out_specs=[_row_spec(tm, 1024), _row_spec(tm, 1024), _row_spec(tm, D_FF), _row_spec(tm, D_FF), _const_spec((1, 1024))],
        out_shape=[jax.ShapeDtypeStruct((T, 1024), F32), jax.ShapeDtypeStruct((T, 1024), BF16),
                   jax.ShapeDtypeStruct((T, D_FF), BF16), jax.ShapeDtypeStruct((T, D_FF), BF16),
                   jax.ShapeDtypeStruct((1, 1024), F32)],
        scratch_shapes=[pltpu.VMEM((D_MODEL, D_FF), BF16), pltpu.VMEM((D_FF, D_MODEL), BF16)],
        compiler_params=_cparams(("arbitrary",)),
    )(x1, dx2, g, w1, w2)


def _loss_head(x2, target, g):
    T = x2.shape[0]
    tm = TOKEN_TILE

    def body(x_ref, t_ref, g_ref, dx_ref, loss_ref, dg_ref):
        @pl.when(pl.program_id(0) == 0)
        def _():
            loss_ref[...] = jnp.zeros_like(loss_ref)
            dg_ref[...] = jnp.zeros_like(dg_ref)

        xn, r = _rms(x_ref[...])
        gv = g_ref[...]
        err = xn * gv - t_ref[...]
        loss_ref[...] += jnp.sum(err * err, axis=0, keepdims=True) * (0.5 / D_MODEL)
        dy = err * (1.0 / D_MODEL)
        dg_ref[...] += jnp.sum(dy * xn, axis=0, keepdims=True)
        dx_ref[...] = _rms_bwd(xn, r, dy * gv)

    return _pc(
        body, name="loss_head", grid=(T // tm,),
        in_specs=[_row_spec(tm, 1024), _row_spec(tm, 1024), _const_spec((1, 1024))],
        out_specs=[_row_spec(tm, 1024), _const_spec((1, 1024)), _const_spec((1, 1024))],
        out_shape=[jax.ShapeDtypeStruct((T, 1024), F32), jax.ShapeDtypeStruct((1, 1024), F32),
                   jax.ShapeDtypeStruct((1, 1024), F32)],
        compiler_params=_cparams(("arbitrary",)),
    )(x2, target, g)


WGRAD_TOKENS = 512
WGRAD_OUT_ELEMS = 2 * 1024 * 1024


def _wgrad(a, b, name):
    T, K = a.shape
    N = b.shape[1]
    tt = min(T, WGRAD_TOKENS)
    nb = min(N, max(128, WGRAD_OUT_ELEMS // K))
    assert N % nb == 0 and T % tt == 0

    def body(a_ref, b_ref, o_ref):
        @pl.when(pl.program_id(1) == 0)
        def _():
            o_ref[...] = jnp.zeros_like(o_ref)

        o_ref[...] += _dot_tn(a_ref[...].astype(BF16), b_ref[...].astype(BF16))

    return _pc(
        body, name=name, grid=(N // nb, T // tt),
        in_specs=[pl.BlockSpec((tt, K), lambda n, t: (t, 0)), pl.BlockSpec((tt, nb), lambda n, t: (t, n))],
        out_specs=pl.BlockSpec((K, nb), lambda n, t: (0, n)),
        out_shape=jax.ShapeDtypeStruct((K, N), F32),
        compiler_params=_cparams(("parallel", "arbitrary")),
    )(a, b)


def _s5_block_weights(bb_re, bb_im, c_re, c_im):
    eye = jnp.eye(8, dtype=F32)
    bre = bb_re.reshape(S5_BLOCKS, 8, 64, 16)
    bim = bb_im.reshape(S5_BLOCKS, 8, 64, 16)
    wb_re = jnp.einsum('jgpk,gh->jhkgp', bre, eye).reshape(S5_BLOCKS, _BI, _BW)
    wb_im = jnp.einsum('jgpk,gh->jhkgp', bim, eye).reshape(S5_BLOCKS, _BI, _BW)
    wb4 = jnp.concatenate([wb_re, wb_im], axis=2).astype(BF16)
    cre = c_re.reshape(S5_BLOCKS, 8, 16, 64)
    cim = c_im.reshape(S5_BLOCKS, 8, 16, 64)
    wc_re = jnp.einsum('jgkp,gh->jgphk', cre, eye).reshape(S5_BLOCKS, _BW, _BI)
    wc_im = jnp.einsum('jgkp,gh->jgphk', -cim, eye).reshape(S5_BLOCKS, _BW, _BI)
    wc4 = jnp.concatenate([wc_re, wc_im], axis=1).astype(BF16)
    return wb4, wc4


def _s5_block_grads(dwb4, dwc4):
    eye = jnp.eye(8, dtype=F32)
    dwb = dwb4.reshape(S5_BLOCKS, 8, 16, 2, 8, 64)
    dbb = jnp.einsum('jhkrgp,gh->rjgpk', dwb, eye).reshape(2, 32, 64, 16)
    dwc = dwc4.reshape(S5_BLOCKS, 2, 8, 64, 8, 16)
    dc = jnp.einsum('jrgphk,gh->rjgkp', dwc, eye).reshape(2, 32, 16, 64)
    return dbb[0], dbb[1], dc[0], -dc[1]


def _permute_w_in(w_in):
    pad = jnp.zeros((D_MODEL, DT_PAD - 16), w_in.dtype)
    return jnp.concatenate([w_in[:, :OFF_DT], w_in[:, OFF_U:], w_in[:, OFF_DT:OFF_U], pad], axis=1)


def _row(v, width=None):
    v = v.reshape(1, -1)
    if width is not None and v.shape[1] < width:
        v = jnp.concatenate([v, jnp.zeros((1, width - v.shape[1]), v.dtype)], axis=1)
    return v


def _local_step(x, target, p):
    g_mix, g_mlp, g_fin = _row(p["norm_mix_g"]), _row(p["norm_mlp_g"]), _row(p["norm_final_g"])
    conv_b = _row(p["conv_b"])
    dt_bias = _row(p["dt_bias"], DT_PAD)
    alog = _row(p["a_log"], DT_PAD)
    dvec = _row(jnp.repeat(p["d_ssd"], SSD_HEADDIM))
    gssd = _row(p["ssd_norm_g"])
    s5d = _row(p["s5_d"])
    glu_b = _row(p["s5_glu_b"])
    head_sel = (jnp.arange(SSD_INNER)[:, None] // SSD_HEADDIM == jnp.arange(DT_PAD)[None, :]).astype(F32)

    a_re = p["s5_a_re"].reshape(S5_STATES, 1)
    a_im = p["s5_a_im"].reshape(S5_STATES, 1)
    log_dt = jnp.repeat(p["s5_log_dt"], 64).reshape(S5_STATES, 1)
    b_re = p["s5_b_re"].reshape(S5_STATES, 16)
    b_im = p["s5_b_im"].reshape(S5_STATES, 16)
    ab_re, ab_im, bb_re, bb_im = _s5_disc(a_re, a_im, log_dt, b_re, b_im)
    wb4, wc4 = _s5_block_weights(bb_re, bb_im, p["s5_c_re"], p["s5_c_im"])
    ab = jnp.concatenate([ab_re.reshape(1, S5_STATES), ab_im.reshape(1, S5_STATES), jnp.zeros((6, S5_STATES), F32)], axis=0)

    wp, wbr, wout, w1, w2, glu_w = p["w_in_perm"], p["w_branch"], p["w_out"], p["w_mlp_in"], p["w_mlp_out"], p["s5_glu_w"]

    z, xbc_raw, u5, gates, dt_raw = _inproj_fwd(x, g_mix, wp)
    xbc_act, dt = _conv_fwd(xbc_raw, dt_raw, p["conv_w"], conv_b, dt_bias)
    ys, ssd_states = _ssd_fwd(xbc_act, dt, alog)
    y5, s5_states = _s5_fwd(u5, wb4, wc4, ab, s5d)
    x1 = _merge_fwd(ys, xbc_act, z, y5, gates, x, dvec, gssd, glu_w, glu_b, wbr, wout)
    x2 = _mlp_fwd(x1, g_mlp, w1, w2)
    dx2, loss_lanes, d_gfin = _loss_head(x2, target, g_fin)

    dx1, h2, act, da1, d_gmlp = _mlp_bwd(x1, dx2, g_mlp, w1, w2)
    d_w_mlp_out = _wgrad(act, dx2, "wgrad_mlp_out")
    d_w_mlp_in = _wgrad(h2, da1, "wgrad_mlp_in")
    (dys, dxs_m, dz, dy5, dgates, mg, ya, yb, dpa, dpb, gel, dpre, d_dssd, d_gssd, d_glu_b) = _merge_bwd(
        ys, xbc_act, z, y5, gates, dx1, dvec, gssd, glu_w, glu_b, wbr, wout, head_sel)
    d_w_out = _wgrad(mg, dx1, "wgrad_out")
    d_w_branch = jnp.concatenate([_wgrad(ya, dpa, "wgrad_branch_a"), _wgrad(yb, dpb, "wgrad_branch_b")], axis=0)
    d_glu_w = _wgrad(gel, dpre, "wgrad_glu")
    du5, dwb4, dwc4, dab, d_s5d = _s5_bwd(u5, dy5, wb4, wc4, ab, s5d, s5_states)
    dbb_re, dbb_im, d_c_re, d_c_im = _s5_block_grads(dwb4, dwc4)
    d_a_re, d_a_im, d_log_dt, d_b_re, d_b_im = _s5_disc_bwd(
        a_re, a_im, log_dt, b_re, b_im, dab[0].reshape(S5_STATES, 1), dab[1].reshape(S5_STATES, 1),
        dbb_re.reshape(S5_STATES, 16), dbb_im.reshape(S5_STATES, 16))
    dxs_s, dB, dC, ddt, d_alog = _ssd_bwd(xbc_act, dt, alog, ssd_states, dys)
    dxbc_raw, ddt_raw, d_conv_w, d_conv_b, d_dt_bias = _conv_bwd(
        xbc_raw, dt_raw, dxs_m, dxs_s, dB, dC, ddt, p["conv_w"], conv_b, dt_bias)
    dx, h, d_gmix = _inproj_bwd(x, dx1, dz, dxbc_raw, du5, dgates, ddt_raw, g_mix, wp)
    d_w_in = jnp.concatenate([
        _wgrad(h, dz, "wgrad_in_z"), _wgrad(h, dxbc_raw, "wgrad_in_xbc"), _wgrad(h, ddt_raw, "wgrad_in_dt")[:, :16],
        _wgrad(h, du5, "wgrad_in_u5"), _wgrad(h, dgates, "wgrad_in_gates")], axis=1)

    grads = dict(
        norm_mix_g=d_gmix.reshape(-1), w_in=d_w_in, conv_w=d_conv_w[:CONV_K], conv_b=d_conv_b.reshape(-1),
        dt_bias=d_dt_bias[0, :16], a_log=d_alog[0, :16], d_ssd=d_dssd[0, :16], ssd_norm_g=d_gssd.reshape(-1),
        s5_a_re=d_a_re.reshape(32, 64), s5_a_im=d_a_im.reshape(32, 64), s5_log_dt=d_log_dt.reshape(32),
        s5_b_re=d_b_re.reshape(32, 64, 16), s5_b_im=d_b_im.reshape(32, 64, 16), s5_c_re=d_c_re, s5_c_im=d_c_im,
        s5_d=d_s5d.reshape(-1), s5_glu_w=d_glu_w, s5_glu_b=d_glu_b.reshape(-1), w_branch=d_w_branch, w_out=d_w_out,
        norm_mlp_g=d_gmlp.reshape(-1), w_mlp_in=d_w_mlp_in, w_mlp_out=d_w_mlp_out, norm_final_g=d_gfin.reshape(-1))
    return jnp.sum(loss_lanes), dx, grads


MESH = pl.DeviceIdType.MESH
N_CHIPS = 4


def _place():
    x, y, c = lax.axis_index("x"), lax.axis_index("y"), lax.axis_index("c")
    chips = [(1 - x, y), (x, 1 - y), (1 - x, 1 - y)]
    return x, y, c, chips


def _remote(src, dst, send_sems, recv_sems, k, to):
    return pltpu.make_async_remote_copy(src_ref=src, dst_ref=dst, send_sem=send_sems.at[k], recv_sem=recv_sems.at[k],
                                        device_id=to, device_id_type=MESH)


def _row_chunks(rows, k, align):
    step = rows // k
    assert rows % k == 0 and step % align == 0, (rows, k, align)
    return [(i * step, step) for i in range(k)]


ICI_CHUNKS = 4
D2D_CHUNKS = 24


def _allgather_chips(src, name, k_ici):
    R, C = src.shape
    H = R // 2
    pieces = _row_chunks(H, k_ici, 32 // src.dtype.itemsize)
    n = 3 * k_ici

    def body(src_ref, out_ref, send_sems, recv_sems):
        x, y, c, chips = _place()
        own = 2 * x + y
        sib = (x, y, 1 - c)

        def part(s, hc, r0, nr):
            return out_ref.at[s, pl.ds(hc * H + r0, nr), :]

        first = []
        for i, (r0, nr) in enumerate(pieces):
            for j, (cx, cy) in enumerate(chips):
                first.append(_remote(src_ref.at[pl.ds(c * H + r0, nr), :], part(own, c, r0, nr), send_sems, recv_sems,
                                     j * k_ici + i, (cx, cy, c)))
        for cp in first:
            cp.start()
        passed = []
        for i, (r0, nr) in enumerate(pieces):
            for j, (cx, cy) in enumerate(chips):
                got = part(2 * cx + cy, c, r0, nr)
                _remote(got, got, send_sems, recv_sems, j * k_ici + i, (cx, cy, c)).wait_recv()
                fw = _remote(got, got, send_sems, recv_sems, n + j * k_ici + i, sib)
                fw.start()
                passed.append(fw)
        for i, (r0, nr) in enumerate(pieces):
            for j, (cx, cy) in enumerate(chips):
                got = part(2 * cx + cy, 1 - c, r0, nr)
                _remote(got, got, send_sems, recv_sems, n + j * k_ici + i, sib).wait_recv()
        for cp in first + passed:
            cp.wait_send()

    return _pc(
        body, name=name, in_specs=[_hbm_spec()], out_specs=_hbm_spec(),
        out_shape=jax.ShapeDtypeStruct((N_CHIPS, R, C), src.dtype),
        scratch_shapes=[pltpu.SemaphoreType.DMA((2 * n,)), pltpu.SemaphoreType.DMA((2 * n,))],
    )(src)


def _pair_exchange(gpack, small):
    _, R, C = gpack.shape
    H = R // 2
    pieces = _row_chunks(H, D2D_CHUNKS, 8)

    def body(g_ref, s_ref, sib_ref, sibs_ref, send_sems, recv_sems):
        x, y, c, _ = _place()
        sib = (x, y, 1 - c)
        for s in range(N_CHIPS):
            for r0, nr in pieces:
                _remote(g_ref.at[s, pl.ds((1 - c) * H + r0, nr), :], sib_ref.at[s, pl.ds(r0, nr), :], send_sems, recv_sems, 0,
                        sib).start()
        sm = _remote(s_ref, sibs_ref, send_sems, recv_sems, 1, sib)
        sm.start()
        _remote(sib_ref, sib_ref, send_sems, recv_sems, 0, sib).wait()
        sm.wait()

    return _pc(
        body, name="pair_exchange", in_specs=[_hbm_spec(), _hbm_spec()], out_specs=[_hbm_spec()] * 2,
        out_shape=[jax.ShapeDtypeStruct((N_CHIPS, H, C), F32), jax.ShapeDtypeStruct(small.shape, F32)],
        scratch_shapes=[pltpu.SemaphoreType.DMA((2,)), pltpu.SemaphoreType.DMA((2,))],
    )(gpack, small)


PACK_BLOCK_ROWS = 4


def _pair_sum(mine, sib, small, sib_small):
    n, H, C = mine.shape
    rb = H // PACK_BLOCK_ROWS
    assert H % PACK_BLOCK_ROWS == 0 and rb % 16 == 0

    def body(a_ref, b_ref, s_ref, t_ref, pf_ref, pb_ref, ps_ref):
        p = a_ref[...] + b_ref[...]
        pf_ref[...] = p
        pb_ref[...] = p.astype(BF16)

        @pl.when((pl.program_id(0) == 0) & (pl.program_id(1) == 0))
        def _():
            ps_ref[...] = s_ref[...] + t_ref[...]

    blk = pl.BlockSpec((1, rb, C), lambda s, i: (s, i, 0))
    sm = pl.BlockSpec(small.shape, lambda s, i: (0, 0))
    return _pc(
        body, name="pair_sum", grid=(n, PACK_BLOCK_ROWS), in_specs=[blk, blk, sm, sm], out_specs=[blk, blk, sm],
        out_shape=[jax.ShapeDtypeStruct(mine.shape, F32), jax.ShapeDtypeStruct(mine.shape, BF16),
                   jax.ShapeDtypeStruct(small.shape, F32)],
        compiler_params=_cparams(("arbitrary", "arbitrary")),
    )(mine, sib, small, sib_small)


def _chip_exchange(pb, psmall):
    _, H, C = pb.shape
    pieces = _row_chunks(H, ICI_CHUNKS, 16)

    def body(pb_ref, ps_ref, got_ref, small4_ref, send_sems, recv_sems):
        x, y, c, chips = _place()
        own = 2 * x + y
        small = []
        for j, (cx, cy) in enumerate(chips):
            for r0, nr in pieces:
                _remote(pb_ref.at[2 * cx + cy, pl.ds(r0, nr), :], got_ref.at[j, pl.ds(r0, nr), :], send_sems, recv_sems, j,
                        (cx, cy, c)).start()
            small.append(_remote(ps_ref, small4_ref.at[own], send_sems, recv_sems, 3 + j, (cx, cy, c)))
            small[-1].start()
        for j, (cx, cy) in enumerate(chips):
            _remote(pb_ref.at[own], got_ref.at[j], send_sems, recv_sems, j, (cx, cy, c)).wait()
            _remote(ps_ref, small4_ref.at[2 * cx + cy], send_sems, recv_sems, 3 + j, (cx, cy, c)).wait_recv()
        for cp in small:
            cp.wait_send()

    return _pc(
        body, name="chip_exchange", in_specs=[_hbm_spec()] * 2, out_specs=[_hbm_spec()] * 2,
        out_shape=[jax.ShapeDtypeStruct((3, H, C), BF16), jax.ShapeDtypeStruct((N_CHIPS,) + psmall.shape, F32)],
        scratch_shapes=[pltpu.SemaphoreType.DMA((6,)), pltpu.SemaphoreType.DMA((6,))],
    )(pb, psmall)


def _chip_sum(own, got, small4):
    H, C = own.shape
    rb = H // PACK_BLOCK_ROWS

    def body(o_ref, g_ref, s_ref, tot_ref, st_ref):
        tot_ref[...] = ((o_ref[...] + g_ref[0].astype(F32)) + g_ref[1].astype(F32)) + g_ref[2].astype(F32)

        @pl.when(pl.program_id(0) == 0)
        def _():
            st_ref[...] = ((s_ref[0] + s_ref[1]) + s_ref[2]) + s_ref[3]

    return _pc(
        body, name="chip_sum", grid=(PACK_BLOCK_ROWS,),
        in_specs=[pl.BlockSpec((rb, C), lambda i: (i, 0)), pl.BlockSpec((3, rb, C), lambda i: (0, i, 0)),
                  _const_spec(small4.shape)],
        out_specs=[pl.BlockSpec((rb, C), lambda i: (i, 0)), _const_spec(small4.shape[1:])],
        out_shape=[jax.ShapeDtypeStruct((H, C), F32), jax.ShapeDtypeStruct(small4.shape[1:], F32)],
        compiler_params=_cparams(("arbitrary",)),
    )(own, got, small4)


def _half_exchange(tot):
    H, C = tot.shape
    pieces = _row_chunks(H, D2D_CHUNKS, 8)

    def body(t_ref, other_ref, send_sems, recv_sems):
        x, y, c, _ = _place()
        sib = (x, y, 1 - c)
        for r0, nr in pieces:
            _remote(t_ref.at[pl.ds(r0, nr), :], other_ref.at[pl.ds(r0, nr), :], send_sems, recv_sems, 0, sib).start()
        _remote(t_ref, other_ref, send_sems, recv_sems, 0, sib).wait()

    return _pc(
        body, name="half_exchange", in_specs=[_hbm_spec()], out_specs=_hbm_spec(),
        out_shape=jax.ShapeDtypeStruct((H, C), F32),
        scratch_shapes=[pltpu.SemaphoreType.DMA((1,)), pltpu.SemaphoreType.DMA((1,))],
    )(tot)


def _adamw(w, g, m, v, name):
    R, C = w.shape
    rb = 256 if R % 256 == 0 else (128 if R % 128 == 0 else R)

    def body(w_ref, g_ref, m_ref, v_ref, d_ref, nm_ref, nv_ref):
        gv = g_ref[...]
        m2 = ADAM_B1 * m_ref[...] + (1.0 - ADAM_B1) * gv
        v2 = ADAM_B2 * v_ref[...] + (1.0 - ADAM_B2) * (gv * gv)
        m_hat = m2 / (1.0 - ADAM_B1 ** ADAM_STEP)
        v_hat = v2 / (1.0 - ADAM_B2 ** ADAM_STEP)
        d_ref[...] = -ADAM_LR * (m_hat / (jnp.sqrt(v_hat) + ADAM_EPS) + ADAM_WD * w_ref[...])
        nm_ref[...] = m2
        nv_ref[...] = v2

    spec = pl.BlockSpec((rb, C), lambda i: (i, 0))
    return _pc(
        body, name=name, grid=(R // rb,), in_specs=[spec] * 4, out_specs=[spec] * 3,
        out_shape=[jax.ShapeDtypeStruct((R, C), F32)] * 3, compiler_params=_cparams(("parallel",)),
    )(w, g, m, v)


PACK_COLS = 1024
PACK_ROWS = 4224
BIG = (("w_in", (1024, 1412), 1), ("s5_glu_w", (128, 512), 0), ("w_branch", (384, 1024), 0), ("w_out", (256, 1024), 0),
       ("w_mlp_in", (1024, 1024), 1), ("w_mlp_out", (1024, 1024), 0), ("conv_w", (4, 512), 1))
SMALL = (("norm_mix_g", (1024,)), ("conv_b", (2048,)), ("dt_bias", (16,)), ("a_log", (16,)), ("d_ssd", (16,)),
         ("ssd_norm_g", (1024,)), ("s5_a_re", (32, 64)), ("s5_a_im", (32, 64)), ("s5_log_dt", (32,)),
         ("s5_b_re", (32, 64, 16)), ("s5_b_im", (32, 64, 16)), ("s5_c_re", (32, 16, 64)), ("s5_c_im", (32, 16, 64)),
         ("s5_d", (512,)), ("s5_glu_b", (512,)), ("norm_mlp_g", (1024,)), ("norm_final_g", (1024,)))
SMALL_ROWS = 144


PART_ALIGN = 16


def _part_rows(n):
    return -(-n // PART_ALIGN) * PART_ALIGN


def _pack_rows(parts, rows, dtype):
    flat, used = [], 0
    for a in parts:
        a = a.astype(dtype).reshape(-1, PACK_COLS)
        n = a.shape[0]
        if _part_rows(n) != n:
            a = jnp.pad(a, ((0, _part_rows(n) - n), (0, 0)))
        flat.append(a)
        used += a.shape[0]
    return jnp.concatenate(flat + [jnp.zeros((rows - used, PACK_COLS), dtype)], axis=0)


def _unpack_rows(pack, shapes):
    out, r = [], 0
    for shp in shapes:
        n = math.prod(shp) // PACK_COLS
        out.append(pack[r:r + n].reshape(shp))
        r += _part_rows(n)
    return out


def _pack_small(parts):
    flat = jnp.concatenate([a.astype(F32).reshape(-1) for a in parts])
    return jnp.concatenate([flat, jnp.zeros((SMALL_ROWS * PACK_COLS - flat.shape[0],), F32)]).reshape(SMALL_ROWS, PACK_COLS)


def _unpack_small(pack):
    flat, out, r = pack.reshape(-1), {}, 0
    for name, shp in SMALL:
        n = math.prod(shp)
        out[name] = flat[r:r + n].reshape(shp)
        r += n
    return out


def _join_shards(gathered, names_shapes):
    per_chip = [_unpack_rows(gathered[s], [shp for _, shp, _ in names_shapes]) for s in range(N_CHIPS)]
    return {name: jnp.concatenate([per_chip[s][i] for s in range(N_CHIPS)], axis=axis)
            for i, (name, _, axis) in enumerate(names_shapes)}


def _split_shards(full):
    packs = []
    for s in range(N_CHIPS):
        parts = []
        for name, shp, axis in BIG:
            n = shp[axis]
            parts.append(lax.slice_in_dim(full[name], s * n, (s + 1) * n, axis=axis))
        packs.append(_pack_rows(parts, PACK_ROWS, F32))
    return jnp.stack(packs)


def kernel(x, norm_mix_g, w_in, conv_w, conv_b, dt_bias, a_log, d_ssd, ssd_norm_g, s5_a_re, s5_a_im, s5_log_dt, s5_b_re, s5_b_im, s5_c_re, s5_c_im, s5_d, s5_glu_w, s5_glu_b, w_branch, w_out, norm_mlp_g, w_mlp_in, w_mlp_out, norm_final_g, loss_target, m_norm_mix_g, m_w_in, m_conv_w, m_conv_b, m_dt_bias, m_a_log, m_d_ssd, m_ssd_norm_g, m_s5_a_re, m_s5_a_im, m_s5_log_dt, m_s5_b_re, m_s5_b_im, m_s5_c_re, m_s5_c_im, m_s5_d, m_s5_glu_w, m_s5_glu_b, m_w_branch, m_w_out, m_norm_mlp_g, m_w_mlp_in, m_w_mlp_out, m_norm_final_g, v_norm_mix_g, v_w_in, v_conv_w, v_conv_b, v_dt_bias, v_a_log, v_d_ssd, v_ssd_norm_g, v_s5_a_re, v_s5_a_im, v_s5_log_dt, v_s5_b_re, v_s5_b_im, v_s5_c_re, v_s5_c_im, v_s5_d, v_s5_glu_w, v_s5_glu_b, v_w_branch, v_w_out, v_norm_mlp_g, v_w_mlp_in, v_w_mlp_out, v_norm_final_g):
    names = ("norm_mix_g", "w_in", "conv_w", "conv_b", "dt_bias", "a_log", "d_ssd", "ssd_norm_g", "s5_a_re", "s5_a_im",
             "s5_log_dt", "s5_b_re", "s5_b_im", "s5_c_re", "s5_c_im", "s5_d", "s5_glu_w", "s5_glu_b", "w_branch", "w_out",
             "norm_mlp_g", "w_mlp_in", "w_mlp_out", "norm_final_g")
    w = dict(zip(names, (norm_mix_g, w_in, conv_w, conv_b, dt_bias, a_log, d_ssd, ssd_norm_g, s5_a_re, s5_a_im, s5_log_dt,
                         s5_b_re, s5_b_im, s5_c_re, s5_c_im, s5_d, s5_glu_w, s5_glu_b, w_branch, w_out, norm_mlp_g,
                         w_mlp_in, w_mlp_out, norm_final_g)))
    m = dict(zip(names, (m_norm_mix_g, m_w_in, m_conv_w, m_conv_b, m_dt_bias, m_a_log, m_d_ssd, m_ssd_norm_g, m_s5_a_re,
                         m_s5_a_im, m_s5_log_dt, m_s5_b_re, m_s5_b_im, m_s5_c_re, m_s5_c_im, m_s5_d, m_s5_glu_w,
                         m_s5_glu_b, m_w_branch, m_w_out, m_norm_mlp_g, m_w_mlp_in, m_w_mlp_out, m_norm_final_g)))
    v = dict(zip(names, (v_norm_mix_g, v_w_in, v_conv_w, v_conv_b, v_dt_bias, v_a_log, v_d_ssd, v_ssd_norm_g, v_s5_a_re,
                         v_s5_a_im, v_s5_log_dt, v_s5_b_re, v_s5_b_im, v_s5_c_re, v_s5_c_im, v_s5_d, v_s5_glu_w,
                         v_s5_glu_b, v_w_branch, v_w_out, v_norm_mlp_g, v_w_mlp_in, v_w_mlp_out, v_norm_final_g)))

    mats = [(n, s, a) for n, s, a in BIG if n != "conv_w"]
    wpack = _pack_rows([w[n] for n, _, _ in mats], PACK_ROWS, BF16)
    cx, cy, cc = lax.axis_index("x"), lax.axis_index("y"), lax.axis_index("c")
    own = 2 * cx + cy
    slot = jnp.arange(N_CHIPS)[:, None, None] == own

    def with_own(gathered, mine):
        return jnp.where(slot, mine[None], gathered)

    full = _join_shards(with_own(_allgather_chips(wpack, "gather_weights", ICI_CHUNKS), wpack), mats)
    cpack = jnp.concatenate([conv_w, jnp.zeros((12, 512), F32)], axis=0)
    conv_full = with_own(_allgather_chips(cpack, "gather_conv", 1), cpack)[:, :CONV_K, :]
    p = {n: w[n] for n, _ in SMALL}
    p["conv_w"] = jnp.concatenate([conv_full[s] for s in range(N_CHIPS)], axis=1)
    p["w_in_perm"] = _permute_w_in(full["w_in"])
    for n in ("s5_glu_w", "w_branch", "w_out", "w_mlp_in", "w_mlp_out"):
        p[n] = full[n]

    loss_part, grad_x, g = _local_step(x[0], loss_target[0], p)
    loss = lax.psum(loss_part, ("x", "y", "c"))

    gpack = _split_shards(g)
    spack = _pack_small([g[n] for n, _ in SMALL])
    half_rows = PACK_ROWS // 2
    sib, sib_small = _pair_exchange(gpack, spack)
    mine = lax.dynamic_slice_in_dim(gpack, cc * half_rows, half_rows, axis=1)
    pf, pb, psmall = _pair_sum(mine, sib, spack, sib_small)
    got, small4 = _chip_exchange(pb, psmall)
    tot, small_tot = _chip_sum(lax.dynamic_index_in_dim(pf, own, 0, keepdims=False), got, with_own(small4, psmall))
    other = _half_exchange(tot)
    both = jnp.where(cc == 0, jnp.concatenate([tot, other], axis=0), jnp.concatenate([other, tot], axis=0))
    gshard = _unpack_rows(both, [shp for _, shp, _ in BIG])
    grads = _unpack_small(small_tot)
    for (n, _, _), gs in zip(BIG, gshard):
        grads[n] = gs

    delta, new_m, new_v = {}, {}, {}
    for n, _, _ in BIG:
        delta[n], new_m[n], new_v[n] = _adamw(w[n], grads[n], m[n], v[n], "adamw_" + n)
    ds, ms, vs = _adamw(_pack_small([w[n] for n, _ in SMALL]), small_tot, _pack_small([m[n] for n, _ in SMALL]),
                        _pack_small([v[n] for n, _ in SMALL]), "adamw_small")
    delta.update(_unpack_small(ds))
    new_m.update(_unpack_small(ms))
    new_v.update(_unpack_small(vs))

    return (loss, grad_x[None], *[grads[n] for n in names], *[delta[n] for n in names],
            *[new_m[n] for n in names], *[new_v[n] for n in names])
```

```python
import functools
import math

import jax
import jax.numpy as jnp
from jax import lax
from jax.experimental import pallas as pl
from jax.experimental.pallas import tpu as pltpu

F32 = jnp.float32
BF16 = jnp.bfloat16

D_MODEL = 1024
SSD_INNER = 1024
SSD_HEADS = 16
SSD_HEADDIM = 64
SSD_GROUPS = 4
SSD_HPG = 4
SSD_STATE = 128
SSD_CHUNK = 128
CONV_K = 4
CONV_DIM = 2048
S5_WIDTH = 512
S5_STATES = 2048
S5_BLOCKS = 4
S5_CHUNK = 128
D_FF = 4096
FF_SHARDS = 4
FF_SHARD = D_FF // FF_SHARDS
EPS = 1e-6
P_Z, P_XBC, P_U5, P_G, P_DT, P_END = 0, 1024, 3072, 3584, 5632, 5760
DT_PAD = 128
OFF_DT, OFF_U = 3072, 3088
D_IN_PROJ = 5648

ADAM_LR, ADAM_B1, ADAM_B2, ADAM_EPS, ADAM_WD, ADAM_STEP = 0.001, 0.9, 0.999, 1e-08, 0.01, 10

TOKEN_TILE = 256
VMEM_LIMIT = 56 * 1024 * 1024
HALO = 8


def _pc(body, **kw):
    return pl.pallas_call(body, **kw)


def _cparams(sem=None):
    return pltpu.CompilerParams(dimension_semantics=sem, vmem_limit_bytes=VMEM_LIMIT)


def _dot(a, b):
    return jnp.dot(a, b, preferred_element_type=F32)


def _dot_nt(a, b):
    return lax.dot_general(a, b, (((1,), (1,)), ((), ())), preferred_element_type=F32)


def _dot_tn(a, b):
    return lax.dot_general(a, b, (((0,), (0,)), ((), ())), preferred_element_type=F32)


def _dot_hi(a, b, dims=(((1,), (0,)), ((), ()))):
    return lax.dot_general(a, b, dims, preferred_element_type=F32, precision=lax.Precision.HIGHEST)


def _split_bf16(x, terms):
    out = []
    for _ in range(terms - 1):
        t = x.astype(BF16)
        out.append(t)
        x = x - t.astype(F32)
    out.append(x.astype(BF16))
    return out


def _dot_split(x, onehots, terms, dims=(((1,), (0,)), ((), ()))):
    acc = None
    for t in _split_bf16(x, terms):
        p = lax.dot_general(t, onehots, dims, preferred_element_type=F32)
        acc = p if acc is None else acc + p
    return acc


def _dot_split_rhs(onehots, x, terms, dims=(((1,), (0,)), ((), ()))):
    acc = None
    for t in _split_bf16(x, terms):
        p = lax.dot_general(onehots, t, dims, preferred_element_type=F32)
        acc = p if acc is None else acc + p
    return acc


def _sigmoid(x):
    return 1.0 / (1.0 + jnp.exp(-x))


def _softplus(x):
    return jnp.maximum(x, 0.0) + jnp.log(1.0 + jnp.exp(-jnp.abs(x)))


_GELU_C = math.sqrt(2.0 / math.pi)


def _gelu(x):
    return 0.5 * x * (1.0 + jnp.tanh(_GELU_C * (x + 0.044715 * x * x * x)))


def _gelu_grad(x):
    t = jnp.tanh(_GELU_C * (x + 0.044715 * x * x * x))
    return 0.5 * (1.0 + t) + 0.5 * x * (1.0 - t * t) * _GELU_C * (1.0 + 3.0 * 0.044715 * x * x)


def _rms(x):
    r = lax.rsqrt(jnp.mean(x * x, axis=-1, keepdims=True) + EPS)
    return x * r, r


def _rms_bwd(xn, r, dxn):
    return r * (dxn - xn * jnp.mean(dxn * xn, axis=-1, keepdims=True))


def _row_spec(tm, width, col=0):
    return pl.BlockSpec((tm, width), lambda i: (i, col))


def _const_spec(shape):
    nd = len(shape)
    return pl.BlockSpec(shape, lambda i: (0,) * nd)


def _hbm_spec():
    return pl.BlockSpec(memory_space=pl.ANY)


def _inproj_fwd(x, g, wp):
    T = x.shape[0]
    tm = TOKEN_TILE

    def body(x_ref, g_ref, w_hbm, z_ref, xbc_ref, u5_ref, gt_ref, dt_ref, w_ref):
        @pl.when(pl.program_id(0) == 0)
        def _():
            pltpu.sync_copy(w_hbm, w_ref)

        xn, _ = _rms(x_ref[...])
        h = (xn * g_ref[...]).astype(BF16)
        z_ref[...] = _dot(h, w_ref[:, P_Z:P_XBC])
        xbc_ref[...] = _dot(h, w_ref[:, P_XBC:P_U5])
        u5_ref[...] = _dot(h, w_ref[:, P_U5:P_G])
        gt_ref[...] = _dot(h, w_ref[:, P_G:P_DT])
        dt_ref[...] = _dot(h, w_ref[:, P_DT:P_END])

    widths = (1024, 2048, 512, 2048, DT_PAD)
    return _pc(
        body, name="inproj_fwd", grid=(T // tm,),
        in_specs=[_row_spec(tm, D_MODEL), _const_spec((1, D_MODEL)), _hbm_spec()],
        out_specs=[_row_spec(tm, w) for w in widths],
        out_shape=[jax.ShapeDtypeStruct((T, w), F32) for w in widths],
        scratch_shapes=[pltpu.VMEM((D_MODEL, P_END), BF16)],
        compiler_params=_cparams(("arbitrary",)),
    )(x, g, wp)


def _inproj_bwd(x, dx1, dz, dxbc, du5, dgt, ddt, g, wp):
    T = x.shape[0]
    tm = TOKEN_TILE

    def body(x_ref, dx1_ref, dz_ref, dxbc_ref, du5_ref, dgt_ref, ddt_ref, g_ref, w_hbm, dx_ref, h_ref, dg_ref, w_ref):
        @pl.when(pl.program_id(0) == 0)
        def _():
            pltpu.sync_copy(w_hbm, w_ref)
            dg_ref[...] = jnp.zeros_like(dg_ref)

        xn, r = _rms(x_ref[...])
        gv = g_ref[...]
        h_ref[...] = (xn * gv).astype(BF16)
        dh = _dot_nt(dz_ref[...].astype(BF16), w_ref[:, P_Z:P_XBC])
        dh += _dot_nt(dxbc_ref[...].astype(BF16), w_ref[:, P_XBC:P_U5])
        dh += _dot_nt(du5_ref[...].astype(BF16), w_ref[:, P_U5:P_G])
        dh += _dot_nt(dgt_ref[...].astype(BF16), w_ref[:, P_G:P_DT])
        dh += _dot_nt(ddt_ref[...].astype(BF16), w_ref[:, P_DT:P_END])
        dg_ref[...] += jnp.sum(dh * xn, axis=0, keepdims=True)
        dx_ref[...] = dx1_ref[...] + _rms_bwd(xn, r, dh * gv)

    return _pc(
        body, name="inproj_bwd", grid=(T // tm,),
        in_specs=[_row_spec(tm, 1024), _row_spec(tm, 1024), _row_spec(tm, 1024), _row_spec(tm, 2048),
                  _row_spec(tm, 512), _row_spec(tm, 2048), _row_spec(tm, DT_PAD), _const_spec((1, 1024)), _hbm_spec()],
        out_specs=[_row_spec(tm, 1024), _row_spec(tm, 1024), _const_spec((1, 1024))],
        out_shape=[jax.ShapeDtypeStruct((T, 1024), F32), jax.ShapeDtypeStruct((T, 1024), BF16),
                   jax.ShapeDtypeStruct((1, 1024), F32)],
        scratch_shapes=[pltpu.VMEM((D_MODEL, P_END), BF16)],
        compiler_params=_cparams(("arbitrary",)),
    )(x, dx1, dz, dxbc, du5, dgt, ddt, g, wp)


def _conv_fwd(xbc_raw, dt_raw, conv_w, conv_b, dt_bias):
    T = xbc_raw.shape[0]
    tm = TOKEN_TILE

    def body(u_ref, dtr_ref, w_ref, b_ref, db_ref, act_ref, dt_ref, ext_ref):
        @pl.when(pl.program_id(0) == 0)
        def _():
            ext_ref[0:HALO, :] = jnp.zeros((HALO, CONV_DIM), F32)

        ext_ref[HALO:, :] = u_ref[...]
        y = b_ref[...] + jnp.zeros((tm, CONV_DIM), F32)
        for k in range(CONV_K):
            y += w_ref[k:k + 1, :] * ext_ref[pl.ds(HALO - (CONV_K - 1) + k, tm), :]
        act_ref[...] = y * _sigmoid(y)
        ext_ref[0:HALO, :] = u_ref[tm - HALO:tm, :]
        dt_ref[...] = _softplus(dtr_ref[...] + db_ref[...])

    return _pc(
        body, name="conv_fwd", grid=(T // tm,),
        in_specs=[_row_spec(tm, CONV_DIM), _row_spec(tm, DT_PAD), _const_spec((CONV_K, CONV_DIM)),
                  _const_spec((1, CONV_DIM)), _const_spec((1, DT_PAD))],
        out_specs=[_row_spec(tm, CONV_DIM), _row_spec(tm, DT_PAD)],
        out_shape=[jax.ShapeDtypeStruct((T, CONV_DIM), F32), jax.ShapeDtypeStruct((T, DT_PAD), F32)],
        scratch_shapes=[pltpu.VMEM((tm + HALO, CONV_DIM), F32)],
        compiler_params=_cparams(("arbitrary",)),
    )(xbc_raw, dt_raw, conv_w, conv_b, dt_bias)


def _conv_bwd(xbc_raw, dt_raw, dxs_a, dxs_b, dB, dC, ddt, conv_w, conv_b, dt_bias):
    T = xbc_raw.shape[0]
    tm = TOKEN_TILE
    n = T // tm
    hb = tm // HALO

    def rev(width):
        return pl.BlockSpec((tm, width), lambda i: (n - 1 - i, 0))

    def body(u_ref, up_ref, dtr_ref, dxa_ref, dxb_ref, dB_ref, dC_ref, ddt_ref, w_ref, b_ref, db_ref,
             du_ref, ddtr_ref, dw_ref, dcb_ref, ddb_ref, ext_ref, dye_ref):
        i = pl.program_id(0)

        @pl.when(i == 0)
        def _():
            dye_ref[tm:, :] = jnp.zeros((HALO, CONV_DIM), F32)
            dw_ref[...] = jnp.zeros_like(dw_ref)
            dcb_ref[...] = jnp.zeros_like(dcb_ref)
            ddb_ref[...] = jnp.zeros_like(ddb_ref)

        first = (i == n - 1).astype(F32)
        ext_ref[0:HALO, :] = up_ref[...] * (1.0 - first)
        ext_ref[HALO:, :] = u_ref[...]
        y = b_ref[...] + jnp.zeros((tm, CONV_DIM), F32)
        for k in range(CONV_K):
            y += w_ref[k:k + 1, :] * ext_ref[pl.ds(HALO - (CONV_K - 1) + k, tm), :]
        s = _sigmoid(y)
        dsilu = s * (1.0 + y * (1.0 - s))
        dy = jnp.concatenate([dxa_ref[...] + dxb_ref[...], dB_ref[...], dC_ref[...]], axis=1) * dsilu
        dye_ref[0:tm, :] = dy
        dcb_ref[...] += jnp.sum(dy, axis=0, keepdims=True)
        du = jnp.zeros((tm, CONV_DIM), F32)
        for k in range(CONV_K):
            dw_ref[k:k + 1, :] += jnp.sum(dy * ext_ref[pl.ds(HALO - (CONV_K - 1) + k, tm), :], axis=0, keepdims=True)
            du += w_ref[k:k + 1, :] * dye_ref[pl.ds(CONV_K - 1 - k, tm), :]
        du_ref[...] = du
        dye_ref[tm:, :] = dy[0:HALO, :]
        sg = _sigmoid(dtr_ref[...] + db_ref[...])
        ddtr = ddt_ref[...] * sg
        ddtr_ref[...] = ddtr
        ddb_ref[...] += jnp.sum(ddtr, axis=0, keepdims=True)

    prev_spec = pl.BlockSpec((HALO, CONV_DIM), lambda i: (jnp.maximum((n - 1 - i) * hb - 1, 0), 0))
    return _pc(
        body, name="conv_bwd", grid=(n,),
        in_specs=[rev(CONV_DIM), prev_spec, rev(DT_PAD), rev(1024), rev(1024), rev(512), rev(512), rev(DT_PAD),
                  _const_spec((CONV_K, CONV_DIM)), _const_spec((1, CONV_DIM)), _const_spec((1, DT_PAD))],
        out_specs=[rev(CONV_DIM), rev(DT_PAD), _const_spec((HALO, CONV_DIM)), _const_spec((1, CONV_DIM)),
                   _const_spec((1, DT_PAD))],
        out_shape=[jax.ShapeDtypeStruct((T, CONV_DIM), F32), jax.ShapeDtypeStruct((T, DT_PAD), F32),
                   jax.ShapeDtypeStruct((HALO, CONV_DIM), F32), jax.ShapeDtypeStruct((1, CONV_DIM), F32),
                   jax.ShapeDtypeStruct((1, DT_PAD), F32)],
        scratch_shapes=[pltpu.VMEM((tm + HALO, CONV_DIM), F32), pltpu.VMEM((tm + HALO, CONV_DIM), F32)],
        compiler_params=_cparams(("arbitrary",)),
    )(xbc_raw, xbc_raw, dt_raw, dxs_a, dxs_b, dB, dC, ddt, conv_w, conv_b, dt_bias)


GROUP_LANES = SSD_HPG * SSD_HEADDIM


def _ssd_expanders():
    head = jnp.arange(DT_PAD)[:, None]
    to_wide = (jnp.arange(SSD_INNER)[None, :] // SSD_HEADDIM == head).astype(BF16)
    to_cols = (jnp.arange(SSD_HEADS * SSD_CHUNK)[None, :] // SSD_CHUNK == head).astype(BF16)
    return to_wide, to_wide.T, to_cols


def _ssd_prep(dt_ref, alog_ref, wide_ref, cols_ref):
    q = SSD_CHUNK
    a = -jnp.exp(alog_ref[...])
    dtv = dt_ref[...]
    la = dtv * a
    row = lax.broadcasted_iota(jnp.int32, (q, q), 0)
    col = lax.broadcasted_iota(jnp.int32, (q, q), 1)
    tri = (col <= row).astype(BF16)
    cum = _dot_split_rhs(tri, la, 3)
    cum_t = _dot_split(la, tri, 3, (((0,), (1,)), ((), ())))
    dtw = _dot_split(dtv, wide_ref[...], 2)
    cumw = _dot_split(cum, wide_ref[...], 3)
    segcol = _dot_split(cum, cols_ref[...], 3)
    return a, dtv, row, col, tri, cum_t, dtw, cumw, segcol


def _decay(segcol, cum_t, h, keep):
    return jnp.where(keep, jnp.exp(jnp.minimum(segcol[:, 128 * h:128 * h + 128] - cum_t[h:h + 1, :], 0.0)), 0.0)


def _decay_t(segcol, cum_t, h, keep_t):
    return jnp.where(keep_t, jnp.exp(jnp.minimum(cum_t[h:h + 1, :] - segcol[:, 128 * h:128 * h + 128], 0.0)), 0.0)


def _ssd_fwd(xbc_act, dt, alog):
    T = xbc_act.shape[0]
    q = SSD_CHUNK
    nc = T // q
    to_wide, _, to_cols = _ssd_expanders()

    def body(xbc_ref, dt_ref, alog_ref, wide_ref, cols_ref, y_ref, sp_ref, st_ref, xd_ref, xde_ref):
        @pl.when(pl.program_id(0) == 0)
        def _():
            st_ref[...] = jnp.zeros_like(st_ref)

        a, dtv, row, col, tri, cum_t, dtw, cumw, segcol = _ssd_prep(dt_ref, alog_ref, wide_ref, cols_ref)
        clw = cumw[q - 1:q, :]
        ecw = jnp.exp(cumw)
        xd = xbc_ref[:, 0:SSD_INNER] * dtw
        xd_ref[...] = xd.astype(BF16)
        xde_ref[...] = (xd * jnp.exp(clw - cumw)).astype(BF16)
        cdw = jnp.exp(clw)
        keep = col <= row
        sp_ref[0] = st_ref[...]
        for g in range(SSD_GROUPS):
            gl = slice(GROUP_LANES * g, GROUP_LANES * (g + 1))
            bb = xbc_ref[:, 1024 + 128 * g:1152 + 128 * g].astype(BF16)
            cb = xbc_ref[:, 1536 + 128 * g:1664 + 128 * g].astype(BF16)
            gm = _dot_nt(cb, bb)
            stp = st_ref[g]
            yoff = _dot(cb, stp.astype(BF16)) * ecw[:, gl]
            for r in range(SSD_HPG):
                h = SSD_HPG * g + r
                m = (gm * _decay(segcol, cum_t, h, keep)).astype(BF16)
                y_ref[:, 64 * h:64 * h + 64] = _dot(m, xd_ref[:, 64 * h:64 * h + 64]) + yoff[:, 64 * r:64 * r + 64]
            st_ref[g] = stp * cdw[:, gl] + _dot_tn(bb, xde_ref[:, gl])

    return _pc(
        body, name="ssd_fwd", grid=(nc,),
        in_specs=[_row_spec(q, CONV_DIM), _row_spec(q, DT_PAD), _const_spec((1, DT_PAD)),
                  _const_spec(to_wide.shape), _const_spec(to_cols.shape)],
        out_specs=[_row_spec(q, SSD_INNER),
                   pl.BlockSpec((1, SSD_GROUPS, SSD_STATE, GROUP_LANES), lambda i: (i, 0, 0, 0))],
        out_shape=[jax.ShapeDtypeStruct((T, SSD_INNER), F32),
                   jax.ShapeDtypeStruct((nc, SSD_GROUPS, SSD_STATE, GROUP_LANES), F32)],
        scratch_shapes=[pltpu.VMEM((SSD_GROUPS, SSD_STATE, GROUP_LANES), F32), pltpu.VMEM((q, SSD_INNER), BF16),
                        pltpu.VMEM((q, SSD_INNER), BF16)],
        compiler_params=_cparams(("arbitrary",)),
    )(xbc_act, dt, alog, to_wide, to_cols)


def _ssd_bwd(xbc_act, dt, alog, sprev, dy):
    T = xbc_act.shape[0]
    q = SSD_CHUNK
    nc = T // q
    to_wide, to_heads, to_cols = _ssd_expanders()

    def rev(width):
        return pl.BlockSpec((q, width), lambda i: (nc - 1 - i, 0))

    def body(xbc_ref, dt_ref, alog_ref, sp_ref, dy_ref, wide_ref, heads_ref, cols_ref,
             dxs_ref, dB_ref, dC_ref, ddt_ref, dalog_ref, ds_ref, xd_ref, dxd_ref):
        i = pl.program_id(0)

        @pl.when(i == 0)
        def _():
            ds_ref[...] = jnp.zeros_like(ds_ref)
            dalog_ref[...] = jnp.zeros_like(dalog_ref)

        a, dtv, row, col, tri, cum_t, dtw, cumw, segcol = _ssd_prep(dt_ref, alog_ref, wide_ref, cols_ref)
        clw = cumw[q - 1:q, :]
        ecw = jnp.exp(cumw)
        dew = jnp.exp(clw - cumw)
        cdw = jnp.exp(clw)
        xs = xbc_ref[:, 0:SSD_INNER]
        xd = xs * dtw
        xd_ref[...] = xd.astype(BF16)
        dyv = dy_ref[...]
        dye = (dyv * ecw).astype(BF16)
        xde = (xd * dew).astype(BF16)
        keep = col <= row
        keep_t = col >= row
        rows_k = lax.broadcasted_iota(jnp.int32, (SSD_HPG * q, DT_PAD), 0) // q
        lanes_k = lax.broadcasted_iota(jnp.int32, (SSD_HPG * q, DT_PAD), 1)
        dcw_parts = []
        dcum = jnp.zeros((q, DT_PAD), F32)
        for g in range(SSD_GROUPS):
            gl = slice(GROUP_LANES * g, GROUP_LANES * (g + 1))
            bb = xbc_ref[:, 1024 + 128 * g:1152 + 128 * g].astype(BF16)
            cb = xbc_ref[:, 1536 + 128 * g:1664 + 128 * g].astype(BF16)
            gm = _dot_nt(cb, bb)
            gmt = _dot_nt(bb, cb)
            stp = sp_ref[0, g]
            dst = ds_ref[g]
            stpb = stp.astype(BF16)
            dstb = dst.astype(BF16)
            yoff = _dot(cb, stpb) * ecw[:, gl]
            dcg = _dot_nt(dye[:, gl], stpb)
            ds_ref[g] = dst * cdw[:, gl] + _dot_tn(cb, dye[:, gl])
            dlast = jnp.sum(dst * stp, axis=0, keepdims=True) * cdw[:, gl]
            dbg = _dot_nt(xde[:, gl], dstb)
            w = _dot(bb, dstb) * dew[:, gl]
            wx = w * xd[:, gl]
            dlast = dlast + jnp.sum(wx, axis=0, keepdims=True)
            dcw_parts.append(dyv[:, gl] * yoff - wx
                             + jnp.where(lax.broadcasted_iota(jnp.int32, (q, 1), 0) == q - 1, dlast, 0.0))
            dgm = jnp.zeros((q, q), F32)
            diag = []
            for r in range(SSD_HPG):
                h = SSD_HPG * g + r
                hl = slice(64 * h, 64 * h + 64)
                dyb = dy_ref[:, hl].astype(BF16)
                xdh = xd_ref[:, hl]
                dm = _dot_nt(dyb, xdh)
                dmt = _dot_nt(xdh, dyb)
                dec = _decay(segcol, cum_t, h, keep)
                mt = gmt * _decay_t(segcol, cum_t, h, keep_t)
                dgm += dm * dec
                diag.append(dm * (gm * dec) - dmt * mt)
                dxd_ref[:, hl] = _dot(mt.astype(BF16), dyb) + w[:, 64 * r:64 * r + 64]
            onehots = (lanes_k == SSD_HPG * g + rows_k).astype(BF16)
            dcum += _dot_split(jnp.concatenate(diag, axis=1), onehots, 2)
            dgb = dgm.astype(BF16)
            dC_ref[:, 128 * g:128 * g + 128] = dcg + _dot(dgb, bb)
            dB_ref[:, 128 * g:128 * g + 128] = dbg + _dot_tn(dgb, cb)
        dxd = dxd_ref[...]
        dxs_ref[...] = dxd * dtw
        dcum += _dot_split(jnp.concatenate(dcw_parts, axis=1), heads_ref[...], 2)
        dla = _dot_split_rhs(tri, dcum, 3, (((0,), (0,)), ((), ())))
        ddt_ref[...] = _dot_split(xs * dxd, heads_ref[...], 2) + dla * a
        dalog_ref[...] += jnp.sum(dla * dtv, axis=0, keepdims=True)

        @pl.when(i == nc - 1)
        def _():
            dalog_ref[...] = dalog_ref[...] * a

    st_spec = pl.BlockSpec((1, SSD_GROUPS, SSD_STATE, GROUP_LANES), lambda i: (nc - 1 - i, 0, 0, 0))
    return _pc(
        body, name="ssd_bwd", grid=(nc,),
        in_specs=[rev(CONV_DIM), rev(DT_PAD), _const_spec((1, DT_PAD)), st_spec, rev(SSD_INNER),
                  _const_spec(to_wide.shape), _const_spec(to_heads.shape), _const_spec(to_cols.shape)],
        out_specs=[rev(SSD_INNER), rev(512), rev(512), rev(DT_PAD), _const_spec((1, DT_PAD))],
        out_shape=[jax.ShapeDtypeStruct((T, SSD_INNER), F32), jax.ShapeDtypeStruct((T, 512), F32),
                   jax.ShapeDtypeStruct((T, 512), F32), jax.ShapeDtypeStruct((T, DT_PAD), F32),
                   jax.ShapeDtypeStruct((1, DT_PAD), F32)],
        scratch_shapes=[pltpu.VMEM((SSD_GROUPS, SSD_STATE, GROUP_LANES), F32), pltpu.VMEM((q, SSD_INNER), BF16),
                        pltpu.VMEM((q, SSD_INNER), F32)],
        compiler_params=_cparams(("arbitrary",)),
    )(xbc_act, dt, alog, sprev, dy, to_wide, to_heads, to_cols)


def _s5_disc_vals(a_re, a_im, log_dt, b_re, b_im):
    dt = jnp.exp(log_dt)
    mag = jnp.exp(a_re * dt)
    ab_re = mag * jnp.cos(a_im * dt)
    ab_im = mag * jnp.sin(a_im * dt)
    den = a_re * a_re + a_im * a_im
    nr = ab_re - 1.0
    ni = ab_im
    coef_re = (nr * a_re + ni * a_im) / den
    coef_im = (ni * a_re - nr * a_im) / den
    bb_re = coef_re * b_re - coef_im * b_im
    bb_im = coef_re * b_im + coef_im * b_re
    return ab_re, ab_im, bb_re, bb_im


def _s5_disc(a_re, a_im, log_dt, b_re, b_im):
    def body(ar, ai, ld, br, bi, o1, o2, o3, o4):
        o1[...], o2[...], o3[...], o4[...] = _s5_disc_vals(ar[...], ai[...], ld[...], br[...], bi[...])

    return _pc(
        body, name="s5_disc",
        out_shape=[jax.ShapeDtypeStruct((S5_STATES, 1), F32), jax.ShapeDtypeStruct((S5_STATES, 1), F32),
                   jax.ShapeDtypeStruct((S5_STATES, 16), F32), jax.ShapeDtypeStruct((S5_STATES, 16), F32)],
    )(a_re, a_im, log_dt, b_re, b_im)


def _s5_disc_bwd(a_re, a_im, log_dt, b_re, b_im, d_ab_re, d_ab_im, d_bb_re, d_bb_im):
    def body(ar, ai, ld, br, bi, g1, g2, g3, g4, o1, o2, o3, o4, o5):
        _, vjp = jax.vjp(_s5_disc_vals, ar[...], ai[...], ld[...], br[...], bi[...])
        d1, d2, d3, d4, d5 = vjp((g1[...], g2[...], g3[...], g4[...]))
        o1[...] = d1
        o2[...] = d2
        grp = lax.broadcasted_iota(jnp.int32, (32, S5_STATES), 0)
        st = lax.broadcasted_iota(jnp.int32, (32, S5_STATES), 1)
        sel = (st // 64 == grp).astype(F32)
        o3[...] = _dot_hi(sel, d3)
        o4[...] = d4
        o5[...] = d5

    return _pc(
        body, name="s5_disc_bwd",
        out_shape=[jax.ShapeDtypeStruct((S5_STATES, 1), F32), jax.ShapeDtypeStruct((S5_STATES, 1), F32),
                   jax.ShapeDtypeStruct((32, 1), F32),
                   jax.ShapeDtypeStruct((S5_STATES, 16), F32), jax.ShapeDtypeStruct((S5_STATES, 16), F32)],
    )(a_re, a_im, log_dt, b_re, b_im, d_ab_re, d_ab_im, d_bb_re, d_bb_im)


def _cmul_add(xr, xi, pr, pi, yr, yi):
    return xr + pr * yr - pi * yi, xi + pr * yi + pi * yr


def _powers(ar, ai, n):
    out = [(ar, ai)]
    for _ in range(n - 1):
        pr, pi = out[-1]
        out.append((pr * pr - pi * pi, 2.0 * pr * pi))
    return out


_BW = S5_STATES // S5_BLOCKS
_BI = S5_WIDTH // S5_BLOCKS
SUB = 8
S5_ROWS = S5_CHUNK // SUB


def _scan8(br, bi, pws, rowin, reverse):
    k = 1
    for pr, pi in pws:
        if reverse:
            keep = rowin < SUB - k
            sr = jnp.where(keep, pltpu.roll(br, SUB - k, 0), 0.0)
            si = jnp.where(keep, pltpu.roll(bi, SUB - k, 0), 0.0)
        else:
            keep = rowin >= k
            sr = jnp.where(keep, pltpu.roll(br, k, 0), 0.0)
            si = jnp.where(keep, pltpu.roll(bi, k, 0), 0.0)
        br, bi = _cmul_add(br, bi, pr, pi, sr, si)
        k *= 2
    return br, bi


def _s5_tables(ab_ref, tab_ref, reverse):
    rowin = lax.broadcasted_iota(jnp.int32, (SUB, 1), 0)
    ar = ab_ref[0:1, :]
    ai = -ab_ref[1:2, :] if reverse else ab_ref[1:2, :]
    hit = rowin == (SUB - 1 if reverse else 0)
    zero = jnp.zeros((SUB, S5_STATES), F32)
    pr, pi = _scan8(jnp.where(hit, ar, 0.0) + zero, jnp.where(hit, ai, 0.0) + zero, _powers(ar, ai, 3), rowin, reverse)
    tab_ref[0:SUB, :] = pr
    tab_ref[SUB:2 * SUB, :] = pi


def _s5_fwd(u5, wb4, wc4, ab, dvec):
    T = u5.shape[0]
    q = S5_CHUNK
    nc = T // q

    def body(u_ref, wb_ref, wc_ref, ab_ref, d_ref, y_ref, sp_ref, carry_ref, tab_ref, sr_ref, si_ref):
        i = pl.program_id(0)
        rowin = lax.broadcasted_iota(jnp.int32, (SUB, 1), 0)

        @pl.when(i == 0)
        def _():
            carry_ref[...] = jnp.zeros_like(carry_ref)
            _s5_tables(ab_ref, tab_ref, False)

        sp_ref[0] = carry_ref[...]
        for j in range(S5_BLOCKS):
            bu = _dot(u_ref[:, _BI * j:_BI * (j + 1)].astype(BF16), wb_ref[j])
            sr_ref[:, :, _BW * j:_BW * (j + 1)] = bu[:, :_BW].reshape(S5_ROWS, SUB, _BW)
            si_ref[:, :, _BW * j:_BW * (j + 1)] = bu[:, _BW:].reshape(S5_ROWS, SUB, _BW)
        pws = _powers(ab_ref[0:1, :], ab_ref[1:2, :], 3)
        tr, ti = tab_ref[0:SUB, :], tab_ref[SUB:2 * SUB, :]
        cr, ci = carry_ref[0:1, :], carry_ref[1:2, :]
        for k in range(S5_ROWS):
            sr, si = _scan8(sr_ref[k], si_ref[k], pws, rowin, False)
            sr, si = _cmul_add(sr, si, tr, ti, cr, ci)
            sr_ref[k] = sr
            si_ref[k] = si
            cr, ci = sr[SUB - 1:SUB, :], si[SUB - 1:SUB, :]
        carry_ref[0:1, :] = cr
        carry_ref[1:2, :] = ci
        for j in range(S5_BLOCKS):
            sl = slice(_BW * j, _BW * (j + 1))
            ul = slice(_BI * j, _BI * (j + 1))
            s = jnp.concatenate([sr_ref[:, :, sl].reshape(q, _BW), si_ref[:, :, sl].reshape(q, _BW)], axis=1).astype(BF16)
            y_ref[:, ul] = _dot(s, wc_ref[j]) + d_ref[:, ul] * u_ref[:, ul]

    return _pc(
        body, name="s5_fwd", grid=(nc,),
        in_specs=[_row_spec(q, S5_WIDTH), _const_spec((S5_BLOCKS, _BI, 2 * _BW)), _const_spec((S5_BLOCKS, 2 * _BW, _BI)),
                  _const_spec((8, S5_STATES)), _const_spec((1, S5_WIDTH))],
        out_specs=[_row_spec(q, S5_WIDTH), pl.BlockSpec((1, 8, S5_STATES), lambda i: (i, 0, 0))],
        out_shape=[jax.ShapeDtypeStruct((T, S5_WIDTH), F32), jax.ShapeDtypeStruct((nc, 8, S5_STATES), F32)],
        scratch_shapes=[pltpu.VMEM((8, S5_STATES), F32), pltpu.VMEM((2 * SUB, S5_STATES), F32),
                        pltpu.VMEM((S5_ROWS, SUB, S5_STATES), F32), pltpu.VMEM((S5_ROWS, SUB, S5_STATES), F32)],
        compiler_params=_cparams(("arbitrary",)),
    )(u5, wb4, wc4, ab, dvec)


def _s5_bwd(u5, dy5, wb4, wc4, ab, dvec, sprev):
    T = u5.shape[0]
    q = S5_CHUNK
    nc = T // q

    def rev(width):
        return pl.BlockSpec((q, width), lambda i: (nc - 1 - i, 0))

    def body(u_ref, dy_ref, wb_ref, wc_ref, ab_ref, d_ref, sp_ref, du_ref, dwb_ref, dwc_ref, dab_ref, dd_ref,
             carry_ref, tab_ref, rtab_ref, sr_ref, si_ref, lr_ref, li_ref):
        i = pl.program_id(0)
        rowin = lax.broadcasted_iota(jnp.int32, (SUB, 1), 0)

        @pl.when(i == 0)
        def _():
            carry_ref[...] = jnp.zeros_like(carry_ref)
            dwb_ref[...] = jnp.zeros_like(dwb_ref)
            dwc_ref[...] = jnp.zeros_like(dwc_ref)
            dab_ref[...] = jnp.zeros_like(dab_ref)
            dd_ref[...] = jnp.zeros_like(dd_ref)
            _s5_tables(ab_ref, tab_ref, False)
            _s5_tables(ab_ref, rtab_ref, True)

        for j in range(S5_BLOCKS):
            sl = slice(_BW * j, _BW * (j + 1))
            ul = slice(_BI * j, _BI * (j + 1))
            bu = _dot(u_ref[:, ul].astype(BF16), wb_ref[j])
            sr_ref[:, :, sl] = bu[:, :_BW].reshape(S5_ROWS, SUB, _BW)
            si_ref[:, :, sl] = bu[:, _BW:].reshape(S5_ROWS, SUB, _BW)
            ds = _dot_nt(dy_ref[:, ul].astype(BF16), wc_ref[j])
            lr_ref[:, :, sl] = ds[:, :_BW].reshape(S5_ROWS, SUB, _BW)
            li_ref[:, :, sl] = ds[:, _BW:].reshape(S5_ROWS, SUB, _BW)
        ar, ai = ab_ref[0:1, :], ab_ref[1:2, :]
        pws = _powers(ar, ai, 3)
        tr, ti = tab_ref[0:SUB, :], tab_ref[SUB:2 * SUB, :]
        cr, ci = sp_ref[0, 0:1, :], sp_ref[0, 1:2, :]
        for k in range(S5_ROWS):
            sr, si = _scan8(sr_ref[k], si_ref[k], pws, rowin, False)
            sr, si = _cmul_add(sr, si, tr, ti, cr, ci)
            sr_ref[k] = sr
            si_ref[k] = si
            cr, ci = sr[SUB - 1:SUB, :], si[SUB - 1:SUB, :]
        pws = _powers(ar, -ai, 3)
        tr, ti = rtab_ref[0:SUB, :], rtab_ref[SUB:2 * SUB, :]
        cr, ci = carry_ref[0:1, :], carry_ref[1:2, :]
        acc_r = jnp.zeros((SUB, S5_STATES), F32)
        acc_i = jnp.zeros((SUB, S5_STATES), F32)
        for k in reversed(range(S5_ROWS)):
            lr, li = _scan8(lr_ref[k], li_ref[k], pws, rowin, True)
            lr, li = _cmul_add(lr, li, tr, ti, cr, ci)
            lr_ref[k] = lr
            li_ref[k] = li
            cr, ci = lr[0:1, :], li[0:1, :]
            if k > 0:
                before_r, before_i = sr_ref[k - 1, SUB - 1:SUB, :], si_ref[k - 1, SUB - 1:SUB, :]
            else:
                before_r, before_i = sp_ref[0, 0:1, :], sp_ref[0, 1:2, :]
            keep = rowin >= 1
            pr = jnp.where(keep, pltpu.roll(sr_ref[k], 1, 0), before_r)
            pi = jnp.where(keep, pltpu.roll(si_ref[k], 1, 0), before_i)
            acc_r += lr * pr + li * pi
            acc_i += li * pr - lr * pi
        carry_ref[0:1, :] = cr
        carry_ref[1:2, :] = ci
        dab_ref[0:1, :] += jnp.sum(acc_r, axis=0, keepdims=True)
        dab_ref[1:2, :] += jnp.sum(acc_i, axis=0, keepdims=True)
        for j in range(S5_BLOCKS):
            sl = slice(_BW * j, _BW * (j + 1))
            ul = slice(_BI * j, _BI * (j + 1))
            u = u_ref[:, ul]
            dy = dy_ref[:, ul]
            dyb = dy.astype(BF16)
            lam = jnp.concatenate([lr_ref[:, :, sl].reshape(q, _BW), li_ref[:, :, sl].reshape(q, _BW)], axis=1).astype(BF16)
            s = jnp.concatenate([sr_ref[:, :, sl].reshape(q, _BW), si_ref[:, :, sl].reshape(q, _BW)], axis=1).astype(BF16)
            du_ref[:, ul] = _dot_nt(lam, wb_ref[j]) + d_ref[:, ul] * dy
            dwb_ref[j] += _dot_tn(u.astype(BF16), lam)
            dwc_ref[j] += _dot_tn(s, dyb)
            dd_ref[:, ul] += jnp.sum(dy * u, axis=0, keepdims=True)

    big = pltpu.VMEM((S5_ROWS, SUB, S5_STATES), F32)
    return _pc(
        body, name="s5_bwd", grid=(nc,),
        in_specs=[rev(S5_WIDTH), rev(S5_WIDTH), _const_spec((S5_BLOCKS, _BI, 2 * _BW)), _const_spec((S5_BLOCKS, 2 * _BW, _BI)),
                  _const_spec((8, S5_STATES)), _const_spec((1, S5_WIDTH)),
                  pl.BlockSpec((1, 8, S5_STATES), lambda i: (nc - 1 - i, 0, 0))],
        out_specs=[rev(S5_WIDTH), _const_spec((S5_BLOCKS, _BI, 2 * _BW)), _const_spec((S5_BLOCKS, 2 * _BW, _BI)),
                   _const_spec((8, S5_STATES)), _const_spec((1, S5_WIDTH))],
        out_shape=[jax.ShapeDtypeStruct((T, S5_WIDTH), F32), jax.ShapeDtypeStruct((S5_BLOCKS, _BI, 2 * _BW), F32),
                   jax.ShapeDtypeStruct((S5_BLOCKS, 2 * _BW, _BI), F32), jax.ShapeDtypeStruct((8, S5_STATES), F32),
                   jax.ShapeDtypeStruct((1, S5_WIDTH), F32)],
        scratch_shapes=[pltpu.VMEM((8, S5_STATES), F32), pltpu.VMEM((2 * SUB, S5_STATES), F32),
                        pltpu.VMEM((2 * SUB, S5_STATES), F32), big, big, big, big],
        compiler_params=_cparams(("arbitrary",)),
    )(u5, dy5, wb4, wc4, ab, dvec, sprev)


def _merge_vals(ys, xs, z, y5, gates, dvec, gssd, glu_w, glu_b, wbr):
    sz = _sigmoid(z)
    qv = ys + dvec * xs
    pre = qv * (z * sz)
    yn, rs = [], []
    for gi in range(SSD_GROUPS):
        p, r = _rms(pre[:, 256 * gi:256 * (gi + 1)])
        yn.append(p)
        rs.append(r)
    yn = jnp.concatenate(yn, axis=1)
    ya = yn * gssd
    gel = _gelu(y5)
    sg = _sigmoid(_dot(gel.astype(BF16), glu_w) + glu_b)
    yb = gel * sg
    pa = _dot(ya.astype(BF16), wbr[0:SSD_INNER, :])
    pb = _dot(yb.astype(BF16), wbr[SSD_INNER:, :])
    s0 = _sigmoid(gates[:, :D_MODEL])
    s1 = _sigmoid(gates[:, D_MODEL:])
    merged = s0 * pa + s1 * pb
    return dict(sz=sz, qv=qv, yn=yn, rs=rs, ya=ya, gel=gel, sg=sg, yb=yb, pa=pa, pb=pb, s0=s0, s1=s1, merged=merged)


def _merge_specs(tm):
    acts = [_row_spec(tm, 1024), _row_spec(tm, 1024, 0), _row_spec(tm, 1024), _row_spec(tm, 512), _row_spec(tm, 2048),
            _row_spec(tm, 1024)]
    params = [_const_spec((1, 1024)), _const_spec((1, 1024)), _const_spec((512, 512)), _const_spec((1, 512)),
              _hbm_spec(), _hbm_spec()]
    return acts, params


def _merge_fwd(ys, xbc_act, z, y5, gates, x, dvec, gssd, glu_w, glu_b, wbr, wout):
    T = x.shape[0]
    tm = TOKEN_TILE
    acts, params = _merge_specs(tm)

    def body(ys_ref, xs_ref, z_ref, y5_ref, gt_ref, x_ref, dv_ref, gs_ref, gw_ref, gb_ref, wbr_hbm, wout_hbm, x1_ref,
             wbr_ref, wout_ref):
        @pl.when(pl.program_id(0) == 0)
        def _():
            pltpu.sync_copy(wbr_hbm, wbr_ref)
            pltpu.sync_copy(wout_hbm, wout_ref)

        v = _merge_vals(ys_ref[...], xs_ref[...], z_ref[...], y5_ref[...], gt_ref[...], dv_ref[...], gs_ref[...],
                        gw_ref[...], gb_ref[...], wbr_ref)
        x1_ref[...] = x_ref[...] + _dot(v["merged"].astype(BF16), wout_ref[...])

    return _pc(
        body, name="merge_fwd", grid=(T // tm,),
        in_specs=acts + params, out_specs=_row_spec(tm, 1024),
        out_shape=jax.ShapeDtypeStruct((T, 1024), F32),
        scratch_shapes=[pltpu.VMEM((1536, 1024), BF16), pltpu.VMEM((1024, 1024), BF16)],
        compiler_params=_cparams(("arbitrary",)),
    )(ys, xbc_act, z, y5, gates, x, dvec, gssd, glu_w, glu_b, wbr, wout)


def _merge_bwd(ys, xbc_act, z, y5, gates, dx1, dvec, gssd, glu_w, glu_b, wbr, wout, head_sel):
    T = dx1.shape[0]
    tm = TOKEN_TILE
    acts, params = _merge_specs(tm)

    def body(ys_ref, xs_ref, z_ref, y5_ref, gt_ref, dx1_ref, dv_ref, gs_ref, gw_ref, gb_ref, wbr_hbm, wout_hbm, hs_ref,
             dys_ref, dxs_ref, dz_ref, dy5_ref, dgt_ref, mg_ref, ya_ref, yb_ref, dpa_ref, dpb_ref, gel_ref, dpre_ref,
             ddv_ref, dgs_ref, dgb_ref, wbr_ref, wout_ref, ddacc_ref):
        i = pl.program_id(0)

        @pl.when(i == 0)
        def _():
            pltpu.sync_copy(wbr_hbm, wbr_ref)
            pltpu.sync_copy(wout_hbm, wout_ref)
            ddacc_ref[...] = jnp.zeros_like(ddacc_ref)
            dgs_ref[...] = jnp.zeros_like(dgs_ref)
            dgb_ref[...] = jnp.zeros_like(dgb_ref)

        ys, xs, z, y5, gates = ys_ref[...], xs_ref[...], z_ref[...], y5_ref[...], gt_ref[...]
        dvv, gsv, gw = dv_ref[...], gs_ref[...], gw_ref[...]
        v = _merge_vals(ys, xs, z, y5, gates, dvv, gsv, gw, gb_ref[...], wbr_ref)
        dmg = _dot_nt(dx1_ref[...].astype(BF16), wout_ref[...])
        s0, s1, pa, pb = v["s0"], v["s1"], v["pa"], v["pb"]
        dgt_ref[:, :D_MODEL] = dmg * pa * s0 * (1.0 - s0)
        dgt_ref[:, D_MODEL:] = dmg * pb * s1 * (1.0 - s1)
        dpa = (dmg * s0).astype(BF16)
        dpb = (dmg * s1).astype(BF16)
        dya = _dot_nt(dpa, wbr_ref[0:SSD_INNER, :])
        dyb = _dot_nt(dpb, wbr_ref[SSD_INNER:, :])
        gel, sg = v["gel"], v["sg"]
        dpre = (dyb * gel * sg * (1.0 - sg))
        dgb_ref[...] += jnp.sum(dpre, axis=0, keepdims=True)
        dpre_b = dpre.astype(BF16)
        dgel = dyb * sg + _dot_nt(dpre_b, gw)
        dy5_ref[...] = dgel * _gelu_grad(y5)
        yn = v["yn"]
        dgs_ref[...] += jnp.sum(dya * yn, axis=0, keepdims=True)
        dyn = dya * gsv
        dpre_a = jnp.concatenate(
            [_rms_bwd(yn[:, 256 * gi:256 * (gi + 1)], v["rs"][gi], dyn[:, 256 * gi:256 * (gi + 1)])
             for gi in range(SSD_GROUPS)], axis=1)
        sz, qv = v["sz"], v["qv"]
        dq = dpre_a * (z * sz)
        dz_ref[...] = dpre_a * qv * (sz * (1.0 + z * (1.0 - sz)))
        dys_ref[...] = dq
        dxs_ref[...] = dq * dvv
        ddacc_ref[...] += jnp.sum(dq * xs, axis=0, keepdims=True)
        mg_ref[...] = v["merged"].astype(BF16)
        ya_ref[...] = v["ya"].astype(BF16)
        yb_ref[...] = v["yb"].astype(BF16)
        dpa_ref[...] = dpa
        dpb_ref[...] = dpb
        gel_ref[...] = gel.astype(BF16)
        dpre_ref[...] = dpre_b

        @pl.when(i == pl.num_programs(0) - 1)
        def _():
            ddv_ref[...] = _dot_hi(ddacc_ref[...], hs_ref[...])

    outs = [(1024, F32), (1024, F32), (1024, F32), (512, F32), (2048, F32),
            (1024, BF16), (1024, BF16), (512, BF16), (1024, BF16), (1024, BF16), (512, BF16), (512, BF16)]
    return _pc(
        body, name="merge_bwd", grid=(T // tm,),
        in_specs=acts + params + [_const_spec((1024, DT_PAD))],
        out_specs=[_row_spec(tm, w) for w, _ in outs] + [_const_spec((1, DT_PAD)), _const_spec((1, 1024)), _const_spec((1, 512))],
        out_shape=[jax.ShapeDtypeStruct((T, w), d) for w, d in outs] + [
            jax.ShapeDtypeStruct((1, DT_PAD), F32), jax.ShapeDtypeStruct((1, 1024), F32), jax.ShapeDtypeStruct((1, 512), F32)],
        scratch_shapes=[pltpu.VMEM((1536, 1024), BF16), pltpu.VMEM((1024, 1024), BF16), pltpu.VMEM((1, 1024), F32)],
        compiler_params=_cparams(("arbitrary",)),
    )(ys, xbc_act, z, y5, gates, dx1, dvec, gssd, glu_w, glu_b, wbr, wout, head_sel)


def _mlp_fwd(x1, g, w1, w2):
    T = x1.shape[0]
    tm = TOKEN_TILE

    def body(x_ref, g_ref, w1_hbm, w2_hbm, o_ref, w1_ref, w2_ref):
        @pl.when(pl.program_id(0) == 0)
        def _():
            pltpu.sync_copy(w1_hbm, w1_ref)
            pltpu.sync_copy(w2_hbm, w2_ref)

        xv = x_ref[...]
        xn, _ = _rms(xv)
        h = (xn * g_ref[...]).astype(BF16)
        acc = xv
        for s in range(FF_SHARDS):
            rl = jnp.maximum(_dot(h, w1_ref[s]), 0.0)
            acc += _dot((rl * rl).astype(BF16), w2_ref[FF_SHARD * s:FF_SHARD * (s + 1), :])
        o_ref[...] = acc

    return _pc(
        body, name="mlp_fwd", grid=(T // tm,),
        in_specs=[_row_spec(tm, 1024), _const_spec((1, 1024)), _hbm_spec(), _hbm_spec()],
        out_specs=_row_spec(tm, 1024), out_shape=jax.ShapeDtypeStruct((T, 1024), F32),
        scratch_shapes=[pltpu.VMEM((FF_SHARDS, D_MODEL, FF_SHARD), BF16), pltpu.VMEM((D_FF, D_MODEL), BF16)],
        compiler_params=_cparams(("arbitrary",)),
    )(x1, g, w1, w2)


def _mlp_bwd(x1, dx2, g, w1, w2):
    T = x1.shape[0]
    tm = TOKEN_TILE

    def body(x_ref, dx2_ref, g_ref, w1_hbm, w2_hbm, dx1_ref, h_ref, act_ref, da_ref, dg_ref, w1_ref, w2_ref):
        @pl.when(pl.program_id(0) == 0)
        def _():
            pltpu.sync_copy(w1_hbm, w1_ref)
            pltpu.sync_copy(w2_hbm, w2_ref)
            dg_ref[...] = jnp.zeros_like(dg_ref)

        xn, r = _rms(x_ref[...])
        gv = g_ref[...]
        h = (xn * gv).astype(BF16)
        h_ref[...] = h
        dx2 = dx2_ref[...]
        dx2b = dx2.astype(BF16)
        dh = jnp.zeros((tm, D_MODEL), F32)
        for s in range(FF_SHARDS):
            ff = slice(FF_SHARD * s, FF_SHARD * (s + 1))
            rl = jnp.maximum(_dot(h, w1_ref[s]), 0.0)
            act_ref[:, ff] = (rl * rl).astype(BF16)
            da = (_dot_nt(dx2b, w2_ref[ff, :]) * (2.0 * rl)).astype(BF16)
            da_ref[:, ff] = da
            dh += _dot_nt(da, w1_ref[s])
        dg_ref[...] += jnp.sum(dh * xn, axis=0, keepdims=True)
        dx1_ref[...] = dx2 + _rms_bwd(xn, r, dh * gv)

    return _pc(
        body, name="mlp_bwd", grid=(T // tm,),
        in_specs=[_row_spec(tm, 1024), _row_spec(tm, 1024), _const_spec((1, 1024)), _hbm_spec(), _hbm_spec()],
        out_specs=[_row_spec(tm, 1024), _row_spec(tm, 1024), _row_spec(tm, D_FF), _row_spec(tm, D_FF), _const_spec((1, 1024))],
        out_shape=[jax.ShapeDtypeStruct((T, 1024), F32), jax.ShapeDtypeStruct((T, 1024), BF16),
                   jax.ShapeDtypeStruct((T, D_FF), BF16), jax.ShapeDtypeStruct((T, D_FF), BF16),
                   jax.ShapeDtypeStruct((1, 1024), F32)],
        scratch_shapes=[pltpu.VMEM((FF_SHARDS, D_MODEL, FF_SHARD), BF16), pltpu.VMEM((D_FF, D_MODEL), BF16)],
        compiler_params=_cparams(("arbitrary",)),
    )(x1, dx2, g, w1, w2)


def _loss_head(x2, target, g):
    T = x2.shape[0]
    tm = TOKEN_TILE

    def body(x_ref, t_ref, g_ref, dx_ref, loss_ref, dg_ref):
        @pl.when(pl.program_id(0) == 0)
        def _():
            loss_ref[...] = jnp.zeros_like(loss_ref)
            dg_ref[...] = jnp.zeros_like(dg_ref)

        xn, r = _rms(x_ref[...])
        gv = g_ref[...]
        err = xn * gv - t_ref[...]
        loss_ref[...] += jnp.sum(err * err, axis=0, keepdims=True) * (0.5 / D_MODEL)
        dy = err * (1.0 / D_MODEL)
        dg_ref[...] += jnp.sum(dy * xn, axis=0, keepdims=True)
        dx_ref[...] = _rms_bwd(xn, r, dy * gv)

    return _pc(
        body, name="loss_head", grid=(T // tm,),
        in_specs=[_row_spec(tm, 1024), _row_spec(tm, 1024), _const_spec((1, 1024))],
        out_specs=[_row_spec(tm, 1024), _const_spec((1, 1024)), _const_spec((1, 1024))],
        out_shape=[jax.ShapeDtypeStruct((T, 1024), F32), jax.ShapeDtypeStruct((1, 1024), F32),
                   jax.ShapeDtypeStruct((1, 1024), F32)],
        compiler_params=_cparams(("arbitrary",)),
    )(x2, target, g)


WGRAD_TOKENS = 512
WGRAD_OUT_ELEMS = 2 * 1024 * 1024


def _wgrad(a, b, name, col_shards=None):
    T, K = a.shape
    N = b.shape[1]
    tt = min(T, WGRAD_TOKENS)
    nb = N // col_shards if col_shards else min(N, max(128, WGRAD_OUT_ELEMS // K))
    assert N % nb == 0 and T % tt == 0
    if col_shards:
        out_spec = pl.BlockSpec((None, K, nb), lambda n, t: (n, 0, 0))
        out_shape = jax.ShapeDtypeStruct((col_shards, K, nb), F32)
    else:
        out_spec = pl.BlockSpec((K, nb), lambda n, t: (0, n))
        out_shape = jax.ShapeDtypeStruct((K, N), F32)

    def body(a_ref, b_ref, o_ref):
        @pl.when(pl.program_id(1) == 0)
        def _():
            o_ref[...] = jnp.zeros_like(o_ref)

        o_ref[...] += _dot_tn(a_ref[...].astype(BF16), b_ref[...].astype(BF16))

    return _pc(
        body, name=name, grid=(N // nb, T // tt),
        in_specs=[pl.BlockSpec((tt, K), lambda n, t: (t, 0)), pl.BlockSpec((tt, nb), lambda n, t: (t, n))],
        out_specs=out_spec, out_shape=out_shape,
        compiler_params=_cparams(("parallel", "arbitrary")),
    )(a, b)


def _s5_block_weights(bb_re, bb_im, c_re, c_im):
    eye = jnp.eye(8, dtype=F32)
    bre = bb_re.reshape(S5_BLOCKS, 8, 64, 16)
    bim = bb_im.reshape(S5_BLOCKS, 8, 64, 16)
    wb_re = jnp.einsum('jgpk,gh->jhkgp', bre, eye).reshape(S5_BLOCKS, _BI, _BW)
    wb_im = jnp.einsum('jgpk,gh->jhkgp', bim, eye).reshape(S5_BLOCKS, _BI, _BW)
    wb4 = jnp.concatenate([wb_re, wb_im], axis=2).astype(BF16)
    cre = c_re.reshape(S5_BLOCKS, 8, 16, 64)
    cim = c_im.reshape(S5_BLOCKS, 8, 16, 64)
    wc_re = jnp.einsum('jgkp,gh->jgphk', cre, eye).reshape(S5_BLOCKS, _BW, _BI)
    wc_im = jnp.einsum('jgkp,gh->jgphk', -cim, eye).reshape(S5_BLOCKS, _BW, _BI)
    wc4 = jnp.concatenate([wc_re, wc_im], axis=1).astype(BF16)
    return wb4, wc4


def _s5_block_grads(dwb4, dwc4):
    eye = jnp.eye(8, dtype=F32)
    dwb = dwb4.reshape(S5_BLOCKS, 8, 16, 2, 8, 64)
    dbb = jnp.einsum('jhkrgp,gh->rjgpk', dwb, eye).reshape(2, 32, 64, 16)
    dwc = dwc4.reshape(S5_BLOCKS, 2, 8, 64, 8, 16)
    dc = jnp.einsum('jrgphk,gh->rjgkp', dwc, eye).reshape(2, 32, 16, 64)
    return dbb[0], dbb[1], dc[0], -dc[1]


def _row(v, width=None):
    v = v.reshape(1, -1)
    if width is not None and v.shape[1] < width:
        v = jnp.concatenate([v, jnp.zeros((1, width - v.shape[1]), v.dtype)], axis=1)
    return v


def _local_step(x, target, p):
    g_mix, g_mlp, g_fin = _row(p["norm_mix_g"]), _row(p["norm_mlp_g"]), _row(p["norm_final_g"])
    conv_b = _row(p["conv_b"])
    dt_bias = _row(p["dt_bias"], DT_PAD)
    alog = _row(p["a_log"], DT_PAD)
    dvec = _row(jnp.repeat(p["d_ssd"], SSD_HEADDIM))
    gssd = _row(p["ssd_norm_g"])
    s5d = _row(p["s5_d"])
    glu_b = _row(p["s5_glu_b"])
    head_sel = (jnp.arange(SSD_INNER)[:, None] // SSD_HEADDIM == jnp.arange(DT_PAD)[None, :]).astype(F32)

    a_re = p["s5_a_re"].reshape(S5_STATES, 1)
    a_im = p["s5_a_im"].reshape(S5_STATES, 1)
    log_dt = jnp.repeat(p["s5_log_dt"], 64).reshape(S5_STATES, 1)
    b_re = p["s5_b_re"].reshape(S5_STATES, 16)
    b_im = p["s5_b_im"].reshape(S5_STATES, 16)
    ab_re, ab_im, bb_re, bb_im = _s5_disc(a_re, a_im, log_dt, b_re, b_im)
    wb4, wc4 = _s5_block_weights(bb_re, bb_im, p["s5_c_re"], p["s5_c_im"])
    ab = jnp.concatenate([ab_re.reshape(1, S5_STATES), ab_im.reshape(1, S5_STATES), jnp.zeros((6, S5_STATES), F32)], axis=0)

    wp, wbr, wout, w1, w2, glu_w = p["w_in_perm"], p["w_branch"], p["w_out"], p["w_mlp_in"], p["w_mlp_out"], p["s5_glu_w"]

    z, xbc_raw, u5, gates, dt_raw = _inproj_fwd(x, g_mix, wp)
    xbc_act, dt = _conv_fwd(xbc_raw, dt_raw, p["conv_w"], conv_b, dt_bias)
    ys, ssd_states = _ssd_fwd(xbc_act, dt, alog)
    y5, s5_states = _s5_fwd(u5, wb4, wc4, ab, s5d)
    x1 = _merge_fwd(ys, xbc_act, z, y5, gates, x, dvec, gssd, glu_w, glu_b, wbr, wout)
    x2 = _mlp_fwd(x1, g_mlp, w1, w2)
    dx2, loss_lanes, d_gfin = _loss_head(x2, target, g_fin)

    dx1, h2, act, da1, d_gmlp = _mlp_bwd(x1, dx2, g_mlp, w1, w2)
    d_w_mlp_out = _wgrad(act, dx2, "wgrad_mlp_out")
    d_w_mlp_in = _wgrad(h2, da1, "wgrad_mlp_in", col_shards=FF_SHARDS)
    (dys, dxs_m, dz, dy5, dgates, mg, ya, yb, dpa, dpb, gel, dpre, d_dssd, d_gssd, d_glu_b) = _merge_bwd(
        ys, xbc_act, z, y5, gates, dx1, dvec, gssd, glu_w, glu_b, wbr, wout, head_sel)
    d_w_out = _wgrad(mg, dx1, "wgrad_out")
    d_w_branch = jnp.concatenate([_wgrad(ya, dpa, "wgrad_branch_a"), _wgrad(yb, dpb, "wgrad_branch_b")], axis=0)
    d_glu_w = _wgrad(gel, dpre, "wgrad_glu")
    du5, dwb4, dwc4, dab, d_s5d = _s5_bwd(u5, dy5, wb4, wc4, ab, s5d, s5_states)
    dbb_re, dbb_im, d_c_re, d_c_im = _s5_block_grads(dwb4, dwc4)
    d_a_re, d_a_im, d_log_dt, d_b_re, d_b_im = _s5_disc_bwd(
        a_re, a_im, log_dt, b_re, b_im, dab[0].reshape(S5_STATES, 1), dab[1].reshape(S5_STATES, 1),
        dbb_re.reshape(S5_STATES, 16), dbb_im.reshape(S5_STATES, 16))
    dxs_s, dB, dC, ddt, d_alog = _ssd_bwd(xbc_act, dt, alog, ssd_states, dys)
    dxbc_raw, ddt_raw, d_conv_w, d_conv_b, d_dt_bias = _conv_bwd(
        xbc_raw, dt_raw, dxs_m, dxs_s, dB, dC, ddt, p["conv_w"], conv_b, dt_bias)
    dx, h, d_gmix = _inproj_bwd(x, dx1, dz, dxbc_raw, du5, dgates, ddt_raw, g_mix, wp)
    d_w_in = dict(z=_wgrad(h, dz, "wgrad_in_z"), xbc=_wgrad(h, dxbc_raw, "wgrad_in_xbc"),
                  dt=_wgrad(h, ddt_raw, "wgrad_in_dt")[:, :16], u5=_wgrad(h, du5, "wgrad_in_u5"),
                  gates=_wgrad(h, dgates, "wgrad_in_gates"))

    grads = dict(
        norm_mix_g=d_gmix.reshape(-1), w_in_pieces=[(c0, d_w_in[n]) for n, c0, _ in W_IN_PIECES],
        conv_w=d_conv_w[:CONV_K], conv_b=d_conv_b.reshape(-1),
        dt_bias=d_dt_bias[0, :16], a_log=d_alog[0, :16], d_ssd=d_dssd[0, :16], ssd_norm_g=d_gssd.reshape(-1),
        s5_a_re=d_a_re.reshape(32, 64), s5_a_im=d_a_im.reshape(32, 64), s5_log_dt=d_log_dt.reshape(32),
        s5_b_re=d_b_re.reshape(32, 64, 16), s5_b_im=d_b_im.reshape(32, 64, 16), s5_c_re=d_c_re, s5_c_im=d_c_im,
        s5_d=d_s5d.reshape(-1), s5_glu_w=d_glu_w, s5_glu_b=d_glu_b.reshape(-1), w_branch=d_w_branch, w_out=d_w_out,
        norm_mlp_g=d_gmlp.reshape(-1), w_mlp_in=d_w_mlp_in, w_mlp_out=d_w_mlp_out, norm_final_g=d_gfin.reshape(-1))
    return jnp.sum(loss_lanes), dx, grads


MESH = pl.DeviceIdType.MESH
N_CHIPS = 4


def _place():
    x, y, c = lax.axis_index("x"), lax.axis_index("y"), lax.axis_index("c")
    chips = [(1 - x, y), (x, 1 - y), (1 - x, 1 - y)]
    return x, y, c, chips


def _remote(src, dst, send_sems, recv_sems, k, to):
    return pltpu.make_async_remote_copy(src_ref=src, dst_ref=dst, send_sem=send_sems.at[k], recv_sem=recv_sems.at[k],
                                        device_id=to, device_id_type=MESH)


def _row_chunks(rows, k, align):
    step = rows // k
    assert rows % k == 0 and step % align == 0, (rows, k, align)
    return [(i * step, step) for i in range(k)]


ICI_CHUNKS = 4
D2D_CHUNKS = 24


def _allgather_chips(srcs, ks, name):
    halves = [a.shape[0] // 2 for a in srcs]
    pieces = [_row_chunks(h, k, 32 // a.dtype.itemsize) for a, h, k in zip(srcs, halves, ks)]
    n_ici = 3 * sum(ks)
    n_sem = 2 * n_ici + len(srcs)

    def body(*refs):
        src_refs, out_refs = refs[:len(srcs)], refs[len(srcs):2 * len(srcs)]
        send_sems, recv_sems = refs[2 * len(srcs):]
        x, y, c, chips = _place()
        own = 2 * x + y
        sib = (x, y, 1 - c)
        first, fwd_plan, k = [], [], 0
        for a, (src_ref, out_ref) in enumerate(zip(src_refs, out_refs)):
            h = halves[a]
            for r0, nr in pieces[a]:
                for cx, cy in chips:
                    first.append(_remote(src_ref.at[pl.ds(c * h + r0, nr), :], out_ref.at[own, pl.ds(c * h + r0, nr), :],
                                         send_sems, recv_sems, k, (cx, cy, c)))
                    fwd_plan.append((out_ref, 2 * cx + cy, h, r0, nr, k, (cx, cy, c)))
                    k += 1
        for a, (src_ref, out_ref) in enumerate(zip(src_refs, out_refs)):
            first.append(_remote(src_ref, out_ref.at[own], send_sems, recv_sems, 2 * n_ici + a, sib))
        for cp in first:
            cp.start()
        passed = []
        for out_ref, s, h, r0, nr, k, frm in fwd_plan:
            got = out_ref.at[s, pl.ds(c * h + r0, nr), :]
            _remote(got, got, send_sems, recv_sems, k, frm).wait_recv()
            fw = _remote(got, got, send_sems, recv_sems, n_ici + k, sib)
            fw.start()
            passed.append(fw)
        for out_ref, s, h, r0, nr, k, frm in fwd_plan:
            got = out_ref.at[s, pl.ds((1 - c) * h + r0, nr), :]
            _remote(got, got, send_sems, recv_sems, n_ici + k, sib).wait_recv()
        for a, (src_ref, out_ref) in enumerate(zip(src_refs, out_refs)):
            _remote(src_ref, out_ref.at[own], send_sems, recv_sems, 2 * n_ici + a, sib).wait_recv()
        for cp in first + passed:
            cp.wait_send()

    return _pc(
        body, name=name, in_specs=[_hbm_spec()] * len(srcs), out_specs=[_hbm_spec()] * len(srcs),
        out_shape=[jax.ShapeDtypeStruct((N_CHIPS,) + a.shape, a.dtype) for a in srcs],
        scratch_shapes=[pltpu.SemaphoreType.DMA((n_sem,)), pltpu.SemaphoreType.DMA((n_sem,))],
    )(*srcs)


def _d2d_pieces(rows):
    k = next(k for k in (24, 16, 8, 4, 2, 1) if rows % k == 0 and (rows // k) % 8 == 0)
    return _row_chunks(rows, k, 8)


def _pair_exchange(gs, small):
    n = len(gs)
    halves = [g.shape[1] // 2 for g in gs]

    def body(*refs):
        g_refs, s_ref, sib_refs, sibs_ref = refs[:n], refs[n], refs[n + 1:2 * n + 1], refs[2 * n + 1]
        send_sems, recv_sems = refs[2 * n + 2:]
        x, y, c, _ = _place()
        sib = (x, y, 1 - c)
        for a in range(n):
            h = halves[a]
            for s in range(N_CHIPS):
                for r0, nr in _d2d_pieces(h):
                    _remote(g_refs[a].at[s, pl.ds((1 - c) * h + r0, nr), :], sib_refs[a].at[s, pl.ds(r0, nr), :],
                            send_sems, recv_sems, a, sib).start()
        sm = _remote(s_ref, sibs_ref, send_sems, recv_sems, n, sib)
        sm.start()
        for a in range(n):
            _remote(sib_refs[a], sib_refs[a], send_sems, recv_sems, a, sib).wait()
        sm.wait()

    return _pc(
        body, name="pair_exchange", in_specs=[_hbm_spec()] * (n + 1), out_specs=[_hbm_spec()] * (n + 1),
        out_shape=[jax.ShapeDtypeStruct((N_CHIPS, h, g.shape[2]), F32) for g, h in zip(gs, halves)]
        + [jax.ShapeDtypeStruct(small.shape, F32)],
        scratch_shapes=[pltpu.SemaphoreType.DMA((n + 1,)), pltpu.SemaphoreType.DMA((n + 1,))],
    )(*gs, small)


SUM_BLOCKS = 4


def _pair_sum(place, g, sib, name, small=None, sib_small=None):
    n, R, C = g.shape
    H = R // 2
    rb = H // SUM_BLOCKS
    assert H % SUM_BLOCKS == 0 and rb % 16 == 0

    def body(place_ref, a_ref, b_ref, *rest):
        if small is None:
            pf_ref, pb_ref = rest
        else:
            s_ref, t_ref, pf_ref, pb_ref, ps_ref = rest

            @pl.when((pl.program_id(0) == 0) & (pl.program_id(1) == 0))
            def _():
                ps_ref[...] = s_ref[...] + t_ref[...]

        p = a_ref[...] + b_ref[...]
        pf_ref[...] = p
        pb_ref[...] = p.astype(BF16)

    blk = pl.BlockSpec((1, rb, C), lambda s, i, pr: (s, i, 0))
    mine = pl.BlockSpec((1, rb, C), lambda s, i, pr: (s, pr[1] * SUM_BLOCKS + i, 0))
    ins, outs, shapes, args = [mine, blk], [blk, blk], [jax.ShapeDtypeStruct((n, H, C), F32),
                                                        jax.ShapeDtypeStruct((n, H, C), BF16)], [g, sib]
    if small is not None:
        sm = pl.BlockSpec(small.shape, lambda s, i, pr: (0, 0))
        ins += [sm, sm]
        outs += [sm]
        shapes += [jax.ShapeDtypeStruct(small.shape, F32)]
        args += [small, sib_small]
    return _pc(
        body, name=name, out_shape=shapes,
        grid_spec=pltpu.PrefetchScalarGridSpec(num_scalar_prefetch=1, grid=(n, SUM_BLOCKS), in_specs=ins, out_specs=outs),
        compiler_params=_cparams(("arbitrary", "arbitrary")),
    )(place, *args)


def _chip_exchange(pbs, psmall):
    n = len(pbs)

    def body(*refs):
        pb_refs, ps_ref, got_refs, small4_ref = refs[:n], refs[n], refs[n + 1:2 * n + 1], refs[2 * n + 1]
        send_sems, recv_sems = refs[2 * n + 2:]
        x, y, c, chips = _place()
        own = 2 * x + y
        small = []
        for j, (cx, cy) in enumerate(chips):
            for a in range(n):
                for r0, nr in _row_chunks(pbs[a].shape[1], ICI_CHUNKS, 16):
                    _remote(pb_refs[a].at[2 * cx + cy, pl.ds(r0, nr), :], got_refs[a].at[j, pl.ds(r0, nr), :],
                            send_sems, recv_sems, 3 * a + j, (cx, cy, c)).start()
            small.append(_remote(ps_ref, small4_ref.at[own], send_sems, recv_sems, 3 * n + j, (cx, cy, c)))
            small[-1].start()
        for j, (cx, cy) in enumerate(chips):
            for a in range(n):
                _remote(pb_refs[a].at[own], got_refs[a].at[j], send_sems, recv_sems, 3 * a + j, (cx, cy, c)).wait()
            _remote(ps_ref, small4_ref.at[2 * cx + cy], send_sems, recv_sems, 3 * n + j, (cx, cy, c)).wait_recv()
        for cp in small:
            cp.wait_send()

    return _pc(
        body, name="chip_exchange", in_specs=[_hbm_spec()] * (n + 1), out_specs=[_hbm_spec()] * (n + 1),
        out_shape=[jax.ShapeDtypeStruct((3,) + pb.shape[1:], BF16) for pb in pbs]
        + [jax.ShapeDtypeStruct((N_CHIPS,) + psmall.shape, F32)],
        scratch_shapes=[pltpu.SemaphoreType.DMA((3 * n + 3,)), pltpu.SemaphoreType.DMA((3 * n + 3,))],
    )(*pbs, psmall)


def _chip_sum(place, pf, got, name, small4=None, psmall=None):
    _, H, C = pf.shape
    rb = H // SUM_BLOCKS

    def body(place_ref, o_ref, g_ref, *rest):
        if small4 is None:
            (tot_ref,) = rest
        else:
            s_ref, p_ref, tot_ref, st_ref = rest

            @pl.when(pl.program_id(0) == 0)
            def _():
                terms = [jnp.where(place_ref[0] == s, p_ref[...], s_ref[s]) for s in range(N_CHIPS)]
                st_ref[...] = ((terms[0] + terms[1]) + terms[2]) + terms[3]

        tot_ref[...] = ((o_ref[0] + g_ref[0].astype(F32)) + g_ref[1].astype(F32)) + g_ref[2].astype(F32)

    ins = [pl.BlockSpec((1, rb, C), lambda i, pr: (pr[0], i, 0)), pl.BlockSpec((3, rb, C), lambda i, pr: (0, i, 0))]
    outs = [pl.BlockSpec((rb, C), lambda i, pr: (pr[1] * SUM_BLOCKS + i, 0))]
    shapes = [jax.ShapeDtypeStruct((2 * H, C), F32)]
    args = [pf, got]
    if small4 is not None:
        ins += [pl.BlockSpec(small4.shape, lambda i, pr: (0, 0, 0)), pl.BlockSpec(psmall.shape, lambda i, pr: (0, 0))]
        outs += [pl.BlockSpec(psmall.shape, lambda i, pr: (0, 0))]
        shapes += [jax.ShapeDtypeStruct(psmall.shape, F32)]
        args += [small4, psmall]
    return _pc(
        body, name=name, out_shape=shapes,
        grid_spec=pltpu.PrefetchScalarGridSpec(num_scalar_prefetch=1, grid=(SUM_BLOCKS,), in_specs=ins, out_specs=outs),
        compiler_params=_cparams(("arbitrary",)),
    )(place, *args)


def _half_exchange(fulls):
    n = len(fulls)

    def body(*refs):
        in_refs, out_refs = refs[:n], refs[n:2 * n]
        send_sems, recv_sems = refs[2 * n:]
        x, y, c, _ = _place()
        sib = (x, y, 1 - c)
        for a in range(n):
            h = fulls[a].shape[0] // 2
            for r0, nr in _d2d_pieces(h):
                rows = pl.ds(c * h + r0, nr)
                _remote(in_refs[a].at[rows, :], out_refs[a].at[rows, :], send_sems, recv_sems, a, sib).start()
        for a in range(n):
            h = fulls[a].shape[0] // 2
            _remote(in_refs[a].at[pl.ds(c * h, h), :], out_refs[a].at[pl.ds((1 - c) * h, h), :], send_sems, recv_sems, a,
                    sib).wait()

    return _pc(
        body, name="half_exchange", in_specs=[_hbm_spec()] * n, out_specs=[_hbm_spec()] * n,
        out_shape=[jax.ShapeDtypeStruct(f.shape, F32) for f in fulls],
        input_output_aliases={a: a for a in range(n)},
        scratch_shapes=[pltpu.SemaphoreType.DMA((n,)), pltpu.SemaphoreType.DMA((n,))],
    )(*fulls)


def _adamw(w, g, m, v, name, g_row0=0, with_grad=False):
    R, C = w.shape
    rb = 256 if R % 256 == 0 else (128 if R % 128 == 0 else R)
    assert g_row0 % rb == 0

    def body(w_ref, g_ref, m_ref, v_ref, d_ref, nm_ref, nv_ref, *g_out):
        gv = g_ref[...]
        m2 = ADAM_B1 * m_ref[...] + (1.0 - ADAM_B1) * gv
        v2 = ADAM_B2 * v_ref[...] + (1.0 - ADAM_B2) * (gv * gv)
        m_hat = m2 / (1.0 - ADAM_B1 ** ADAM_STEP)
        v_hat = v2 / (1.0 - ADAM_B2 ** ADAM_STEP)
        d_ref[...] = -ADAM_LR * (m_hat / (jnp.sqrt(v_hat) + ADAM_EPS) + ADAM_WD * w_ref[...])
        nm_ref[...] = m2
        nv_ref[...] = v2
        if with_grad:
            g_out[0][...] = gv

    spec = pl.BlockSpec((rb, C), lambda i: (i, 0))
    g_spec = pl.BlockSpec((rb, C), lambda i: (g_row0 // rb + i, 0))
    n_out = 4 if with_grad else 3
    return _pc(
        body, name=name, grid=(R // rb,), in_specs=[spec, g_spec, spec, spec], out_specs=[spec] * n_out,
        out_shape=[jax.ShapeDtypeStruct((R, C), F32)] * n_out, compiler_params=_cparams(("parallel",)),
    )(w, g, m, v)


PACK_COLS = 1024
ROWS_A = (("w_mlp_in", 0, 1024), ("w_mlp_out", 1024, 1024), ("w_out", 2048, 256), ("w_branch", 2304, 384))
ROWS_A_TOTAL = 2688
W_IN_SHARD = 1412
CONV_PAD_ROWS = 16
SMALL = (("norm_mix_g", (1024,)), ("conv_b", (2048,)), ("dt_bias", (16,)), ("a_log", (16,)), ("d_ssd", (16,)),
         ("ssd_norm_g", (1024,)), ("s5_a_re", (32, 64)), ("s5_a_im", (32, 64)), ("s5_log_dt", (32,)),
         ("s5_b_re", (32, 64, 16)), ("s5_b_im", (32, 64, 16)), ("s5_c_re", (32, 16, 64)), ("s5_c_im", (32, 16, 64)),
         ("s5_d", (512,)), ("s5_glu_b", (512,)), ("norm_mlp_g", (1024,)), ("norm_final_g", (1024,)))
SMALL_ROWS = 144
GLU_ROWS = S5_WIDTH * S5_WIDTH // PACK_COLS
CONV_ROWS = CONV_K * CONV_DIM // PACK_COLS
W_IN_PIECES = (("z", 0, 1024), ("xbc", 1024, 2048), ("dt", OFF_DT, 16), ("u5", OFF_U, 512), ("gates", 3600, 2048))


def _pack_small(parts):
    flat = jnp.concatenate([a.astype(F32).reshape(-1) for a in parts])
    return jnp.concatenate([flat, jnp.zeros((SMALL_ROWS * PACK_COLS - flat.shape[0],), F32)]).reshape(SMALL_ROWS, PACK_COLS)


def _unpack_small(pack):
    flat, out, r = pack.reshape(-1), {}, 0
    for name, shp in SMALL:
        n = math.prod(shp)
        out[name] = flat[r:r + n].reshape(shp)
        r += n
    return out


def _column_range(pieces, lo, hi):
    out = []
    for c0, a in pieces:
        a0, a1 = max(lo, c0), min(hi, c0 + a.shape[-1])
        if a0 < a1:
            out.append(a[..., a0 - c0:a1 - c0])
    return out


def kernel(x, norm_mix_g, w_in, conv_w, conv_b, dt_bias, a_log, d_ssd, ssd_norm_g, s5_a_re, s5_a_im, s5_log_dt, s5_b_re, s5_b_im, s5_c_re, s5_c_im, s5_d, s5_glu_w, s5_glu_b, w_branch, w_out, norm_mlp_g, w_mlp_in, w_mlp_out, norm_final_g, loss_target, m_norm_mix_g, m_w_in, m_conv_w, m_conv_b, m_dt_bias, m_a_log, m_d_ssd, m_ssd_norm_g, m_s5_a_re, m_s5_a_im, m_s5_log_dt, m_s5_b_re, m_s5_b_im, m_s5_c_re, m_s5_c_im, m_s5_d, m_s5_glu_w, m_s5_glu_b, m_w_branch, m_w_out, m_norm_mlp_g, m_w_mlp_in, m_w_mlp_out, m_norm_final_g, v_norm_mix_g, v_w_in, v_conv_w, v_conv_b, v_dt_bias, v_a_log, v_d_ssd, v_ssd_norm_g, v_s5_a_re, v_s5_a_im, v_s5_log_dt, v_s5_b_re, v_s5_b_im, v_s5_c_re, v_s5_c_im, v_s5_d, v_s5_glu_w, v_s5_glu_b, v_w_branch, v_w_out, v_norm_mlp_g, v_w_mlp_in, v_w_mlp_out, v_norm_final_g):
    names = ("norm_mix_g", "w_in", "conv_w", "conv_b", "dt_bias", "a_log", "d_ssd", "ssd_norm_g", "s5_a_re", "s5_a_im",
             "s5_log_dt", "s5_b_re", "s5_b_im", "s5_c_re", "s5_c_im", "s5_d", "s5_glu_w", "s5_glu_b", "w_branch", "w_out",
             "norm_mlp_g", "w_mlp_in", "w_mlp_out", "norm_final_g")
    w = dict(zip(names, (norm_mix_g, w_in, conv_w, conv_b, dt_bias, a_log, d_ssd, ssd_norm_g, s5_a_re, s5_a_im, s5_log_dt,
                         s5_b_re, s5_b_im, s5_c_re, s5_c_im, s5_d, s5_glu_w, s5_glu_b, w_branch, w_out, norm_mlp_g,
                         w_mlp_in, w_mlp_out, norm_final_g)))
    m = dict(zip(names, (m_norm_mix_g, m_w_in, m_conv_w, m_conv_b, m_dt_bias, m_a_log, m_d_ssd, m_ssd_norm_g, m_s5_a_re,
                         m_s5_a_im, m_s5_log_dt, m_s5_b_re, m_s5_b_im, m_s5_c_re, m_s5_c_im, m_s5_d, m_s5_glu_w,
                         m_s5_glu_b, m_w_branch, m_w_out, m_norm_mlp_g, m_w_mlp_in, m_w_mlp_out, m_norm_final_g)))
    v = dict(zip(names, (v_norm_mix_g, v_w_in, v_conv_w, v_conv_b, v_dt_bias, v_a_log, v_d_ssd, v_ssd_norm_g, v_s5_a_re,
                         v_s5_a_im, v_s5_log_dt, v_s5_b_re, v_s5_b_im, v_s5_c_re, v_s5_c_im, v_s5_d, v_s5_glu_w,
                         v_s5_glu_b, v_w_branch, v_w_out, v_norm_mlp_g, v_w_mlp_in, v_w_mlp_out, v_norm_final_g)))

    cx, cy, cc = lax.axis_index("x"), lax.axis_index("y"), lax.axis_index("c")
    own = 2 * cx + cy
    place = jnp.stack([own, cc]).astype(jnp.int32)

    src_a = jnp.concatenate([w[n].astype(BF16) for n, _, _ in ROWS_A], axis=0)
    src_conv = jnp.concatenate([conv_w, jnp.zeros((CONV_PAD_ROWS - CONV_K, 512), F32)], axis=0)
    all_a, all_in, all_glu, all_conv = _allgather_chips(
        [src_a, w_in.astype(BF16), s5_glu_w.astype(BF16), src_conv], [ICI_CHUNKS, ICI_CHUNKS, 1, 1], "gather_weights")
    p = {n: w[n] for n, _ in SMALL}
    p["w_mlp_in"] = all_a[:, 0:1024]
    for n, r0, nr in ROWS_A[1:]:
        p[n] = all_a[:, r0:r0 + nr].reshape(N_CHIPS * nr, PACK_COLS)
    p["s5_glu_w"] = all_glu.reshape(S5_WIDTH, S5_WIDTH)
    p["conv_w"] = jnp.concatenate([all_conv[s, :CONV_K] for s in range(N_CHIPS)], axis=1)
    shards = [(W_IN_SHARD * s, all_in[s]) for s in range(N_CHIPS)]
    p["w_in_perm"] = jnp.concatenate(
        _column_range(shards, 0, OFF_DT) + _column_range(shards, OFF_U, D_IN_PROJ) + _column_range(shards, OFF_DT, OFF_U)
        + [jnp.zeros((D_MODEL, DT_PAD - 16), BF16)], axis=1)

    loss_part, grad_x, g = _local_step(x[0], loss_target[0], p)
    loss = lax.psum(loss_part, ("x", "y", "c"))

    g_a = jnp.concatenate([g["w_mlp_in"]] + [g[n].reshape(N_CHIPS, nr, PACK_COLS) for n, _, nr in ROWS_A[1:]], axis=1)
    g_in = jnp.stack([jnp.concatenate(_column_range(g["w_in_pieces"], W_IN_SHARD * s, W_IN_SHARD * (s + 1)), axis=1)
                      for s in range(N_CHIPS)])
    spack = jnp.concatenate([_pack_small([g[n] for n, _ in SMALL]), g["s5_glu_w"].reshape(GLU_ROWS, PACK_COLS),
                             g["conv_w"].reshape(CONV_ROWS, PACK_COLS)], axis=0)
    sib_a, sib_in, sib_small = _pair_exchange([g_a, g_in], spack)
    pf_a, pb_a, psmall = _pair_sum(place, g_a, sib_a, "pair_sum_a", spack, sib_small)
    pf_in, pb_in = _pair_sum(place, g_in, sib_in, "pair_sum_in")
    got_a, got_in, small4 = _chip_exchange([pb_a, pb_in], psmall)
    tot_a, small_tot = _chip_sum(place, pf_a, got_a, "chip_sum_a", small4, psmall)
    (tot_in,) = _chip_sum(place, pf_in, got_in, "chip_sum_in")
    red_a, red_in = _half_exchange([tot_a, tot_in])

    grads = _unpack_small(small_tot[:SMALL_ROWS])
    delta, new_m, new_v = {}, {}, {}
    for n, r0, _ in ROWS_A:
        delta[n], new_m[n], new_v[n], grads[n] = _adamw(w[n], red_a, m[n], v[n], "adamw_" + n, g_row0=r0, with_grad=True)
    grads["w_in"] = red_in
    glu_sum = small_tot[SMALL_ROWS:SMALL_ROWS + GLU_ROWS].reshape(S5_WIDTH, S5_WIDTH)
    grads["s5_glu_w"] = lax.dynamic_slice_in_dim(glu_sum, own * (S5_WIDTH // N_CHIPS), S5_WIDTH // N_CHIPS, axis=0)
    conv_sum = small_tot[SMALL_ROWS + GLU_ROWS:].reshape(CONV_K, CONV_DIM)
    grads["conv_w"] = lax.dynamic_slice_in_dim(conv_sum, own * (CONV_DIM // N_CHIPS), CONV_DIM // N_CHIPS, axis=1)
    for n in ("w_in", "s5_glu_w", "conv_w"):
        delta[n], new_m[n], new_v[n] = _adamw(w[n], grads[n], m[n], v[n], "adamw_" + n)
    ds, ms, vs = _adamw(_pack_small([w[n] for n, _ in SMALL]), small_tot, _pack_small([m[n] for n, _ in SMALL]),
                        _pack_small([v[n] for n, _ in SMALL]), "adamw_small")
    delta.update(_unpack_small(ds))
    new_m.update(_unpack_small(ms))
    new_v.update(_unpack_small(vs))

    return (loss, grad_x[None], *[grads[n] for n in names], *[delta[n] for n in names],
            *[new_m[n] for n in names], *[new_v[n] for n in names])
```

```python
import functools
import math

import jax
import jax.numpy as jnp
from jax import lax
from jax.experimental import pallas as pl
from jax.experimental.pallas import tpu as pltpu

F32 = jnp.float32
BF16 = jnp.bfloat16

D_MODEL = 1024
SSD_INNER = 1024
SSD_HEADS = 16
SSD_HEADDIM = 64
SSD_GROUPS = 4
SSD_HPG = 4
SSD_STATE = 128
SSD_CHUNK = 128
CONV_K = 4
CONV_DIM = 2048
S5_WIDTH = 512
S5_STATES = 2048
S5_BLOCKS = 4
S5_CHUNK = 128
D_FF = 4096
FF_SHARDS = 4
FF_SHARD = D_FF // FF_SHARDS
EPS = 1e-6
P_Z, P_XBC, P_U5, P_G, P_DT, P_END = 0, 1024, 3072, 3584, 5632, 5760
DT_PAD = 128
OFF_DT, OFF_U = 3072, 3088
D_IN_PROJ = 5648

ADAM_LR, ADAM_B1, ADAM_B2, ADAM_EPS, ADAM_WD, ADAM_STEP = 0.001, 0.9, 0.999, 1e-08, 0.01, 10

TOKEN_TILE = 256
VMEM_LIMIT = 56 * 1024 * 1024
HALO = 8
CONV_COLS = 256
CONV_ROWS_BLK = 64
WGRAD_TOKENS = 2048


def _pc(body, **kw):
    return pl.pallas_call(body, **kw)


def _cparams(sem=None):
    return pltpu.CompilerParams(dimension_semantics=sem, vmem_limit_bytes=VMEM_LIMIT)


def _dot(a, b):
    return jnp.dot(a, b, preferred_element_type=F32)


def _dot_nt(a, b):
    return lax.dot_general(a, b, (((1,), (1,)), ((), ())), preferred_element_type=F32)


def _dot_tn(a, b):
    return lax.dot_general(a, b, (((0,), (0,)), ((), ())), preferred_element_type=F32)


def _dot_hi(a, b, dims=(((1,), (0,)), ((), ()))):
    return lax.dot_general(a, b, dims, preferred_element_type=F32, precision=lax.Precision.HIGHEST)


def _split_bf16(x, terms):
    out = []
    for _ in range(terms - 1):
        t = x.astype(BF16)
        out.append(t)
        x = x - t.astype(F32)
    out.append(x.astype(BF16))
    return out


def _dot_split(x, onehots, terms, dims=(((1,), (0,)), ((), ()))):
    acc = None
    for t in _split_bf16(x, terms):
        p = lax.dot_general(t, onehots, dims, preferred_element_type=F32)
        acc = p if acc is None else acc + p
    return acc


def _dot_split_rhs(onehots, x, terms, dims=(((1,), (0,)), ((), ()))):
    acc = None
    for t in _split_bf16(x, terms):
        p = lax.dot_general(onehots, t, dims, preferred_element_type=F32)
        acc = p if acc is None else acc + p
    return acc


def _sigmoid(x):
    return 1.0 / (1.0 + jnp.exp(-x))


def _softplus(x):
    return jnp.maximum(x, 0.0) + jnp.log(1.0 + jnp.exp(-jnp.abs(x)))


_GELU_C = math.sqrt(2.0 / math.pi)


def _gelu(x):
    return 0.5 * x * (1.0 + jnp.tanh(_GELU_C * (x + 0.044715 * x * x * x)))


def _gelu_grad(x):
    t = jnp.tanh(_GELU_C * (x + 0.044715 * x * x * x))
    return 0.5 * (1.0 + t) + 0.5 * x * (1.0 - t * t) * _GELU_C * (1.0 + 3.0 * 0.044715 * x * x)


def _rms(x):
    r = lax.rsqrt(jnp.mean(x * x, axis=-1, keepdims=True) + EPS)
    return x * r, r


def _rms_bwd(xn, r, dxn):
    return r * (dxn - xn * jnp.mean(dxn * xn, axis=-1, keepdims=True))


def _row_spec(tm, width, col=0):
    return pl.BlockSpec((tm, width), lambda i: (i, col))


def _const_spec(shape):
    nd = len(shape)
    return pl.BlockSpec(shape, lambda i: (0,) * nd)


def _hbm_spec():
    return pl.BlockSpec(memory_space=pl.ANY)


def _inproj_fwd(x, g, wp):
    T = x.shape[0]
    tm = TOKEN_TILE

    def body(x_ref, g_ref, w_hbm, z_ref, xbc_ref, u5_ref, gt_ref, dt_ref, w_ref):
        @pl.when(pl.program_id(0) == 0)
        def _():
            pltpu.sync_copy(w_hbm, w_ref)

        xn, _ = _rms(x_ref[...])
        h = (xn * g_ref[...]).astype(BF16)
        z_ref[...] = _dot(h, w_ref[:, P_Z:P_XBC])
        xbc_ref[...] = _dot(h, w_ref[:, P_XBC:P_U5])
        u5_ref[...] = _dot(h, w_ref[:, P_U5:P_G])
        gt_ref[...] = _dot(h, w_ref[:, P_G:P_DT])
        dt_ref[...] = _dot(h, w_ref[:, P_DT:P_END])

    widths = (1024, 2048, 512, 2048, DT_PAD)
    return _pc(
        body, name="inproj_fwd", grid=(T // tm,),
        in_specs=[_row_spec(tm, D_MODEL), _const_spec((1, D_MODEL)), _hbm_spec()],
        out_specs=[_row_spec(tm, w) for w in widths],
        out_shape=[jax.ShapeDtypeStruct((T, w), F32) for w in widths],
        scratch_shapes=[pltpu.VMEM((D_MODEL, P_END), BF16)],
        compiler_params=_cparams(("arbitrary",)),
    )(x, g, wp)


def _inproj_bwd(x, dx1, dz, dxbc, du5, dgt, ddt, g, wp):
    T = x.shape[0]
    tm = TOKEN_TILE

    def body(x_ref, dx1_ref, dz_ref, dxbc_ref, du5_ref, dgt_ref, ddt_ref, g_ref, w_hbm, dx_ref, h_ref, dg_ref, w_ref):
        @pl.when(pl.program_id(0) == 0)
        def _():
            pltpu.sync_copy(w_hbm, w_ref)
            dg_ref[...] = jnp.zeros_like(dg_ref)

        xn, r = _rms(x_ref[...])
        gv = g_ref[...]
        h_ref[...] = (xn * gv).astype(BF16)
        dh = _dot_nt(dz_ref[...].astype(BF16), w_ref[:, P_Z:P_XBC])
        dh += _dot_nt(dxbc_ref[...].astype(BF16), w_ref[:, P_XBC:P_U5])
        dh += _dot_nt(du5_ref[...].astype(BF16), w_ref[:, P_U5:P_G])
        dh += _dot_nt(dgt_ref[...].astype(BF16), w_ref[:, P_G:P_DT])
        dh += _dot_nt(ddt_ref[...].astype(BF16), w_ref[:, P_DT:P_END])
        dg_ref[...] += jnp.sum(dh * xn, axis=0, keepdims=True)
        dx_ref[...] = dx1_ref[...] + _rms_bwd(xn, r, dh * gv)

    return _pc(
        body, name="inproj_bwd", grid=(T // tm,),
        in_specs=[_row_spec(tm, 1024), _row_spec(tm, 1024), _row_spec(tm, 1024), _row_spec(tm, 2048),
                  _row_spec(tm, 512), _row_spec(tm, 2048), _row_spec(tm, DT_PAD), _const_spec((1, 1024)), _hbm_spec()],
        out_specs=[_row_spec(tm, 1024), _row_spec(tm, 1024), _const_spec((1, 1024))],
        out_shape=[jax.ShapeDtypeStruct((T, 1024), F32), jax.ShapeDtypeStruct((T, 1024), BF16),
                   jax.ShapeDtypeStruct((1, 1024), F32)],
        scratch_shapes=[pltpu.VMEM((D_MODEL, P_END), BF16)],
        compiler_params=_cparams(("arbitrary",)),
    )(x, dx1, dz, dxbc, du5, dgt, ddt, g, wp)


def _conv_fwd(xbc_raw, dt_raw, conv_w, conv_b, dt_bias):
    T = xbc_raw.shape[0]
    tm = TOKEN_TILE

    def body(u_ref, dtr_ref, w_ref, b_ref, db_ref, act_ref, dt_ref, ext_ref):
        @pl.when(pl.program_id(0) == 0)
        def _():
            ext_ref[0:HALO, :] = jnp.zeros((HALO, CONV_DIM), F32)

        ext_ref[HALO:, :] = u_ref[...]
        for c0 in range(0, CONV_DIM, CONV_COLS):
            cols = slice(c0, c0 + CONV_COLS)
            taps = [w_ref[k:k + 1, cols] for k in range(CONV_K)]
            bias = b_ref[:, cols]
            for r0 in range(0, tm, CONV_ROWS_BLK):
                y = bias + taps[0] * ext_ref[pl.ds(HALO - (CONV_K - 1) + r0, CONV_ROWS_BLK), cols]
                for k in range(1, CONV_K):
                    y += taps[k] * ext_ref[pl.ds(HALO - (CONV_K - 1) + k + r0, CONV_ROWS_BLK), cols]
                act_ref[r0:r0 + CONV_ROWS_BLK, cols] = y * _sigmoid(y)
        ext_ref[0:HALO, :] = u_ref[tm - HALO:tm, :]
        dt_ref[...] = _softplus(dtr_ref[...] + db_ref[...])

    return _pc(
        body, name="conv_fwd", grid=(T // tm,),
        in_specs=[_row_spec(tm, CONV_DIM), _row_spec(tm, DT_PAD), _const_spec((CONV_K, CONV_DIM)),
                  _const_spec((1, CONV_DIM)), _const_spec((1, DT_PAD))],
        out_specs=[_row_spec(tm, CONV_DIM), _row_spec(tm, DT_PAD)],
        out_shape=[jax.ShapeDtypeStruct((T, CONV_DIM), F32), jax.ShapeDtypeStruct((T, DT_PAD), F32)],
        scratch_shapes=[pltpu.VMEM((tm + HALO, CONV_DIM), F32)],
        compiler_params=_cparams(("arbitrary",)),
    )(xbc_raw, dt_raw, conv_w, conv_b, dt_bias)


def _conv_bwd(xbc_raw, dt_raw, dxs_a, dxs_b, dB, dC, ddt, conv_w, conv_b, dt_bias):
    T = xbc_raw.shape[0]
    tm = TOKEN_TILE
    n = T // tm
    hb = tm // HALO

    def rev(width):
        return pl.BlockSpec((tm, width), lambda i: (n - 1 - i, 0))

    def body(u_ref, up_ref, dtr_ref, dxa_ref, dxb_ref, dB_ref, dC_ref, ddt_ref, w_ref, b_ref, db_ref,
             du_ref, ddtr_ref, dw_ref, dcb_ref, ddb_ref, ext_ref, dye_ref):
        i = pl.program_id(0)

        @pl.when(i == 0)
        def _():
            dye_ref[tm:, :] = jnp.zeros((HALO, CONV_DIM), F32)
            dw_ref[...] = jnp.zeros_like(dw_ref)
            dcb_ref[...] = jnp.zeros_like(dcb_ref)
            ddb_ref[...] = jnp.zeros_like(ddb_ref)

        first = (i == n - 1).astype(F32)
        ext_ref[0:HALO, :] = up_ref[...] * (1.0 - first)
        ext_ref[HALO:, :] = u_ref[...]
        for c0 in range(0, CONV_DIM, CONV_COLS):
            cols = slice(c0, c0 + CONV_COLS)
            taps = [w_ref[k:k + 1, cols] for k in range(CONV_K)]
            bias = b_ref[:, cols]
            acc_b = jnp.zeros((HALO, CONV_COLS), F32)
            acc_w = [jnp.zeros((HALO, CONV_COLS), F32) for _ in range(CONV_K)]
            for r0 in range(0, tm, CONV_ROWS_BLK):
                rows = slice(r0, r0 + CONV_ROWS_BLK)
                us = [ext_ref[pl.ds(HALO - (CONV_K - 1) + k + r0, CONV_ROWS_BLK), cols] for k in range(CONV_K)]
                y = bias + taps[0] * us[0]
                for k in range(1, CONV_K):
                    y += taps[k] * us[k]
                s = _sigmoid(y)
                if c0 < SSD_INNER:
                    dact = dxa_ref[rows, cols] + dxb_ref[rows, cols]
                elif c0 < SSD_INNER + 512:
                    dact = dB_ref[rows, c0 - SSD_INNER:c0 - SSD_INNER + CONV_COLS]
                else:
                    dact = dC_ref[rows, c0 - SSD_INNER - 512:c0 - SSD_INNER - 512 + CONV_COLS]
                dy = dact * (s * (1.0 + y * (1.0 - s)))
                dye_ref[rows, cols] = dy
                acc_b += jnp.sum(dy.reshape(CONV_ROWS_BLK // HALO, HALO, CONV_COLS), axis=0)
                for k in range(CONV_K):
                    acc_w[k] += jnp.sum((dy * us[k]).reshape(CONV_ROWS_BLK // HALO, HALO, CONV_COLS), axis=0)
            dcb_ref[:, cols] += jnp.sum(acc_b, axis=0, keepdims=True)
            for k in range(CONV_K):
                dw_ref[k:k + 1, cols] += jnp.sum(acc_w[k], axis=0, keepdims=True)
        for c0 in range(0, CONV_DIM, CONV_COLS):
            cols = slice(c0, c0 + CONV_COLS)
            taps = [w_ref[k:k + 1, cols] for k in range(CONV_K)]
            for r0 in range(0, tm, CONV_ROWS_BLK):
                du = taps[0] * dye_ref[pl.ds(CONV_K - 1 + r0, CONV_ROWS_BLK), cols]
                for k in range(1, CONV_K):
                    du += taps[k] * dye_ref[pl.ds(CONV_K - 1 - k + r0, CONV_ROWS_BLK), cols]
                du_ref[r0:r0 + CONV_ROWS_BLK, cols] = du
        dye_ref[tm:, :] = dye_ref[0:HALO, :]
        sg = _sigmoid(dtr_ref[...] + db_ref[...])
        ddtr = ddt_ref[...] * sg
        ddtr_ref[...] = ddtr
        ddb_ref[...] += jnp.sum(ddtr, axis=0, keepdims=True)

    prev_spec = pl.BlockSpec((HALO, CONV_DIM), lambda i: (jnp.maximum((n - 1 - i) * hb - 1, 0), 0))
    return _pc(
        body, name="conv_bwd", grid=(n,),
        in_specs=[rev(CONV_DIM), prev_spec, rev(DT_PAD), rev(1024), rev(1024), rev(512), rev(512), rev(DT_PAD),
                  _const_spec((CONV_K, CONV_DIM)), _const_spec((1, CONV_DIM)), _const_spec((1, DT_PAD))],
        out_specs=[rev(CONV_DIM), rev(DT_PAD), _const_spec((HALO, CONV_DIM)), _const_spec((1, CONV_DIM)),
                   _const_spec((1, DT_PAD))],
        out_shape=[jax.ShapeDtypeStruct((T, CONV_DIM), F32), jax.ShapeDtypeStruct((T, DT_PAD), F32),
                   jax.ShapeDtypeStruct((HALO, CONV_DIM), F32), jax.ShapeDtypeStruct((1, CONV_DIM), F32),
                   jax.ShapeDtypeStruct((1, DT_PAD), F32)],
        scratch_shapes=[pltpu.VMEM((tm + HALO, CONV_DIM), F32), pltpu.VMEM((tm + HALO, CONV_DIM), F32)],
        compiler_params=_cparams(("arbitrary",)),
    )(xbc_raw, xbc_raw, dt_raw, dxs_a, dxs_b, dB, dC, ddt, conv_w, conv_b, dt_bias)


GROUP_LANES = SSD_HPG * SSD_HEADDIM


def _ssd_expanders():
    head = jnp.arange(DT_PAD)[:, None]
    to_wide = (jnp.arange(SSD_INNER)[None, :] // SSD_HEADDIM == head).astype(BF16)
    to_cols = (jnp.arange(SSD_HEADS * SSD_CHUNK)[None, :] // SSD_CHUNK == head).astype(BF16)
    return to_wide, to_wide.T, to_cols


def _ssd_prep(dt_ref, alog_ref, wide_ref, cols_ref):
    q = SSD_CHUNK
    a = -jnp.exp(alog_ref[...])
    dtv = dt_ref[...]
    la = dtv * a
    row = lax.broadcasted_iota(jnp.int32, (q, q), 0)
    col = lax.broadcasted_iota(jnp.int32, (q, q), 1)
    tri = (col <= row).astype(BF16)
    cum = _dot_split_rhs(tri, la, 3)
    cum_t = _dot_split(la, tri, 3, (((0,), (1,)), ((), ())))
    dtw = _dot_split(dtv, wide_ref[...], 2)
    cumw = _dot_split(cum, wide_ref[...], 3)
    segcol = _dot_split(cum, cols_ref[...], 3)
    return a, dtv, row, col, tri, cum_t, dtw, cumw, segcol


def _decay(segcol, cum_t, h, keep):
    return jnp.where(keep, jnp.exp(jnp.minimum(segcol[:, 128 * h:128 * h + 128] - cum_t[h:h + 1, :], 0.0)), 0.0)


def _decay_t(segcol, cum_t, h, keep_t):
    return jnp.where(keep_t, jnp.exp(jnp.minimum(cum_t[h:h + 1, :] - segcol[:, 128 * h:128 * h + 128], 0.0)), 0.0)


def _ssd_fwd(xbc_act, dt, alog):
    T = xbc_act.shape[0]
    q = SSD_CHUNK
    nc = T // q
    to_wide, _, to_cols = _ssd_expanders()

    def body(xbc_ref, dt_ref, alog_ref, wide_ref, cols_ref, y_ref, sp_ref, st_ref, xd_ref, xde_ref):
        @pl.when(pl.program_id(0) == 0)
        def _():
            st_ref[...] = jnp.zeros_like(st_ref)

        a, dtv, row, col, tri, cum_t, dtw, cumw, segcol = _ssd_prep(dt_ref, alog_ref, wide_ref, cols_ref)
        clw = cumw[q - 1:q, :]
        ecw = jnp.exp(cumw)
        xd = xbc_ref[:, 0:SSD_INNER] * dtw
        xd_ref[...] = xd.astype(BF16)
        xde_ref[...] = (xd * jnp.exp(clw - cumw)).astype(BF16)
        cdw = jnp.exp(clw)
        keep = col <= row
        sp_ref[0] = st_ref[...]
        for g in range(SSD_GROUPS):
            gl = slice(GROUP_LANES * g, GROUP_LANES * (g + 1))
            bb = xbc_ref[:, 1024 + 128 * g:1152 + 128 * g].astype(BF16)
            cb = xbc_ref[:, 1536 + 128 * g:1664 + 128 * g].astype(BF16)
            gm = _dot_nt(cb, bb)
            stp = st_ref[g]
            yoff = _dot(cb, stp.astype(BF16)) * ecw[:, gl]
            for r in range(SSD_HPG):
                h = SSD_HPG * g + r
                m = (gm * _decay(segcol, cum_t, h, keep)).astype(BF16)
                y_ref[:, 64 * h:64 * h + 64] = _dot(m, xd_ref[:, 64 * h:64 * h + 64]) + yoff[:, 64 * r:64 * r + 64]
            st_ref[g] = stp * cdw[:, gl] + _dot_tn(bb, xde_ref[:, gl])

    return _pc(
        body, name="ssd_fwd", grid=(nc,),
        in_specs=[_row_spec(q, CONV_DIM), _row_spec(q, DT_PAD), _const_spec((1, DT_PAD)),
                  _const_spec(to_wide.shape), _const_spec(to_cols.shape)],
        out_specs=[_row_spec(q, SSD_INNER),
                   pl.BlockSpec((1, SSD_GROUPS, SSD_STATE, GROUP_LANES), lambda i: (i, 0, 0, 0))],
        out_shape=[jax.ShapeDtypeStruct((T, SSD_INNER), F32),
                   jax.ShapeDtypeStruct((nc, SSD_GROUPS, SSD_STATE, GROUP_LANES), F32)],
        scratch_shapes=[pltpu.VMEM((SSD_GROUPS, SSD_STATE, GROUP_LANES), F32), pltpu.VMEM((q, SSD_INNER), BF16),
                        pltpu.VMEM((q, SSD_INNER), BF16)],
        compiler_params=_cparams(("arbitrary",)),
    )(xbc_act, dt, alog, to_wide, to_cols)


def _ssd_bwd(xbc_act, dt, alog, sprev, dy):
    T = xbc_act.shape[0]
    q = SSD_CHUNK
    nc = T // q
    to_wide, to_heads, to_cols = _ssd_expanders()

    def rev(width):
        return pl.BlockSpec((q, width), lambda i: (nc - 1 - i, 0))

    def body(xbc_ref, dt_ref, alog_ref, sp_ref, dy_ref, wide_ref, heads_ref, cols_ref,
             dxs_ref, dB_ref, dC_ref, ddt_ref, dalog_ref, ds_ref, xd_ref, dxd_ref):
        i = pl.program_id(0)

        @pl.when(i == 0)
        def _():
            ds_ref[...] = jnp.zeros_like(ds_ref)
            dalog_ref[...] = jnp.zeros_like(dalog_ref)

        a, dtv, row, col, tri, cum_t, dtw, cumw, segcol = _ssd_prep(dt_ref, alog_ref, wide_ref, cols_ref)
        clw = cumw[q - 1:q, :]
        ecw = jnp.exp(cumw)
        dew = jnp.exp(clw - cumw)
        cdw = jnp.exp(clw)
        xs = xbc_ref[:, 0:SSD_INNER]
        xd = xs * dtw
        xd_ref[...] = xd.astype(BF16)
        dyv = dy_ref[...]
        dye = (dyv * ecw).astype(BF16)
        xde = (xd * dew).astype(BF16)
        keep = col <= row
        keep_t = col >= row
        rows_k = lax.broadcasted_iota(jnp.int32, (SSD_HPG * q, DT_PAD), 0) // q
        lanes_k = lax.broadcasted_iota(jnp.int32, (SSD_HPG * q, DT_PAD), 1)
        dcw_parts = []
        dcum = jnp.zeros((q, DT_PAD), F32)
        for g in range(SSD_GROUPS):
            gl = slice(GROUP_LANES * g, GROUP_LANES * (g + 1))
            bb = xbc_ref[:, 1024 + 128 * g:1152 + 128 * g].astype(BF16)
            cb = xbc_ref[:, 1536 + 128 * g:1664 + 128 * g].astype(BF16)
            gm = _dot_nt(cb, bb)
            gmt = _dot_nt(bb, cb)
            stp = sp_ref[0, g]
            dst = ds_ref[g]
            stpb = stp.astype(BF16)
            dstb = dst.astype(BF16)
            yoff = _dot(cb, stpb) * ecw[:, gl]
            dcg = _dot_nt(dye[:, gl], stpb)
            ds_ref[g] = dst * cdw[:, gl] + _dot_tn(cb, dye[:, gl])
            dlast = jnp.sum(dst * stp, axis=0, keepdims=True) * cdw[:, gl]
            dbg = _dot_nt(xde[:, gl], dstb)
            w = _dot(bb, dstb) * dew[:, gl]
            wx = w * xd[:, gl]
            dlast = dlast + jnp.sum(wx, axis=0, keepdims=True)
            dcw_parts.append(dyv[:, gl] * yoff - wx
                             + jnp.where(lax.broadcasted_iota(jnp.int32, (q, 1), 0) == q - 1, dlast, 0.0))
            dgm = jnp.zeros((q, q), F32)
            diag = []
            for r in range(SSD_HPG):
                h = SSD_HPG * g + r
                hl = slice(64 * h, 64 * h + 64)
                dyb = dy_ref[:, hl].astype(BF16)
                xdh = xd_ref[:, hl]
                dm = _dot_nt(dyb, xdh)
                dmt = _dot_nt(xdh, dyb)
                dec = _decay(segcol, cum_t, h, keep)
                mt = gmt * _decay_t(segcol, cum_t, h, keep_t)
                dgm += dm * dec
                diag.append(dm * (gm * dec) - dmt * mt)
                dxd_ref[:, hl] = _dot(mt.astype(BF16), dyb) + w[:, 64 * r:64 * r + 64]
            onehots = (lanes_k == SSD_HPG * g + rows_k).astype(BF16)
            dcum += _dot_split(jnp.concatenate(diag, axis=1), onehots, 2)
            dgb = dgm.astype(BF16)
            dC_ref[:, 128 * g:128 * g + 128] = dcg + _dot(dgb, bb)
            dB_ref[:, 128 * g:128 * g + 128] = dbg + _dot_tn(dgb, cb)
        dxd = dxd_ref[...]
        dxs_ref[...] = dxd * dtw
        dcum += _dot_split(jnp.concatenate(dcw_parts, axis=1), heads_ref[...], 2)
        dla = _dot_split_rhs(tri, dcum, 3, (((0,), (0,)), ((), ())))
        ddt_ref[...] = _dot_split(xs * dxd, heads_ref[...], 2) + dla * a
        dalog_ref[...] += jnp.sum(dla * dtv, axis=0, keepdims=True)

        @pl.when(i == nc - 1)
        def _():
            dalog_ref[...] = dalog_ref[...] * a

    st_spec = pl.BlockSpec((1, SSD_GROUPS, SSD_STATE, GROUP_LANES), lambda i: (nc - 1 - i, 0, 0, 0))
    return _pc(
        body, name="ssd_bwd", grid=(nc,),
        in_specs=[rev(CONV_DIM), rev(DT_PAD), _const_spec((1, DT_PAD)), st_spec, rev(SSD_INNER),
                  _const_spec(to_wide.shape), _const_spec(to_heads.shape), _const_spec(to_cols.shape)],
        out_specs=[rev(SSD_INNER), rev(512), rev(512), rev(DT_PAD), _const_spec((1, DT_PAD))],
        out_shape=[jax.ShapeDtypeStruct((T, SSD_INNER), F32), jax.ShapeDtypeStruct((T, 512), F32),
                   jax.ShapeDtypeStruct((T, 512), F32), jax.ShapeDtypeStruct((T, DT_PAD), F32),
                   jax.ShapeDtypeStruct((1, DT_PAD), F32)],
        scratch_shapes=[pltpu.VMEM((SSD_GROUPS, SSD_STATE, GROUP_LANES), F32), pltpu.VMEM((q, SSD_INNER), BF16),
                        pltpu.VMEM((q, SSD_INNER), F32)],
        compiler_params=_cparams(("arbitrary",)),
    )(xbc_act, dt, alog, sprev, dy, to_wide, to_heads, to_cols)


def _s5_disc_vals(a_re, a_im, log_dt, b_re, b_im):
    dt = jnp.exp(log_dt)
    mag = jnp.exp(a_re * dt)
    ab_re = mag * jnp.cos(a_im * dt)
    ab_im = mag * jnp.sin(a_im * dt)
    den = a_re * a_re + a_im * a_im
    nr = ab_re - 1.0
    ni = ab_im
    coef_re = (nr * a_re + ni * a_im) / den
    coef_im = (ni * a_re - nr * a_im) / den
    bb_re = coef_re * b_re - coef_im * b_im
    bb_im = coef_re * b_im + coef_im * b_re
    return ab_re, ab_im, bb_re, bb_im


def _s5_disc(a_re, a_im, log_dt, b_re, b_im):
    def body(ar, ai, ld, br, bi, o1, o2, o3, o4):
        o1[...], o2[...], o3[...], o4[...] = _s5_disc_vals(ar[...], ai[...], ld[...], br[...], bi[...])

    return _pc(
        body, name="s5_disc",
        out_shape=[jax.ShapeDtypeStruct((S5_STATES, 1), F32), jax.ShapeDtypeStruct((S5_STATES, 1), F32),
                   jax.ShapeDtypeStruct((S5_STATES, 16), F32), jax.ShapeDtypeStruct((S5_STATES, 16), F32)],
    )(a_re, a_im, log_dt, b_re, b_im)


def _s5_disc_bwd(a_re, a_im, log_dt, b_re, b_im, d_ab_re, d_ab_im, d_bb_re, d_bb_im):
    def body(ar, ai, ld, br, bi, g1, g2, g3, g4, o1, o2, o3, o4, o5):
        _, vjp = jax.vjp(_s5_disc_vals, ar[...], ai[...], ld[...], br[...], bi[...])
        d1, d2, d3, d4, d5 = vjp((g1[...], g2[...], g3[...], g4[...]))
        o1[...] = d1
        o2[...] = d2
        grp = lax.broadcasted_iota(jnp.int32, (32, S5_STATES), 0)
        st = lax.broadcasted_iota(jnp.int32, (32, S5_STATES), 1)
        sel = (st // 64 == grp).astype(F32)
        o3[...] = _dot_hi(sel, d3)
        o4[...] = d4
        o5[...] = d5

    return _pc(
        body, name="s5_disc_bwd",
        out_shape=[jax.ShapeDtypeStruct((S5_STATES, 1), F32), jax.ShapeDtypeStruct((S5_STATES, 1), F32),
                   jax.ShapeDtypeStruct((32, 1), F32),
                   jax.ShapeDtypeStruct((S5_STATES, 16), F32), jax.ShapeDtypeStruct((S5_STATES, 16), F32)],
    )(a_re, a_im, log_dt, b_re, b_im, d_ab_re, d_ab_im, d_bb_re, d_bb_im)


def _cmul_add(xr, xi, pr, pi, yr, yi):
    return xr + pr * yr - pi * yi, xi + pr * yi + pi * yr


def _powers(ar, ai, n):
    out = [(ar, ai)]
    for _ in range(n - 1):
        pr, pi = out[-1]
        out.append((pr * pr - pi * pi, 2.0 * pr * pi))
    return out


_BW = S5_STATES // S5_BLOCKS
_BI = S5_WIDTH // S5_BLOCKS
SUB = 8
S5_ROWS = S5_CHUNK // SUB


def _scan8(br, bi, pws, rowin, reverse):
    k = 1
    for pr, pi in pws:
        if reverse:
            keep = rowin < SUB - k
            sr = jnp.where(keep, pltpu.roll(br, SUB - k, 0), 0.0)
            si = jnp.where(keep, pltpu.roll(bi, SUB - k, 0), 0.0)
        else:
            keep = rowin >= k
            sr = jnp.where(keep, pltpu.roll(br, k, 0), 0.0)
            si = jnp.where(keep, pltpu.roll(bi, k, 0), 0.0)
        br, bi = _cmul_add(br, bi, pr, pi, sr, si)
        k *= 2
    return br, bi


def _s5_tables(ab_ref, tab_ref, reverse):
    rowin = lax.broadcasted_iota(jnp.int32, (SUB, 1), 0)
    ar = ab_ref[0:1, :]
    ai = -ab_ref[1:2, :] if reverse else ab_ref[1:2, :]
    hit = rowin == (SUB - 1 if reverse else 0)
    zero = jnp.zeros((SUB, S5_STATES), F32)
    pr, pi = _scan8(jnp.where(hit, ar, 0.0) + zero, jnp.where(hit, ai, 0.0) + zero, _powers(ar, ai, 3), rowin, reverse)
    tab_ref[0:SUB, :] = pr
    tab_ref[SUB:2 * SUB, :] = pi


def _s5_fwd(u5, wb4, wc4, ab, dvec):
    T = u5.shape[0]
    q = S5_CHUNK
    nc = T // q

    def body(u_ref, wb_ref, wc_ref, ab_ref, d_ref, y_ref, sp_ref, carry_ref, tab_ref, sr_ref, si_ref):
        i = pl.program_id(0)
        rowin = lax.broadcasted_iota(jnp.int32, (SUB, 1), 0)

        @pl.when(i == 0)
        def _():
            carry_ref[...] = jnp.zeros_like(carry_ref)
            _s5_tables(ab_ref, tab_ref, False)

        sp_ref[0] = carry_ref[...]
        for j in range(S5_BLOCKS):
            bu = _dot(u_ref[:, _BI * j:_BI * (j + 1)].astype(BF16), wb_ref[j])
            sr_ref[:, :, _BW * j:_BW * (j + 1)] = bu[:, :_BW].reshape(S5_ROWS, SUB, _BW)
            si_ref[:, :, _BW * j:_BW * (j + 1)] = bu[:, _BW:].reshape(S5_ROWS, SUB, _BW)
        pws = _powers(ab_ref[0:1, :], ab_ref[1:2, :], 3)
        tr, ti = tab_ref[0:SUB, :], tab_ref[SUB:2 * SUB, :]
        cr, ci = carry_ref[0:1, :], carry_ref[1:2, :]
        for k in range(S5_ROWS):
            sr, si = _scan8(sr_ref[k], si_ref[k], pws, rowin, False)
            sr, si = _cmul_add(sr, si, tr, ti, cr, ci)
            sr_ref[k] = sr
            si_ref[k] = si
            cr, ci = sr[SUB - 1:SUB, :], si[SUB - 1:SUB, :]
        carry_ref[0:1, :] = cr
        carry_ref[1:2, :] = ci
        for j in range(S5_BLOCKS):
            sl = slice(_BW * j, _BW * (j + 1))
            ul = slice(_BI * j, _BI * (j + 1))
            s = jnp.concatenate([sr_ref[:, :, sl].reshape(q, _BW), si_ref[:, :, sl].reshape(q, _BW)], axis=1).astype(BF16)
            y_ref[:, ul] = _dot(s, wc_ref[j]) + d_ref[:, ul] * u_ref[:, ul]

    return _pc(
        body, name="s5_fwd", grid=(nc,),
        in_specs=[_row_spec(q, S5_WIDTH), _const_spec((S5_BLOCKS, _BI, 2 * _BW)), _const_spec((S5_BLOCKS, 2 * _BW, _BI)),
                  _const_spec((8, S5_STATES)), _const_spec((1, S5_WIDTH))],
        out_specs=[_row_spec(q, S5_WIDTH), pl.BlockSpec((1, 8, S5_STATES), lambda i: (i, 0, 0))],
        out_shape=[jax.ShapeDtypeStruct((T, S5_WIDTH), F32), jax.ShapeDtypeStruct((nc, 8, S5_STATES), F32)],
        scratch_shapes=[pltpu.VMEM((8, S5_STATES), F32), pltpu.VMEM((2 * SUB, S5_STATES), F32),
                        pltpu.VMEM((S5_ROWS, SUB, S5_STATES), F32), pltpu.VMEM((S5_ROWS, SUB, S5_STATES), F32)],
        compiler_params=_cparams(("arbitrary",)),
    )(u5, wb4, wc4, ab, dvec)


def _s5_bwd(u5, dy5, wb4, wc4, ab, dvec, sprev):
    T = u5.shape[0]
    q = S5_CHUNK
    nc = T // q

    def rev(width):
        return pl.BlockSpec((q, width), lambda i: (nc - 1 - i, 0))

    def body(u_ref, dy_ref, wb_ref, wc_ref, ab_ref, d_ref, sp_ref, du_ref, dwb_ref, dwc_ref, dab_ref, dd_ref,
             carry_ref, tab_ref, rtab_ref, sr_ref, si_ref, lr_ref, li_ref):
        i = pl.program_id(0)
        rowin = lax.broadcasted_iota(jnp.int32, (SUB, 1), 0)

        @pl.when(i == 0)
        def _():
            carry_ref[...] = jnp.zeros_like(carry_ref)
            dwb_ref[...] = jnp.zeros_like(dwb_ref)
            dwc_ref[...] = jnp.zeros_like(dwc_ref)
            dab_ref[...] = jnp.zeros_like(dab_ref)
            dd_ref[...] = jnp.zeros_like(dd_ref)
            _s5_tables(ab_ref, tab_ref, False)
            _s5_tables(ab_ref, rtab_ref, True)

        for j in range(S5_BLOCKS):
            sl = slice(_BW * j, _BW * (j + 1))
            ul = slice(_BI * j, _BI * (j + 1))
            bu = _dot(u_ref[:, ul].astype(BF16), wb_ref[j])
            sr_ref[:, :, sl] = bu[:, :_BW].reshape(S5_ROWS, SUB, _BW)
            si_ref[:, :, sl] = bu[:, _BW:].reshape(S5_ROWS, SUB, _BW)
            ds = _dot_nt(dy_ref[:, ul].astype(BF16), wc_ref[j])
            lr_ref[:, :, sl] = ds[:, :_BW].reshape(S5_ROWS, SUB, _BW)
            li_ref[:, :, sl] = ds[:, _BW:].reshape(S5_ROWS, SUB, _BW)
        ar, ai = ab_ref[0:1, :], ab_ref[1:2, :]
        pws = _powers(ar, ai, 3)
        tr, ti = tab_ref[0:SUB, :], tab_ref[SUB:2 * SUB, :]
        cr, ci = sp_ref[0, 0:1, :], sp_ref[0, 1:2, :]
        for k in range(S5_ROWS):
            sr, si = _scan8(sr_ref[k], si_ref[k], pws, rowin, False)
            sr, si = _cmul_add(sr, si, tr, ti, cr, ci)
            sr_ref[k] = sr
            si_ref[k] = si
            cr, ci = sr[SUB - 1:SUB, :], si[SUB - 1:SUB, :]
        pws = _powers(ar, -ai, 3)
        tr, ti = rtab_ref[0:SUB, :], rtab_ref[SUB:2 * SUB, :]
        cr, ci = carry_ref[0:1, :], carry_ref[1:2, :]
        acc_r = jnp.zeros((SUB, S5_STATES), F32)
        acc_i = jnp.zeros((SUB, S5_STATES), F32)
        for k in reversed(range(S5_ROWS)):
            lr, li = _scan8(lr_ref[k], li_ref[k], pws, rowin, True)
            lr, li = _cmul_add(lr, li, tr, ti, cr, ci)
            lr_ref[k] = lr
            li_ref[k] = li
            cr, ci = lr[0:1, :], li[0:1, :]
            if k > 0:
                before_r, before_i = sr_ref[k - 1, SUB - 1:SUB, :], si_ref[k - 1, SUB - 1:SUB, :]
            else:
                before_r, before_i = sp_ref[0, 0:1, :], sp_ref[0, 1:2, :]
            keep = rowin >= 1
            pr = jnp.where(keep, pltpu.roll(sr_ref[k], 1, 0), before_r)
            pi = jnp.where(keep, pltpu.roll(si_ref[k], 1, 0), before_i)
            acc_r += lr * pr + li * pi
            acc_i += li * pr - lr * pi
        carry_ref[0:1, :] = cr
        carry_ref[1:2, :] = ci
        dab_ref[0:1, :] += jnp.sum(acc_r, axis=0, keepdims=True)
        dab_ref[1:2, :] += jnp.sum(acc_i, axis=0, keepdims=True)
        for j in range(S5_BLOCKS):
            sl = slice(_BW * j, _BW * (j + 1))
            ul = slice(_BI * j, _BI * (j + 1))
            u = u_ref[:, ul]
            dy = dy_ref[:, ul]
            dyb = dy.astype(BF16)
            lam = jnp.concatenate([lr_ref[:, :, sl].reshape(q, _BW), li_ref[:, :, sl].reshape(q, _BW)], axis=1).astype(BF16)
            s = jnp.concatenate([sr_ref[:, :, sl].reshape(q, _BW), si_ref[:, :, sl].reshape(q, _BW)], axis=1).astype(BF16)
            du_ref[:, ul] = _dot_nt(lam, wb_ref[j]) + d_ref[:, ul] * dy
            dwb_ref[j] += _dot_tn(u.astype(BF16), lam)
            dwc_ref[j] += _dot_tn(s, dyb)
            dd_ref[:, ul] += jnp.sum(dy * u, axis=0, keepdims=True)

    big = pltpu.VMEM((S5_ROWS, SUB, S5_STATES), F32)
    return _pc(
        body, name="s5_bwd", grid=(nc,),
        in_specs=[rev(S5_WIDTH), rev(S5_WIDTH), _const_spec((S5_BLOCKS, _BI, 2 * _BW)), _const_spec((S5_BLOCKS, 2 * _BW, _BI)),
                  _const_spec((8, S5_STATES)), _const_spec((1, S5_WIDTH)),
                  pl.BlockSpec((1, 8, S5_STATES), lambda i: (nc - 1 - i, 0, 0))],
        out_specs=[rev(S5_WIDTH), _const_spec((S5_BLOCKS, _BI, 2 * _BW)), _const_spec((S5_BLOCKS, 2 * _BW, _BI)),
                   _const_spec((8, S5_STATES)), _const_spec((1, S5_WIDTH))],
        out_shape=[jax.ShapeDtypeStruct((T, S5_WIDTH), F32), jax.ShapeDtypeStruct((S5_BLOCKS, _BI, 2 * _BW), F32),
                   jax.ShapeDtypeStruct((S5_BLOCKS, 2 * _BW, _BI), F32), jax.ShapeDtypeStruct((8, S5_STATES), F32),
                   jax.ShapeDtypeStruct((1, S5_WIDTH), F32)],
        scratch_shapes=[pltpu.VMEM((8, S5_STATES), F32), pltpu.VMEM((2 * SUB, S5_STATES), F32),
                        pltpu.VMEM((2 * SUB, S5_STATES), F32), big, big, big, big],
        compiler_params=_cparams(("arbitrary",)),
    )(u5, dy5, wb4, wc4, ab, dvec, sprev)


def _merge_vals(ys, xs, z, y5, gates, dvec, gssd, glu_w, glu_b, wbr):
    sz = _sigmoid(z)
    qv = ys + dvec * xs
    pre = qv * (z * sz)
    yn, rs = [], []
    for gi in range(SSD_GROUPS):
        p, r = _rms(pre[:, 256 * gi:256 * (gi + 1)])
        yn.append(p)
        rs.append(r)
    yn = jnp.concatenate(yn, axis=1)
    ya = yn * gssd
    gel = _gelu(y5)
    sg = _sigmoid(_dot(gel.astype(BF16), glu_w) + glu_b)
    yb = gel * sg
    pa = _dot(ya.astype(BF16), wbr[0:SSD_INNER, :])
    pb = _dot(yb.astype(BF16), wbr[SSD_INNER:, :])
    s0 = _sigmoid(gates[:, :D_MODEL])
    s1 = _sigmoid(gates[:, D_MODEL:])
    merged = s0 * pa + s1 * pb
    return dict(sz=sz, qv=qv, yn=yn, rs=rs, ya=ya, gel=gel, sg=sg, yb=yb, pa=pa, pb=pb, s0=s0, s1=s1, merged=merged)


def _merge_specs(tm):
    acts = [_row_spec(tm, 1024), _row_spec(tm, 1024, 0), _row_spec(tm, 1024), _row_spec(tm, 512), _row_spec(tm, 2048),
            _row_spec(tm, 1024)]
    params = [_const_spec((1, 1024)), _const_spec((1, 1024)), _const_spec((512, 512)), _const_spec((1, 512)),
              _hbm_spec(), _hbm_spec()]
    return acts, params


def _merge_fwd(ys, xbc_act, z, y5, gates, x, dvec, gssd, glu_w, glu_b, wbr, wout):
    T = x.shape[0]
    tm = TOKEN_TILE
    acts, params = _merge_specs(tm)

    def body(ys_ref, xs_ref, z_ref, y5_ref, gt_ref, x_ref, dv_ref, gs_ref, gw_ref, gb_ref, wbr_hbm, wout_hbm, x1_ref,
             wbr_ref, wout_ref):
        @pl.when(pl.program_id(0) == 0)
        def _():
            pltpu.sync_copy(wbr_hbm, wbr_ref)
            pltpu.sync_copy(wout_hbm, wout_ref)

        v = _merge_vals(ys_ref[...], xs_ref[...], z_ref[...], y5_ref[...], gt_ref[...], dv_ref[...], gs_ref[...],
                        gw_ref[...], gb_ref[...], wbr_ref)
        x1_ref[...] = x_ref[...] + _dot(v["merged"].astype(BF16), wout_ref[...])

    return _pc(
        body, name="merge_fwd", grid=(T // tm,),
        in_specs=acts + params, out_specs=_row_spec(tm, 1024),
        out_shape=jax.ShapeDtypeStruct((T, 1024), F32),
        scratch_shapes=[pltpu.VMEM((1536, 1024), BF16), pltpu.VMEM((1024, 1024), BF16)],
        compiler_params=_cparams(("arbitrary",)),
    )(ys, xbc_act, z, y5, gates, x, dvec, gssd, glu_w, glu_b, wbr, wout)


def _merge_bwd(ys, xbc_act, z, y5, gates, dx1, dvec, gssd, glu_w, glu_b, wbr, wout, head_sel):
    T = dx1.shape[0]
    tm = TOKEN_TILE
    acts, params = _merge_specs(tm)

    def body(ys_ref, xs_ref, z_ref, y5_ref, gt_ref, dx1_ref, dv_ref, gs_ref, gw_ref, gb_ref, wbr_hbm, wout_hbm, hs_ref,
             dys_ref, dxs_ref, dz_ref, dy5_ref, dgt_ref, mg_ref, ya_ref, yb_ref, dpa_ref, dpb_ref, gel_ref, dpre_ref,
             ddv_ref, dgs_ref, dgb_ref, wbr_ref, wout_ref, ddacc_ref):
        i = pl.program_id(0)

        @pl.when(i == 0)
        def _():
            pltpu.sync_copy(wbr_hbm, wbr_ref)
            pltpu.sync_copy(wout_hbm, wout_ref)
            ddacc_ref[...] = jnp.zeros_like(ddacc_ref)
            dgs_ref[...] = jnp.zeros_like(dgs_ref)
            dgb_ref[...] = jnp.zeros_like(dgb_ref)

        ys, xs, z, y5, gates = ys_ref[...], xs_ref[...], z_ref[...], y5_ref[...], gt_ref[...]
        dvv, gsv, gw = dv_ref[...], gs_ref[...], gw_ref[...]
        v = _merge_vals(ys, xs, z, y5, gates, dvv, gsv, gw, gb_ref[...], wbr_ref)
        dmg = _dot_nt(dx1_ref[...].astype(BF16), wout_ref[...])
        s0, s1, pa, pb = v["s0"], v["s1"], v["pa"], v["pb"]
        dgt_ref[:, :D_MODEL] = dmg * pa * s0 * (1.0 - s0)
        dgt_ref[:, D_MODEL:] = dmg * pb * s1 * (1.0 - s1)
        dpa = (dmg * s0).astype(BF16)
        dpb = (dmg * s1).astype(BF16)
        dya = _dot_nt(dpa, wbr_ref[0:SSD_INNER, :])
        dyb = _dot_nt(dpb, wbr_ref[SSD_INNER:, :])
        gel, sg = v["gel"], v["sg"]
        dpre = (dyb * gel * sg * (1.0 - sg))
        dgb_ref[...] += jnp.sum(dpre, axis=0, keepdims=True)
        dpre_b = dpre.astype(BF16)
        dgel = dyb * sg + _dot_nt(dpre_b, gw)
        dy5_ref[...] = dgel * _gelu_grad(y5)
        yn = v["yn"]
        dgs_ref[...] += jnp.sum(dya * yn, axis=0, keepdims=True)
        dyn = dya * gsv
        dpre_a = jnp.concatenate(
            [_rms_bwd(yn[:, 256 * gi:256 * (gi + 1)], v["rs"][gi], dyn[:, 256 * gi:256 * (gi + 1)])
             for gi in range(SSD_GROUPS)], axis=1)
        sz, qv = v["sz"], v["qv"]
        dq = dpre_a * (z * sz)
        dz_ref[...] = dpre_a * qv * (sz * (1.0 + z * (1.0 - sz)))
        dys_ref[...] = dq
        dxs_ref[...] = dq * dvv
        ddacc_ref[...] += jnp.sum(dq * xs, axis=0, keepdims=True)
        mg_ref[...] = v["merged"].astype(BF16)
        ya_ref[...] = v["ya"].astype(BF16)
        yb_ref[...] = v["yb"].astype(BF16)
        dpa_ref[...] = dpa
        dpb_ref[...] = dpb
        gel_ref[...] = gel.astype(BF16)
        dpre_ref[...] = dpre_b

        @pl.when(i == pl.num_programs(0) - 1)
        def _():
            ddv_ref[...] = _dot_hi(ddacc_ref[...], hs_ref[...])

    outs = [(1024, F32), (1024, F32), (1024, F32), (512, F32), (2048, F32),
            (1024, BF16), (1024, BF16), (512, BF16), (1024, BF16), (1024, BF16), (512, BF16), (512, BF16)]
    return _pc(
        body, name="merge_bwd", grid=(T // tm,),
        in_specs=acts + params + [_const_spec((1024, DT_PAD))],
        out_specs=[_row_spec(tm, w) for w, _ in outs] + [_const_spec((1, DT_PAD)), _const_spec((1, 1024)), _const_spec((1, 512))],
        out_shape=[jax.ShapeDtypeStruct((T, w), d) for w, d in outs] + [
            jax.ShapeDtypeStruct((1, DT_PAD), F32), jax.ShapeDtypeStruct((1, 1024), F32), jax.ShapeDtypeStruct((1, 512), F32)],
        scratch_shapes=[pltpu.VMEM((1536, 1024), BF16), pltpu.VMEM((1024, 1024), BF16), pltpu.VMEM((1, 1024), F32)],
        compiler_params=_cparams(("arbitrary",)),
    )(ys, xbc_act, z, y5, gates, dx1, dvec, gssd, glu_w, glu_b, wbr, wout, head_sel)


def _mlp_fwd(x1, g, w1, w2):
    T = x1.shape[0]
    tm = TOKEN_TILE

    def body(x_ref, g_ref, w1_hbm, w2_hbm, o_ref, w1_ref, w2_ref):
        @pl.when(pl.program_id(0) == 0)
        def _():
            pltpu.sync_copy(w1_hbm, w1_ref)
            pltpu.sync_copy(w2_hbm, w2_ref)

        xv = x_ref[...]
        xn, _ = _rms(xv)
        h = (xn * g_ref[...]).astype(BF16)
        acc = xv
        for s in range(FF_SHARDS):
            rl = jnp.maximum(_dot(h, w1_ref[s]), 0.0)
            acc += _dot((rl * rl).astype(BF16), w2_ref[FF_SHARD * s:FF_SHARD * (s + 1), :])
        o_ref[...] = acc

    return _pc(
        body, name="mlp_fwd", grid=(T // tm,),
        in_specs=[_row_spec(tm, 1024), _const_spec((1, 1024)), _hbm_spec(), _hbm_spec()],
        out_specs=_row_spec(tm, 1024), out_shape=jax.ShapeDtypeStruct((T, 1024), F32),
        scratch_shapes=[pltpu.VMEM((FF_SHARDS, D_MODEL, FF_SHARD), BF16), pltpu.VMEM((D_FF, D_MODEL), BF16)],
        compiler_params=_cparams(("arbitrary",)),
    )(x1, g, w1, w2)


def _mlp_bwd(x1, dx2, g, w1, w2):
    T = x1.shape[0]
    tm = TOKEN_TILE

    def body(x_ref, dx2_ref, g_ref, w1_hbm, w2_hbm, dx1_ref, h_ref, act_ref, da_ref, dg_ref, w1_ref, w2_ref):
        @pl.when(pl.program_id(0) == 0)
        def _():
            pltpu.sync_copy(w1_hbm, w1_ref)
            pltpu.sync_copy(w2_hbm, w2_ref)
            dg_ref[...] = jnp.zeros_like(dg_ref)

        xn, r = _rms(x_ref[...])
        gv = g_ref[...]
        h = (xn * gv).astype(BF16)
        h_ref[...] = h
        dx2 = dx2_ref[...]
        dx2b = dx2.astype(BF16)
        dh = jnp.zeros((tm, D_MODEL), F32)
        for s in range(FF_SHARDS):
            ff = slice(FF_SHARD * s, FF_SHARD * (s + 1))
            rl = jnp.maximum(_dot(h, w1_ref[s]), 0.0)
            act_ref[:, ff] = (rl * rl).astype(BF16)
            da = (_dot_nt(dx2b, w2_ref[ff, :]) * (2.0 * rl)).astype(BF16)
            da_ref[:, ff] = da
            dh += _dot_nt(da, w1_ref[s])
        dg_ref[...] += jnp.sum(dh * xn, axis=0, keepdims=True)
        dx1_ref[...] = dx2 + _rms_bwd(xn, r, dh * gv)

    return _pc(
        body, name="mlp_bwd", grid=(T // tm,),
        in_specs=[_row_spec(tm, 1024), _row_spec(tm, 1024), _const_spec((1, 1024)), _hbm_spec(), _hbm_spec()],
        out_specs=[_row_spec(tm, 1024), _row_spec(tm, 1024), _row_spec(tm, D_FF), _row_spec(tm, D_FF), _const_spec((1, 1024))],
        out_shape=[jax.ShapeDtypeStruct((T, 1024), F32), jax.ShapeDtypeStruct((T, 1024), BF16),
                   jax.ShapeDtypeStruct((T, D_FF), BF16), jax.ShapeDtypeStruct((T, D_FF), BF16),
                   jax.ShapeDtypeStruct((1, 1024), F32)],
        scratch_shapes=[pltpu.VMEM((FF_SHARDS, D_MODEL, FF_SHARD), BF16), pltpu.VMEM((D_FF, D_MODEL), BF16)],
        compiler_params=_cparams(("arbitrary",)),
    )(x1, dx2, g, w1, w2)


def _loss_head(x2, target, g):
    T = x2.shape[0]
    tm = TOKEN_TILE

    def body(x_ref, t_ref, g_ref, dx_ref, loss_ref, dg_ref):
        @pl.when(pl.program_id(0) == 0)
        def _():
            loss_ref[...] = jnp.zeros_like(loss_ref)
            dg_ref[...] = jnp.zeros_like(dg_ref)

        xn, r = _rms(x_ref[...])
        gv = g_ref[...]
        err = xn * gv - t_ref[...]
        loss_ref[...] += jnp.sum(err * err, axis=0, keepdims=True) * (0.5 / D_MODEL)
        dy = err * (1.0 / D_MODEL)
        dg_ref[...] += jnp.sum(dy * xn, axis=0, keepdims=True)
        dx_ref[...] = _rms_bwd(xn, r, dy * gv)

    return _pc(
        body, name="loss_head", grid=(T // tm,),
        in_specs=[_row_spec(tm, 1024), _row_spec(tm, 1024), _const_spec((1, 1024))],
        out_specs=[_row_spec(tm, 1024), _const_spec((1, 1024)), _const_spec((1, 1024))],
        out_shape=[jax.ShapeDtypeStruct((T, 1024), F32), jax.ShapeDtypeStruct((1, 1024), F32),
                   jax.ShapeDtypeStruct((1, 1024), F32)],
        compiler_params=_cparams(("arbitrary",)),
    )(x2, target, g)


WGRAD_OUT_ELEMS = 2 * 1024 * 1024
WGRAD_TILE_BYTES = 4 * 1024 * 1024


def _wgrad(a, b, name, col_shards=None):
    T, K = a.shape
    N = b.shape[1]
    nb = N // col_shards if col_shards else min(N, 1024, max(128, WGRAD_OUT_ELEMS // K))
    tt = min(T, WGRAD_TOKENS)
    while tt * max(K * a.dtype.itemsize, nb * b.dtype.itemsize) > WGRAD_TILE_BYTES:
        tt //= 2
    assert N % nb == 0 and T % tt == 0
    if col_shards:
        out_spec = pl.BlockSpec((None, K, nb), lambda n, t: (n, 0, 0))
        out_shape = jax.ShapeDtypeStruct((col_shards, K, nb), F32)
    else:
        out_spec = pl.BlockSpec((K, nb), lambda n, t: (0, n))
        out_shape = jax.ShapeDtypeStruct((K, N), F32)

    def body(a_ref, b_ref, o_ref):
        @pl.when(pl.program_id(1) == 0)
        def _():
            o_ref[...] = jnp.zeros_like(o_ref)

        o_ref[...] += _dot_tn(a_ref[...].astype(BF16), b_ref[...].astype(BF16))

    return _pc(
        body, name=name, grid=(N // nb, T // tt),
        in_specs=[pl.BlockSpec((tt, K), lambda n, t: (t, 0)), pl.BlockSpec((tt, nb), lambda n, t: (t, n))],
        out_specs=out_spec, out_shape=out_shape,
        compiler_params=_cparams(("parallel", "arbitrary")),
    )(a, b)


def _s5_block_weights(bb_re, bb_im, c_re, c_im):
    eye = jnp.eye(8, dtype=F32)
    bre = bb_re.reshape(S5_BLOCKS, 8, 64, 16)
    bim = bb_im.reshape(S5_BLOCKS, 8, 64, 16)
    wb_re = jnp.einsum('jgpk,gh->jhkgp', bre, eye).reshape(S5_BLOCKS, _BI, _BW)
    wb_im = jnp.einsum('jgpk,gh->jhkgp', bim, eye).reshape(S5_BLOCKS, _BI, _BW)
    wb4 = jnp.concatenate([wb_re, wb_im], axis=2).astype(BF16)
    cre = c_re.reshape(S5_BLOCKS, 8, 16, 64)
    cim = c_im.reshape(S5_BLOCKS, 8, 16, 64)
    wc_re = jnp.einsum('jgkp,gh->jgphk', cre, eye).reshape(S5_BLOCKS, _BW, _BI)
    wc_im = jnp.einsum('jgkp,gh->jgphk', -cim, eye).reshape(S5_BLOCKS, _BW, _BI)
    wc4 = jnp.concatenate([wc_re, wc_im], axis=1).astype(BF16)
    return wb4, wc4


def _s5_block_grads(dwb4, dwc4):
    eye = jnp.eye(8, dtype=F32)
    dwb = dwb4.reshape(S5_BLOCKS, 8, 16, 2, 8, 64)
    dbb = jnp.einsum('jhkrgp,gh->rjgpk', dwb, eye).reshape(2, 32, 64, 16)
    dwc = dwc4.reshape(S5_BLOCKS, 2, 8, 64, 8, 16)
    dc = jnp.einsum('jrgphk,gh->rjgkp', dwc, eye).reshape(2, 32, 16, 64)
    return dbb[0], dbb[1], dc[0], -dc[1]


def _row(v, width=None):
    v = v.reshape(1, -1)
    if width is not None and v.shape[1] < width:
        v = jnp.concatenate([v, jnp.zeros((1, width - v.shape[1]), v.dtype)], axis=1)
    return v


def _local_step(x, target, p):
    g_mix, g_mlp, g_fin = _row(p["norm_mix_g"]), _row(p["norm_mlp_g"]), _row(p["norm_final_g"])
    conv_b = _row(p["conv_b"])
    dt_bias = _row(p["dt_bias"], DT_PAD)
    alog = _row(p["a_log"], DT_PAD)
    dvec = _row(jnp.repeat(p["d_ssd"], SSD_HEADDIM))
    gssd = _row(p["ssd_norm_g"])
    s5d = _row(p["s5_d"])
    glu_b = _row(p["s5_glu_b"])
    head_sel = (jnp.arange(SSD_INNER)[:, None] // SSD_HEADDIM == jnp.arange(DT_PAD)[None, :]).astype(F32)

    a_re = p["s5_a_re"].reshape(S5_STATES, 1)
    a_im = p["s5_a_im"].reshape(S5_STATES, 1)
    log_dt = jnp.repeat(p["s5_log_dt"], 64).reshape(S5_STATES, 1)
    b_re = p["s5_b_re"].reshape(S5_STATES, 16)
    b_im = p["s5_b_im"].reshape(S5_STATES, 16)
    ab_re, ab_im, bb_re, bb_im = _s5_disc(a_re, a_im, log_dt, b_re, b_im)
    wb4, wc4 = _s5_block_weights(bb_re, bb_im, p["s5_c_re"], p["s5_c_im"])
    ab = jnp.concatenate([ab_re.reshape(1, S5_STATES), ab_im.reshape(1, S5_STATES), jnp.zeros((6, S5_STATES), F32)], axis=0)

    wp, wbr, wout, w1, w2, glu_w = p["w_in_perm"], p["w_branch"], p["w_out"], p["w_mlp_in"], p["w_mlp_out"], p["s5_glu_w"]

    z, xbc_raw, u5, gates, dt_raw = _inproj_fwd(x, g_mix, wp)
    xbc_act, dt = _conv_fwd(xbc_raw, dt_raw, p["conv_w"], conv_b, dt_bias)
    ys, ssd_states = _ssd_fwd(xbc_act, dt, alog)
    y5, s5_states = _s5_fwd(u5, wb4, wc4, ab, s5d)
    x1 = _merge_fwd(ys, xbc_act, z, y5, gates, x, dvec, gssd, glu_w, glu_b, wbr, wout)
    x2 = _mlp_fwd(x1, g_mlp, w1, w2)
    dx2, loss_lanes, d_gfin = _loss_head(x2, target, g_fin)

    dx1, h2, act, da1, d_gmlp = _mlp_bwd(x1, dx2, g_mlp, w1, w2)
    d_w_mlp_out = _wgrad(act, dx2, "wgrad_mlp_out")
    d_w_mlp_in = _wgrad(h2, da1, "wgrad_mlp_in", col_shards=FF_SHARDS)
    (dys, dxs_m, dz, dy5, dgates, mg, ya, yb, dpa, dpb, gel, dpre, d_dssd, d_gssd, d_glu_b) = _merge_bwd(
        ys, xbc_act, z, y5, gates, dx1, dvec, gssd, glu_w, glu_b, wbr, wout, head_sel)
    d_w_out = _wgrad(mg, dx1, "wgrad_out")
    d_w_branch = jnp.concatenate([_wgrad(ya, dpa, "wgrad_branch_a"), _wgrad(yb, dpb, "wgrad_branch_b")], axis=0)
    d_glu_w = _wgrad(gel, dpre, "wgrad_glu")
    du5, dwb4, dwc4, dab, d_s5d = _s5_bwd(u5, dy5, wb4, wc4, ab, s5d, s5_states)
    dbb_re, dbb_im, d_c_re, d_c_im = _s5_block_grads(dwb4, dwc4)
    d_a_re, d_a_im, d_log_dt, d_b_re, d_b_im = _s5_disc_bwd(
        a_re, a_im, log_dt, b_re, b_im, dab[0].reshape(S5_STATES, 1), dab[1].reshape(S5_STATES, 1),
        dbb_re.reshape(S5_STATES, 16), dbb_im.reshape(S5_STATES, 16))
    dxs_s, dB, dC, ddt, d_alog = _ssd_bwd(xbc_act, dt, alog, ssd_states, dys)
    dxbc_raw, ddt_raw, d_conv_w, d_conv_b, d_dt_bias = _conv_bwd(
        xbc_raw, dt_raw, dxs_m, dxs_s, dB, dC, ddt, p["conv_w"], conv_b, dt_bias)
    dx, h, d_gmix = _inproj_bwd(x, dx1, dz, dxbc_raw, du5, dgates, ddt_raw, g_mix, wp)
    d_w_in = dict(z=_wgrad(h, dz, "wgrad_in_z"), xbc=_wgrad(h, dxbc_raw, "wgrad_in_xbc"),
                  dt=_wgrad(h, ddt_raw, "wgrad_in_dt")[:, :16], u5=_wgrad(h, du5, "wgrad_in_u5"),
                  gates=_wgrad(h, dgates, "wgrad_in_gates"))

    grads = dict(
        norm_mix_g=d_gmix.reshape(-1), w_in_pieces=[(c0, d_w_in[n]) for n, c0, _ in W_IN_PIECES],
        conv_w=d_conv_w[:CONV_K], conv_b=d_conv_b.reshape(-1),
        dt_bias=d_dt_bias[0, :16], a_log=d_alog[0, :16], d_ssd=d_dssd[0, :16], ssd_norm_g=d_gssd.reshape(-1),
        s5_a_re=d_a_re.reshape(32, 64), s5_a_im=d_a_im.reshape(32, 64), s5_log_dt=d_log_dt.reshape(32),
        s5_b_re=d_b_re.reshape(32, 64, 16), s5_b_im=d_b_im.reshape(32, 64, 16), s5_c_re=d_c_re, s5_c_im=d_c_im,
        s5_d=d_s5d.reshape(-1), s5_glu_w=d_glu_w, s5_glu_b=d_glu_b.reshape(-1), w_branch=d_w_branch, w_out=d_w_out,
        norm_mlp_g=d_gmlp.reshape(-1), w_mlp_in=d_w_mlp_in, w_mlp_out=d_w_mlp_out, norm_final_g=d_gfin.reshape(-1))
    return jnp.sum(loss_lanes), dx, grads


MESH = pl.DeviceIdType.MESH
N_CHIPS = 4


def _place():
    x, y, c = lax.axis_index("x"), lax.axis_index("y"), lax.axis_index("c")
    chips = [(1 - x, y), (x, 1 - y), (1 - x, 1 - y)]
    return x, y, c, chips


def _remote(src, dst, send_sems, recv_sems, k, to):
    return pltpu.make_async_remote_copy(src_ref=src, dst_ref=dst, send_sem=send_sems.at[k], recv_sem=recv_sems.at[k],
                                        device_id=to, device_id_type=MESH)


def _row_chunks(rows, k, align):
    step = rows // k
    assert rows % k == 0 and step % align == 0, (rows, k, align)
    return [(i * step, step) for i in range(k)]


ICI_CHUNKS = 4
D2D_CHUNKS = 24


def _allgather_chips(srcs, ks, name):
    halves = [a.shape[0] // 2 for a in srcs]
    pieces = [_row_chunks(h, k, 32 // a.dtype.itemsize) for a, h, k in zip(srcs, halves, ks)]
    n_ici = 3 * sum(ks)
    n_sem = 2 * n_ici + len(srcs)

    def body(*refs):
        src_refs, out_refs = refs[:len(srcs)], refs[len(srcs):2 * len(srcs)]
        send_sems, recv_sems = refs[2 * len(srcs):]
        x, y, c, chips = _place()
        own = 2 * x + y
        sib = (x, y, 1 - c)
        first, fwd_plan, k = [], [], 0
        for a, (src_ref, out_ref) in enumerate(zip(src_refs, out_refs)):
            h = halves[a]
            for r0, nr in pieces[a]:
                for cx, cy in chips:
                    first.append(_remote(src_ref.at[pl.ds(c * h + r0, nr), :], out_ref.at[own, pl.ds(c * h + r0, nr), :],
                                         send_sems, recv_sems, k, (cx, cy, c)))
                    fwd_plan.append((out_ref, 2 * cx + cy, h, r0, nr, k, (cx, cy, c)))
                    k += 1
        for a, (src_ref, out_ref) in enumerate(zip(src_refs, out_refs)):
            first.append(_remote(src_ref, out_ref.at[own], send_sems, recv_sems, 2 * n_ici + a, sib))
        for cp in first:
            cp.start()
        passed = []
        for out_ref, s, h, r0, nr, k, frm in fwd_plan:
            got = out_ref.at[s, pl.ds(c * h + r0, nr), :]
            _remote(got, got, send_sems, recv_sems, k, frm).wait_recv()
            fw = _remote(got, got, send_sems, recv_sems, n_ici + k, sib)
            fw.start()
            passed.append(fw)
        for out_ref, s, h, r0, nr, k, frm in fwd_plan:
            got = out_ref.at[s, pl.ds((1 - c) * h + r0, nr), :]
            _remote(got, got, send_sems, recv_sems, n_ici + k, sib).wait_recv()
        for a, (src_ref, out_ref) in enumerate(zip(src_refs, out_refs)):
            _remote(src_ref, out_ref.at[own], send_sems, recv_sems, 2 * n_ici + a, sib).wait_recv()
        for cp in first + passed:
            cp.wait_send()

    return _pc(
        body, name=name, in_specs=[_hbm_spec()] * len(srcs), out_specs=[_hbm_spec()] * len(srcs),
        out_shape=[jax.ShapeDtypeStruct((N_CHIPS,) + a.shape, a.dtype) for a in srcs],
        scratch_shapes=[pltpu.SemaphoreType.DMA((n_sem,)), pltpu.SemaphoreType.DMA((n_sem,))],
    )(*srcs)


def _d2d_pieces(rows):
    k = next(k for k in (24, 16, 8, 4, 2, 1) if rows % k == 0 and (rows // k) % 8 == 0)
    return _row_chunks(rows, k, 8)


def _pair_exchange(gs, small):
    n = len(gs)
    halves = [g.shape[1] // 2 for g in gs]

    def body(*refs):
        g_refs, s_ref, sib_refs, sibs_ref = refs[:n], refs[n], refs[n + 1:2 * n + 1], refs[2 * n + 1]
        send_sems, recv_sems = refs[2 * n + 2:]
        x, y, c, _ = _place()
        sib = (x, y, 1 - c)
        for a in range(n):
            h = halves[a]
            for s in range(N_CHIPS):
                for r0, nr in _d2d_pieces(h):
                    _remote(g_refs[a].at[s, pl.ds((1 - c) * h + r0, nr), :], sib_refs[a].at[s, pl.ds(r0, nr), :],
                            send_sems, recv_sems, a, sib).start()
        sm = _remote(s_ref, sibs_ref, send_sems, recv_sems, n, sib)
        sm.start()
        for a in range(n):
            _remote(sib_refs[a], sib_refs[a], send_sems, recv_sems, a, sib).wait()
        sm.wait()

    return _pc(
        body, name="pair_exchange", in_specs=[_hbm_spec()] * (n + 1), out_specs=[_hbm_spec()] * (n + 1),
        out_shape=[jax.ShapeDtypeStruct((N_CHIPS, h, g.shape[2]), F32) for g, h in zip(gs, halves)]
        + [jax.ShapeDtypeStruct(small.shape, F32)],
        scratch_shapes=[pltpu.SemaphoreType.DMA((n + 1,)), pltpu.SemaphoreType.DMA((n + 1,))],
    )(*gs, small)


SUM_BLOCKS = 4


def _pair_sum(place, g, sib, name, small=None, sib_small=None):
    n, R, C = g.shape
    H = R // 2
    rb = H // SUM_BLOCKS
    assert H % SUM_BLOCKS == 0 and rb % 16 == 0

    def body(place_ref, a_ref, b_ref, *rest):
        if small is None:
            pf_ref, pb_ref = rest
        else:
            s_ref, t_ref, pf_ref, pb_ref, ps_ref = rest

            @pl.when((pl.program_id(0) == 0) & (pl.program_id(1) == 0))
            def _():
                ps_ref[...] = s_ref[...] + t_ref[...]

        p = a_ref[...] + b_ref[...]
        pf_ref[...] = p
        pb_ref[...] = p.astype(BF16)

    blk = pl.BlockSpec((1, rb, C), lambda s, i, pr: (s, i, 0))
    mine = pl.BlockSpec((1, rb, C), lambda s, i, pr: (s, pr[1] * SUM_BLOCKS + i, 0))
    ins, outs, shapes, args = [mine, blk], [blk, blk], [jax.ShapeDtypeStruct((n, H, C), F32),
                                                        jax.ShapeDtypeStruct((n, H, C), BF16)], [g, sib]
    if small is not None:
        sm = pl.BlockSpec(small.shape, lambda s, i, pr: (0, 0))
        ins += [sm, sm]
        outs += [sm]
        shapes += [jax.ShapeDtypeStruct(small.shape, F32)]
        args += [small, sib_small]
    return _pc(
        body, name=name, out_shape=shapes,
        grid_spec=pltpu.PrefetchScalarGridSpec(num_scalar_prefetch=1, grid=(n, SUM_BLOCKS), in_specs=ins, out_specs=outs),
        compiler_params=_cparams(("arbitrary", "arbitrary")),
    )(place, *args)


def _chip_exchange(pbs, psmall):
    n = len(pbs)

    def body(*refs):
        pb_refs, ps_ref, got_refs, small4_ref = refs[:n], refs[n], refs[n + 1:2 * n + 1], refs[2 * n + 1]
        send_sems, recv_sems = refs[2 * n + 2:]
        x, y, c, chips = _place()
        own = 2 * x + y
        small = []
        for j, (cx, cy) in enumerate(chips):
            for a in range(n):
                for r0, nr in _row_chunks(pbs[a].shape[1], ICI_CHUNKS, 16):
                    _remote(pb_refs[a].at[2 * cx + cy, pl.ds(r0, nr), :], got_refs[a].at[j, pl.ds(r0, nr), :],
                            send_sems, recv_sems, 3 * a + j, (cx, cy, c)).start()
            small.append(_remote(ps_ref, small4_ref.at[own], send_sems, recv_sems, 3 * n + j, (cx, cy, c)))
            small[-1].start()
        for j, (cx, cy) in enumerate(chips):
            for a in range(n):
                _remote(pb_refs[a].at[own], got_refs[a].at[j], send_sems, recv_sems, 3 * a + j, (cx, cy, c)).wait()
            _remote(ps_ref, small4_ref.at[2 * cx + cy], send_sems, recv_sems, 3 * n + j, (cx, cy, c)).wait_recv()
        for cp in small:
            cp.wait_send()

    return _pc(
        body, name="chip_exchange", in_specs=[_hbm_spec()] * (n + 1), out_specs=[_hbm_spec()] * (n + 1),
        out_shape=[jax.ShapeDtypeStruct((3,) + pb.shape[1:], BF16) for pb in pbs]
        + [jax.ShapeDtypeStruct((N_CHIPS,) + psmall.shape, F32)],
        scratch_shapes=[pltpu.SemaphoreType.DMA((3 * n + 3,)), pltpu.SemaphoreType.DMA((3 * n + 3,))],
    )(*pbs, psmall)


def _chip_sum(place, pf, got, name, small4=None, psmall=None):
    _, H, C = pf.shape
    rb = H // SUM_BLOCKS

    def body(place_ref, o_ref, g_ref, *rest):
        if small4 is None:
            (tot_ref,) = rest
        else:
            s_ref, p_ref, tot_ref, st_ref = rest

            @pl.when(pl.program_id(0) == 0)
            def _():
                terms = [jnp.where(place_ref[0] == s, p_ref[...], s_ref[s]) for s in range(N_CHIPS)]
                st_ref[...] = ((terms[0] + terms[1]) + terms[2]) + terms[3]

        tot_ref[...] = ((o_ref[0] + g_ref[0].astype(F32)) + g_ref[1].astype(F32)) + g_ref[2].astype(F32)

    ins = [pl.BlockSpec((1, rb, C), lambda i, pr: (pr[0], i, 0)), pl.BlockSpec((3, rb, C), lambda i, pr: (0, i, 0))]
    outs = [pl.BlockSpec((rb, C), lambda i, pr: (pr[1] * SUM_BLOCKS + i, 0))]
    shapes = [jax.ShapeDtypeStruct((2 * H, C), F32)]
    args = [pf, got]
    if small4 is not None:
        ins += [pl.BlockSpec(small4.shape, lambda i, pr: (0, 0, 0)), pl.BlockSpec(psmall.shape, lambda i, pr: (0, 0))]
        outs += [pl.BlockSpec(psmall.shape, lambda i, pr: (0, 0))]
        shapes += [jax.ShapeDtypeStruct(psmall.shape, F32)]
        args += [small4, psmall]
    return _pc(
        body, name=name, out_shape=shapes,
        grid_spec=pltpu.PrefetchScalarGridSpec(num_scalar_prefetch=1, grid=(SUM_BLOCKS,), in_specs=ins, out_specs=outs),
        compiler_params=_cparams(("arbitrary",)),
    )(place, *args)


def _half_exchange(fulls):
    n = len(fulls)

    def body(*refs):
        in_refs, out_refs = refs[:n], refs[n:2 * n]
        send_sems, recv_sems = refs[2 * n:]
        x, y, c, _ = _place()
        sib = (x, y, 1 - c)
        for a in range(n):
            h = fulls[a].shape[0] // 2
            for r0, nr in _d2d_pieces(h):
                rows = pl.ds(c * h + r0, nr)
                _remote(in_refs[a].at[rows, :], out_refs[a].at[rows, :], send_sems, recv_sems, a, sib).start()
        for a in range(n):
            h = fulls[a].shape[0] // 2
            _remote(in_refs[a].at[pl.ds(c * h, h), :], out_refs[a].at[pl.ds((1 - c) * h, h), :], send_sems, recv_sems, a,
                    sib).wait()

    return _pc(
        body, name="half_exchange", in_specs=[_hbm_spec()] * n, out_specs=[_hbm_spec()] * n,
        out_shape=[jax.ShapeDtypeStruct(f.shape, F32) for f in fulls],
        input_output_aliases={a: a for a in range(n)},
        scratch_shapes=[pltpu.SemaphoreType.DMA((n,)), pltpu.SemaphoreType.DMA((n,))],
    )(*fulls)


def _adamw(w, g, m, v, name, g_row0=0, with_grad=False):
    R, C = w.shape
    rb = 256 if R % 256 == 0 else (128 if R % 128 == 0 else R)
    assert g_row0 % rb == 0

    def body(w_ref, g_ref, m_ref, v_ref, d_ref, nm_ref, nv_ref, *g_out):
        gv = g_ref[...]
        m2 = ADAM_B1 * m_ref[...] + (1.0 - ADAM_B1) * gv
        v2 = ADAM_B2 * v_ref[...] + (1.0 - ADAM_B2) * (gv * gv)
        m_hat = m2 / (1.0 - ADAM_B1 ** ADAM_STEP)
        v_hat = v2 / (1.0 - ADAM_B2 ** ADAM_STEP)
        d_ref[...] = -ADAM_LR * (m_hat / (jnp.sqrt(v_hat) + ADAM_EPS) + ADAM_WD * w_ref[...])
        nm_ref[...] = m2
        nv_ref[...] = v2
        if with_grad:
            g_out[0][...] = gv

    spec = pl.BlockSpec((rb, C), lambda i: (i, 0))
    g_spec = pl.BlockSpec((rb, C), lambda i: (g_row0 // rb + i, 0))
    n_out = 4 if with_grad else 3
    return _pc(
        body, name=name, grid=(R // rb,), in_specs=[spec, g_spec, spec, spec], out_specs=[spec] * n_out,
        out_shape=[jax.ShapeDtypeStruct((R, C), F32)] * n_out, compiler_params=_cparams(("parallel",)),
    )(w, g, m, v)


PACK_COLS = 1024
ROWS_A = (("w_mlp_in", 0, 1024), ("w_mlp_out", 1024, 1024), ("w_out", 2048, 256), ("w_branch", 2304, 384))
ROWS_A_TOTAL = 2688
W_IN_SHARD = 1412
CONV_PAD_ROWS = 16
SMALL = (("norm_mix_g", (1024,)), ("conv_b", (2048,)), ("dt_bias", (16,)), ("a_log", (16,)), ("d_ssd", (16,)),
         ("ssd_norm_g", (1024,)), ("s5_a_re", (32, 64)), ("s5_a_im", (32, 64)), ("s5_log_dt", (32,)),
         ("s5_b_re", (32, 64, 16)), ("s5_b_im", (32, 64, 16)), ("s5_c_re", (32, 16, 64)), ("s5_c_im", (32, 16, 64)),
         ("s5_d", (512,)), ("s5_glu_b", (512,)), ("norm_mlp_g", (1024,)), ("norm_final_g", (1024,)))
SMALL_ROWS = 144
GLU_ROWS = S5_WIDTH * S5_WIDTH // PACK_COLS
CONV_ROWS = CONV_K * CONV_DIM // PACK_COLS
W_IN_PIECES = (("z", 0, 1024), ("xbc", 1024, 2048), ("dt", OFF_DT, 16), ("u5", OFF_U, 512), ("gates", 3600, 2048))


def _pack_small(parts):
    flat = jnp.concatenate([a.astype(F32).reshape(-1) for a in parts])
    return jnp.concatenate([flat, jnp.zeros((SMALL_ROWS * PACK_COLS - flat.shape[0],), F32)]).reshape(SMALL_ROWS, PACK_COLS)


def _unpack_small(pack):
    flat, out, r = pack.reshape(-1), {}, 0
    for name, shp in SMALL:
        n = math.prod(shp)
        out[name] = flat[r:r + n].reshape(shp)
        r += n
    return out


def _column_range(pieces, lo, hi):
    out = []
    for c0, a in pieces:
        a0, a1 = max(lo, c0), min(hi, c0 + a.shape[-1])
        if a0 < a1:
            out.append(a[..., a0 - c0:a1 - c0])
    return out


def kernel(x, norm_mix_g, w_in, conv_w, conv_b, dt_bias, a_log, d_ssd, ssd_norm_g, s5_a_re, s5_a_im, s5_log_dt, s5_b_re, s5_b_im, s5_c_re, s5_c_im, s5_d, s5_glu_w, s5_glu_b, w_branch, w_out, norm_mlp_g, w_mlp_in, w_mlp_out, norm_final_g, loss_target, m_norm_mix_g, m_w_in, m_conv_w, m_conv_b, m_dt_bias, m_a_log, m_d_ssd, m_ssd_norm_g, m_s5_a_re, m_s5_a_im, m_s5_log_dt, m_s5_b_re, m_s5_b_im, m_s5_c_re, m_s5_c_im, m_s5_d, m_s5_glu_w, m_s5_glu_b, m_w_branch, m_w_out, m_norm_mlp_g, m_w_mlp_in, m_w_mlp_out, m_norm_final_g, v_norm_mix_g, v_w_in, v_conv_w, v_conv_b, v_dt_bias, v_a_log, v_d_ssd, v_ssd_norm_g, v_s5_a_re, v_s5_a_im, v_s5_log_dt, v_s5_b_re, v_s5_b_im, v_s5_c_re, v_s5_c_im, v_s5_d, v_s5_glu_w, v_s5_glu_b, v_w_branch, v_w_out, v_norm_mlp_g, v_w_mlp_in, v_w_mlp_out, v_norm_final_g):
    names = ("norm_mix_g", "w_in", "conv_w", "conv_b", "dt_bias", "a_log", "d_ssd", "ssd_norm_g", "s5_a_re", "s5_a_im",
             "s5_log_dt", "s5_b_re", "s5_b_im", "s5_c_re", "s5_c_im", "s5_d", "s5_glu_w", "s5_glu_b", "w_branch", "w_out",
             "norm_mlp_g", "w_mlp_in", "w_mlp_out", "norm_final_g")
    w = dict(zip(names, (norm_mix_g, w_in, conv_w, conv_b, dt_bias, a_log, d_ssd, ssd_norm_g, s5_a_re, s5_a_im, s5_log_dt,
                         s5_b_re, s5_b_im, s5_c_re, s5_c_im, s5_d, s5_glu_w, s5_glu_b, w_branch, w_out, norm_mlp_g,
                         w_mlp_in, w_mlp_out, norm_final_g)))
    m = dict(zip(names, (m_norm_mix_g, m_w_in, m_conv_w, m_conv_b, m_dt_bias, m_a_log, m_d_ssd, m_ssd_norm_g, m_s5_a_re,
                         m_s5_a_im, m_s5_log_dt, m_s5_b_re, m_s5_b_im, m_s5_c_re, m_s5_c_im, m_s5_d, m_s5_glu_w,
                         m_s5_glu_b, m_w_branch, m_w_out, m_norm_mlp_g, m_w_mlp_in, m_w_mlp_out, m_norm_final_g)))
    v = dict(zip(names, (v_norm_mix_g, v_w_in, v_conv_w, v_conv_b, v_dt_bias, v_a_log, v_d_ssd, v_ssd_norm_g, v_s5_a_re,
                         v_s5_a_im, v_s5_log_dt, v_s5_b_re, v_s5_b_im, v_s5_c_re, v_s5_c_im, v_s5_d, v_s5_glu_w,
                         v_s5_glu_b, v_w_branch, v_w_out, v_norm_mlp_g, v_w_mlp_in, v_w_mlp_out, v_norm_final_g)))

    cx, cy, cc = lax.axis_index("x"), lax.axis_index("y"), lax.axis_index("c")
    own = 2 * cx + cy
    place = jnp.stack([own, cc]).astype(jnp.int32)

    src_a = jnp.concatenate([w[n].astype(BF16) for n, _, _ in ROWS_A], axis=0)
    src_conv = jnp.concatenate([conv_w, jnp.zeros((CONV_PAD_ROWS - CONV_K, 512), F32)], axis=0)
    all_a, all_in, all_glu, all_conv = _allgather_chips(
        [src_a, w_in.astype(BF16), s5_glu_w.astype(BF16), src_conv], [ICI_CHUNKS, ICI_CHUNKS, 1, 1], "gather_weights")
    p = {n: w[n] for n, _ in SMALL}
    p["w_mlp_in"] = all_a[:, 0:1024]
    for n, r0, nr in ROWS_A[1:]:
        p[n] = all_a[:, r0:r0 + nr].reshape(N_CHIPS * nr, PACK_COLS)
    p["s5_glu_w"] = all_glu.reshape(S5_WIDTH, S5_WIDTH)
    p["conv_w"] = jnp.concatenate([all_conv[s, :CONV_K] for s in range(N_CHIPS)], axis=1)
    shards = [(W_IN_SHARD * s, all_in[s]) for s in range(N_CHIPS)]
    p["w_in_perm"] = jnp.concatenate(
        _column_range(shards, 0, OFF_DT) + _column_range(shards, OFF_U, D_IN_PROJ) + _column_range(shards, OFF_DT, OFF_U)
        + [jnp.zeros((D_MODEL, DT_PAD - 16), BF16)], axis=1)

    loss_part, grad_x, g = _local_step(x[0], loss_target[0], p)
    loss = lax.psum(loss_part, ("x", "y", "c"))

    g_a = jnp.concatenate([g["w_mlp_in"]] + [g[n].reshape(N_CHIPS, nr, PACK_COLS) for n, _, nr in ROWS_A[1:]], axis=1)
    g_in = jnp.stack([jnp.concatenate(_column_range(g["w_in_pieces"], W_IN_SHARD * s, W_IN_SHARD * (s + 1)), axis=1)
                      for s in range(N_CHIPS)])
    spack = jnp.concatenate([_pack_small([g[n] for n, _ in SMALL]), g["s5_glu_w"].reshape(GLU_ROWS, PACK_COLS),
                             g["conv_w"].reshape(CONV_ROWS, PACK_COLS)], axis=0)
    sib_a, sib_in, sib_small = _pair_exchange([g_a, g_in], spack)
    pf_a, pb_a, psmall = _pair_sum(place, g_a, sib_a, "pair_sum_a", spack, sib_small)
    pf_in, pb_in = _pair_sum(place, g_in, sib_in, "pair_sum_in")
    got_a, got_in, small4 = _chip_exchange([pb_a, pb_in], psmall)
    tot_a, small_tot = _chip_sum(place, pf_a, got_a, "chip_sum_a", small4, psmall)
    (tot_in,) = _chip_sum(place, pf_in, got_in, "chip_sum_in")
    red_a, red_in = _half_exchange([tot_a, tot_in])

    grads = _unpack_small(small_tot[:SMALL_ROWS])
    delta, new_m, new_v = {}, {}, {}
    for n, r0, _ in ROWS_A:
        delta[n], new_m[n], new_v[n], grads[n] = _adamw(w[n], red_a, m[n], v[n], "adamw_" + n, g_row0=r0, with_grad=True)
    grads["w_in"] = red_in
    glu_sum = small_tot[SMALL_ROWS:SMALL_ROWS + GLU_ROWS].reshape(S5_WIDTH, S5_WIDTH)
    grads["s5_glu_w"] = lax.dynamic_slice_in_dim(glu_sum, own * (S5_WIDTH // N_CHIPS), S5_WIDTH // N_CHIPS, axis=0)
    conv_sum = small_tot[SMALL_ROWS + GLU_ROWS:].reshape(CONV_K, CONV_DIM)
    grads["conv_w"] = lax.dynamic_slice_in_dim(conv_sum, own * (CONV_DIM // N_CHIPS), CONV_DIM // N_CHIPS, axis=1)
    for n in ("w_in", "s5_glu_w", "conv_w"):
        delta[n], new_m[n], new_v[n] = _adamw(w[n], grads[n], m[n], v[n], "adamw_" + n)
    ds, ms, vs = _adamw(_pack_small([w[n] for n, _ in SMALL]), small_tot, _pack_small([m[n] for n, _ in SMALL]),
                        _pack_small([v[n] for n, _ in SMALL]), "adamw_small")
    delta.update(_unpack_small(ds))
    new_m.update(_unpack_small(ms))
    new_v.update(_unpack_small(vs))

    return (loss, grad_x[None], *[grads[n] for n in names], *[delta[n] for n in names],
            *[new_m[n] for n in names], *[new_v[n] for n in names])
```

```python
import functools
import math

import jax
import jax.numpy as jnp
from jax import lax
from jax.experimental import pallas as pl
from jax.experimental.pallas import tpu as pltpu

F32 = jnp.float32
BF16 = jnp.bfloat16

D_MODEL = 1024
SSD_INNER = 1024
SSD_HEADS = 16
SSD_HEADDIM = 64
SSD_GROUPS = 4
SSD_HPG = 4
SSD_STATE = 128
SSD_CHUNK = 128
CONV_K = 4
CONV_DIM = 2048
S5_WIDTH = 512
S5_STATES = 2048
S5_BLOCKS = 4
S5_CHUNK = 128
D_FF = 4096
FF_SHARDS = 4
FF_SHARD = D_FF // FF_SHARDS
EPS = 1e-6
P_Z, P_XBC, P_U5, P_G, P_DT, P_END = 0, 1024, 3072, 3584, 5632, 5760
DT_PAD = 128
OFF_DT, OFF_U = 3072, 3088
D_IN_PROJ = 5648

ADAM_LR, ADAM_B1, ADAM_B2, ADAM_EPS, ADAM_WD, ADAM_STEP = 0.001, 0.9, 0.999, 1e-08, 0.01, 10

TOKEN_TILE = 256
VMEM_LIMIT = 56 * 1024 * 1024
HALO = 8
CONV_COLS = 256
CONV_ROWS_BLK = 64
WGRAD_TOKENS = 2048


def _pc(body, **kw):
    return pl.pallas_call(body, **kw)


def _cparams(sem=None):
    return pltpu.CompilerParams(dimension_semantics=sem, vmem_limit_bytes=VMEM_LIMIT)


def _dot(a, b):
    return jnp.dot(a, b, preferred_element_type=F32)


def _dot_nt(a, b):
    return lax.dot_general(a, b, (((1,), (1,)), ((), ())), preferred_element_type=F32)


def _dot_tn(a, b):
    return lax.dot_general(a, b, (((0,), (0,)), ((), ())), preferred_element_type=F32)


def _dot_hi(a, b, dims=(((1,), (0,)), ((), ()))):
    return lax.dot_general(a, b, dims, preferred_element_type=F32, precision=lax.Precision.HIGHEST)


def _split_bf16(x, terms):
    out = []
    for _ in range(terms - 1):
        t = x.astype(BF16)
        out.append(t)
        x = x - t.astype(F32)
    out.append(x.astype(BF16))
    return out


def _dot_split(x, onehots, terms, dims=(((1,), (0,)), ((), ()))):
    acc = None
    for t in _split_bf16(x, terms):
        p = lax.dot_general(t, onehots, dims, preferred_element_type=F32)
        acc = p if acc is None else acc + p
    return acc


def _dot_split_rhs(onehots, x, terms, dims=(((1,), (0,)), ((), ()))):
    acc = None
    for t in _split_bf16(x, terms):
        p = lax.dot_general(onehots, t, dims, preferred_element_type=F32)
        acc = p if acc is None else acc + p
    return acc


def _sigmoid(x):
    return 1.0 / (1.0 + jnp.exp(-x))


def _softplus(x):
    return jnp.maximum(x, 0.0) + jnp.log(1.0 + jnp.exp(-jnp.abs(x)))


_GELU_C = math.sqrt(2.0 / math.pi)


def _gelu(x):
    return 0.5 * x * (1.0 + jnp.tanh(_GELU_C * (x + 0.044715 * x * x * x)))


def _gelu_grad(x):
    t = jnp.tanh(_GELU_C * (x + 0.044715 * x * x * x))
    return 0.5 * (1.0 + t) + 0.5 * x * (1.0 - t * t) * _GELU_C * (1.0 + 3.0 * 0.044715 * x * x)


def _rms(x):
    r = lax.rsqrt(jnp.mean(x * x, axis=-1, keepdims=True) + EPS)
    return x * r, r


def _rms_bwd(xn, r, dxn):
    return r * (dxn - xn * jnp.mean(dxn * xn, axis=-1, keepdims=True))


def _row_spec(tm, width, col=0):
    return pl.BlockSpec((tm, width), lambda i: (i, col))


def _const_spec(shape):
    nd = len(shape)
    return pl.BlockSpec(shape, lambda i: (0,) * nd)


def _hbm_spec():
    return pl.BlockSpec(memory_space=pl.ANY)


def _inproj_fwd(x, g, wp):
    T = x.shape[0]
    tm = TOKEN_TILE

    def body(x_ref, g_ref, w_hbm, z_ref, xbc_ref, u5_ref, gt_ref, dt_ref, w_ref):
        @pl.when(pl.program_id(0) == 0)
        def _():
            pltpu.sync_copy(w_hbm, w_ref)

        xn, _ = _rms(x_ref[...])
        h = (xn * g_ref[...]).astype(BF16)
        z_ref[...] = _dot(h, w_ref[:, P_Z:P_XBC])
        xbc_ref[...] = _dot(h, w_ref[:, P_XBC:P_U5])
        u5_ref[...] = _dot(h, w_ref[:, P_U5:P_G])
        gt_ref[...] = _dot(h, w_ref[:, P_G:P_DT])
        dt_ref[...] = _dot(h, w_ref[:, P_DT:P_END])

    widths = (1024, 2048, 512, 2048, DT_PAD)
    return _pc(
        body, name="inproj_fwd", grid=(T // tm,),
        in_specs=[_row_spec(tm, D_MODEL), _const_spec((1, D_MODEL)), _hbm_spec()],
        out_specs=[_row_spec(tm, w) for w in widths],
        out_shape=[jax.ShapeDtypeStruct((T, w), F32) for w in widths],
        scratch_shapes=[pltpu.VMEM((D_MODEL, P_END), BF16)],
        compiler_params=_cparams(("arbitrary",)),
    )(x, g, wp)


def _inproj_bwd(x, dx1, dz, dxbc, du5, dgt, ddt, g, wp):
    T = x.shape[0]
    tm = TOKEN_TILE

    def body(x_ref, dx1_ref, dz_ref, dxbc_ref, du5_ref, dgt_ref, ddt_ref, g_ref, w_hbm, dx_ref, h_ref, dg_ref, w_ref):
        @pl.when(pl.program_id(0) == 0)
        def _():
            pltpu.sync_copy(w_hbm, w_ref)
            dg_ref[...] = jnp.zeros_like(dg_ref)

        xn, r = _rms(x_ref[...])
        gv = g_ref[...]
        h_ref[...] = (xn * gv).astype(BF16)
        dh = _dot_nt(dz_ref[...].astype(BF16), w_ref[:, P_Z:P_XBC])
        dh += _dot_nt(dxbc_ref[...].astype(BF16), w_ref[:, P_XBC:P_U5])
        dh += _dot_nt(du5_ref[...].astype(BF16), w_ref[:, P_U5:P_G])
        dh += _dot_nt(dgt_ref[...].astype(BF16), w_ref[:, P_G:P_DT])
        dh += _dot_nt(ddt_ref[...].astype(BF16), w_ref[:, P_DT:P_END])
        dg_ref[...] += jnp.sum(dh * xn, axis=0, keepdims=True)
        dx_ref[...] = dx1_ref[...] + _rms_bwd(xn, r, dh * gv)

    return _pc(
        body, name="inproj_bwd", grid=(T // tm,),
        in_specs=[_row_spec(tm, 1024), _row_spec(tm, 1024), _row_spec(tm, 1024), _row_spec(tm, 2048),
                  _row_spec(tm, 512), _row_spec(tm, 2048), _row_spec(tm, DT_PAD), _const_spec((1, 1024)), _hbm_spec()],
        out_specs=[_row_spec(tm, 1024), _row_spec(tm, 1024), _const_spec((1, 1024))],
        out_shape=[jax.ShapeDtypeStruct((T, 1024), F32), jax.ShapeDtypeStruct((T, 1024), BF16),
                   jax.ShapeDtypeStruct((1, 1024), F32)],
        scratch_shapes=[pltpu.VMEM((D_MODEL, P_END), BF16)],
        compiler_params=_cparams(("arbitrary",)),
    )(x, dx1, dz, dxbc, du5, dgt, ddt, g, wp)


def _conv_fwd(xbc_raw, dt_raw, conv_w, conv_b, dt_bias):
    T = xbc_raw.shape[0]
    tm = TOKEN_TILE

    def body(u_ref, dtr_ref, w_ref, b_ref, db_ref, act_ref, dt_ref, ext_ref):
        @pl.when(pl.program_id(0) == 0)
        def _():
            ext_ref[0:HALO, :] = jnp.zeros((HALO, CONV_DIM), F32)

        ext_ref[HALO:, :] = u_ref[...]
        for c0 in range(0, CONV_DIM, CONV_COLS):
            cols = slice(c0, c0 + CONV_COLS)
            taps = [w_ref[k:k + 1, cols] for k in range(CONV_K)]
            bias = b_ref[:, cols]
            for r0 in range(0, tm, CONV_ROWS_BLK):
                y = bias + taps[0] * ext_ref[pl.ds(HALO - (CONV_K - 1) + r0, CONV_ROWS_BLK), cols]
                for k in range(1, CONV_K):
                    y += taps[k] * ext_ref[pl.ds(HALO - (CONV_K - 1) + k + r0, CONV_ROWS_BLK), cols]
                act_ref[r0:r0 + CONV_ROWS_BLK, cols] = y * _sigmoid(y)
        ext_ref[0:HALO, :] = u_ref[tm - HALO:tm, :]
        dt_ref[...] = _softplus(dtr_ref[...] + db_ref[...])

    return _pc(
        body, name="conv_fwd", grid=(T // tm,),
        in_specs=[_row_spec(tm, CONV_DIM), _row_spec(tm, DT_PAD), _const_spec((CONV_K, CONV_DIM)),
                  _const_spec((1, CONV_DIM)), _const_spec((1, DT_PAD))],
        out_specs=[_row_spec(tm, CONV_DIM), _row_spec(tm, DT_PAD)],
        out_shape=[jax.ShapeDtypeStruct((T, CONV_DIM), F32), jax.ShapeDtypeStruct((T, DT_PAD), F32)],
        scratch_shapes=[pltpu.VMEM((tm + HALO, CONV_DIM), F32)],
        compiler_params=_cparams(("arbitrary",)),
    )(xbc_raw, dt_raw, conv_w, conv_b, dt_bias)


def _conv_bwd(xbc_raw, dt_raw, dxs_a, dxs_b, dB, dC, ddt, conv_w, conv_b, dt_bias):
    T = xbc_raw.shape[0]
    tm = TOKEN_TILE
    n = T // tm
    hb = tm // HALO

    def rev(width):
        return pl.BlockSpec((tm, width), lambda i: (n - 1 - i, 0))

    def body(u_ref, up_ref, dtr_ref, dxa_ref, dxb_ref, dB_ref, dC_ref, ddt_ref, w_ref, b_ref, db_ref,
             du_ref, ddtr_ref, dw_ref, dcb_ref, ddb_ref, ext_ref, dye_ref):
        i = pl.program_id(0)

        @pl.when(i == 0)
        def _():
            dye_ref[tm:, :] = jnp.zeros((HALO, CONV_DIM), F32)
            dw_ref[...] = jnp.zeros_like(dw_ref)
            dcb_ref[...] = jnp.zeros_like(dcb_ref)
            ddb_ref[...] = jnp.zeros_like(ddb_ref)

        first = (i == n - 1).astype(F32)
        ext_ref[0:HALO, :] = up_ref[...] * (1.0 - first)
        ext_ref[HALO:, :] = u_ref[...]
        for c0 in range(0, CONV_DIM, CONV_COLS):
            cols = slice(c0, c0 + CONV_COLS)
            taps = [w_ref[k:k + 1, cols] for k in range(CONV_K)]
            bias = b_ref[:, cols]
            acc_b = jnp.zeros((HALO, CONV_COLS), F32)
            acc_w = [jnp.zeros((HALO, CONV_COLS), F32) for _ in range(CONV_K)]
            for r0 in range(0, tm, CONV_ROWS_BLK):
                rows = slice(r0, r0 + CONV_ROWS_BLK)
                us = [ext_ref[pl.ds(HALO - (CONV_K - 1) + k + r0, CONV_ROWS_BLK), cols] for k in range(CONV_K)]
                y = bias + taps[0] * us[0]
                for k in range(1, CONV_K):
                    y += taps[k] * us[k]
                s = _sigmoid(y)
                if c0 < SSD_INNER:
                    dact = dxa_ref[rows, cols] + dxb_ref[rows, cols]
                elif c0 < SSD_INNER + 512:
                    dact = dB_ref[rows, c0 - SSD_INNER:c0 - SSD_INNER + CONV_COLS]
                else:
                    dact = dC_ref[rows, c0 - SSD_INNER - 512:c0 - SSD_INNER - 512 + CONV_COLS]
                dy = dact * (s * (1.0 + y * (1.0 - s)))
                dye_ref[rows, cols] = dy
                acc_b += jnp.sum(dy.reshape(CONV_ROWS_BLK // HALO, HALO, CONV_COLS), axis=0)
                for k in range(CONV_K):
                    acc_w[k] += jnp.sum((dy * us[k]).reshape(CONV_ROWS_BLK // HALO, HALO, CONV_COLS), axis=0)
            dcb_ref[:, cols] += jnp.sum(acc_b, axis=0, keepdims=True)
            for k in range(CONV_K):
                dw_ref[k:k + 1, cols] += jnp.sum(acc_w[k], axis=0, keepdims=True)
        for c0 in range(0, CONV_DIM, CONV_COLS):
            cols = slice(c0, c0 + CONV_COLS)
            taps = [w_ref[k:k + 1, cols] for k in range(CONV_K)]
            for r0 in range(0, tm, CONV_ROWS_BLK):
                du = taps[0] * dye_ref[pl.ds(CONV_K - 1 + r0, CONV_ROWS_BLK), cols]
                for k in range(1, CONV_K):
                    du += taps[k] * dye_ref[pl.ds(CONV_K - 1 - k + r0, CONV_ROWS_BLK), cols]
                du_ref[r0:r0 + CONV_ROWS_BLK, cols] = du
        dye_ref[tm:, :] = dye_ref[0:HALO, :]
        sg = _sigmoid(dtr_ref[...] + db_ref[...])
        ddtr = ddt_ref[...] * sg
        ddtr_ref[...] = ddtr
        ddb_ref[...] += jnp.sum(ddtr, axis=0, keepdims=True)

    prev_spec = pl.BlockSpec((HALO, CONV_DIM), lambda i: (jnp.maximum((n - 1 - i) * hb - 1, 0), 0))
    return _pc(
        body, name="conv_bwd", grid=(n,),
        in_specs=[rev(CONV_DIM), prev_spec, rev(DT_PAD), rev(1024), rev(1024), rev(512), rev(512), rev(DT_PAD),
                  _const_spec((CONV_K, CONV_DIM)), _const_spec((1, CONV_DIM)), _const_spec((1, DT_PAD))],
        out_specs=[rev(CONV_DIM), rev(DT_PAD), _const_spec((HALO, CONV_DIM)), _const_spec((1, CONV_DIM)),
                   _const_spec((1, DT_PAD))],
        out_shape=[jax.ShapeDtypeStruct((T, CONV_DIM), F32), jax.ShapeDtypeStruct((T, DT_PAD), F32),
                   jax.ShapeDtypeStruct((HALO, CONV_DIM), F32), jax.ShapeDtypeStruct((1, CONV_DIM), F32),
                   jax.ShapeDtypeStruct((1, DT_PAD), F32)],
        scratch_shapes=[pltpu.VMEM((tm + HALO, CONV_DIM), F32), pltpu.VMEM((tm + HALO, CONV_DIM), F32)],
        compiler_params=_cparams(("arbitrary",)),
    )(xbc_raw, xbc_raw, dt_raw, dxs_a, dxs_b, dB, dC, ddt, conv_w, conv_b, dt_bias)


GROUP_LANES = SSD_HPG * SSD_HEADDIM


def _ssd_expanders():
    head = jnp.arange(DT_PAD)[:, None]
    to_wide = (jnp.arange(SSD_INNER)[None, :] // SSD_HEADDIM == head).astype(BF16)
    to_cols = (jnp.arange(SSD_HEADS * SSD_CHUNK)[None, :] // SSD_CHUNK == head).astype(BF16)
    return to_wide, to_wide.T, to_cols


def _ssd_prep(dt_ref, alog_ref, wide_ref, cols_ref):
    q = SSD_CHUNK
    a = -jnp.exp(alog_ref[...])
    dtv = dt_ref[...]
    la = dtv * a
    row = lax.broadcasted_iota(jnp.int32, (q, q), 0)
    col = lax.broadcasted_iota(jnp.int32, (q, q), 1)
    tri = (col <= row).astype(BF16)
    cum = _dot_split_rhs(tri, la, 3)
    cum_t = _dot_split(la, tri, 3, (((0,), (1,)), ((), ())))
    dtw = _dot_split(dtv, wide_ref[...], 2)
    cumw = _dot_split(cum, wide_ref[...], 3)
    segcol = _dot_split(cum, cols_ref[...], 3)
    return a, dtv, row, col, tri, cum_t, dtw, cumw, segcol


def _decay(segcol, cum_t, h, keep):
    return jnp.where(keep, jnp.exp(jnp.minimum(segcol[:, 128 * h:128 * h + 128] - cum_t[h:h + 1, :], 0.0)), 0.0)


def _decay_t(segcol, cum_t, h, keep_t):
    return jnp.where(keep_t, jnp.exp(jnp.minimum(cum_t[h:h + 1, :] - segcol[:, 128 * h:128 * h + 128], 0.0)), 0.0)


def _ssd_fwd(xbc_act, dt, alog):
    T = xbc_act.shape[0]
    q = SSD_CHUNK
    nc = T // q
    to_wide, _, to_cols = _ssd_expanders()

    def body(xbc_ref, dt_ref, alog_ref, wide_ref, cols_ref, y_ref, sp_ref, st_ref, xd_ref, xde_ref):
        @pl.when(pl.program_id(0) == 0)
        def _():
            st_ref[...] = jnp.zeros_like(st_ref)

        a, dtv, row, col, tri, cum_t, dtw, cumw, segcol = _ssd_prep(dt_ref, alog_ref, wide_ref, cols_ref)
        clw = cumw[q - 1:q, :]
        ecw = jnp.exp(cumw)
        xd = xbc_ref[:, 0:SSD_INNER] * dtw
        xd_ref[...] = xd.astype(BF16)
        xde_ref[...] = (xd * jnp.exp(clw - cumw)).astype(BF16)
        cdw = jnp.exp(clw)
        keep = col <= row
        sp_ref[0] = st_ref[...]
        for g in range(SSD_GROUPS):
            gl = slice(GROUP_LANES * g, GROUP_LANES * (g + 1))
            bb = xbc_ref[:, 1024 + 128 * g:1152 + 128 * g].astype(BF16)
            cb = xbc_ref[:, 1536 + 128 * g:1664 + 128 * g].astype(BF16)
            gm = _dot_nt(cb, bb)
            stp = st_ref[g]
            yoff = _dot(cb, stp.astype(BF16)) * ecw[:, gl]
            for r in range(SSD_HPG):
                h = SSD_HPG * g + r
                m = (gm * _decay(segcol, cum_t, h, keep)).astype(BF16)
                y_ref[:, 64 * h:64 * h + 64] = _dot(m, xd_ref[:, 64 * h:64 * h + 64]) + yoff[:, 64 * r:64 * r + 64]
            st_ref[g] = stp * cdw[:, gl] + _dot_tn(bb, xde_ref[:, gl])

    return _pc(
        body, name="ssd_fwd", grid=(nc,),
        in_specs=[_row_spec(q, CONV_DIM), _row_spec(q, DT_PAD), _const_spec((1, DT_PAD)),
                  _const_spec(to_wide.shape), _const_spec(to_cols.shape)],
        out_specs=[_row_spec(q, SSD_INNER),
                   pl.BlockSpec((1, SSD_GROUPS, SSD_STATE, GROUP_LANES), lambda i: (i, 0, 0, 0))],
        out_shape=[jax.ShapeDtypeStruct((T, SSD_INNER), F32),
                   jax.ShapeDtypeStruct((nc, SSD_GROUPS, SSD_STATE, GROUP_LANES), F32)],
        scratch_shapes=[pltpu.VMEM((SSD_GROUPS, SSD_STATE, GROUP_LANES), F32), pltpu.VMEM((q, SSD_INNER), BF16),
                        pltpu.VMEM((q, SSD_INNER), BF16)],
        compiler_params=_cparams(("arbitrary",)),
    )(xbc_act, dt, alog, to_wide, to_cols)


def _ssd_bwd(xbc_act, dt, alog, sprev, dy):
    T = xbc_act.shape[0]
    q = SSD_CHUNK
    nc = T // q
    to_wide, to_heads, to_cols = _ssd_expanders()

    def rev(width):
        return pl.BlockSpec((q, width), lambda i: (nc - 1 - i, 0))

    def body(xbc_ref, dt_ref, alog_ref, sp_ref, dy_ref, wide_ref, heads_ref, cols_ref,
             dxs_ref, dB_ref, dC_ref, ddt_ref, dalog_ref, ds_ref, xd_ref, dxd_ref):
        i = pl.program_id(0)

        @pl.when(i == 0)
        def _():
            ds_ref[...] = jnp.zeros_like(ds_ref)
            dalog_ref[...] = jnp.zeros_like(dalog_ref)

        a, dtv, row, col, tri, cum_t, dtw, cumw, segcol = _ssd_prep(dt_ref, alog_ref, wide_ref, cols_ref)
        clw = cumw[q - 1:q, :]
        ecw = jnp.exp(cumw)
        dew = jnp.exp(clw - cumw)
        cdw = jnp.exp(clw)
        xs = xbc_ref[:, 0:SSD_INNER]
        xd = xs * dtw
        xd_ref[...] = xd.astype(BF16)
        dyv = dy_ref[...]
        dye = (dyv * ecw).astype(BF16)
        xde = (xd * dew).astype(BF16)
        keep = col <= row
        keep_t = col >= row
        rows_k = lax.broadcasted_iota(jnp.int32, (SSD_HPG * q, DT_PAD), 0) // q
        lanes_k = lax.broadcasted_iota(jnp.int32, (SSD_HPG * q, DT_PAD), 1)
        dcw_parts = []
        dcum = jnp.zeros((q, DT_PAD), F32)
        for g in range(SSD_GROUPS):
            gl = slice(GROUP_LANES * g, GROUP_LANES * (g + 1))
            bb = xbc_ref[:, 1024 + 128 * g:1152 + 128 * g].astype(BF16)
            cb = xbc_ref[:, 1536 + 128 * g:1664 + 128 * g].astype(BF16)
            gm = _dot_nt(cb, bb)
            gmt = _dot_nt(bb, cb)
            stp = sp_ref[0, g]
            dst = ds_ref[g]
            stpb = stp.astype(BF16)
            dstb = dst.astype(BF16)
            yoff = _dot(cb, stpb) * ecw[:, gl]
            dcg = _dot_nt(dye[:, gl], stpb)
            ds_ref[g] = dst * cdw[:, gl] + _dot_tn(cb, dye[:, gl])
            dlast = jnp.sum(dst * stp, axis=0, keepdims=True) * cdw[:, gl]
            dbg = _dot_nt(xde[:, gl], dstb)
            w = _dot(bb, dstb) * dew[:, gl]
            wx = w * xd[:, gl]
            dlast = dlast + jnp.sum(wx, axis=0, keepdims=True)
            dcw_parts.append(dyv[:, gl] * yoff - wx
                             + jnp.where(lax.broadcasted_iota(jnp.int32, (q, 1), 0) == q - 1, dlast, 0.0))
            dgm = jnp.zeros((q, q), F32)
            diag = []
            for r in range(SSD_HPG):
                h = SSD_HPG * g + r
                hl = slice(64 * h, 64 * h + 64)
                dyb = dy_ref[:, hl].astype(BF16)
                xdh = xd_ref[:, hl]
                dm = _dot_nt(dyb, xdh)
                dmt = _dot_nt(xdh, dyb)
                dec = _decay(segcol, cum_t, h, keep)
                mt = gmt * _decay_t(segcol, cum_t, h, keep_t)
                dgm += dm * dec
                diag.append(dm * (gm * dec) - dmt * mt)
                dxd_ref[:, hl] = _dot(mt.astype(BF16), dyb) + w[:, 64 * r:64 * r + 64]
            onehots = (lanes_k == SSD_HPG * g + rows_k).astype(BF16)
            dcum += _dot_split(jnp.concatenate(diag, axis=1), onehots, 2)
            dgb = dgm.astype(BF16)
            dC_ref[:, 128 * g:128 * g + 128] = dcg + _dot(dgb, bb)
            dB_ref[:, 128 * g:128 * g + 128] = dbg + _dot_tn(dgb, cb)
        dxd = dxd_ref[...]
        dxs_ref[...] = dxd * dtw
        dcum += _dot_split(jnp.concatenate(dcw_parts, axis=1), heads_ref[...], 2)
        dla = _dot_split_rhs(tri, dcum, 3, (((0,), (0,)), ((), ())))
        ddt_ref[...] = _dot_split(xs * dxd, heads_ref[...], 2) + dla * a
        dalog_ref[...] += jnp.sum(dla * dtv, axis=0, keepdims=True)

        @pl.when(i == nc - 1)
        def _():
            dalog_ref[...] = dalog_ref[...] * a

    st_spec = pl.BlockSpec((1, SSD_GROUPS, SSD_STATE, GROUP_LANES), lambda i: (nc - 1 - i, 0, 0, 0))
    return _pc(
        body, name="ssd_bwd", grid=(nc,),
        in_specs=[rev(CONV_DIM), rev(DT_PAD), _const_spec((1, DT_PAD)), st_spec, rev(SSD_INNER),
                  _const_spec(to_wide.shape), _const_spec(to_heads.shape), _const_spec(to_cols.shape)],
        out_specs=[rev(SSD_INNER), rev(512), rev(512), rev(DT_PAD), _const_spec((1, DT_PAD))],
        out_shape=[jax.ShapeDtypeStruct((T, SSD_INNER), F32), jax.ShapeDtypeStruct((T, 512), F32),
                   jax.ShapeDtypeStruct((T, 512), F32), jax.ShapeDtypeStruct((T, DT_PAD), F32),
                   jax.ShapeDtypeStruct((1, DT_PAD), F32)],
        scratch_shapes=[pltpu.VMEM((SSD_GROUPS, SSD_STATE, GROUP_LANES), F32), pltpu.VMEM((q, SSD_INNER), BF16),
                        pltpu.VMEM((q, SSD_INNER), F32)],
        compiler_params=_cparams(("arbitrary",)),
    )(xbc_act, dt, alog, sprev, dy, to_wide, to_heads, to_cols)


def _s5_disc_vals(a_re, a_im, log_dt, b_re, b_im):
    dt = jnp.exp(log_dt)
    mag = jnp.exp(a_re * dt)
    ab_re = mag * jnp.cos(a_im * dt)
    ab_im = mag * jnp.sin(a_im * dt)
    den = a_re * a_re + a_im * a_im
    nr = ab_re - 1.0
    ni = ab_im
    coef_re = (nr * a_re + ni * a_im) / den
    coef_im = (ni * a_re - nr * a_im) / den
    bb_re = coef_re * b_re - coef_im * b_im
    bb_im = coef_re * b_im + coef_im * b_re
    return ab_re, ab_im, bb_re, bb_im


def _s5_disc(a_re, a_im, log_dt, b_re, b_im):
    def body(ar, ai, ld, br, bi, o1, o2, o3, o4):
        o1[...], o2[...], o3[...], o4[...] = _s5_disc_vals(ar[...], ai[...], ld[...], br[...], bi[...])

    return _pc(
        body, name="s5_disc",
        out_shape=[jax.ShapeDtypeStruct((S5_STATES, 1), F32), jax.ShapeDtypeStruct((S5_STATES, 1), F32),
                   jax.ShapeDtypeStruct((S5_STATES, 16), F32), jax.ShapeDtypeStruct((S5_STATES, 16), F32)],
    )(a_re, a_im, log_dt, b_re, b_im)


def _s5_disc_bwd(a_re, a_im, log_dt, b_re, b_im, d_ab_re, d_ab_im, d_bb_re, d_bb_im):
    def body(ar, ai, ld, br, bi, g1, g2, g3, g4, o1, o2, o3, o4, o5):
        _, vjp = jax.vjp(_s5_disc_vals, ar[...], ai[...], ld[...], br[...], bi[...])
        d1, d2, d3, d4, d5 = vjp((g1[...], g2[...], g3[...], g4[...]))
        o1[...] = d1
        o2[...] = d2
        grp = lax.broadcasted_iota(jnp.int32, (32, S5_STATES), 0)
        st = lax.broadcasted_iota(jnp.int32, (32, S5_STATES), 1)
        sel = (st // 64 == grp).astype(F32)
        o3[...] = _dot_hi(sel, d3)
        o4[...] = d4
        o5[...] = d5

    return _pc(
        body, name="s5_disc_bwd",
        out_shape=[jax.ShapeDtypeStruct((S5_STATES, 1), F32), jax.ShapeDtypeStruct((S5_STATES, 1), F32),
                   jax.ShapeDtypeStruct((32, 1), F32),
                   jax.ShapeDtypeStruct((S5_STATES, 16), F32), jax.ShapeDtypeStruct((S5_STATES, 16), F32)],
    )(a_re, a_im, log_dt, b_re, b_im, d_ab_re, d_ab_im, d_bb_re, d_bb_im)


def _cmul_add(xr, xi, pr, pi, yr, yi):
    return xr + pr * yr - pi * yi, xi + pr * yi + pi * yr


def _powers(ar, ai, n):
    out = [(ar, ai)]
    for _ in range(n - 1):
        pr, pi = out[-1]
        out.append((pr * pr - pi * pi, 2.0 * pr * pi))
    return out


_BW = S5_STATES // S5_BLOCKS
_BI = S5_WIDTH // S5_BLOCKS
SUB = 8
S5_ROWS = S5_CHUNK // SUB


def _scan8(br, bi, pws, rowin, reverse):
    k = 1
    for pr, pi in pws:
        if reverse:
            keep = rowin < SUB - k
            sr = jnp.where(keep, pltpu.roll(br, SUB - k, 0), 0.0)
            si = jnp.where(keep, pltpu.roll(bi, SUB - k, 0), 0.0)
        else:
            keep = rowin >= k
            sr = jnp.where(keep, pltpu.roll(br, k, 0), 0.0)
            si = jnp.where(keep, pltpu.roll(bi, k, 0), 0.0)
        br, bi = _cmul_add(br, bi, pr, pi, sr, si)
        k *= 2
    return br, bi


def _s5_tables(ab_ref, tab_ref, reverse):
    rowin = lax.broadcasted_iota(jnp.int32, (SUB, 1), 0)
    ar = ab_ref[0:1, :]
    ai = -ab_ref[1:2, :] if reverse else ab_ref[1:2, :]
    hit = rowin == (SUB - 1 if reverse else 0)
    zero = jnp.zeros((SUB, S5_STATES), F32)
    pr, pi = _scan8(jnp.where(hit, ar, 0.0) + zero, jnp.where(hit, ai, 0.0) + zero, _powers(ar, ai, 3), rowin, reverse)
    tab_ref[0:SUB, :] = pr
    tab_ref[SUB:2 * SUB, :] = pi


def _s5_fwd(u5, wb4, wc4, ab, dvec, rider=None):
    T = u5.shape[0]
    q = S5_CHUNK
    nc = T // q

    def body(u_ref, wb_ref, wc_ref, ab_ref, d_ref, y_ref, sp_ref, carry_ref, tab_ref, sr_ref, si_ref):
        i = pl.program_id(0)
        rowin = lax.broadcasted_iota(jnp.int32, (SUB, 1), 0)

        @pl.when(i == 0)
        def _():
            carry_ref[...] = jnp.zeros_like(carry_ref)
            _s5_tables(ab_ref, tab_ref, False)

        sp_ref[0] = carry_ref[...]
        for j in range(S5_BLOCKS):
            bu = _dot(u_ref[:, _BI * j:_BI * (j + 1)].astype(BF16), wb_ref[j])
            sr_ref[:, :, _BW * j:_BW * (j + 1)] = bu[:, :_BW].reshape(S5_ROWS, SUB, _BW)
            si_ref[:, :, _BW * j:_BW * (j + 1)] = bu[:, _BW:].reshape(S5_ROWS, SUB, _BW)
        pws = _powers(ab_ref[0:1, :], ab_ref[1:2, :], 3)
        tr, ti = tab_ref[0:SUB, :], tab_ref[SUB:2 * SUB, :]
        cr, ci = carry_ref[0:1, :], carry_ref[1:2, :]
        for k in range(S5_ROWS):
            sr, si = _scan8(sr_ref[k], si_ref[k], pws, rowin, False)
            sr, si = _cmul_add(sr, si, tr, ti, cr, ci)
            sr_ref[k] = sr
            si_ref[k] = si
            cr, ci = sr[SUB - 1:SUB, :], si[SUB - 1:SUB, :]
        carry_ref[0:1, :] = cr
        carry_ref[1:2, :] = ci
        for j in range(S5_BLOCKS):
            sl = slice(_BW * j, _BW * (j + 1))
            ul = slice(_BI * j, _BI * (j + 1))
            s = jnp.concatenate([sr_ref[:, :, sl].reshape(q, _BW), si_ref[:, :, sl].reshape(q, _BW)], axis=1).astype(BF16)
            y_ref[:, ul] = _dot(s, wc_ref[j]) + d_ref[:, ul] * u_ref[:, ul]

    return _call(
        body, rider, name="s5_fwd", grid=(nc,),
        in_specs=[_row_spec(q, S5_WIDTH), _const_spec((S5_BLOCKS, _BI, 2 * _BW)), _const_spec((S5_BLOCKS, 2 * _BW, _BI)),
                  _const_spec((8, S5_STATES)), _const_spec((1, S5_WIDTH))],
        out_specs=[_row_spec(q, S5_WIDTH), pl.BlockSpec((1, 8, S5_STATES), lambda i: (i, 0, 0))],
        out_shape=[jax.ShapeDtypeStruct((T, S5_WIDTH), F32), jax.ShapeDtypeStruct((nc, 8, S5_STATES), F32)],
        scratch_shapes=[pltpu.VMEM((8, S5_STATES), F32), pltpu.VMEM((2 * SUB, S5_STATES), F32),
                        pltpu.VMEM((S5_ROWS, SUB, S5_STATES), F32), pltpu.VMEM((S5_ROWS, SUB, S5_STATES), F32)],
        compiler_params=_cparams(("arbitrary",)),
    )(u5, wb4, wc4, ab, dvec)


def _s5_bwd(u5, dy5, wb4, wc4, ab, dvec, sprev, rider=None):
    T = u5.shape[0]
    q = S5_CHUNK
    nc = T // q

    def rev(width):
        return pl.BlockSpec((q, width), lambda i: (nc - 1 - i, 0))

    def body(u_ref, dy_ref, wb_ref, wc_ref, ab_ref, d_ref, sp_ref, du_ref, dwb_ref, dwc_ref, dab_ref, dd_ref,
             carry_ref, tab_ref, rtab_ref, sr_ref, si_ref, lr_ref, li_ref):
        i = pl.program_id(0)
        rowin = lax.broadcasted_iota(jnp.int32, (SUB, 1), 0)

        @pl.when(i == 0)
        def _():
            carry_ref[...] = jnp.zeros_like(carry_ref)
            dwb_ref[...] = jnp.zeros_like(dwb_ref)
            dwc_ref[...] = jnp.zeros_like(dwc_ref)
            dab_ref[...] = jnp.zeros_like(dab_ref)
            dd_ref[...] = jnp.zeros_like(dd_ref)
            _s5_tables(ab_ref, tab_ref, False)
            _s5_tables(ab_ref, rtab_ref, True)

        for j in range(S5_BLOCKS):
            sl = slice(_BW * j, _BW * (j + 1))
            ul = slice(_BI * j, _BI * (j + 1))
            bu = _dot(u_ref[:, ul].astype(BF16), wb_ref[j])
            sr_ref[:, :, sl] = bu[:, :_BW].reshape(S5_ROWS, SUB, _BW)
            si_ref[:, :, sl] = bu[:, _BW:].reshape(S5_ROWS, SUB, _BW)
            ds = _dot_nt(dy_ref[:, ul].astype(BF16), wc_ref[j])
            lr_ref[:, :, sl] = ds[:, :_BW].reshape(S5_ROWS, SUB, _BW)
            li_ref[:, :, sl] = ds[:, _BW:].reshape(S5_ROWS, SUB, _BW)
        ar, ai = ab_ref[0:1, :], ab_ref[1:2, :]
        pws = _powers(ar, ai, 3)
        tr, ti = tab_ref[0:SUB, :], tab_ref[SUB:2 * SUB, :]
        cr, ci = sp_ref[0, 0:1, :], sp_ref[0, 1:2, :]
        for k in range(S5_ROWS):
            sr, si = _scan8(sr_ref[k], si_ref[k], pws, rowin, False)
            sr, si = _cmul_add(sr, si, tr, ti, cr, ci)
            sr_ref[k] = sr
            si_ref[k] = si
            cr, ci = sr[SUB - 1:SUB, :], si[SUB - 1:SUB, :]
        pws = _powers(ar, -ai, 3)
        tr, ti = rtab_ref[0:SUB, :], rtab_ref[SUB:2 * SUB, :]
        cr, ci = carry_ref[0:1, :], carry_ref[1:2, :]
        acc_r = jnp.zeros((SUB, S5_STATES), F32)
        acc_i = jnp.zeros((SUB, S5_STATES), F32)
        for k in reversed(range(S5_ROWS)):
            lr, li = _scan8(lr_ref[k], li_ref[k], pws, rowin, True)
            lr, li = _cmul_add(lr, li, tr, ti, cr, ci)
            lr_ref[k] = lr
            li_ref[k] = li
            cr, ci = lr[0:1, :], li[0:1, :]
            if k > 0:
                before_r, before_i = sr_ref[k - 1, SUB - 1:SUB, :], si_ref[k - 1, SUB - 1:SUB, :]
            else:
                before_r, before_i = sp_ref[0, 0:1, :], sp_ref[0, 1:2, :]
            keep = rowin >= 1
            pr = jnp.where(keep, pltpu.roll(sr_ref[k], 1, 0), before_r)
            pi = jnp.where(keep, pltpu.roll(si_ref[k], 1, 0), before_i)
            acc_r += lr * pr + li * pi
            acc_i += li * pr - lr * pi
        carry_ref[0:1, :] = cr
        carry_ref[1:2, :] = ci
        dab_ref[0:1, :] += jnp.sum(acc_r, axis=0, keepdims=True)
        dab_ref[1:2, :] += jnp.sum(acc_i, axis=0, keepdims=True)
        for j in range(S5_BLOCKS):
            sl = slice(_BW * j, _BW * (j + 1))
            ul = slice(_BI * j, _BI * (j + 1))
            u = u_ref[:, ul]
            dy = dy_ref[:, ul]
            dyb = dy.astype(BF16)
            lam = jnp.concatenate([lr_ref[:, :, sl].reshape(q, _BW), li_ref[:, :, sl].reshape(q, _BW)], axis=1).astype(BF16)
            s = jnp.concatenate([sr_ref[:, :, sl].reshape(q, _BW), si_ref[:, :, sl].reshape(q, _BW)], axis=1).astype(BF16)
            du_ref[:, ul] = _dot_nt(lam, wb_ref[j]) + d_ref[:, ul] * dy
            dwb_ref[j] += _dot_tn(u.astype(BF16), lam)
            dwc_ref[j] += _dot_tn(s, dyb)
            dd_ref[:, ul] += jnp.sum(dy * u, axis=0, keepdims=True)

    big = pltpu.VMEM((S5_ROWS, SUB, S5_STATES), F32)
    return _call(
        body, rider, name="s5_bwd", grid=(nc,),
        in_specs=[rev(S5_WIDTH), rev(S5_WIDTH), _const_spec((S5_BLOCKS, _BI, 2 * _BW)), _const_spec((S5_BLOCKS, 2 * _BW, _BI)),
                  _const_spec((8, S5_STATES)), _const_spec((1, S5_WIDTH)),
                  pl.BlockSpec((1, 8, S5_STATES), lambda i: (nc - 1 - i, 0, 0))],
        out_specs=[rev(S5_WIDTH), _const_spec((S5_BLOCKS, _BI, 2 * _BW)), _const_spec((S5_BLOCKS, 2 * _BW, _BI)),
                   _const_spec((8, S5_STATES)), _const_spec((1, S5_WIDTH))],
        out_shape=[jax.ShapeDtypeStruct((T, S5_WIDTH), F32), jax.ShapeDtypeStruct((S5_BLOCKS, _BI, 2 * _BW), F32),
                   jax.ShapeDtypeStruct((S5_BLOCKS, 2 * _BW, _BI), F32), jax.ShapeDtypeStruct((8, S5_STATES), F32),
                   jax.ShapeDtypeStruct((1, S5_WIDTH), F32)],
        scratch_shapes=[pltpu.VMEM((8, S5_STATES), F32), pltpu.VMEM((2 * SUB, S5_STATES), F32),
                        pltpu.VMEM((2 * SUB, S5_STATES), F32), big, big, big, big],
        compiler_params=_cparams(("arbitrary",)),
    )(u5, dy5, wb4, wc4, ab, dvec, sprev)


def _merge_vals(ys, xs, z, y5, gates, dvec, gssd, glu_w, glu_b, wbr):
    sz = _sigmoid(z)
    qv = ys + dvec * xs
    pre = qv * (z * sz)
    yn, rs = [], []
    for gi in range(SSD_GROUPS):
        p, r = _rms(pre[:, 256 * gi:256 * (gi + 1)])
        yn.append(p)
        rs.append(r)
    yn = jnp.concatenate(yn, axis=1)
    ya = yn * gssd
    gel = _gelu(y5)
    sg = _sigmoid(_dot(gel.astype(BF16), glu_w) + glu_b)
    yb = gel * sg
    pa = _dot(ya.astype(BF16), wbr[0:SSD_INNER, :])
    pb = _dot(yb.astype(BF16), wbr[SSD_INNER:, :])
    s0 = _sigmoid(gates[:, :D_MODEL])
    s1 = _sigmoid(gates[:, D_MODEL:])
    merged = s0 * pa + s1 * pb
    return dict(sz=sz, qv=qv, yn=yn, rs=rs, ya=ya, gel=gel, sg=sg, yb=yb, pa=pa, pb=pb, s0=s0, s1=s1, merged=merged)


def _merge_specs(tm):
    acts = [_row_spec(tm, 1024), _row_spec(tm, 1024, 0), _row_spec(tm, 1024), _row_spec(tm, 512), _row_spec(tm, 2048),
            _row_spec(tm, 1024)]
    params = [_const_spec((1, 1024)), _const_spec((1, 1024)), _const_spec((512, 512)), _const_spec((1, 512)),
              _hbm_spec(), _hbm_spec()]
    return acts, params


def _merge_fwd(ys, xbc_act, z, y5, gates, x, dvec, gssd, glu_w, glu_b, wbr, wout):
    T = x.shape[0]
    tm = TOKEN_TILE
    acts, params = _merge_specs(tm)

    def body(ys_ref, xs_ref, z_ref, y5_ref, gt_ref, x_ref, dv_ref, gs_ref, gw_ref, gb_ref, wbr_hbm, wout_hbm, x1_ref,
             wbr_ref, wout_ref):
        @pl.when(pl.program_id(0) == 0)
        def _():
            pltpu.sync_copy(wbr_hbm, wbr_ref)
            pltpu.sync_copy(wout_hbm, wout_ref)

        v = _merge_vals(ys_ref[...], xs_ref[...], z_ref[...], y5_ref[...], gt_ref[...], dv_ref[...], gs_ref[...],
                        gw_ref[...], gb_ref[...], wbr_ref)
        x1_ref[...] = x_ref[...] + _dot(v["merged"].astype(BF16), wout_ref[...])

    return _pc(
        body, name="merge_fwd", grid=(T // tm,),
        in_specs=acts + params, out_specs=_row_spec(tm, 1024),
        out_shape=jax.ShapeDtypeStruct((T, 1024), F32),
        scratch_shapes=[pltpu.VMEM((1536, 1024), BF16), pltpu.VMEM((1024, 1024), BF16)],
        compiler_params=_cparams(("arbitrary",)),
    )(ys, xbc_act, z, y5, gates, x, dvec, gssd, glu_w, glu_b, wbr, wout)


def _merge_bwd(ys, xbc_act, z, y5, gates, dx1, dvec, gssd, glu_w, glu_b, wbr, wout, head_sel, rider=None):
    T = dx1.shape[0]
    tm = TOKEN_TILE
    acts, params = _merge_specs(tm)

    def body(ys_ref, xs_ref, z_ref, y5_ref, gt_ref, dx1_ref, dv_ref, gs_ref, gw_ref, gb_ref, wbr_hbm, wout_hbm, hs_ref,
             dys_ref, dxs_ref, dz_ref, dy5_ref, dgt_ref, mg_ref, ya_ref, yb_ref, dpa_ref, dpb_ref, gel_ref, dpre_ref,
             ddv_ref, dgs_ref, dgb_ref, wbr_ref, wout_ref, ddacc_ref):
        i = pl.program_id(0)

        @pl.when(i == 0)
        def _():
            pltpu.sync_copy(wbr_hbm, wbr_ref)
            pltpu.sync_copy(wout_hbm, wout_ref)
            ddacc_ref[...] = jnp.zeros_like(ddacc_ref)
            dgs_ref[...] = jnp.zeros_like(dgs_ref)
            dgb_ref[...] = jnp.zeros_like(dgb_ref)

        ys, xs, z, y5, gates = ys_ref[...], xs_ref[...], z_ref[...], y5_ref[...], gt_ref[...]
        dvv, gsv, gw = dv_ref[...], gs_ref[...], gw_ref[...]
        v = _merge_vals(ys, xs, z, y5, gates, dvv, gsv, gw, gb_ref[...], wbr_ref)
        dmg = _dot_nt(dx1_ref[...].astype(BF16), wout_ref[...])
        s0, s1, pa, pb = v["s0"], v["s1"], v["pa"], v["pb"]
        dgt_ref[:, :D_MODEL] = dmg * pa * s0 * (1.0 - s0)
        dgt_ref[:, D_MODEL:] = dmg * pb * s1 * (1.0 - s1)
        dpa = (dmg * s0).astype(BF16)
        dpb = (dmg * s1).astype(BF16)
        dya = _dot_nt(dpa, wbr_ref[0:SSD_INNER, :])
        dyb = _dot_nt(dpb, wbr_ref[SSD_INNER:, :])
        gel, sg = v["gel"], v["sg"]
        dpre = (dyb * gel * sg * (1.0 - sg))
        dgb_ref[...] += jnp.sum(dpre, axis=0, keepdims=True)
        dpre_b = dpre.astype(BF16)
        dgel = dyb * sg + _dot_nt(dpre_b, gw)
        dy5_ref[...] = dgel * _gelu_grad(y5)
        yn = v["yn"]
        dgs_ref[...] += jnp.sum(dya * yn, axis=0, keepdims=True)
        dyn = dya * gsv
        dpre_a = jnp.concatenate(
            [_rms_bwd(yn[:, 256 * gi:256 * (gi + 1)], v["rs"][gi], dyn[:, 256 * gi:256 * (gi + 1)])
             for gi in range(SSD_GROUPS)], axis=1)
        sz, qv = v["sz"], v["qv"]
        dq = dpre_a * (z * sz)
        dz_ref[...] = dpre_a * qv * (sz * (1.0 + z * (1.0 - sz)))
        dys_ref[...] = dq
        dxs_ref[...] = dq * dvv
        ddacc_ref[...] += jnp.sum(dq * xs, axis=0, keepdims=True)
        mg_ref[...] = v["merged"].astype(BF16)
        ya_ref[...] = v["ya"].astype(BF16)
        yb_ref[...] = v["yb"].astype(BF16)
        dpa_ref[...] = dpa
        dpb_ref[...] = dpb
        gel_ref[...] = gel.astype(BF16)
        dpre_ref[...] = dpre_b

        @pl.when(i == pl.num_programs(0) - 1)
        def _():
            ddv_ref[...] = _dot_hi(ddacc_ref[...], hs_ref[...])

    outs = [(1024, F32), (1024, F32), (1024, F32), (512, F32), (2048, F32),
            (1024, BF16), (1024, BF16), (512, BF16), (1024, BF16), (1024, BF16), (512, BF16), (512, BF16)]
    return _call(
        body, rider, name="merge_bwd", grid=(T // tm,),
        in_specs=acts + params + [_const_spec((1024, DT_PAD))],
        out_specs=[_row_spec(tm, w) for w, _ in outs] + [_const_spec((1, DT_PAD)), _const_spec((1, 1024)), _const_spec((1, 512))],
        out_shape=[jax.ShapeDtypeStruct((T, w), d) for w, d in outs] + [
            jax.ShapeDtypeStruct((1, DT_PAD), F32), jax.ShapeDtypeStruct((1, 1024), F32), jax.ShapeDtypeStruct((1, 512), F32)],
        scratch_shapes=[pltpu.VMEM((1536, 1024), BF16), pltpu.VMEM((1024, 1024), BF16), pltpu.VMEM((1, 1024), F32)],
        compiler_params=_cparams(("arbitrary",)),
    )(ys, xbc_act, z, y5, gates, dx1, dvec, gssd, glu_w, glu_b, wbr, wout, head_sel)


def _mlp_fwd(x1, g, w1, w2):
    T = x1.shape[0]
    tm = TOKEN_TILE

    def body(x_ref, g_ref, w1_hbm, w2_hbm, o_ref, w1_ref, w2_ref):
        @pl.when(pl.program_id(0) == 0)
        def _():
            pltpu.sync_copy(w1_hbm, w1_ref)
            pltpu.sync_copy(w2_hbm, w2_ref)

        xv = x_ref[...]
        xn, _ = _rms(xv)
        h = (xn * g_ref[...]).astype(BF16)
        acc = xv
        for s in range(FF_SHARDS):
            rl = jnp.maximum(_dot(h, w1_ref[s]), 0.0)
            acc += _dot((rl * rl).astype(BF16), w2_ref[FF_SHARD * s:FF_SHARD * (s + 1), :])
        o_ref[...] = acc

    return _pc(
        body, name="mlp_fwd", grid=(T // tm,),
        in_specs=[_row_spec(tm, 1024), _const_spec((1, 1024)), _hbm_spec(), _hbm_spec()],
        out_specs=_row_spec(tm, 1024), out_shape=jax.ShapeDtypeStruct((T, 1024), F32),
        scratch_shapes=[pltpu.VMEM((FF_SHARDS, D_MODEL, FF_SHARD), BF16), pltpu.VMEM((D_FF, D_MODEL), BF16)],
        compiler_params=_cparams(("arbitrary",)),
    )(x1, g, w1, w2)


def _mlp_bwd(x1, dx2, g, w1, w2):
    T = x1.shape[0]
    tm = TOKEN_TILE

    def body(x_ref, dx2_ref, g_ref, w1_hbm, w2_hbm, dx1_ref, h_ref, act_ref, da_ref, dg_ref, w1_ref, w2_ref):
        @pl.when(pl.program_id(0) == 0)
        def _():
            pltpu.sync_copy(w1_hbm, w1_ref)
            pltpu.sync_copy(w2_hbm, w2_ref)
            dg_ref[...] = jnp.zeros_like(dg_ref)

        xn, r = _rms(x_ref[...])
        gv = g_ref[...]
        h = (xn * gv).astype(BF16)
        h_ref[...] = h
        dx2 = dx2_ref[...]
        dx2b = dx2.astype(BF16)
        dh = jnp.zeros((tm, D_MODEL), F32)
        for s in range(FF_SHARDS):
            ff = slice(FF_SHARD * s, FF_SHARD * (s + 1))
            rl = jnp.maximum(_dot(h, w1_ref[s]), 0.0)
            act_ref[:, ff] = (rl * rl).astype(BF16)
            da = (_dot_nt(dx2b, w2_ref[ff, :]) * (2.0 * rl)).astype(BF16)
            da_ref[:, ff] = da
            dh += _dot_nt(da, w1_ref[s])
        dg_ref[...] += jnp.sum(dh * xn, axis=0, keepdims=True)
        dx1_ref[...] = dx2 + _rms_bwd(xn, r, dh * gv)

    return _pc(
        body, name="mlp_bwd", grid=(T // tm,),
        in_specs=[_row_spec(tm, 1024), _row_spec(tm, 1024), _const_spec((1, 1024)), _hbm_spec(), _hbm_spec()],
        out_specs=[_row_spec(tm, 1024), _row_spec(tm, 1024), _row_spec(tm, D_FF), _row_spec(tm, D_FF), _const_spec((1, 1024))],
        out_shape=[jax.ShapeDtypeStruct((T, 1024), F32), jax.ShapeDtypeStruct((T, 1024), BF16),
                   jax.ShapeDtypeStruct((T, D_FF), BF16), jax.ShapeDtypeStruct((T, D_FF), BF16),
                   jax.ShapeDtypeStruct((1, 1024), F32)],
        scratch_shapes=[pltpu.VMEM((FF_SHARDS, D_MODEL, FF_SHARD), BF16), pltpu.VMEM((D_FF, D_MODEL), BF16)],
        compiler_params=_cparams(("arbitrary",)),
    )(x1, dx2, g, w1, w2)


def _loss_head(x2, target, g):
    T = x2.shape[0]
    tm = TOKEN_TILE

    def body(x_ref, t_ref, g_ref, dx_ref, loss_ref, dg_ref):
        @pl.when(pl.program_id(0) == 0)
        def _():
            loss_ref[...] = jnp.zeros_like(loss_ref)
            dg_ref[...] = jnp.zeros_like(dg_ref)

        xn, r = _rms(x_ref[...])
        gv = g_ref[...]
        err = xn * gv - t_ref[...]
        loss_ref[...] += jnp.sum(err * err, axis=0, keepdims=True) * (0.5 / D_MODEL)
        dy = err * (1.0 / D_MODEL)
        dg_ref[...] += jnp.sum(dy * xn, axis=0, keepdims=True)
        dx_ref[...] = _rms_bwd(xn, r, dy * gv)

    return _pc(
        body, name="loss_head", grid=(T // tm,),
        in_specs=[_row_spec(tm, 1024), _row_spec(tm, 1024), _const_spec((1, 1024))],
        out_specs=[_row_spec(tm, 1024), _const_spec((1, 1024)), _const_spec((1, 1024))],
        out_shape=[jax.ShapeDtypeStruct((T, 1024), F32), jax.ShapeDtypeStruct((1, 1024), F32),
                   jax.ShapeDtypeStruct((1, 1024), F32)],
        compiler_params=_cparams(("arbitrary",)),
    )(x2, target, g)


WGRAD_OUT_ELEMS = 2 * 1024 * 1024
WGRAD_TILE_BYTES = 4 * 1024 * 1024


def _wgrad(a, b, name, col_shards=None):
    T, K = a.shape
    N = b.shape[1]
    nb = N // col_shards if col_shards else min(N, 1024, max(128, WGRAD_OUT_ELEMS // K))
    tt = min(T, WGRAD_TOKENS)
    while tt * max(K * a.dtype.itemsize, nb * b.dtype.itemsize) > WGRAD_TILE_BYTES:
        tt //= 2
    assert N % nb == 0 and T % tt == 0
    if col_shards:
        out_spec = pl.BlockSpec((None, K, nb), lambda n, t: (n, 0, 0))
        out_shape = jax.ShapeDtypeStruct((col_shards, K, nb), F32)
    else:
        out_spec = pl.BlockSpec((K, nb), lambda n, t: (0, n))
        out_shape = jax.ShapeDtypeStruct((K, N), F32)

    def body(a_ref, b_ref, o_ref):
        @pl.when(pl.program_id(1) == 0)
        def _():
            o_ref[...] = jnp.zeros_like(o_ref)

        o_ref[...] += _dot_tn(a_ref[...].astype(BF16), b_ref[...].astype(BF16))

    return _pc(
        body, name=name, grid=(N // nb, T // tt),
        in_specs=[pl.BlockSpec((tt, K), lambda n, t: (t, 0)), pl.BlockSpec((tt, nb), lambda n, t: (t, n))],
        out_specs=out_spec, out_shape=out_shape,
        compiler_params=_cparams(("parallel", "arbitrary")),
    )(a, b)


def _s5_block_weights(bb_re, bb_im, c_re, c_im):
    eye = jnp.eye(8, dtype=F32)
    bre = bb_re.reshape(S5_BLOCKS, 8, 64, 16)
    bim = bb_im.reshape(S5_BLOCKS, 8, 64, 16)
    wb_re = jnp.einsum('jgpk,gh->jhkgp', bre, eye).reshape(S5_BLOCKS, _BI, _BW)
    wb_im = jnp.einsum('jgpk,gh->jhkgp', bim, eye).reshape(S5_BLOCKS, _BI, _BW)
    wb4 = jnp.concatenate([wb_re, wb_im], axis=2).astype(BF16)
    cre = c_re.reshape(S5_BLOCKS, 8, 16, 64)
    cim = c_im.reshape(S5_BLOCKS, 8, 16, 64)
    wc_re = jnp.einsum('jgkp,gh->jgphk', cre, eye).reshape(S5_BLOCKS, _BW, _BI)
    wc_im = jnp.einsum('jgkp,gh->jgphk', -cim, eye).reshape(S5_BLOCKS, _BW, _BI)
    wc4 = jnp.concatenate([wc_re, wc_im], axis=1).astype(BF16)
    return wb4, wc4


def _s5_block_grads(dwb4, dwc4):
    eye = jnp.eye(8, dtype=F32)
    dwb = dwb4.reshape(S5_BLOCKS, 8, 16, 2, 8, 64)
    dbb = jnp.einsum('jhkrgp,gh->rjgpk', dwb, eye).reshape(2, 32, 64, 16)
    dwc = dwc4.reshape(S5_BLOCKS, 2, 8, 64, 8, 16)
    dc = jnp.einsum('jrgphk,gh->rjgkp', dwc, eye).reshape(2, 32, 16, 64)
    return dbb[0], dbb[1], dc[0], -dc[1]


def _row(v, width=None):
    v = v.reshape(1, -1)
    if width is not None and v.shape[1] < width:
        v = jnp.concatenate([v, jnp.zeros((1, width - v.shape[1]), v.dtype)], axis=1)
    return v


def _local_step(x, target, p, comm=None):
    g_mix, g_mlp, g_fin = _row(p["norm_mix_g"]), _row(p["norm_mlp_g"]), _row(p["norm_final_g"])
    conv_b = _row(p["conv_b"])
    dt_bias = _row(p["dt_bias"], DT_PAD)
    alog = _row(p["a_log"], DT_PAD)
    dvec = _row(jnp.repeat(p["d_ssd"], SSD_HEADDIM))
    gssd = _row(p["ssd_norm_g"])
    s5d = _row(p["s5_d"])
    glu_b = _row(p["s5_glu_b"])
    head_sel = (jnp.arange(SSD_INNER)[:, None] // SSD_HEADDIM == jnp.arange(DT_PAD)[None, :]).astype(F32)

    a_re = p["s5_a_re"].reshape(S5_STATES, 1)
    a_im = p["s5_a_im"].reshape(S5_STATES, 1)
    log_dt = jnp.repeat(p["s5_log_dt"], 64).reshape(S5_STATES, 1)
    b_re = p["s5_b_re"].reshape(S5_STATES, 16)
    b_im = p["s5_b_im"].reshape(S5_STATES, 16)
    ab_re, ab_im, bb_re, bb_im = _s5_disc(a_re, a_im, log_dt, b_re, b_im)
    wb4, wc4 = _s5_block_weights(bb_re, bb_im, p["s5_c_re"], p["s5_c_im"])
    ab = jnp.concatenate([ab_re.reshape(1, S5_STATES), ab_im.reshape(1, S5_STATES), jnp.zeros((6, S5_STATES), F32)], axis=0)

    wp = p["w_in_perm"]

    z, xbc_raw, u5, gates, dt_raw = _inproj_fwd(x, g_mix, wp)
    xbc_act, dt = _conv_fwd(xbc_raw, dt_raw, p["conv_w"], conv_b, dt_bias)
    ys, ssd_states = _ssd_fwd(xbc_act, dt, alog)
    if comm is None:
        y5, s5_states = _s5_fwd(u5, wb4, wc4, ab, s5d)
    else:
        (y5, s5_states), late = _s5_fwd(u5, wb4, wc4, ab, s5d, rider=_Gather(comm["late_srcs"], comm["late_ks"]))
        p = {**p, **comm["late_unpack"](late)}
    wbr, wout, w1, w2, glu_w = p["w_branch"], p["w_out"], p["w_mlp_in"], p["w_mlp_out"], p["s5_glu_w"]
    x1 = _merge_fwd(ys, xbc_act, z, y5, gates, x, dvec, gssd, glu_w, glu_b, wbr, wout)
    x2 = _mlp_fwd(x1, g_mlp, w1, w2)
    dx2, loss_lanes, d_gfin = _loss_head(x2, target, g_fin)

    dx1, h2, act, da1, d_gmlp = _mlp_bwd(x1, dx2, g_mlp, w1, w2)
    d_w_mlp_out = _wgrad(act, dx2, "wgrad_mlp_out")
    d_w_mlp_in = _wgrad(h2, da1, "wgrad_mlp_in", col_shards=FF_SHARDS)
    merge_args = (ys, xbc_act, z, y5, gates, dx1, dvec, gssd, glu_w, glu_b, wbr, wout, head_sel)
    if comm is None:
        merge_out = _merge_bwd(*merge_args)
    else:
        g_mlp = jnp.concatenate([d_w_mlp_in, d_w_mlp_out.reshape(N_CHIPS, FF_SHARD, D_MODEL)], axis=1)
        merge_out, (sib_mlp,) = _merge_bwd(*merge_args, rider=_Pair([g_mlp]))
        pf_mlp, pb_mlp = _pair_sum(comm["place"], g_mlp, sib_mlp, "pair_sum_mlp")
    (dys, dxs_m, dz, dy5, dgates, mg, ya, yb, dpa, dpb, gel, dpre, d_dssd, d_gssd, d_glu_b) = merge_out
    d_w_out = _wgrad(mg, dx1, "wgrad_out")
    d_w_branch = jnp.concatenate([_wgrad(ya, dpa, "wgrad_branch_a"), _wgrad(yb, dpb, "wgrad_branch_b")], axis=0)
    d_glu_w = _wgrad(gel, dpre, "wgrad_glu")
    s5_args = (u5, dy5, wb4, wc4, ab, s5d, s5_states)
    if comm is None:
        du5, dwb4, dwc4, dab, d_s5d = _s5_bwd(*s5_args)
        mlp_total = None
    else:
        (du5, dwb4, dwc4, dab, d_s5d), (got_mlp,) = _s5_bwd(*s5_args, rider=_Chip([pb_mlp]))
        (mlp_total,) = _chip_sum(comm["place"], pf_mlp, got_mlp, "chip_sum_mlp")
    dbb_re, dbb_im, d_c_re, d_c_im = _s5_block_grads(dwb4, dwc4)
    d_a_re, d_a_im, d_log_dt, d_b_re, d_b_im = _s5_disc_bwd(
        a_re, a_im, log_dt, b_re, b_im, dab[0].reshape(S5_STATES, 1), dab[1].reshape(S5_STATES, 1),
        dbb_re.reshape(S5_STATES, 16), dbb_im.reshape(S5_STATES, 16))
    dxs_s, dB, dC, ddt, d_alog = _ssd_bwd(xbc_act, dt, alog, ssd_states, dys)
    dxbc_raw, ddt_raw, d_conv_w, d_conv_b, d_dt_bias = _conv_bwd(
        xbc_raw, dt_raw, dxs_m, dxs_s, dB, dC, ddt, p["conv_w"], conv_b, dt_bias)
    dx, h, d_gmix = _inproj_bwd(x, dx1, dz, dxbc_raw, du5, dgates, ddt_raw, g_mix, wp)
    d_w_in = dict(z=_wgrad(h, dz, "wgrad_in_z"), xbc=_wgrad(h, dxbc_raw, "wgrad_in_xbc"),
                  dt=_wgrad(h, ddt_raw, "wgrad_in_dt")[:, :16], u5=_wgrad(h, du5, "wgrad_in_u5"),
                  gates=_wgrad(h, dgates, "wgrad_in_gates"))

    grads = dict(
        norm_mix_g=d_gmix.reshape(-1), w_in_pieces=[(c0, d_w_in[n]) for n, c0, _ in W_IN_PIECES],
        conv_w=d_conv_w[:CONV_K], conv_b=d_conv_b.reshape(-1),
        dt_bias=d_dt_bias[0, :16], a_log=d_alog[0, :16], d_ssd=d_dssd[0, :16], ssd_norm_g=d_gssd.reshape(-1),
        s5_a_re=d_a_re.reshape(32, 64), s5_a_im=d_a_im.reshape(32, 64), s5_log_dt=d_log_dt.reshape(32),
        s5_b_re=d_b_re.reshape(32, 64, 16), s5_b_im=d_b_im.reshape(32, 64, 16), s5_c_re=d_c_re, s5_c_im=d_c_im,
        s5_d=d_s5d.reshape(-1), s5_glu_w=d_glu_w, s5_glu_b=d_glu_b.reshape(-1), w_branch=d_w_branch, w_out=d_w_out,
        norm_mlp_g=d_gmlp.reshape(-1), w_mlp_in=d_w_mlp_in, w_mlp_out=d_w_mlp_out, norm_final_g=d_gfin.reshape(-1),
        mlp_total=mlp_total)
    return jnp.sum(loss_lanes), dx, grads


MESH = pl.DeviceIdType.MESH
N_CHIPS = 4


def _place():
    x, y, c = lax.axis_index("x"), lax.axis_index("y"), lax.axis_index("c")
    chips = [(1 - x, y), (x, 1 - y), (1 - x, 1 - y)]
    return x, y, c, chips


def _remote(src, dst, send_sems, recv_sems, k, to):
    return pltpu.make_async_remote_copy(src_ref=src, dst_ref=dst, send_sem=send_sems.at[k], recv_sem=recv_sems.at[k],
                                        device_id=to, device_id_type=MESH)


def _row_chunks(rows, k, align):
    step = rows // k
    assert rows % k == 0 and step % align == 0, (rows, k, align)
    return [(i * step, step) for i in range(k)]


ICI_CHUNKS = 4
D2D_CHUNKS = 24


class _Gather:
    def __init__(self, srcs, ks):
        self.inputs = list(srcs)
        self.out_shapes = [jax.ShapeDtypeStruct((N_CHIPS,) + a.shape, a.dtype) for a in srcs]
        self.halves = [a.shape[0] // 2 for a in srcs]
        self.pieces = [_row_chunks(h, k, 32 // a.dtype.itemsize) for a, h, k in zip(srcs, self.halves, ks)]
        self.n_ici = 3 * sum(ks)
        self.n_sems = 2 * self.n_ici + len(srcs)

    def _plan(self, src_refs, out_refs, send_sems, recv_sems):
        x, y, c, chips = _place()
        own = 2 * x + y
        sib = (x, y, 1 - c)
        first, fwd_plan, k = [], [], 0
        for a, (src_ref, out_ref) in enumerate(zip(src_refs, out_refs)):
            h = self.halves[a]
            for r0, nr in self.pieces[a]:
                for cx, cy in chips:
                    first.append(_remote(src_ref.at[pl.ds(c * h + r0, nr), :], out_ref.at[own, pl.ds(c * h + r0, nr), :],
                                         send_sems, recv_sems, k, (cx, cy, c)))
                    fwd_plan.append((out_ref, 2 * cx + cy, h, r0, nr, k, (cx, cy, c)))
                    k += 1
        for a, (src_ref, out_ref) in enumerate(zip(src_refs, out_refs)):
            first.append(_remote(src_ref, out_ref.at[own], send_sems, recv_sems, 2 * self.n_ici + a, sib))
        return first, fwd_plan, c, sib

    def issue(self, src_refs, out_refs, send_sems, recv_sems):
        for cp in self._plan(src_refs, out_refs, send_sems, recv_sems)[0]:
            cp.start()

    def complete(self, src_refs, out_refs, send_sems, recv_sems):
        first, fwd_plan, c, sib = self._plan(src_refs, out_refs, send_sems, recv_sems)
        passed = []
        for out_ref, s, h, r0, nr, k, frm in fwd_plan:
            got = out_ref.at[s, pl.ds(c * h + r0, nr), :]
            _remote(got, got, send_sems, recv_sems, k, frm).wait_recv()
            fw = _remote(got, got, send_sems, recv_sems, self.n_ici + k, sib)
            fw.start()
            passed.append(fw)
        for out_ref, s, h, r0, nr, k, frm in fwd_plan:
            got = out_ref.at[s, pl.ds((1 - c) * h + r0, nr), :]
            _remote(got, got, send_sems, recv_sems, self.n_ici + k, sib).wait_recv()
        own_copies = first[self.n_ici:]
        for cp in own_copies:
            cp.wait_recv()
        for cp in first + passed:
            cp.wait_send()


def _exchange(rider, name):
    ri, ro = len(rider.inputs), len(rider.out_shapes)

    def body(*refs):
        rider.issue(refs[:ri], refs[ri:ri + ro], *refs[ri + ro:])
        rider.complete(refs[:ri], refs[ri:ri + ro], *refs[ri + ro:])

    return _pc(
        body, name=name, in_specs=[_hbm_spec()] * ri, out_specs=[_hbm_spec()] * ro, out_shape=list(rider.out_shapes),
        scratch_shapes=[pltpu.SemaphoreType.DMA((rider.n_sems,))] * 2,
    )(*rider.inputs)


def _call(body, rider=None, **kw):
    if rider is None:
        return _pc(body, **kw)
    single = not isinstance(kw["out_shape"], (list, tuple))
    out_specs = [kw["out_specs"]] if single else list(kw["out_specs"])
    out_shape = [kw["out_shape"]] if single else list(kw["out_shape"])
    scratch = list(kw.get("scratch_shapes", ()))
    n_in, n_out, n_scr = len(kw["in_specs"]), len(out_specs), len(scratch)
    ri, ro = len(rider.inputs), len(rider.out_shapes)
    steps = kw["grid"][0]

    def wrapped(*refs):
        o0 = n_in + ri
        s0 = o0 + n_out + ro
        r_in, r_out, sems = refs[n_in:o0], refs[o0 + n_out:s0], refs[s0 + n_scr:]

        @pl.when(pl.program_id(0) == 0)
        def _():
            rider.issue(r_in, r_out, *sems)

        body(*refs[:n_in], *refs[o0:o0 + n_out], *refs[s0:s0 + n_scr])

        @pl.when(pl.program_id(0) == steps - 1)
        def _():
            rider.complete(r_in, r_out, *sems)

    f = _pc(wrapped, name=kw["name"], grid=kw["grid"], in_specs=list(kw["in_specs"]) + [_hbm_spec()] * ri,
            out_specs=out_specs + [_hbm_spec()] * ro, out_shape=out_shape + list(rider.out_shapes),
            scratch_shapes=scratch + [pltpu.SemaphoreType.DMA((rider.n_sems,))] * 2, compiler_params=kw["compiler_params"])

    def run(*args):
        res = f(*args, *rider.inputs)
        return (res[0] if single else res[:n_out]), res[n_out:]

    return run


def _d2d_pieces(rows):
    k = next(k for k in (24, 16, 8, 4, 2, 1) if rows % k == 0 and (rows // k) % 8 == 0)
    return _row_chunks(rows, k, 8)


class _Pair:
    def __init__(self, gs, small=None):
        self.n = len(gs)
        self.halves = [g.shape[1] // 2 for g in gs]
        self.inputs = list(gs) + ([small] if small is not None else [])
        self.out_shapes = [jax.ShapeDtypeStruct((N_CHIPS, h, g.shape[2]), F32) for g, h in zip(gs, self.halves)]
        if small is not None:
            self.out_shapes.append(jax.ShapeDtypeStruct(small.shape, F32))
        self.n_sems = len(self.inputs)

    def issue(self, in_refs, out_refs, send_sems, recv_sems):
        x, y, c, _ = _place()
        sib = (x, y, 1 - c)
        for a in range(self.n):
            h = self.halves[a]
            for s in range(N_CHIPS):
                for r0, nr in _d2d_pieces(h):
                    _remote(in_refs[a].at[s, pl.ds((1 - c) * h + r0, nr), :], out_refs[a].at[s, pl.ds(r0, nr), :],
                            send_sems, recv_sems, a, sib).start()
        for a in range(self.n, len(self.inputs)):
            _remote(in_refs[a], out_refs[a], send_sems, recv_sems, a, sib).start()

    def complete(self, in_refs, out_refs, send_sems, recv_sems):
        x, y, c, _ = _place()
        for a in range(len(self.inputs)):
            _remote(out_refs[a], out_refs[a], send_sems, recv_sems, a, (x, y, 1 - c)).wait()


SUM_BLOCKS = 4


def _pair_sum(place, g, sib, name, small=None, sib_small=None):
    n, R, C = g.shape
    H = R // 2
    rb = H // SUM_BLOCKS
    assert H % SUM_BLOCKS == 0 and rb % 16 == 0

    def body(place_ref, a_ref, b_ref, *rest):
        if small is None:
            pf_ref, pb_ref = rest
        else:
            s_ref, t_ref, pf_ref, pb_ref, ps_ref = rest

            @pl.when((pl.program_id(0) == 0) & (pl.program_id(1) == 0))
            def _():
                ps_ref[...] = s_ref[...] + t_ref[...]

        p = a_ref[...] + b_ref[...]
        pf_ref[...] = p
        pb_ref[...] = p.astype(BF16)

    blk = pl.BlockSpec((1, rb, C), lambda s, i, pr: (s, i, 0))
    mine = pl.BlockSpec((1, rb, C), lambda s, i, pr: (s, pr[1] * SUM_BLOCKS + i, 0))
    ins, outs, shapes, args = [mine, blk], [blk, blk], [jax.ShapeDtypeStruct((n, H, C), F32),
                                                        jax.ShapeDtypeStruct((n, H, C), BF16)], [g, sib]
    if small is not None:
        sm = pl.BlockSpec(small.shape, lambda s, i, pr: (0, 0))
        ins += [sm, sm]
        outs += [sm]
        shapes += [jax.ShapeDtypeStruct(small.shape, F32)]
        args += [small, sib_small]
    return _pc(
        body, name=name, out_shape=shapes,
        grid_spec=pltpu.PrefetchScalarGridSpec(num_scalar_prefetch=1, grid=(n, SUM_BLOCKS), in_specs=ins, out_specs=outs),
        compiler_params=_cparams(("arbitrary", "arbitrary")),
    )(place, *args)


class _Chip:
    def __init__(self, pbs, psmall=None):
        self.n = len(pbs)
        self.rows = [pb.shape[1] for pb in pbs]
        self.inputs = list(pbs) + ([psmall] if psmall is not None else [])
        self.out_shapes = [jax.ShapeDtypeStruct((3,) + pb.shape[1:], BF16) for pb in pbs]
        if psmall is not None:
            self.out_shapes.append(jax.ShapeDtypeStruct((N_CHIPS,) + psmall.shape, F32))
        self.n_sems = 3 * len(self.inputs)

    def issue(self, in_refs, out_refs, send_sems, recv_sems):
        x, y, c, chips = _place()
        own = 2 * x + y
        for j, (cx, cy) in enumerate(chips):
            for a in range(self.n):
                for r0, nr in _row_chunks(self.rows[a], ICI_CHUNKS, 16):
                    _remote(in_refs[a].at[2 * cx + cy, pl.ds(r0, nr), :], out_refs[a].at[j, pl.ds(r0, nr), :],
                            send_sems, recv_sems, 3 * a + j, (cx, cy, c)).start()
            for a in range(self.n, len(self.inputs)):
                _remote(in_refs[a], out_refs[a].at[own], send_sems, recv_sems, 3 * a + j, (cx, cy, c)).start()

    def complete(self, in_refs, out_refs, send_sems, recv_sems):
        x, y, c, chips = _place()
        own = 2 * x + y
        for j, (cx, cy) in enumerate(chips):
            for a in range(self.n):
                _remote(in_refs[a].at[own], out_refs[a].at[j], send_sems, recv_sems, 3 * a + j, (cx, cy, c)).wait()
            for a in range(self.n, len(self.inputs)):
                _remote(in_refs[a], out_refs[a].at[2 * cx + cy], send_sems, recv_sems, 3 * a + j, (cx, cy, c)).wait()


def _chip_sum(place, pf, got, name, small4=None, psmall=None):
    _, H, C = pf.shape
    rb = H // SUM_BLOCKS

    def body(place_ref, o_ref, g_ref, *rest):
        if small4 is None:
            (tot_ref,) = rest
        else:
            s_ref, p_ref, tot_ref, st_ref = rest

            @pl.when(pl.program_id(0) == 0)
            def _():
                terms = [jnp.where(place_ref[0] == s, p_ref[...], s_ref[s]) for s in range(N_CHIPS)]
                st_ref[...] = ((terms[0] + terms[1]) + terms[2]) + terms[3]

        tot_ref[...] = ((o_ref[0] + g_ref[0].astype(F32)) + g_ref[1].astype(F32)) + g_ref[2].astype(F32)

    ins = [pl.BlockSpec((1, rb, C), lambda i, pr: (pr[0], i, 0)), pl.BlockSpec((3, rb, C), lambda i, pr: (0, i, 0))]
    outs = [pl.BlockSpec((rb, C), lambda i, pr: (pr[1] * SUM_BLOCKS + i, 0))]
    shapes = [jax.ShapeDtypeStruct((2 * H, C), F32)]
    args = [pf, got]
    if small4 is not None:
        ins += [pl.BlockSpec(small4.shape, lambda i, pr: (0, 0, 0)), pl.BlockSpec(psmall.shape, lambda i, pr: (0, 0))]
        outs += [pl.BlockSpec(psmall.shape, lambda i, pr: (0, 0))]
        shapes += [jax.ShapeDtypeStruct(psmall.shape, F32)]
        args += [small4, psmall]
    return _pc(
        body, name=name, out_shape=shapes,
        grid_spec=pltpu.PrefetchScalarGridSpec(num_scalar_prefetch=1, grid=(SUM_BLOCKS,), in_specs=ins, out_specs=outs),
        compiler_params=_cparams(("arbitrary",)),
    )(place, *args)


def _half_exchange(fulls):
    n = len(fulls)

    def body(*refs):
        in_refs, out_refs = refs[:n], refs[n:2 * n]
        send_sems, recv_sems = refs[2 * n:]
        x, y, c, _ = _place()
        sib = (x, y, 1 - c)
        for a in range(n):
            h = fulls[a].shape[0] // 2
            for r0, nr in _d2d_pieces(h):
                rows = pl.ds(c * h + r0, nr)
                _remote(in_refs[a].at[rows, :], out_refs[a].at[rows, :], send_sems, recv_sems, a, sib).start()
        for a in range(n):
            h = fulls[a].shape[0] // 2
            _remote(in_refs[a].at[pl.ds(c * h, h), :], out_refs[a].at[pl.ds((1 - c) * h, h), :], send_sems, recv_sems, a,
                    sib).wait()

    return _pc(
        body, name="half_exchange", in_specs=[_hbm_spec()] * n, out_specs=[_hbm_spec()] * n,
        out_shape=[jax.ShapeDtypeStruct(f.shape, F32) for f in fulls],
        input_output_aliases={a: a for a in range(n)},
        scratch_shapes=[pltpu.SemaphoreType.DMA((n,)), pltpu.SemaphoreType.DMA((n,))],
    )(*fulls)


def _adamw(w, g, m, v, name, g_row0=0, with_grad=False):
    R, C = w.shape
    rb = 256 if R % 256 == 0 else (128 if R % 128 == 0 else R)
    assert g_row0 % rb == 0

    def body(w_ref, g_ref, m_ref, v_ref, d_ref, nm_ref, nv_ref, *g_out):
        gv = g_ref[...]
        m2 = ADAM_B1 * m_ref[...] + (1.0 - ADAM_B1) * gv
        v2 = ADAM_B2 * v_ref[...] + (1.0 - ADAM_B2) * (gv * gv)
        m_hat = m2 / (1.0 - ADAM_B1 ** ADAM_STEP)
        v_hat = v2 / (1.0 - ADAM_B2 ** ADAM_STEP)
        d_ref[...] = -ADAM_LR * (m_hat / (jnp.sqrt(v_hat) + ADAM_EPS) + ADAM_WD * w_ref[...])
        nm_ref[...] = m2
        nv_ref[...] = v2
        if with_grad:
            g_out[0][...] = gv

    spec = pl.BlockSpec((rb, C), lambda i: (i, 0))
    g_spec = pl.BlockSpec((rb, C), lambda i: (g_row0 // rb + i, 0))
    n_out = 4 if with_grad else 3
    return _pc(
        body, name=name, grid=(R // rb,), in_specs=[spec, g_spec, spec, spec], out_specs=[spec] * n_out,
        out_shape=[jax.ShapeDtypeStruct((R, C), F32)] * n_out, compiler_params=_cparams(("parallel",)),
    )(w, g, m, v)


PACK_COLS = 1024
ROWS_A = (("w_mlp_in", 0, 1024), ("w_mlp_out", 1024, 1024), ("w_out", 2048, 256), ("w_branch", 2304, 384))
ROWS_A_TOTAL = 2688
ROWS_B = (("w_out", 0, 256), ("w_branch", 256, 384))
ROW_B_GLU, ROW_B_CONV, ROWS_B_TOTAL = 640, 704, 768
W_IN_SHARD = 1412
CONV_PAD_ROWS = 16
SMALL = (("norm_mix_g", (1024,)), ("conv_b", (2048,)), ("dt_bias", (16,)), ("a_log", (16,)), ("d_ssd", (16,)),
         ("ssd_norm_g", (1024,)), ("s5_a_re", (32, 64)), ("s5_a_im", (32, 64)), ("s5_log_dt", (32,)),
         ("s5_b_re", (32, 64, 16)), ("s5_b_im", (32, 64, 16)), ("s5_c_re", (32, 16, 64)), ("s5_c_im", (32, 16, 64)),
         ("s5_d", (512,)), ("s5_glu_b", (512,)), ("norm_mlp_g", (1024,)), ("norm_final_g", (1024,)))
SMALL_ROWS = 144
GLU_ROWS = S5_WIDTH * S5_WIDTH // PACK_COLS
CONV_ROWS = CONV_K * CONV_DIM // PACK_COLS
W_IN_PIECES = (("z", 0, 1024), ("xbc", 1024, 2048), ("dt", OFF_DT, 16), ("u5", OFF_U, 512), ("gates", 3600, 2048))


def _pack_small(parts):
    flat = jnp.concatenate([a.astype(F32).reshape(-1) for a in parts])
    return jnp.concatenate([flat, jnp.zeros((SMALL_ROWS * PACK_COLS - flat.shape[0],), F32)]).reshape(SMALL_ROWS, PACK_COLS)


def _unpack_small(pack):
    flat, out, r = pack.reshape(-1), {}, 0
    for name, shp in SMALL:
        n = math.prod(shp)
        out[name] = flat[r:r + n].reshape(shp)
        r += n
    return out


def _column_range(pieces, lo, hi):
    out = []
    for c0, a in pieces:
        a0, a1 = max(lo, c0), min(hi, c0 + a.shape[-1])
        if a0 < a1:
            out.append(a[..., a0 - c0:a1 - c0])
    return out


def kernel(x, norm_mix_g, w_in, conv_w, conv_b, dt_bias, a_log, d_ssd, ssd_norm_g, s5_a_re, s5_a_im, s5_log_dt, s5_b_re, s5_b_im, s5_c_re, s5_c_im, s5_d, s5_glu_w, s5_glu_b, w_branch, w_out, norm_mlp_g, w_mlp_in, w_mlp_out, norm_final_g, loss_target, m_norm_mix_g, m_w_in, m_conv_w, m_conv_b, m_dt_bias, m_a_log, m_d_ssd, m_ssd_norm_g, m_s5_a_re, m_s5_a_im, m_s5_log_dt, m_s5_b_re, m_s5_b_im, m_s5_c_re, m_s5_c_im, m_s5_d, m_s5_glu_w, m_s5_glu_b, m_w_branch, m_w_out, m_norm_mlp_g, m_w_mlp_in, m_w_mlp_out, m_norm_final_g, v_norm_mix_g, v_w_in, v_conv_w, v_conv_b, v_dt_bias, v_a_log, v_d_ssd, v_ssd_norm_g, v_s5_a_re, v_s5_a_im, v_s5_log_dt, v_s5_b_re, v_s5_b_im, v_s5_c_re, v_s5_c_im, v_s5_d, v_s5_glu_w, v_s5_glu_b, v_w_branch, v_w_out, v_norm_mlp_g, v_w_mlp_in, v_w_mlp_out, v_norm_final_g):
    names = ("norm_mix_g", "w_in", "conv_w", "conv_b", "dt_bias", "a_log", "d_ssd", "ssd_norm_g", "s5_a_re", "s5_a_im",
             "s5_log_dt", "s5_b_re", "s5_b_im", "s5_c_re", "s5_c_im", "s5_d", "s5_glu_w", "s5_glu_b", "w_branch", "w_out",
             "norm_mlp_g", "w_mlp_in", "w_mlp_out", "norm_final_g")
    w = dict(zip(names, (norm_mix_g, w_in, conv_w, conv_b, dt_bias, a_log, d_ssd, ssd_norm_g, s5_a_re, s5_a_im, s5_log_dt,
                         s5_b_re, s5_b_im, s5_c_re, s5_c_im, s5_d, s5_glu_w, s5_glu_b, w_branch, w_out, norm_mlp_g,
                         w_mlp_in, w_mlp_out, norm_final_g)))
    m = dict(zip(names, (m_norm_mix_g, m_w_in, m_conv_w, m_conv_b, m_dt_bias, m_a_log, m_d_ssd, m_ssd_norm_g, m_s5_a_re,
                         m_s5_a_im, m_s5_log_dt, m_s5_b_re, m_s5_b_im, m_s5_c_re, m_s5_c_im, m_s5_d, m_s5_glu_w,
                         m_s5_glu_b, m_w_branch, m_w_out, m_norm_mlp_g, m_w_mlp_in, m_w_mlp_out, m_norm_final_g)))
    v = dict(zip(names, (v_norm_mix_g, v_w_in, v_conv_w, v_conv_b, v_dt_bias, v_a_log, v_d_ssd, v_ssd_norm_g, v_s5_a_re,
                         v_s5_a_im, v_s5_log_dt, v_s5_b_re, v_s5_b_im, v_s5_c_re, v_s5_c_im, v_s5_d, v_s5_glu_w,
                         v_s5_glu_b, v_w_branch, v_w_out, v_norm_mlp_g, v_w_mlp_in, v_w_mlp_out, v_norm_final_g)))

    cx, cy, cc = lax.axis_index("x"), lax.axis_index("y"), lax.axis_index("c")
    own = 2 * cx + cy
    place = jnp.stack([own, cc]).astype(jnp.int32)

    src_conv = jnp.concatenate([conv_w, jnp.zeros((CONV_PAD_ROWS - CONV_K, 512), F32)], axis=0)
    all_in, all_conv = _exchange(_Gather([w_in.astype(BF16), src_conv], [ICI_CHUNKS, 1]), "gather_first")
    p = {n: w[n] for n, _ in SMALL}
    p["conv_w"] = jnp.concatenate([all_conv[s, :CONV_K] for s in range(N_CHIPS)], axis=1)
    shards = [(W_IN_SHARD * s, all_in[s]) for s in range(N_CHIPS)]
    p["w_in_perm"] = jnp.concatenate(
        _column_range(shards, 0, OFF_DT) + _column_range(shards, OFF_U, D_IN_PROJ) + _column_range(shards, OFF_DT, OFF_U)
        + [jnp.zeros((D_MODEL, DT_PAD - 16), BF16)], axis=1)

    def late_unpack(gathered):
        all_a, all_glu = gathered
        out = {"w_mlp_in": all_a[:, 0:1024], "s5_glu_w": all_glu.reshape(S5_WIDTH, S5_WIDTH)}
        for n, r0, nr in ROWS_A[1:]:
            out[n] = all_a[:, r0:r0 + nr].reshape(N_CHIPS * nr, PACK_COLS)
        return out

    comm = dict(place=place, late_ks=[ICI_CHUNKS, 1], late_unpack=late_unpack,
                late_srcs=[jnp.concatenate([w[n].astype(BF16) for n, _, _ in ROWS_A], axis=0), s5_glu_w.astype(BF16)])
    loss_part, grad_x, g = _local_step(x[0], loss_target[0], p, comm)
    loss = lax.psum(loss_part, ("x", "y", "c"))

    conv4 = g["conv_w"].reshape(CONV_K, N_CHIPS, 512).transpose(1, 0, 2).reshape(N_CHIPS, CONV_ROWS // N_CHIPS, PACK_COLS)
    g_b = jnp.concatenate(
        [g[n].reshape(N_CHIPS, nr, PACK_COLS) for n, _, nr in ROWS_B]
        + [g["s5_glu_w"].reshape(N_CHIPS, GLU_ROWS // N_CHIPS, PACK_COLS),
           jnp.pad(conv4, ((0, 0), (0, ROWS_B_TOTAL - ROW_B_CONV - CONV_ROWS // N_CHIPS), (0, 0)))], axis=1)
    g_in = jnp.stack([jnp.concatenate(_column_range(g["w_in_pieces"], W_IN_SHARD * s, W_IN_SHARD * (s + 1)), axis=1)
                      for s in range(N_CHIPS)])
    spack = _pack_small([g[n] for n, _ in SMALL])
    sib_b, sib_in, sib_small = _exchange(_Pair([g_b, g_in], spack), "pair_exchange")
    pf_b, pb_b, psmall = _pair_sum(place, g_b, sib_b, "pair_sum_b", spack, sib_small)
    pf_in, pb_in = _pair_sum(place, g_in, sib_in, "pair_sum_in")
    got_b, got_in, small4 = _exchange(_Chip([pb_b, pb_in], psmall), "chip_exchange")
    tot_b, small_tot = _chip_sum(place, pf_b, got_b, "chip_sum_b", small4, psmall)
    (tot_in,) = _chip_sum(place, pf_in, got_in, "chip_sum_in")
    red_mlp, red_b, red_in = _half_exchange([g["mlp_total"], tot_b, tot_in])

    grads = _unpack_small(small_tot)
    delta, new_m, new_v = {}, {}, {}
    for n, r0, _ in ROWS_A[:2]:
        delta[n], new_m[n], new_v[n], grads[n] = _adamw(w[n], red_mlp, m[n], v[n], "adamw_" + n, g_row0=r0, with_grad=True)
    for n, r0, _ in ROWS_B:
        delta[n], new_m[n], new_v[n], grads[n] = _adamw(w[n], red_b, m[n], v[n], "adamw_" + n, g_row0=r0, with_grad=True)
    grads["w_in"] = red_in
    grads["s5_glu_w"] = red_b[ROW_B_GLU:ROW_B_GLU + GLU_ROWS // N_CHIPS].reshape(S5_WIDTH // N_CHIPS, S5_WIDTH)
    grads["conv_w"] = red_b[ROW_B_CONV:ROW_B_CONV + CONV_ROWS // N_CHIPS].reshape(CONV_K, CONV_DIM // N_CHIPS)
    for n in ("w_in", "s5_glu_w", "conv_w"):
        delta[n], new_m[n], new_v[n] = _adamw(w[n], grads[n], m[n], v[n], "adamw_" + n)
    ds, ms, vs = _adamw(_pack_small([w[n] for n, _ in SMALL]), small_tot, _pack_small([m[n] for n, _ in SMALL]),
                        _pack_small([v[n] for n, _ in SMALL]), "adamw_small")
    delta.update(_unpack_small(ds))
    new_m.update(_unpack_small(ms))
    new_v.update(_unpack_small(vs))

    return (loss, grad_x[None], *[grads[n] for n in names], *[delta[n] for n in names],
            *[new_m[n] for n in names], *[new_v[n] for n in names])
```

```python
import functools
import math

import jax
import jax.numpy as jnp
from jax import lax
from jax.experimental import pallas as pl
from jax.experimental.pallas import tpu as pltpu

F32 = jnp.float32
BF16 = jnp.bfloat16

D_MODEL = 1024
SSD_INNER = 1024
SSD_HEADS = 16
SSD_HEADDIM = 64
SSD_GROUPS = 4
SSD_HPG = 4
SSD_STATE = 128
SSD_CHUNK = 128
CONV_K = 4
CONV_DIM = 2048
S5_WIDTH = 512
S5_STATES = 2048
S5_BLOCKS = 4
S5_CHUNK = 128
D_FF = 4096
FF_SHARDS = 4
FF_SHARD = D_FF // FF_SHARDS
EPS = 1e-6
P_Z, P_XBC, P_U5, P_G, P_DT, P_END = 0, 1024, 3072, 3584, 5632, 5760
DT_PAD = 128
OFF_DT, OFF_U = 3072, 3088
D_IN_PROJ = 5648

ADAM_LR, ADAM_B1, ADAM_B2, ADAM_EPS, ADAM_WD, ADAM_STEP = 0.001, 0.9, 0.999, 1e-08, 0.01, 10

TOKEN_TILE = 256
VMEM_LIMIT = 56 * 1024 * 1024
HALO = 8
CONV_COLS = 256
CONV_ROWS_BLK = 64
WGRAD_TOKENS = 2048


def _pc(body, **kw):
    return pl.pallas_call(body, **kw)


def _cparams(sem=None):
    return pltpu.CompilerParams(dimension_semantics=sem, vmem_limit_bytes=VMEM_LIMIT)


def _dot(a, b):
    return jnp.dot(a, b, preferred_element_type=F32)


def _dot_nt(a, b):
    return lax.dot_general(a, b, (((1,), (1,)), ((), ())), preferred_element_type=F32)


def _dot_tn(a, b):
    return lax.dot_general(a, b, (((0,), (0,)), ((), ())), preferred_element_type=F32)


def _dot_hi(a, b, dims=(((1,), (0,)), ((), ()))):
    return lax.dot_general(a, b, dims, preferred_element_type=F32, precision=lax.Precision.HIGHEST)


def _split_bf16(x, terms):
    out = []
    for _ in range(terms - 1):
        t = x.astype(BF16)
        out.append(t)
        x = x - t.astype(F32)
    out.append(x.astype(BF16))
    return out


def _dot_split(x, onehots, terms, dims=(((1,), (0,)), ((), ()))):
    acc = None
    for t in _split_bf16(x, terms):
        p = lax.dot_general(t, onehots, dims, preferred_element_type=F32)
        acc = p if acc is None else acc + p
    return acc


def _dot_split_rhs(onehots, x, terms, dims=(((1,), (0,)), ((), ()))):
    acc = None
    for t in _split_bf16(x, terms):
        p = lax.dot_general(onehots, t, dims, preferred_element_type=F32)
        acc = p if acc is None else acc + p
    return acc


def _sigmoid(x):
    return 1.0 / (1.0 + jnp.exp(-x))


def _softplus(x):
    return jnp.maximum(x, 0.0) + jnp.log(1.0 + jnp.exp(-jnp.abs(x)))


_GELU_C = math.sqrt(2.0 / math.pi)


def _gelu(x):
    return 0.5 * x * (1.0 + jnp.tanh(_GELU_C * (x + 0.044715 * x * x * x)))


def _gelu_grad(x):
    t = jnp.tanh(_GELU_C * (x + 0.044715 * x * x * x))
    return 0.5 * (1.0 + t) + 0.5 * x * (1.0 - t * t) * _GELU_C * (1.0 + 3.0 * 0.044715 * x * x)


def _rms(x):
    r = lax.rsqrt(jnp.mean(x * x, axis=-1, keepdims=True) + EPS)
    return x * r, r


def _rms_bwd(xn, r, dxn):
    return r * (dxn - xn * jnp.mean(dxn * xn, axis=-1, keepdims=True))


def _row_spec(tm, width, col=0):
    return pl.BlockSpec((tm, width), lambda i: (i, col))


def _const_spec(shape):
    nd = len(shape)
    return pl.BlockSpec(shape, lambda i: (0,) * nd)


def _hbm_spec():
    return pl.BlockSpec(memory_space=pl.ANY)


def _inproj_fwd(x, g, wp):
    T = x.shape[0]
    tm = TOKEN_TILE

    def body(x_ref, g_ref, w_hbm, z_ref, xbc_ref, u5_ref, gt_ref, dt_ref, h_ref, w_ref):
        @pl.when(pl.program_id(0) == 0)
        def _():
            pltpu.sync_copy(w_hbm, w_ref)

        xn, _ = _rms(x_ref[...])
        h = (xn * g_ref[...]).astype(BF16)
        h_ref[...] = h
        z_ref[...] = _dot(h, w_ref[:, P_Z:P_XBC])
        xbc_ref[...] = _dot(h, w_ref[:, P_XBC:P_U5])
        u5_ref[...] = _dot(h, w_ref[:, P_U5:P_G])
        gt_ref[...] = _dot(h, w_ref[:, P_G:P_DT])
        dt_ref[...] = _dot(h, w_ref[:, P_DT:P_END])

    widths = (1024, 2048, 512, 2048, DT_PAD)
    return _pc(
        body, name="inproj_fwd", grid=(T // tm,),
        in_specs=[_row_spec(tm, D_MODEL), _const_spec((1, D_MODEL)), _hbm_spec()],
        out_specs=[_row_spec(tm, w) for w in widths] + [_row_spec(tm, D_MODEL)],
        out_shape=[jax.ShapeDtypeStruct((T, w), F32) for w in widths] + [jax.ShapeDtypeStruct((T, D_MODEL), BF16)],
        scratch_shapes=[pltpu.VMEM((D_MODEL, P_END), BF16)],
        compiler_params=_cparams(("arbitrary",)),
    )(x, g, wp)


def _inproj_bwd(x, dx1, dz, dxbc, du5, dgt, ddt, g, wp, rider=None):
    T = x.shape[0]
    tm = TOKEN_TILE

    def body(x_ref, dx1_ref, dz_ref, dxbc_ref, du5_ref, dgt_ref, ddt_ref, g_ref, w_hbm, dx_ref, dg_ref, w_ref):
        @pl.when(pl.program_id(0) == 0)
        def _():
            pltpu.sync_copy(w_hbm, w_ref)
            dg_ref[...] = jnp.zeros_like(dg_ref)

        xn, r = _rms(x_ref[...])
        gv = g_ref[...]
        dh = _dot_nt(dz_ref[...].astype(BF16), w_ref[:, P_Z:P_XBC])
        dh += _dot_nt(dxbc_ref[...].astype(BF16), w_ref[:, P_XBC:P_U5])
        dh += _dot_nt(du5_ref[...].astype(BF16), w_ref[:, P_U5:P_G])
        dh += _dot_nt(dgt_ref[...].astype(BF16), w_ref[:, P_G:P_DT])
        dh += _dot_nt(ddt_ref[...].astype(BF16), w_ref[:, P_DT:P_END])
        dg_ref[...] += jnp.sum(dh * xn, axis=0, keepdims=True)
        dx_ref[...] = dx1_ref[...] + _rms_bwd(xn, r, dh * gv)

    return _call(
        body, rider, name="inproj_bwd", grid=(T // tm,),
        in_specs=[_row_spec(tm, 1024), _row_spec(tm, 1024), _row_spec(tm, 1024), _row_spec(tm, 2048),
                  _row_spec(tm, 512), _row_spec(tm, 2048), _row_spec(tm, DT_PAD), _const_spec((1, 1024)), _hbm_spec()],
        out_specs=[_row_spec(tm, 1024), _const_spec((1, 1024))],
        out_shape=[jax.ShapeDtypeStruct((T, 1024), F32), jax.ShapeDtypeStruct((1, 1024), F32)],
        scratch_shapes=[pltpu.VMEM((D_MODEL, P_END), BF16)],
        compiler_params=_cparams(("arbitrary",)),
    )(x, dx1, dz, dxbc, du5, dgt, ddt, g, wp)


def _conv_fwd(xbc_raw, dt_raw, conv_w, conv_b, dt_bias):
    T = xbc_raw.shape[0]
    tm = TOKEN_TILE

    def body(u_ref, dtr_ref, w_ref, b_ref, db_ref, act_ref, dt_ref, ext_ref):
        @pl.when(pl.program_id(0) == 0)
        def _():
            ext_ref[0:HALO, :] = jnp.zeros((HALO, CONV_DIM), F32)

        ext_ref[HALO:, :] = u_ref[...]
        for c0 in range(0, CONV_DIM, CONV_COLS):
            cols = slice(c0, c0 + CONV_COLS)
            taps = [w_ref[k:k + 1, cols] for k in range(CONV_K)]
            bias = b_ref[:, cols]
            for r0 in range(0, tm, CONV_ROWS_BLK):
                y = bias + taps[0] * ext_ref[pl.ds(HALO - (CONV_K - 1) + r0, CONV_ROWS_BLK), cols]
                for k in range(1, CONV_K):
                    y += taps[k] * ext_ref[pl.ds(HALO - (CONV_K - 1) + k + r0, CONV_ROWS_BLK), cols]
                act_ref[r0:r0 + CONV_ROWS_BLK, cols] = y * _sigmoid(y)
        ext_ref[0:HALO, :] = u_ref[tm - HALO:tm, :]
        dt_ref[...] = _softplus(dtr_ref[...] + db_ref[...])

    return _pc(
        body, name="conv_fwd", grid=(T // tm,),
        in_specs=[_row_spec(tm, CONV_DIM), _row_spec(tm, DT_PAD), _const_spec((CONV_K, CONV_DIM)),
                  _const_spec((1, CONV_DIM)), _const_spec((1, DT_PAD))],
        out_specs=[_row_spec(tm, CONV_DIM), _row_spec(tm, DT_PAD)],
        out_shape=[jax.ShapeDtypeStruct((T, CONV_DIM), F32), jax.ShapeDtypeStruct((T, DT_PAD), F32)],
        scratch_shapes=[pltpu.VMEM((tm + HALO, CONV_DIM), F32)],
        compiler_params=_cparams(("arbitrary",)),
    )(xbc_raw, dt_raw, conv_w, conv_b, dt_bias)


def _conv_bwd(xbc_raw, dt_raw, dxs_a, dxs_b, dB, dC, ddt, conv_w, conv_b, dt_bias):
    T = xbc_raw.shape[0]
    tm = TOKEN_TILE
    n = T // tm
    hb = tm // HALO

    def rev(width):
        return pl.BlockSpec((tm, width), lambda i: (n - 1 - i, 0))

    def body(u_ref, up_ref, dtr_ref, dxa_ref, dxb_ref, dB_ref, dC_ref, ddt_ref, w_ref, b_ref, db_ref,
             du_ref, ddtr_ref, dw_ref, dcb_ref, ddb_ref, ext_ref, dye_ref):
        i = pl.program_id(0)

        @pl.when(i == 0)
        def _():
            dye_ref[tm:, :] = jnp.zeros((HALO, CONV_DIM), F32)
            dw_ref[...] = jnp.zeros_like(dw_ref)
            dcb_ref[...] = jnp.zeros_like(dcb_ref)
            ddb_ref[...] = jnp.zeros_like(ddb_ref)

        first = (i == n - 1).astype(F32)
        ext_ref[0:HALO, :] = up_ref[...] * (1.0 - first)
        ext_ref[HALO:, :] = u_ref[...]
        for c0 in range(0, CONV_DIM, CONV_COLS):
            cols = slice(c0, c0 + CONV_COLS)
            taps = [w_ref[k:k + 1, cols] for k in range(CONV_K)]
            bias = b_ref[:, cols]
            acc_b = jnp.zeros((HALO, CONV_COLS), F32)
            acc_w = [jnp.zeros((HALO, CONV_COLS), F32) for _ in range(CONV_K)]
            for r0 in range(0, tm, CONV_ROWS_BLK):
                rows = slice(r0, r0 + CONV_ROWS_BLK)
                us = [ext_ref[pl.ds(HALO - (CONV_K - 1) + k + r0, CONV_ROWS_BLK), cols] for k in range(CONV_K)]
                y = bias + taps[0] * us[0]
                for k in range(1, CONV_K):
                    y += taps[k] * us[k]
                s = _sigmoid(y)
                if c0 < SSD_INNER:
                    dact = dxa_ref[rows, cols] + dxb_ref[rows, cols]
                elif c0 < SSD_INNER + 512:
                    dact = dB_ref[rows, c0 - SSD_INNER:c0 - SSD_INNER + CONV_COLS]
                else:
                    dact = dC_ref[rows, c0 - SSD_INNER - 512:c0 - SSD_INNER - 512 + CONV_COLS]
                dy = dact * (s * (1.0 + y * (1.0 - s)))
                dye_ref[rows, cols] = dy
                acc_b += jnp.sum(dy.reshape(CONV_ROWS_BLK // HALO, HALO, CONV_COLS), axis=0)
                for k in range(CONV_K):
                    acc_w[k] += jnp.sum((dy * us[k]).reshape(CONV_ROWS_BLK // HALO, HALO, CONV_COLS), axis=0)
            dcb_ref[:, cols] += jnp.sum(acc_b, axis=0, keepdims=True)
            for k in range(CONV_K):
                dw_ref[k:k + 1, cols] += jnp.sum(acc_w[k], axis=0, keepdims=True)
        for c0 in range(0, CONV_DIM, CONV_COLS):
            cols = slice(c0, c0 + CONV_COLS)
            taps = [w_ref[k:k + 1, cols] for k in range(CONV_K)]
            for r0 in range(0, tm, CONV_ROWS_BLK):
                du = taps[0] * dye_ref[pl.ds(CONV_K - 1 + r0, CONV_ROWS_BLK), cols]
                for k in range(1, CONV_K):
                    du += taps[k] * dye_ref[pl.ds(CONV_K - 1 - k + r0, CONV_ROWS_BLK), cols]
                du_ref[r0:r0 + CONV_ROWS_BLK, cols] = du
        dye_ref[tm:, :] = dye_ref[0:HALO, :]
        sg = _sigmoid(dtr_ref[...] + db_ref[...])
        ddtr = ddt_ref[...] * sg
        ddtr_ref[...] = ddtr
        ddb_ref[...] += jnp.sum(ddtr, axis=0, keepdims=True)

    prev_spec = pl.BlockSpec((HALO, CONV_DIM), lambda i: (jnp.maximum((n - 1 - i) * hb - 1, 0), 0))
    return _pc(
        body, name="conv_bwd", grid=(n,),
        in_specs=[rev(CONV_DIM), prev_spec, rev(DT_PAD), rev(1024), rev(1024), rev(512), rev(512), rev(DT_PAD),
                  _const_spec((CONV_K, CONV_DIM)), _const_spec((1, CONV_DIM)), _const_spec((1, DT_PAD))],
        out_specs=[rev(CONV_DIM), rev(DT_PAD), _const_spec((HALO, CONV_DIM)), _const_spec((1, CONV_DIM)),
                   _const_spec((1, DT_PAD))],
        out_shape=[jax.ShapeDtypeStruct((T, CONV_DIM), F32), jax.ShapeDtypeStruct((T, DT_PAD), F32),
                   jax.ShapeDtypeStruct((HALO, CONV_DIM), F32), jax.ShapeDtypeStruct((1, CONV_DIM), F32),
                   jax.ShapeDtypeStruct((1, DT_PAD), F32)],
        scratch_shapes=[pltpu.VMEM((tm + HALO, CONV_DIM), F32), pltpu.VMEM((tm + HALO, CONV_DIM), F32)],
        compiler_params=_cparams(("arbitrary",)),
    )(xbc_raw, xbc_raw, dt_raw, dxs_a, dxs_b, dB, dC, ddt, conv_w, conv_b, dt_bias)


GROUP_LANES = SSD_HPG * SSD_HEADDIM


def _ssd_expanders():
    head = jnp.arange(DT_PAD)[:, None]
    to_wide = (jnp.arange(SSD_INNER)[None, :] // SSD_HEADDIM == head).astype(BF16)
    to_cols = (jnp.arange(SSD_HEADS * SSD_CHUNK)[None, :] // SSD_CHUNK == head).astype(BF16)
    return to_wide, to_wide.T, to_cols


def _ssd_prep(dt_ref, alog_ref, wide_ref, cols_ref):
    q = SSD_CHUNK
    a = -jnp.exp(alog_ref[...])
    dtv = dt_ref[...]
    la = dtv * a
    row = lax.broadcasted_iota(jnp.int32, (q, q), 0)
    col = lax.broadcasted_iota(jnp.int32, (q, q), 1)
    tri = (col <= row).astype(BF16)
    cum = _dot_split_rhs(tri, la, 3)
    cum_t = _dot_split(la, tri, 3, (((0,), (1,)), ((), ())))
    dtw = _dot_split(dtv, wide_ref[...], 2)
    cumw = _dot_split(cum, wide_ref[...], 3)
    segcol = _dot_split(cum, cols_ref[...], 3)
    return a, dtv, row, col, tri, cum_t, dtw, cumw, segcol


def _decay(segcol, cum_t, h, keep):
    return jnp.where(keep, jnp.exp(jnp.minimum(segcol[:, 128 * h:128 * h + 128] - cum_t[h:h + 1, :], 0.0)), 0.0)


def _decay_t(segcol, cum_t, h, keep_t):
    return jnp.where(keep_t, jnp.exp(jnp.minimum(cum_t[h:h + 1, :] - segcol[:, 128 * h:128 * h + 128], 0.0)), 0.0)


def _ssd_fwd(xbc_act, dt, alog):
    T = xbc_act.shape[0]
    q = SSD_CHUNK
    nc = T // q
    to_wide, _, to_cols = _ssd_expanders()

    def body(xbc_ref, dt_ref, alog_ref, wide_ref, cols_ref, y_ref, sp_ref, st_ref, xd_ref, xde_ref):
        @pl.when(pl.program_id(0) == 0)
        def _():
            st_ref[...] = jnp.zeros_like(st_ref)

        a, dtv, row, col, tri, cum_t, dtw, cumw, segcol = _ssd_prep(dt_ref, alog_ref, wide_ref, cols_ref)
        clw = cumw[q - 1:q, :]
        ecw = jnp.exp(cumw)
        xd = xbc_ref[:, 0:SSD_INNER] * dtw
        xd_ref[...] = xd.astype(BF16)
        xde_ref[...] = (xd * jnp.exp(clw - cumw)).astype(BF16)
        cdw = jnp.exp(clw)
        keep = col <= row
        sp_ref[0] = st_ref[...]
        for g in range(SSD_GROUPS):
            gl = slice(GROUP_LANES * g, GROUP_LANES * (g + 1))
            bb = xbc_ref[:, 1024 + 128 * g:1152 + 128 * g].astype(BF16)
            cb = xbc_ref[:, 1536 + 128 * g:1664 + 128 * g].astype(BF16)
            gm = _dot_nt(cb, bb)
            stp = st_ref[g]
            yoff = _dot(cb, stp.astype(BF16)) * ecw[:, gl]
            for r in range(SSD_HPG):
                h = SSD_HPG * g + r
                m = (gm * _decay(segcol, cum_t, h, keep)).astype(BF16)
                y_ref[:, 64 * h:64 * h + 64] = _dot(m, xd_ref[:, 64 * h:64 * h + 64]) + yoff[:, 64 * r:64 * r + 64]
            st_ref[g] = stp * cdw[:, gl] + _dot_tn(bb, xde_ref[:, gl])

    return _pc(
        body, name="ssd_fwd", grid=(nc,),
        in_specs=[_row_spec(q, CONV_DIM), _row_spec(q, DT_PAD), _const_spec((1, DT_PAD)),
                  _const_spec(to_wide.shape), _const_spec(to_cols.shape)],
        out_specs=[_row_spec(q, SSD_INNER),
                   pl.BlockSpec((1, SSD_GROUPS, SSD_STATE, GROUP_LANES), lambda i: (i, 0, 0, 0))],
        out_shape=[jax.ShapeDtypeStruct((T, SSD_INNER), F32),
                   jax.ShapeDtypeStruct((nc, SSD_GROUPS, SSD_STATE, GROUP_LANES), F32)],
        scratch_shapes=[pltpu.VMEM((SSD_GROUPS, SSD_STATE, GROUP_LANES), F32), pltpu.VMEM((q, SSD_INNER), BF16),
                        pltpu.VMEM((q, SSD_INNER), BF16)],
        compiler_params=_cparams(("arbitrary",)),
    )(xbc_act, dt, alog, to_wide, to_cols)


def _ssd_bwd(xbc_act, dt, alog, sprev, dy):
    T = xbc_act.shape[0]
    q = SSD_CHUNK
    nc = T // q
    to_wide, to_heads, to_cols = _ssd_expanders()

    def rev(width):
        return pl.BlockSpec((q, width), lambda i: (nc - 1 - i, 0))

    def body(xbc_ref, dt_ref, alog_ref, sp_ref, dy_ref, wide_ref, heads_ref, cols_ref,
             dxs_ref, dB_ref, dC_ref, ddt_ref, dalog_ref, ds_ref, xd_ref, dxd_ref):
        i = pl.program_id(0)

        @pl.when(i == 0)
        def _():
            ds_ref[...] = jnp.zeros_like(ds_ref)
            dalog_ref[...] = jnp.zeros_like(dalog_ref)

        a, dtv, row, col, tri, cum_t, dtw, cumw, segcol = _ssd_prep(dt_ref, alog_ref, wide_ref, cols_ref)
        clw = cumw[q - 1:q, :]
        ecw = jnp.exp(cumw)
        dew = jnp.exp(clw - cumw)
        cdw = jnp.exp(clw)
        xs = xbc_ref[:, 0:SSD_INNER]
        xd = xs * dtw
        xd_ref[...] = xd.astype(BF16)
        dyv = dy_ref[...]
        dye = (dyv * ecw).astype(BF16)
        xde = (xd * dew).astype(BF16)
        keep = col <= row
        keep_t = col >= row
        rows_k = lax.broadcasted_iota(jnp.int32, (SSD_HPG * q, DT_PAD), 0) // q
        lanes_k = lax.broadcasted_iota(jnp.int32, (SSD_HPG * q, DT_PAD), 1)
        dcw_parts = []
        dcum = jnp.zeros((q, DT_PAD), F32)
        for g in range(SSD_GROUPS):
            gl = slice(GROUP_LANES * g, GROUP_LANES * (g + 1))
            bb = xbc_ref[:, 1024 + 128 * g:1152 + 128 * g].astype(BF16)
            cb = xbc_ref[:, 1536 + 128 * g:1664 + 128 * g].astype(BF16)
            gm = _dot_nt(cb, bb)
            gmt = _dot_nt(bb, cb)
            stp = sp_ref[0, g]
            dst = ds_ref[g]
            stpb = stp.astype(BF16)
            dstb = dst.astype(BF16)
            yoff = _dot(cb, stpb) * ecw[:, gl]
            dcg = _dot_nt(dye[:, gl], stpb)
            ds_ref[g] = dst * cdw[:, gl] + _dot_tn(cb, dye[:, gl])
            dlast = jnp.sum(dst * stp, axis=0, keepdims=True) * cdw[:, gl]
            dbg = _dot_nt(xde[:, gl], dstb)
            w = _dot(bb, dstb) * dew[:, gl]
            wx = w * xd[:, gl]
            dlast = dlast + jnp.sum(wx, axis=0, keepdims=True)
            dcw_parts.append(dyv[:, gl] * yoff - wx
                             + jnp.where(lax.broadcasted_iota(jnp.int32, (q, 1), 0) == q - 1, dlast, 0.0))
            dgm = jnp.zeros((q, q), F32)
            diag = []
            for r in range(SSD_HPG):
                h = SSD_HPG * g + r
                hl = slice(64 * h, 64 * h + 64)
                dyb = dy_ref[:, hl].astype(BF16)
                xdh = xd_ref[:, hl]
                dm = _dot_nt(dyb, xdh)
                dmt = _dot_nt(xdh, dyb)
                dec = _decay(segcol, cum_t, h, keep)
                mt = gmt * _decay_t(segcol, cum_t, h, keep_t)
                dgm += dm * dec
                diag.append(dm * (gm * dec) - dmt * mt)
                dxd_ref[:, hl] = _dot(mt.astype(BF16), dyb) + w[:, 64 * r:64 * r + 64]
            onehots = (lanes_k == SSD_HPG * g + rows_k).astype(BF16)
            dcum += _dot_split(jnp.concatenate(diag, axis=1), onehots, 2)
            dgb = dgm.astype(BF16)
            dC_ref[:, 128 * g:128 * g + 128] = dcg + _dot(dgb, bb)
            dB_ref[:, 128 * g:128 * g + 128] = dbg + _dot_tn(dgb, cb)
        dxd = dxd_ref[...]
        dxs_ref[...] = dxd * dtw
        dcum += _dot_split(jnp.concatenate(dcw_parts, axis=1), heads_ref[...], 2)
        dla = _dot_split_rhs(tri, dcum, 3, (((0,), (0,)), ((), ())))
        ddt_ref[...] = _dot_split(xs * dxd, heads_ref[...], 2) + dla * a
        dalog_ref[...] += jnp.sum(dla * dtv, axis=0, keepdims=True)

        @pl.when(i == nc - 1)
        def _():
            dalog_ref[...] = dalog_ref[...] * a

    st_spec = pl.BlockSpec((1, SSD_GROUPS, SSD_STATE, GROUP_LANES), lambda i: (nc - 1 - i, 0, 0, 0))
    return _pc(
        body, name="ssd_bwd", grid=(nc,),
        in_specs=[rev(CONV_DIM), rev(DT_PAD), _const_spec((1, DT_PAD)), st_spec, rev(SSD_INNER),
                  _const_spec(to_wide.shape), _const_spec(to_heads.shape), _const_spec(to_cols.shape)],
        out_specs=[rev(SSD_INNER), rev(512), rev(512), rev(DT_PAD), _const_spec((1, DT_PAD))],
        out_shape=[jax.ShapeDtypeStruct((T, SSD_INNER), F32), jax.ShapeDtypeStruct((T, 512), F32),
                   jax.ShapeDtypeStruct((T, 512), F32), jax.ShapeDtypeStruct((T, DT_PAD), F32),
                   jax.ShapeDtypeStruct((1, DT_PAD), F32)],
        scratch_shapes=[pltpu.VMEM((SSD_GROUPS, SSD_STATE, GROUP_LANES), F32), pltpu.VMEM((q, SSD_INNER), BF16),
                        pltpu.VMEM((q, SSD_INNER), F32)],
        compiler_params=_cparams(("arbitrary",)),
    )(xbc_act, dt, alog, sprev, dy, to_wide, to_heads, to_cols)


def _s5_disc_vals(a_re, a_im, log_dt, b_re, b_im):
    dt = jnp.exp(log_dt)
    mag = jnp.exp(a_re * dt)
    ab_re = mag * jnp.cos(a_im * dt)
    ab_im = mag * jnp.sin(a_im * dt)
    den = a_re * a_re + a_im * a_im
    nr = ab_re - 1.0
    ni = ab_im
    coef_re = (nr * a_re + ni * a_im) / den
    coef_im = (ni * a_re - nr * a_im) / den
    bb_re = coef_re * b_re - coef_im * b_im
    bb_im = coef_re * b_im + coef_im * b_re
    return ab_re, ab_im, bb_re, bb_im


def _s5_disc(a_re, a_im, log_dt, b_re, b_im):
    def body(ar, ai, ld, br, bi, o1, o2, o3, o4):
        o1[...], o2[...], o3[...], o4[...] = _s5_disc_vals(ar[...], ai[...], ld[...], br[...], bi[...])

    return _pc(
        body, name="s5_disc",
        out_shape=[jax.ShapeDtypeStruct((S5_STATES, 1), F32), jax.ShapeDtypeStruct((S5_STATES, 1), F32),
                   jax.ShapeDtypeStruct((S5_STATES, 16), F32), jax.ShapeDtypeStruct((S5_STATES, 16), F32)],
    )(a_re, a_im, log_dt, b_re, b_im)


def _s5_disc_bwd(a_re, a_im, log_dt, b_re, b_im, d_ab_re, d_ab_im, d_bb_re, d_bb_im):
    def body(ar, ai, ld, br, bi, g1, g2, g3, g4, o1, o2, o3, o4, o5):
        _, vjp = jax.vjp(_s5_disc_vals, ar[...], ai[...], ld[...], br[...], bi[...])
        d1, d2, d3, d4, d5 = vjp((g1[...], g2[...], g3[...], g4[...]))
        o1[...] = d1
        o2[...] = d2
        grp = lax.broadcasted_iota(jnp.int32, (32, S5_STATES), 0)
        st = lax.broadcasted_iota(jnp.int32, (32, S5_STATES), 1)
        sel = (st // 64 == grp).astype(F32)
        o3[...] = _dot_hi(sel, d3)
        o4[...] = d4
        o5[...] = d5

    return _pc(
        body, name="s5_disc_bwd",
        out_shape=[jax.ShapeDtypeStruct((S5_STATES, 1), F32), jax.ShapeDtypeStruct((S5_STATES, 1), F32),
                   jax.ShapeDtypeStruct((32, 1), F32),
                   jax.ShapeDtypeStruct((S5_STATES, 16), F32), jax.ShapeDtypeStruct((S5_STATES, 16), F32)],
    )(a_re, a_im, log_dt, b_re, b_im, d_ab_re, d_ab_im, d_bb_re, d_bb_im)


def _cmul_add(xr, xi, pr, pi, yr, yi):
    return xr + pr * yr - pi * yi, xi + pr * yi + pi * yr


def _powers(ar, ai, n):
    out = [(ar, ai)]
    for _ in range(n - 1):
        pr, pi = out[-1]
        out.append((pr * pr - pi * pi, 2.0 * pr * pi))
    return out


_BW = S5_STATES // S5_BLOCKS
_BI = S5_WIDTH // S5_BLOCKS
SUB = 8
S5_ROWS = S5_CHUNK // SUB


def _scan8(br, bi, pws, rowin, reverse):
    k = 1
    for pr, pi in pws:
        if reverse:
            keep = rowin < SUB - k
            sr = jnp.where(keep, pltpu.roll(br, SUB - k, 0), 0.0)
            si = jnp.where(keep, pltpu.roll(bi, SUB - k, 0), 0.0)
        else:
            keep = rowin >= k
            sr = jnp.where(keep, pltpu.roll(br, k, 0), 0.0)
            si = jnp.where(keep, pltpu.roll(bi, k, 0), 0.0)
        br, bi = _cmul_add(br, bi, pr, pi, sr, si)
        k *= 2
    return br, bi


def _s5_tables(ab_ref, tab_ref, reverse):
    rowin = lax.broadcasted_iota(jnp.int32, (SUB, 1), 0)
    ar = ab_ref[0:1, :]
    ai = -ab_ref[1:2, :] if reverse else ab_ref[1:2, :]
    hit = rowin == (SUB - 1 if reverse else 0)
    zero = jnp.zeros((SUB, S5_STATES), F32)
    pr, pi = _scan8(jnp.where(hit, ar, 0.0) + zero, jnp.where(hit, ai, 0.0) + zero, _powers(ar, ai, 3), rowin, reverse)
    tab_ref[0:SUB, :] = pr
    tab_ref[SUB:2 * SUB, :] = pi


def _s5_fwd(u5, wb4, wc4, ab, dvec, rider=None):
    T = u5.shape[0]
    q = S5_CHUNK
    nc = T // q

    def body(u_ref, wb_ref, wc_ref, ab_ref, d_ref, y_ref, sp_ref, carry_ref, tab_ref, sr_ref, si_ref):
        i = pl.program_id(0)
        rowin = lax.broadcasted_iota(jnp.int32, (SUB, 1), 0)

        @pl.when(i == 0)
        def _():
            carry_ref[...] = jnp.zeros_like(carry_ref)
            _s5_tables(ab_ref, tab_ref, False)

        sp_ref[0] = carry_ref[...]
        for j in range(S5_BLOCKS):
            bu = _dot(u_ref[:, _BI * j:_BI * (j + 1)].astype(BF16), wb_ref[j])
            sr_ref[:, :, _BW * j:_BW * (j + 1)] = bu[:, :_BW].reshape(S5_ROWS, SUB, _BW)
            si_ref[:, :, _BW * j:_BW * (j + 1)] = bu[:, _BW:].reshape(S5_ROWS, SUB, _BW)
        pws = _powers(ab_ref[0:1, :], ab_ref[1:2, :], 3)
        tr, ti = tab_ref[0:SUB, :], tab_ref[SUB:2 * SUB, :]
        cr, ci = carry_ref[0:1, :], carry_ref[1:2, :]
        for k in range(S5_ROWS):
            sr, si = _scan8(sr_ref[k], si_ref[k], pws, rowin, False)
            sr, si = _cmul_add(sr, si, tr, ti, cr, ci)
            sr_ref[k] = sr
            si_ref[k] = si
            cr, ci = sr[SUB - 1:SUB, :], si[SUB - 1:SUB, :]
        carry_ref[0:1, :] = cr
        carry_ref[1:2, :] = ci
        for j in range(S5_BLOCKS):
            sl = slice(_BW * j, _BW * (j + 1))
            ul = slice(_BI * j, _BI * (j + 1))
            s = jnp.concatenate([sr_ref[:, :, sl].reshape(q, _BW), si_ref[:, :, sl].reshape(q, _BW)], axis=1).astype(BF16)
            y_ref[:, ul] = _dot(s, wc_ref[j]) + d_ref[:, ul] * u_ref[:, ul]

    return _call(
        body, rider, name="s5_fwd", grid=(nc,),
        in_specs=[_row_spec(q, S5_WIDTH), _const_spec((S5_BLOCKS, _BI, 2 * _BW)), _const_spec((S5_BLOCKS, 2 * _BW, _BI)),
                  _const_spec((8, S5_STATES)), _const_spec((1, S5_WIDTH))],
        out_specs=[_row_spec(q, S5_WIDTH), pl.BlockSpec((1, 8, S5_STATES), lambda i: (i, 0, 0))],
        out_shape=[jax.ShapeDtypeStruct((T, S5_WIDTH), F32), jax.ShapeDtypeStruct((nc, 8, S5_STATES), F32)],
        scratch_shapes=[pltpu.VMEM((8, S5_STATES), F32), pltpu.VMEM((2 * SUB, S5_STATES), F32),
                        pltpu.VMEM((S5_ROWS, SUB, S5_STATES), F32), pltpu.VMEM((S5_ROWS, SUB, S5_STATES), F32)],
        compiler_params=_cparams(("arbitrary",)),
    )(u5, wb4, wc4, ab, dvec)


def _s5_bwd(u5, dy5, wb4, wc4, ab, dvec, sprev, rider=None):
    T = u5.shape[0]
    q = S5_CHUNK
    nc = T // q

    def rev(width):
        return pl.BlockSpec((q, width), lambda i: (nc - 1 - i, 0))

    def body(u_ref, dy_ref, wb_ref, wc_ref, ab_ref, d_ref, sp_ref, du_ref, dwb_ref, dwc_ref, dab_ref, dd_ref,
             carry_ref, tab_ref, rtab_ref, sr_ref, si_ref, lr_ref, li_ref):
        i = pl.program_id(0)
        rowin = lax.broadcasted_iota(jnp.int32, (SUB, 1), 0)

        @pl.when(i == 0)
        def _():
            carry_ref[...] = jnp.zeros_like(carry_ref)
            dwb_ref[...] = jnp.zeros_like(dwb_ref)
            dwc_ref[...] = jnp.zeros_like(dwc_ref)
            dab_ref[...] = jnp.zeros_like(dab_ref)
            dd_ref[...] = jnp.zeros_like(dd_ref)
            _s5_tables(ab_ref, tab_ref, False)
            _s5_tables(ab_ref, rtab_ref, True)

        for j in range(S5_BLOCKS):
            sl = slice(_BW * j, _BW * (j + 1))
            ul = slice(_BI * j, _BI * (j + 1))
            bu = _dot(u_ref[:, ul].astype(BF16), wb_ref[j])
            sr_ref[:, :, sl] = bu[:, :_BW].reshape(S5_ROWS, SUB, _BW)
            si_ref[:, :, sl] = bu[:, _BW:].reshape(S5_ROWS, SUB, _BW)
            ds = _dot_nt(dy_ref[:, ul].astype(BF16), wc_ref[j])
            lr_ref[:, :, sl] = ds[:, :_BW].reshape(S5_ROWS, SUB, _BW)
            li_ref[:, :, sl] = ds[:, _BW:].reshape(S5_ROWS, SUB, _BW)
        ar, ai = ab_ref[0:1, :], ab_ref[1:2, :]
        pws = _powers(ar, ai, 3)
        tr, ti = tab_ref[0:SUB, :], tab_ref[SUB:2 * SUB, :]
        cr, ci = sp_ref[0, 0:1, :], sp_ref[0, 1:2, :]
        for k in range(S5_ROWS):
            sr, si = _scan8(sr_ref[k], si_ref[k], pws, rowin, False)
            sr, si = _cmul_add(sr, si, tr, ti, cr, ci)
            sr_ref[k] = sr
            si_ref[k] = si
            cr, ci = sr[SUB - 1:SUB, :], si[SUB - 1:SUB, :]
        pws = _powers(ar, -ai, 3)
        tr, ti = rtab_ref[0:SUB, :], rtab_ref[SUB:2 * SUB, :]
        cr, ci = carry_ref[0:1, :], carry_ref[1:2, :]
        acc_r = jnp.zeros((SUB, S5_STATES), F32)
        acc_i = jnp.zeros((SUB, S5_STATES), F32)
        for k in reversed(range(S5_ROWS)):
            lr, li = _scan8(lr_ref[k], li_ref[k], pws, rowin, True)
            lr, li = _cmul_add(lr, li, tr, ti, cr, ci)
            lr_ref[k] = lr
            li_ref[k] = li
            cr, ci = lr[0:1, :], li[0:1, :]
            if k > 0:
                before_r, before_i = sr_ref[k - 1, SUB - 1:SUB, :], si_ref[k - 1, SUB - 1:SUB, :]
            else:
                before_r, before_i = sp_ref[0, 0:1, :], sp_ref[0, 1:2, :]
            keep = rowin >= 1
            pr = jnp.where(keep, pltpu.roll(sr_ref[k], 1, 0), before_r)
            pi = jnp.where(keep, pltpu.roll(si_ref[k], 1, 0), before_i)
            acc_r += lr * pr + li * pi
            acc_i += li * pr - lr * pi
        carry_ref[0:1, :] = cr
        carry_ref[1:2, :] = ci
        dab_ref[0:1, :] += jnp.sum(acc_r, axis=0, keepdims=True)
        dab_ref[1:2, :] += jnp.sum(acc_i, axis=0, keepdims=True)
        for j in range(S5_BLOCKS):
            sl = slice(_BW * j, _BW * (j + 1))
            ul = slice(_BI * j, _BI * (j + 1))
            u = u_ref[:, ul]
            dy = dy_ref[:, ul]
            dyb = dy.astype(BF16)
            lam = jnp.concatenate([lr_ref[:, :, sl].reshape(q, _BW), li_ref[:, :, sl].reshape(q, _BW)], axis=1).astype(BF16)
            s = jnp.concatenate([sr_ref[:, :, sl].reshape(q, _BW), si_ref[:, :, sl].reshape(q, _BW)], axis=1).astype(BF16)
            du_ref[:, ul] = _dot_nt(lam, wb_ref[j]) + d_ref[:, ul] * dy
            dwb_ref[j] += _dot_tn(u.astype(BF16), lam)
            dwc_ref[j] += _dot_tn(s, dyb)
            dd_ref[:, ul] += jnp.sum(dy * u, axis=0, keepdims=True)

    big = pltpu.VMEM((S5_ROWS, SUB, S5_STATES), F32)
    return _call(
        body, rider, name="s5_bwd", grid=(nc,),
        in_specs=[rev(S5_WIDTH), rev(S5_WIDTH), _const_spec((S5_BLOCKS, _BI, 2 * _BW)), _const_spec((S5_BLOCKS, 2 * _BW, _BI)),
                  _const_spec((8, S5_STATES)), _const_spec((1, S5_WIDTH)),
                  pl.BlockSpec((1, 8, S5_STATES), lambda i: (nc - 1 - i, 0, 0))],
        out_specs=[rev(S5_WIDTH), _const_spec((S5_BLOCKS, _BI, 2 * _BW)), _const_spec((S5_BLOCKS, 2 * _BW, _BI)),
                   _const_spec((8, S5_STATES)), _const_spec((1, S5_WIDTH))],
        out_shape=[jax.ShapeDtypeStruct((T, S5_WIDTH), F32), jax.ShapeDtypeStruct((S5_BLOCKS, _BI, 2 * _BW), F32),
                   jax.ShapeDtypeStruct((S5_BLOCKS, 2 * _BW, _BI), F32), jax.ShapeDtypeStruct((8, S5_STATES), F32),
                   jax.ShapeDtypeStruct((1, S5_WIDTH), F32)],
        scratch_shapes=[pltpu.VMEM((8, S5_STATES), F32), pltpu.VMEM((2 * SUB, S5_STATES), F32),
                        pltpu.VMEM((2 * SUB, S5_STATES), F32), big, big, big, big],
        compiler_params=_cparams(("arbitrary",)),
    )(u5, dy5, wb4, wc4, ab, dvec, sprev)


def _merge_vals(ys, xs, z, y5, gates, dvec, gssd, glu_w, glu_b, wbr):
    sz = _sigmoid(z)
    qv = ys + dvec * xs
    pre = qv * (z * sz)
    yn, rs = [], []
    for gi in range(SSD_GROUPS):
        p, r = _rms(pre[:, 256 * gi:256 * (gi + 1)])
        yn.append(p)
        rs.append(r)
    yn = jnp.concatenate(yn, axis=1)
    ya = yn * gssd
    gel = _gelu(y5)
    sg = _sigmoid(_dot(gel.astype(BF16), glu_w) + glu_b)
    yb = gel * sg
    pa = _dot(ya.astype(BF16), wbr[0:SSD_INNER, :])
    pb = _dot(yb.astype(BF16), wbr[SSD_INNER:, :])
    s0 = _sigmoid(gates[:, :D_MODEL])
    s1 = _sigmoid(gates[:, D_MODEL:])
    merged = s0 * pa + s1 * pb
    return dict(sz=sz, qv=qv, yn=yn, rs=rs, ya=ya, gel=gel, sg=sg, yb=yb, pa=pa, pb=pb, s0=s0, s1=s1, merged=merged)


def _merge_specs(tm):
    acts = [_row_spec(tm, 1024), _row_spec(tm, 1024, 0), _row_spec(tm, 1024), _row_spec(tm, 512), _row_spec(tm, 2048),
            _row_spec(tm, 1024)]
    params = [_const_spec((1, 1024)), _const_spec((1, 1024)), _const_spec((512, 512)), _const_spec((1, 512)),
              _hbm_spec(), _hbm_spec()]
    return acts, params


def _merge_fwd(ys, xbc_act, z, y5, gates, x, dvec, gssd, glu_w, glu_b, wbr, wout):
    T = x.shape[0]
    tm = TOKEN_TILE
    acts, params = _merge_specs(tm)

    def body(ys_ref, xs_ref, z_ref, y5_ref, gt_ref, x_ref, dv_ref, gs_ref, gw_ref, gb_ref, wbr_hbm, wout_hbm, x1_ref,
             wbr_ref, wout_ref):
        @pl.when(pl.program_id(0) == 0)
        def _():
            pltpu.sync_copy(wbr_hbm, wbr_ref)
            pltpu.sync_copy(wout_hbm, wout_ref)

        v = _merge_vals(ys_ref[...], xs_ref[...], z_ref[...], y5_ref[...], gt_ref[...], dv_ref[...], gs_ref[...],
                        gw_ref[...], gb_ref[...], wbr_ref)
        x1_ref[...] = x_ref[...] + _dot(v["merged"].astype(BF16), wout_ref[...])

    return _pc(
        body, name="merge_fwd", grid=(T // tm,),
        in_specs=acts + params, out_specs=_row_spec(tm, 1024),
        out_shape=jax.ShapeDtypeStruct((T, 1024), F32),
        scratch_shapes=[pltpu.VMEM((1536, 1024), BF16), pltpu.VMEM((1024, 1024), BF16)],
        compiler_params=_cparams(("arbitrary",)),
    )(ys, xbc_act, z, y5, gates, x, dvec, gssd, glu_w, glu_b, wbr, wout)


def _merge_bwd(ys, xbc_act, z, y5, gates, dx1, dvec, gssd, glu_w, glu_b, wbr, wout, head_sel, rider=None):
    T = dx1.shape[0]
    tm = TOKEN_TILE
    acts, params = _merge_specs(tm)

    def body(ys_ref, xs_ref, z_ref, y5_ref, gt_ref, dx1_ref, dv_ref, gs_ref, gw_ref, gb_ref, wbr_hbm, wout_hbm, hs_ref,
             dys_ref, dxs_ref, dz_ref, dy5_ref, dgt_ref, mg_ref, ya_ref, yb_ref, dpa_ref, dpb_ref, gel_ref, dpre_ref,
             ddv_ref, dgs_ref, dgb_ref, wbr_ref, wout_ref, ddacc_ref):
        i = pl.program_id(0)

        @pl.when(i == 0)
        def _():
            pltpu.sync_copy(wbr_hbm, wbr_ref)
            pltpu.sync_copy(wout_hbm, wout_ref)
            ddacc_ref[...] = jnp.zeros_like(ddacc_ref)
            dgs_ref[...] = jnp.zeros_like(dgs_ref)
            dgb_ref[...] = jnp.zeros_like(dgb_ref)

        ys, xs, z, y5, gates = ys_ref[...], xs_ref[...], z_ref[...], y5_ref[...], gt_ref[...]
        dvv, gsv, gw = dv_ref[...], gs_ref[...], gw_ref[...]
        v = _merge_vals(ys, xs, z, y5, gates, dvv, gsv, gw, gb_ref[...], wbr_ref)
        dmg = _dot_nt(dx1_ref[...].astype(BF16), wout_ref[...])
        s0, s1, pa, pb = v["s0"], v["s1"], v["pa"], v["pb"]
        dgt_ref[:, :D_MODEL] = dmg * pa * s0 * (1.0 - s0)
        dgt_ref[:, D_MODEL:] = dmg * pb * s1 * (1.0 - s1)
        dpa = (dmg * s0).astype(BF16)
        dpb = (dmg * s1).astype(BF16)
        dya = _dot_nt(dpa, wbr_ref[0:SSD_INNER, :])
        dyb = _dot_nt(dpb, wbr_ref[SSD_INNER:, :])
        gel, sg = v["gel"], v["sg"]
        dpre = (dyb * gel * sg * (1.0 - sg))
        dgb_ref[...] += jnp.sum(dpre, axis=0, keepdims=True)
        dpre_b = dpre.astype(BF16)
        dgel = dyb * sg + _dot_nt(dpre_b, gw)
        dy5_ref[...] = dgel * _gelu_grad(y5)
        yn = v["yn"]
        dgs_ref[...] += jnp.sum(dya * yn, axis=0, keepdims=True)
        dyn = dya * gsv
        dpre_a = jnp.concatenate(
            [_rms_bwd(yn[:, 256 * gi:256 * (gi + 1)], v["rs"][gi], dyn[:, 256 * gi:256 * (gi + 1)])
             for gi in range(SSD_GROUPS)], axis=1)
        sz, qv = v["sz"], v["qv"]
        dq = dpre_a * (z * sz)
        dz_ref[...] = dpre_a * qv * (sz * (1.0 + z * (1.0 - sz)))
        dys_ref[...] = dq
        dxs_ref[...] = dq * dvv
        ddacc_ref[...] += jnp.sum(dq * xs, axis=0, keepdims=True)
        mg_ref[...] = v["merged"].astype(BF16)
        ya_ref[...] = v["ya"].astype(BF16)
        yb_ref[...] = v["yb"].astype(BF16)
        dpa_ref[...] = dpa
        dpb_ref[...] = dpb
        gel_ref[...] = gel.astype(BF16)
        dpre_ref[...] = dpre_b

        @pl.when(i == pl.num_programs(0) - 1)
        def _():
            ddv_ref[...] = _dot_hi(ddacc_ref[...], hs_ref[...])

    outs = [(1024, F32), (1024, F32), (1024, F32), (512, F32), (2048, F32),
            (1024, BF16), (1024, BF16), (512, BF16), (1024, BF16), (1024, BF16), (512, BF16), (512, BF16)]
    return _call(
        body, rider, name="merge_bwd", grid=(T // tm,),
        in_specs=acts + params + [_const_spec((1024, DT_PAD))],
        out_specs=[_row_spec(tm, w) for w, _ in outs] + [_const_spec((1, DT_PAD)), _const_spec((1, 1024)), _const_spec((1, 512))],
        out_shape=[jax.ShapeDtypeStruct((T, w), d) for w, d in outs] + [
            jax.ShapeDtypeStruct((1, DT_PAD), F32), jax.ShapeDtypeStruct((1, 1024), F32), jax.ShapeDtypeStruct((1, 512), F32)],
        scratch_shapes=[pltpu.VMEM((1536, 1024), BF16), pltpu.VMEM((1024, 1024), BF16), pltpu.VMEM((1, 1024), F32)],
        compiler_params=_cparams(("arbitrary",)),
    )(ys, xbc_act, z, y5, gates, dx1, dvec, gssd, glu_w, glu_b, wbr, wout, head_sel)


def _mlp_fwd(x1, g, w1, w2):
    T = x1.shape[0]
    tm = TOKEN_TILE

    def body(x_ref, g_ref, w1_hbm, w2_hbm, o_ref, w1_ref, w2_ref):
        @pl.when(pl.program_id(0) == 0)
        def _():
            pltpu.sync_copy(w1_hbm, w1_ref)
            pltpu.sync_copy(w2_hbm, w2_ref)

        xv = x_ref[...]
        xn, _ = _rms(xv)
        h = (xn * g_ref[...]).astype(BF16)
        acc = xv
        for s in range(FF_SHARDS):
            rl = jnp.maximum(_dot(h, w1_ref[s]), 0.0)
            acc += _dot((rl * rl).astype(BF16), w2_ref[FF_SHARD * s:FF_SHARD * (s + 1), :])
        o_ref[...] = acc

    return _pc(
        body, name="mlp_fwd", grid=(T // tm,),
        in_specs=[_row_spec(tm, 1024), _const_spec((1, 1024)), _hbm_spec(), _hbm_spec()],
        out_specs=_row_spec(tm, 1024), out_shape=jax.ShapeDtypeStruct((T, 1024), F32),
        scratch_shapes=[pltpu.VMEM((FF_SHARDS, D_MODEL, FF_SHARD), BF16), pltpu.VMEM((D_FF, D_MODEL), BF16)],
        compiler_params=_cparams(("arbitrary",)),
    )(x1, g, w1, w2)


def _mlp_bwd(x1, dx2, g, w1, w2):
    T = x1.shape[0]
    tm = TOKEN_TILE

    def body(x_ref, dx2_ref, g_ref, w1_hbm, w2_hbm, dx1_ref, h_ref, act_ref, da_ref, dg_ref, w1_ref, w2_ref):
        @pl.when(pl.program_id(0) == 0)
        def _():
            pltpu.sync_copy(w1_hbm, w1_ref)
            pltpu.sync_copy(w2_hbm, w2_ref)
            dg_ref[...] = jnp.zeros_like(dg_ref)

        xn, r = _rms(x_ref[...])
        gv = g_ref[...]
        h = (xn * gv).astype(BF16)
        h_ref[...] = h
        dx2 = dx2_ref[...]
        dx2b = dx2.astype(BF16)
        dh = jnp.zeros((tm, D_MODEL), F32)
        for s in range(FF_SHARDS):
            ff = slice(FF_SHARD * s, FF_SHARD * (s + 1))
            rl = jnp.maximum(_dot(h, w1_ref[s]), 0.0)
            act_ref[:, ff] = (rl * rl).astype(BF16)
            da = (_dot_nt(dx2b, w2_ref[ff, :]) * (2.0 * rl)).astype(BF16)
            da_ref[:, ff] = da
            dh += _dot_nt(da, w1_ref[s])
        dg_ref[...] += jnp.sum(dh * xn, axis=0, keepdims=True)
        dx1_ref[...] = dx2 + _rms_bwd(xn, r, dh * gv)

    return _pc(
        body, name="mlp_bwd", grid=(T // tm,),
        in_specs=[_row_spec(tm, 1024), _row_spec(tm, 1024), _const_spec((1, 1024)), _hbm_spec(), _hbm_spec()],
        out_specs=[_row_spec(tm, 1024), _row_spec(tm, 1024), _row_spec(tm, D_FF), _row_spec(tm, D_FF), _const_spec((1, 1024))],
        out_shape=[jax.ShapeDtypeStruct((T, 1024), F32), jax.ShapeDtypeStruct((T, 1024), BF16),
                   jax.ShapeDtypeStruct((T, D_FF), BF16), jax.ShapeDtypeStruct((T, D_FF), BF16),
                   jax.ShapeDtypeStruct((1, 1024), F32)],
        scratch_shapes=[pltpu.VMEM((FF_SHARDS, D_MODEL, FF_SHARD), BF16), pltpu.VMEM((D_FF, D_MODEL), BF16)],
        compiler_params=_cparams(("arbitrary",)),
    )(x1, dx2, g, w1, w2)


def _loss_head(x2, target, g):
    T = x2.shape[0]
    tm = TOKEN_TILE

    def body(x_ref, t_ref, g_ref, dx_ref, loss_ref, dg_ref):
        @pl.when(pl.program_id(0) == 0)
        def _():
            loss_ref[...] = jnp.zeros_like(loss_ref)
            dg_ref[...] = jnp.zeros_like(dg_ref)

        xn, r = _rms(x_ref[...])
        gv = g_ref[...]
        err = xn * gv - t_ref[...]
        loss_ref[...] += jnp.sum(err * err, axis=0, keepdims=True) * (0.5 / D_MODEL)
        dy = err * (1.0 / D_MODEL)
        dg_ref[...] += jnp.sum(dy * xn, axis=0, keepdims=True)
        dx_ref[...] = _rms_bwd(xn, r, dy * gv)

    return _pc(
        body, name="loss_head", grid=(T // tm,),
        in_specs=[_row_spec(tm, 1024), _row_spec(tm, 1024), _const_spec((1, 1024))],
        out_specs=[_row_spec(tm, 1024), _const_spec((1, 1024)), _const_spec((1, 1024))],
        out_shape=[jax.ShapeDtypeStruct((T, 1024), F32), jax.ShapeDtypeStruct((1, 1024), F32),
                   jax.ShapeDtypeStruct((1, 1024), F32)],
        compiler_params=_cparams(("arbitrary",)),
    )(x2, target, g)


WGRAD_OUT_ELEMS = 2 * 1024 * 1024
WGRAD_TILE_BYTES = 4 * 1024 * 1024


def _wgrad(a, b, name, col_shards=None):
    T, K = a.shape
    N = b.shape[1]
    nb = N // col_shards if col_shards else min(N, 1024, max(128, WGRAD_OUT_ELEMS // K))
    tt = min(T, WGRAD_TOKENS)
    while tt * max(K * a.dtype.itemsize, nb * b.dtype.itemsize) > WGRAD_TILE_BYTES:
        tt //= 2
    assert N % nb == 0 and T % tt == 0
    if col_shards:
        out_spec = pl.BlockSpec((None, K, nb), lambda n, t: (n, 0, 0))
        out_shape = jax.ShapeDtypeStruct((col_shards, K, nb), F32)
    else:
        out_spec = pl.BlockSpec((K, nb), lambda n, t: (0, n))
        out_shape = jax.ShapeDtypeStruct((K, N), F32)

    def body(a_ref, b_ref, o_ref):
        @pl.when(pl.program_id(1) == 0)
        def _():
            o_ref[...] = jnp.zeros_like(o_ref)

        o_ref[...] += _dot_tn(a_ref[...].astype(BF16), b_ref[...].astype(BF16))

    return _pc(
        body, name=name, grid=(N // nb, T // tt),
        in_specs=[pl.BlockSpec((tt, K), lambda n, t: (t, 0)), pl.BlockSpec((tt, nb), lambda n, t: (t, n))],
        out_specs=out_spec, out_shape=out_shape,
        compiler_params=_cparams(("parallel", "arbitrary")),
    )(a, b)


def _s5_block_weights(bb_re, bb_im, c_re, c_im):
    eye = jnp.eye(8, dtype=F32)
    bre = bb_re.reshape(S5_BLOCKS, 8, 64, 16)
    bim = bb_im.reshape(S5_BLOCKS, 8, 64, 16)
    wb_re = jnp.einsum('jgpk,gh->jhkgp', bre, eye).reshape(S5_BLOCKS, _BI, _BW)
    wb_im = jnp.einsum('jgpk,gh->jhkgp', bim, eye).reshape(S5_BLOCKS, _BI, _BW)
    wb4 = jnp.concatenate([wb_re, wb_im], axis=2).astype(BF16)
    cre = c_re.reshape(S5_BLOCKS, 8, 16, 64)
    cim = c_im.reshape(S5_BLOCKS, 8, 16, 64)
    wc_re = jnp.einsum('jgkp,gh->jgphk', cre, eye).reshape(S5_BLOCKS, _BW, _BI)
    wc_im = jnp.einsum('jgkp,gh->jgphk', -cim, eye).reshape(S5_BLOCKS, _BW, _BI)
    wc4 = jnp.concatenate([wc_re, wc_im], axis=1).astype(BF16)
    return wb4, wc4


def _s5_block_grads(dwb4, dwc4):
    eye = jnp.eye(8, dtype=F32)
    dwb = dwb4.reshape(S5_BLOCKS, 8, 16, 2, 8, 64)
    dbb = jnp.einsum('jhkrgp,gh->rjgpk', dwb, eye).reshape(2, 32, 64, 16)
    dwc = dwc4.reshape(S5_BLOCKS, 2, 8, 64, 8, 16)
    dc = jnp.einsum('jrgphk,gh->rjgkp', dwc, eye).reshape(2, 32, 16, 64)
    return dbb[0], dbb[1], dc[0], -dc[1]


def _row(v, width=None):
    v = v.reshape(1, -1)
    if width is not None and v.shape[1] < width:
        v = jnp.concatenate([v, jnp.zeros((1, width - v.shape[1]), v.dtype)], axis=1)
    return v


def _local_step(x, target, p, comm=None):
    g_mix, g_mlp, g_fin = _row(p["norm_mix_g"]), _row(p["norm_mlp_g"]), _row(p["norm_final_g"])
    conv_b = _row(p["conv_b"])
    dt_bias = _row(p["dt_bias"], DT_PAD)
    alog = _row(p["a_log"], DT_PAD)
    dvec = _row(jnp.repeat(p["d_ssd"], SSD_HEADDIM))
    gssd = _row(p["ssd_norm_g"])
    s5d = _row(p["s5_d"])
    glu_b = _row(p["s5_glu_b"])
    head_sel = (jnp.arange(SSD_INNER)[:, None] // SSD_HEADDIM == jnp.arange(DT_PAD)[None, :]).astype(F32)

    a_re = p["s5_a_re"].reshape(S5_STATES, 1)
    a_im = p["s5_a_im"].reshape(S5_STATES, 1)
    log_dt = jnp.repeat(p["s5_log_dt"], 64).reshape(S5_STATES, 1)
    b_re = p["s5_b_re"].reshape(S5_STATES, 16)
    b_im = p["s5_b_im"].reshape(S5_STATES, 16)
    ab_re, ab_im, bb_re, bb_im = _s5_disc(a_re, a_im, log_dt, b_re, b_im)
    wb4, wc4 = _s5_block_weights(bb_re, bb_im, p["s5_c_re"], p["s5_c_im"])
    ab = jnp.concatenate([ab_re.reshape(1, S5_STATES), ab_im.reshape(1, S5_STATES), jnp.zeros((6, S5_STATES), F32)], axis=0)

    wp = p["w_in_perm"]

    z, xbc_raw, u5, gates, dt_raw, h = _inproj_fwd(x, g_mix, wp)
    xbc_act, dt = _conv_fwd(xbc_raw, dt_raw, p["conv_w"], conv_b, dt_bias)
    ys, ssd_states = _ssd_fwd(xbc_act, dt, alog)
    if comm is None:
        y5, s5_states = _s5_fwd(u5, wb4, wc4, ab, s5d)
    else:
        (y5, s5_states), late = _s5_fwd(u5, wb4, wc4, ab, s5d, rider=_Gather(comm["late_srcs"], comm["late_ks"]))
        p = {**p, **comm["late_unpack"](late)}
    wbr, wout, w1, w2, glu_w = p["w_branch"], p["w_out"], p["w_mlp_in"], p["w_mlp_out"], p["s5_glu_w"]
    x1 = _merge_fwd(ys, xbc_act, z, y5, gates, x, dvec, gssd, glu_w, glu_b, wbr, wout)
    x2 = _mlp_fwd(x1, g_mlp, w1, w2)
    dx2, loss_lanes, d_gfin = _loss_head(x2, target, g_fin)

    dx1, h2, act, da1, d_gmlp = _mlp_bwd(x1, dx2, g_mlp, w1, w2)
    d_w_mlp_out = _wgrad(act, dx2, "wgrad_mlp_out")
    d_w_mlp_in = _wgrad(h2, da1, "wgrad_mlp_in", col_shards=FF_SHARDS)
    merge_args = (ys, xbc_act, z, y5, gates, dx1, dvec, gssd, glu_w, glu_b, wbr, wout, head_sel)
    if comm is None:
        merge_out = _merge_bwd(*merge_args)
    else:
        g_mlp = jnp.concatenate([d_w_mlp_in, d_w_mlp_out.reshape(N_CHIPS, FF_SHARD, D_MODEL)], axis=1)
        merge_out, (sib_mlp,) = _merge_bwd(*merge_args, rider=_Pair([g_mlp]))
        pf_mlp, pb_mlp = _pair_sum(comm["place"], g_mlp, sib_mlp, "pair_sum_mlp")
    (dys, dxs_m, dz, dy5, dgates, mg, ya, yb, dpa, dpb, gel, dpre, d_dssd, d_gssd, d_glu_b) = merge_out
    d_w_out = _wgrad(mg, dx1, "wgrad_out")
    d_w_branch = jnp.concatenate([_wgrad(ya, dpa, "wgrad_branch_a"), _wgrad(yb, dpb, "wgrad_branch_b")], axis=0)
    d_glu_w = _wgrad(gel, dpre, "wgrad_glu")
    s5_args = (u5, dy5, wb4, wc4, ab, s5d, s5_states)
    if comm is None:
        du5, dwb4, dwc4, dab, d_s5d = _s5_bwd(*s5_args)
        mlp_total = None
    else:
        (du5, dwb4, dwc4, dab, d_s5d), (got_mlp,) = _s5_bwd(*s5_args, rider=_Chip([pb_mlp]))
        (mlp_total,) = _chip_sum(comm["place"], pf_mlp, got_mlp, "chip_sum_mlp")
    dbb_re, dbb_im, d_c_re, d_c_im = _s5_block_grads(dwb4, dwc4)
    d_a_re, d_a_im, d_log_dt, d_b_re, d_b_im = _s5_disc_bwd(
        a_re, a_im, log_dt, b_re, b_im, dab[0].reshape(S5_STATES, 1), dab[1].reshape(S5_STATES, 1),
        dbb_re.reshape(S5_STATES, 16), dbb_im.reshape(S5_STATES, 16))
    dxs_s, dB, dC, ddt, d_alog = _ssd_bwd(xbc_act, dt, alog, ssd_states, dys)
    dxbc_raw, ddt_raw, d_conv_w, d_conv_b, d_dt_bias = _conv_bwd(
        xbc_raw, dt_raw, dxs_m, dxs_s, dB, dC, ddt, p["conv_w"], conv_b, dt_bias)
    d_w_in = dict(z=_wgrad(h, dz, "wgrad_in_z"), xbc=_wgrad(h, dxbc_raw, "wgrad_in_xbc"),
                  dt=_wgrad(h, ddt_raw, "wgrad_in_dt")[:, :16], u5=_wgrad(h, du5, "wgrad_in_u5"),
                  gates=_wgrad(h, dgates, "wgrad_in_gates"))
    w_in_pieces = [(c0, d_w_in[n]) for n, c0, _ in W_IN_PIECES]
    inproj_args = (x, dx1, dz, dxbc_raw, du5, dgates, ddt_raw, g_mix, wp)
    if comm is None:
        dx, d_gmix = _inproj_bwd(*inproj_args)
        late_totals = None
    else:
        g_b, g_in = _late_buffers(d_w_out, d_w_branch, d_glu_w, d_conv_w[:CONV_K], w_in_pieces)
        sib_b, sib_in = _exchange(_Pair([g_b, g_in]), "pair_exchange")
        pf_b, pb_b = _pair_sum(comm["place"], g_b, sib_b, "pair_sum_b")
        pf_in, pb_in = _pair_sum(comm["place"], g_in, sib_in, "pair_sum_in")
        (dx, d_gmix), (got_b, got_in) = _inproj_bwd(*inproj_args, rider=_Chip([pb_b, pb_in]))
        late_totals = (_chip_sum(comm["place"], pf_b, got_b, "chip_sum_b")[0],
                       _chip_sum(comm["place"], pf_in, got_in, "chip_sum_in")[0])

    grads = dict(
        norm_mix_g=d_gmix.reshape(-1), w_in_pieces=w_in_pieces, late_totals=late_totals,
        conv_w=d_conv_w[:CONV_K], conv_b=d_conv_b.reshape(-1),
        dt_bias=d_dt_bias[0, :16], a_log=d_alog[0, :16], d_ssd=d_dssd[0, :16], ssd_norm_g=d_gssd.reshape(-1),
        s5_a_re=d_a_re.reshape(32, 64), s5_a_im=d_a_im.reshape(32, 64), s5_log_dt=d_log_dt.reshape(32),
        s5_b_re=d_b_re.reshape(32, 64, 16), s5_b_im=d_b_im.reshape(32, 64, 16), s5_c_re=d_c_re, s5_c_im=d_c_im,
        s5_d=d_s5d.reshape(-1), s5_glu_w=d_glu_w, s5_glu_b=d_glu_b.reshape(-1), w_branch=d_w_branch, w_out=d_w_out,
        norm_mlp_g=d_gmlp.reshape(-1), w_mlp_in=d_w_mlp_in, w_mlp_out=d_w_mlp_out, norm_final_g=d_gfin.reshape(-1),
        mlp_total=mlp_total)
    return jnp.sum(loss_lanes), dx, grads


MESH = pl.DeviceIdType.MESH
N_CHIPS = 4


def _place():
    x, y, c = lax.axis_index("x"), lax.axis_index("y"), lax.axis_index("c")
    chips = [(1 - x, y), (x, 1 - y), (1 - x, 1 - y)]
    return x, y, c, chips


def _remote(src, dst, send_sems, recv_sems, k, to):
    return pltpu.make_async_remote_copy(src_ref=src, dst_ref=dst, send_sem=send_sems.at[k], recv_sem=recv_sems.at[k],
                                        device_id=to, device_id_type=MESH)


def _row_chunks(rows, k, align):
    step = rows // k
    assert rows % k == 0 and step % align == 0, (rows, k, align)
    return [(i * step, step) for i in range(k)]


ICI_CHUNKS = 4
D2D_CHUNKS = 24


class _Gather:
    def __init__(self, srcs, ks):
        self.inputs = list(srcs)
        self.out_shapes = [jax.ShapeDtypeStruct((N_CHIPS,) + a.shape, a.dtype) for a in srcs]
        self.halves = [a.shape[0] // 2 for a in srcs]
        self.pieces = [_row_chunks(h, k, 32 // a.dtype.itemsize) for a, h, k in zip(srcs, self.halves, ks)]
        self.n_ici = 3 * sum(ks)
        self.n_sems = 2 * self.n_ici + len(srcs)

    def _plan(self, src_refs, out_refs, send_sems, recv_sems):
        x, y, c, chips = _place()
        own = 2 * x + y
        sib = (x, y, 1 - c)
        first, fwd_plan, k = [], [], 0
        for a, (src_ref, out_ref) in enumerate(zip(src_refs, out_refs)):
            h = self.halves[a]
            for r0, nr in self.pieces[a]:
                for cx, cy in chips:
                    first.append(_remote(src_ref.at[pl.ds(c * h + r0, nr), :], out_ref.at[own, pl.ds(c * h + r0, nr), :],
                                         send_sems, recv_sems, k, (cx, cy, c)))
                    fwd_plan.append((out_ref, 2 * cx + cy, h, r0, nr, k, (cx, cy, c)))
                    k += 1
        for a, (src_ref, out_ref) in enumerate(zip(src_refs, out_refs)):
            first.append(_remote(src_ref, out_ref.at[own], send_sems, recv_sems, 2 * self.n_ici + a, sib))
        return first, fwd_plan, c, sib

    def issue(self, src_refs, out_refs, send_sems, recv_sems):
        for cp in self._plan(src_refs, out_refs, send_sems, recv_sems)[0]:
            cp.start()

    def complete(self, src_refs, out_refs, send_sems, recv_sems):
        first, fwd_plan, c, sib = self._plan(src_refs, out_refs, send_sems, recv_sems)
        passed = []
        for out_ref, s, h, r0, nr, k, frm in fwd_plan:
            got = out_ref.at[s, pl.ds(c * h + r0, nr), :]
            _remote(got, got, send_sems, recv_sems, k, frm).wait_recv()
            fw = _remote(got, got, send_sems, recv_sems, self.n_ici + k, sib)
            fw.start()
            passed.append(fw)
        for out_ref, s, h, r0, nr, k, frm in fwd_plan:
            got = out_ref.at[s, pl.ds((1 - c) * h + r0, nr), :]
            _remote(got, got, send_sems, recv_sems, self.n_ici + k, sib).wait_recv()
        own_copies = first[self.n_ici:]
        for cp in own_copies:
            cp.wait_recv()
        for cp in first + passed:
            cp.wait_send()


def _exchange(rider, name):
    ri, ro = len(rider.inputs), len(rider.out_shapes)

    def body(*refs):
        rider.issue(refs[:ri], refs[ri:ri + ro], *refs[ri + ro:])
        rider.complete(refs[:ri], refs[ri:ri + ro], *refs[ri + ro:])

    return _pc(
        body, name=name, in_specs=[_hbm_spec()] * ri, out_specs=[_hbm_spec()] * ro, out_shape=list(rider.out_shapes),
        scratch_shapes=[pltpu.SemaphoreType.DMA((rider.n_sems,))] * 2,
    )(*rider.inputs)


def _call(body, rider=None, **kw):
    if rider is None:
        return _pc(body, **kw)
    single = not isinstance(kw["out_shape"], (list, tuple))
    out_specs = [kw["out_specs"]] if single else list(kw["out_specs"])
    out_shape = [kw["out_shape"]] if single else list(kw["out_shape"])
    scratch = list(kw.get("scratch_shapes", ()))
    n_in, n_out, n_scr = len(kw["in_specs"]), len(out_specs), len(scratch)
    ri, ro = len(rider.inputs), len(rider.out_shapes)
    steps = kw["grid"][0]

    def wrapped(*refs):
        o0 = n_in + ri
        s0 = o0 + n_out + ro
        r_in, r_out, sems = refs[n_in:o0], refs[o0 + n_out:s0], refs[s0 + n_scr:]

        @pl.when(pl.program_id(0) == 0)
        def _():
            rider.issue(r_in, r_out, *sems)

        body(*refs[:n_in], *refs[o0:o0 + n_out], *refs[s0:s0 + n_scr])

        @pl.when(pl.program_id(0) == steps - 1)
        def _():
            rider.complete(r_in, r_out, *sems)

    f = _pc(wrapped, name=kw["name"], grid=kw["grid"], in_specs=list(kw["in_specs"]) + [_hbm_spec()] * ri,
            out_specs=out_specs + [_hbm_spec()] * ro, out_shape=out_shape + list(rider.out_shapes),
            scratch_shapes=scratch + [pltpu.SemaphoreType.DMA((rider.n_sems,))] * 2, compiler_params=kw["compiler_params"])

    def run(*args):
        res = f(*args, *rider.inputs)
        return (res[0] if single else res[:n_out]), res[n_out:]

    return run


def _d2d_pieces(rows):
    k = next(k for k in range(24, 0, -1) if rows % k == 0 and (rows // k) % 8 == 0)
    return _row_chunks(rows, k, 8)


class _Pair:
    def __init__(self, gs, small=None):
        self.n = len(gs)
        self.halves = [g.shape[1] // 2 for g in gs]
        self.inputs = list(gs) + ([small] if small is not None else [])
        self.out_shapes = [jax.ShapeDtypeStruct((N_CHIPS, h, g.shape[2]), F32) for g, h in zip(gs, self.halves)]
        if small is not None:
            self.out_shapes.append(jax.ShapeDtypeStruct(small.shape, F32))
        self.n_sems = len(self.inputs)

    def issue(self, in_refs, out_refs, send_sems, recv_sems):
        x, y, c, _ = _place()
        sib = (x, y, 1 - c)
        for a in range(self.n):
            h = self.halves[a]
            for s in range(N_CHIPS):
                for r0, nr in _d2d_pieces(h):
                    _remote(in_refs[a].at[s, pl.ds((1 - c) * h + r0, nr), :], out_refs[a].at[s, pl.ds(r0, nr), :],
                            send_sems, recv_sems, a, sib).start()
        for a in range(self.n, len(self.inputs)):
            _remote(in_refs[a], out_refs[a], send_sems, recv_sems, a, sib).start()

    def complete(self, in_refs, out_refs, send_sems, recv_sems):
        x, y, c, _ = _place()
        for a in range(len(self.inputs)):
            _remote(out_refs[a], out_refs[a], send_sems, recv_sems, a, (x, y, 1 - c)).wait()


SUM_BLOCKS = 4


def _pair_sum(place, g, sib, name, small=None, sib_small=None):
    n, R, C = g.shape
    H = R // 2
    rb = H // SUM_BLOCKS
    assert H % SUM_BLOCKS == 0 and rb % 16 == 0

    def body(place_ref, a_ref, b_ref, *rest):
        if small is None:
            pf_ref, pb_ref = rest
        else:
            s_ref, t_ref, pf_ref, pb_ref, ps_ref = rest

            @pl.when((pl.program_id(0) == 0) & (pl.program_id(1) == 0))
            def _():
                ps_ref[...] = s_ref[...] + t_ref[...]

        p = a_ref[...] + b_ref[...]
        pf_ref[...] = p
        pb_ref[...] = p.astype(BF16)

    blk = pl.BlockSpec((1, rb, C), lambda s, i, pr: (s, i, 0))
    mine = pl.BlockSpec((1, rb, C), lambda s, i, pr: (s, pr[1] * SUM_BLOCKS + i, 0))
    ins, outs, shapes, args = [mine, blk], [blk, blk], [jax.ShapeDtypeStruct((n, H, C), F32),
                                                        jax.ShapeDtypeStruct((n, H, C), BF16)], [g, sib]
    if small is not None:
        sm = pl.BlockSpec(small.shape, lambda s, i, pr: (0, 0))
        ins += [sm, sm]
        outs += [sm]
        shapes += [jax.ShapeDtypeStruct(small.shape, F32)]
        args += [small, sib_small]
    return _pc(
        body, name=name, out_shape=shapes,
        grid_spec=pltpu.PrefetchScalarGridSpec(num_scalar_prefetch=1, grid=(n, SUM_BLOCKS), in_specs=ins, out_specs=outs),
        compiler_params=_cparams(("arbitrary", "arbitrary")),
    )(place, *args)


class _Chip:
    def __init__(self, pbs, psmall=None):
        self.n = len(pbs)
        self.rows = [pb.shape[1] for pb in pbs]
        self.inputs = list(pbs) + ([psmall] if psmall is not None else [])
        self.out_shapes = [jax.ShapeDtypeStruct((3,) + pb.shape[1:], BF16) for pb in pbs]
        if psmall is not None:
            self.out_shapes.append(jax.ShapeDtypeStruct((N_CHIPS,) + psmall.shape, F32))
        self.n_sems = 3 * len(self.inputs)

    def issue(self, in_refs, out_refs, send_sems, recv_sems):
        x, y, c, chips = _place()
        own = 2 * x + y
        for j, (cx, cy) in enumerate(chips):
            for a in range(self.n):
                for r0, nr in _row_chunks(self.rows[a], ICI_CHUNKS, 16):
                    _remote(in_refs[a].at[2 * cx + cy, pl.ds(r0, nr), :], out_refs[a].at[j, pl.ds(r0, nr), :],
                            send_sems, recv_sems, 3 * a + j, (cx, cy, c)).start()
            for a in range(self.n, len(self.inputs)):
                _remote(in_refs[a], out_refs[a].at[own], send_sems, recv_sems, 3 * a + j, (cx, cy, c)).start()

    def complete(self, in_refs, out_refs, send_sems, recv_sems):
        x, y, c, chips = _place()
        own = 2 * x + y
        for j, (cx, cy) in enumerate(chips):
            for a in range(self.n):
                _remote(in_refs[a].at[own], out_refs[a].at[j], send_sems, recv_sems, 3 * a + j, (cx, cy, c)).wait()
            for a in range(self.n, len(self.inputs)):
                _remote(in_refs[a], out_refs[a].at[2 * cx + cy], send_sems, recv_sems, 3 * a + j, (cx, cy, c)).wait()


def _chip_sum(place, pf, got, name, small4=None, psmall=None):
    _, H, C = pf.shape
    rb = H // SUM_BLOCKS

    def body(place_ref, o_ref, g_ref, *rest):
        if small4 is None:
            (tot_ref,) = rest
        else:
            s_ref, p_ref, tot_ref, st_ref = rest

            @pl.when(pl.program_id(0) == 0)
            def _():
                terms = [jnp.where(place_ref[0] == s, p_ref[...], s_ref[s]) for s in range(N_CHIPS)]
                st_ref[...] = ((terms[0] + terms[1]) + terms[2]) + terms[3]

        tot_ref[...] = ((o_ref[0] + g_ref[0].astype(F32)) + g_ref[1].astype(F32)) + g_ref[2].astype(F32)

    ins = [pl.BlockSpec((1, rb, C), lambda i, pr: (pr[0], i, 0)), pl.BlockSpec((3, rb, C), lambda i, pr: (0, i, 0))]
    outs = [pl.BlockSpec((rb, C), lambda i, pr: (pr[1] * SUM_BLOCKS + i, 0))]
    shapes = [jax.ShapeDtypeStruct((2 * H, C), F32)]
    args = [pf, got]
    if small4 is not None:
        ins += [pl.BlockSpec(small4.shape, lambda i, pr: (0, 0, 0)), pl.BlockSpec(psmall.shape, lambda i, pr: (0, 0))]
        outs += [pl.BlockSpec(psmall.shape, lambda i, pr: (0, 0))]
        shapes += [jax.ShapeDtypeStruct(psmall.shape, F32)]
        args += [small4, psmall]
    return _pc(
        body, name=name, out_shape=shapes,
        grid_spec=pltpu.PrefetchScalarGridSpec(num_scalar_prefetch=1, grid=(SUM_BLOCKS,), in_specs=ins, out_specs=outs),
        compiler_params=_cparams(("arbitrary",)),
    )(place, *args)


def _half_exchange(fulls):
    n = len(fulls)

    def body(*refs):
        in_refs, out_refs = refs[:n], refs[n:2 * n]
        send_sems, recv_sems = refs[2 * n:]
        x, y, c, _ = _place()
        sib = (x, y, 1 - c)
        for a in range(n):
            h = fulls[a].shape[0] // 2
            for r0, nr in _d2d_pieces(h):
                rows = pl.ds(c * h + r0, nr)
                _remote(in_refs[a].at[rows, :], out_refs[a].at[rows, :], send_sems, recv_sems, a, sib).start()
        for a in range(n):
            h = fulls[a].shape[0] // 2
            _remote(in_refs[a].at[pl.ds(c * h, h), :], out_refs[a].at[pl.ds((1 - c) * h, h), :], send_sems, recv_sems, a,
                    sib).wait()

    return _pc(
        body, name="half_exchange", in_specs=[_hbm_spec()] * n, out_specs=[_hbm_spec()] * n,
        out_shape=[jax.ShapeDtypeStruct(f.shape, F32) for f in fulls],
        input_output_aliases={a: a for a in range(n)},
        scratch_shapes=[pltpu.SemaphoreType.DMA((n,)), pltpu.SemaphoreType.DMA((n,))],
    )(*fulls)


def _small_allreduce(pack):
    R, C = pack.shape

    def body(p_ref, o_ref, sib_ref, pair_ref, slots_ref, send_sems, recv_sems):
        x, y, c, chips = _place()
        own = 2 * x + y
        cp = _remote(p_ref, sib_ref, send_sems, recv_sems, 0, (x, y, 1 - c))
        cp.start()
        cp.wait()
        pair_ref[...] = p_ref[...] + sib_ref[...]
        slots_ref[own] = pair_ref[...]
        out = [_remote(pair_ref, slots_ref.at[own], send_sems, recv_sems, 1 + j, (cx, cy, c)) for j, (cx, cy) in enumerate(chips)]
        for cp in out:
            cp.start()
        for j, (cx, cy) in enumerate(chips):
            _remote(pair_ref, slots_ref.at[2 * cx + cy], send_sems, recv_sems, 1 + j, (cx, cy, c)).wait()
        o_ref[...] = ((slots_ref[0] + slots_ref[1]) + slots_ref[2]) + slots_ref[3]

    vmem = pl.BlockSpec(memory_space=pltpu.VMEM)
    return _pc(
        body, name="small_allreduce", in_specs=[vmem], out_specs=vmem, out_shape=jax.ShapeDtypeStruct((R, C), F32),
        scratch_shapes=[pltpu.VMEM((R, C), F32), pltpu.VMEM((R, C), F32), pltpu.VMEM((N_CHIPS, R, C), F32),
                        pltpu.SemaphoreType.DMA((4,)), pltpu.SemaphoreType.DMA((4,))],
    )(pack)


def _adamw(w, g, m, v, name, g_row0=0, with_grad=False, col_block=None):
    R, C = w.shape
    rb = 256 if R % 256 == 0 else (128 if R % 128 == 0 else R)
    if col_block:
        rb = R
    assert g_row0 % rb == 0

    def body(w_ref, g_ref, m_ref, v_ref, d_ref, nm_ref, nv_ref, *g_out):
        gv = g_ref[...]
        m2 = ADAM_B1 * m_ref[...] + (1.0 - ADAM_B1) * gv
        v2 = ADAM_B2 * v_ref[...] + (1.0 - ADAM_B2) * (gv * gv)
        m_hat = m2 / (1.0 - ADAM_B1 ** ADAM_STEP)
        v_hat = v2 / (1.0 - ADAM_B2 ** ADAM_STEP)
        d_ref[...] = -ADAM_LR * (m_hat / (jnp.sqrt(v_hat) + ADAM_EPS) + ADAM_WD * w_ref[...])
        nm_ref[...] = m2
        nv_ref[...] = v2
        if with_grad:
            g_out[0][...] = gv

    if col_block:
        spec = g_spec = pl.BlockSpec((R, col_block), lambda i: (0, i))
        steps = C // col_block
    else:
        spec = pl.BlockSpec((rb, C), lambda i: (i, 0))
        g_spec = pl.BlockSpec((rb, C), lambda i: (g_row0 // rb + i, 0))
        steps = R // rb
    n_out = 4 if with_grad else 3
    return _pc(
        body, name=name, grid=(steps,), in_specs=[spec, g_spec, spec, spec], out_specs=[spec] * n_out,
        out_shape=[jax.ShapeDtypeStruct((R, C), F32)] * n_out, compiler_params=_cparams(("parallel",)),
    )(w, g, m, v)


PACK_COLS = 1024
ROWS_A = (("w_mlp_in", 0, 1024), ("w_mlp_out", 1024, 1024), ("w_out", 2048, 256), ("w_branch", 2304, 384))
ROWS_A_TOTAL = 2688
ROWS_B = (("w_out", 0, 256), ("w_branch", 256, 384))
ROW_B_GLU, ROW_B_CONV, ROWS_B_TOTAL = 640, 704, 768
W_IN_SHARD = 1412
CONV_PAD_ROWS = 16
SMALL = (("norm_mix_g", (1024,)), ("conv_b", (2048,)), ("dt_bias", (16,)), ("a_log", (16,)), ("d_ssd", (16,)),
         ("ssd_norm_g", (1024,)), ("s5_a_re", (32, 64)), ("s5_a_im", (32, 64)), ("s5_log_dt", (32,)),
         ("s5_b_re", (32, 64, 16)), ("s5_b_im", (32, 64, 16)), ("s5_c_re", (32, 16, 64)), ("s5_c_im", (32, 16, 64)),
         ("s5_d", (512,)), ("s5_glu_b", (512,)), ("norm_mlp_g", (1024,)), ("norm_final_g", (1024,)))
SMALL_ROWS = 144
GLU_ROWS = S5_WIDTH * S5_WIDTH // PACK_COLS
CONV_ROWS = CONV_K * CONV_DIM // PACK_COLS
W_IN_PIECES = (("z", 0, 1024), ("xbc", 1024, 2048), ("dt", OFF_DT, 16), ("u5", OFF_U, 512), ("gates", 3600, 2048))


def _pack_small(parts):
    flat = jnp.concatenate([a.astype(F32).reshape(-1) for a in parts])
    return jnp.concatenate([flat, jnp.zeros((SMALL_ROWS * PACK_COLS - flat.shape[0],), F32)]).reshape(SMALL_ROWS, PACK_COLS)


def _unpack_small(pack):
    flat, out, r = pack.reshape(-1), {}, 0
    for name, shp in SMALL:
        n = math.prod(shp)
        out[name] = flat[r:r + n].reshape(shp)
        r += n
    return out


def _late_buffers(d_w_out, d_w_branch, d_glu_w, d_conv_w, w_in_pieces):
    conv4 = d_conv_w.reshape(CONV_K, N_CHIPS, 512).transpose(1, 0, 2).reshape(N_CHIPS, CONV_ROWS // N_CHIPS, PACK_COLS)
    g_b = jnp.concatenate(
        [d_w_out.reshape(N_CHIPS, -1, PACK_COLS), d_w_branch.reshape(N_CHIPS, -1, PACK_COLS),
         d_glu_w.reshape(N_CHIPS, GLU_ROWS // N_CHIPS, PACK_COLS),
         jnp.pad(conv4, ((0, 0), (0, ROWS_B_TOTAL - ROW_B_CONV - CONV_ROWS // N_CHIPS), (0, 0)))], axis=1)
    g_in = jnp.stack([jnp.concatenate(_column_range(w_in_pieces, W_IN_SHARD * s, W_IN_SHARD * (s + 1)), axis=1)
                      for s in range(N_CHIPS)])
    return g_b, g_in


def _column_range(pieces, lo, hi):
    out = []
    for c0, a in pieces:
        a0, a1 = max(lo, c0), min(hi, c0 + a.shape[-1])
        if a0 < a1:
            out.append(a[..., a0 - c0:a1 - c0])
    return out


def kernel(x, norm_mix_g, w_in, conv_w, conv_b, dt_bias, a_log, d_ssd, ssd_norm_g, s5_a_re, s5_a_im, s5_log_dt, s5_b_re, s5_b_im, s5_c_re, s5_c_im, s5_d, s5_glu_w, s5_glu_b, w_branch, w_out, norm_mlp_g, w_mlp_in, w_mlp_out, norm_final_g, loss_target, m_norm_mix_g, m_w_in, m_conv_w, m_conv_b, m_dt_bias, m_a_log, m_d_ssd, m_ssd_norm_g, m_s5_a_re, m_s5_a_im, m_s5_log_dt, m_s5_b_re, m_s5_b_im, m_s5_c_re, m_s5_c_im, m_s5_d, m_s5_glu_w, m_s5_glu_b, m_w_branch, m_w_out, m_norm_mlp_g, m_w_mlp_in, m_w_mlp_out, m_norm_final_g, v_norm_mix_g, v_w_in, v_conv_w, v_conv_b, v_dt_bias, v_a_log, v_d_ssd, v_ssd_norm_g, v_s5_a_re, v_s5_a_im, v_s5_log_dt, v_s5_b_re, v_s5_b_im, v_s5_c_re, v_s5_c_im, v_s5_d, v_s5_glu_w, v_s5_glu_b, v_w_branch, v_w_out, v_norm_mlp_g, v_w_mlp_in, v_w_mlp_out, v_norm_final_g):
    names = ("norm_mix_g", "w_in", "conv_w", "conv_b", "dt_bias", "a_log", "d_ssd", "ssd_norm_g", "s5_a_re", "s5_a_im",
             "s5_log_dt", "s5_b_re", "s5_b_im", "s5_c_re", "s5_c_im", "s5_d", "s5_glu_w", "s5_glu_b", "w_branch", "w_out",
             "norm_mlp_g", "w_mlp_in", "w_mlp_out", "norm_final_g")
    w = dict(zip(names, (norm_mix_g, w_in, conv_w, conv_b, dt_bias, a_log, d_ssd, ssd_norm_g, s5_a_re, s5_a_im, s5_log_dt,
                         s5_b_re, s5_b_im, s5_c_re, s5_c_im, s5_d, s5_glu_w, s5_glu_b, w_branch, w_out, norm_mlp_g,
                         w_mlp_in, w_mlp_out, norm_final_g)))
    m = dict(zip(names, (m_norm_mix_g, m_w_in, m_conv_w, m_conv_b, m_dt_bias, m_a_log, m_d_ssd, m_ssd_norm_g, m_s5_a_re,
                         m_s5_a_im, m_s5_log_dt, m_s5_b_re, m_s5_b_im, m_s5_c_re, m_s5_c_im, m_s5_d, m_s5_glu_w,
                         m_s5_glu_b, m_w_branch, m_w_out, m_norm_mlp_g, m_w_mlp_in, m_w_mlp_out, m_norm_final_g)))
    v = dict(zip(names, (v_norm_mix_g, v_w_in, v_conv_w, v_conv_b, v_dt_bias, v_a_log, v_d_ssd, v_ssd_norm_g, v_s5_a_re,
                         v_s5_a_im, v_s5_log_dt, v_s5_b_re, v_s5_b_im, v_s5_c_re, v_s5_c_im, v_s5_d, v_s5_glu_w,
                         v_s5_glu_b, v_w_branch, v_w_out, v_norm_mlp_g, v_w_mlp_in, v_w_mlp_out, v_norm_final_g)))

    cx, cy, cc = lax.axis_index("x"), lax.axis_index("y"), lax.axis_index("c")
    own = 2 * cx + cy
    place = jnp.stack([own, cc]).astype(jnp.int32)

    src_conv = jnp.concatenate([conv_w, jnp.zeros((CONV_PAD_ROWS - CONV_K, 512), F32)], axis=0)
    all_in, all_conv = _exchange(_Gather([w_in.astype(BF16), src_conv], [ICI_CHUNKS, 1]), "gather_first")
    p = {n: w[n] for n, _ in SMALL}
    p["conv_w"] = jnp.concatenate([all_conv[s, :CONV_K] for s in range(N_CHIPS)], axis=1)
    shards = [(W_IN_SHARD * s, all_in[s]) for s in range(N_CHIPS)]
    p["w_in_perm"] = jnp.concatenate(
        _column_range(shards, 0, OFF_DT) + _column_range(shards, OFF_U, D_IN_PROJ) + _column_range(shards, OFF_DT, OFF_U)
        + [jnp.zeros((D_MODEL, DT_PAD - 16), BF16)], axis=1)

    def late_unpack(gathered):
        all_a, all_glu = gathered
        out = {"w_mlp_in": all_a[:, 0:1024], "s5_glu_w": all_glu.reshape(S5_WIDTH, S5_WIDTH)}
        for n, r0, nr in ROWS_A[1:]:
            out[n] = all_a[:, r0:r0 + nr].reshape(N_CHIPS * nr, PACK_COLS)
        return out

    comm = dict(place=place, late_ks=[ICI_CHUNKS, 1], late_unpack=late_unpack,
                late_srcs=[jnp.concatenate([w[n].astype(BF16) for n, _, _ in ROWS_A], axis=0), s5_glu_w.astype(BF16)])
    loss_part, grad_x, g = _local_step(x[0], loss_target[0], p, comm)
    loss = lax.psum(loss_part, ("x", "y", "c"))

    red_mlp, red_b, red_in = _half_exchange([g["mlp_total"], *g["late_totals"]])
    small_tot = _small_allreduce(_pack_small([g[n] for n, _ in SMALL]))

    grads = _unpack_small(small_tot)
    delta, new_m, new_v = {}, {}, {}
    for n, r0, _ in ROWS_A[:2]:
        delta[n], new_m[n], new_v[n], grads[n] = _adamw(w[n], red_mlp, m[n], v[n], "adamw_" + n, g_row0=r0, with_grad=True)
    for n, r0, _ in ROWS_B:
        delta[n], new_m[n], new_v[n], grads[n] = _adamw(w[n], red_b, m[n], v[n], "adamw_" + n, g_row0=r0, with_grad=True)
    d_t, m_t, v_t, g_t = _adamw(w_in.T, red_in.T, m_w_in.T, v_w_in.T, "adamw_w_in", with_grad=True, col_block=128)
    delta["w_in"], new_m["w_in"], new_v["w_in"], grads["w_in"] = d_t.T, m_t.T, v_t.T, g_t.T
    grads["s5_glu_w"] = red_b[ROW_B_GLU:ROW_B_GLU + GLU_ROWS // N_CHIPS].reshape(S5_WIDTH // N_CHIPS, S5_WIDTH)
    grads["conv_w"] = red_b[ROW_B_CONV:ROW_B_CONV + CONV_ROWS // N_CHIPS].reshape(CONV_K, CONV_DIM // N_CHIPS)
    for n in ("s5_glu_w", "conv_w"):
        delta[n], new_m[n], new_v[n] = _adamw(w[n], grads[n], m[n], v[n], "adamw_" + n)
    ds, ms, vs = _adamw(_pack_small([w[n] for n, _ in SMALL]), small_tot, _pack_small([m[n] for n, _ in SMALL]),
                        _pack_small([v[n] for n, _ in SMALL]), "adamw_small")
    delta.update(_unpack_small(ds))
    new_m.update(_unpack_small(ms))
    new_v.update(_unpack_small(vs))

    return (loss, grad_x[None], *[grads[n] for n in names], *[delta[n] for n in names],
            *[new_m[n] for n in names], *[new_v[n] for n in names])
```

```python
import functools
import math

import jax
import jax.numpy as jnp
from jax import lax
from jax.experimental import pallas as pl
from jax.experimental.pallas import tpu as pltpu

F32 = jnp.float32
BF16 = jnp.bfloat16

D_MODEL = 1024
SSD_INNER = 1024
SSD_HEADS = 16
SSD_HEADDIM = 64
SSD_GROUPS = 4
SSD_HPG = 4
SSD_STATE = 128
SSD_CHUNK = 128
CONV_K = 4
CONV_DIM = 2048
S5_WIDTH = 512
S5_STATES = 2048
S5_BLOCKS = 4
S5_CHUNK = 128
D_FF = 4096
FF_SHARDS = 4
FF_SHARD = D_FF // FF_SHARDS
EPS = 1e-6
P_Z, P_XBC, P_U5, P_G, P_DT, P_END = 0, 1024, 3072, 3584, 5632, 5760
DT_PAD = 128
OFF_DT, OFF_U = 3072, 3088
D_IN_PROJ = 5648

ADAM_LR, ADAM_B1, ADAM_B2, ADAM_EPS, ADAM_WD, ADAM_STEP = 0.001, 0.9, 0.999, 1e-08, 0.01, 10

TOKEN_TILE = 256
VMEM_LIMIT = 56 * 1024 * 1024
HALO = 8
CONV_COLS = 256
CONV_ROWS_BLK = 64
WGRAD_TOKENS = 2048


def _pc(body, **kw):
    return pl.pallas_call(body, **kw)


def _cparams(sem=None):
    return pltpu.CompilerParams(dimension_semantics=sem, vmem_limit_bytes=VMEM_LIMIT)


def _dot(a, b):
    return jnp.dot(a, b, preferred_element_type=F32)


def _dot_nt(a, b):
    return lax.dot_general(a, b, (((1,), (1,)), ((), ())), preferred_element_type=F32)


def _dot_tn(a, b):
    return lax.dot_general(a, b, (((0,), (0,)), ((), ())), preferred_element_type=F32)


def _dot_hi(a, b, dims=(((1,), (0,)), ((), ()))):
    return lax.dot_general(a, b, dims, preferred_element_type=F32, precision=lax.Precision.HIGHEST)


def _split_bf16(x, terms):
    out = []
    for _ in range(terms - 1):
        t = x.astype(BF16)
        out.append(t)
        x = x - t.astype(F32)
    out.append(x.astype(BF16))
    return out


def _dot_split(x, onehots, terms, dims=(((1,), (0,)), ((), ()))):
    acc = None
    for t in _split_bf16(x, terms):
        p = lax.dot_general(t, onehots, dims, preferred_element_type=F32)
        acc = p if acc is None else acc + p
    return acc


def _dot_split_rhs(onehots, x, terms, dims=(((1,), (0,)), ((), ()))):
    acc = None
    for t in _split_bf16(x, terms):
        p = lax.dot_general(onehots, t, dims, preferred_element_type=F32)
        acc = p if acc is None else acc + p
    return acc


def _sigmoid(x):
    return 0.5 * jnp.tanh(0.5 * x) + 0.5


def _softplus(x):
    return jnp.maximum(x, 0.0) + jnp.log(1.0 + jnp.exp(-jnp.abs(x)))


_GELU_C = math.sqrt(2.0 / math.pi)


def _gelu(x):
    return 0.5 * x * (1.0 + jnp.tanh(_GELU_C * (x + 0.044715 * x * x * x)))


def _gelu_grad(x):
    t = jnp.tanh(_GELU_C * (x + 0.044715 * x * x * x))
    return 0.5 * (1.0 + t) + 0.5 * x * (1.0 - t * t) * _GELU_C * (1.0 + 3.0 * 0.044715 * x * x)


def _rms(x):
    r = lax.rsqrt(jnp.mean(x * x, axis=-1, keepdims=True) + EPS)
    return x * r, r


def _rms_bwd(xn, r, dxn):
    return r * (dxn - xn * jnp.mean(dxn * xn, axis=-1, keepdims=True))


def _row_spec(tm, width, col=0):
    return pl.BlockSpec((tm, width), lambda i: (i, col))


def _const_spec(shape):
    nd = len(shape)
    return pl.BlockSpec(shape, lambda i: (0,) * nd)


def _hbm_spec():
    return pl.BlockSpec(memory_space=pl.ANY)


def _inproj_fwd(x, g, wp):
    T = x.shape[0]
    tm = TOKEN_TILE

    def body(x_ref, g_ref, w_hbm, z_ref, xbc_ref, u5_ref, gt_ref, dt_ref, h_ref, w_ref):
        @pl.when(pl.program_id(0) == 0)
        def _():
            pltpu.sync_copy(w_hbm, w_ref)

        xn, _ = _rms(x_ref[...])
        h = (xn * g_ref[...]).astype(BF16)
        h_ref[...] = h
        z_ref[...] = _dot(h, w_ref[:, P_Z:P_XBC])
        xbc_ref[...] = _dot(h, w_ref[:, P_XBC:P_U5])
        u5_ref[...] = _dot(h, w_ref[:, P_U5:P_G])
        gt_ref[...] = _dot(h, w_ref[:, P_G:P_DT])
        dt_ref[...] = _dot(h, w_ref[:, P_DT:P_END])

    widths = (1024, 2048, 512, 2048, DT_PAD)
    return _pc(
        body, name="inproj_fwd", grid=(T // tm,),
        in_specs=[_row_spec(tm, D_MODEL), _const_spec((1, D_MODEL)), _hbm_spec()],
        out_specs=[_row_spec(tm, w) for w in widths] + [_row_spec(tm, D_MODEL)],
        out_shape=[jax.ShapeDtypeStruct((T, w), F32) for w in widths] + [jax.ShapeDtypeStruct((T, D_MODEL), BF16)],
        scratch_shapes=[pltpu.VMEM((D_MODEL, P_END), BF16)],
        compiler_params=_cparams(("arbitrary",)),
    )(x, g, wp)


def _inproj_bwd(x, dx1, dz, dxbc, du5, dgt, ddt, g, wp, rider=None):
    T = x.shape[0]
    tm = TOKEN_TILE

    def body(x_ref, dx1_ref, dz_ref, dxbc_ref, du5_ref, dgt_ref, ddt_ref, g_ref, w_hbm, dx_ref, dg_ref, w_ref):
        @pl.when(pl.program_id(0) == 0)
        def _():
            pltpu.sync_copy(w_hbm, w_ref)
            dg_ref[...] = jnp.zeros_like(dg_ref)

        xn, r = _rms(x_ref[...])
        gv = g_ref[...]
        dh = _dot_nt(dz_ref[...].astype(BF16), w_ref[:, P_Z:P_XBC])
        dh += _dot_nt(dxbc_ref[...].astype(BF16), w_ref[:, P_XBC:P_U5])
        dh += _dot_nt(du5_ref[...].astype(BF16), w_ref[:, P_U5:P_G])
        dh += _dot_nt(dgt_ref[...].astype(BF16), w_ref[:, P_G:P_DT])
        dh += _dot_nt(ddt_ref[...].astype(BF16), w_ref[:, P_DT:P_END])
        dg_ref[...] += jnp.sum(dh * xn, axis=0, keepdims=True)
        dx_ref[...] = dx1_ref[...] + _rms_bwd(xn, r, dh * gv)

    return _call(
        body, rider, name="inproj_bwd", grid=(T // tm,),
        in_specs=[_row_spec(tm, 1024), _row_spec(tm, 1024), _row_spec(tm, 1024), _row_spec(tm, 2048),
                  _row_spec(tm, 512), _row_spec(tm, 2048), _row_spec(tm, DT_PAD), _const_spec((1, 1024)), _hbm_spec()],
        out_specs=[_row_spec(tm, 1024), _const_spec((1, 1024))],
        out_shape=[jax.ShapeDtypeStruct((T, 1024), F32), jax.ShapeDtypeStruct((1, 1024), F32)],
        scratch_shapes=[pltpu.VMEM((D_MODEL, P_END), BF16)],
        compiler_params=_cparams(("arbitrary",)),
    )(x, dx1, dz, dxbc, du5, dgt, ddt, g, wp)


def _conv_fwd(xbc_raw, dt_raw, conv_w, conv_b, dt_bias):
    T = xbc_raw.shape[0]
    tm = TOKEN_TILE

    def body(u_ref, dtr_ref, w_ref, b_ref, db_ref, act_ref, dt_ref, ext_ref):
        @pl.when(pl.program_id(0) == 0)
        def _():
            ext_ref[0:HALO, :] = jnp.zeros((HALO, CONV_DIM), F32)

        ext_ref[HALO:, :] = u_ref[...]
        for c0 in range(0, CONV_DIM, CONV_COLS):
            cols = slice(c0, c0 + CONV_COLS)
            taps = [w_ref[k:k + 1, cols] for k in range(CONV_K)]
            bias = b_ref[:, cols]
            for r0 in range(0, tm, CONV_ROWS_BLK):
                y = bias + taps[0] * ext_ref[pl.ds(HALO - (CONV_K - 1) + r0, CONV_ROWS_BLK), cols]
                for k in range(1, CONV_K):
                    y += taps[k] * ext_ref[pl.ds(HALO - (CONV_K - 1) + k + r0, CONV_ROWS_BLK), cols]
                act_ref[r0:r0 + CONV_ROWS_BLK, cols] = y * _sigmoid(y)
        ext_ref[0:HALO, :] = u_ref[tm - HALO:tm, :]
        dt_ref[...] = _softplus(dtr_ref[...] + db_ref[...])

    return _pc(
        body, name="conv_fwd", grid=(T // tm,),
        in_specs=[_row_spec(tm, CONV_DIM), _row_spec(tm, DT_PAD), _const_spec((CONV_K, CONV_DIM)),
                  _const_spec((1, CONV_DIM)), _const_spec((1, DT_PAD))],
        out_specs=[_row_spec(tm, CONV_DIM), _row_spec(tm, DT_PAD)],
        out_shape=[jax.ShapeDtypeStruct((T, CONV_DIM), F32), jax.ShapeDtypeStruct((T, DT_PAD), F32)],
        scratch_shapes=[pltpu.VMEM((tm + HALO, CONV_DIM), F32)],
        compiler_params=_cparams(("arbitrary",)),
    )(xbc_raw, dt_raw, conv_w, conv_b, dt_bias)


def _conv_bwd(xbc_raw, dt_raw, dxs_a, dxs_b, dB, dC, ddt, conv_w, conv_b, dt_bias):
    T = xbc_raw.shape[0]
    tm = TOKEN_TILE
    n = T // tm
    hb = tm // HALO

    def rev(width):
        return pl.BlockSpec((tm, width), lambda i: (n - 1 - i, 0))

    def body(u_ref, up_ref, dtr_ref, dxa_ref, dxb_ref, dB_ref, dC_ref, ddt_ref, w_ref, b_ref, db_ref,
             du_ref, ddtr_ref, dw_ref, dcb_ref, ddb_ref, ext_ref, dye_ref):
        i = pl.program_id(0)

        @pl.when(i == 0)
        def _():
            dye_ref[tm:, :] = jnp.zeros((HALO, CONV_DIM), F32)
            dw_ref[...] = jnp.zeros_like(dw_ref)
            dcb_ref[...] = jnp.zeros_like(dcb_ref)
            ddb_ref[...] = jnp.zeros_like(ddb_ref)

        first = (i == n - 1).astype(F32)
        ext_ref[0:HALO, :] = up_ref[...] * (1.0 - first)
        ext_ref[HALO:, :] = u_ref[...]
        for c0 in range(0, CONV_DIM, CONV_COLS):
            cols = slice(c0, c0 + CONV_COLS)
            taps = [w_ref[k:k + 1, cols] for k in range(CONV_K)]
            bias = b_ref[:, cols]
            acc_b = jnp.zeros((HALO, CONV_COLS), F32)
            acc_w = [jnp.zeros((HALO, CONV_COLS), F32) for _ in range(CONV_K)]
            for r0 in range(0, tm, CONV_ROWS_BLK):
                rows = slice(r0, r0 + CONV_ROWS_BLK)
                us = [ext_ref[pl.ds(HALO - (CONV_K - 1) + k + r0, CONV_ROWS_BLK), cols] for k in range(CONV_K)]
                y = bias + taps[0] * us[0]
                for k in range(1, CONV_K):
                    y += taps[k] * us[k]
                s = _sigmoid(y)
                if c0 < SSD_INNER:
                    dact = dxa_ref[rows, cols] + dxb_ref[rows, cols]
                elif c0 < SSD_INNER + 512:
                    dact = dB_ref[rows, c0 - SSD_INNER:c0 - SSD_INNER + CONV_COLS]
                else:
                    dact = dC_ref[rows, c0 - SSD_INNER - 512:c0 - SSD_INNER - 512 + CONV_COLS]
                dy = dact * (s * (1.0 + y * (1.0 - s)))
                dye_ref[rows, cols] = dy
                acc_b += jnp.sum(dy.reshape(CONV_ROWS_BLK // HALO, HALO, CONV_COLS), axis=0)
                for k in range(CONV_K):
                    acc_w[k] += jnp.sum((dy * us[k]).reshape(CONV_ROWS_BLK // HALO, HALO, CONV_COLS), axis=0)
            dcb_ref[:, cols] += jnp.sum(acc_b, axis=0, keepdims=True)
            for k in range(CONV_K):
                dw_ref[k:k + 1, cols] += jnp.sum(acc_w[k], axis=0, keepdims=True)
        for c0 in range(0, CONV_DIM, CONV_COLS):
            cols = slice(c0, c0 + CONV_COLS)
            taps = [w_ref[k:k + 1, cols] for k in range(CONV_K)]
            for r0 in range(0, tm, CONV_ROWS_BLK):
                du = taps[0] * dye_ref[pl.ds(CONV_K - 1 + r0, CONV_ROWS_BLK), cols]
                for k in range(1, CONV_K):
                    du += taps[k] * dye_ref[pl.ds(CONV_K - 1 - k + r0, CONV_ROWS_BLK), cols]
                du_ref[r0:r0 + CONV_ROWS_BLK, cols] = du
        dye_ref[tm:, :] = dye_ref[0:HALO, :]
        sg = _sigmoid(dtr_ref[...] + db_ref[...])
        ddtr = ddt_ref[...] * sg
        ddtr_ref[...] = ddtr
        ddb_ref[...] += jnp.sum(ddtr, axis=0, keepdims=True)

    prev_spec = pl.BlockSpec((HALO, CONV_DIM), lambda i: (jnp.maximum((n - 1 - i) * hb - 1, 0), 0))
    return _pc(
        body, name="conv_bwd", grid=(n,),
        in_specs=[rev(CONV_DIM), prev_spec, rev(DT_PAD), rev(1024), rev(1024), rev(512), rev(512), rev(DT_PAD),
                  _const_spec((CONV_K, CONV_DIM)), _const_spec((1, CONV_DIM)), _const_spec((1, DT_PAD))],
        out_specs=[rev(CONV_DIM), rev(DT_PAD), _const_spec((HALO, CONV_DIM)), _const_spec((1, CONV_DIM)),
                   _const_spec((1, DT_PAD))],
        out_shape=[jax.ShapeDtypeStruct((T, CONV_DIM), F32), jax.ShapeDtypeStruct((T, DT_PAD), F32),
                   jax.ShapeDtypeStruct((HALO, CONV_DIM), F32), jax.ShapeDtypeStruct((1, CONV_DIM), F32),
                   jax.ShapeDtypeStruct((1, DT_PAD), F32)],
        scratch_shapes=[pltpu.VMEM((tm + HALO, CONV_DIM), F32), pltpu.VMEM((tm + HALO, CONV_DIM), F32)],
        compiler_params=_cparams(("arbitrary",)),
    )(xbc_raw, xbc_raw, dt_raw, dxs_a, dxs_b, dB, dC, ddt, conv_w, conv_b, dt_bias)


GROUP_LANES = SSD_HPG * SSD_HEADDIM


def _ssd_expanders():
    head = jnp.arange(DT_PAD)[:, None]
    to_wide = (jnp.arange(SSD_INNER)[None, :] // SSD_HEADDIM == head).astype(BF16)
    to_cols = (jnp.arange(SSD_HEADS * SSD_CHUNK)[None, :] // SSD_CHUNK == head).astype(BF16)
    return to_wide, to_wide.T, to_cols


def _ssd_prep(dt_ref, alog_ref, wide_ref, cols_ref):
    q = SSD_CHUNK
    a = -jnp.exp(alog_ref[...])
    dtv = dt_ref[...]
    la = dtv * a
    row = lax.broadcasted_iota(jnp.int32, (q, q), 0)
    col = lax.broadcasted_iota(jnp.int32, (q, q), 1)
    tri = (col <= row).astype(BF16)
    cum = _dot_split_rhs(tri, la, 3)
    cum_t = _dot_split(la, tri, 3, (((0,), (1,)), ((), ())))
    dtw = _dot_split(dtv, wide_ref[...], 2)
    cumw = _dot_split(cum, wide_ref[...], 3)
    segcol = _dot_split(cum, cols_ref[...], 3)
    return a, dtv, row, col, tri, cum_t, dtw, cumw, segcol


def _decay(segcol, cum_t, h, keep):
    return jnp.where(keep, jnp.exp(jnp.minimum(segcol[:, 128 * h:128 * h + 128] - cum_t[h:h + 1, :], 0.0)), 0.0)


def _decay_t(segcol, cum_t, h, keep_t):
    return jnp.where(keep_t, jnp.exp(jnp.minimum(cum_t[h:h + 1, :] - segcol[:, 128 * h:128 * h + 128], 0.0)), 0.0)


def _ssd_fwd(xbc_act, dt, alog):
    T = xbc_act.shape[0]
    q = SSD_CHUNK
    nc = T // q
    to_wide, _, to_cols = _ssd_expanders()

    def body(xbc_ref, dt_ref, alog_ref, wide_ref, cols_ref, y_ref, sp_ref, st_ref, xd_ref, xde_ref):
        @pl.when(pl.program_id(0) == 0)
        def _():
            st_ref[...] = jnp.zeros_like(st_ref)

        a, dtv, row, col, tri, cum_t, dtw, cumw, segcol = _ssd_prep(dt_ref, alog_ref, wide_ref, cols_ref)
        clw = cumw[q - 1:q, :]
        ecw = jnp.exp(cumw)
        xd = xbc_ref[:, 0:SSD_INNER] * dtw
        xd_ref[...] = xd.astype(BF16)
        xde_ref[...] = (xd * jnp.exp(clw - cumw)).astype(BF16)
        cdw = jnp.exp(clw)
        keep = col <= row
        sp_ref[0] = st_ref[...]
        for g in range(SSD_GROUPS):
            gl = slice(GROUP_LANES * g, GROUP_LANES * (g + 1))
            bb = xbc_ref[:, 1024 + 128 * g:1152 + 128 * g].astype(BF16)
            cb = xbc_ref[:, 1536 + 128 * g:1664 + 128 * g].astype(BF16)
            gm = _dot_nt(cb, bb)
            stp = st_ref[g]
            yoff = _dot(cb, stp.astype(BF16)) * ecw[:, gl]
            for r in range(SSD_HPG):
                h = SSD_HPG * g + r
                m = (gm * _decay(segcol, cum_t, h, keep)).astype(BF16)
                y_ref[:, 64 * h:64 * h + 64] = _dot(m, xd_ref[:, 64 * h:64 * h + 64]) + yoff[:, 64 * r:64 * r + 64]
            st_ref[g] = stp * cdw[:, gl] + _dot_tn(bb, xde_ref[:, gl])

    return _pc(
        body, name="ssd_fwd", grid=(nc,),
        in_specs=[_row_spec(q, CONV_DIM), _row_spec(q, DT_PAD), _const_spec((1, DT_PAD)),
                  _const_spec(to_wide.shape), _const_spec(to_cols.shape)],
        out_specs=[_row_spec(q, SSD_INNER),
                   pl.BlockSpec((1, SSD_GROUPS, SSD_STATE, GROUP_LANES), lambda i: (i, 0, 0, 0))],
        out_shape=[jax.ShapeDtypeStruct((T, SSD_INNER), F32),
                   jax.ShapeDtypeStruct((nc, SSD_GROUPS, SSD_STATE, GROUP_LANES), F32)],
        scratch_shapes=[pltpu.VMEM((SSD_GROUPS, SSD_STATE, GROUP_LANES), F32), pltpu.VMEM((q, SSD_INNER), BF16),
                        pltpu.VMEM((q, SSD_INNER), BF16)],
        compiler_params=_cparams(("arbitrary",)),
    )(xbc_act, dt, alog, to_wide, to_cols)


def _ssd_bwd(xbc_act, dt, alog, sprev, dy):
    T = xbc_act.shape[0]
    q = SSD_CHUNK
    nc = T // q
    to_wide, to_heads, to_cols = _ssd_expanders()

    def rev(width):
        return pl.BlockSpec((q, width), lambda i: (nc - 1 - i, 0))

    def body(xbc_ref, dt_ref, alog_ref, sp_ref, dy_ref, wide_ref, heads_ref, cols_ref,
             dxs_ref, dB_ref, dC_ref, ddt_ref, dalog_ref, ds_ref, xd_ref, dxd_ref):
        i = pl.program_id(0)

        @pl.when(i == 0)
        def _():
            ds_ref[...] = jnp.zeros_like(ds_ref)
            dalog_ref[...] = jnp.zeros_like(dalog_ref)

        a, dtv, row, col, tri, cum_t, dtw, cumw, segcol = _ssd_prep(dt_ref, alog_ref, wide_ref, cols_ref)
        clw = cumw[q - 1:q, :]
        ecw = jnp.exp(cumw)
        dew = jnp.exp(clw - cumw)
        cdw = jnp.exp(clw)
        xs = xbc_ref[:, 0:SSD_INNER]
        xd = xs * dtw
        xd_ref[...] = xd.astype(BF16)
        dyv = dy_ref[...]
        dye = (dyv * ecw).astype(BF16)
        xde = (xd * dew).astype(BF16)
        keep = col <= row
        keep_t = col >= row
        rows_k = lax.broadcasted_iota(jnp.int32, (SSD_HPG * q, DT_PAD), 0) // q
        lanes_k = lax.broadcasted_iota(jnp.int32, (SSD_HPG * q, DT_PAD), 1)
        dcw_parts = []
        dcum = jnp.zeros((q, DT_PAD), F32)
        for g in range(SSD_GROUPS):
            gl = slice(GROUP_LANES * g, GROUP_LANES * (g + 1))
            bb = xbc_ref[:, 1024 + 128 * g:1152 + 128 * g].astype(BF16)
            cb = xbc_ref[:, 1536 + 128 * g:1664 + 128 * g].astype(BF16)
            gm = _dot_nt(cb, bb)
            gmt = _dot_nt(bb, cb)
            stp = sp_ref[0, g]
            dst = ds_ref[g]
            stpb = stp.astype(BF16)
            dstb = dst.astype(BF16)
            yoff = _dot(cb, stpb) * ecw[:, gl]
            dcg = _dot_nt(dye[:, gl], stpb)
            ds_ref[g] = dst * cdw[:, gl] + _dot_tn(cb, dye[:, gl])
            dlast = jnp.sum(dst * stp, axis=0, keepdims=True) * cdw[:, gl]
            dbg = _dot_nt(xde[:, gl], dstb)
            w = _dot(bb, dstb) * dew[:, gl]
            wx = w * xd[:, gl]
            dlast = dlast + jnp.sum(wx, axis=0, keepdims=True)
            dcw_parts.append(dyv[:, gl] * yoff - wx
                             + jnp.where(lax.broadcasted_iota(jnp.int32, (q, 1), 0) == q - 1, dlast, 0.0))
            dgm = jnp.zeros((q, q), F32)
            diag = []
            for r in range(SSD_HPG):
                h = SSD_HPG * g + r
                hl = slice(64 * h, 64 * h + 64)
                dyb = dy_ref[:, hl].astype(BF16)
                xdh = xd_ref[:, hl]
                dm = _dot_nt(dyb, xdh)
                dmt = _dot_nt(xdh, dyb)
                dec = _decay(segcol, cum_t, h, keep)
                mt = gmt * _decay_t(segcol, cum_t, h, keep_t)
                dgm += dm * dec
                diag.append(dm * (gm * dec) - dmt * mt)
                dxd_ref[:, hl] = _dot(mt.astype(BF16), dyb) + w[:, 64 * r:64 * r + 64]
            onehots = (lanes_k == SSD_HPG * g + rows_k).astype(BF16)
            dcum += _dot_split(jnp.concatenate(diag, axis=1), onehots, 2)
            dgb = dgm.astype(BF16)
            dC_ref[:, 128 * g:128 * g + 128] = dcg + _dot(dgb, bb)
            dB_ref[:, 128 * g:128 * g + 128] = dbg + _dot_tn(dgb, cb)
        dxd = dxd_ref[...]
        dxs_ref[...] = dxd * dtw
        dcum += _dot_split(jnp.concatenate(dcw_parts, axis=1), heads_ref[...], 2)
        dla = _dot_split_rhs(tri, dcum, 3, (((0,), (0,)), ((), ())))
        ddt_ref[...] = _dot_split(xs * dxd, heads_ref[...], 2) + dla * a
        dalog_ref[...] += jnp.sum(dla * dtv, axis=0, keepdims=True)

        @pl.when(i == nc - 1)
        def _():
            dalog_ref[...] = dalog_ref[...] * a

    st_spec = pl.BlockSpec((1, SSD_GROUPS, SSD_STATE, GROUP_LANES), lambda i: (nc - 1 - i, 0, 0, 0))
    return _pc(
        body, name="ssd_bwd", grid=(nc,),
        in_specs=[rev(CONV_DIM), rev(DT_PAD), _const_spec((1, DT_PAD)), st_spec, rev(SSD_INNER),
                  _const_spec(to_wide.shape), _const_spec(to_heads.shape), _const_spec(to_cols.shape)],
        out_specs=[rev(SSD_INNER), rev(512), rev(512), rev(DT_PAD), _const_spec((1, DT_PAD))],
        out_shape=[jax.ShapeDtypeStruct((T, SSD_INNER), F32), jax.ShapeDtypeStruct((T, 512), F32),
                   jax.ShapeDtypeStruct((T, 512), F32), jax.ShapeDtypeStruct((T, DT_PAD), F32),
                   jax.ShapeDtypeStruct((1, DT_PAD), F32)],
        scratch_shapes=[pltpu.VMEM((SSD_GROUPS, SSD_STATE, GROUP_LANES), F32), pltpu.VMEM((q, SSD_INNER), BF16),
                        pltpu.VMEM((q, SSD_INNER), F32)],
        compiler_params=_cparams(("arbitrary",)),
    )(xbc_act, dt, alog, sprev, dy, to_wide, to_heads, to_cols)


def _s5_disc_vals(a_re, a_im, log_dt, b_re, b_im):
    dt = jnp.exp(log_dt)
    mag = jnp.exp(a_re * dt)
    ab_re = mag * jnp.cos(a_im * dt)
    ab_im = mag * jnp.sin(a_im * dt)
    den = a_re * a_re + a_im * a_im
    nr = ab_re - 1.0
    ni = ab_im
    coef_re = (nr * a_re + ni * a_im) / den
    coef_im = (ni * a_re - nr * a_im) / den
    bb_re = coef_re * b_re - coef_im * b_im
    bb_im = coef_re * b_im + coef_im * b_re
    return ab_re, ab_im, bb_re, bb_im


def _s5_disc(a_re, a_im, log_dt, b_re, b_im):
    def body(ar, ai, ld, br, bi, o1, o2, o3, o4):
        o1[...], o2[...], o3[...], o4[...] = _s5_disc_vals(ar[...], ai[...], ld[...], br[...], bi[...])

    return _pc(
        body, name="s5_disc",
        out_shape=[jax.ShapeDtypeStruct((S5_STATES, 1), F32), jax.ShapeDtypeStruct((S5_STATES, 1), F32),
                   jax.ShapeDtypeStruct((S5_STATES, 16), F32), jax.ShapeDtypeStruct((S5_STATES, 16), F32)],
    )(a_re, a_im, log_dt, b_re, b_im)


def _s5_disc_bwd(a_re, a_im, log_dt, b_re, b_im, d_ab_re, d_ab_im, d_bb_re, d_bb_im):
    def body(ar, ai, ld, br, bi, g1, g2, g3, g4, o1, o2, o3, o4, o5):
        _, vjp = jax.vjp(_s5_disc_vals, ar[...], ai[...], ld[...], br[...], bi[...])
        d1, d2, d3, d4, d5 = vjp((g1[...], g2[...], g3[...], g4[...]))
        o1[...] = d1
        o2[...] = d2
        grp = lax.broadcasted_iota(jnp.int32, (32, S5_STATES), 0)
        st = lax.broadcasted_iota(jnp.int32, (32, S5_STATES), 1)
        sel = (st // 64 == grp).astype(F32)
        o3[...] = _dot_hi(sel, d3)
        o4[...] = d4
        o5[...] = d5

    return _pc(
        body, name="s5_disc_bwd",
        out_shape=[jax.ShapeDtypeStruct((S5_STATES, 1), F32), jax.ShapeDtypeStruct((S5_STATES, 1), F32),
                   jax.ShapeDtypeStruct((32, 1), F32),
                   jax.ShapeDtypeStruct((S5_STATES, 16), F32), jax.ShapeDtypeStruct((S5_STATES, 16), F32)],
    )(a_re, a_im, log_dt, b_re, b_im, d_ab_re, d_ab_im, d_bb_re, d_bb_im)


def _cmul_add(xr, xi, pr, pi, yr, yi):
    return xr + pr * yr - pi * yi, xi + pr * yi + pi * yr


def _powers(ar, ai, n):
    out = [(ar, ai)]
    for _ in range(n - 1):
        pr, pi = out[-1]
        out.append((pr * pr - pi * pi, 2.0 * pr * pi))
    return out


_BW = S5_STATES // S5_BLOCKS
_BI = S5_WIDTH // S5_BLOCKS
SUB = 8
S5_ROWS = S5_CHUNK // SUB


def _scan8(br, bi, pws, rowin, reverse):
    k = 1
    for pr, pi in pws:
        if reverse:
            keep = rowin < SUB - k
            sr = jnp.where(keep, pltpu.roll(br, SUB - k, 0), 0.0)
            si = jnp.where(keep, pltpu.roll(bi, SUB - k, 0), 0.0)
        else:
            keep = rowin >= k
            sr = jnp.where(keep, pltpu.roll(br, k, 0), 0.0)
            si = jnp.where(keep, pltpu.roll(bi, k, 0), 0.0)
        br, bi = _cmul_add(br, bi, pr, pi, sr, si)
        k *= 2
    return br, bi


def _s5_tables(ab_ref, tab_ref, reverse):
    rowin = lax.broadcasted_iota(jnp.int32, (SUB, 1), 0)
    ar = ab_ref[0:1, :]
    ai = -ab_ref[1:2, :] if reverse else ab_ref[1:2, :]
    hit = rowin == (SUB - 1 if reverse else 0)
    zero = jnp.zeros((SUB, S5_STATES), F32)
    pr, pi = _scan8(jnp.where(hit, ar, 0.0) + zero, jnp.where(hit, ai, 0.0) + zero, _powers(ar, ai, 3), rowin, reverse)
    tab_ref[0:SUB, :] = pr
    tab_ref[SUB:2 * SUB, :] = pi


def _s5_fwd(u5, wb4, wc4, ab, dvec, rider=None):
    T = u5.shape[0]
    q = S5_CHUNK
    nc = T // q

    def body(u_ref, wb_ref, wc_ref, ab_ref, d_ref, y_ref, sp_ref, carry_ref, tab_ref, sr_ref, si_ref):
        i = pl.program_id(0)
        rowin = lax.broadcasted_iota(jnp.int32, (SUB, 1), 0)

        @pl.when(i == 0)
        def _():
            carry_ref[...] = jnp.zeros_like(carry_ref)
            _s5_tables(ab_ref, tab_ref, False)

        sp_ref[0] = carry_ref[...]
        for j in range(S5_BLOCKS):
            bu = _dot(u_ref[:, _BI * j:_BI * (j + 1)].astype(BF16), wb_ref[j])
            sr_ref[:, :, _BW * j:_BW * (j + 1)] = bu[:, :_BW].reshape(S5_ROWS, SUB, _BW)
            si_ref[:, :, _BW * j:_BW * (j + 1)] = bu[:, _BW:].reshape(S5_ROWS, SUB, _BW)
        pws = _powers(ab_ref[0:1, :], ab_ref[1:2, :], 3)
        tr, ti = tab_ref[0:SUB, :], tab_ref[SUB:2 * SUB, :]
        cr, ci = carry_ref[0:1, :], carry_ref[1:2, :]
        for k in range(S5_ROWS):
            sr, si = _scan8(sr_ref[k], si_ref[k], pws, rowin, False)
            sr, si = _cmul_add(sr, si, tr, ti, cr, ci)
            sr_ref[k] = sr
            si_ref[k] = si
            cr, ci = sr[SUB - 1:SUB, :], si[SUB - 1:SUB, :]
        carry_ref[0:1, :] = cr
        carry_ref[1:2, :] = ci
        for j in range(S5_BLOCKS):
            sl = slice(_BW * j, _BW * (j + 1))
            ul = slice(_BI * j, _BI * (j + 1))
            s = jnp.concatenate([sr_ref[:, :, sl].reshape(q, _BW), si_ref[:, :, sl].reshape(q, _BW)], axis=1).astype(BF16)
            y_ref[:, ul] = _dot(s, wc_ref[j]) + d_ref[:, ul] * u_ref[:, ul]

    return _call(
        body, rider, name="s5_fwd", grid=(nc,),
        in_specs=[_row_spec(q, S5_WIDTH), _const_spec((S5_BLOCKS, _BI, 2 * _BW)), _const_spec((S5_BLOCKS, 2 * _BW, _BI)),
                  _const_spec((8, S5_STATES)), _const_spec((1, S5_WIDTH))],
        out_specs=[_row_spec(q, S5_WIDTH), pl.BlockSpec((1, 8, S5_STATES), lambda i: (i, 0, 0))],
        out_shape=[jax.ShapeDtypeStruct((T, S5_WIDTH), F32), jax.ShapeDtypeStruct((nc, 8, S5_STATES), F32)],
        scratch_shapes=[pltpu.VMEM((8, S5_STATES), F32), pltpu.VMEM((2 * SUB, S5_STATES), F32),
                        pltpu.VMEM((S5_ROWS, SUB, S5_STATES), F32), pltpu.VMEM((S5_ROWS, SUB, S5_STATES), F32)],
        compiler_params=_cparams(("arbitrary",)),
    )(u5, wb4, wc4, ab, dvec)


def _s5_bwd(u5, dy5, wb4, wc4, ab, dvec, sprev, rider=None):
    T = u5.shape[0]
    q = S5_CHUNK
    nc = T // q

    def rev(width):
        return pl.BlockSpec((q, width), lambda i: (nc - 1 - i, 0))

    def body(u_ref, dy_ref, wb_ref, wc_ref, ab_ref, d_ref, sp_ref, du_ref, dwb_ref, dwc_ref, dab_ref, dd_ref,
             carry_ref, tab_ref, rtab_ref, sr_ref, si_ref, lr_ref, li_ref):
        i = pl.program_id(0)
        rowin = lax.broadcasted_iota(jnp.int32, (SUB, 1), 0)

        @pl.when(i == 0)
        def _():
            carry_ref[...] = jnp.zeros_like(carry_ref)
            dwb_ref[...] = jnp.zeros_like(dwb_ref)
            dwc_ref[...] = jnp.zeros_like(dwc_ref)
            dab_ref[...] = jnp.zeros_like(dab_ref)
            dd_ref[...] = jnp.zeros_like(dd_ref)
            _s5_tables(ab_ref, tab_ref, False)
            _s5_tables(ab_ref, rtab_ref, True)

        for j in range(S5_BLOCKS):
            sl = slice(_BW * j, _BW * (j + 1))
            ul = slice(_BI * j, _BI * (j + 1))
            bu = _dot(u_ref[:, ul].astype(BF16), wb_ref[j])
            sr_ref[:, :, sl] = bu[:, :_BW].reshape(S5_ROWS, SUB, _BW)
            si_ref[:, :, sl] = bu[:, _BW:].reshape(S5_ROWS, SUB, _BW)
            ds = _dot_nt(dy_ref[:, ul].astype(BF16), wc_ref[j])
            lr_ref[:, :, sl] = ds[:, :_BW].reshape(S5_ROWS, SUB, _BW)
            li_ref[:, :, sl] = ds[:, _BW:].reshape(S5_ROWS, SUB, _BW)
        ar, ai = ab_ref[0:1, :], ab_ref[1:2, :]
        pws = _powers(ar, ai, 3)
        tr, ti = tab_ref[0:SUB, :], tab_ref[SUB:2 * SUB, :]
        cr, ci = sp_ref[0, 0:1, :], sp_ref[0, 1:2, :]
        for k in range(S5_ROWS):
            sr, si = _scan8(sr_ref[k], si_ref[k], pws, rowin, False)
            sr, si = _cmul_add(sr, si, tr, ti, cr, ci)
            sr_ref[k] = sr
            si_ref[k] = si
            cr, ci = sr[SUB - 1:SUB, :], si[SUB - 1:SUB, :]
        pws = _powers(ar, -ai, 3)
        tr, ti = rtab_ref[0:SUB, :], rtab_ref[SUB:2 * SUB, :]
        cr, ci = carry_ref[0:1, :], carry_ref[1:2, :]
        acc_r = jnp.zeros((SUB, S5_STATES), F32)
        acc_i = jnp.zeros((SUB, S5_STATES), F32)
        for k in reversed(range(S5_ROWS)):
            lr, li = _scan8(lr_ref[k], li_ref[k], pws, rowin, True)
            lr, li = _cmul_add(lr, li, tr, ti, cr, ci)
            lr_ref[k] = lr
            li_ref[k] = li
            cr, ci = lr[0:1, :], li[0:1, :]
            if k > 0:
                before_r, before_i = sr_ref[k - 1, SUB - 1:SUB, :], si_ref[k - 1, SUB - 1:SUB, :]
            else:
                before_r, before_i = sp_ref[0, 0:1, :], sp_ref[0, 1:2, :]
            keep = rowin >= 1
            pr = jnp.where(keep, pltpu.roll(sr_ref[k], 1, 0), before_r)
            pi = jnp.where(keep, pltpu.roll(si_ref[k], 1, 0), before_i)
            acc_r += lr * pr + li * pi
            acc_i += li * pr - lr * pi
        carry_ref[0:1, :] = cr
        carry_ref[1:2, :] = ci
        dab_ref[0:1, :] += jnp.sum(acc_r, axis=0, keepdims=True)
        dab_ref[1:2, :] += jnp.sum(acc_i, axis=0, keepdims=True)
        for j in range(S5_BLOCKS):
            sl = slice(_BW * j, _BW * (j + 1))
            ul = slice(_BI * j, _BI * (j + 1))
            u = u_ref[:, ul]
            dy = dy_ref[:, ul]
            dyb = dy.astype(BF16)
            lam = jnp.concatenate([lr_ref[:, :, sl].reshape(q, _BW), li_ref[:, :, sl].reshape(q, _BW)], axis=1).astype(BF16)
            s = jnp.concatenate([sr_ref[:, :, sl].reshape(q, _BW), si_ref[:, :, sl].reshape(q, _BW)], axis=1).astype(BF16)
            du_ref[:, ul] = _dot_nt(lam, wb_ref[j]) + d_ref[:, ul] * dy
            dwb_ref[j] += _dot_tn(u.astype(BF16), lam)
            dwc_ref[j] += _dot_tn(s, dyb)
            dd_ref[:, ul] += jnp.sum(dy * u, axis=0, keepdims=True)

    big = pltpu.VMEM((S5_ROWS, SUB, S5_STATES), F32)
    return _call(
        body, rider, name="s5_bwd", grid=(nc,),
        in_specs=[rev(S5_WIDTH), rev(S5_WIDTH), _const_spec((S5_BLOCKS, _BI, 2 * _BW)), _const_spec((S5_BLOCKS, 2 * _BW, _BI)),
                  _const_spec((8, S5_STATES)), _const_spec((1, S5_WIDTH)),
                  pl.BlockSpec((1, 8, S5_STATES), lambda i: (nc - 1 - i, 0, 0))],
        out_specs=[rev(S5_WIDTH), _const_spec((S5_BLOCKS, _BI, 2 * _BW)), _const_spec((S5_BLOCKS, 2 * _BW, _BI)),
                   _const_spec((8, S5_STATES)), _const_spec((1, S5_WIDTH))],
        out_shape=[jax.ShapeDtypeStruct((T, S5_WIDTH), F32), jax.ShapeDtypeStruct((S5_BLOCKS, _BI, 2 * _BW), F32),
                   jax.ShapeDtypeStruct((S5_BLOCKS, 2 * _BW, _BI), F32), jax.ShapeDtypeStruct((8, S5_STATES), F32),
                   jax.ShapeDtypeStruct((1, S5_WIDTH), F32)],
        scratch_shapes=[pltpu.VMEM((8, S5_STATES), F32), pltpu.VMEM((2 * SUB, S5_STATES), F32),
                        pltpu.VMEM((2 * SUB, S5_STATES), F32), big, big, big, big],
        compiler_params=_cparams(("arbitrary",)),
    )(u5, dy5, wb4, wc4, ab, dvec, sprev)


def _merge_vals(ys, xs, z, y5, gates, dvec, gssd, glu_w, glu_b, wbr):
    sz = _sigmoid(z)
    qv = ys + dvec * xs
    pre = qv * (z * sz)
    yn, rs = [], []
    for gi in range(SSD_GROUPS):
        p, r = _rms(pre[:, 256 * gi:256 * (gi + 1)])
        yn.append(p)
        rs.append(r)
    yn = jnp.concatenate(yn, axis=1)
    ya = yn * gssd
    gel = _gelu(y5)
    sg = _sigmoid(_dot(gel.astype(BF16), glu_w) + glu_b)
    yb = gel * sg
    pa = _dot(ya.astype(BF16), wbr[0:SSD_INNER, :])
    pb = _dot(yb.astype(BF16), wbr[SSD_INNER:, :])
    s0 = _sigmoid(gates[:, :D_MODEL])
    s1 = _sigmoid(gates[:, D_MODEL:])
    merged = s0 * pa + s1 * pb
    return dict(sz=sz, qv=qv, yn=yn, rs=rs, ya=ya, gel=gel, sg=sg, yb=yb, pa=pa, pb=pb, s0=s0, s1=s1, merged=merged)


def _merge_specs(tm):
    acts = [_row_spec(tm, 1024), _row_spec(tm, 1024, 0), _row_spec(tm, 1024), _row_spec(tm, 512), _row_spec(tm, 2048),
            _row_spec(tm, 1024)]
    params = [_const_spec((1, 1024)), _const_spec((1, 1024)), _const_spec((512, 512)), _const_spec((1, 512)),
              _hbm_spec(), _hbm_spec()]
    return acts, params


def _merge_fwd(ys, xbc_act, z, y5, gates, x, dvec, gssd, glu_w, glu_b, wbr, wout):
    T = x.shape[0]
    tm = TOKEN_TILE
    acts, params = _merge_specs(tm)

    def body(ys_ref, xs_ref, z_ref, y5_ref, gt_ref, x_ref, dv_ref, gs_ref, gw_ref, gb_ref, wbr_hbm, wout_hbm, x1_ref,
             wbr_ref, wout_ref):
        @pl.when(pl.program_id(0) == 0)
        def _():
            pltpu.sync_copy(wbr_hbm, wbr_ref)
            pltpu.sync_copy(wout_hbm, wout_ref)

        v = _merge_vals(ys_ref[...], xs_ref[...], z_ref[...], y5_ref[...], gt_ref[...], dv_ref[...], gs_ref[...],
                        gw_ref[...], gb_ref[...], wbr_ref)
        x1_ref[...] = x_ref[...] + _dot(v["merged"].astype(BF16), wout_ref[...])

    return _pc(
        body, name="merge_fwd", grid=(T // tm,),
        in_specs=acts + params, out_specs=_row_spec(tm, 1024),
        out_shape=jax.ShapeDtypeStruct((T, 1024), F32),
        scratch_shapes=[pltpu.VMEM((1536, 1024), BF16), pltpu.VMEM((1024, 1024), BF16)],
        compiler_params=_cparams(("arbitrary",)),
    )(ys, xbc_act, z, y5, gates, x, dvec, gssd, glu_w, glu_b, wbr, wout)


def _merge_bwd(ys, xbc_act, z, y5, gates, dx1, dvec, gssd, glu_w, glu_b, wbr, wout, head_sel, rider=None):
    T = dx1.shape[0]
    tm = TOKEN_TILE
    acts, params = _merge_specs(tm)

    def body(ys_ref, xs_ref, z_ref, y5_ref, gt_ref, dx1_ref, dv_ref, gs_ref, gw_ref, gb_ref, wbr_hbm, wout_hbm, hs_ref,
             dys_ref, dxs_ref, dz_ref, dy5_ref, dgt_ref, mg_ref, ya_ref, yb_ref, dpa_ref, dpb_ref, gel_ref, dpre_ref,
             ddv_ref, dgs_ref, dgb_ref, wbr_ref, wout_ref, ddacc_ref):
        i = pl.program_id(0)

        @pl.when(i == 0)
        def _():
            pltpu.sync_copy(wbr_hbm, wbr_ref)
            pltpu.sync_copy(wout_hbm, wout_ref)
            ddacc_ref[...] = jnp.zeros_like(ddacc_ref)
            dgs_ref[...] = jnp.zeros_like(dgs_ref)
            dgb_ref[...] = jnp.zeros_like(dgb_ref)

        ys, xs, z, y5, gates = ys_ref[...], xs_ref[...], z_ref[...], y5_ref[...], gt_ref[...]
        dvv, gsv, gw = dv_ref[...], gs_ref[...], gw_ref[...]
        v = _merge_vals(ys, xs, z, y5, gates, dvv, gsv, gw, gb_ref[...], wbr_ref)
        dmg = _dot_nt(dx1_ref[...].astype(BF16), wout_ref[...])
        s0, s1, pa, pb = v["s0"], v["s1"], v["pa"], v["pb"]
        dgt_ref[:, :D_MODEL] = dmg * pa * s0 * (1.0 - s0)
        dgt_ref[:, D_MODEL:] = dmg * pb * s1 * (1.0 - s1)
        dpa = (dmg * s0).astype(BF16)
        dpb = (dmg * s1).astype(BF16)
        dya = _dot_nt(dpa, wbr_ref[0:SSD_INNER, :])
        dyb = _dot_nt(dpb, wbr_ref[SSD_INNER:, :])
        gel, sg = v["gel"], v["sg"]
        dpre = (dyb * gel * sg * (1.0 - sg))
        dgb_ref[...] += jnp.sum(dpre, axis=0, keepdims=True)
        dpre_b = dpre.astype(BF16)
        dgel = dyb * sg + _dot_nt(dpre_b, gw)
        dy5_ref[...] = dgel * _gelu_grad(y5)
        yn = v["yn"]
        dgs_ref[...] += jnp.sum(dya * yn, axis=0, keepdims=True)
        dyn = dya * gsv
        dpre_a = jnp.concatenate(
            [_rms_bwd(yn[:, 256 * gi:256 * (gi + 1)], v["rs"][gi], dyn[:, 256 * gi:256 * (gi + 1)])
             for gi in range(SSD_GROUPS)], axis=1)
        sz, qv = v["sz"], v["qv"]
        dq = dpre_a * (z * sz)
        dz_ref[...] = dpre_a * qv * (sz * (1.0 + z * (1.0 - sz)))
        dys_ref[...] = dq
        dxs_ref[...] = dq * dvv
        ddacc_ref[...] += jnp.sum(dq * xs, axis=0, keepdims=True)
        mg_ref[...] = v["merged"].astype(BF16)
        ya_ref[...] = v["ya"].astype(BF16)
        yb_ref[...] = v["yb"].astype(BF16)
        dpa_ref[...] = dpa
        dpb_ref[...] = dpb
        gel_ref[...] = gel.astype(BF16)
        dpre_ref[...] = dpre_b

        @pl.when(i == pl.num_programs(0) - 1)
        def _():
            ddv_ref[...] = _dot_hi(ddacc_ref[...], hs_ref[...])

    outs = [(1024, F32), (1024, F32), (1024, F32), (512, F32), (2048, F32),
            (1024, BF16), (1024, BF16), (512, BF16), (1024, BF16), (1024, BF16), (512, BF16), (512, BF16)]
    return _call(
        body, rider, name="merge_bwd", grid=(T // tm,),
        in_specs=acts + params + [_const_spec((1024, DT_PAD))],
        out_specs=[_row_spec(tm, w) for w, _ in outs] + [_const_spec((1, DT_PAD)), _const_spec((1, 1024)), _const_spec((1, 512))],
        out_shape=[jax.ShapeDtypeStruct((T, w), d) for w, d in outs] + [
            jax.ShapeDtypeStruct((1, DT_PAD), F32), jax.ShapeDtypeStruct((1, 1024), F32), jax.ShapeDtypeStruct((1, 512), F32)],
        scratch_shapes=[pltpu.VMEM((1536, 1024), BF16), pltpu.VMEM((1024, 1024), BF16), pltpu.VMEM((1, 1024), F32)],
        compiler_params=_cparams(("arbitrary",)),
    )(ys, xbc_act, z, y5, gates, dx1, dvec, gssd, glu_w, glu_b, wbr, wout, head_sel)


def _mlp_fwd(x1, g, w1, w2):
    T = x1.shape[0]
    tm = TOKEN_TILE

    def body(x_ref, g_ref, w1_hbm, w2_hbm, o_ref, w1_ref, w2_ref):
        @pl.when(pl.program_id(0) == 0)
        def _():
            pltpu.sync_copy(w1_hbm, w1_ref)
            pltpu.sync_copy(w2_hbm, w2_ref)

        xv = x_ref[...]
        xn, _ = _rms(xv)
        h = (xn * g_ref[...]).astype(BF16)
        acc = xv
        for s in range(FF_SHARDS):
            rl = jnp.maximum(_dot(h, w1_ref[s]), 0.0)
            acc += _dot((rl * rl).astype(BF16), w2_ref[FF_SHARD * s:FF_SHARD * (s + 1), :])
        o_ref[...] = acc

    return _pc(
        body, name="mlp_fwd", grid=(T // tm,),
        in_specs=[_row_spec(tm, 1024), _const_spec((1, 1024)), _hbm_spec(), _hbm_spec()],
        out_specs=_row_spec(tm, 1024), out_shape=jax.ShapeDtypeStruct((T, 1024), F32),
        scratch_shapes=[pltpu.VMEM((FF_SHARDS, D_MODEL, FF_SHARD), BF16), pltpu.VMEM((D_FF, D_MODEL), BF16)],
        compiler_params=_cparams(("arbitrary",)),
    )(x1, g, w1, w2)


def _mlp_bwd(x1, dx2, g, w1, w2):
    T = x1.shape[0]
    tm = TOKEN_TILE

    def body(x_ref, dx2_ref, g_ref, w1_hbm, w2_hbm, dx1_ref, h_ref, act_ref, da_ref, dg_ref, w1_ref, w2_ref):
        @pl.when(pl.program_id(0) == 0)
        def _():
            pltpu.sync_copy(w1_hbm, w1_ref)
            pltpu.sync_copy(w2_hbm, w2_ref)
            dg_ref[...] = jnp.zeros_like(dg_ref)

        xn, r = _rms(x_ref[...])
        gv = g_ref[...]
        h = (xn * gv).astype(BF16)
        h_ref[...] = h
        dx2 = dx2_ref[...]
        dx2b = dx2.astype(BF16)
        dh = jnp.zeros((tm, D_MODEL), F32)
        for s in range(FF_SHARDS):
            ff = slice(FF_SHARD * s, FF_SHARD * (s + 1))
            rl = jnp.maximum(_dot(h, w1_ref[s]), 0.0)
            act_ref[:, ff] = (rl * rl).astype(BF16)
            da = (_dot_nt(dx2b, w2_ref[ff, :]) * (2.0 * rl)).astype(BF16)
            da_ref[:, ff] = da
            dh += _dot_nt(da, w1_ref[s])
        dg_ref[...] += jnp.sum(dh * xn, axis=0, keepdims=True)
        dx1_ref[...] = dx2 + _rms_bwd(xn, r, dh * gv)

    return _pc(
        body, name="mlp_bwd", grid=(T // tm,),
        in_specs=[_row_spec(tm, 1024), _row_spec(tm, 1024), _const_spec((1, 1024)), _hbm_spec(), _hbm_spec()],
        out_specs=[_row_spec(tm, 1024), _row_spec(tm, 1024), _row_spec(tm, D_FF), _row_spec(tm, D_FF), _const_spec((1, 1024))],
        out_shape=[jax.ShapeDtypeStruct((T, 1024), F32), jax.ShapeDtypeStruct((T, 1024), BF16),
                   jax.ShapeDtypeStruct((T, D_FF), BF16), jax.ShapeDtypeStruct((T, D_FF), BF16),
                   jax.ShapeDtypeStruct((1, 1024), F32)],
        scratch_shapes=[pltpu.VMEM((FF_SHARDS, D_MODEL, FF_SHARD), BF16), pltpu.VMEM((D_FF, D_MODEL), BF16)],
        compiler_params=_cparams(("arbitrary",)),
    )(x1, dx2, g, w1, w2)


def _loss_head(x2, target, g):
    T = x2.shape[0]
    tm = TOKEN_TILE

    def body(x_ref, t_ref, g_ref, dx_ref, loss_ref, dg_ref):
        @pl.when(pl.program_id(0) == 0)
        def _():
            loss_ref[...] = jnp.zeros_like(loss_ref)
            dg_ref[...] = jnp.zeros_like(dg_ref)

        xn, r = _rms(x_ref[...])
        gv = g_ref[...]
        err = xn * gv - t_ref[...]
        loss_ref[...] += jnp.sum(err * err, axis=0, keepdims=True) * (0.5 / D_MODEL)
        dy = err * (1.0 / D_MODEL)
        dg_ref[...] += jnp.sum(dy * xn, axis=0, keepdims=True)
        dx_ref[...] = _rms_bwd(xn, r, dy * gv)

    return _pc(
        body, name="loss_head", grid=(T // tm,),
        in_specs=[_row_spec(tm, 1024), _row_spec(tm, 1024), _const_spec((1, 1024))],
        out_specs=[_row_spec(tm, 1024), _const_spec((1, 1024)), _const_spec((1, 1024))],
        out_shape=[jax.ShapeDtypeStruct((T, 1024), F32), jax.ShapeDtypeStruct((1, 1024), F32),
                   jax.ShapeDtypeStruct((1, 1024), F32)],
        compiler_params=_cparams(("arbitrary",)),
    )(x2, target, g)


WGRAD_OUT_ELEMS = 2 * 1024 * 1024
WGRAD_TILE_BYTES = 4 * 1024 * 1024


def _wgrad(a, b, name, col_shards=None, row_shards_into=None):
    T, K = a.shape
    N = b.shape[1]
    nb = N // col_shards if col_shards else min(N, 1024, max(128, WGRAD_OUT_ELEMS // K))
    tt = min(T, WGRAD_TOKENS)
    while tt * max(K * a.dtype.itemsize, nb * b.dtype.itemsize) > WGRAD_TILE_BYTES:
        tt //= 2
    assert N % nb == 0 and T % tt == 0
    in_specs = [pl.BlockSpec((tt, K), lambda n, t: (t, 0)), pl.BlockSpec((tt, nb), lambda n, t: (t, n))]
    args, aliases = [a, b], {}
    if col_shards:
        out_spec = pl.BlockSpec((None, None, K, nb), lambda n, t: (n, 0, 0, 0))
        out_shape = jax.ShapeDtypeStruct((col_shards, 2, K, nb), F32)
    elif row_shards_into is not None:
        shards, _, rows, cols = row_shards_into.shape
        assert shards * rows == K and cols == N
        out_spec = pl.BlockSpec((shards, None, rows, nb), lambda n, t: (0, 1, 0, n))
        out_shape = jax.ShapeDtypeStruct(row_shards_into.shape, F32)
        in_specs.append(_hbm_spec())
        args.append(row_shards_into)
        aliases = {2: 0}
    else:
        out_spec = pl.BlockSpec((K, nb), lambda n, t: (0, n))
        out_shape = jax.ShapeDtypeStruct((K, N), F32)

    def body(a_ref, b_ref, *rest):
        o_ref = rest[-1]

        @pl.when(pl.program_id(1) == 0)
        def _():
            o_ref[...] = jnp.zeros_like(o_ref)

        o_ref[...] += _dot_tn(a_ref[...].astype(BF16), b_ref[...].astype(BF16)).reshape(o_ref.shape)

    return _pc(
        body, name=name, grid=(N // nb, T // tt), in_specs=in_specs, out_specs=out_spec, out_shape=out_shape,
        input_output_aliases=aliases, compiler_params=_cparams(("parallel", "arbitrary")),
    )(*args)


def _s5_block_weights(bb_re, bb_im, c_re, c_im):
    eye = jnp.eye(8, dtype=F32)
    bre = bb_re.reshape(S5_BLOCKS, 8, 64, 16)
    bim = bb_im.reshape(S5_BLOCKS, 8, 64, 16)
    wb_re = jnp.einsum('jgpk,gh->jhkgp', bre, eye).reshape(S5_BLOCKS, _BI, _BW)
    wb_im = jnp.einsum('jgpk,gh->jhkgp', bim, eye).reshape(S5_BLOCKS, _BI, _BW)
    wb4 = jnp.concatenate([wb_re, wb_im], axis=2).astype(BF16)
    cre = c_re.reshape(S5_BLOCKS, 8, 16, 64)
    cim = c_im.reshape(S5_BLOCKS, 8, 16, 64)
    wc_re = jnp.einsum('jgkp,gh->jgphk', cre, eye).reshape(S5_BLOCKS, _BW, _BI)
    wc_im = jnp.einsum('jgkp,gh->jgphk', -cim, eye).reshape(S5_BLOCKS, _BW, _BI)
    wc4 = jnp.concatenate([wc_re, wc_im], axis=1).astype(BF16)
    return wb4, wc4


def _s5_block_grads(dwb4, dwc4):
    eye = jnp.eye(8, dtype=F32)
    dwb = dwb4.reshape(S5_BLOCKS, 8, 16, 2, 8, 64)
    dbb = jnp.einsum('jhkrgp,gh->rjgpk', dwb, eye).reshape(2, 32, 64, 16)
    dwc = dwc4.reshape(S5_BLOCKS, 2, 8, 64, 8, 16)
    dc = jnp.einsum('jrgphk,gh->rjgkp', dwc, eye).reshape(2, 32, 16, 64)
    return dbb[0], dbb[1], dc[0], -dc[1]


def _row(v, width=None):
    v = v.reshape(1, -1)
    if width is not None and v.shape[1] < width:
        v = jnp.concatenate([v, jnp.zeros((1, width - v.shape[1]), v.dtype)], axis=1)
    return v


def _local_step(x, target, p, comm=None):
    g_mix, g_mlp, g_fin = _row(p["norm_mix_g"]), _row(p["norm_mlp_g"]), _row(p["norm_final_g"])
    conv_b = _row(p["conv_b"])
    dt_bias = _row(p["dt_bias"], DT_PAD)
    alog = _row(p["a_log"], DT_PAD)
    dvec = _row(jnp.repeat(p["d_ssd"], SSD_HEADDIM))
    gssd = _row(p["ssd_norm_g"])
    s5d = _row(p["s5_d"])
    glu_b = _row(p["s5_glu_b"])
    head_sel = (jnp.arange(SSD_INNER)[:, None] // SSD_HEADDIM == jnp.arange(DT_PAD)[None, :]).astype(F32)

    a_re = p["s5_a_re"].reshape(S5_STATES, 1)
    a_im = p["s5_a_im"].reshape(S5_STATES, 1)
    log_dt = jnp.repeat(p["s5_log_dt"], 64).reshape(S5_STATES, 1)
    b_re = p["s5_b_re"].reshape(S5_STATES, 16)
    b_im = p["s5_b_im"].reshape(S5_STATES, 16)
    ab_re, ab_im, bb_re, bb_im = _s5_disc(a_re, a_im, log_dt, b_re, b_im)
    wb4, wc4 = _s5_block_weights(bb_re, bb_im, p["s5_c_re"], p["s5_c_im"])
    ab = jnp.concatenate([ab_re.reshape(1, S5_STATES), ab_im.reshape(1, S5_STATES), jnp.zeros((6, S5_STATES), F32)], axis=0)

    wp = p["w_in_perm"]

    z, xbc_raw, u5, gates, dt_raw, h = _inproj_fwd(x, g_mix, wp)
    xbc_act, dt = _conv_fwd(xbc_raw, dt_raw, p["conv_w"], conv_b, dt_bias)
    ys, ssd_states = _ssd_fwd(xbc_act, dt, alog)
    if comm is None:
        y5, s5_states = _s5_fwd(u5, wb4, wc4, ab, s5d)
    else:
        (y5, s5_states), late = _s5_fwd(u5, wb4, wc4, ab, s5d, rider=_Gather(comm["late_srcs"], comm["late_ks"]))
        p = {**p, **comm["late_unpack"](late)}
    wbr, wout, w1, w2, glu_w = p["w_branch"], p["w_out"], p["w_mlp_in"], p["w_mlp_out"], p["s5_glu_w"]
    x1 = _merge_fwd(ys, xbc_act, z, y5, gates, x, dvec, gssd, glu_w, glu_b, wbr, wout)
    x2 = _mlp_fwd(x1, g_mlp, w1, w2)
    dx2, loss_lanes, d_gfin = _loss_head(x2, target, g_fin)

    dx1, h2, act, da1, d_gmlp = _mlp_bwd(x1, dx2, g_mlp, w1, w2)
    g_mlp4 = _wgrad(h2, da1, "wgrad_mlp_in", col_shards=FF_SHARDS)
    g_mlp4 = _wgrad(act, dx2, "wgrad_mlp_out", row_shards_into=g_mlp4)
    d_w_mlp_in, d_w_mlp_out = g_mlp4[:, 0], g_mlp4[:, 1].reshape(D_FF, D_MODEL)
    merge_args = (ys, xbc_act, z, y5, gates, dx1, dvec, gssd, glu_w, glu_b, wbr, wout, head_sel)
    if comm is None:
        merge_out = _merge_bwd(*merge_args)
    else:
        g_mlp = g_mlp4.reshape(N_CHIPS, 2 * FF_SHARD, D_MODEL)
        merge_out, (sib_mlp,) = _merge_bwd(*merge_args, rider=_Pair([g_mlp]))
        pf_mlp, pb_mlp = _pair_sum(comm["place"], g_mlp, sib_mlp, "pair_sum_mlp")
    (dys, dxs_m, dz, dy5, dgates, mg, ya, yb, dpa, dpb, gel, dpre, d_dssd, d_gssd, d_glu_b) = merge_out
    d_w_out = _wgrad(mg, dx1, "wgrad_out")
    d_w_branch = jnp.concatenate([_wgrad(ya, dpa, "wgrad_branch_a"), _wgrad(yb, dpb, "wgrad_branch_b")], axis=0)
    d_glu_w = _wgrad(gel, dpre, "wgrad_glu")
    s5_args = (u5, dy5, wb4, wc4, ab, s5d, s5_states)
    if comm is None:
        du5, dwb4, dwc4, dab, d_s5d = _s5_bwd(*s5_args)
        mlp_total = None
    else:
        (du5, dwb4, dwc4, dab, d_s5d), (got_mlp,) = _s5_bwd(*s5_args, rider=_Chip([pb_mlp]))
        (mlp_total,) = _chip_sum(comm["place"], pf_mlp, got_mlp, "chip_sum_mlp")
    dbb_re, dbb_im, d_c_re, d_c_im = _s5_block_grads(dwb4, dwc4)
    d_a_re, d_a_im, d_log_dt, d_b_re, d_b_im = _s5_disc_bwd(
        a_re, a_im, log_dt, b_re, b_im, dab[0].reshape(S5_STATES, 1), dab[1].reshape(S5_STATES, 1),
        dbb_re.reshape(S5_STATES, 16), dbb_im.reshape(S5_STATES, 16))
    dxs_s, dB, dC, ddt, d_alog = _ssd_bwd(xbc_act, dt, alog, ssd_states, dys)
    dxbc_raw, ddt_raw, d_conv_w, d_conv_b, d_dt_bias = _conv_bwd(
        xbc_raw, dt_raw, dxs_m, dxs_s, dB, dC, ddt, p["conv_w"], conv_b, dt_bias)
    d_w_in = dict(z=_wgrad(h, dz, "wgrad_in_z"), xbc=_wgrad(h, dxbc_raw, "wgrad_in_xbc"),
                  dt=_wgrad(h, ddt_raw, "wgrad_in_dt")[:, :16], u5=_wgrad(h, du5, "wgrad_in_u5"),
                  gates=_wgrad(h, dgates, "wgrad_in_gates"))
    w_in_pieces = [(c0, d_w_in[n]) for n, c0, _ in W_IN_PIECES]
    inproj_args = (x, dx1, dz, dxbc_raw, du5, dgates, ddt_raw, g_mix, wp)
    if comm is None:
        dx, d_gmix = _inproj_bwd(*inproj_args)
        late_totals = None
    else:
        g_b, g_in = _late_buffers(d_w_out, d_w_branch, d_glu_w, d_conv_w[:CONV_K], w_in_pieces)
        sib_b, sib_in = _exchange(_Pair([g_b, g_in]), "pair_exchange")
        pf_b, pb_b = _pair_sum(comm["place"], g_b, sib_b, "pair_sum_b")
        pf_in, pb_in = _pair_sum(comm["place"], g_in, sib_in, "pair_sum_in")
        (dx, d_gmix), (got_b, got_in) = _inproj_bwd(*inproj_args, rider=_Chip([pb_b, pb_in]))
        late_totals = (_chip_sum(comm["place"], pf_b, got_b, "chip_sum_b")[0],
                       _chip_sum(comm["place"], pf_in, got_in, "chip_sum_in")[0])

    grads = dict(
        norm_mix_g=d_gmix.reshape(-1), w_in_pieces=w_in_pieces, late_totals=late_totals,
        conv_w=d_conv_w[:CONV_K], conv_b=d_conv_b.reshape(-1),
        dt_bias=d_dt_bias[0, :16], a_log=d_alog[0, :16], d_ssd=d_dssd[0, :16], ssd_norm_g=d_gssd.reshape(-1),
        s5_a_re=d_a_re.reshape(32, 64), s5_a_im=d_a_im.reshape(32, 64), s5_log_dt=d_log_dt.reshape(32),
        s5_b_re=d_b_re.reshape(32, 64, 16), s5_b_im=d_b_im.reshape(32, 64, 16), s5_c_re=d_c_re, s5_c_im=d_c_im,
        s5_d=d_s5d.reshape(-1), s5_glu_w=d_glu_w, s5_glu_b=d_glu_b.reshape(-1), w_branch=d_w_branch, w_out=d_w_out,
        norm_mlp_g=d_gmlp.reshape(-1), w_mlp_in=d_w_mlp_in, w_mlp_out=d_w_mlp_out, norm_final_g=d_gfin.reshape(-1),
        mlp_total=mlp_total)
    return jnp.sum(loss_lanes), dx, grads


MESH = pl.DeviceIdType.MESH
N_CHIPS = 4


def _place():
    x, y, c = lax.axis_index("x"), lax.axis_index("y"), lax.axis_index("c")
    chips = [(1 - x, y), (x, 1 - y), (1 - x, 1 - y)]
    return x, y, c, chips


def _remote(src, dst, send_sems, recv_sems, k, to):
    return pltpu.make_async_remote_copy(src_ref=src, dst_ref=dst, send_sem=send_sems.at[k], recv_sem=recv_sems.at[k],
                                        device_id=to, device_id_type=MESH)


def _row_chunks(rows, k, align):
    step = rows // k
    assert rows % k == 0 and step % align == 0, (rows, k, align)
    return [(i * step, step) for i in range(k)]


ICI_CHUNKS = 4
D2D_CHUNKS = 24


class _Gather:
    def __init__(self, srcs, ks):
        self.inputs = list(srcs)
        self.out_shapes = [jax.ShapeDtypeStruct((N_CHIPS,) + a.shape, a.dtype) for a in srcs]
        self.halves = [a.shape[0] // 2 for a in srcs]
        self.pieces = [_row_chunks(h, k, 32 // a.dtype.itemsize) for a, h, k in zip(srcs, self.halves, ks)]
        self.n_ici = 3 * sum(ks)
        self.n_sems = 2 * self.n_ici + len(srcs)

    def _plan(self, src_refs, out_refs, send_sems, recv_sems):
        x, y, c, chips = _place()
        own = 2 * x + y
        sib = (x, y, 1 - c)
        first, fwd_plan, k = [], [], 0
        for a, (src_ref, out_ref) in enumerate(zip(src_refs, out_refs)):
            h = self.halves[a]
            for r0, nr in self.pieces[a]:
                for cx, cy in chips:
                    first.append(_remote(src_ref.at[pl.ds(c * h + r0, nr), :], out_ref.at[own, pl.ds(c * h + r0, nr), :],
                                         send_sems, recv_sems, k, (cx, cy, c)))
                    fwd_plan.append((out_ref, 2 * cx + cy, h, r0, nr, k, (cx, cy, c)))
                    k += 1
        for a, (src_ref, out_ref) in enumerate(zip(src_refs, out_refs)):
            first.append(_remote(src_ref, out_ref.at[own], send_sems, recv_sems, 2 * self.n_ici + a, sib))
        return first, fwd_plan, c, sib

    def issue(self, src_refs, out_refs, send_sems, recv_sems):
        for cp in self._plan(src_refs, out_refs, send_sems, recv_sems)[0]:
            cp.start()

    def complete(self, src_refs, out_refs, send_sems, recv_sems):
        first, fwd_plan, c, sib = self._plan(src_refs, out_refs, send_sems, recv_sems)
        passed = []
        for out_ref, s, h, r0, nr, k, frm in fwd_plan:
            got = out_ref.at[s, pl.ds(c * h + r0, nr), :]
            _remote(got, got, send_sems, recv_sems, k, frm).wait_recv()
            fw = _remote(got, got, send_sems, recv_sems, self.n_ici + k, sib)
            fw.start()
            passed.append(fw)
        for out_ref, s, h, r0, nr, k, frm in fwd_plan:
            got = out_ref.at[s, pl.ds((1 - c) * h + r0, nr), :]
            _remote(got, got, send_sems, recv_sems, self.n_ici + k, sib).wait_recv()
        own_copies = first[self.n_ici:]
        for cp in own_copies:
            cp.wait_recv()
        for cp in first + passed:
            cp.wait_send()


def _exchange(rider, name):
    ri, ro = len(rider.inputs), len(rider.out_shapes)

    def body(*refs):
        rider.issue(refs[:ri], refs[ri:ri + ro], *refs[ri + ro:])
        rider.complete(refs[:ri], refs[ri:ri + ro], *refs[ri + ro:])

    return _pc(
        body, name=name, in_specs=[_hbm_spec()] * ri, out_specs=[_hbm_spec()] * ro, out_shape=list(rider.out_shapes),
        scratch_shapes=[pltpu.SemaphoreType.DMA((rider.n_sems,))] * 2,
    )(*rider.inputs)


def _call(body, rider=None, **kw):
    if rider is None:
        return _pc(body, **kw)
    single = not isinstance(kw["out_shape"], (list, tuple))
    out_specs = [kw["out_specs"]] if single else list(kw["out_specs"])
    out_shape = [kw["out_shape"]] if single else list(kw["out_shape"])
    scratch = list(kw.get("scratch_shapes", ()))
    n_in, n_out, n_scr = len(kw["in_specs"]), len(out_specs), len(scratch)
    ri, ro = len(rider.inputs), len(rider.out_shapes)
    steps = kw["grid"][0]

    def wrapped(*refs):
        o0 = n_in + ri
        s0 = o0 + n_out + ro
        r_in, r_out, sems = refs[n_in:o0], refs[o0 + n_out:s0], refs[s0 + n_scr:]

        @pl.when(pl.program_id(0) == 0)
        def _():
            rider.issue(r_in, r_out, *sems)

        body(*refs[:n_in], *refs[o0:o0 + n_out], *refs[s0:s0 + n_scr])

        @pl.when(pl.program_id(0) == steps - 1)
        def _():
            rider.complete(r_in, r_out, *sems)

    f = _pc(wrapped, name=kw["name"], grid=kw["grid"], in_specs=list(kw["in_specs"]) + [_hbm_spec()] * ri,
            out_specs=out_specs + [_hbm_spec()] * ro, out_shape=out_shape + list(rider.out_shapes),
            scratch_shapes=scratch + [pltpu.SemaphoreType.DMA((rider.n_sems,))] * 2, compiler_params=kw["compiler_params"])

    def run(*args):
        res = f(*args, *rider.inputs)
        return (res[0] if single else res[:n_out]), res[n_out:]

    return run


def _d2d_pieces(rows):
    k = next(k for k in range(24, 0, -1) if rows % k == 0 and (rows // k) % 8 == 0)
    return _row_chunks(rows, k, 8)


class _Pair:
    def __init__(self, gs, small=None):
        self.n = len(gs)
        self.halves = [g.shape[1] // 2 for g in gs]
        self.inputs = list(gs) + ([small] if small is not None else [])
        self.out_shapes = [jax.ShapeDtypeStruct((N_CHIPS, h, g.shape[2]), F32) for g, h in zip(gs, self.halves)]
        if small is not None:
            self.out_shapes.append(jax.ShapeDtypeStruct(small.shape, F32))
        self.n_sems = len(self.inputs)

    def issue(self, in_refs, out_refs, send_sems, recv_sems):
        x, y, c, _ = _place()
        sib = (x, y, 1 - c)
        for a in range(self.n):
            h = self.halves[a]
            for s in range(N_CHIPS):
                for r0, nr in _d2d_pieces(h):
                    _remote(in_refs[a].at[s, pl.ds((1 - c) * h + r0, nr), :], out_refs[a].at[s, pl.ds(r0, nr), :],
                            send_sems, recv_sems, a, sib).start()
        for a in range(self.n, len(self.inputs)):
            _remote(in_refs[a], out_refs[a], send_sems, recv_sems, a, sib).start()

    def complete(self, in_refs, out_refs, send_sems, recv_sems):
        x, y, c, _ = _place()
        for a in range(len(self.inputs)):
            _remote(out_refs[a], out_refs[a], send_sems, recv_sems, a, (x, y, 1 - c)).wait()


SUM_BLOCKS = 4


def _pair_sum(place, g, sib, name, small=None, sib_small=None):
    n, R, C = g.shape
    H = R // 2
    rb = H // SUM_BLOCKS
    assert H % SUM_BLOCKS == 0 and rb % 16 == 0

    def body(place_ref, a_ref, b_ref, *rest):
        if small is None:
            pf_ref, pb_ref = rest
        else:
            s_ref, t_ref, pf_ref, pb_ref, ps_ref = rest

            @pl.when((pl.program_id(0) == 0) & (pl.program_id(1) == 0))
            def _():
                ps_ref[...] = s_ref[...] + t_ref[...]

        p = a_ref[...] + b_ref[...]
        pf_ref[...] = p
        pb_ref[...] = p.astype(BF16)

    blk = pl.BlockSpec((1, rb, C), lambda s, i, pr: (s, i, 0))
    mine = pl.BlockSpec((1, rb, C), lambda s, i, pr: (s, pr[1] * SUM_BLOCKS + i, 0))
    ins, outs, shapes, args = [mine, blk], [blk, blk], [jax.ShapeDtypeStruct((n, H, C), F32),
                                                        jax.ShapeDtypeStruct((n, H, C), BF16)], [g, sib]
    if small is not None:
        sm = pl.BlockSpec(small.shape, lambda s, i, pr: (0, 0))
        ins += [sm, sm]
        outs += [sm]
        shapes += [jax.ShapeDtypeStruct(small.shape, F32)]
        args += [small, sib_small]
    return _pc(
        body, name=name, out_shape=shapes,
        grid_spec=pltpu.PrefetchScalarGridSpec(num_scalar_prefetch=1, grid=(n, SUM_BLOCKS), in_specs=ins, out_specs=outs),
        compiler_params=_cparams(("arbitrary", "arbitrary")),
    )(place, *args)


class _Chip:
    def __init__(self, pbs, psmall=None):
        self.n = len(pbs)
        self.rows = [pb.shape[1] for pb in pbs]
        self.inputs = list(pbs) + ([psmall] if psmall is not None else [])
        self.out_shapes = [jax.ShapeDtypeStruct((3,) + pb.shape[1:], BF16) for pb in pbs]
        if psmall is not None:
            self.out_shapes.append(jax.ShapeDtypeStruct((N_CHIPS,) + psmall.shape, F32))
        self.n_sems = 3 * len(self.inputs)

    def issue(self, in_refs, out_refs, send_sems, recv_sems):
        x, y, c, chips = _place()
        own = 2 * x + y
        for j, (cx, cy) in enumerate(chips):
            for a in range(self.n):
                for r0, nr in _row_chunks(self.rows[a], ICI_CHUNKS, 16):
                    _remote(in_refs[a].at[2 * cx + cy, pl.ds(r0, nr), :], out_refs[a].at[j, pl.ds(r0, nr), :],
                            send_sems, recv_sems, 3 * a + j, (cx, cy, c)).start()
            for a in range(self.n, len(self.inputs)):
                _remote(in_refs[a], out_refs[a].at[own], send_sems, recv_sems, 3 * a + j, (cx, cy, c)).start()

    def complete(self, in_refs, out_refs, send_sems, recv_sems):
        x, y, c, chips = _place()
        own = 2 * x + y
        for j, (cx, cy) in enumerate(chips):
            for a in range(self.n):
                _remote(in_refs[a].at[own], out_refs[a].at[j], send_sems, recv_sems, 3 * a + j, (cx, cy, c)).wait()
            for a in range(self.n, len(self.inputs)):
                _remote(in_refs[a], out_refs[a].at[2 * cx + cy], send_sems, recv_sems, 3 * a + j, (cx, cy, c)).wait()


def _chip_sum(place, pf, got, name, small4=None, psmall=None):
    _, H, C = pf.shape
    rb = H // SUM_BLOCKS

    def body(place_ref, o_ref, g_ref, *rest):
        if small4 is None:
            (tot_ref,) = rest
        else:
            s_ref, p_ref, tot_ref, st_ref = rest

            @pl.when(pl.program_id(0) == 0)
            def _():
                terms = [jnp.where(place_ref[0] == s, p_ref[...], s_ref[s]) for s in range(N_CHIPS)]
                st_ref[...] = ((terms[0] + terms[1]) + terms[2]) + terms[3]

        tot_ref[...] = ((o_ref[0] + g_ref[0].astype(F32)) + g_ref[1].astype(F32)) + g_ref[2].astype(F32)

    ins = [pl.BlockSpec((1, rb, C), lambda i, pr: (pr[0], i, 0)), pl.BlockSpec((3, rb, C), lambda i, pr: (0, i, 0))]
    outs = [pl.BlockSpec((rb, C), lambda i, pr: (pr[1] * SUM_BLOCKS + i, 0))]
    shapes = [jax.ShapeDtypeStruct((2 * H, C), F32)]
    args = [pf, got]
    if small4 is not None:
        ins += [pl.BlockSpec(small4.shape, lambda i, pr: (0, 0, 0)), pl.BlockSpec(psmall.shape, lambda i, pr: (0, 0))]
        outs += [pl.BlockSpec(psmall.shape, lambda i, pr: (0, 0))]
        shapes += [jax.ShapeDtypeStruct(psmall.shape, F32)]
        args += [small4, psmall]
    return _pc(
        body, name=name, out_shape=shapes,
        grid_spec=pltpu.PrefetchScalarGridSpec(num_scalar_prefetch=1, grid=(SUM_BLOCKS,), in_specs=ins, out_specs=outs),
        compiler_params=_cparams(("arbitrary",)),
    )(place, *args)


def _half_exchange(fulls):
    n = len(fulls)

    def body(*refs):
        in_refs, out_refs = refs[:n], refs[n:2 * n]
        send_sems, recv_sems = refs[2 * n:]
        x, y, c, _ = _place()
        sib = (x, y, 1 - c)
        for a in range(n):
            h = fulls[a].shape[0] // 2
            for r0, nr in _d2d_pieces(h):
                rows = pl.ds(c * h + r0, nr)
                _remote(in_refs[a].at[rows, :], out_refs[a].at[rows, :], send_sems, recv_sems, a, sib).start()
        for a in range(n):
            h = fulls[a].shape[0] // 2
            _remote(in_refs[a].at[pl.ds(c * h, h), :], out_refs[a].at[pl.ds((1 - c) * h, h), :], send_sems, recv_sems, a,
                    sib).wait()

    return _pc(
        body, name="half_exchange", in_specs=[_hbm_spec()] * n, out_specs=[_hbm_spec()] * n,
        out_shape=[jax.ShapeDtypeStruct(f.shape, F32) for f in fulls],
        input_output_aliases={a: a for a in range(n)},
        scratch_shapes=[pltpu.SemaphoreType.DMA((n,)), pltpu.SemaphoreType.DMA((n,))],
    )(*fulls)


def _small_allreduce(pack):
    R, C = pack.shape

    def body(p_ref, o_ref, sib_ref, pair_ref, slots_ref, send_sems, recv_sems):
        x, y, c, chips = _place()
        own = 2 * x + y
        cp = _remote(p_ref, sib_ref, send_sems, recv_sems, 0, (x, y, 1 - c))
        cp.start()
        cp.wait()
        pair_ref[...] = p_ref[...] + sib_ref[...]
        slots_ref[own] = pair_ref[...]
        out = [_remote(pair_ref, slots_ref.at[own], send_sems, recv_sems, 1 + j, (cx, cy, c)) for j, (cx, cy) in enumerate(chips)]
        for cp in out:
            cp.start()
        for j, (cx, cy) in enumerate(chips):
            _remote(pair_ref, slots_ref.at[2 * cx + cy], send_sems, recv_sems, 1 + j, (cx, cy, c)).wait()
        o_ref[...] = ((slots_ref[0] + slots_ref[1]) + slots_ref[2]) + slots_ref[3]

    vmem = pl.BlockSpec(memory_space=pltpu.VMEM)
    return _pc(
        body, name="small_allreduce", in_specs=[vmem], out_specs=vmem, out_shape=jax.ShapeDtypeStruct((R, C), F32),
        scratch_shapes=[pltpu.VMEM((R, C), F32), pltpu.VMEM((R, C), F32), pltpu.VMEM((N_CHIPS, R, C), F32),
                        pltpu.SemaphoreType.DMA((4,)), pltpu.SemaphoreType.DMA((4,))],
    )(pack)


def _adamw(w, g, m, v, name, g_row0=0, with_grad=False, col_block=None):
    R, C = w.shape
    rb = 256 if R % 256 == 0 else (128 if R % 128 == 0 else R)
    if col_block:
        rb = R
    assert g_row0 % rb == 0

    def body(w_ref, g_ref, m_ref, v_ref, d_ref, nm_ref, nv_ref, *g_out):
        gv = g_ref[...]
        m2 = ADAM_B1 * m_ref[...] + (1.0 - ADAM_B1) * gv
        v2 = ADAM_B2 * v_ref[...] + (1.0 - ADAM_B2) * (gv * gv)
        m_hat = m2 / (1.0 - ADAM_B1 ** ADAM_STEP)
        v_hat = v2 / (1.0 - ADAM_B2 ** ADAM_STEP)
        d_ref[...] = -ADAM_LR * (m_hat / (jnp.sqrt(v_hat) + ADAM_EPS) + ADAM_WD * w_ref[...])
        nm_ref[...] = m2
        nv_ref[...] = v2
        if with_grad:
            g_out[0][...] = gv

    if col_block:
        spec = g_spec = pl.BlockSpec((R, col_block), lambda i: (0, i))
        steps = C // col_block
    else:
        spec = pl.BlockSpec((rb, C), lambda i: (i, 0))
        g_spec = pl.BlockSpec((rb, C), lambda i: (g_row0 // rb + i, 0))
        steps = R // rb
    n_out = 4 if with_grad else 3
    return _pc(
        body, name=name, grid=(steps,), in_specs=[spec, g_spec, spec, spec], out_specs=[spec] * n_out,
        out_shape=[jax.ShapeDtypeStruct((R, C), F32)] * n_out, compiler_params=_cparams(("parallel",)),
    )(w, g, m, v)


PACK_COLS = 1024
ROWS_A = (("w_mlp_in", 0, 1024), ("w_mlp_out", 1024, 1024), ("w_out", 2048, 256), ("w_branch", 2304, 384))
ROWS_A_TOTAL = 2688
ROWS_B = (("w_out", 0, 256), ("w_branch", 256, 384))
ROW_B_GLU, ROW_B_CONV, ROWS_B_TOTAL = 640, 704, 768
W_IN_SHARD = 1412
CONV_PAD_ROWS = 16
SMALL = (("norm_mix_g", (1024,)), ("conv_b", (2048,)), ("dt_bias", (16,)), ("a_log", (16,)), ("d_ssd", (16,)),
         ("ssd_norm_g", (1024,)), ("s5_a_re", (32, 64)), ("s5_a_im", (32, 64)), ("s5_log_dt", (32,)),
         ("s5_b_re", (32, 64, 16)), ("s5_b_im", (32, 64, 16)), ("s5_c_re", (32, 16, 64)), ("s5_c_im", (32, 16, 64)),
         ("s5_d", (512,)), ("s5_glu_b", (512,)), ("norm_mlp_g", (1024,)), ("norm_final_g", (1024,)))
SMALL_ROWS = 144
SMALL_COUNT = sum(math.prod(shp) for _, shp in SMALL)
GLU_ROWS = S5_WIDTH * S5_WIDTH // PACK_COLS
CONV_ROWS = CONV_K * CONV_DIM // PACK_COLS
W_IN_PIECES = (("z", 0, 1024), ("xbc", 1024, 2048), ("dt", OFF_DT, 16), ("u5", OFF_U, 512), ("gates", 3600, 2048))


def _pack_small(parts):
    flat = jnp.concatenate([a.astype(F32).reshape(-1) for a in parts])
    return jnp.concatenate([flat, jnp.zeros((SMALL_ROWS * PACK_COLS - flat.shape[0],), F32)]).reshape(SMALL_ROWS, PACK_COLS)


def _unpack_small(pack):
    flat, out, r = pack.reshape(-1), {}, 0
    for name, shp in SMALL:
        n = math.prod(shp)
        out[name] = flat[r:r + n].reshape(shp)
        r += n
    return out


def _late_buffers(d_w_out, d_w_branch, d_glu_w, d_conv_w, w_in_pieces):
    conv4 = d_conv_w.reshape(CONV_K, N_CHIPS, 512).transpose(1, 0, 2).reshape(N_CHIPS, CONV_ROWS // N_CHIPS, PACK_COLS)
    g_b = jnp.concatenate(
        [d_w_out.reshape(N_CHIPS, -1, PACK_COLS), d_w_branch.reshape(N_CHIPS, -1, PACK_COLS),
         d_glu_w.reshape(N_CHIPS, GLU_ROWS // N_CHIPS, PACK_COLS),
         jnp.pad(conv4, ((0, 0), (0, ROWS_B_TOTAL - ROW_B_CONV - CONV_ROWS // N_CHIPS), (0, 0)))], axis=1)
    g_in = jnp.stack([jnp.concatenate(_column_range(w_in_pieces, W_IN_SHARD * s, W_IN_SHARD * (s + 1)), axis=1)
                      for s in range(N_CHIPS)])
    return g_b, g_in


def _column_range(pieces, lo, hi):
    out = []
    for c0, a in pieces:
        a0, a1 = max(lo, c0), min(hi, c0 + a.shape[-1])
        if a0 < a1:
            out.append(a[..., a0 - c0:a1 - c0])
    return out


def kernel(x, norm_mix_g, w_in, conv_w, conv_b, dt_bias, a_log, d_ssd, ssd_norm_g, s5_a_re, s5_a_im, s5_log_dt, s5_b_re, s5_b_im, s5_c_re, s5_c_im, s5_d, s5_glu_w, s5_glu_b, w_branch, w_out, norm_mlp_g, w_mlp_in, w_mlp_out, norm_final_g, loss_target, m_norm_mix_g, m_w_in, m_conv_w, m_conv_b, m_dt_bias, m_a_log, m_d_ssd, m_ssd_norm_g, m_s5_a_re, m_s5_a_im, m_s5_log_dt, m_s5_b_re, m_s5_b_im, m_s5_c_re, m_s5_c_im, m_s5_d, m_s5_glu_w, m_s5_glu_b, m_w_branch, m_w_out, m_norm_mlp_g, m_w_mlp_in, m_w_mlp_out, m_norm_final_g, v_norm_mix_g, v_w_in, v_conv_w, v_conv_b, v_dt_bias, v_a_log, v_d_ssd, v_ssd_norm_g, v_s5_a_re, v_s5_a_im, v_s5_log_dt, v_s5_b_re, v_s5_b_im, v_s5_c_re, v_s5_c_im, v_s5_d, v_s5_glu_w, v_s5_glu_b, v_w_branch, v_w_out, v_norm_mlp_g, v_w_mlp_in, v_w_mlp_out, v_norm_final_g):
    names = ("norm_mix_g", "w_in", "conv_w", "conv_b", "dt_bias", "a_log", "d_ssd", "ssd_norm_g", "s5_a_re", "s5_a_im",
             "s5_log_dt", "s5_b_re", "s5_b_im", "s5_c_re", "s5_c_im", "s5_d", "s5_glu_w", "s5_glu_b", "w_branch", "w_out",
             "norm_mlp_g", "w_mlp_in", "w_mlp_out", "norm_final_g")
    w = dict(zip(names, (norm_mix_g, w_in, conv_w, conv_b, dt_bias, a_log, d_ssd, ssd_norm_g, s5_a_re, s5_a_im, s5_log_dt,
                         s5_b_re, s5_b_im, s5_c_re, s5_c_im, s5_d, s5_glu_w, s5_glu_b, w_branch, w_out, norm_mlp_g,
                         w_mlp_in, w_mlp_out, norm_final_g)))
    m = dict(zip(names, (m_norm_mix_g, m_w_in, m_conv_w, m_conv_b, m_dt_bias, m_a_log, m_d_ssd, m_ssd_norm_g, m_s5_a_re,
                         m_s5_a_im, m_s5_log_dt, m_s5_b_re, m_s5_b_im, m_s5_c_re, m_s5_c_im, m_s5_d, m_s5_glu_w,
                         m_s5_glu_b, m_w_branch, m_w_out, m_norm_mlp_g, m_w_mlp_in, m_w_mlp_out, m_norm_final_g)))
    v = dict(zip(names, (v_norm_mix_g, v_w_in, v_conv_w, v_conv_b, v_dt_bias, v_a_log, v_d_ssd, v_ssd_norm_g, v_s5_a_re,
                         v_s5_a_im, v_s5_log_dt, v_s5_b_re, v_s5_b_im, v_s5_c_re, v_s5_c_im, v_s5_d, v_s5_glu_w,
                         v_s5_glu_b, v_w_branch, v_w_out, v_norm_mlp_g, v_w_mlp_in, v_w_mlp_out, v_norm_final_g)))

    cx, cy, cc = lax.axis_index("x"), lax.axis_index("y"), lax.axis_index("c")
    own = 2 * cx + cy
    place = jnp.stack([own, cc]).astype(jnp.int32)

    src_conv = jnp.concatenate([conv_w, jnp.zeros((CONV_PAD_ROWS - CONV_K, 512), F32)], axis=0)
    all_in, all_conv = _exchange(_Gather([w_in.astype(BF16), src_conv], [ICI_CHUNKS, 1]), "gather_first")
    p = {n: w[n] for n, _ in SMALL}
    p["conv_w"] = jnp.concatenate([all_conv[s, :CONV_K] for s in range(N_CHIPS)], axis=1)
    shards = [(W_IN_SHARD * s, all_in[s]) for s in range(N_CHIPS)]
    p["w_in_perm"] = jnp.concatenate(
        _column_range(shards, 0, OFF_DT) + _column_range(shards, OFF_U, D_IN_PROJ) + _column_range(shards, OFF_DT, OFF_U)
        + [jnp.zeros((D_MODEL, DT_PAD - 16), BF16)], axis=1)

    def late_unpack(gathered):
        all_a, all_glu = gathered
        out = {"w_mlp_in": all_a[:, 0:1024], "s5_glu_w": all_glu.reshape(S5_WIDTH, S5_WIDTH)}
        for n, r0, nr in ROWS_A[1:]:
            out[n] = all_a[:, r0:r0 + nr].reshape(N_CHIPS * nr, PACK_COLS)
        return out

    comm = dict(place=place, late_ks=[ICI_CHUNKS, 1], late_unpack=late_unpack,
                late_srcs=[jnp.concatenate([w[n].astype(BF16) for n, _, _ in ROWS_A], axis=0), s5_glu_w.astype(BF16)])
    loss_part, grad_x, g = _local_step(x[0], loss_target[0], p, comm)

    red_mlp, red_b, red_in = _half_exchange([g["mlp_total"], *g["late_totals"]])
    small_tot = _small_allreduce(_pack_small([g[n] for n, _ in SMALL] + [loss_part.reshape(1)]))
    loss = small_tot.reshape(-1)[SMALL_COUNT]

    grads = _unpack_small(small_tot)
    delta, new_m, new_v = {}, {}, {}
    for n, r0, _ in ROWS_A[:2]:
        delta[n], new_m[n], new_v[n], grads[n] = _adamw(w[n], red_mlp, m[n], v[n], "adamw_" + n, g_row0=r0, with_grad=True)
    for n, r0, _ in ROWS_B:
        delta[n], new_m[n], new_v[n], grads[n] = _adamw(w[n], red_b, m[n], v[n], "adamw_" + n, g_row0=r0, with_grad=True)
    d_t, m_t, v_t, g_t = _adamw(w_in.T, red_in.T, m_w_in.T, v_w_in.T, "adamw_w_in", with_grad=True, col_block=128)
    delta["w_in"], new_m["w_in"], new_v["w_in"], grads["w_in"] = d_t.T, m_t.T, v_t.T, g_t.T
    grads["s5_glu_w"] = red_b[ROW_B_GLU:ROW_B_GLU + GLU_ROWS // N_CHIPS].reshape(S5_WIDTH // N_CHIPS, S5_WIDTH)
    grads["conv_w"] = red_b[ROW_B_CONV:ROW_B_CONV + CONV_ROWS // N_CHIPS].reshape(CONV_K, CONV_DIM // N_CHIPS)
    for n in ("s5_glu_w", "conv_w"):
        delta[n], new_m[n], new_v[n] = _adamw(w[n], grads[n], m[n], v[n], "adamw_" + n)
    ds, ms, vs = _adamw(_pack_small([w[n] for n, _ in SMALL]), small_tot, _pack_small([m[n] for n, _ in SMALL]),
                        _pack_small([v[n] for n, _ in SMALL]), "adamw_small")
    delta.update(_unpack_small(ds))
    new_m.update(_unpack_small(ms))
    new_v.update(_unpack_small(vs))

    return (loss, grad_x[None], *[grads[n] for n in names], *[delta[n] for n in names],
            *[new_m[n] for n in names], *[new_v[n] for n in names])
```

```python
import functools
import math

import jax
import jax.numpy as jnp
from jax import lax
from jax.experimental import pallas as pl
from jax.experimental.pallas import tpu as pltpu

F32 = jnp.float32
BF16 = jnp.bfloat16

D_MODEL = 1024
SSD_INNER = 1024
SSD_HEADS = 16
SSD_HEADDIM = 64
SSD_GROUPS = 4
SSD_HPG = 4
SSD_STATE = 128
SSD_CHUNK = 128
CONV_K = 4
CONV_DIM = 2048
S5_WIDTH = 512
S5_STATES = 2048
S5_BLOCKS = 4
S5_CHUNK = 128
D_FF = 4096
FF_SHARDS = 4
FF_SHARD = D_FF // FF_SHARDS
EPS = 1e-6
P_Z, P_XBC, P_U5, P_G, P_DT, P_END = 0, 1024, 3072, 3584, 5632, 5760
DT_PAD = 128
OFF_DT, OFF_U = 3072, 3088
D_IN_PROJ = 5648

ADAM_LR, ADAM_B1, ADAM_B2, ADAM_EPS, ADAM_WD, ADAM_STEP = 0.001, 0.9, 0.999, 1e-08, 0.01, 10

TOKEN_TILE = 256
VMEM_LIMIT = 56 * 1024 * 1024
HALO = 8
CONV_COLS = 256
CONV_ROWS_BLK = 64
WGRAD_TOKENS = 2048


def _pc(body, **kw):
    return pl.pallas_call(body, **kw)


def _cparams(sem=None):
    return pltpu.CompilerParams(dimension_semantics=sem, vmem_limit_bytes=VMEM_LIMIT)


def _dot(a, b):
    return jnp.dot(a, b, preferred_element_type=F32)


def _dot_nt(a, b):
    return lax.dot_general(a, b, (((1,), (1,)), ((), ())), preferred_element_type=F32)


def _dot_tn(a, b):
    return lax.dot_general(a, b, (((0,), (0,)), ((), ())), preferred_element_type=F32)


def _dot_hi(a, b, dims=(((1,), (0,)), ((), ()))):
    return lax.dot_general(a, b, dims, preferred_element_type=F32, precision=lax.Precision.HIGHEST)


def _split_bf16(x, terms):
    out = []
    for _ in range(terms - 1):
        t = x.astype(BF16)
        out.append(t)
        x = x - t.astype(F32)
    out.append(x.astype(BF16))
    return out


def _dot_split(x, onehots, terms, dims=(((1,), (0,)), ((), ()))):
    acc = None
    for t in _split_bf16(x, terms):
        p = lax.dot_general(t, onehots, dims, preferred_element_type=F32)
        acc = p if acc is None else acc + p
    return acc


def _dot_split_rhs(onehots, x, terms, dims=(((1,), (0,)), ((), ()))):
    acc = None
    for t in _split_bf16(x, terms):
        p = lax.dot_general(onehots, t, dims, preferred_element_type=F32)
        acc = p if acc is None else acc + p
    return acc


def _sigmoid(x):
    return 0.5 * jnp.tanh(0.5 * x) + 0.5


def _softplus(x):
    return jnp.maximum(x, 0.0) + jnp.log(1.0 + jnp.exp(-jnp.abs(x)))


_GELU_C = math.sqrt(2.0 / math.pi)


def _gelu(x):
    return 0.5 * x * (1.0 + jnp.tanh(_GELU_C * (x + 0.044715 * x * x * x)))


def _gelu_grad(x):
    t = jnp.tanh(_GELU_C * (x + 0.044715 * x * x * x))
    return 0.5 * (1.0 + t) + 0.5 * x * (1.0 - t * t) * _GELU_C * (1.0 + 3.0 * 0.044715 * x * x)


def _rms(x):
    r = lax.rsqrt(jnp.mean(x * x, axis=-1, keepdims=True) + EPS)
    return x * r, r


def _rms_bwd(xn, r, dxn):
    return r * (dxn - xn * jnp.mean(dxn * xn, axis=-1, keepdims=True))


def _row_spec(tm, width, col=0):
    return pl.BlockSpec((tm, width), lambda i: (i, col))


def _const_spec(shape):
    nd = len(shape)
    return pl.BlockSpec(shape, lambda i: (0,) * nd)


def _hbm_spec():
    return pl.BlockSpec(memory_space=pl.ANY)


def _inproj_fwd(x, g, wp):
    T = x.shape[0]
    tm = TOKEN_TILE

    def body(x_ref, g_ref, w_hbm, z_ref, xbc_ref, u5_ref, gt_ref, dt_ref, h_ref, w_ref):
        @pl.when(pl.program_id(0) == 0)
        def _():
            pltpu.sync_copy(w_hbm, w_ref)

        xn, _ = _rms(x_ref[...])
        h = (xn * g_ref[...]).astype(BF16)
        h_ref[...] = h
        z_ref[...] = _dot(h, w_ref[:, P_Z:P_XBC])
        xbc_ref[...] = _dot(h, w_ref[:, P_XBC:P_U5])
        u5_ref[...] = _dot(h, w_ref[:, P_U5:P_G])
        gt_ref[...] = _dot(h, w_ref[:, P_G:P_DT])
        dt_ref[...] = _dot(h, w_ref[:, P_DT:P_END])

    widths = (1024, 2048, 512, 2048, DT_PAD)
    return _pc(
        body, name="inproj_fwd", grid=(T // tm,),
        in_specs=[_row_spec(tm, D_MODEL), _const_spec((1, D_MODEL)), _hbm_spec()],
        out_specs=[_row_spec(tm, w) for w in widths] + [_row_spec(tm, D_MODEL)],
        out_shape=[jax.ShapeDtypeStruct((T, w), F32) for w in widths] + [jax.ShapeDtypeStruct((T, D_MODEL), BF16)],
        scratch_shapes=[pltpu.VMEM((D_MODEL, P_END), BF16)],
        compiler_params=_cparams(("arbitrary",)),
    )(x, g, wp)


def _inproj_bwd(x, dx1, dz, dxbc, du5, dgt, ddt, g, wp, rider=None):
    T = x.shape[0]
    tm = TOKEN_TILE

    def body(x_ref, dx1_ref, dz_ref, dxbc_ref, du5_ref, dgt_ref, ddt_ref, g_ref, w_hbm, dx_ref, dg_ref, w_ref):
        @pl.when(pl.program_id(0) == 0)
        def _():
            pltpu.sync_copy(w_hbm, w_ref)
            dg_ref[...] = jnp.zeros_like(dg_ref)

        xn, r = _rms(x_ref[...])
        gv = g_ref[...]
        dh = _dot_nt(dz_ref[...].astype(BF16), w_ref[:, P_Z:P_XBC])
        dh += _dot_nt(dxbc_ref[...].astype(BF16), w_ref[:, P_XBC:P_U5])
        dh += _dot_nt(du5_ref[...].astype(BF16), w_ref[:, P_U5:P_G])
        dh += _dot_nt(dgt_ref[...].astype(BF16), w_ref[:, P_G:P_DT])
        dh += _dot_nt(ddt_ref[...].astype(BF16), w_ref[:, P_DT:P_END])
        dg_ref[...] += jnp.sum(dh * xn, axis=0, keepdims=True)
        dx_ref[...] = dx1_ref[...] + _rms_bwd(xn, r, dh * gv)

    return _call(
        body, rider, name="inproj_bwd", grid=(T // tm,),
        in_specs=[_row_spec(tm, 1024), _row_spec(tm, 1024), _row_spec(tm, 1024), _row_spec(tm, 2048),
                  _row_spec(tm, 512), _row_spec(tm, 2048), _row_spec(tm, DT_PAD), _const_spec((1, 1024)), _hbm_spec()],
        out_specs=[_row_spec(tm, 1024), _const_spec((1, 1024))],
        out_shape=[jax.ShapeDtypeStruct((T, 1024), F32), jax.ShapeDtypeStruct((1, 1024), F32)],
        scratch_shapes=[pltpu.VMEM((D_MODEL, P_END), BF16)],
        compiler_params=_cparams(("arbitrary",)),
    )(x, dx1, dz, dxbc, du5, dgt, ddt, g, wp)


def _conv_fwd(xbc_raw, dt_raw, conv_w, conv_b, dt_bias):
    T = xbc_raw.shape[0]
    tm = TOKEN_TILE

    def body(u_ref, dtr_ref, w_ref, b_ref, db_ref, act_ref, dt_ref, ext_ref):
        @pl.when(pl.program_id(0) == 0)
        def _():
            ext_ref[0:HALO, :] = jnp.zeros((HALO, CONV_DIM), F32)

        ext_ref[HALO:, :] = u_ref[...]
        for c0 in range(0, CONV_DIM, CONV_COLS):
            cols = slice(c0, c0 + CONV_COLS)
            taps = [w_ref[k:k + 1, cols] for k in range(CONV_K)]
            bias = b_ref[:, cols]
            for r0 in range(0, tm, CONV_ROWS_BLK):
                y = bias + taps[0] * ext_ref[pl.ds(HALO - (CONV_K - 1) + r0, CONV_ROWS_BLK), cols]
                for k in range(1, CONV_K):
                    y += taps[k] * ext_ref[pl.ds(HALO - (CONV_K - 1) + k + r0, CONV_ROWS_BLK), cols]
                act_ref[r0:r0 + CONV_ROWS_BLK, cols] = y * _sigmoid(y)
        ext_ref[0:HALO, :] = u_ref[tm - HALO:tm, :]
        dt_ref[...] = _softplus(dtr_ref[...] + db_ref[...])

    return _pc(
        body, name="conv_fwd", grid=(T // tm,),
        in_specs=[_row_spec(tm, CONV_DIM), _row_spec(tm, DT_PAD), _const_spec((CONV_K, CONV_DIM)),
                  _const_spec((1, CONV_DIM)), _const_spec((1, DT_PAD))],
        out_specs=[_row_spec(tm, CONV_DIM), _row_spec(tm, DT_PAD)],
        out_shape=[jax.ShapeDtypeStruct((T, CONV_DIM), F32), jax.ShapeDtypeStruct((T, DT_PAD), F32)],
        scratch_shapes=[pltpu.VMEM((tm + HALO, CONV_DIM), F32)],
        compiler_params=_cparams(("arbitrary",)),
    )(xbc_raw, dt_raw, conv_w, conv_b, dt_bias)


def _conv_bwd(xbc_raw, dt_raw, dxs_a, dxs_b, dB, dC, ddt, conv_w, conv_b, dt_bias):
    T = xbc_raw.shape[0]
    tm = TOKEN_TILE
    n = T // tm
    hb = tm // HALO

    def rev(width):
        return pl.BlockSpec((tm, width), lambda i: (n - 1 - i, 0))

    def body(u_ref, up_ref, dtr_ref, dxa_ref, dxb_ref, dB_ref, dC_ref, ddt_ref, w_ref, b_ref, db_ref,
             du_ref, ddtr_ref, dw_ref, dcb_ref, ddb_ref, ext_ref, dye_ref):
        i = pl.program_id(0)

        @pl.when(i == 0)
        def _():
            dye_ref[tm:, :] = jnp.zeros((HALO, CONV_DIM), F32)
            dw_ref[...] = jnp.zeros_like(dw_ref)
            dcb_ref[...] = jnp.zeros_like(dcb_ref)
            ddb_ref[...] = jnp.zeros_like(ddb_ref)

        first = (i == n - 1).astype(F32)
        ext_ref[0:HALO, :] = up_ref[...] * (1.0 - first)
        ext_ref[HALO:, :] = u_ref[...]
        for c0 in range(0, CONV_DIM, CONV_COLS):
            cols = slice(c0, c0 + CONV_COLS)
            taps = [w_ref[k:k + 1, cols] for k in range(CONV_K)]
            bias = b_ref[:, cols]
            acc_b = jnp.zeros((HALO, CONV_COLS), F32)
            acc_w = [jnp.zeros((HALO, CONV_COLS), F32) for _ in range(CONV_K)]
            for r0 in range(0, tm, CONV_ROWS_BLK):
                rows = slice(r0, r0 + CONV_ROWS_BLK)
                us = [ext_ref[pl.ds(HALO - (CONV_K - 1) + k + r0, CONV_ROWS_BLK), cols] for k in range(CONV_K)]
                y = bias + taps[0] * us[0]
                for k in range(1, CONV_K):
                    y += taps[k] * us[k]
                s = _sigmoid(y)
                if c0 < SSD_INNER:
                    dact = dxa_ref[rows, cols] + dxb_ref[rows, cols]
                elif c0 < SSD_INNER + 512:
                    dact = dB_ref[rows, c0 - SSD_INNER:c0 - SSD_INNER + CONV_COLS]
                else:
                    dact = dC_ref[rows, c0 - SSD_INNER - 512:c0 - SSD_INNER - 512 + CONV_COLS]
                dy = dact * (s * (1.0 + y * (1.0 - s)))
                dye_ref[rows, cols] = dy
                acc_b += jnp.sum(dy.reshape(CONV_ROWS_BLK // HALO, HALO, CONV_COLS), axis=0)
                for k in range(CONV_K):
                    acc_w[k] += jnp.sum((dy * us[k]).reshape(CONV_ROWS_BLK // HALO, HALO, CONV_COLS), axis=0)
            dcb_ref[:, cols] += jnp.sum(acc_b, axis=0, keepdims=True)
            for k in range(CONV_K):
                dw_ref[k:k + 1, cols] += jnp.sum(acc_w[k], axis=0, keepdims=True)
        for c0 in range(0, CONV_DIM, CONV_COLS):
            cols = slice(c0, c0 + CONV_COLS)
            taps = [w_ref[k:k + 1, cols] for k in range(CONV_K)]
            for r0 in range(0, tm, CONV_ROWS_BLK):
                du = taps[0] * dye_ref[pl.ds(CONV_K - 1 + r0, CONV_ROWS_BLK), cols]
                for k in range(1, CONV_K):
                    du += taps[k] * dye_ref[pl.ds(CONV_K - 1 - k + r0, CONV_ROWS_BLK), cols]
                du_ref[r0:r0 + CONV_ROWS_BLK, cols] = du.astype(BF16)
        dye_ref[tm:, :] = dye_ref[0:HALO, :]
        sg = _sigmoid(dtr_ref[...] + db_ref[...])
        ddtr = ddt_ref[...] * sg
        ddtr_ref[...] = ddtr.astype(BF16)
        ddb_ref[...] += jnp.sum(ddtr, axis=0, keepdims=True)

    prev_spec = pl.BlockSpec((HALO, CONV_DIM), lambda i: (jnp.maximum((n - 1 - i) * hb - 1, 0), 0))
    return _pc(
        body, name="conv_bwd", grid=(n,),
        in_specs=[rev(CONV_DIM), prev_spec, rev(DT_PAD), rev(1024), rev(1024), rev(512), rev(512), rev(DT_PAD),
                  _const_spec((CONV_K, CONV_DIM)), _const_spec((1, CONV_DIM)), _const_spec((1, DT_PAD))],
        out_specs=[rev(CONV_DIM), rev(DT_PAD), _const_spec((HALO, CONV_DIM)), _const_spec((1, CONV_DIM)),
                   _const_spec((1, DT_PAD))],
        out_shape=[jax.ShapeDtypeStruct((T, CONV_DIM), BF16), jax.ShapeDtypeStruct((T, DT_PAD), BF16),
                   jax.ShapeDtypeStruct((HALO, CONV_DIM), F32), jax.ShapeDtypeStruct((1, CONV_DIM), F32),
                   jax.ShapeDtypeStruct((1, DT_PAD), F32)],
        scratch_shapes=[pltpu.VMEM((tm + HALO, CONV_DIM), F32), pltpu.VMEM((tm + HALO, CONV_DIM), F32)],
        compiler_params=_cparams(("arbitrary",)),
    )(xbc_raw, xbc_raw, dt_raw, dxs_a, dxs_b, dB, dC, ddt, conv_w, conv_b, dt_bias)


GROUP_LANES = SSD_HPG * SSD_HEADDIM


def _ssd_expanders():
    head = jnp.arange(DT_PAD)[:, None]
    to_wide = (jnp.arange(SSD_INNER)[None, :] // SSD_HEADDIM == head).astype(BF16)
    to_cols = (jnp.arange(SSD_HEADS * SSD_CHUNK)[None, :] // SSD_CHUNK == head).astype(BF16)
    return to_wide, to_wide.T, to_cols


def _ssd_prep(dt_ref, alog_ref, wide_ref, cols_ref):
    q = SSD_CHUNK
    a = -jnp.exp(alog_ref[...])
    dtv = dt_ref[...]
    la = dtv * a
    row = lax.broadcasted_iota(jnp.int32, (q, q), 0)
    col = lax.broadcasted_iota(jnp.int32, (q, q), 1)
    tri = (col <= row).astype(BF16)
    cum = _dot_split_rhs(tri, la, 3)
    cum_t = _dot_split(la, tri, 3, (((0,), (1,)), ((), ())))
    dtw = _dot_split(dtv, wide_ref[...], 2)
    cumw = _dot_split(cum, wide_ref[...], 3)
    segcol = _dot_split(cum, cols_ref[...], 3)
    return a, dtv, row, col, tri, cum_t, dtw, cumw, segcol


def _decay(segcol, cum_t, h, keep):
    return jnp.where(keep, jnp.exp(jnp.minimum(segcol[:, 128 * h:128 * h + 128] - cum_t[h:h + 1, :], 0.0)), 0.0)


def _decay_t(segcol, cum_t, h, keep_t):
    return jnp.where(keep_t, jnp.exp(jnp.minimum(cum_t[h:h + 1, :] - segcol[:, 128 * h:128 * h + 128], 0.0)), 0.0)


def _ssd_fwd(xbc_act, dt, alog):
    T = xbc_act.shape[0]
    q = SSD_CHUNK
    nc = T // q
    to_wide, _, to_cols = _ssd_expanders()

    def body(xbc_ref, dt_ref, alog_ref, wide_ref, cols_ref, y_ref, sp_ref, st_ref, xd_ref, xde_ref):
        @pl.when(pl.program_id(0) == 0)
        def _():
            st_ref[...] = jnp.zeros_like(st_ref)

        a, dtv, row, col, tri, cum_t, dtw, cumw, segcol = _ssd_prep(dt_ref, alog_ref, wide_ref, cols_ref)
        clw = cumw[q - 1:q, :]
        ecw = jnp.exp(cumw)
        xd = xbc_ref[:, 0:SSD_INNER] * dtw
        xd_ref[...] = xd.astype(BF16)
        xde_ref[...] = (xd * jnp.exp(clw - cumw)).astype(BF16)
        cdw = jnp.exp(clw)
        keep = col <= row
        sp_ref[0] = st_ref[...]
        for g in range(SSD_GROUPS):
            gl = slice(GROUP_LANES * g, GROUP_LANES * (g + 1))
            bb = xbc_ref[:, 1024 + 128 * g:1152 + 128 * g].astype(BF16)
            cb = xbc_ref[:, 1536 + 128 * g:1664 + 128 * g].astype(BF16)
            gm = _dot_nt(cb, bb)
            stp = st_ref[g]
            yoff = _dot(cb, stp.astype(BF16)) * ecw[:, gl]
            for r in range(SSD_HPG):
                h = SSD_HPG * g + r
                m = (gm * _decay(segcol, cum_t, h, keep)).astype(BF16)
                y_ref[:, 64 * h:64 * h + 64] = _dot(m, xd_ref[:, 64 * h:64 * h + 64]) + yoff[:, 64 * r:64 * r + 64]
            st_ref[g] = stp * cdw[:, gl] + _dot_tn(bb, xde_ref[:, gl])

    return _pc(
        body, name="ssd_fwd", grid=(nc,),
        in_specs=[_row_spec(q, CONV_DIM), _row_spec(q, DT_PAD), _const_spec((1, DT_PAD)),
                  _const_spec(to_wide.shape), _const_spec(to_cols.shape)],
        out_specs=[_row_spec(q, SSD_INNER),
                   pl.BlockSpec((1, SSD_GROUPS, SSD_STATE, GROUP_LANES), lambda i: (i, 0, 0, 0))],
        out_shape=[jax.ShapeDtypeStruct((T, SSD_INNER), F32),
                   jax.ShapeDtypeStruct((nc, SSD_GROUPS, SSD_STATE, GROUP_LANES), F32)],
        scratch_shapes=[pltpu.VMEM((SSD_GROUPS, SSD_STATE, GROUP_LANES), F32), pltpu.VMEM((q, SSD_INNER), BF16),
                        pltpu.VMEM((q, SSD_INNER), BF16)],
        compiler_params=_cparams(("arbitrary",)),
    )(xbc_act, dt, alog, to_wide, to_cols)


def _ssd_bwd(xbc_act, dt, alog, sprev, dy):
    T = xbc_act.shape[0]
    q = SSD_CHUNK
    nc = T // q
    to_wide, to_heads, to_cols = _ssd_expanders()

    def rev(width):
        return pl.BlockSpec((q, width), lambda i: (nc - 1 - i, 0))

    def body(xbc_ref, dt_ref, alog_ref, sp_ref, dy_ref, wide_ref, heads_ref, cols_ref,
             dxs_ref, dB_ref, dC_ref, ddt_ref, dalog_ref, ds_ref, xd_ref, dxd_ref):
        i = pl.program_id(0)

        @pl.when(i == 0)
        def _():
            ds_ref[...] = jnp.zeros_like(ds_ref)
            dalog_ref[...] = jnp.zeros_like(dalog_ref)

        a, dtv, row, col, tri, cum_t, dtw, cumw, segcol = _ssd_prep(dt_ref, alog_ref, wide_ref, cols_ref)
        clw = cumw[q - 1:q, :]
        ecw = jnp.exp(cumw)
        dew = jnp.exp(clw - cumw)
        cdw = jnp.exp(clw)
        xs = xbc_ref[:, 0:SSD_INNER]
        xd = xs * dtw
        xd_ref[...] = xd.astype(BF16)
        dyv = dy_ref[...]
        dye = (dyv * ecw).astype(BF16)
        xde = (xd * dew).astype(BF16)
        keep = col <= row
        keep_t = col >= row
        rows_k = lax.broadcasted_iota(jnp.int32, (SSD_HPG * q, DT_PAD), 0) // q
        lanes_k = lax.broadcasted_iota(jnp.int32, (SSD_HPG * q, DT_PAD), 1)
        dcw_parts = []
        dcum = jnp.zeros((q, DT_PAD), F32)
        for g in range(SSD_GROUPS):
            gl = slice(GROUP_LANES * g, GROUP_LANES * (g + 1))
            bb = xbc_ref[:, 1024 + 128 * g:1152 + 128 * g].astype(BF16)
            cb = xbc_ref[:, 1536 + 128 * g:1664 + 128 * g].astype(BF16)
            gm = _dot_nt(cb, bb)
            gmt = _dot_nt(bb, cb)
            stp = sp_ref[0, g]
            dst = ds_ref[g]
            stpb = stp.astype(BF16)
            dstb = dst.astype(BF16)
            yoff = _dot(cb, stpb) * ecw[:, gl]
            dcg = _dot_nt(dye[:, gl], stpb)
            ds_ref[g] = dst * cdw[:, gl] + _dot_tn(cb, dye[:, gl])
            dlast = jnp.sum(dst * stp, axis=0, keepdims=True) * cdw[:, gl]
            dbg = _dot_nt(xde[:, gl], dstb)
            w = _dot(bb, dstb) * dew[:, gl]
            wx = w * xd[:, gl]
            dlast = dlast + jnp.sum(wx, axis=0, keepdims=True)
            dcw_parts.append(dyv[:, gl] * yoff - wx
                             + jnp.where(lax.broadcasted_iota(jnp.int32, (q, 1), 0) == q - 1, dlast, 0.0))
            dgm = jnp.zeros((q, q), F32)
            diag = []
            for r in range(SSD_HPG):
                h = SSD_HPG * g + r
                hl = slice(64 * h, 64 * h + 64)
                dyb = dy_ref[:, hl].astype(BF16)
                xdh = xd_ref[:, hl]
                dm = _dot_nt(dyb, xdh)
                dmt = _dot_nt(xdh, dyb)
                dec = _decay(segcol, cum_t, h, keep)
                mt = gmt * _decay_t(segcol, cum_t, h, keep_t)
                dgm += dm * dec
                diag.append(dm * (gm * dec) - dmt * mt)
                dxd_ref[:, hl] = _dot(mt.astype(BF16), dyb) + w[:, 64 * r:64 * r + 64]
            onehots = (lanes_k == SSD_HPG * g + rows_k).astype(BF16)
            dcum += _dot_split(jnp.concatenate(diag, axis=1), onehots, 2)
            dgb = dgm.astype(BF16)
            dC_ref[:, 128 * g:128 * g + 128] = dcg + _dot(dgb, bb)
            dB_ref[:, 128 * g:128 * g + 128] = dbg + _dot_tn(dgb, cb)
        dxd = dxd_ref[...]
        dxs_ref[...] = dxd * dtw
        dcum += _dot_split(jnp.concatenate(dcw_parts, axis=1), heads_ref[...], 2)
        dla = _dot_split_rhs(tri, dcum, 3, (((0,), (0,)), ((), ())))
        ddt_ref[...] = _dot_split(xs * dxd, heads_ref[...], 2) + dla * a
        dalog_ref[...] += jnp.sum(dla * dtv, axis=0, keepdims=True)

        @pl.when(i == nc - 1)
        def _():
            dalog_ref[...] = dalog_ref[...] * a

    st_spec = pl.BlockSpec((1, SSD_GROUPS, SSD_STATE, GROUP_LANES), lambda i: (nc - 1 - i, 0, 0, 0))
    return _pc(
        body, name="ssd_bwd", grid=(nc,),
        in_specs=[rev(CONV_DIM), rev(DT_PAD), _const_spec((1, DT_PAD)), st_spec, rev(SSD_INNER),
                  _const_spec(to_wide.shape), _const_spec(to_heads.shape), _const_spec(to_cols.shape)],
        out_specs=[rev(SSD_INNER), rev(512), rev(512), rev(DT_PAD), _const_spec((1, DT_PAD))],
        out_shape=[jax.ShapeDtypeStruct((T, SSD_INNER), F32), jax.ShapeDtypeStruct((T, 512), F32),
                   jax.ShapeDtypeStruct((T, 512), F32), jax.ShapeDtypeStruct((T, DT_PAD), F32),
                   jax.ShapeDtypeStruct((1, DT_PAD), F32)],
        scratch_shapes=[pltpu.VMEM((SSD_GROUPS, SSD_STATE, GROUP_LANES), F32), pltpu.VMEM((q, SSD_INNER), BF16),
                        pltpu.VMEM((q, SSD_INNER), F32)],
        compiler_params=_cparams(("arbitrary",)),
    )(xbc_act, dt, alog, sprev, dy, to_wide, to_heads, to_cols)


def _s5_disc_vals(a_re, a_im, log_dt, b_re, b_im):
    dt = jnp.exp(log_dt)
    mag = jnp.exp(a_re * dt)
    ab_re = mag * jnp.cos(a_im * dt)
    ab_im = mag * jnp.sin(a_im * dt)
    den = a_re * a_re + a_im * a_im
    nr = ab_re - 1.0
    ni = ab_im
    coef_re = (nr * a_re + ni * a_im) / den
    coef_im = (ni * a_re - nr * a_im) / den
    bb_re = coef_re * b_re - coef_im * b_im
    bb_im = coef_re * b_im + coef_im * b_re
    return ab_re, ab_im, bb_re, bb_im


def _s5_disc(a_re, a_im, log_dt, b_re, b_im):
    def body(ar, ai, ld, br, bi, o1, o2, o3, o4):
        o1[...], o2[...], o3[...], o4[...] = _s5_disc_vals(ar[...], ai[...], ld[...], br[...], bi[...])

    return _pc(
        body, name="s5_disc",
        out_shape=[jax.ShapeDtypeStruct((S5_STATES, 1), F32), jax.ShapeDtypeStruct((S5_STATES, 1), F32),
                   jax.ShapeDtypeStruct((S5_STATES, 16), F32), jax.ShapeDtypeStruct((S5_STATES, 16), F32)],
    )(a_re, a_im, log_dt, b_re, b_im)


def _s5_disc_bwd(a_re, a_im, log_dt, b_re, b_im, d_ab_re, d_ab_im, d_bb_re, d_bb_im):
    def body(ar, ai, ld, br, bi, g1, g2, g3, g4, o1, o2, o3, o4, o5):
        _, vjp = jax.vjp(_s5_disc_vals, ar[...], ai[...], ld[...], br[...], bi[...])
        d1, d2, d3, d4, d5 = vjp((g1[...], g2[...], g3[...], g4[...]))
        o1[...] = d1
        o2[...] = d2
        grp = lax.broadcasted_iota(jnp.int32, (32, S5_STATES), 0)
        st = lax.broadcasted_iota(jnp.int32, (32, S5_STATES), 1)
        sel = (st // 64 == grp).astype(F32)
        o3[...] = _dot_hi(sel, d3)
        o4[...] = d4
        o5[...] = d5

    return _pc(
        body, name="s5_disc_bwd",
        out_shape=[jax.ShapeDtypeStruct((S5_STATES, 1), F32), jax.ShapeDtypeStruct((S5_STATES, 1), F32),
                   jax.ShapeDtypeStruct((32, 1), F32),
                   jax.ShapeDtypeStruct((S5_STATES, 16), F32), jax.ShapeDtypeStruct((S5_STATES, 16), F32)],
    )(a_re, a_im, log_dt, b_re, b_im, d_ab_re, d_ab_im, d_bb_re, d_bb_im)


def _cmul_add(xr, xi, pr, pi, yr, yi):
    return xr + pr * yr - pi * yi, xi + pr * yi + pi * yr


def _powers(ar, ai, n):
    out = [(ar, ai)]
    for _ in range(n - 1):
        pr, pi = out[-1]
        out.append((pr * pr - pi * pi, 2.0 * pr * pi))
    return out


_BW = S5_STATES // S5_BLOCKS
_BI = S5_WIDTH // S5_BLOCKS
SUB = 8
S5_ROWS = S5_CHUNK // SUB


S5_TAB_ROWS = 8 * SUB


def _scan8(br, bi, tab_ref, reverse):
    for level, k in enumerate((1, 2, 4)):
        r0 = 2 * SUB * (level + 1)
        shift = SUB - k if reverse else k
        br, bi = _cmul_add(br, bi, tab_ref[r0:r0 + SUB, :], tab_ref[r0 + SUB:r0 + 2 * SUB, :],
                           pltpu.roll(br, shift, 0), pltpu.roll(bi, shift, 0))
    return br, bi


def _s5_tables(ab_ref, tab_ref, reverse):
    rowin = lax.broadcasted_iota(jnp.int32, (SUB, 1), 0)
    ar = ab_ref[0:1, :]
    ai = -ab_ref[1:2, :] if reverse else ab_ref[1:2, :]
    zero = jnp.zeros((SUB, S5_STATES), F32)
    for level, (pr, pi) in enumerate(_powers(ar, ai, 3)):
        k = 2 ** level
        keep = (rowin < SUB - k) if reverse else (rowin >= k)
        r0 = 2 * SUB * (level + 1)
        tab_ref[r0:r0 + SUB, :] = jnp.where(keep, pr, 0.0) + zero
        tab_ref[r0 + SUB:r0 + 2 * SUB, :] = jnp.where(keep, pi, 0.0) + zero
    hit = rowin == (SUB - 1 if reverse else 0)
    pr, pi = _scan8(jnp.where(hit, ar, 0.0) + zero, jnp.where(hit, ai, 0.0) + zero, tab_ref, reverse)
    tab_ref[0:SUB, :] = pr
    tab_ref[SUB:2 * SUB, :] = pi


def _s5_fwd(u5, wb4, wc4, ab, dvec, rider=None):
    T = u5.shape[0]
    q = S5_CHUNK
    nc = T // q

    def body(u_ref, wb_ref, wc_ref, ab_ref, d_ref, y_ref, sp_ref, carry_ref, tab_ref, sr_ref, si_ref):
        i = pl.program_id(0)
        rowin = lax.broadcasted_iota(jnp.int32, (SUB, 1), 0)

        @pl.when(i == 0)
        def _():
            carry_ref[...] = jnp.zeros_like(carry_ref)
            _s5_tables(ab_ref, tab_ref, False)

        sp_ref[0] = carry_ref[...]
        for j in range(S5_BLOCKS):
            bu = _dot(u_ref[:, _BI * j:_BI * (j + 1)].astype(BF16), wb_ref[j])
            sr_ref[:, :, _BW * j:_BW * (j + 1)] = bu[:, :_BW].reshape(S5_ROWS, SUB, _BW)
            si_ref[:, :, _BW * j:_BW * (j + 1)] = bu[:, _BW:].reshape(S5_ROWS, SUB, _BW)
        tr, ti = tab_ref[0:SUB, :], tab_ref[SUB:2 * SUB, :]
        cr, ci = carry_ref[0:1, :], carry_ref[1:2, :]
        for k in range(S5_ROWS):
            sr, si = _scan8(sr_ref[k], si_ref[k], tab_ref, False)
            sr, si = _cmul_add(sr, si, tr, ti, cr, ci)
            sr_ref[k] = sr
            si_ref[k] = si
            cr, ci = sr[SUB - 1:SUB, :], si[SUB - 1:SUB, :]
        carry_ref[0:1, :] = cr
        carry_ref[1:2, :] = ci
        for j in range(S5_BLOCKS):
            sl = slice(_BW * j, _BW * (j + 1))
            ul = slice(_BI * j, _BI * (j + 1))
            s = jnp.concatenate([sr_ref[:, :, sl].reshape(q, _BW), si_ref[:, :, sl].reshape(q, _BW)], axis=1).astype(BF16)
            y_ref[:, ul] = _dot(s, wc_ref[j]) + d_ref[:, ul] * u_ref[:, ul]

    return _call(
        body, rider, name="s5_fwd", grid=(nc,),
        in_specs=[_row_spec(q, S5_WIDTH), _const_spec((S5_BLOCKS, _BI, 2 * _BW)), _const_spec((S5_BLOCKS, 2 * _BW, _BI)),
                  _const_spec((8, S5_STATES)), _const_spec((1, S5_WIDTH))],
        out_specs=[_row_spec(q, S5_WIDTH), pl.BlockSpec((1, 8, S5_STATES), lambda i: (i, 0, 0))],
        out_shape=[jax.ShapeDtypeStruct((T, S5_WIDTH), F32), jax.ShapeDtypeStruct((nc, 8, S5_STATES), F32)],
        scratch_shapes=[pltpu.VMEM((8, S5_STATES), F32), pltpu.VMEM((S5_TAB_ROWS, S5_STATES), F32),
                        pltpu.VMEM((S5_ROWS, SUB, S5_STATES), F32), pltpu.VMEM((S5_ROWS, SUB, S5_STATES), F32)],
        compiler_params=_cparams(("arbitrary",)),
    )(u5, wb4, wc4, ab, dvec)


def _s5_bwd(u5, dy5, wb4, wc4, ab, dvec, sprev, rider=None):
    T = u5.shape[0]
    q = S5_CHUNK
    nc = T // q

    def rev(width):
        return pl.BlockSpec((q, width), lambda i: (nc - 1 - i, 0))

    def body(u_ref, dy_ref, wb_ref, wc_ref, ab_ref, d_ref, sp_ref, du_ref, dwb_ref, dwc_ref, dab_ref, dd_ref,
             carry_ref, tab_ref, rtab_ref, sr_ref, si_ref, lr_ref, li_ref):
        i = pl.program_id(0)
        rowin = lax.broadcasted_iota(jnp.int32, (SUB, 1), 0)

        @pl.when(i == 0)
        def _():
            carry_ref[...] = jnp.zeros_like(carry_ref)
            dwb_ref[...] = jnp.zeros_like(dwb_ref)
            dwc_ref[...] = jnp.zeros_like(dwc_ref)
            dab_ref[...] = jnp.zeros_like(dab_ref)
            dd_ref[...] = jnp.zeros_like(dd_ref)
            _s5_tables(ab_ref, tab_ref, False)
            _s5_tables(ab_ref, rtab_ref, True)

        for j in range(S5_BLOCKS):
            sl = slice(_BW * j, _BW * (j + 1))
            ul = slice(_BI * j, _BI * (j + 1))
            bu = _dot(u_ref[:, ul].astype(BF16), wb_ref[j])
            sr_ref[:, :, sl] = bu[:, :_BW].reshape(S5_ROWS, SUB, _BW)
            si_ref[:, :, sl] = bu[:, _BW:].reshape(S5_ROWS, SUB, _BW)
            ds = _dot_nt(dy_ref[:, ul].astype(BF16), wc_ref[j])
            lr_ref[:, :, sl] = ds[:, :_BW].reshape(S5_ROWS, SUB, _BW)
            li_ref[:, :, sl] = ds[:, _BW:].reshape(S5_ROWS, SUB, _BW)
        ar, ai = ab_ref[0:1, :], ab_ref[1:2, :]
        tr, ti = tab_ref[0:SUB, :], tab_ref[SUB:2 * SUB, :]
        cr, ci = sp_ref[0, 0:1, :], sp_ref[0, 1:2, :]
        for k in range(S5_ROWS):
            sr, si = _scan8(sr_ref[k], si_ref[k], tab_ref, False)
            sr, si = _cmul_add(sr, si, tr, ti, cr, ci)
            sr_ref[k] = sr
            si_ref[k] = si
            cr, ci = sr[SUB - 1:SUB, :], si[SUB - 1:SUB, :]
        tr, ti = rtab_ref[0:SUB, :], rtab_ref[SUB:2 * SUB, :]
        cr, ci = carry_ref[0:1, :], carry_ref[1:2, :]
        acc_r = jnp.zeros((SUB, S5_STATES), F32)
        acc_i = jnp.zeros((SUB, S5_STATES), F32)
        for k in reversed(range(S5_ROWS)):
            lr, li = _scan8(lr_ref[k], li_ref[k], rtab_ref, True)
            lr, li = _cmul_add(lr, li, tr, ti, cr, ci)
            lr_ref[k] = lr
            li_ref[k] = li
            cr, ci = lr[0:1, :], li[0:1, :]
            if k > 0:
                before_r, before_i = sr_ref[k - 1, SUB - 1:SUB, :], si_ref[k - 1, SUB - 1:SUB, :]
            else:
                before_r, before_i = sp_ref[0, 0:1, :], sp_ref[0, 1:2, :]
            keep = rowin >= 1
            pr = jnp.where(keep, pltpu.roll(sr_ref[k], 1, 0), before_r)
            pi = jnp.where(keep, pltpu.roll(si_ref[k], 1, 0), before_i)
            acc_r += lr * pr + li * pi
            acc_i += li * pr - lr * pi
        carry_ref[0:1, :] = cr
        carry_ref[1:2, :] = ci
        dab_ref[0:1, :] += jnp.sum(acc_r, axis=0, keepdims=True)
        dab_ref[1:2, :] += jnp.sum(acc_i, axis=0, keepdims=True)
        for j in range(S5_BLOCKS):
            sl = slice(_BW * j, _BW * (j + 1))
            ul = slice(_BI * j, _BI * (j + 1))
            u = u_ref[:, ul]
            dy = dy_ref[:, ul]
            dyb = dy.astype(BF16)
            lam = jnp.concatenate([lr_ref[:, :, sl].reshape(q, _BW), li_ref[:, :, sl].reshape(q, _BW)], axis=1).astype(BF16)
            s = jnp.concatenate([sr_ref[:, :, sl].reshape(q, _BW), si_ref[:, :, sl].reshape(q, _BW)], axis=1).astype(BF16)
            du_ref[:, ul] = (_dot_nt(lam, wb_ref[j]) + d_ref[:, ul] * dy).astype(BF16)
            dwb_ref[j] += _dot_tn(u.astype(BF16), lam)
            dwc_ref[j] += _dot_tn(s, dyb)
            dd_ref[:, ul] += jnp.sum(dy * u, axis=0, keepdims=True)

    big = pltpu.VMEM((S5_ROWS, SUB, S5_STATES), F32)
    return _call(
        body, rider, name="s5_bwd", grid=(nc,),
        in_specs=[rev(S5_WIDTH), rev(S5_WIDTH), _const_spec((S5_BLOCKS, _BI, 2 * _BW)), _const_spec((S5_BLOCKS, 2 * _BW, _BI)),
                  _const_spec((8, S5_STATES)), _const_spec((1, S5_WIDTH)),
                  pl.BlockSpec((1, 8, S5_STATES), lambda i: (nc - 1 - i, 0, 0))],
        out_specs=[rev(S5_WIDTH), _const_spec((S5_BLOCKS, _BI, 2 * _BW)), _const_spec((S5_BLOCKS, 2 * _BW, _BI)),
                   _const_spec((8, S5_STATES)), _const_spec((1, S5_WIDTH))],
        out_shape=[jax.ShapeDtypeStruct((T, S5_WIDTH), BF16), jax.ShapeDtypeStruct((S5_BLOCKS, _BI, 2 * _BW), F32),
                   jax.ShapeDtypeStruct((S5_BLOCKS, 2 * _BW, _BI), F32), jax.ShapeDtypeStruct((8, S5_STATES), F32),
                   jax.ShapeDtypeStruct((1, S5_WIDTH), F32)],
        scratch_shapes=[pltpu.VMEM((8, S5_STATES), F32), pltpu.VMEM((S5_TAB_ROWS, S5_STATES), F32),
                        pltpu.VMEM((S5_TAB_ROWS, S5_STATES), F32), big, big, big, big],
        compiler_params=_cparams(("arbitrary",)),
    )(u5, dy5, wb4, wc4, ab, dvec, sprev)


def _merge_vals(ys, xs, z, y5, gates, dvec, gssd, glu_w, glu_b, wbr):
    sz = _sigmoid(z)
    qv = ys + dvec * xs
    pre = qv * (z * sz)
    yn, rs = [], []
    for gi in range(SSD_GROUPS):
        p, r = _rms(pre[:, 256 * gi:256 * (gi + 1)])
        yn.append(p)
        rs.append(r)
    yn = jnp.concatenate(yn, axis=1)
    ya = yn * gssd
    gel = _gelu(y5)
    sg = _sigmoid(_dot(gel.astype(BF16), glu_w) + glu_b)
    yb = gel * sg
    pa = _dot(ya.astype(BF16), wbr[0:SSD_INNER, :])
    pb = _dot(yb.astype(BF16), wbr[SSD_INNER:, :])
    s0 = _sigmoid(gates[:, :D_MODEL])
    s1 = _sigmoid(gates[:, D_MODEL:])
    merged = s0 * pa + s1 * pb
    return dict(sz=sz, qv=qv, yn=yn, rs=rs, ya=ya, gel=gel, sg=sg, yb=yb, pa=pa, pb=pb, s0=s0, s1=s1, merged=merged)


def _merge_specs(tm):
    acts = [_row_spec(tm, 1024), _row_spec(tm, 1024, 0), _row_spec(tm, 1024), _row_spec(tm, 512), _row_spec(tm, 2048),
            _row_spec(tm, 1024)]
    params = [_const_spec((1, 1024)), _const_spec((1, 1024)), _const_spec((512, 512)), _const_spec((1, 512)),
              _hbm_spec(), _hbm_spec()]
    return acts, params


def _merge_fwd(ys, xbc_act, z, y5, gates, x, dvec, gssd, glu_w, glu_b, wbr, wout):
    T = x.shape[0]
    tm = TOKEN_TILE
    acts, params = _merge_specs(tm)

    def body(ys_ref, xs_ref, z_ref, y5_ref, gt_ref, x_ref, dv_ref, gs_ref, gw_ref, gb_ref, wbr_hbm, wout_hbm, x1_ref,
             wbr_ref, wout_ref):
        @pl.when(pl.program_id(0) == 0)
        def _():
            pltpu.sync_copy(wbr_hbm, wbr_ref)
            pltpu.sync_copy(wout_hbm, wout_ref)

        v = _merge_vals(ys_ref[...], xs_ref[...], z_ref[...], y5_ref[...], gt_ref[...], dv_ref[...], gs_ref[...],
                        gw_ref[...], gb_ref[...], wbr_ref)
        x1_ref[...] = x_ref[...] + _dot(v["merged"].astype(BF16), wout_ref[...])

    return _pc(
        body, name="merge_fwd", grid=(T // tm,),
        in_specs=acts + params, out_specs=_row_spec(tm, 1024),
        out_shape=jax.ShapeDtypeStruct((T, 1024), F32),
        scratch_shapes=[pltpu.VMEM((1536, 1024), BF16), pltpu.VMEM((1024, 1024), BF16)],
        compiler_params=_cparams(("arbitrary",)),
    )(ys, xbc_act, z, y5, gates, x, dvec, gssd, glu_w, glu_b, wbr, wout)


def _merge_bwd(ys, xbc_act, z, y5, gates, dx1, dvec, gssd, glu_w, glu_b, wbr, wout, head_sel, rider=None):
    T = dx1.shape[0]
    tm = TOKEN_TILE
    acts, params = _merge_specs(tm)

    def body(ys_ref, xs_ref, z_ref, y5_ref, gt_ref, dx1_ref, dv_ref, gs_ref, gw_ref, gb_ref, wbr_hbm, wout_hbm, hs_ref,
             dys_ref, dxs_ref, dz_ref, dy5_ref, dgt_ref, mg_ref, ya_ref, yb_ref, dpa_ref, dpb_ref, gel_ref, dpre_ref,
             ddv_ref, dgs_ref, dgb_ref, wbr_ref, wout_ref, ddacc_ref):
        i = pl.program_id(0)

        @pl.when(i == 0)
        def _():
            pltpu.sync_copy(wbr_hbm, wbr_ref)
            pltpu.sync_copy(wout_hbm, wout_ref)
            ddacc_ref[...] = jnp.zeros_like(ddacc_ref)
            dgs_ref[...] = jnp.zeros_like(dgs_ref)
            dgb_ref[...] = jnp.zeros_like(dgb_ref)

        ys, xs, z, y5, gates = ys_ref[...], xs_ref[...], z_ref[...], y5_ref[...], gt_ref[...]
        dvv, gsv, gw = dv_ref[...], gs_ref[...], gw_ref[...]
        v = _merge_vals(ys, xs, z, y5, gates, dvv, gsv, gw, gb_ref[...], wbr_ref)
        dmg = _dot_nt(dx1_ref[...].astype(BF16), wout_ref[...])
        s0, s1, pa, pb = v["s0"], v["s1"], v["pa"], v["pb"]
        dgt_ref[:, :D_MODEL] = (dmg * pa * s0 * (1.0 - s0)).astype(BF16)
        dgt_ref[:, D_MODEL:] = (dmg * pb * s1 * (1.0 - s1)).astype(BF16)
        dpa = (dmg * s0).astype(BF16)
        dpb = (dmg * s1).astype(BF16)
        dya = _dot_nt(dpa, wbr_ref[0:SSD_INNER, :])
        dyb = _dot_nt(dpb, wbr_ref[SSD_INNER:, :])
        gel, sg = v["gel"], v["sg"]
        dpre = (dyb * gel * sg * (1.0 - sg))
        dgb_ref[...] += jnp.sum(dpre, axis=0, keepdims=True)
        dpre_b = dpre.astype(BF16)
        dgel = dyb * sg + _dot_nt(dpre_b, gw)
        dy5_ref[...] = dgel * _gelu_grad(y5)
        yn = v["yn"]
        dgs_ref[...] += jnp.sum(dya * yn, axis=0, keepdims=True)
        dyn = dya * gsv
        dpre_a = jnp.concatenate(
            [_rms_bwd(yn[:, 256 * gi:256 * (gi + 1)], v["rs"][gi], dyn[:, 256 * gi:256 * (gi + 1)])
             for gi in range(SSD_GROUPS)], axis=1)
        sz, qv = v["sz"], v["qv"]
        dq = dpre_a * (z * sz)
        dz_ref[...] = (dpre_a * qv * (sz * (1.0 + z * (1.0 - sz)))).astype(BF16)
        dys_ref[...] = dq
        dxs_ref[...] = dq * dvv
        ddacc_ref[...] += jnp.sum(dq * xs, axis=0, keepdims=True)
        mg_ref[...] = v["merged"].astype(BF16)
        ya_ref[...] = v["ya"].astype(BF16)
        yb_ref[...] = v["yb"].astype(BF16)
        dpa_ref[...] = dpa
        dpb_ref[...] = dpb
        gel_ref[...] = gel.astype(BF16)
        dpre_ref[...] = dpre_b

        @pl.when(i == pl.num_programs(0) - 1)
        def _():
            ddv_ref[...] = _dot_hi(ddacc_ref[...], hs_ref[...])

    outs = [(1024, F32), (1024, F32), (1024, BF16), (512, F32), (2048, BF16),
            (1024, BF16), (1024, BF16), (512, BF16), (1024, BF16), (1024, BF16), (512, BF16), (512, BF16)]
    return _call(
        body, rider, name="merge_bwd", grid=(T // tm,),
        in_specs=acts + params + [_const_spec((1024, DT_PAD))],
        out_specs=[_row_spec(tm, w) for w, _ in outs] + [_const_spec((1, DT_PAD)), _const_spec((1, 1024)), _const_spec((1, 512))],
        out_shape=[jax.ShapeDtypeStruct((T, w), d) for w, d in outs] + [
            jax.ShapeDtypeStruct((1, DT_PAD), F32), jax.ShapeDtypeStruct((1, 1024), F32), jax.ShapeDtypeStruct((1, 512), F32)],
        scratch_shapes=[pltpu.VMEM((1536, 1024), BF16), pltpu.VMEM((1024, 1024), BF16), pltpu.VMEM((1, 1024), F32)],
        compiler_params=_cparams(("arbitrary",)),
    )(ys, xbc_act, z, y5, gates, dx1, dvec, gssd, glu_w, glu_b, wbr, wout, head_sel)


def _mlp_fwd_loss(x1, target, g, g_fin, w1, w2):
    T = x1.shape[0]
    tm = TOKEN_TILE

    def body(x_ref, t_ref, g_ref, gf_ref, w1_hbm, w2_hbm, dx_ref, loss_ref, dg_ref, w1_ref, w2_ref):
        @pl.when(pl.program_id(0) == 0)
        def _():
            pltpu.sync_copy(w1_hbm, w1_ref)
            pltpu.sync_copy(w2_hbm, w2_ref)
            loss_ref[...] = jnp.zeros_like(loss_ref)
            dg_ref[...] = jnp.zeros_like(dg_ref)

        xv = x_ref[...]
        xn, _ = _rms(xv)
        h = (xn * g_ref[...]).astype(BF16)
        acc = xv
        for s in range(FF_SHARDS):
            rl = jnp.maximum(_dot(h, w1_ref[s]), 0.0)
            acc += _dot((rl * rl).astype(BF16), w2_ref[FF_SHARD * s:FF_SHARD * (s + 1), :])
        yn, r = _rms(acc)
        gv = gf_ref[...]
        err = yn * gv - t_ref[...]
        loss_ref[...] += jnp.sum(err * err, axis=0, keepdims=True) * (0.5 / D_MODEL)
        dy = err * (1.0 / D_MODEL)
        dg_ref[...] += jnp.sum(dy * yn, axis=0, keepdims=True)
        dx_ref[...] = _rms_bwd(yn, r, dy * gv)

    return _pc(
        body, name="mlp_fwd_loss", grid=(T // tm,),
        in_specs=[_row_spec(tm, 1024), _row_spec(tm, 1024), _const_spec((1, 1024)), _const_spec((1, 1024)),
                  _hbm_spec(), _hbm_spec()],
        out_specs=[_row_spec(tm, 1024), _const_spec((1, 1024)), _const_spec((1, 1024))],
        out_shape=[jax.ShapeDtypeStruct((T, 1024), F32), jax.ShapeDtypeStruct((1, 1024), F32),
                   jax.ShapeDtypeStruct((1, 1024), F32)],
        scratch_shapes=[pltpu.VMEM((FF_SHARDS, D_MODEL, FF_SHARD), BF16), pltpu.VMEM((D_FF, D_MODEL), BF16)],
        compiler_params=_cparams(("arbitrary",)),
    )(x1, target, g, g_fin, w1, w2)


def _mlp_bwd(x1, dx2, g, w1, w2):
    T = x1.shape[0]
    tm = TOKEN_TILE

    def body(x_ref, dx2_ref, g_ref, w1_hbm, w2_hbm, dx1_ref, h_ref, act_ref, da_ref, dg_ref, w1_ref, w2_ref):
        @pl.when(pl.program_id(0) == 0)
        def _():
            pltpu.sync_copy(w1_hbm, w1_ref)
            pltpu.sync_copy(w2_hbm, w2_ref)
            dg_ref[...] = jnp.zeros_like(dg_ref)

        xn, r = _rms(x_ref[...])
        gv = g_ref[...]
        h = (xn * gv).astype(BF16)
        h_ref[...] = h
        dx2 = dx2_ref[...]
        dx2b = dx2.astype(BF16)
        dh = jnp.zeros((tm, D_MODEL), F32)
        for s in range(FF_SHARDS):
            ff = slice(FF_SHARD * s, FF_SHARD * (s + 1))
            rl = jnp.maximum(_dot(h, w1_ref[s]), 0.0)
            act_ref[:, ff] = (rl * rl).astype(BF16)
            da = (_dot_nt(dx2b, w2_ref[ff, :]) * (2.0 * rl)).astype(BF16)
            da_ref[:, ff] = da
            dh += _dot_nt(da, w1_ref[s])
        dg_ref[...] += jnp.sum(dh * xn, axis=0, keepdims=True)
        dx1_ref[...] = dx2 + _rms_bwd(xn, r, dh * gv)

    return _pc(
        body, name="mlp_bwd", grid=(T // tm,),
        in_specs=[_row_spec(tm, 1024), _row_spec(tm, 1024), _const_spec((1, 1024)), _hbm_spec(), _hbm_spec()],
        out_specs=[_row_spec(tm, 1024), _row_spec(tm, 1024), _row_spec(tm, D_FF), _row_spec(tm, D_FF), _const_spec((1, 1024))],
        out_shape=[jax.ShapeDtypeStruct((T, 1024), F32), jax.ShapeDtypeStruct((T, 1024), BF16),
                   jax.ShapeDtypeStruct((T, D_FF), BF16), jax.ShapeDtypeStruct((T, D_FF), BF16),
                   jax.ShapeDtypeStruct((1, 1024), F32)],
        scratch_shapes=[pltpu.VMEM((FF_SHARDS, D_MODEL, FF_SHARD), BF16), pltpu.VMEM((D_FF, D_MODEL), BF16)],
        compiler_params=_cparams(("arbitrary",)),
    )(x1, dx2, g, w1, w2)


WGRAD_OUT_ELEMS = 2 * 1024 * 1024
WGRAD_TILE_BYTES = 4 * 1024 * 1024


def _wgrad(a, b, name, col_shards=None, row_shards_into=None):
    T, K = a.shape
    N = b.shape[1]
    nb = N // col_shards if col_shards else min(N, 1024, max(128, WGRAD_OUT_ELEMS // K))
    tt = min(T, WGRAD_TOKENS)
    while tt * max(K * a.dtype.itemsize, nb * b.dtype.itemsize) > WGRAD_TILE_BYTES:
        tt //= 2
    assert N % nb == 0 and T % tt == 0
    in_specs = [pl.BlockSpec((tt, K), lambda n, t: (t, 0)), pl.BlockSpec((tt, nb), lambda n, t: (t, n))]
    args, aliases = [a, b], {}
    if col_shards:
        out_spec = pl.BlockSpec((None, None, K, nb), lambda n, t: (n, 0, 0, 0))
        out_shape = jax.ShapeDtypeStruct((col_shards, 2, K, nb), F32)
    elif row_shards_into is not None:
        shards, _, rows, cols = row_shards_into.shape
        assert shards * rows == K and cols == N
        out_spec = pl.BlockSpec((shards, None, rows, nb), lambda n, t: (0, 1, 0, n))
        out_shape = jax.ShapeDtypeStruct(row_shards_into.shape, F32)
        in_specs.append(_hbm_spec())
        args.append(row_shards_into)
        aliases = {2: 0}
    else:
        out_spec = pl.BlockSpec((K, nb), lambda n, t: (0, n))
        out_shape = jax.ShapeDtypeStruct((K, N), F32)

    def body(a_ref, b_ref, *rest):
        o_ref = rest[-1]

        @pl.when(pl.program_id(1) == 0)
        def _():
            o_ref[...] = jnp.zeros_like(o_ref)

        o_ref[...] += _dot_tn(a_ref[...].astype(BF16), b_ref[...].astype(BF16)).reshape(o_ref.shape)

    return _pc(
        body, name=name, grid=(N // nb, T // tt), in_specs=in_specs, out_specs=out_spec, out_shape=out_shape,
        input_output_aliases=aliases, compiler_params=_cparams(("parallel", "arbitrary")),
    )(*args)


def _s5_block_weights(bb_re, bb_im, c_re, c_im):
    eye = jnp.eye(8, dtype=F32)
    bre = bb_re.reshape(S5_BLOCKS, 8, 64, 16)
    bim = bb_im.reshape(S5_BLOCKS, 8, 64, 16)
    wb_re = jnp.einsum('jgpk,gh->jhkgp', bre, eye).reshape(S5_BLOCKS, _BI, _BW)
    wb_im = jnp.einsum('jgpk,gh->jhkgp', bim, eye).reshape(S5_BLOCKS, _BI, _BW)
    wb4 = jnp.concatenate([wb_re, wb_im], axis=2).astype(BF16)
    cre = c_re.reshape(S5_BLOCKS, 8, 16, 64)
    cim = c_im.reshape(S5_BLOCKS, 8, 16, 64)
    wc_re = jnp.einsum('jgkp,gh->jgphk', cre, eye).reshape(S5_BLOCKS, _BW, _BI)
    wc_im = jnp.einsum('jgkp,gh->jgphk', -cim, eye).reshape(S5_BLOCKS, _BW, _BI)
    wc4 = jnp.concatenate([wc_re, wc_im], axis=1).astype(BF16)
    return wb4, wc4


def _s5_block_grads(dwb4, dwc4):
    eye = jnp.eye(8, dtype=F32)
    dwb = dwb4.reshape(S5_BLOCKS, 8, 16, 2, 8, 64)
    dbb = jnp.einsum('jhkrgp,gh->rjgpk', dwb, eye).reshape(2, 32, 64, 16)
    dwc = dwc4.reshape(S5_BLOCKS, 2, 8, 64, 8, 16)
    dc = jnp.einsum('jrgphk,gh->rjgkp', dwc, eye).reshape(2, 32, 16, 64)
    return dbb[0], dbb[1], dc[0], -dc[1]


def _row(v, width=None):
    v = v.reshape(1, -1)
    if width is not None and v.shape[1] < width:
        v = jnp.concatenate([v, jnp.zeros((1, width - v.shape[1]), v.dtype)], axis=1)
    return v


def _local_step(x, target, p, comm=None):
    g_mix, g_mlp, g_fin = _row(p["norm_mix_g"]), _row(p["norm_mlp_g"]), _row(p["norm_final_g"])
    conv_b = _row(p["conv_b"])
    dt_bias = _row(p["dt_bias"], DT_PAD)
    alog = _row(p["a_log"], DT_PAD)
    dvec = _row(jnp.repeat(p["d_ssd"], SSD_HEADDIM))
    gssd = _row(p["ssd_norm_g"])
    s5d = _row(p["s5_d"])
    glu_b = _row(p["s5_glu_b"])
    head_sel = (jnp.arange(SSD_INNER)[:, None] // SSD_HEADDIM == jnp.arange(DT_PAD)[None, :]).astype(F32)

    a_re = p["s5_a_re"].reshape(S5_STATES, 1)
    a_im = p["s5_a_im"].reshape(S5_STATES, 1)
    log_dt = jnp.repeat(p["s5_log_dt"], 64).reshape(S5_STATES, 1)
    b_re = p["s5_b_re"].reshape(S5_STATES, 16)
    b_im = p["s5_b_im"].reshape(S5_STATES, 16)
    ab_re, ab_im, bb_re, bb_im = _s5_disc(a_re, a_im, log_dt, b_re, b_im)
    wb4, wc4 = _s5_block_weights(bb_re, bb_im, p["s5_c_re"], p["s5_c_im"])
    ab = jnp.concatenate([ab_re.reshape(1, S5_STATES), ab_im.reshape(1, S5_STATES), jnp.zeros((6, S5_STATES), F32)], axis=0)

    wp = p["w_in_perm"]

    z, xbc_raw, u5, gates, dt_raw, h = _inproj_fwd(x, g_mix, wp)
    xbc_act, dt = _conv_fwd(xbc_raw, dt_raw, p["conv_w"], conv_b, dt_bias)
    ys, ssd_states = _ssd_fwd(xbc_act, dt, alog)
    if comm is None:
        y5, s5_states = _s5_fwd(u5, wb4, wc4, ab, s5d)
    else:
        (y5, s5_states), late = _s5_fwd(u5, wb4, wc4, ab, s5d, rider=_Gather(comm["late_srcs"], comm["late_ks"]))
        p = {**p, **comm["late_unpack"](late)}
    wbr, wout, w1, w2, glu_w = p["w_branch"], p["w_out"], p["w_mlp_in"], p["w_mlp_out"], p["s5_glu_w"]
    x1 = _merge_fwd(ys, xbc_act, z, y5, gates, x, dvec, gssd, glu_w, glu_b, wbr, wout)
    dx2, loss_lanes, d_gfin = _mlp_fwd_loss(x1, target, g_mlp, g_fin, w1, w2)

    dx1, h2, act, da1, d_gmlp = _mlp_bwd(x1, dx2, g_mlp, w1, w2)
    g_mlp4 = _wgrad(h2, da1, "wgrad_mlp_in", col_shards=FF_SHARDS)
    g_mlp4 = _wgrad(act, dx2, "wgrad_mlp_out", row_shards_into=g_mlp4)
    d_w_mlp_in, d_w_mlp_out = g_mlp4[:, 0], g_mlp4[:, 1].reshape(D_FF, D_MODEL)
    merge_args = (ys, xbc_act, z, y5, gates, dx1, dvec, gssd, glu_w, glu_b, wbr, wout, head_sel)
    if comm is None:
        merge_out = _merge_bwd(*merge_args)
    else:
        g_mlp = g_mlp4.reshape(N_CHIPS, 2 * FF_SHARD, D_MODEL)
        merge_out, (sib_mlp,) = _merge_bwd(*merge_args, rider=_Pair([g_mlp]))
        pf_mlp, pb_mlp = _pair_sum(comm["place"], g_mlp, sib_mlp, "pair_sum_mlp")
    (dys, dxs_m, dz, dy5, dgates, mg, ya, yb, dpa, dpb, gel, dpre, d_dssd, d_gssd, d_glu_b) = merge_out
    d_w_out = _wgrad(mg, dx1, "wgrad_out")
    d_w_branch = jnp.concatenate([_wgrad(ya, dpa, "wgrad_branch_a"), _wgrad(yb, dpb, "wgrad_branch_b")], axis=0)
    d_glu_w = _wgrad(gel, dpre, "wgrad_glu")
    s5_args = (u5, dy5, wb4, wc4, ab, s5d, s5_states)
    if comm is None:
        du5, dwb4, dwc4, dab, d_s5d = _s5_bwd(*s5_args)
        mlp_total = None
    else:
        (du5, dwb4, dwc4, dab, d_s5d), (got_mlp,) = _s5_bwd(*s5_args, rider=_Chip([pb_mlp]))
        (mlp_total,) = _chip_sum(comm["place"], pf_mlp, got_mlp, "chip_sum_mlp")
    dbb_re, dbb_im, d_c_re, d_c_im = _s5_block_grads(dwb4, dwc4)
    d_a_re, d_a_im, d_log_dt, d_b_re, d_b_im = _s5_disc_bwd(
        a_re, a_im, log_dt, b_re, b_im, dab[0].reshape(S5_STATES, 1), dab[1].reshape(S5_STATES, 1),
        dbb_re.reshape(S5_STATES, 16), dbb_im.reshape(S5_STATES, 16))
    dxs_s, dB, dC, ddt, d_alog = _ssd_bwd(xbc_act, dt, alog, ssd_states, dys)
    dxbc_raw, ddt_raw, d_conv_w, d_conv_b, d_dt_bias = _conv_bwd(
        xbc_raw, dt_raw, dxs_m, dxs_s, dB, dC, ddt, p["conv_w"], conv_b, dt_bias)
    d_w_in = dict(z=_wgrad(h, dz, "wgrad_in_z"), xbc=_wgrad(h, dxbc_raw, "wgrad_in_xbc"),
                  dt=_wgrad(h, ddt_raw, "wgrad_in_dt")[:, :16], u5=_wgrad(h, du5, "wgrad_in_u5"),
                  gates=_wgrad(h, dgates, "wgrad_in_gates"))
    w_in_pieces = [(c0, d_w_in[n]) for n, c0, _ in W_IN_PIECES]
    inproj_args = (x, dx1, dz, dxbc_raw, du5, dgates, ddt_raw, g_mix, wp)
    if comm is None:
        dx, d_gmix = _inproj_bwd(*inproj_args)
        late_totals = None
    else:
        g_b, g_in = _late_buffers(d_w_out, d_w_branch, d_glu_w, d_conv_w[:CONV_K], w_in_pieces)
        sib_b, sib_in = _exchange(_Pair([g_b, g_in]), "pair_exchange")
        pf_b, pb_b = _pair_sum(comm["place"], g_b, sib_b, "pair_sum_b")
        pf_in, pb_in = _pair_sum(comm["place"], g_in, sib_in, "pair_sum_in")
        (dx, d_gmix), (got_b, got_in) = _inproj_bwd(*inproj_args, rider=_Chip([pb_b, pb_in]))
        late_totals = (_chip_sum(comm["place"], pf_b, got_b, "chip_sum_b")[0],
                       _chip_sum(comm["place"], pf_in, got_in, "chip_sum_in")[0])

    grads = dict(
        norm_mix_g=d_gmix.reshape(-1), w_in_pieces=w_in_pieces, late_totals=late_totals,
        conv_w=d_conv_w[:CONV_K], conv_b=d_conv_b.reshape(-1),
        dt_bias=d_dt_bias[0, :16], a_log=d_alog[0, :16], d_ssd=d_dssd[0, :16], ssd_norm_g=d_gssd.reshape(-1),
        s5_a_re=d_a_re.reshape(32, 64), s5_a_im=d_a_im.reshape(32, 64), s5_log_dt=d_log_dt.reshape(32),
        s5_b_re=d_b_re.reshape(32, 64, 16), s5_b_im=d_b_im.reshape(32, 64, 16), s5_c_re=d_c_re, s5_c_im=d_c_im,
        s5_d=d_s5d.reshape(-1), s5_glu_w=d_glu_w, s5_glu_b=d_glu_b.reshape(-1), w_branch=d_w_branch, w_out=d_w_out,
        norm_mlp_g=d_gmlp.reshape(-1), w_mlp_in=d_w_mlp_in, w_mlp_out=d_w_mlp_out, norm_final_g=d_gfin.reshape(-1),
        mlp_total=mlp_total)
    return jnp.sum(loss_lanes), dx, grads


MESH = pl.DeviceIdType.MESH
N_CHIPS = 4


def _place():
    x, y, c = lax.axis_index("x"), lax.axis_index("y"), lax.axis_index("c")
    chips = [(1 - x, y), (x, 1 - y), (1 - x, 1 - y)]
    return x, y, c, chips


def _remote(src, dst, send_sems, recv_sems, k, to):
    return pltpu.make_async_remote_copy(src_ref=src, dst_ref=dst, send_sem=send_sems.at[k], recv_sem=recv_sems.at[k],
                                        device_id=to, device_id_type=MESH)


def _row_chunks(rows, k, align):
    step = rows // k
    assert rows % k == 0 and step % align == 0, (rows, k, align)
    return [(i * step, step) for i in range(k)]


ICI_CHUNKS = 4
D2D_CHUNKS = 24


class _Gather:
    def __init__(self, srcs, ks):
        self.inputs = list(srcs)
        self.out_shapes = [jax.ShapeDtypeStruct((N_CHIPS,) + a.shape, a.dtype) for a in srcs]
        self.halves = [a.shape[0] // 2 for a in srcs]
        self.pieces = [_row_chunks(h, k, 32 // a.dtype.itemsize) for a, h, k in zip(srcs, self.halves, ks)]
        self.n_ici = 3 * sum(ks)
        self.n_sems = 2 * self.n_ici + len(srcs)

    def _plan(self, src_refs, out_refs, send_sems, recv_sems):
        x, y, c, chips = _place()
        own = 2 * x + y
        sib = (x, y, 1 - c)
        first, fwd_plan, k = [], [], 0
        for a, (src_ref, out_ref) in enumerate(zip(src_refs, out_refs)):
            h = self.halves[a]
            for r0, nr in self.pieces[a]:
                for cx, cy in chips:
                    first.append(_remote(src_ref.at[pl.ds(c * h + r0, nr), :], out_ref.at[own, pl.ds(c * h + r0, nr), :],
                                         send_sems, recv_sems, k, (cx, cy, c)))
                    fwd_plan.append((out_ref, 2 * cx + cy, h, r0, nr, k, (cx, cy, c)))
                    k += 1
        for a, (src_ref, out_ref) in enumerate(zip(src_refs, out_refs)):
            first.append(_remote(src_ref, out_ref.at[own], send_sems, recv_sems, 2 * self.n_ici + a, sib))
        return first, fwd_plan, c, sib

    def issue(self, src_refs, out_refs, send_sems, recv_sems):
        for cp in self._plan(src_refs, out_refs, send_sems, recv_sems)[0]:
            cp.start()

    def complete(self, src_refs, out_refs, send_sems, recv_sems):
        first, fwd_plan, c, sib = self._plan(src_refs, out_refs, send_sems, recv_sems)
        passed = []
        for out_ref, s, h, r0, nr, k, frm in fwd_plan:
            got = out_ref.at[s, pl.ds(c * h + r0, nr), :]
            _remote(got, got, send_sems, recv_sems, k, frm).wait_recv()
            fw = _remote(got, got, send_sems, recv_sems, self.n_ici + k, sib)
            fw.start()
            passed.append(fw)
        for out_ref, s, h, r0, nr, k, frm in fwd_plan:
            got = out_ref.at[s, pl.ds((1 - c) * h + r0, nr), :]
            _remote(got, got, send_sems, recv_sems, self.n_ici + k, sib).wait_recv()
        own_copies = first[self.n_ici:]
        for cp in own_copies:
            cp.wait_recv()
        for cp in first + passed:
            cp.wait_send()


def _exchange(rider, name):
    ri, ro = len(rider.inputs), len(rider.out_shapes)

    def body(*refs):
        rider.issue(refs[:ri], refs[ri:ri + ro], *refs[ri + ro:])
        rider.complete(refs[:ri], refs[ri:ri + ro], *refs[ri + ro:])

    return _pc(
        body, name=name, in_specs=[_hbm_spec()] * ri, out_specs=[_hbm_spec()] * ro, out_shape=list(rider.out_shapes),
        scratch_shapes=[pltpu.SemaphoreType.DMA((rider.n_sems,))] * 2,
    )(*rider.inputs)


def _call(body, rider=None, **kw):
    if rider is None:
        return _pc(body, **kw)
    single = not isinstance(kw["out_shape"], (list, tuple))
    out_specs = [kw["out_specs"]] if single else list(kw["out_specs"])
    out_shape = [kw["out_shape"]] if single else list(kw["out_shape"])
    scratch = list(kw.get("scratch_shapes", ()))
    n_in, n_out, n_scr = len(kw["in_specs"]), len(out_specs), len(scratch)
    ri, ro = len(rider.inputs), len(rider.out_shapes)
    steps = kw["grid"][0]

    def wrapped(*refs):
        o0 = n_in + ri
        s0 = o0 + n_out + ro
        r_in, r_out, sems = refs[n_in:o0], refs[o0 + n_out:s0], refs[s0 + n_scr:]

        @pl.when(pl.program_id(0) == 0)
        def _():
            rider.issue(r_in, r_out, *sems)

        body(*refs[:n_in], *refs[o0:o0 + n_out], *refs[s0:s0 + n_scr])

        @pl.when(pl.program_id(0) == steps - 1)
        def _():
            rider.complete(r_in, r_out, *sems)

    f = _pc(wrapped, name=kw["name"], grid=kw["grid"], in_specs=list(kw["in_specs"]) + [_hbm_spec()] * ri,
            out_specs=out_specs + [_hbm_spec()] * ro, out_shape=out_shape + list(rider.out_shapes),
            scratch_shapes=scratch + [pltpu.SemaphoreType.DMA((rider.n_sems,))] * 2, compiler_params=kw["compiler_params"])

    def run(*args):
        res = f(*args, *rider.inputs)
        return (res[0] if single else res[:n_out]), res[n_out:]

    return run


def _d2d_pieces(rows):
    k = next(k for k in range(24, 0, -1) if rows % k == 0 and (rows // k) % 8 == 0)
    return _row_chunks(rows, k, 8)


class _Pair:
    def __init__(self, gs, small=None):
        self.n = len(gs)
        self.halves = [g.shape[1] // 2 for g in gs]
        self.inputs = list(gs) + ([small] if small is not None else [])
        self.out_shapes = [jax.ShapeDtypeStruct((N_CHIPS, h, g.shape[2]), F32) for g, h in zip(gs, self.halves)]
        if small is not None:
            self.out_shapes.append(jax.ShapeDtypeStruct(small.shape, F32))
        self.n_sems = len(self.inputs)

    def issue(self, in_refs, out_refs, send_sems, recv_sems):
        x, y, c, _ = _place()
        sib = (x, y, 1 - c)
        for a in range(self.n):
            h = self.halves[a]
            for s in range(N_CHIPS):
                for r0, nr in _d2d_pieces(h):
                    _remote(in_refs[a].at[s, pl.ds((1 - c) * h + r0, nr), :], out_refs[a].at[s, pl.ds(r0, nr), :],
                            send_sems, recv_sems, a, sib).start()
        for a in range(self.n, len(self.inputs)):
            _remote(in_refs[a], out_refs[a], send_sems, recv_sems, a, sib).start()

    def complete(self, in_refs, out_refs, send_sems, recv_sems):
        x, y, c, _ = _place()
        for a in range(len(self.inputs)):
            _remote(out_refs[a], out_refs[a], send_sems, recv_sems, a, (x, y, 1 - c)).wait()


SUM_BLOCKS = 4


def _pair_sum(place, g, sib, name, small=None, sib_small=None):
    n, R, C = g.shape
    H = R // 2
    rb = H // SUM_BLOCKS
    assert H % SUM_BLOCKS == 0 and rb % 16 == 0

    def body(place_ref, a_ref, b_ref, *rest):
        if small is None:
            pf_ref, pb_ref = rest
        else:
            s_ref, t_ref, pf_ref, pb_ref, ps_ref = rest

            @pl.when((pl.program_id(0) == 0) & (pl.program_id(1) == 0))
            def _():
                ps_ref[...] = s_ref[...] + t_ref[...]

        p = a_ref[...] + b_ref[...]
        pf_ref[...] = p
        pb_ref[...] = p.astype(BF16)

    blk = pl.BlockSpec((1, rb, C), lambda s, i, pr: (s, i, 0))
    mine = pl.BlockSpec((1, rb, C), lambda s, i, pr: (s, pr[1] * SUM_BLOCKS + i, 0))
    ins, outs, shapes, args = [mine, blk], [blk, blk], [jax.ShapeDtypeStruct((n, H, C), F32),
                                                        jax.ShapeDtypeStruct((n, H, C), BF16)], [g, sib]
    if small is not None:
        sm = pl.BlockSpec(small.shape, lambda s, i, pr: (0, 0))
        ins += [sm, sm]
        outs += [sm]
        shapes += [jax.ShapeDtypeStruct(small.shape, F32)]
        args += [small, sib_small]
    return _pc(
        body, name=name, out_shape=shapes,
        grid_spec=pltpu.PrefetchScalarGridSpec(num_scalar_prefetch=1, grid=(n, SUM_BLOCKS), in_specs=ins, out_specs=outs),
        compiler_params=_cparams(("arbitrary", "arbitrary")),
    )(place, *args)


class _Chip:
    def __init__(self, pbs, psmall=None):
        self.n = len(pbs)
        self.rows = [pb.shape[1] for pb in pbs]
        self.inputs = list(pbs) + ([psmall] if psmall is not None else [])
        self.out_shapes = [jax.ShapeDtypeStruct((3,) + pb.shape[1:], BF16) for pb in pbs]
        if psmall is not None:
            self.out_shapes.append(jax.ShapeDtypeStruct((N_CHIPS,) + psmall.shape, F32))
        self.n_sems = 3 * len(self.inputs)

    def issue(self, in_refs, out_refs, send_sems, recv_sems):
        x, y, c, chips = _place()
        own = 2 * x + y
        for j, (cx, cy) in enumerate(chips):
            for a in range(self.n):
                for r0, nr in _row_chunks(self.rows[a], ICI_CHUNKS, 16):
                    _remote(in_refs[a].at[2 * cx + cy, pl.ds(r0, nr), :], out_refs[a].at[j, pl.ds(r0, nr), :],
                            send_sems, recv_sems, 3 * a + j, (cx, cy, c)).start()
            for a in range(self.n, len(self.inputs)):
                _remote(in_refs[a], out_refs[a].at[own], send_sems, recv_sems, 3 * a + j, (cx, cy, c)).start()

    def complete(self, in_refs, out_refs, send_sems, recv_sems):
        x, y, c, chips = _place()
        own = 2 * x + y
        for j, (cx, cy) in enumerate(chips):
            for a in range(self.n):
                _remote(in_refs[a].at[own], out_refs[a].at[j], send_sems, recv_sems, 3 * a + j, (cx, cy, c)).wait()
            for a in range(self.n, len(self.inputs)):
                _remote(in_refs[a], out_refs[a].at[2 * cx + cy], send_sems, recv_sems, 3 * a + j, (cx, cy, c)).wait()


def _chip_sum(place, pf, got, name, small4=None, psmall=None):
    _, H, C = pf.shape
    rb = H // SUM_BLOCKS

    def body(place_ref, o_ref, g_ref, *rest):
        if small4 is None:
            (tot_ref,) = rest
        else:
            s_ref, p_ref, tot_ref, st_ref = rest

            @pl.when(pl.program_id(0) == 0)
            def _():
                terms = [jnp.where(place_ref[0] == s, p_ref[...], s_ref[s]) for s in range(N_CHIPS)]
                st_ref[...] = ((terms[0] + terms[1]) + terms[2]) + terms[3]

        tot_ref[...] = ((o_ref[0] + g_ref[0].astype(F32)) + g_ref[1].astype(F32)) + g_ref[2].astype(F32)

    ins = [pl.BlockSpec((1, rb, C), lambda i, pr: (pr[0], i, 0)), pl.BlockSpec((3, rb, C), lambda i, pr: (0, i, 0))]
    outs = [pl.BlockSpec((rb, C), lambda i, pr: (pr[1] * SUM_BLOCKS + i, 0))]
    shapes = [jax.ShapeDtypeStruct((2 * H, C), F32)]
    args = [pf, got]
    if small4 is not None:
        ins += [pl.BlockSpec(small4.shape, lambda i, pr: (0, 0, 0)), pl.BlockSpec(psmall.shape, lambda i, pr: (0, 0))]
        outs += [pl.BlockSpec(psmall.shape, lambda i, pr: (0, 0))]
        shapes += [jax.ShapeDtypeStruct(psmall.shape, F32)]
        args += [small4, psmall]
    return _pc(
        body, name=name, out_shape=shapes,
        grid_spec=pltpu.PrefetchScalarGridSpec(num_scalar_prefetch=1, grid=(SUM_BLOCKS,), in_specs=ins, out_specs=outs),
        compiler_params=_cparams(("arbitrary",)),
    )(place, *args)


def _half_exchange(fulls):
    n = len(fulls)

    def body(*refs):
        in_refs, out_refs = refs[:n], refs[n:2 * n]
        send_sems, recv_sems = refs[2 * n:]
        x, y, c, _ = _place()
        sib = (x, y, 1 - c)
        for a in range(n):
            h = fulls[a].shape[0] // 2
            for r0, nr in _d2d_pieces(h):
                rows = pl.ds(c * h + r0, nr)
                _remote(in_refs[a].at[rows, :], out_refs[a].at[rows, :], send_sems, recv_sems, a, sib).start()
        for a in range(n):
            h = fulls[a].shape[0] // 2
            _remote(in_refs[a].at[pl.ds(c * h, h), :], out_refs[a].at[pl.ds((1 - c) * h, h), :], send_sems, recv_sems, a,
                    sib).wait()

    return _pc(
        body, name="half_exchange", in_specs=[_hbm_spec()] * n, out_specs=[_hbm_spec()] * n,
        out_shape=[jax.ShapeDtypeStruct(f.shape, F32) for f in fulls],
        input_output_aliases={a: a for a in range(n)},
        scratch_shapes=[pltpu.SemaphoreType.DMA((n,)), pltpu.SemaphoreType.DMA((n,))],
    )(*fulls)


def _small_allreduce(pack):
    R, C = pack.shape

    def body(p_ref, o_ref, sib_ref, pair_ref, slots_ref, send_sems, recv_sems):
        x, y, c, chips = _place()
        own = 2 * x + y
        cp = _remote(p_ref, sib_ref, send_sems, recv_sems, 0, (x, y, 1 - c))
        cp.start()
        cp.wait()
        pair_ref[...] = p_ref[...] + sib_ref[...]
        slots_ref[own] = pair_ref[...]
        out = [_remote(pair_ref, slots_ref.at[own], send_sems, recv_sems, 1 + j, (cx, cy, c)) for j, (cx, cy) in enumerate(chips)]
        for cp in out:
            cp.start()
        for j, (cx, cy) in enumerate(chips):
            _remote(pair_ref, slots_ref.at[2 * cx + cy], send_sems, recv_sems, 1 + j, (cx, cy, c)).wait()
        o_ref[...] = ((slots_ref[0] + slots_ref[1]) + slots_ref[2]) + slots_ref[3]

    vmem = pl.BlockSpec(memory_space=pltpu.VMEM)
    return _pc(
        body, name="small_allreduce", in_specs=[vmem], out_specs=vmem, out_shape=jax.ShapeDtypeStruct((R, C), F32),
        scratch_shapes=[pltpu.VMEM((R, C), F32), pltpu.VMEM((R, C), F32), pltpu.VMEM((N_CHIPS, R, C), F32),
                        pltpu.SemaphoreType.DMA((4,)), pltpu.SemaphoreType.DMA((4,))],
    )(pack)


def _adamw(w, g, m, v, name, g_row0=0, with_grad=False, col_block=None):
    R, C = w.shape
    rb = 256 if R % 256 == 0 else (128 if R % 128 == 0 else R)
    if col_block:
        rb = R
    assert g_row0 % rb == 0

    def body(w_ref, g_ref, m_ref, v_ref, d_ref, nm_ref, nv_ref, *g_out):
        gv = g_ref[...]
        m2 = ADAM_B1 * m_ref[...] + (1.0 - ADAM_B1) * gv
        v2 = ADAM_B2 * v_ref[...] + (1.0 - ADAM_B2) * (gv * gv)
        m_hat = m2 / (1.0 - ADAM_B1 ** ADAM_STEP)
        v_hat = v2 / (1.0 - ADAM_B2 ** ADAM_STEP)
        d_ref[...] = -ADAM_LR * (m_hat / (jnp.sqrt(v_hat) + ADAM_EPS) + ADAM_WD * w_ref[...])
        nm_ref[...] = m2
        nv_ref[...] = v2
        if with_grad:
            g_out[0][...] = gv

    if col_block:
        spec = g_spec = pl.BlockSpec((R, col_block), lambda i: (0, i))
        steps = C // col_block
    else:
        spec = pl.BlockSpec((rb, C), lambda i: (i, 0))
        g_spec = pl.BlockSpec((rb, C), lambda i: (g_row0 // rb + i, 0))
        steps = R // rb
    n_out = 4 if with_grad else 3
    return _pc(
        body, name=name, grid=(steps,), in_specs=[spec, g_spec, spec, spec], out_specs=[spec] * n_out,
        out_shape=[jax.ShapeDtypeStruct((R, C), F32)] * n_out, compiler_params=_cparams(("parallel",)),
    )(w, g, m, v)


PACK_COLS = 1024
ROWS_A = (("w_mlp_in", 0, 1024), ("w_mlp_out", 1024, 1024), ("w_out", 2048, 256), ("w_branch", 2304, 384))
ROWS_A_TOTAL = 2688
ROWS_B = (("w_out", 0, 256), ("w_branch", 256, 384))
ROW_B_GLU, ROW_B_CONV, ROWS_B_TOTAL = 640, 704, 768
W_IN_SHARD = 1412
CONV_PAD_ROWS = 16
SMALL = (("norm_mix_g", (1024,)), ("conv_b", (2048,)), ("dt_bias", (16,)), ("a_log", (16,)), ("d_ssd", (16,)),
         ("ssd_norm_g", (1024,)), ("s5_a_re", (32, 64)), ("s5_a_im", (32, 64)), ("s5_log_dt", (32,)),
         ("s5_b_re", (32, 64, 16)), ("s5_b_im", (32, 64, 16)), ("s5_c_re", (32, 16, 64)), ("s5_c_im", (32, 16, 64)),
         ("s5_d", (512,)), ("s5_glu_b", (512,)), ("norm_mlp_g", (1024,)), ("norm_final_g", (1024,)))
SMALL_ROWS = 144
SMALL_COUNT = sum(math.prod(shp) for _, shp in SMALL)
GLU_ROWS = S5_WIDTH * S5_WIDTH // PACK_COLS
CONV_ROWS = CONV_K * CONV_DIM // PACK_COLS
W_IN_PIECES = (("z", 0, 1024), ("xbc", 1024, 2048), ("dt", OFF_DT, 16), ("u5", OFF_U, 512), ("gates", 3600, 2048))


def _pack_small(parts):
    flat = jnp.concatenate([a.astype(F32).reshape(-1) for a in parts])
    return jnp.concatenate([flat, jnp.zeros((SMALL_ROWS * PACK_COLS - flat.shape[0],), F32)]).reshape(SMALL_ROWS, PACK_COLS)


def _unpack_small(pack):
    flat, out, r = pack.reshape(-1), {}, 0
    for name, shp in SMALL:
        n = math.prod(shp)
        out[name] = flat[r:r + n].reshape(shp)
        r += n
    return out


def _late_buffers(d_w_out, d_w_branch, d_glu_w, d_conv_w, w_in_pieces):
    conv4 = d_conv_w.reshape(CONV_K, N_CHIPS, 512).transpose(1, 0, 2).reshape(N_CHIPS, CONV_ROWS // N_CHIPS, PACK_COLS)
    g_b = jnp.concatenate(
        [d_w_out.reshape(N_CHIPS, -1, PACK_COLS), d_w_branch.reshape(N_CHIPS, -1, PACK_COLS),
         d_glu_w.reshape(N_CHIPS, GLU_ROWS // N_CHIPS, PACK_COLS),
         jnp.pad(conv4, ((0, 0), (0, ROWS_B_TOTAL - ROW_B_CONV - CONV_ROWS // N_CHIPS), (0, 0)))], axis=1)
    g_in = jnp.stack([jnp.concatenate(_column_range(w_in_pieces, W_IN_SHARD * s, W_IN_SHARD * (s + 1)), axis=1)
                      for s in range(N_CHIPS)])
    return g_b, g_in


def _column_range(pieces, lo, hi):
    out = []
    for c0, a in pieces:
        a0, a1 = max(lo, c0), min(hi, c0 + a.shape[-1])
        if a0 < a1:
            out.append(a[..., a0 - c0:a1 - c0])
    return out


def kernel(x, norm_mix_g, w_in, conv_w, conv_b, dt_bias, a_log, d_ssd, ssd_norm_g, s5_a_re, s5_a_im, s5_log_dt, s5_b_re, s5_b_im, s5_c_re, s5_c_im, s5_d, s5_glu_w, s5_glu_b, w_branch, w_out, norm_mlp_g, w_mlp_in, w_mlp_out, norm_final_g, loss_target, m_norm_mix_g, m_w_in, m_conv_w, m_conv_b, m_dt_bias, m_a_log, m_d_ssd, m_ssd_norm_g, m_s5_a_re, m_s5_a_im, m_s5_log_dt, m_s5_b_re, m_s5_b_im, m_s5_c_re, m_s5_c_im, m_s5_d, m_s5_glu_w, m_s5_glu_b, m_w_branch, m_w_out, m_norm_mlp_g, m_w_mlp_in, m_w_mlp_out, m_norm_final_g, v_norm_mix_g, v_w_in, v_conv_w, v_conv_b, v_dt_bias, v_a_log, v_d_ssd, v_ssd_norm_g, v_s5_a_re, v_s5_a_im, v_s5_log_dt, v_s5_b_re, v_s5_b_im, v_s5_c_re, v_s5_c_im, v_s5_d, v_s5_glu_w, v_s5_glu_b, v_w_branch, v_w_out, v_norm_mlp_g, v_w_mlp_in, v_w_mlp_out, v_norm_final_g):
    names = ("norm_mix_g", "w_in", "conv_w", "conv_b", "dt_bias", "a_log", "d_ssd", "ssd_norm_g", "s5_a_re", "s5_a_im",
             "s5_log_dt", "s5_b_re", "s5_b_im", "s5_c_re", "s5_c_im", "s5_d", "s5_glu_w", "s5_glu_b", "w_branch", "w_out",
             "norm_mlp_g", "w_mlp_in", "w_mlp_out", "norm_final_g")
    w = dict(zip(names, (norm_mix_g, w_in, conv_w, conv_b, dt_bias, a_log, d_ssd, ssd_norm_g, s5_a_re, s5_a_im, s5_log_dt,
                         s5_b_re, s5_b_im, s5_c_re, s5_c_im, s5_d, s5_glu_w, s5_glu_b, w_branch, w_out, norm_mlp_g,
                         w_mlp_in, w_mlp_out, norm_final_g)))
    m = dict(zip(names, (m_norm_mix_g, m_w_in, m_conv_w, m_conv_b, m_dt_bias, m_a_log, m_d_ssd, m_ssd_norm_g, m_s5_a_re,
                         m_s5_a_im, m_s5_log_dt, m_s5_b_re, m_s5_b_im, m_s5_c_re, m_s5_c_im, m_s5_d, m_s5_glu_w,
                         m_s5_glu_b, m_w_branch, m_w_out, m_norm_mlp_g, m_w_mlp_in, m_w_mlp_out, m_norm_final_g)))
    v = dict(zip(names, (v_norm_mix_g, v_w_in, v_conv_w, v_conv_b, v_dt_bias, v_a_log, v_d_ssd, v_ssd_norm_g, v_s5_a_re,
                         v_s5_a_im, v_s5_log_dt, v_s5_b_re, v_s5_b_im, v_s5_c_re, v_s5_c_im, v_s5_d, v_s5_glu_w,
                         v_s5_glu_b, v_w_branch, v_w_out, v_norm_mlp_g, v_w_mlp_in, v_w_mlp_out, v_norm_final_g)))

    cx, cy, cc = lax.axis_index("x"), lax.axis_index("y"), lax.axis_index("c")
    own = 2 * cx + cy
    place = jnp.stack([own, cc]).astype(jnp.int32)

    src_conv = jnp.concatenate([conv_w, jnp.zeros((CONV_PAD_ROWS - CONV_K, 512), F32)], axis=0)
    all_in, all_conv = _exchange(_Gather([w_in.astype(BF16), src_conv], [ICI_CHUNKS, 1]), "gather_first")
    p = {n: w[n] for n, _ in SMALL}
    p["conv_w"] = jnp.concatenate([all_conv[s, :CONV_K] for s in range(N_CHIPS)], axis=1)
    shards = [(W_IN_SHARD * s, all_in[s]) for s in range(N_CHIPS)]
    p["w_in_perm"] = jnp.concatenate(
        _column_range(shards, 0, OFF_DT) + _column_range(shards, OFF_U, D_IN_PROJ) + _column_range(shards, OFF_DT, OFF_U)
        + [jnp.zeros((D_MODEL, DT_PAD - 16), BF16)], axis=1)

    def late_unpack(gathered):
        all_a, all_glu = gathered
        out = {"w_mlp_in": all_a[:, 0:1024], "s5_glu_w": all_glu.reshape(S5_WIDTH, S5_WIDTH)}
        for n, r0, nr in ROWS_A[1:]:
            out[n] = all_a[:, r0:r0 + nr].reshape(N_CHIPS * nr, PACK_COLS)
        return out

    comm = dict(place=place, late_ks=[ICI_CHUNKS, 1], late_unpack=late_unpack,
                late_srcs=[jnp.concatenate([w[n].astype(BF16) for n, _, _ in ROWS_A], axis=0), s5_glu_w.astype(BF16)])
    loss_part, grad_x, g = _local_step(x[0], loss_target[0], p, comm)

    red_mlp, red_b, red_in = _half_exchange([g["mlp_total"], *g["late_totals"]])
    small_tot = _small_allreduce(_pack_small([g[n] for n, _ in SMALL] + [loss_part.reshape(1)]))
    loss = small_tot.reshape(-1)[SMALL_COUNT]

    grads = _unpack_small(small_tot)
    delta, new_m, new_v = {}, {}, {}
    for n, r0, _ in ROWS_A[:2]:
        delta[n], new_m[n], new_v[n], grads[n] = _adamw(w[n], red_mlp, m[n], v[n], "adamw_" + n, g_row0=r0, with_grad=True)
    for n, r0, _ in ROWS_B:
        delta[n], new_m[n], new_v[n], grads[n] = _adamw(w[n], red_b, m[n], v[n], "adamw_" + n, g_row0=r0, with_grad=True)
    d_t, m_t, v_t, g_t = _adamw(w_in.T, red_in.T, m_w_in.T, v_w_in.T, "adamw_w_in", with_grad=True, col_block=128)
    delta["w_in"], new_m["w_in"], new_v["w_in"], grads["w_in"] = d_t.T, m_t.T, v_t.T, g_t.T
    grads["s5_glu_w"] = red_b[ROW_B_GLU:ROW_B_GLU + GLU_ROWS // N_CHIPS].reshape(S5_WIDTH // N_CHIPS, S5_WIDTH)
    grads["conv_w"] = red_b[ROW_B_CONV:ROW_B_CONV + CONV_ROWS // N_CHIPS].reshape(CONV_K, CONV_DIM // N_CHIPS)
    for n in ("s5_glu_w", "conv_w"):
        delta[n], new_m[n], new_v[n] = _adamw(w[n], grads[n], m[n], v[n], "adamw_" + n)
    ds, ms, vs = _adamw(_pack_small([w[n] for n, _ in SMALL]), small_tot, _pack_small([m[n] for n, _ in SMALL]),
                        _pack_small([v[n] for n, _ in SMALL]), "adamw_small")
    delta.update(_unpack_small(ds))
    new_m.update(_unpack_small(ms))
    new_v.update(_unpack_small(vs))

    return (loss, grad_x[None], *[grads[n] for n in names], *[delta[n] for n in names],
            *[new_m[n] for n in names], *[new_v[n] for n in names])
```

```python
import functools
import math

import jax
import jax.numpy as jnp
from jax import lax
from jax.experimental import pallas as pl
from jax.experimental.pallas import tpu as pltpu

F32 = jnp.float32
BF16 = jnp.bfloat16

D_MODEL = 1024
SSD_INNER = 1024
SSD_HEADS = 16
SSD_HEADDIM = 64
SSD_GROUPS = 4
SSD_HPG = 4
SSD_STATE = 128
SSD_CHUNK = 128
CONV_K = 4
CONV_DIM = 2048
S5_WIDTH = 512
S5_STATES = 2048
S5_BLOCKS = 4
S5_CHUNK = 128
D_FF = 4096
FF_SHARDS = 4
FF_SHARD = D_FF // FF_SHARDS
EPS = 1e-6
P_Z, P_XBC, P_U5, P_G, P_DT, P_END = 0, 1024, 3072, 3584, 5632, 5760
DT_PAD = 128
OFF_DT, OFF_U = 3072, 3088
D_IN_PROJ = 5648

ADAM_LR, ADAM_B1, ADAM_B2, ADAM_EPS, ADAM_WD, ADAM_STEP = 0.001, 0.9, 0.999, 1e-08, 0.01, 10

TOKEN_TILE = 256
VMEM_LIMIT = 56 * 1024 * 1024
HALO = 8
CONV_COLS = 256
CONV_ROWS_BLK = 64
WGRAD_TOKENS = 2048


def _pc(body, **kw):
    return pl.pallas_call(body, **kw)


def _cparams(sem=None):
    return pltpu.CompilerParams(dimension_semantics=sem, vmem_limit_bytes=VMEM_LIMIT)


def _dot(a, b):
    return jnp.dot(a, b, preferred_element_type=F32)


def _dot_nt(a, b):
    return lax.dot_general(a, b, (((1,), (1,)), ((), ())), preferred_element_type=F32)


def _dot_tn(a, b):
    return lax.dot_general(a, b, (((0,), (0,)), ((), ())), preferred_element_type=F32)


def _dot_hi(a, b, dims=(((1,), (0,)), ((), ()))):
    return lax.dot_general(a, b, dims, preferred_element_type=F32, precision=lax.Precision.HIGHEST)


def _split_bf16(x, terms):
    out = []
    for _ in range(terms - 1):
        t = x.astype(BF16)
        out.append(t)
        x = x - t.astype(F32)
    out.append(x.astype(BF16))
    return out


def _dot_split(x, onehots, terms, dims=(((1,), (0,)), ((), ()))):
    acc = None
    for t in _split_bf16(x, terms):
        p = lax.dot_general(t, onehots, dims, preferred_element_type=F32)
        acc = p if acc is None else acc + p
    return acc


def _dot_split_rhs(onehots, x, terms, dims=(((1,), (0,)), ((), ()))):
    acc = None
    for t in _split_bf16(x, terms):
        p = lax.dot_general(onehots, t, dims, preferred_element_type=F32)
        acc = p if acc is None else acc + p
    return acc


def _sigmoid(x):
    return 0.5 * jnp.tanh(0.5 * x) + 0.5


def _softplus(x):
    return jnp.maximum(x, 0.0) + jnp.log(1.0 + jnp.exp(-jnp.abs(x)))


_GELU_C = math.sqrt(2.0 / math.pi)


def _gelu(x):
    return 0.5 * x * (1.0 + jnp.tanh(_GELU_C * (x + 0.044715 * x * x * x)))


def _gelu_grad(x):
    t = jnp.tanh(_GELU_C * (x + 0.044715 * x * x * x))
    return 0.5 * (1.0 + t) + 0.5 * x * (1.0 - t * t) * _GELU_C * (1.0 + 3.0 * 0.044715 * x * x)


def _rms(x):
    r = lax.rsqrt(jnp.mean(x * x, axis=-1, keepdims=True) + EPS)
    return x * r, r


def _rms_bwd(xn, r, dxn):
    return r * (dxn - xn * jnp.mean(dxn * xn, axis=-1, keepdims=True))


def _row_spec(tm, width, col=0):
    return pl.BlockSpec((tm, width), lambda i: (i, col))


def _const_spec(shape):
    nd = len(shape)
    return pl.BlockSpec(shape, lambda i: (0,) * nd)


def _hbm_spec():
    return pl.BlockSpec(memory_space=pl.ANY)


def _inproj_fwd(x, g, wp):
    T = x.shape[0]
    tm = TOKEN_TILE

    def body(x_ref, g_ref, w_hbm, z_ref, xbc_ref, u5_ref, gt_ref, dt_ref, h_ref, w_ref):
        @pl.when(pl.program_id(0) == 0)
        def _():
            pltpu.sync_copy(w_hbm, w_ref)

        xn, _ = _rms(x_ref[...])
        h = (xn * g_ref[...]).astype(BF16)
        h_ref[...] = h
        z_ref[...] = _dot(h, w_ref[:, P_Z:P_XBC])
        xbc_ref[...] = _dot(h, w_ref[:, P_XBC:P_U5])
        u5_ref[...] = _dot(h, w_ref[:, P_U5:P_G])
        gt_ref[...] = _dot(h, w_ref[:, P_G:P_DT])
        dt_ref[...] = _dot(h, w_ref[:, P_DT:P_END])

    widths = (1024, 2048, 512, 2048, DT_PAD)
    return _pc(
        body, name="inproj_fwd", grid=(T // tm,),
        in_specs=[_row_spec(tm, D_MODEL), _const_spec((1, D_MODEL)), _hbm_spec()],
        out_specs=[_row_spec(tm, w) for w in widths] + [_row_spec(tm, D_MODEL)],
        out_shape=[jax.ShapeDtypeStruct((T, w), F32) for w in widths] + [jax.ShapeDtypeStruct((T, D_MODEL), BF16)],
        scratch_shapes=[pltpu.VMEM((D_MODEL, P_END), BF16)],
        compiler_params=_cparams(("arbitrary",)),
    )(x, g, wp)


def _inproj_bwd(x, dx1, dz, dxbc, du5, dgt, ddt, g, wp, rider=None):
    T = x.shape[0]
    tm = TOKEN_TILE

    def body(x_ref, dx1_ref, dz_ref, dxbc_ref, du5_ref, dgt_ref, ddt_ref, g_ref, w_hbm, dx_ref, dg_ref, w_ref):
        @pl.when(pl.program_id(0) == 0)
        def _():
            pltpu.sync_copy(w_hbm, w_ref)
            dg_ref[...] = jnp.zeros_like(dg_ref)

        xn, r = _rms(x_ref[...])
        gv = g_ref[...]
        dh = _dot_nt(dz_ref[...].astype(BF16), w_ref[:, P_Z:P_XBC])
        dh += _dot_nt(dxbc_ref[...].astype(BF16), w_ref[:, P_XBC:P_U5])
        dh += _dot_nt(du5_ref[...].astype(BF16), w_ref[:, P_U5:P_G])
        dh += _dot_nt(dgt_ref[...].astype(BF16), w_ref[:, P_G:P_DT])
        dh += _dot_nt(ddt_ref[...].astype(BF16), w_ref[:, P_DT:P_END])
        dg_ref[...] += jnp.sum(dh * xn, axis=0, keepdims=True)
        dx_ref[...] = dx1_ref[...] + _rms_bwd(xn, r, dh * gv)

    return _call(
        body, rider, name="inproj_bwd", grid=(T // tm,),
        in_specs=[_row_spec(tm, 1024), _row_spec(tm, 1024), _row_spec(tm, 1024), _row_spec(tm, 2048),
                  _row_spec(tm, 512), _row_spec(tm, 2048), _row_spec(tm, DT_PAD), _const_spec((1, 1024)), _hbm_spec()],
        out_specs=[_row_spec(tm, 1024), _const_spec((1, 1024))],
        out_shape=[jax.ShapeDtypeStruct((T, 1024), F32), jax.ShapeDtypeStruct((1, 1024), F32)],
        scratch_shapes=[pltpu.VMEM((D_MODEL, P_END), BF16)],
        compiler_params=_cparams(("arbitrary",)),
    )(x, dx1, dz, dxbc, du5, dgt, ddt, g, wp)


def _conv_fwd(xbc_raw, dt_raw, conv_w, conv_b, dt_bias):
    T = xbc_raw.shape[0]
    tm = TOKEN_TILE

    def body(u_ref, dtr_ref, w_ref, b_ref, db_ref, act_ref, dt_ref, ext_ref):
        @pl.when(pl.program_id(0) == 0)
        def _():
            ext_ref[0:HALO, :] = jnp.zeros((HALO, CONV_DIM), F32)

        ext_ref[HALO:, :] = u_ref[...]
        for c0 in range(0, CONV_DIM, CONV_COLS):
            cols = slice(c0, c0 + CONV_COLS)
            taps = [w_ref[k:k + 1, cols] for k in range(CONV_K)]
            bias = b_ref[:, cols]
            for r0 in range(0, tm, CONV_ROWS_BLK):
                y = bias + taps[0] * ext_ref[pl.ds(HALO - (CONV_K - 1) + r0, CONV_ROWS_BLK), cols]
                for k in range(1, CONV_K):
                    y += taps[k] * ext_ref[pl.ds(HALO - (CONV_K - 1) + k + r0, CONV_ROWS_BLK), cols]
                act_ref[r0:r0 + CONV_ROWS_BLK, cols] = y * _sigmoid(y)
        ext_ref[0:HALO, :] = u_ref[tm - HALO:tm, :]
        dt_ref[...] = _softplus(dtr_ref[...] + db_ref[...])

    return _pc(
        body, name="conv_fwd", grid=(T // tm,),
        in_specs=[_row_spec(tm, CONV_DIM), _row_spec(tm, DT_PAD), _const_spec((CONV_K, CONV_DIM)),
                  _const_spec((1, CONV_DIM)), _const_spec((1, DT_PAD))],
        out_specs=[_row_spec(tm, CONV_DIM), _row_spec(tm, DT_PAD)],
        out_shape=[jax.ShapeDtypeStruct((T, CONV_DIM), F32), jax.ShapeDtypeStruct((T, DT_PAD), F32)],
        scratch_shapes=[pltpu.VMEM((tm + HALO, CONV_DIM), F32)],
        compiler_params=_cparams(("arbitrary",)),
    )(xbc_raw, dt_raw, conv_w, conv_b, dt_bias)


def _conv_bwd(xbc_raw, dt_raw, dxs_a, dxs_b, dB, dC, ddt, conv_w, conv_b, dt_bias):
    T = xbc_raw.shape[0]
    tm = TOKEN_TILE
    n = T // tm
    hb = tm // HALO

    def rev(width):
        return pl.BlockSpec((tm, width), lambda i: (n - 1 - i, 0))

    def body(u_ref, up_ref, dtr_ref, dxa_ref, dxb_ref, dB_ref, dC_ref, ddt_ref, w_ref, b_ref, db_ref,
             du_ref, ddtr_ref, dw_ref, dcb_ref, ddb_ref, ext_ref, dye_ref):
        i = pl.program_id(0)

        @pl.when(i == 0)
        def _():
            dye_ref[tm:, :] = jnp.zeros((HALO, CONV_DIM), F32)
            dw_ref[...] = jnp.zeros_like(dw_ref)
            dcb_ref[...] = jnp.zeros_like(dcb_ref)
            ddb_ref[...] = jnp.zeros_like(ddb_ref)

        first = (i == n - 1).astype(F32)
        ext_ref[0:HALO, :] = up_ref[...] * (1.0 - first)
        ext_ref[HALO:, :] = u_ref[...]
        for c0 in range(0, CONV_DIM, CONV_COLS):
            cols = slice(c0, c0 + CONV_COLS)
            taps = [w_ref[k:k + 1, cols] for k in range(CONV_K)]
            bias = b_ref[:, cols]
            acc_b = jnp.zeros((HALO, CONV_COLS), F32)
            acc_w = [jnp.zeros((HALO, CONV_COLS), F32) for _ in range(CONV_K)]
            for r0 in range(0, tm, CONV_ROWS_BLK):
                rows = slice(r0, r0 + CONV_ROWS_BLK)
                us = [ext_ref[pl.ds(HALO - (CONV_K - 1) + k + r0, CONV_ROWS_BLK), cols] for k in range(CONV_K)]
                y = bias + taps[0] * us[0]
                for k in range(1, CONV_K):
                    y += taps[k] * us[k]
                s = _sigmoid(y)
                if c0 < SSD_INNER:
                    dact = dxa_ref[rows, cols] + dxb_ref[rows, cols]
                elif c0 < SSD_INNER + 512:
                    dact = dB_ref[rows, c0 - SSD_INNER:c0 - SSD_INNER + CONV_COLS]
                else:
                    dact = dC_ref[rows, c0 - SSD_INNER - 512:c0 - SSD_INNER - 512 + CONV_COLS]
                dy = dact * (s * (1.0 + y * (1.0 - s)))
                dye_ref[rows, cols] = dy
                acc_b += jnp.sum(dy.reshape(CONV_ROWS_BLK // HALO, HALO, CONV_COLS), axis=0)
                for k in range(CONV_K):
                    acc_w[k] += jnp.sum((dy * us[k]).reshape(CONV_ROWS_BLK // HALO, HALO, CONV_COLS), axis=0)
            dcb_ref[:, cols] += jnp.sum(acc_b, axis=0, keepdims=True)
            for k in range(CONV_K):
                dw_ref[k:k + 1, cols] += jnp.sum(acc_w[k], axis=0, keepdims=True)
        for c0 in range(0, CONV_DIM, CONV_COLS):
            cols = slice(c0, c0 + CONV_COLS)
            taps = [w_ref[k:k + 1, cols] for k in range(CONV_K)]
            for r0 in range(0, tm, CONV_ROWS_BLK):
                du = taps[0] * dye_ref[pl.ds(CONV_K - 1 + r0, CONV_ROWS_BLK), cols]
                for k in range(1, CONV_K):
                    du += taps[k] * dye_ref[pl.ds(CONV_K - 1 - k + r0, CONV_ROWS_BLK), cols]
                du_ref[r0:r0 + CONV_ROWS_BLK, cols] = du.astype(BF16)
        dye_ref[tm:, :] = dye_ref[0:HALO, :]
        sg = _sigmoid(dtr_ref[...] + db_ref[...])
        ddtr = ddt_ref[...] * sg
        ddtr_ref[...] = ddtr.astype(BF16)
        ddb_ref[...] += jnp.sum(ddtr, axis=0, keepdims=True)

    prev_spec = pl.BlockSpec((HALO, CONV_DIM), lambda i: (jnp.maximum((n - 1 - i) * hb - 1, 0), 0))
    return _pc(
        body, name="conv_bwd", grid=(n,),
        in_specs=[rev(CONV_DIM), prev_spec, rev(DT_PAD), rev(1024), rev(1024), rev(512), rev(512), rev(DT_PAD),
                  _const_spec((CONV_K, CONV_DIM)), _const_spec((1, CONV_DIM)), _const_spec((1, DT_PAD))],
        out_specs=[rev(CONV_DIM), rev(DT_PAD), _const_spec((HALO, CONV_DIM)), _const_spec((1, CONV_DIM)),
                   _const_spec((1, DT_PAD))],
        out_shape=[jax.ShapeDtypeStruct((T, CONV_DIM), BF16), jax.ShapeDtypeStruct((T, DT_PAD), BF16),
                   jax.ShapeDtypeStruct((HALO, CONV_DIM), F32), jax.ShapeDtypeStruct((1, CONV_DIM), F32),
                   jax.ShapeDtypeStruct((1, DT_PAD), F32)],
        scratch_shapes=[pltpu.VMEM((tm + HALO, CONV_DIM), F32), pltpu.VMEM((tm + HALO, CONV_DIM), F32)],
        compiler_params=_cparams(("arbitrary",)),
    )(xbc_raw, xbc_raw, dt_raw, dxs_a, dxs_b, dB, dC, ddt, conv_w, conv_b, dt_bias)


GROUP_LANES = SSD_HPG * SSD_HEADDIM


def _ssd_expanders():
    head = jnp.arange(DT_PAD)[:, None]
    to_wide = (jnp.arange(SSD_INNER)[None, :] // SSD_HEADDIM == head).astype(BF16)
    return to_wide, to_wide.T


def _ssd_prep(dt_ref, alog_ref, wide_ref):
    q = SSD_CHUNK
    a = -jnp.exp(alog_ref[...])
    dtv = dt_ref[...]
    la = dtv * a
    row = lax.broadcasted_iota(jnp.int32, (q, q), 0)
    col = lax.broadcasted_iota(jnp.int32, (q, q), 1)
    tri = (col <= row).astype(BF16)
    cum = _dot_split_rhs(tri, la, 3)
    cum_t = _dot_split(la, tri, 3, (((0,), (1,)), ((), ())))
    dtw = _dot_split(dtv, wide_ref[...], 2)
    cumw = _dot_split(cum, wide_ref[...], 3)
    return a, dtv, row, col, tri, cum_t, dtw, cumw, cum


def _decay(cum, cum_t, h, keep):
    return jnp.where(keep, jnp.exp(jnp.minimum(cum[:, h:h + 1] - cum_t[h:h + 1, :], 0.0)), 0.0)


def _decay_t(cum, cum_t, h, keep_t):
    return jnp.where(keep_t, jnp.exp(jnp.minimum(cum_t[h:h + 1, :] - cum[:, h:h + 1], 0.0)), 0.0)


def _ssd_fwd(xbc_act, dt, alog):
    T = xbc_act.shape[0]
    q = SSD_CHUNK
    nc = T // q
    to_wide, _ = _ssd_expanders()

    def body(xbc_ref, dt_ref, alog_ref, wide_ref, y_ref, sp_ref, st_ref, xd_ref, xde_ref):
        @pl.when(pl.program_id(0) == 0)
        def _():
            st_ref[...] = jnp.zeros_like(st_ref)

        a, dtv, row, col, tri, cum_t, dtw, cumw, segcol = _ssd_prep(dt_ref, alog_ref, wide_ref)
        clw = cumw[q - 1:q, :]
        ecw = jnp.exp(cumw)
        xd = xbc_ref[:, 0:SSD_INNER] * dtw
        xd_ref[...] = xd.astype(BF16)
        xde_ref[...] = (xd * jnp.exp(clw - cumw)).astype(BF16)
        cdw = jnp.exp(clw)
        keep = col <= row
        sp_ref[0] = st_ref[...]
        for g in range(SSD_GROUPS):
            gl = slice(GROUP_LANES * g, GROUP_LANES * (g + 1))
            bb = xbc_ref[:, 1024 + 128 * g:1152 + 128 * g].astype(BF16)
            cb = xbc_ref[:, 1536 + 128 * g:1664 + 128 * g].astype(BF16)
            gm = _dot_nt(cb, bb)
            stp = st_ref[g]
            yoff = _dot(cb, stp.astype(BF16)) * ecw[:, gl]
            for r in range(SSD_HPG):
                h = SSD_HPG * g + r
                m = (gm * _decay(segcol, cum_t, h, keep)).astype(BF16)
                y_ref[:, 64 * h:64 * h + 64] = _dot(m, xd_ref[:, 64 * h:64 * h + 64]) + yoff[:, 64 * r:64 * r + 64]
            st_ref[g] = stp * cdw[:, gl] + _dot_tn(bb, xde_ref[:, gl])

    return _pc(
        body, name="ssd_fwd", grid=(nc,),
        in_specs=[_row_spec(q, CONV_DIM), _row_spec(q, DT_PAD), _const_spec((1, DT_PAD)),
                  _const_spec(to_wide.shape)],
        out_specs=[_row_spec(q, SSD_INNER),
                   pl.BlockSpec((1, SSD_GROUPS, SSD_STATE, GROUP_LANES), lambda i: (i, 0, 0, 0))],
        out_shape=[jax.ShapeDtypeStruct((T, SSD_INNER), F32),
                   jax.ShapeDtypeStruct((nc, SSD_GROUPS, SSD_STATE, GROUP_LANES), F32)],
        scratch_shapes=[pltpu.VMEM((SSD_GROUPS, SSD_STATE, GROUP_LANES), F32), pltpu.VMEM((q, SSD_INNER), BF16),
                        pltpu.VMEM((q, SSD_INNER), BF16)],
        compiler_params=_cparams(("arbitrary",)),
    )(xbc_act, dt, alog, to_wide)


def _ssd_bwd(xbc_act, dt, alog, sprev, dy):
    T = xbc_act.shape[0]
    q = SSD_CHUNK
    nc = T // q
    to_wide, to_heads = _ssd_expanders()

    def rev(width):
        return pl.BlockSpec((q, width), lambda i: (nc - 1 - i, 0))

    def body(xbc_ref, dt_ref, alog_ref, sp_ref, dy_ref, wide_ref, heads_ref,
             dxs_ref, dB_ref, dC_ref, ddt_ref, dalog_ref, ds_ref, xd_ref, dxd_ref):
        i = pl.program_id(0)

        @pl.when(i == 0)
        def _():
            ds_ref[...] = jnp.zeros_like(ds_ref)
            dalog_ref[...] = jnp.zeros_like(dalog_ref)

        a, dtv, row, col, tri, cum_t, dtw, cumw, segcol = _ssd_prep(dt_ref, alog_ref, wide_ref)
        clw = cumw[q - 1:q, :]
        ecw = jnp.exp(cumw)
        dew = jnp.exp(clw - cumw)
        cdw = jnp.exp(clw)
        xs = xbc_ref[:, 0:SSD_INNER]
        xd = xs * dtw
        xd_ref[...] = xd.astype(BF16)
        dyv = dy_ref[...]
        dye = (dyv * ecw).astype(BF16)
        xde = (xd * dew).astype(BF16)
        keep = col <= row
        keep_t = col >= row
        rows_k = lax.broadcasted_iota(jnp.int32, (SSD_HPG * q, DT_PAD), 0) // q
        lanes_k = lax.broadcasted_iota(jnp.int32, (SSD_HPG * q, DT_PAD), 1)
        dcw_parts = []
        dcum = jnp.zeros((q, DT_PAD), F32)
        for g in range(SSD_GROUPS):
            gl = slice(GROUP_LANES * g, GROUP_LANES * (g + 1))
            bb = xbc_ref[:, 1024 + 128 * g:1152 + 128 * g].astype(BF16)
            cb = xbc_ref[:, 1536 + 128 * g:1664 + 128 * g].astype(BF16)
            gm = _dot_nt(cb, bb)
            gmt = _dot_nt(bb, cb)
            stp = sp_ref[0, g]
            dst = ds_ref[g]
            stpb = stp.astype(BF16)
            dstb = dst.astype(BF16)
            yoff = _dot(cb, stpb) * ecw[:, gl]
            dcg = _dot_nt(dye[:, gl], stpb)
            ds_ref[g] = dst * cdw[:, gl] + _dot_tn(cb, dye[:, gl])
            dlast = jnp.sum(dst * stp, axis=0, keepdims=True) * cdw[:, gl]
            dbg = _dot_nt(xde[:, gl], dstb)
            w = _dot(bb, dstb) * dew[:, gl]
            wx = w * xd[:, gl]
            dlast = dlast + jnp.sum(wx, axis=0, keepdims=True)
            dcw_parts.append(dyv[:, gl] * yoff - wx
                             + jnp.where(lax.broadcasted_iota(jnp.int32, (q, 1), 0) == q - 1, dlast, 0.0))
            dgm = jnp.zeros((q, q), F32)
            diag = []
            for r in range(SSD_HPG):
                h = SSD_HPG * g + r
                hl = slice(64 * h, 64 * h + 64)
                dyb = dy_ref[:, hl].astype(BF16)
                xdh = xd_ref[:, hl]
                dm = _dot_nt(dyb, xdh)
                dmt = _dot_nt(xdh, dyb)
                dec = _decay(segcol, cum_t, h, keep)
                mt = gmt * _decay_t(segcol, cum_t, h, keep_t)
                dgm += dm * dec
                diag.append(dm * (gm * dec) - dmt * mt)
                dxd_ref[:, hl] = _dot(mt.astype(BF16), dyb) + w[:, 64 * r:64 * r + 64]
            onehots = (lanes_k == SSD_HPG * g + rows_k).astype(BF16)
            dcum += _dot_split(jnp.concatenate(diag, axis=1), onehots, 2)
            dgb = dgm.astype(BF16)
            dC_ref[:, 128 * g:128 * g + 128] = dcg + _dot(dgb, bb)
            dB_ref[:, 128 * g:128 * g + 128] = dbg + _dot_tn(dgb, cb)
        dxd = dxd_ref[...]
        dxs_ref[...] = dxd * dtw
        dcum += _dot_split(jnp.concatenate(dcw_parts, axis=1), heads_ref[...], 2)
        dla = _dot_split_rhs(tri, dcum, 3, (((0,), (0,)), ((), ())))
        ddt_ref[...] = _dot_split(xs * dxd, heads_ref[...], 2) + dla * a
        dalog_ref[...] += jnp.sum(dla * dtv, axis=0, keepdims=True)

        @pl.when(i == nc - 1)
        def _():
            dalog_ref[...] = dalog_ref[...] * a

    st_spec = pl.BlockSpec((1, SSD_GROUPS, SSD_STATE, GROUP_LANES), lambda i: (nc - 1 - i, 0, 0, 0))
    return _pc(
        body, name="ssd_bwd", grid=(nc,),
        in_specs=[rev(CONV_DIM), rev(DT_PAD), _const_spec((1, DT_PAD)), st_spec, rev(SSD_INNER),
                  _const_spec(to_wide.shape), _const_spec(to_heads.shape)],
        out_specs=[rev(SSD_INNER), rev(512), rev(512), rev(DT_PAD), _const_spec((1, DT_PAD))],
        out_shape=[jax.ShapeDtypeStruct((T, SSD_INNER), F32), jax.ShapeDtypeStruct((T, 512), F32),
                   jax.ShapeDtypeStruct((T, 512), F32), jax.ShapeDtypeStruct((T, DT_PAD), F32),
                   jax.ShapeDtypeStruct((1, DT_PAD), F32)],
        scratch_shapes=[pltpu.VMEM((SSD_GROUPS, SSD_STATE, GROUP_LANES), F32), pltpu.VMEM((q, SSD_INNER), BF16),
                        pltpu.VMEM((q, SSD_INNER), F32)],
        compiler_params=_cparams(("arbitrary",)),
    )(xbc_act, dt, alog, sprev, dy, to_wide, to_heads)


def _s5_disc_vals(a_re, a_im, log_dt, b_re, b_im):
    dt = jnp.exp(log_dt)
    mag = jnp.exp(a_re * dt)
    ab_re = mag * jnp.cos(a_im * dt)
    ab_im = mag * jnp.sin(a_im * dt)
    den = a_re * a_re + a_im * a_im
    nr = ab_re - 1.0
    ni = ab_im
    coef_re = (nr * a_re + ni * a_im) / den
    coef_im = (ni * a_re - nr * a_im) / den
    bb_re = coef_re * b_re - coef_im * b_im
    bb_im = coef_re * b_im + coef_im * b_re
    return ab_re, ab_im, bb_re, bb_im


def _s5_disc(a_re, a_im, log_dt, b_re, b_im):
    def body(ar, ai, ld, br, bi, o1, o2, o3, o4):
        o1[...], o2[...], o3[...], o4[...] = _s5_disc_vals(ar[...], ai[...], ld[...], br[...], bi[...])

    return _pc(
        body, name="s5_disc",
        out_shape=[jax.ShapeDtypeStruct((S5_STATES, 1), F32), jax.ShapeDtypeStruct((S5_STATES, 1), F32),
                   jax.ShapeDtypeStruct((S5_STATES, 16), F32), jax.ShapeDtypeStruct((S5_STATES, 16), F32)],
    )(a_re, a_im, log_dt, b_re, b_im)


def _s5_disc_bwd(a_re, a_im, log_dt, b_re, b_im, d_ab_re, d_ab_im, d_bb_re, d_bb_im):
    def body(ar, ai, ld, br, bi, g1, g2, g3, g4, o1, o2, o3, o4, o5):
        _, vjp = jax.vjp(_s5_disc_vals, ar[...], ai[...], ld[...], br[...], bi[...])
        d1, d2, d3, d4, d5 = vjp((g1[...], g2[...], g3[...], g4[...]))
        o1[...] = d1
        o2[...] = d2
        grp = lax.broadcasted_iota(jnp.int32, (32, S5_STATES), 0)
        st = lax.broadcasted_iota(jnp.int32, (32, S5_STATES), 1)
        sel = (st // 64 == grp).astype(F32)
        o3[...] = _dot_hi(sel, d3)
        o4[...] = d4
        o5[...] = d5

    return _pc(
        body, name="s5_disc_bwd",
        out_shape=[jax.ShapeDtypeStruct((S5_STATES, 1), F32), jax.ShapeDtypeStruct((S5_STATES, 1), F32),
                   jax.ShapeDtypeStruct((32, 1), F32),
                   jax.ShapeDtypeStruct((S5_STATES, 16), F32), jax.ShapeDtypeStruct((S5_STATES, 16), F32)],
    )(a_re, a_im, log_dt, b_re, b_im, d_ab_re, d_ab_im, d_bb_re, d_bb_im)


def _cmul_add(xr, xi, pr, pi, yr, yi):
    return xr + pr * yr - pi * yi, xi + pr * yi + pi * yr


def _powers(ar, ai, n):
    out = [(ar, ai)]
    for _ in range(n - 1):
        pr, pi = out[-1]
        out.append((pr * pr - pi * pi, 2.0 * pr * pi))
    return out


_BW = S5_STATES // S5_BLOCKS
_BI = S5_WIDTH // S5_BLOCKS
SUB = 8
S5_ROWS = S5_CHUNK // SUB


S5_TAB_ROWS = 8 * SUB


def _scan8(br, bi, tab_ref, reverse):
    for level, k in enumerate((1, 2, 4)):
        r0 = 2 * SUB * (level + 1)
        shift = SUB - k if reverse else k
        br, bi = _cmul_add(br, bi, tab_ref[r0:r0 + SUB, :], tab_ref[r0 + SUB:r0 + 2 * SUB, :],
                           pltpu.roll(br, shift, 0), pltpu.roll(bi, shift, 0))
    return br, bi


def _s5_tables(ab_ref, tab_ref, reverse):
    rowin = lax.broadcasted_iota(jnp.int32, (SUB, 1), 0)
    ar = ab_ref[0:1, :]
    ai = -ab_ref[1:2, :] if reverse else ab_ref[1:2, :]
    zero = jnp.zeros((SUB, S5_STATES), F32)
    for level, (pr, pi) in enumerate(_powers(ar, ai, 3)):
        k = 2 ** level
        keep = (rowin < SUB - k) if reverse else (rowin >= k)
        r0 = 2 * SUB * (level + 1)
        tab_ref[r0:r0 + SUB, :] = jnp.where(keep, pr, 0.0) + zero
        tab_ref[r0 + SUB:r0 + 2 * SUB, :] = jnp.where(keep, pi, 0.0) + zero
    hit = rowin == (SUB - 1 if reverse else 0)
    pr, pi = _scan8(jnp.where(hit, ar, 0.0) + zero, jnp.where(hit, ai, 0.0) + zero, tab_ref, reverse)
    tab_ref[0:SUB, :] = pr
    tab_ref[SUB:2 * SUB, :] = pi


def _s5_fwd(u5, wb4, wc4, ab, dvec, rider=None):
    T = u5.shape[0]
    q = S5_CHUNK
    nc = T // q

    def body(u_ref, wb_ref, wc_ref, ab_ref, d_ref, y_ref, sp_ref, carry_ref, tab_ref, sr_ref, si_ref):
        i = pl.program_id(0)
        rowin = lax.broadcasted_iota(jnp.int32, (SUB, 1), 0)

        @pl.when(i == 0)
        def _():
            carry_ref[...] = jnp.zeros_like(carry_ref)
            _s5_tables(ab_ref, tab_ref, False)

        sp_ref[0] = carry_ref[...]
        for j in range(S5_BLOCKS):
            bu = _dot(u_ref[:, _BI * j:_BI * (j + 1)].astype(BF16), wb_ref[j])
            sr_ref[:, :, _BW * j:_BW * (j + 1)] = bu[:, :_BW].reshape(S5_ROWS, SUB, _BW)
            si_ref[:, :, _BW * j:_BW * (j + 1)] = bu[:, _BW:].reshape(S5_ROWS, SUB, _BW)
        tr, ti = tab_ref[0:SUB, :], tab_ref[SUB:2 * SUB, :]
        cr, ci = carry_ref[0:1, :], carry_ref[1:2, :]
        for k in range(S5_ROWS):
            sr, si = _scan8(sr_ref[k], si_ref[k], tab_ref, False)
            sr, si = _cmul_add(sr, si, tr, ti, cr, ci)
            sr_ref[k] = sr
            si_ref[k] = si
            cr, ci = sr[SUB - 1:SUB, :], si[SUB - 1:SUB, :]
        carry_ref[0:1, :] = cr
        carry_ref[1:2, :] = ci
        for j in range(S5_BLOCKS):
            sl = slice(_BW * j, _BW * (j + 1))
            ul = slice(_BI * j, _BI * (j + 1))
            s = jnp.concatenate([sr_ref[:, :, sl].reshape(q, _BW), si_ref[:, :, sl].reshape(q, _BW)], axis=1).astype(BF16)
            y_ref[:, ul] = _dot(s, wc_ref[j]) + d_ref[:, ul] * u_ref[:, ul]

    return _call(
        body, rider, name="s5_fwd", grid=(nc,),
        in_specs=[_row_spec(q, S5_WIDTH), _const_spec((S5_BLOCKS, _BI, 2 * _BW)), _const_spec((S5_BLOCKS, 2 * _BW, _BI)),
                  _const_spec((8, S5_STATES)), _const_spec((1, S5_WIDTH))],
        out_specs=[_row_spec(q, S5_WIDTH), pl.BlockSpec((1, 8, S5_STATES), lambda i: (i, 0, 0))],
        out_shape=[jax.ShapeDtypeStruct((T, S5_WIDTH), F32), jax.ShapeDtypeStruct((nc, 8, S5_STATES), F32)],
        scratch_shapes=[pltpu.VMEM((8, S5_STATES), F32), pltpu.VMEM((S5_TAB_ROWS, S5_STATES), F32),
                        pltpu.VMEM((S5_ROWS, SUB, S5_STATES), F32), pltpu.VMEM((S5_ROWS, SUB, S5_STATES), F32)],
        compiler_params=_cparams(("arbitrary",)),
    )(u5, wb4, wc4, ab, dvec)


def _s5_bwd(u5, dy5, wb4, wc4, ab, dvec, sprev, rider=None):
    T = u5.shape[0]
    q = S5_CHUNK
    nc = T // q

    def rev(width):
        return pl.BlockSpec((q, width), lambda i: (nc - 1 - i, 0))

    def body(u_ref, dy_ref, wb_ref, wc_ref, ab_ref, d_ref, sp_ref, du_ref, dwb_ref, dwc_ref, dab_ref, dd_ref,
             carry_ref, tab_ref, rtab_ref, sr_ref, si_ref, lr_ref, li_ref):
        i = pl.program_id(0)
        rowin = lax.broadcasted_iota(jnp.int32, (SUB, 1), 0)

        @pl.when(i == 0)
        def _():
            carry_ref[...] = jnp.zeros_like(carry_ref)
            dwb_ref[...] = jnp.zeros_like(dwb_ref)
            dwc_ref[...] = jnp.zeros_like(dwc_ref)
            dab_ref[...] = jnp.zeros_like(dab_ref)
            dd_ref[...] = jnp.zeros_like(dd_ref)
            _s5_tables(ab_ref, tab_ref, False)
            _s5_tables(ab_ref, rtab_ref, True)

        for j in range(S5_BLOCKS):
            sl = slice(_BW * j, _BW * (j + 1))
            ul = slice(_BI * j, _BI * (j + 1))
            bu = _dot(u_ref[:, ul].astype(BF16), wb_ref[j])
            sr_ref[:, :, sl] = bu[:, :_BW].reshape(S5_ROWS, SUB, _BW)
            si_ref[:, :, sl] = bu[:, _BW:].reshape(S5_ROWS, SUB, _BW)
            ds = _dot_nt(dy_ref[:, ul].astype(BF16), wc_ref[j])
            lr_ref[:, :, sl] = ds[:, :_BW].reshape(S5_ROWS, SUB, _BW)
            li_ref[:, :, sl] = ds[:, _BW:].reshape(S5_ROWS, SUB, _BW)
        ar, ai = ab_ref[0:1, :], ab_ref[1:2, :]
        tr, ti = tab_ref[0:SUB, :], tab_ref[SUB:2 * SUB, :]
        cr, ci = sp_ref[0, 0:1, :], sp_ref[0, 1:2, :]
        for k in range(S5_ROWS):
            sr, si = _scan8(sr_ref[k], si_ref[k], tab_ref, False)
            sr, si = _cmul_add(sr, si, tr, ti, cr, ci)
            sr_ref[k] = sr
            si_ref[k] = si
            cr, ci = sr[SUB - 1:SUB, :], si[SUB - 1:SUB, :]
        tr, ti = rtab_ref[0:SUB, :], rtab_ref[SUB:2 * SUB, :]
        cr, ci = carry_ref[0:1, :], carry_ref[1:2, :]
        acc_r = jnp.zeros((SUB, S5_STATES), F32)
        acc_i = jnp.zeros((SUB, S5_STATES), F32)
        for k in reversed(range(S5_ROWS)):
            lr, li = _scan8(lr_ref[k], li_ref[k], rtab_ref, True)
            lr, li = _cmul_add(lr, li, tr, ti, cr, ci)
            lr_ref[k] = lr
            li_ref[k] = li
            cr, ci = lr[0:1, :], li[0:1, :]
            if k > 0:
                before_r, before_i = sr_ref[k - 1, SUB - 1:SUB, :], si_ref[k - 1, SUB - 1:SUB, :]
            else:
                before_r, before_i = sp_ref[0, 0:1, :], sp_ref[0, 1:2, :]
            keep = rowin >= 1
            pr = jnp.where(keep, pltpu.roll(sr_ref[k], 1, 0), before_r)
            pi = jnp.where(keep, pltpu.roll(si_ref[k], 1, 0), before_i)
            acc_r += lr * pr + li * pi
            acc_i += li * pr - lr * pi
        carry_ref[0:1, :] = cr
        carry_ref[1:2, :] = ci
        dab_ref[0:1, :] += jnp.sum(acc_r, axis=0, keepdims=True)
        dab_ref[1:2, :] += jnp.sum(acc_i, axis=0, keepdims=True)
        for j in range(S5_BLOCKS):
            sl = slice(_BW * j, _BW * (j + 1))
            ul = slice(_BI * j, _BI * (j + 1))
            u = u_ref[:, ul]
            dy = dy_ref[:, ul]
            dyb = dy.astype(BF16)
            lam = jnp.concatenate([lr_ref[:, :, sl].reshape(q, _BW), li_ref[:, :, sl].reshape(q, _BW)], axis=1).astype(BF16)
            s = jnp.concatenate([sr_ref[:, :, sl].reshape(q, _BW), si_ref[:, :, sl].reshape(q, _BW)], axis=1).astype(BF16)
            du_ref[:, ul] = (_dot_nt(lam, wb_ref[j]) + d_ref[:, ul] * dy).astype(BF16)
            dwb_ref[j] += _dot_tn(u.astype(BF16), lam)
            dwc_ref[j] += _dot_tn(s, dyb)
            dd_ref[:, ul] += jnp.sum(dy * u, axis=0, keepdims=True)

    big = pltpu.VMEM((S5_ROWS, SUB, S5_STATES), F32)
    return _call(
        body, rider, name="s5_bwd", grid=(nc,),
        in_specs=[rev(S5_WIDTH), rev(S5_WIDTH), _const_spec((S5_BLOCKS, _BI, 2 * _BW)), _const_spec((S5_BLOCKS, 2 * _BW, _BI)),
                  _const_spec((8, S5_STATES)), _const_spec((1, S5_WIDTH)),
                  pl.BlockSpec((1, 8, S5_STATES), lambda i: (nc - 1 - i, 0, 0))],
        out_specs=[rev(S5_WIDTH), _const_spec((S5_BLOCKS, _BI, 2 * _BW)), _const_spec((S5_BLOCKS, 2 * _BW, _BI)),
                   _const_spec((8, S5_STATES)), _const_spec((1, S5_WIDTH))],
        out_shape=[jax.ShapeDtypeStruct((T, S5_WIDTH), BF16), jax.ShapeDtypeStruct((S5_BLOCKS, _BI, 2 * _BW), F32),
                   jax.ShapeDtypeStruct((S5_BLOCKS, 2 * _BW, _BI), F32), jax.ShapeDtypeStruct((8, S5_STATES), F32),
                   jax.ShapeDtypeStruct((1, S5_WIDTH), F32)],
        scratch_shapes=[pltpu.VMEM((8, S5_STATES), F32), pltpu.VMEM((S5_TAB_ROWS, S5_STATES), F32),
                        pltpu.VMEM((S5_TAB_ROWS, S5_STATES), F32), big, big, big, big],
        compiler_params=_cparams(("arbitrary",)),
    )(u5, dy5, wb4, wc4, ab, dvec, sprev)


def _merge_vals(ys, xs, z, y5, gates, dvec, gssd, glu_w, glu_b, wbr):
    sz = _sigmoid(z)
    qv = ys + dvec * xs
    pre = qv * (z * sz)
    yn, rs = [], []
    for gi in range(SSD_GROUPS):
        p, r = _rms(pre[:, 256 * gi:256 * (gi + 1)])
        yn.append(p)
        rs.append(r)
    yn = jnp.concatenate(yn, axis=1)
    ya = yn * gssd
    gel = _gelu(y5)
    sg = _sigmoid(_dot(gel.astype(BF16), glu_w) + glu_b)
    yb = gel * sg
    pa = _dot(ya.astype(BF16), wbr[0:SSD_INNER, :])
    pb = _dot(yb.astype(BF16), wbr[SSD_INNER:, :])
    s0 = _sigmoid(gates[:, :D_MODEL])
    s1 = _sigmoid(gates[:, D_MODEL:])
    merged = s0 * pa + s1 * pb
    return dict(sz=sz, qv=qv, yn=yn, rs=rs, ya=ya, gel=gel, sg=sg, yb=yb, pa=pa, pb=pb, s0=s0, s1=s1, merged=merged)


def _merge_specs(tm):
    acts = [_row_spec(tm, 1024), _row_spec(tm, 1024, 0), _row_spec(tm, 1024), _row_spec(tm, 512), _row_spec(tm, 2048),
            _row_spec(tm, 1024)]
    params = [_const_spec((1, 1024)), _const_spec((1, 1024)), _const_spec((512, 512)), _const_spec((1, 512)),
              _hbm_spec(), _hbm_spec()]
    return acts, params


def _merge_fwd(ys, xbc_act, z, y5, gates, x, dvec, gssd, glu_w, glu_b, wbr, wout):
    T = x.shape[0]
    tm = TOKEN_TILE
    acts, params = _merge_specs(tm)

    def body(ys_ref, xs_ref, z_ref, y5_ref, gt_ref, x_ref, dv_ref, gs_ref, gw_ref, gb_ref, wbr_hbm, wout_hbm, x1_ref,
             wbr_ref, wout_ref):
        @pl.when(pl.program_id(0) == 0)
        def _():
            pltpu.sync_copy(wbr_hbm, wbr_ref)
            pltpu.sync_copy(wout_hbm, wout_ref)

        v = _merge_vals(ys_ref[...], xs_ref[...], z_ref[...], y5_ref[...], gt_ref[...], dv_ref[...], gs_ref[...],
                        gw_ref[...], gb_ref[...], wbr_ref)
        x1_ref[...] = x_ref[...] + _dot(v["merged"].astype(BF16), wout_ref[...])

    return _pc(
        body, name="merge_fwd", grid=(T // tm,),
        in_specs=acts + params, out_specs=_row_spec(tm, 1024),
        out_shape=jax.ShapeDtypeStruct((T, 1024), F32),
        scratch_shapes=[pltpu.VMEM((1536, 1024), BF16), pltpu.VMEM((1024, 1024), BF16)],
        compiler_params=_cparams(("arbitrary",)),
    )(ys, xbc_act, z, y5, gates, x, dvec, gssd, glu_w, glu_b, wbr, wout)


def _merge_bwd(ys, xbc_act, z, y5, gates, dx1, dvec, gssd, glu_w, glu_b, wbr, wout, head_sel, rider=None):
    T = dx1.shape[0]
    tm = TOKEN_TILE
    acts, params = _merge_specs(tm)

    def body(ys_ref, xs_ref, z_ref, y5_ref, gt_ref, dx1_ref, dv_ref, gs_ref, gw_ref, gb_ref, wbr_hbm, wout_hbm, hs_ref,
             dys_ref, dxs_ref, dz_ref, dy5_ref, dgt_ref, mg_ref, ya_ref, yb_ref, dpa_ref, dpb_ref, gel_ref, dpre_ref,
             ddv_ref, dgs_ref, dgb_ref, wbr_ref, wout_ref, ddacc_ref):
        i = pl.program_id(0)

        @pl.when(i == 0)
        def _():
            pltpu.sync_copy(wbr_hbm, wbr_ref)
            pltpu.sync_copy(wout_hbm, wout_ref)
            ddacc_ref[...] = jnp.zeros_like(ddacc_ref)
            dgs_ref[...] = jnp.zeros_like(dgs_ref)
            dgb_ref[...] = jnp.zeros_like(dgb_ref)

        ys, xs, z, y5, gates = ys_ref[...], xs_ref[...], z_ref[...], y5_ref[...], gt_ref[...]
        dvv, gsv, gw = dv_ref[...], gs_ref[...], gw_ref[...]
        v = _merge_vals(ys, xs, z, y5, gates, dvv, gsv, gw, gb_ref[...], wbr_ref)
        dmg = _dot_nt(dx1_ref[...].astype(BF16), wout_ref[...])
        s0, s1, pa, pb = v["s0"], v["s1"], v["pa"], v["pb"]
        dgt_ref[:, :D_MODEL] = (dmg * pa * s0 * (1.0 - s0)).astype(BF16)
        dgt_ref[:, D_MODEL:] = (dmg * pb * s1 * (1.0 - s1)).astype(BF16)
        dpa = (dmg * s0).astype(BF16)
        dpb = (dmg * s1).astype(BF16)
        dya = _dot_nt(dpa, wbr_ref[0:SSD_INNER, :])
        dyb = _dot_nt(dpb, wbr_ref[SSD_INNER:, :])
        gel, sg = v["gel"], v["sg"]
        dpre = (dyb * gel * sg * (1.0 - sg))
        dgb_ref[...] += jnp.sum(dpre, axis=0, keepdims=True)
        dpre_b = dpre.astype(BF16)
        dgel = dyb * sg + _dot_nt(dpre_b, gw)
        dy5_ref[...] = dgel * _gelu_grad(y5)
        yn = v["yn"]
        dgs_ref[...] += jnp.sum(dya * yn, axis=0, keepdims=True)
        dyn = dya * gsv
        dpre_a = jnp.concatenate(
            [_rms_bwd(yn[:, 256 * gi:256 * (gi + 1)], v["rs"][gi], dyn[:, 256 * gi:256 * (gi + 1)])
             for gi in range(SSD_GROUPS)], axis=1)
        sz, qv = v["sz"], v["qv"]
        dq = dpre_a * (z * sz)
        dz_ref[...] = (dpre_a * qv * (sz * (1.0 + z * (1.0 - sz)))).astype(BF16)
        dys_ref[...] = dq
        dxs_ref[...] = dq * dvv
        ddacc_ref[...] += jnp.sum(dq * xs, axis=0, keepdims=True)
        mg_ref[...] = v["merged"].astype(BF16)
        ya_ref[...] = v["ya"].astype(BF16)
        yb_ref[...] = v["yb"].astype(BF16)
        dpa_ref[...] = dpa
        dpb_ref[...] = dpb
        gel_ref[...] = gel.astype(BF16)
        dpre_ref[...] = dpre_b

        @pl.when(i == pl.num_programs(0) - 1)
        def _():
            ddv_ref[...] = _dot_hi(ddacc_ref[...], hs_ref[...])

    outs = [(1024, F32), (1024, F32), (1024, BF16), (512, F32), (2048, BF16),
            (1024, BF16), (1024, BF16), (512, BF16), (1024, BF16), (1024, BF16), (512, BF16), (512, BF16)]
    return _call(
        body, rider, name="merge_bwd", grid=(T // tm,),
        in_specs=acts + params + [_const_spec((1024, DT_PAD))],
        out_specs=[_row_spec(tm, w) for w, _ in outs] + [_const_spec((1, DT_PAD)), _const_spec((1, 1024)), _const_spec((1, 512))],
        out_shape=[jax.ShapeDtypeStruct((T, w), d) for w, d in outs] + [
            jax.ShapeDtypeStruct((1, DT_PAD), F32), jax.ShapeDtypeStruct((1, 1024), F32), jax.ShapeDtypeStruct((1, 512), F32)],
        scratch_shapes=[pltpu.VMEM((1536, 1024), BF16), pltpu.VMEM((1024, 1024), BF16), pltpu.VMEM((1, 1024), F32)],
        compiler_params=_cparams(("arbitrary",)),
    )(ys, xbc_act, z, y5, gates, dx1, dvec, gssd, glu_w, glu_b, wbr, wout, head_sel)


def _mlp_fwd_loss(x1, target, g, g_fin, w1, w2):
    T = x1.shape[0]
    tm = TOKEN_TILE

    def body(x_ref, t_ref, g_ref, gf_ref, w1_hbm, w2_hbm, dx_ref, loss_ref, dg_ref, w1_ref, w2_ref):
        @pl.when(pl.program_id(0) == 0)
        def _():
            pltpu.sync_copy(w1_hbm, w1_ref)
            pltpu.sync_copy(w2_hbm, w2_ref)
            loss_ref[...] = jnp.zeros_like(loss_ref)
            dg_ref[...] = jnp.zeros_like(dg_ref)

        xv = x_ref[...]
        xn, _ = _rms(xv)
        h = (xn * g_ref[...]).astype(BF16)
        acc = xv
        for s in range(FF_SHARDS):
            rl = jnp.maximum(_dot(h, w1_ref[s]), 0.0)
            acc += _dot((rl * rl).astype(BF16), w2_ref[FF_SHARD * s:FF_SHARD * (s + 1), :])
        yn, r = _rms(acc)
        gv = gf_ref[...]
        err = yn * gv - t_ref[...]
        loss_ref[...] += jnp.sum(err * err, axis=0, keepdims=True) * (0.5 / D_MODEL)
        dy = err * (1.0 / D_MODEL)
        dg_ref[...] += jnp.sum(dy * yn, axis=0, keepdims=True)
        dx_ref[...] = _rms_bwd(yn, r, dy * gv)

    return _pc(
        body, name="mlp_fwd_loss", grid=(T // tm,),
        in_specs=[_row_spec(tm, 1024), _row_spec(tm, 1024), _const_spec((1, 1024)), _const_spec((1, 1024)),
                  _hbm_spec(), _hbm_spec()],
        out_specs=[_row_spec(tm, 1024), _const_spec((1, 1024)), _const_spec((1, 1024))],
        out_shape=[jax.ShapeDtypeStruct((T, 1024), F32), jax.ShapeDtypeStruct((1, 1024), F32),
                   jax.ShapeDtypeStruct((1, 1024), F32)],
        scratch_shapes=[pltpu.VMEM((FF_SHARDS, D_MODEL, FF_SHARD), BF16), pltpu.VMEM((D_FF, D_MODEL), BF16)],
        compiler_params=_cparams(("arbitrary",)),
    )(x1, target, g, g_fin, w1, w2)


def _mlp_bwd(x1, dx2, g, w1, w2):
    T = x1.shape[0]
    tm = TOKEN_TILE

    def body(x_ref, dx2_ref, g_ref, w1_hbm, w2_hbm, dx1_ref, h_ref, act_ref, da_ref, dg_ref, w1_ref, w2_ref):
        @pl.when(pl.program_id(0) == 0)
        def _():
            pltpu.sync_copy(w1_hbm, w1_ref)
            pltpu.sync_copy(w2_hbm, w2_ref)
            dg_ref[...] = jnp.zeros_like(dg_ref)

        xn, r = _rms(x_ref[...])
        gv = g_ref[...]
        h = (xn * gv).astype(BF16)
        h_ref[...] = h
        dx2 = dx2_ref[...]
        dx2b = dx2.astype(BF16)
        dh = jnp.zeros((tm, D_MODEL), F32)
        for s in range(FF_SHARDS):
            ff = slice(FF_SHARD * s, FF_SHARD * (s + 1))
            rl = jnp.maximum(_dot(h, w1_ref[s]), 0.0)
            act_ref[:, ff] = (rl * rl).astype(BF16)
            da = (_dot_nt(dx2b, w2_ref[ff, :]) * (2.0 * rl)).astype(BF16)
            da_ref[:, ff] = da
            dh += _dot_nt(da, w1_ref[s])
        dg_ref[...] += jnp.sum(dh * xn, axis=0, keepdims=True)
        dx1_ref[...] = dx2 + _rms_bwd(xn, r, dh * gv)

    return _pc(
        body, name="mlp_bwd", grid=(T // tm,),
        in_specs=[_row_spec(tm, 1024), _row_spec(tm, 1024), _const_spec((1, 1024)), _hbm_spec(), _hbm_spec()],
        out_specs=[_row_spec(tm, 1024), _row_spec(tm, 1024), _row_spec(tm, D_FF), _row_spec(tm, D_FF), _const_spec((1, 1024))],
        out_shape=[jax.ShapeDtypeStruct((T, 1024), F32), jax.ShapeDtypeStruct((T, 1024), BF16),
                   jax.ShapeDtypeStruct((T, D_FF), BF16), jax.ShapeDtypeStruct((T, D_FF), BF16),
                   jax.ShapeDtypeStruct((1, 1024), F32)],
        scratch_shapes=[pltpu.VMEM((FF_SHARDS, D_MODEL, FF_SHARD), BF16), pltpu.VMEM((D_FF, D_MODEL), BF16)],
        compiler_params=_cparams(("arbitrary",)),
    )(x1, dx2, g, w1, w2)


WGRAD_OUT_ELEMS = 2 * 1024 * 1024
WGRAD_TILE_BYTES = 4 * 1024 * 1024


def _wgrad(a, b, name, col_shards=None, row_shards_into=None):
    T, K = a.shape
    N = b.shape[1]
    nb = N // col_shards if col_shards else min(N, 1024, max(128, WGRAD_OUT_ELEMS // K))
    tt = min(T, WGRAD_TOKENS)
    while tt * max(K * a.dtype.itemsize, nb * b.dtype.itemsize) > WGRAD_TILE_BYTES:
        tt //= 2
    assert N % nb == 0 and T % tt == 0
    in_specs = [pl.BlockSpec((tt, K), lambda n, t: (t, 0)), pl.BlockSpec((tt, nb), lambda n, t: (t, n))]
    args, aliases = [a, b], {}
    if col_shards:
        out_spec = pl.BlockSpec((None, None, K, nb), lambda n, t: (n, 0, 0, 0))
        out_shape = jax.ShapeDtypeStruct((col_shards, 2, K, nb), F32)
    elif row_shards_into is not None:
        shards, _, rows, cols = row_shards_into.shape
        assert shards * rows == K and cols == N
        out_spec = pl.BlockSpec((shards, None, rows, nb), lambda n, t: (0, 1, 0, n))
        out_shape = jax.ShapeDtypeStruct(row_shards_into.shape, F32)
        in_specs.append(_hbm_spec())
        args.append(row_shards_into)
        aliases = {2: 0}
    else:
        out_spec = pl.BlockSpec((K, nb), lambda n, t: (0, n))
        out_shape = jax.ShapeDtypeStruct((K, N), F32)

    def body(a_ref, b_ref, *rest):
        o_ref = rest[-1]

        @pl.when(pl.program_id(1) == 0)
        def _():
            o_ref[...] = jnp.zeros_like(o_ref)

        o_ref[...] += _dot_tn(a_ref[...].astype(BF16), b_ref[...].astype(BF16)).reshape(o_ref.shape)

    return _pc(
        body, name=name, grid=(N // nb, T // tt), in_specs=in_specs, out_specs=out_spec, out_shape=out_shape,
        input_output_aliases=aliases, compiler_params=_cparams(("parallel", "arbitrary")),
    )(*args)


def _s5_block_weights(bb_re, bb_im, c_re, c_im):
    eye = jnp.eye(8, dtype=F32)
    bre = bb_re.reshape(S5_BLOCKS, 8, 64, 16)
    bim = bb_im.reshape(S5_BLOCKS, 8, 64, 16)
    wb_re = jnp.einsum('jgpk,gh->jhkgp', bre, eye).reshape(S5_BLOCKS, _BI, _BW)
    wb_im = jnp.einsum('jgpk,gh->jhkgp', bim, eye).reshape(S5_BLOCKS, _BI, _BW)
    wb4 = jnp.concatenate([wb_re, wb_im], axis=2).astype(BF16)
    cre = c_re.reshape(S5_BLOCKS, 8, 16, 64)
    cim = c_im.reshape(S5_BLOCKS, 8, 16, 64)
    wc_re = jnp.einsum('jgkp,gh->jgphk', cre, eye).reshape(S5_BLOCKS, _BW, _BI)
    wc_im = jnp.einsum('jgkp,gh->jgphk', -cim, eye).reshape(S5_BLOCKS, _BW, _BI)
    wc4 = jnp.concatenate([wc_re, wc_im], axis=1).astype(BF16)
    return wb4, wc4


def _s5_block_grads(dwb4, dwc4):
    eye = jnp.eye(8, dtype=F32)
    dwb = dwb4.reshape(S5_BLOCKS, 8, 16, 2, 8, 64)
    dbb = jnp.einsum('jhkrgp,gh->rjgpk', dwb, eye).reshape(2, 32, 64, 16)
    dwc = dwc4.reshape(S5_BLOCKS, 2, 8, 64, 8, 16)
    dc = jnp.einsum('jrgphk,gh->rjgkp', dwc, eye).reshape(2, 32, 16, 64)
    return dbb[0], dbb[1], dc[0], -dc[1]


def _row(v, width=None):
    v = v.reshape(1, -1)
    if width is not None and v.shape[1] < width:
        v = jnp.concatenate([v, jnp.zeros((1, width - v.shape[1]), v.dtype)], axis=1)
    return v


def _local_step(x, target, p, comm=None):
    g_mix, g_mlp, g_fin = _row(p["norm_mix_g"]), _row(p["norm_mlp_g"]), _row(p["norm_final_g"])
    conv_b = _row(p["conv_b"])
    dt_bias = _row(p["dt_bias"], DT_PAD)
    alog = _row(p["a_log"], DT_PAD)
    dvec = _row(jnp.repeat(p["d_ssd"], SSD_HEADDIM))
    gssd = _row(p["ssd_norm_g"])
    s5d = _row(p["s5_d"])
    glu_b = _row(p["s5_glu_b"])
    head_sel = (jnp.arange(SSD_INNER)[:, None] // SSD_HEADDIM == jnp.arange(DT_PAD)[None, :]).astype(F32)

    a_re = p["s5_a_re"].reshape(S5_STATES, 1)
    a_im = p["s5_a_im"].reshape(S5_STATES, 1)
    log_dt = jnp.repeat(p["s5_log_dt"], 64).reshape(S5_STATES, 1)
    b_re = p["s5_b_re"].reshape(S5_STATES, 16)
    b_im = p["s5_b_im"].reshape(S5_STATES, 16)
    ab_re, ab_im, bb_re, bb_im = _s5_disc(a_re, a_im, log_dt, b_re, b_im)
    wb4, wc4 = _s5_block_weights(bb_re, bb_im, p["s5_c_re"], p["s5_c_im"])
    ab = jnp.concatenate([ab_re.reshape(1, S5_STATES), ab_im.reshape(1, S5_STATES), jnp.zeros((6, S5_STATES), F32)], axis=0)

    wp = p["w_in_perm"]

    z, xbc_raw, u5, gates, dt_raw, h = _inproj_fwd(x, g_mix, wp)
    xbc_act, dt = _conv_fwd(xbc_raw, dt_raw, p["conv_w"], conv_b, dt_bias)
    ys, ssd_states = _ssd_fwd(xbc_act, dt, alog)
    if comm is None:
        y5, s5_states = _s5_fwd(u5, wb4, wc4, ab, s5d)
    else:
        (y5, s5_states), late = _s5_fwd(u5, wb4, wc4, ab, s5d, rider=_Gather(comm["late_srcs"], comm["late_ks"]))
        p = {**p, **comm["late_unpack"](late)}
    wbr, wout, w1, w2, glu_w = p["w_branch"], p["w_out"], p["w_mlp_in"], p["w_mlp_out"], p["s5_glu_w"]
    x1 = _merge_fwd(ys, xbc_act, z, y5, gates, x, dvec, gssd, glu_w, glu_b, wbr, wout)
    dx2, loss_lanes, d_gfin = _mlp_fwd_loss(x1, target, g_mlp, g_fin, w1, w2)

    dx1, h2, act, da1, d_gmlp = _mlp_bwd(x1, dx2, g_mlp, w1, w2)
    g_mlp4 = _wgrad(h2, da1, "wgrad_mlp_in", col_shards=FF_SHARDS)
    g_mlp4 = _wgrad(act, dx2, "wgrad_mlp_out", row_shards_into=g_mlp4)
    d_w_mlp_in, d_w_mlp_out = g_mlp4[:, 0], g_mlp4[:, 1].reshape(D_FF, D_MODEL)
    merge_args = (ys, xbc_act, z, y5, gates, dx1, dvec, gssd, glu_w, glu_b, wbr, wout, head_sel)
    if comm is None:
        merge_out = _merge_bwd(*merge_args)
    else:
        g_mlp = g_mlp4.reshape(N_CHIPS, 2 * FF_SHARD, D_MODEL)
        merge_out, (sib_mlp,) = _merge_bwd(*merge_args, rider=_Pair([g_mlp]))
        pf_mlp, pb_mlp = _pair_sum(comm["place"], g_mlp, sib_mlp, "pair_sum_mlp")
    (dys, dxs_m, dz, dy5, dgates, mg, ya, yb, dpa, dpb, gel, dpre, d_dssd, d_gssd, d_glu_b) = merge_out
    d_w_out = _wgrad(mg, dx1, "wgrad_out")
    d_w_branch = jnp.concatenate([_wgrad(ya, dpa, "wgrad_branch_a"), _wgrad(yb, dpb, "wgrad_branch_b")], axis=0)
    d_glu_w = _wgrad(gel, dpre, "wgrad_glu")
    s5_args = (u5, dy5, wb4, wc4, ab, s5d, s5_states)
    if comm is None:
        du5, dwb4, dwc4, dab, d_s5d = _s5_bwd(*s5_args)
        mlp_total = None
    else:
        (du5, dwb4, dwc4, dab, d_s5d), (got_mlp,) = _s5_bwd(*s5_args, rider=_Chip([pb_mlp]))
        (mlp_total,) = _chip_sum(comm["place"], pf_mlp, got_mlp, "chip_sum_mlp")
    dbb_re, dbb_im, d_c_re, d_c_im = _s5_block_grads(dwb4, dwc4)
    d_a_re, d_a_im, d_log_dt, d_b_re, d_b_im = _s5_disc_bwd(
        a_re, a_im, log_dt, b_re, b_im, dab[0].reshape(S5_STATES, 1), dab[1].reshape(S5_STATES, 1),
        dbb_re.reshape(S5_STATES, 16), dbb_im.reshape(S5_STATES, 16))
    dxs_s, dB, dC, ddt, d_alog = _ssd_bwd(xbc_act, dt, alog, ssd_states, dys)
    dxbc_raw, ddt_raw, d_conv_w, d_conv_b, d_dt_bias = _conv_bwd(
        xbc_raw, dt_raw, dxs_m, dxs_s, dB, dC, ddt, p["conv_w"], conv_b, dt_bias)
    d_w_in = dict(z=_wgrad(h, dz, "wgrad_in_z"), xbc=_wgrad(h, dxbc_raw, "wgrad_in_xbc"),
                  dt=_wgrad(h, ddt_raw, "wgrad_in_dt")[:, :16], u5=_wgrad(h, du5, "wgrad_in_u5"),
                  gates=_wgrad(h, dgates, "wgrad_in_gates"))
    w_in_pieces = [(c0, d_w_in[n]) for n, c0, _ in W_IN_PIECES]
    inproj_args = (x, dx1, dz, dxbc_raw, du5, dgates, ddt_raw, g_mix, wp)
    if comm is None:
        dx, d_gmix = _inproj_bwd(*inproj_args)
        late_totals = None
    else:
        g_b, g_in = _late_buffers(d_w_out, d_w_branch, d_glu_w, d_conv_w[:CONV_K], w_in_pieces)
        sib_b, sib_in = _exchange(_Pair([g_b, g_in]), "pair_exchange")
        pf_b, pb_b = _pair_sum(comm["place"], g_b, sib_b, "pair_sum_b")
        pf_in, pb_in = _pair_sum(comm["place"], g_in, sib_in, "pair_sum_in")
        (dx, d_gmix), (got_b, got_in) = _inproj_bwd(*inproj_args, rider=_Chip([pb_b, pb_in]))
        late_totals = (_chip_sum(comm["place"], pf_b, got_b, "chip_sum_b")[0],
                       _chip_sum(comm["place"], pf_in, got_in, "chip_sum_in")[0])

    grads = dict(
        norm_mix_g=d_gmix.reshape(-1), w_in_pieces=w_in_pieces, late_totals=late_totals,
        conv_w=d_conv_w[:CONV_K], conv_b=d_conv_b.reshape(-1),
        dt_bias=d_dt_bias[0, :16], a_log=d_alog[0, :16], d_ssd=d_dssd[0, :16], ssd_norm_g=d_gssd.reshape(-1),
        s5_a_re=d_a_re.reshape(32, 64), s5_a_im=d_a_im.reshape(32, 64), s5_log_dt=d_log_dt.reshape(32),
        s5_b_re=d_b_re.reshape(32, 64, 16), s5_b_im=d_b_im.reshape(32, 64, 16), s5_c_re=d_c_re, s5_c_im=d_c_im,
        s5_d=d_s5d.reshape(-1), s5_glu_w=d_glu_w, s5_glu_b=d_glu_b.reshape(-1), w_branch=d_w_branch, w_out=d_w_out,
        norm_mlp_g=d_gmlp.reshape(-1), w_mlp_in=d_w_mlp_in, w_mlp_out=d_w_mlp_out, norm_final_g=d_gfin.reshape(-1),
        mlp_total=mlp_total)
    return jnp.sum(loss_lanes), dx, grads


MESH = pl.DeviceIdType.MESH
N_CHIPS = 4


def _place():
    x, y, c = lax.axis_index("x"), lax.axis_index("y"), lax.axis_index("c")
    chips = [(1 - x, y), (x, 1 - y), (1 - x, 1 - y)]
    return x, y, c, chips


def _remote(src, dst, send_sems, recv_sems, k, to):
    return pltpu.make_async_remote_copy(src_ref=src, dst_ref=dst, send_sem=send_sems.at[k], recv_sem=recv_sems.at[k],
                                        device_id=to, device_id_type=MESH)


def _row_chunks(rows, k, align):
    step = rows // k
    assert rows % k == 0 and step % align == 0, (rows, k, align)
    return [(i * step, step) for i in range(k)]


ICI_CHUNKS = 4
D2D_CHUNKS = 24


class _Gather:
    def __init__(self, srcs, ks):
        self.inputs = list(srcs)
        self.out_shapes = [jax.ShapeDtypeStruct((N_CHIPS,) + a.shape, a.dtype) for a in srcs]
        self.halves = [a.shape[0] // 2 for a in srcs]
        self.pieces = [_row_chunks(h, k, 32 // a.dtype.itemsize) for a, h, k in zip(srcs, self.halves, ks)]
        self.n_ici = 3 * sum(ks)
        self.n_sems = 2 * self.n_ici + len(srcs)

    def _plan(self, src_refs, out_refs, send_sems, recv_sems):
        x, y, c, chips = _place()
        own = 2 * x + y
        sib = (x, y, 1 - c)
        first, fwd_plan, k = [], [], 0
        for a, (src_ref, out_ref) in enumerate(zip(src_refs, out_refs)):
            h = self.halves[a]
            for r0, nr in self.pieces[a]:
                for cx, cy in chips:
                    first.append(_remote(src_ref.at[pl.ds(c * h + r0, nr), :], out_ref.at[own, pl.ds(c * h + r0, nr), :],
                                         send_sems, recv_sems, k, (cx, cy, c)))
                    fwd_plan.append((out_ref, 2 * cx + cy, h, r0, nr, k, (cx, cy, c)))
                    k += 1
        for a, (src_ref, out_ref) in enumerate(zip(src_refs, out_refs)):
            first.append(_remote(src_ref, out_ref.at[own], send_sems, recv_sems, 2 * self.n_ici + a, sib))
        return first, fwd_plan, c, sib

    def issue(self, src_refs, out_refs, send_sems, recv_sems):
        for cp in self._plan(src_refs, out_refs, send_sems, recv_sems)[0]:
            cp.start()

    def complete(self, src_refs, out_refs, send_sems, recv_sems):
        first, fwd_plan, c, sib = self._plan(src_refs, out_refs, send_sems, recv_sems)
        passed = []
        for out_ref, s, h, r0, nr, k, frm in fwd_plan:
            got = out_ref.at[s, pl.ds(c * h + r0, nr), :]
            _remote(got, got, send_sems, recv_sems, k, frm).wait_recv()
            fw = _remote(got, got, send_sems, recv_sems, self.n_ici + k, sib)
            fw.start()
            passed.append(fw)
        for out_ref, s, h, r0, nr, k, frm in fwd_plan:
            got = out_ref.at[s, pl.ds((1 - c) * h + r0, nr), :]
            _remote(got, got, send_sems, recv_sems, self.n_ici + k, sib).wait_recv()
        own_copies = first[self.n_ici:]
        for cp in own_copies:
            cp.wait_recv()
        for cp in first + passed:
            cp.wait_send()


def _exchange(rider, name):
    ri, ro = len(rider.inputs), len(rider.out_shapes)

    def body(*refs):
        rider.issue(refs[:ri], refs[ri:ri + ro], *refs[ri + ro:])
        rider.complete(refs[:ri], refs[ri:ri + ro], *refs[ri + ro:])

    return _pc(
        body, name=name, in_specs=[_hbm_spec()] * ri, out_specs=[_hbm_spec()] * ro, out_shape=list(rider.out_shapes),
        scratch_shapes=[pltpu.SemaphoreType.DMA((rider.n_sems,))] * 2,
    )(*rider.inputs)


def _call(body, rider=None, **kw):
    if rider is None:
        return _pc(body, **kw)
    single = not isinstance(kw["out_shape"], (list, tuple))
    out_specs = [kw["out_specs"]] if single else list(kw["out_specs"])
    out_shape = [kw["out_shape"]] if single else list(kw["out_shape"])
    scratch = list(kw.get("scratch_shapes", ()))
    n_in, n_out, n_scr = len(kw["in_specs"]), len(out_specs), len(scratch)
    ri, ro = len(rider.inputs), len(rider.out_shapes)
    steps = kw["grid"][0]

    def wrapped(*refs):
        o0 = n_in + ri
        s0 = o0 + n_out + ro
        r_in, r_out, sems = refs[n_in:o0], refs[o0 + n_out:s0], refs[s0 + n_scr:]

        @pl.when(pl.program_id(0) == 0)
        def _():
            rider.issue(r_in, r_out, *sems)

        body(*refs[:n_in], *refs[o0:o0 + n_out], *refs[s0:s0 + n_scr])

        @pl.when(pl.program_id(0) == steps - 1)
        def _():
            rider.complete(r_in, r_out, *sems)

    f = _pc(wrapped, name=kw["name"], grid=kw["grid"], in_specs=list(kw["in_specs"]) + [_hbm_spec()] * ri,
            out_specs=out_specs + [_hbm_spec()] * ro, out_shape=out_shape + list(rider.out_shapes),
            scratch_shapes=scratch + [pltpu.SemaphoreType.DMA((rider.n_sems,))] * 2, compiler_params=kw["compiler_params"])

    def run(*args):
        res = f(*args, *rider.inputs)
        return (res[0] if single else res[:n_out]), res[n_out:]

    return run


def _d2d_pieces(rows):
    k = next(k for k in range(24, 0, -1) if rows % k == 0 and (rows // k) % 8 == 0)
    return _row_chunks(rows, k, 8)


class _Pair:
    def __init__(self, gs, small=None):
        self.n = len(gs)
        self.halves = [g.shape[1] // 2 for g in gs]
        self.inputs = list(gs) + ([small] if small is not None else [])
        self.out_shapes = [jax.ShapeDtypeStruct((N_CHIPS, h, g.shape[2]), F32) for g, h in zip(gs, self.halves)]
        if small is not None:
            self.out_shapes.append(jax.ShapeDtypeStruct(small.shape, F32))
        self.n_sems = len(self.inputs)

    def issue(self, in_refs, out_refs, send_sems, recv_sems):
        x, y, c, _ = _place()
        sib = (x, y, 1 - c)
        for a in range(self.n):
            h = self.halves[a]
            for s in range(N_CHIPS):
                for r0, nr in _d2d_pieces(h):
                    _remote(in_refs[a].at[s, pl.ds((1 - c) * h + r0, nr), :], out_refs[a].at[s, pl.ds(r0, nr), :],
                            send_sems, recv_sems, a, sib).start()
        for a in range(self.n, len(self.inputs)):
            _remote(in_refs[a], out_refs[a], send_sems, recv_sems, a, sib).start()

    def complete(self, in_refs, out_refs, send_sems, recv_sems):
        x, y, c, _ = _place()
        for a in range(len(self.inputs)):
            _remote(out_refs[a], out_refs[a], send_sems, recv_sems, a, (x, y, 1 - c)).wait()


SUM_BLOCKS = 4


def _pair_sum(place, g, sib, name, small=None, sib_small=None):
    n, R, C = g.shape
    H = R // 2
    rb = H // SUM_BLOCKS
    assert H % SUM_BLOCKS == 0 and rb % 16 == 0

    def body(place_ref, a_ref, b_ref, *rest):
        if small is None:
            pf_ref, pb_ref = rest
        else:
            s_ref, t_ref, pf_ref, pb_ref, ps_ref = rest

            @pl.when((pl.program_id(0) == 0) & (pl.program_id(1) == 0))
            def _():
                ps_ref[...] = s_ref[...] + t_ref[...]

        p = a_ref[...] + b_ref[...]
        pf_ref[...] = p
        pb_ref[...] = p.astype(BF16)

    blk = pl.BlockSpec((1, rb, C), lambda s, i, pr: (s, i, 0))
    mine = pl.BlockSpec((1, rb, C), lambda s, i, pr: (s, pr[1] * SUM_BLOCKS + i, 0))
    ins, outs, shapes, args = [mine, blk], [blk, blk], [jax.ShapeDtypeStruct((n, H, C), F32),
                                                        jax.ShapeDtypeStruct((n, H, C), BF16)], [g, sib]
    if small is not None:
        sm = pl.BlockSpec(small.shape, lambda s, i, pr: (0, 0))
        ins += [sm, sm]
        outs += [sm]
        shapes += [jax.ShapeDtypeStruct(small.shape, F32)]
        args += [small, sib_small]
    return _pc(
        body, name=name, out_shape=shapes,
        grid_spec=pltpu.PrefetchScalarGridSpec(num_scalar_prefetch=1, grid=(n, SUM_BLOCKS), in_specs=ins, out_specs=outs),
        compiler_params=_cparams(("arbitrary", "arbitrary")),
    )(place, *args)


class _Chip:
    def __init__(self, pbs, psmall=None):
        self.n = len(pbs)
        self.rows = [pb.shape[1] for pb in pbs]
        self.inputs = list(pbs) + ([psmall] if psmall is not None else [])
        self.out_shapes = [jax.ShapeDtypeStruct((3,) + pb.shape[1:], BF16) for pb in pbs]
        if psmall is not None:
            self.out_shapes.append(jax.ShapeDtypeStruct((N_CHIPS,) + psmall.shape, F32))
        self.n_sems = 3 * len(self.inputs)

    def issue(self, in_refs, out_refs, send_sems, recv_sems):
        x, y, c, chips = _place()
        own = 2 * x + y
        for j, (cx, cy) in enumerate(chips):
            for a in range(self.n):
                for r0, nr in _row_chunks(self.rows[a], ICI_CHUNKS, 16):
                    _remote(in_refs[a].at[2 * cx + cy, pl.ds(r0, nr), :], out_refs[a].at[j, pl.ds(r0, nr), :],
                            send_sems, recv_sems, 3 * a + j, (cx, cy, c)).start()
            for a in range(self.n, len(self.inputs)):
                _remote(in_refs[a], out_refs[a].at[own], send_sems, recv_sems, 3 * a + j, (cx, cy, c)).start()

    def complete(self, in_refs, out_refs, send_sems, recv_sems):
        x, y, c, chips = _place()
        own = 2 * x + y
        for j, (cx, cy) in enumerate(chips):
            for a in range(self.n):
                _remote(in_refs[a].at[own], out_refs[a].at[j], send_sems, recv_sems, 3 * a + j, (cx, cy, c)).wait()
            for a in range(self.n, len(self.inputs)):
                _remote(in_refs[a], out_refs[a].at[2 * cx + cy], send_sems, recv_sems, 3 * a + j, (cx, cy, c)).wait()


def _chip_sum(place, pf, got, name, small4=None, psmall=None):
    _, H, C = pf.shape
    rb = H // SUM_BLOCKS

    def body(place_ref, o_ref, g_ref, *rest):
        if small4 is None:
            (tot_ref,) = rest
        else:
            s_ref, p_ref, tot_ref, st_ref = rest

            @pl.when(pl.program_id(0) == 0)
            def _():
                terms = [jnp.where(place_ref[0] == s, p_ref[...], s_ref[s]) for s in range(N_CHIPS)]
                st_ref[...] = ((terms[0] + terms[1]) + terms[2]) + terms[3]

        tot_ref[...] = ((o_ref[0] + g_ref[0].astype(F32)) + g_ref[1].astype(F32)) + g_ref[2].astype(F32)

    ins = [pl.BlockSpec((1, rb, C), lambda i, pr: (pr[0], i, 0)), pl.BlockSpec((3, rb, C), lambda i, pr: (0, i, 0))]
    outs = [pl.BlockSpec((rb, C), lambda i, pr: (pr[1] * SUM_BLOCKS + i, 0))]
    shapes = [jax.ShapeDtypeStruct((2 * H, C), F32)]
    args = [pf, got]
    if small4 is not None:
        ins += [pl.BlockSpec(small4.shape, lambda i, pr: (0, 0, 0)), pl.BlockSpec(psmall.shape, lambda i, pr: (0, 0))]
        outs += [pl.BlockSpec(psmall.shape, lambda i, pr: (0, 0))]
        shapes += [jax.ShapeDtypeStruct(psmall.shape, F32)]
        args += [small4, psmall]
    return _pc(
        body, name=name, out_shape=shapes,
        grid_spec=pltpu.PrefetchScalarGridSpec(num_scalar_prefetch=1, grid=(SUM_BLOCKS,), in_specs=ins, out_specs=outs),
        compiler_params=_cparams(("arbitrary",)),
    )(place, *args)


def _half_exchange(fulls):
    n = len(fulls)

    def body(*refs):
        in_refs, out_refs = refs[:n], refs[n:2 * n]
        send_sems, recv_sems = refs[2 * n:]
        x, y, c, _ = _place()
        sib = (x, y, 1 - c)
        for a in range(n):
            h = fulls[a].shape[0] // 2
            for r0, nr in _d2d_pieces(h):
                rows = pl.ds(c * h + r0, nr)
                _remote(in_refs[a].at[rows, :], out_refs[a].at[rows, :], send_sems, recv_sems, a, sib).start()
        for a in range(n):
            h = fulls[a].shape[0] // 2
            _remote(in_refs[a].at[pl.ds(c * h, h), :], out_refs[a].at[pl.ds((1 - c) * h, h), :], send_sems, recv_sems, a,
                    sib).wait()

    return _pc(
        body, name="half_exchange", in_specs=[_hbm_spec()] * n, out_specs=[_hbm_spec()] * n,
        out_shape=[jax.ShapeDtypeStruct(f.shape, F32) for f in fulls],
        input_output_aliases={a: a for a in range(n)},
        scratch_shapes=[pltpu.SemaphoreType.DMA((n,)), pltpu.SemaphoreType.DMA((n,))],
    )(*fulls)


def _small_allreduce(pack):
    R, C = pack.shape

    def body(p_ref, o_ref, sib_ref, pair_ref, slots_ref, send_sems, recv_sems):
        x, y, c, chips = _place()
        own = 2 * x + y
        cp = _remote(p_ref, sib_ref, send_sems, recv_sems, 0, (x, y, 1 - c))
        cp.start()
        cp.wait()
        pair_ref[...] = p_ref[...] + sib_ref[...]
        slots_ref[own] = pair_ref[...]
        out = [_remote(pair_ref, slots_ref.at[own], send_sems, recv_sems, 1 + j, (cx, cy, c)) for j, (cx, cy) in enumerate(chips)]
        for cp in out:
            cp.start()
        for j, (cx, cy) in enumerate(chips):
            _remote(pair_ref, slots_ref.at[2 * cx + cy], send_sems, recv_sems, 1 + j, (cx, cy, c)).wait()
        o_ref[...] = ((slots_ref[0] + slots_ref[1]) + slots_ref[2]) + slots_ref[3]

    vmem = pl.BlockSpec(memory_space=pltpu.VMEM)
    return _pc(
        body, name="small_allreduce", in_specs=[vmem], out_specs=vmem, out_shape=jax.ShapeDtypeStruct((R, C), F32),
        scratch_shapes=[pltpu.VMEM((R, C), F32), pltpu.VMEM((R, C), F32), pltpu.VMEM((N_CHIPS, R, C), F32),
                        pltpu.SemaphoreType.DMA((4,)), pltpu.SemaphoreType.DMA((4,))],
    )(pack)


def _adamw(w, g, m, v, name, g_row0=0, with_grad=False, col_block=None):
    R, C = w.shape
    rb = 256 if R % 256 == 0 else (128 if R % 128 == 0 else R)
    if col_block:
        rb = R
    assert g_row0 % rb == 0

    def body(w_ref, g_ref, m_ref, v_ref, d_ref, nm_ref, nv_ref, *g_out):
        gv = g_ref[...]
        m2 = ADAM_B1 * m_ref[...] + (1.0 - ADAM_B1) * gv
        v2 = ADAM_B2 * v_ref[...] + (1.0 - ADAM_B2) * (gv * gv)
        m_hat = m2 / (1.0 - ADAM_B1 ** ADAM_STEP)
        v_hat = v2 / (1.0 - ADAM_B2 ** ADAM_STEP)
        d_ref[...] = -ADAM_LR * (m_hat / (jnp.sqrt(v_hat) + ADAM_EPS) + ADAM_WD * w_ref[...])
        nm_ref[...] = m2
        nv_ref[...] = v2
        if with_grad:
            g_out[0][...] = gv

    if col_block:
        spec = g_spec = pl.BlockSpec((R, col_block), lambda i: (0, i))
        steps = C // col_block
    else:
        spec = pl.BlockSpec((rb, C), lambda i: (i, 0))
        g_spec = pl.BlockSpec((rb, C), lambda i: (g_row0 // rb + i, 0))
        steps = R // rb
    n_out = 4 if with_grad else 3
    return _pc(
        body, name=name, grid=(steps,), in_specs=[spec, g_spec, spec, spec], out_specs=[spec] * n_out,
        out_shape=[jax.ShapeDtypeStruct((R, C), F32)] * n_out, compiler_params=_cparams(("parallel",)),
    )(w, g, m, v)


PACK_COLS = 1024
ROWS_A = (("w_mlp_in", 0, 1024), ("w_mlp_out", 1024, 1024), ("w_out", 2048, 256), ("w_branch", 2304, 384))
ROWS_A_TOTAL = 2688
ROWS_B = (("w_out", 0, 256), ("w_branch", 256, 384))
ROW_B_GLU, ROW_B_CONV, ROWS_B_TOTAL = 640, 704, 768
W_IN_SHARD = 1412
CONV_PAD_ROWS = 16
SMALL = (("norm_mix_g", (1024,)), ("conv_b", (2048,)), ("dt_bias", (16,)), ("a_log", (16,)), ("d_ssd", (16,)),
         ("ssd_norm_g", (1024,)), ("s5_a_re", (32, 64)), ("s5_a_im", (32, 64)), ("s5_log_dt", (32,)),
         ("s5_b_re", (32, 64, 16)), ("s5_b_im", (32, 64, 16)), ("s5_c_re", (32, 16, 64)), ("s5_c_im", (32, 16, 64)),
         ("s5_d", (512,)), ("s5_glu_b", (512,)), ("norm_mlp_g", (1024,)), ("norm_final_g", (1024,)))
SMALL_ROWS = 144
SMALL_COUNT = sum(math.prod(shp) for _, shp in SMALL)
GLU_ROWS = S5_WIDTH * S5_WIDTH // PACK_COLS
CONV_ROWS = CONV_K * CONV_DIM // PACK_COLS
W_IN_PIECES = (("z", 0, 1024), ("xbc", 1024, 2048), ("dt", OFF_DT, 16), ("u5", OFF_U, 512), ("gates", 3600, 2048))


def _pack_small(parts):
    flat = jnp.concatenate([a.astype(F32).reshape(-1) for a in parts])
    return jnp.concatenate([flat, jnp.zeros((SMALL_ROWS * PACK_COLS - flat.shape[0],), F32)]).reshape(SMALL_ROWS, PACK_COLS)


def _unpack_small(pack):
    flat, out, r = pack.reshape(-1), {}, 0
    for name, shp in SMALL:
        n = math.prod(shp)
        out[name] = flat[r:r + n].reshape(shp)
        r += n
    return out


def _late_buffers(d_w_out, d_w_branch, d_glu_w, d_conv_w, w_in_pieces):
    conv4 = d_conv_w.reshape(CONV_K, N_CHIPS, 512).transpose(1, 0, 2).reshape(N_CHIPS, CONV_ROWS // N_CHIPS, PACK_COLS)
    g_b = jnp.concatenate(
        [d_w_out.reshape(N_CHIPS, -1, PACK_COLS), d_w_branch.reshape(N_CHIPS, -1, PACK_COLS),
         d_glu_w.reshape(N_CHIPS, GLU_ROWS // N_CHIPS, PACK_COLS),
         jnp.pad(conv4, ((0, 0), (0, ROWS_B_TOTAL - ROW_B_CONV - CONV_ROWS // N_CHIPS), (0, 0)))], axis=1)
    g_in = jnp.stack([jnp.concatenate(_column_range(w_in_pieces, W_IN_SHARD * s, W_IN_SHARD * (s + 1)), axis=1)
                      for s in range(N_CHIPS)])
    return g_b, g_in


def _column_range(pieces, lo, hi):
    out = []
    for c0, a in pieces:
        a0, a1 = max(lo, c0), min(hi, c0 + a.shape[-1])
        if a0 < a1:
            out.append(a[..., a0 - c0:a1 - c0])
    return out


def kernel(x, norm_mix_g, w_in, conv_w, conv_b, dt_bias, a_log, d_ssd, ssd_norm_g, s5_a_re, s5_a_im, s5_log_dt, s5_b_re, s5_b_im, s5_c_re, s5_c_im, s5_d, s5_glu_w, s5_glu_b, w_branch, w_out, norm_mlp_g, w_mlp_in, w_mlp_out, norm_final_g, loss_target, m_norm_mix_g, m_w_in, m_conv_w, m_conv_b, m_dt_bias, m_a_log, m_d_ssd, m_ssd_norm_g, m_s5_a_re, m_s5_a_im, m_s5_log_dt, m_s5_b_re, m_s5_b_im, m_s5_c_re, m_s5_c_im, m_s5_d, m_s5_glu_w, m_s5_glu_b, m_w_branch, m_w_out, m_norm_mlp_g, m_w_mlp_in, m_w_mlp_out, m_norm_final_g, v_norm_mix_g, v_w_in, v_conv_w, v_conv_b, v_dt_bias, v_a_log, v_d_ssd, v_ssd_norm_g, v_s5_a_re, v_s5_a_im, v_s5_log_dt, v_s5_b_re, v_s5_b_im, v_s5_c_re, v_s5_c_im, v_s5_d, v_s5_glu_w, v_s5_glu_b, v_w_branch, v_w_out, v_norm_mlp_g, v_w_mlp_in, v_w_mlp_out, v_norm_final_g):
    names = ("norm_mix_g", "w_in", "conv_w", "conv_b", "dt_bias", "a_log", "d_ssd", "ssd_norm_g", "s5_a_re", "s5_a_im",
             "s5_log_dt", "s5_b_re", "s5_b_im", "s5_c_re", "s5_c_im", "s5_d", "s5_glu_w", "s5_glu_b", "w_branch", "w_out",
             "norm_mlp_g", "w_mlp_in", "w_mlp_out", "norm_final_g")
    w = dict(zip(names, (norm_mix_g, w_in, conv_w, conv_b, dt_bias, a_log, d_ssd, ssd_norm_g, s5_a_re, s5_a_im, s5_log_dt,
                         s5_b_re, s5_b_im, s5_c_re, s5_c_im, s5_d, s5_glu_w, s5_glu_b, w_branch, w_out, norm_mlp_g,
                         w_mlp_in, w_mlp_out, norm_final_g)))
    m = dict(zip(names, (m_norm_mix_g, m_w_in, m_conv_w, m_conv_b, m_dt_bias, m_a_log, m_d_ssd, m_ssd_norm_g, m_s5_a_re,
                         m_s5_a_im, m_s5_log_dt, m_s5_b_re, m_s5_b_im, m_s5_c_re, m_s5_c_im, m_s5_d, m_s5_glu_w,
                         m_s5_glu_b, m_w_branch, m_w_out, m_norm_mlp_g, m_w_mlp_in, m_w_mlp_out, m_norm_final_g)))
    v = dict(zip(names, (v_norm_mix_g, v_w_in, v_conv_w, v_conv_b, v_dt_bias, v_a_log, v_d_ssd, v_ssd_norm_g, v_s5_a_re,
                         v_s5_a_im, v_s5_log_dt, v_s5_b_re, v_s5_b_im, v_s5_c_re, v_s5_c_im, v_s5_d, v_s5_glu_w,
                         v_s5_glu_b, v_w_branch, v_w_out, v_norm_mlp_g, v_w_mlp_in, v_w_mlp_out, v_norm_final_g)))

    cx, cy, cc = lax.axis_index("x"), lax.axis_index("y"), lax.axis_index("c")
    own = 2 * cx + cy
    place = jnp.stack([own, cc]).astype(jnp.int32)

    src_conv = jnp.concatenate([conv_w, jnp.zeros((CONV_PAD_ROWS - CONV_K, 512), F32)], axis=0)
    all_in, all_conv = _exchange(_Gather([w_in.astype(BF16), src_conv], [ICI_CHUNKS, 1]), "gather_first")
    p = {n: w[n] for n, _ in SMALL}
    p["conv_w"] = jnp.concatenate([all_conv[s, :CONV_K] for s in range(N_CHIPS)], axis=1)
    shards = [(W_IN_SHARD * s, all_in[s]) for s in range(N_CHIPS)]
    p["w_in_perm"] = jnp.concatenate(
        _column_range(shards, 0, OFF_DT) + _column_range(shards, OFF_U, D_IN_PROJ) + _column_range(shards, OFF_DT, OFF_U)
        + [jnp.zeros((D_MODEL, DT_PAD - 16), BF16)], axis=1)

    def late_unpack(gathered):
        all_a, all_glu = gathered
        out = {"w_mlp_in": all_a[:, 0:1024], "s5_glu_w": all_glu.reshape(S5_WIDTH, S5_WIDTH)}
        for n, r0, nr in ROWS_A[1:]:
            out[n] = all_a[:, r0:r0 + nr].reshape(N_CHIPS * nr, PACK_COLS)
        return out

    comm = dict(place=place, late_ks=[ICI_CHUNKS, 1], late_unpack=late_unpack,
                late_srcs=[jnp.concatenate([w[n].astype(BF16) for n, _, _ in ROWS_A], axis=0), s5_glu_w.astype(BF16)])
    loss_part, grad_x, g = _local_step(x[0], loss_target[0], p, comm)

    red_mlp, red_b, red_in = _half_exchange([g["mlp_total"], *g["late_totals"]])
    small_tot = _small_allreduce(_pack_small([g[n] for n, _ in SMALL] + [loss_part.reshape(1)]))
    loss = small_tot.reshape(-1)[SMALL_COUNT]

    grads = _unpack_small(small_tot)
    delta, new_m, new_v = {}, {}, {}
    for n, r0, _ in ROWS_A[:2]:
        delta[n], new_m[n], new_v[n], grads[n] = _adamw(w[n], red_mlp, m[n], v[n], "adamw_" + n, g_row0=r0, with_grad=True)
    for n, r0, _ in ROWS_B:
        delta[n], new_m[n], new_v[n], grads[n] = _adamw(w[n], red_b, m[n], v[n], "adamw_" + n, g_row0=r0, with_grad=True)
    d_t, m_t, v_t, g_t = _adamw(w_in.T, red_in.T, m_w_in.T, v_w_in.T, "adamw_w_in", with_grad=True, col_block=128)
    delta["w_in"], new_m["w_in"], new_v["w_in"], grads["w_in"] = d_t.T, m_t.T, v_t.T, g_t.T
    grads["s5_glu_w"] = red_b[ROW_B_GLU:ROW_B_GLU + GLU_ROWS // N_CHIPS].reshape(S5_WIDTH // N_CHIPS, S5_WIDTH)
    grads["conv_w"] = red_b[ROW_B_CONV:ROW_B_CONV + CONV_ROWS // N_CHIPS].reshape(CONV_K, CONV_DIM // N_CHIPS)
    for n in ("s5_glu_w", "conv_w"):
        delta[n], new_m[n], new_v[n] = _adamw(w[n], grads[n], m[n], v[n], "adamw_" + n)
    ds, ms, vs = _adamw(_pack_small([w[n] for n, _ in SMALL]), small_tot, _pack_small([m[n] for n, _ in SMALL]),
                        _pack_small([v[n] for n, _ in SMALL]), "adamw_small")
    delta.update(_unpack_small(ds))
    new_m.update(_unpack_small(ms))
    new_v.update(_unpack_small(vs))

    return (loss, grad_x[None], *[grads[n] for n in names], *[delta[n] for n in names],
            *[new_m[n] for n in names], *[new_v[n] for n in names])
```

```python
import functools
import math

import jax
import jax.numpy as jnp
from jax import lax
from jax.experimental import pallas as pl
from jax.experimental.pallas import tpu as pltpu

F32 = jnp.float32
BF16 = jnp.bfloat16

D_MODEL = 1024
SSD_INNER = 1024
SSD_HEADS = 16
SSD_HEADDIM = 64
SSD_GROUPS = 4
SSD_HPG = 4
SSD_STATE = 128
SSD_CHUNK = 128
CONV_K = 4
CONV_DIM = 2048
S5_WIDTH = 512
S5_STATES = 2048
S5_BLOCKS = 4
S5_CHUNK = 128
D_FF = 4096
FF_SHARDS = 4
FF_SHARD = D_FF // FF_SHARDS
EPS = 1e-6
P_Z, P_XBC, P_U5, P_G, P_DT, P_END = 0, 1024, 3072, 3584, 5632, 5760
DT_PAD = 128
OFF_DT, OFF_U = 3072, 3088
D_IN_PROJ = 5648

ADAM_LR, ADAM_B1, ADAM_B2, ADAM_EPS, ADAM_WD, ADAM_STEP = 0.001, 0.9, 0.999, 1e-08, 0.01, 10

TOKEN_TILE = 256
VMEM_LIMIT = 56 * 1024 * 1024
HALO = 8
CONV_COLS = 256
CONV_ROWS_BLK = 64
WGRAD_TOKENS = 2048


def _pc(body, **kw):
    return pl.pallas_call(body, **kw)


def _cparams(sem=None):
    return pltpu.CompilerParams(dimension_semantics=sem, vmem_limit_bytes=VMEM_LIMIT)


def _dot(a, b):
    return jnp.dot(a, b, preferred_element_type=F32)


def _dot_nt(a, b):
    return lax.dot_general(a, b, (((1,), (1,)), ((), ())), preferred_element_type=F32)


def _dot_tn(a, b):
    return lax.dot_general(a, b, (((0,), (0,)), ((), ())), preferred_element_type=F32)


def _dot_hi(a, b, dims=(((1,), (0,)), ((), ()))):
    return lax.dot_general(a, b, dims, preferred_element_type=F32, precision=lax.Precision.HIGHEST)


def _split_bf16(x, terms):
    out = []
    for _ in range(terms - 1):
        t = x.astype(BF16)
        out.append(t)
        x = x - t.astype(F32)
    out.append(x.astype(BF16))
    return out


def _dot_split(x, onehots, terms, dims=(((1,), (0,)), ((), ()))):
    acc = None
    for t in _split_bf16(x, terms):
        p = lax.dot_general(t, onehots, dims, preferred_element_type=F32)
        acc = p if acc is None else acc + p
    return acc


def _dot_split_rhs(onehots, x, terms, dims=(((1,), (0,)), ((), ()))):
    acc = None
    for t in _split_bf16(x, terms):
        p = lax.dot_general(onehots, t, dims, preferred_element_type=F32)
        acc = p if acc is None else acc + p
    return acc


def _sigmoid(x):
    return 0.5 * jnp.tanh(0.5 * x) + 0.5


def _softplus(x):
    return jnp.maximum(x, 0.0) + jnp.log(1.0 + jnp.exp(-jnp.abs(x)))


_GELU_C = math.sqrt(2.0 / math.pi)


def _gelu(x):
    return 0.5 * x * (1.0 + jnp.tanh(_GELU_C * (x + 0.044715 * x * x * x)))


def _gelu_grad(x):
    t = jnp.tanh(_GELU_C * (x + 0.044715 * x * x * x))
    return 0.5 * (1.0 + t) + 0.5 * x * (1.0 - t * t) * _GELU_C * (1.0 + 3.0 * 0.044715 * x * x)


def _rms(x):
    r = lax.rsqrt(jnp.mean(x * x, axis=-1, keepdims=True) + EPS)
    return x * r, r


def _rms_bwd(xn, r, dxn):
    return r * (dxn - xn * jnp.mean(dxn * xn, axis=-1, keepdims=True))


def _row_spec(tm, width, col=0):
    return pl.BlockSpec((tm, width), lambda i: (i, col))


def _const_spec(shape):
    nd = len(shape)
    return pl.BlockSpec(shape, lambda i: (0,) * nd)


def _hbm_spec():
    return pl.BlockSpec(memory_space=pl.ANY)


def _inproj_fwd(x, g, wp):
    T = x.shape[0]
    tm = TOKEN_TILE

    def body(x_ref, g_ref, w_hbm, z_ref, xbc_ref, u5_ref, gt_ref, dt_ref, h_ref, w_ref):
        @pl.when(pl.program_id(0) == 0)
        def _():
            pltpu.sync_copy(w_hbm, w_ref)

        xn, _ = _rms(x_ref[...])
        h = (xn * g_ref[...]).astype(BF16)
        h_ref[...] = h
        z_ref[...] = _dot(h, w_ref[:, P_Z:P_XBC])
        xbc_ref[...] = _dot(h, w_ref[:, P_XBC:P_U5])
        u5_ref[...] = _dot(h, w_ref[:, P_U5:P_G])
        tail = _dot(h, w_ref[:, P_G:P_END])
        gt_ref[...] = tail[:, :P_DT - P_G]
        dt_ref[...] = tail[:, P_DT - P_G:]

    widths = (1024, 2048, 512, 2048, DT_PAD)
    return _pc(
        body, name="inproj_fwd", grid=(T // tm,),
        in_specs=[_row_spec(tm, D_MODEL), _const_spec((1, D_MODEL)), _hbm_spec()],
        out_specs=[_row_spec(tm, w) for w in widths] + [_row_spec(tm, D_MODEL)],
        out_shape=[jax.ShapeDtypeStruct((T, w), F32) for w in widths] + [jax.ShapeDtypeStruct((T, D_MODEL), BF16)],
        scratch_shapes=[pltpu.VMEM((D_MODEL, P_END), BF16)],
        compiler_params=_cparams(("arbitrary",)),
    )(x, g, wp)


def _inproj_bwd(x, dx1, dz, dxbc, du5, dgt, ddt, g, wp, rider=None):
    T = x.shape[0]
    tm = TOKEN_TILE

    def body(x_ref, dx1_ref, dz_ref, dxbc_ref, du5_ref, dgt_ref, ddt_ref, g_ref, w_hbm, dx_ref, dg_ref, w_ref):
        @pl.when(pl.program_id(0) == 0)
        def _():
            pltpu.sync_copy(w_hbm, w_ref)
            dg_ref[...] = jnp.zeros_like(dg_ref)

        xn, r = _rms(x_ref[...])
        gv = g_ref[...]
        dh = _dot_nt(dz_ref[...].astype(BF16), w_ref[:, P_Z:P_XBC])
        dh += _dot_nt(dxbc_ref[...].astype(BF16), w_ref[:, P_XBC:P_U5])
        dh += _dot_nt(du5_ref[...].astype(BF16), w_ref[:, P_U5:P_G])
        dh += _dot_nt(dgt_ref[...].astype(BF16), w_ref[:, P_G:P_DT])
        dh += _dot_nt(ddt_ref[...].astype(BF16), w_ref[:, P_DT:P_END])
        dg_ref[...] += jnp.sum(dh * xn, axis=0, keepdims=True)
        dx_ref[...] = dx1_ref[...] + _rms_bwd(xn, r, dh * gv)

    return _call(
        body, rider, name="inproj_bwd", grid=(T // tm,),
        in_specs=[_row_spec(tm, 1024), _row_spec(tm, 1024), _row_spec(tm, 1024), _row_spec(tm, 2048),
                  _row_spec(tm, 512), _row_spec(tm, 2048), _row_spec(tm, DT_PAD), _const_spec((1, 1024)), _hbm_spec()],
        out_specs=[_row_spec(tm, 1024), _const_spec((1, 1024))],
        out_shape=[jax.ShapeDtypeStruct((T, 1024), F32), jax.ShapeDtypeStruct((1, 1024), F32)],
        scratch_shapes=[pltpu.VMEM((D_MODEL, P_END), BF16)],
        compiler_params=_cparams(("arbitrary",)),
    )(x, dx1, dz, dxbc, du5, dgt, ddt, g, wp)


def _conv_fwd(xbc_raw, dt_raw, conv_w, conv_b, dt_bias):
    T = xbc_raw.shape[0]
    tm = TOKEN_TILE

    def body(u_ref, dtr_ref, w_ref, b_ref, db_ref, act_ref, dt_ref, ext_ref):
        @pl.when(pl.program_id(0) == 0)
        def _():
            ext_ref[0:HALO, :] = jnp.zeros((HALO, CONV_DIM), F32)

        ext_ref[HALO:, :] = u_ref[...]
        for c0 in range(0, CONV_DIM, CONV_COLS):
            cols = slice(c0, c0 + CONV_COLS)
            taps = [w_ref[k:k + 1, cols] for k in range(CONV_K)]
            bias = b_ref[:, cols]
            for r0 in range(0, tm, CONV_ROWS_BLK):
                y = bias + taps[0] * ext_ref[pl.ds(HALO - (CONV_K - 1) + r0, CONV_ROWS_BLK), cols]
                for k in range(1, CONV_K):
                    y += taps[k] * ext_ref[pl.ds(HALO - (CONV_K - 1) + k + r0, CONV_ROWS_BLK), cols]
                act_ref[r0:r0 + CONV_ROWS_BLK, cols] = y * _sigmoid(y)
        ext_ref[0:HALO, :] = u_ref[tm - HALO:tm, :]
        dt_ref[...] = _softplus(dtr_ref[...] + db_ref[...])

    return _pc(
        body, name="conv_fwd", grid=(T // tm,),
        in_specs=[_row_spec(tm, CONV_DIM), _row_spec(tm, DT_PAD), _const_spec((CONV_K, CONV_DIM)),
                  _const_spec((1, CONV_DIM)), _const_spec((1, DT_PAD))],
        out_specs=[_row_spec(tm, CONV_DIM), _row_spec(tm, DT_PAD)],
        out_shape=[jax.ShapeDtypeStruct((T, CONV_DIM), F32), jax.ShapeDtypeStruct((T, DT_PAD), F32)],
        scratch_shapes=[pltpu.VMEM((tm + HALO, CONV_DIM), F32)],
        compiler_params=_cparams(("arbitrary",)),
    )(xbc_raw, dt_raw, conv_w, conv_b, dt_bias)


def _conv_bwd(xbc_raw, dt_raw, dxs_a, dxs_b, dB, dC, ddt, conv_w, conv_b, dt_bias):
    T = xbc_raw.shape[0]
    tm = TOKEN_TILE
    n = T // tm
    hb = tm // HALO

    def rev(width):
        return pl.BlockSpec((tm, width), lambda i: (n - 1 - i, 0))

    def body(u_ref, up_ref, dtr_ref, dxa_ref, dxb_ref, dB_ref, dC_ref, ddt_ref, w_ref, b_ref, db_ref,
             du_ref, ddtr_ref, dw_ref, dcb_ref, ddb_ref, ext_ref, dye_ref):
        i = pl.program_id(0)

        @pl.when(i == 0)
        def _():
            dye_ref[tm:, :] = jnp.zeros((HALO, CONV_DIM), F32)
            dw_ref[...] = jnp.zeros_like(dw_ref)
            dcb_ref[...] = jnp.zeros_like(dcb_ref)
            ddb_ref[...] = jnp.zeros_like(ddb_ref)

        first = (i == n - 1).astype(F32)
        ext_ref[0:HALO, :] = up_ref[...] * (1.0 - first)
        ext_ref[HALO:, :] = u_ref[...]
        for c0 in range(0, CONV_DIM, CONV_COLS):
            cols = slice(c0, c0 + CONV_COLS)
            taps = [w_ref[k:k + 1, cols] for k in range(CONV_K)]
            bias = b_ref[:, cols]
            acc_b = jnp.zeros((HALO, CONV_COLS), F32)
            acc_w = [jnp.zeros((HALO, CONV_COLS), F32) for _ in range(CONV_K)]
            for r0 in range(0, tm, CONV_ROWS_BLK):
                rows = slice(r0, r0 + CONV_ROWS_BLK)
                us = [ext_ref[pl.ds(HALO - (CONV_K - 1) + k + r0, CONV_ROWS_BLK), cols] for k in range(CONV_K)]
                y = bias + taps[0] * us[0]
                for k in range(1, CONV_K):
                    y += taps[k] * us[k]
                s = _sigmoid(y)
                if c0 < SSD_INNER:
                    dact = dxa_ref[rows, cols] + dxb_ref[rows, cols]
                elif c0 < SSD_INNER + 512:
                    dact = dB_ref[rows, c0 - SSD_INNER:c0 - SSD_INNER + CONV_COLS]
                else:
                    dact = dC_ref[rows, c0 - SSD_INNER - 512:c0 - SSD_INNER - 512 + CONV_COLS]
                dy = dact * (s * (1.0 + y * (1.0 - s)))
                dye_ref[rows, cols] = dy
                acc_b += jnp.sum(dy.reshape(CONV_ROWS_BLK // HALO, HALO, CONV_COLS), axis=0)
                for k in range(CONV_K):
                    acc_w[k] += jnp.sum((dy * us[k]).reshape(CONV_ROWS_BLK // HALO, HALO, CONV_COLS), axis=0)
            dcb_ref[:, cols] += jnp.sum(acc_b, axis=0, keepdims=True)
            for k in range(CONV_K):
                dw_ref[k:k + 1, cols] += jnp.sum(acc_w[k], axis=0, keepdims=True)
        for c0 in range(0, CONV_DIM, CONV_COLS):
            cols = slice(c0, c0 + CONV_COLS)
            taps = [w_ref[k:k + 1, cols] for k in range(CONV_K)]
            for r0 in range(0, tm, CONV_ROWS_BLK):
                du = taps[0] * dye_ref[pl.ds(CONV_K - 1 + r0, CONV_ROWS_BLK), cols]
                for k in range(1, CONV_K):
                    du += taps[k] * dye_ref[pl.ds(CONV_K - 1 - k + r0, CONV_ROWS_BLK), cols]
                du_ref[r0:r0 + CONV_ROWS_BLK, cols] = du.astype(BF16)
        dye_ref[tm:, :] = dye_ref[0:HALO, :]
        sg = _sigmoid(dtr_ref[...] + db_ref[...])
        ddtr = ddt_ref[...] * sg
        ddtr_ref[...] = ddtr.astype(BF16)
        ddb_ref[...] += jnp.sum(ddtr, axis=0, keepdims=True)

    prev_spec = pl.BlockSpec((HALO, CONV_DIM), lambda i: (jnp.maximum((n - 1 - i) * hb - 1, 0), 0))
    return _pc(
        body, name="conv_bwd", grid=(n,),
        in_specs=[rev(CONV_DIM), prev_spec, rev(DT_PAD), rev(1024), rev(1024), rev(512), rev(512), rev(DT_PAD),
                  _const_spec((CONV_K, CONV_DIM)), _const_spec((1, CONV_DIM)), _const_spec((1, DT_PAD))],
        out_specs=[rev(CONV_DIM), rev(DT_PAD), _const_spec((HALO, CONV_DIM)), _const_spec((1, CONV_DIM)),
                   _const_spec((1, DT_PAD))],
        out_shape=[jax.ShapeDtypeStruct((T, CONV_DIM), BF16), jax.ShapeDtypeStruct((T, DT_PAD), BF16),
                   jax.ShapeDtypeStruct((HALO, CONV_DIM), F32), jax.ShapeDtypeStruct((1, CONV_DIM), F32),
                   jax.ShapeDtypeStruct((1, DT_PAD), F32)],
        scratch_shapes=[pltpu.VMEM((tm + HALO, CONV_DIM), F32), pltpu.VMEM((tm + HALO, CONV_DIM), F32)],
        compiler_params=_cparams(("arbitrary",)),
    )(xbc_raw, xbc_raw, dt_raw, dxs_a, dxs_b, dB, dC, ddt, conv_w, conv_b, dt_bias)


GROUP_LANES = SSD_HPG * SSD_HEADDIM


def _ssd_expanders():
    head = jnp.arange(DT_PAD)[:, None]
    to_wide = (jnp.arange(SSD_INNER)[None, :] // SSD_HEADDIM == head).astype(BF16)
    return to_wide, to_wide.T


def _ssd_prep(dt_ref, alog_ref, wide_ref):
    q = SSD_CHUNK
    a = -jnp.exp(alog_ref[...])
    dtv = dt_ref[...]
    la = dtv * a
    row = lax.broadcasted_iota(jnp.int32, (q, q), 0)
    col = lax.broadcasted_iota(jnp.int32, (q, q), 1)
    tri = (col <= row).astype(BF16)
    cum = _dot_split_rhs(tri, la, 3)
    cum_t = _dot_split(la, tri, 3, (((0,), (1,)), ((), ())))
    dtw = _dot_split(dtv, wide_ref[...], 2)
    cumw = _dot_split(cum, wide_ref[...], 3)
    return a, dtv, row, col, tri, cum_t, dtw, cumw, cum


def _decay(cum, cum_t, h, keep):
    return jnp.where(keep, jnp.exp(jnp.minimum(cum[:, h:h + 1] - cum_t[h:h + 1, :], 0.0)), 0.0)


def _decay_t(cum, cum_t, h, keep_t):
    return jnp.where(keep_t, jnp.exp(jnp.minimum(cum_t[h:h + 1, :] - cum[:, h:h + 1], 0.0)), 0.0)


def _ssd_fwd(xbc_act, dt, alog):
    T = xbc_act.shape[0]
    q = SSD_CHUNK
    nc = T // q
    to_wide, _ = _ssd_expanders()

    def body(xbc_ref, dt_ref, alog_ref, wide_ref, y_ref, sp_ref, st_ref, xd_ref, xde_ref):
        @pl.when(pl.program_id(0) == 0)
        def _():
            st_ref[...] = jnp.zeros_like(st_ref)

        a, dtv, row, col, tri, cum_t, dtw, cumw, segcol = _ssd_prep(dt_ref, alog_ref, wide_ref)
        clw = cumw[q - 1:q, :]
        ecw = jnp.exp(cumw)
        xd = xbc_ref[:, 0:SSD_INNER] * dtw
        xd_ref[...] = xd.astype(BF16)
        xde_ref[...] = (xd * jnp.exp(clw - cumw)).astype(BF16)
        cdw = jnp.exp(clw)
        keep = col <= row
        sp_ref[0] = st_ref[...]
        for g in range(SSD_GROUPS):
            gl = slice(GROUP_LANES * g, GROUP_LANES * (g + 1))
            bb = xbc_ref[:, 1024 + 128 * g:1152 + 128 * g].astype(BF16)
            cb = xbc_ref[:, 1536 + 128 * g:1664 + 128 * g].astype(BF16)
            gm = _dot_nt(cb, bb)
            stp = st_ref[g]
            yoff = _dot(cb, stp.astype(BF16)) * ecw[:, gl]
            for r in range(SSD_HPG):
                h = SSD_HPG * g + r
                m = (gm * _decay(segcol, cum_t, h, keep)).astype(BF16)
                y_ref[:, 64 * h:64 * h + 64] = _dot(m, xd_ref[:, 64 * h:64 * h + 64]) + yoff[:, 64 * r:64 * r + 64]
            st_ref[g] = stp * cdw[:, gl] + _dot_tn(bb, xde_ref[:, gl])

    return _pc(
        body, name="ssd_fwd", grid=(nc,),
        in_specs=[_row_spec(q, CONV_DIM), _row_spec(q, DT_PAD), _const_spec((1, DT_PAD)),
                  _const_spec(to_wide.shape)],
        out_specs=[_row_spec(q, SSD_INNER),
                   pl.BlockSpec((1, SSD_GROUPS, SSD_STATE, GROUP_LANES), lambda i: (i, 0, 0, 0))],
        out_shape=[jax.ShapeDtypeStruct((T, SSD_INNER), F32),
                   jax.ShapeDtypeStruct((nc, SSD_GROUPS, SSD_STATE, GROUP_LANES), F32)],
        scratch_shapes=[pltpu.VMEM((SSD_GROUPS, SSD_STATE, GROUP_LANES), F32), pltpu.VMEM((q, SSD_INNER), BF16),
                        pltpu.VMEM((q, SSD_INNER), BF16)],
        compiler_params=_cparams(("arbitrary",)),
    )(xbc_act, dt, alog, to_wide)


def _ssd_bwd(xbc_act, dt, alog, sprev, dy):
    T = xbc_act.shape[0]
    q = SSD_CHUNK
    nc = T // q
    to_wide, to_heads = _ssd_expanders()

    def rev(width):
        return pl.BlockSpec((q, width), lambda i: (nc - 1 - i, 0))

    def body(xbc_ref, dt_ref, alog_ref, sp_ref, dy_ref, wide_ref, heads_ref,
             dxs_ref, dB_ref, dC_ref, ddt_ref, dalog_ref, ds_ref, xd_ref, dxd_ref):
        i = pl.program_id(0)

        @pl.when(i == 0)
        def _():
            ds_ref[...] = jnp.zeros_like(ds_ref)
            dalog_ref[...] = jnp.zeros_like(dalog_ref)

        a, dtv, row, col, tri, cum_t, dtw, cumw, segcol = _ssd_prep(dt_ref, alog_ref, wide_ref)
        clw = cumw[q - 1:q, :]
        ecw = jnp.exp(cumw)
        dew = jnp.exp(clw - cumw)
        cdw = jnp.exp(clw)
        xs = xbc_ref[:, 0:SSD_INNER]
        xd = xs * dtw
        xd_ref[...] = xd.astype(BF16)
        dyv = dy_ref[...]
        dye = (dyv * ecw).astype(BF16)
        xde = (xd * dew).astype(BF16)
        keep = col <= row
        keep_t = col >= row
        rows_k = lax.broadcasted_iota(jnp.int32, (SSD_HPG * q, DT_PAD), 0) // q
        lanes_k = lax.broadcasted_iota(jnp.int32, (SSD_HPG * q, DT_PAD), 1)
        dcw_parts = []
        dcum = jnp.zeros((q, DT_PAD), F32)
        for g in range(SSD_GROUPS):
            gl = slice(GROUP_LANES * g, GROUP_LANES * (g + 1))
            bb = xbc_ref[:, 1024 + 128 * g:1152 + 128 * g].astype(BF16)
            cb = xbc_ref[:, 1536 + 128 * g:1664 + 128 * g].astype(BF16)
            gm = _dot_nt(cb, bb)
            gmt = _dot_nt(bb, cb)
            stp = sp_ref[0, g]
            dst = ds_ref[g]
            stpb = stp.astype(BF16)
            dstb = dst.astype(BF16)
            yoff = _dot(cb, stpb) * ecw[:, gl]
            dcg = _dot_nt(dye[:, gl], stpb)
            ds_ref[g] = dst * cdw[:, gl] + _dot_tn(cb, dye[:, gl])
            dlast = jnp.sum(dst * stp, axis=0, keepdims=True) * cdw[:, gl]
            dbg = _dot_nt(xde[:, gl], dstb)
            w = _dot(bb, dstb) * dew[:, gl]
            wx = w * xd[:, gl]
            dlast = dlast + jnp.sum(wx, axis=0, keepdims=True)
            dcw_parts.append(dyv[:, gl] * yoff - wx
                             + jnp.where(lax.broadcasted_iota(jnp.int32, (q, 1), 0) == q - 1, dlast, 0.0))
            dgm = jnp.zeros((q, q), F32)
            diag = []
            for r in range(SSD_HPG):
                h = SSD_HPG * g + r
                hl = slice(64 * h, 64 * h + 64)
                dyb = dy_ref[:, hl].astype(BF16)
                xdh = xd_ref[:, hl]
                dm = _dot_nt(dyb, xdh)
                dmt = _dot_nt(xdh, dyb)
                dec = _decay(segcol, cum_t, h, keep)
                mt = gmt * _decay_t(segcol, cum_t, h, keep_t)
                dgm += dm * dec
                diag.append(dm * (gm * dec) - dmt * mt)
                dxd_ref[:, hl] = _dot(mt.astype(BF16), dyb) + w[:, 64 * r:64 * r + 64]
            onehots = (lanes_k == SSD_HPG * g + rows_k).astype(BF16)
            dcum += _dot_split(jnp.concatenate(diag, axis=1), onehots, 2)
            dgb = dgm.astype(BF16)
            dC_ref[:, 128 * g:128 * g + 128] = dcg + _dot(dgb, bb)
            dB_ref[:, 128 * g:128 * g + 128] = dbg + _dot_tn(dgb, cb)
        dxd = dxd_ref[...]
        dxs_ref[...] = dxd * dtw
        dcum += _dot_split(jnp.concatenate(dcw_parts, axis=1), heads_ref[...], 2)
        dla = _dot_split_rhs(tri, dcum, 3, (((0,), (0,)), ((), ())))
        ddt_ref[...] = _dot_split(xs * dxd, heads_ref[...], 2) + dla * a
        dalog_ref[...] += jnp.sum(dla * dtv, axis=0, keepdims=True)

        @pl.when(i == nc - 1)
        def _():
            dalog_ref[...] = dalog_ref[...] * a

    st_spec = pl.BlockSpec((1, SSD_GROUPS, SSD_STATE, GROUP_LANES), lambda i: (nc - 1 - i, 0, 0, 0))
    return _pc(
        body, name="ssd_bwd", grid=(nc,),
        in_specs=[rev(CONV_DIM), rev(DT_PAD), _const_spec((1, DT_PAD)), st_spec, rev(SSD_INNER),
                  _const_spec(to_wide.shape), _const_spec(to_heads.shape)],
        out_specs=[rev(SSD_INNER), rev(512), rev(512), rev(DT_PAD), _const_spec((1, DT_PAD))],
        out_shape=[jax.ShapeDtypeStruct((T, SSD_INNER), F32), jax.ShapeDtypeStruct((T, 512), F32),
                   jax.ShapeDtypeStruct((T, 512), F32), jax.ShapeDtypeStruct((T, DT_PAD), F32),
                   jax.ShapeDtypeStruct((1, DT_PAD), F32)],
        scratch_shapes=[pltpu.VMEM((SSD_GROUPS, SSD_STATE, GROUP_LANES), F32), pltpu.VMEM((q, SSD_INNER), BF16),
                        pltpu.VMEM((q, SSD_INNER), F32)],
        compiler_params=_cparams(("arbitrary",)),
    )(xbc_act, dt, alog, sprev, dy, to_wide, to_heads)


def _s5_disc_vals(a_re, a_im, log_dt, b_re, b_im):
    dt = jnp.exp(log_dt)
    mag = jnp.exp(a_re * dt)
    ab_re = mag * jnp.cos(a_im * dt)
    ab_im = mag * jnp.sin(a_im * dt)
    den = a_re * a_re + a_im * a_im
    nr = ab_re - 1.0
    ni = ab_im
    coef_re = (nr * a_re + ni * a_im) / den
    coef_im = (ni * a_re - nr * a_im) / den
    bb_re = coef_re * b_re - coef_im * b_im
    bb_im = coef_re * b_im + coef_im * b_re
    return ab_re, ab_im, bb_re, bb_im


def _s5_disc(a_re, a_im, log_dt, b_re, b_im):
    def body(ar, ai, ld, br, bi, o1, o2, o3, o4):
        o1[...], o2[...], o3[...], o4[...] = _s5_disc_vals(ar[...], ai[...], ld[...], br[...], bi[...])

    return _pc(
        body, name="s5_disc",
        out_shape=[jax.ShapeDtypeStruct((S5_STATES, 1), F32), jax.ShapeDtypeStruct((S5_STATES, 1), F32),
                   jax.ShapeDtypeStruct((S5_STATES, 16), F32), jax.ShapeDtypeStruct((S5_STATES, 16), F32)],
    )(a_re, a_im, log_dt, b_re, b_im)


def _s5_disc_bwd(a_re, a_im, log_dt, b_re, b_im, d_ab_re, d_ab_im, d_bb_re, d_bb_im):
    def body(ar, ai, ld, br, bi, g1, g2, g3, g4, o1, o2, o3, o4, o5):
        _, vjp = jax.vjp(_s5_disc_vals, ar[...], ai[...], ld[...], br[...], bi[...])
        d1, d2, d3, d4, d5 = vjp((g1[...], g2[...], g3[...], g4[...]))
        o1[...] = d1
        o2[...] = d2
        grp = lax.broadcasted_iota(jnp.int32, (32, S5_STATES), 0)
        st = lax.broadcasted_iota(jnp.int32, (32, S5_STATES), 1)
        sel = (st // 64 == grp).astype(F32)
        o3[...] = _dot_hi(sel, d3)
        o4[...] = d4
        o5[...] = d5

    return _pc(
        body, name="s5_disc_bwd",
        out_shape=[jax.ShapeDtypeStruct((S5_STATES, 1), F32), jax.ShapeDtypeStruct((S5_STATES, 1), F32),
                   jax.ShapeDtypeStruct((32, 1), F32),
                   jax.ShapeDtypeStruct((S5_STATES, 16), F32), jax.ShapeDtypeStruct((S5_STATES, 16), F32)],
    )(a_re, a_im, log_dt, b_re, b_im, d_ab_re, d_ab_im, d_bb_re, d_bb_im)


def _cmul_add(xr, xi, pr, pi, yr, yi):
    return xr + pr * yr - pi * yi, xi + pr * yi + pi * yr


def _powers(ar, ai, n):
    out = [(ar, ai)]
    for _ in range(n - 1):
        pr, pi = out[-1]
        out.append((pr * pr - pi * pi, 2.0 * pr * pi))
    return out


_BW = S5_STATES // S5_BLOCKS
_BI = S5_WIDTH // S5_BLOCKS
SUB = 8
S5_ROWS = S5_CHUNK // SUB


S5_TAB_ROWS = 8 * SUB


def _scan8(br, bi, tab_ref, reverse):
    for level, k in enumerate((1, 2, 4)):
        r0 = 2 * SUB * (level + 1)
        shift = SUB - k if reverse else k
        br, bi = _cmul_add(br, bi, tab_ref[r0:r0 + SUB, :], tab_ref[r0 + SUB:r0 + 2 * SUB, :],
                           pltpu.roll(br, shift, 0), pltpu.roll(bi, shift, 0))
    return br, bi


def _s5_tables(ab_ref, tab_ref, reverse):
    rowin = lax.broadcasted_iota(jnp.int32, (SUB, 1), 0)
    ar = ab_ref[0:1, :]
    ai = -ab_ref[1:2, :] if reverse else ab_ref[1:2, :]
    zero = jnp.zeros((SUB, S5_STATES), F32)
    for level, (pr, pi) in enumerate(_powers(ar, ai, 3)):
        k = 2 ** level
        keep = (rowin < SUB - k) if reverse else (rowin >= k)
        r0 = 2 * SUB * (level + 1)
        tab_ref[r0:r0 + SUB, :] = jnp.where(keep, pr, 0.0) + zero
        tab_ref[r0 + SUB:r0 + 2 * SUB, :] = jnp.where(keep, pi, 0.0) + zero
    hit = rowin == (SUB - 1 if reverse else 0)
    pr, pi = _scan8(jnp.where(hit, ar, 0.0) + zero, jnp.where(hit, ai, 0.0) + zero, tab_ref, reverse)
    tab_ref[0:SUB, :] = pr
    tab_ref[SUB:2 * SUB, :] = pi


def _s5_fwd(u5, wb4, wc4, ab, dvec, rider=None):
    T = u5.shape[0]
    q = S5_CHUNK
    nc = T // q

    def body(u_ref, wb_ref, wc_ref, ab_ref, d_ref, y_ref, sp_ref, carry_ref, tab_ref, sr_ref, si_ref):
        i = pl.program_id(0)
        rowin = lax.broadcasted_iota(jnp.int32, (SUB, 1), 0)

        @pl.when(i == 0)
        def _():
            carry_ref[...] = jnp.zeros_like(carry_ref)
            _s5_tables(ab_ref, tab_ref, False)

        sp_ref[0] = carry_ref[...]
        for j in range(S5_BLOCKS):
            bu = _dot(u_ref[:, _BI * j:_BI * (j + 1)].astype(BF16), wb_ref[j])
            sr_ref[:, :, _BW * j:_BW * (j + 1)] = bu[:, :_BW].reshape(S5_ROWS, SUB, _BW)
            si_ref[:, :, _BW * j:_BW * (j + 1)] = bu[:, _BW:].reshape(S5_ROWS, SUB, _BW)
        tr, ti = tab_ref[0:SUB, :], tab_ref[SUB:2 * SUB, :]
        cr, ci = carry_ref[0:1, :], carry_ref[1:2, :]
        for k in range(S5_ROWS):
            sr, si = _scan8(sr_ref[k], si_ref[k], tab_ref, False)
            sr, si = _cmul_add(sr, si, tr, ti, cr, ci)
            sr_ref[k] = sr
            si_ref[k] = si
            cr, ci = sr[SUB - 1:SUB, :], si[SUB - 1:SUB, :]
        carry_ref[0:1, :] = cr
        carry_ref[1:2, :] = ci
        for j in range(S5_BLOCKS):
            sl = slice(_BW * j, _BW * (j + 1))
            ul = slice(_BI * j, _BI * (j + 1))
            s = jnp.concatenate([sr_ref[:, :, sl].reshape(q, _BW), si_ref[:, :, sl].reshape(q, _BW)], axis=1).astype(BF16)
            y_ref[:, ul] = _dot(s, wc_ref[j]) + d_ref[:, ul] * u_ref[:, ul]

    return _call(
        body, rider, name="s5_fwd", grid=(nc,),
        in_specs=[_row_spec(q, S5_WIDTH), _const_spec((S5_BLOCKS, _BI, 2 * _BW)), _const_spec((S5_BLOCKS, 2 * _BW, _BI)),
                  _const_spec((8, S5_STATES)), _const_spec((1, S5_WIDTH))],
        out_specs=[_row_spec(q, S5_WIDTH), pl.BlockSpec((1, 8, S5_STATES), lambda i: (i, 0, 0))],
        out_shape=[jax.ShapeDtypeStruct((T, S5_WIDTH), F32), jax.ShapeDtypeStruct((nc, 8, S5_STATES), F32)],
        scratch_shapes=[pltpu.VMEM((8, S5_STATES), F32), pltpu.VMEM((S5_TAB_ROWS, S5_STATES), F32),
                        pltpu.VMEM((S5_ROWS, SUB, S5_STATES), F32), pltpu.VMEM((S5_ROWS, SUB, S5_STATES), F32)],
        compiler_params=_cparams(("arbitrary",)),
    )(u5, wb4, wc4, ab, dvec)


def _s5_bwd(u5, dy5, wb4, wc4, ab, dvec, sprev, rider=None):
    T = u5.shape[0]
    q = S5_CHUNK
    nc = T // q

    def rev(width):
        return pl.BlockSpec((q, width), lambda i: (nc - 1 - i, 0))

    def body(u_ref, dy_ref, wb_ref, wc_ref, ab_ref, d_ref, sp_ref, du_ref, dwb_ref, dwc_ref, dab_ref, dd_ref,
             carry_ref, tab_ref, rtab_ref, sr_ref, si_ref, lr_ref, li_ref):
        i = pl.program_id(0)
        rowin = lax.broadcasted_iota(jnp.int32, (SUB, 1), 0)

        @pl.when(i == 0)
        def _():
            carry_ref[...] = jnp.zeros_like(carry_ref)
            dwb_ref[...] = jnp.zeros_like(dwb_ref)
            dwc_ref[...] = jnp.zeros_like(dwc_ref)
            dab_ref[...] = jnp.zeros_like(dab_ref)
            dd_ref[...] = jnp.zeros_like(dd_ref)
            _s5_tables(ab_ref, tab_ref, False)
            _s5_tables(ab_ref, rtab_ref, True)

        for j in range(S5_BLOCKS):
            sl = slice(_BW * j, _BW * (j + 1))
            ul = slice(_BI * j, _BI * (j + 1))
            bu = _dot(u_ref[:, ul].astype(BF16), wb_ref[j])
            sr_ref[:, :, sl] = bu[:, :_BW].reshape(S5_ROWS, SUB, _BW)
            si_ref[:, :, sl] = bu[:, _BW:].reshape(S5_ROWS, SUB, _BW)
            ds = _dot_nt(dy_ref[:, ul].astype(BF16), wc_ref[j])
            lr_ref[:, :, sl] = ds[:, :_BW].reshape(S5_ROWS, SUB, _BW)
            li_ref[:, :, sl] = ds[:, _BW:].reshape(S5_ROWS, SUB, _BW)
        ar, ai = ab_ref[0:1, :], ab_ref[1:2, :]
        tr, ti = tab_ref[0:SUB, :], tab_ref[SUB:2 * SUB, :]
        cr, ci = sp_ref[0, 0:1, :], sp_ref[0, 1:2, :]
        for k in range(S5_ROWS):
            sr, si = _scan8(sr_ref[k], si_ref[k], tab_ref, False)
            sr, si = _cmul_add(sr, si, tr, ti, cr, ci)
            sr_ref[k] = sr
            si_ref[k] = si
            cr, ci = sr[SUB - 1:SUB, :], si[SUB - 1:SUB, :]
        tr, ti = rtab_ref[0:SUB, :], rtab_ref[SUB:2 * SUB, :]
        cr, ci = carry_ref[0:1, :], carry_ref[1:2, :]
        acc_r = jnp.zeros((SUB, S5_STATES), F32)
        acc_i = jnp.zeros((SUB, S5_STATES), F32)
        for k in reversed(range(S5_ROWS)):
            lr, li = _scan8(lr_ref[k], li_ref[k], rtab_ref, True)
            lr, li = _cmul_add(lr, li, tr, ti, cr, ci)
            lr_ref[k] = lr
            li_ref[k] = li
            cr, ci = lr[0:1, :], li[0:1, :]
            if k > 0:
                before_r, before_i = sr_ref[k - 1, SUB - 1:SUB, :], si_ref[k - 1, SUB - 1:SUB, :]
            else:
                before_r, before_i = sp_ref[0, 0:1, :], sp_ref[0, 1:2, :]
            keep = rowin >= 1
            pr = jnp.where(keep, pltpu.roll(sr_ref[k], 1, 0), before_r)
            pi = jnp.where(keep, pltpu.roll(si_ref[k], 1, 0), before_i)
            acc_r += lr * pr + li * pi
            acc_i += li * pr - lr * pi
        carry_ref[0:1, :] = cr
        carry_ref[1:2, :] = ci
        dab_ref[0:1, :] += jnp.sum(acc_r, axis=0, keepdims=True)
        dab_ref[1:2, :] += jnp.sum(acc_i, axis=0, keepdims=True)
        for j in range(S5_BLOCKS):
            sl = slice(_BW * j, _BW * (j + 1))
            ul = slice(_BI * j, _BI * (j + 1))
            u = u_ref[:, ul]
            dy = dy_ref[:, ul]
            dyb = dy.astype(BF16)
            lam = jnp.concatenate([lr_ref[:, :, sl].reshape(q, _BW), li_ref[:, :, sl].reshape(q, _BW)], axis=1).astype(BF16)
            s = jnp.concatenate([sr_ref[:, :, sl].reshape(q, _BW), si_ref[:, :, sl].reshape(q, _BW)], axis=1).astype(BF16)
            du_ref[:, ul] = (_dot_nt(lam, wb_ref[j]) + d_ref[:, ul] * dy).astype(BF16)
            dwb_ref[j] += _dot_tn(u.astype(BF16), lam)
            dwc_ref[j] += _dot_tn(s, dyb)
            dd_ref[:, ul] += jnp.sum(dy * u, axis=0, keepdims=True)

    big = pltpu.VMEM((S5_ROWS, SUB, S5_STATES), F32)
    return _call(
        body, rider, name="s5_bwd", grid=(nc,),
        in_specs=[rev(S5_WIDTH), rev(S5_WIDTH), _const_spec((S5_BLOCKS, _BI, 2 * _BW)), _const_spec((S5_BLOCKS, 2 * _BW, _BI)),
                  _const_spec((8, S5_STATES)), _const_spec((1, S5_WIDTH)),
                  pl.BlockSpec((1, 8, S5_STATES), lambda i: (nc - 1 - i, 0, 0))],
        out_specs=[rev(S5_WIDTH), _const_spec((S5_BLOCKS, _BI, 2 * _BW)), _const_spec((S5_BLOCKS, 2 * _BW, _BI)),
                   _const_spec((8, S5_STATES)), _const_spec((1, S5_WIDTH))],
        out_shape=[jax.ShapeDtypeStruct((T, S5_WIDTH), BF16), jax.ShapeDtypeStruct((S5_BLOCKS, _BI, 2 * _BW), F32),
                   jax.ShapeDtypeStruct((S5_BLOCKS, 2 * _BW, _BI), F32), jax.ShapeDtypeStruct((8, S5_STATES), F32),
                   jax.ShapeDtypeStruct((1, S5_WIDTH), F32)],
        scratch_shapes=[pltpu.VMEM((8, S5_STATES), F32), pltpu.VMEM((S5_TAB_ROWS, S5_STATES), F32),
                        pltpu.VMEM((S5_TAB_ROWS, S5_STATES), F32), big, big, big, big],
        compiler_params=_cparams(("arbitrary",)),
    )(u5, dy5, wb4, wc4, ab, dvec, sprev)


def _merge_vals(ys, xs, z, y5, gates, dvec, gssd, glu_w, glu_b, wbr):
    sz = _sigmoid(z)
    qv = ys + dvec * xs
    pre = qv * (z * sz)
    yn, rs = [], []
    for gi in range(SSD_GROUPS):
        p, r = _rms(pre[:, 256 * gi:256 * (gi + 1)])
        yn.append(p)
        rs.append(r)
    yn = jnp.concatenate(yn, axis=1)
    ya = yn * gssd
    gel = _gelu(y5)
    sg = _sigmoid(_dot(gel.astype(BF16), glu_w) + glu_b)
    yb = gel * sg
    pa = _dot(ya.astype(BF16), wbr[0:SSD_INNER, :])
    pb = _dot(yb.astype(BF16), wbr[SSD_INNER:, :])
    s0 = _sigmoid(gates[:, :D_MODEL])
    s1 = _sigmoid(gates[:, D_MODEL:])
    merged = s0 * pa + s1 * pb
    return dict(sz=sz, qv=qv, yn=yn, rs=rs, ya=ya, gel=gel, sg=sg, yb=yb, pa=pa, pb=pb, s0=s0, s1=s1, merged=merged)


def _merge_specs(tm):
    acts = [_row_spec(tm, 1024), _row_spec(tm, 1024, 0), _row_spec(tm, 1024), _row_spec(tm, 512), _row_spec(tm, 2048),
            _row_spec(tm, 1024)]
    params = [_const_spec((1, 1024)), _const_spec((1, 1024)), _const_spec((512, 512)), _const_spec((1, 512)),
              _hbm_spec(), _hbm_spec()]
    return acts, params


def _merge_fwd(ys, xbc_act, z, y5, gates, x, dvec, gssd, glu_w, glu_b, wbr, wout):
    T = x.shape[0]
    tm = TOKEN_TILE
    acts, params = _merge_specs(tm)

    def body(ys_ref, xs_ref, z_ref, y5_ref, gt_ref, x_ref, dv_ref, gs_ref, gw_ref, gb_ref, wbr_hbm, wout_hbm, x1_ref,
             wbr_ref, wout_ref):
        @pl.when(pl.program_id(0) == 0)
        def _():
            pltpu.sync_copy(wbr_hbm, wbr_ref)
            pltpu.sync_copy(wout_hbm, wout_ref)

        v = _merge_vals(ys_ref[...], xs_ref[...], z_ref[...], y5_ref[...], gt_ref[...], dv_ref[...], gs_ref[...],
                        gw_ref[...], gb_ref[...], wbr_ref)
        x1_ref[...] = x_ref[...] + _dot(v["merged"].astype(BF16), wout_ref[...])

    return _pc(
        body, name="merge_fwd", grid=(T // tm,),
        in_specs=acts + params, out_specs=_row_spec(tm, 1024),
        out_shape=jax.ShapeDtypeStruct((T, 1024), F32),
        scratch_shapes=[pltpu.VMEM((1536, 1024), BF16), pltpu.VMEM((1024, 1024), BF16)],
        compiler_params=_cparams(("arbitrary",)),
    )(ys, xbc_act, z, y5, gates, x, dvec, gssd, glu_w, glu_b, wbr, wout)


def _merge_bwd(ys, xbc_act, z, y5, gates, dx1, dvec, gssd, glu_w, glu_b, wbr, wout, head_sel, rider=None):
    T = dx1.shape[0]
    tm = TOKEN_TILE
    acts, params = _merge_specs(tm)

    def body(ys_ref, xs_ref, z_ref, y5_ref, gt_ref, dx1_ref, dv_ref, gs_ref, gw_ref, gb_ref, wbr_hbm, wout_hbm, hs_ref,
             dys_ref, dxs_ref, dz_ref, dy5_ref, dgt_ref, mg_ref, ya_ref, yb_ref, dpa_ref, dpb_ref, gel_ref, dpre_ref,
             ddv_ref, dgs_ref, dgb_ref, wbr_ref, wout_ref, ddacc_ref):
        i = pl.program_id(0)

        @pl.when(i == 0)
        def _():
            pltpu.sync_copy(wbr_hbm, wbr_ref)
            pltpu.sync_copy(wout_hbm, wout_ref)
            ddacc_ref[...] = jnp.zeros_like(ddacc_ref)
            dgs_ref[...] = jnp.zeros_like(dgs_ref)
            dgb_ref[...] = jnp.zeros_like(dgb_ref)

        ys, xs, z, y5, gates = ys_ref[...], xs_ref[...], z_ref[...], y5_ref[...], gt_ref[...]
        dvv, gsv, gw = dv_ref[...], gs_ref[...], gw_ref[...]
        v = _merge_vals(ys, xs, z, y5, gates, dvv, gsv, gw, gb_ref[...], wbr_ref)
        dmg = _dot_nt(dx1_ref[...].astype(BF16), wout_ref[...])
        s0, s1, pa, pb = v["s0"], v["s1"], v["pa"], v["pb"]
        dgt_ref[:, :D_MODEL] = (dmg * pa * s0 * (1.0 - s0)).astype(BF16)
        dgt_ref[:, D_MODEL:] = (dmg * pb * s1 * (1.0 - s1)).astype(BF16)
        dpa = (dmg * s0).astype(BF16)
        dpb = (dmg * s1).astype(BF16)
        dya = _dot_nt(dpa, wbr_ref[0:SSD_INNER, :])
        dyb = _dot_nt(dpb, wbr_ref[SSD_INNER:, :])
        gel, sg = v["gel"], v["sg"]
        dpre = (dyb * gel * sg * (1.0 - sg))
        dgb_ref[...] += jnp.sum(dpre, axis=0, keepdims=True)
        dpre_b = dpre.astype(BF16)
        dgel = dyb * sg + _dot_nt(dpre_b, gw)
        dy5_ref[...] = dgel * _gelu_grad(y5)
        yn = v["yn"]
        dgs_ref[...] += jnp.sum(dya * yn, axis=0, keepdims=True)
        dyn = dya * gsv
        dpre_a = jnp.concatenate(
            [_rms_bwd(yn[:, 256 * gi:256 * (gi + 1)], v["rs"][gi], dyn[:, 256 * gi:256 * (gi + 1)])
             for gi in range(SSD_GROUPS)], axis=1)
        sz, qv = v["sz"], v["qv"]
        dq = dpre_a * (z * sz)
        dz_ref[...] = (dpre_a * qv * (sz * (1.0 + z * (1.0 - sz)))).astype(BF16)
        dys_ref[...] = dq
        dxs_ref[...] = dq * dvv
        ddacc_ref[...] += jnp.sum(dq * xs, axis=0, keepdims=True)
        mg_ref[...] = v["merged"].astype(BF16)
        ya_ref[...] = v["ya"].astype(BF16)
        yb_ref[...] = v["yb"].astype(BF16)
        dpa_ref[...] = dpa
        dpb_ref[...] = dpb
        gel_ref[...] = gel.astype(BF16)
        dpre_ref[...] = dpre_b

        @pl.when(i == pl.num_programs(0) - 1)
        def _():
            ddv_ref[...] = _dot_hi(ddacc_ref[...], hs_ref[...])

    outs = [(1024, F32), (1024, F32), (1024, BF16), (512, F32), (2048, BF16),
            (1024, BF16), (1024, BF16), (512, BF16), (1024, BF16), (1024, BF16), (512, BF16), (512, BF16)]
    return _call(
        body, rider, name="merge_bwd", grid=(T // tm,),
        in_specs=acts + params + [_const_spec((1024, DT_PAD))],
        out_specs=[_row_spec(tm, w) for w, _ in outs] + [_const_spec((1, DT_PAD)), _const_spec((1, 1024)), _const_spec((1, 512))],
        out_shape=[jax.ShapeDtypeStruct((T, w), d) for w, d in outs] + [
            jax.ShapeDtypeStruct((1, DT_PAD), F32), jax.ShapeDtypeStruct((1, 1024), F32), jax.ShapeDtypeStruct((1, 512), F32)],
        scratch_shapes=[pltpu.VMEM((1536, 1024), BF16), pltpu.VMEM((1024, 1024), BF16), pltpu.VMEM((1, 1024), F32)],
        compiler_params=_cparams(("arbitrary",)),
    )(ys, xbc_act, z, y5, gates, dx1, dvec, gssd, glu_w, glu_b, wbr, wout, head_sel)


def _mlp_fwd_loss(x1, target, g, g_fin, w1, w2):
    T = x1.shape[0]
    tm = TOKEN_TILE

    def body(x_ref, t_ref, g_ref, gf_ref, w1_hbm, w2_hbm, dx_ref, loss_ref, dg_ref, w1_ref, w2_ref):
        @pl.when(pl.program_id(0) == 0)
        def _():
            pltpu.sync_copy(w1_hbm, w1_ref)
            pltpu.sync_copy(w2_hbm, w2_ref)
            loss_ref[...] = jnp.zeros_like(loss_ref)
            dg_ref[...] = jnp.zeros_like(dg_ref)

        xv = x_ref[...]
        xn, _ = _rms(xv)
        h = (xn * g_ref[...]).astype(BF16)
        acc = xv
        for s in range(FF_SHARDS):
            rl = jnp.maximum(_dot(h, w1_ref[s]), 0.0)
            acc += _dot((rl * rl).astype(BF16), w2_ref[FF_SHARD * s:FF_SHARD * (s + 1), :])
        yn, r = _rms(acc)
        gv = gf_ref[...]
        err = yn * gv - t_ref[...]
        loss_ref[...] += jnp.sum(err * err, axis=0, keepdims=True) * (0.5 / D_MODEL)
        dy = err * (1.0 / D_MODEL)
        dg_ref[...] += jnp.sum(dy * yn, axis=0, keepdims=True)
        dx_ref[...] = _rms_bwd(yn, r, dy * gv)

    return _pc(
        body, name="mlp_fwd_loss", grid=(T // tm,),
        in_specs=[_row_spec(tm, 1024), _row_spec(tm, 1024), _const_spec((1, 1024)), _const_spec((1, 1024)),
                  _hbm_spec(), _hbm_spec()],
        out_specs=[_row_spec(tm, 1024), _const_spec((1, 1024)), _const_spec((1, 1024))],
        out_shape=[jax.ShapeDtypeStruct((T, 1024), F32), jax.ShapeDtypeStruct((1, 1024), F32),
                   jax.ShapeDtypeStruct((1, 1024), F32)],
        scratch_shapes=[pltpu.VMEM((FF_SHARDS, D_MODEL, FF_SHARD), BF16), pltpu.VMEM((D_FF, D_MODEL), BF16)],
        compiler_params=_cparams(("arbitrary",)),
    )(x1, target, g, g_fin, w1, w2)


def _mlp_bwd(x1, dx2, g, w1, w2):
    T = x1.shape[0]
    tm = TOKEN_TILE

    def body(x_ref, dx2_ref, g_ref, w1_hbm, w2_hbm, dx1_ref, h_ref, act_ref, da_ref, dg_ref, w1_ref, w2_ref):
        @pl.when(pl.program_id(0) == 0)
        def _():
            pltpu.sync_copy(w1_hbm, w1_ref)
            pltpu.sync_copy(w2_hbm, w2_ref)
            dg_ref[...] = jnp.zeros_like(dg_ref)

        xn, r = _rms(x_ref[...])
        gv = g_ref[...]
        h = (xn * gv).astype(BF16)
        h_ref[...] = h
        dx2 = dx2_ref[...]
        dx2b = dx2.astype(BF16)
        dh = jnp.zeros((tm, D_MODEL), F32)
        for s in range(FF_SHARDS):
            ff = slice(FF_SHARD * s, FF_SHARD * (s + 1))
            rl = jnp.maximum(_dot(h, w1_ref[s]), 0.0)
            act_ref[:, ff] = (rl * rl).astype(BF16)
            da = (_dot_nt(dx2b, w2_ref[ff, :]) * (2.0 * rl)).astype(BF16)
            da_ref[:, ff] = da
            dh += _dot_nt(da, w1_ref[s])
        dg_ref[...] += jnp.sum(dh * xn, axis=0, keepdims=True)
        dx1_ref[...] = dx2 + _rms_bwd(xn, r, dh * gv)

    return _pc(
        body, name="mlp_bwd", grid=(T // tm,),
        in_specs=[_row_spec(tm, 1024), _row_spec(tm, 1024), _const_spec((1, 1024)), _hbm_spec(), _hbm_spec()],
        out_specs=[_row_spec(tm, 1024), _row_spec(tm, 1024), _row_spec(tm, D_FF), _row_spec(tm, D_FF), _const_spec((1, 1024))],
        out_shape=[jax.ShapeDtypeStruct((T, 1024), F32), jax.ShapeDtypeStruct((T, 1024), BF16),
                   jax.ShapeDtypeStruct((T, D_FF), BF16), jax.ShapeDtypeStruct((T, D_FF), BF16),
                   jax.ShapeDtypeStruct((1, 1024), F32)],
        scratch_shapes=[pltpu.VMEM((FF_SHARDS, D_MODEL, FF_SHARD), BF16), pltpu.VMEM((D_FF, D_MODEL), BF16)],
        compiler_params=_cparams(("arbitrary",)),
    )(x1, dx2, g, w1, w2)


WGRAD_OUT_ELEMS = 2 * 1024 * 1024
WGRAD_TILE_BYTES = 4 * 1024 * 1024


def _wgrad(a, b, name, col_shards=None, row_shards_into=None):
    T, K = a.shape
    N = b.shape[1]
    nb = N // col_shards if col_shards else min(N, 1024, max(128, WGRAD_OUT_ELEMS // K))
    tt = min(T, WGRAD_TOKENS)
    while tt * max(K * a.dtype.itemsize, nb * b.dtype.itemsize) > WGRAD_TILE_BYTES:
        tt //= 2
    assert N % nb == 0 and T % tt == 0
    in_specs = [pl.BlockSpec((tt, K), lambda n, t: (t, 0)), pl.BlockSpec((tt, nb), lambda n, t: (t, n))]
    args, aliases = [a, b], {}
    if col_shards:
        out_spec = pl.BlockSpec((None, None, K, nb), lambda n, t: (n, 0, 0, 0))
        out_shape = jax.ShapeDtypeStruct((col_shards, 2, K, nb), F32)
    elif row_shards_into is not None:
        shards, _, rows, cols = row_shards_into.shape
        assert shards * rows == K and cols == N
        out_spec = pl.BlockSpec((shards, None, rows, nb), lambda n, t: (0, 1, 0, n))
        out_shape = jax.ShapeDtypeStruct(row_shards_into.shape, F32)
        in_specs.append(_hbm_spec())
        args.append(row_shards_into)
        aliases = {2: 0}
    else:
        out_spec = pl.BlockSpec((K, nb), lambda n, t: (0, n))
        out_shape = jax.ShapeDtypeStruct((K, N), F32)

    def body(a_ref, b_ref, *rest):
        o_ref = rest[-1]

        @pl.when(pl.program_id(1) == 0)
        def _():
            o_ref[...] = jnp.zeros_like(o_ref)

        o_ref[...] += _dot_tn(a_ref[...].astype(BF16), b_ref[...].astype(BF16)).reshape(o_ref.shape)

    return _pc(
        body, name=name, grid=(N // nb, T // tt), in_specs=in_specs, out_specs=out_spec, out_shape=out_shape,
        input_output_aliases=aliases, compiler_params=_cparams(("parallel", "arbitrary")),
    )(*args)


def _s5_block_weights(bb_re, bb_im, c_re, c_im):
    eye = jnp.eye(8, dtype=F32)
    bre = bb_re.reshape(S5_BLOCKS, 8, 64, 16)
    bim = bb_im.reshape(S5_BLOCKS, 8, 64, 16)
    wb_re = jnp.einsum('jgpk,gh->jhkgp', bre, eye).reshape(S5_BLOCKS, _BI, _BW)
    wb_im = jnp.einsum('jgpk,gh->jhkgp', bim, eye).reshape(S5_BLOCKS, _BI, _BW)
    wb4 = jnp.concatenate([wb_re, wb_im], axis=2).astype(BF16)
    cre = c_re.reshape(S5_BLOCKS, 8, 16, 64)
    cim = c_im.reshape(S5_BLOCKS, 8, 16, 64)
    wc_re = jnp.einsum('jgkp,gh->jgphk', cre, eye).reshape(S5_BLOCKS, _BW, _BI)
    wc_im = jnp.einsum('jgkp,gh->jgphk', -cim, eye).reshape(S5_BLOCKS, _BW, _BI)
    wc4 = jnp.concatenate([wc_re, wc_im], axis=1).astype(BF16)
    return wb4, wc4


def _s5_block_grads(dwb4, dwc4):
    eye = jnp.eye(8, dtype=F32)
    dwb = dwb4.reshape(S5_BLOCKS, 8, 16, 2, 8, 64)
    dbb = jnp.einsum('jhkrgp,gh->rjgpk', dwb, eye).reshape(2, 32, 64, 16)
    dwc = dwc4.reshape(S5_BLOCKS, 2, 8, 64, 8, 16)
    dc = jnp.einsum('jrgphk,gh->rjgkp', dwc, eye).reshape(2, 32, 16, 64)
    return dbb[0], dbb[1], dc[0], -dc[1]


def _row(v, width=None):
    v = v.reshape(1, -1)
    if width is not None and v.shape[1] < width:
        v = jnp.concatenate([v, jnp.zeros((1, width - v.shape[1]), v.dtype)], axis=1)
    return v


def _local_step(x, target, p, comm=None):
    g_mix, g_mlp, g_fin = _row(p["norm_mix_g"]), _row(p["norm_mlp_g"]), _row(p["norm_final_g"])
    conv_b = _row(p["conv_b"])
    dt_bias = _row(p["dt_bias"], DT_PAD)
    alog = _row(p["a_log"], DT_PAD)
    dvec = _row(jnp.repeat(p["d_ssd"], SSD_HEADDIM))
    gssd = _row(p["ssd_norm_g"])
    s5d = _row(p["s5_d"])
    glu_b = _row(p["s5_glu_b"])
    head_sel = (jnp.arange(SSD_INNER)[:, None] // SSD_HEADDIM == jnp.arange(DT_PAD)[None, :]).astype(F32)

    a_re = p["s5_a_re"].reshape(S5_STATES, 1)
    a_im = p["s5_a_im"].reshape(S5_STATES, 1)
    log_dt = jnp.repeat(p["s5_log_dt"], 64).reshape(S5_STATES, 1)
    b_re = p["s5_b_re"].reshape(S5_STATES, 16)
    b_im = p["s5_b_im"].reshape(S5_STATES, 16)
    ab_re, ab_im, bb_re, bb_im = _s5_disc(a_re, a_im, log_dt, b_re, b_im)
    wb4, wc4 = _s5_block_weights(bb_re, bb_im, p["s5_c_re"], p["s5_c_im"])
    ab = jnp.concatenate([ab_re.reshape(1, S5_STATES), ab_im.reshape(1, S5_STATES), jnp.zeros((6, S5_STATES), F32)], axis=0)

    wp = p["w_in_perm"]

    z, xbc_raw, u5, gates, dt_raw, h = _inproj_fwd(x, g_mix, wp)
    xbc_act, dt = _conv_fwd(xbc_raw, dt_raw, p["conv_w"], conv_b, dt_bias)
    ys, ssd_states = _ssd_fwd(xbc_act, dt, alog)
    if comm is None:
        y5, s5_states = _s5_fwd(u5, wb4, wc4, ab, s5d)
    else:
        (y5, s5_states), late = _s5_fwd(u5, wb4, wc4, ab, s5d, rider=_Gather(comm["late_srcs"], comm["late_ks"]))
        p = {**p, **comm["late_unpack"](late)}
    wbr, wout, w1, w2, glu_w = p["w_branch"], p["w_out"], p["w_mlp_in"], p["w_mlp_out"], p["s5_glu_w"]
    x1 = _merge_fwd(ys, xbc_act, z, y5, gates, x, dvec, gssd, glu_w, glu_b, wbr, wout)
    dx2, loss_lanes, d_gfin = _mlp_fwd_loss(x1, target, g_mlp, g_fin, w1, w2)

    dx1, h2, act, da1, d_gmlp = _mlp_bwd(x1, dx2, g_mlp, w1, w2)
    g_mlp4 = _wgrad(h2, da1, "wgrad_mlp_in", col_shards=FF_SHARDS)
    g_mlp4 = _wgrad(act, dx2, "wgrad_mlp_out", row_shards_into=g_mlp4)
    d_w_mlp_in, d_w_mlp_out = g_mlp4[:, 0], g_mlp4[:, 1].reshape(D_FF, D_MODEL)
    merge_args = (ys, xbc_act, z, y5, gates, dx1, dvec, gssd, glu_w, glu_b, wbr, wout, head_sel)
    if comm is None:
        merge_out = _merge_bwd(*merge_args)
    else:
        g_mlp = g_mlp4.reshape(N_CHIPS, 2 * FF_SHARD, D_MODEL)
        merge_out, (sib_mlp,) = _merge_bwd(*merge_args, rider=_Pair([g_mlp]))
        pf_mlp, pb_mlp = _pair_sum(comm["place"], g_mlp, sib_mlp, "pair_sum_mlp")
    (dys, dxs_m, dz, dy5, dgates, mg, ya, yb, dpa, dpb, gel, dpre, d_dssd, d_gssd, d_glu_b) = merge_out
    d_w_out = _wgrad(mg, dx1, "wgrad_out")
    d_w_branch = jnp.concatenate([_wgrad(ya, dpa, "wgrad_branch_a"), _wgrad(yb, dpb, "wgrad_branch_b")], axis=0)
    d_glu_w = _wgrad(gel, dpre, "wgrad_glu")
    s5_args = (u5, dy5, wb4, wc4, ab, s5d, s5_states)
    if comm is None:
        du5, dwb4, dwc4, dab, d_s5d = _s5_bwd(*s5_args)
        mlp_total = None
    else:
        (du5, dwb4, dwc4, dab, d_s5d), (got_mlp,) = _s5_bwd(*s5_args, rider=_Chip([pb_mlp]))
        mlp_total = _chip_sum(comm["place"], pf_mlp, got_mlp, "chip_sum_mlp")
    dbb_re, dbb_im, d_c_re, d_c_im = _s5_block_grads(dwb4, dwc4)
    d_a_re, d_a_im, d_log_dt, d_b_re, d_b_im = _s5_disc_bwd(
        a_re, a_im, log_dt, b_re, b_im, dab[0].reshape(S5_STATES, 1), dab[1].reshape(S5_STATES, 1),
        dbb_re.reshape(S5_STATES, 16), dbb_im.reshape(S5_STATES, 16))
    dxs_s, dB, dC, ddt, d_alog = _ssd_bwd(xbc_act, dt, alog, ssd_states, dys)
    dxbc_raw, ddt_raw, d_conv_w, d_conv_b, d_dt_bias = _conv_bwd(
        xbc_raw, dt_raw, dxs_m, dxs_s, dB, dC, ddt, p["conv_w"], conv_b, dt_bias)
    d_w_in = dict(z=_wgrad(h, dz, "wgrad_in_z"), xbc=_wgrad(h, dxbc_raw, "wgrad_in_xbc"),
                  dt=_wgrad(h, ddt_raw, "wgrad_in_dt")[:, :16], u5=_wgrad(h, du5, "wgrad_in_u5"),
                  gates=_wgrad(h, dgates, "wgrad_in_gates"))
    w_in_pieces = [(c0, d_w_in[n]) for n, c0, _ in W_IN_PIECES]
    inproj_args = (x, dx1, dz, dxbc_raw, du5, dgates, ddt_raw, g_mix, wp)
    if comm is None:
        dx, d_gmix = _inproj_bwd(*inproj_args)
        late_totals = None
    else:
        g_b, g_in = _late_buffers(d_w_out, d_w_branch, d_glu_w, d_conv_w[:CONV_K], w_in_pieces)
        sib_b, sib_in = _exchange(_Pair([g_b, g_in]), "pair_exchange")
        pf_b, pb_b = _pair_sum(comm["place"], g_b, sib_b, "pair_sum_b")
        pf_in, pb_in = _pair_sum(comm["place"], g_in, sib_in, "pair_sum_in")
        (dx, d_gmix), (got_b, got_in) = _inproj_bwd(*inproj_args, rider=_Chip([pb_b, pb_in]))
        late_totals = (_chip_sum(comm["place"], pf_b, got_b, "chip_sum_b"),
                       _chip_sum(comm["place"], pf_in, got_in, "chip_sum_in"))

    grads = dict(
        norm_mix_g=d_gmix.reshape(-1), w_in_pieces=w_in_pieces, late_totals=late_totals,
        conv_w=d_conv_w[:CONV_K], conv_b=d_conv_b.reshape(-1),
        dt_bias=d_dt_bias[0, :16], a_log=d_alog[0, :16], d_ssd=d_dssd[0, :16], ssd_norm_g=d_gssd.reshape(-1),
        s5_a_re=d_a_re.reshape(32, 64), s5_a_im=d_a_im.reshape(32, 64), s5_log_dt=d_log_dt.reshape(32),
        s5_b_re=d_b_re.reshape(32, 64, 16), s5_b_im=d_b_im.reshape(32, 64, 16), s5_c_re=d_c_re, s5_c_im=d_c_im,
        s5_d=d_s5d.reshape(-1), s5_glu_w=d_glu_w, s5_glu_b=d_glu_b.reshape(-1), w_branch=d_w_branch, w_out=d_w_out,
        norm_mlp_g=d_gmlp.reshape(-1), w_mlp_in=d_w_mlp_in, w_mlp_out=d_w_mlp_out, norm_final_g=d_gfin.reshape(-1),
        mlp_total=mlp_total)
    return jnp.sum(loss_lanes), dx, grads


MESH = pl.DeviceIdType.MESH
N_CHIPS = 4


def _place():
    x, y, c = lax.axis_index("x"), lax.axis_index("y"), lax.axis_index("c")
    chips = [(1 - x, y), (x, 1 - y), (1 - x, 1 - y)]
    return x, y, c, chips


def _remote(src, dst, send_sems, recv_sems, k, to):
    return pltpu.make_async_remote_copy(src_ref=src, dst_ref=dst, send_sem=send_sems.at[k], recv_sem=recv_sems.at[k],
                                        device_id=to, device_id_type=MESH)


def _row_chunks(rows, k, align):
    step = rows // k
    assert rows % k == 0 and step % align == 0, (rows, k, align)
    return [(i * step, step) for i in range(k)]


ICI_CHUNKS = 4
D2D_CHUNKS = 24


class _Gather:
    def __init__(self, srcs, ks):
        self.inputs = list(srcs)
        self.out_shapes = [jax.ShapeDtypeStruct((N_CHIPS,) + a.shape, a.dtype) for a in srcs]
        self.halves = [a.shape[0] // 2 for a in srcs]
        self.pieces = [_row_chunks(h, k, 32 // a.dtype.itemsize) for a, h, k in zip(srcs, self.halves, ks)]
        self.n_ici = 3 * sum(ks)
        self.n_sems = 2 * self.n_ici + len(srcs)

    def _plan(self, src_refs, out_refs, send_sems, recv_sems):
        x, y, c, chips = _place()
        own = 2 * x + y
        sib = (x, y, 1 - c)
        first, fwd_plan, k = [], [], 0
        for a, (src_ref, out_ref) in enumerate(zip(src_refs, out_refs)):
            h = self.halves[a]
            for r0, nr in self.pieces[a]:
                for cx, cy in chips:
                    first.append(_remote(src_ref.at[pl.ds(c * h + r0, nr), :], out_ref.at[own, pl.ds(c * h + r0, nr), :],
                                         send_sems, recv_sems, k, (cx, cy, c)))
                    fwd_plan.append((out_ref, 2 * cx + cy, h, r0, nr, k, (cx, cy, c)))
                    k += 1
        for a, (src_ref, out_ref) in enumerate(zip(src_refs, out_refs)):
            first.append(_remote(src_ref, out_ref.at[own], send_sems, recv_sems, 2 * self.n_ici + a, sib))
        return first, fwd_plan, c, sib

    def issue(self, src_refs, out_refs, send_sems, recv_sems):
        for cp in self._plan(src_refs, out_refs, send_sems, recv_sems)[0]:
            cp.start()

    def complete(self, src_refs, out_refs, send_sems, recv_sems):
        first, fwd_plan, c, sib = self._plan(src_refs, out_refs, send_sems, recv_sems)
        passed = []
        for out_ref, s, h, r0, nr, k, frm in fwd_plan:
            got = out_ref.at[s, pl.ds(c * h + r0, nr), :]
            _remote(got, got, send_sems, recv_sems, k, frm).wait_recv()
            fw = _remote(got, got, send_sems, recv_sems, self.n_ici + k, sib)
            fw.start()
            passed.append(fw)
        for out_ref, s, h, r0, nr, k, frm in fwd_plan:
            got = out_ref.at[s, pl.ds((1 - c) * h + r0, nr), :]
            _remote(got, got, send_sems, recv_sems, self.n_ici + k, sib).wait_recv()
        own_copies = first[self.n_ici:]
        for cp in own_copies:
            cp.wait_recv()
        for cp in first + passed:
            cp.wait_send()


def _exchange(rider, name):
    ri, ro = len(rider.inputs), len(rider.out_shapes)

    def body(*refs):
        rider.issue(refs[:ri], refs[ri:ri + ro], *refs[ri + ro:])
        rider.complete(refs[:ri], refs[ri:ri + ro], *refs[ri + ro:])

    return _pc(
        body, name=name, in_specs=[_hbm_spec()] * ri, out_specs=[_hbm_spec()] * ro, out_shape=list(rider.out_shapes),
        scratch_shapes=[pltpu.SemaphoreType.DMA((rider.n_sems,))] * 2,
    )(*rider.inputs)


def _call(body, rider=None, **kw):
    if rider is None:
        return _pc(body, **kw)
    single = not isinstance(kw["out_shape"], (list, tuple))
    out_specs = [kw["out_specs"]] if single else list(kw["out_specs"])
    out_shape = [kw["out_shape"]] if single else list(kw["out_shape"])
    scratch = list(kw.get("scratch_shapes", ()))
    n_in, n_out, n_scr = len(kw["in_specs"]), len(out_specs), len(scratch)
    ri, ro = len(rider.inputs), len(rider.out_shapes)
    steps = kw["grid"][0]

    def wrapped(*refs):
        o0 = n_in + ri
        s0 = o0 + n_out + ro
        r_in, r_out, sems = refs[n_in:o0], refs[o0 + n_out:s0], refs[s0 + n_scr:]

        @pl.when(pl.program_id(0) == 0)
        def _():
            rider.issue(r_in, r_out, *sems)

        body(*refs[:n_in], *refs[o0:o0 + n_out], *refs[s0:s0 + n_scr])

        @pl.when(pl.program_id(0) == steps - 1)
        def _():
            rider.complete(r_in, r_out, *sems)

    f = _pc(wrapped, name=kw["name"], grid=kw["grid"], in_specs=list(kw["in_specs"]) + [_hbm_spec()] * ri,
            out_specs=out_specs + [_hbm_spec()] * ro, out_shape=out_shape + list(rider.out_shapes),
            scratch_shapes=scratch + [pltpu.SemaphoreType.DMA((rider.n_sems,))] * 2, compiler_params=kw["compiler_params"])

    def run(*args):
        res = f(*args, *rider.inputs)
        return (res[0] if single else res[:n_out]), res[n_out:]

    return run


def _d2d_pieces(rows):
    k = next(k for k in range(24, 0, -1) if rows % k == 0 and (rows // k) % 8 == 0)
    return _row_chunks(rows, k, 8)


class _Pair:
    def __init__(self, gs, small=None):
        self.n = len(gs)
        self.halves = [g.shape[1] // 2 for g in gs]
        self.inputs = list(gs) + ([small] if small is not None else [])
        self.out_shapes = [jax.ShapeDtypeStruct((N_CHIPS, h, g.shape[2]), F32) for g, h in zip(gs, self.halves)]
        if small is not None:
            self.out_shapes.append(jax.ShapeDtypeStruct(small.shape, F32))
        self.n_sems = len(self.inputs)

    def issue(self, in_refs, out_refs, send_sems, recv_sems):
        x, y, c, _ = _place()
        sib = (x, y, 1 - c)
        for a in range(self.n):
            h = self.halves[a]
            for s in range(N_CHIPS):
                for r0, nr in _d2d_pieces(h):
                    _remote(in_refs[a].at[s, pl.ds((1 - c) * h + r0, nr), :], out_refs[a].at[s, pl.ds(r0, nr), :],
                            send_sems, recv_sems, a, sib).start()
        for a in range(self.n, len(self.inputs)):
            _remote(in_refs[a], out_refs[a], send_sems, recv_sems, a, sib).start()

    def complete(self, in_refs, out_refs, send_sems, recv_sems):
        x, y, c, _ = _place()
        for a in range(len(self.inputs)):
            _remote(out_refs[a], out_refs[a], send_sems, recv_sems, a, (x, y, 1 - c)).wait()


SUM_BLOCKS = 4


def _pair_sum(place, g, sib, name):
    n, R, C = g.shape
    H = R // 2
    rb = H // SUM_BLOCKS
    assert H % SUM_BLOCKS == 0 and rb % 16 == 0

    def body(place_ref, a_ref, b_ref, pf_ref, pb_ref):
        p = a_ref[...] + b_ref[...]
        pf_ref[...] = p
        pb_ref[...] = p.astype(BF16)

    blk = pl.BlockSpec((1, rb, C), lambda s, i, pr: (s, i, 0))
    mine = pl.BlockSpec((1, rb, C), lambda s, i, pr: (s, pr[1] * SUM_BLOCKS + i, 0))
    return _pc(
        body, name=name, out_shape=[jax.ShapeDtypeStruct((n, H, C), F32), jax.ShapeDtypeStruct((n, H, C), BF16)],
        grid_spec=pltpu.PrefetchScalarGridSpec(num_scalar_prefetch=1, grid=(n, SUM_BLOCKS), in_specs=[mine, blk],
                                               out_specs=[blk, blk]),
        compiler_params=_cparams(("arbitrary", "arbitrary")),
    )(place, g, sib)


class _Chip:
    def __init__(self, pbs, psmall=None):
        self.n = len(pbs)
        self.rows = [pb.shape[1] for pb in pbs]
        self.inputs = list(pbs) + ([psmall] if psmall is not None else [])
        self.out_shapes = [jax.ShapeDtypeStruct((3,) + pb.shape[1:], BF16) for pb in pbs]
        if psmall is not None:
            self.out_shapes.append(jax.ShapeDtypeStruct((N_CHIPS,) + psmall.shape, F32))
        self.n_sems = 3 * len(self.inputs)

    def issue(self, in_refs, out_refs, send_sems, recv_sems):
        x, y, c, chips = _place()
        own = 2 * x + y
        for j, (cx, cy) in enumerate(chips):
            for a in range(self.n):
                for r0, nr in _row_chunks(self.rows[a], ICI_CHUNKS, 16):
                    _remote(in_refs[a].at[2 * cx + cy, pl.ds(r0, nr), :], out_refs[a].at[j, pl.ds(r0, nr), :],
                            send_sems, recv_sems, 3 * a + j, (cx, cy, c)).start()
            for a in range(self.n, len(self.inputs)):
                _remote(in_refs[a], out_refs[a].at[own], send_sems, recv_sems, 3 * a + j, (cx, cy, c)).start()

    def complete(self, in_refs, out_refs, send_sems, recv_sems):
        x, y, c, chips = _place()
        own = 2 * x + y
        for j, (cx, cy) in enumerate(chips):
            for a in range(self.n):
                _remote(in_refs[a].at[own], out_refs[a].at[j], send_sems, recv_sems, 3 * a + j, (cx, cy, c)).wait()
            for a in range(self.n, len(self.inputs)):
                _remote(in_refs[a], out_refs[a].at[2 * cx + cy], send_sems, recv_sems, 3 * a + j, (cx, cy, c)).wait()


def _chip_sum(place, pf, got, name):
    _, H, C = pf.shape
    rb = H // SUM_BLOCKS

    def body(place_ref, o_ref, g_ref, tot_ref):
        tot_ref[...] = ((o_ref[0] + g_ref[0].astype(F32)) + g_ref[1].astype(F32)) + g_ref[2].astype(F32)

    ins = [pl.BlockSpec((1, rb, C), lambda i, pr: (pr[0], i, 0)), pl.BlockSpec((3, rb, C), lambda i, pr: (0, i, 0))]
    out = pl.BlockSpec((rb, C), lambda i, pr: (pr[1] * SUM_BLOCKS + i, 0))
    return _pc(
        body, name=name, out_shape=jax.ShapeDtypeStruct((2 * H, C), F32),
        grid_spec=pltpu.PrefetchScalarGridSpec(num_scalar_prefetch=1, grid=(SUM_BLOCKS,), in_specs=ins, out_specs=out),
        compiler_params=_cparams(("arbitrary",)),
    )(place, pf, got)


def _half_exchange(fulls):
    n = len(fulls)

    def body(*refs):
        in_refs, out_refs = refs[:n], refs[n:2 * n]
        send_sems, recv_sems = refs[2 * n:]
        x, y, c, _ = _place()
        sib = (x, y, 1 - c)
        for a in range(n):
            h = fulls[a].shape[0] // 2
            for r0, nr in _d2d_pieces(h):
                rows = pl.ds(c * h + r0, nr)
                _remote(in_refs[a].at[rows, :], out_refs[a].at[rows, :], send_sems, recv_sems, a, sib).start()
        for a in range(n):
            h = fulls[a].shape[0] // 2
            _remote(in_refs[a].at[pl.ds(c * h, h), :], out_refs[a].at[pl.ds((1 - c) * h, h), :], send_sems, recv_sems, a,
                    sib).wait()

    return _pc(
        body, name="half_exchange", in_specs=[_hbm_spec()] * n, out_specs=[_hbm_spec()] * n,
        out_shape=[jax.ShapeDtypeStruct(f.shape, F32) for f in fulls],
        input_output_aliases={a: a for a in range(n)},
        scratch_shapes=[pltpu.SemaphoreType.DMA((n,)), pltpu.SemaphoreType.DMA((n,))],
    )(*fulls)


def _small_allreduce(pack):
    R, C = pack.shape

    def body(p_ref, o_ref, sib_ref, pair_ref, slots_ref, send_sems, recv_sems):
        x, y, c, chips = _place()
        own = 2 * x + y
        cp = _remote(p_ref, sib_ref, send_sems, recv_sems, 0, (x, y, 1 - c))
        cp.start()
        cp.wait()
        pair_ref[...] = p_ref[...] + sib_ref[...]
        slots_ref[own] = pair_ref[...]
        out = [_remote(pair_ref, slots_ref.at[own], send_sems, recv_sems, 1 + j, (cx, cy, c)) for j, (cx, cy) in enumerate(chips)]
        for cp in out:
            cp.start()
        for j, (cx, cy) in enumerate(chips):
            _remote(pair_ref, slots_ref.at[2 * cx + cy], send_sems, recv_sems, 1 + j, (cx, cy, c)).wait()
        o_ref[...] = ((slots_ref[0] + slots_ref[1]) + slots_ref[2]) + slots_ref[3]

    vmem = pl.BlockSpec(memory_space=pltpu.VMEM)
    return _pc(
        body, name="small_allreduce", in_specs=[vmem], out_specs=vmem, out_shape=jax.ShapeDtypeStruct((R, C), F32),
        scratch_shapes=[pltpu.VMEM((R, C), F32), pltpu.VMEM((R, C), F32), pltpu.VMEM((N_CHIPS, R, C), F32),
                        pltpu.SemaphoreType.DMA((4,)), pltpu.SemaphoreType.DMA((4,))],
    )(pack)


def _adamw(w, g, m, v, name, g_row0=0, with_grad=False, col_block=None):
    R, C = w.shape
    rb = 256 if R % 256 == 0 else (128 if R % 128 == 0 else R)
    if col_block:
        rb = R
    assert g_row0 % rb == 0

    def body(w_ref, g_ref, m_ref, v_ref, d_ref, nm_ref, nv_ref, *g_out):
        gv = g_ref[...]
        m2 = ADAM_B1 * m_ref[...] + (1.0 - ADAM_B1) * gv
        v2 = ADAM_B2 * v_ref[...] + (1.0 - ADAM_B2) * (gv * gv)
        m_hat = m2 * (1.0 / (1.0 - ADAM_B1 ** ADAM_STEP))
        v_hat = v2 * (1.0 / (1.0 - ADAM_B2 ** ADAM_STEP))
        d_ref[...] = -ADAM_LR * (m_hat / (jnp.sqrt(v_hat) + ADAM_EPS) + ADAM_WD * w_ref[...])
        nm_ref[...] = m2
        nv_ref[...] = v2
        if with_grad:
            g_out[0][...] = gv

    if col_block:
        spec = g_spec = pl.BlockSpec((R, col_block), lambda i: (0, i))
        steps = C // col_block
    else:
        spec = pl.BlockSpec((rb, C), lambda i: (i, 0))
        g_spec = pl.BlockSpec((rb, C), lambda i: (g_row0 // rb + i, 0))
        steps = R // rb
    n_out = 4 if with_grad else 3
    return _pc(
        body, name=name, grid=(steps,), in_specs=[spec, g_spec, spec, spec], out_specs=[spec] * n_out,
        out_shape=[jax.ShapeDtypeStruct((R, C), F32)] * n_out, compiler_params=_cparams(("parallel",)),
    )(w, g, m, v)


PACK_COLS = 1024
ROWS_A = (("w_mlp_in", 0, 1024), ("w_mlp_out", 1024, 1024), ("w_out", 2048, 256), ("w_branch", 2304, 384))
ROWS_A_TOTAL = 2688
ROWS_B = (("w_out", 0, 256), ("w_branch", 256, 384))
ROW_B_GLU, ROW_B_CONV, ROWS_B_TOTAL = 640, 704, 768
W_IN_SHARD = 1412
CONV_PAD_ROWS = 16
SMALL = (("norm_mix_g", (1024,)), ("conv_b", (2048,)), ("dt_bias", (16,)), ("a_log", (16,)), ("d_ssd", (16,)),
         ("ssd_norm_g", (1024,)), ("s5_a_re", (32, 64)), ("s5_a_im", (32, 64)), ("s5_log_dt", (32,)),
         ("s5_b_re", (32, 64, 16)), ("s5_b_im", (32, 64, 16)), ("s5_c_re", (32, 16, 64)), ("s5_c_im", (32, 16, 64)),
         ("s5_d", (512,)), ("s5_glu_b", (512,)), ("norm_mlp_g", (1024,)), ("norm_final_g", (1024,)))
SMALL_ROWS = 144
SMALL_COUNT = sum(math.prod(shp) for _, shp in SMALL)
GLU_ROWS = S5_WIDTH * S5_WIDTH // PACK_COLS
CONV_ROWS = CONV_K * CONV_DIM // PACK_COLS
W_IN_PIECES = (("z", 0, 1024), ("xbc", 1024, 2048), ("dt", OFF_DT, 16), ("u5", OFF_U, 512), ("gates", 3600, 2048))


def _pack_small(parts):
    flat = jnp.concatenate([a.astype(F32).reshape(-1) for a in parts])
    return jnp.concatenate([flat, jnp.zeros((SMALL_ROWS * PACK_COLS - flat.shape[0],), F32)]).reshape(SMALL_ROWS, PACK_COLS)


def _unpack_small(pack):
    flat, out, r = pack.reshape(-1), {}, 0
    for name, shp in SMALL:
        n = math.prod(shp)
        out[name] = flat[r:r + n].reshape(shp)
        r += n
    return out


def _late_buffers(d_w_out, d_w_branch, d_glu_w, d_conv_w, w_in_pieces):
    conv4 = d_conv_w.reshape(CONV_K, N_CHIPS, 512).transpose(1, 0, 2).reshape(N_CHIPS, CONV_ROWS // N_CHIPS, PACK_COLS)
    g_b = jnp.concatenate(
        [d_w_out.reshape(N_CHIPS, -1, PACK_COLS), d_w_branch.reshape(N_CHIPS, -1, PACK_COLS),
         d_glu_w.reshape(N_CHIPS, GLU_ROWS // N_CHIPS, PACK_COLS),
         jnp.pad(conv4, ((0, 0), (0, ROWS_B_TOTAL - ROW_B_CONV - CONV_ROWS // N_CHIPS), (0, 0)))], axis=1)
    g_in = jnp.stack([jnp.concatenate(_column_range(w_in_pieces, W_IN_SHARD * s, W_IN_SHARD * (s + 1)), axis=1)
                      for s in range(N_CHIPS)])
    return g_b, g_in


def _column_range(pieces, lo, hi):
    out = []
    for c0, a in pieces:
        a0, a1 = max(lo, c0), min(hi, c0 + a.shape[-1])
        if a0 < a1:
            out.append(a[..., a0 - c0:a1 - c0])
    return out


def kernel(x, norm_mix_g, w_in, conv_w, conv_b, dt_bias, a_log, d_ssd, ssd_norm_g, s5_a_re, s5_a_im, s5_log_dt, s5_b_re, s5_b_im, s5_c_re, s5_c_im, s5_d, s5_glu_w, s5_glu_b, w_branch, w_out, norm_mlp_g, w_mlp_in, w_mlp_out, norm_final_g, loss_target, m_norm_mix_g, m_w_in, m_conv_w, m_conv_b, m_dt_bias, m_a_log, m_d_ssd, m_ssd_norm_g, m_s5_a_re, m_s5_a_im, m_s5_log_dt, m_s5_b_re, m_s5_b_im, m_s5_c_re, m_s5_c_im, m_s5_d, m_s5_glu_w, m_s5_glu_b, m_w_branch, m_w_out, m_norm_mlp_g, m_w_mlp_in, m_w_mlp_out, m_norm_final_g, v_norm_mix_g, v_w_in, v_conv_w, v_conv_b, v_dt_bias, v_a_log, v_d_ssd, v_ssd_norm_g, v_s5_a_re, v_s5_a_im, v_s5_log_dt, v_s5_b_re, v_s5_b_im, v_s5_c_re, v_s5_c_im, v_s5_d, v_s5_glu_w, v_s5_glu_b, v_w_branch, v_w_out, v_norm_mlp_g, v_w_mlp_in, v_w_mlp_out, v_norm_final_g):
    names = ("norm_mix_g", "w_in", "conv_w", "conv_b", "dt_bias", "a_log", "d_ssd", "ssd_norm_g", "s5_a_re", "s5_a_im",
             "s5_log_dt", "s5_b_re", "s5_b_im", "s5_c_re", "s5_c_im", "s5_d", "s5_glu_w", "s5_glu_b", "w_branch", "w_out",
             "norm_mlp_g", "w_mlp_in", "w_mlp_out", "norm_final_g")
    w = dict(zip(names, (norm_mix_g, w_in, conv_w, conv_b, dt_bias, a_log, d_ssd, ssd_norm_g, s5_a_re, s5_a_im, s5_log_dt,
                         s5_b_re, s5_b_im, s5_c_re, s5_c_im, s5_d, s5_glu_w, s5_glu_b, w_branch, w_out, norm_mlp_g,
                         w_mlp_in, w_mlp_out, norm_final_g)))
    m = dict(zip(names, (m_norm_mix_g, m_w_in, m_conv_w, m_conv_b, m_dt_bias, m_a_log, m_d_ssd, m_ssd_norm_g, m_s5_a_re,
                         m_s5_a_im, m_s5_log_dt, m_s5_b_re, m_s5_b_im, m_s5_c_re, m_s5_c_im, m_s5_d, m_s5_glu_w,
                         m_s5_glu_b, m_w_branch, m_w_out, m_norm_mlp_g, m_w_mlp_in, m_w_mlp_out, m_norm_final_g)))
    v = dict(zip(names, (v_norm_mix_g, v_w_in, v_conv_w, v_conv_b, v_dt_bias, v_a_log, v_d_ssd, v_ssd_norm_g, v_s5_a_re,
                         v_s5_a_im, v_s5_log_dt, v_s5_b_re, v_s5_b_im, v_s5_c_re, v_s5_c_im, v_s5_d, v_s5_glu_w,
                         v_s5_glu_b, v_w_branch, v_w_out, v_norm_mlp_g, v_w_mlp_in, v_w_mlp_out, v_norm_final_g)))

    cx, cy, cc = lax.axis_index("x"), lax.axis_index("y"), lax.axis_index("c")
    own = 2 * cx + cy
    place = jnp.stack([own, cc]).astype(jnp.int32)

    src_conv = jnp.concatenate([conv_w, jnp.zeros((CONV_PAD_ROWS - CONV_K, 512), F32)], axis=0)
    all_in, all_conv = _exchange(_Gather([w_in.astype(BF16), src_conv], [ICI_CHUNKS, 1]), "gather_first")
    p = {n: w[n] for n, _ in SMALL}
    p["conv_w"] = jnp.concatenate([all_conv[s, :CONV_K] for s in range(N_CHIPS)], axis=1)
    shards = [(W_IN_SHARD * s, all_in[s]) for s in range(N_CHIPS)]
    p["w_in_perm"] = jnp.concatenate(
        _column_range(shards, 0, OFF_DT) + _column_range(shards, OFF_U, D_IN_PROJ) + _column_range(shards, OFF_DT, OFF_U)
        + [jnp.zeros((D_MODEL, DT_PAD - 16), BF16)], axis=1)

    def late_unpack(gathered):
        all_a, all_glu = gathered
        out = {"w_mlp_in": all_a[:, 0:1024], "s5_glu_w": all_glu.reshape(S5_WIDTH, S5_WIDTH)}
        for n, r0, nr in ROWS_A[1:]:
            out[n] = all_a[:, r0:r0 + nr].reshape(N_CHIPS * nr, PACK_COLS)
        return out

    comm = dict(place=place, late_ks=[ICI_CHUNKS, 1], late_unpack=late_unpack,
                late_srcs=[jnp.concatenate([w[n].astype(BF16) for n, _, _ in ROWS_A], axis=0), s5_glu_w.astype(BF16)])
    loss_part, grad_x, g = _local_step(x[0], loss_target[0], p, comm)

    red_mlp, red_b, red_in = _half_exchange([g["mlp_total"], *g["late_totals"]])
    small_tot = _small_allreduce(_pack_small([g[n] for n, _ in SMALL] + [loss_part.reshape(1)]))
    loss = small_tot.reshape(-1)[SMALL_COUNT]

    grads = _unpack_small(small_tot)
    delta, new_m, new_v = {}, {}, {}
    for n, r0, _ in ROWS_A[:2]:
        delta[n], new_m[n], new_v[n], grads[n] = _adamw(w[n], red_mlp, m[n], v[n], "adamw_" + n, g_row0=r0, with_grad=True)
    for n, r0, _ in ROWS_B:
        delta[n], new_m[n], new_v[n], grads[n] = _adamw(w[n], red_b, m[n], v[n], "adamw_" + n, g_row0=r0, with_grad=True)
    d_t, m_t, v_t, g_t = _adamw(w_in.T, red_in.T, m_w_in.T, v_w_in.T, "adamw_w_in", with_grad=True, col_block=128)
    delta["w_in"], new_m["w_in"], new_v["w_in"], grads["w_in"] = d_t.T, m_t.T, v_t.T, g_t.T
    grads["s5_glu_w"] = red_b[ROW_B_GLU:ROW_B_GLU + GLU_ROWS // N_CHIPS].reshape(S5_WIDTH // N_CHIPS, S5_WIDTH)
    grads["conv_w"] = red_b[ROW_B_CONV:ROW_B_CONV + CONV_ROWS // N_CHIPS].reshape(CONV_K, CONV_DIM // N_CHIPS)
    for n in ("s5_glu_w", "conv_w"):
        delta[n], new_m[n], new_v[n] = _adamw(w[n], grads[n], m[n], v[n], "adamw_" + n)
    ds, ms, vs = _adamw(_pack_small([w[n] for n, _ in SMALL]), small_tot, _pack_small([m[n] for n, _ in SMALL]),
                        _pack_small([v[n] for n, _ in SMALL]), "adamw_small")
    delta.update(_unpack_small(ds))
    new_m.update(_unpack_small(ms))
    new_v.update(_unpack_small(vs))

    return (loss, grad_x[None], *[grads[n] for n in names], *[delta[n] for n in names],
            *[new_m[n] for n in names], *[new_v[n] for n in names])
```

```python
import functools
import math

import jax
import jax.numpy as jnp
from jax import lax
from jax.experimental import pallas as pl
from jax.experimental.pallas import tpu as pltpu

F32 = jnp.float32
BF16 = jnp.bfloat16

D_MODEL = 1024
SSD_INNER = 1024
SSD_HEADS = 16
SSD_HEADDIM = 64
SSD_GROUPS = 4
SSD_HPG = 4
SSD_STATE = 128
SSD_CHUNK = 128
CONV_K = 4
CONV_DIM = 2048
S5_WIDTH = 512
S5_STATES = 2048
S5_BLOCKS = 4
S5_CHUNK = 128
D_FF = 4096
FF_SHARDS = 4
FF_SHARD = D_FF // FF_SHARDS
EPS = 1e-6
P_Z, P_XBC, P_U5, P_G, P_DT, P_END = 0, 1024, 3072, 3584, 5632, 5760
DT_PAD = 128
OFF_DT, OFF_U = 3072, 3088
D_IN_PROJ = 5648

ADAM_LR, ADAM_B1, ADAM_B2, ADAM_EPS, ADAM_WD, ADAM_STEP = 0.001, 0.9, 0.999, 1e-08, 0.01, 10

TOKEN_TILE = 256
VMEM_LIMIT = 56 * 1024 * 1024
HALO = 8
CONV_COLS = 256
CONV_ROWS_BLK = 64
WGRAD_TOKENS = 2048


def _pc(body, **kw):
    return pl.pallas_call(body, **kw)


def _cparams(sem=None):
    return pltpu.CompilerParams(dimension_semantics=sem, vmem_limit_bytes=VMEM_LIMIT)


def _dot(a, b):
    return jnp.dot(a, b, preferred_element_type=F32)


def _dot_nt(a, b):
    return lax.dot_general(a, b, (((1,), (1,)), ((), ())), preferred_element_type=F32)


def _dot_tn(a, b):
    return lax.dot_general(a, b, (((0,), (0,)), ((), ())), preferred_element_type=F32)


def _dot_hi(a, b, dims=(((1,), (0,)), ((), ()))):
    return lax.dot_general(a, b, dims, preferred_element_type=F32, precision=lax.Precision.HIGHEST)


def _split_bf16(x, terms):
    out = []
    for _ in range(terms - 1):
        t = x.astype(BF16)
        out.append(t)
        x = x - t.astype(F32)
    out.append(x.astype(BF16))
    return out


def _dot_split(x, onehots, terms, dims=(((1,), (0,)), ((), ()))):
    acc = None
    for t in _split_bf16(x, terms):
        p = lax.dot_general(t, onehots, dims, preferred_element_type=F32)
        acc = p if acc is None else acc + p
    return acc


def _dot_split_rhs(onehots, x, terms, dims=(((1,), (0,)), ((), ()))):
    acc = None
    for t in _split_bf16(x, terms):
        p = lax.dot_general(onehots, t, dims, preferred_element_type=F32)
        acc = p if acc is None else acc + p
    return acc


def _sigmoid(x):
    return 0.5 * jnp.tanh(0.5 * x) + 0.5


def _softplus(x):
    return jnp.maximum(x, 0.0) + jnp.log(1.0 + jnp.exp(-jnp.abs(x)))


_GELU_C = math.sqrt(2.0 / math.pi)


def _gelu(x):
    return 0.5 * x * (1.0 + jnp.tanh(_GELU_C * (x + 0.044715 * x * x * x)))


def _gelu_grad(x):
    t = jnp.tanh(_GELU_C * (x + 0.044715 * x * x * x))
    return 0.5 * (1.0 + t) + 0.5 * x * (1.0 - t * t) * _GELU_C * (1.0 + 3.0 * 0.044715 * x * x)


def _rms(x):
    r = lax.rsqrt(jnp.mean(x * x, axis=-1, keepdims=True) + EPS)
    return x * r, r


def _rms_bwd(xn, r, dxn):
    return r * (dxn - xn * jnp.mean(dxn * xn, axis=-1, keepdims=True))


def _row_spec(tm, width, col=0):
    return pl.BlockSpec((tm, width), lambda i: (i, col))


def _const_spec(shape):
    nd = len(shape)
    return pl.BlockSpec(shape, lambda i: (0,) * nd)


def _hbm_spec():
    return pl.BlockSpec(memory_space=pl.ANY)


def _load_late_weight(wa_hbm, dst_ref, name):
    r0, nr = next((r0, nr) for n, r0, nr in ROWS_A if n == name)
    for s in range(N_CHIPS):
        dst = dst_ref.at[s] if len(dst_ref.shape) == 3 else dst_ref.at[pl.ds(nr * s, nr), :]
        pltpu.sync_copy(wa_hbm.at[s, pl.ds(r0, nr), :], dst)


def _inproj_fwd(x, g, wp):
    T = x.shape[0]
    tm = TOKEN_TILE

    def body(x_ref, g_ref, w_hbm, z_ref, xbc_ref, u5_ref, gt_ref, dt_ref, h_ref, w_ref):
        @pl.when(pl.program_id(0) == 0)
        def _():
            pltpu.sync_copy(w_hbm, w_ref)

        xn, _ = _rms(x_ref[...])
        h = (xn * g_ref[...]).astype(BF16)
        h_ref[...] = h
        z_ref[...] = _dot(h, w_ref[:, P_Z:P_XBC])
        xbc_ref[...] = _dot(h, w_ref[:, P_XBC:P_U5])
        u5_ref[...] = _dot(h, w_ref[:, P_U5:P_G])
        tail = _dot(h, w_ref[:, P_G:P_END])
        gt_ref[...] = tail[:, :P_DT - P_G]
        dt_ref[...] = tail[:, P_DT - P_G:]

    widths = (1024, 2048, 512, 2048, DT_PAD)
    return _pc(
        body, name="inproj_fwd", grid=(T // tm,),
        in_specs=[_row_spec(tm, D_MODEL), _const_spec((1, D_MODEL)), _hbm_spec()],
        out_specs=[_row_spec(tm, w) for w in widths] + [_row_spec(tm, D_MODEL)],
        out_shape=[jax.ShapeDtypeStruct((T, w), F32) for w in widths] + [jax.ShapeDtypeStruct((T, D_MODEL), BF16)],
        scratch_shapes=[pltpu.VMEM((D_MODEL, P_END), BF16)],
        compiler_params=_cparams(("arbitrary",)),
    )(x, g, wp)


def _inproj_bwd(x, dx1, dz, dxbc, du5, dgt, ddt, g, wp, rider=None):
    T = x.shape[0]
    tm = TOKEN_TILE

    def body(x_ref, dx1_ref, dz_ref, dxbc_ref, du5_ref, dgt_ref, ddt_ref, g_ref, w_hbm, dx_ref, dg_ref, w_ref):
        @pl.when(pl.program_id(0) == 0)
        def _():
            pltpu.sync_copy(w_hbm, w_ref)
            dg_ref[...] = jnp.zeros_like(dg_ref)

        xn, r = _rms(x_ref[...])
        gv = g_ref[...]
        dh = _dot_nt(dz_ref[...].astype(BF16), w_ref[:, P_Z:P_XBC])
        dh += _dot_nt(dxbc_ref[...].astype(BF16), w_ref[:, P_XBC:P_U5])
        dh += _dot_nt(du5_ref[...].astype(BF16), w_ref[:, P_U5:P_G])
        dh += _dot_nt(dgt_ref[...].astype(BF16), w_ref[:, P_G:P_DT])
        dh += _dot_nt(ddt_ref[...].astype(BF16), w_ref[:, P_DT:P_END])
        dg_ref[...] += jnp.sum(dh * xn, axis=0, keepdims=True)
        dx_ref[...] = dx1_ref[...] + _rms_bwd(xn, r, dh * gv)

    return _call(
        body, rider, name="inproj_bwd", grid=(T // tm,),
        in_specs=[_row_spec(tm, 1024), _row_spec(tm, 1024), _row_spec(tm, 1024), _row_spec(tm, 2048),
                  _row_spec(tm, 512), _row_spec(tm, 2048), _row_spec(tm, DT_PAD), _const_spec((1, 1024)), _hbm_spec()],
        out_specs=[_row_spec(tm, 1024), _const_spec((1, 1024))],
        out_shape=[jax.ShapeDtypeStruct((T, 1024), F32), jax.ShapeDtypeStruct((1, 1024), F32)],
        scratch_shapes=[pltpu.VMEM((D_MODEL, P_END), BF16)],
        compiler_params=_cparams(("arbitrary",)),
    )(x, dx1, dz, dxbc, du5, dgt, ddt, g, wp)


def _conv_fwd(xbc_raw, dt_raw, conv_w, conv_b, dt_bias):
    T = xbc_raw.shape[0]
    tm = TOKEN_TILE

    def body(u_ref, dtr_ref, w_ref, b_ref, db_ref, act_ref, dt_ref, ext_ref):
        @pl.when(pl.program_id(0) == 0)
        def _():
            ext_ref[0:HALO, :] = jnp.zeros((HALO, CONV_DIM), F32)

        ext_ref[HALO:, :] = u_ref[...]
        for c0 in range(0, CONV_DIM, CONV_COLS):
            cols = slice(c0, c0 + CONV_COLS)
            taps = [w_ref[k:k + 1, cols] for k in range(CONV_K)]
            bias = b_ref[:, cols]
            for r0 in range(0, tm, CONV_ROWS_BLK):
                y = bias + taps[0] * ext_ref[pl.ds(HALO - (CONV_K - 1) + r0, CONV_ROWS_BLK), cols]
                for k in range(1, CONV_K):
                    y += taps[k] * ext_ref[pl.ds(HALO - (CONV_K - 1) + k + r0, CONV_ROWS_BLK), cols]
                act_ref[r0:r0 + CONV_ROWS_BLK, cols] = y * _sigmoid(y)
        ext_ref[0:HALO, :] = u_ref[tm - HALO:tm, :]
        dt_ref[...] = _softplus(dtr_ref[...] + db_ref[...])

    return _pc(
        body, name="conv_fwd", grid=(T // tm,),
        in_specs=[_row_spec(tm, CONV_DIM), _row_spec(tm, DT_PAD), _const_spec((CONV_K, CONV_DIM)),
                  _const_spec((1, CONV_DIM)), _const_spec((1, DT_PAD))],
        out_specs=[_row_spec(tm, CONV_DIM), _row_spec(tm, DT_PAD)],
        out_shape=[jax.ShapeDtypeStruct((T, CONV_DIM), F32), jax.ShapeDtypeStruct((T, DT_PAD), F32)],
        scratch_shapes=[pltpu.VMEM((tm + HALO, CONV_DIM), F32)],
        compiler_params=_cparams(("arbitrary",)),
    )(xbc_raw, dt_raw, conv_w, conv_b, dt_bias)


def _conv_bwd(xbc_raw, dt_raw, dxs_a, dxs_b, dB, dC, ddt, conv_w, conv_b, dt_bias):
    T = xbc_raw.shape[0]
    tm = TOKEN_TILE
    n = T // tm
    hb = tm // HALO

    def rev(width):
        return pl.BlockSpec((tm, width), lambda i: (n - 1 - i, 0))

    def body(u_ref, up_ref, dtr_ref, dxa_ref, dxb_ref, dB_ref, dC_ref, ddt_ref, w_ref, b_ref, db_ref,
             du_ref, ddtr_ref, dw_ref, dcb_ref, ddb_ref, ext_ref, dye_ref):
        i = pl.program_id(0)

        @pl.when(i == 0)
        def _():
            dye_ref[tm:, :] = jnp.zeros((HALO, CONV_DIM), F32)
            dw_ref[...] = jnp.zeros_like(dw_ref)
            dcb_ref[...] = jnp.zeros_like(dcb_ref)
            ddb_ref[...] = jnp.zeros_like(ddb_ref)

        first = (i == n - 1).astype(F32)
        ext_ref[0:HALO, :] = up_ref[...] * (1.0 - first)
        ext_ref[HALO:, :] = u_ref[...]
        for c0 in range(0, CONV_DIM, CONV_COLS):
            cols = slice(c0, c0 + CONV_COLS)
            taps = [w_ref[k:k + 1, cols] for k in range(CONV_K)]
            bias = b_ref[:, cols]
            acc_b = jnp.zeros((HALO, CONV_COLS), F32)
            acc_w = [jnp.zeros((HALO, CONV_COLS), F32) for _ in range(CONV_K)]
            for r0 in range(0, tm, CONV_ROWS_BLK):
                rows = slice(r0, r0 + CONV_ROWS_BLK)
                us = [ext_ref[pl.ds(HALO - (CONV_K - 1) + k + r0, CONV_ROWS_BLK), cols] for k in range(CONV_K)]
                y = bias + taps[0] * us[0]
                for k in range(1, CONV_K):
                    y += taps[k] * us[k]
                s = _sigmoid(y)
                if c0 < SSD_INNER:
                    dact = dxa_ref[rows, cols] + dxb_ref[rows, cols]
                elif c0 < SSD_INNER + 512:
                    dact = dB_ref[rows, c0 - SSD_INNER:c0 - SSD_INNER + CONV_COLS]
                else:
                    dact = dC_ref[rows, c0 - SSD_INNER - 512:c0 - SSD_INNER - 512 + CONV_COLS]
                dy = dact * (s * (1.0 + y * (1.0 - s)))
                dye_ref[rows, cols] = dy
                acc_b += jnp.sum(dy.reshape(CONV_ROWS_BLK // HALO, HALO, CONV_COLS), axis=0)
                for k in range(CONV_K):
                    acc_w[k] += jnp.sum((dy * us[k]).reshape(CONV_ROWS_BLK // HALO, HALO, CONV_COLS), axis=0)
            dcb_ref[:, cols] += jnp.sum(acc_b, axis=0, keepdims=True)
            for k in range(CONV_K):
                dw_ref[k:k + 1, cols] += jnp.sum(acc_w[k], axis=0, keepdims=True)
        for c0 in range(0, CONV_DIM, CONV_COLS):
            cols = slice(c0, c0 + CONV_COLS)
            taps = [w_ref[k:k + 1, cols] for k in range(CONV_K)]
            for r0 in range(0, tm, CONV_ROWS_BLK):
                du = taps[0] * dye_ref[pl.ds(CONV_K - 1 + r0, CONV_ROWS_BLK), cols]
                for k in range(1, CONV_K):
                    du += taps[k] * dye_ref[pl.ds(CONV_K - 1 - k + r0, CONV_ROWS_BLK), cols]
                du_ref[r0:r0 + CONV_ROWS_BLK, cols] = du.astype(BF16)
        dye_ref[tm:, :] = dye_ref[0:HALO, :]
        sg = _sigmoid(dtr_ref[...] + db_ref[...])
        ddtr = ddt_ref[...] * sg
        ddtr_ref[...] = ddtr.astype(BF16)
        ddb_ref[...] += jnp.sum(ddtr, axis=0, keepdims=True)

    prev_spec = pl.BlockSpec((HALO, CONV_DIM), lambda i: (jnp.maximum((n - 1 - i) * hb - 1, 0), 0))
    return _pc(
        body, name="conv_bwd", grid=(n,),
        in_specs=[rev(CONV_DIM), prev_spec, rev(DT_PAD), rev(1024), rev(1024), rev(512), rev(512), rev(DT_PAD),
                  _const_spec((CONV_K, CONV_DIM)), _const_spec((1, CONV_DIM)), _const_spec((1, DT_PAD))],
        out_specs=[rev(CONV_DIM), rev(DT_PAD), _const_spec((HALO, CONV_DIM)), _const_spec((1, CONV_DIM)),
                   _const_spec((1, DT_PAD))],
        out_shape=[jax.ShapeDtypeStruct((T, CONV_DIM), BF16), jax.ShapeDtypeStruct((T, DT_PAD), BF16),
                   jax.ShapeDtypeStruct((HALO, CONV_DIM), F32), jax.ShapeDtypeStruct((1, CONV_DIM), F32),
                   jax.ShapeDtypeStruct((1, DT_PAD), F32)],
        scratch_shapes=[pltpu.VMEM((tm + HALO, CONV_DIM), F32), pltpu.VMEM((tm + HALO, CONV_DIM), F32)],
        compiler_params=_cparams(("arbitrary",)),
    )(xbc_raw, xbc_raw, dt_raw, dxs_a, dxs_b, dB, dC, ddt, conv_w, conv_b, dt_bias)


GROUP_LANES = SSD_HPG * SSD_HEADDIM


def _ssd_expanders():
    head = jnp.arange(DT_PAD)[:, None]
    to_wide = (jnp.arange(SSD_INNER)[None, :] // SSD_HEADDIM == head).astype(BF16)
    return to_wide, to_wide.T


def _ssd_prep(dt_ref, alog_ref, wide_ref):
    q = SSD_CHUNK
    a = -jnp.exp(alog_ref[...])
    dtv = dt_ref[...]
    la = dtv * a
    row = lax.broadcasted_iota(jnp.int32, (q, q), 0)
    col = lax.broadcasted_iota(jnp.int32, (q, q), 1)
    tri = (col <= row).astype(BF16)
    cum = _dot_split_rhs(tri, la, 3)
    cum_t = _dot_split(la, tri, 3, (((0,), (1,)), ((), ())))
    dtw = _dot_split(dtv, wide_ref[...], 2)
    cumw = _dot_split(cum, wide_ref[...], 3)
    return a, dtv, row, col, tri, cum_t, dtw, cumw, cum


def _decay(cum, cum_t, h, keep):
    return jnp.where(keep, jnp.exp(jnp.minimum(cum[:, h:h + 1] - cum_t[h:h + 1, :], 0.0)), 0.0)


def _decay_t(cum, cum_t, h, keep_t):
    return jnp.where(keep_t, jnp.exp(jnp.minimum(cum_t[h:h + 1, :] - cum[:, h:h + 1], 0.0)), 0.0)


def _ssd_fwd(xbc_act, dt, alog):
    T = xbc_act.shape[0]
    q = SSD_CHUNK
    nc = T // q
    to_wide, _ = _ssd_expanders()

    def body(xbc_ref, dt_ref, alog_ref, wide_ref, y_ref, sp_ref, st_ref, xd_ref, xde_ref):
        @pl.when(pl.program_id(0) == 0)
        def _():
            st_ref[...] = jnp.zeros_like(st_ref)

        a, dtv, row, col, tri, cum_t, dtw, cumw, segcol = _ssd_prep(dt_ref, alog_ref, wide_ref)
        clw = cumw[q - 1:q, :]
        ecw = jnp.exp(cumw)
        xd = xbc_ref[:, 0:SSD_INNER] * dtw
        xd_ref[...] = xd.astype(BF16)
        xde_ref[...] = (xd * jnp.exp(clw - cumw)).astype(BF16)
        cdw = jnp.exp(clw)
        keep = col <= row
        sp_ref[0] = st_ref[...]
        for g in range(SSD_GROUPS):
            gl = slice(GROUP_LANES * g, GROUP_LANES * (g + 1))
            bb = xbc_ref[:, 1024 + 128 * g:1152 + 128 * g].astype(BF16)
            cb = xbc_ref[:, 1536 + 128 * g:1664 + 128 * g].astype(BF16)
            gm = _dot_nt(cb, bb)
            stp = st_ref[g]
            yoff = _dot(cb, stp.astype(BF16)) * ecw[:, gl]
            for r in range(SSD_HPG):
                h = SSD_HPG * g + r
                m = (gm * _decay(segcol, cum_t, h, keep)).astype(BF16)
                y_ref[:, 64 * h:64 * h + 64] = _dot(m, xd_ref[:, 64 * h:64 * h + 64]) + yoff[:, 64 * r:64 * r + 64]
            st_ref[g] = stp * cdw[:, gl] + _dot_tn(bb, xde_ref[:, gl])

    return _pc(
        body, name="ssd_fwd", grid=(nc,),
        in_specs=[_row_spec(q, CONV_DIM), _row_spec(q, DT_PAD), _const_spec((1, DT_PAD)),
                  _const_spec(to_wide.shape)],
        out_specs=[_row_spec(q, SSD_INNER),
                   pl.BlockSpec((1, SSD_GROUPS, SSD_STATE, GROUP_LANES), lambda i: (i, 0, 0, 0))],
        out_shape=[jax.ShapeDtypeStruct((T, SSD_INNER), F32),
                   jax.ShapeDtypeStruct((nc, SSD_GROUPS, SSD_STATE, GROUP_LANES), F32)],
        scratch_shapes=[pltpu.VMEM((SSD_GROUPS, SSD_STATE, GROUP_LANES), F32), pltpu.VMEM((q, SSD_INNER), BF16),
                        pltpu.VMEM((q, SSD_INNER), BF16)],
        compiler_params=_cparams(("arbitrary",)),
    )(xbc_act, dt, alog, to_wide)


def _ssd_bwd(xbc_act, dt, alog, sprev, dy):
    T = xbc_act.shape[0]
    q = SSD_CHUNK
    nc = T // q
    to_wide, to_heads = _ssd_expanders()

    def rev(width):
        return pl.BlockSpec((q, width), lambda i: (nc - 1 - i, 0))

    def body(xbc_ref, dt_ref, alog_ref, sp_ref, dy_ref, wide_ref, heads_ref,
             dxs_ref, dB_ref, dC_ref, ddt_ref, dalog_ref, ds_ref, xd_ref, dxd_ref):
        i = pl.program_id(0)

        @pl.when(i == 0)
        def _():
            ds_ref[...] = jnp.zeros_like(ds_ref)
            dalog_ref[...] = jnp.zeros_like(dalog_ref)

        a, dtv, row, col, tri, cum_t, dtw, cumw, segcol = _ssd_prep(dt_ref, alog_ref, wide_ref)
        clw = cumw[q - 1:q, :]
        ecw = jnp.exp(cumw)
        dew = jnp.exp(clw - cumw)
        cdw = jnp.exp(clw)
        xs = xbc_ref[:, 0:SSD_INNER]
        xd = xs * dtw
        xd_ref[...] = xd.astype(BF16)
        dyv = dy_ref[...]
        dye = (dyv * ecw).astype(BF16)
        xde = (xd * dew).astype(BF16)
        keep = col <= row
        keep_t = col >= row
        rows_k = lax.broadcasted_iota(jnp.int32, (SSD_HPG * q, DT_PAD), 0) // q
        lanes_k = lax.broadcasted_iota(jnp.int32, (SSD_HPG * q, DT_PAD), 1)
        dcw_parts = []
        dcum = jnp.zeros((q, DT_PAD), F32)
        for g in range(SSD_GROUPS):
            gl = slice(GROUP_LANES * g, GROUP_LANES * (g + 1))
            bb = xbc_ref[:, 1024 + 128 * g:1152 + 128 * g].astype(BF16)
            cb = xbc_ref[:, 1536 + 128 * g:1664 + 128 * g].astype(BF16)
            gm = _dot_nt(cb, bb)
            gmt = _dot_nt(bb, cb)
            stp = sp_ref[0, g]
            dst = ds_ref[g]
            stpb = stp.astype(BF16)
            dstb = dst.astype(BF16)
            yoff = _dot(cb, stpb) * ecw[:, gl]
            dcg = _dot_nt(dye[:, gl], stpb)
            ds_ref[g] = dst * cdw[:, gl] + _dot_tn(cb, dye[:, gl])
            dlast = jnp.sum(dst * stp, axis=0, keepdims=True) * cdw[:, gl]
            dbg = _dot_nt(xde[:, gl], dstb)
            w = _dot(bb, dstb) * dew[:, gl]
            wx = w * xd[:, gl]
            dlast = dlast + jnp.sum(wx, axis=0, keepdims=True)
            dcw_parts.append(dyv[:, gl] * yoff - wx
                             + jnp.where(lax.broadcasted_iota(jnp.int32, (q, 1), 0) == q - 1, dlast, 0.0))
            dgm = jnp.zeros((q, q), F32)
            diag = []
            for r in range(SSD_HPG):
                h = SSD_HPG * g + r
                hl = slice(64 * h, 64 * h + 64)
                dyb = dy_ref[:, hl].astype(BF16)
                xdh = xd_ref[:, hl]
                dm = _dot_nt(dyb, xdh)
                dmt = _dot_nt(xdh, dyb)
                dec = _decay(segcol, cum_t, h, keep)
                mt = gmt * _decay_t(segcol, cum_t, h, keep_t)
                dgm += dm * dec
                diag.append(dm * (gm * dec) - dmt * mt)
                dxd_ref[:, hl] = _dot(mt.astype(BF16), dyb) + w[:, 64 * r:64 * r + 64]
            onehots = (lanes_k == SSD_HPG * g + rows_k).astype(BF16)
            dcum += _dot_split(jnp.concatenate(diag, axis=1), onehots, 2)
            dgb = dgm.astype(BF16)
            dC_ref[:, 128 * g:128 * g + 128] = dcg + _dot(dgb, bb)
            dB_ref[:, 128 * g:128 * g + 128] = dbg + _dot_tn(dgb, cb)
        dxd = dxd_ref[...]
        dxs_ref[...] = dxd * dtw
        dcum += _dot_split(jnp.concatenate(dcw_parts, axis=1), heads_ref[...], 2)
        dla = _dot_split_rhs(tri, dcum, 3, (((0,), (0,)), ((), ())))
        ddt_ref[...] = _dot_split(xs * dxd, heads_ref[...], 2) + dla * a
        dalog_ref[...] += jnp.sum(dla * dtv, axis=0, keepdims=True)

        @pl.when(i == nc - 1)
        def _():
            dalog_ref[...] = dalog_ref[...] * a

    st_spec = pl.BlockSpec((1, SSD_GROUPS, SSD_STATE, GROUP_LANES), lambda i: (nc - 1 - i, 0, 0, 0))
    return _pc(
        body, name="ssd_bwd", grid=(nc,),
        in_specs=[rev(CONV_DIM), rev(DT_PAD), _const_spec((1, DT_PAD)), st_spec, rev(SSD_INNER),
                  _const_spec(to_wide.shape), _const_spec(to_heads.shape)],
        out_specs=[rev(SSD_INNER), rev(512), rev(512), rev(DT_PAD), _const_spec((1, DT_PAD))],
        out_shape=[jax.ShapeDtypeStruct((T, SSD_INNER), F32), jax.ShapeDtypeStruct((T, 512), F32),
                   jax.ShapeDtypeStruct((T, 512), F32), jax.ShapeDtypeStruct((T, DT_PAD), F32),
                   jax.ShapeDtypeStruct((1, DT_PAD), F32)],
        scratch_shapes=[pltpu.VMEM((SSD_GROUPS, SSD_STATE, GROUP_LANES), F32), pltpu.VMEM((q, SSD_INNER), BF16),
                        pltpu.VMEM((q, SSD_INNER), F32)],
        compiler_params=_cparams(("arbitrary",)),
    )(xbc_act, dt, alog, sprev, dy, to_wide, to_heads)


def _s5_disc_vals(a_re, a_im, log_dt, b_re, b_im):
    dt = jnp.exp(log_dt)
    mag = jnp.exp(a_re * dt)
    ab_re = mag * jnp.cos(a_im * dt)
    ab_im = mag * jnp.sin(a_im * dt)
    den = a_re * a_re + a_im * a_im
    nr = ab_re - 1.0
    ni = ab_im
    coef_re = (nr * a_re + ni * a_im) / den
    coef_im = (ni * a_re - nr * a_im) / den
    bb_re = coef_re * b_re - coef_im * b_im
    bb_im = coef_re * b_im + coef_im * b_re
    return ab_re, ab_im, bb_re, bb_im


def _s5_disc(a_re, a_im, log_dt, b_re, b_im):
    def body(ar, ai, ld, br, bi, o1, o2, o3, o4):
        o1[...], o2[...], o3[...], o4[...] = _s5_disc_vals(ar[...], ai[...], ld[...], br[...], bi[...])

    return _pc(
        body, name="s5_disc",
        out_shape=[jax.ShapeDtypeStruct((1, S5_STATES), F32), jax.ShapeDtypeStruct((1, S5_STATES), F32),
                   jax.ShapeDtypeStruct((16, S5_STATES), F32), jax.ShapeDtypeStruct((16, S5_STATES), F32)],
    )(a_re, a_im, log_dt, b_re, b_im)


def _s5_disc_bwd(a_re, a_im, log_dt, b_re, b_im, d_ab_re, d_ab_im, d_bb_re, d_bb_im):
    def body(ar, ai, ld, br, bi, g1, g2, g3, g4, o1, o2, o3, o4, o5):
        _, vjp = jax.vjp(_s5_disc_vals, ar[...], ai[...], ld[...], br[...], bi[...])
        d1, d2, d3, d4, d5 = vjp((g1[...], g2[...], g3[...], g4[...]))
        o1[...] = d1
        o2[...] = d2
        st = lax.broadcasted_iota(jnp.int32, (S5_STATES, DT_PAD), 0)
        grp = lax.broadcasted_iota(jnp.int32, (S5_STATES, DT_PAD), 1)
        sel = (st // 64 == grp).astype(F32)
        o3[...] = _dot_hi(d3, sel)
        o4[...] = d4
        o5[...] = d5

    return _pc(
        body, name="s5_disc_bwd",
        out_shape=[jax.ShapeDtypeStruct((1, S5_STATES), F32), jax.ShapeDtypeStruct((1, S5_STATES), F32),
                   jax.ShapeDtypeStruct((1, DT_PAD), F32),
                   jax.ShapeDtypeStruct((16, S5_STATES), F32), jax.ShapeDtypeStruct((16, S5_STATES), F32)],
    )(a_re, a_im, log_dt, b_re, b_im, d_ab_re, d_ab_im, d_bb_re, d_bb_im)


def _cmul_add(xr, xi, pr, pi, yr, yi):
    return xr + pr * yr - pi * yi, xi + pr * yi + pi * yr


def _powers(ar, ai, n):
    out = [(ar, ai)]
    for _ in range(n - 1):
        pr, pi = out[-1]
        out.append((pr * pr - pi * pi, 2.0 * pr * pi))
    return out


_BW = S5_STATES // S5_BLOCKS
_BI = S5_WIDTH // S5_BLOCKS
SUB = 8
S5_ROWS = S5_CHUNK // SUB


S5_TAB_ROWS = 8 * SUB


def _scan8(br, bi, tab_ref, reverse):
    for level, k in enumerate((1, 2, 4)):
        r0 = 2 * SUB * (level + 1)
        shift = SUB - k if reverse else k
        br, bi = _cmul_add(br, bi, tab_ref[r0:r0 + SUB, :], tab_ref[r0 + SUB:r0 + 2 * SUB, :],
                           pltpu.roll(br, shift, 0), pltpu.roll(bi, shift, 0))
    return br, bi


def _s5_tables(ab_ref, tab_ref, reverse):
    rowin = lax.broadcasted_iota(jnp.int32, (SUB, 1), 0)
    ar = ab_ref[0:1, :]
    ai = -ab_ref[1:2, :] if reverse else ab_ref[1:2, :]
    zero = jnp.zeros((SUB, S5_STATES), F32)
    for level, (pr, pi) in enumerate(_powers(ar, ai, 3)):
        k = 2 ** level
        keep = (rowin < SUB - k) if reverse else (rowin >= k)
        r0 = 2 * SUB * (level + 1)
        tab_ref[r0:r0 + SUB, :] = jnp.where(keep, pr, 0.0) + zero
        tab_ref[r0 + SUB:r0 + 2 * SUB, :] = jnp.where(keep, pi, 0.0) + zero
    hit = rowin == (SUB - 1 if reverse else 0)
    pr, pi = _scan8(jnp.where(hit, ar, 0.0) + zero, jnp.where(hit, ai, 0.0) + zero, tab_ref, reverse)
    tab_ref[0:SUB, :] = pr
    tab_ref[SUB:2 * SUB, :] = pi


def _s5_fwd(u5, wb4, wc4, ab, dvec, rider=None):
    T = u5.shape[0]
    q = S5_CHUNK
    nc = T // q

    def body(u_ref, wb_ref, wc_ref, ab_ref, d_ref, y_ref, sp_ref, carry_ref, tab_ref, sr_ref, si_ref):
        i = pl.program_id(0)
        rowin = lax.broadcasted_iota(jnp.int32, (SUB, 1), 0)

        @pl.when(i == 0)
        def _():
            carry_ref[...] = jnp.zeros_like(carry_ref)
            _s5_tables(ab_ref, tab_ref, False)

        sp_ref[0] = carry_ref[...]
        for j in range(S5_BLOCKS):
            bu = _dot(u_ref[:, _BI * j:_BI * (j + 1)].astype(BF16), wb_ref[j])
            sr_ref[:, :, _BW * j:_BW * (j + 1)] = bu[:, :_BW].reshape(S5_ROWS, SUB, _BW)
            si_ref[:, :, _BW * j:_BW * (j + 1)] = bu[:, _BW:].reshape(S5_ROWS, SUB, _BW)
        tr, ti = tab_ref[0:SUB, :], tab_ref[SUB:2 * SUB, :]
        cr, ci = carry_ref[0:1, :], carry_ref[1:2, :]
        for k in range(S5_ROWS):
            sr, si = _scan8(sr_ref[k], si_ref[k], tab_ref, False)
            sr, si = _cmul_add(sr, si, tr, ti, cr, ci)
            sr_ref[k] = sr
            si_ref[k] = si
            cr, ci = sr[SUB - 1:SUB, :], si[SUB - 1:SUB, :]
        carry_ref[0:1, :] = cr
        carry_ref[1:2, :] = ci
        for j in range(S5_BLOCKS):
            sl = slice(_BW * j, _BW * (j + 1))
            ul = slice(_BI * j, _BI * (j + 1))
            s = jnp.concatenate([sr_ref[:, :, sl].reshape(q, _BW), si_ref[:, :, sl].reshape(q, _BW)], axis=1).astype(BF16)
            y_ref[:, ul] = _dot(s, wc_ref[j]) + d_ref[:, ul] * u_ref[:, ul]

    return _call(
        body, rider, name="s5_fwd", grid=(nc,),
        in_specs=[_row_spec(q, S5_WIDTH), _const_spec((S5_BLOCKS, _BI, 2 * _BW)), _const_spec((S5_BLOCKS, 2 * _BW, _BI)),
                  _const_spec((8, S5_STATES)), _const_spec((1, S5_WIDTH))],
        out_specs=[_row_spec(q, S5_WIDTH), pl.BlockSpec((1, 8, S5_STATES), lambda i: (i, 0, 0))],
        out_shape=[jax.ShapeDtypeStruct((T, S5_WIDTH), F32), jax.ShapeDtypeStruct((nc, 8, S5_STATES), F32)],
        scratch_shapes=[pltpu.VMEM((8, S5_STATES), F32), pltpu.VMEM((S5_TAB_ROWS, S5_STATES), F32),
                        pltpu.VMEM((S5_ROWS, SUB, S5_STATES), F32), pltpu.VMEM((S5_ROWS, SUB, S5_STATES), F32)],
        compiler_params=_cparams(("arbitrary",)),
    )(u5, wb4, wc4, ab, dvec)


def _s5_bwd(u5, dy5, wb4, wc4, ab, dvec, sprev, rider=None):
    T = u5.shape[0]
    q = S5_CHUNK
    nc = T // q

    def rev(width):
        return pl.BlockSpec((q, width), lambda i: (nc - 1 - i, 0))

    def body(u_ref, dy_ref, wb_ref, wc_ref, ab_ref, d_ref, sp_ref, du_ref, dwb_ref, dwc_ref, dab_ref, dd_ref,
             carry_ref, tab_ref, rtab_ref, sr_ref, si_ref, lr_ref, li_ref):
        i = pl.program_id(0)
        rowin = lax.broadcasted_iota(jnp.int32, (SUB, 1), 0)

        @pl.when(i == 0)
        def _():
            carry_ref[...] = jnp.zeros_like(carry_ref)
            dwb_ref[...] = jnp.zeros_like(dwb_ref)
            dwc_ref[...] = jnp.zeros_like(dwc_ref)
            dab_ref[...] = jnp.zeros_like(dab_ref)
            dd_ref[...] = jnp.zeros_like(dd_ref)
            _s5_tables(ab_ref, tab_ref, False)
            _s5_tables(ab_ref, rtab_ref, True)

        for j in range(S5_BLOCKS):
            sl = slice(_BW * j, _BW * (j + 1))
            ul = slice(_BI * j, _BI * (j + 1))
            bu = _dot(u_ref[:, ul].astype(BF16), wb_ref[j])
            sr_ref[:, :, sl] = bu[:, :_BW].reshape(S5_ROWS, SUB, _BW)
            si_ref[:, :, sl] = bu[:, _BW:].reshape(S5_ROWS, SUB, _BW)
            ds = _dot_nt(dy_ref[:, ul].astype(BF16), wc_ref[j])
            lr_ref[:, :, sl] = ds[:, :_BW].reshape(S5_ROWS, SUB, _BW)
            li_ref[:, :, sl] = ds[:, _BW:].reshape(S5_ROWS, SUB, _BW)
        ar, ai = ab_ref[0:1, :], ab_ref[1:2, :]
        tr, ti = tab_ref[0:SUB, :], tab_ref[SUB:2 * SUB, :]
        cr, ci = sp_ref[0, 0:1, :], sp_ref[0, 1:2, :]
        for k in range(S5_ROWS):
            sr, si = _scan8(sr_ref[k], si_ref[k], tab_ref, False)
            sr, si = _cmul_add(sr, si, tr, ti, cr, ci)
            sr_ref[k] = sr
            si_ref[k] = si
            cr, ci = sr[SUB - 1:SUB, :], si[SUB - 1:SUB, :]
        tr, ti = rtab_ref[0:SUB, :], rtab_ref[SUB:2 * SUB, :]
        cr, ci = carry_ref[0:1, :], carry_ref[1:2, :]
        acc_r = jnp.zeros((SUB, S5_STATES), F32)
        acc_i = jnp.zeros((SUB, S5_STATES), F32)
        for k in reversed(range(S5_ROWS)):
            lr, li = _scan8(lr_ref[k], li_ref[k], rtab_ref, True)
            lr, li = _cmul_add(lr, li, tr, ti, cr, ci)
            lr_ref[k] = lr
            li_ref[k] = li
            cr, ci = lr[0:1, :], li[0:1, :]
            if k > 0:
                before_r, before_i = sr_ref[k - 1, SUB - 1:SUB, :], si_ref[k - 1, SUB - 1:SUB, :]
            else:
                before_r, before_i = sp_ref[0, 0:1, :], sp_ref[0, 1:2, :]
            keep = rowin >= 1
            pr = jnp.where(keep, pltpu.roll(sr_ref[k], 1, 0), before_r)
            pi = jnp.where(keep, pltpu.roll(si_ref[k], 1, 0), before_i)
            acc_r += lr * pr + li * pi
            acc_i += li * pr - lr * pi
        carry_ref[0:1, :] = cr
        carry_ref[1:2, :] = ci
        dab_ref[0:1, :] += jnp.sum(acc_r, axis=0, keepdims=True)
        dab_ref[1:2, :] += jnp.sum(acc_i, axis=0, keepdims=True)
        for j in range(S5_BLOCKS):
            sl = slice(_BW * j, _BW * (j + 1))
            ul = slice(_BI * j, _BI * (j + 1))
            u = u_ref[:, ul]
            dy = dy_ref[:, ul]
            dyb = dy.astype(BF16)
            lam = jnp.concatenate([lr_ref[:, :, sl].reshape(q, _BW), li_ref[:, :, sl].reshape(q, _BW)], axis=1).astype(BF16)
            s = jnp.concatenate([sr_ref[:, :, sl].reshape(q, _BW), si_ref[:, :, sl].reshape(q, _BW)], axis=1).astype(BF16)
            du_ref[:, ul] = (_dot_nt(lam, wb_ref[j]) + d_ref[:, ul] * dy).astype(BF16)
            dwb_ref[j] += _dot_tn(u.astype(BF16), lam)
            dwc_ref[j] += _dot_tn(s, dyb)
            dd_ref[:, ul] += jnp.sum(dy * u, axis=0, keepdims=True)

    big = pltpu.VMEM((S5_ROWS, SUB, S5_STATES), F32)
    return _call(
        body, rider, name="s5_bwd", grid=(nc,),
        in_specs=[rev(S5_WIDTH), rev(S5_WIDTH), _const_spec((S5_BLOCKS, _BI, 2 * _BW)), _const_spec((S5_BLOCKS, 2 * _BW, _BI)),
                  _const_spec((8, S5_STATES)), _const_spec((1, S5_WIDTH)),
                  pl.BlockSpec((1, 8, S5_STATES), lambda i: (nc - 1 - i, 0, 0))],
        out_specs=[rev(S5_WIDTH), _const_spec((S5_BLOCKS, _BI, 2 * _BW)), _const_spec((S5_BLOCKS, 2 * _BW, _BI)),
                   _const_spec((8, S5_STATES)), _const_spec((1, S5_WIDTH))],
        out_shape=[jax.ShapeDtypeStruct((T, S5_WIDTH), BF16), jax.ShapeDtypeStruct((S5_BLOCKS, _BI, 2 * _BW), F32),
                   jax.ShapeDtypeStruct((S5_BLOCKS, 2 * _BW, _BI), F32), jax.ShapeDtypeStruct((8, S5_STATES), F32),
                   jax.ShapeDtypeStruct((1, S5_WIDTH), F32)],
        scratch_shapes=[pltpu.VMEM((8, S5_STATES), F32), pltpu.VMEM((S5_TAB_ROWS, S5_STATES), F32),
                        pltpu.VMEM((S5_TAB_ROWS, S5_STATES), F32), big, big, big, big],
        compiler_params=_cparams(("arbitrary",)),
    )(u5, dy5, wb4, wc4, ab, dvec, sprev)


def _merge_vals(ys, xs, z, y5, gates, dvec, gssd, glu_w, glu_b, wbr):
    sz = _sigmoid(z)
    qv = ys + dvec * xs
    pre = qv * (z * sz)
    yn, rs = [], []
    for gi in range(SSD_GROUPS):
        p, r = _rms(pre[:, 256 * gi:256 * (gi + 1)])
        yn.append(p)
        rs.append(r)
    yn = jnp.concatenate(yn, axis=1)
    ya = yn * gssd
    gel = _gelu(y5)
    sg = _sigmoid(_dot(gel.astype(BF16), glu_w) + glu_b)
    yb = gel * sg
    pa = _dot(ya.astype(BF16), wbr[0:SSD_INNER, :])
    pb = _dot(yb.astype(BF16), wbr[SSD_INNER:, :])
    s0 = _sigmoid(gates[:, :D_MODEL])
    s1 = _sigmoid(gates[:, D_MODEL:])
    merged = s0 * pa + s1 * pb
    return dict(sz=sz, qv=qv, yn=yn, rs=rs, ya=ya, gel=gel, sg=sg, yb=yb, pa=pa, pb=pb, s0=s0, s1=s1, merged=merged)


def _merge_specs(tm):
    acts = [_row_spec(tm, 1024), _row_spec(tm, 1024, 0), _row_spec(tm, 1024), _row_spec(tm, 512), _row_spec(tm, 2048),
            _row_spec(tm, 1024)]
    params = [_const_spec((1, 1024)), _const_spec((1, 1024)), _const_spec((512, 512)), _const_spec((1, 512)), _hbm_spec()]
    return acts, params


def _merge_fwd(ys, xbc_act, z, y5, gates, x, dvec, gssd, glu_w, glu_b, wa):
    T = x.shape[0]
    tm = TOKEN_TILE
    acts, params = _merge_specs(tm)

    def body(ys_ref, xs_ref, z_ref, y5_ref, gt_ref, x_ref, dv_ref, gs_ref, gw_ref, gb_ref, wa_hbm, x1_ref,
             wbr_ref, wout_ref):
        @pl.when(pl.program_id(0) == 0)
        def _():
            _load_late_weight(wa_hbm, wbr_ref, "w_branch")
            _load_late_weight(wa_hbm, wout_ref, "w_out")

        v = _merge_vals(ys_ref[...], xs_ref[...], z_ref[...], y5_ref[...], gt_ref[...], dv_ref[...], gs_ref[...],
                        gw_ref[...], gb_ref[...], wbr_ref)
        x1_ref[...] = x_ref[...] + _dot(v["merged"].astype(BF16), wout_ref[...])

    return _pc(
        body, name="merge_fwd", grid=(T // tm,),
        in_specs=acts + params, out_specs=_row_spec(tm, 1024),
        out_shape=jax.ShapeDtypeStruct((T, 1024), F32),
        scratch_shapes=[pltpu.VMEM((1536, 1024), BF16), pltpu.VMEM((1024, 1024), BF16)],
        compiler_params=_cparams(("arbitrary",)),
    )(ys, xbc_act, z, y5, gates, x, dvec, gssd, glu_w, glu_b, wa)


def _merge_bwd(ys, xbc_act, z, y5, gates, dx1, dvec, gssd, glu_w, glu_b, wa, head_sel, rider=None):
    T = dx1.shape[0]
    tm = TOKEN_TILE
    acts, params = _merge_specs(tm)

    def body(ys_ref, xs_ref, z_ref, y5_ref, gt_ref, dx1_ref, dv_ref, gs_ref, gw_ref, gb_ref, wa_hbm, hs_ref,
             dys_ref, dxs_ref, dz_ref, dy5_ref, dgt_ref, mg_ref, ya_ref, yb_ref, dpa_ref, dpb_ref, gel_ref, dpre_ref,
             ddv_ref, dgs_ref, dgb_ref, wbr_ref, wout_ref, ddacc_ref):
        i = pl.program_id(0)

        @pl.when(i == 0)
        def _():
            _load_late_weight(wa_hbm, wbr_ref, "w_branch")
            _load_late_weight(wa_hbm, wout_ref, "w_out")
            ddacc_ref[...] = jnp.zeros_like(ddacc_ref)
            dgs_ref[...] = jnp.zeros_like(dgs_ref)
            dgb_ref[...] = jnp.zeros_like(dgb_ref)

        ys, xs, z, y5, gates = ys_ref[...], xs_ref[...], z_ref[...], y5_ref[...], gt_ref[...]
        dvv, gsv, gw = dv_ref[...], gs_ref[...], gw_ref[...]
        v = _merge_vals(ys, xs, z, y5, gates, dvv, gsv, gw, gb_ref[...], wbr_ref)
        dmg = _dot_nt(dx1_ref[...].astype(BF16), wout_ref[...])
        s0, s1, pa, pb = v["s0"], v["s1"], v["pa"], v["pb"]
        dgt_ref[:, :D_MODEL] = (dmg * pa * s0 * (1.0 - s0)).astype(BF16)
        dgt_ref[:, D_MODEL:] = (dmg * pb * s1 * (1.0 - s1)).astype(BF16)
        dpa = (dmg * s0).astype(BF16)
        dpb = (dmg * s1).astype(BF16)
        dya = _dot_nt(dpa, wbr_ref[0:SSD_INNER, :])
        dyb = _dot_nt(dpb, wbr_ref[SSD_INNER:, :])
        gel, sg = v["gel"], v["sg"]
        dpre = (dyb * gel * sg * (1.0 - sg))
        dgb_ref[...] += jnp.sum(dpre, axis=0, keepdims=True)
        dpre_b = dpre.astype(BF16)
        dgel = dyb * sg + _dot_nt(dpre_b, gw)
        dy5_ref[...] = dgel * _gelu_grad(y5)
        yn = v["yn"]
        dgs_ref[...] += jnp.sum(dya * yn, axis=0, keepdims=True)
        dyn = dya * gsv
        dpre_a = jnp.concatenate(
            [_rms_bwd(yn[:, 256 * gi:256 * (gi + 1)], v["rs"][gi], dyn[:, 256 * gi:256 * (gi + 1)])
             for gi in range(SSD_GROUPS)], axis=1)
        sz, qv = v["sz"], v["qv"]
        dq = dpre_a * (z * sz)
        dz_ref[...] = (dpre_a * qv * (sz * (1.0 + z * (1.0 - sz)))).astype(BF16)
        dys_ref[...] = dq
        dxs_ref[...] = dq * dvv
        ddacc_ref[...] += jnp.sum(dq * xs, axis=0, keepdims=True)
        mg_ref[...] = v["merged"].astype(BF16)
        ya_ref[...] = v["ya"].astype(BF16)
        yb_ref[...] = v["yb"].astype(BF16)
        dpa_ref[...] = dpa
        dpb_ref[...] = dpb
        gel_ref[...] = gel.astype(BF16)
        dpre_ref[...] = dpre_b

        @pl.when(i == pl.num_programs(0) - 1)
        def _():
            ddv_ref[...] = _dot_hi(ddacc_ref[...], hs_ref[...])

    outs = [(1024, F32), (1024, F32), (1024, BF16), (512, F32), (2048, BF16),
            (1024, BF16), (1024, BF16), (512, BF16), (1024, BF16), (1024, BF16), (512, BF16), (512, BF16)]
    return _call(
        body, rider, name="merge_bwd", grid=(T // tm,),
        in_specs=acts + params + [_const_spec((1024, DT_PAD))],
        out_specs=[_row_spec(tm, w) for w, _ in outs] + [_const_spec((1, DT_PAD)), _const_spec((1, 1024)), _const_spec((1, 512))],
        out_shape=[jax.ShapeDtypeStruct((T, w), d) for w, d in outs] + [
            jax.ShapeDtypeStruct((1, DT_PAD), F32), jax.ShapeDtypeStruct((1, 1024), F32), jax.ShapeDtypeStruct((1, 512), F32)],
        scratch_shapes=[pltpu.VMEM((1536, 1024), BF16), pltpu.VMEM((1024, 1024), BF16), pltpu.VMEM((1, 1024), F32)],
        compiler_params=_cparams(("arbitrary",)),
    )(ys, xbc_act, z, y5, gates, dx1, dvec, gssd, glu_w, glu_b, wa, head_sel)


def _mlp_fwd_loss(x1, target, g, g_fin, wa):
    T = x1.shape[0]
    tm = TOKEN_TILE

    def body(x_ref, t_ref, g_ref, gf_ref, wa_hbm, dx_ref, loss_ref, dg_ref, w1_ref, w2_ref):
        @pl.when(pl.program_id(0) == 0)
        def _():
            _load_late_weight(wa_hbm, w1_ref, "w_mlp_in")
            _load_late_weight(wa_hbm, w2_ref, "w_mlp_out")
            loss_ref[...] = jnp.zeros_like(loss_ref)
            dg_ref[...] = jnp.zeros_like(dg_ref)

        xv = x_ref[...]
        xn, _ = _rms(xv)
        h = (xn * g_ref[...]).astype(BF16)
        acc = xv
        for s in range(FF_SHARDS):
            rl = jnp.maximum(_dot(h, w1_ref[s]), 0.0)
            acc += _dot((rl * rl).astype(BF16), w2_ref[FF_SHARD * s:FF_SHARD * (s + 1), :])
        yn, r = _rms(acc)
        gv = gf_ref[...]
        err = yn * gv - t_ref[...]
        loss_ref[...] += jnp.sum(err * err, axis=0, keepdims=True) * (0.5 / D_MODEL)
        dy = err * (1.0 / D_MODEL)
        dg_ref[...] += jnp.sum(dy * yn, axis=0, keepdims=True)
        dx_ref[...] = _rms_bwd(yn, r, dy * gv)

    return _pc(
        body, name="mlp_fwd_loss", grid=(T // tm,),
        in_specs=[_row_spec(tm, 1024), _row_spec(tm, 1024), _const_spec((1, 1024)), _const_spec((1, 1024)), _hbm_spec()],
        out_specs=[_row_spec(tm, 1024), _const_spec((1, 1024)), _const_spec((1, 1024))],
        out_shape=[jax.ShapeDtypeStruct((T, 1024), F32), jax.ShapeDtypeStruct((1, 1024), F32),
                   jax.ShapeDtypeStruct((1, 1024), F32)],
        scratch_shapes=[pltpu.VMEM((FF_SHARDS, D_MODEL, FF_SHARD), BF16), pltpu.VMEM((D_FF, D_MODEL), BF16)],
        compiler_params=_cparams(("arbitrary",)),
    )(x1, target, g, g_fin, wa)


def _mlp_bwd(x1, dx2, g, wa):
    T = x1.shape[0]
    tm = TOKEN_TILE

    def body(x_ref, dx2_ref, g_ref, wa_hbm, dx1_ref, h_ref, act_ref, da_ref, dg_ref, w1_ref, w2_ref):
        @pl.when(pl.program_id(0) == 0)
        def _():
            _load_late_weight(wa_hbm, w1_ref, "w_mlp_in")
            _load_late_weight(wa_hbm, w2_ref, "w_mlp_out")
            dg_ref[...] = jnp.zeros_like(dg_ref)

        xn, r = _rms(x_ref[...])
        gv = g_ref[...]
        h = (xn * gv).astype(BF16)
        h_ref[...] = h
        dx2 = dx2_ref[...]
        dx2b = dx2.astype(BF16)
        dh = jnp.zeros((tm, D_MODEL), F32)
        for s in range(FF_SHARDS):
            ff = slice(FF_SHARD * s, FF_SHARD * (s + 1))
            rl = jnp.maximum(_dot(h, w1_ref[s]), 0.0)
            act_ref[:, ff] = (rl * rl).astype(BF16)
            da = (_dot_nt(dx2b, w2_ref[ff, :]) * (2.0 * rl)).astype(BF16)
            da_ref[:, ff] = da
            dh += _dot_nt(da, w1_ref[s])
        dg_ref[...] += jnp.sum(dh * xn, axis=0, keepdims=True)
        dx1_ref[...] = dx2 + _rms_bwd(xn, r, dh * gv)

    return _pc(
        body, name="mlp_bwd", grid=(T // tm,),
        in_specs=[_row_spec(tm, 1024), _row_spec(tm, 1024), _const_spec((1, 1024)), _hbm_spec()],
        out_specs=[_row_spec(tm, 1024), _row_spec(tm, 1024), _row_spec(tm, D_FF), _row_spec(tm, D_FF), _const_spec((1, 1024))],
        out_shape=[jax.ShapeDtypeStruct((T, 1024), F32), jax.ShapeDtypeStruct((T, 1024), BF16),
                   jax.ShapeDtypeStruct((T, D_FF), BF16), jax.ShapeDtypeStruct((T, D_FF), BF16),
                   jax.ShapeDtypeStruct((1, 1024), F32)],
        scratch_shapes=[pltpu.VMEM((FF_SHARDS, D_MODEL, FF_SHARD), BF16), pltpu.VMEM((D_FF, D_MODEL), BF16)],
        compiler_params=_cparams(("arbitrary",)),
    )(x1, dx2, g, wa)


WGRAD_OUT_ELEMS = 2 * 1024 * 1024
WGRAD_TILE_BYTES = 4 * 1024 * 1024


def _wgrad(a, b, name, col_shards=None, row_shards_into=None):
    T, K = a.shape
    N = b.shape[1]
    nb = N // col_shards if col_shards else min(N, 1024, max(128, WGRAD_OUT_ELEMS // K))
    tt = min(T, WGRAD_TOKENS)
    while tt * max(K * a.dtype.itemsize, nb * b.dtype.itemsize) > WGRAD_TILE_BYTES:
        tt //= 2
    assert N % nb == 0 and T % tt == 0
    in_specs = [pl.BlockSpec((tt, K), lambda n, t: (t, 0)), pl.BlockSpec((tt, nb), lambda n, t: (t, n))]
    args, aliases = [a, b], {}
    if col_shards:
        out_spec = pl.BlockSpec((None, None, K, nb), lambda n, t: (n, 0, 0, 0))
        out_shape = jax.ShapeDtypeStruct((col_shards, 2, K, nb), F32)
    elif row_shards_into is not None:
        shards, _, rows, cols = row_shards_into.shape
        assert shards * rows == K and cols == N
        out_spec = pl.BlockSpec((shards, None, rows, nb), lambda n, t: (0, 1, 0, n))
        out_shape = jax.ShapeDtypeStruct(row_shards_into.shape, F32)
        in_specs.append(_hbm_spec())
        args.append(row_shards_into)
        aliases = {2: 0}
    else:
        out_spec = pl.BlockSpec((K, nb), lambda n, t: (0, n))
        out_shape = jax.ShapeDtypeStruct((K, N), F32)

    def body(a_ref, b_ref, *rest):
        o_ref = rest[-1]

        @pl.when(pl.program_id(1) == 0)
        def _():
            o_ref[...] = jnp.zeros_like(o_ref)

        o_ref[...] += _dot_tn(a_ref[...].astype(BF16), b_ref[...].astype(BF16)).reshape(o_ref.shape)

    return _pc(
        body, name=name, grid=(N // nb, T // tt), in_specs=in_specs, out_specs=out_spec, out_shape=out_shape,
        input_output_aliases=aliases, compiler_params=_cparams(("parallel", "arbitrary")),
    )(*args)


def _s5_block_weights(bb_re, bb_im, c_re, c_im):
    eye = jnp.eye(8, dtype=F32)
    bre = bb_re.reshape(16, S5_BLOCKS, 8, 64)
    bim = bb_im.reshape(16, S5_BLOCKS, 8, 64)
    wb_re = jnp.einsum('kjgp,gh->jhkgp', bre, eye).reshape(S5_BLOCKS, _BI, _BW)
    wb_im = jnp.einsum('kjgp,gh->jhkgp', bim, eye).reshape(S5_BLOCKS, _BI, _BW)
    wb4 = jnp.concatenate([wb_re, wb_im], axis=2).astype(BF16)
    cre = c_re.reshape(S5_BLOCKS, 8, 16, 64)
    cim = c_im.reshape(S5_BLOCKS, 8, 16, 64)
    wc_re = jnp.einsum('jgkp,gh->jgphk', cre, eye).reshape(S5_BLOCKS, _BW, _BI)
    wc_im = jnp.einsum('jgkp,gh->jgphk', -cim, eye).reshape(S5_BLOCKS, _BW, _BI)
    wc4 = jnp.concatenate([wc_re, wc_im], axis=1).astype(BF16)
    return wb4, wc4


def _s5_block_grads(dwb4, dwc4):
    eye = jnp.eye(8, dtype=F32)
    dwb = dwb4.reshape(S5_BLOCKS, 8, 16, 2, 8, 64)
    dbb = jnp.einsum('jhkrgp,gh->rkjgp', dwb, eye).reshape(2, 16, S5_STATES)
    dwc = dwc4.reshape(S5_BLOCKS, 2, 8, 64, 8, 16)
    dc = jnp.einsum('jrgphk,gh->rjgkp', dwc, eye).reshape(2, 32, 16, 64)
    return dbb[0], dbb[1], dc[0], -dc[1]


def _row(v, width=None):
    v = v.reshape(1, -1)
    if width is not None and v.shape[1] < width:
        v = jnp.concatenate([v, jnp.zeros((1, width - v.shape[1]), v.dtype)], axis=1)
    return v


def _local_step(x, target, p, comm=None):
    g_mix, g_mlp, g_fin = _row(p["norm_mix_g"]), _row(p["norm_mlp_g"]), _row(p["norm_final_g"])
    conv_b = _row(p["conv_b"])
    dt_bias = _row(p["dt_bias"], DT_PAD)
    alog = _row(p["a_log"], DT_PAD)
    dvec = _row(jnp.repeat(p["d_ssd"], SSD_HEADDIM))
    gssd = _row(p["ssd_norm_g"])
    s5d = _row(p["s5_d"])
    glu_b = _row(p["s5_glu_b"])
    head_sel = (jnp.arange(SSD_INNER)[:, None] // SSD_HEADDIM == jnp.arange(DT_PAD)[None, :]).astype(F32)

    a_re = p["s5_a_re"].reshape(1, S5_STATES)
    a_im = p["s5_a_im"].reshape(1, S5_STATES)
    log_dt = jnp.repeat(p["s5_log_dt"], 64).reshape(1, S5_STATES)
    b_re = p["s5_b_re"].reshape(S5_STATES, 16).T
    b_im = p["s5_b_im"].reshape(S5_STATES, 16).T
    ab_re, ab_im, bb_re, bb_im = _s5_disc(a_re, a_im, log_dt, b_re, b_im)
    wb4, wc4 = _s5_block_weights(bb_re, bb_im, p["s5_c_re"], p["s5_c_im"])
    ab = jnp.concatenate([ab_re, ab_im, jnp.zeros((6, S5_STATES), F32)], axis=0)

    wp = p["w_in_perm"]

    z, xbc_raw, u5, gates, dt_raw, h = _inproj_fwd(x, g_mix, wp)
    xbc_act, dt = _conv_fwd(xbc_raw, dt_raw, p["conv_w"], conv_b, dt_bias)
    ys, ssd_states = _ssd_fwd(xbc_act, dt, alog)
    if comm is None:
        y5, s5_states = _s5_fwd(u5, wb4, wc4, ab, s5d)
    else:
        (y5, s5_states), late = _s5_fwd(u5, wb4, wc4, ab, s5d, rider=_Gather(comm["late_srcs"], comm["late_ks"]))
        p = {**p, **comm["late_unpack"](late)}
    wa, glu_w = p["late_weights"], p["s5_glu_w"]
    x1 = _merge_fwd(ys, xbc_act, z, y5, gates, x, dvec, gssd, glu_w, glu_b, wa)
    dx2, loss_lanes, d_gfin = _mlp_fwd_loss(x1, target, g_mlp, g_fin, wa)

    dx1, h2, act, da1, d_gmlp = _mlp_bwd(x1, dx2, g_mlp, wa)
    g_mlp4 = _wgrad(h2, da1, "wgrad_mlp_in", col_shards=FF_SHARDS)
    g_mlp4 = _wgrad(act, dx2, "wgrad_mlp_out", row_shards_into=g_mlp4)
    d_w_mlp_in, d_w_mlp_out = g_mlp4[:, 0], g_mlp4[:, 1].reshape(D_FF, D_MODEL)
    merge_args = (ys, xbc_act, z, y5, gates, dx1, dvec, gssd, glu_w, glu_b, wa, head_sel)
    if comm is None:
        merge_out = _merge_bwd(*merge_args)
    else:
        g_mlp = g_mlp4.reshape(N_CHIPS, 2 * FF_SHARD, D_MODEL)
        merge_out, (sib_mlp,) = _merge_bwd(*merge_args, rider=_Pair([g_mlp]))
        pf_mlp, pb_mlp = _pair_sum(comm["place"], g_mlp, sib_mlp, "pair_sum_mlp")
    (dys, dxs_m, dz, dy5, dgates, mg, ya, yb, dpa, dpb, gel, dpre, d_dssd, d_gssd, d_glu_b) = merge_out
    d_w_out = _wgrad(mg, dx1, "wgrad_out")
    d_w_branch = jnp.concatenate([_wgrad(ya, dpa, "wgrad_branch_a"), _wgrad(yb, dpb, "wgrad_branch_b")], axis=0)
    d_glu_w = _wgrad(gel, dpre, "wgrad_glu")
    s5_args = (u5, dy5, wb4, wc4, ab, s5d, s5_states)
    if comm is None:
        du5, dwb4, dwc4, dab, d_s5d = _s5_bwd(*s5_args)
        mlp_total = None
    else:
        (du5, dwb4, dwc4, dab, d_s5d), (got_mlp,) = _s5_bwd(*s5_args, rider=_Chip([pb_mlp]))
        mlp_total = _chip_sum(comm["place"], pf_mlp, got_mlp, "chip_sum_mlp")
    dbb_re, dbb_im, d_c_re, d_c_im = _s5_block_grads(dwb4, dwc4)
    d_a_re, d_a_im, d_log_dt, d_b_re, d_b_im = _s5_disc_bwd(
        a_re, a_im, log_dt, b_re, b_im, dab[0:1], dab[1:2], dbb_re, dbb_im)
    dxs_s, dB, dC, ddt, d_alog = _ssd_bwd(xbc_act, dt, alog, ssd_states, dys)
    dxbc_raw, ddt_raw, d_conv_w, d_conv_b, d_dt_bias = _conv_bwd(
        xbc_raw, dt_raw, dxs_m, dxs_s, dB, dC, ddt, p["conv_w"], conv_b, dt_bias)
    d_w_in = dict(z=_wgrad(h, dz, "wgrad_in_z"), xbc=_wgrad(h, dxbc_raw, "wgrad_in_xbc"),
                  dt=_wgrad(h, ddt_raw, "wgrad_in_dt")[:, :16], u5=_wgrad(h, du5, "wgrad_in_u5"),
                  gates=_wgrad(h, dgates, "wgrad_in_gates"))
    w_in_pieces = [(c0, d_w_in[n]) for n, c0, _ in W_IN_PIECES]
    inproj_args = (x, dx1, dz, dxbc_raw, du5, dgates, ddt_raw, g_mix, wp)
    if comm is None:
        dx, d_gmix = _inproj_bwd(*inproj_args)
        late_totals = None
    else:
        g_b, g_in = _late_buffers(d_w_out, d_w_branch, d_glu_w, d_conv_w[:CONV_K], w_in_pieces)
        sib_b, sib_in = _exchange(_Pair([g_b, g_in]), "pair_exchange")
        pf_b, pb_b = _pair_sum(comm["place"], g_b, sib_b, "pair_sum_b")
        pf_in, pb_in = _pair_sum(comm["place"], g_in, sib_in, "pair_sum_in")
        (dx, d_gmix), (got_b, got_in) = _inproj_bwd(*inproj_args, rider=_Chip([pb_b, pb_in]))
        late_totals = (_chip_sum(comm["place"], pf_b, got_b, "chip_sum_b"),
                       _chip_sum(comm["place"], pf_in, got_in, "chip_sum_in"))

    grads = dict(
        norm_mix_g=d_gmix.reshape(-1), w_in_pieces=w_in_pieces, late_totals=late_totals,
        conv_w=d_conv_w[:CONV_K], conv_b=d_conv_b.reshape(-1),
        dt_bias=d_dt_bias[0, :16], a_log=d_alog[0, :16], d_ssd=d_dssd[0, :16], ssd_norm_g=d_gssd.reshape(-1),
        s5_a_re=d_a_re.reshape(32, 64), s5_a_im=d_a_im.reshape(32, 64), s5_log_dt=d_log_dt[0, :32],
        s5_b_re=d_b_re.T.reshape(32, 64, 16), s5_b_im=d_b_im.T.reshape(32, 64, 16), s5_c_re=d_c_re, s5_c_im=d_c_im,
        s5_d=d_s5d.reshape(-1), s5_glu_w=d_glu_w, s5_glu_b=d_glu_b.reshape(-1), w_branch=d_w_branch, w_out=d_w_out,
        norm_mlp_g=d_gmlp.reshape(-1), w_mlp_in=d_w_mlp_in, w_mlp_out=d_w_mlp_out, norm_final_g=d_gfin.reshape(-1),
        mlp_total=mlp_total)
    return jnp.sum(loss_lanes), dx, grads


MESH = pl.DeviceIdType.MESH
N_CHIPS = 4


def _place():
    x, y, c = lax.axis_index("x"), lax.axis_index("y"), lax.axis_index("c")
    chips = [(1 - x, y), (x, 1 - y), (1 - x, 1 - y)]
    return x, y, c, chips


def _remote(src, dst, send_sems, recv_sems, k, to):
    return pltpu.make_async_remote_copy(src_ref=src, dst_ref=dst, send_sem=send_sems.at[k], recv_sem=recv_sems.at[k],
                                        device_id=to, device_id_type=MESH)


def _row_chunks(rows, k, align):
    step = rows // k
    assert rows % k == 0 and step % align == 0, (rows, k, align)
    return [(i * step, step) for i in range(k)]


ICI_CHUNKS = 4
D2D_CHUNKS = 24


class _Gather:
    def __init__(self, srcs, ks):
        self.inputs = list(srcs)
        self.out_shapes = [jax.ShapeDtypeStruct((N_CHIPS,) + a.shape, a.dtype) for a in srcs]
        self.halves = [a.shape[0] // 2 for a in srcs]
        self.pieces = [_row_chunks(h, k, 32 // a.dtype.itemsize) for a, h, k in zip(srcs, self.halves, ks)]
        self.n_ici = 3 * sum(ks)
        self.n_sems = 2 * self.n_ici + len(srcs)

    def _plan(self, src_refs, out_refs, send_sems, recv_sems):
        x, y, c, chips = _place()
        own = 2 * x + y
        sib = (x, y, 1 - c)
        first, fwd_plan, k = [], [], 0
        for a, (src_ref, out_ref) in enumerate(zip(src_refs, out_refs)):
            h = self.halves[a]
            for r0, nr in self.pieces[a]:
                for cx, cy in chips:
                    first.append(_remote(src_ref.at[pl.ds(c * h + r0, nr), :], out_ref.at[own, pl.ds(c * h + r0, nr), :],
                                         send_sems, recv_sems, k, (cx, cy, c)))
                    fwd_plan.append((out_ref, 2 * cx + cy, h, r0, nr, k, (cx, cy, c)))
                    k += 1
        for a, (src_ref, out_ref) in enumerate(zip(src_refs, out_refs)):
            first.append(_remote(src_ref, out_ref.at[own], send_sems, recv_sems, 2 * self.n_ici + a, sib))
        return first, fwd_plan, c, sib

    def issue(self, src_refs, out_refs, send_sems, recv_sems):
        for cp in self._plan(src_refs, out_refs, send_sems, recv_sems)[0]:
            cp.start()

    def complete(self, src_refs, out_refs, send_sems, recv_sems):
        first, fwd_plan, c, sib = self._plan(src_refs, out_refs, send_sems, recv_sems)
        passed = []
        for out_ref, s, h, r0, nr, k, frm in fwd_plan:
            got = out_ref.at[s, pl.ds(c * h + r0, nr), :]
            _remote(got, got, send_sems, recv_sems, k, frm).wait_recv()
            fw = _remote(got, got, send_sems, recv_sems, self.n_ici + k, sib)
            fw.start()
            passed.append(fw)
        for out_ref, s, h, r0, nr, k, frm in fwd_plan:
            got = out_ref.at[s, pl.ds((1 - c) * h + r0, nr), :]
            _remote(got, got, send_sems, recv_sems, self.n_ici + k, sib).wait_recv()
        own_copies = first[self.n_ici:]
        for cp in own_copies:
            cp.wait_recv()
        for cp in first + passed:
            cp.wait_send()


def _exchange(rider, name):
    ri, ro = len(rider.inputs), len(rider.out_shapes)

    def body(*refs):
        rider.issue(refs[:ri], refs[ri:ri + ro], *refs[ri + ro:])
        rider.complete(refs[:ri], refs[ri:ri + ro], *refs[ri + ro:])

    return _pc(
        body, name=name, in_specs=[_hbm_spec()] * ri, out_specs=[_hbm_spec()] * ro, out_shape=list(rider.out_shapes),
        scratch_shapes=[pltpu.SemaphoreType.DMA((rider.n_sems,))] * 2,
    )(*rider.inputs)


def _call(body, rider=None, **kw):
    if rider is None:
        return _pc(body, **kw)
    single = not isinstance(kw["out_shape"], (list, tuple))
    out_specs = [kw["out_specs"]] if single else list(kw["out_specs"])
    out_shape = [kw["out_shape"]] if single else list(kw["out_shape"])
    scratch = list(kw.get("scratch_shapes", ()))
    n_in, n_out, n_scr = len(kw["in_specs"]), len(out_specs), len(scratch)
    ri, ro = len(rider.inputs), len(rider.out_shapes)
    steps = kw["grid"][0]

    def wrapped(*refs):
        o0 = n_in + ri
        s0 = o0 + n_out + ro
        r_in, r_out, sems = refs[n_in:o0], refs[o0 + n_out:s0], refs[s0 + n_scr:]

        @pl.when(pl.program_id(0) == 0)
        def _():
            rider.issue(r_in, r_out, *sems)

        body(*refs[:n_in], *refs[o0:o0 + n_out], *refs[s0:s0 + n_scr])

        @pl.when(pl.program_id(0) == steps - 1)
        def _():
            rider.complete(r_in, r_out, *sems)

    f = _pc(wrapped, name=kw["name"], grid=kw["grid"], in_specs=list(kw["in_specs"]) + [_hbm_spec()] * ri,
            out_specs=out_specs + [_hbm_spec()] * ro, out_shape=out_shape + list(rider.out_shapes),
            scratch_shapes=scratch + [pltpu.SemaphoreType.DMA((rider.n_sems,))] * 2, compiler_params=kw["compiler_params"])

    def run(*args):
        res = f(*args, *rider.inputs)
        return (res[0] if single else res[:n_out]), res[n_out:]

    return run


def _d2d_pieces(rows):
    k = next(k for k in range(24, 0, -1) if rows % k == 0 and (rows // k) % 8 == 0)
    return _row_chunks(rows, k, 8)


class _Pair:
    def __init__(self, gs, small=None):
        self.n = len(gs)
        self.halves = [g.shape[1] // 2 for g in gs]
        self.inputs = list(gs) + ([small] if small is not None else [])
        self.out_shapes = [jax.ShapeDtypeStruct((N_CHIPS, h, g.shape[2]), F32) for g, h in zip(gs, self.halves)]
        if small is not None:
            self.out_shapes.append(jax.ShapeDtypeStruct(small.shape, F32))
        self.n_sems = len(self.inputs)

    def issue(self, in_refs, out_refs, send_sems, recv_sems):
        x, y, c, _ = _place()
        sib = (x, y, 1 - c)
        for a in range(self.n):
            h = self.halves[a]
            for s in range(N_CHIPS):
                for r0, nr in _d2d_pieces(h):
                    _remote(in_refs[a].at[s, pl.ds((1 - c) * h + r0, nr), :], out_refs[a].at[s, pl.ds(r0, nr), :],
                            send_sems, recv_sems, a, sib).start()
        for a in range(self.n, len(self.inputs)):
            _remote(in_refs[a], out_refs[a], send_sems, recv_sems, a, sib).start()

    def complete(self, in_refs, out_refs, send_sems, recv_sems):
        x, y, c, _ = _place()
        for a in range(len(self.inputs)):
            _remote(out_refs[a], out_refs[a], send_sems, recv_sems, a, (x, y, 1 - c)).wait()


SUM_BLOCKS = 4


def _pair_sum(place, g, sib, name):
    n, R, C = g.shape
    H = R // 2
    rb = H // SUM_BLOCKS
    assert H % SUM_BLOCKS == 0 and rb % 16 == 0

    def body(place_ref, a_ref, b_ref, pf_ref, pb_ref):
        p = a_ref[...] + b_ref[...]
        pf_ref[...] = p
        pb_ref[...] = p.astype(BF16)

    blk = pl.BlockSpec((1, rb, C), lambda s, i, pr: (s, i, 0))
    mine = pl.BlockSpec((1, rb, C), lambda s, i, pr: (s, pr[1] * SUM_BLOCKS + i, 0))
    return _pc(
        body, name=name, out_shape=[jax.ShapeDtypeStruct((n, H, C), F32), jax.ShapeDtypeStruct((n, H, C), BF16)],
        grid_spec=pltpu.PrefetchScalarGridSpec(num_scalar_prefetch=1, grid=(n, SUM_BLOCKS), in_specs=[mine, blk],
                                               out_specs=[blk, blk]),
        compiler_params=_cparams(("arbitrary", "arbitrary")),
    )(place, g, sib)


class _Chip:
    def __init__(self, pbs, psmall=None):
        self.n = len(pbs)
        self.rows = [pb.shape[1] for pb in pbs]
        self.inputs = list(pbs) + ([psmall] if psmall is not None else [])
        self.out_shapes = [jax.ShapeDtypeStruct((3,) + pb.shape[1:], BF16) for pb in pbs]
        if psmall is not None:
            self.out_shapes.append(jax.ShapeDtypeStruct((N_CHIPS,) + psmall.shape, F32))
        self.n_sems = 3 * len(self.inputs)

    def issue(self, in_refs, out_refs, send_sems, recv_sems):
        x, y, c, chips = _place()
        own = 2 * x + y
        for j, (cx, cy) in enumerate(chips):
            for a in range(self.n):
                for r0, nr in _row_chunks(self.rows[a], ICI_CHUNKS, 16):
                    _remote(in_refs[a].at[2 * cx + cy, pl.ds(r0, nr), :], out_refs[a].at[j, pl.ds(r0, nr), :],
                            send_sems, recv_sems, 3 * a + j, (cx, cy, c)).start()
            for a in range(self.n, len(self.inputs)):
                _remote(in_refs[a], out_refs[a].at[own], send_sems, recv_sems, 3 * a + j, (cx, cy, c)).start()

    def complete(self, in_refs, out_refs, send_sems, recv_sems):
        x, y, c, chips = _place()
        own = 2 * x + y
        for j, (cx, cy) in enumerate(chips):
            for a in range(self.n):
                _remote(in_refs[a].at[own], out_refs[a].at[j], send_sems, recv_sems, 3 * a + j, (cx, cy, c)).wait()
            for a in range(self.n, len(self.inputs)):
                _remote(in_refs[a], out_refs[a].at[2 * cx + cy], send_sems, recv_sems, 3 * a + j, (cx, cy, c)).wait()


def _chip_sum(place, pf, got, name):
    _, H, C = pf.shape
    rb = H // SUM_BLOCKS

    def body(place_ref, o_ref, g_ref, tot_ref):
        tot_ref[...] = ((o_ref[0] + g_ref[0].astype(F32)) + g_ref[1].astype(F32)) + g_ref[2].astype(F32)

    ins = [pl.BlockSpec((1, rb, C), lambda i, pr: (pr[0], i, 0)), pl.BlockSpec((3, rb, C), lambda i, pr: (0, i, 0))]
    out = pl.BlockSpec((rb, C), lambda i, pr: (pr[1] * SUM_BLOCKS + i, 0))
    return _pc(
        body, name=name, out_shape=jax.ShapeDtypeStruct((2 * H, C), F32),
        grid_spec=pltpu.PrefetchScalarGridSpec(num_scalar_prefetch=1, grid=(SUM_BLOCKS,), in_specs=ins, out_specs=out),
        compiler_params=_cparams(("arbitrary",)),
    )(place, pf, got)


def _half_exchange(fulls):
    n = len(fulls)

    def body(*refs):
        in_refs, out_refs = refs[:n], refs[n:2 * n]
        send_sems, recv_sems = refs[2 * n:]
        x, y, c, _ = _place()
        sib = (x, y, 1 - c)
        for a in range(n):
            h = fulls[a].shape[0] // 2
            for r0, nr in _d2d_pieces(h):
                rows = pl.ds(c * h + r0, nr)
                _remote(in_refs[a].at[rows, :], out_refs[a].at[rows, :], send_sems, recv_sems, a, sib).start()
        for a in range(n):
            h = fulls[a].shape[0] // 2
            _remote(in_refs[a].at[pl.ds(c * h, h), :], out_refs[a].at[pl.ds((1 - c) * h, h), :], send_sems, recv_sems, a,
                    sib).wait()

    return _pc(
        body, name="half_exchange", in_specs=[_hbm_spec()] * n, out_specs=[_hbm_spec()] * n,
        out_shape=[jax.ShapeDtypeStruct(f.shape, F32) for f in fulls],
        input_output_aliases={a: a for a in range(n)},
        scratch_shapes=[pltpu.SemaphoreType.DMA((n,)), pltpu.SemaphoreType.DMA((n,))],
    )(*fulls)


def _small_allreduce(pack):
    R, C = pack.shape

    def body(p_ref, o_ref, sib_ref, pair_ref, slots_ref, send_sems, recv_sems):
        x, y, c, chips = _place()
        own = 2 * x + y
        cp = _remote(p_ref, sib_ref, send_sems, recv_sems, 0, (x, y, 1 - c))
        cp.start()
        cp.wait()
        pair_ref[...] = p_ref[...] + sib_ref[...]
        slots_ref[own] = pair_ref[...]
        out = [_remote(pair_ref, slots_ref.at[own], send_sems, recv_sems, 1 + j, (cx, cy, c)) for j, (cx, cy) in enumerate(chips)]
        for cp in out:
            cp.start()
        for j, (cx, cy) in enumerate(chips):
            _remote(pair_ref, slots_ref.at[2 * cx + cy], send_sems, recv_sems, 1 + j, (cx, cy, c)).wait()
        o_ref[...] = ((slots_ref[0] + slots_ref[1]) + slots_ref[2]) + slots_ref[3]

    vmem = pl.BlockSpec(memory_space=pltpu.VMEM)
    return _pc(
        body, name="small_allreduce", in_specs=[vmem], out_specs=vmem, out_shape=jax.ShapeDtypeStruct((R, C), F32),
        scratch_shapes=[pltpu.VMEM((R, C), F32), pltpu.VMEM((R, C), F32), pltpu.VMEM((N_CHIPS, R, C), F32),
                        pltpu.SemaphoreType.DMA((4,)), pltpu.SemaphoreType.DMA((4,))],
    )(pack)


def _adamw(w, g, m, v, name, g_row0=0, with_grad=False, col_block=None):
    R, C = w.shape
    rb = 256 if R % 256 == 0 else (128 if R % 128 == 0 else R)
    if col_block:
        rb = R
    assert g_row0 % rb == 0

    def body(w_ref, g_ref, m_ref, v_ref, d_ref, nm_ref, nv_ref, *g_out):
        gv = g_ref[...]
        m2 = ADAM_B1 * m_ref[...] + (1.0 - ADAM_B1) * gv
        v2 = ADAM_B2 * v_ref[...] + (1.0 - ADAM_B2) * (gv * gv)
        m_hat = m2 * (1.0 / (1.0 - ADAM_B1 ** ADAM_STEP))
        v_hat = v2 * (1.0 / (1.0 - ADAM_B2 ** ADAM_STEP))
        d_ref[...] = -ADAM_LR * (m_hat / (jnp.sqrt(v_hat) + ADAM_EPS) + ADAM_WD * w_ref[...])
        nm_ref[...] = m2
        nv_ref[...] = v2
        if with_grad:
            g_out[0][...] = gv

    if col_block:
        spec = g_spec = pl.BlockSpec((R, col_block), lambda i: (0, i))
        steps = C // col_block
    else:
        spec = pl.BlockSpec((rb, C), lambda i: (i, 0))
        g_spec = pl.BlockSpec((rb, C), lambda i: (g_row0 // rb + i, 0))
        steps = R // rb
    n_out = 4 if with_grad else 3
    return _pc(
        body, name=name, grid=(steps,), in_specs=[spec, g_spec, spec, spec], out_specs=[spec] * n_out,
        out_shape=[jax.ShapeDtypeStruct((R, C), F32)] * n_out, compiler_params=_cparams(("parallel",)),
    )(w, g, m, v)


PACK_COLS = 1024
ROWS_A = (("w_mlp_in", 0, 1024), ("w_mlp_out", 1024, 1024), ("w_out", 2048, 256), ("w_branch", 2304, 384))
ROWS_A_TOTAL = 2688
ROWS_B = (("w_out", 0, 256), ("w_branch", 256, 384))
ROW_B_GLU, ROW_B_CONV, ROWS_B_TOTAL = 640, 704, 768
W_IN_SHARD = 1412
CONV_PAD_ROWS = 16
SMALL = (("norm_mix_g", (1024,)), ("conv_b", (2048,)), ("dt_bias", (16,)), ("a_log", (16,)), ("d_ssd", (16,)),
         ("ssd_norm_g", (1024,)), ("s5_a_re", (32, 64)), ("s5_a_im", (32, 64)), ("s5_log_dt", (32,)),
         ("s5_b_re", (32, 64, 16)), ("s5_b_im", (32, 64, 16)), ("s5_c_re", (32, 16, 64)), ("s5_c_im", (32, 16, 64)),
         ("s5_d", (512,)), ("s5_glu_b", (512,)), ("norm_mlp_g", (1024,)), ("norm_final_g", (1024,)))
SMALL_ROWS = 144
SMALL_COUNT = sum(math.prod(shp) for _, shp in SMALL)
GLU_ROWS = S5_WIDTH * S5_WIDTH // PACK_COLS
CONV_ROWS = CONV_K * CONV_DIM // PACK_COLS
W_IN_PIECES = (("z", 0, 1024), ("xbc", 1024, 2048), ("dt", OFF_DT, 16), ("u5", OFF_U, 512), ("gates", 3600, 2048))


def _pack_small(parts):
    flat = jnp.concatenate([a.astype(F32).reshape(-1) for a in parts])
    return jnp.concatenate([flat, jnp.zeros((SMALL_ROWS * PACK_COLS - flat.shape[0],), F32)]).reshape(SMALL_ROWS, PACK_COLS)


def _unpack_small(pack):
    flat, out, r = pack.reshape(-1), {}, 0
    for name, shp in SMALL:
        n = math.prod(shp)
        out[name] = flat[r:r + n].reshape(shp)
        r += n
    return out


def _late_buffers(d_w_out, d_w_branch, d_glu_w, d_conv_w, w_in_pieces):
    conv4 = d_conv_w.reshape(CONV_K, N_CHIPS, 512).transpose(1, 0, 2).reshape(N_CHIPS, CONV_ROWS // N_CHIPS, PACK_COLS)
    g_b = jnp.concatenate(
        [d_w_out.reshape(N_CHIPS, -1, PACK_COLS), d_w_branch.reshape(N_CHIPS, -1, PACK_COLS),
         d_glu_w.reshape(N_CHIPS, GLU_ROWS // N_CHIPS, PACK_COLS),
         jnp.pad(conv4, ((0, 0), (0, ROWS_B_TOTAL - ROW_B_CONV - CONV_ROWS // N_CHIPS), (0, 0)))], axis=1)
    g_in = jnp.stack([jnp.concatenate(_column_range(w_in_pieces, W_IN_SHARD * s, W_IN_SHARD * (s + 1)), axis=1)
                      for s in range(N_CHIPS)])
    return g_b, g_in


def _column_range(pieces, lo, hi):
    out = []
    for c0, a in pieces:
        a0, a1 = max(lo, c0), min(hi, c0 + a.shape[-1])
        if a0 < a1:
            out.append(a[..., a0 - c0:a1 - c0])
    return out


def kernel(x, norm_mix_g, w_in, conv_w, conv_b, dt_bias, a_log, d_ssd, ssd_norm_g, s5_a_re, s5_a_im, s5_log_dt, s5_b_re, s5_b_im, s5_c_re, s5_c_im, s5_d, s5_glu_w, s5_glu_b, w_branch, w_out, norm_mlp_g, w_mlp_in, w_mlp_out, norm_final_g, loss_target, m_norm_mix_g, m_w_in, m_conv_w, m_conv_b, m_dt_bias, m_a_log, m_d_ssd, m_ssd_norm_g, m_s5_a_re, m_s5_a_im, m_s5_log_dt, m_s5_b_re, m_s5_b_im, m_s5_c_re, m_s5_c_im, m_s5_d, m_s5_glu_w, m_s5_glu_b, m_w_branch, m_w_out, m_norm_mlp_g, m_w_mlp_in, m_w_mlp_out, m_norm_final_g, v_norm_mix_g, v_w_in, v_conv_w, v_conv_b, v_dt_bias, v_a_log, v_d_ssd, v_ssd_norm_g, v_s5_a_re, v_s5_a_im, v_s5_log_dt, v_s5_b_re, v_s5_b_im, v_s5_c_re, v_s5_c_im, v_s5_d, v_s5_glu_w, v_s5_glu_b, v_w_branch, v_w_out, v_norm_mlp_g, v_w_mlp_in, v_w_mlp_out, v_norm_final_g):
    names = ("norm_mix_g", "w_in", "conv_w", "conv_b", "dt_bias", "a_log", "d_ssd", "ssd_norm_g", "s5_a_re", "s5_a_im",
             "s5_log_dt", "s5_b_re", "s5_b_im", "s5_c_re", "s5_c_im", "s5_d", "s5_glu_w", "s5_glu_b", "w_branch", "w_out",
             "norm_mlp_g", "w_mlp_in", "w_mlp_out", "norm_final_g")
    w = dict(zip(names, (norm_mix_g, w_in, conv_w, conv_b, dt_bias, a_log, d_ssd, ssd_norm_g, s5_a_re, s5_a_im, s5_log_dt,
                         s5_b_re, s5_b_im, s5_c_re, s5_c_im, s5_d, s5_glu_w, s5_glu_b, w_branch, w_out, norm_mlp_g,
                         w_mlp_in, w_mlp_out, norm_final_g)))
    m = dict(zip(names, (m_norm_mix_g, m_w_in, m_conv_w, m_conv_b, m_dt_bias, m_a_log, m_d_ssd, m_ssd_norm_g, m_s5_a_re,
                         m_s5_a_im, m_s5_log_dt, m_s5_b_re, m_s5_b_im, m_s5_c_re, m_s5_c_im, m_s5_d, m_s5_glu_w,
                         m_s5_glu_b, m_w_branch, m_w_out, m_norm_mlp_g, m_w_mlp_in, m_w_mlp_out, m_norm_final_g)))
    v = dict(zip(names, (v_norm_mix_g, v_w_in, v_conv_w, v_conv_b, v_dt_bias, v_a_log, v_d_ssd, v_ssd_norm_g, v_s5_a_re,
                         v_s5_a_im, v_s5_log_dt, v_s5_b_re, v_s5_b_im, v_s5_c_re, v_s5_c_im, v_s5_d, v_s5_glu_w,
                         v_s5_glu_b, v_w_branch, v_w_out, v_norm_mlp_g, v_w_mlp_in, v_w_mlp_out, v_norm_final_g)))

    cx, cy, cc = lax.axis_index("x"), lax.axis_index("y"), lax.axis_index("c")
    own = 2 * cx + cy
    place = jnp.stack([own, cc]).astype(jnp.int32)

    src_conv = jnp.concatenate([conv_w, jnp.zeros((CONV_PAD_ROWS - CONV_K, 512), F32)], axis=0)
    all_in, all_conv = _exchange(_Gather([w_in.astype(BF16), src_conv], [ICI_CHUNKS, 1]), "gather_first")
    p = {n: w[n] for n, _ in SMALL}
    p["conv_w"] = jnp.concatenate([all_conv[s, :CONV_K] for s in range(N_CHIPS)], axis=1)
    shards = [(W_IN_SHARD * s, all_in[s]) for s in range(N_CHIPS)]
    p["w_in_perm"] = jnp.concatenate(
        _column_range(shards, 0, OFF_DT) + _column_range(shards, OFF_U, D_IN_PROJ) + _column_range(shards, OFF_DT, OFF_U)
        + [jnp.zeros((D_MODEL, DT_PAD - 16), BF16)], axis=1)

    def late_unpack(gathered):
        all_a, all_glu = gathered
        return {"late_weights": all_a, "s5_glu_w": all_glu.reshape(S5_WIDTH, S5_WIDTH)}

    comm = dict(place=place, late_ks=[ICI_CHUNKS, 1], late_unpack=late_unpack,
                late_srcs=[jnp.concatenate([w[n].astype(BF16) for n, _, _ in ROWS_A], axis=0), s5_glu_w.astype(BF16)])
    loss_part, grad_x, g = _local_step(x[0], loss_target[0], p, comm)

    red_mlp, red_b, red_in = _half_exchange([g["mlp_total"], *g["late_totals"]])
    small_tot = _small_allreduce(_pack_small([g[n] for n, _ in SMALL] + [loss_part.reshape(1)]))
    loss = small_tot.reshape(-1)[SMALL_COUNT]

    grads = _unpack_small(small_tot)
    delta, new_m, new_v = {}, {}, {}
    for n, r0, _ in ROWS_A[:2]:
        delta[n], new_m[n], new_v[n], grads[n] = _adamw(w[n], red_mlp, m[n], v[n], "adamw_" + n, g_row0=r0, with_grad=True)
    for n, r0, _ in ROWS_B:
        delta[n], new_m[n], new_v[n], grads[n] = _adamw(w[n], red_b, m[n], v[n], "adamw_" + n, g_row0=r0, with_grad=True)
    d_t, m_t, v_t, g_t = _adamw(w_in.T, red_in.T, m_w_in.T, v_w_in.T, "adamw_w_in", with_grad=True, col_block=128)
    delta["w_in"], new_m["w_in"], new_v["w_in"], grads["w_in"] = d_t.T, m_t.T, v_t.T, g_t.T
    grads["s5_glu_w"] = red_b[ROW_B_GLU:ROW_B_GLU + GLU_ROWS // N_CHIPS].reshape(S5_WIDTH // N_CHIPS, S5_WIDTH)
    grads["conv_w"] = red_b[ROW_B_CONV:ROW_B_CONV + CONV_ROWS // N_CHIPS].reshape(CONV_K, CONV_DIM // N_CHIPS)
    for n in ("s5_glu_w", "conv_w"):
        delta[n], new_m[n], new_v[n] = _adamw(w[n], grads[n], m[n], v[n], "adamw_" + n)
    ds, ms, vs = _adamw(_pack_small([w[n] for n, _ in SMALL]), small_tot, _pack_small([m[n] for n, _ in SMALL]),
                        _pack_small([v[n] for n, _ in SMALL]), "adamw_small")
    delta.update(_unpack_small(ds))
    new_m.update(_unpack_small(ms))
    new_v.update(_unpack_small(vs))

    return (loss, grad_x[None], *[grads[n] for n in names], *[delta[n] for n in names],
            *[new_m[n] for n in names], *[new_v[n] for n in names])
```

```python
import functools
import math

import jax
import jax.numpy as jnp
from jax import lax
from jax.experimental import pallas as pl
from jax.experimental.pallas import tpu as pltpu

F32 = jnp.float32
BF16 = jnp.bfloat16

D_MODEL = 1024
SSD_INNER = 1024
SSD_HEADS = 16
SSD_HEADDIM = 64
SSD_GROUPS = 4
SSD_HPG = 4
SSD_STATE = 128
SSD_CHUNK = 128
CONV_K = 4
CONV_DIM = 2048
S5_WIDTH = 512
S5_STATES = 2048
S5_BLOCKS = 4
S5_CHUNK = 128
D_FF = 4096
FF_SHARDS = 4
FF_SHARD = D_FF // FF_SHARDS
EPS = 1e-6
P_Z, P_XBC, P_U5, P_G, P_DT, P_END = 0, 1024, 3072, 3584, 5632, 5760
DT_PAD = 128
OFF_DT, OFF_U = 3072, 3088
D_IN_PROJ = 5648

ADAM_LR, ADAM_B1, ADAM_B2, ADAM_EPS, ADAM_WD, ADAM_STEP = 0.001, 0.9, 0.999, 1e-08, 0.01, 10

TOKEN_TILE = 256
VMEM_LIMIT = 56 * 1024 * 1024
HALO = 8
CONV_COLS = 256
CONV_ROWS_BLK = 64
WGRAD_TOKENS = 2048


def _pc(body, **kw):
    return pl.pallas_call(body, **kw)


def _cparams(sem=None):
    return pltpu.CompilerParams(dimension_semantics=sem, vmem_limit_bytes=VMEM_LIMIT)


def _dot(a, b):
    return jnp.dot(a, b, preferred_element_type=F32)


def _dot_nt(a, b):
    return lax.dot_general(a, b, (((1,), (1,)), ((), ())), preferred_element_type=F32)


def _dot_tn(a, b):
    return lax.dot_general(a, b, (((0,), (0,)), ((), ())), preferred_element_type=F32)


def _dot_hi(a, b, dims=(((1,), (0,)), ((), ()))):
    return lax.dot_general(a, b, dims, preferred_element_type=F32, precision=lax.Precision.HIGHEST)


def _split_bf16(x, terms):
    out = []
    for _ in range(terms - 1):
        t = x.astype(BF16)
        out.append(t)
        x = x - t.astype(F32)
    out.append(x.astype(BF16))
    return out


def _dot_split(x, onehots, terms, dims=(((1,), (0,)), ((), ()))):
    acc = None
    for t in _split_bf16(x, terms):
        p = lax.dot_general(t, onehots, dims, preferred_element_type=F32)
        acc = p if acc is None else acc + p
    return acc


def _dot_split_rhs(onehots, x, terms, dims=(((1,), (0,)), ((), ()))):
    acc = None
    for t in _split_bf16(x, terms):
        p = lax.dot_general(onehots, t, dims, preferred_element_type=F32)
        acc = p if acc is None else acc + p
    return acc


def _sigmoid(x):
    return 0.5 * jnp.tanh(0.5 * x) + 0.5


def _softplus(x):
    return jnp.maximum(x, 0.0) + jnp.log(1.0 + jnp.exp(-jnp.abs(x)))


_GELU_C = math.sqrt(2.0 / math.pi)


def _gelu(x):
    return 0.5 * x * (1.0 + jnp.tanh(_GELU_C * (x + 0.044715 * x * x * x)))


def _gelu_grad(x):
    t = jnp.tanh(_GELU_C * (x + 0.044715 * x * x * x))
    return 0.5 * (1.0 + t) + 0.5 * x * (1.0 - t * t) * _GELU_C * (1.0 + 3.0 * 0.044715 * x * x)


def _rms(x):
    r = lax.rsqrt(jnp.mean(x * x, axis=-1, keepdims=True) + EPS)
    return x * r, r


def _rms_bwd(xn, r, dxn):
    return r * (dxn - xn * jnp.mean(dxn * xn, axis=-1, keepdims=True))


def _row_spec(tm, width, col=0):
    return pl.BlockSpec((tm, width), lambda i: (i, col))


def _const_spec(shape):
    nd = len(shape)
    return pl.BlockSpec(shape, lambda i: (0,) * nd)


def _hbm_spec():
    return pl.BlockSpec(memory_space=pl.ANY)


def _load_late_weight(wa_hbm, dst_ref, name):
    r0, nr = next((r0, nr) for n, r0, nr in ROWS_A if n == name)
    for s in range(N_CHIPS):
        dst = dst_ref.at[s] if len(dst_ref.shape) == 3 else dst_ref.at[pl.ds(nr * s, nr), :]
        pltpu.sync_copy(wa_hbm.at[s, pl.ds(r0, nr), :], dst)


def _inproj_fwd(x, g, wp):
    T = x.shape[0]
    tm = TOKEN_TILE

    def body(x_ref, g_ref, w_hbm, z_ref, xbc_ref, u5_ref, gt_ref, dt_ref, h_ref, w_ref):
        @pl.when(pl.program_id(0) == 0)
        def _():
            pltpu.sync_copy(w_hbm, w_ref)

        xn, _ = _rms(x_ref[...])
        h = (xn * g_ref[...]).astype(BF16)
        h_ref[...] = h
        z_ref[...] = _dot(h, w_ref[:, P_Z:P_XBC])
        xbc_ref[...] = _dot(h, w_ref[:, P_XBC:P_U5])
        u5_ref[...] = _dot(h, w_ref[:, P_U5:P_G])
        tail = _dot(h, w_ref[:, P_G:P_END])
        gt_ref[...] = tail[:, :P_DT - P_G]
        dt_ref[...] = tail[:, P_DT - P_G:]

    widths = (1024, 2048, 512, 2048, DT_PAD)
    return _pc(
        body, name="inproj_fwd", grid=(T // tm,),
        in_specs=[_row_spec(tm, D_MODEL), _const_spec((1, D_MODEL)), _hbm_spec()],
        out_specs=[_row_spec(tm, w) for w in widths] + [_row_spec(tm, D_MODEL)],
        out_shape=[jax.ShapeDtypeStruct((T, w), F32) for w in widths] + [jax.ShapeDtypeStruct((T, D_MODEL), BF16)],
        scratch_shapes=[pltpu.VMEM((D_MODEL, P_END), BF16)],
        compiler_params=_cparams(("arbitrary",)),
    )(x, g, wp)


def _inproj_bwd(x, dx1, dz, dxbc, du5, dgt, ddt, g, wp, rider=None):
    T = x.shape[0]
    tm = TOKEN_TILE

    def body(x_ref, dx1_ref, dz_ref, dxbc_ref, du5_ref, dgt_ref, ddt_ref, g_ref, w_hbm, dx_ref, dg_ref, w_ref):
        @pl.when(pl.program_id(0) == 0)
        def _():
            pltpu.sync_copy(w_hbm, w_ref)
            dg_ref[...] = jnp.zeros_like(dg_ref)

        xn, r = _rms(x_ref[...])
        gv = g_ref[...]
        dh = _dot_nt(dz_ref[...].astype(BF16), w_ref[:, P_Z:P_XBC])
        dh += _dot_nt(dxbc_ref[...].astype(BF16), w_ref[:, P_XBC:P_U5])
        dh += _dot_nt(du5_ref[...].astype(BF16), w_ref[:, P_U5:P_G])
        dh += _dot_nt(dgt_ref[...].astype(BF16), w_ref[:, P_G:P_DT])
        dh += _dot_nt(ddt_ref[...].astype(BF16), w_ref[:, P_DT:P_END])
        dg_ref[...] += jnp.sum(dh * xn, axis=0, keepdims=True)
        dx_ref[...] = dx1_ref[...] + _rms_bwd(xn, r, dh * gv)

    return _call(
        body, rider, name="inproj_bwd", grid=(T // tm,),
        in_specs=[_row_spec(tm, 1024), _row_spec(tm, 1024), _row_spec(tm, 1024), _row_spec(tm, 2048),
                  _row_spec(tm, 512), _row_spec(tm, 2048), _row_spec(tm, DT_PAD), _const_spec((1, 1024)), _hbm_spec()],
        out_specs=[_row_spec(tm, 1024), _const_spec((1, 1024))],
        out_shape=[jax.ShapeDtypeStruct((T, 1024), F32), jax.ShapeDtypeStruct((1, 1024), F32)],
        scratch_shapes=[pltpu.VMEM((D_MODEL, P_END), BF16)],
        compiler_params=_cparams(("arbitrary",)),
    )(x, dx1, dz, dxbc, du5, dgt, ddt, g, wp)


def _conv_fwd(xbc_raw, dt_raw, conv_w, conv_b, dt_bias):
    T = xbc_raw.shape[0]
    tm = TOKEN_TILE

    def body(u_ref, dtr_ref, w_ref, b_ref, db_ref, act_ref, dt_ref, ext_ref):
        @pl.when(pl.program_id(0) == 0)
        def _():
            ext_ref[0:HALO, :] = jnp.zeros((HALO, CONV_DIM), F32)

        ext_ref[HALO:, :] = u_ref[...]
        for c0 in range(0, CONV_DIM, CONV_COLS):
            cols = slice(c0, c0 + CONV_COLS)
            taps = [w_ref[k:k + 1, cols] for k in range(CONV_K)]
            bias = b_ref[:, cols]
            for r0 in range(0, tm, CONV_ROWS_BLK):
                y = bias + taps[0] * ext_ref[pl.ds(HALO - (CONV_K - 1) + r0, CONV_ROWS_BLK), cols]
                for k in range(1, CONV_K):
                    y += taps[k] * ext_ref[pl.ds(HALO - (CONV_K - 1) + k + r0, CONV_ROWS_BLK), cols]
                act_ref[r0:r0 + CONV_ROWS_BLK, cols] = y * _sigmoid(y)
        ext_ref[0:HALO, :] = u_ref[tm - HALO:tm, :]
        dt_ref[...] = _softplus(dtr_ref[...] + db_ref[...])

    return _pc(
        body, name="conv_fwd", grid=(T // tm,),
        in_specs=[_row_spec(tm, CONV_DIM), _row_spec(tm, DT_PAD), _const_spec((CONV_K, CONV_DIM)),
                  _const_spec((1, CONV_DIM)), _const_spec((1, DT_PAD))],
        out_specs=[_row_spec(tm, CONV_DIM), _row_spec(tm, DT_PAD)],
        out_shape=[jax.ShapeDtypeStruct((T, CONV_DIM), F32), jax.ShapeDtypeStruct((T, DT_PAD), F32)],
        scratch_shapes=[pltpu.VMEM((tm + HALO, CONV_DIM), F32)],
        compiler_params=_cparams(("arbitrary",)),
    )(xbc_raw, dt_raw, conv_w, conv_b, dt_bias)


def _conv_bwd(xbc_raw, dt_raw, dxs_a, dxs_b, dB, dC, ddt, conv_w, conv_b, dt_bias):
    T = xbc_raw.shape[0]
    tm = TOKEN_TILE
    n = T // tm
    hb = tm // HALO

    def rev(width):
        return pl.BlockSpec((tm, width), lambda i: (n - 1 - i, 0))

    def body(u_ref, up_ref, dtr_ref, dxa_ref, dxb_ref, dB_ref, dC_ref, ddt_ref, w_ref, b_ref, db_ref,
             du_ref, ddtr_ref, dw_ref, dcb_ref, ddb_ref, ext_ref, dye_ref):
        i = pl.program_id(0)

        @pl.when(i == 0)
        def _():
            dye_ref[tm:, :] = jnp.zeros((HALO, CONV_DIM), F32)
            dw_ref[...] = jnp.zeros_like(dw_ref)
            dcb_ref[...] = jnp.zeros_like(dcb_ref)
            ddb_ref[...] = jnp.zeros_like(ddb_ref)

        first = (i == n - 1).astype(F32)
        ext_ref[0:HALO, :] = up_ref[...] * (1.0 - first)
        ext_ref[HALO:, :] = u_ref[...]
        for c0 in range(0, CONV_DIM, CONV_COLS):
            cols = slice(c0, c0 + CONV_COLS)
            taps = [w_ref[k:k + 1, cols] for k in range(CONV_K)]
            bias = b_ref[:, cols]
            acc_b = jnp.zeros((HALO, CONV_COLS), F32)
            acc_w = [jnp.zeros((HALO, CONV_COLS), F32) for _ in range(CONV_K)]
            for r0 in range(0, tm, CONV_ROWS_BLK):
                rows = slice(r0, r0 + CONV_ROWS_BLK)
                us = [ext_ref[pl.ds(HALO - (CONV_K - 1) + k + r0, CONV_ROWS_BLK), cols] for k in range(CONV_K)]
                y = bias + taps[0] * us[0]
                for k in range(1, CONV_K):
                    y += taps[k] * us[k]
                s = _sigmoid(y)
                if c0 < SSD_INNER:
                    dact = dxa_ref[rows, cols] + dxb_ref[rows, cols]
                elif c0 < SSD_INNER + 512:
                    dact = dB_ref[rows, c0 - SSD_INNER:c0 - SSD_INNER + CONV_COLS]
                else:
                    dact = dC_ref[rows, c0 - SSD_INNER - 512:c0 - SSD_INNER - 512 + CONV_COLS]
                dy = dact * (s * (1.0 + y * (1.0 - s)))
                dye_ref[rows, cols] = dy
                acc_b += jnp.sum(dy.reshape(CONV_ROWS_BLK // HALO, HALO, CONV_COLS), axis=0)
                for k in range(CONV_K):
                    acc_w[k] += jnp.sum((dy * us[k]).reshape(CONV_ROWS_BLK // HALO, HALO, CONV_COLS), axis=0)
            dcb_ref[:, cols] += jnp.sum(acc_b, axis=0, keepdims=True)
            for k in range(CONV_K):
                dw_ref[k:k + 1, cols] += jnp.sum(acc_w[k], axis=0, keepdims=True)
        for c0 in range(0, CONV_DIM, CONV_COLS):
            cols = slice(c0, c0 + CONV_COLS)
            taps = [w_ref[k:k + 1, cols] for k in range(CONV_K)]
            for r0 in range(0, tm, CONV_ROWS_BLK):
                du = taps[0] * dye_ref[pl.ds(CONV_K - 1 + r0, CONV_ROWS_BLK), cols]
                for k in range(1, CONV_K):
                    du += taps[k] * dye_ref[pl.ds(CONV_K - 1 - k + r0, CONV_ROWS_BLK), cols]
                du_ref[r0:r0 + CONV_ROWS_BLK, cols] = du.astype(BF16)
        dye_ref[tm:, :] = dye_ref[0:HALO, :]
        sg = _sigmoid(dtr_ref[...] + db_ref[...])
        ddtr = ddt_ref[...] * sg
        ddtr_ref[...] = ddtr.astype(BF16)
        ddb_ref[...] += jnp.sum(ddtr, axis=0, keepdims=True)

    prev_spec = pl.BlockSpec((HALO, CONV_DIM), lambda i: (jnp.maximum((n - 1 - i) * hb - 1, 0), 0))
    return _pc(
        body, name="conv_bwd", grid=(n,),
        in_specs=[rev(CONV_DIM), prev_spec, rev(DT_PAD), rev(1024), rev(1024), rev(512), rev(512), rev(DT_PAD),
                  _const_spec((CONV_K, CONV_DIM)), _const_spec((1, CONV_DIM)), _const_spec((1, DT_PAD))],
        out_specs=[rev(CONV_DIM), rev(DT_PAD), _const_spec((HALO, CONV_DIM)), _const_spec((1, CONV_DIM)),
                   _const_spec((1, DT_PAD))],
        out_shape=[jax.ShapeDtypeStruct((T, CONV_DIM), BF16), jax.ShapeDtypeStruct((T, DT_PAD), BF16),
                   jax.ShapeDtypeStruct((HALO, CONV_DIM), F32), jax.ShapeDtypeStruct((1, CONV_DIM), F32),
                   jax.ShapeDtypeStruct((1, DT_PAD), F32)],
        scratch_shapes=[pltpu.VMEM((tm + HALO, CONV_DIM), F32), pltpu.VMEM((tm + HALO, CONV_DIM), F32)],
        compiler_params=_cparams(("arbitrary",)),
    )(xbc_raw, xbc_raw, dt_raw, dxs_a, dxs_b, dB, dC, ddt, conv_w, conv_b, dt_bias)


GROUP_LANES = SSD_HPG * SSD_HEADDIM


def _ssd_expanders():
    head = jnp.arange(DT_PAD)[:, None]
    to_wide = (jnp.arange(SSD_INNER)[None, :] // SSD_HEADDIM == head).astype(BF16)
    return to_wide, to_wide.T


def _ssd_prep(dt_ref, alog_ref, wide_ref):
    q = SSD_CHUNK
    a = -jnp.exp(alog_ref[...])
    dtv = dt_ref[...]
    la = dtv * a
    row = lax.broadcasted_iota(jnp.int32, (q, q), 0)
    col = lax.broadcasted_iota(jnp.int32, (q, q), 1)
    tri = (col <= row).astype(BF16)
    cum = _dot_split_rhs(tri, la, 3)
    cum_t = _dot_split(la, tri, 3, (((0,), (1,)), ((), ())))
    dtw = _dot_split(dtv, wide_ref[...], 2)
    cumw = _dot_split(cum, wide_ref[...], 3)
    return a, dtv, row, col, tri, cum_t, dtw, cumw, cum


def _decay(cum, cum_t, h, keep):
    return jnp.where(keep, jnp.exp(jnp.minimum(cum[:, h:h + 1] - cum_t[h:h + 1, :], 0.0)), 0.0)


def _decay_t(cum, cum_t, h, keep_t):
    return jnp.where(keep_t, jnp.exp(jnp.minimum(cum_t[h:h + 1, :] - cum[:, h:h + 1], 0.0)), 0.0)


def _ssd_fwd(xbc_act, dt, alog):
    T = xbc_act.shape[0]
    q = SSD_CHUNK
    nc = T // q
    to_wide, _ = _ssd_expanders()

    def body(xbc_ref, dt_ref, alog_ref, wide_ref, y_ref, sp_ref, st_ref, xd_ref, xde_ref):
        @pl.when(pl.program_id(0) == 0)
        def _():
            st_ref[...] = jnp.zeros_like(st_ref)

        a, dtv, row, col, tri, cum_t, dtw, cumw, segcol = _ssd_prep(dt_ref, alog_ref, wide_ref)
        clw = cumw[q - 1:q, :]
        ecw = jnp.exp(cumw)
        xd = xbc_ref[:, 0:SSD_INNER] * dtw
        xd_ref[...] = xd.astype(BF16)
        xde_ref[...] = (xd * jnp.exp(clw - cumw)).astype(BF16)
        cdw = jnp.exp(clw)
        keep = col <= row
        sp_ref[0] = st_ref[...]
        for g in range(SSD_GROUPS):
            gl = slice(GROUP_LANES * g, GROUP_LANES * (g + 1))
            bb = xbc_ref[:, 1024 + 128 * g:1152 + 128 * g].astype(BF16)
            cb = xbc_ref[:, 1536 + 128 * g:1664 + 128 * g].astype(BF16)
            gm = _dot_nt(cb, bb)
            stp = st_ref[g]
            yoff = _dot(cb, stp.astype(BF16)) * ecw[:, gl]
            for r in range(SSD_HPG):
                h = SSD_HPG * g + r
                m = (gm * _decay(segcol, cum_t, h, keep)).astype(BF16)
                y_ref[:, 64 * h:64 * h + 64] = _dot(m, xd_ref[:, 64 * h:64 * h + 64]) + yoff[:, 64 * r:64 * r + 64]
            st_ref[g] = stp * cdw[:, gl] + _dot_tn(bb, xde_ref[:, gl])

    return _pc(
        body, name="ssd_fwd", grid=(nc,),
        in_specs=[_row_spec(q, CONV_DIM), _row_spec(q, DT_PAD), _const_spec((1, DT_PAD)),
                  _const_spec(to_wide.shape)],
        out_specs=[_row_spec(q, SSD_INNER),
                   pl.BlockSpec((1, SSD_GROUPS, SSD_STATE, GROUP_LANES), lambda i: (i, 0, 0, 0))],
        out_shape=[jax.ShapeDtypeStruct((T, SSD_INNER), F32),
                   jax.ShapeDtypeStruct((nc, SSD_GROUPS, SSD_STATE, GROUP_LANES), F32)],
        scratch_shapes=[pltpu.VMEM((SSD_GROUPS, SSD_STATE, GROUP_LANES), F32), pltpu.VMEM((q, SSD_INNER), BF16),
                        pltpu.VMEM((q, SSD_INNER), BF16)],
        compiler_params=_cparams(("arbitrary",)),
    )(xbc_act, dt, alog, to_wide)


def _ssd_bwd(xbc_act, dt, alog, sprev, dy):
    T = xbc_act.shape[0]
    q = SSD_CHUNK
    nc = T // q
    to_wide, to_heads = _ssd_expanders()

    def rev(width):
        return pl.BlockSpec((q, width), lambda i: (nc - 1 - i, 0))

    def body(xbc_ref, dt_ref, alog_ref, sp_ref, dy_ref, wide_ref, heads_ref,
             dxs_ref, dB_ref, dC_ref, ddt_ref, dalog_ref, ds_ref, xd_ref, dxd_ref):
        i = pl.program_id(0)

        @pl.when(i == 0)
        def _():
            ds_ref[...] = jnp.zeros_like(ds_ref)
            dalog_ref[...] = jnp.zeros_like(dalog_ref)

        a, dtv, row, col, tri, cum_t, dtw, cumw, segcol = _ssd_prep(dt_ref, alog_ref, wide_ref)
        clw = cumw[q - 1:q, :]
        ecw = jnp.exp(cumw)
        dew = jnp.exp(clw - cumw)
        cdw = jnp.exp(clw)
        xs = xbc_ref[:, 0:SSD_INNER]
        xd = xs * dtw
        xd_ref[...] = xd.astype(BF16)
        dyv = dy_ref[...]
        dye = (dyv * ecw).astype(BF16)
        xde = (xd * dew).astype(BF16)
        keep = col <= row
        keep_t = col >= row
        rows_k = lax.broadcasted_iota(jnp.int32, (SSD_HPG * q, DT_PAD), 0) // q
        lanes_k = lax.broadcasted_iota(jnp.int32, (SSD_HPG * q, DT_PAD), 1)
        dcw_parts = []
        dcum = jnp.zeros((q, DT_PAD), F32)
        for g in range(SSD_GROUPS):
            gl = slice(GROUP_LANES * g, GROUP_LANES * (g + 1))
            bb = xbc_ref[:, 1024 + 128 * g:1152 + 128 * g].astype(BF16)
            cb = xbc_ref[:, 1536 + 128 * g:1664 + 128 * g].astype(BF16)
            gm = _dot_nt(cb, bb)
            gmt = _dot_nt(bb, cb)
            stp = sp_ref[0, g]
            dst = ds_ref[g]
            stpb = stp.astype(BF16)
            dstb = dst.astype(BF16)
            yoff = _dot(cb, stpb) * ecw[:, gl]
            dcg = _dot_nt(dye[:, gl], stpb)
            ds_ref[g] = dst * cdw[:, gl] + _dot_tn(cb, dye[:, gl])
            dlast = jnp.sum(dst * stp, axis=0, keepdims=True) * cdw[:, gl]
            dbg = _dot_nt(xde[:, gl], dstb)
            w = _dot(bb, dstb) * dew[:, gl]
            wx = w * xd[:, gl]
            dlast = dlast + jnp.sum(wx, axis=0, keepdims=True)
            dcw_parts.append(dyv[:, gl] * yoff - wx
                             + jnp.where(lax.broadcasted_iota(jnp.int32, (q, 1), 0) == q - 1, dlast, 0.0))
            dgm = jnp.zeros((q, q), F32)
            diag = []
            for r in range(SSD_HPG):
                h = SSD_HPG * g + r
                hl = slice(64 * h, 64 * h + 64)
                dyb = dy_ref[:, hl].astype(BF16)
                xdh = xd_ref[:, hl]
                dm = _dot_nt(dyb, xdh)
                dmt = _dot_nt(xdh, dyb)
                dec = _decay(segcol, cum_t, h, keep)
                mt = gmt * _decay_t(segcol, cum_t, h, keep_t)
                dgm += dm * dec
                diag.append(dm * (gm * dec) - dmt * mt)
                dxd_ref[:, hl] = _dot(mt.astype(BF16), dyb) + w[:, 64 * r:64 * r + 64]
            onehots = (lanes_k == SSD_HPG * g + rows_k).astype(BF16)
            dcum += _dot_split(jnp.concatenate(diag, axis=1), onehots, 2)
            dgb = dgm.astype(BF16)
            dC_ref[:, 128 * g:128 * g + 128] = dcg + _dot(dgb, bb)
            dB_ref[:, 128 * g:128 * g + 128] = dbg + _dot_tn(dgb, cb)
        dxd = dxd_ref[...]
        dxs_ref[...] = dxd * dtw
        dcum += _dot_split(jnp.concatenate(dcw_parts, axis=1), heads_ref[...], 2)
        dla = _dot_split_rhs(tri, dcum, 3, (((0,), (0,)), ((), ())))
        ddt_ref[...] = _dot_split(xs * dxd, heads_ref[...], 2) + dla * a
        dalog_ref[...] += jnp.sum(dla * dtv, axis=0, keepdims=True)

        @pl.when(i == nc - 1)
        def _():
            dalog_ref[...] = dalog_ref[...] * a

    st_spec = pl.BlockSpec((1, SSD_GROUPS, SSD_STATE, GROUP_LANES), lambda i: (nc - 1 - i, 0, 0, 0))
    return _pc(
        body, name="ssd_bwd", grid=(nc,),
        in_specs=[rev(CONV_DIM), rev(DT_PAD), _const_spec((1, DT_PAD)), st_spec, rev(SSD_INNER),
                  _const_spec(to_wide.shape), _const_spec(to_heads.shape)],
        out_specs=[rev(SSD_INNER), rev(512), rev(512), rev(DT_PAD), _const_spec((1, DT_PAD))],
        out_shape=[jax.ShapeDtypeStruct((T, SSD_INNER), F32), jax.ShapeDtypeStruct((T, 512), F32),
                   jax.ShapeDtypeStruct((T, 512), F32), jax.ShapeDtypeStruct((T, DT_PAD), F32),
                   jax.ShapeDtypeStruct((1, DT_PAD), F32)],
        scratch_shapes=[pltpu.VMEM((SSD_GROUPS, SSD_STATE, GROUP_LANES), F32), pltpu.VMEM((q, SSD_INNER), BF16),
                        pltpu.VMEM((q, SSD_INNER), F32)],
        compiler_params=_cparams(("arbitrary",)),
    )(xbc_act, dt, alog, sprev, dy, to_wide, to_heads)


def _s5_disc_vals(a_re, a_im, log_dt, b_re, b_im):
    dt = jnp.exp(log_dt)
    mag = jnp.exp(a_re * dt)
    ab_re = mag * jnp.cos(a_im * dt)
    ab_im = mag * jnp.sin(a_im * dt)
    den = a_re * a_re + a_im * a_im
    nr = ab_re - 1.0
    ni = ab_im
    coef_re = (nr * a_re + ni * a_im) / den
    coef_im = (ni * a_re - nr * a_im) / den
    bb_re = coef_re * b_re - coef_im * b_im
    bb_im = coef_re * b_im + coef_im * b_re
    return ab_re, ab_im, bb_re, bb_im


def _s5_disc(a_re, a_im, log_dt, b_re, b_im):
    def body(ar, ai, ld, br, bi, o1, o2, o3, o4):
        o1[...], o2[...], o3[...], o4[...] = _s5_disc_vals(ar[...], ai[...], ld[...], br[...], bi[...])

    return _pc(
        body, name="s5_disc",
        out_shape=[jax.ShapeDtypeStruct((1, S5_STATES), F32), jax.ShapeDtypeStruct((1, S5_STATES), F32),
                   jax.ShapeDtypeStruct((16, S5_STATES), F32), jax.ShapeDtypeStruct((16, S5_STATES), F32)],
    )(a_re, a_im, log_dt, b_re, b_im)


def _s5_disc_bwd(a_re, a_im, log_dt, b_re, b_im, d_ab_re, d_ab_im, d_bb_re, d_bb_im):
    def body(ar, ai, ld, br, bi, g1, g2, g3, g4, o1, o2, o3, o4, o5):
        _, vjp = jax.vjp(_s5_disc_vals, ar[...], ai[...], ld[...], br[...], bi[...])
        d1, d2, d3, d4, d5 = vjp((g1[...], g2[...], g3[...], g4[...]))
        o1[...] = d1
        o2[...] = d2
        st = lax.broadcasted_iota(jnp.int32, (S5_STATES, DT_PAD), 0)
        grp = lax.broadcasted_iota(jnp.int32, (S5_STATES, DT_PAD), 1)
        sel = (st // 64 == grp).astype(F32)
        o3[...] = _dot_hi(d3, sel)
        o4[...] = d4
        o5[...] = d5

    return _pc(
        body, name="s5_disc_bwd",
        out_shape=[jax.ShapeDtypeStruct((1, S5_STATES), F32), jax.ShapeDtypeStruct((1, S5_STATES), F32),
                   jax.ShapeDtypeStruct((1, DT_PAD), F32),
                   jax.ShapeDtypeStruct((16, S5_STATES), F32), jax.ShapeDtypeStruct((16, S5_STATES), F32)],
    )(a_re, a_im, log_dt, b_re, b_im, d_ab_re, d_ab_im, d_bb_re, d_bb_im)


def _cmul_add(xr, xi, pr, pi, yr, yi):
    return xr + pr * yr - pi * yi, xi + pr * yi + pi * yr


def _powers(ar, ai, n):
    out = [(ar, ai)]
    for _ in range(n - 1):
        pr, pi = out[-1]
        out.append((pr * pr - pi * pi, 2.0 * pr * pi))
    return out


_BW = S5_STATES // S5_BLOCKS
_BI = S5_WIDTH // S5_BLOCKS
SUB = 8
S5_ROWS = S5_CHUNK // SUB


S5_TAB_ROWS = 8 * SUB


def _scan8(br, bi, tab_ref, reverse):
    for level, k in enumerate((1, 2, 4)):
        r0 = 2 * SUB * (level + 1)
        shift = SUB - k if reverse else k
        br, bi = _cmul_add(br, bi, tab_ref[r0:r0 + SUB, :], tab_ref[r0 + SUB:r0 + 2 * SUB, :],
                           pltpu.roll(br, shift, 0), pltpu.roll(bi, shift, 0))
    return br, bi


def _s5_tables(ab_ref, tab_ref, reverse):
    rowin = lax.broadcasted_iota(jnp.int32, (SUB, 1), 0)
    ar = ab_ref[0:1, :]
    ai = -ab_ref[1:2, :] if reverse else ab_ref[1:2, :]
    zero = jnp.zeros((SUB, S5_STATES), F32)
    for level, (pr, pi) in enumerate(_powers(ar, ai, 3)):
        k = 2 ** level
        keep = (rowin < SUB - k) if reverse else (rowin >= k)
        r0 = 2 * SUB * (level + 1)
        tab_ref[r0:r0 + SUB, :] = jnp.where(keep, pr, 0.0) + zero
        tab_ref[r0 + SUB:r0 + 2 * SUB, :] = jnp.where(keep, pi, 0.0) + zero
    hit = rowin == (SUB - 1 if reverse else 0)
    pr, pi = _scan8(jnp.where(hit, ar, 0.0) + zero, jnp.where(hit, ai, 0.0) + zero, tab_ref, reverse)
    tab_ref[0:SUB, :] = pr
    tab_ref[SUB:2 * SUB, :] = pi


def _s5_fwd(u5, wb4, wc4, ab, dvec, rider=None):
    T = u5.shape[0]
    q = S5_CHUNK
    nc = T // q

    def body(u_ref, wb_ref, wc_ref, ab_ref, d_ref, y_ref, sp_ref, carry_ref, tab_ref, sr_ref, si_ref):
        i = pl.program_id(0)
        rowin = lax.broadcasted_iota(jnp.int32, (SUB, 1), 0)

        @pl.when(i == 0)
        def _():
            carry_ref[...] = jnp.zeros_like(carry_ref)
            _s5_tables(ab_ref, tab_ref, False)

        sp_ref[0] = carry_ref[...]
        for j in range(S5_BLOCKS):
            bu = _dot(u_ref[:, _BI * j:_BI * (j + 1)].astype(BF16), wb_ref[j])
            sr_ref[:, :, _BW * j:_BW * (j + 1)] = bu[:, :_BW].reshape(S5_ROWS, SUB, _BW)
            si_ref[:, :, _BW * j:_BW * (j + 1)] = bu[:, _BW:].reshape(S5_ROWS, SUB, _BW)
        tr, ti = tab_ref[0:SUB, :], tab_ref[SUB:2 * SUB, :]
        cr, ci = carry_ref[0:1, :], carry_ref[1:2, :]
        for k in range(S5_ROWS):
            sr, si = _scan8(sr_ref[k], si_ref[k], tab_ref, False)
            sr, si = _cmul_add(sr, si, tr, ti, cr, ci)
            sr_ref[k] = sr
            si_ref[k] = si
            cr, ci = sr[SUB - 1:SUB, :], si[SUB - 1:SUB, :]
        carry_ref[0:1, :] = cr
        carry_ref[1:2, :] = ci
        for j in range(S5_BLOCKS):
            sl = slice(_BW * j, _BW * (j + 1))
            ul = slice(_BI * j, _BI * (j + 1))
            s = jnp.concatenate([sr_ref[:, :, sl].reshape(q, _BW), si_ref[:, :, sl].reshape(q, _BW)], axis=1).astype(BF16)
            y_ref[:, ul] = _dot(s, wc_ref[j]) + d_ref[:, ul] * u_ref[:, ul]

    return _call(
        body, rider, name="s5_fwd", grid=(nc,),
        in_specs=[_row_spec(q, S5_WIDTH), _const_spec((S5_BLOCKS, _BI, 2 * _BW)), _const_spec((S5_BLOCKS, 2 * _BW, _BI)),
                  _const_spec((8, S5_STATES)), _const_spec((1, S5_WIDTH))],
        out_specs=[_row_spec(q, S5_WIDTH), pl.BlockSpec((1, 8, S5_STATES), lambda i: (i, 0, 0))],
        out_shape=[jax.ShapeDtypeStruct((T, S5_WIDTH), F32), jax.ShapeDtypeStruct((nc, 8, S5_STATES), F32)],
        scratch_shapes=[pltpu.VMEM((8, S5_STATES), F32), pltpu.VMEM((S5_TAB_ROWS, S5_STATES), F32),
                        pltpu.VMEM((S5_ROWS, SUB, S5_STATES), F32), pltpu.VMEM((S5_ROWS, SUB, S5_STATES), F32)],
        compiler_params=_cparams(("arbitrary",)),
    )(u5, wb4, wc4, ab, dvec)


def _s5_bwd(u5, dy5, wb4, wc4, ab, dvec, sprev, rider=None):
    T = u5.shape[0]
    q = S5_CHUNK
    nc = T // q

    def rev(width):
        return pl.BlockSpec((q, width), lambda i: (nc - 1 - i, 0))

    def body(u_ref, dy_ref, wb_ref, wc_ref, ab_ref, d_ref, sp_ref, du_ref, dwb_ref, dwc_ref, dab_ref, dd_ref,
             carry_ref, tab_ref, rtab_ref, sr_ref, si_ref, lr_ref, li_ref):
        i = pl.program_id(0)
        rowin = lax.broadcasted_iota(jnp.int32, (SUB, 1), 0)

        @pl.when(i == 0)
        def _():
            carry_ref[...] = jnp.zeros_like(carry_ref)
            dwb_ref[...] = jnp.zeros_like(dwb_ref)
            dwc_ref[...] = jnp.zeros_like(dwc_ref)
            dab_ref[...] = jnp.zeros_like(dab_ref)
            dd_ref[...] = jnp.zeros_like(dd_ref)
            _s5_tables(ab_ref, tab_ref, False)
            _s5_tables(ab_ref, rtab_ref, True)

        for j in range(S5_BLOCKS):
            sl = slice(_BW * j, _BW * (j + 1))
            ul = slice(_BI * j, _BI * (j + 1))
            bu = _dot(u_ref[:, ul].astype(BF16), wb_ref[j])
            sr_ref[:, :, sl] = bu[:, :_BW].reshape(S5_ROWS, SUB, _BW)
            si_ref[:, :, sl] = bu[:, _BW:].reshape(S5_ROWS, SUB, _BW)
            ds = _dot_nt(dy_ref[:, ul].astype(BF16), wc_ref[j])
            lr_ref[:, :, sl] = ds[:, :_BW].reshape(S5_ROWS, SUB, _BW)
            li_ref[:, :, sl] = ds[:, _BW:].reshape(S5_ROWS, SUB, _BW)
        ar, ai = ab_ref[0:1, :], ab_ref[1:2, :]
        tr, ti = tab_ref[0:SUB, :], tab_ref[SUB:2 * SUB, :]
        cr, ci = sp_ref[0, 0:1, :], sp_ref[0, 1:2, :]
        for k in range(S5_ROWS):
            sr, si = _scan8(sr_ref[k], si_ref[k], tab_ref, False)
            sr, si = _cmul_add(sr, si, tr, ti, cr, ci)
            sr_ref[k] = sr
            si_ref[k] = si
            cr, ci = sr[SUB - 1:SUB, :], si[SUB - 1:SUB, :]
        tr, ti = rtab_ref[0:SUB, :], rtab_ref[SUB:2 * SUB, :]
        cr, ci = carry_ref[0:1, :], carry_ref[1:2, :]
        acc_r = jnp.zeros((SUB, S5_STATES), F32)
        acc_i = jnp.zeros((SUB, S5_STATES), F32)
        for k in reversed(range(S5_ROWS)):
            lr, li = _scan8(lr_ref[k], li_ref[k], rtab_ref, True)
            lr, li = _cmul_add(lr, li, tr, ti, cr, ci)
            lr_ref[k] = lr
            li_ref[k] = li
            cr, ci = lr[0:1, :], li[0:1, :]
            if k > 0:
                before_r, before_i = sr_ref[k - 1, SUB - 1:SUB, :], si_ref[k - 1, SUB - 1:SUB, :]
            else:
                before_r, before_i = sp_ref[0, 0:1, :], sp_ref[0, 1:2, :]
            keep = rowin >= 1
            pr = jnp.where(keep, pltpu.roll(sr_ref[k], 1, 0), before_r)
            pi = jnp.where(keep, pltpu.roll(si_ref[k], 1, 0), before_i)
            acc_r += lr * pr + li * pi
            acc_i += li * pr - lr * pi
        carry_ref[0:1, :] = cr
        carry_ref[1:2, :] = ci
        dab_ref[0:1, :] += jnp.sum(acc_r, axis=0, keepdims=True)
        dab_ref[1:2, :] += jnp.sum(acc_i, axis=0, keepdims=True)
        for j in range(S5_BLOCKS):
            sl = slice(_BW * j, _BW * (j + 1))
            ul = slice(_BI * j, _BI * (j + 1))
            u = u_ref[:, ul]
            dy = dy_ref[:, ul]
            dyb = dy.astype(BF16)
            lam = jnp.concatenate([lr_ref[:, :, sl].reshape(q, _BW), li_ref[:, :, sl].reshape(q, _BW)], axis=1).astype(BF16)
            s = jnp.concatenate([sr_ref[:, :, sl].reshape(q, _BW), si_ref[:, :, sl].reshape(q, _BW)], axis=1).astype(BF16)
            du_ref[:, ul] = (_dot_nt(lam, wb_ref[j]) + d_ref[:, ul] * dy).astype(BF16)
            dwb_ref[j] += _dot_tn(u.astype(BF16), lam)
            dwc_ref[j] += _dot_tn(s, dyb)
            dd_ref[:, ul] += jnp.sum(dy * u, axis=0, keepdims=True)

    big = pltpu.VMEM((S5_ROWS, SUB, S5_STATES), F32)
    return _call(
        body, rider, name="s5_bwd", grid=(nc,),
        in_specs=[rev(S5_WIDTH), rev(S5_WIDTH), _const_spec((S5_BLOCKS, _BI, 2 * _BW)), _const_spec((S5_BLOCKS, 2 * _BW, _BI)),
                  _const_spec((8, S5_STATES)), _const_spec((1, S5_WIDTH)),
                  pl.BlockSpec((1, 8, S5_STATES), lambda i: (nc - 1 - i, 0, 0))],
        out_specs=[rev(S5_WIDTH), _const_spec((S5_BLOCKS, _BI, 2 * _BW)), _const_spec((S5_BLOCKS, 2 * _BW, _BI)),
                   _const_spec((8, S5_STATES)), _const_spec((1, S5_WIDTH))],
        out_shape=[jax.ShapeDtypeStruct((T, S5_WIDTH), BF16), jax.ShapeDtypeStruct((S5_BLOCKS, _BI, 2 * _BW), F32),
                   jax.ShapeDtypeStruct((S5_BLOCKS, 2 * _BW, _BI), F32), jax.ShapeDtypeStruct((8, S5_STATES), F32),
                   jax.ShapeDtypeStruct((1, S5_WIDTH), F32)],
        scratch_shapes=[pltpu.VMEM((8, S5_STATES), F32), pltpu.VMEM((S5_TAB_ROWS, S5_STATES), F32),
                        pltpu.VMEM((S5_TAB_ROWS, S5_STATES), F32), big, big, big, big],
        compiler_params=_cparams(("arbitrary",)),
    )(u5, dy5, wb4, wc4, ab, dvec, sprev)


def _merge_vals(ys, xs, z, y5, gates, dvec, gssd, glu_w, glu_b, wbr):
    sz = _sigmoid(z)
    qv = ys + dvec * xs
    pre = qv * (z * sz)
    yn, rs = [], []
    for gi in range(SSD_GROUPS):
        p, r = _rms(pre[:, 256 * gi:256 * (gi + 1)])
        yn.append(p)
        rs.append(r)
    yn = jnp.concatenate(yn, axis=1)
    ya = yn * gssd
    gel = _gelu(y5)
    sg = _sigmoid(_dot(gel.astype(BF16), glu_w) + glu_b)
    yb = gel * sg
    pa = _dot(ya.astype(BF16), wbr[0:SSD_INNER, :])
    pb = _dot(yb.astype(BF16), wbr[SSD_INNER:, :])
    s0 = _sigmoid(gates[:, :D_MODEL])
    s1 = _sigmoid(gates[:, D_MODEL:])
    merged = s0 * pa + s1 * pb
    return dict(sz=sz, qv=qv, yn=yn, rs=rs, ya=ya, gel=gel, sg=sg, yb=yb, pa=pa, pb=pb, s0=s0, s1=s1, merged=merged)


def _merge_specs(tm):
    acts = [_row_spec(tm, 1024), _row_spec(tm, 1024, 0), _row_spec(tm, 1024), _row_spec(tm, 512), _row_spec(tm, 2048),
            _row_spec(tm, 1024)]
    params = [_const_spec((1, 1024)), _const_spec((1, 1024)), _const_spec((512, 512)), _const_spec((1, 512)), _hbm_spec()]
    return acts, params


def _merge_fwd(ys, xbc_act, z, y5, gates, x, dvec, gssd, glu_w, glu_b, wa):
    T = x.shape[0]
    tm = TOKEN_TILE
    acts, params = _merge_specs(tm)

    def body(ys_ref, xs_ref, z_ref, y5_ref, gt_ref, x_ref, dv_ref, gs_ref, gw_ref, gb_ref, wa_hbm, x1_ref,
             wbr_ref, wout_ref):
        @pl.when(pl.program_id(0) == 0)
        def _():
            _load_late_weight(wa_hbm, wbr_ref, "w_branch")
            _load_late_weight(wa_hbm, wout_ref, "w_out")

        v = _merge_vals(ys_ref[...], xs_ref[...], z_ref[...], y5_ref[...], gt_ref[...], dv_ref[...], gs_ref[...],
                        gw_ref[...], gb_ref[...], wbr_ref)
        x1_ref[...] = x_ref[...] + _dot(v["merged"].astype(BF16), wout_ref[...])

    return _pc(
        body, name="merge_fwd", grid=(T // tm,),
        in_specs=acts + params, out_specs=_row_spec(tm, 1024),
        out_shape=jax.ShapeDtypeStruct((T, 1024), F32),
        scratch_shapes=[pltpu.VMEM((1536, 1024), BF16), pltpu.VMEM((1024, 1024), BF16)],
        compiler_params=_cparams(("arbitrary",)),
    )(ys, xbc_act, z, y5, gates, x, dvec, gssd, glu_w, glu_b, wa)


def _merge_bwd(ys, xbc_act, z, y5, gates, dx1, dvec, gssd, glu_w, glu_b, wa, head_sel, rider=None):
    T = dx1.shape[0]
    tm = TOKEN_TILE
    acts, params = _merge_specs(tm)

    def body(ys_ref, xs_ref, z_ref, y5_ref, gt_ref, dx1_ref, dv_ref, gs_ref, gw_ref, gb_ref, wa_hbm, hs_ref,
             dys_ref, dxs_ref, dz_ref, dy5_ref, dgt_ref, mg_ref, ya_ref, yb_ref, dpa_ref, dpb_ref, gel_ref, dpre_ref,
             ddv_ref, dgs_ref, dgb_ref, wbr_ref, wout_ref, ddacc_ref):
        i = pl.program_id(0)

        @pl.when(i == 0)
        def _():
            _load_late_weight(wa_hbm, wbr_ref, "w_branch")
            _load_late_weight(wa_hbm, wout_ref, "w_out")
            ddacc_ref[...] = jnp.zeros_like(ddacc_ref)
            dgs_ref[...] = jnp.zeros_like(dgs_ref)
            dgb_ref[...] = jnp.zeros_like(dgb_ref)

        ys, xs, z, y5, gates = ys_ref[...], xs_ref[...], z_ref[...], y5_ref[...], gt_ref[...]
        dvv, gsv, gw = dv_ref[...], gs_ref[...], gw_ref[...]
        v = _merge_vals(ys, xs, z, y5, gates, dvv, gsv, gw, gb_ref[...], wbr_ref)
        dmg = _dot_nt(dx1_ref[...].astype(BF16), wout_ref[...])
        s0, s1, pa, pb = v["s0"], v["s1"], v["pa"], v["pb"]
        dgt_ref[:, :D_MODEL] = (dmg * pa * s0 * (1.0 - s0)).astype(BF16)
        dgt_ref[:, D_MODEL:] = (dmg * pb * s1 * (1.0 - s1)).astype(BF16)
        dpa = (dmg * s0).astype(BF16)
        dpb = (dmg * s1).astype(BF16)
        dya = _dot_nt(dpa, wbr_ref[0:SSD_INNER, :])
        dyb = _dot_nt(dpb, wbr_ref[SSD_INNER:, :])
        gel, sg = v["gel"], v["sg"]
        dpre = (dyb * gel * sg * (1.0 - sg))
        dgb_ref[...] += jnp.sum(dpre, axis=0, keepdims=True)
        dpre_b = dpre.astype(BF16)
        dgel = dyb * sg + _dot_nt(dpre_b, gw)
        dy5_ref[...] = dgel * _gelu_grad(y5)
        yn = v["yn"]
        dgs_ref[...] += jnp.sum(dya * yn, axis=0, keepdims=True)
        dyn = dya * gsv
        dpre_a = jnp.concatenate(
            [_rms_bwd(yn[:, 256 * gi:256 * (gi + 1)], v["rs"][gi], dyn[:, 256 * gi:256 * (gi + 1)])
             for gi in range(SSD_GROUPS)], axis=1)
        sz, qv = v["sz"], v["qv"]
        dq = dpre_a * (z * sz)
        dz_ref[...] = (dpre_a * qv * (sz * (1.0 + z * (1.0 - sz)))).astype(BF16)
        dys_ref[...] = dq
        dxs_ref[...] = dq * dvv
        ddacc_ref[...] += jnp.sum(dq * xs, axis=0, keepdims=True)
        mg_ref[...] = v["merged"].astype(BF16)
        ya_ref[...] = v["ya"].astype(BF16)
        yb_ref[...] = v["yb"].astype(BF16)
        dpa_ref[...] = dpa
        dpb_ref[...] = dpb
        gel_ref[...] = gel.astype(BF16)
        dpre_ref[...] = dpre_b

        @pl.when(i == pl.num_programs(0) - 1)
        def _():
            ddv_ref[...] = _dot_hi(ddacc_ref[...], hs_ref[...])

    outs = [(1024, F32), (1024, F32), (1024, BF16), (512, F32), (2048, BF16),
            (1024, BF16), (1024, BF16), (512, BF16), (1024, BF16), (1024, BF16), (512, BF16), (512, BF16)]
    return _call(
        body, rider, name="merge_bwd", grid=(T // tm,),
        in_specs=acts + params + [_const_spec((1024, DT_PAD))],
        out_specs=[_row_spec(tm, w) for w, _ in outs] + [_const_spec((1, DT_PAD)), _const_spec((1, 1024)), _const_spec((1, 512))],
        out_shape=[jax.ShapeDtypeStruct((T, w), d) for w, d in outs] + [
            jax.ShapeDtypeStruct((1, DT_PAD), F32), jax.ShapeDtypeStruct((1, 1024), F32), jax.ShapeDtypeStruct((1, 512), F32)],
        scratch_shapes=[pltpu.VMEM((1536, 1024), BF16), pltpu.VMEM((1024, 1024), BF16), pltpu.VMEM((1, 1024), F32)],
        compiler_params=_cparams(("arbitrary",)),
    )(ys, xbc_act, z, y5, gates, dx1, dvec, gssd, glu_w, glu_b, wa, head_sel)


def _mlp_fwd_loss(x1, target, g, g_fin, wa):
    T = x1.shape[0]
    tm = TOKEN_TILE

    def body(x_ref, t_ref, g_ref, gf_ref, wa_hbm, dx_ref, loss_ref, dg_ref, w1_ref, w2_ref):
        @pl.when(pl.program_id(0) == 0)
        def _():
            _load_late_weight(wa_hbm, w1_ref, "w_mlp_in")
            _load_late_weight(wa_hbm, w2_ref, "w_mlp_out")
            loss_ref[...] = jnp.zeros_like(loss_ref)
            dg_ref[...] = jnp.zeros_like(dg_ref)

        xv = x_ref[...]
        xn, _ = _rms(xv)
        h = (xn * g_ref[...]).astype(BF16)
        acc = xv
        for s in range(FF_SHARDS):
            rl = jnp.maximum(_dot(h, w1_ref[s]), 0.0)
            acc += _dot((rl * rl).astype(BF16), w2_ref[FF_SHARD * s:FF_SHARD * (s + 1), :])
        yn, r = _rms(acc)
        gv = gf_ref[...]
        err = yn * gv - t_ref[...]
        loss_ref[...] += jnp.sum(err * err, axis=0, keepdims=True) * (0.5 / D_MODEL)
        dy = err * (1.0 / D_MODEL)
        dg_ref[...] += jnp.sum(dy * yn, axis=0, keepdims=True)
        dx_ref[...] = _rms_bwd(yn, r, dy * gv)

    return _pc(
        body, name="mlp_fwd_loss", grid=(T // tm,),
        in_specs=[_row_spec(tm, 1024), _row_spec(tm, 1024), _const_spec((1, 1024)), _const_spec((1, 1024)), _hbm_spec()],
        out_specs=[_row_spec(tm, 1024), _const_spec((1, 1024)), _const_spec((1, 1024))],
        out_shape=[jax.ShapeDtypeStruct((T, 1024), F32), jax.ShapeDtypeStruct((1, 1024), F32),
                   jax.ShapeDtypeStruct((1, 1024), F32)],
        scratch_shapes=[pltpu.VMEM((FF_SHARDS, D_MODEL, FF_SHARD), BF16), pltpu.VMEM((D_FF, D_MODEL), BF16)],
        compiler_params=_cparams(("arbitrary",)),
    )(x1, target, g, g_fin, wa)


def _mlp_bwd(x1, dx2, g, wa):
    T = x1.shape[0]
    tm = TOKEN_TILE

    def body(x_ref, dx2_ref, g_ref, wa_hbm, dx1_ref, h_ref, act_ref, da_ref, dg_ref, w1_ref, w2_ref):
        @pl.when(pl.program_id(0) == 0)
        def _():
            _load_late_weight(wa_hbm, w1_ref, "w_mlp_in")
            _load_late_weight(wa_hbm, w2_ref, "w_mlp_out")
            dg_ref[...] = jnp.zeros_like(dg_ref)

        xn, r = _rms(x_ref[...])
        gv = g_ref[...]
        h = (xn * gv).astype(BF16)
        h_ref[...] = h
        dx2 = dx2_ref[...]
        dx2b = dx2.astype(BF16)
        dh = jnp.zeros((tm, D_MODEL), F32)
        for s in range(FF_SHARDS):
            ff = slice(FF_SHARD * s, FF_SHARD * (s + 1))
            rl = jnp.maximum(_dot(h, w1_ref[s]), 0.0)
            act_ref[:, ff] = (rl * rl).astype(BF16)
            da = (_dot_nt(dx2b, w2_ref[ff, :]) * (2.0 * rl)).astype(BF16)
            da_ref[:, ff] = da
            dh += _dot_nt(da, w1_ref[s])
        dg_ref[...] += jnp.sum(dh * xn, axis=0, keepdims=True)
        dx1_ref[...] = dx2 + _rms_bwd(xn, r, dh * gv)

    return _pc(
        body, name="mlp_bwd", grid=(T // tm,),
        in_specs=[_row_spec(tm, 1024), _row_spec(tm, 1024), _const_spec((1, 1024)), _hbm_spec()],
        out_specs=[_row_spec(tm, 1024), _row_spec(tm, 1024), _row_spec(tm, D_FF), _row_spec(tm, D_FF), _const_spec((1, 1024))],
        out_shape=[jax.ShapeDtypeStruct((T, 1024), F32), jax.ShapeDtypeStruct((T, 1024), BF16),
                   jax.ShapeDtypeStruct((T, D_FF), BF16), jax.ShapeDtypeStruct((T, D_FF), BF16),
                   jax.ShapeDtypeStruct((1, 1024), F32)],
        scratch_shapes=[pltpu.VMEM((FF_SHARDS, D_MODEL, FF_SHARD), BF16), pltpu.VMEM((D_FF, D_MODEL), BF16)],
        compiler_params=_cparams(("arbitrary",)),
    )(x1, dx2, g, wa)


WGRAD_OUT_ELEMS = 2 * 1024 * 1024
WGRAD_TILE_BYTES = 4 * 1024 * 1024


def _wgrad(a, b, name, col_shards=None, row_shards_into=None):
    T, K = a.shape
    N = b.shape[1]
    nb = N // col_shards if col_shards else min(N, 1024, max(128, WGRAD_OUT_ELEMS // K))
    tt = min(T, WGRAD_TOKENS)
    while tt * max(K * a.dtype.itemsize, nb * b.dtype.itemsize) > WGRAD_TILE_BYTES:
        tt //= 2
    assert N % nb == 0 and T % tt == 0
    in_specs = [pl.BlockSpec((tt, K), lambda n, t: (t, 0)), pl.BlockSpec((tt, nb), lambda n, t: (t, n))]
    args, aliases = [a, b], {}
    if col_shards:
        out_spec = pl.BlockSpec((None, None, K, nb), lambda n, t: (n, 0, 0, 0))
        out_shape = jax.ShapeDtypeStruct((col_shards, 2, K, nb), F32)
    elif row_shards_into is not None:
        shards, _, rows, cols = row_shards_into.shape
        assert shards * rows == K and cols == N
        out_spec = pl.BlockSpec((shards, None, rows, nb), lambda n, t: (0, 1, 0, n))
        out_shape = jax.ShapeDtypeStruct(row_shards_into.shape, F32)
        in_specs.append(_hbm_spec())
        args.append(row_shards_into)
        aliases = {2: 0}
    else:
        out_spec = pl.BlockSpec((K, nb), lambda n, t: (0, n))
        out_shape = jax.ShapeDtypeStruct((K, N), F32)

    def body(a_ref, b_ref, *rest):
        o_ref = rest[-1]

        @pl.when(pl.program_id(1) == 0)
        def _():
            o_ref[...] = jnp.zeros_like(o_ref)

        o_ref[...] += _dot_tn(a_ref[...].astype(BF16), b_ref[...].astype(BF16)).reshape(o_ref.shape)

    return _pc(
        body, name=name, grid=(N // nb, T // tt), in_specs=in_specs, out_specs=out_spec, out_shape=out_shape,
        input_output_aliases=aliases, compiler_params=_cparams(("parallel", "arbitrary")),
    )(*args)


def _s5_block_weights(bb_re, bb_im, c_re, c_im):
    eye = jnp.eye(8, dtype=F32)
    bre = bb_re.reshape(16, S5_BLOCKS, 8, 64)
    bim = bb_im.reshape(16, S5_BLOCKS, 8, 64)
    wb_re = jnp.einsum('kjgp,gh->jhkgp', bre, eye).reshape(S5_BLOCKS, _BI, _BW)
    wb_im = jnp.einsum('kjgp,gh->jhkgp', bim, eye).reshape(S5_BLOCKS, _BI, _BW)
    wb4 = jnp.concatenate([wb_re, wb_im], axis=2).astype(BF16)
    cre = c_re.reshape(S5_BLOCKS, 8, 16, 64)
    cim = c_im.reshape(S5_BLOCKS, 8, 16, 64)
    wc_re = jnp.einsum('jgkp,gh->jgphk', cre, eye).reshape(S5_BLOCKS, _BW, _BI)
    wc_im = jnp.einsum('jgkp,gh->jgphk', -cim, eye).reshape(S5_BLOCKS, _BW, _BI)
    wc4 = jnp.concatenate([wc_re, wc_im], axis=1).astype(BF16)
    return wb4, wc4


def _s5_block_grads(dwb4, dwc4):
    eye = jnp.eye(8, dtype=F32)
    dwb = dwb4.reshape(S5_BLOCKS, 8, 16, 2, 8, 64)
    dbb = jnp.einsum('jhkrgp,gh->rkjgp', dwb, eye).reshape(2, 16, S5_STATES)
    dwc = dwc4.reshape(S5_BLOCKS, 2, 8, 64, 8, 16)
    dc = jnp.einsum('jrgphk,gh->rjgkp', dwc, eye).reshape(2, 32, 16, 64)
    return dbb[0], dbb[1], dc[0], -dc[1]


def _row(v, width=None):
    v = v.reshape(1, -1)
    if width is not None and v.shape[1] < width:
        v = jnp.concatenate([v, jnp.zeros((1, width - v.shape[1]), v.dtype)], axis=1)
    return v


def _local_step(x, target, p, comm=None):
    g_mix, g_mlp, g_fin = _row(p["norm_mix_g"]), _row(p["norm_mlp_g"]), _row(p["norm_final_g"])
    conv_b = _row(p["conv_b"])
    dt_bias = _row(p["dt_bias"], DT_PAD)
    alog = _row(p["a_log"], DT_PAD)
    dvec = _row(jnp.repeat(p["d_ssd"], SSD_HEADDIM))
    gssd = _row(p["ssd_norm_g"])
    s5d = _row(p["s5_d"])
    glu_b = _row(p["s5_glu_b"])
    head_sel = (jnp.arange(SSD_INNER)[:, None] // SSD_HEADDIM == jnp.arange(DT_PAD)[None, :]).astype(F32)

    a_re = p["s5_a_re"].reshape(1, S5_STATES)
    a_im = p["s5_a_im"].reshape(1, S5_STATES)
    log_dt = jnp.repeat(p["s5_log_dt"], 64).reshape(1, S5_STATES)
    b_re = p["s5_b_re"].reshape(S5_STATES, 16).T
    b_im = p["s5_b_im"].reshape(S5_STATES, 16).T
    ab_re, ab_im, bb_re, bb_im = _s5_disc(a_re, a_im, log_dt, b_re, b_im)
    wb4, wc4 = _s5_block_weights(bb_re, bb_im, p["s5_c_re"], p["s5_c_im"])
    ab = jnp.concatenate([ab_re, ab_im, jnp.zeros((6, S5_STATES), F32)], axis=0)

    wp = p["w_in_perm"]

    z, xbc_raw, u5, gates, dt_raw, h = _inproj_fwd(x, g_mix, wp)
    xbc_act, dt = _conv_fwd(xbc_raw, dt_raw, p["conv_w"], conv_b, dt_bias)
    ys, ssd_states = _ssd_fwd(xbc_act, dt, alog)
    if comm is None:
        y5, s5_states = _s5_fwd(u5, wb4, wc4, ab, s5d)
    else:
        (y5, s5_states), late = _s5_fwd(u5, wb4, wc4, ab, s5d, rider=_Gather(comm["late_srcs"], comm["late_ks"]))
        p = {**p, **comm["late_unpack"](late)}
    wa, glu_w = p["late_weights"], p["s5_glu_w"]
    x1 = _merge_fwd(ys, xbc_act, z, y5, gates, x, dvec, gssd, glu_w, glu_b, wa)
    dx2, loss_lanes, d_gfin = _mlp_fwd_loss(x1, target, g_mlp, g_fin, wa)

    dx1, h2, act, da1, d_gmlp = _mlp_bwd(x1, dx2, g_mlp, wa)
    g_mlp4 = _wgrad(h2, da1, "wgrad_mlp_in", col_shards=FF_SHARDS)
    g_mlp4 = _wgrad(act, dx2, "wgrad_mlp_out", row_shards_into=g_mlp4)
    d_w_mlp_in, d_w_mlp_out = g_mlp4[:, 0], g_mlp4[:, 1].reshape(D_FF, D_MODEL)
    merge_args = (ys, xbc_act, z, y5, gates, dx1, dvec, gssd, glu_w, glu_b, wa, head_sel)
    if comm is None:
        merge_out = _merge_bwd(*merge_args)
    else:
        g_mlp = g_mlp4.reshape(N_CHIPS, 2 * FF_SHARD, D_MODEL)
        merge_out, (sib_mlp,) = _merge_bwd(*merge_args, rider=_Pair([g_mlp]))
        pf_mlp, pb_mlp = _pair_sum(comm["place"], g_mlp, sib_mlp, "pair_sum_mlp")
    (dys, dxs_m, dz, dy5, dgates, mg, ya, yb, dpa, dpb, gel, dpre, d_dssd, d_gssd, d_glu_b) = merge_out
    d_w_out = _wgrad(mg, dx1, "wgrad_out")
    d_w_branch = jnp.concatenate([_wgrad(ya, dpa, "wgrad_branch_a"), _wgrad(yb, dpb, "wgrad_branch_b")], axis=0)
    d_glu_w = _wgrad(gel, dpre, "wgrad_glu")
    s5_args = (u5, dy5, wb4, wc4, ab, s5d, s5_states)
    if comm is None:
        du5, dwb4, dwc4, dab, d_s5d = _s5_bwd(*s5_args)
        mlp_total = None
    else:
        (du5, dwb4, dwc4, dab, d_s5d), (got_mlp,) = _s5_bwd(*s5_args, rider=_Chip([pb_mlp]))
        mlp_total = _chip_sum(comm["place"], pf_mlp, got_mlp, "chip_sum_mlp")
    dbb_re, dbb_im, d_c_re, d_c_im = _s5_block_grads(dwb4, dwc4)
    d_a_re, d_a_im, d_log_dt, d_b_re, d_b_im = _s5_disc_bwd(
        a_re, a_im, log_dt, b_re, b_im, dab[0:1], dab[1:2], dbb_re, dbb_im)
    dxs_s, dB, dC, ddt, d_alog = _ssd_bwd(xbc_act, dt, alog, ssd_states, dys)
    dxbc_raw, ddt_raw, d_conv_w, d_conv_b, d_dt_bias = _conv_bwd(
        xbc_raw, dt_raw, dxs_m, dxs_s, dB, dC, ddt, p["conv_w"], conv_b, dt_bias)
    d_w_in = dict(z=_wgrad(h, dz, "wgrad_in_z"), xbc=_wgrad(h, dxbc_raw, "wgrad_in_xbc"),
                  dt=_wgrad(h, ddt_raw, "wgrad_in_dt")[:, :16], u5=_wgrad(h, du5, "wgrad_in_u5"),
                  gates=_wgrad(h, dgates, "wgrad_in_gates"))
    w_in_pieces = [(c0, d_w_in[n]) for n, c0, _ in W_IN_PIECES]
    inproj_args = (x, dx1, dz, dxbc_raw, du5, dgates, ddt_raw, g_mix, wp)
    if comm is None:
        dx, d_gmix = _inproj_bwd(*inproj_args)
        late_totals = None
    else:
        g_b, g_in = _late_buffers(d_w_out, d_w_branch, d_glu_w, d_conv_w[:CONV_K], w_in_pieces)
        sib_b, sib_in = _exchange(_Pair([g_b, g_in]), "pair_exchange")
        pf_b, pb_b = _pair_sum(comm["place"], g_b, sib_b, "pair_sum_b")
        pf_in, pb_in = _pair_sum(comm["place"], g_in, sib_in, "pair_sum_in")
        (dx, d_gmix), (got_b, got_in) = _inproj_bwd(*inproj_args, rider=_Chip([pb_b, pb_in]))
        late_totals = (_chip_sum(comm["place"], pf_b, got_b, "chip_sum_b"),
                       _chip_sum(comm["place"], pf_in, got_in, "chip_sum_in"))

    grads = dict(
        norm_mix_g=d_gmix.reshape(-1), w_in_pieces=w_in_pieces, late_totals=late_totals,
        conv_w=d_conv_w[:CONV_K], conv_b=d_conv_b.reshape(-1),
        dt_bias=d_dt_bias[0, :16], a_log=d_alog[0, :16], d_ssd=d_dssd[0, :16], ssd_norm_g=d_gssd.reshape(-1),
        s5_a_re=d_a_re.reshape(32, 64), s5_a_im=d_a_im.reshape(32, 64), s5_log_dt=d_log_dt[0, :32],
        s5_b_re=d_b_re.T.reshape(32, 64, 16), s5_b_im=d_b_im.T.reshape(32, 64, 16), s5_c_re=d_c_re, s5_c_im=d_c_im,
        s5_d=d_s5d.reshape(-1), s5_glu_w=d_glu_w, s5_glu_b=d_glu_b.reshape(-1), w_branch=d_w_branch, w_out=d_w_out,
        norm_mlp_g=d_gmlp.reshape(-1), w_mlp_in=d_w_mlp_in, w_mlp_out=d_w_mlp_out, norm_final_g=d_gfin.reshape(-1),
        mlp_total=mlp_total)
    return jnp.sum(loss_lanes), dx, grads


MESH = pl.DeviceIdType.MESH
N_CHIPS = 4


def _place():
    x, y, c = lax.axis_index("x"), lax.axis_index("y"), lax.axis_index("c")
    chips = [(1 - x, y), (x, 1 - y), (1 - x, 1 - y)]
    return x, y, c, chips


def _remote(src, dst, send_sems, recv_sems, k, to):
    return pltpu.make_async_remote_copy(src_ref=src, dst_ref=dst, send_sem=send_sems.at[k], recv_sem=recv_sems.at[k],
                                        device_id=to, device_id_type=MESH)


def _row_chunks(rows, k, align):
    step = rows // k
    assert rows % k == 0 and step % align == 0, (rows, k, align)
    return [(i * step, step) for i in range(k)]


ICI_CHUNKS = 4
D2D_CHUNKS = 24


class _Gather:
    def __init__(self, srcs, ks):
        self.inputs = list(srcs)
        self.out_shapes = [jax.ShapeDtypeStruct((N_CHIPS,) + a.shape, a.dtype) for a in srcs]
        self.halves = [a.shape[0] // 2 for a in srcs]
        self.pieces = [_row_chunks(h, k, 32 // a.dtype.itemsize) for a, h, k in zip(srcs, self.halves, ks)]
        self.n_ici = 3 * sum(ks)
        self.n_sems = 2 * self.n_ici + len(srcs)

    def _plan(self, src_refs, out_refs, send_sems, recv_sems):
        x, y, c, chips = _place()
        own = 2 * x + y
        sib = (x, y, 1 - c)
        first, fwd_plan, k = [], [], 0
        for a, (src_ref, out_ref) in enumerate(zip(src_refs, out_refs)):
            h = self.halves[a]
            for r0, nr in self.pieces[a]:
                for cx, cy in chips:
                    first.append(_remote(src_ref.at[pl.ds(c * h + r0, nr), :], out_ref.at[own, pl.ds(c * h + r0, nr), :],
                                         send_sems, recv_sems, k, (cx, cy, c)))
                    fwd_plan.append((out_ref, 2 * cx + cy, h, r0, nr, k, (cx, cy, c)))
                    k += 1
        for a, (src_ref, out_ref) in enumerate(zip(src_refs, out_refs)):
            first.append(_remote(src_ref, out_ref.at[own], send_sems, recv_sems, 2 * self.n_ici + a, sib))
        return first, fwd_plan, c, sib

    def issue(self, src_refs, out_refs, send_sems, recv_sems):
        for cp in self._plan(src_refs, out_refs, send_sems, recv_sems)[0]:
            cp.start()

    def complete(self, src_refs, out_refs, send_sems, recv_sems):
        first, fwd_plan, c, sib = self._plan(src_refs, out_refs, send_sems, recv_sems)
        passed = []
        for out_ref, s, h, r0, nr, k, frm in fwd_plan:
            got = out_ref.at[s, pl.ds(c * h + r0, nr), :]
            _remote(got, got, send_sems, recv_sems, k, frm).wait_recv()
            fw = _remote(got, got, send_sems, recv_sems, self.n_ici + k, sib)
            fw.start()
            passed.append(fw)
        for out_ref, s, h, r0, nr, k, frm in fwd_plan:
            got = out_ref.at[s, pl.ds((1 - c) * h + r0, nr), :]
            _remote(got, got, send_sems, recv_sems, self.n_ici + k, sib).wait_recv()
        own_copies = first[self.n_ici:]
        for cp in own_copies:
            cp.wait_recv()
        for cp in first + passed:
            cp.wait_send()


def _exchange(rider, name):
    ri, ro = len(rider.inputs), len(rider.out_shapes)

    def body(*refs):
        rider.issue(refs[:ri], refs[ri:ri + ro], *refs[ri + ro:])
        rider.complete(refs[:ri], refs[ri:ri + ro], *refs[ri + ro:])

    return _pc(
        body, name=name, in_specs=[_hbm_spec()] * ri, out_specs=[_hbm_spec()] * ro, out_shape=list(rider.out_shapes),
        scratch_shapes=[pltpu.SemaphoreType.DMA((rider.n_sems,))] * 2,
    )(*rider.inputs)


def _call(body, rider=None, **kw):
    if rider is None:
        return _pc(body, **kw)
    single = not isinstance(kw["out_shape"], (list, tuple))
    out_specs = [kw["out_specs"]] if single else list(kw["out_specs"])
    out_shape = [kw["out_shape"]] if single else list(kw["out_shape"])
    scratch = list(kw.get("scratch_shapes", ()))
    n_in, n_out, n_scr = len(kw["in_specs"]), len(out_specs), len(scratch)
    ri, ro = len(rider.inputs), len(rider.out_shapes)
    steps = kw["grid"][0]

    def wrapped(*refs):
        o0 = n_in + ri
        s0 = o0 + n_out + ro
        r_in, r_out, sems = refs[n_in:o0], refs[o0 + n_out:s0], refs[s0 + n_scr:]

        @pl.when(pl.program_id(0) == 0)
        def _():
            rider.issue(r_in, r_out, *sems)

        body(*refs[:n_in], *refs[o0:o0 + n_out], *refs[s0:s0 + n_scr])

        @pl.when(pl.program_id(0) == steps - 1)
        def _():
            rider.complete(r_in, r_out, *sems)

    f = _pc(wrapped, name=kw["name"], grid=kw["grid"], in_specs=list(kw["in_specs"]) + [_hbm_spec()] * ri,
            out_specs=out_specs + [_hbm_spec()] * ro, out_shape=out_shape + list(rider.out_shapes),
            scratch_shapes=scratch + [pltpu.SemaphoreType.DMA((rider.n_sems,))] * 2, compiler_params=kw["compiler_params"])

    def run(*args):
        res = f(*args, *rider.inputs)
        return (res[0] if single else res[:n_out]), res[n_out:]

    return run


def _d2d_pieces(rows):
    k = next(k for k in range(24, 0, -1) if rows % k == 0 and (rows // k) % 8 == 0)
    return _row_chunks(rows, k, 8)


class _Pair:
    def __init__(self, gs, small=None):
        self.n = len(gs)
        self.halves = [g.shape[1] // 2 for g in gs]
        self.inputs = list(gs) + ([small] if small is not None else [])
        self.out_shapes = [jax.ShapeDtypeStruct((N_CHIPS, h, g.shape[2]), F32) for g, h in zip(gs, self.halves)]
        if small is not None:
            self.out_shapes.append(jax.ShapeDtypeStruct(small.shape, F32))
        self.n_sems = len(self.inputs)

    def issue(self, in_refs, out_refs, send_sems, recv_sems):
        x, y, c, _ = _place()
        sib = (x, y, 1 - c)
        for a in range(self.n):
            h = self.halves[a]
            for s in range(N_CHIPS):
                for r0, nr in _d2d_pieces(h):
                    _remote(in_refs[a].at[s, pl.ds((1 - c) * h + r0, nr), :], out_refs[a].at[s, pl.ds(r0, nr), :],
                            send_sems, recv_sems, a, sib).start()
        for a in range(self.n, len(self.inputs)):
            _remote(in_refs[a], out_refs[a], send_sems, recv_sems, a, sib).start()

    def complete(self, in_refs, out_refs, send_sems, recv_sems):
        x, y, c, _ = _place()
        for a in range(len(self.inputs)):
            _remote(out_refs[a], out_refs[a], send_sems, recv_sems, a, (x, y, 1 - c)).wait()


SUM_BLOCKS = 4


def _pair_sum(place, g, sib, name):
    n, R, C = g.shape
    H = R // 2
    rb = H // SUM_BLOCKS
    assert H % SUM_BLOCKS == 0 and rb % 16 == 0

    def body(place_ref, a_ref, b_ref, pf_ref, pb_ref):
        p = a_ref[...] + b_ref[...]
        pf_ref[...] = p
        pb_ref[...] = p.astype(BF16)

    blk = pl.BlockSpec((1, rb, C), lambda s, i, pr: (s, i, 0))
    mine = pl.BlockSpec((1, rb, C), lambda s, i, pr: (s, pr[1] * SUM_BLOCKS + i, 0))
    return _pc(
        body, name=name, out_shape=[jax.ShapeDtypeStruct((n, H, C), F32), jax.ShapeDtypeStruct((n, H, C), BF16)],
        grid_spec=pltpu.PrefetchScalarGridSpec(num_scalar_prefetch=1, grid=(n, SUM_BLOCKS), in_specs=[mine, blk],
                                               out_specs=[blk, blk]),
        compiler_params=_cparams(("arbitrary", "arbitrary")),
    )(place, g, sib)


class _Chip:
    def __init__(self, pbs, psmall=None):
        self.n = len(pbs)
        self.rows = [pb.shape[1] for pb in pbs]
        self.inputs = list(pbs) + ([psmall] if psmall is not None else [])
        self.out_shapes = [jax.ShapeDtypeStruct((3,) + pb.shape[1:], BF16) for pb in pbs]
        if psmall is not None:
            self.out_shapes.append(jax.ShapeDtypeStruct((N_CHIPS,) + psmall.shape, F32))
        self.n_sems = 3 * len(self.inputs)

    def issue(self, in_refs, out_refs, send_sems, recv_sems):
        x, y, c, chips = _place()
        own = 2 * x + y
        for j, (cx, cy) in enumerate(chips):
            for a in range(self.n):
                for r0, nr in _row_chunks(self.rows[a], ICI_CHUNKS, 16):
                    _remote(in_refs[a].at[2 * cx + cy, pl.ds(r0, nr), :], out_refs[a].at[j, pl.ds(r0, nr), :],
                            send_sems, recv_sems, 3 * a + j, (cx, cy, c)).start()
            for a in range(self.n, len(self.inputs)):
                _remote(in_refs[a], out_refs[a].at[own], send_sems, recv_sems, 3 * a + j, (cx, cy, c)).start()

    def complete(self, in_refs, out_refs, send_sems, recv_sems):
        x, y, c, chips = _place()
        own = 2 * x + y
        for j, (cx, cy) in enumerate(chips):
            for a in range(self.n):
                _remote(in_refs[a].at[own], out_refs[a].at[j], send_sems, recv_sems, 3 * a + j, (cx, cy, c)).wait()
            for a in range(self.n, len(self.inputs)):
                _remote(in_refs[a], out_refs[a].at[2 * cx + cy], send_sems, recv_sems, 3 * a + j, (cx, cy, c)).wait()


def _chip_sum(place, pf, got, name):
    _, H, C = pf.shape
    rb = H // SUM_BLOCKS

    def body(place_ref, o_ref, g_ref, tot_ref):
        tot_ref[...] = ((o_ref[0] + g_ref[0].astype(F32)) + g_ref[1].astype(F32)) + g_ref[2].astype(F32)

    ins = [pl.BlockSpec((1, rb, C), lambda i, pr: (pr[0], i, 0)), pl.BlockSpec((3, rb, C), lambda i, pr: (0, i, 0))]
    out = pl.BlockSpec((rb, C), lambda i, pr: (pr[1] * SUM_BLOCKS + i, 0))
    return _pc(
        body, name=name, out_shape=jax.ShapeDtypeStruct((2 * H, C), F32),
        grid_spec=pltpu.PrefetchScalarGridSpec(num_scalar_prefetch=1, grid=(SUM_BLOCKS,), in_specs=ins, out_specs=out),
        compiler_params=_cparams(("arbitrary",)),
    )(place, pf, got)


def _half_exchange(fulls):
    n = len(fulls)

    def body(*refs):
        in_refs, out_refs = refs[:n], refs[n:2 * n]
        send_sems, recv_sems = refs[2 * n:]
        x, y, c, _ = _place()
        sib = (x, y, 1 - c)
        for a in range(n):
            h = fulls[a].shape[0] // 2
            for r0, nr in _d2d_pieces(h):
                rows = pl.ds(c * h + r0, nr)
                _remote(in_refs[a].at[rows, :], out_refs[a].at[rows, :], send_sems, recv_sems, a, sib).start()
        for a in range(n):
            h = fulls[a].shape[0] // 2
            _remote(in_refs[a].at[pl.ds(c * h, h), :], out_refs[a].at[pl.ds((1 - c) * h, h), :], send_sems, recv_sems, a,
                    sib).wait()

    return _pc(
        body, name="half_exchange", in_specs=[_hbm_spec()] * n, out_specs=[_hbm_spec()] * n,
        out_shape=[jax.ShapeDtypeStruct(f.shape, F32) for f in fulls],
        input_output_aliases={a: a for a in range(n)},
        scratch_shapes=[pltpu.SemaphoreType.DMA((n,)), pltpu.SemaphoreType.DMA((n,))],
    )(*fulls)


def _small_allreduce(pack):
    R, C = pack.shape

    def body(p_ref, o_ref, sib_ref, pair_ref, slots_ref, send_sems, recv_sems):
        x, y, c, chips = _place()
        own = 2 * x + y
        cp = _remote(p_ref, sib_ref, send_sems, recv_sems, 0, (x, y, 1 - c))
        cp.start()
        cp.wait()
        pair_ref[...] = p_ref[...] + sib_ref[...]
        slots_ref[own] = pair_ref[...]
        out = [_remote(pair_ref, slots_ref.at[own], send_sems, recv_sems, 1 + j, (cx, cy, c)) for j, (cx, cy) in enumerate(chips)]
        for cp in out:
            cp.start()
        for j, (cx, cy) in enumerate(chips):
            _remote(pair_ref, slots_ref.at[2 * cx + cy], send_sems, recv_sems, 1 + j, (cx, cy, c)).wait()
        o_ref[...] = ((slots_ref[0] + slots_ref[1]) + slots_ref[2]) + slots_ref[3]

    vmem = pl.BlockSpec(memory_space=pltpu.VMEM)
    return _pc(
        body, name="small_allreduce", in_specs=[vmem], out_specs=vmem, out_shape=jax.ShapeDtypeStruct((R, C), F32),
        scratch_shapes=[pltpu.VMEM((R, C), F32), pltpu.VMEM((R, C), F32), pltpu.VMEM((N_CHIPS, R, C), F32),
                        pltpu.SemaphoreType.DMA((4,)), pltpu.SemaphoreType.DMA((4,))],
    )(pack)


def _adamw(w, g, m, v, name, g_row0=0, with_grad=False, col_block=None):
    R, C = w.shape
    rb = 256 if R % 256 == 0 else (128 if R % 128 == 0 else R)
    if col_block:
        rb = R
    assert g_row0 % rb == 0

    def body(w_ref, g_ref, m_ref, v_ref, d_ref, nm_ref, nv_ref, *g_out):
        gv = g_ref[...]
        m2 = ADAM_B1 * m_ref[...] + (1.0 - ADAM_B1) * gv
        v2 = ADAM_B2 * v_ref[...] + (1.0 - ADAM_B2) * (gv * gv)
        m_hat = m2 * (1.0 / (1.0 - ADAM_B1 ** ADAM_STEP))
        v_hat = v2 * (1.0 / (1.0 - ADAM_B2 ** ADAM_STEP))
        d_ref[...] = -ADAM_LR * (m_hat / (jnp.sqrt(v_hat) + ADAM_EPS) + ADAM_WD * w_ref[...])
        nm_ref[...] = m2
        nv_ref[...] = v2
        if with_grad:
            g_out[0][...] = gv

    if col_block:
        spec = g_spec = pl.BlockSpec((R, col_block), lambda i: (0, i))
        steps = C // col_block
    else:
        spec = pl.BlockSpec((rb, C), lambda i: (i, 0))
        g_spec = pl.BlockSpec((rb, C), lambda i: (g_row0 // rb + i, 0))
        steps = R // rb
    n_out = 4 if with_grad else 3
    return _pc(
        body, name=name, grid=(steps,), in_specs=[spec, g_spec, spec, spec], out_specs=[spec] * n_out,
        out_shape=[jax.ShapeDtypeStruct((R, C), F32)] * n_out, compiler_params=_cparams(("parallel",)),
    )(w, g, m, v)


PACK_COLS = 1024
ROWS_A = (("w_mlp_in", 0, 1024), ("w_mlp_out", 1024, 1024), ("w_out", 2048, 256), ("w_branch", 2304, 384))
ROWS_A_TOTAL = 2688
ROWS_B = (("w_out", 0, 256), ("w_branch", 256, 384))
ROW_B_GLU, ROW_B_CONV, ROWS_B_TOTAL = 640, 704, 768
W_IN_SHARD = 1412
CONV_PAD_ROWS = 16
SMALL = (("norm_mix_g", (1024,)), ("conv_b", (2048,)), ("dt_bias", (16,)), ("a_log", (16,)), ("d_ssd", (16,)),
         ("ssd_norm_g", (1024,)), ("s5_a_re", (32, 64)), ("s5_a_im", (32, 64)), ("s5_log_dt", (32,)),
         ("s5_b_re", (32, 64, 16)), ("s5_b_im", (32, 64, 16)), ("s5_c_re", (32, 16, 64)), ("s5_c_im", (32, 16, 64)),
         ("s5_d", (512,)), ("s5_glu_b", (512,)), ("norm_mlp_g", (1024,)), ("norm_final_g", (1024,)))
SMALL_ROWS = 144
SMALL_COUNT = sum(math.prod(shp) for _, shp in SMALL)
GLU_ROWS = S5_WIDTH * S5_WIDTH // PACK_COLS
CONV_ROWS = CONV_K * CONV_DIM // PACK_COLS
W_IN_PIECES = (("z", 0, 1024), ("xbc", 1024, 2048), ("dt", OFF_DT, 16), ("u5", OFF_U, 512), ("gates", 3600, 2048))


def _pack_small(parts):
    flat = jnp.concatenate([a.astype(F32).reshape(-1) for a in parts])
    return jnp.concatenate([flat, jnp.zeros((SMALL_ROWS * PACK_COLS - flat.shape[0],), F32)]).reshape(SMALL_ROWS, PACK_COLS)


def _unpack_small(pack):
    flat, out, r = pack.reshape(-1), {}, 0
    for name, shp in SMALL:
        n = math.prod(shp)
        out[name] = flat[r:r + n].reshape(shp)
        r += n
    return out


def _late_buffers(d_w_out, d_w_branch, d_glu_w, d_conv_w, w_in_pieces):
    conv4 = d_conv_w.reshape(CONV_K, N_CHIPS, 512).transpose(1, 0, 2).reshape(N_CHIPS, CONV_ROWS // N_CHIPS, PACK_COLS)
    g_b = jnp.concatenate(
        [d_w_out.reshape(N_CHIPS, -1, PACK_COLS), d_w_branch.reshape(N_CHIPS, -1, PACK_COLS),
         d_glu_w.reshape(N_CHIPS, GLU_ROWS // N_CHIPS, PACK_COLS),
         jnp.pad(conv4, ((0, 0), (0, ROWS_B_TOTAL - ROW_B_CONV - CONV_ROWS // N_CHIPS), (0, 0)))], axis=1)
    full = jnp.concatenate([a for _, a in w_in_pieces], axis=1)
    g_in = full.reshape(D_MODEL, N_CHIPS, W_IN_SHARD).transpose(1, 0, 2)
    return g_b, g_in


def kernel(x, norm_mix_g, w_in, conv_w, conv_b, dt_bias, a_log, d_ssd, ssd_norm_g, s5_a_re, s5_a_im, s5_log_dt, s5_b_re, s5_b_im, s5_c_re, s5_c_im, s5_d, s5_glu_w, s5_glu_b, w_branch, w_out, norm_mlp_g, w_mlp_in, w_mlp_out, norm_final_g, loss_target, m_norm_mix_g, m_w_in, m_conv_w, m_conv_b, m_dt_bias, m_a_log, m_d_ssd, m_ssd_norm_g, m_s5_a_re, m_s5_a_im, m_s5_log_dt, m_s5_b_re, m_s5_b_im, m_s5_c_re, m_s5_c_im, m_s5_d, m_s5_glu_w, m_s5_glu_b, m_w_branch, m_w_out, m_norm_mlp_g, m_w_mlp_in, m_w_mlp_out, m_norm_final_g, v_norm_mix_g, v_w_in, v_conv_w, v_conv_b, v_dt_bias, v_a_log, v_d_ssd, v_ssd_norm_g, v_s5_a_re, v_s5_a_im, v_s5_log_dt, v_s5_b_re, v_s5_b_im, v_s5_c_re, v_s5_c_im, v_s5_d, v_s5_glu_w, v_s5_glu_b, v_w_branch, v_w_out, v_norm_mlp_g, v_w_mlp_in, v_w_mlp_out, v_norm_final_g):
    names = ("norm_mix_g", "w_in", "conv_w", "conv_b", "dt_bias", "a_log", "d_ssd", "ssd_norm_g", "s5_a_re", "s5_a_im",
             "s5_log_dt", "s5_b_re", "s5_b_im", "s5_c_re", "s5_c_im", "s5_d", "s5_glu_w", "s5_glu_b", "w_branch", "w_out",
             "norm_mlp_g", "w_mlp_in", "w_mlp_out", "norm_final_g")
    w = dict(zip(names, (norm_mix_g, w_in, conv_w, conv_b, dt_bias, a_log, d_ssd, ssd_norm_g, s5_a_re, s5_a_im, s5_log_dt,
                         s5_b_re, s5_b_im, s5_c_re, s5_c_im, s5_d, s5_glu_w, s5_glu_b, w_branch, w_out, norm_mlp_g,
                         w_mlp_in, w_mlp_out, norm_final_g)))
    m = dict(zip(names, (m_norm_mix_g, m_w_in, m_conv_w, m_conv_b, m_dt_bias, m_a_log, m_d_ssd, m_ssd_norm_g, m_s5_a_re,
                         m_s5_a_im, m_s5_log_dt, m_s5_b_re, m_s5_b_im, m_s5_c_re, m_s5_c_im, m_s5_d, m_s5_glu_w,
                         m_s5_glu_b, m_w_branch, m_w_out, m_norm_mlp_g, m_w_mlp_in, m_w_mlp_out, m_norm_final_g)))
    v = dict(zip(names, (v_norm_mix_g, v_w_in, v_conv_w, v_conv_b, v_dt_bias, v_a_log, v_d_ssd, v_ssd_norm_g, v_s5_a_re,
                         v_s5_a_im, v_s5_log_dt, v_s5_b_re, v_s5_b_im, v_s5_c_re, v_s5_c_im, v_s5_d, v_s5_glu_w,
                         v_s5_glu_b, v_w_branch, v_w_out, v_norm_mlp_g, v_w_mlp_in, v_w_mlp_out, v_norm_final_g)))

    cx, cy, cc = lax.axis_index("x"), lax.axis_index("y"), lax.axis_index("c")
    own = 2 * cx + cy
    place = jnp.stack([own, cc]).astype(jnp.int32)

    src_conv = jnp.concatenate([conv_w, jnp.zeros((CONV_PAD_ROWS - CONV_K, 512), F32)], axis=0)
    all_in, all_conv = _exchange(_Gather([w_in.astype(BF16), src_conv], [ICI_CHUNKS, 1]), "gather_first")
    p = {n: w[n] for n, _ in SMALL}
    p["conv_w"] = jnp.concatenate([all_conv[s, :CONV_K] for s in range(N_CHIPS)], axis=1)
    w_in_full = all_in.transpose(1, 0, 2).reshape(D_MODEL, D_IN_PROJ)
    p["w_in_perm"] = jnp.concatenate([w_in_full[:, :OFF_DT], w_in_full[:, OFF_U:], w_in_full[:, OFF_DT:OFF_U],
                                      jnp.zeros((D_MODEL, DT_PAD - 16), BF16)], axis=1)

    def late_unpack(gathered):
        all_a, all_glu = gathered
        return {"late_weights": all_a, "s5_glu_w": all_glu.reshape(S5_WIDTH, S5_WIDTH)}

    comm = dict(place=place, late_ks=[ICI_CHUNKS, 1], late_unpack=late_unpack,
                late_srcs=[jnp.concatenate([w[n].astype(BF16) for n, _, _ in ROWS_A], axis=0), s5_glu_w.astype(BF16)])
    loss_part, grad_x, g = _local_step(x[0], loss_target[0], p, comm)

    red_mlp, red_b, red_in = _half_exchange([g["mlp_total"], *g["late_totals"]])
    small_tot = _small_allreduce(_pack_small([g[n] for n, _ in SMALL] + [loss_part.reshape(1)]))
    loss = small_tot.reshape(-1)[SMALL_COUNT]

    grads = _unpack_small(small_tot)
    delta, new_m, new_v = {}, {}, {}
    for n, r0, _ in ROWS_A[:2]:
        delta[n], new_m[n], new_v[n], grads[n] = _adamw(w[n], red_mlp, m[n], v[n], "adamw_" + n, g_row0=r0, with_grad=True)
    for n, r0, _ in ROWS_B:
        delta[n], new_m[n], new_v[n], grads[n] = _adamw(w[n], red_b, m[n], v[n], "adamw_" + n, g_row0=r0, with_grad=True)
    d_t, m_t, v_t, g_t = _adamw(w_in.T, red_in.T, m_w_in.T, v_w_in.T, "adamw_w_in", with_grad=True, col_block=128)
    delta["w_in"], new_m["w_in"], new_v["w_in"], grads["w_in"] = d_t.T, m_t.T, v_t.T, g_t.T
    grads["s5_glu_w"] = red_b[ROW_B_GLU:ROW_B_GLU + GLU_ROWS // N_CHIPS].reshape(S5_WIDTH // N_CHIPS, S5_WIDTH)
    grads["conv_w"] = red_b[ROW_B_CONV:ROW_B_CONV + CONV_ROWS // N_CHIPS].reshape(CONV_K, CONV_DIM // N_CHIPS)
    for n in ("s5_glu_w", "conv_w"):
        delta[n], new_m[n], new_v[n] = _adamw(w[n], grads[n], m[n], v[n], "adamw_" + n)
    ds, ms, vs = _adamw(_pack_small([w[n] for n, _ in SMALL]), small_tot, _pack_small([m[n] for n, _ in SMALL]),
                        _pack_small([v[n] for n, _ in SMALL]), "adamw_small")
    delta.update(_unpack_small(ds))
    new_m.update(_unpack_small(ms))
    new_v.update(_unpack_small(vs))

    return (loss, grad_x[None], *[grads[n] for n in names], *[delta[n] for n in names],
            *[new_m[n] for n in names], *[new_v[n] for n in names])
```

```python
import functools
import math

import jax
import jax.numpy as jnp
from jax import lax
from jax.experimental import pallas as pl
from jax.experimental.pallas import tpu as pltpu

F32 = jnp.float32
BF16 = jnp.bfloat16

D_MODEL = 1024
SSD_INNER = 1024
SSD_HEADS = 16
SSD_HEADDIM = 64
SSD_GROUPS = 4
SSD_HPG = 4
SSD_STATE = 128
SSD_CHUNK = 128
CONV_K = 4
CONV_DIM = 2048
S5_WIDTH = 512
S5_STATES = 2048
S5_BLOCKS = 4
S5_CHUNK = 128
D_FF = 4096
FF_SHARDS = 4
FF_SHARD = D_FF // FF_SHARDS
EPS = 1e-6
P_Z, P_XBC, P_U5, P_G, P_DT, P_END = 0, 1024, 3072, 3584, 5632, 5760
DT_PAD = 128
OFF_DT, OFF_U = 3072, 3088
D_IN_PROJ = 5648

ADAM_LR, ADAM_B1, ADAM_B2, ADAM_EPS, ADAM_WD, ADAM_STEP = 0.001, 0.9, 0.999, 1e-08, 0.01, 10

TOKEN_TILE = 256
VMEM_LIMIT = 56 * 1024 * 1024
HALO = 8
INPROJ_PIECE = 256
CONV_COLS = 256
CONV_ROWS_BLK = 64
WGRAD_TOKENS = 2048


def _pc(body, **kw):
    return pl.pallas_call(body, **kw)


def _cparams(sem=None):
    return pltpu.CompilerParams(dimension_semantics=sem, vmem_limit_bytes=VMEM_LIMIT)


def _dot(a, b):
    return jnp.dot(a, b, preferred_element_type=F32)


def _dot_nt(a, b):
    return lax.dot_general(a, b, (((1,), (1,)), ((), ())), preferred_element_type=F32)


def _dot_tn(a, b):
    return lax.dot_general(a, b, (((0,), (0,)), ((), ())), preferred_element_type=F32)


def _dot_hi(a, b, dims=(((1,), (0,)), ((), ()))):
    return lax.dot_general(a, b, dims, preferred_element_type=F32, precision=lax.Precision.HIGHEST)


def _split_bf16(x, terms):
    out = []
    for _ in range(terms - 1):
        t = x.astype(BF16)
        out.append(t)
        x = x - t.astype(F32)
    out.append(x.astype(BF16))
    return out


def _dot_split(x, onehots, terms, dims=(((1,), (0,)), ((), ()))):
    acc = None
    for t in _split_bf16(x, terms):
        p = lax.dot_general(t, onehots, dims, preferred_element_type=F32)
        acc = p if acc is None else acc + p
    return acc


def _dot_split_rhs(onehots, x, terms, dims=(((1,), (0,)), ((), ()))):
    acc = None
    for t in _split_bf16(x, terms):
        p = lax.dot_general(onehots, t, dims, preferred_element_type=F32)
        acc = p if acc is None else acc + p
    return acc


def _sigmoid(x):
    return 0.5 * jnp.tanh(0.5 * x) + 0.5


def _softplus(x):
    return jnp.maximum(x, 0.0) + jnp.log(1.0 + jnp.exp(-jnp.abs(x)))


_GELU_C = math.sqrt(2.0 / math.pi)


def _gelu(x):
    return 0.5 * x * (1.0 + jnp.tanh(_GELU_C * (x + 0.044715 * x * x * x)))


def _gelu_grad(x):
    t = jnp.tanh(_GELU_C * (x + 0.044715 * x * x * x))
    return 0.5 * (1.0 + t) + 0.5 * x * (1.0 - t * t) * _GELU_C * (1.0 + 3.0 * 0.044715 * x * x)


def _rms(x):
    r = lax.rsqrt(jnp.mean(x * x, axis=-1, keepdims=True) + EPS)
    return x * r, r


def _rms_bwd(xn, r, dxn):
    return r * (dxn - xn * jnp.mean(dxn * xn, axis=-1, keepdims=True))


def _row_spec(tm, width, col=0):
    return pl.BlockSpec((tm, width), lambda i: (i, col))


def _const_spec(shape):
    nd = len(shape)
    return pl.BlockSpec(shape, lambda i: (0,) * nd)


def _hbm_spec():
    return pl.BlockSpec(memory_space=pl.ANY)


def _load_late_weight(wa_hbm, dst_ref, name):
    r0, nr = next((r0, nr) for n, r0, nr in ROWS_A if n == name)
    for s in range(N_CHIPS):
        dst = dst_ref.at[s] if len(dst_ref.shape) == 3 else dst_ref.at[pl.ds(nr * s, nr), :]
        pltpu.sync_copy(wa_hbm.at[s, pl.ds(r0, nr), :], dst)


def _inproj_bwd(x, dx1, dz, dxbc, du5, dgt, ddt, g, wp, rider=None):
    T = x.shape[0]
    tm = TOKEN_TILE

    def body(x_ref, dx1_ref, dz_ref, dxbc_ref, du5_ref, dgt_ref, ddt_ref, g_ref, w_hbm, dx_ref, dg_ref, w_ref):
        @pl.when(pl.program_id(0) == 0)
        def _():
            pltpu.sync_copy(w_hbm, w_ref)
            dg_ref[...] = jnp.zeros_like(dg_ref)

        xn, r = _rms(x_ref[...])
        gv = g_ref[...]
        dh = _dot_nt(dz_ref[...].astype(BF16), w_ref[:, P_Z:P_XBC])
        dh += _dot_nt(dxbc_ref[...].astype(BF16), w_ref[:, P_XBC:P_U5])
        dh += _dot_nt(du5_ref[...].astype(BF16), w_ref[:, P_U5:P_G])
        dh += _dot_nt(dgt_ref[...].astype(BF16), w_ref[:, P_G:P_DT])
        dh += _dot_nt(ddt_ref[...].astype(BF16), w_ref[:, P_DT:P_END])
        dg_ref[...] += jnp.sum(dh * xn, axis=0, keepdims=True)
        dx_ref[...] = dx1_ref[...] + _rms_bwd(xn, r, dh * gv)

    return _call(
        body, rider, name="inproj_bwd", grid=(T // tm,),
        in_specs=[_row_spec(tm, 1024), _row_spec(tm, 1024), _row_spec(tm, 1024), _row_spec(tm, 2048),
                  _row_spec(tm, 512), _row_spec(tm, 2048), _row_spec(tm, DT_PAD), _const_spec((1, 1024)), _hbm_spec()],
        out_specs=[_row_spec(tm, 1024), _const_spec((1, 1024))],
        out_shape=[jax.ShapeDtypeStruct((T, 1024), F32), jax.ShapeDtypeStruct((1, 1024), F32)],
        scratch_shapes=[pltpu.VMEM((D_MODEL, P_END), BF16)],
        compiler_params=_cparams(("arbitrary",)),
    )(x, dx1, dz, dxbc, du5, dgt, ddt, g, wp)


def _conv_fwd(xbc_raw, dt_raw, conv_w, conv_b, dt_bias):
    T = xbc_raw.shape[0]
    tm = TOKEN_TILE

    def body(u_ref, dtr_ref, w_ref, b_ref, db_ref, act_ref, dt_ref, ext_ref):
        @pl.when(pl.program_id(0) == 0)
        def _():
            ext_ref[0:HALO, :] = jnp.zeros((HALO, CONV_DIM), F32)

        ext_ref[HALO:, :] = u_ref[...]
        for c0 in range(0, CONV_DIM, CONV_COLS):
            cols = slice(c0, c0 + CONV_COLS)
            taps = [w_ref[k:k + 1, cols] for k in range(CONV_K)]
            bias = b_ref[:, cols]
            for r0 in range(0, tm, CONV_ROWS_BLK):
                y = bias + taps[0] * ext_ref[pl.ds(HALO - (CONV_K - 1) + r0, CONV_ROWS_BLK), cols]
                for k in range(1, CONV_K):
                    y += taps[k] * ext_ref[pl.ds(HALO - (CONV_K - 1) + k + r0, CONV_ROWS_BLK), cols]
                act_ref[r0:r0 + CONV_ROWS_BLK, cols] = y * _sigmoid(y)
        ext_ref[0:HALO, :] = u_ref[tm - HALO:tm, :]
        dt_ref[...] = _softplus(dtr_ref[...] + db_ref[...])

    return _pc(
        body, name="conv_fwd", grid=(T // tm,),
        in_specs=[_row_spec(tm, CONV_DIM), _row_spec(tm, DT_PAD), _const_spec((CONV_K, CONV_DIM)),
                  _const_spec((1, CONV_DIM)), _const_spec((1, DT_PAD))],
        out_specs=[_row_spec(tm, CONV_DIM), _row_spec(tm, DT_PAD)],
        out_shape=[jax.ShapeDtypeStruct((T, CONV_DIM), F32), jax.ShapeDtypeStruct((T, DT_PAD), F32)],
        scratch_shapes=[pltpu.VMEM((tm + HALO, CONV_DIM), F32)],
        compiler_params=_cparams(("arbitrary",)),
    )(xbc_raw, dt_raw, conv_w, conv_b, dt_bias)


def _conv_bwd(xbc_raw, dt_raw, dxs_a, dxs_b, dB, dC, ddt, conv_w, conv_b, dt_bias):
    T = xbc_raw.shape[0]
    tm = TOKEN_TILE
    n = T // tm
    hb = tm // HALO

    def rev(width):
        return pl.BlockSpec((tm, width), lambda i: (n - 1 - i, 0))

    def body(u_ref, up_ref, dtr_ref, dxa_ref, dxb_ref, dB_ref, dC_ref, ddt_ref, w_ref, b_ref, db_ref,
             du_ref, ddtr_ref, dw_ref, dcb_ref, ddb_ref, ext_ref, dye_ref):
        i = pl.program_id(0)

        @pl.when(i == 0)
        def _():
            dye_ref[tm:, :] = jnp.zeros((HALO, CONV_DIM), F32)
            dw_ref[...] = jnp.zeros_like(dw_ref)
            dcb_ref[...] = jnp.zeros_like(dcb_ref)
            ddb_ref[...] = jnp.zeros_like(ddb_ref)

        first = (i == n - 1).astype(F32)
        ext_ref[0:HALO, :] = up_ref[...] * (1.0 - first)
        ext_ref[HALO:, :] = u_ref[...]
        for c0 in range(0, CONV_DIM, CONV_COLS):
            cols = slice(c0, c0 + CONV_COLS)
            taps = [w_ref[k:k + 1, cols] for k in range(CONV_K)]
            bias = b_ref[:, cols]
            acc_b = jnp.zeros((HALO, CONV_COLS), F32)
            acc_w = [jnp.zeros((HALO, CONV_COLS), F32) for _ in range(CONV_K)]
            for r0 in range(0, tm, CONV_ROWS_BLK):
                rows = slice(r0, r0 + CONV_ROWS_BLK)
                us = [ext_ref[pl.ds(HALO - (CONV_K - 1) + k + r0, CONV_ROWS_BLK), cols] for k in range(CONV_K)]
                y = bias + taps[0] * us[0]
                for k in range(1, CONV_K):
                    y += taps[k] * us[k]
                s = _sigmoid(y)
                if c0 < SSD_INNER:
                    dact = dxa_ref[rows, cols] + dxb_ref[rows, cols]
                elif c0 < SSD_INNER + 512:
                    dact = dB_ref[rows, c0 - SSD_INNER:c0 - SSD_INNER + CONV_COLS]
                else:
                    dact = dC_ref[rows, c0 - SSD_INNER - 512:c0 - SSD_INNER - 512 + CONV_COLS]
                dy = dact * (s * (1.0 + y * (1.0 - s)))
                dye_ref[rows, cols] = dy
                acc_b += jnp.sum(dy.reshape(CONV_ROWS_BLK // HALO, HALO, CONV_COLS), axis=0)
                for k in range(CONV_K):
                    acc_w[k] += jnp.sum((dy * us[k]).reshape(CONV_ROWS_BLK // HALO, HALO, CONV_COLS), axis=0)
            dcb_ref[:, cols] += jnp.sum(acc_b, axis=0, keepdims=True)
            for k in range(CONV_K):
                dw_ref[k:k + 1, cols] += jnp.sum(acc_w[k], axis=0, keepdims=True)
        for c0 in range(0, CONV_DIM, CONV_COLS):
            cols = slice(c0, c0 + CONV_COLS)
            taps = [w_ref[k:k + 1, cols] for k in range(CONV_K)]
            for r0 in range(0, tm, CONV_ROWS_BLK):
                du = taps[0] * dye_ref[pl.ds(CONV_K - 1 + r0, CONV_ROWS_BLK), cols]
                for k in range(1, CONV_K):
                    du += taps[k] * dye_ref[pl.ds(CONV_K - 1 - k + r0, CONV_ROWS_BLK), cols]
                du_ref[r0:r0 + CONV_ROWS_BLK, cols] = du.astype(BF16)
        dye_ref[tm:, :] = dye_ref[0:HALO, :]
        sg = _sigmoid(dtr_ref[...] + db_ref[...])
        ddtr = ddt_ref[...] * sg
        ddtr_ref[...] = ddtr.astype(BF16)
        ddb_ref[...] += jnp.sum(ddtr, axis=0, keepdims=True)

    prev_spec = pl.BlockSpec((HALO, CONV_DIM), lambda i: (jnp.maximum((n - 1 - i) * hb - 1, 0), 0))
    return _pc(
        body, name="conv_bwd", grid=(n,),
        in_specs=[rev(CONV_DIM), prev_spec, rev(DT_PAD), rev(1024), rev(1024), rev(512), rev(512), rev(DT_PAD),
                  _const_spec((CONV_K, CONV_DIM)), _const_spec((1, CONV_DIM)), _const_spec((1, DT_PAD))],
        out_specs=[rev(CONV_DIM), rev(DT_PAD), _const_spec((HALO, CONV_DIM)), _const_spec((1, CONV_DIM)),
                   _const_spec((1, DT_PAD))],
        out_shape=[jax.ShapeDtypeStruct((T, CONV_DIM), BF16), jax.ShapeDtypeStruct((T, DT_PAD), BF16),
                   jax.ShapeDtypeStruct((HALO, CONV_DIM), F32), jax.ShapeDtypeStruct((1, CONV_DIM), F32),
                   jax.ShapeDtypeStruct((1, DT_PAD), F32)],
        scratch_shapes=[pltpu.VMEM((tm + HALO, CONV_DIM), F32), pltpu.VMEM((tm + HALO, CONV_DIM), F32)],
        compiler_params=_cparams(("arbitrary",)),
    )(xbc_raw, xbc_raw, dt_raw, dxs_a, dxs_b, dB, dC, ddt, conv_w, conv_b, dt_bias)


GROUP_LANES = SSD_HPG * SSD_HEADDIM


def _ssd_expanders():
    head = jnp.arange(DT_PAD)[:, None]
    to_wide = (jnp.arange(SSD_INNER)[None, :] // SSD_HEADDIM == head).astype(BF16)
    return to_wide, to_wide.T


def _ssd_prep(dt_ref, alog_ref, wide_ref):
    q = SSD_CHUNK
    a = -jnp.exp(alog_ref[...])
    dtv = dt_ref[...]
    la = dtv * a
    row = lax.broadcasted_iota(jnp.int32, (q, q), 0)
    col = lax.broadcasted_iota(jnp.int32, (q, q), 1)
    tri = (col <= row).astype(BF16)
    cum = _dot_split_rhs(tri, la, 3)
    cum_t = _dot_split(la, tri, 3, (((0,), (1,)), ((), ())))
    dtw = _dot_split(dtv, wide_ref[...], 2)
    cumw = _dot_split(cum, wide_ref[...], 3)
    return a, dtv, row, col, tri, cum_t, dtw, cumw, cum


def _decay(cum, cum_t, h, keep):
    return jnp.where(keep, jnp.exp(jnp.minimum(cum[:, h:h + 1] - cum_t[h:h + 1, :], 0.0)), 0.0)


def _decay_t(cum, cum_t, h, keep_t):
    return jnp.where(keep_t, jnp.exp(jnp.minimum(cum_t[h:h + 1, :] - cum[:, h:h + 1], 0.0)), 0.0)


def _ssd_fwd(xbc_act, dt, alog):
    T = xbc_act.shape[0]
    q = SSD_CHUNK
    nc = T // q
    to_wide, _ = _ssd_expanders()

    def body(xbc_ref, dt_ref, alog_ref, wide_ref, y_ref, sp_ref, st_ref, xd_ref, xde_ref):
        @pl.when(pl.program_id(0) == 0)
        def _():
            st_ref[...] = jnp.zeros_like(st_ref)

        a, dtv, row, col, tri, cum_t, dtw, cumw, segcol = _ssd_prep(dt_ref, alog_ref, wide_ref)
        clw = cumw[q - 1:q, :]
        ecw = jnp.exp(cumw)
        xd = xbc_ref[:, 0:SSD_INNER] * dtw
        xd_ref[...] = xd.astype(BF16)
        xde_ref[...] = (xd * jnp.exp(clw - cumw)).astype(BF16)
        cdw = jnp.exp(clw)
        keep = col <= row
        sp_ref[0] = st_ref[...]
        for g in range(SSD_GROUPS):
            gl = slice(GROUP_LANES * g, GROUP_LANES * (g + 1))
            bb = xbc_ref[:, 1024 + 128 * g:1152 + 128 * g].astype(BF16)
            cb = xbc_ref[:, 1536 + 128 * g:1664 + 128 * g].astype(BF16)
            gm = _dot_nt(cb, bb)
            stp = st_ref[g]
            yoff = _dot(cb, stp.astype(BF16)) * ecw[:, gl]
            for r in range(SSD_HPG):
                h = SSD_HPG * g + r
                m = (gm * _decay(segcol, cum_t, h, keep)).astype(BF16)
                y_ref[:, 64 * h:64 * h + 64] = _dot(m, xd_ref[:, 64 * h:64 * h + 64]) + yoff[:, 64 * r:64 * r + 64]
            st_ref[g] = stp * cdw[:, gl] + _dot_tn(bb, xde_ref[:, gl])

    return _pc(
        body, name="ssd_fwd", grid=(nc,),
        in_specs=[_row_spec(q, CONV_DIM), _row_spec(q, DT_PAD), _const_spec((1, DT_PAD)),
                  _const_spec(to_wide.shape)],
        out_specs=[_row_spec(q, SSD_INNER),
                   pl.BlockSpec((1, SSD_GROUPS, SSD_STATE, GROUP_LANES), lambda i: (i, 0, 0, 0))],
        out_shape=[jax.ShapeDtypeStruct((T, SSD_INNER), F32),
                   jax.ShapeDtypeStruct((nc, SSD_GROUPS, SSD_STATE, GROUP_LANES), F32)],
        scratch_shapes=[pltpu.VMEM((SSD_GROUPS, SSD_STATE, GROUP_LANES), F32), pltpu.VMEM((q, SSD_INNER), BF16),
                        pltpu.VMEM((q, SSD_INNER), BF16)],
        compiler_params=_cparams(("arbitrary",)),
    )(xbc_act, dt, alog, to_wide)


def _ssd_bwd(xbc_act, dt, alog, sprev, dy):
    T = xbc_act.shape[0]
    q = SSD_CHUNK
    nc = T // q
    to_wide, to_heads = _ssd_expanders()

    def rev(width):
        return pl.BlockSpec((q, width), lambda i: (nc - 1 - i, 0))

    def body(xbc_ref, dt_ref, alog_ref, sp_ref, dy_ref, wide_ref, heads_ref,
             dxs_ref, dB_ref, dC_ref, ddt_ref, dalog_ref, ds_ref, xd_ref, dxd_ref):
        i = pl.program_id(0)

        @pl.when(i == 0)
        def _():
            ds_ref[...] = jnp.zeros_like(ds_ref)
            dalog_ref[...] = jnp.zeros_like(dalog_ref)

        a, dtv, row, col, tri, cum_t, dtw, cumw, segcol = _ssd_prep(dt_ref, alog_ref, wide_ref)
        clw = cumw[q - 1:q, :]
        ecw = jnp.exp(cumw)
        dew = jnp.exp(clw - cumw)
        cdw = jnp.exp(clw)
        xs = xbc_ref[:, 0:SSD_INNER]
        xd = xs * dtw
        xd_ref[...] = xd.astype(BF16)
        dyv = dy_ref[...]
        dye = (dyv * ecw).astype(BF16)
        xde = (xd * dew).astype(BF16)
        keep = col <= row
        keep_t = col >= row
        rows_k = lax.broadcasted_iota(jnp.int32, (SSD_HPG * q, DT_PAD), 0) // q
        lanes_k = lax.broadcasted_iota(jnp.int32, (SSD_HPG * q, DT_PAD), 1)
        dcw_parts = []
        dcum = jnp.zeros((q, DT_PAD), F32)
        for g in range(SSD_GROUPS):
            gl = slice(GROUP_LANES * g, GROUP_LANES * (g + 1))
            bb = xbc_ref[:, 1024 + 128 * g:1152 + 128 * g].astype(BF16)
            cb = xbc_ref[:, 1536 + 128 * g:1664 + 128 * g].astype(BF16)
            gm = _dot_nt(cb, bb)
            gmt = _dot_nt(bb, cb)
            stp = sp_ref[0, g]
            dst = ds_ref[g]
            stpb = stp.astype(BF16)
            dstb = dst.astype(BF16)
            yoff = _dot(cb, stpb) * ecw[:, gl]
            dcg = _dot_nt(dye[:, gl], stpb)
            ds_ref[g] = dst * cdw[:, gl] + _dot_tn(cb, dye[:, gl])
            dlast = jnp.sum(dst * stp, axis=0, keepdims=True) * cdw[:, gl]
            dbg = _dot_nt(xde[:, gl], dstb)
            w = _dot(bb, dstb) * dew[:, gl]
            wx = w * xd[:, gl]
            dlast = dlast + jnp.sum(wx, axis=0, keepdims=True)
            dcw_parts.append(dyv[:, gl] * yoff - wx
                             + jnp.where(lax.broadcasted_iota(jnp.int32, (q, 1), 0) == q - 1, dlast, 0.0))
            dgm = jnp.zeros((q, q), F32)
            diag = []
            for r in range(SSD_HPG):
                h = SSD_HPG * g + r
                hl = slice(64 * h, 64 * h + 64)
                dyb = dy_ref[:, hl].astype(BF16)
                xdh = xd_ref[:, hl]
                dm = _dot_nt(dyb, xdh)
                dmt = _dot_nt(xdh, dyb)
                dec = _decay(segcol, cum_t, h, keep)
                mt = gmt * _decay_t(segcol, cum_t, h, keep_t)
                dgm += dm * dec
                diag.append(dm * (gm * dec) - dmt * mt)
                dxd_ref[:, hl] = _dot(mt.astype(BF16), dyb) + w[:, 64 * r:64 * r + 64]
            onehots = (lanes_k == SSD_HPG * g + rows_k).astype(BF16)
            dcum += _dot_split(jnp.concatenate(diag, axis=1), onehots, 2)
            dgb = dgm.astype(BF16)
            dC_ref[:, 128 * g:128 * g + 128] = dcg + _dot(dgb, bb)
            dB_ref[:, 128 * g:128 * g + 128] = dbg + _dot_tn(dgb, cb)
        dxd = dxd_ref[...]
        dxs_ref[...] = dxd * dtw
        dcum += _dot_split(jnp.concatenate(dcw_parts, axis=1), heads_ref[...], 2)
        dla = _dot_split_rhs(tri, dcum, 3, (((0,), (0,)), ((), ())))
        ddt_ref[...] = _dot_split(xs * dxd, heads_ref[...], 2) + dla * a
        dalog_ref[...] += jnp.sum(dla * dtv, axis=0, keepdims=True)

        @pl.when(i == nc - 1)
        def _():
            dalog_ref[...] = dalog_ref[...] * a

    st_spec = pl.BlockSpec((1, SSD_GROUPS, SSD_STATE, GROUP_LANES), lambda i: (nc - 1 - i, 0, 0, 0))
    return _pc(
        body, name="ssd_bwd", grid=(nc,),
        in_specs=[rev(CONV_DIM), rev(DT_PAD), _const_spec((1, DT_PAD)), st_spec, rev(SSD_INNER),
                  _const_spec(to_wide.shape), _const_spec(to_heads.shape)],
        out_specs=[rev(SSD_INNER), rev(512), rev(512), rev(DT_PAD), _const_spec((1, DT_PAD))],
        out_shape=[jax.ShapeDtypeStruct((T, SSD_INNER), F32), jax.ShapeDtypeStruct((T, 512), F32),
                   jax.ShapeDtypeStruct((T, 512), F32), jax.ShapeDtypeStruct((T, DT_PAD), F32),
                   jax.ShapeDtypeStruct((1, DT_PAD), F32)],
        scratch_shapes=[pltpu.VMEM((SSD_GROUPS, SSD_STATE, GROUP_LANES), F32), pltpu.VMEM((q, SSD_INNER), BF16),
                        pltpu.VMEM((q, SSD_INNER), F32)],
        compiler_params=_cparams(("arbitrary",)),
    )(xbc_act, dt, alog, sprev, dy, to_wide, to_heads)


def _s5_disc_vals(a_re, a_im, log_dt, b_re, b_im):
    dt = jnp.exp(log_dt)
    mag = jnp.exp(a_re * dt)
    ab_re = mag * jnp.cos(a_im * dt)
    ab_im = mag * jnp.sin(a_im * dt)
    den = a_re * a_re + a_im * a_im
    nr = ab_re - 1.0
    ni = ab_im
    coef_re = (nr * a_re + ni * a_im) / den
    coef_im = (ni * a_re - nr * a_im) / den
    bb_re = coef_re * b_re - coef_im * b_im
    bb_im = coef_re * b_im + coef_im * b_re
    return ab_re, ab_im, bb_re, bb_im


def _s5_disc(a_re, a_im, log_dt, b_re, b_im):
    def body(ar, ai, ld, br, bi, o1, o2, o3, o4):
        o1[...], o2[...], o3[...], o4[...] = _s5_disc_vals(ar[...], ai[...], ld[...], br[...], bi[...])

    return _pc(
        body, name="s5_disc",
        out_shape=[jax.ShapeDtypeStruct((1, S5_STATES), F32), jax.ShapeDtypeStruct((1, S5_STATES), F32),
                   jax.ShapeDtypeStruct((16, S5_STATES), F32), jax.ShapeDtypeStruct((16, S5_STATES), F32)],
    )(a_re, a_im, log_dt, b_re, b_im)


def _s5_disc_bwd(a_re, a_im, log_dt, b_re, b_im, d_ab_re, d_ab_im, d_bb_re, d_bb_im):
    def body(ar, ai, ld, br, bi, g1, g2, g3, g4, o1, o2, o3, o4, o5):
        _, vjp = jax.vjp(_s5_disc_vals, ar[...], ai[...], ld[...], br[...], bi[...])
        d1, d2, d3, d4, d5 = vjp((g1[...], g2[...], g3[...], g4[...]))
        o1[...] = d1
        o2[...] = d2
        st = lax.broadcasted_iota(jnp.int32, (S5_STATES, DT_PAD), 0)
        grp = lax.broadcasted_iota(jnp.int32, (S5_STATES, DT_PAD), 1)
        sel = (st // 64 == grp).astype(F32)
        o3[...] = _dot_hi(d3, sel)
        o4[...] = d4
        o5[...] = d5

    return _pc(
        body, name="s5_disc_bwd",
        out_shape=[jax.ShapeDtypeStruct((1, S5_STATES), F32), jax.ShapeDtypeStruct((1, S5_STATES), F32),
                   jax.ShapeDtypeStruct((1, DT_PAD), F32),
                   jax.ShapeDtypeStruct((16, S5_STATES), F32), jax.ShapeDtypeStruct((16, S5_STATES), F32)],
    )(a_re, a_im, log_dt, b_re, b_im, d_ab_re, d_ab_im, d_bb_re, d_bb_im)


def _cmul_add(xr, xi, pr, pi, yr, yi):
    return xr + pr * yr - pi * yi, xi + pr * yi + pi * yr


def _powers(ar, ai, n):
    out = [(ar, ai)]
    for _ in range(n - 1):
        pr, pi = out[-1]
        out.append((pr * pr - pi * pi, 2.0 * pr * pi))
    return out


_BW = S5_STATES // S5_BLOCKS
_BI = S5_WIDTH // S5_BLOCKS
SUB = 8
S5_ROWS = S5_CHUNK // SUB


S5_TAB_ROWS = 8 * SUB


def _scan8(br, bi, tab_ref, reverse):
    for level, k in enumerate((1, 2, 4)):
        r0 = 2 * SUB * (level + 1)
        shift = SUB - k if reverse else k
        br, bi = _cmul_add(br, bi, tab_ref[r0:r0 + SUB, :], tab_ref[r0 + SUB:r0 + 2 * SUB, :],
                           pltpu.roll(br, shift, 0), pltpu.roll(bi, shift, 0))
    return br, bi


def _s5_tables(ab_ref, tab_ref, reverse):
    rowin = lax.broadcasted_iota(jnp.int32, (SUB, 1), 0)
    ar = ab_ref[0:1, :]
    ai = -ab_ref[1:2, :] if reverse else ab_ref[1:2, :]
    zero = jnp.zeros((SUB, S5_STATES), F32)
    for level, (pr, pi) in enumerate(_powers(ar, ai, 3)):
        k = 2 ** level
        keep = (rowin < SUB - k) if reverse else (rowin >= k)
        r0 = 2 * SUB * (level + 1)
        tab_ref[r0:r0 + SUB, :] = jnp.where(keep, pr, 0.0) + zero
        tab_ref[r0 + SUB:r0 + 2 * SUB, :] = jnp.where(keep, pi, 0.0) + zero
    hit = rowin == (SUB - 1 if reverse else 0)
    pr, pi = _scan8(jnp.where(hit, ar, 0.0) + zero, jnp.where(hit, ai, 0.0) + zero, tab_ref, reverse)
    tab_ref[0:SUB, :] = pr
    tab_ref[SUB:2 * SUB, :] = pi


def _s5_fwd_chunk(u_ref, y_ref, r0, wb_ref, wc_ref, d_ref, carry_ref, tab_ref, sr_ref, si_ref, between):
    q = S5_CHUNK
    rows = slice(r0, r0 + q)
    for j in range(S5_BLOCKS):
        bu = _dot(u_ref[rows, _BI * j:_BI * (j + 1)].astype(BF16), wb_ref[j])
        sr_ref[:, :, _BW * j:_BW * (j + 1)] = bu[:, :_BW].reshape(S5_ROWS, SUB, _BW)
        si_ref[:, :, _BW * j:_BW * (j + 1)] = bu[:, _BW:].reshape(S5_ROWS, SUB, _BW)
    tr, ti = tab_ref[0:SUB, :], tab_ref[SUB:2 * SUB, :]
    cr, ci = carry_ref[0:1, :], carry_ref[1:2, :]
    for k in range(S5_ROWS):
        sr, si = _scan8(sr_ref[k], si_ref[k], tab_ref, False)
        sr, si = _cmul_add(sr, si, tr, ti, cr, ci)
        sr_ref[k] = sr
        si_ref[k] = si
        cr, ci = sr[SUB - 1:SUB, :], si[SUB - 1:SUB, :]
        between()
    carry_ref[0:1, :] = cr
    carry_ref[1:2, :] = ci
    for j in range(S5_BLOCKS):
        sl = slice(_BW * j, _BW * (j + 1))
        ul = slice(_BI * j, _BI * (j + 1))
        s = jnp.concatenate([sr_ref[:, :, sl].reshape(q, _BW), si_ref[:, :, sl].reshape(q, _BW)], axis=1).astype(BF16)
        y_ref[rows, ul] = _dot(s, wc_ref[j]) + d_ref[:, ul] * u_ref[rows, ul]


def _inproj_s5_fwd(x, g, wp, wb4, wc4, ab, dvec, rider=None):
    T = x.shape[0]
    tm = TOKEN_TILE
    per_tile = tm // S5_CHUNK
    nc = T // S5_CHUNK

    def body(x_ref, g_ref, w_hbm, wb_ref, wc_ref, ab_ref, d_ref, z_ref, xbc_ref, u5_ref, gt_ref, dt_ref, h_ref, y_ref, sp_ref,
             w_ref, carry_ref, tab_ref, sr_ref, si_ref):
        @pl.when(pl.program_id(0) == 0)
        def _():
            pltpu.sync_copy(w_hbm, w_ref)
            carry_ref[...] = jnp.zeros_like(carry_ref)
            _s5_tables(ab_ref, tab_ref, False)

        xn, _ = _rms(x_ref[...])
        h = (xn * g_ref[...]).astype(BF16)
        h_ref[...] = h
        u5_ref[...] = _dot(h, w_ref[:, P_U5:P_G])
        pieces = [(z_ref, P_Z, c0) for c0 in range(0, P_XBC - P_Z, INPROJ_PIECE)]
        pieces += [(xbc_ref, P_XBC, c0) for c0 in range(0, P_U5 - P_XBC, INPROJ_PIECE)]
        pieces += [(gt_ref, P_G, c0) for c0 in range(0, P_DT - P_G, INPROJ_PIECE)]
        todo = iter(pieces)

        def between():
            nxt = next(todo, None)
            if nxt is not None:
                o_ref, base, c0 = nxt
                o_ref[:, c0:c0 + INPROJ_PIECE] = _dot(h, w_ref[:, base + c0:base + c0 + INPROJ_PIECE])

        dt_ref[...] = _dot(h, w_ref[:, P_DT:P_END])
        for c in range(per_tile):
            sp_ref[c] = carry_ref[...]
            _s5_fwd_chunk(u5_ref, y_ref, S5_CHUNK * c, wb_ref, wc_ref, d_ref, carry_ref, tab_ref, sr_ref, si_ref, between)
        for _ in pieces:
            between()

    widths = (1024, 2048, 512, 2048, DT_PAD)
    return _call(
        body, rider, name="inproj_s5_fwd", grid=(T // tm,),
        in_specs=[_row_spec(tm, D_MODEL), _const_spec((1, D_MODEL)), _hbm_spec(),
                  _const_spec((S5_BLOCKS, _BI, 2 * _BW)), _const_spec((S5_BLOCKS, 2 * _BW, _BI)),
                  _const_spec((8, S5_STATES)), _const_spec((1, S5_WIDTH))],
        out_specs=[_row_spec(tm, w) for w in widths] + [_row_spec(tm, D_MODEL), _row_spec(tm, S5_WIDTH),
                                                         pl.BlockSpec((per_tile, 8, S5_STATES), lambda i: (i, 0, 0))],
        out_shape=[jax.ShapeDtypeStruct((T, w), F32) for w in widths] + [
            jax.ShapeDtypeStruct((T, D_MODEL), BF16), jax.ShapeDtypeStruct((T, S5_WIDTH), F32),
            jax.ShapeDtypeStruct((nc, 8, S5_STATES), F32)],
        scratch_shapes=[pltpu.VMEM((D_MODEL, P_END), BF16), pltpu.VMEM((8, S5_STATES), F32),
                        pltpu.VMEM((S5_TAB_ROWS, S5_STATES), F32), pltpu.VMEM((S5_ROWS, SUB, S5_STATES), F32),
                        pltpu.VMEM((S5_ROWS, SUB, S5_STATES), F32)],
        compiler_params=_cparams(("arbitrary",)),
    )(x, g, wp, wb4, wc4, ab, dvec)


def _s5_bwd(u5, dy5, wb4, wc4, ab, dvec, sprev, rider=None):
    T = u5.shape[0]
    q = S5_CHUNK
    nc = T // q

    def rev(width):
        return pl.BlockSpec((q, width), lambda i: (nc - 1 - i, 0))

    def body(u_ref, dy_ref, wb_ref, wc_ref, ab_ref, d_ref, sp_ref, du_ref, dwb_ref, dwc_ref, dab_ref, dd_ref,
             carry_ref, tab_ref, rtab_ref, sr_ref, si_ref, lr_ref, li_ref):
        i = pl.program_id(0)
        rowin = lax.broadcasted_iota(jnp.int32, (SUB, 1), 0)

        @pl.when(i == 0)
        def _():
            carry_ref[...] = jnp.zeros_like(carry_ref)
            dwb_ref[...] = jnp.zeros_like(dwb_ref)
            dwc_ref[...] = jnp.zeros_like(dwc_ref)
            dab_ref[...] = jnp.zeros_like(dab_ref)
            dd_ref[...] = jnp.zeros_like(dd_ref)
            _s5_tables(ab_ref, tab_ref, False)
            _s5_tables(ab_ref, rtab_ref, True)

        for j in range(S5_BLOCKS):
            sl = slice(_BW * j, _BW * (j + 1))
            ul = slice(_BI * j, _BI * (j + 1))
            bu = _dot(u_ref[:, ul].astype(BF16), wb_ref[j])
            sr_ref[:, :, sl] = bu[:, :_BW].reshape(S5_ROWS, SUB, _BW)
            si_ref[:, :, sl] = bu[:, _BW:].reshape(S5_ROWS, SUB, _BW)
            ds = _dot_nt(dy_ref[:, ul].astype(BF16), wc_ref[j])
            lr_ref[:, :, sl] = ds[:, :_BW].reshape(S5_ROWS, SUB, _BW)
            li_ref[:, :, sl] = ds[:, _BW:].reshape(S5_ROWS, SUB, _BW)
        ar, ai = ab_ref[0:1, :], ab_ref[1:2, :]
        tr, ti = tab_ref[0:SUB, :], tab_ref[SUB:2 * SUB, :]
        cr, ci = sp_ref[0, 0:1, :], sp_ref[0, 1:2, :]
        for k in range(S5_ROWS):
            sr, si = _scan8(sr_ref[k], si_ref[k], tab_ref, False)
            sr, si = _cmul_add(sr, si, tr, ti, cr, ci)
            sr_ref[k] = sr
            si_ref[k] = si
            cr, ci = sr[SUB - 1:SUB, :], si[SUB - 1:SUB, :]
        tr, ti = rtab_ref[0:SUB, :], rtab_ref[SUB:2 * SUB, :]
        cr, ci = carry_ref[0:1, :], carry_ref[1:2, :]
        acc_r = jnp.zeros((SUB, S5_STATES), F32)
        acc_i = jnp.zeros((SUB, S5_STATES), F32)
        for k in reversed(range(S5_ROWS)):
            lr, li = _scan8(lr_ref[k], li_ref[k], rtab_ref, True)
            lr, li = _cmul_add(lr, li, tr, ti, cr, ci)
            lr_ref[k] = lr
            li_ref[k] = li
            cr, ci = lr[0:1, :], li[0:1, :]
            if k > 0:
                before_r, before_i = sr_ref[k - 1, SUB - 1:SUB, :], si_ref[k - 1, SUB - 1:SUB, :]
            else:
                before_r, before_i = sp_ref[0, 0:1, :], sp_ref[0, 1:2, :]
            keep = rowin >= 1
            pr = jnp.where(keep, pltpu.roll(sr_ref[k], 1, 0), before_r)
            pi = jnp.where(keep, pltpu.roll(si_ref[k], 1, 0), before_i)
            acc_r += lr * pr + li * pi
            acc_i += li * pr - lr * pi
        carry_ref[0:1, :] = cr
        carry_ref[1:2, :] = ci
        dab_ref[0:1, :] += jnp.sum(acc_r, axis=0, keepdims=True)
        dab_ref[1:2, :] += jnp.sum(acc_i, axis=0, keepdims=True)
        for j in range(S5_BLOCKS):
            sl = slice(_BW * j, _BW * (j + 1))
            ul = slice(_BI * j, _BI * (j + 1))
            u = u_ref[:, ul]
            dy = dy_ref[:, ul]
            dyb = dy.astype(BF16)
            lam = jnp.concatenate([lr_ref[:, :, sl].reshape(q, _BW), li_ref[:, :, sl].reshape(q, _BW)], axis=1).astype(BF16)
            s = jnp.concatenate([sr_ref[:, :, sl].reshape(q, _BW), si_ref[:, :, sl].reshape(q, _BW)], axis=1).astype(BF16)
            du_ref[:, ul] = (_dot_nt(lam, wb_ref[j]) + d_ref[:, ul] * dy).astype(BF16)
            dwb_ref[j] += _dot_tn(u.astype(BF16), lam)
            dwc_ref[j] += _dot_tn(s, dyb)
            dd_ref[:, ul] += jnp.sum(dy * u, axis=0, keepdims=True)

    big = pltpu.VMEM((S5_ROWS, SUB, S5_STATES), F32)
    return _call(
        body, rider, name="s5_bwd", grid=(nc,),
        in_specs=[rev(S5_WIDTH), rev(S5_WIDTH), _const_spec((S5_BLOCKS, _BI, 2 * _BW)), _const_spec((S5_BLOCKS, 2 * _BW, _BI)),
                  _const_spec((8, S5_STATES)), _const_spec((1, S5_WIDTH)),
                  pl.BlockSpec((1, 8, S5_STATES), lambda i: (nc - 1 - i, 0, 0))],
        out_specs=[rev(S5_WIDTH), _const_spec((S5_BLOCKS, _BI, 2 * _BW)), _const_spec((S5_BLOCKS, 2 * _BW, _BI)),
                   _const_spec((8, S5_STATES)), _const_spec((1, S5_WIDTH))],
        out_shape=[jax.ShapeDtypeStruct((T, S5_WIDTH), BF16), jax.ShapeDtypeStruct((S5_BLOCKS, _BI, 2 * _BW), F32),
                   jax.ShapeDtypeStruct((S5_BLOCKS, 2 * _BW, _BI), F32), jax.ShapeDtypeStruct((8, S5_STATES), F32),
                   jax.ShapeDtypeStruct((1, S5_WIDTH), F32)],
        scratch_shapes=[pltpu.VMEM((8, S5_STATES), F32), pltpu.VMEM((S5_TAB_ROWS, S5_STATES), F32),
                        pltpu.VMEM((S5_TAB_ROWS, S5_STATES), F32), big, big, big, big],
        compiler_params=_cparams(("arbitrary",)),
    )(u5, dy5, wb4, wc4, ab, dvec, sprev)


def _merge_vals(ys, xs, z, y5, gates, dvec, gssd, glu_w, glu_b, wbr):
    sz = _sigmoid(z)
    qv = ys + dvec * xs
    pre = qv * (z * sz)
    yn, rs = [], []
    for gi in range(SSD_GROUPS):
        p, r = _rms(pre[:, 256 * gi:256 * (gi + 1)])
        yn.append(p)
        rs.append(r)
    yn = jnp.concatenate(yn, axis=1)
    ya = yn * gssd
    gel = _gelu(y5)
    sg = _sigmoid(_dot(gel.astype(BF16), glu_w) + glu_b)
    yb = gel * sg
    pa = _dot(ya.astype(BF16), wbr[0:SSD_INNER, :])
    pb = _dot(yb.astype(BF16), wbr[SSD_INNER:, :])
    s0 = _sigmoid(gates[:, :D_MODEL])
    s1 = _sigmoid(gates[:, D_MODEL:])
    merged = s0 * pa + s1 * pb
    return dict(sz=sz, qv=qv, yn=yn, rs=rs, ya=ya, gel=gel, sg=sg, yb=yb, pa=pa, pb=pb, s0=s0, s1=s1, merged=merged)


def _merge_specs(tm):
    acts = [_row_spec(tm, 1024), _row_spec(tm, 1024, 0), _row_spec(tm, 1024), _row_spec(tm, 512), _row_spec(tm, 2048),
            _row_spec(tm, 1024)]
    params = [_const_spec((1, 1024)), _const_spec((1, 1024)), _const_spec((512, 512)), _const_spec((1, 512)), _hbm_spec()]
    return acts, params


def _merge_fwd(ys, xbc_act, z, y5, gates, x, dvec, gssd, glu_w, glu_b, wa):
    T = x.shape[0]
    tm = TOKEN_TILE
    acts, params = _merge_specs(tm)

    def body(ys_ref, xs_ref, z_ref, y5_ref, gt_ref, x_ref, dv_ref, gs_ref, gw_ref, gb_ref, wa_hbm, x1_ref,
             wbr_ref, wout_ref):
        @pl.when(pl.program_id(0) == 0)
        def _():
            _load_late_weight(wa_hbm, wbr_ref, "w_branch")
            _load_late_weight(wa_hbm, wout_ref, "w_out")

        v = _merge_vals(ys_ref[...], xs_ref[...], z_ref[...], y5_ref[...], gt_ref[...], dv_ref[...], gs_ref[...],
                        gw_ref[...], gb_ref[...], wbr_ref)
        x1_ref[...] = x_ref[...] + _dot(v["merged"].astype(BF16), wout_ref[...])

    return _pc(
        body, name="merge_fwd", grid=(T // tm,),
        in_specs=acts + params, out_specs=_row_spec(tm, 1024),
        out_shape=jax.ShapeDtypeStruct((T, 1024), F32),
        scratch_shapes=[pltpu.VMEM((1536, 1024), BF16), pltpu.VMEM((1024, 1024), BF16)],
        compiler_params=_cparams(("arbitrary",)),
    )(ys, xbc_act, z, y5, gates, x, dvec, gssd, glu_w, glu_b, wa)


def _merge_bwd(ys, xbc_act, z, y5, gates, dx1, dvec, gssd, glu_w, glu_b, wa, head_sel, rider=None):
    T = dx1.shape[0]
    tm = TOKEN_TILE
    acts, params = _merge_specs(tm)

    def body(ys_ref, xs_ref, z_ref, y5_ref, gt_ref, dx1_ref, dv_ref, gs_ref, gw_ref, gb_ref, wa_hbm, hs_ref,
             dys_ref, dxs_ref, dz_ref, dy5_ref, dgt_ref, mg_ref, ya_ref, yb_ref, dpa_ref, dpb_ref, gel_ref, dpre_ref,
             ddv_ref, dgs_ref, dgb_ref, wbr_ref, wout_ref, ddacc_ref):
        i = pl.program_id(0)

        @pl.when(i == 0)
        def _():
            _load_late_weight(wa_hbm, wbr_ref, "w_branch")
            _load_late_weight(wa_hbm, wout_ref, "w_out")
            ddacc_ref[...] = jnp.zeros_like(ddacc_ref)
            dgs_ref[...] = jnp.zeros_like(dgs_ref)
            dgb_ref[...] = jnp.zeros_like(dgb_ref)

        ys, xs, z, y5, gates = ys_ref[...], xs_ref[...], z_ref[...], y5_ref[...], gt_ref[...]
        dvv, gsv, gw = dv_ref[...], gs_ref[...], gw_ref[...]
        v = _merge_vals(ys, xs, z, y5, gates, dvv, gsv, gw, gb_ref[...], wbr_ref)
        dmg = _dot_nt(dx1_ref[...].astype(BF16), wout_ref[...])
        s0, s1, pa, pb = v["s0"], v["s1"], v["pa"], v["pb"]
        dgt_ref[:, :D_MODEL] = (dmg * pa * s0 * (1.0 - s0)).astype(BF16)
        dgt_ref[:, D_MODEL:] = (dmg * pb * s1 * (1.0 - s1)).astype(BF16)
        dpa = (dmg * s0).astype(BF16)
        dpb = (dmg * s1).astype(BF16)
        dya = _dot_nt(dpa, wbr_ref[0:SSD_INNER, :])
        dyb = _dot_nt(dpb, wbr_ref[SSD_INNER:, :])
        gel, sg = v["gel"], v["sg"]
        dpre = (dyb * gel * sg * (1.0 - sg))
        dgb_ref[...] += jnp.sum(dpre, axis=0, keepdims=True)
        dpre_b = dpre.astype(BF16)
        dgel = dyb * sg + _dot_nt(dpre_b, gw)
        dy5_ref[...] = dgel * _gelu_grad(y5)
        yn = v["yn"]
        dgs_ref[...] += jnp.sum(dya * yn, axis=0, keepdims=True)
        dyn = dya * gsv
        dpre_a = jnp.concatenate(
            [_rms_bwd(yn[:, 256 * gi:256 * (gi + 1)], v["rs"][gi], dyn[:, 256 * gi:256 * (gi + 1)])
             for gi in range(SSD_GROUPS)], axis=1)
        sz, qv = v["sz"], v["qv"]
        dq = dpre_a * (z * sz)
        dz_ref[...] = (dpre_a * qv * (sz * (1.0 + z * (1.0 - sz)))).astype(BF16)
        dys_ref[...] = dq
        dxs_ref[...] = dq * dvv
        ddacc_ref[...] += jnp.sum(dq * xs, axis=0, keepdims=True)
        mg_ref[...] = v["merged"].astype(BF16)
        ya_ref[...] = v["ya"].astype(BF16)
        yb_ref[...] = v["yb"].astype(BF16)
        dpa_ref[...] = dpa
        dpb_ref[...] = dpb
        gel_ref[...] = gel.astype(BF16)
        dpre_ref[...] = dpre_b

        @pl.when(i == pl.num_programs(0) - 1)
        def _():
            ddv_ref[...] = _dot_hi(ddacc_ref[...], hs_ref[...])

    outs = [(1024, F32), (1024, F32), (1024, BF16), (512, F32), (2048, BF16),
            (1024, BF16), (1024, BF16), (512, BF16), (1024, BF16), (1024, BF16), (512, BF16), (512, BF16)]
    return _call(
        body, rider, name="merge_bwd", grid=(T // tm,),
        in_specs=acts + params + [_const_spec((1024, DT_PAD))],
        out_specs=[_row_spec(tm, w) for w, _ in outs] + [_const_spec((1, DT_PAD)), _const_spec((1, 1024)), _const_spec((1, 512))],
        out_shape=[jax.ShapeDtypeStruct((T, w), d) for w, d in outs] + [
            jax.ShapeDtypeStruct((1, DT_PAD), F32), jax.ShapeDtypeStruct((1, 1024), F32), jax.ShapeDtypeStruct((1, 512), F32)],
        scratch_shapes=[pltpu.VMEM((1536, 1024), BF16), pltpu.VMEM((1024, 1024), BF16), pltpu.VMEM((1, 1024), F32)],
        compiler_params=_cparams(("arbitrary",)),
    )(ys, xbc_act, z, y5, gates, dx1, dvec, gssd, glu_w, glu_b, wa, head_sel)


def _mlp_fwd_loss(x1, target, g, g_fin, wa):
    T = x1.shape[0]
    tm = TOKEN_TILE

    def body(x_ref, t_ref, g_ref, gf_ref, wa_hbm, dx_ref, loss_ref, dg_ref, w1_ref, w2_ref):
        @pl.when(pl.program_id(0) == 0)
        def _():
            _load_late_weight(wa_hbm, w1_ref, "w_mlp_in")
            _load_late_weight(wa_hbm, w2_ref, "w_mlp_out")
            loss_ref[...] = jnp.zeros_like(loss_ref)
            dg_ref[...] = jnp.zeros_like(dg_ref)

        xv = x_ref[...]
        xn, _ = _rms(xv)
        h = (xn * g_ref[...]).astype(BF16)
        acc = xv
        for s in range(FF_SHARDS):
            rl = jnp.maximum(_dot(h, w1_ref[s]), 0.0)
            acc += _dot((rl * rl).astype(BF16), w2_ref[FF_SHARD * s:FF_SHARD * (s + 1), :])
        yn, r = _rms(acc)
        gv = gf_ref[...]
        err = yn * gv - t_ref[...]
        loss_ref[...] += jnp.sum(err * err, axis=0, keepdims=True) * (0.5 / D_MODEL)
        dy = err * (1.0 / D_MODEL)
        dg_ref[...] += jnp.sum(dy * yn, axis=0, keepdims=True)
        dx_ref[...] = _rms_bwd(yn, r, dy * gv)

    return _pc(
        body, name="mlp_fwd_loss", grid=(T // tm,),
        in_specs=[_row_spec(tm, 1024), _row_spec(tm, 1024), _const_spec((1, 1024)), _const_spec((1, 1024)), _hbm_spec()],
        out_specs=[_row_spec(tm, 1024), _const_spec((1, 1024)), _const_spec((1, 1024))],
        out_shape=[jax.ShapeDtypeStruct((T, 1024), F32), jax.ShapeDtypeStruct((1, 1024), F32),
                   jax.ShapeDtypeStruct((1, 1024), F32)],
        scratch_shapes=[pltpu.VMEM((FF_SHARDS, D_MODEL, FF_SHARD), BF16), pltpu.VMEM((D_FF, D_MODEL), BF16)],
        compiler_params=_cparams(("arbitrary",)),
    )(x1, target, g, g_fin, wa)


def _mlp_bwd(x1, dx2, g, wa):
    T = x1.shape[0]
    tm = TOKEN_TILE

    def body(x_ref, dx2_ref, g_ref, wa_hbm, dx1_ref, h_ref, act_ref, da_ref, dg_ref, w1_ref, w2_ref):
        @pl.when(pl.program_id(0) == 0)
        def _():
            _load_late_weight(wa_hbm, w1_ref, "w_mlp_in")
            _load_late_weight(wa_hbm, w2_ref, "w_mlp_out")
            dg_ref[...] = jnp.zeros_like(dg_ref)

        xn, r = _rms(x_ref[...])
        gv = g_ref[...]
        h = (xn * gv).astype(BF16)
        h_ref[...] = h
        dx2 = dx2_ref[...]
        dx2b = dx2.astype(BF16)
        dh = jnp.zeros((tm, D_MODEL), F32)
        for s in range(FF_SHARDS):
            ff = slice(FF_SHARD * s, FF_SHARD * (s + 1))
            rl = jnp.maximum(_dot(h, w1_ref[s]), 0.0)
            act_ref[:, ff] = (rl * rl).astype(BF16)
            da = (_dot_nt(dx2b, w2_ref[ff, :]) * (2.0 * rl)).astype(BF16)
            da_ref[:, ff] = da
            dh += _dot_nt(da, w1_ref[s])
        dg_ref[...] += jnp.sum(dh * xn, axis=0, keepdims=True)
        dx1_ref[...] = dx2 + _rms_bwd(xn, r, dh * gv)

    return _pc(
        body, name="mlp_bwd", grid=(T // tm,),
        in_specs=[_row_spec(tm, 1024), _row_spec(tm, 1024), _const_spec((1, 1024)), _hbm_spec()],
        out_specs=[_row_spec(tm, 1024), _row_spec(tm, 1024), _row_spec(tm, D_FF), _row_spec(tm, D_FF), _const_spec((1, 1024))],
        out_shape=[jax.ShapeDtypeStruct((T, 1024), F32), jax.ShapeDtypeStruct((T, 1024), BF16),
                   jax.ShapeDtypeStruct((T, D_FF), BF16), jax.ShapeDtypeStruct((T, D_FF), BF16),
                   jax.ShapeDtypeStruct((1, 1024), F32)],
        scratch_shapes=[pltpu.VMEM((FF_SHARDS, D_MODEL, FF_SHARD), BF16), pltpu.VMEM((D_FF, D_MODEL), BF16)],
        compiler_params=_cparams(("arbitrary",)),
    )(x1, dx2, g, wa)


WGRAD_OUT_ELEMS = 2 * 1024 * 1024
WGRAD_TILE_BYTES = 4 * 1024 * 1024


def _wgrad(a, b, name, col_shards=None, row_shards_into=None):
    T, K = a.shape
    N = b.shape[1]
    nb = N // col_shards if col_shards else min(N, 1024, max(128, WGRAD_OUT_ELEMS // K))
    tt = min(T, WGRAD_TOKENS)
    while tt * max(K * a.dtype.itemsize, nb * b.dtype.itemsize) > WGRAD_TILE_BYTES:
        tt //= 2
    assert N % nb == 0 and T % tt == 0
    in_specs = [pl.BlockSpec((tt, K), lambda n, t: (t, 0)), pl.BlockSpec((tt, nb), lambda n, t: (t, n))]
    args, aliases = [a, b], {}
    if col_shards:
        out_spec = pl.BlockSpec((None, None, K, nb), lambda n, t: (n, 0, 0, 0))
        out_shape = jax.ShapeDtypeStruct((col_shards, 2, K, nb), F32)
    elif row_shards_into is not None:
        shards, _, rows, cols = row_shards_into.shape
        assert shards * rows == K and cols == N
        out_spec = pl.BlockSpec((shards, None, rows, nb), lambda n, t: (0, 1, 0, n))
        out_shape = jax.ShapeDtypeStruct(row_shards_into.shape, F32)
        in_specs.append(_hbm_spec())
        args.append(row_shards_into)
        aliases = {2: 0}
    else:
        out_spec = pl.BlockSpec((K, nb), lambda n, t: (0, n))
        out_shape = jax.ShapeDtypeStruct((K, N), F32)

    def body(a_ref, b_ref, *rest):
        o_ref = rest[-1]

        @pl.when(pl.program_id(1) == 0)
        def _():
            o_ref[...] = jnp.zeros_like(o_ref)

        o_ref[...] += _dot_tn(a_ref[...].astype(BF16), b_ref[...].astype(BF16)).reshape(o_ref.shape)

    return _pc(
        body, name=name, grid=(N // nb, T // tt), in_specs=in_specs, out_specs=out_spec, out_shape=out_shape,
        input_output_aliases=aliases, compiler_params=_cparams(("parallel", "arbitrary")),
    )(*args)


def _s5_block_weights(bb_re, bb_im, c_re, c_im):
    eye = jnp.eye(8, dtype=F32)
    bre = bb_re.reshape(16, S5_BLOCKS, 8, 64)
    bim = bb_im.reshape(16, S5_BLOCKS, 8, 64)
    wb_re = jnp.einsum('kjgp,gh->jhkgp', bre, eye).reshape(S5_BLOCKS, _BI, _BW)
    wb_im = jnp.einsum('kjgp,gh->jhkgp', bim, eye).reshape(S5_BLOCKS, _BI, _BW)
    wb4 = jnp.concatenate([wb_re, wb_im], axis=2).astype(BF16)
    cre = c_re.reshape(S5_BLOCKS, 8, 16, 64)
    cim = c_im.reshape(S5_BLOCKS, 8, 16, 64)
    wc_re = jnp.einsum('jgkp,gh->jgphk', cre, eye).reshape(S5_BLOCKS, _BW, _BI)
    wc_im = jnp.einsum('jgkp,gh->jgphk', -cim, eye).reshape(S5_BLOCKS, _BW, _BI)
    wc4 = jnp.concatenate([wc_re, wc_im], axis=1).astype(BF16)
    return wb4, wc4


def _s5_block_grads(dwb4, dwc4):
    eye = jnp.eye(8, dtype=F32)
    dwb = dwb4.reshape(S5_BLOCKS, 8, 16, 2, 8, 64)
    dbb = jnp.einsum('jhkrgp,gh->rkjgp', dwb, eye).reshape(2, 16, S5_STATES)
    dwc = dwc4.reshape(S5_BLOCKS, 2, 8, 64, 8, 16)
    dc = jnp.einsum('jrgphk,gh->rjgkp', dwc, eye).reshape(2, 32, 16, 64)
    return dbb[0], dbb[1], dc[0], -dc[1]


def _row(v, width=None):
    v = v.reshape(1, -1)
    if width is not None and v.shape[1] < width:
        v = jnp.concatenate([v, jnp.zeros((1, width - v.shape[1]), v.dtype)], axis=1)
    return v


def _local_step(x, target, p, comm=None):
    g_mix, g_mlp, g_fin = _row(p["norm_mix_g"]), _row(p["norm_mlp_g"]), _row(p["norm_final_g"])
    conv_b = _row(p["conv_b"])
    dt_bias = _row(p["dt_bias"], DT_PAD)
    alog = _row(p["a_log"], DT_PAD)
    dvec = _row(jnp.repeat(p["d_ssd"], SSD_HEADDIM))
    gssd = _row(p["ssd_norm_g"])
    s5d = _row(p["s5_d"])
    glu_b = _row(p["s5_glu_b"])
    head_sel = (jnp.arange(SSD_INNER)[:, None] // SSD_HEADDIM == jnp.arange(DT_PAD)[None, :]).astype(F32)

    a_re = p["s5_a_re"].reshape(1, S5_STATES)
    a_im = p["s5_a_im"].reshape(1, S5_STATES)
    log_dt = jnp.repeat(p["s5_log_dt"], 64).reshape(1, S5_STATES)
    b_re = p["s5_b_re"].reshape(S5_STATES, 16).T
    b_im = p["s5_b_im"].reshape(S5_STATES, 16).T
    ab_re, ab_im, bb_re, bb_im = _s5_disc(a_re, a_im, log_dt, b_re, b_im)
    wb4, wc4 = _s5_block_weights(bb_re, bb_im, p["s5_c_re"], p["s5_c_im"])
    ab = jnp.concatenate([ab_re, ab_im, jnp.zeros((6, S5_STATES), F32)], axis=0)

    wp = p["w_in_perm"]

    first_args = (x, g_mix, wp, wb4, wc4, ab, s5d)
    if comm is None:
        z, xbc_raw, u5, gates, dt_raw, h, y5, s5_states = _inproj_s5_fwd(*first_args)
    else:
        first_out, late = _inproj_s5_fwd(*first_args, rider=_Gather(comm["late_srcs"], comm["late_ks"]))
        z, xbc_raw, u5, gates, dt_raw, h, y5, s5_states = first_out
        p = {**p, **comm["late_unpack"](late)}
    xbc_act, dt = _conv_fwd(xbc_raw, dt_raw, p["conv_w"], conv_b, dt_bias)
    ys, ssd_states = _ssd_fwd(xbc_act, dt, alog)
    wa, glu_w = p["late_weights"], p["s5_glu_w"]
    x1 = _merge_fwd(ys, xbc_act, z, y5, gates, x, dvec, gssd, glu_w, glu_b, wa)
    dx2, loss_lanes, d_gfin = _mlp_fwd_loss(x1, target, g_mlp, g_fin, wa)

    dx1, h2, act, da1, d_gmlp = _mlp_bwd(x1, dx2, g_mlp, wa)
    g_mlp4 = _wgrad(h2, da1, "wgrad_mlp_in", col_shards=FF_SHARDS)
    g_mlp4 = _wgrad(act, dx2, "wgrad_mlp_out", row_shards_into=g_mlp4)
    d_w_mlp_in, d_w_mlp_out = g_mlp4[:, 0], g_mlp4[:, 1].reshape(D_FF, D_MODEL)
    merge_args = (ys, xbc_act, z, y5, gates, dx1, dvec, gssd, glu_w, glu_b, wa, head_sel)
    if comm is None:
        merge_out = _merge_bwd(*merge_args)
    else:
        g_mlp = g_mlp4.reshape(N_CHIPS, 2 * FF_SHARD, D_MODEL)
        merge_out, (sib_mlp,) = _merge_bwd(*merge_args, rider=_Pair([g_mlp]))
        pf_mlp, pb_mlp = _pair_sum(comm["place"], g_mlp, sib_mlp, "pair_sum_mlp")
    (dys, dxs_m, dz, dy5, dgates, mg, ya, yb, dpa, dpb, gel, dpre, d_dssd, d_gssd, d_glu_b) = merge_out
    d_w_out = _wgrad(mg, dx1, "wgrad_out")
    d_w_branch = jnp.concatenate([_wgrad(ya, dpa, "wgrad_branch_a"), _wgrad(yb, dpb, "wgrad_branch_b")], axis=0)
    d_glu_w = _wgrad(gel, dpre, "wgrad_glu")
    s5_args = (u5, dy5, wb4, wc4, ab, s5d, s5_states)
    if comm is None:
        du5, dwb4, dwc4, dab, d_s5d = _s5_bwd(*s5_args)
        mlp_total = None
    else:
        (du5, dwb4, dwc4, dab, d_s5d), (got_mlp,) = _s5_bwd(*s5_args, rider=_Chip([pb_mlp]))
        mlp_total = _chip_sum(comm["place"], pf_mlp, got_mlp, "chip_sum_mlp")
    dbb_re, dbb_im, d_c_re, d_c_im = _s5_block_grads(dwb4, dwc4)
    d_a_re, d_a_im, d_log_dt, d_b_re, d_b_im = _s5_disc_bwd(
        a_re, a_im, log_dt, b_re, b_im, dab[0:1], dab[1:2], dbb_re, dbb_im)
    dxs_s, dB, dC, ddt, d_alog = _ssd_bwd(xbc_act, dt, alog, ssd_states, dys)
    dxbc_raw, ddt_raw, d_conv_w, d_conv_b, d_dt_bias = _conv_bwd(
        xbc_raw, dt_raw, dxs_m, dxs_s, dB, dC, ddt, p["conv_w"], conv_b, dt_bias)
    d_w_in = dict(z=_wgrad(h, dz, "wgrad_in_z"), xbc=_wgrad(h, dxbc_raw, "wgrad_in_xbc"),
                  dt=_wgrad(h, ddt_raw, "wgrad_in_dt")[:, :16], u5=_wgrad(h, du5, "wgrad_in_u5"),
                  gates=_wgrad(h, dgates, "wgrad_in_gates"))
    w_in_pieces = [(c0, d_w_in[n]) for n, c0, _ in W_IN_PIECES]
    inproj_args = (x, dx1, dz, dxbc_raw, du5, dgates, ddt_raw, g_mix, wp)
    if comm is None:
        dx, d_gmix = _inproj_bwd(*inproj_args)
        late_totals = None
    else:
        g_b, g_in = _late_buffers(d_w_out, d_w_branch, d_glu_w, d_conv_w[:CONV_K], w_in_pieces)
        sib_b, sib_in = _exchange(_Pair([g_b, g_in]), "pair_exchange")
        pf_b, pb_b = _pair_sum(comm["place"], g_b, sib_b, "pair_sum_b")
        pf_in, pb_in = _pair_sum(comm["place"], g_in, sib_in, "pair_sum_in")
        (dx, d_gmix), (got_b, got_in) = _inproj_bwd(*inproj_args, rider=_Chip([pb_b, pb_in]))
        late_totals = (_chip_sum(comm["place"], pf_b, got_b, "chip_sum_b"),
                       _chip_sum(comm["place"], pf_in, got_in, "chip_sum_in"))

    grads = dict(
        norm_mix_g=d_gmix.reshape(-1), w_in_pieces=w_in_pieces, late_totals=late_totals,
        conv_w=d_conv_w[:CONV_K], conv_b=d_conv_b.reshape(-1),
        dt_bias=d_dt_bias[0, :16], a_log=d_alog[0, :16], d_ssd=d_dssd[0, :16], ssd_norm_g=d_gssd.reshape(-1),
        s5_a_re=d_a_re.reshape(32, 64), s5_a_im=d_a_im.reshape(32, 64), s5_log_dt=d_log_dt[0, :32],
        s5_b_re=d_b_re.T.reshape(32, 64, 16), s5_b_im=d_b_im.T.reshape(32, 64, 16), s5_c_re=d_c_re, s5_c_im=d_c_im,
        s5_d=d_s5d.reshape(-1), s5_glu_w=d_glu_w, s5_glu_b=d_glu_b.reshape(-1), w_branch=d_w_branch, w_out=d_w_out,
        norm_mlp_g=d_gmlp.reshape(-1), w_mlp_in=d_w_mlp_in, w_mlp_out=d_w_mlp_out, norm_final_g=d_gfin.reshape(-1),
        mlp_total=mlp_total)
    return jnp.sum(loss_lanes), dx, grads


MESH = pl.DeviceIdType.MESH
N_CHIPS = 4


def _place():
    x, y, c = lax.axis_index("x"), lax.axis_index("y"), lax.axis_index("c")
    chips = [(1 - x, y), (x, 1 - y), (1 - x, 1 - y)]
    return x, y, c, chips


def _remote(src, dst, send_sems, recv_sems, k, to):
    return pltpu.make_async_remote_copy(src_ref=src, dst_ref=dst, send_sem=send_sems.at[k], recv_sem=recv_sems.at[k],
                                        device_id=to, device_id_type=MESH)


def _row_chunks(rows, k, align):
    step = rows // k
    assert rows % k == 0 and step % align == 0, (rows, k, align)
    return [(i * step, step) for i in range(k)]


ICI_CHUNKS = 4
D2D_CHUNKS = 24


class _Gather:
    def __init__(self, srcs, ks):
        self.inputs = list(srcs)
        self.out_shapes = [jax.ShapeDtypeStruct((N_CHIPS,) + a.shape, a.dtype) for a in srcs]
        self.halves = [a.shape[0] // 2 for a in srcs]
        self.pieces = [_row_chunks(h, k, 32 // a.dtype.itemsize) for a, h, k in zip(srcs, self.halves, ks)]
        self.n_ici = 3 * sum(ks)
        self.n_sems = 2 * self.n_ici + len(srcs)

    def _plan(self, src_refs, out_refs, send_sems, recv_sems):
        x, y, c, chips = _place()
        own = 2 * x + y
        sib = (x, y, 1 - c)
        first, fwd_plan, k = [], [], 0
        for a, (src_ref, out_ref) in enumerate(zip(src_refs, out_refs)):
            h = self.halves[a]
            for r0, nr in self.pieces[a]:
                for cx, cy in chips:
                    first.append(_remote(src_ref.at[pl.ds(c * h + r0, nr), :], out_ref.at[own, pl.ds(c * h + r0, nr), :],
                                         send_sems, recv_sems, k, (cx, cy, c)))
                    fwd_plan.append((out_ref, 2 * cx + cy, h, r0, nr, k, (cx, cy, c)))
                    k += 1
        for a, (src_ref, out_ref) in enumerate(zip(src_refs, out_refs)):
            first.append(_remote(src_ref, out_ref.at[own], send_sems, recv_sems, 2 * self.n_ici + a, sib))
        return first, fwd_plan, c, sib

    def issue(self, src_refs, out_refs, send_sems, recv_sems):
        for cp in self._plan(src_refs, out_refs, send_sems, recv_sems)[0]:
            cp.start()

    def complete(self, src_refs, out_refs, send_sems, recv_sems):
        first, fwd_plan, c, sib = self._plan(src_refs, out_refs, send_sems, recv_sems)
        passed = []
        for out_ref, s, h, r0, nr, k, frm in fwd_plan:
            got = out_ref.at[s, pl.ds(c * h + r0, nr), :]
            _remote(got, got, send_sems, recv_sems, k, frm).wait_recv()
            fw = _remote(got, got, send_sems, recv_sems, self.n_ici + k, sib)
            fw.start()
            passed.append(fw)
        for out_ref, s, h, r0, nr, k, frm in fwd_plan:
            got = out_ref.at[s, pl.ds((1 - c) * h + r0, nr), :]
            _remote(got, got, send_sems, recv_sems, self.n_ici + k, sib).wait_recv()
        own_copies = first[self.n_ici:]
        for cp in own_copies:
            cp.wait_recv()
        for cp in first + passed:
            cp.wait_send()


def _exchange(rider, name):
    ri, ro = len(rider.inputs), len(rider.out_shapes)

    def body(*refs):
        rider.issue(refs[:ri], refs[ri:ri + ro], *refs[ri + ro:])
        rider.complete(refs[:ri], refs[ri:ri + ro], *refs[ri + ro:])

    return _pc(
        body, name=name, in_specs=[_hbm_spec()] * ri, out_specs=[_hbm_spec()] * ro, out_shape=list(rider.out_shapes),
        scratch_shapes=[pltpu.SemaphoreType.DMA((rider.n_sems,))] * 2,
    )(*rider.inputs)


def _call(body, rider=None, **kw):
    if rider is None:
        return _pc(body, **kw)
    single = not isinstance(kw["out_shape"], (list, tuple))
    out_specs = [kw["out_specs"]] if single else list(kw["out_specs"])
    out_shape = [kw["out_shape"]] if single else list(kw["out_shape"])
    scratch = list(kw.get("scratch_shapes", ()))
    n_in, n_out, n_scr = len(kw["in_specs"]), len(out_specs), len(scratch)
    ri, ro = len(rider.inputs), len(rider.out_shapes)
    steps = kw["grid"][0]

    def wrapped(*refs):
        o0 = n_in + ri
        s0 = o0 + n_out + ro
        r_in, r_out, sems = refs[n_in:o0], refs[o0 + n_out:s0], refs[s0 + n_scr:]

        @pl.when(pl.program_id(0) == 0)
        def _():
            rider.issue(r_in, r_out, *sems)

        body(*refs[:n_in], *refs[o0:o0 + n_out], *refs[s0:s0 + n_scr])

        @pl.when(pl.program_id(0) == steps - 1)
        def _():
            rider.complete(r_in, r_out, *sems)

    f = _pc(wrapped, name=kw["name"], grid=kw["grid"], in_specs=list(kw["in_specs"]) + [_hbm_spec()] * ri,
            out_specs=out_specs + [_hbm_spec()] * ro, out_shape=out_shape + list(rider.out_shapes),
            scratch_shapes=scratch + [pltpu.SemaphoreType.DMA((rider.n_sems,))] * 2, compiler_params=kw["compiler_params"])

    def run(*args):
        res = f(*args, *rider.inputs)
        return (res[0] if single else res[:n_out]), res[n_out:]

    return run


def _d2d_pieces(rows):
    k = next(k for k in range(24, 0, -1) if rows % k == 0 and (rows // k) % 8 == 0)
    return _row_chunks(rows, k, 8)


class _Pair:
    def __init__(self, gs, small=None):
        self.n = len(gs)
        self.halves = [g.shape[1] // 2 for g in gs]
        self.inputs = list(gs) + ([small] if small is not None else [])
        self.out_shapes = [jax.ShapeDtypeStruct((N_CHIPS, h, g.shape[2]), F32) for g, h in zip(gs, self.halves)]
        if small is not None:
            self.out_shapes.append(jax.ShapeDtypeStruct(small.shape, F32))
        self.n_sems = len(self.inputs)

    def issue(self, in_refs, out_refs, send_sems, recv_sems):
        x, y, c, _ = _place()
        sib = (x, y, 1 - c)
        for a in range(self.n):
            h = self.halves[a]
            for s in range(N_CHIPS):
                for r0, nr in _d2d_pieces(h):
                    _remote(in_refs[a].at[s, pl.ds((1 - c) * h + r0, nr), :], out_refs[a].at[s, pl.ds(r0, nr), :],
                            send_sems, recv_sems, a, sib).start()
        for a in range(self.n, len(self.inputs)):
            _remote(in_refs[a], out_refs[a], send_sems, recv_sems, a, sib).start()

    def complete(self, in_refs, out_refs, send_sems, recv_sems):
        x, y, c, _ = _place()
        for a in range(len(self.inputs)):
            _remote(out_refs[a], out_refs[a], send_sems, recv_sems, a, (x, y, 1 - c)).wait()


SUM_BLOCKS = 4


def _pair_sum(place, g, sib, name):
    n, R, C = g.shape
    H = R // 2
    rb = H // SUM_BLOCKS
    assert H % SUM_BLOCKS == 0 and rb % 16 == 0

    def body(place_ref, a_ref, b_ref, pf_ref, pb_ref):
        p = a_ref[...] + b_ref[...]
        pf_ref[...] = p
        pb_ref[...] = p.astype(BF16)

    blk = pl.BlockSpec((1, rb, C), lambda s, i, pr: (s, i, 0))
    mine = pl.BlockSpec((1, rb, C), lambda s, i, pr: (s, pr[1] * SUM_BLOCKS + i, 0))
    return _pc(
        body, name=name, out_shape=[jax.ShapeDtypeStruct((n, H, C), F32), jax.ShapeDtypeStruct((n, H, C), BF16)],
        grid_spec=pltpu.PrefetchScalarGridSpec(num_scalar_prefetch=1, grid=(n, SUM_BLOCKS), in_specs=[mine, blk],
                                               out_specs=[blk, blk]),
        compiler_params=_cparams(("arbitrary", "arbitrary")),
    )(place, g, sib)


class _Chip:
    def __init__(self, pbs, psmall=None):
        self.n = len(pbs)
        self.rows = [pb.shape[1] for pb in pbs]
        self.inputs = list(pbs) + ([psmall] if psmall is not None else [])
        self.out_shapes = [jax.ShapeDtypeStruct((3,) + pb.shape[1:], BF16) for pb in pbs]
        if psmall is not None:
            self.out_shapes.append(jax.ShapeDtypeStruct((N_CHIPS,) + psmall.shape, F32))
        self.n_sems = 3 * len(self.inputs)

    def issue(self, in_refs, out_refs, send_sems, recv_sems):
        x, y, c, chips = _place()
        own = 2 * x + y
        for j, (cx, cy) in enumerate(chips):
            for a in range(self.n):
                for r0, nr in _row_chunks(self.rows[a], ICI_CHUNKS, 16):
                    _remote(in_refs[a].at[2 * cx + cy, pl.ds(r0, nr), :], out_refs[a].at[j, pl.ds(r0, nr), :],
                            send_sems, recv_sems, 3 * a + j, (cx, cy, c)).start()
            for a in range(self.n, len(self.inputs)):
                _remote(in_refs[a], out_refs[a].at[own], send_sems, recv_sems, 3 * a + j, (cx, cy, c)).start()

    def complete(self, in_refs, out_refs, send_sems, recv_sems):
        x, y, c, chips = _place()
        own = 2 * x + y
        for j, (cx, cy) in enumerate(chips):
            for a in range(self.n):
                _remote(in_refs[a].at[own], out_refs[a].at[j], send_sems, recv_sems, 3 * a + j, (cx, cy, c)).wait()
            for a in range(self.n, len(self.inputs)):
                _remote(in_refs[a], out_refs[a].at[2 * cx + cy], send_sems, recv_sems, 3 * a + j, (cx, cy, c)).wait()


def _chip_sum(place, pf, got, name):
    _, H, C = pf.shape
    rb = H // SUM_BLOCKS

    def body(place_ref, o_ref, g_ref, tot_ref):
        tot_ref[...] = ((o_ref[0] + g_ref[0].astype(F32)) + g_ref[1].astype(F32)) + g_ref[2].astype(F32)

    ins = [pl.BlockSpec((1, rb, C), lambda i, pr: (pr[0], i, 0)), pl.BlockSpec((3, rb, C), lambda i, pr: (0, i, 0))]
    out = pl.BlockSpec((rb, C), lambda i, pr: (pr[1] * SUM_BLOCKS + i, 0))
    return _pc(
        body, name=name, out_shape=jax.ShapeDtypeStruct((2 * H, C), F32),
        grid_spec=pltpu.PrefetchScalarGridSpec(num_scalar_prefetch=1, grid=(SUM_BLOCKS,), in_specs=ins, out_specs=out),
        compiler_params=_cparams(("arbitrary",)),
    )(place, pf, got)


def _half_exchange(fulls):
    n = len(fulls)

    def body(*refs):
        in_refs, out_refs = refs[:n], refs[n:2 * n]
        send_sems, recv_sems = refs[2 * n:]
        x, y, c, _ = _place()
        sib = (x, y, 1 - c)
        for a in range(n):
            h = fulls[a].shape[0] // 2
            for r0, nr in _d2d_pieces(h):
                rows = pl.ds(c * h + r0, nr)
                _remote(in_refs[a].at[rows, :], out_refs[a].at[rows, :], send_sems, recv_sems, a, sib).start()
        for a in range(n):
            h = fulls[a].shape[0] // 2
            _remote(in_refs[a].at[pl.ds(c * h, h), :], out_refs[a].at[pl.ds((1 - c) * h, h), :], send_sems, recv_sems, a,
                    sib).wait()

    return _pc(
        body, name="half_exchange", in_specs=[_hbm_spec()] * n, out_specs=[_hbm_spec()] * n,
        out_shape=[jax.ShapeDtypeStruct(f.shape, F32) for f in fulls],
        input_output_aliases={a: a for a in range(n)},
        scratch_shapes=[pltpu.SemaphoreType.DMA((n,)), pltpu.SemaphoreType.DMA((n,))],
    )(*fulls)


def _small_allreduce(pack):
    R, C = pack.shape

    def body(p_ref, o_ref, sib_ref, pair_ref, slots_ref, send_sems, recv_sems):
        x, y, c, chips = _place()
        own = 2 * x + y
        cp = _remote(p_ref, sib_ref, send_sems, recv_sems, 0, (x, y, 1 - c))
        cp.start()
        cp.wait()
        pair_ref[...] = p_ref[...] + sib_ref[...]
        slots_ref[own] = pair_ref[...]
        out = [_remote(pair_ref, slots_ref.at[own], send_sems, recv_sems, 1 + j, (cx, cy, c)) for j, (cx, cy) in enumerate(chips)]
        for cp in out:
            cp.start()
        for j, (cx, cy) in enumerate(chips):
            _remote(pair_ref, slots_ref.at[2 * cx + cy], send_sems, recv_sems, 1 + j, (cx, cy, c)).wait()
        o_ref[...] = ((slots_ref[0] + slots_ref[1]) + slots_ref[2]) + slots_ref[3]

    vmem = pl.BlockSpec(memory_space=pltpu.VMEM)
    return _pc(
        body, name="small_allreduce", in_specs=[vmem], out_specs=vmem, out_shape=jax.ShapeDtypeStruct((R, C), F32),
        scratch_shapes=[pltpu.VMEM((R, C), F32), pltpu.VMEM((R, C), F32), pltpu.VMEM((N_CHIPS, R, C), F32),
                        pltpu.SemaphoreType.DMA((4,)), pltpu.SemaphoreType.DMA((4,))],
    )(pack)


def _adamw(w, g, m, v, name, g_row0=0, with_grad=False, col_block=None):
    R, C = w.shape
    rb = 256 if R % 256 == 0 else (128 if R % 128 == 0 else R)
    if col_block:
        rb = R
    assert g_row0 % rb == 0

    def body(w_ref, g_ref, m_ref, v_ref, d_ref, nm_ref, nv_ref, *g_out):
        gv = g_ref[...]
        m2 = ADAM_B1 * m_ref[...] + (1.0 - ADAM_B1) * gv
        v2 = ADAM_B2 * v_ref[...] + (1.0 - ADAM_B2) * (gv * gv)
        m_hat = m2 * (1.0 / (1.0 - ADAM_B1 ** ADAM_STEP))
        v_hat = v2 * (1.0 / (1.0 - ADAM_B2 ** ADAM_STEP))
        d_ref[...] = -ADAM_LR * (m_hat / (jnp.sqrt(v_hat) + ADAM_EPS) + ADAM_WD * w_ref[...])
        nm_ref[...] = m2
        nv_ref[...] = v2
        if with_grad:
            g_out[0][...] = gv

    if col_block:
        spec = g_spec = pl.BlockSpec((R, col_block), lambda i: (0, i))
        steps = C // col_block
    else:
        spec = pl.BlockSpec((rb, C), lambda i: (i, 0))
        g_spec = pl.BlockSpec((rb, C), lambda i: (g_row0 // rb + i, 0))
        steps = R // rb
    n_out = 4 if with_grad else 3
    return _pc(
        body, name=name, grid=(steps,), in_specs=[spec, g_spec, spec, spec], out_specs=[spec] * n_out,
        out_shape=[jax.ShapeDtypeStruct((R, C), F32)] * n_out, compiler_params=_cparams(("parallel",)),
    )(w, g, m, v)


PACK_COLS = 1024
ROWS_A = (("w_mlp_in", 0, 1024), ("w_mlp_out", 1024, 1024), ("w_out", 2048, 256), ("w_branch", 2304, 384))
ROWS_A_TOTAL = 2688
ROWS_B = (("w_out", 0, 256), ("w_branch", 256, 384))
ROW_B_GLU, ROW_B_CONV, ROWS_B_TOTAL = 640, 704, 768
W_IN_SHARD = 1412
CONV_PAD_ROWS = 16
SMALL = (("norm_mix_g", (1024,)), ("conv_b", (2048,)), ("dt_bias", (16,)), ("a_log", (16,)), ("d_ssd", (16,)),
         ("ssd_norm_g", (1024,)), ("s5_a_re", (32, 64)), ("s5_a_im", (32, 64)), ("s5_log_dt", (32,)),
         ("s5_b_re", (32, 64, 16)), ("s5_b_im", (32, 64, 16)), ("s5_c_re", (32, 16, 64)), ("s5_c_im", (32, 16, 64)),
         ("s5_d", (512,)), ("s5_glu_b", (512,)), ("norm_mlp_g", (1024,)), ("norm_final_g", (1024,)))
SMALL_ROWS = 144
SMALL_COUNT = sum(math.prod(shp) for _, shp in SMALL)
GLU_ROWS = S5_WIDTH * S5_WIDTH // PACK_COLS
CONV_ROWS = CONV_K * CONV_DIM // PACK_COLS
W_IN_PIECES = (("z", 0, 1024), ("xbc", 1024, 2048), ("dt", OFF_DT, 16), ("u5", OFF_U, 512), ("gates", 3600, 2048))


def _pack_small(parts):
    flat = jnp.concatenate([a.astype(F32).reshape(-1) for a in parts])
    return jnp.concatenate([flat, jnp.zeros((SMALL_ROWS * PACK_COLS - flat.shape[0],), F32)]).reshape(SMALL_ROWS, PACK_COLS)


def _unpack_small(pack):
    flat, out, r = pack.reshape(-1), {}, 0
    for name, shp in SMALL:
        n = math.prod(shp)
        out[name] = flat[r:r + n].reshape(shp)
        r += n
    return out


def _late_buffers(d_w_out, d_w_branch, d_glu_w, d_conv_w, w_in_pieces):
    conv4 = d_conv_w.reshape(CONV_K, N_CHIPS, 512).transpose(1, 0, 2).reshape(N_CHIPS, CONV_ROWS // N_CHIPS, PACK_COLS)
    g_b = jnp.concatenate(
        [d_w_out.reshape(N_CHIPS, -1, PACK_COLS), d_w_branch.reshape(N_CHIPS, -1, PACK_COLS),
         d_glu_w.reshape(N_CHIPS, GLU_ROWS // N_CHIPS, PACK_COLS),
         jnp.pad(conv4, ((0, 0), (0, ROWS_B_TOTAL - ROW_B_CONV - CONV_ROWS // N_CHIPS), (0, 0)))], axis=1)
    g_in = jnp.stack([jnp.concatenate(_column_range(w_in_pieces, W_IN_SHARD * s, W_IN_SHARD * (s + 1)), axis=1)
                      for s in range(N_CHIPS)])
    return g_b, g_in


def _column_range(pieces, lo, hi):
    out = []
    for c0, a in pieces:
        a0, a1 = max(lo, c0), min(hi, c0 + a.shape[-1])
        if a0 < a1:
            out.append(a[..., a0 - c0:a1 - c0])
    return out


def kernel(x, norm_mix_g, w_in, conv_w, conv_b, dt_bias, a_log, d_ssd, ssd_norm_g, s5_a_re, s5_a_im, s5_log_dt, s5_b_re, s5_b_im, s5_c_re, s5_c_im, s5_d, s5_glu_w, s5_glu_b, w_branch, w_out, norm_mlp_g, w_mlp_in, w_mlp_out, norm_final_g, loss_target, m_norm_mix_g, m_w_in, m_conv_w, m_conv_b, m_dt_bias, m_a_log, m_d_ssd, m_ssd_norm_g, m_s5_a_re, m_s5_a_im, m_s5_log_dt, m_s5_b_re, m_s5_b_im, m_s5_c_re, m_s5_c_im, m_s5_d, m_s5_glu_w, m_s5_glu_b, m_w_branch, m_w_out, m_norm_mlp_g, m_w_mlp_in, m_w_mlp_out, m_norm_final_g, v_norm_mix_g, v_w_in, v_conv_w, v_conv_b, v_dt_bias, v_a_log, v_d_ssd, v_ssd_norm_g, v_s5_a_re, v_s5_a_im, v_s5_log_dt, v_s5_b_re, v_s5_b_im, v_s5_c_re, v_s5_c_im, v_s5_d, v_s5_glu_w, v_s5_glu_b, v_w_branch, v_w_out, v_norm_mlp_g, v_w_mlp_in, v_w_mlp_out, v_norm_final_g):
    names = ("norm_mix_g", "w_in", "conv_w", "conv_b", "dt_bias", "a_log", "d_ssd", "ssd_norm_g", "s5_a_re", "s5_a_im",
             "s5_log_dt", "s5_b_re", "s5_b_im", "s5_c_re", "s5_c_im", "s5_d", "s5_glu_w", "s5_glu_b", "w_branch", "w_out",
             "norm_mlp_g", "w_mlp_in", "w_mlp_out", "norm_final_g")
    w = dict(zip(names, (norm_mix_g, w_in, conv_w, conv_b, dt_bias, a_log, d_ssd, ssd_norm_g, s5_a_re, s5_a_im, s5_log_dt,
                         s5_b_re, s5_b_im, s5_c_re, s5_c_im, s5_d, s5_glu_w, s5_glu_b, w_branch, w_out, norm_mlp_g,
                         w_mlp_in, w_mlp_out, norm_final_g)))
    m = dict(zip(names, (m_norm_mix_g, m_w_in, m_conv_w, m_conv_b, m_dt_bias, m_a_log, m_d_ssd, m_ssd_norm_g, m_s5_a_re,
                         m_s5_a_im, m_s5_log_dt, m_s5_b_re, m_s5_b_im, m_s5_c_re, m_s5_c_im, m_s5_d, m_s5_glu_w,
                         m_s5_glu_b, m_w_branch, m_w_out, m_norm_mlp_g, m_w_mlp_in, m_w_mlp_out, m_norm_final_g)))
    v = dict(zip(names, (v_norm_mix_g, v_w_in, v_conv_w, v_conv_b, v_dt_bias, v_a_log, v_d_ssd, v_ssd_norm_g, v_s5_a_re,
                         v_s5_a_im, v_s5_log_dt, v_s5_b_re, v_s5_b_im, v_s5_c_re, v_s5_c_im, v_s5_d, v_s5_glu_w,
                         v_s5_glu_b, v_w_branch, v_w_out, v_norm_mlp_g, v_w_mlp_in, v_w_mlp_out, v_norm_final_g)))

    cx, cy, cc = lax.axis_index("x"), lax.axis_index("y"), lax.axis_index("c")
    own = 2 * cx + cy
    place = jnp.stack([own, cc]).astype(jnp.int32)

    src_conv = jnp.concatenate([conv_w, jnp.zeros((CONV_PAD_ROWS - CONV_K, 512), F32)], axis=0)
    all_in, all_conv = _exchange(_Gather([w_in.astype(BF16), src_conv], [ICI_CHUNKS, 1]), "gather_first")
    p = {n: w[n] for n, _ in SMALL}
    p["conv_w"] = jnp.concatenate([all_conv[s, :CONV_K] for s in range(N_CHIPS)], axis=1)
    shards = [(W_IN_SHARD * s, all_in[s]) for s in range(N_CHIPS)]
    p["w_in_perm"] = jnp.concatenate(
        _column_range(shards, 0, OFF_DT) + _column_range(shards, OFF_U, D_IN_PROJ) + _column_range(shards, OFF_DT, OFF_U)
        + [jnp.zeros((D_MODEL, DT_PAD - 16), BF16)], axis=1)

    def late_unpack(gathered):
        all_a, all_glu = gathered
        return {"late_weights": all_a, "s5_glu_w": all_glu.reshape(S5_WIDTH, S5_WIDTH)}

    comm = dict(place=place, late_ks=[ICI_CHUNKS, 1], late_unpack=late_unpack,
                late_srcs=[jnp.concatenate([w[n].astype(BF16) for n, _, _ in ROWS_A], axis=0), s5_glu_w.astype(BF16)])
    loss_part, grad_x, g = _local_step(x[0], loss_target[0], p, comm)

    red_mlp, red_b, red_in = _half_exchange([g["mlp_total"], *g["late_totals"]])
    small_tot = _small_allreduce(_pack_small([g[n] for n, _ in SMALL] + [loss_part.reshape(1)]))
    loss = small_tot.reshape(-1)[SMALL_COUNT]

    grads = _unpack_small(small_tot)
    delta, new_m, new_v = {}, {}, {}
    for n, r0, _ in ROWS_A[:2]:
        delta[n], new_m[n], new_v[n], grads[n] = _adamw(w[n], red_mlp, m[n], v[n], "adamw_" + n, g_row0=r0, with_grad=True)
    for n, r0, _ in ROWS_B:
        delta[n], new_m[n], new_v[n], grads[n] = _adamw(w[n], red_b, m[n], v[n], "adamw_" + n, g_row0=r0, with_grad=True)
    d_t, m_t, v_t, g_t = _adamw(w_in.T, red_in.T, m_w_in.T, v_w_in.T, "adamw_w_in", with_grad=True, col_block=128)
    delta["w_in"], new_m["w_in"], new_v["w_in"], grads["w_in"] = d_t.T, m_t.T, v_t.T, g_t.T
    grads["s5_glu_w"] = red_b[ROW_B_GLU:ROW_B_GLU + GLU_ROWS // N_CHIPS].reshape(S5_WIDTH // N_CHIPS, S5_WIDTH)
    grads["conv_w"] = red_b[ROW_B_CONV:ROW_B_CONV + CONV_ROWS // N_CHIPS].reshape(CONV_K, CONV_DIM // N_CHIPS)
    for n in ("s5_glu_w", "conv_w"):
        delta[n], new_m[n], new_v[n] = _adamw(w[n], grads[n], m[n], v[n], "adamw_" + n)
    ds, ms, vs = _adamw(_pack_small([w[n] for n, _ in SMALL]), small_tot, _pack_small([m[n] for n, _ in SMALL]),
                        _pack_small([v[n] for n, _ in SMALL]), "adamw_small")
    delta.update(_unpack_small(ds))
    new_m.update(_unpack_small(ms))
    new_v.update(_unpack_small(vs))

    return (loss, grad_x[None], *[grads[n] for n in names], *[delta[n] for n in names],
            *[new_m[n] for n in names], *[new_v[n] for n in names])
```

```python
import functools
import math

import jax
import jax.numpy as jnp
from jax import lax
from jax.experimental import pallas as pl
from jax.experimental.pallas import tpu as pltpu

F32 = jnp.float32
BF16 = jnp.bfloat16

D_MODEL = 1024
SSD_INNER = 1024
SSD_HEADS = 16
SSD_HEADDIM = 64
SSD_GROUPS = 4
SSD_HPG = 4
SSD_STATE = 128
SSD_CHUNK = 128
CONV_K = 4
CONV_DIM = 2048
S5_WIDTH = 512
S5_STATES = 2048
S5_BLOCKS = 4
S5_CHUNK = 128
D_FF = 4096
FF_SHARDS = 4
FF_SHARD = D_FF // FF_SHARDS
EPS = 1e-6
P_Z, P_XBC, P_U5, P_G, P_DT, P_END = 0, 1024, 3072, 3584, 5632, 5760
DT_PAD = 128
OFF_DT, OFF_U = 3072, 3088
D_IN_PROJ = 5648

ADAM_LR, ADAM_B1, ADAM_B2, ADAM_EPS, ADAM_WD, ADAM_STEP = 0.001, 0.9, 0.999, 1e-08, 0.01, 10

TOKEN_TILE = 256
VMEM_LIMIT = 56 * 1024 * 1024
HALO = 8
INPROJ_PIECE = 256
CONV_COLS = 256
CONV_ROWS_BLK = 64
WGRAD_TOKENS = 2048


def _pc(body, **kw):
    return pl.pallas_call(body, **kw)


def _cparams(sem=None):
    return pltpu.CompilerParams(dimension_semantics=sem, vmem_limit_bytes=VMEM_LIMIT)


def _dot(a, b):
    return jnp.dot(a, b, preferred_element_type=F32)


def _dot_nt(a, b):
    return lax.dot_general(a, b, (((1,), (1,)), ((), ())), preferred_element_type=F32)


def _dot_tn(a, b):
    return lax.dot_general(a, b, (((0,), (0,)), ((), ())), preferred_element_type=F32)


def _dot_hi(a, b, dims=(((1,), (0,)), ((), ()))):
    return lax.dot_general(a, b, dims, preferred_element_type=F32, precision=lax.Precision.HIGHEST)


def _split_bf16(x, terms):
    out = []
    for _ in range(terms - 1):
        t = x.astype(BF16)
        out.append(t)
        x = x - t.astype(F32)
    out.append(x.astype(BF16))
    return out


def _dot_split(x, onehots, terms, dims=(((1,), (0,)), ((), ()))):
    acc = None
    for t in _split_bf16(x, terms):
        p = lax.dot_general(t, onehots, dims, preferred_element_type=F32)
        acc = p if acc is None else acc + p
    return acc


def _dot_split_rhs(onehots, x, terms, dims=(((1,), (0,)), ((), ()))):
    acc = None
    for t in _split_bf16(x, terms):
        p = lax.dot_general(onehots, t, dims, preferred_element_type=F32)
        acc = p if acc is None else acc + p
    return acc


def _sigmoid(x):
    return 0.5 * jnp.tanh(0.5 * x) + 0.5


def _softplus(x):
    return jnp.maximum(x, 0.0) + jnp.log(1.0 + jnp.exp(-jnp.abs(x)))


_GELU_C = math.sqrt(2.0 / math.pi)


def _gelu(x):
    return 0.5 * x * (1.0 + jnp.tanh(_GELU_C * (x + 0.044715 * x * x * x)))


def _gelu_grad(x):
    t = jnp.tanh(_GELU_C * (x + 0.044715 * x * x * x))
    return 0.5 * (1.0 + t) + 0.5 * x * (1.0 - t * t) * _GELU_C * (1.0 + 3.0 * 0.044715 * x * x)


def _rms(x):
    r = lax.rsqrt(jnp.mean(x * x, axis=-1, keepdims=True) + EPS)
    return x * r, r


def _rms_bwd(xn, r, dxn):
    return r * (dxn - xn * jnp.mean(dxn * xn, axis=-1, keepdims=True))


def _row_spec(tm, width, col=0):
    return pl.BlockSpec((tm, width), lambda i: (i, col))


def _const_spec(shape):
    nd = len(shape)
    return pl.BlockSpec(shape, lambda i: (0,) * nd)


def _hbm_spec():
    return pl.BlockSpec(memory_space=pl.ANY)


def _load_late_weight(wa_hbm, dst_ref, name):
    r0, nr = next((r0, nr) for n, r0, nr in ROWS_A if n == name)
    for s in range(N_CHIPS):
        dst = dst_ref.at[s] if len(dst_ref.shape) == 3 else dst_ref.at[pl.ds(nr * s, nr), :]
        pltpu.sync_copy(wa_hbm.at[s, pl.ds(r0, nr), :], dst)


def _inproj_bwd(x, dx1, dz, dxbc, du5, dgt, ddt, g, wp, rider=None):
    T = x.shape[0]
    tm = TOKEN_TILE

    def body(x_ref, dx1_ref, dz_ref, dxbc_ref, du5_ref, dgt_ref, ddt_ref, g_ref, w_hbm, dx_ref, dg_ref, w_ref):
        @pl.when(pl.program_id(0) == 0)
        def _():
            pltpu.sync_copy(w_hbm, w_ref)
            dg_ref[...] = jnp.zeros_like(dg_ref)

        xn, r = _rms(x_ref[...])
        gv = g_ref[...]
        dh = _dot_nt(dz_ref[...].astype(BF16), w_ref[:, P_Z:P_XBC])
        dh += _dot_nt(dxbc_ref[...].astype(BF16), w_ref[:, P_XBC:P_U5])
        dh += _dot_nt(du5_ref[...].astype(BF16), w_ref[:, P_U5:P_G])
        dh += _dot_nt(dgt_ref[...].astype(BF16), w_ref[:, P_G:P_DT])
        dh += _dot_nt(ddt_ref[...].astype(BF16), w_ref[:, P_DT:P_END])
        dg_ref[...] += jnp.sum(dh * xn, axis=0, keepdims=True)
        dx_ref[...] = dx1_ref[...] + _rms_bwd(xn, r, dh * gv)

    return _call(
        body, rider, name="inproj_bwd", grid=(T // tm,),
        in_specs=[_row_spec(tm, 1024), _row_spec(tm, 1024), _row_spec(tm, 1024), _row_spec(tm, 2048),
                  _row_spec(tm, 512), _row_spec(tm, 2048), _row_spec(tm, DT_PAD), _const_spec((1, 1024)), _hbm_spec()],
        out_specs=[_row_spec(tm, 1024), _const_spec((1, 1024))],
        out_shape=[jax.ShapeDtypeStruct((T, 1024), F32), jax.ShapeDtypeStruct((1, 1024), F32)],
        scratch_shapes=[pltpu.VMEM((D_MODEL, P_END), BF16)],
        compiler_params=_cparams(("arbitrary",)),
    )(x, dx1, dz, dxbc, du5, dgt, ddt, g, wp)


def _conv_fwd(xbc_raw, dt_raw, conv_w, conv_b, dt_bias):
    T = xbc_raw.shape[0]
    tm = TOKEN_TILE

    def body(u_ref, dtr_ref, w_ref, b_ref, db_ref, act_ref, dt_ref, ext_ref):
        @pl.when(pl.program_id(0) == 0)
        def _():
            ext_ref[0:HALO, :] = jnp.zeros((HALO, CONV_DIM), F32)

        ext_ref[HALO:, :] = u_ref[...]
        for c0 in range(0, CONV_DIM, CONV_COLS):
            cols = slice(c0, c0 + CONV_COLS)
            taps = [w_ref[k:k + 1, cols] for k in range(CONV_K)]
            bias = b_ref[:, cols]
            for r0 in range(0, tm, CONV_ROWS_BLK):
                y = bias + taps[0] * ext_ref[pl.ds(HALO - (CONV_K - 1) + r0, CONV_ROWS_BLK), cols]
                for k in range(1, CONV_K):
                    y += taps[k] * ext_ref[pl.ds(HALO - (CONV_K - 1) + k + r0, CONV_ROWS_BLK), cols]
                act_ref[r0:r0 + CONV_ROWS_BLK, cols] = y * _sigmoid(y)
        ext_ref[0:HALO, :] = u_ref[tm - HALO:tm, :]
        dt_ref[...] = _softplus(dtr_ref[...] + db_ref[...])

    return _pc(
        body, name="conv_fwd", grid=(T // tm,),
        in_specs=[_row_spec(tm, CONV_DIM), _row_spec(tm, DT_PAD), _const_spec((CONV_K, CONV_DIM)),
                  _const_spec((1, CONV_DIM)), _const_spec((1, DT_PAD))],
        out_specs=[_row_spec(tm, CONV_DIM), _row_spec(tm, DT_PAD)],
        out_shape=[jax.ShapeDtypeStruct((T, CONV_DIM), F32), jax.ShapeDtypeStruct((T, DT_PAD), F32)],
        scratch_shapes=[pltpu.VMEM((tm + HALO, CONV_DIM), F32)],
        compiler_params=_cparams(("arbitrary",)),
    )(xbc_raw, dt_raw, conv_w, conv_b, dt_bias)


def _conv_bwd(xbc_raw, dt_raw, dxs_a, dxs_b, dB, dC, ddt, conv_w, conv_b, dt_bias):
    T = xbc_raw.shape[0]
    tm = TOKEN_TILE
    n = T // tm
    hb = tm // HALO

    def rev(width):
        return pl.BlockSpec((tm, width), lambda i: (n - 1 - i, 0))

    def body(u_ref, up_ref, dtr_ref, dxa_ref, dxb_ref, dB_ref, dC_ref, ddt_ref, w_ref, b_ref, db_ref,
             du_ref, ddtr_ref, dw_ref, dcb_ref, ddb_ref, ext_ref, dye_ref):
        i = pl.program_id(0)

        @pl.when(i == 0)
        def _():
            dye_ref[tm:, :] = jnp.zeros((HALO, CONV_DIM), F32)
            dw_ref[...] = jnp.zeros_like(dw_ref)
            dcb_ref[...] = jnp.zeros_like(dcb_ref)
            ddb_ref[...] = jnp.zeros_like(ddb_ref)

        first = (i == n - 1).astype(F32)
        ext_ref[0:HALO, :] = up_ref[...] * (1.0 - first)
        ext_ref[HALO:, :] = u_ref[...]
        for c0 in range(0, CONV_DIM, CONV_COLS):
            cols = slice(c0, c0 + CONV_COLS)
            taps = [w_ref[k:k + 1, cols] for k in range(CONV_K)]
            bias = b_ref[:, cols]
            acc_b = jnp.zeros((HALO, CONV_COLS), F32)
            acc_w = [jnp.zeros((HALO, CONV_COLS), F32) for _ in range(CONV_K)]
            for r0 in range(0, tm, CONV_ROWS_BLK):
                rows = slice(r0, r0 + CONV_ROWS_BLK)
                us = [ext_ref[pl.ds(HALO - (CONV_K - 1) + k + r0, CONV_ROWS_BLK), cols] for k in range(CONV_K)]
                y = bias + taps[0] * us[0]
                for k in range(1, CONV_K):
                    y += taps[k] * us[k]
                s = _sigmoid(y)
                if c0 < SSD_INNER:
                    dact = dxa_ref[rows, cols] + dxb_ref[rows, cols]
                elif c0 < SSD_INNER + 512:
                    dact = dB_ref[rows, c0 - SSD_INNER:c0 - SSD_INNER + CONV_COLS]
                else:
                    dact = dC_ref[rows, c0 - SSD_INNER - 512:c0 - SSD_INNER - 512 + CONV_COLS]
                dy = dact * (s * (1.0 + y * (1.0 - s)))
                dye_ref[rows, cols] = dy
                acc_b += jnp.sum(dy.reshape(CONV_ROWS_BLK // HALO, HALO, CONV_COLS), axis=0)
                for k in range(CONV_K):
                    acc_w[k] += jnp.sum((dy * us[k]).reshape(CONV_ROWS_BLK // HALO, HALO, CONV_COLS), axis=0)
            dcb_ref[:, cols] += jnp.sum(acc_b, axis=0, keepdims=True)
            for k in range(CONV_K):
                dw_ref[k:k + 1, cols] += jnp.sum(acc_w[k], axis=0, keepdims=True)
        for c0 in range(0, CONV_DIM, CONV_COLS):
            cols = slice(c0, c0 + CONV_COLS)
            taps = [w_ref[k:k + 1, cols] for k in range(CONV_K)]
            for r0 in range(0, tm, CONV_ROWS_BLK):
                du = taps[0] * dye_ref[pl.ds(CONV_K - 1 + r0, CONV_ROWS_BLK), cols]
                for k in range(1, CONV_K):
                    du += taps[k] * dye_ref[pl.ds(CONV_K - 1 - k + r0, CONV_ROWS_BLK), cols]
                du_ref[r0:r0 + CONV_ROWS_BLK, cols] = du.astype(BF16)
        dye_ref[tm:, :] = dye_ref[0:HALO, :]
        sg = _sigmoid(dtr_ref[...] + db_ref[...])
        ddtr = ddt_ref[...] * sg
        ddtr_ref[...] = ddtr.astype(BF16)
        ddb_ref[...] += jnp.sum(ddtr, axis=0, keepdims=True)

    prev_spec = pl.BlockSpec((HALO, CONV_DIM), lambda i: (jnp.maximum((n - 1 - i) * hb - 1, 0), 0))
    return _pc(
        body, name="conv_bwd", grid=(n,),
        in_specs=[rev(CONV_DIM), prev_spec, rev(DT_PAD), rev(1024), rev(1024), rev(512), rev(512), rev(DT_PAD),
                  _const_spec((CONV_K, CONV_DIM)), _const_spec((1, CONV_DIM)), _const_spec((1, DT_PAD))],
        out_specs=[rev(CONV_DIM), rev(DT_PAD), _const_spec((HALO, CONV_DIM)), _const_spec((1, CONV_DIM)),
                   _const_spec((1, DT_PAD))],
        out_shape=[jax.ShapeDtypeStruct((T, CONV_DIM), BF16), jax.ShapeDtypeStruct((T, DT_PAD), BF16),
                   jax.ShapeDtypeStruct((HALO, CONV_DIM), F32), jax.ShapeDtypeStruct((1, CONV_DIM), F32),
                   jax.ShapeDtypeStruct((1, DT_PAD), F32)],
        scratch_shapes=[pltpu.VMEM((tm + HALO, CONV_DIM), F32), pltpu.VMEM((tm + HALO, CONV_DIM), F32)],
        compiler_params=_cparams(("arbitrary",)),
    )(xbc_raw, xbc_raw, dt_raw, dxs_a, dxs_b, dB, dC, ddt, conv_w, conv_b, dt_bias)


GROUP_LANES = SSD_HPG * SSD_HEADDIM


def _ssd_expanders():
    head = jnp.arange(DT_PAD)[:, None]
    to_wide = (jnp.arange(SSD_INNER)[None, :] // SSD_HEADDIM == head).astype(BF16)
    return to_wide, to_wide.T


def _ssd_prep(dt_ref, alog_ref, wide_ref):
    q = SSD_CHUNK
    a = -jnp.exp(alog_ref[...])
    dtv = dt_ref[...]
    la = dtv * a
    row = lax.broadcasted_iota(jnp.int32, (q, q), 0)
    col = lax.broadcasted_iota(jnp.int32, (q, q), 1)
    tri = (col <= row).astype(BF16)
    cum = _dot_split_rhs(tri, la, 3)
    cum_t = _dot_split(la, tri, 3, (((0,), (1,)), ((), ())))
    dtw = _dot_split(dtv, wide_ref[...], 2)
    cumw = _dot_split(cum, wide_ref[...], 3)
    return a, dtv, row, col, tri, cum_t, dtw, cumw, cum


def _decay(cum, cum_t, h, keep):
    return jnp.where(keep, jnp.exp(jnp.minimum(cum[:, h:h + 1] - cum_t[h:h + 1, :], 0.0)), 0.0)


def _decay_t(cum, cum_t, h, keep_t):
    return jnp.where(keep_t, jnp.exp(jnp.minimum(cum_t[h:h + 1, :] - cum[:, h:h + 1], 0.0)), 0.0)


def _ssd_fwd(xbc_act, dt, alog):
    T = xbc_act.shape[0]
    q = SSD_CHUNK
    nc = T // q
    to_wide, _ = _ssd_expanders()

    def body(xbc_ref, dt_ref, alog_ref, wide_ref, y_ref, sp_ref, st_ref, xd_ref, xde_ref):
        @pl.when(pl.program_id(0) == 0)
        def _():
            st_ref[...] = jnp.zeros_like(st_ref)

        a, dtv, row, col, tri, cum_t, dtw, cumw, segcol = _ssd_prep(dt_ref, alog_ref, wide_ref)
        clw = cumw[q - 1:q, :]
        ecw = jnp.exp(cumw)
        xd = xbc_ref[:, 0:SSD_INNER] * dtw
        xd_ref[...] = xd.astype(BF16)
        xde_ref[...] = (xd * jnp.exp(clw - cumw)).astype(BF16)
        cdw = jnp.exp(clw)
        keep = col <= row
        sp_ref[0] = st_ref[...]
        for g in range(SSD_GROUPS):
            gl = slice(GROUP_LANES * g, GROUP_LANES * (g + 1))
            bb = xbc_ref[:, 1024 + 128 * g:1152 + 128 * g].astype(BF16)
            cb = xbc_ref[:, 1536 + 128 * g:1664 + 128 * g].astype(BF16)
            gm = _dot_nt(cb, bb)
            stp = st_ref[g]
            yoff = _dot(cb, stp.astype(BF16)) * ecw[:, gl]
            for r in range(SSD_HPG):
                h = SSD_HPG * g + r
                m = (gm * _decay(segcol, cum_t, h, keep)).astype(BF16)
                y_ref[:, 64 * h:64 * h + 64] = _dot(m, xd_ref[:, 64 * h:64 * h + 64]) + yoff[:, 64 * r:64 * r + 64]
            st_ref[g] = stp * cdw[:, gl] + _dot_tn(bb, xde_ref[:, gl])

    return _pc(
        body, name="ssd_fwd", grid=(nc,),
        in_specs=[_row_spec(q, CONV_DIM), _row_spec(q, DT_PAD), _const_spec((1, DT_PAD)),
                  _const_spec(to_wide.shape)],
        out_specs=[_row_spec(q, SSD_INNER),
                   pl.BlockSpec((1, SSD_GROUPS, SSD_STATE, GROUP_LANES), lambda i: (i, 0, 0, 0))],
        out_shape=[jax.ShapeDtypeStruct((T, SSD_INNER), F32),
                   jax.ShapeDtypeStruct((nc, SSD_GROUPS, SSD_STATE, GROUP_LANES), F32)],
        scratch_shapes=[pltpu.VMEM((SSD_GROUPS, SSD_STATE, GROUP_LANES), F32), pltpu.VMEM((q, SSD_INNER), BF16),
                        pltpu.VMEM((q, SSD_INNER), BF16)],
        compiler_params=_cparams(("arbitrary",)),
    )(xbc_act, dt, alog, to_wide)


def _ssd_bwd(xbc_act, dt, alog, sprev, dy):
    T = xbc_act.shape[0]
    q = SSD_CHUNK
    nc = T // q
    to_wide, to_heads = _ssd_expanders()

    def rev(width):
        return pl.BlockSpec((q, width), lambda i: (nc - 1 - i, 0))

    def body(xbc_ref, dt_ref, alog_ref, sp_ref, dy_ref, wide_ref, heads_ref,
             dxs_ref, dB_ref, dC_ref, ddt_ref, dalog_ref, ds_ref, xd_ref, dxd_ref):
        i = pl.program_id(0)

        @pl.when(i == 0)
        def _():
            ds_ref[...] = jnp.zeros_like(ds_ref)
            dalog_ref[...] = jnp.zeros_like(dalog_ref)

        a, dtv, row, col, tri, cum_t, dtw, cumw, segcol = _ssd_prep(dt_ref, alog_ref, wide_ref)
        clw = cumw[q - 1:q, :]
        ecw = jnp.exp(cumw)
        dew = jnp.exp(clw - cumw)
        cdw = jnp.exp(clw)
        xs = xbc_ref[:, 0:SSD_INNER]
        xd = xs * dtw
        xd_ref[...] = xd.astype(BF16)
        dyv = dy_ref[...]
        dye = (dyv * ecw).astype(BF16)
        xde = (xd * dew).astype(BF16)
        keep = col <= row
        keep_t = col >= row
        rows_k = lax.broadcasted_iota(jnp.int32, (SSD_HPG * q, DT_PAD), 0) // q
        lanes_k = lax.broadcasted_iota(jnp.int32, (SSD_HPG * q, DT_PAD), 1)
        dcw_parts = []
        dcum = jnp.zeros((q, DT_PAD), F32)
        for g in range(SSD_GROUPS):
            gl = slice(GROUP_LANES * g, GROUP_LANES * (g + 1))
            bb = xbc_ref[:, 1024 + 128 * g:1152 + 128 * g].astype(BF16)
            cb = xbc_ref[:, 1536 + 128 * g:1664 + 128 * g].astype(BF16)
            gm = _dot_nt(cb, bb)
            gmt = _dot_nt(bb, cb)
            stp = sp_ref[0, g]
            dst = ds_ref[g]
            stpb = stp.astype(BF16)
            dstb = dst.astype(BF16)
            yoff = _dot(cb, stpb) * ecw[:, gl]
            dcg = _dot_nt(dye[:, gl], stpb)
            ds_ref[g] = dst * cdw[:, gl] + _dot_tn(cb, dye[:, gl])
            dlast = jnp.sum(dst * stp, axis=0, keepdims=True) * cdw[:, gl]
            dbg = _dot_nt(xde[:, gl], dstb)
            w = _dot(bb, dstb) * dew[:, gl]
            wx = w * xd[:, gl]
            dlast = dlast + jnp.sum(wx, axis=0, keepdims=True)
            dcw_parts.append(dyv[:, gl] * yoff - wx
                             + jnp.where(lax.broadcasted_iota(jnp.int32, (q, 1), 0) == q - 1, dlast, 0.0))
            dgm = jnp.zeros((q, q), F32)
            diag = []
            for r in range(SSD_HPG):
                h = SSD_HPG * g + r
                hl = slice(64 * h, 64 * h + 64)
                dyb = dy_ref[:, hl].astype(BF16)
                xdh = xd_ref[:, hl]
                dm = _dot_nt(dyb, xdh)
                dmt = _dot_nt(xdh, dyb)
                dec = _decay(segcol, cum_t, h, keep)
                mt = gmt * _decay_t(segcol, cum_t, h, keep_t)
                dgm += dm * dec
                diag.append(dm * (gm * dec) - dmt * mt)
                dxd_ref[:, hl] = _dot(mt.astype(BF16), dyb) + w[:, 64 * r:64 * r + 64]
            onehots = (lanes_k == SSD_HPG * g + rows_k).astype(BF16)
            dcum += _dot_split(jnp.concatenate(diag, axis=1), onehots, 2)
            dgb = dgm.astype(BF16)
            dC_ref[:, 128 * g:128 * g + 128] = dcg + _dot(dgb, bb)
            dB_ref[:, 128 * g:128 * g + 128] = dbg + _dot_tn(dgb, cb)
        dxd = dxd_ref[...]
        dxs_ref[...] = dxd * dtw
        dcum += _dot_split(jnp.concatenate(dcw_parts, axis=1), heads_ref[...], 2)
        dla = _dot_split_rhs(tri, dcum, 3, (((0,), (0,)), ((), ())))
        ddt_ref[...] = _dot_split(xs * dxd, heads_ref[...], 2) + dla * a
        dalog_ref[...] += jnp.sum(dla * dtv, axis=0, keepdims=True)

        @pl.when(i == nc - 1)
        def _():
            dalog_ref[...] = dalog_ref[...] * a

    st_spec = pl.BlockSpec((1, SSD_GROUPS, SSD_STATE, GROUP_LANES), lambda i: (nc - 1 - i, 0, 0, 0))
    return _pc(
        body, name="ssd_bwd", grid=(nc,),
        in_specs=[rev(CONV_DIM), rev(DT_PAD), _const_spec((1, DT_PAD)), st_spec, rev(SSD_INNER),
                  _const_spec(to_wide.shape), _const_spec(to_heads.shape)],
        out_specs=[rev(SSD_INNER), rev(512), rev(512), rev(DT_PAD), _const_spec((1, DT_PAD))],
        out_shape=[jax.ShapeDtypeStruct((T, SSD_INNER), F32), jax.ShapeDtypeStruct((T, 512), F32),
                   jax.ShapeDtypeStruct((T, 512), F32), jax.ShapeDtypeStruct((T, DT_PAD), F32),
                   jax.ShapeDtypeStruct((1, DT_PAD), F32)],
        scratch_shapes=[pltpu.VMEM((SSD_GROUPS, SSD_STATE, GROUP_LANES), F32), pltpu.VMEM((q, SSD_INNER), BF16),
                        pltpu.VMEM((q, SSD_INNER), F32)],
        compiler_params=_cparams(("arbitrary",)),
    )(xbc_act, dt, alog, sprev, dy, to_wide, to_heads)


def _s5_disc_vals(a_re, a_im, log_dt, b_re, b_im):
    dt = jnp.exp(log_dt)
    mag = jnp.exp(a_re * dt)
    ab_re = mag * jnp.cos(a_im * dt)
    ab_im = mag * jnp.sin(a_im * dt)
    den = a_re * a_re + a_im * a_im
    nr = ab_re - 1.0
    ni = ab_im
    coef_re = (nr * a_re + ni * a_im) / den
    coef_im = (ni * a_re - nr * a_im) / den
    bb_re = coef_re * b_re - coef_im * b_im
    bb_im = coef_re * b_im + coef_im * b_re
    return ab_re, ab_im, bb_re, bb_im


def _s5_disc(a_re, a_im, log_dt, b_re, b_im):
    def body(ar, ai, ld, br, bi, o1, o2, o3, o4):
        o1[...], o2[...], o3[...], o4[...] = _s5_disc_vals(ar[...], ai[...], ld[...], br[...], bi[...])

    return _pc(
        body, name="s5_disc",
        out_shape=[jax.ShapeDtypeStruct((1, S5_STATES), F32), jax.ShapeDtypeStruct((1, S5_STATES), F32),
                   jax.ShapeDtypeStruct((16, S5_STATES), F32), jax.ShapeDtypeStruct((16, S5_STATES), F32)],
    )(a_re, a_im, log_dt, b_re, b_im)


def _s5_disc_bwd(a_re, a_im, log_dt, b_re, b_im, d_ab_re, d_ab_im, d_bb_re, d_bb_im):
    def body(ar, ai, ld, br, bi, g1, g2, g3, g4, o1, o2, o3, o4, o5):
        _, vjp = jax.vjp(_s5_disc_vals, ar[...], ai[...], ld[...], br[...], bi[...])
        d1, d2, d3, d4, d5 = vjp((g1[...], g2[...], g3[...], g4[...]))
        o1[...] = d1
        o2[...] = d2
        st = lax.broadcasted_iota(jnp.int32, (S5_STATES, DT_PAD), 0)
        grp = lax.broadcasted_iota(jnp.int32, (S5_STATES, DT_PAD), 1)
        sel = (st // 64 == grp).astype(F32)
        o3[...] = _dot_hi(d3, sel)
        o4[...] = d4
        o5[...] = d5

    return _pc(
        body, name="s5_disc_bwd",
        out_shape=[jax.ShapeDtypeStruct((1, S5_STATES), F32), jax.ShapeDtypeStruct((1, S5_STATES), F32),
                   jax.ShapeDtypeStruct((1, DT_PAD), F32),
                   jax.ShapeDtypeStruct((16, S5_STATES), F32), jax.ShapeDtypeStruct((16, S5_STATES), F32)],
    )(a_re, a_im, log_dt, b_re, b_im, d_ab_re, d_ab_im, d_bb_re, d_bb_im)


def _cmul_add(xr, xi, pr, pi, yr, yi):
    return xr + pr * yr - pi * yi, xi + pr * yi + pi * yr


def _powers(ar, ai, n):
    out = [(ar, ai)]
    for _ in range(n - 1):
        pr, pi = out[-1]
        out.append((pr * pr - pi * pi, 2.0 * pr * pi))
    return out


_BW = S5_STATES // S5_BLOCKS
_BI = S5_WIDTH // S5_BLOCKS
SUB = 8
S5_ROWS = S5_CHUNK // SUB


S5_TAB_ROWS = 8 * SUB


def _scan8(br, bi, tab_ref, reverse):
    for level, k in enumerate((1, 2, 4)):
        r0 = 2 * SUB * (level + 1)
        shift = SUB - k if reverse else k
        br, bi = _cmul_add(br, bi, tab_ref[r0:r0 + SUB, :], tab_ref[r0 + SUB:r0 + 2 * SUB, :],
                           pltpu.roll(br, shift, 0), pltpu.roll(bi, shift, 0))
    return br, bi


def _s5_tables(ab_ref, tab_ref, reverse):
    rowin = lax.broadcasted_iota(jnp.int32, (SUB, 1), 0)
    ar = ab_ref[0:1, :]
    ai = -ab_ref[1:2, :] if reverse else ab_ref[1:2, :]
    zero = jnp.zeros((SUB, S5_STATES), F32)
    for level, (pr, pi) in enumerate(_powers(ar, ai, 3)):
        k = 2 ** level
        keep = (rowin < SUB - k) if reverse else (rowin >= k)
        r0 = 2 * SUB * (level + 1)
        tab_ref[r0:r0 + SUB, :] = jnp.where(keep, pr, 0.0) + zero
        tab_ref[r0 + SUB:r0 + 2 * SUB, :] = jnp.where(keep, pi, 0.0) + zero
    hit = rowin == (SUB - 1 if reverse else 0)
    pr, pi = _scan8(jnp.where(hit, ar, 0.0) + zero, jnp.where(hit, ai, 0.0) + zero, tab_ref, reverse)
    tab_ref[0:SUB, :] = pr
    tab_ref[SUB:2 * SUB, :] = pi


def _s5_fwd_chunk(u_ref, y_ref, r0, wb_ref, wc_ref, d_ref, carry_ref, tab_ref, sr_ref, si_ref, between):
    q = S5_CHUNK
    rows = slice(r0, r0 + q)
    for j in range(S5_BLOCKS):
        bu = _dot(u_ref[rows, _BI * j:_BI * (j + 1)].astype(BF16), wb_ref[j])
        sr_ref[:, :, _BW * j:_BW * (j + 1)] = bu[:, :_BW].reshape(S5_ROWS, SUB, _BW)
        si_ref[:, :, _BW * j:_BW * (j + 1)] = bu[:, _BW:].reshape(S5_ROWS, SUB, _BW)
    tr, ti = tab_ref[0:SUB, :], tab_ref[SUB:2 * SUB, :]
    cr, ci = carry_ref[0:1, :], carry_ref[1:2, :]
    for k in range(S5_ROWS):
        sr, si = _scan8(sr_ref[k], si_ref[k], tab_ref, False)
        sr, si = _cmul_add(sr, si, tr, ti, cr, ci)
        sr_ref[k] = sr
        si_ref[k] = si
        cr, ci = sr[SUB - 1:SUB, :], si[SUB - 1:SUB, :]
        between()
    carry_ref[0:1, :] = cr
    carry_ref[1:2, :] = ci
    for j in range(S5_BLOCKS):
        sl = slice(_BW * j, _BW * (j + 1))
        ul = slice(_BI * j, _BI * (j + 1))
        s = jnp.concatenate([sr_ref[:, :, sl].reshape(q, _BW), si_ref[:, :, sl].reshape(q, _BW)], axis=1).astype(BF16)
        y_ref[rows, ul] = _dot(s, wc_ref[j]) + d_ref[:, ul] * u_ref[rows, ul]


def _inproj_s5_fwd(x, g, wp, wb4, wc4, ab, dvec, rider=None):
    T = x.shape[0]
    tm = TOKEN_TILE
    per_tile = tm // S5_CHUNK
    nc = T // S5_CHUNK

    def body(x_ref, g_ref, w_hbm, wb_ref, wc_ref, ab_ref, d_ref, z_ref, xbc_ref, u5_ref, gt_ref, dt_ref, h_ref, y_ref, sp_ref,
             w_ref, carry_ref, tab_ref, sr_ref, si_ref):
        @pl.when(pl.program_id(0) == 0)
        def _():
            pltpu.sync_copy(w_hbm, w_ref)
            carry_ref[...] = jnp.zeros_like(carry_ref)
            _s5_tables(ab_ref, tab_ref, False)

        xn, _ = _rms(x_ref[...])
        h = (xn * g_ref[...]).astype(BF16)
        h_ref[...] = h
        u5_ref[...] = _dot(h, w_ref[:, P_U5:P_G])
        pieces = [(z_ref, P_Z, c0) for c0 in range(0, P_XBC - P_Z, INPROJ_PIECE)]
        pieces += [(xbc_ref, P_XBC, c0) for c0 in range(0, P_U5 - P_XBC, INPROJ_PIECE)]
        pieces += [(gt_ref, P_G, c0) for c0 in range(0, P_DT - P_G, INPROJ_PIECE)]
        todo = iter(pieces)
        slabs, calls = per_tile * S5_ROWS, [0]

        def between():
            calls[0] += 1
            if (calls[0] * len(pieces)) // slabs > ((calls[0] - 1) * len(pieces)) // slabs:
                o_ref, base, c0 = next(todo)
                o_ref[:, c0:c0 + INPROJ_PIECE] = _dot(h, w_ref[:, base + c0:base + c0 + INPROJ_PIECE])

        dt_ref[...] = _dot(h, w_ref[:, P_DT:P_END])
        for c in range(per_tile):
            sp_ref[c] = carry_ref[...]
            _s5_fwd_chunk(u5_ref, y_ref, S5_CHUNK * c, wb_ref, wc_ref, d_ref, carry_ref, tab_ref, sr_ref, si_ref, between)
        assert next(todo, None) is None

    widths = (1024, 2048, 512, 2048, DT_PAD)
    return _call(
        body, rider, name="inproj_s5_fwd", grid=(T // tm,),
        in_specs=[_row_spec(tm, D_MODEL), _const_spec((1, D_MODEL)), _hbm_spec(),
                  _const_spec((S5_BLOCKS, _BI, 2 * _BW)), _const_spec((S5_BLOCKS, 2 * _BW, _BI)),
                  _const_spec((8, S5_STATES)), _const_spec((1, S5_WIDTH))],
        out_specs=[_row_spec(tm, w) for w in widths] + [_row_spec(tm, D_MODEL), _row_spec(tm, S5_WIDTH),
                                                         pl.BlockSpec((per_tile, 8, S5_STATES), lambda i: (i, 0, 0))],
        out_shape=[jax.ShapeDtypeStruct((T, w), F32) for w in widths] + [
            jax.ShapeDtypeStruct((T, D_MODEL), BF16), jax.ShapeDtypeStruct((T, S5_WIDTH), F32),
            jax.ShapeDtypeStruct((nc, 8, S5_STATES), F32)],
        scratch_shapes=[pltpu.VMEM((D_MODEL, P_END), BF16), pltpu.VMEM((8, S5_STATES), F32),
                        pltpu.VMEM((S5_TAB_ROWS, S5_STATES), F32), pltpu.VMEM((S5_ROWS, SUB, S5_STATES), F32),
                        pltpu.VMEM((S5_ROWS, SUB, S5_STATES), F32)],
        compiler_params=_cparams(("arbitrary",)),
    )(x, g, wp, wb4, wc4, ab, dvec)


def _s5_bwd(u5, dy5, wb4, wc4, ab, dvec, sprev, rider=None):
    T = u5.shape[0]
    q = S5_CHUNK
    nc = T // q

    def rev(width):
        return pl.BlockSpec((q, width), lambda i: (nc - 1 - i, 0))

    def body(u_ref, dy_ref, wb_ref, wc_ref, ab_ref, d_ref, sp_ref, du_ref, dwb_ref, dwc_ref, dab_ref, dd_ref,
             carry_ref, tab_ref, rtab_ref, sr_ref, si_ref, lr_ref, li_ref):
        i = pl.program_id(0)
        rowin = lax.broadcasted_iota(jnp.int32, (SUB, 1), 0)

        @pl.when(i == 0)
        def _():
            carry_ref[...] = jnp.zeros_like(carry_ref)
            dwb_ref[...] = jnp.zeros_like(dwb_ref)
            dwc_ref[...] = jnp.zeros_like(dwc_ref)
            dab_ref[...] = jnp.zeros_like(dab_ref)
            dd_ref[...] = jnp.zeros_like(dd_ref)
            _s5_tables(ab_ref, tab_ref, False)
            _s5_tables(ab_ref, rtab_ref, True)

        for j in range(S5_BLOCKS):
            sl = slice(_BW * j, _BW * (j + 1))
            ul = slice(_BI * j, _BI * (j + 1))
            bu = _dot(u_ref[:, ul].astype(BF16), wb_ref[j])
            sr_ref[:, :, sl] = bu[:, :_BW].reshape(S5_ROWS, SUB, _BW)
            si_ref[:, :, sl] = bu[:, _BW:].reshape(S5_ROWS, SUB, _BW)
            ds = _dot_nt(dy_ref[:, ul].astype(BF16), wc_ref[j])
            lr_ref[:, :, sl] = ds[:, :_BW].reshape(S5_ROWS, SUB, _BW)
            li_ref[:, :, sl] = ds[:, _BW:].reshape(S5_ROWS, SUB, _BW)
        ar, ai = ab_ref[0:1, :], ab_ref[1:2, :]
        tr, ti = tab_ref[0:SUB, :], tab_ref[SUB:2 * SUB, :]
        cr, ci = sp_ref[0, 0:1, :], sp_ref[0, 1:2, :]
        for k in range(S5_ROWS):
            sr, si = _scan8(sr_ref[k], si_ref[k], tab_ref, False)
            sr, si = _cmul_add(sr, si, tr, ti, cr, ci)
            sr_ref[k] = sr
            si_ref[k] = si
            cr, ci = sr[SUB - 1:SUB, :], si[SUB - 1:SUB, :]
        tr, ti = rtab_ref[0:SUB, :], rtab_ref[SUB:2 * SUB, :]
        cr, ci = carry_ref[0:1, :], carry_ref[1:2, :]
        acc_r = jnp.zeros((SUB, S5_STATES), F32)
        acc_i = jnp.zeros((SUB, S5_STATES), F32)
        for k in reversed(range(S5_ROWS)):
            lr, li = _scan8(lr_ref[k], li_ref[k], rtab_ref, True)
            lr, li = _cmul_add(lr, li, tr, ti, cr, ci)
            lr_ref[k] = lr
            li_ref[k] = li
            cr, ci = lr[0:1, :], li[0:1, :]
            if k > 0:
                before_r, before_i = sr_ref[k - 1, SUB - 1:SUB, :], si_ref[k - 1, SUB - 1:SUB, :]
            else:
                before_r, before_i = sp_ref[0, 0:1, :], sp_ref[0, 1:2, :]
            keep = rowin >= 1
            pr = jnp.where(keep, pltpu.roll(sr_ref[k], 1, 0), before_r)
            pi = jnp.where(keep, pltpu.roll(si_ref[k], 1, 0), before_i)
            acc_r += lr * pr + li * pi
            acc_i += li * pr - lr * pi
        carry_ref[0:1, :] = cr
        carry_ref[1:2, :] = ci
        dab_ref[0:1, :] += jnp.sum(acc_r, axis=0, keepdims=True)
        dab_ref[1:2, :] += jnp.sum(acc_i, axis=0, keepdims=True)
        for j in range(S5_BLOCKS):
            sl = slice(_BW * j, _BW * (j + 1))
            ul = slice(_BI * j, _BI * (j + 1))
            u = u_ref[:, ul]
            dy = dy_ref[:, ul]
            dyb = dy.astype(BF16)
            lam = jnp.concatenate([lr_ref[:, :, sl].reshape(q, _BW), li_ref[:, :, sl].reshape(q, _BW)], axis=1).astype(BF16)
            s = jnp.concatenate([sr_ref[:, :, sl].reshape(q, _BW), si_ref[:, :, sl].reshape(q, _BW)], axis=1).astype(BF16)
            du_ref[:, ul] = (_dot_nt(lam, wb_ref[j]) + d_ref[:, ul] * dy).astype(BF16)
            dwb_ref[j] += _dot_tn(u.astype(BF16), lam)
            dwc_ref[j] += _dot_tn(s, dyb)
            dd_ref[:, ul] += jnp.sum(dy * u, axis=0, keepdims=True)

    big = pltpu.VMEM((S5_ROWS, SUB, S5_STATES), F32)
    return _call(
        body, rider, name="s5_bwd", grid=(nc,),
        in_specs=[rev(S5_WIDTH), rev(S5_WIDTH), _const_spec((S5_BLOCKS, _BI, 2 * _BW)), _const_spec((S5_BLOCKS, 2 * _BW, _BI)),
                  _const_spec((8, S5_STATES)), _const_spec((1, S5_WIDTH)),
                  pl.BlockSpec((1, 8, S5_STATES), lambda i: (nc - 1 - i, 0, 0))],
        out_specs=[rev(S5_WIDTH), _const_spec((S5_BLOCKS, _BI, 2 * _BW)), _const_spec((S5_BLOCKS, 2 * _BW, _BI)),
                   _const_spec((8, S5_STATES)), _const_spec((1, S5_WIDTH))],
        out_shape=[jax.ShapeDtypeStruct((T, S5_WIDTH), BF16), jax.ShapeDtypeStruct((S5_BLOCKS, _BI, 2 * _BW), F32),
                   jax.ShapeDtypeStruct((S5_BLOCKS, 2 * _BW, _BI), F32), jax.ShapeDtypeStruct((8, S5_STATES), F32),
                   jax.ShapeDtypeStruct((1, S5_WIDTH), F32)],
        scratch_shapes=[pltpu.VMEM((8, S5_STATES), F32), pltpu.VMEM((S5_TAB_ROWS, S5_STATES), F32),
                        pltpu.VMEM((S5_TAB_ROWS, S5_STATES), F32), big, big, big, big],
        compiler_params=_cparams(("arbitrary",)),
    )(u5, dy5, wb4, wc4, ab, dvec, sprev)


def _merge_vals(ys, xs, z, y5, gates, dvec, gssd, glu_w, glu_b, wbr):
    sz = _sigmoid(z)
    qv = ys + dvec * xs
    pre = qv * (z * sz)
    yn, rs = [], []
    for gi in range(SSD_GROUPS):
        p, r = _rms(pre[:, 256 * gi:256 * (gi + 1)])
        yn.append(p)
        rs.append(r)
    yn = jnp.concatenate(yn, axis=1)
    ya = yn * gssd
    gel = _gelu(y5)
    sg = _sigmoid(_dot(gel.astype(BF16), glu_w) + glu_b)
    yb = gel * sg
    pa = _dot(ya.astype(BF16), wbr[0:SSD_INNER, :])
    pb = _dot(yb.astype(BF16), wbr[SSD_INNER:, :])
    s0 = _sigmoid(gates[:, :D_MODEL])
    s1 = _sigmoid(gates[:, D_MODEL:])
    merged = s0 * pa + s1 * pb
    return dict(sz=sz, qv=qv, yn=yn, rs=rs, ya=ya, gel=gel, sg=sg, yb=yb, pa=pa, pb=pb, s0=s0, s1=s1, merged=merged)


def _merge_specs(tm):
    acts = [_row_spec(tm, 1024), _row_spec(tm, 1024, 0), _row_spec(tm, 1024), _row_spec(tm, 512), _row_spec(tm, 2048),
            _row_spec(tm, 1024)]
    params = [_const_spec((1, 1024)), _const_spec((1, 1024)), _const_spec((512, 512)), _const_spec((1, 512)), _hbm_spec()]
    return acts, params


def _merge_fwd(ys, xbc_act, z, y5, gates, x, dvec, gssd, glu_w, glu_b, wa):
    T = x.shape[0]
    tm = TOKEN_TILE
    acts, params = _merge_specs(tm)

    def body(ys_ref, xs_ref, z_ref, y5_ref, gt_ref, x_ref, dv_ref, gs_ref, gw_ref, gb_ref, wa_hbm, x1_ref,
             wbr_ref, wout_ref):
        @pl.when(pl.program_id(0) == 0)
        def _():
            _load_late_weight(wa_hbm, wbr_ref, "w_branch")
            _load_late_weight(wa_hbm, wout_ref, "w_out")

        v = _merge_vals(ys_ref[...], xs_ref[...], z_ref[...], y5_ref[...], gt_ref[...], dv_ref[...], gs_ref[...],
                        gw_ref[...], gb_ref[...], wbr_ref)
        x1_ref[...] = x_ref[...] + _dot(v["merged"].astype(BF16), wout_ref[...])

    return _pc(
        body, name="merge_fwd", grid=(T // tm,),
        in_specs=acts + params, out_specs=_row_spec(tm, 1024),
        out_shape=jax.ShapeDtypeStruct((T, 1024), F32),
        scratch_shapes=[pltpu.VMEM((1536, 1024), BF16), pltpu.VMEM((1024, 1024), BF16)],
        compiler_params=_cparams(("arbitrary",)),
    )(ys, xbc_act, z, y5, gates, x, dvec, gssd, glu_w, glu_b, wa)


def _merge_bwd(ys, xbc_act, z, y5, gates, dx1, dvec, gssd, glu_w, glu_b, wa, head_sel, rider=None):
    T = dx1.shape[0]
    tm = TOKEN_TILE
    acts, params = _merge_specs(tm)

    def body(ys_ref, xs_ref, z_ref, y5_ref, gt_ref, dx1_ref, dv_ref, gs_ref, gw_ref, gb_ref, wa_hbm, hs_ref,
             dys_ref, dxs_ref, dz_ref, dy5_ref, dgt_ref, mg_ref, ya_ref, yb_ref, dpa_ref, dpb_ref, gel_ref, dpre_ref,
             ddv_ref, dgs_ref, dgb_ref, wbr_ref, wout_ref, ddacc_ref):
        i = pl.program_id(0)

        @pl.when(i == 0)
        def _():
            _load_late_weight(wa_hbm, wbr_ref, "w_branch")
            _load_late_weight(wa_hbm, wout_ref, "w_out")
            ddacc_ref[...] = jnp.zeros_like(ddacc_ref)
            dgs_ref[...] = jnp.zeros_like(dgs_ref)
            dgb_ref[...] = jnp.zeros_like(dgb_ref)

        ys, xs, z, y5, gates = ys_ref[...], xs_ref[...], z_ref[...], y5_ref[...], gt_ref[...]
        dvv, gsv, gw = dv_ref[...], gs_ref[...], gw_ref[...]
        v = _merge_vals(ys, xs, z, y5, gates, dvv, gsv, gw, gb_ref[...], wbr_ref)
        dmg = _dot_nt(dx1_ref[...].astype(BF16), wout_ref[...])
        s0, s1, pa, pb = v["s0"], v["s1"], v["pa"], v["pb"]
        dgt_ref[:, :D_MODEL] = (dmg * pa * s0 * (1.0 - s0)).astype(BF16)
        dgt_ref[:, D_MODEL:] = (dmg * pb * s1 * (1.0 - s1)).astype(BF16)
        dpa = (dmg * s0).astype(BF16)
        dpb = (dmg * s1).astype(BF16)
        dya = _dot_nt(dpa, wbr_ref[0:SSD_INNER, :])
        dyb = _dot_nt(dpb, wbr_ref[SSD_INNER:, :])
        gel, sg = v["gel"], v["sg"]
        dpre = (dyb * gel * sg * (1.0 - sg))
        dgb_ref[...] += jnp.sum(dpre, axis=0, keepdims=True)
        dpre_b = dpre.astype(BF16)
        dgel = dyb * sg + _dot_nt(dpre_b, gw)
        dy5_ref[...] = dgel * _gelu_grad(y5)
        yn = v["yn"]
        dgs_ref[...] += jnp.sum(dya * yn, axis=0, keepdims=True)
        dyn = dya * gsv
        dpre_a = jnp.concatenate(
            [_rms_bwd(yn[:, 256 * gi:256 * (gi + 1)], v["rs"][gi], dyn[:, 256 * gi:256 * (gi + 1)])
             for gi in range(SSD_GROUPS)], axis=1)
        sz, qv = v["sz"], v["qv"]
        dq = dpre_a * (z * sz)
        dz_ref[...] = (dpre_a * qv * (sz * (1.0 + z * (1.0 - sz)))).astype(BF16)
        dys_ref[...] = dq
        dxs_ref[...] = dq * dvv
        ddacc_ref[...] += jnp.sum(dq * xs, axis=0, keepdims=True)
        mg_ref[...] = v["merged"].astype(BF16)
        ya_ref[...] = v["ya"].astype(BF16)
        yb_ref[...] = v["yb"].astype(BF16)
        dpa_ref[...] = dpa
        dpb_ref[...] = dpb
        gel_ref[...] = gel.astype(BF16)
        dpre_ref[...] = dpre_b

        @pl.when(i == pl.num_programs(0) - 1)
        def _():
            ddv_ref[...] = _dot_hi(ddacc_ref[...], hs_ref[...])

    outs = [(1024, F32), (1024, F32), (1024, BF16), (512, F32), (2048, BF16),
            (1024, BF16), (1024, BF16), (512, BF16), (1024, BF16), (1024, BF16), (512, BF16), (512, BF16)]
    return _call(
        body, rider, name="merge_bwd", grid=(T // tm,),
        in_specs=acts + params + [_const_spec((1024, DT_PAD))],
        out_specs=[_row_spec(tm, w) for w, _ in outs] + [_const_spec((1, DT_PAD)), _const_spec((1, 1024)), _const_spec((1, 512))],
        out_shape=[jax.ShapeDtypeStruct((T, w), d) for w, d in outs] + [
            jax.ShapeDtypeStruct((1, DT_PAD), F32), jax.ShapeDtypeStruct((1, 1024), F32), jax.ShapeDtypeStruct((1, 512), F32)],
        scratch_shapes=[pltpu.VMEM((1536, 1024), BF16), pltpu.VMEM((1024, 1024), BF16), pltpu.VMEM((1, 1024), F32)],
        compiler_params=_cparams(("arbitrary",)),
    )(ys, xbc_act, z, y5, gates, dx1, dvec, gssd, glu_w, glu_b, wa, head_sel)


def _mlp_fwd_loss(x1, target, g, g_fin, wa):
    T = x1.shape[0]
    tm = TOKEN_TILE

    def body(x_ref, t_ref, g_ref, gf_ref, wa_hbm, dx_ref, loss_ref, dg_ref, w1_ref, w2_ref):
        @pl.when(pl.program_id(0) == 0)
        def _():
            _load_late_weight(wa_hbm, w1_ref, "w_mlp_in")
            _load_late_weight(wa_hbm, w2_ref, "w_mlp_out")
            loss_ref[...] = jnp.zeros_like(loss_ref)
            dg_ref[...] = jnp.zeros_like(dg_ref)

        xv = x_ref[...]
        xn, _ = _rms(xv)
        h = (xn * g_ref[...]).astype(BF16)
        acc = xv
        for s in range(FF_SHARDS):
            rl = jnp.maximum(_dot(h, w1_ref[s]), 0.0)
            acc += _dot((rl * rl).astype(BF16), w2_ref[FF_SHARD * s:FF_SHARD * (s + 1), :])
        yn, r = _rms(acc)
        gv = gf_ref[...]
        err = yn * gv - t_ref[...]
        loss_ref[...] += jnp.sum(err * err, axis=0, keepdims=True) * (0.5 / D_MODEL)
        dy = err * (1.0 / D_MODEL)
        dg_ref[...] += jnp.sum(dy * yn, axis=0, keepdims=True)
        dx_ref[...] = _rms_bwd(yn, r, dy * gv)

    return _pc(
        body, name="mlp_fwd_loss", grid=(T // tm,),
        in_specs=[_row_spec(tm, 1024), _row_spec(tm, 1024), _const_spec((1, 1024)), _const_spec((1, 1024)), _hbm_spec()],
        out_specs=[_row_spec(tm, 1024), _const_spec((1, 1024)), _const_spec((1, 1024))],
        out_shape=[jax.ShapeDtypeStruct((T, 1024), F32), jax.ShapeDtypeStruct((1, 1024), F32),
                   jax.ShapeDtypeStruct((1, 1024), F32)],
        scratch_shapes=[pltpu.VMEM((FF_SHARDS, D_MODEL, FF_SHARD), BF16), pltpu.VMEM((D_FF, D_MODEL), BF16)],
        compiler_params=_cparams(("arbitrary",)),
    )(x1, target, g, g_fin, wa)


def _mlp_bwd(x1, dx2, g, wa):
    T = x1.shape[0]
    tm = TOKEN_TILE

    def body(x_ref, dx2_ref, g_ref, wa_hbm, dx1_ref, h_ref, act_ref, da_ref, dg_ref, w1_ref, w2_ref):
        @pl.when(pl.program_id(0) == 0)
        def _():
            _load_late_weight(wa_hbm, w1_ref, "w_mlp_in")
            _load_late_weight(wa_hbm, w2_ref, "w_mlp_out")
            dg_ref[...] = jnp.zeros_like(dg_ref)

        xn, r = _rms(x_ref[...])
        gv = g_ref[...]
        h = (xn * gv).astype(BF16)
        h_ref[...] = h
        dx2 = dx2_ref[...]
        dx2b = dx2.astype(BF16)
        dh = jnp.zeros((tm, D_MODEL), F32)
        for s in range(FF_SHARDS):
            ff = slice(FF_SHARD * s, FF_SHARD * (s + 1))
            rl = jnp.maximum(_dot(h, w1_ref[s]), 0.0)
            act_ref[:, ff] = (rl * rl).astype(BF16)
            da = (_dot_nt(dx2b, w2_ref[ff, :]) * (2.0 * rl)).astype(BF16)
            da_ref[:, ff] = da
            dh += _dot_nt(da, w1_ref[s])
        dg_ref[...] += jnp.sum(dh * xn, axis=0, keepdims=True)
        dx1_ref[...] = dx2 + _rms_bwd(xn, r, dh * gv)

    return _pc(
        body, name="mlp_bwd", grid=(T // tm,),
        in_specs=[_row_spec(tm, 1024), _row_spec(tm, 1024), _const_spec((1, 1024)), _hbm_spec()],
        out_specs=[_row_spec(tm, 1024), _row_spec(tm, 1024), _row_spec(tm, D_FF), _row_spec(tm, D_FF), _const_spec((1, 1024))],
        out_shape=[jax.ShapeDtypeStruct((T, 1024), F32), jax.ShapeDtypeStruct((T, 1024), BF16),
                   jax.ShapeDtypeStruct((T, D_FF), BF16), jax.ShapeDtypeStruct((T, D_FF), BF16),
                   jax.ShapeDtypeStruct((1, 1024), F32)],
        scratch_shapes=[pltpu.VMEM((FF_SHARDS, D_MODEL, FF_SHARD), BF16), pltpu.VMEM((D_FF, D_MODEL), BF16)],
        compiler_params=_cparams(("arbitrary",)),
    )(x1, dx2, g, wa)


WGRAD_OUT_ELEMS = 2 * 1024 * 1024
WGRAD_TILE_BYTES = 4 * 1024 * 1024


def _wgrad(a, b, name, col_shards=None, row_shards_into=None):
    T, K = a.shape
    N = b.shape[1]
    nb = N // col_shards if col_shards else min(N, 1024, max(128, WGRAD_OUT_ELEMS // K))
    tt = min(T, WGRAD_TOKENS)
    while tt * max(K * a.dtype.itemsize, nb * b.dtype.itemsize) > WGRAD_TILE_BYTES:
        tt //= 2
    assert N % nb == 0 and T % tt == 0
    in_specs = [pl.BlockSpec((tt, K), lambda n, t: (t, 0)), pl.BlockSpec((tt, nb), lambda n, t: (t, n))]
    args, aliases = [a, b], {}
    if col_shards:
        out_spec = pl.BlockSpec((None, None, K, nb), lambda n, t: (n, 0, 0, 0))
        out_shape = jax.ShapeDtypeStruct((col_shards, 2, K, nb), F32)
    elif row_shards_into is not None:
        shards, _, rows, cols = row_shards_into.shape
        assert shards * rows == K and cols == N
        out_spec = pl.BlockSpec((shards, None, rows, nb), lambda n, t: (0, 1, 0, n))
        out_shape = jax.ShapeDtypeStruct(row_shards_into.shape, F32)
        in_specs.append(_hbm_spec())
        args.append(row_shards_into)
        aliases = {2: 0}
    else:
        out_spec = pl.BlockSpec((K, nb), lambda n, t: (0, n))
        out_shape = jax.ShapeDtypeStruct((K, N), F32)

    def body(a_ref, b_ref, *rest):
        o_ref = rest[-1]

        @pl.when(pl.program_id(1) == 0)
        def _():
            o_ref[...] = jnp.zeros_like(o_ref)

        o_ref[...] += _dot_tn(a_ref[...].astype(BF16), b_ref[...].astype(BF16)).reshape(o_ref.shape)

    return _pc(
        body, name=name, grid=(N // nb, T // tt), in_specs=in_specs, out_specs=out_spec, out_shape=out_shape,
        input_output_aliases=aliases, compiler_params=_cparams(("parallel", "arbitrary")),
    )(*args)


def _s5_block_weights(bb_re, bb_im, c_re, c_im):
    eye = jnp.eye(8, dtype=F32)
    bre = bb_re.reshape(16, S5_BLOCKS, 8, 64)
    bim = bb_im.reshape(16, S5_BLOCKS, 8, 64)
    wb_re = jnp.einsum('kjgp,gh->jhkgp', bre, eye).reshape(S5_BLOCKS, _BI, _BW)
    wb_im = jnp.einsum('kjgp,gh->jhkgp', bim, eye).reshape(S5_BLOCKS, _BI, _BW)
    wb4 = jnp.concatenate([wb_re, wb_im], axis=2).astype(BF16)
    cre = c_re.reshape(S5_BLOCKS, 8, 16, 64)
    cim = c_im.reshape(S5_BLOCKS, 8, 16, 64)
    wc_re = jnp.einsum('jgkp,gh->jgphk', cre, eye).reshape(S5_BLOCKS, _BW, _BI)
    wc_im = jnp.einsum('jgkp,gh->jgphk', -cim, eye).reshape(S5_BLOCKS, _BW, _BI)
    wc4 = jnp.concatenate([wc_re, wc_im], axis=1).astype(BF16)
    return wb4, wc4


def _s5_block_grads(dwb4, dwc4):
    eye = jnp.eye(8, dtype=F32)
    dwb = dwb4.reshape(S5_BLOCKS, 8, 16, 2, 8, 64)
    dbb = jnp.einsum('jhkrgp,gh->rkjgp', dwb, eye).reshape(2, 16, S5_STATES)
    dwc = dwc4.reshape(S5_BLOCKS, 2, 8, 64, 8, 16)
    dc = jnp.einsum('jrgphk,gh->rjgkp', dwc, eye).reshape(2, 32, 16, 64)
    return dbb[0], dbb[1], dc[0], -dc[1]


def _row(v, width=None):
    v = v.reshape(1, -1)
    if width is not None and v.shape[1] < width:
        v = jnp.concatenate([v, jnp.zeros((1, width - v.shape[1]), v.dtype)], axis=1)
    return v


def _local_step(x, target, p, comm=None):
    g_mix, g_mlp, g_fin = _row(p["norm_mix_g"]), _row(p["norm_mlp_g"]), _row(p["norm_final_g"])
    conv_b = _row(p["conv_b"])
    dt_bias = _row(p["dt_bias"], DT_PAD)
    alog = _row(p["a_log"], DT_PAD)
    dvec = _row(jnp.repeat(p["d_ssd"], SSD_HEADDIM))
    gssd = _row(p["ssd_norm_g"])
    s5d = _row(p["s5_d"])
    glu_b = _row(p["s5_glu_b"])
    head_sel = (jnp.arange(SSD_INNER)[:, None] // SSD_HEADDIM == jnp.arange(DT_PAD)[None, :]).astype(F32)

    a_re = p["s5_a_re"].reshape(1, S5_STATES)
    a_im = p["s5_a_im"].reshape(1, S5_STATES)
    log_dt = jnp.repeat(p["s5_log_dt"], 64).reshape(1, S5_STATES)
    b_re = p["s5_b_re"].reshape(S5_STATES, 16).T
    b_im = p["s5_b_im"].reshape(S5_STATES, 16).T
    ab_re, ab_im, bb_re, bb_im = _s5_disc(a_re, a_im, log_dt, b_re, b_im)
    wb4, wc4 = _s5_block_weights(bb_re, bb_im, p["s5_c_re"], p["s5_c_im"])
    ab = jnp.concatenate([ab_re, ab_im, jnp.zeros((6, S5_STATES), F32)], axis=0)

    wp = p["w_in_perm"]

    first_args = (x, g_mix, wp, wb4, wc4, ab, s5d)
    if comm is None:
        z, xbc_raw, u5, gates, dt_raw, h, y5, s5_states = _inproj_s5_fwd(*first_args)
    else:
        first_out, late = _inproj_s5_fwd(*first_args, rider=_Gather(comm["late_srcs"], comm["late_ks"]))
        z, xbc_raw, u5, gates, dt_raw, h, y5, s5_states = first_out
        p = {**p, **comm["late_unpack"](late)}
    xbc_act, dt = _conv_fwd(xbc_raw, dt_raw, p["conv_w"], conv_b, dt_bias)
    ys, ssd_states = _ssd_fwd(xbc_act, dt, alog)
    wa, glu_w = p["late_weights"], p["s5_glu_w"]
    x1 = _merge_fwd(ys, xbc_act, z, y5, gates, x, dvec, gssd, glu_w, glu_b, wa)
    dx2, loss_lanes, d_gfin = _mlp_fwd_loss(x1, target, g_mlp, g_fin, wa)

    dx1, h2, act, da1, d_gmlp = _mlp_bwd(x1, dx2, g_mlp, wa)
    g_mlp4 = _wgrad(h2, da1, "wgrad_mlp_in", col_shards=FF_SHARDS)
    g_mlp4 = _wgrad(act, dx2, "wgrad_mlp_out", row_shards_into=g_mlp4)
    d_w_mlp_in, d_w_mlp_out = g_mlp4[:, 0], g_mlp4[:, 1].reshape(D_FF, D_MODEL)
    merge_args = (ys, xbc_act, z, y5, gates, dx1, dvec, gssd, glu_w, glu_b, wa, head_sel)
    if comm is None:
        merge_out = _merge_bwd(*merge_args)
    else:
        g_mlp = g_mlp4.reshape(N_CHIPS, 2 * FF_SHARD, D_MODEL)
        merge_out, (sib_mlp,) = _merge_bwd(*merge_args, rider=_Pair([g_mlp]))
        pf_mlp, pb_mlp = _pair_sum(comm["place"], g_mlp, sib_mlp, "pair_sum_mlp")
    (dys, dxs_m, dz, dy5, dgates, mg, ya, yb, dpa, dpb, gel, dpre, d_dssd, d_gssd, d_glu_b) = merge_out
    d_w_out = _wgrad(mg, dx1, "wgrad_out")
    d_w_branch = jnp.concatenate([_wgrad(ya, dpa, "wgrad_branch_a"), _wgrad(yb, dpb, "wgrad_branch_b")], axis=0)
    d_glu_w = _wgrad(gel, dpre, "wgrad_glu")
    s5_args = (u5, dy5, wb4, wc4, ab, s5d, s5_states)
    if comm is None:
        du5, dwb4, dwc4, dab, d_s5d = _s5_bwd(*s5_args)
        mlp_total = None
    else:
        (du5, dwb4, dwc4, dab, d_s5d), (got_mlp,) = _s5_bwd(*s5_args, rider=_Chip([pb_mlp]))
        mlp_total = _chip_sum(comm["place"], pf_mlp, got_mlp, "chip_sum_mlp")
    dbb_re, dbb_im, d_c_re, d_c_im = _s5_block_grads(dwb4, dwc4)
    d_a_re, d_a_im, d_log_dt, d_b_re, d_b_im = _s5_disc_bwd(
        a_re, a_im, log_dt, b_re, b_im, dab[0:1], dab[1:2], dbb_re, dbb_im)
    dxs_s, dB, dC, ddt, d_alog = _ssd_bwd(xbc_act, dt, alog, ssd_states, dys)
    dxbc_raw, ddt_raw, d_conv_w, d_conv_b, d_dt_bias = _conv_bwd(
        xbc_raw, dt_raw, dxs_m, dxs_s, dB, dC, ddt, p["conv_w"], conv_b, dt_bias)
    d_w_in = dict(z=_wgrad(h, dz, "wgrad_in_z"), xbc=_wgrad(h, dxbc_raw, "wgrad_in_xbc"),
                  dt=_wgrad(h, ddt_raw, "wgrad_in_dt")[:, :16], u5=_wgrad(h, du5, "wgrad_in_u5"),
                  gates=_wgrad(h, dgates, "wgrad_in_gates"))
    w_in_pieces = [(c0, d_w_in[n]) for n, c0, _ in W_IN_PIECES]
    inproj_args = (x, dx1, dz, dxbc_raw, du5, dgates, ddt_raw, g_mix, wp)
    if comm is None:
        dx, d_gmix = _inproj_bwd(*inproj_args)
        late_totals = None
    else:
        g_b, g_in = _late_buffers(d_w_out, d_w_branch, d_glu_w, d_conv_w[:CONV_K], w_in_pieces)
        sib_b, sib_in = _exchange(_Pair([g_b, g_in]), "pair_exchange")
        pf_b, pb_b = _pair_sum(comm["place"], g_b, sib_b, "pair_sum_b")
        pf_in, pb_in = _pair_sum(comm["place"], g_in, sib_in, "pair_sum_in")
        (dx, d_gmix), (got_b, got_in) = _inproj_bwd(*inproj_args, rider=_Chip([pb_b, pb_in]))
        late_totals = (_chip_sum(comm["place"], pf_b, got_b, "chip_sum_b"),
                       _chip_sum(comm["place"], pf_in, got_in, "chip_sum_in"))

    grads = dict(
        norm_mix_g=d_gmix.reshape(-1), w_in_pieces=w_in_pieces, late_totals=late_totals,
        conv_w=d_conv_w[:CONV_K], conv_b=d_conv_b.reshape(-1),
        dt_bias=d_dt_bias[0, :16], a_log=d_alog[0, :16], d_ssd=d_dssd[0, :16], ssd_norm_g=d_gssd.reshape(-1),
        s5_a_re=d_a_re.reshape(32, 64), s5_a_im=d_a_im.reshape(32, 64), s5_log_dt=d_log_dt[0, :32],
        s5_b_re=d_b_re.T.reshape(32, 64, 16), s5_b_im=d_b_im.T.reshape(32, 64, 16), s5_c_re=d_c_re, s5_c_im=d_c_im,
        s5_d=d_s5d.reshape(-1), s5_glu_w=d_glu_w, s5_glu_b=d_glu_b.reshape(-1), w_branch=d_w_branch, w_out=d_w_out,
        norm_mlp_g=d_gmlp.reshape(-1), w_mlp_in=d_w_mlp_in, w_mlp_out=d_w_mlp_out, norm_final_g=d_gfin.reshape(-1),
        mlp_total=mlp_total)
    return jnp.sum(loss_lanes), dx, grads


MESH = pl.DeviceIdType.MESH
N_CHIPS = 4


def _place():
    x, y, c = lax.axis_index("x"), lax.axis_index("y"), lax.axis_index("c")
    chips = [(1 - x, y), (x, 1 - y), (1 - x, 1 - y)]
    return x, y, c, chips


def _remote(src, dst, send_sems, recv_sems, k, to):
    return pltpu.make_async_remote_copy(src_ref=src, dst_ref=dst, send_sem=send_sems.at[k], recv_sem=recv_sems.at[k],
                                        device_id=to, device_id_type=MESH)


def _row_chunks(rows, k, align):
    step = rows // k
    assert rows % k == 0 and step % align == 0, (rows, k, align)
    return [(i * step, step) for i in range(k)]


ICI_CHUNKS = 4
D2D_CHUNKS = 24


class _Gather:
    def __init__(self, srcs, ks):
        self.inputs = list(srcs)
        self.out_shapes = [jax.ShapeDtypeStruct((N_CHIPS,) + a.shape, a.dtype) for a in srcs]
        self.halves = [a.shape[0] // 2 for a in srcs]
        self.pieces = [_row_chunks(h, k, 32 // a.dtype.itemsize) for a, h, k in zip(srcs, self.halves, ks)]
        self.n_ici = 3 * sum(ks)
        self.n_sems = 2 * self.n_ici + len(srcs)

    def _plan(self, src_refs, out_refs, send_sems, recv_sems):
        x, y, c, chips = _place()
        own = 2 * x + y
        sib = (x, y, 1 - c)
        first, fwd_plan, k = [], [], 0
        for a, (src_ref, out_ref) in enumerate(zip(src_refs, out_refs)):
            h = self.halves[a]
            for r0, nr in self.pieces[a]:
                for cx, cy in chips:
                    first.append(_remote(src_ref.at[pl.ds(c * h + r0, nr), :], out_ref.at[own, pl.ds(c * h + r0, nr), :],
                                         send_sems, recv_sems, k, (cx, cy, c)))
                    fwd_plan.append((out_ref, 2 * cx + cy, h, r0, nr, k, (cx, cy, c)))
                    k += 1
        for a, (src_ref, out_ref) in enumerate(zip(src_refs, out_refs)):
            first.append(_remote(src_ref, out_ref.at[own], send_sems, recv_sems, 2 * self.n_ici + a, sib))
        return first, fwd_plan, c, sib

    def issue(self, src_refs, out_refs, send_sems, recv_sems):
        for cp in self._plan(src_refs, out_refs, send_sems, recv_sems)[0]:
            cp.start()

    def complete(self, src_refs, out_refs, send_sems, recv_sems):
        first, fwd_plan, c, sib = self._plan(src_refs, out_refs, send_sems, recv_sems)
        passed = []
        for out_ref, s, h, r0, nr, k, frm in fwd_plan:
            got = out_ref.at[s, pl.ds(c * h + r0, nr), :]
            _remote(got, got, send_sems, recv_sems, k, frm).wait_recv()
            fw = _remote(got, got, send_sems, recv_sems, self.n_ici + k, sib)
            fw.start()
            passed.append(fw)
        for out_ref, s, h, r0, nr, k, frm in fwd_plan:
            got = out_ref.at[s, pl.ds((1 - c) * h + r0, nr), :]
            _remote(got, got, send_sems, recv_sems, self.n_ici + k, sib).wait_recv()
        own_copies = first[self.n_ici:]
        for cp in own_copies:
            cp.wait_recv()
        for cp in first + passed:
            cp.wait_send()


def _exchange(rider, name):
    ri, ro = len(rider.inputs), len(rider.out_shapes)

    def body(*refs):
        rider.issue(refs[:ri], refs[ri:ri + ro], *refs[ri + ro:])
        rider.complete(refs[:ri], refs[ri:ri + ro], *refs[ri + ro:])

    return _pc(
        body, name=name, in_specs=[_hbm_spec()] * ri, out_specs=[_hbm_spec()] * ro, out_shape=list(rider.out_shapes),
        scratch_shapes=[pltpu.SemaphoreType.DMA((rider.n_sems,))] * 2,
    )(*rider.inputs)


def _call(body, rider=None, **kw):
    if rider is None:
        return _pc(body, **kw)
    single = not isinstance(kw["out_shape"], (list, tuple))
    out_specs = [kw["out_specs"]] if single else list(kw["out_specs"])
    out_shape = [kw["out_shape"]] if single else list(kw["out_shape"])
    scratch = list(kw.get("scratch_shapes", ()))
    n_in, n_out, n_scr = len(kw["in_specs"]), len(out_specs), len(scratch)
    ri, ro = len(rider.inputs), len(rider.out_shapes)
    steps = kw["grid"][0]

    def wrapped(*refs):
        o0 = n_in + ri
        s0 = o0 + n_out + ro
        r_in, r_out, sems = refs[n_in:o0], refs[o0 + n_out:s0], refs[s0 + n_scr:]

        @pl.when(pl.program_id(0) == 0)
        def _():
            rider.issue(r_in, r_out, *sems)

        body(*refs[:n_in], *refs[o0:o0 + n_out], *refs[s0:s0 + n_scr])

        @pl.when(pl.program_id(0) == steps - 1)
        def _():
            rider.complete(r_in, r_out, *sems)

    f = _pc(wrapped, name=kw["name"], grid=kw["grid"], in_specs=list(kw["in_specs"]) + [_hbm_spec()] * ri,
            out_specs=out_specs + [_hbm_spec()] * ro, out_shape=out_shape + list(rider.out_shapes),
            scratch_shapes=scratch + [pltpu.SemaphoreType.DMA((rider.n_sems,))] * 2, compiler_params=kw["compiler_params"])

    def run(*args):
        res = f(*args, *rider.inputs)
        return (res[0] if single else res[:n_out]), res[n_out:]

    return run


def _d2d_pieces(rows):
    k = next(k for k in range(24, 0, -1) if rows % k == 0 and (rows // k) % 8 == 0)
    return _row_chunks(rows, k, 8)


class _Pair:
    def __init__(self, gs, small=None):
        self.n = len(gs)
        self.halves = [g.shape[1] // 2 for g in gs]
        self.inputs = list(gs) + ([small] if small is not None else [])
        self.out_shapes = [jax.ShapeDtypeStruct((N_CHIPS, h, g.shape[2]), F32) for g, h in zip(gs, self.halves)]
        if small is not None:
            self.out_shapes.append(jax.ShapeDtypeStruct(small.shape, F32))
        self.n_sems = len(self.inputs)

    def issue(self, in_refs, out_refs, send_sems, recv_sems):
        x, y, c, _ = _place()
        sib = (x, y, 1 - c)
        for a in range(self.n):
            h = self.halves[a]
            for s in range(N_CHIPS):
                for r0, nr in _d2d_pieces(h):
                    _remote(in_refs[a].at[s, pl.ds((1 - c) * h + r0, nr), :], out_refs[a].at[s, pl.ds(r0, nr), :],
                            send_sems, recv_sems, a, sib).start()
        for a in range(self.n, len(self.inputs)):
            _remote(in_refs[a], out_refs[a], send_sems, recv_sems, a, sib).start()

    def complete(self, in_refs, out_refs, send_sems, recv_sems):
        x, y, c, _ = _place()
        for a in range(len(self.inputs)):
            _remote(out_refs[a], out_refs[a], send_sems, recv_sems, a, (x, y, 1 - c)).wait()


SUM_BLOCKS = 4


def _pair_sum(place, g, sib, name):
    n, R, C = g.shape
    H = R // 2
    rb = H // SUM_BLOCKS
    assert H % SUM_BLOCKS == 0 and rb % 16 == 0

    def body(place_ref, a_ref, b_ref, pf_ref, pb_ref):
        p = a_ref[...] + b_ref[...]
        pf_ref[...] = p
        pb_ref[...] = p.astype(BF16)

    blk = pl.BlockSpec((1, rb, C), lambda s, i, pr: (s, i, 0))
    mine = pl.BlockSpec((1, rb, C), lambda s, i, pr: (s, pr[1] * SUM_BLOCKS + i, 0))
    return _pc(
        body, name=name, out_shape=[jax.ShapeDtypeStruct((n, H, C), F32), jax.ShapeDtypeStruct((n, H, C), BF16)],
        grid_spec=pltpu.PrefetchScalarGridSpec(num_scalar_prefetch=1, grid=(n, SUM_BLOCKS), in_specs=[mine, blk],
                                               out_specs=[blk, blk]),
        compiler_params=_cparams(("arbitrary", "arbitrary")),
    )(place, g, sib)


class _Chip:
    def __init__(self, pbs, psmall=None):
        self.n = len(pbs)
        self.rows = [pb.shape[1] for pb in pbs]
        self.inputs = list(pbs) + ([psmall] if psmall is not None else [])
        self.out_shapes = [jax.ShapeDtypeStruct((3,) + pb.shape[1:], BF16) for pb in pbs]
        if psmall is not None:
            self.out_shapes.append(jax.ShapeDtypeStruct((N_CHIPS,) + psmall.shape, F32))
        self.n_sems = 3 * len(self.inputs)

    def issue(self, in_refs, out_refs, send_sems, recv_sems):
        x, y, c, chips = _place()
        own = 2 * x + y
        for j, (cx, cy) in enumerate(chips):
            for a in range(self.n):
                for r0, nr in _row_chunks(self.rows[a], ICI_CHUNKS, 16):
                    _remote(in_refs[a].at[2 * cx + cy, pl.ds(r0, nr), :], out_refs[a].at[j, pl.ds(r0, nr), :],
                            send_sems, recv_sems, 3 * a + j, (cx, cy, c)).start()
            for a in range(self.n, len(self.inputs)):
                _remote(in_refs[a], out_refs[a].at[own], send_sems, recv_sems, 3 * a + j, (cx, cy, c)).start()

    def complete(self, in_refs, out_refs, send_sems, recv_sems):
        x, y, c, chips = _place()
        own = 2 * x + y
        for j, (cx, cy) in enumerate(chips):
            for a in range(self.n):
                _remote(in_refs[a].at[own], out_refs[a].at[j], send_sems, recv_sems, 3 * a + j, (cx, cy, c)).wait()
            for a in range(self.n, len(self.inputs)):
                _remote(in_refs[a], out_refs[a].at[2 * cx + cy], send_sems, recv_sems, 3 * a + j, (cx, cy, c)).wait()


def _chip_sum(place, pf, got, name):
    _, H, C = pf.shape
    rb = H // SUM_BLOCKS

    def body(place_ref, o_ref, g_ref, tot_ref):
        tot_ref[...] = ((o_ref[0] + g_ref[0].astype(F32)) + g_ref[1].astype(F32)) + g_ref[2].astype(F32)

    ins = [pl.BlockSpec((1, rb, C), lambda i, pr: (pr[0], i, 0)), pl.BlockSpec((3, rb, C), lambda i, pr: (0, i, 0))]
    out = pl.BlockSpec((rb, C), lambda i, pr: (pr[1] * SUM_BLOCKS + i, 0))
    return _pc(
        body, name=name, out_shape=jax.ShapeDtypeStruct((2 * H, C), F32),
        grid_spec=pltpu.PrefetchScalarGridSpec(num_scalar_prefetch=1, grid=(SUM_BLOCKS,), in_specs=ins, out_specs=out),
        compiler_params=_cparams(("arbitrary",)),
    )(place, pf, got)


def _half_exchange(fulls):
    n = len(fulls)

    def body(*refs):
        in_refs, out_refs = refs[:n], refs[n:2 * n]
        send_sems, recv_sems = refs[2 * n:]
        x, y, c, _ = _place()
        sib = (x, y, 1 - c)
        for a in range(n):
            h = fulls[a].shape[0] // 2
            for r0, nr in _d2d_pieces(h):
                rows = pl.ds(c * h + r0, nr)
                _remote(in_refs[a].at[rows, :], out_refs[a].at[rows, :], send_sems, recv_sems, a, sib).start()
        for a in range(n):
            h = fulls[a].shape[0] // 2
            _remote(in_refs[a].at[pl.ds(c * h, h), :], out_refs[a].at[pl.ds((1 - c) * h, h), :], send_sems, recv_sems, a,
                    sib).wait()

    return _pc(
        body, name="half_exchange", in_specs=[_hbm_spec()] * n, out_specs=[_hbm_spec()] * n,
        out_shape=[jax.ShapeDtypeStruct(f.shape, F32) for f in fulls],
        input_output_aliases={a: a for a in range(n)},
        scratch_shapes=[pltpu.SemaphoreType.DMA((n,)), pltpu.SemaphoreType.DMA((n,))],
    )(*fulls)


def _small_allreduce(pack):
    R, C = pack.shape

    def body(p_ref, o_ref, sib_ref, pair_ref, slots_ref, send_sems, recv_sems):
        x, y, c, chips = _place()
        own = 2 * x + y
        cp = _remote(p_ref, sib_ref, send_sems, recv_sems, 0, (x, y, 1 - c))
        cp.start()
        cp.wait()
        pair_ref[...] = p_ref[...] + sib_ref[...]
        slots_ref[own] = pair_ref[...]
        out = [_remote(pair_ref, slots_ref.at[own], send_sems, recv_sems, 1 + j, (cx, cy, c)) for j, (cx, cy) in enumerate(chips)]
        for cp in out:
            cp.start()
        for j, (cx, cy) in enumerate(chips):
            _remote(pair_ref, slots_ref.at[2 * cx + cy], send_sems, recv_sems, 1 + j, (cx, cy, c)).wait()
        o_ref[...] = ((slots_ref[0] + slots_ref[1]) + slots_ref[2]) + slots_ref[3]

    vmem = pl.BlockSpec(memory_space=pltpu.VMEM)
    return _pc(
        body, name="small_allreduce", in_specs=[vmem], out_specs=vmem, out_shape=jax.ShapeDtypeStruct((R, C), F32),
        scratch_shapes=[pltpu.VMEM((R, C), F32), pltpu.VMEM((R, C), F32), pltpu.VMEM((N_CHIPS, R, C), F32),
                        pltpu.SemaphoreType.DMA((4,)), pltpu.SemaphoreType.DMA((4,))],
    )(pack)


def _adamw(w, g, m, v, name, g_row0=0, with_grad=False, col_block=None):
    R, C = w.shape
    rb = 256 if R % 256 == 0 else (128 if R % 128 == 0 else R)
    if col_block:
        rb = R
    assert g_row0 % rb == 0

    def body(w_ref, g_ref, m_ref, v_ref, d_ref, nm_ref, nv_ref, *g_out):
        gv = g_ref[...]
        m2 = ADAM_B1 * m_ref[...] + (1.0 - ADAM_B1) * gv
        v2 = ADAM_B2 * v_ref[...] + (1.0 - ADAM_B2) * (gv * gv)
        m_hat = m2 * (1.0 / (1.0 - ADAM_B1 ** ADAM_STEP))
        v_hat = v2 * (1.0 / (1.0 - ADAM_B2 ** ADAM_STEP))
        d_ref[...] = -ADAM_LR * (m_hat / (jnp.sqrt(v_hat) + ADAM_EPS) + ADAM_WD * w_ref[...])
        nm_ref[...] = m2
        nv_ref[...] = v2
        if with_grad:
            g_out[0][...] = gv

    if col_block:
        spec = g_spec = pl.BlockSpec((R, col_block), lambda i: (0, i))
        steps = C // col_block
    else:
        spec = pl.BlockSpec((rb, C), lambda i: (i, 0))
        g_spec = pl.BlockSpec((rb, C), lambda i: (g_row0 // rb + i, 0))
        steps = R // rb
    n_out = 4 if with_grad else 3
    return _pc(
        body, name=name, grid=(steps,), in_specs=[spec, g_spec, spec, spec], out_specs=[spec] * n_out,
        out_shape=[jax.ShapeDtypeStruct((R, C), F32)] * n_out, compiler_params=_cparams(("parallel",)),
    )(w, g, m, v)


PACK_COLS = 1024
ROWS_A = (("w_mlp_in", 0, 1024), ("w_mlp_out", 1024, 1024), ("w_out", 2048, 256), ("w_branch", 2304, 384))
ROWS_A_TOTAL = 2688
ROWS_B = (("w_out", 0, 256), ("w_branch", 256, 384))
ROW_B_GLU, ROW_B_CONV, ROWS_B_TOTAL = 640, 704, 768
W_IN_SHARD = 1412
CONV_PAD_ROWS = 16
SMALL = (("norm_mix_g", (1024,)), ("conv_b", (2048,)), ("dt_bias", (16,)), ("a_log", (16,)), ("d_ssd", (16,)),
         ("ssd_norm_g", (1024,)), ("s5_a_re", (32, 64)), ("s5_a_im", (32, 64)), ("s5_log_dt", (32,)),
         ("s5_b_re", (32, 64, 16)), ("s5_b_im", (32, 64, 16)), ("s5_c_re", (32, 16, 64)), ("s5_c_im", (32, 16, 64)),
         ("s5_d", (512,)), ("s5_glu_b", (512,)), ("norm_mlp_g", (1024,)), ("norm_final_g", (1024,)))
SMALL_ROWS = 144
SMALL_COUNT = sum(math.prod(shp) for _, shp in SMALL)
GLU_ROWS = S5_WIDTH * S5_WIDTH // PACK_COLS
CONV_ROWS = CONV_K * CONV_DIM // PACK_COLS
W_IN_PIECES = (("z", 0, 1024), ("xbc", 1024, 2048), ("dt", OFF_DT, 16), ("u5", OFF_U, 512), ("gates", 3600, 2048))


def _pack_small(parts):
    flat = jnp.concatenate([a.astype(F32).reshape(-1) for a in parts])
    return jnp.concatenate([flat, jnp.zeros((SMALL_ROWS * PACK_COLS - flat.shape[0],), F32)]).reshape(SMALL_ROWS, PACK_COLS)


def _unpack_small(pack):
    flat, out, r = pack.reshape(-1), {}, 0
    for name, shp in SMALL:
        n = math.prod(shp)
        out[name] = flat[r:r + n].reshape(shp)
        r += n
    return out


def _late_buffers(d_w_out, d_w_branch, d_glu_w, d_conv_w, w_in_pieces):
    conv4 = d_conv_w.reshape(CONV_K, N_CHIPS, 512).transpose(1, 0, 2).reshape(N_CHIPS, CONV_ROWS // N_CHIPS, PACK_COLS)
    g_b = jnp.concatenate(
        [d_w_out.reshape(N_CHIPS, -1, PACK_COLS), d_w_branch.reshape(N_CHIPS, -1, PACK_COLS),
         d_glu_w.reshape(N_CHIPS, GLU_ROWS // N_CHIPS, PACK_COLS),
         jnp.pad(conv4, ((0, 0), (0, ROWS_B_TOTAL - ROW_B_CONV - CONV_ROWS // N_CHIPS), (0, 0)))], axis=1)
    g_in = jnp.stack([jnp.concatenate(_column_range(w_in_pieces, W_IN_SHARD * s, W_IN_SHARD * (s + 1)), axis=1)
                      for s in range(N_CHIPS)])
    return g_b, g_in


def _column_range(pieces, lo, hi):
    out = []
    for c0, a in pieces:
        a0, a1 = max(lo, c0), min(hi, c0 + a.shape[-1])
        if a0 < a1:
            out.append(a[..., a0 - c0:a1 - c0])
    return out


def kernel(x, norm_mix_g, w_in, conv_w, conv_b, dt_bias, a_log, d_ssd, ssd_norm_g, s5_a_re, s5_a_im, s5_log_dt, s5_b_re, s5_b_im, s5_c_re, s5_c_im, s5_d, s5_glu_w, s5_glu_b, w_branch, w_out, norm_mlp_g, w_mlp_in, w_mlp_out, norm_final_g, loss_target, m_norm_mix_g, m_w_in, m_conv_w, m_conv_b, m_dt_bias, m_a_log, m_d_ssd, m_ssd_norm_g, m_s5_a_re, m_s5_a_im, m_s5_log_dt, m_s5_b_re, m_s5_b_im, m_s5_c_re, m_s5_c_im, m_s5_d, m_s5_glu_w, m_s5_glu_b, m_w_branch, m_w_out, m_norm_mlp_g, m_w_mlp_in, m_w_mlp_out, m_norm_final_g, v_norm_mix_g, v_w_in, v_conv_w, v_conv_b, v_dt_bias, v_a_log, v_d_ssd, v_ssd_norm_g, v_s5_a_re, v_s5_a_im, v_s5_log_dt, v_s5_b_re, v_s5_b_im, v_s5_c_re, v_s5_c_im, v_s5_d, v_s5_glu_w, v_s5_glu_b, v_w_branch, v_w_out, v_norm_mlp_g, v_w_mlp_in, v_w_mlp_out, v_norm_final_g):
    names = ("norm_mix_g", "w_in", "conv_w", "conv_b", "dt_bias", "a_log", "d_ssd", "ssd_norm_g", "s5_a_re", "s5_a_im",
             "s5_log_dt", "s5_b_re", "s5_b_im", "s5_c_re", "s5_c_im", "s5_d", "s5_glu_w", "s5_glu_b", "w_branch", "w_out",
             "norm_mlp_g", "w_mlp_in", "w_mlp_out", "norm_final_g")
    w = dict(zip(names, (norm_mix_g, w_in, conv_w, conv_b, dt_bias, a_log, d_ssd, ssd_norm_g, s5_a_re, s5_a_im, s5_log_dt,
                         s5_b_re, s5_b_im, s5_c_re, s5_c_im, s5_d, s5_glu_w, s5_glu_b, w_branch, w_out, norm_mlp_g,
                         w_mlp_in, w_mlp_out, norm_final_g)))
    m = dict(zip(names, (m_norm_mix_g, m_w_in, m_conv_w, m_conv_b, m_dt_bias, m_a_log, m_d_ssd, m_ssd_norm_g, m_s5_a_re,
                         m_s5_a_im, m_s5_log_dt, m_s5_b_re, m_s5_b_im, m_s5_c_re, m_s5_c_im, m_s5_d, m_s5_glu_w,
                         m_s5_glu_b, m_w_branch, m_w_out, m_norm_mlp_g, m_w_mlp_in, m_w_mlp_out, m_norm_final_g)))
    v = dict(zip(names, (v_norm_mix_g, v_w_in, v_conv_w, v_conv_b, v_dt_bias, v_a_log, v_d_ssd, v_ssd_norm_g, v_s5_a_re,
                         v_s5_a_im, v_s5_log_dt, v_s5_b_re, v_s5_b_im, v_s5_c_re, v_s5_c_im, v_s5_d, v_s5_glu_w,
                         v_s5_glu_b, v_w_branch, v_w_out, v_norm_mlp_g, v_w_mlp_in, v_w_mlp_out, v_norm_final_g)))

    cx, cy, cc = lax.axis_index("x"), lax.axis_index("y"), lax.axis_index("c")
    own = 2 * cx + cy
    place = jnp.stack([own, cc]).astype(jnp.int32)

    src_conv = jnp.concatenate([conv_w, jnp.zeros((CONV_PAD_ROWS - CONV_K, 512), F32)], axis=0)
    all_in, all_conv = _exchange(_Gather([w_in.astype(BF16), src_conv], [ICI_CHUNKS, 1]), "gather_first")
    p = {n: w[n] for n, _ in SMALL}
    p["conv_w"] = jnp.concatenate([all_conv[s, :CONV_K] for s in range(N_CHIPS)], axis=1)
    shards = [(W_IN_SHARD * s, all_in[s]) for s in range(N_CHIPS)]
    p["w_in_perm"] = jnp.concatenate(
        _column_range(shards, 0, OFF_DT) + _column_range(shards, OFF_U, D_IN_PROJ) + _column_range(shards, OFF_DT, OFF_U)
        + [jnp.zeros((D_MODEL, DT_PAD - 16), BF16)], axis=1)

    def late_unpack(gathered):
        all_a, all_glu = gathered
        return {"late_weights": all_a, "s5_glu_w": all_glu.reshape(S5_WIDTH, S5_WIDTH)}

    comm = dict(place=place, late_ks=[ICI_CHUNKS, 1], late_unpack=late_unpack,
                late_srcs=[jnp.concatenate([w[n].astype(BF16) for n, _, _ in ROWS_A], axis=0), s5_glu_w.astype(BF16)])
    loss_part, grad_x, g = _local_step(x[0], loss_target[0], p, comm)

    red_mlp, red_b, red_in = _half_exchange([g["mlp_total"], *g["late_totals"]])
    small_tot = _small_allreduce(_pack_small([g[n] for n, _ in SMALL] + [loss_part.reshape(1)]))
    loss = small_tot.reshape(-1)[SMALL_COUNT]

    grads = _unpack_small(small_tot)
    delta, new_m, new_v = {}, {}, {}
    for n, r0, _ in ROWS_A[:2]:
        delta[n], new_m[n], new_v[n], grads[n] = _adamw(w[n], red_mlp, m[n], v[n], "adamw_" + n, g_row0=r0, with_grad=True)
    for n, r0, _ in ROWS_B:
        delta[n], new_m[n], new_v[n], grads[n] = _adamw(w[n], red_b, m[n], v[n], "adamw_" + n, g_row0=r0, with_grad=True)
    d_t, m_t, v_t, g_t = _adamw(w_in.T, red_in.T, m_w_in.T, v_w_in.T, "adamw_w_in", with_grad=True, col_block=128)
    delta["w_in"], new_m["w_in"], new_v["w_in"], grads["w_in"] = d_t.T, m_t.T, v_t.T, g_t.T
    grads["s5_glu_w"] = red_b[ROW_B_GLU:ROW_B_GLU + GLU_ROWS // N_CHIPS].reshape(S5_WIDTH // N_CHIPS, S5_WIDTH)
    grads["conv_w"] = red_b[ROW_B_CONV:ROW_B_CONV + CONV_ROWS // N_CHIPS].reshape(CONV_K, CONV_DIM // N_CHIPS)
    for n in ("s5_glu_w", "conv_w"):
        delta[n], new_m[n], new_v[n] = _adamw(w[n], grads[n], m[n], v[n], "adamw_" + n)
    ds, ms, vs = _adamw(_pack_small([w[n] for n, _ in SMALL]), small_tot, _pack_small([m[n] for n, _ in SMALL]),
                        _pack_small([v[n] for n, _ in SMALL]), "adamw_small")
    delta.update(_unpack_small(ds))
    new_m.update(_unpack_small(ms))
    new_v.update(_unpack_small(vs))

    return (loss, grad_x[None], *[grads[n] for n in names], *[delta[n] for n in names],
            *[new_m[n] for n in names], *[new_v[n] for n in names])
```

```python
import functools
import math

import jax
import jax.numpy as jnp
from jax import lax
from jax.experimental import pallas as pl
from jax.experimental.pallas import tpu as pltpu

F32 = jnp.float32
BF16 = jnp.bfloat16

D_MODEL = 1024
SSD_INNER = 1024
SSD_HEADS = 16
SSD_HEADDIM = 64
SSD_GROUPS = 4
SSD_HPG = 4
SSD_STATE = 128
SSD_CHUNK = 128
CONV_K = 4
CONV_DIM = 2048
S5_WIDTH = 512
S5_STATES = 2048
S5_BLOCKS = 4
S5_CHUNK = 128
D_FF = 4096
FF_SHARDS = 4
FF_SHARD = D_FF // FF_SHARDS
EPS = 1e-6
P_Z, P_XBC, P_U5, P_G, P_DT, P_END = 0, 1024, 3072, 3584, 5632, 5760
DT_PAD = 128
OFF_DT, OFF_U = 3072, 3088
D_IN_PROJ = 5648

ADAM_LR, ADAM_B1, ADAM_B2, ADAM_EPS, ADAM_WD, ADAM_STEP = 0.001, 0.9, 0.999, 1e-08, 0.01, 10

TOKEN_TILE = 256
VMEM_LIMIT = 56 * 1024 * 1024
HALO = 8
INPROJ_PIECE = 256
CONV_COLS = 256
CONV_ROWS_BLK = 64
WGRAD_TOKENS = 2048


def _pc(body, **kw):
    return pl.pallas_call(body, **kw)


def _cparams(sem=None):
    return pltpu.CompilerParams(dimension_semantics=sem, vmem_limit_bytes=VMEM_LIMIT)


def _dot(a, b):
    return jnp.dot(a, b, preferred_element_type=F32)


def _dot_nt(a, b):
    return lax.dot_general(a, b, (((1,), (1,)), ((), ())), preferred_element_type=F32)


def _dot_tn(a, b):
    return lax.dot_general(a, b, (((0,), (0,)), ((), ())), preferred_element_type=F32)


def _dot_hi(a, b, dims=(((1,), (0,)), ((), ()))):
    return lax.dot_general(a, b, dims, preferred_element_type=F32, precision=lax.Precision.HIGHEST)


def _split_bf16(x, terms):
    out = []
    for _ in range(terms - 1):
        t = x.astype(BF16)
        out.append(t)
        x = x - t.astype(F32)
    out.append(x.astype(BF16))
    return out


def _dot_split(x, onehots, terms, dims=(((1,), (0,)), ((), ()))):
    acc = None
    for t in _split_bf16(x, terms):
        p = lax.dot_general(t, onehots, dims, preferred_element_type=F32)
        acc = p if acc is None else acc + p
    return acc


def _dot_split_rhs(onehots, x, terms, dims=(((1,), (0,)), ((), ()))):
    acc = None
    for t in _split_bf16(x, terms):
        p = lax.dot_general(onehots, t, dims, preferred_element_type=F32)
        acc = p if acc is None else acc + p
    return acc


def _sigmoid(x):
    return 0.5 * jnp.tanh(0.5 * x) + 0.5


def _softplus(x):
    return jnp.maximum(x, 0.0) + jnp.log(1.0 + jnp.exp(-jnp.abs(x)))


_GELU_C = math.sqrt(2.0 / math.pi)


def _gelu(x):
    return 0.5 * x * (1.0 + jnp.tanh(_GELU_C * (x + 0.044715 * x * x * x)))


def _gelu_grad(x):
    t = jnp.tanh(_GELU_C * (x + 0.044715 * x * x * x))
    return 0.5 * (1.0 + t) + 0.5 * x * (1.0 - t * t) * _GELU_C * (1.0 + 3.0 * 0.044715 * x * x)


def _rms(x):
    r = lax.rsqrt(jnp.mean(x * x, axis=-1, keepdims=True) + EPS)
    return x * r, r


def _rms_bwd(xn, r, dxn):
    return r * (dxn - xn * jnp.mean(dxn * xn, axis=-1, keepdims=True))


def _row_spec(tm, width, col=0):
    return pl.BlockSpec((tm, width), lambda i: (i, col))


def _const_spec(shape):
    nd = len(shape)
    return pl.BlockSpec(shape, lambda i: (0,) * nd)


def _hbm_spec():
    return pl.BlockSpec(memory_space=pl.ANY)


def _load_late_weight(wa_hbm, dst_ref, name):
    r0, nr = next((r0, nr) for n, r0, nr in ROWS_A if n == name)
    for s in range(N_CHIPS):
        dst = dst_ref.at[s] if len(dst_ref.shape) == 3 else dst_ref.at[pl.ds(nr * s, nr), :]
        pltpu.sync_copy(wa_hbm.at[s, pl.ds(r0, nr), :], dst)


def _inproj_bwd(x, dx1, dz, dxbc, du5, dgt, ddt, g, wp, rider=None):
    T = x.shape[0]
    tm = TOKEN_TILE

    def body(x_ref, dx1_ref, dz_ref, dxbc_ref, du5_ref, dgt_ref, ddt_ref, g_ref, w_hbm, dx_ref, dg_ref, w_ref):
        @pl.when(pl.program_id(0) == 0)
        def _():
            pltpu.sync_copy(w_hbm, w_ref)
            dg_ref[...] = jnp.zeros_like(dg_ref)

        xn, r = _rms(x_ref[...])
        gv = g_ref[...]
        dh = _dot_nt(dz_ref[...].astype(BF16), w_ref[:, P_Z:P_XBC])
        dh += _dot_nt(dxbc_ref[...].astype(BF16), w_ref[:, P_XBC:P_U5])
        dh += _dot_nt(du5_ref[...].astype(BF16), w_ref[:, P_U5:P_G])
        dh += _dot_nt(dgt_ref[...].astype(BF16), w_ref[:, P_G:P_DT])
        dh += _dot_nt(ddt_ref[...].astype(BF16), w_ref[:, P_DT:P_END])
        dg_ref[...] += jnp.sum(dh * xn, axis=0, keepdims=True)
        dx_ref[...] = dx1_ref[...] + _rms_bwd(xn, r, dh * gv)

    return _call(
        body, rider, name="inproj_bwd", grid=(T // tm,),
        in_specs=[_row_spec(tm, 1024), _row_spec(tm, 1024), _row_spec(tm, 1024), _row_spec(tm, 2048),
                  _row_spec(tm, 512), _row_spec(tm, 2048), _row_spec(tm, DT_PAD), _const_spec((1, 1024)), _hbm_spec()],
        out_specs=[_row_spec(tm, 1024), _const_spec((1, 1024))],
        out_shape=[jax.ShapeDtypeStruct((T, 1024), F32), jax.ShapeDtypeStruct((1, 1024), F32)],
        scratch_shapes=[pltpu.VMEM((D_MODEL, P_END), BF16)],
        compiler_params=_cparams(("arbitrary",)),
    )(x, dx1, dz, dxbc, du5, dgt, ddt, g, wp)


def _conv_fwd(xbc_raw, dt_raw, conv_w, conv_b, dt_bias):
    T = xbc_raw.shape[0]
    tm = TOKEN_TILE

    def body(u_ref, dtr_ref, w_ref, b_ref, db_ref, act_ref, dt_ref, ext_ref):
        @pl.when(pl.program_id(0) == 0)
        def _():
            ext_ref[0:HALO, :] = jnp.zeros((HALO, CONV_DIM), F32)

        ext_ref[HALO:, :] = u_ref[...]
        for c0 in range(0, CONV_DIM, CONV_COLS):
            cols = slice(c0, c0 + CONV_COLS)
            taps = [w_ref[k:k + 1, cols] for k in range(CONV_K)]
            bias = b_ref[:, cols]
            for r0 in range(0, tm, CONV_ROWS_BLK):
                y = bias + taps[0] * ext_ref[pl.ds(HALO - (CONV_K - 1) + r0, CONV_ROWS_BLK), cols]
                for k in range(1, CONV_K):
                    y += taps[k] * ext_ref[pl.ds(HALO - (CONV_K - 1) + k + r0, CONV_ROWS_BLK), cols]
                act_ref[r0:r0 + CONV_ROWS_BLK, cols] = y * _sigmoid(y)
        ext_ref[0:HALO, :] = u_ref[tm - HALO:tm, :]
        dt_ref[...] = _softplus(dtr_ref[...] + db_ref[...])

    return _pc(
        body, name="conv_fwd", grid=(T // tm,),
        in_specs=[_row_spec(tm, CONV_DIM), _row_spec(tm, DT_PAD), _const_spec((CONV_K, CONV_DIM)),
                  _const_spec((1, CONV_DIM)), _const_spec((1, DT_PAD))],
        out_specs=[_row_spec(tm, CONV_DIM), _row_spec(tm, DT_PAD)],
        out_shape=[jax.ShapeDtypeStruct((T, CONV_DIM), F32), jax.ShapeDtypeStruct((T, DT_PAD), F32)],
        scratch_shapes=[pltpu.VMEM((tm + HALO, CONV_DIM), F32)],
        compiler_params=_cparams(("arbitrary",)),
    )(xbc_raw, dt_raw, conv_w, conv_b, dt_bias)


def _conv_bwd(xbc_raw, dt_raw, dys, dvec, dxs_b, dB, dC, ddt, conv_w, conv_b, dt_bias):
    T = xbc_raw.shape[0]
    tm = TOKEN_TILE
    n = T // tm
    hb = tm // HALO

    def rev(width):
        return pl.BlockSpec((tm, width), lambda i: (n - 1 - i, 0))

    def body(u_ref, up_ref, dtr_ref, dys_ref, dv_ref, dxb_ref, dB_ref, dC_ref, ddt_ref, w_ref, b_ref, db_ref,
             du_ref, ddtr_ref, dw_ref, dcb_ref, ddb_ref, ext_ref, dye_ref):
        i = pl.program_id(0)

        @pl.when(i == 0)
        def _():
            dye_ref[tm:, :] = jnp.zeros((HALO, CONV_DIM), F32)
            dw_ref[...] = jnp.zeros_like(dw_ref)
            dcb_ref[...] = jnp.zeros_like(dcb_ref)
            ddb_ref[...] = jnp.zeros_like(ddb_ref)

        first = (i == n - 1).astype(F32)
        ext_ref[0:HALO, :] = up_ref[...] * (1.0 - first)
        ext_ref[HALO:, :] = u_ref[...]
        for c0 in range(0, CONV_DIM, CONV_COLS):
            cols = slice(c0, c0 + CONV_COLS)
            taps = [w_ref[k:k + 1, cols] for k in range(CONV_K)]
            bias = b_ref[:, cols]
            acc_b = jnp.zeros((HALO, CONV_COLS), F32)
            acc_w = [jnp.zeros((HALO, CONV_COLS), F32) for _ in range(CONV_K)]
            for r0 in range(0, tm, CONV_ROWS_BLK):
                rows = slice(r0, r0 + CONV_ROWS_BLK)
                us = [ext_ref[pl.ds(HALO - (CONV_K - 1) + k + r0, CONV_ROWS_BLK), cols] for k in range(CONV_K)]
                y = bias + taps[0] * us[0]
                for k in range(1, CONV_K):
                    y += taps[k] * us[k]
                s = _sigmoid(y)
                if c0 < SSD_INNER:
                    dact = dys_ref[rows, cols] * dv_ref[:, cols] + dxb_ref[rows, cols]
                elif c0 < SSD_INNER + 512:
                    dact = dB_ref[rows, c0 - SSD_INNER:c0 - SSD_INNER + CONV_COLS]
                else:
                    dact = dC_ref[rows, c0 - SSD_INNER - 512:c0 - SSD_INNER - 512 + CONV_COLS]
                dy = dact * (s * (1.0 + y * (1.0 - s)))
                dye_ref[rows, cols] = dy
                acc_b += jnp.sum(dy.reshape(CONV_ROWS_BLK // HALO, HALO, CONV_COLS), axis=0)
                for k in range(CONV_K):
                    acc_w[k] += jnp.sum((dy * us[k]).reshape(CONV_ROWS_BLK // HALO, HALO, CONV_COLS), axis=0)
            dcb_ref[:, cols] += jnp.sum(acc_b, axis=0, keepdims=True)
            for k in range(CONV_K):
                dw_ref[k:k + 1, cols] += jnp.sum(acc_w[k], axis=0, keepdims=True)
        for c0 in range(0, CONV_DIM, CONV_COLS):
            cols = slice(c0, c0 + CONV_COLS)
            taps = [w_ref[k:k + 1, cols] for k in range(CONV_K)]
            for r0 in range(0, tm, CONV_ROWS_BLK):
                du = taps[0] * dye_ref[pl.ds(CONV_K - 1 + r0, CONV_ROWS_BLK), cols]
                for k in range(1, CONV_K):
                    du += taps[k] * dye_ref[pl.ds(CONV_K - 1 - k + r0, CONV_ROWS_BLK), cols]
                du_ref[r0:r0 + CONV_ROWS_BLK, cols] = du.astype(BF16)
        dye_ref[tm:, :] = dye_ref[0:HALO, :]
        sg = _sigmoid(dtr_ref[...] + db_ref[...])
        ddtr = ddt_ref[...] * sg
        ddtr_ref[...] = ddtr.astype(BF16)
        ddb_ref[...] += jnp.sum(ddtr, axis=0, keepdims=True)

    prev_spec = pl.BlockSpec((HALO, CONV_DIM), lambda i: (jnp.maximum((n - 1 - i) * hb - 1, 0), 0))
    return _pc(
        body, name="conv_bwd", grid=(n,),
        in_specs=[rev(CONV_DIM), prev_spec, rev(DT_PAD), rev(1024), _const_spec((1, SSD_INNER)), rev(1024), rev(512), rev(512),
                  rev(DT_PAD), _const_spec((CONV_K, CONV_DIM)), _const_spec((1, CONV_DIM)), _const_spec((1, DT_PAD))],
        out_specs=[rev(CONV_DIM), rev(DT_PAD), _const_spec((HALO, CONV_DIM)), _const_spec((1, CONV_DIM)),
                   _const_spec((1, DT_PAD))],
        out_shape=[jax.ShapeDtypeStruct((T, CONV_DIM), BF16), jax.ShapeDtypeStruct((T, DT_PAD), BF16),
                   jax.ShapeDtypeStruct((HALO, CONV_DIM), F32), jax.ShapeDtypeStruct((1, CONV_DIM), F32),
                   jax.ShapeDtypeStruct((1, DT_PAD), F32)],
        scratch_shapes=[pltpu.VMEM((tm + HALO, CONV_DIM), F32), pltpu.VMEM((tm + HALO, CONV_DIM), F32)],
        compiler_params=_cparams(("arbitrary",)),
    )(xbc_raw, xbc_raw, dt_raw, dys, dvec, dxs_b, dB, dC, ddt, conv_w, conv_b, dt_bias)


GROUP_LANES = SSD_HPG * SSD_HEADDIM


def _ssd_expanders():
    head = jnp.arange(DT_PAD)[:, None]
    to_wide = (jnp.arange(SSD_INNER)[None, :] // SSD_HEADDIM == head).astype(BF16)
    return to_wide, to_wide.T


def _ssd_prep(dt_ref, alog_ref, wide_ref):
    q = SSD_CHUNK
    a = -jnp.exp(alog_ref[...])
    dtv = dt_ref[...]
    la = dtv * a
    row = lax.broadcasted_iota(jnp.int32, (q, q), 0)
    col = lax.broadcasted_iota(jnp.int32, (q, q), 1)
    tri = (col <= row).astype(BF16)
    cum = _dot_split_rhs(tri, la, 3)
    cum_t = _dot_split(la, tri, 3, (((0,), (1,)), ((), ())))
    dtw = _dot_split(dtv, wide_ref[...], 2)
    cumw = _dot_split(cum, wide_ref[...], 3)
    return a, dtv, row, col, tri, cum_t, dtw, cumw, cum


def _decay(cum, cum_t, h, keep):
    return jnp.where(keep, jnp.exp(jnp.minimum(cum[:, h:h + 1] - cum_t[h:h + 1, :], 0.0)), 0.0)


def _decay_t(cum, cum_t, h, keep_t):
    return jnp.where(keep_t, jnp.exp(jnp.minimum(cum_t[h:h + 1, :] - cum[:, h:h + 1], 0.0)), 0.0)


def _ssd_fwd(xbc_act, dt, alog):
    T = xbc_act.shape[0]
    q = SSD_CHUNK
    nc = T // q
    to_wide, _ = _ssd_expanders()

    def body(xbc_ref, dt_ref, alog_ref, wide_ref, y_ref, sp_ref, st_ref, xd_ref, xde_ref):
        @pl.when(pl.program_id(0) == 0)
        def _():
            st_ref[...] = jnp.zeros_like(st_ref)

        a, dtv, row, col, tri, cum_t, dtw, cumw, segcol = _ssd_prep(dt_ref, alog_ref, wide_ref)
        clw = cumw[q - 1:q, :]
        ecw = jnp.exp(cumw)
        xd = xbc_ref[:, 0:SSD_INNER] * dtw
        xd_ref[...] = xd.astype(BF16)
        xde_ref[...] = (xd * jnp.exp(clw - cumw)).astype(BF16)
        cdw = jnp.exp(clw)
        keep = col <= row
        sp_ref[0] = st_ref[...]
        for g in range(SSD_GROUPS):
            gl = slice(GROUP_LANES * g, GROUP_LANES * (g + 1))
            bb = xbc_ref[:, 1024 + 128 * g:1152 + 128 * g].astype(BF16)
            cb = xbc_ref[:, 1536 + 128 * g:1664 + 128 * g].astype(BF16)
            gm = _dot_nt(cb, bb)
            stp = st_ref[g]
            yoff = _dot(cb, stp.astype(BF16)) * ecw[:, gl]
            for r in range(SSD_HPG):
                h = SSD_HPG * g + r
                m = (gm * _decay(segcol, cum_t, h, keep)).astype(BF16)
                y_ref[:, 64 * h:64 * h + 64] = _dot(m, xd_ref[:, 64 * h:64 * h + 64]) + yoff[:, 64 * r:64 * r + 64]
            st_ref[g] = stp * cdw[:, gl] + _dot_tn(bb, xde_ref[:, gl])

    return _pc(
        body, name="ssd_fwd", grid=(nc,),
        in_specs=[_row_spec(q, CONV_DIM), _row_spec(q, DT_PAD), _const_spec((1, DT_PAD)),
                  _const_spec(to_wide.shape)],
        out_specs=[_row_spec(q, SSD_INNER),
                   pl.BlockSpec((1, SSD_GROUPS, SSD_STATE, GROUP_LANES), lambda i: (i, 0, 0, 0))],
        out_shape=[jax.ShapeDtypeStruct((T, SSD_INNER), F32),
                   jax.ShapeDtypeStruct((nc, SSD_GROUPS, SSD_STATE, GROUP_LANES), F32)],
        scratch_shapes=[pltpu.VMEM((SSD_GROUPS, SSD_STATE, GROUP_LANES), F32), pltpu.VMEM((q, SSD_INNER), BF16),
                        pltpu.VMEM((q, SSD_INNER), BF16)],
        compiler_params=_cparams(("arbitrary",)),
    )(xbc_act, dt, alog, to_wide)


def _ssd_bwd(xbc_act, dt, alog, sprev, dy):
    T = xbc_act.shape[0]
    q = SSD_CHUNK
    nc = T // q
    to_wide, to_heads = _ssd_expanders()

    def rev(width):
        return pl.BlockSpec((q, width), lambda i: (nc - 1 - i, 0))

    def body(xbc_ref, dt_ref, alog_ref, sp_ref, dy_ref, wide_ref, heads_ref,
             dxs_ref, dB_ref, dC_ref, ddt_ref, dalog_ref, ds_ref, xd_ref, dxd_ref):
        i = pl.program_id(0)

        @pl.when(i == 0)
        def _():
            ds_ref[...] = jnp.zeros_like(ds_ref)
            dalog_ref[...] = jnp.zeros_like(dalog_ref)

        a, dtv, row, col, tri, cum_t, dtw, cumw, segcol = _ssd_prep(dt_ref, alog_ref, wide_ref)
        clw = cumw[q - 1:q, :]
        ecw = jnp.exp(cumw)
        dew = jnp.exp(clw - cumw)
        cdw = jnp.exp(clw)
        xs = xbc_ref[:, 0:SSD_INNER]
        xd = xs * dtw
        xd_ref[...] = xd.astype(BF16)
        dyv = dy_ref[...]
        dye = (dyv * ecw).astype(BF16)
        xde = (xd * dew).astype(BF16)
        keep = col <= row
        keep_t = col >= row
        rows_k = lax.broadcasted_iota(jnp.int32, (SSD_HPG * q, DT_PAD), 0) // q
        lanes_k = lax.broadcasted_iota(jnp.int32, (SSD_HPG * q, DT_PAD), 1)
        dcw_parts = []
        dcum = jnp.zeros((q, DT_PAD), F32)
        for g in range(SSD_GROUPS):
            gl = slice(GROUP_LANES * g, GROUP_LANES * (g + 1))
            bb = xbc_ref[:, 1024 + 128 * g:1152 + 128 * g].astype(BF16)
            cb = xbc_ref[:, 1536 + 128 * g:1664 + 128 * g].astype(BF16)
            gm = _dot_nt(cb, bb)
            gmt = _dot_nt(bb, cb)
            stp = sp_ref[0, g]
            dst = ds_ref[g]
            stpb = stp.astype(BF16)
            dstb = dst.astype(BF16)
            yoff = _dot(cb, stpb) * ecw[:, gl]
            dcg = _dot_nt(dye[:, gl], stpb)
            ds_ref[g] = dst * cdw[:, gl] + _dot_tn(cb, dye[:, gl])
            dlast = jnp.sum(dst * stp, axis=0, keepdims=True) * cdw[:, gl]
            dbg = _dot_nt(xde[:, gl], dstb)
            w = _dot(bb, dstb) * dew[:, gl]
            wx = w * xd[:, gl]
            dlast = dlast + jnp.sum(wx, axis=0, keepdims=True)
            dcw_parts.append(dyv[:, gl] * yoff - wx
                             + jnp.where(lax.broadcasted_iota(jnp.int32, (q, 1), 0) == q - 1, dlast, 0.0))
            dgm = jnp.zeros((q, q), F32)
            diag = []
            for r in range(SSD_HPG):
                h = SSD_HPG * g + r
                hl = slice(64 * h, 64 * h + 64)
                dyb = dy_ref[:, hl].astype(BF16)
                xdh = xd_ref[:, hl]
                dm = _dot_nt(dyb, xdh)
                dmt = _dot_nt(xdh, dyb)
                dec = _decay(segcol, cum_t, h, keep)
                mt = gmt * _decay_t(segcol, cum_t, h, keep_t)
                dgm += dm * dec
                diag.append(dm * (gm * dec) - dmt * mt)
                dxd_ref[:, hl] = _dot(mt.astype(BF16), dyb) + w[:, 64 * r:64 * r + 64]
            onehots = (lanes_k == SSD_HPG * g + rows_k).astype(BF16)
            dcum += _dot_split(jnp.concatenate(diag, axis=1), onehots, 2)
            dgb = dgm.astype(BF16)
            dC_ref[:, 128 * g:128 * g + 128] = dcg + _dot(dgb, bb)
            dB_ref[:, 128 * g:128 * g + 128] = dbg + _dot_tn(dgb, cb)
        dxd = dxd_ref[...]
        dxs_ref[...] = dxd * dtw
        dcum += _dot_split(jnp.concatenate(dcw_parts, axis=1), heads_ref[...], 2)
        dla = _dot_split_rhs(tri, dcum, 3, (((0,), (0,)), ((), ())))
        ddt_ref[...] = _dot_split(xs * dxd, heads_ref[...], 2) + dla * a
        dalog_ref[...] += jnp.sum(dla * dtv, axis=0, keepdims=True)

        @pl.when(i == nc - 1)
        def _():
            dalog_ref[...] = dalog_ref[...] * a

    st_spec = pl.BlockSpec((1, SSD_GROUPS, SSD_STATE, GROUP_LANES), lambda i: (nc - 1 - i, 0, 0, 0))
    return _pc(
        body, name="ssd_bwd", grid=(nc,),
        in_specs=[rev(CONV_DIM), rev(DT_PAD), _const_spec((1, DT_PAD)), st_spec, rev(SSD_INNER),
                  _const_spec(to_wide.shape), _const_spec(to_heads.shape)],
        out_specs=[rev(SSD_INNER), rev(512), rev(512), rev(DT_PAD), _const_spec((1, DT_PAD))],
        out_shape=[jax.ShapeDtypeStruct((T, SSD_INNER), F32), jax.ShapeDtypeStruct((T, 512), F32),
                   jax.ShapeDtypeStruct((T, 512), F32), jax.ShapeDtypeStruct((T, DT_PAD), F32),
                   jax.ShapeDtypeStruct((1, DT_PAD), F32)],
        scratch_shapes=[pltpu.VMEM((SSD_GROUPS, SSD_STATE, GROUP_LANES), F32), pltpu.VMEM((q, SSD_INNER), BF16),
                        pltpu.VMEM((q, SSD_INNER), F32)],
        compiler_params=_cparams(("arbitrary",)),
    )(xbc_act, dt, alog, sprev, dy, to_wide, to_heads)


def _s5_disc_vals(a_re, a_im, log_dt, b_re, b_im):
    dt = jnp.exp(log_dt)
    mag = jnp.exp(a_re * dt)
    ab_re = mag * jnp.cos(a_im * dt)
    ab_im = mag * jnp.sin(a_im * dt)
    den = a_re * a_re + a_im * a_im
    nr = ab_re - 1.0
    ni = ab_im
    coef_re = (nr * a_re + ni * a_im) / den
    coef_im = (ni * a_re - nr * a_im) / den
    bb_re = coef_re * b_re - coef_im * b_im
    bb_im = coef_re * b_im + coef_im * b_re
    return ab_re, ab_im, bb_re, bb_im


def _s5_disc(a_re, a_im, log_dt, b_re, b_im):
    def body(ar, ai, ld, br, bi, o1, o2, o3, o4):
        o1[...], o2[...], o3[...], o4[...] = _s5_disc_vals(ar[...], ai[...], ld[...], br[...], bi[...])

    return _pc(
        body, name="s5_disc",
        out_shape=[jax.ShapeDtypeStruct((1, S5_STATES), F32), jax.ShapeDtypeStruct((1, S5_STATES), F32),
                   jax.ShapeDtypeStruct((16, S5_STATES), F32), jax.ShapeDtypeStruct((16, S5_STATES), F32)],
    )(a_re, a_im, log_dt, b_re, b_im)


def _s5_disc_bwd(a_re, a_im, log_dt, b_re, b_im, d_ab_re, d_ab_im, d_bb_re, d_bb_im):
    def body(ar, ai, ld, br, bi, g1, g2, g3, g4, o1, o2, o3, o4, o5):
        _, vjp = jax.vjp(_s5_disc_vals, ar[...], ai[...], ld[...], br[...], bi[...])
        d1, d2, d3, d4, d5 = vjp((g1[...], g2[...], g3[...], g4[...]))
        o1[...] = d1
        o2[...] = d2
        st = lax.broadcasted_iota(jnp.int32, (S5_STATES, DT_PAD), 0)
        grp = lax.broadcasted_iota(jnp.int32, (S5_STATES, DT_PAD), 1)
        sel = (st // 64 == grp).astype(F32)
        o3[...] = _dot_hi(d3, sel)
        o4[...] = d4
        o5[...] = d5

    return _pc(
        body, name="s5_disc_bwd",
        out_shape=[jax.ShapeDtypeStruct((1, S5_STATES), F32), jax.ShapeDtypeStruct((1, S5_STATES), F32),
                   jax.ShapeDtypeStruct((1, DT_PAD), F32),
                   jax.ShapeDtypeStruct((16, S5_STATES), F32), jax.ShapeDtypeStruct((16, S5_STATES), F32)],
    )(a_re, a_im, log_dt, b_re, b_im, d_ab_re, d_ab_im, d_bb_re, d_bb_im)


def _cmul_add(xr, xi, pr, pi, yr, yi):
    return xr + pr * yr - pi * yi, xi + pr * yi + pi * yr


def _powers(ar, ai, n):
    out = [(ar, ai)]
    for _ in range(n - 1):
        pr, pi = out[-1]
        out.append((pr * pr - pi * pi, 2.0 * pr * pi))
    return out


_BW = S5_STATES // S5_BLOCKS
_BI = S5_WIDTH // S5_BLOCKS
SUB = 8
S5_ROWS = S5_CHUNK // SUB


S5_TAB_ROWS = 8 * SUB


def _scan8(br, bi, tab_ref, reverse):
    for level, k in enumerate((1, 2, 4)):
        r0 = 2 * SUB * (level + 1)
        shift = SUB - k if reverse else k
        br, bi = _cmul_add(br, bi, tab_ref[r0:r0 + SUB, :], tab_ref[r0 + SUB:r0 + 2 * SUB, :],
                           pltpu.roll(br, shift, 0), pltpu.roll(bi, shift, 0))
    return br, bi


def _s5_tables(ab_ref, tab_ref, reverse):
    rowin = lax.broadcasted_iota(jnp.int32, (SUB, 1), 0)
    ar = ab_ref[0:1, :]
    ai = -ab_ref[1:2, :] if reverse else ab_ref[1:2, :]
    zero = jnp.zeros((SUB, S5_STATES), F32)
    for level, (pr, pi) in enumerate(_powers(ar, ai, 3)):
        k = 2 ** level
        keep = (rowin < SUB - k) if reverse else (rowin >= k)
        r0 = 2 * SUB * (level + 1)
        tab_ref[r0:r0 + SUB, :] = jnp.where(keep, pr, 0.0) + zero
        tab_ref[r0 + SUB:r0 + 2 * SUB, :] = jnp.where(keep, pi, 0.0) + zero
    hit = rowin == (SUB - 1 if reverse else 0)
    pr, pi = _scan8(jnp.where(hit, ar, 0.0) + zero, jnp.where(hit, ai, 0.0) + zero, tab_ref, reverse)
    tab_ref[0:SUB, :] = pr
    tab_ref[SUB:2 * SUB, :] = pi


def _s5_fwd_chunk(u_ref, y_ref, r0, wb_ref, wc_ref, d_ref, carry_ref, tab_ref, sr_ref, si_ref, between):
    q = S5_CHUNK
    rows = slice(r0, r0 + q)
    for j in range(S5_BLOCKS):
        bu = _dot(u_ref[rows, _BI * j:_BI * (j + 1)].astype(BF16), wb_ref[j])
        sr_ref[:, :, _BW * j:_BW * (j + 1)] = bu[:, :_BW].reshape(S5_ROWS, SUB, _BW)
        si_ref[:, :, _BW * j:_BW * (j + 1)] = bu[:, _BW:].reshape(S5_ROWS, SUB, _BW)
    tr, ti = tab_ref[0:SUB, :], tab_ref[SUB:2 * SUB, :]
    cr, ci = carry_ref[0:1, :], carry_ref[1:2, :]
    for k in range(S5_ROWS):
        sr, si = _scan8(sr_ref[k], si_ref[k], tab_ref, False)
        sr, si = _cmul_add(sr, si, tr, ti, cr, ci)
        sr_ref[k] = sr
        si_ref[k] = si
        cr, ci = sr[SUB - 1:SUB, :], si[SUB - 1:SUB, :]
        between()
    carry_ref[0:1, :] = cr
    carry_ref[1:2, :] = ci
    for j in range(S5_BLOCKS):
        sl = slice(_BW * j, _BW * (j + 1))
        ul = slice(_BI * j, _BI * (j + 1))
        s = jnp.concatenate([sr_ref[:, :, sl].reshape(q, _BW), si_ref[:, :, sl].reshape(q, _BW)], axis=1).astype(BF16)
        y_ref[rows, ul] = _dot(s, wc_ref[j]) + d_ref[:, ul] * u_ref[rows, ul]


def _inproj_s5_fwd(x, g, wp, wb4, wc4, ab, dvec, rider=None):
    T = x.shape[0]
    tm = TOKEN_TILE
    per_tile = tm // S5_CHUNK
    nc = T // S5_CHUNK

    def body(x_ref, g_ref, w_hbm, wb_ref, wc_ref, ab_ref, d_ref, z_ref, xbc_ref, u5_ref, gt_ref, dt_ref, h_ref, y_ref, sp_ref,
             w_ref, carry_ref, tab_ref, sr_ref, si_ref):
        @pl.when(pl.program_id(0) == 0)
        def _():
            pltpu.sync_copy(w_hbm, w_ref)
            carry_ref[...] = jnp.zeros_like(carry_ref)
            _s5_tables(ab_ref, tab_ref, False)

        xn, _ = _rms(x_ref[...])
        h = (xn * g_ref[...]).astype(BF16)
        h_ref[...] = h
        u5_ref[...] = _dot(h, w_ref[:, P_U5:P_G])
        pieces = [(z_ref, P_Z, c0) for c0 in range(0, P_XBC - P_Z, INPROJ_PIECE)]
        pieces += [(xbc_ref, P_XBC, c0) for c0 in range(0, P_U5 - P_XBC, INPROJ_PIECE)]
        pieces += [(gt_ref, P_G, c0) for c0 in range(0, P_DT - P_G, INPROJ_PIECE)]
        todo = iter(pieces)
        slabs, calls = per_tile * S5_ROWS, [0]

        def between():
            calls[0] += 1
            if (calls[0] * len(pieces)) // slabs > ((calls[0] - 1) * len(pieces)) // slabs:
                o_ref, base, c0 = next(todo)
                o_ref[:, c0:c0 + INPROJ_PIECE] = _dot(h, w_ref[:, base + c0:base + c0 + INPROJ_PIECE])

        dt_ref[...] = _dot(h, w_ref[:, P_DT:P_END])
        for c in range(per_tile):
            sp_ref[c] = carry_ref[...]
            _s5_fwd_chunk(u5_ref, y_ref, S5_CHUNK * c, wb_ref, wc_ref, d_ref, carry_ref, tab_ref, sr_ref, si_ref, between)
        assert next(todo, None) is None

    widths = (1024, 2048, 512, 2048, DT_PAD)
    return _call(
        body, rider, name="inproj_s5_fwd", grid=(T // tm,),
        in_specs=[_row_spec(tm, D_MODEL), _const_spec((1, D_MODEL)), _hbm_spec(),
                  _const_spec((S5_BLOCKS, _BI, 2 * _BW)), _const_spec((S5_BLOCKS, 2 * _BW, _BI)),
                  _const_spec((8, S5_STATES)), _const_spec((1, S5_WIDTH))],
        out_specs=[_row_spec(tm, w) for w in widths] + [_row_spec(tm, D_MODEL), _row_spec(tm, S5_WIDTH),
                                                         pl.BlockSpec((per_tile, 8, S5_STATES), lambda i: (i, 0, 0))],
        out_shape=[jax.ShapeDtypeStruct((T, w), F32) for w in widths] + [
            jax.ShapeDtypeStruct((T, D_MODEL), BF16), jax.ShapeDtypeStruct((T, S5_WIDTH), F32),
            jax.ShapeDtypeStruct((nc, 8, S5_STATES), F32)],
        scratch_shapes=[pltpu.VMEM((D_MODEL, P_END), BF16), pltpu.VMEM((8, S5_STATES), F32),
                        pltpu.VMEM((S5_TAB_ROWS, S5_STATES), F32), pltpu.VMEM((S5_ROWS, SUB, S5_STATES), F32),
                        pltpu.VMEM((S5_ROWS, SUB, S5_STATES), F32)],
        compiler_params=_cparams(("arbitrary",)),
    )(x, g, wp, wb4, wc4, ab, dvec)


def _s5_bwd(u5, dy5, wb4, wc4, ab, dvec, sprev, rider=None):
    T = u5.shape[0]
    q = S5_CHUNK
    nc = T // q

    def rev(width):
        return pl.BlockSpec((q, width), lambda i: (nc - 1 - i, 0))

    def body(u_ref, dy_ref, wb_ref, wc_ref, ab_ref, d_ref, sp_ref, du_ref, dwb_ref, dwc_ref, dab_ref, dd_ref,
             carry_ref, tab_ref, rtab_ref, sr_ref, si_ref, lr_ref, li_ref):
        i = pl.program_id(0)
        rowin = lax.broadcasted_iota(jnp.int32, (SUB, 1), 0)

        @pl.when(i == 0)
        def _():
            carry_ref[...] = jnp.zeros_like(carry_ref)
            dwb_ref[...] = jnp.zeros_like(dwb_ref)
            dwc_ref[...] = jnp.zeros_like(dwc_ref)
            dab_ref[...] = jnp.zeros_like(dab_ref)
            dd_ref[...] = jnp.zeros_like(dd_ref)
            _s5_tables(ab_ref, tab_ref, False)
            _s5_tables(ab_ref, rtab_ref, True)

        for j in range(S5_BLOCKS):
            sl = slice(_BW * j, _BW * (j + 1))
            ul = slice(_BI * j, _BI * (j + 1))
            bu = _dot(u_ref[:, ul].astype(BF16), wb_ref[j])
            sr_ref[:, :, sl] = bu[:, :_BW].reshape(S5_ROWS, SUB, _BW)
            si_ref[:, :, sl] = bu[:, _BW:].reshape(S5_ROWS, SUB, _BW)
            ds = _dot_nt(dy_ref[:, ul].astype(BF16), wc_ref[j])
            lr_ref[:, :, sl] = ds[:, :_BW].reshape(S5_ROWS, SUB, _BW)
            li_ref[:, :, sl] = ds[:, _BW:].reshape(S5_ROWS, SUB, _BW)
        ar, ai = ab_ref[0:1, :], ab_ref[1:2, :]
        tr, ti = tab_ref[0:SUB, :], tab_ref[SUB:2 * SUB, :]
        cr, ci = sp_ref[0, 0:1, :], sp_ref[0, 1:2, :]
        for k in range(S5_ROWS):
            sr, si = _scan8(sr_ref[k], si_ref[k], tab_ref, False)
            sr, si = _cmul_add(sr, si, tr, ti, cr, ci)
            sr_ref[k] = sr
            si_ref[k] = si
            cr, ci = sr[SUB - 1:SUB, :], si[SUB - 1:SUB, :]
        tr, ti = rtab_ref[0:SUB, :], rtab_ref[SUB:2 * SUB, :]
        cr, ci = carry_ref[0:1, :], carry_ref[1:2, :]
        acc_r = jnp.zeros((SUB, S5_STATES), F32)
        acc_i = jnp.zeros((SUB, S5_STATES), F32)
        for k in reversed(range(S5_ROWS)):
            lr, li = _scan8(lr_ref[k], li_ref[k], rtab_ref, True)
            lr, li = _cmul_add(lr, li, tr, ti, cr, ci)
            lr_ref[k] = lr
            li_ref[k] = li
            cr, ci = lr[0:1, :], li[0:1, :]
            if k > 0:
                before_r, before_i = sr_ref[k - 1, SUB - 1:SUB, :], si_ref[k - 1, SUB - 1:SUB, :]
            else:
                before_r, before_i = sp_ref[0, 0:1, :], sp_ref[0, 1:2, :]
            keep = rowin >= 1
            pr = jnp.where(keep, pltpu.roll(sr_ref[k], 1, 0), before_r)
            pi = jnp.where(keep, pltpu.roll(si_ref[k], 1, 0), before_i)
            acc_r += lr * pr + li * pi
            acc_i += li * pr - lr * pi
        carry_ref[0:1, :] = cr
        carry_ref[1:2, :] = ci
        dab_ref[0:1, :] += jnp.sum(acc_r, axis=0, keepdims=True)
        dab_ref[1:2, :] += jnp.sum(acc_i, axis=0, keepdims=True)
        for j in range(S5_BLOCKS):
            sl = slice(_BW * j, _BW * (j + 1))
            ul = slice(_BI * j, _BI * (j + 1))
            u = u_ref[:, ul]
            dy = dy_ref[:, ul]
            dyb = dy.astype(BF16)
            lam = jnp.concatenate([lr_ref[:, :, sl].reshape(q, _BW), li_ref[:, :, sl].reshape(q, _BW)], axis=1).astype(BF16)
            s = jnp.concatenate([sr_ref[:, :, sl].reshape(q, _BW), si_ref[:, :, sl].reshape(q, _BW)], axis=1).astype(BF16)
            du_ref[:, ul] = (_dot_nt(lam, wb_ref[j]) + d_ref[:, ul] * dy).astype(BF16)
            dwb_ref[j] += _dot_tn(u.astype(BF16), lam)
            dwc_ref[j] += _dot_tn(s, dyb)
            dd_ref[:, ul] += jnp.sum(dy * u, axis=0, keepdims=True)

    big = pltpu.VMEM((S5_ROWS, SUB, S5_STATES), F32)
    return _call(
        body, rider, name="s5_bwd", grid=(nc,),
        in_specs=[rev(S5_WIDTH), rev(S5_WIDTH), _const_spec((S5_BLOCKS, _BI, 2 * _BW)), _const_spec((S5_BLOCKS, 2 * _BW, _BI)),
                  _const_spec((8, S5_STATES)), _const_spec((1, S5_WIDTH)),
                  pl.BlockSpec((1, 8, S5_STATES), lambda i: (nc - 1 - i, 0, 0))],
        out_specs=[rev(S5_WIDTH), _const_spec((S5_BLOCKS, _BI, 2 * _BW)), _const_spec((S5_BLOCKS, 2 * _BW, _BI)),
                   _const_spec((8, S5_STATES)), _const_spec((1, S5_WIDTH))],
        out_shape=[jax.ShapeDtypeStruct((T, S5_WIDTH), BF16), jax.ShapeDtypeStruct((S5_BLOCKS, _BI, 2 * _BW), F32),
                   jax.ShapeDtypeStruct((S5_BLOCKS, 2 * _BW, _BI), F32), jax.ShapeDtypeStruct((8, S5_STATES), F32),
                   jax.ShapeDtypeStruct((1, S5_WIDTH), F32)],
        scratch_shapes=[pltpu.VMEM((8, S5_STATES), F32), pltpu.VMEM((S5_TAB_ROWS, S5_STATES), F32),
                        pltpu.VMEM((S5_TAB_ROWS, S5_STATES), F32), big, big, big, big],
        compiler_params=_cparams(("arbitrary",)),
    )(u5, dy5, wb4, wc4, ab, dvec, sprev)


def _merge_vals(ys, xs, z, y5, gates, dvec, gssd, glu_w, glu_b, wbr):
    sz = _sigmoid(z)
    qv = ys + dvec * xs
    pre = qv * (z * sz)
    yn, rs = [], []
    for gi in range(SSD_GROUPS):
        p, r = _rms(pre[:, 256 * gi:256 * (gi + 1)])
        yn.append(p)
        rs.append(r)
    yn = jnp.concatenate(yn, axis=1)
    ya = yn * gssd
    pa = _dot(ya.astype(BF16), wbr[0:SSD_INNER, :])
    gel = _gelu(y5)
    pre_g = _dot(gel.astype(BF16), glu_w)
    s0 = _sigmoid(gates[:, :D_MODEL])
    s1 = _sigmoid(gates[:, D_MODEL:])
    sg = _sigmoid(pre_g + glu_b)
    yb = gel * sg
    pb = _dot(yb.astype(BF16), wbr[SSD_INNER:, :])
    merged = s0 * pa + s1 * pb
    return dict(sz=sz, qv=qv, yn=yn, rs=rs, ya=ya, gel=gel, sg=sg, yb=yb, pa=pa, pb=pb, s0=s0, s1=s1, merged=merged)


def _merge_specs(tm):
    acts = [_row_spec(tm, 1024), _row_spec(tm, 1024, 0), _row_spec(tm, 1024), _row_spec(tm, 512), _row_spec(tm, 2048),
            _row_spec(tm, 1024)]
    params = [_const_spec((1, 1024)), _const_spec((1, 1024)), _const_spec((512, 512)), _const_spec((1, 512)), _hbm_spec()]
    return acts, params


def _merge_fwd(ys, xbc_act, z, y5, gates, x, dvec, gssd, glu_w, glu_b, wa):
    T = x.shape[0]
    tm = TOKEN_TILE
    acts, params = _merge_specs(tm)

    def body(ys_ref, xs_ref, z_ref, y5_ref, gt_ref, x_ref, dv_ref, gs_ref, gw_ref, gb_ref, wa_hbm, x1_ref,
             wbr_ref, wout_ref):
        @pl.when(pl.program_id(0) == 0)
        def _():
            _load_late_weight(wa_hbm, wbr_ref, "w_branch")
            _load_late_weight(wa_hbm, wout_ref, "w_out")

        v = _merge_vals(ys_ref[...], xs_ref[...], z_ref[...], y5_ref[...], gt_ref[...], dv_ref[...], gs_ref[...],
                        gw_ref[...], gb_ref[...], wbr_ref)
        x1_ref[...] = x_ref[...] + _dot(v["merged"].astype(BF16), wout_ref[...])

    return _pc(
        body, name="merge_fwd", grid=(T // tm,),
        in_specs=acts + params, out_specs=_row_spec(tm, 1024),
        out_shape=jax.ShapeDtypeStruct((T, 1024), F32),
        scratch_shapes=[pltpu.VMEM((1536, 1024), BF16), pltpu.VMEM((1024, 1024), BF16)],
        compiler_params=_cparams(("arbitrary",)),
    )(ys, xbc_act, z, y5, gates, x, dvec, gssd, glu_w, glu_b, wa)


def _merge_bwd(ys, xbc_act, z, y5, gates, dx1, dvec, gssd, glu_w, glu_b, wa, head_sel, rider=None):
    T = dx1.shape[0]
    tm = TOKEN_TILE
    acts, params = _merge_specs(tm)

    def body(ys_ref, xs_ref, z_ref, y5_ref, gt_ref, dx1_ref, dv_ref, gs_ref, gw_ref, gb_ref, wa_hbm, hs_ref,
             dys_ref, dz_ref, dy5_ref, dgt_ref, mg_ref, ya_ref, yb_ref, dpa_ref, dpb_ref, gel_ref, dpre_ref,
             ddv_ref, dgs_ref, dgb_ref, wbr_ref, wout_ref, ddacc_ref):
        i = pl.program_id(0)

        @pl.when(i == 0)
        def _():
            _load_late_weight(wa_hbm, wbr_ref, "w_branch")
            _load_late_weight(wa_hbm, wout_ref, "w_out")
            ddacc_ref[...] = jnp.zeros_like(ddacc_ref)
            dgs_ref[...] = jnp.zeros_like(dgs_ref)
            dgb_ref[...] = jnp.zeros_like(dgb_ref)

        ys, xs, z, y5, gates = ys_ref[...], xs_ref[...], z_ref[...], y5_ref[...], gt_ref[...]
        dvv, gsv, gw = dv_ref[...], gs_ref[...], gw_ref[...]
        v = _merge_vals(ys, xs, z, y5, gates, dvv, gsv, gw, gb_ref[...], wbr_ref)
        dmg = _dot_nt(dx1_ref[...].astype(BF16), wout_ref[...])
        s0, s1, pa, pb = v["s0"], v["s1"], v["pa"], v["pb"]
        dgt_ref[:, :D_MODEL] = (dmg * pa * s0 * (1.0 - s0)).astype(BF16)
        dgt_ref[:, D_MODEL:] = (dmg * pb * s1 * (1.0 - s1)).astype(BF16)
        dpa = (dmg * s0).astype(BF16)
        dpb = (dmg * s1).astype(BF16)
        dya = _dot_nt(dpa, wbr_ref[0:SSD_INNER, :])
        dyb = _dot_nt(dpb, wbr_ref[SSD_INNER:, :])
        gel, sg = v["gel"], v["sg"]
        dpre = (dyb * gel * sg * (1.0 - sg))
        dgb_ref[...] += jnp.sum(dpre, axis=0, keepdims=True)
        dpre_b = dpre.astype(BF16)
        dgel = dyb * sg + _dot_nt(dpre_b, gw)
        dy5_ref[...] = dgel * _gelu_grad(y5)
        yn = v["yn"]
        dgs_ref[...] += jnp.sum(dya * yn, axis=0, keepdims=True)
        dyn = dya * gsv
        dpre_a = jnp.concatenate(
            [_rms_bwd(yn[:, 256 * gi:256 * (gi + 1)], v["rs"][gi], dyn[:, 256 * gi:256 * (gi + 1)])
             for gi in range(SSD_GROUPS)], axis=1)
        sz, qv = v["sz"], v["qv"]
        dq = dpre_a * (z * sz)
        dz_ref[...] = (dpre_a * qv * (sz * (1.0 + z * (1.0 - sz)))).astype(BF16)
        dys_ref[...] = dq
        ddacc_ref[...] += jnp.sum(dq * xs, axis=0, keepdims=True)
        mg_ref[...] = v["merged"].astype(BF16)
        ya_ref[...] = v["ya"].astype(BF16)
        yb_ref[...] = v["yb"].astype(BF16)
        dpa_ref[...] = dpa
        dpb_ref[...] = dpb
        gel_ref[...] = gel.astype(BF16)
        dpre_ref[...] = dpre_b

        @pl.when(i == pl.num_programs(0) - 1)
        def _():
            ddv_ref[...] = _dot_hi(ddacc_ref[...], hs_ref[...])

    outs = [(1024, F32), (1024, BF16), (512, F32), (2048, BF16),
            (1024, BF16), (1024, BF16), (512, BF16), (1024, BF16), (1024, BF16), (512, BF16), (512, BF16)]
    return _call(
        body, rider, name="merge_bwd", grid=(T // tm,),
        in_specs=acts + params + [_const_spec((1024, DT_PAD))],
        out_specs=[_row_spec(tm, w) for w, _ in outs] + [_const_spec((1, DT_PAD)), _const_spec((1, 1024)), _const_spec((1, 512))],
        out_shape=[jax.ShapeDtypeStruct((T, w), d) for w, d in outs] + [
            jax.ShapeDtypeStruct((1, DT_PAD), F32), jax.ShapeDtypeStruct((1, 1024), F32), jax.ShapeDtypeStruct((1, 512), F32)],
        scratch_shapes=[pltpu.VMEM((1536, 1024), BF16), pltpu.VMEM((1024, 1024), BF16), pltpu.VMEM((1, 1024), F32)],
        compiler_params=_cparams(("arbitrary",)),
    )(ys, xbc_act, z, y5, gates, dx1, dvec, gssd, glu_w, glu_b, wa, head_sel)


def _mlp_fwd_loss(x1, target, g, g_fin, wa):
    T = x1.shape[0]
    tm = TOKEN_TILE

    def body(x_ref, t_ref, g_ref, gf_ref, wa_hbm, dx_ref, loss_ref, dg_ref, w1_ref, w2_ref):
        @pl.when(pl.program_id(0) == 0)
        def _():
            _load_late_weight(wa_hbm, w1_ref, "w_mlp_in")
            _load_late_weight(wa_hbm, w2_ref, "w_mlp_out")
            loss_ref[...] = jnp.zeros_like(loss_ref)
            dg_ref[...] = jnp.zeros_like(dg_ref)

        xv = x_ref[...]
        xn, _ = _rms(xv)
        h = (xn * g_ref[...]).astype(BF16)
        acc = xv
        for s in range(FF_SHARDS):
            rl = jnp.maximum(_dot(h, w1_ref[s]), 0.0)
            acc += _dot((rl * rl).astype(BF16), w2_ref[FF_SHARD * s:FF_SHARD * (s + 1), :])
        yn, r = _rms(acc)
        gv = gf_ref[...]
        err = yn * gv - t_ref[...]
        loss_ref[...] += jnp.sum(err * err, axis=0, keepdims=True) * (0.5 / D_MODEL)
        dy = err * (1.0 / D_MODEL)
        dg_ref[...] += jnp.sum(dy * yn, axis=0, keepdims=True)
        dx_ref[...] = _rms_bwd(yn, r, dy * gv)

    return _pc(
        body, name="mlp_fwd_loss", grid=(T // tm,),
        in_specs=[_row_spec(tm, 1024), _row_spec(tm, 1024), _const_spec((1, 1024)), _const_spec((1, 1024)), _hbm_spec()],
        out_specs=[_row_spec(tm, 1024), _const_spec((1, 1024)), _const_spec((1, 1024))],
        out_shape=[jax.ShapeDtypeStruct((T, 1024), F32), jax.ShapeDtypeStruct((1, 1024), F32),
                   jax.ShapeDtypeStruct((1, 1024), F32)],
        scratch_shapes=[pltpu.VMEM((FF_SHARDS, D_MODEL, FF_SHARD), BF16), pltpu.VMEM((D_FF, D_MODEL), BF16)],
        compiler_params=_cparams(("arbitrary",)),
    )(x1, target, g, g_fin, wa)


def _mlp_bwd(x1, dx2, g, wa):
    T = x1.shape[0]
    tm = TOKEN_TILE

    def body(x_ref, dx2_ref, g_ref, wa_hbm, dx1_ref, h_ref, act_ref, da_ref, dg_ref, w1_ref, w2_ref):
        @pl.when(pl.program_id(0) == 0)
        def _():
            _load_late_weight(wa_hbm, w1_ref, "w_mlp_in")
            _load_late_weight(wa_hbm, w2_ref, "w_mlp_out")
            dg_ref[...] = jnp.zeros_like(dg_ref)

        xn, r = _rms(x_ref[...])
        gv = g_ref[...]
        h = (xn * gv).astype(BF16)
        h_ref[...] = h
        dx2 = dx2_ref[...]
        dx2b = dx2.astype(BF16)
        dh = jnp.zeros((tm, D_MODEL), F32)
        for s in range(FF_SHARDS):
            ff = slice(FF_SHARD * s, FF_SHARD * (s + 1))
            rl = jnp.maximum(_dot(h, w1_ref[s]), 0.0)
            act_ref[:, ff] = (rl * rl).astype(BF16)
            da = (_dot_nt(dx2b, w2_ref[ff, :]) * (2.0 * rl)).astype(BF16)
            da_ref[:, ff] = da
            dh += _dot_nt(da, w1_ref[s])
        dg_ref[...] += jnp.sum(dh * xn, axis=0, keepdims=True)
        dx1_ref[...] = dx2 + _rms_bwd(xn, r, dh * gv)

    return _pc(
        body, name="mlp_bwd", grid=(T // tm,),
        in_specs=[_row_spec(tm, 1024), _row_spec(tm, 1024), _const_spec((1, 1024)), _hbm_spec()],
        out_specs=[_row_spec(tm, 1024), _row_spec(tm, 1024), _row_spec(tm, D_FF), _row_spec(tm, D_FF), _const_spec((1, 1024))],
        out_shape=[jax.ShapeDtypeStruct((T, 1024), F32), jax.ShapeDtypeStruct((T, 1024), BF16),
                   jax.ShapeDtypeStruct((T, D_FF), BF16), jax.ShapeDtypeStruct((T, D_FF), BF16),
                   jax.ShapeDtypeStruct((1, 1024), F32)],
        scratch_shapes=[pltpu.VMEM((FF_SHARDS, D_MODEL, FF_SHARD), BF16), pltpu.VMEM((D_FF, D_MODEL), BF16)],
        compiler_params=_cparams(("arbitrary",)),
    )(x1, dx2, g, wa)


WGRAD_OUT_ELEMS = 2 * 1024 * 1024
WGRAD_TILE_BYTES = 4 * 1024 * 1024


def _wgrad(a, b, name, col_shards=None, row_shards_into=None):
    T, K = a.shape
    N = b.shape[1]
    nb = N // col_shards if col_shards else min(N, 1024, max(128, WGRAD_OUT_ELEMS // K))
    tt = min(T, WGRAD_TOKENS)
    while tt * max(K * a.dtype.itemsize, nb * b.dtype.itemsize) > WGRAD_TILE_BYTES:
        tt //= 2
    assert N % nb == 0 and T % tt == 0
    in_specs = [pl.BlockSpec((tt, K), lambda n, t: (t, 0)), pl.BlockSpec((tt, nb), lambda n, t: (t, n))]
    args, aliases = [a, b], {}
    if col_shards:
        out_spec = pl.BlockSpec((None, None, K, nb), lambda n, t: (n, 0, 0, 0))
        out_shape = jax.ShapeDtypeStruct((col_shards, 2, K, nb), F32)
    elif row_shards_into is not None:
        shards, _, rows, cols = row_shards_into.shape
        assert shards * rows == K and cols == N
        out_spec = pl.BlockSpec((shards, None, rows, nb), lambda n, t: (0, 1, 0, n))
        out_shape = jax.ShapeDtypeStruct(row_shards_into.shape, F32)
        in_specs.append(_hbm_spec())
        args.append(row_shards_into)
        aliases = {2: 0}
    else:
        out_spec = pl.BlockSpec((K, nb), lambda n, t: (0, n))
        out_shape = jax.ShapeDtypeStruct((K, N), F32)

    def body(a_ref, b_ref, *rest):
        o_ref = rest[-1]

        @pl.when(pl.program_id(1) == 0)
        def _():
            o_ref[...] = jnp.zeros_like(o_ref)

        o_ref[...] += _dot_tn(a_ref[...].astype(BF16), b_ref[...].astype(BF16)).reshape(o_ref.shape)

    return _pc(
        body, name=name, grid=(N // nb, T // tt), in_specs=in_specs, out_specs=out_spec, out_shape=out_shape,
        input_output_aliases=aliases, compiler_params=_cparams(("parallel", "arbitrary")),
    )(*args)


def _s5_block_weights(bb_re, bb_im, c_re, c_im):
    eye = jnp.eye(8, dtype=F32)
    bre = bb_re.reshape(16, S5_BLOCKS, 8, 64)
    bim = bb_im.reshape(16, S5_BLOCKS, 8, 64)
    wb_re = jnp.einsum('kjgp,gh->jhkgp', bre, eye).reshape(S5_BLOCKS, _BI, _BW)
    wb_im = jnp.einsum('kjgp,gh->jhkgp', bim, eye).reshape(S5_BLOCKS, _BI, _BW)
    wb4 = jnp.concatenate([wb_re, wb_im], axis=2).astype(BF16)
    cre = c_re.reshape(S5_BLOCKS, 8, 16, 64)
    cim = c_im.reshape(S5_BLOCKS, 8, 16, 64)
    wc_re = jnp.einsum('jgkp,gh->jgphk', cre, eye).reshape(S5_BLOCKS, _BW, _BI)
    wc_im = jnp.einsum('jgkp,gh->jgphk', -cim, eye).reshape(S5_BLOCKS, _BW, _BI)
    wc4 = jnp.concatenate([wc_re, wc_im], axis=1).astype(BF16)
    return wb4, wc4


def _s5_block_grads(dwb4, dwc4):
    eye = jnp.eye(8, dtype=F32)
    dwb = dwb4.reshape(S5_BLOCKS, 8, 16, 2, 8, 64)
    dbb = jnp.einsum('jhkrgp,gh->rkjgp', dwb, eye).reshape(2, 16, S5_STATES)
    dwc = dwc4.reshape(S5_BLOCKS, 2, 8, 64, 8, 16)
    dc = jnp.einsum('jrgphk,gh->rjgkp', dwc, eye).reshape(2, 32, 16, 64)
    return dbb[0], dbb[1], dc[0], -dc[1]


def _row(v, width=None):
    v = v.reshape(1, -1)
    if width is not None and v.shape[1] < width:
        v = jnp.concatenate([v, jnp.zeros((1, width - v.shape[1]), v.dtype)], axis=1)
    return v


def _local_step(x, target, p, comm=None):
    g_mix, g_mlp, g_fin = _row(p["norm_mix_g"]), _row(p["norm_mlp_g"]), _row(p["norm_final_g"])
    conv_b = _row(p["conv_b"])
    dt_bias = _row(p["dt_bias"], DT_PAD)
    alog = _row(p["a_log"], DT_PAD)
    dvec = _row(jnp.repeat(p["d_ssd"], SSD_HEADDIM))
    gssd = _row(p["ssd_norm_g"])
    s5d = _row(p["s5_d"])
    glu_b = _row(p["s5_glu_b"])
    head_sel = (jnp.arange(SSD_INNER)[:, None] // SSD_HEADDIM == jnp.arange(DT_PAD)[None, :]).astype(F32)

    a_re = p["s5_a_re"].reshape(1, S5_STATES)
    a_im = p["s5_a_im"].reshape(1, S5_STATES)
    log_dt = jnp.repeat(p["s5_log_dt"], 64).reshape(1, S5_STATES)
    b_re = p["s5_b_re"].reshape(S5_STATES, 16).T
    b_im = p["s5_b_im"].reshape(S5_STATES, 16).T
    ab_re, ab_im, bb_re, bb_im = _s5_disc(a_re, a_im, log_dt, b_re, b_im)
    wb4, wc4 = _s5_block_weights(bb_re, bb_im, p["s5_c_re"], p["s5_c_im"])
    ab = jnp.concatenate([ab_re, ab_im, jnp.zeros((6, S5_STATES), F32)], axis=0)

    wp = p["w_in_perm"]

    first_args = (x, g_mix, wp, wb4, wc4, ab, s5d)
    if comm is None:
        z, xbc_raw, u5, gates, dt_raw, h, y5, s5_states = _inproj_s5_fwd(*first_args)
    else:
        first_out, late = _inproj_s5_fwd(*first_args, rider=_Gather(comm["late_srcs"], comm["late_ks"]))
        z, xbc_raw, u5, gates, dt_raw, h, y5, s5_states = first_out
        p = {**p, **comm["late_unpack"](late)}
    xbc_act, dt = _conv_fwd(xbc_raw, dt_raw, p["conv_w"], conv_b, dt_bias)
    ys, ssd_states = _ssd_fwd(xbc_act, dt, alog)
    wa, glu_w = p["late_weights"], p["s5_glu_w"]
    x1 = _merge_fwd(ys, xbc_act, z, y5, gates, x, dvec, gssd, glu_w, glu_b, wa)
    dx2, loss_lanes, d_gfin = _mlp_fwd_loss(x1, target, g_mlp, g_fin, wa)

    dx1, h2, act, da1, d_gmlp = _mlp_bwd(x1, dx2, g_mlp, wa)
    g_mlp4 = _wgrad(h2, da1, "wgrad_mlp_in", col_shards=FF_SHARDS)
    g_mlp4 = _wgrad(act, dx2, "wgrad_mlp_out", row_shards_into=g_mlp4)
    d_w_mlp_in, d_w_mlp_out = g_mlp4[:, 0], g_mlp4[:, 1].reshape(D_FF, D_MODEL)
    merge_args = (ys, xbc_act, z, y5, gates, dx1, dvec, gssd, glu_w, glu_b, wa, head_sel)
    if comm is None:
        merge_out = _merge_bwd(*merge_args)
    else:
        g_mlp = g_mlp4.reshape(N_CHIPS, 2 * FF_SHARD, D_MODEL)
        merge_out, (sib_mlp,) = _merge_bwd(*merge_args, rider=_Pair([g_mlp]))
        pf_mlp, pb_mlp = _pair_sum(comm["place"], g_mlp, sib_mlp, "pair_sum_mlp")
    (dys, dz, dy5, dgates, mg, ya, yb, dpa, dpb, gel, dpre, d_dssd, d_gssd, d_glu_b) = merge_out
    d_w_out = _wgrad(mg, dx1, "wgrad_out")
    d_w_branch = jnp.concatenate([_wgrad(ya, dpa, "wgrad_branch_a"), _wgrad(yb, dpb, "wgrad_branch_b")], axis=0)
    d_glu_w = _wgrad(gel, dpre, "wgrad_glu")
    s5_args = (u5, dy5, wb4, wc4, ab, s5d, s5_states)
    if comm is None:
        du5, dwb4, dwc4, dab, d_s5d = _s5_bwd(*s5_args)
        mlp_total = None
    else:
        (du5, dwb4, dwc4, dab, d_s5d), (got_mlp,) = _s5_bwd(*s5_args, rider=_Chip([pb_mlp]))
        mlp_total = _chip_sum(comm["place"], pf_mlp, got_mlp, "chip_sum_mlp")
    dbb_re, dbb_im, d_c_re, d_c_im = _s5_block_grads(dwb4, dwc4)
    d_a_re, d_a_im, d_log_dt, d_b_re, d_b_im = _s5_disc_bwd(
        a_re, a_im, log_dt, b_re, b_im, dab[0:1], dab[1:2], dbb_re, dbb_im)
    dxs_s, dB, dC, ddt, d_alog = _ssd_bwd(xbc_act, dt, alog, ssd_states, dys)
    dxbc_raw, ddt_raw, d_conv_w, d_conv_b, d_dt_bias = _conv_bwd(
        xbc_raw, dt_raw, dys, dvec, dxs_s, dB, dC, ddt, p["conv_w"], conv_b, dt_bias)
    d_w_in = dict(z=_wgrad(h, dz, "wgrad_in_z"), xbc=_wgrad(h, dxbc_raw, "wgrad_in_xbc"),
                  dt=_wgrad(h, ddt_raw, "wgrad_in_dt")[:, :16], u5=_wgrad(h, du5, "wgrad_in_u5"),
                  gates=_wgrad(h, dgates, "wgrad_in_gates"))
    w_in_pieces = [(c0, d_w_in[n]) for n, c0, _ in W_IN_PIECES]
    inproj_args = (x, dx1, dz, dxbc_raw, du5, dgates, ddt_raw, g_mix, wp)
    if comm is None:
        dx, d_gmix = _inproj_bwd(*inproj_args)
        late_totals = None
    else:
        g_b, g_in = _late_buffers(d_w_out, d_w_branch, d_glu_w, d_conv_w[:CONV_K], w_in_pieces)
        sib_b, sib_in = _exchange(_Pair([g_b, g_in]), "pair_exchange")
        pf_b, pb_b = _pair_sum(comm["place"], g_b, sib_b, "pair_sum_b")
        pf_in, pb_in = _pair_sum(comm["place"], g_in, sib_in, "pair_sum_in")
        (dx, d_gmix), (got_b, got_in) = _inproj_bwd(*inproj_args, rider=_Chip([pb_b, pb_in]))
        late_totals = (_chip_sum(comm["place"], pf_b, got_b, "chip_sum_b"),
                       _chip_sum(comm["place"], pf_in, got_in, "chip_sum_in"))

    grads = dict(
        norm_mix_g=d_gmix.reshape(-1), w_in_pieces=w_in_pieces, late_totals=late_totals,
        conv_w=d_conv_w[:CONV_K], conv_b=d_conv_b.reshape(-1),
        dt_bias=d_dt_bias[0, :16], a_log=d_alog[0, :16], d_ssd=d_dssd[0, :16], ssd_norm_g=d_gssd.reshape(-1),
        s5_a_re=d_a_re.reshape(32, 64), s5_a_im=d_a_im.reshape(32, 64), s5_log_dt=d_log_dt[0, :32],
        s5_b_re=d_b_re.T.reshape(32, 64, 16), s5_b_im=d_b_im.T.reshape(32, 64, 16), s5_c_re=d_c_re, s5_c_im=d_c_im,
        s5_d=d_s5d.reshape(-1), s5_glu_w=d_glu_w, s5_glu_b=d_glu_b.reshape(-1), w_branch=d_w_branch, w_out=d_w_out,
        norm_mlp_g=d_gmlp.reshape(-1), w_mlp_in=d_w_mlp_in, w_mlp_out=d_w_mlp_out, norm_final_g=d_gfin.reshape(-1),
        mlp_total=mlp_total)
    return jnp.sum(loss_lanes), dx, grads


MESH = pl.DeviceIdType.MESH
N_CHIPS = 4


def _place():
    x, y, c = lax.axis_index("x"), lax.axis_index("y"), lax.axis_index("c")
    chips = [(1 - x, y), (x, 1 - y), (1 - x, 1 - y)]
    return x, y, c, chips


def _remote(src, dst, send_sems, recv_sems, k, to):
    return pltpu.make_async_remote_copy(src_ref=src, dst_ref=dst, send_sem=send_sems.at[k], recv_sem=recv_sems.at[k],
                                        device_id=to, device_id_type=MESH)


def _row_chunks(rows, k, align):
    step = rows // k
    assert rows % k == 0 and step % align == 0, (rows, k, align)
    return [(i * step, step) for i in range(k)]


ICI_CHUNKS = 4
D2D_CHUNKS = 24


class _Gather:
    def __init__(self, srcs, ks):
        self.inputs = list(srcs)
        self.out_shapes = [jax.ShapeDtypeStruct((N_CHIPS,) + a.shape, a.dtype) for a in srcs]
        self.halves = [a.shape[0] // 2 for a in srcs]
        self.pieces = [_row_chunks(h, k, 32 // a.dtype.itemsize) for a, h, k in zip(srcs, self.halves, ks)]
        self.n_ici = 3 * sum(ks)
        self.n_sems = 2 * self.n_ici + len(srcs)

    def _plan(self, src_refs, out_refs, send_sems, recv_sems):
        x, y, c, chips = _place()
        own = 2 * x + y
        sib = (x, y, 1 - c)
        first, fwd_plan, k = [], [], 0
        for a, (src_ref, out_ref) in enumerate(zip(src_refs, out_refs)):
            h = self.halves[a]
            for r0, nr in self.pieces[a]:
                for cx, cy in chips:
                    first.append(_remote(src_ref.at[pl.ds(c * h + r0, nr), :], out_ref.at[own, pl.ds(c * h + r0, nr), :],
                                         send_sems, recv_sems, k, (cx, cy, c)))
                    fwd_plan.append((out_ref, 2 * cx + cy, h, r0, nr, k, (cx, cy, c)))
                    k += 1
        for a, (src_ref, out_ref) in enumerate(zip(src_refs, out_refs)):
            first.append(_remote(src_ref, out_ref.at[own], send_sems, recv_sems, 2 * self.n_ici + a, sib))
        return first, fwd_plan, c, sib

    def issue(self, src_refs, out_refs, send_sems, recv_sems):
        for cp in self._plan(src_refs, out_refs, send_sems, recv_sems)[0]:
            cp.start()

    def complete(self, src_refs, out_refs, send_sems, recv_sems):
        first, fwd_plan, c, sib = self._plan(src_refs, out_refs, send_sems, recv_sems)
        passed = []
        for out_ref, s, h, r0, nr, k, frm in fwd_plan:
            got = out_ref.at[s, pl.ds(c * h + r0, nr), :]
            _remote(got, got, send_sems, recv_sems, k, frm).wait_recv()
            fw = _remote(got, got, send_sems, recv_sems, self.n_ici + k, sib)
            fw.start()
            passed.append(fw)
        for out_ref, s, h, r0, nr, k, frm in fwd_plan:
            got = out_ref.at[s, pl.ds((1 - c) * h + r0, nr), :]
            _remote(got, got, send_sems, recv_sems, self.n_ici + k, sib).wait_recv()
        own_copies = first[self.n_ici:]
        for cp in own_copies:
            cp.wait_recv()
        for cp in first + passed:
            cp.wait_send()


def _exchange(rider, name):
    ri, ro = len(rider.inputs), len(rider.out_shapes)

    def body(*refs):
        rider.issue(refs[:ri], refs[ri:ri + ro], *refs[ri + ro:])
        rider.complete(refs[:ri], refs[ri:ri + ro], *refs[ri + ro:])

    return _pc(
        body, name=name, in_specs=[_hbm_spec()] * ri, out_specs=[_hbm_spec()] * ro, out_shape=list(rider.out_shapes),
        scratch_shapes=[pltpu.SemaphoreType.DMA((rider.n_sems,))] * 2,
    )(*rider.inputs)


def _call(body, rider=None, **kw):
    if rider is None:
        return _pc(body, **kw)
    single = not isinstance(kw["out_shape"], (list, tuple))
    out_specs = [kw["out_specs"]] if single else list(kw["out_specs"])
    out_shape = [kw["out_shape"]] if single else list(kw["out_shape"])
    scratch = list(kw.get("scratch_shapes", ()))
    n_in, n_out, n_scr = len(kw["in_specs"]), len(out_specs), len(scratch)
    ri, ro = len(rider.inputs), len(rider.out_shapes)
    steps = kw["grid"][0]

    def wrapped(*refs):
        o0 = n_in + ri
        s0 = o0 + n_out + ro
        r_in, r_out, sems = refs[n_in:o0], refs[o0 + n_out:s0], refs[s0 + n_scr:]

        @pl.when(pl.program_id(0) == 0)
        def _():
            rider.issue(r_in, r_out, *sems)

        body(*refs[:n_in], *refs[o0:o0 + n_out], *refs[s0:s0 + n_scr])

        @pl.when(pl.program_id(0) == steps - 1)
        def _():
            rider.complete(r_in, r_out, *sems)

    f = _pc(wrapped, name=kw["name"], grid=kw["grid"], in_specs=list(kw["in_specs"]) + [_hbm_spec()] * ri,
            out_specs=out_specs + [_hbm_spec()] * ro, out_shape=out_shape + list(rider.out_shapes),
            scratch_shapes=scratch + [pltpu.SemaphoreType.DMA((rider.n_sems,))] * 2, compiler_params=kw["compiler_params"])

    def run(*args):
        res = f(*args, *rider.inputs)
        return (res[0] if single else res[:n_out]), res[n_out:]

    return run


def _d2d_pieces(rows):
    k = next(k for k in range(24, 0, -1) if rows % k == 0 and (rows // k) % 8 == 0)
    return _row_chunks(rows, k, 8)


class _Pair:
    def __init__(self, gs, small=None):
        self.n = len(gs)
        self.halves = [g.shape[1] // 2 for g in gs]
        self.inputs = list(gs) + ([small] if small is not None else [])
        self.out_shapes = [jax.ShapeDtypeStruct((N_CHIPS, h, g.shape[2]), F32) for g, h in zip(gs, self.halves)]
        if small is not None:
            self.out_shapes.append(jax.ShapeDtypeStruct(small.shape, F32))
        self.n_sems = len(self.inputs)

    def issue(self, in_refs, out_refs, send_sems, recv_sems):
        x, y, c, _ = _place()
        sib = (x, y, 1 - c)
        for a in range(self.n):
            h = self.halves[a]
            for s in range(N_CHIPS):
                for r0, nr in _d2d_pieces(h):
                    _remote(in_refs[a].at[s, pl.ds((1 - c) * h + r0, nr), :], out_refs[a].at[s, pl.ds(r0, nr), :],
                            send_sems, recv_sems, a, sib).start()
        for a in range(self.n, len(self.inputs)):
            _remote(in_refs[a], out_refs[a], send_sems, recv_sems, a, sib).start()

    def complete(self, in_refs, out_refs, send_sems, recv_sems):
        x, y, c, _ = _place()
        for a in range(len(self.inputs)):
            _remote(out_refs[a], out_refs[a], send_sems, recv_sems, a, (x, y, 1 - c)).wait()


SUM_BLOCKS = 4


def _pair_sum(place, g, sib, name):
    n, R, C = g.shape
    H = R // 2
    rb = H // SUM_BLOCKS
    assert H % SUM_BLOCKS == 0 and rb % 16 == 0

    def body(place_ref, a_ref, b_ref, pf_ref, pb_ref):
        p = a_ref[...] + b_ref[...]
        pf_ref[...] = p
        pb_ref[...] = p.astype(BF16)

    blk = pl.BlockSpec((1, rb, C), lambda s, i, pr: (s, i, 0))
    mine = pl.BlockSpec((1, rb, C), lambda s, i, pr: (s, pr[1] * SUM_BLOCKS + i, 0))
    return _pc(
        body, name=name, out_shape=[jax.ShapeDtypeStruct((n, H, C), F32), jax.ShapeDtypeStruct((n, H, C), BF16)],
        grid_spec=pltpu.PrefetchScalarGridSpec(num_scalar_prefetch=1, grid=(n, SUM_BLOCKS), in_specs=[mine, blk],
                                               out_specs=[blk, blk]),
        compiler_params=_cparams(("arbitrary", "arbitrary")),
    )(place, g, sib)


class _Chip:
    def __init__(self, pbs, psmall=None):
        self.n = len(pbs)
        self.rows = [pb.shape[1] for pb in pbs]
        self.inputs = list(pbs) + ([psmall] if psmall is not None else [])
        self.out_shapes = [jax.ShapeDtypeStruct((3,) + pb.shape[1:], BF16) for pb in pbs]
        if psmall is not None:
            self.out_shapes.append(jax.ShapeDtypeStruct((N_CHIPS,) + psmall.shape, F32))
        self.n_sems = 3 * len(self.inputs)

    def issue(self, in_refs, out_refs, send_sems, recv_sems):
        x, y, c, chips = _place()
        own = 2 * x + y
        for j, (cx, cy) in enumerate(chips):
            for a in range(self.n):
                for r0, nr in _row_chunks(self.rows[a], ICI_CHUNKS, 16):
                    _remote(in_refs[a].at[2 * cx + cy, pl.ds(r0, nr), :], out_refs[a].at[j, pl.ds(r0, nr), :],
                            send_sems, recv_sems, 3 * a + j, (cx, cy, c)).start()
            for a in range(self.n, len(self.inputs)):
                _remote(in_refs[a], out_refs[a].at[own], send_sems, recv_sems, 3 * a + j, (cx, cy, c)).start()

    def complete(self, in_refs, out_refs, send_sems, recv_sems):
        x, y, c, chips = _place()
        own = 2 * x + y
        for j, (cx, cy) in enumerate(chips):
            for a in range(self.n):
                _remote(in_refs[a].at[own], out_refs[a].at[j], send_sems, recv_sems, 3 * a + j, (cx, cy, c)).wait()
            for a in range(self.n, len(self.inputs)):
                _remote(in_refs[a], out_refs[a].at[2 * cx + cy], send_sems, recv_sems, 3 * a + j, (cx, cy, c)).wait()


def _chip_sum(place, pf, got, name):
    _, H, C = pf.shape
    rb = H // SUM_BLOCKS

    def body(place_ref, o_ref, g_ref, tot_ref):
        tot_ref[...] = ((o_ref[0] + g_ref[0].astype(F32)) + g_ref[1].astype(F32)) + g_ref[2].astype(F32)

    ins = [pl.BlockSpec((1, rb, C), lambda i, pr: (pr[0], i, 0)), pl.BlockSpec((3, rb, C), lambda i, pr: (0, i, 0))]
    out = pl.BlockSpec((rb, C), lambda i, pr: (pr[1] * SUM_BLOCKS + i, 0))
    return _pc(
        body, name=name, out_shape=jax.ShapeDtypeStruct((2 * H, C), F32),
        grid_spec=pltpu.PrefetchScalarGridSpec(num_scalar_prefetch=1, grid=(SUM_BLOCKS,), in_specs=ins, out_specs=out),
        compiler_params=_cparams(("arbitrary",)),
    )(place, pf, got)


def _half_exchange(fulls):
    n = len(fulls)

    def body(*refs):
        in_refs, out_refs = refs[:n], refs[n:2 * n]
        send_sems, recv_sems = refs[2 * n:]
        x, y, c, _ = _place()
        sib = (x, y, 1 - c)
        for a in range(n):
            h = fulls[a].shape[0] // 2
            for r0, nr in _d2d_pieces(h):
                rows = pl.ds(c * h + r0, nr)
                _remote(in_refs[a].at[rows, :], out_refs[a].at[rows, :], send_sems, recv_sems, a, sib).start()
        for a in range(n):
            h = fulls[a].shape[0] // 2
            _remote(in_refs[a].at[pl.ds(c * h, h), :], out_refs[a].at[pl.ds((1 - c) * h, h), :], send_sems, recv_sems, a,
                    sib).wait()

    return _pc(
        body, name="half_exchange", in_specs=[_hbm_spec()] * n, out_specs=[_hbm_spec()] * n,
        out_shape=[jax.ShapeDtypeStruct(f.shape, F32) for f in fulls],
        input_output_aliases={a: a for a in range(n)},
        scratch_shapes=[pltpu.SemaphoreType.DMA((n,)), pltpu.SemaphoreType.DMA((n,))],
    )(*fulls)


def _small_allreduce(pack):
    R, C = pack.shape

    def body(p_ref, o_ref, sib_ref, pair_ref, slots_ref, send_sems, recv_sems):
        x, y, c, chips = _place()
        own = 2 * x + y
        cp = _remote(p_ref, sib_ref, send_sems, recv_sems, 0, (x, y, 1 - c))
        cp.start()
        cp.wait()
        pair_ref[...] = p_ref[...] + sib_ref[...]
        slots_ref[own] = pair_ref[...]
        out = [_remote(pair_ref, slots_ref.at[own], send_sems, recv_sems, 1 + j, (cx, cy, c)) for j, (cx, cy) in enumerate(chips)]
        for cp in out:
            cp.start()
        for j, (cx, cy) in enumerate(chips):
            _remote(pair_ref, slots_ref.at[2 * cx + cy], send_sems, recv_sems, 1 + j, (cx, cy, c)).wait()
        o_ref[...] = ((slots_ref[0] + slots_ref[1]) + slots_ref[2]) + slots_ref[3]

    vmem = pl.BlockSpec(memory_space=pltpu.VMEM)
    return _pc(
        body, name="small_allreduce", in_specs=[vmem], out_specs=vmem, out_shape=jax.ShapeDtypeStruct((R, C), F32),
        scratch_shapes=[pltpu.VMEM((R, C), F32), pltpu.VMEM((R, C), F32), pltpu.VMEM((N_CHIPS, R, C), F32),
                        pltpu.SemaphoreType.DMA((4,)), pltpu.SemaphoreType.DMA((4,))],
    )(pack)


def _adamw(w, g, m, v, name, g_row0=0, with_grad=False, col_block=None):
    R, C = w.shape
    rb = 256 if R % 256 == 0 else (128 if R % 128 == 0 else R)
    if col_block:
        rb = R
    assert g_row0 % rb == 0

    def body(w_ref, g_ref, m_ref, v_ref, d_ref, nm_ref, nv_ref, *g_out):
        gv = g_ref[...]
        m2 = ADAM_B1 * m_ref[...] + (1.0 - ADAM_B1) * gv
        v2 = ADAM_B2 * v_ref[...] + (1.0 - ADAM_B2) * (gv * gv)
        m_hat = m2 * (1.0 / (1.0 - ADAM_B1 ** ADAM_STEP))
        v_hat = v2 * (1.0 / (1.0 - ADAM_B2 ** ADAM_STEP))
        d_ref[...] = -ADAM_LR * (m_hat / (jnp.sqrt(v_hat) + ADAM_EPS) + ADAM_WD * w_ref[...])
        nm_ref[...] = m2
        nv_ref[...] = v2
        if with_grad:
            g_out[0][...] = gv

    if col_block:
        spec = g_spec = pl.BlockSpec((R, col_block), lambda i: (0, i))
        steps = C // col_block
    else:
        spec = pl.BlockSpec((rb, C), lambda i: (i, 0))
        g_spec = pl.BlockSpec((rb, C), lambda i: (g_row0 // rb + i, 0))
        steps = R // rb
    n_out = 4 if with_grad else 3
    return _pc(
        body, name=name, grid=(steps,), in_specs=[spec, g_spec, spec, spec], out_specs=[spec] * n_out,
        out_shape=[jax.ShapeDtypeStruct((R, C), F32)] * n_out, compiler_params=_cparams(("parallel",)),
    )(w, g, m, v)


PACK_COLS = 1024
ROWS_A = (("w_mlp_in", 0, 1024), ("w_mlp_out", 1024, 1024), ("w_out", 2048, 256), ("w_branch", 2304, 384))
ROWS_A_TOTAL = 2688
ROWS_B = (("w_out", 0, 256), ("w_branch", 256, 384))
ROW_B_GLU, ROW_B_CONV, ROWS_B_TOTAL = 640, 704, 768
W_IN_SHARD = 1412
CONV_PAD_ROWS = 16
SMALL = (("norm_mix_g", (1024,)), ("conv_b", (2048,)), ("dt_bias", (16,)), ("a_log", (16,)), ("d_ssd", (16,)),
         ("ssd_norm_g", (1024,)), ("s5_a_re", (32, 64)), ("s5_a_im", (32, 64)), ("s5_log_dt", (32,)),
         ("s5_b_re", (32, 64, 16)), ("s5_b_im", (32, 64, 16)), ("s5_c_re", (32, 16, 64)), ("s5_c_im", (32, 16, 64)),
         ("s5_d", (512,)), ("s5_glu_b", (512,)), ("norm_mlp_g", (1024,)), ("norm_final_g", (1024,)))
SMALL_ROWS = 144
SMALL_COUNT = sum(math.prod(shp) for _, shp in SMALL)
GLU_ROWS = S5_WIDTH * S5_WIDTH // PACK_COLS
CONV_ROWS = CONV_K * CONV_DIM // PACK_COLS
W_IN_PIECES = (("z", 0, 1024), ("xbc", 1024, 2048), ("dt", OFF_DT, 16), ("u5", OFF_U, 512), ("gates", 3600, 2048))


def _pack_small(parts):
    flat = jnp.concatenate([a.astype(F32).reshape(-1) for a in parts])
    return jnp.concatenate([flat, jnp.zeros((SMALL_ROWS * PACK_COLS - flat.shape[0],), F32)]).reshape(SMALL_ROWS, PACK_COLS)


def _unpack_small(pack):
    flat, out, r = pack.reshape(-1), {}, 0
    for name, shp in SMALL:
        n = math.prod(shp)
        out[name] = flat[r:r + n].reshape(shp)
        r += n
    return out


def _late_buffers(d_w_out, d_w_branch, d_glu_w, d_conv_w, w_in_pieces):
    conv4 = d_conv_w.reshape(CONV_K, N_CHIPS, 512).transpose(1, 0, 2).reshape(N_CHIPS, CONV_ROWS // N_CHIPS, PACK_COLS)
    g_b = jnp.concatenate(
        [d_w_out.reshape(N_CHIPS, -1, PACK_COLS), d_w_branch.reshape(N_CHIPS, -1, PACK_COLS),
         d_glu_w.reshape(N_CHIPS, GLU_ROWS // N_CHIPS, PACK_COLS),
         jnp.pad(conv4, ((0, 0), (0, ROWS_B_TOTAL - ROW_B_CONV - CONV_ROWS // N_CHIPS), (0, 0)))], axis=1)
    g_in = jnp.stack([jnp.concatenate(_column_range(w_in_pieces, W_IN_SHARD * s, W_IN_SHARD * (s + 1)), axis=1)
                      for s in range(N_CHIPS)])
    return g_b, g_in


def _column_range(pieces, lo, hi):
    out = []
    for c0, a in pieces:
        a0, a1 = max(lo, c0), min(hi, c0 + a.shape[-1])
        if a0 < a1:
            out.append(a[..., a0 - c0:a1 - c0])
    return out


def kernel(x, norm_mix_g, w_in, conv_w, conv_b, dt_bias, a_log, d_ssd, ssd_norm_g, s5_a_re, s5_a_im, s5_log_dt, s5_b_re, s5_b_im, s5_c_re, s5_c_im, s5_d, s5_glu_w, s5_glu_b, w_branch, w_out, norm_mlp_g, w_mlp_in, w_mlp_out, norm_final_g, loss_target, m_norm_mix_g, m_w_in, m_conv_w, m_conv_b, m_dt_bias, m_a_log, m_d_ssd, m_ssd_norm_g, m_s5_a_re, m_s5_a_im, m_s5_log_dt, m_s5_b_re, m_s5_b_im, m_s5_c_re, m_s5_c_im, m_s5_d, m_s5_glu_w, m_s5_glu_b, m_w_branch, m_w_out, m_norm_mlp_g, m_w_mlp_in, m_w_mlp_out, m_norm_final_g, v_norm_mix_g, v_w_in, v_conv_w, v_conv_b, v_dt_bias, v_a_log, v_d_ssd, v_ssd_norm_g, v_s5_a_re, v_s5_a_im, v_s5_log_dt, v_s5_b_re, v_s5_b_im, v_s5_c_re, v_s5_c_im, v_s5_d, v_s5_glu_w, v_s5_glu_b, v_w_branch, v_w_out, v_norm_mlp_g, v_w_mlp_in, v_w_mlp_out, v_norm_final_g):
    names = ("norm_mix_g", "w_in", "conv_w", "conv_b", "dt_bias", "a_log", "d_ssd", "ssd_norm_g", "s5_a_re", "s5_a_im",
             "s5_log_dt", "s5_b_re", "s5_b_im", "s5_c_re", "s5_c_im", "s5_d", "s5_glu_w", "s5_glu_b", "w_branch", "w_out",
             "norm_mlp_g", "w_mlp_in", "w_mlp_out", "norm_final_g")
    w = dict(zip(names, (norm_mix_g, w_in, conv_w, conv_b, dt_bias, a_log, d_ssd, ssd_norm_g, s5_a_re, s5_a_im, s5_log_dt,
                         s5_b_re, s5_b_im, s5_c_re, s5_c_im, s5_d, s5_glu_w, s5_glu_b, w_branch, w_out, norm_mlp_g,
                         w_mlp_in, w_mlp_out, norm_final_g)))
    m = dict(zip(names, (m_norm_mix_g, m_w_in, m_conv_w, m_conv_b, m_dt_bias, m_a_log, m_d_ssd, m_ssd_norm_g, m_s5_a_re,
                         m_s5_a_im, m_s5_log_dt, m_s5_b_re, m_s5_b_im, m_s5_c_re, m_s5_c_im, m_s5_d, m_s5_glu_w,
                         m_s5_glu_b, m_w_branch, m_w_out, m_norm_mlp_g, m_w_mlp_in, m_w_mlp_out, m_norm_final_g)))
    v = dict(zip(names, (v_norm_mix_g, v_w_in, v_conv_w, v_conv_b, v_dt_bias, v_a_log, v_d_ssd, v_ssd_norm_g, v_s5_a_re,
                         v_s5_a_im, v_s5_log_dt, v_s5_b_re, v_s5_b_im, v_s5_c_re, v_s5_c_im, v_s5_d, v_s5_glu_w,
                         v_s5_glu_b, v_w_branch, v_w_out, v_norm_mlp_g, v_w_mlp_in, v_w_mlp_out, v_norm_final_g)))

    cx, cy, cc = lax.axis_index("x"), lax.axis_index("y"), lax.axis_index("c")
    own = 2 * cx + cy
    place = jnp.stack([own, cc]).astype(jnp.int32)

    src_conv = jnp.concatenate([conv_w, jnp.zeros((CONV_PAD_ROWS - CONV_K, 512), F32)], axis=0)
    all_in, all_conv = _exchange(_Gather([w_in.astype(BF16), src_conv], [ICI_CHUNKS, 1]), "gather_first")
    p = {n: w[n] for n, _ in SMALL}
    p["conv_w"] = jnp.concatenate([all_conv[s, :CONV_K] for s in range(N_CHIPS)], axis=1)
    shards = [(W_IN_SHARD * s, all_in[s]) for s in range(N_CHIPS)]
    p["w_in_perm"] = jnp.concatenate(
        _column_range(shards, 0, OFF_DT) + _column_range(shards, OFF_U, D_IN_PROJ) + _column_range(shards, OFF_DT, OFF_U)
        + [jnp.zeros((D_MODEL, DT_PAD - 16), BF16)], axis=1)

    def late_unpack(gathered):
        all_a, all_glu = gathered
        return {"late_weights": all_a, "s5_glu_w": all_glu.reshape(S5_WIDTH, S5_WIDTH)}

    comm = dict(place=place, late_ks=[ICI_CHUNKS, 1], late_unpack=late_unpack,
                late_srcs=[jnp.concatenate([w[n].astype(BF16) for n, _, _ in ROWS_A], axis=0), s5_glu_w.astype(BF16)])
    loss_part, grad_x, g = _local_step(x[0], loss_target[0], p, comm)

    red_mlp, red_b, red_in = _half_exchange([g["mlp_total"], *g["late_totals"]])
    small_tot = _small_allreduce(_pack_small([g[n] for n, _ in SMALL] + [loss_part.reshape(1)]))
    loss = small_tot.reshape(-1)[SMALL_COUNT]

    grads = _unpack_small(small_tot)
    delta, new_m, new_v = {}, {}, {}
    for n, r0, _ in ROWS_A[:2]:
        delta[n], new_m[n], new_v[n], grads[n] = _adamw(w[n], red_mlp, m[n], v[n], "adamw_" + n, g_row0=r0, with_grad=True)
    for n, r0, _ in ROWS_B:
        delta[n], new_m[n], new_v[n], grads[n] = _adamw(w[n], red_b, m[n], v[n], "adamw_" + n, g_row0=r0, with_grad=True)
    d_t, m_t, v_t, g_t = _adamw(w_in.T, red_in.T, m_w_in.T, v_w_in.T, "adamw_w_in", with_grad=True, col_block=128)
    delta["w_in"], new_m["w_in"], new_v["w_in"], grads["w_in"] = d_t.T, m_t.T, v_t.T, g_t.T
    grads["s5_glu_w"] = red_b[ROW_B_GLU:ROW_B_GLU + GLU_ROWS // N_CHIPS].reshape(S5_WIDTH // N_CHIPS, S5_WIDTH)
    grads["conv_w"] = red_b[ROW_B_CONV:ROW_B_CONV + CONV_ROWS // N_CHIPS].reshape(CONV_K, CONV_DIM // N_CHIPS)
    for n in ("s5_glu_w", "conv_w"):
        delta[n], new_m[n], new_v[n] = _adamw(w[n], grads[n], m[n], v[n], "adamw_" + n)
    ds, ms, vs = _adamw(_pack_small([w[n] for n, _ in SMALL]), small_tot, _pack_small([m[n] for n, _ in SMALL]),
                        _pack_small([v[n] for n, _ in SMALL]), "adamw_small")
    delta.update(_unpack_small(ds))
    new_m.update(_unpack_small(ms))
    new_v.update(_unpack_small(vs))

    return (loss, grad_x[None], *[grads[n] for n in names], *[delta[n] for n in names],
            *[new_m[n] for n in names], *[new_v[n] for n in names])
```

```python
import functools
import math

import jax
import jax.numpy as jnp
from jax import lax
from jax.experimental import pallas as pl
from jax.experimental.pallas import tpu as pltpu

F32 = jnp.float32
BF16 = jnp.bfloat16

D_MODEL = 1024
SSD_INNER = 1024
SSD_HEADS = 16
SSD_HEADDIM = 64
SSD_GROUPS = 4
SSD_HPG = 4
SSD_STATE = 128
SSD_CHUNK = 128
CONV_K = 4
CONV_DIM = 2048
S5_WIDTH = 512
S5_STATES = 2048
S5_BLOCKS = 4
S5_CHUNK = 128
D_FF = 4096
FF_SHARDS = 4
FF_SHARD = D_FF // FF_SHARDS
EPS = 1e-6
P_Z, P_XBC, P_U5, P_G, P_DT, P_END = 0, 1024, 3072, 3584, 5632, 5760
DT_PAD = 128
OFF_DT, OFF_U = 3072, 3088
D_IN_PROJ = 5648

ADAM_LR, ADAM_B1, ADAM_B2, ADAM_EPS, ADAM_WD, ADAM_STEP = 0.001, 0.9, 0.999, 1e-08, 0.01, 10

TOKEN_TILE = 256
VMEM_LIMIT = 56 * 1024 * 1024
HALO = 8
INPROJ_PIECE = 256
CONV_COLS = 256
CONV_ROWS_BLK = 64
WGRAD_TOKENS = 2048


def _pc(body, **kw):
    return pl.pallas_call(body, **kw)


def _cparams(sem=None):
    return pltpu.CompilerParams(dimension_semantics=sem, vmem_limit_bytes=VMEM_LIMIT)


def _dot(a, b):
    return jnp.dot(a, b, preferred_element_type=F32)


def _dot_nt(a, b):
    return lax.dot_general(a, b, (((1,), (1,)), ((), ())), preferred_element_type=F32)


def _dot_tn(a, b):
    return lax.dot_general(a, b, (((0,), (0,)), ((), ())), preferred_element_type=F32)


def _dot_hi(a, b, dims=(((1,), (0,)), ((), ()))):
    return lax.dot_general(a, b, dims, preferred_element_type=F32, precision=lax.Precision.HIGHEST)


def _split_bf16(x, terms):
    out = []
    for _ in range(terms - 1):
        t = x.astype(BF16)
        out.append(t)
        x = x - t.astype(F32)
    out.append(x.astype(BF16))
    return out


def _dot_split(x, onehots, terms, dims=(((1,), (0,)), ((), ()))):
    acc = None
    for t in _split_bf16(x, terms):
        p = lax.dot_general(t, onehots, dims, preferred_element_type=F32)
        acc = p if acc is None else acc + p
    return acc


def _dot_split_rhs(onehots, x, terms, dims=(((1,), (0,)), ((), ()))):
    acc = None
    for t in _split_bf16(x, terms):
        p = lax.dot_general(onehots, t, dims, preferred_element_type=F32)
        acc = p if acc is None else acc + p
    return acc


def _sigmoid(x):
    return 0.5 * jnp.tanh(0.5 * x) + 0.5


def _softplus(x):
    return jnp.maximum(x, 0.0) + jnp.log(1.0 + jnp.exp(-jnp.abs(x)))


_GELU_C = math.sqrt(2.0 / math.pi)


def _gelu(x):
    return 0.5 * x * (1.0 + jnp.tanh(_GELU_C * (x + 0.044715 * x * x * x)))


def _gelu_grad(x):
    t = jnp.tanh(_GELU_C * (x + 0.044715 * x * x * x))
    return 0.5 * (1.0 + t) + 0.5 * x * (1.0 - t * t) * _GELU_C * (1.0 + 3.0 * 0.044715 * x * x)


def _rms(x):
    r = lax.rsqrt(jnp.mean(x * x, axis=-1, keepdims=True) + EPS)
    return x * r, r


def _rms_bwd(xn, r, dxn):
    return r * (dxn - xn * jnp.mean(dxn * xn, axis=-1, keepdims=True))


def _row_spec(tm, width, col=0):
    return pl.BlockSpec((tm, width), lambda i: (i, col))


def _const_spec(shape):
    nd = len(shape)
    return pl.BlockSpec(shape, lambda i: (0,) * nd)


def _hbm_spec():
    return pl.BlockSpec(memory_space=pl.ANY)


def _load_late_weight(wa_hbm, dst_ref, name):
    r0, nr = next((r0, nr) for n, r0, nr in ROWS_A if n == name)
    for s in range(N_CHIPS):
        dst = dst_ref.at[s] if len(dst_ref.shape) == 3 else dst_ref.at[pl.ds(nr * s, nr), :]
        pltpu.sync_copy(wa_hbm.at[s, pl.ds(r0, nr), :], dst)


def _inproj_bwd(x, dx1, dz, dxbc, du5, dgt, ddt, g, wp, rider=None):
    T = x.shape[0]
    tm = TOKEN_TILE

    def body(x_ref, dx1_ref, dz_ref, dxbc_ref, du5_ref, dgt_ref, ddt_ref, g_ref, w_hbm, dx_ref, dg_ref, w_ref):
        @pl.when(pl.program_id(0) == 0)
        def _():
            pltpu.sync_copy(w_hbm, w_ref)
            dg_ref[...] = jnp.zeros_like(dg_ref)

        xn, r = _rms(x_ref[...])
        gv = g_ref[...]
        dh = _dot_nt(dz_ref[...].astype(BF16), w_ref[:, P_Z:P_XBC])
        dh += _dot_nt(dxbc_ref[...].astype(BF16), w_ref[:, P_XBC:P_U5])
        dh += _dot_nt(du5_ref[...].astype(BF16), w_ref[:, P_U5:P_G])
        dh += _dot_nt(dgt_ref[...].astype(BF16), w_ref[:, P_G:P_DT])
        dh += _dot_nt(ddt_ref[...].astype(BF16), w_ref[:, P_DT:P_END])
        dg_ref[...] += jnp.sum(dh * xn, axis=0, keepdims=True)
        dx_ref[...] = dx1_ref[...] + _rms_bwd(xn, r, dh * gv)

    return _call(
        body, rider, name="inproj_bwd", grid=(T // tm,),
        in_specs=[_row_spec(tm, 1024), _row_spec(tm, 1024), _row_spec(tm, 1024), _row_spec(tm, 2048),
                  _row_spec(tm, 512), _row_spec(tm, 2048), _row_spec(tm, DT_PAD), _const_spec((1, 1024)), _hbm_spec()],
        out_specs=[_row_spec(tm, 1024), _const_spec((1, 1024))],
        out_shape=[jax.ShapeDtypeStruct((T, 1024), F32), jax.ShapeDtypeStruct((1, 1024), F32)],
        scratch_shapes=[pltpu.VMEM((D_MODEL, P_END), BF16)],
        compiler_params=_cparams(("arbitrary",)),
    )(x, dx1, dz, dxbc, du5, dgt, ddt, g, wp)


def _conv_fwd(xbc_raw, dt_raw, conv_w, conv_b, dt_bias):
    T = xbc_raw.shape[0]
    tm = TOKEN_TILE

    def body(u_ref, dtr_ref, w_ref, b_ref, db_ref, act_ref, dt_ref, ext_ref):
        @pl.when(pl.program_id(0) == 0)
        def _():
            ext_ref[0:HALO, :] = jnp.zeros((HALO, CONV_DIM), F32)

        ext_ref[HALO:, :] = u_ref[...]
        for c0 in range(0, CONV_DIM, CONV_COLS):
            cols = slice(c0, c0 + CONV_COLS)
            taps = [w_ref[k:k + 1, cols] for k in range(CONV_K)]
            bias = b_ref[:, cols]
            for r0 in range(0, tm, CONV_ROWS_BLK):
                y = bias + taps[0] * ext_ref[pl.ds(HALO - (CONV_K - 1) + r0, CONV_ROWS_BLK), cols]
                for k in range(1, CONV_K):
                    y += taps[k] * ext_ref[pl.ds(HALO - (CONV_K - 1) + k + r0, CONV_ROWS_BLK), cols]
                act_ref[r0:r0 + CONV_ROWS_BLK, cols] = y * _sigmoid(y)
        ext_ref[0:HALO, :] = u_ref[tm - HALO:tm, :]
        dt_ref[...] = _softplus(dtr_ref[...] + db_ref[...])

    return _pc(
        body, name="conv_fwd", grid=(T // tm,),
        in_specs=[_row_spec(tm, CONV_DIM), _row_spec(tm, DT_PAD), _const_spec((CONV_K, CONV_DIM)),
                  _const_spec((1, CONV_DIM)), _const_spec((1, DT_PAD))],
        out_specs=[_row_spec(tm, CONV_DIM), _row_spec(tm, DT_PAD)],
        out_shape=[jax.ShapeDtypeStruct((T, CONV_DIM), F32), jax.ShapeDtypeStruct((T, DT_PAD), F32)],
        scratch_shapes=[pltpu.VMEM((tm + HALO, CONV_DIM), F32)],
        compiler_params=_cparams(("arbitrary",)),
    )(xbc_raw, dt_raw, conv_w, conv_b, dt_bias)


def _conv_bwd(xbc_raw, dt_raw, dys, dvec, dxs_b, dB, dC, ddt, conv_w, conv_b, dt_bias):
    T = xbc_raw.shape[0]
    tm = TOKEN_TILE
    n = T // tm
    hb = tm // HALO

    def rev(width):
        return pl.BlockSpec((tm, width), lambda i: (n - 1 - i, 0))

    def body(u_ref, up_ref, dtr_ref, dys_ref, dv_ref, dxb_ref, dB_ref, dC_ref, ddt_ref, w_ref, b_ref, db_ref,
             du_ref, ddtr_ref, dw_ref, dcb_ref, ddb_ref, ext_ref, dye_ref):
        i = pl.program_id(0)

        @pl.when(i == 0)
        def _():
            dye_ref[tm:, :] = jnp.zeros((HALO, CONV_DIM), F32)
            dw_ref[...] = jnp.zeros_like(dw_ref)
            dcb_ref[...] = jnp.zeros_like(dcb_ref)
            ddb_ref[...] = jnp.zeros_like(ddb_ref)

        first = (i == n - 1).astype(F32)
        ext_ref[0:HALO, :] = up_ref[...] * (1.0 - first)
        ext_ref[HALO:, :] = u_ref[...]
        for c0 in range(0, CONV_DIM, CONV_COLS):
            cols = slice(c0, c0 + CONV_COLS)
            taps = [w_ref[k:k + 1, cols] for k in range(CONV_K)]
            bias = b_ref[:, cols]
            acc_b = jnp.zeros((HALO, CONV_COLS), F32)
            acc_w = [jnp.zeros((HALO, CONV_COLS), F32) for _ in range(CONV_K)]
            for r0 in range(0, tm, CONV_ROWS_BLK):
                rows = slice(r0, r0 + CONV_ROWS_BLK)
                us = [ext_ref[pl.ds(HALO - (CONV_K - 1) + k + r0, CONV_ROWS_BLK), cols] for k in range(CONV_K)]
                y = bias + taps[0] * us[0]
                for k in range(1, CONV_K):
                    y += taps[k] * us[k]
                s = _sigmoid(y)
                if c0 < SSD_INNER:
                    dact = dys_ref[rows, cols] * dv_ref[:, cols] + dxb_ref[rows, cols]
                elif c0 < SSD_INNER + 512:
                    dact = dB_ref[rows, c0 - SSD_INNER:c0 - SSD_INNER + CONV_COLS]
                else:
                    dact = dC_ref[rows, c0 - SSD_INNER - 512:c0 - SSD_INNER - 512 + CONV_COLS]
                dy = dact * (s * (1.0 + y * (1.0 - s)))
                dye_ref[rows, cols] = dy
                acc_b += jnp.sum(dy.reshape(CONV_ROWS_BLK // HALO, HALO, CONV_COLS), axis=0)
                for k in range(CONV_K):
                    acc_w[k] += jnp.sum((dy * us[k]).reshape(CONV_ROWS_BLK // HALO, HALO, CONV_COLS), axis=0)
            dcb_ref[:, cols] += jnp.sum(acc_b, axis=0, keepdims=True)
            for k in range(CONV_K):
                dw_ref[k:k + 1, cols] += jnp.sum(acc_w[k], axis=0, keepdims=True)
        for c0 in range(0, CONV_DIM, CONV_COLS):
            cols = slice(c0, c0 + CONV_COLS)
            taps = [w_ref[k:k + 1, cols] for k in range(CONV_K)]
            for r0 in range(0, tm, CONV_ROWS_BLK):
                du = taps[0] * dye_ref[pl.ds(CONV_K - 1 + r0, CONV_ROWS_BLK), cols]
                for k in range(1, CONV_K):
                    du += taps[k] * dye_ref[pl.ds(CONV_K - 1 - k + r0, CONV_ROWS_BLK), cols]
                du_ref[r0:r0 + CONV_ROWS_BLK, cols] = du.astype(BF16)
        dye_ref[tm:, :] = dye_ref[0:HALO, :]
        sg = _sigmoid(dtr_ref[...] + db_ref[...])
        ddtr = ddt_ref[...] * sg
        ddtr_ref[...] = ddtr.astype(BF16)
        ddb_ref[...] += jnp.sum(ddtr, axis=0, keepdims=True)

    prev_spec = pl.BlockSpec((HALO, CONV_DIM), lambda i: (jnp.maximum((n - 1 - i) * hb - 1, 0), 0))
    return _pc(
        body, name="conv_bwd", grid=(n,),
        in_specs=[rev(CONV_DIM), prev_spec, rev(DT_PAD), rev(1024), _const_spec((1, SSD_INNER)), rev(1024), rev(512), rev(512),
                  rev(DT_PAD), _const_spec((CONV_K, CONV_DIM)), _const_spec((1, CONV_DIM)), _const_spec((1, DT_PAD))],
        out_specs=[rev(CONV_DIM), rev(DT_PAD), _const_spec((HALO, CONV_DIM)), _const_spec((1, CONV_DIM)),
                   _const_spec((1, DT_PAD))],
        out_shape=[jax.ShapeDtypeStruct((T, CONV_DIM), BF16), jax.ShapeDtypeStruct((T, DT_PAD), BF16),
                   jax.ShapeDtypeStruct((HALO, CONV_DIM), F32), jax.ShapeDtypeStruct((1, CONV_DIM), F32),
                   jax.ShapeDtypeStruct((1, DT_PAD), F32)],
        scratch_shapes=[pltpu.VMEM((tm + HALO, CONV_DIM), F32), pltpu.VMEM((tm + HALO, CONV_DIM), F32)],
        compiler_params=_cparams(("arbitrary",)),
    )(xbc_raw, xbc_raw, dt_raw, dys, dvec, dxs_b, dB, dC, ddt, conv_w, conv_b, dt_bias)


GROUP_LANES = SSD_HPG * SSD_HEADDIM


def _ssd_expanders():
    head = jnp.arange(DT_PAD)[:, None]
    to_wide = (jnp.arange(SSD_INNER)[None, :] // SSD_HEADDIM == head).astype(BF16)
    return to_wide, to_wide.T


def _ssd_prep(dt_ref, alog_ref, wide_ref):
    q = SSD_CHUNK
    a = -jnp.exp(alog_ref[...])
    dtv = dt_ref[...]
    la = dtv * a
    row = lax.broadcasted_iota(jnp.int32, (q, q), 0)
    col = lax.broadcasted_iota(jnp.int32, (q, q), 1)
    tri = (col <= row).astype(BF16)
    cum = _dot_split_rhs(tri, la, 3)
    cum_t = _dot_split(la, tri, 3, (((0,), (1,)), ((), ())))
    dtw = _dot_split(dtv, wide_ref[...], 2)
    cumw = _dot_split(cum, wide_ref[...], 3)
    return a, dtv, row, col, tri, cum_t, dtw, cumw, cum


def _decay(cum, cum_t, h, keep):
    return jnp.where(keep, jnp.exp(jnp.minimum(cum[:, h:h + 1] - cum_t[h:h + 1, :], 0.0)), 0.0)


def _decay_t(cum, cum_t, h, keep_t):
    return jnp.where(keep_t, jnp.exp(jnp.minimum(cum_t[h:h + 1, :] - cum[:, h:h + 1], 0.0)), 0.0)


def _ssd_fwd(xbc_act, dt, alog):
    T = xbc_act.shape[0]
    q = SSD_CHUNK
    nc = T // q
    to_wide, _ = _ssd_expanders()

    def body(xbc_ref, dt_ref, alog_ref, wide_ref, y_ref, sp_ref, st_ref, xd_ref, xde_ref):
        @pl.when(pl.program_id(0) == 0)
        def _():
            st_ref[...] = jnp.zeros_like(st_ref)

        a, dtv, row, col, tri, cum_t, dtw, cumw, segcol = _ssd_prep(dt_ref, alog_ref, wide_ref)
        clw = cumw[q - 1:q, :]
        ecw = jnp.exp(cumw)
        xd = xbc_ref[:, 0:SSD_INNER] * dtw
        xd_ref[...] = xd.astype(BF16)
        xde_ref[...] = (xd * jnp.exp(clw - cumw)).astype(BF16)
        cdw = jnp.exp(clw)
        keep = col <= row
        sp_ref[0] = st_ref[...]
        for g in range(SSD_GROUPS):
            gl = slice(GROUP_LANES * g, GROUP_LANES * (g + 1))
            bb = xbc_ref[:, 1024 + 128 * g:1152 + 128 * g].astype(BF16)
            cb = xbc_ref[:, 1536 + 128 * g:1664 + 128 * g].astype(BF16)
            gm = _dot_nt(cb, bb)
            stp = st_ref[g]
            yoff = _dot(cb, stp.astype(BF16)) * ecw[:, gl]
            for r in range(SSD_HPG):
                h = SSD_HPG * g + r
                m = (gm * _decay(segcol, cum_t, h, keep)).astype(BF16)
                y_ref[:, 64 * h:64 * h + 64] = _dot(m, xd_ref[:, 64 * h:64 * h + 64]) + yoff[:, 64 * r:64 * r + 64]
            st_ref[g] = stp * cdw[:, gl] + _dot_tn(bb, xde_ref[:, gl])

    return _pc(
        body, name="ssd_fwd", grid=(nc,),
        in_specs=[_row_spec(q, CONV_DIM), _row_spec(q, DT_PAD), _const_spec((1, DT_PAD)),
                  _const_spec(to_wide.shape)],
        out_specs=[_row_spec(q, SSD_INNER),
                   pl.BlockSpec((1, SSD_GROUPS, SSD_STATE, GROUP_LANES), lambda i: (i, 0, 0, 0))],
        out_shape=[jax.ShapeDtypeStruct((T, SSD_INNER), F32),
                   jax.ShapeDtypeStruct((nc, SSD_GROUPS, SSD_STATE, GROUP_LANES), F32)],
        scratch_shapes=[pltpu.VMEM((SSD_GROUPS, SSD_STATE, GROUP_LANES), F32), pltpu.VMEM((q, SSD_INNER), BF16),
                        pltpu.VMEM((q, SSD_INNER), BF16)],
        compiler_params=_cparams(("arbitrary",)),
    )(xbc_act, dt, alog, to_wide)


def _ssd_bwd(xbc_act, dt, alog, sprev, dy):
    T = xbc_act.shape[0]
    q = SSD_CHUNK
    nc = T // q
    to_wide, to_heads = _ssd_expanders()

    def rev(width):
        return pl.BlockSpec((q, width), lambda i: (nc - 1 - i, 0))

    def body(xbc_ref, dt_ref, alog_ref, sp_ref, dy_ref, wide_ref, heads_ref,
             dxs_ref, dB_ref, dC_ref, ddt_ref, dalog_ref, ds_ref, xd_ref, dxd_ref):
        i = pl.program_id(0)

        @pl.when(i == 0)
        def _():
            ds_ref[...] = jnp.zeros_like(ds_ref)
            dalog_ref[...] = jnp.zeros_like(dalog_ref)

        a, dtv, row, col, tri, cum_t, dtw, cumw, segcol = _ssd_prep(dt_ref, alog_ref, wide_ref)
        clw = cumw[q - 1:q, :]
        ecw = jnp.exp(cumw)
        dew = jnp.exp(clw - cumw)
        cdw = jnp.exp(clw)
        xs = xbc_ref[:, 0:SSD_INNER]
        xd = xs * dtw
        xd_ref[...] = xd.astype(BF16)
        dyv = dy_ref[...]
        dye = (dyv * ecw).astype(BF16)
        xde = (xd * dew).astype(BF16)
        keep = col <= row
        keep_t = col >= row
        rows_k = lax.broadcasted_iota(jnp.int32, (SSD_HPG * q, DT_PAD), 0) // q
        lanes_k = lax.broadcasted_iota(jnp.int32, (SSD_HPG * q, DT_PAD), 1)
        dcw_parts = []
        dcum = jnp.zeros((q, DT_PAD), F32)
        for g in range(SSD_GROUPS):
            gl = slice(GROUP_LANES * g, GROUP_LANES * (g + 1))
            bb = xbc_ref[:, 1024 + 128 * g:1152 + 128 * g].astype(BF16)
            cb = xbc_ref[:, 1536 + 128 * g:1664 + 128 * g].astype(BF16)
            gm = _dot_nt(cb, bb)
            gmt = _dot_nt(bb, cb)
            stp = sp_ref[0, g]
            dst = ds_ref[g]
            stpb = stp.astype(BF16)
            dstb = dst.astype(BF16)
            yoff = _dot(cb, stpb) * ecw[:, gl]
            dcg = _dot_nt(dye[:, gl], stpb)
            ds_ref[g] = dst * cdw[:, gl] + _dot_tn(cb, dye[:, gl])
            dlast = jnp.sum(dst * stp, axis=0, keepdims=True) * cdw[:, gl]
            dbg = _dot_nt(xde[:, gl], dstb)
            w = _dot(bb, dstb) * dew[:, gl]
            wx = w * xd[:, gl]
            dlast = dlast + jnp.sum(wx, axis=0, keepdims=True)
            dcw_parts.append(dyv[:, gl] * yoff - wx
                             + jnp.where(lax.broadcasted_iota(jnp.int32, (q, 1), 0) == q - 1, dlast, 0.0))
            dgm = jnp.zeros((q, q), F32)
            diag = []
            for r in range(SSD_HPG):
                h = SSD_HPG * g + r
                hl = slice(64 * h, 64 * h + 64)
                dyb = dy_ref[:, hl].astype(BF16)
                xdh = xd_ref[:, hl]
                dm = _dot_nt(dyb, xdh)
                dmt = _dot_nt(xdh, dyb)
                dec = _decay(segcol, cum_t, h, keep)
                mt = gmt * _decay_t(segcol, cum_t, h, keep_t)
                dgm += dm * dec
                diag.append(dm * (gm * dec) - dmt * mt)
                dxd_ref[:, hl] = _dot(mt.astype(BF16), dyb) + w[:, 64 * r:64 * r + 64]
            onehots = (lanes_k == SSD_HPG * g + rows_k).astype(BF16)
            dcum += _dot_split(jnp.concatenate(diag, axis=1), onehots, 2)
            dgb = dgm.astype(BF16)
            dC_ref[:, 128 * g:128 * g + 128] = dcg + _dot(dgb, bb)
            dB_ref[:, 128 * g:128 * g + 128] = dbg + _dot_tn(dgb, cb)
        dxd = dxd_ref[...]
        dxs_ref[...] = dxd * dtw
        dcum += _dot_split(jnp.concatenate(dcw_parts, axis=1), heads_ref[...], 2)
        dla = _dot_split_rhs(tri, dcum, 3, (((0,), (0,)), ((), ())))
        ddt_ref[...] = _dot_split(xs * dxd, heads_ref[...], 2) + dla * a
        dalog_ref[...] += jnp.sum(dla * dtv, axis=0, keepdims=True)

        @pl.when(i == nc - 1)
        def _():
            dalog_ref[...] = dalog_ref[...] * a

    st_spec = pl.BlockSpec((1, SSD_GROUPS, SSD_STATE, GROUP_LANES), lambda i: (nc - 1 - i, 0, 0, 0))
    return _pc(
        body, name="ssd_bwd", grid=(nc,),
        in_specs=[rev(CONV_DIM), rev(DT_PAD), _const_spec((1, DT_PAD)), st_spec, rev(SSD_INNER),
                  _const_spec(to_wide.shape), _const_spec(to_heads.shape)],
        out_specs=[rev(SSD_INNER), rev(512), rev(512), rev(DT_PAD), _const_spec((1, DT_PAD))],
        out_shape=[jax.ShapeDtypeStruct((T, SSD_INNER), F32), jax.ShapeDtypeStruct((T, 512), F32),
                   jax.ShapeDtypeStruct((T, 512), F32), jax.ShapeDtypeStruct((T, DT_PAD), F32),
                   jax.ShapeDtypeStruct((1, DT_PAD), F32)],
        scratch_shapes=[pltpu.VMEM((SSD_GROUPS, SSD_STATE, GROUP_LANES), F32), pltpu.VMEM((q, SSD_INNER), BF16),
                        pltpu.VMEM((q, SSD_INNER), F32)],
        compiler_params=_cparams(("arbitrary",)),
    )(xbc_act, dt, alog, sprev, dy, to_wide, to_heads)


def _s5_disc_vals(a_re, a_im, log_dt, b_re, b_im):
    dt = jnp.exp(log_dt)
    mag = jnp.exp(a_re * dt)
    ab_re = mag * jnp.cos(a_im * dt)
    ab_im = mag * jnp.sin(a_im * dt)
    den = a_re * a_re + a_im * a_im
    nr = ab_re - 1.0
    ni = ab_im
    coef_re = (nr * a_re + ni * a_im) / den
    coef_im = (ni * a_re - nr * a_im) / den
    bb_re = coef_re * b_re - coef_im * b_im
    bb_im = coef_re * b_im + coef_im * b_re
    return ab_re, ab_im, bb_re, bb_im


def _s5_disc(a_re, a_im, log_dt, b_re, b_im):
    def body(ar, ai, ld, br, bi, o1, o2, o3, o4):
        o1[...], o2[...], o3[...], o4[...] = _s5_disc_vals(ar[...], ai[...], ld[...], br[...], bi[...])

    return _pc(
        body, name="s5_disc",
        out_shape=[jax.ShapeDtypeStruct((1, S5_STATES), F32), jax.ShapeDtypeStruct((1, S5_STATES), F32),
                   jax.ShapeDtypeStruct((16, S5_STATES), F32), jax.ShapeDtypeStruct((16, S5_STATES), F32)],
    )(a_re, a_im, log_dt, b_re, b_im)


def _s5_disc_bwd(a_re, a_im, log_dt, b_re, b_im, d_ab_re, d_ab_im, d_bb_re, d_bb_im):
    def body(ar, ai, ld, br, bi, g1, g2, g3, g4, o1, o2, o3, o4, o5):
        _, vjp = jax.vjp(_s5_disc_vals, ar[...], ai[...], ld[...], br[...], bi[...])
        d1, d2, d3, d4, d5 = vjp((g1[...], g2[...], g3[...], g4[...]))
        o1[...] = d1
        o2[...] = d2
        st = lax.broadcasted_iota(jnp.int32, (S5_STATES, DT_PAD), 0)
        grp = lax.broadcasted_iota(jnp.int32, (S5_STATES, DT_PAD), 1)
        sel = (st // 64 == grp).astype(F32)
        o3[...] = _dot_hi(d3, sel)
        o4[...] = d4
        o5[...] = d5

    return _pc(
        body, name="s5_disc_bwd",
        out_shape=[jax.ShapeDtypeStruct((1, S5_STATES), F32), jax.ShapeDtypeStruct((1, S5_STATES), F32),
                   jax.ShapeDtypeStruct((1, DT_PAD), F32),
                   jax.ShapeDtypeStruct((16, S5_STATES), F32), jax.ShapeDtypeStruct((16, S5_STATES), F32)],
    )(a_re, a_im, log_dt, b_re, b_im, d_ab_re, d_ab_im, d_bb_re, d_bb_im)


def _cmul_add(xr, xi, pr, pi, yr, yi):
    return xr + pr * yr - pi * yi, xi + pr * yi + pi * yr


def _powers(ar, ai, n):
    out = [(ar, ai)]
    for _ in range(n - 1):
        pr, pi = out[-1]
        out.append((pr * pr - pi * pi, 2.0 * pr * pi))
    return out


_BW = S5_STATES // S5_BLOCKS
_BI = S5_WIDTH // S5_BLOCKS
SUB = 8
S5_ROWS = S5_CHUNK // SUB


S5_TAB_ROWS = 8 * SUB


def _scan8(br, bi, tab_ref, reverse):
    for level, k in enumerate((1, 2, 4)):
        r0 = 2 * SUB * (level + 1)
        shift = SUB - k if reverse else k
        br, bi = _cmul_add(br, bi, tab_ref[r0:r0 + SUB, :], tab_ref[r0 + SUB:r0 + 2 * SUB, :],
                           pltpu.roll(br, shift, 0), pltpu.roll(bi, shift, 0))
    return br, bi


def _s5_tables(ab_ref, tab_ref, reverse):
    rowin = lax.broadcasted_iota(jnp.int32, (SUB, 1), 0)
    ar = ab_ref[0:1, :]
    ai = -ab_ref[1:2, :] if reverse else ab_ref[1:2, :]
    zero = jnp.zeros((SUB, S5_STATES), F32)
    for level, (pr, pi) in enumerate(_powers(ar, ai, 3)):
        k = 2 ** level
        keep = (rowin < SUB - k) if reverse else (rowin >= k)
        r0 = 2 * SUB * (level + 1)
        tab_ref[r0:r0 + SUB, :] = jnp.where(keep, pr, 0.0) + zero
        tab_ref[r0 + SUB:r0 + 2 * SUB, :] = jnp.where(keep, pi, 0.0) + zero
    hit = rowin == (SUB - 1 if reverse else 0)
    pr, pi = _scan8(jnp.where(hit, ar, 0.0) + zero, jnp.where(hit, ai, 0.0) + zero, tab_ref, reverse)
    tab_ref[0:SUB, :] = pr
    tab_ref[SUB:2 * SUB, :] = pi


def _s5_fwd_chunk(u_ref, y_ref, r0, wb_ref, wc_ref, d_ref, carry_ref, tab_ref, sr_ref, si_ref, between):
    q = S5_CHUNK
    rows = slice(r0, r0 + q)
    for j in range(S5_BLOCKS):
        bu = _dot(u_ref[rows, _BI * j:_BI * (j + 1)].astype(BF16), wb_ref[j])
        sr_ref[:, :, _BW * j:_BW * (j + 1)] = bu[:, :_BW].reshape(S5_ROWS, SUB, _BW)
        si_ref[:, :, _BW * j:_BW * (j + 1)] = bu[:, _BW:].reshape(S5_ROWS, SUB, _BW)
    tr, ti = tab_ref[0:SUB, :], tab_ref[SUB:2 * SUB, :]
    cr, ci = carry_ref[0:1, :], carry_ref[1:2, :]
    for k in range(S5_ROWS):
        sr, si = _scan8(sr_ref[k], si_ref[k], tab_ref, False)
        sr, si = _cmul_add(sr, si, tr, ti, cr, ci)
        sr_ref[k] = sr
        si_ref[k] = si
        cr, ci = sr[SUB - 1:SUB, :], si[SUB - 1:SUB, :]
        between()
    carry_ref[0:1, :] = cr
    carry_ref[1:2, :] = ci
    for j in range(S5_BLOCKS):
        sl = slice(_BW * j, _BW * (j + 1))
        ul = slice(_BI * j, _BI * (j + 1))
        s = jnp.concatenate([sr_ref[:, :, sl].reshape(q, _BW), si_ref[:, :, sl].reshape(q, _BW)], axis=1).astype(BF16)
        y_ref[rows, ul] = _dot(s, wc_ref[j]) + d_ref[:, ul] * u_ref[rows, ul]


def _inproj_s5_fwd(x, g, wp, wb4, wc4, ab, dvec, rider=None):
    T = x.shape[0]
    tm = TOKEN_TILE
    per_tile = tm // S5_CHUNK
    nc = T // S5_CHUNK

    def body(x_ref, g_ref, w_hbm, wb_ref, wc_ref, ab_ref, d_ref, z_ref, xbc_ref, u5_ref, gt_ref, dt_ref, h_ref, y_ref, sp_ref,
             w_ref, carry_ref, tab_ref, sr_ref, si_ref):
        @pl.when(pl.program_id(0) == 0)
        def _():
            pltpu.sync_copy(w_hbm, w_ref)
            carry_ref[...] = jnp.zeros_like(carry_ref)
            _s5_tables(ab_ref, tab_ref, False)

        xn, _ = _rms(x_ref[...])
        h = (xn * g_ref[...]).astype(BF16)
        h_ref[...] = h
        u5_ref[...] = _dot(h, w_ref[:, P_U5:P_G])
        pieces = [(z_ref, P_Z, c0) for c0 in range(0, P_XBC - P_Z, INPROJ_PIECE)]
        pieces += [(xbc_ref, P_XBC, c0) for c0 in range(0, P_U5 - P_XBC, INPROJ_PIECE)]
        pieces += [(gt_ref, P_G, c0) for c0 in range(0, P_DT - P_G, INPROJ_PIECE)]
        todo = iter(pieces)
        slabs, calls = per_tile * S5_ROWS, [0]

        def between():
            calls[0] += 1
            if (calls[0] * len(pieces)) // slabs > ((calls[0] - 1) * len(pieces)) // slabs:
                o_ref, base, c0 = next(todo)
                o_ref[:, c0:c0 + INPROJ_PIECE] = _dot(h, w_ref[:, base + c0:base + c0 + INPROJ_PIECE])

        dt_ref[...] = _dot(h, w_ref[:, P_DT:P_END])
        for c in range(per_tile):
            sp_ref[c] = carry_ref[...]
            _s5_fwd_chunk(u5_ref, y_ref, S5_CHUNK * c, wb_ref, wc_ref, d_ref, carry_ref, tab_ref, sr_ref, si_ref, between)
        assert next(todo, None) is None

    widths = (1024, 2048, 512, 2048, DT_PAD)
    return _call(
        body, rider, name="inproj_s5_fwd", grid=(T // tm,),
        in_specs=[_row_spec(tm, D_MODEL), _const_spec((1, D_MODEL)), _hbm_spec(),
                  _const_spec((S5_BLOCKS, _BI, 2 * _BW)), _const_spec((S5_BLOCKS, 2 * _BW, _BI)),
                  _const_spec((8, S5_STATES)), _const_spec((1, S5_WIDTH))],
        out_specs=[_row_spec(tm, w) for w in widths] + [_row_spec(tm, D_MODEL), _row_spec(tm, S5_WIDTH),
                                                         pl.BlockSpec((per_tile, 8, S5_STATES), lambda i: (i, 0, 0))],
        out_shape=[jax.ShapeDtypeStruct((T, w), F32) for w in widths] + [
            jax.ShapeDtypeStruct((T, D_MODEL), BF16), jax.ShapeDtypeStruct((T, S5_WIDTH), F32),
            jax.ShapeDtypeStruct((nc, 8, S5_STATES), F32)],
        scratch_shapes=[pltpu.VMEM((D_MODEL, P_END), BF16), pltpu.VMEM((8, S5_STATES), F32),
                        pltpu.VMEM((S5_TAB_ROWS, S5_STATES), F32), pltpu.VMEM((S5_ROWS, SUB, S5_STATES), F32),
                        pltpu.VMEM((S5_ROWS, SUB, S5_STATES), F32)],
        compiler_params=_cparams(("arbitrary",)),
    )(x, g, wp, wb4, wc4, ab, dvec)


def _s5_bwd(u5, dy5, wb4, wc4, ab, dvec, sprev, rider=None):
    T = u5.shape[0]
    q = S5_CHUNK
    nc = T // q

    def rev(width):
        return pl.BlockSpec((q, width), lambda i: (nc - 1 - i, 0))

    def body(u_ref, dy_ref, wb_ref, wc_ref, ab_ref, d_ref, sp_ref, du_ref, dwb_ref, dwc_ref, dab_ref, dd_ref,
             carry_ref, tab_ref, rtab_ref, sr_ref, si_ref, lr_ref, li_ref):
        i = pl.program_id(0)
        rowin = lax.broadcasted_iota(jnp.int32, (SUB, 1), 0)

        @pl.when(i == 0)
        def _():
            carry_ref[...] = jnp.zeros_like(carry_ref)
            dwb_ref[...] = jnp.zeros_like(dwb_ref)
            dwc_ref[...] = jnp.zeros_like(dwc_ref)
            dab_ref[...] = jnp.zeros_like(dab_ref)
            dd_ref[...] = jnp.zeros_like(dd_ref)
            _s5_tables(ab_ref, tab_ref, False)
            _s5_tables(ab_ref, rtab_ref, True)

        for j in range(S5_BLOCKS):
            sl = slice(_BW * j, _BW * (j + 1))
            ul = slice(_BI * j, _BI * (j + 1))
            bu = _dot(u_ref[:, ul].astype(BF16), wb_ref[j])
            sr_ref[:, :, sl] = bu[:, :_BW].reshape(S5_ROWS, SUB, _BW)
            si_ref[:, :, sl] = bu[:, _BW:].reshape(S5_ROWS, SUB, _BW)
            ds = _dot_nt(dy_ref[:, ul].astype(BF16), wc_ref[j])
            lr_ref[:, :, sl] = ds[:, :_BW].reshape(S5_ROWS, SUB, _BW)
            li_ref[:, :, sl] = ds[:, _BW:].reshape(S5_ROWS, SUB, _BW)
        ar, ai = ab_ref[0:1, :], ab_ref[1:2, :]
        tr, ti = tab_ref[0:SUB, :], tab_ref[SUB:2 * SUB, :]
        cr, ci = sp_ref[0, 0:1, :], sp_ref[0, 1:2, :]
        for k in range(S5_ROWS):
            sr, si = _scan8(sr_ref[k], si_ref[k], tab_ref, False)
            sr, si = _cmul_add(sr, si, tr, ti, cr, ci)
            sr_ref[k] = sr
            si_ref[k] = si
            cr, ci = sr[SUB - 1:SUB, :], si[SUB - 1:SUB, :]
        tr, ti = rtab_ref[0:SUB, :], rtab_ref[SUB:2 * SUB, :]
        cr, ci = carry_ref[0:1, :], carry_ref[1:2, :]
        acc_r = jnp.zeros((SUB, S5_STATES), F32)
        acc_i = jnp.zeros((SUB, S5_STATES), F32)
        for k in reversed(range(S5_ROWS)):
            lr, li = _scan8(lr_ref[k], li_ref[k], rtab_ref, True)
            lr, li = _cmul_add(lr, li, tr, ti, cr, ci)
            lr_ref[k] = lr
            li_ref[k] = li
            cr, ci = lr[0:1, :], li[0:1, :]
            if k > 0:
                before_r, before_i = sr_ref[k - 1, SUB - 1:SUB, :], si_ref[k - 1, SUB - 1:SUB, :]
            else:
                before_r, before_i = sp_ref[0, 0:1, :], sp_ref[0, 1:2, :]
            keep = rowin >= 1
            pr = jnp.where(keep, pltpu.roll(sr_ref[k], 1, 0), before_r)
            pi = jnp.where(keep, pltpu.roll(si_ref[k], 1, 0), before_i)
            acc_r += lr * pr + li * pi
            acc_i += li * pr - lr * pi
        carry_ref[0:1, :] = cr
        carry_ref[1:2, :] = ci
        dab_ref[0:1, :] += jnp.sum(acc_r, axis=0, keepdims=True)
        dab_ref[1:2, :] += jnp.sum(acc_i, axis=0, keepdims=True)
        for j in range(S5_BLOCKS):
            sl = slice(_BW * j, _BW * (j + 1))
            ul = slice(_BI * j, _BI * (j + 1))
            u = u_ref[:, ul]
            dy = dy_ref[:, ul]
            dyb = dy.astype(BF16)
            lam = jnp.concatenate([lr_ref[:, :, sl].reshape(q, _BW), li_ref[:, :, sl].reshape(q, _BW)], axis=1).astype(BF16)
            s = jnp.concatenate([sr_ref[:, :, sl].reshape(q, _BW), si_ref[:, :, sl].reshape(q, _BW)], axis=1).astype(BF16)
            du_ref[:, ul] = (_dot_nt(lam, wb_ref[j]) + d_ref[:, ul] * dy).astype(BF16)
            dwb_ref[j] += _dot_tn(u.astype(BF16), lam)
            dwc_ref[j] += _dot_tn(s, dyb)
            dd_ref[:, ul] += jnp.sum(dy * u, axis=0, keepdims=True)

    big = pltpu.VMEM((S5_ROWS, SUB, S5_STATES), F32)
    return _call(
        body, rider, name="s5_bwd", grid=(nc,),
        in_specs=[rev(S5_WIDTH), rev(S5_WIDTH), _const_spec((S5_BLOCKS, _BI, 2 * _BW)), _const_spec((S5_BLOCKS, 2 * _BW, _BI)),
                  _const_spec((8, S5_STATES)), _const_spec((1, S5_WIDTH)),
                  pl.BlockSpec((1, 8, S5_STATES), lambda i: (nc - 1 - i, 0, 0))],
        out_specs=[rev(S5_WIDTH), _const_spec((S5_BLOCKS, _BI, 2 * _BW)), _const_spec((S5_BLOCKS, 2 * _BW, _BI)),
                   _const_spec((8, S5_STATES)), _const_spec((1, S5_WIDTH))],
        out_shape=[jax.ShapeDtypeStruct((T, S5_WIDTH), BF16), jax.ShapeDtypeStruct((S5_BLOCKS, _BI, 2 * _BW), F32),
                   jax.ShapeDtypeStruct((S5_BLOCKS, 2 * _BW, _BI), F32), jax.ShapeDtypeStruct((8, S5_STATES), F32),
                   jax.ShapeDtypeStruct((1, S5_WIDTH), F32)],
        scratch_shapes=[pltpu.VMEM((8, S5_STATES), F32), pltpu.VMEM((S5_TAB_ROWS, S5_STATES), F32),
                        pltpu.VMEM((S5_TAB_ROWS, S5_STATES), F32), big, big, big, big],
        compiler_params=_cparams(("arbitrary",)),
    )(u5, dy5, wb4, wc4, ab, dvec, sprev)


def _merge_vals(ys, xs, z, y5, gates, dvec, gssd, glu_w, glu_b, wbr):
    sz = _sigmoid(z)
    qv = ys + dvec * xs
    pre = qv * (z * sz)
    yn, rs = [], []
    for gi in range(SSD_GROUPS):
        p, r = _rms(pre[:, 256 * gi:256 * (gi + 1)])
        yn.append(p)
        rs.append(r)
    yn = jnp.concatenate(yn, axis=1)
    ya = yn * gssd
    pa = _dot(ya.astype(BF16), wbr[0:SSD_INNER, :])
    gel = _gelu(y5)
    pre_g = _dot(gel.astype(BF16), glu_w)
    s0 = _sigmoid(gates[:, :D_MODEL])
    s1 = _sigmoid(gates[:, D_MODEL:])
    sg = _sigmoid(pre_g + glu_b)
    yb = gel * sg
    pb = _dot(yb.astype(BF16), wbr[SSD_INNER:, :])
    merged = s0 * pa + s1 * pb
    return dict(sz=sz, qv=qv, yn=yn, rs=rs, ya=ya, gel=gel, sg=sg, yb=yb, pa=pa, pb=pb, s0=s0, s1=s1, merged=merged)


def _merge_specs(tm):
    acts = [_row_spec(tm, 1024), _row_spec(tm, 1024, 0), _row_spec(tm, 1024), _row_spec(tm, 512), _row_spec(tm, 2048),
            _row_spec(tm, 1024)]
    params = [_const_spec((1, 1024)), _const_spec((1, 1024)), _const_spec((512, 512)), _const_spec((1, 512)), _hbm_spec()]
    return acts, params


def _merge_fwd(ys, xbc_act, z, y5, gates, x, dvec, gssd, glu_w, glu_b, wa):
    T = x.shape[0]
    tm = TOKEN_TILE
    acts, params = _merge_specs(tm)

    def body(ys_ref, xs_ref, z_ref, y5_ref, gt_ref, x_ref, dv_ref, gs_ref, gw_ref, gb_ref, wa_hbm, x1_ref,
             wbr_ref, wout_ref):
        @pl.when(pl.program_id(0) == 0)
        def _():
            _load_late_weight(wa_hbm, wbr_ref, "w_branch")
            _load_late_weight(wa_hbm, wout_ref, "w_out")

        v = _merge_vals(ys_ref[...], xs_ref[...], z_ref[...], y5_ref[...], gt_ref[...], dv_ref[...], gs_ref[...],
                        gw_ref[...], gb_ref[...], wbr_ref)
        x1_ref[...] = x_ref[...] + _dot(v["merged"].astype(BF16), wout_ref[...])

    return _pc(
        body, name="merge_fwd", grid=(T // tm,),
        in_specs=acts + params, out_specs=_row_spec(tm, 1024),
        out_shape=jax.ShapeDtypeStruct((T, 1024), F32),
        scratch_shapes=[pltpu.VMEM((1536, 1024), BF16), pltpu.VMEM((1024, 1024), BF16)],
        compiler_params=_cparams(("arbitrary",)),
    )(ys, xbc_act, z, y5, gates, x, dvec, gssd, glu_w, glu_b, wa)


def _merge_bwd(ys, xbc_act, z, y5, gates, dx1, dvec, gssd, glu_w, glu_b, wa, head_sel, rider=None):
    T = dx1.shape[0]
    tm = TOKEN_TILE
    acts, params = _merge_specs(tm)

    def body(ys_ref, xs_ref, z_ref, y5_ref, gt_ref, dx1_ref, dv_ref, gs_ref, gw_ref, gb_ref, wa_hbm, hs_ref,
             dys_ref, dz_ref, dy5_ref, dgt_ref, mg_ref, ya_ref, yb_ref, dpa_ref, dpb_ref, gel_ref, dpre_ref,
             ddv_ref, dgs_ref, dgb_ref, wbr_ref, wout_ref, ddacc_ref):
        i = pl.program_id(0)

        @pl.when(i == 0)
        def _():
            _load_late_weight(wa_hbm, wbr_ref, "w_branch")
            _load_late_weight(wa_hbm, wout_ref, "w_out")
            ddacc_ref[...] = jnp.zeros_like(ddacc_ref)
            dgs_ref[...] = jnp.zeros_like(dgs_ref)
            dgb_ref[...] = jnp.zeros_like(dgb_ref)

        ys, xs, z, y5, gates = ys_ref[...], xs_ref[...], z_ref[...], y5_ref[...], gt_ref[...]
        dvv, gsv, gw = dv_ref[...], gs_ref[...], gw_ref[...]
        v = _merge_vals(ys, xs, z, y5, gates, dvv, gsv, gw, gb_ref[...], wbr_ref)
        dmg = _dot_nt(dx1_ref[...].astype(BF16), wout_ref[...])
        s0, s1, pa, pb = v["s0"], v["s1"], v["pa"], v["pb"]
        dgt_ref[:, :D_MODEL] = (dmg * pa * s0 * (1.0 - s0)).astype(BF16)
        dgt_ref[:, D_MODEL:] = (dmg * pb * s1 * (1.0 - s1)).astype(BF16)
        dpa = (dmg * s0).astype(BF16)
        dpb = (dmg * s1).astype(BF16)
        dya = _dot_nt(dpa, wbr_ref[0:SSD_INNER, :])
        dyb = _dot_nt(dpb, wbr_ref[SSD_INNER:, :])
        gel, sg = v["gel"], v["sg"]
        dpre = (dyb * gel * sg * (1.0 - sg))
        dgb_ref[...] += jnp.sum(dpre, axis=0, keepdims=True)
        dpre_b = dpre.astype(BF16)
        dgel = dyb * sg + _dot_nt(dpre_b, gw)
        dy5_ref[...] = dgel * _gelu_grad(y5)
        yn = v["yn"]
        dgs_ref[...] += jnp.sum(dya * yn, axis=0, keepdims=True)
        dyn = dya * gsv
        dpre_a = jnp.concatenate(
            [_rms_bwd(yn[:, 256 * gi:256 * (gi + 1)], v["rs"][gi], dyn[:, 256 * gi:256 * (gi + 1)])
             for gi in range(SSD_GROUPS)], axis=1)
        sz, qv = v["sz"], v["qv"]
        dq = dpre_a * (z * sz)
        dz_ref[...] = (dpre_a * qv * (sz * (1.0 + z * (1.0 - sz)))).astype(BF16)
        dys_ref[...] = dq
        ddacc_ref[...] += jnp.sum(dq * xs, axis=0, keepdims=True)
        mg_ref[...] = v["merged"].astype(BF16)
        ya_ref[...] = v["ya"].astype(BF16)
        yb_ref[...] = v["yb"].astype(BF16)
        dpa_ref[...] = dpa
        dpb_ref[...] = dpb
        gel_ref[...] = gel.astype(BF16)
        dpre_ref[...] = dpre_b

        @pl.when(i == pl.num_programs(0) - 1)
        def _():
            ddv_ref[...] = _dot_hi(ddacc_ref[...], hs_ref[...])

    outs = [(1024, F32), (1024, BF16), (512, F32), (2048, BF16),
            (1024, BF16), (1024, BF16), (512, BF16), (1024, BF16), (1024, BF16), (512, BF16), (512, BF16)]
    return _call(
        body, rider, name="merge_bwd", grid=(T // tm,),
        in_specs=acts + params + [_const_spec((1024, DT_PAD))],
        out_specs=[_row_spec(tm, w) for w, _ in outs] + [_const_spec((1, DT_PAD)), _const_spec((1, 1024)), _const_spec((1, 512))],
        out_shape=[jax.ShapeDtypeStruct((T, w), d) for w, d in outs] + [
            jax.ShapeDtypeStruct((1, DT_PAD), F32), jax.ShapeDtypeStruct((1, 1024), F32), jax.ShapeDtypeStruct((1, 512), F32)],
        scratch_shapes=[pltpu.VMEM((1536, 1024), BF16), pltpu.VMEM((1024, 1024), BF16), pltpu.VMEM((1, 1024), F32)],
        compiler_params=_cparams(("arbitrary",)),
    )(ys, xbc_act, z, y5, gates, dx1, dvec, gssd, glu_w, glu_b, wa, head_sel)


def _mlp_fwd_loss(x1, target, g, g_fin, wa):
    T = x1.shape[0]
    tm = TOKEN_TILE

    def body(x_ref, t_ref, g_ref, gf_ref, wa_hbm, dx_ref, loss_ref, dg_ref, w1_ref, w2_ref):
        @pl.when(pl.program_id(0) == 0)
        def _():
            _load_late_weight(wa_hbm, w1_ref, "w_mlp_in")
            _load_late_weight(wa_hbm, w2_ref, "w_mlp_out")
            loss_ref[...] = jnp.zeros_like(loss_ref)
            dg_ref[...] = jnp.zeros_like(dg_ref)

        xv = x_ref[...]
        xn, _ = _rms(xv)
        h = (xn * g_ref[...]).astype(BF16)
        acc = xv
        for s in range(FF_SHARDS):
            rl = jnp.maximum(_dot(h, w1_ref[s]), 0.0)
            acc += _dot((rl * rl).astype(BF16), w2_ref[FF_SHARD * s:FF_SHARD * (s + 1), :])
        yn, r = _rms(acc)
        gv = gf_ref[...]
        err = yn * gv - t_ref[...]
        loss_ref[...] += jnp.sum(err * err, axis=0, keepdims=True) * (0.5 / D_MODEL)
        dy = err * (1.0 / D_MODEL)
        dg_ref[...] += jnp.sum(dy * yn, axis=0, keepdims=True)
        dx_ref[...] = _rms_bwd(yn, r, dy * gv)

    return _pc(
        body, name="mlp_fwd_loss", grid=(T // tm,),
        in_specs=[_row_spec(tm, 1024), _row_spec(tm, 1024), _const_spec((1, 1024)), _const_spec((1, 1024)), _hbm_spec()],
        out_specs=[_row_spec(tm, 1024), _const_spec((1, 1024)), _const_spec((1, 1024))],
        out_shape=[jax.ShapeDtypeStruct((T, 1024), F32), jax.ShapeDtypeStruct((1, 1024), F32),
                   jax.ShapeDtypeStruct((1, 1024), F32)],
        scratch_shapes=[pltpu.VMEM((FF_SHARDS, D_MODEL, FF_SHARD), BF16), pltpu.VMEM((D_FF, D_MODEL), BF16)],
        compiler_params=_cparams(("arbitrary",)),
    )(x1, target, g, g_fin, wa)


def _mlp_bwd(x1, dx2, g, wa):
    T = x1.shape[0]
    tm = TOKEN_TILE

    def body(x_ref, dx2_ref, g_ref, wa_hbm, dx1_ref, h_ref, act_ref, da_ref, dg_ref, w1_ref, w2_ref):
        @pl.when(pl.program_id(0) == 0)
        def _():
            _load_late_weight(wa_hbm, w1_ref, "w_mlp_in")
            _load_late_weight(wa_hbm, w2_ref, "w_mlp_out")
            dg_ref[...] = jnp.zeros_like(dg_ref)

        xn, r = _rms(x_ref[...])
        gv = g_ref[...]
        h = (xn * gv).astype(BF16)
        h_ref[...] = h
        dx2 = dx2_ref[...]
        dx2b = dx2.astype(BF16)
        dh = jnp.zeros((tm, D_MODEL), F32)
        for s in range(FF_SHARDS):
            ff = slice(FF_SHARD * s, FF_SHARD * (s + 1))
            rl = jnp.maximum(_dot(h, w1_ref[s]), 0.0)
            act_ref[:, ff] = (rl * rl).astype(BF16)
            da = (_dot_nt(dx2b, w2_ref[ff, :]) * (2.0 * rl)).astype(BF16)
            da_ref[:, ff] = da
            dh += _dot_nt(da, w1_ref[s])
        dg_ref[...] += jnp.sum(dh * xn, axis=0, keepdims=True)
        dx1_ref[...] = dx2 + _rms_bwd(xn, r, dh * gv)

    return _pc(
        body, name="mlp_bwd", grid=(T // tm,),
        in_specs=[_row_spec(tm, 1024), _row_spec(tm, 1024), _const_spec((1, 1024)), _hbm_spec()],
        out_specs=[_row_spec(tm, 1024), _row_spec(tm, 1024), _row_spec(tm, D_FF), _row_spec(tm, D_FF), _const_spec((1, 1024))],
        out_shape=[jax.ShapeDtypeStruct((T, 1024), F32), jax.ShapeDtypeStruct((T, 1024), BF16),
                   jax.ShapeDtypeStruct((T, D_FF), BF16), jax.ShapeDtypeStruct((T, D_FF), BF16),
                   jax.ShapeDtypeStruct((1, 1024), F32)],
        scratch_shapes=[pltpu.VMEM((FF_SHARDS, D_MODEL, FF_SHARD), BF16), pltpu.VMEM((D_FF, D_MODEL), BF16)],
        compiler_params=_cparams(("arbitrary",)),
    )(x1, dx2, g, wa)


WGRAD_OUT_ELEMS = 2 * 1024 * 1024
WGRAD_TILE_BYTES = 8 * 1024 * 1024


def _wgrad(a, b, name, col_shards=None, row_shards_into=None):
    T, K = a.shape
    N = b.shape[1]
    nb = N // col_shards if col_shards else min(N, 1024, max(128, WGRAD_OUT_ELEMS // K))
    tt = min(T, WGRAD_TOKENS)
    while tt * max(K * a.dtype.itemsize, nb * b.dtype.itemsize) > WGRAD_TILE_BYTES:
        tt //= 2
    assert N % nb == 0 and T % tt == 0
    in_specs = [pl.BlockSpec((tt, K), lambda n, t: (t, 0)), pl.BlockSpec((tt, nb), lambda n, t: (t, n))]
    args, aliases = [a, b], {}
    if col_shards:
        out_spec = pl.BlockSpec((None, None, K, nb), lambda n, t: (n, 0, 0, 0))
        out_shape = jax.ShapeDtypeStruct((col_shards, 2, K, nb), F32)
    elif row_shards_into is not None:
        shards, _, rows, cols = row_shards_into.shape
        assert shards * rows == K and cols == N
        out_spec = pl.BlockSpec((shards, None, rows, nb), lambda n, t: (0, 1, 0, n))
        out_shape = jax.ShapeDtypeStruct(row_shards_into.shape, F32)
        in_specs.append(_hbm_spec())
        args.append(row_shards_into)
        aliases = {2: 0}
    else:
        out_spec = pl.BlockSpec((K, nb), lambda n, t: (0, n))
        out_shape = jax.ShapeDtypeStruct((K, N), F32)

    def body(a_ref, b_ref, *rest):
        o_ref = rest[-1]

        @pl.when(pl.program_id(1) == 0)
        def _():
            o_ref[...] = jnp.zeros_like(o_ref)

        o_ref[...] += _dot_tn(a_ref[...].astype(BF16), b_ref[...].astype(BF16)).reshape(o_ref.shape)

    return _pc(
        body, name=name, grid=(N // nb, T // tt), in_specs=in_specs, out_specs=out_spec, out_shape=out_shape,
        input_output_aliases=aliases, compiler_params=_cparams(("parallel", "arbitrary")),
    )(*args)


def _s5_block_weights(bb_re, bb_im, c_re, c_im):
    eye = jnp.eye(8, dtype=F32)
    bre = bb_re.reshape(16, S5_BLOCKS, 8, 64)
    bim = bb_im.reshape(16, S5_BLOCKS, 8, 64)
    wb_re = jnp.einsum('kjgp,gh->jhkgp', bre, eye).reshape(S5_BLOCKS, _BI, _BW)
    wb_im = jnp.einsum('kjgp,gh->jhkgp', bim, eye).reshape(S5_BLOCKS, _BI, _BW)
    wb4 = jnp.concatenate([wb_re, wb_im], axis=2).astype(BF16)
    cre = c_re.reshape(S5_BLOCKS, 8, 16, 64)
    cim = c_im.reshape(S5_BLOCKS, 8, 16, 64)
    wc_re = jnp.einsum('jgkp,gh->jgphk', cre, eye).reshape(S5_BLOCKS, _BW, _BI)
    wc_im = jnp.einsum('jgkp,gh->jgphk', -cim, eye).reshape(S5_BLOCKS, _BW, _BI)
    wc4 = jnp.concatenate([wc_re, wc_im], axis=1).astype(BF16)
    return wb4, wc4


def _s5_block_grads(dwb4, dwc4):
    eye = jnp.eye(8, dtype=F32)
    dwb = dwb4.reshape(S5_BLOCKS, 8, 16, 2, 8, 64)
    dbb = jnp.einsum('jhkrgp,gh->rkjgp', dwb, eye).reshape(2, 16, S5_STATES)
    dwc = dwc4.reshape(S5_BLOCKS, 2, 8, 64, 8, 16)
    dc = jnp.einsum('jrgphk,gh->rjgkp', dwc, eye).reshape(2, 32, 16, 64)
    return dbb[0], dbb[1], dc[0], -dc[1]


def _row(v, width=None):
    v = v.reshape(1, -1)
    if width is not None and v.shape[1] < width:
        v = jnp.concatenate([v, jnp.zeros((1, width - v.shape[1]), v.dtype)], axis=1)
    return v


def _local_step(x, target, p, comm=None):
    g_mix, g_mlp, g_fin = _row(p["norm_mix_g"]), _row(p["norm_mlp_g"]), _row(p["norm_final_g"])
    conv_b = _row(p["conv_b"])
    dt_bias = _row(p["dt_bias"], DT_PAD)
    alog = _row(p["a_log"], DT_PAD)
    dvec = _row(jnp.repeat(p["d_ssd"], SSD_HEADDIM))
    gssd = _row(p["ssd_norm_g"])
    s5d = _row(p["s5_d"])
    glu_b = _row(p["s5_glu_b"])
    head_sel = (jnp.arange(SSD_INNER)[:, None] // SSD_HEADDIM == jnp.arange(DT_PAD)[None, :]).astype(F32)

    a_re = p["s5_a_re"].reshape(1, S5_STATES)
    a_im = p["s5_a_im"].reshape(1, S5_STATES)
    log_dt = jnp.repeat(p["s5_log_dt"], 64).reshape(1, S5_STATES)
    b_re = p["s5_b_re"].reshape(S5_STATES, 16).T
    b_im = p["s5_b_im"].reshape(S5_STATES, 16).T
    ab_re, ab_im, bb_re, bb_im = _s5_disc(a_re, a_im, log_dt, b_re, b_im)
    wb4, wc4 = _s5_block_weights(bb_re, bb_im, p["s5_c_re"], p["s5_c_im"])
    ab = jnp.concatenate([ab_re, ab_im, jnp.zeros((6, S5_STATES), F32)], axis=0)

    wp = p["w_in_perm"]

    first_args = (x, g_mix, wp, wb4, wc4, ab, s5d)
    if comm is None:
        z, xbc_raw, u5, gates, dt_raw, h, y5, s5_states = _inproj_s5_fwd(*first_args)
    else:
        first_out, late = _inproj_s5_fwd(*first_args, rider=_Gather(comm["late_srcs"], comm["late_ks"]))
        z, xbc_raw, u5, gates, dt_raw, h, y5, s5_states = first_out
        p = {**p, **comm["late_unpack"](late)}
    xbc_act, dt = _conv_fwd(xbc_raw, dt_raw, p["conv_w"], conv_b, dt_bias)
    ys, ssd_states = _ssd_fwd(xbc_act, dt, alog)
    wa, glu_w = p["late_weights"], p["s5_glu_w"]
    x1 = _merge_fwd(ys, xbc_act, z, y5, gates, x, dvec, gssd, glu_w, glu_b, wa)
    dx2, loss_lanes, d_gfin = _mlp_fwd_loss(x1, target, g_mlp, g_fin, wa)

    dx1, h2, act, da1, d_gmlp = _mlp_bwd(x1, dx2, g_mlp, wa)
    g_mlp4 = _wgrad(h2, da1, "wgrad_mlp_in", col_shards=FF_SHARDS)
    g_mlp4 = _wgrad(act, dx2, "wgrad_mlp_out", row_shards_into=g_mlp4)
    d_w_mlp_in, d_w_mlp_out = g_mlp4[:, 0], g_mlp4[:, 1].reshape(D_FF, D_MODEL)
    merge_args = (ys, xbc_act, z, y5, gates, dx1, dvec, gssd, glu_w, glu_b, wa, head_sel)
    if comm is None:
        merge_out = _merge_bwd(*merge_args)
    else:
        g_mlp = g_mlp4.reshape(N_CHIPS, 2 * FF_SHARD, D_MODEL)
        merge_out, (sib_mlp,) = _merge_bwd(*merge_args, rider=_Pair([g_mlp]))
        pf_mlp, pb_mlp = _pair_sum(comm["place"], g_mlp, sib_mlp, "pair_sum_mlp")
    (dys, dz, dy5, dgates, mg, ya, yb, dpa, dpb, gel, dpre, d_dssd, d_gssd, d_glu_b) = merge_out
    d_w_out = _wgrad(mg, dx1, "wgrad_out")
    d_w_branch = jnp.concatenate([_wgrad(ya, dpa, "wgrad_branch_a"), _wgrad(yb, dpb, "wgrad_branch_b")], axis=0)
    d_glu_w = _wgrad(gel, dpre, "wgrad_glu")
    s5_args = (u5, dy5, wb4, wc4, ab, s5d, s5_states)
    if comm is None:
        du5, dwb4, dwc4, dab, d_s5d = _s5_bwd(*s5_args)
        mlp_total = None
    else:
        (du5, dwb4, dwc4, dab, d_s5d), (got_mlp,) = _s5_bwd(*s5_args, rider=_Chip([pb_mlp]))
        mlp_total = _chip_sum(comm["place"], pf_mlp, got_mlp, "chip_sum_mlp")
    dbb_re, dbb_im, d_c_re, d_c_im = _s5_block_grads(dwb4, dwc4)
    d_a_re, d_a_im, d_log_dt, d_b_re, d_b_im = _s5_disc_bwd(
        a_re, a_im, log_dt, b_re, b_im, dab[0:1], dab[1:2], dbb_re, dbb_im)
    dxs_s, dB, dC, ddt, d_alog = _ssd_bwd(xbc_act, dt, alog, ssd_states, dys)
    dxbc_raw, ddt_raw, d_conv_w, d_conv_b, d_dt_bias = _conv_bwd(
        xbc_raw, dt_raw, dys, dvec, dxs_s, dB, dC, ddt, p["conv_w"], conv_b, dt_bias)
    d_w_in = dict(z=_wgrad(h, dz, "wgrad_in_z"), xbc=_wgrad(h, dxbc_raw, "wgrad_in_xbc"),
                  dt=_wgrad(h, ddt_raw, "wgrad_in_dt")[:, :16], u5=_wgrad(h, du5, "wgrad_in_u5"),
                  gates=_wgrad(h, dgates, "wgrad_in_gates"))
    w_in_pieces = [(c0, d_w_in[n]) for n, c0, _ in W_IN_PIECES]
    inproj_args = (x, dx1, dz, dxbc_raw, du5, dgates, ddt_raw, g_mix, wp)
    if comm is None:
        dx, d_gmix = _inproj_bwd(*inproj_args)
        late_totals = None
    else:
        g_b, g_in = _late_buffers(d_w_out, d_w_branch, d_glu_w, d_conv_w[:CONV_K], w_in_pieces)
        sib_b, sib_in = _exchange(_Pair([g_b, g_in]), "pair_exchange")
        pf_b, pb_b = _pair_sum(comm["place"], g_b, sib_b, "pair_sum_b")
        pf_in, pb_in = _pair_sum(comm["place"], g_in, sib_in, "pair_sum_in")
        (dx, d_gmix), (got_b, got_in) = _inproj_bwd(*inproj_args, rider=_Chip([pb_b, pb_in]))
        late_totals = (_chip_sum(comm["place"], pf_b, got_b, "chip_sum_b"),
                       _chip_sum(comm["place"], pf_in, got_in, "chip_sum_in"))

    grads = dict(
        norm_mix_g=d_gmix.reshape(-1), w_in_pieces=w_in_pieces, late_totals=late_totals,
        conv_w=d_conv_w[:CONV_K], conv_b=d_conv_b.reshape(-1),
        dt_bias=d_dt_bias[0, :16], a_log=d_alog[0, :16], d_ssd=d_dssd[0, :16], ssd_norm_g=d_gssd.reshape(-1),
        s5_a_re=d_a_re.reshape(32, 64), s5_a_im=d_a_im.reshape(32, 64), s5_log_dt=d_log_dt[0, :32],
        s5_b_re=d_b_re.T.reshape(32, 64, 16), s5_b_im=d_b_im.T.reshape(32, 64, 16), s5_c_re=d_c_re, s5_c_im=d_c_im,
        s5_d=d_s5d.reshape(-1), s5_glu_w=d_glu_w, s5_glu_b=d_glu_b.reshape(-1), w_branch=d_w_branch, w_out=d_w_out,
        norm_mlp_g=d_gmlp.reshape(-1), w_mlp_in=d_w_mlp_in, w_mlp_out=d_w_mlp_out, norm_final_g=d_gfin.reshape(-1),
        mlp_total=mlp_total)
    return jnp.sum(loss_lanes), dx, grads


MESH = pl.DeviceIdType.MESH
N_CHIPS = 4


def _place():
    x, y, c = lax.axis_index("x"), lax.axis_index("y"), lax.axis_index("c")
    chips = [(1 - x, y), (x, 1 - y), (1 - x, 1 - y)]
    return x, y, c, chips


def _remote(src, dst, send_sems, recv_sems, k, to):
    return pltpu.make_async_remote_copy(src_ref=src, dst_ref=dst, send_sem=send_sems.at[k], recv_sem=recv_sems.at[k],
                                        device_id=to, device_id_type=MESH)


def _row_chunks(rows, k, align):
    step = rows // k
    assert rows % k == 0 and step % align == 0, (rows, k, align)
    return [(i * step, step) for i in range(k)]


ICI_CHUNKS = 4
D2D_CHUNKS = 24


class _Gather:
    def __init__(self, srcs, ks):
        self.inputs = list(srcs)
        self.out_shapes = [jax.ShapeDtypeStruct((N_CHIPS,) + a.shape, a.dtype) for a in srcs]
        self.halves = [a.shape[0] // 2 for a in srcs]
        self.pieces = [_row_chunks(h, k, 32 // a.dtype.itemsize) for a, h, k in zip(srcs, self.halves, ks)]
        self.n_ici = 3 * sum(ks)
        self.n_sems = 2 * self.n_ici + len(srcs)

    def _plan(self, src_refs, out_refs, send_sems, recv_sems):
        x, y, c, chips = _place()
        own = 2 * x + y
        sib = (x, y, 1 - c)
        first, fwd_plan, k = [], [], 0
        for a, (src_ref, out_ref) in enumerate(zip(src_refs, out_refs)):
            h = self.halves[a]
            for r0, nr in self.pieces[a]:
                for cx, cy in chips:
                    first.append(_remote(src_ref.at[pl.ds(c * h + r0, nr), :], out_ref.at[own, pl.ds(c * h + r0, nr), :],
                                         send_sems, recv_sems, k, (cx, cy, c)))
                    fwd_plan.append((out_ref, 2 * cx + cy, h, r0, nr, k, (cx, cy, c)))
                    k += 1
        for a, (src_ref, out_ref) in enumerate(zip(src_refs, out_refs)):
            first.append(_remote(src_ref, out_ref.at[own], send_sems, recv_sems, 2 * self.n_ici + a, sib))
        return first, fwd_plan, c, sib

    def issue(self, src_refs, out_refs, send_sems, recv_sems):
        for cp in self._plan(src_refs, out_refs, send_sems, recv_sems)[0]:
            cp.start()

    def complete(self, src_refs, out_refs, send_sems, recv_sems):
        first, fwd_plan, c, sib = self._plan(src_refs, out_refs, send_sems, recv_sems)
        passed = []
        for out_ref, s, h, r0, nr, k, frm in fwd_plan:
            got = out_ref.at[s, pl.ds(c * h + r0, nr), :]
            _remote(got, got, send_sems, recv_sems, k, frm).wait_recv()
            fw = _remote(got, got, send_sems, recv_sems, self.n_ici + k, sib)
            fw.start()
            passed.append(fw)
        for out_ref, s, h, r0, nr, k, frm in fwd_plan:
            got = out_ref.at[s, pl.ds((1 - c) * h + r0, nr), :]
            _remote(got, got, send_sems, recv_sems, self.n_ici + k, sib).wait_recv()
        own_copies = first[self.n_ici:]
        for cp in own_copies:
            cp.wait_recv()
        for cp in first + passed:
            cp.wait_send()


def _exchange(rider, name):
    ri, ro = len(rider.inputs), len(rider.out_shapes)

    def body(*refs):
        rider.issue(refs[:ri], refs[ri:ri + ro], *refs[ri + ro:])
        rider.complete(refs[:ri], refs[ri:ri + ro], *refs[ri + ro:])

    return _pc(
        body, name=name, in_specs=[_hbm_spec()] * ri, out_specs=[_hbm_spec()] * ro, out_shape=list(rider.out_shapes),
        scratch_shapes=[pltpu.SemaphoreType.DMA((rider.n_sems,))] * 2,
    )(*rider.inputs)


def _call(body, rider=None, **kw):
    if rider is None:
        return _pc(body, **kw)
    single = not isinstance(kw["out_shape"], (list, tuple))
    out_specs = [kw["out_specs"]] if single else list(kw["out_specs"])
    out_shape = [kw["out_shape"]] if single else list(kw["out_shape"])
    scratch = list(kw.get("scratch_shapes", ()))
    n_in, n_out, n_scr = len(kw["in_specs"]), len(out_specs), len(scratch)
    ri, ro = len(rider.inputs), len(rider.out_shapes)
    steps = kw["grid"][0]

    def wrapped(*refs):
        o0 = n_in + ri
        s0 = o0 + n_out + ro
        r_in, r_out, sems = refs[n_in:o0], refs[o0 + n_out:s0], refs[s0 + n_scr:]

        @pl.when(pl.program_id(0) == 0)
        def _():
            rider.issue(r_in, r_out, *sems)

        body(*refs[:n_in], *refs[o0:o0 + n_out], *refs[s0:s0 + n_scr])

        @pl.when(pl.program_id(0) == steps - 1)
        def _():
            rider.complete(r_in, r_out, *sems)

    f = _pc(wrapped, name=kw["name"], grid=kw["grid"], in_specs=list(kw["in_specs"]) + [_hbm_spec()] * ri,
            out_specs=out_specs + [_hbm_spec()] * ro, out_shape=out_shape + list(rider.out_shapes),
            scratch_shapes=scratch + [pltpu.SemaphoreType.DMA((rider.n_sems,))] * 2, compiler_params=kw["compiler_params"])

    def run(*args):
        res = f(*args, *rider.inputs)
        return (res[0] if single else res[:n_out]), res[n_out:]

    return run


def _d2d_pieces(rows):
    k = next(k for k in range(24, 0, -1) if rows % k == 0 and (rows // k) % 8 == 0)
    return _row_chunks(rows, k, 8)


class _Pair:
    def __init__(self, gs, small=None):
        self.n = len(gs)
        self.halves = [g.shape[1] // 2 for g in gs]
        self.inputs = list(gs) + ([small] if small is not None else [])
        self.out_shapes = [jax.ShapeDtypeStruct((N_CHIPS, h, g.shape[2]), F32) for g, h in zip(gs, self.halves)]
        if small is not None:
            self.out_shapes.append(jax.ShapeDtypeStruct(small.shape, F32))
        self.n_sems = len(self.inputs)

    def issue(self, in_refs, out_refs, send_sems, recv_sems):
        x, y, c, _ = _place()
        sib = (x, y, 1 - c)
        for a in range(self.n):
            h = self.halves[a]
            for s in range(N_CHIPS):
                for r0, nr in _d2d_pieces(h):
                    _remote(in_refs[a].at[s, pl.ds((1 - c) * h + r0, nr), :], out_refs[a].at[s, pl.ds(r0, nr), :],
                            send_sems, recv_sems, a, sib).start()
        for a in range(self.n, len(self.inputs)):
            _remote(in_refs[a], out_refs[a], send_sems, recv_sems, a, sib).start()

    def complete(self, in_refs, out_refs, send_sems, recv_sems):
        x, y, c, _ = _place()
        for a in range(len(self.inputs)):
            _remote(out_refs[a], out_refs[a], send_sems, recv_sems, a, (x, y, 1 - c)).wait()


SUM_BLOCKS = 4


def _pair_sum(place, g, sib, name):
    n, R, C = g.shape
    H = R // 2
    rb = H // SUM_BLOCKS
    assert H % SUM_BLOCKS == 0 and rb % 16 == 0

    def body(place_ref, a_ref, b_ref, pf_ref, pb_ref):
        p = a_ref[...] + b_ref[...]
        pf_ref[...] = p
        pb_ref[...] = p.astype(BF16)

    blk = pl.BlockSpec((1, rb, C), lambda s, i, pr: (s, i, 0))
    mine = pl.BlockSpec((1, rb, C), lambda s, i, pr: (s, pr[1] * SUM_BLOCKS + i, 0))
    return _pc(
        body, name=name, out_shape=[jax.ShapeDtypeStruct((n, H, C), F32), jax.ShapeDtypeStruct((n, H, C), BF16)],
        grid_spec=pltpu.PrefetchScalarGridSpec(num_scalar_prefetch=1, grid=(n, SUM_BLOCKS), in_specs=[mine, blk],
                                               out_specs=[blk, blk]),
        compiler_params=_cparams(("arbitrary", "arbitrary")),
    )(place, g, sib)


class _Chip:
    def __init__(self, pbs, psmall=None):
        self.n = len(pbs)
        self.rows = [pb.shape[1] for pb in pbs]
        self.inputs = list(pbs) + ([psmall] if psmall is not None else [])
        self.out_shapes = [jax.ShapeDtypeStruct((3,) + pb.shape[1:], BF16) for pb in pbs]
        if psmall is not None:
            self.out_shapes.append(jax.ShapeDtypeStruct((N_CHIPS,) + psmall.shape, F32))
        self.n_sems = 3 * len(self.inputs)

    def issue(self, in_refs, out_refs, send_sems, recv_sems):
        x, y, c, chips = _place()
        own = 2 * x + y
        for j, (cx, cy) in enumerate(chips):
            for a in range(self.n):
                for r0, nr in _row_chunks(self.rows[a], ICI_CHUNKS, 16):
                    _remote(in_refs[a].at[2 * cx + cy, pl.ds(r0, nr), :], out_refs[a].at[j, pl.ds(r0, nr), :],
                            send_sems, recv_sems, 3 * a + j, (cx, cy, c)).start()
            for a in range(self.n, len(self.inputs)):
                _remote(in_refs[a], out_refs[a].at[own], send_sems, recv_sems, 3 * a + j, (cx, cy, c)).start()

    def complete(self, in_refs, out_refs, send_sems, recv_sems):
        x, y, c, chips = _place()
        own = 2 * x + y
        for j, (cx, cy) in enumerate(chips):
            for a in range(self.n):
                _remote(in_refs[a].at[own], out_refs[a].at[j], send_sems, recv_sems, 3 * a + j, (cx, cy, c)).wait()
            for a in range(self.n, len(self.inputs)):
                _remote(in_refs[a], out_refs[a].at[2 * cx + cy], send_sems, recv_sems, 3 * a + j, (cx, cy, c)).wait()


def _chip_sum(place, pf, got, name):
    _, H, C = pf.shape
    rb = H // SUM_BLOCKS

    def body(place_ref, o_ref, g_ref, tot_ref):
        tot_ref[...] = ((o_ref[0] + g_ref[0].astype(F32)) + g_ref[1].astype(F32)) + g_ref[2].astype(F32)

    ins = [pl.BlockSpec((1, rb, C), lambda i, pr: (pr[0], i, 0)), pl.BlockSpec((3, rb, C), lambda i, pr: (0, i, 0))]
    out = pl.BlockSpec((rb, C), lambda i, pr: (pr[1] * SUM_BLOCKS + i, 0))
    return _pc(
        body, name=name, out_shape=jax.ShapeDtypeStruct((2 * H, C), F32),
        grid_spec=pltpu.PrefetchScalarGridSpec(num_scalar_prefetch=1, grid=(SUM_BLOCKS,), in_specs=ins, out_specs=out),
        compiler_params=_cparams(("arbitrary",)),
    )(place, pf, got)


def _half_exchange(fulls):
    n = len(fulls)

    def body(*refs):
        in_refs, out_refs = refs[:n], refs[n:2 * n]
        send_sems, recv_sems = refs[2 * n:]
        x, y, c, _ = _place()
        sib = (x, y, 1 - c)
        for a in range(n):
            h = fulls[a].shape[0] // 2
            for r0, nr in _d2d_pieces(h):
                rows = pl.ds(c * h + r0, nr)
                _remote(in_refs[a].at[rows, :], out_refs[a].at[rows, :], send_sems, recv_sems, a, sib).start()
        for a in range(n):
            h = fulls[a].shape[0] // 2
            _remote(in_refs[a].at[pl.ds(c * h, h), :], out_refs[a].at[pl.ds((1 - c) * h, h), :], send_sems, recv_sems, a,
                    sib).wait()

    return _pc(
        body, name="half_exchange", in_specs=[_hbm_spec()] * n, out_specs=[_hbm_spec()] * n,
        out_shape=[jax.ShapeDtypeStruct(f.shape, F32) for f in fulls],
        input_output_aliases={a: a for a in range(n)},
        scratch_shapes=[pltpu.SemaphoreType.DMA((n,)), pltpu.SemaphoreType.DMA((n,))],
    )(*fulls)


def _small_allreduce(pack):
    R, C = pack.shape

    def body(p_ref, o_ref, sib_ref, pair_ref, slots_ref, send_sems, recv_sems):
        x, y, c, chips = _place()
        own = 2 * x + y
        cp = _remote(p_ref, sib_ref, send_sems, recv_sems, 0, (x, y, 1 - c))
        cp.start()
        cp.wait()
        pair_ref[...] = p_ref[...] + sib_ref[...]
        slots_ref[own] = pair_ref[...]
        out = [_remote(pair_ref, slots_ref.at[own], send_sems, recv_sems, 1 + j, (cx, cy, c)) for j, (cx, cy) in enumerate(chips)]
        for cp in out:
            cp.start()
        for j, (cx, cy) in enumerate(chips):
            _remote(pair_ref, slots_ref.at[2 * cx + cy], send_sems, recv_sems, 1 + j, (cx, cy, c)).wait()
        o_ref[...] = ((slots_ref[0] + slots_ref[1]) + slots_ref[2]) + slots_ref[3]

    vmem = pl.BlockSpec(memory_space=pltpu.VMEM)
    return _pc(
        body, name="small_allreduce", in_specs=[vmem], out_specs=vmem, out_shape=jax.ShapeDtypeStruct((R, C), F32),
        scratch_shapes=[pltpu.VMEM((R, C), F32), pltpu.VMEM((R, C), F32), pltpu.VMEM((N_CHIPS, R, C), F32),
                        pltpu.SemaphoreType.DMA((4,)), pltpu.SemaphoreType.DMA((4,))],
    )(pack)


def _adamw(w, g, m, v, name, g_row0=0, with_grad=False, col_block=None):
    R, C = w.shape
    rb = 256 if R % 256 == 0 else (128 if R % 128 == 0 else R)
    if col_block:
        rb = R
    assert g_row0 % rb == 0

    def body(w_ref, g_ref, m_ref, v_ref, d_ref, nm_ref, nv_ref, *g_out):
        gv = g_ref[...]
        m2 = ADAM_B1 * m_ref[...] + (1.0 - ADAM_B1) * gv
        v2 = ADAM_B2 * v_ref[...] + (1.0 - ADAM_B2) * (gv * gv)
        m_hat = m2 * (1.0 / (1.0 - ADAM_B1 ** ADAM_STEP))
        v_hat = v2 * (1.0 / (1.0 - ADAM_B2 ** ADAM_STEP))
        d_ref[...] = -ADAM_LR * (m_hat / (jnp.sqrt(v_hat) + ADAM_EPS) + ADAM_WD * w_ref[...])
        nm_ref[...] = m2
        nv_ref[...] = v2
        if with_grad:
            g_out[0][...] = gv

    if col_block:
        spec = g_spec = pl.BlockSpec((R, col_block), lambda i: (0, i))
        steps = C // col_block
    else:
        spec = pl.BlockSpec((rb, C), lambda i: (i, 0))
        g_spec = pl.BlockSpec((rb, C), lambda i: (g_row0 // rb + i, 0))
        steps = R // rb
    n_out = 4 if with_grad else 3
    return _pc(
        body, name=name, grid=(steps,), in_specs=[spec, g_spec, spec, spec], out_specs=[spec] * n_out,
        out_shape=[jax.ShapeDtypeStruct((R, C), F32)] * n_out, compiler_params=_cparams(("parallel",)),
    )(w, g, m, v)


PACK_COLS = 1024
ROWS_A = (("w_mlp_in", 0, 1024), ("w_mlp_out", 1024, 1024), ("w_out", 2048, 256), ("w_branch", 2304, 384))
ROWS_A_TOTAL = 2688
ROWS_B = (("w_out", 0, 256), ("w_branch", 256, 384))
ROW_B_GLU, ROW_B_CONV, ROWS_B_TOTAL = 640, 704, 768
W_IN_SHARD = 1412
CONV_PAD_ROWS = 16
SMALL = (("norm_mix_g", (1024,)), ("conv_b", (2048,)), ("dt_bias", (16,)), ("a_log", (16,)), ("d_ssd", (16,)),
         ("ssd_norm_g", (1024,)), ("s5_a_re", (32, 64)), ("s5_a_im", (32, 64)), ("s5_log_dt", (32,)),
         ("s5_b_re", (32, 64, 16)), ("s5_b_im", (32, 64, 16)), ("s5_c_re", (32, 16, 64)), ("s5_c_im", (32, 16, 64)),
         ("s5_d", (512,)), ("s5_glu_b", (512,)), ("norm_mlp_g", (1024,)), ("norm_final_g", (1024,)))
SMALL_ROWS = 144
SMALL_COUNT = sum(math.prod(shp) for _, shp in SMALL)
GLU_ROWS = S5_WIDTH * S5_WIDTH // PACK_COLS
CONV_ROWS = CONV_K * CONV_DIM // PACK_COLS
W_IN_PIECES = (("z", 0, 1024), ("xbc", 1024, 2048), ("dt", OFF_DT, 16), ("u5", OFF_U, 512), ("gates", 3600, 2048))


def _pack_small(parts):
    flat = jnp.concatenate([a.astype(F32).reshape(-1) for a in parts])
    return jnp.concatenate([flat, jnp.zeros((SMALL_ROWS * PACK_COLS - flat.shape[0],), F32)]).reshape(SMALL_ROWS, PACK_COLS)


def _unpack_small(pack):
    flat, out, r = pack.reshape(-1), {}, 0
    for name, shp in SMALL:
        n = math.prod(shp)
        out[name] = flat[r:r + n].reshape(shp)
        r += n
    return out


def _late_buffers(d_w_out, d_w_branch, d_glu_w, d_conv_w, w_in_pieces):
    conv4 = d_conv_w.reshape(CONV_K, N_CHIPS, 512).transpose(1, 0, 2).reshape(N_CHIPS, CONV_ROWS // N_CHIPS, PACK_COLS)
    g_b = jnp.concatenate(
        [d_w_out.reshape(N_CHIPS, -1, PACK_COLS), d_w_branch.reshape(N_CHIPS, -1, PACK_COLS),
         d_glu_w.reshape(N_CHIPS, GLU_ROWS // N_CHIPS, PACK_COLS),
         jnp.pad(conv4, ((0, 0), (0, ROWS_B_TOTAL - ROW_B_CONV - CONV_ROWS // N_CHIPS), (0, 0)))], axis=1)
    g_in = jnp.stack([jnp.concatenate(_column_range(w_in_pieces, W_IN_SHARD * s, W_IN_SHARD * (s + 1)), axis=1)
                      for s in range(N_CHIPS)])
    return g_b, g_in


def _column_range(pieces, lo, hi):
    out = []
    for c0, a in pieces:
        a0, a1 = max(lo, c0), min(hi, c0 + a.shape[-1])
        if a0 < a1:
            out.append(a[..., a0 - c0:a1 - c0])
    return out


def kernel(x, norm_mix_g, w_in, conv_w, conv_b, dt_bias, a_log, d_ssd, ssd_norm_g, s5_a_re, s5_a_im, s5_log_dt, s5_b_re, s5_b_im, s5_c_re, s5_c_im, s5_d, s5_glu_w, s5_glu_b, w_branch, w_out, norm_mlp_g, w_mlp_in, w_mlp_out, norm_final_g, loss_target, m_norm_mix_g, m_w_in, m_conv_w, m_conv_b, m_dt_bias, m_a_log, m_d_ssd, m_ssd_norm_g, m_s5_a_re, m_s5_a_im, m_s5_log_dt, m_s5_b_re, m_s5_b_im, m_s5_c_re, m_s5_c_im, m_s5_d, m_s5_glu_w, m_s5_glu_b, m_w_branch, m_w_out, m_norm_mlp_g, m_w_mlp_in, m_w_mlp_out, m_norm_final_g, v_norm_mix_g, v_w_in, v_conv_w, v_conv_b, v_dt_bias, v_a_log, v_d_ssd, v_ssd_norm_g, v_s5_a_re, v_s5_a_im, v_s5_log_dt, v_s5_b_re, v_s5_b_im, v_s5_c_re, v_s5_c_im, v_s5_d, v_s5_glu_w, v_s5_glu_b, v_w_branch, v_w_out, v_norm_mlp_g, v_w_mlp_in, v_w_mlp_out, v_norm_final_g):
    names = ("norm_mix_g", "w_in", "conv_w", "conv_b", "dt_bias", "a_log", "d_ssd", "ssd_norm_g", "s5_a_re", "s5_a_im",
             "s5_log_dt", "s5_b_re", "s5_b_im", "s5_c_re", "s5_c_im", "s5_d", "s5_glu_w", "s5_glu_b", "w_branch", "w_out",
             "norm_mlp_g", "w_mlp_in", "w_mlp_out", "norm_final_g")
    w = dict(zip(names, (norm_mix_g, w_in, conv_w, conv_b, dt_bias, a_log, d_ssd, ssd_norm_g, s5_a_re, s5_a_im, s5_log_dt,
                         s5_b_re, s5_b_im, s5_c_re, s5_c_im, s5_d, s5_glu_w, s5_glu_b, w_branch, w_out, norm_mlp_g,
                         w_mlp_in, w_mlp_out, norm_final_g)))
    m = dict(zip(names, (m_norm_mix_g, m_w_in, m_conv_w, m_conv_b, m_dt_bias, m_a_log, m_d_ssd, m_ssd_norm_g, m_s5_a_re,
                         m_s5_a_im, m_s5_log_dt, m_s5_b_re, m_s5_b_im, m_s5_c_re, m_s5_c_im, m_s5_d, m_s5_glu_w,
                         m_s5_glu_b, m_w_branch, m_w_out, m_norm_mlp_g, m_w_mlp_in, m_w_mlp_out, m_norm_final_g)))
    v = dict(zip(names, (v_norm_mix_g, v_w_in, v_conv_w, v_conv_b, v_dt_bias, v_a_log, v_d_ssd, v_ssd_norm_g, v_s5_a_re,
                         v_s5_a_im, v_s5_log_dt, v_s5_b_re, v_s5_b_im, v_s5_c_re, v_s5_c_im, v_s5_d, v_s5_glu_w,
                         v_s5_glu_b, v_w_branch, v_w_out, v_norm_mlp_g, v_w_mlp_in, v_w_mlp_out, v_norm_final_g)))

    cx, cy, cc = lax.axis_index("x"), lax.axis_index("y"), lax.axis_index("c")
    own = 2 * cx + cy
    place = jnp.stack([own, cc]).astype(jnp.int32)

    src_conv = jnp.concatenate([conv_w, jnp.zeros((CONV_PAD_ROWS - CONV_K, 512), F32)], axis=0)
    all_in, all_conv = _exchange(_Gather([w_in.astype(BF16), src_conv], [ICI_CHUNKS, 1]), "gather_first")
    p = {n: w[n] for n, _ in SMALL}
    p["conv_w"] = jnp.concatenate([all_conv[s, :CONV_K] for s in range(N_CHIPS)], axis=1)
    shards = [(W_IN_SHARD * s, all_in[s]) for s in range(N_CHIPS)]
    p["w_in_perm"] = jnp.concatenate(
        _column_range(shards, 0, OFF_DT) + _column_range(shards, OFF_U, D_IN_PROJ) + _column_range(shards, OFF_DT, OFF_U)
        + [jnp.zeros((D_MODEL, DT_PAD - 16), BF16)], axis=1)

    def late_unpack(gathered):
        all_a, all_glu = gathered
        return {"late_weights": all_a, "s5_glu_w": all_glu.reshape(S5_WIDTH, S5_WIDTH)}

    comm = dict(place=place, late_ks=[ICI_CHUNKS, 1], late_unpack=late_unpack,
                late_srcs=[jnp.concatenate([w[n].astype(BF16) for n, _, _ in ROWS_A], axis=0), s5_glu_w.astype(BF16)])
    loss_part, grad_x, g = _local_step(x[0], loss_target[0], p, comm)

    red_mlp, red_b, red_in = _half_exchange([g["mlp_total"], *g["late_totals"]])
    small_tot = _small_allreduce(_pack_small([g[n] for n, _ in SMALL] + [loss_part.reshape(1)]))
    loss = small_tot.reshape(-1)[SMALL_COUNT]

    grads = _unpack_small(small_tot)
    delta, new_m, new_v = {}, {}, {}
    for n, r0, _ in ROWS_A[:2]:
        delta[n], new_m[n], new_v[n], grads[n] = _adamw(w[n], red_mlp, m[n], v[n], "adamw_" + n, g_row0=r0, with_grad=True)
    for n, r0, _ in ROWS_B:
        delta[n], new_m[n], new_v[n], grads[n] = _adamw(w[n], red_b, m[n], v[n], "adamw_" + n, g_row0=r0, with_grad=True)
    d_t, m_t, v_t, g_t = _adamw(w_in.T, red_in.T, m_w_in.T, v_w_in.T, "adamw_w_in", with_grad=True, col_block=128)
    delta["w_in"], new_m["w_in"], new_v["w_in"], grads["w_in"] = d_t.T, m_t.T, v_t.T, g_t.T
    grads["s5_glu_w"] = red_b[ROW_B_GLU:ROW_B_GLU + GLU_ROWS // N_CHIPS].reshape(S5_WIDTH // N_CHIPS, S5_WIDTH)
    grads["conv_w"] = red_b[ROW_B_CONV:ROW_B_CONV + CONV_ROWS // N_CHIPS].reshape(CONV_K, CONV_DIM // N_CHIPS)
    for n in ("s5_glu_w", "conv_w"):
        delta[n], new_m[n], new_v[n] = _adamw(w[n], grads[n], m[n], v[n], "adamw_" + n)
    ds, ms, vs = _adamw(_pack_small([w[n] for n, _ in SMALL]), small_tot, _pack_small([m[n] for n, _ in SMALL]),
                        _pack_small([v[n] for n, _ in SMALL]), "adamw_small")
    delta.update(_unpack_small(ds))
    new_m.update(_unpack_small(ms))
    new_v.update(_unpack_small(vs))

    return (loss, grad_x[None], *[grads[n] for n in names], *[delta[n] for n in names],
            *[new_m[n] for n in names], *[new_v[n] for n in names])
```

```python
import functools
import math

import jax
import jax.numpy as jnp
from jax import lax
from jax.experimental import pallas as pl
from jax.experimental.pallas import tpu as pltpu

F32 = jnp.float32
BF16 = jnp.bfloat16

D_MODEL = 1024
SSD_INNER = 1024
SSD_HEADS = 16
SSD_HEADDIM = 64
SSD_GROUPS = 4
SSD_HPG = 4
SSD_STATE = 128
SSD_CHUNK = 128
CONV_K = 4
CONV_DIM = 2048
S5_WIDTH = 512
S5_STATES = 2048
S5_BLOCKS = 4
S5_CHUNK = 128
D_FF = 4096
FF_SHARDS = 4
FF_SHARD = D_FF // FF_SHARDS
EPS = 1e-6
P_Z, P_XBC, P_U5, P_G, P_DT, P_END = 0, 1024, 3072, 3584, 5632, 5760
DT_PAD = 128
OFF_DT, OFF_U = 3072, 3088
D_IN_PROJ = 5648

ADAM_LR, ADAM_B1, ADAM_B2, ADAM_EPS, ADAM_WD, ADAM_STEP = 0.001, 0.9, 0.999, 1e-08, 0.01, 10

TOKEN_TILE = 256
VMEM_LIMIT = 56 * 1024 * 1024
HALO = 8
INPROJ_PIECE = 256
CONV_COLS = 256
CONV_ROWS_BLK = 64
WGRAD_TOKENS = 2048


def _pc(body, **kw):
    return pl.pallas_call(body, **kw)


def _cparams(sem=None):
    return pltpu.CompilerParams(dimension_semantics=sem, vmem_limit_bytes=VMEM_LIMIT)


def _dot(a, b):
    return jnp.dot(a, b, preferred_element_type=F32)


def _dot_nt(a, b):
    return lax.dot_general(a, b, (((1,), (1,)), ((), ())), preferred_element_type=F32)


def _dot_tn(a, b):
    return lax.dot_general(a, b, (((0,), (0,)), ((), ())), preferred_element_type=F32)


def _dot_hi(a, b, dims=(((1,), (0,)), ((), ()))):
    return lax.dot_general(a, b, dims, preferred_element_type=F32, precision=lax.Precision.HIGHEST)


def _split_bf16(x, terms):
    out = []
    for _ in range(terms - 1):
        t = x.astype(BF16)
        out.append(t)
        x = x - t.astype(F32)
    out.append(x.astype(BF16))
    return out


def _dot_split(x, onehots, terms, dims=(((1,), (0,)), ((), ()))):
    acc = None
    for t in _split_bf16(x, terms):
        p = lax.dot_general(t, onehots, dims, preferred_element_type=F32)
        acc = p if acc is None else acc + p
    return acc


def _dot_split_rhs(onehots, x, terms, dims=(((1,), (0,)), ((), ()))):
    acc = None
    for t in _split_bf16(x, terms):
        p = lax.dot_general(onehots, t, dims, preferred_element_type=F32)
        acc = p if acc is None else acc + p
    return acc


def _sigmoid(x):
    return 0.5 * jnp.tanh(0.5 * x) + 0.5


def _softplus(x):
    return jnp.maximum(x, 0.0) + jnp.log(1.0 + jnp.exp(-jnp.abs(x)))


_GELU_C = math.sqrt(2.0 / math.pi)


def _gelu(x):
    return 0.5 * x * (1.0 + jnp.tanh(_GELU_C * (x + 0.044715 * x * x * x)))


def _gelu_grad(x):
    t = jnp.tanh(_GELU_C * (x + 0.044715 * x * x * x))
    return 0.5 * (1.0 + t) + 0.5 * x * (1.0 - t * t) * _GELU_C * (1.0 + 3.0 * 0.044715 * x * x)


def _rms(x):
    r = lax.rsqrt(jnp.mean(x * x, axis=-1, keepdims=True) + EPS)
    return x * r, r


def _rms_bwd(xn, r, dxn):
    return r * (dxn - xn * jnp.mean(dxn * xn, axis=-1, keepdims=True))


def _row_spec(tm, width, col=0):
    return pl.BlockSpec((tm, width), lambda i: (i, col))


def _const_spec(shape):
    nd = len(shape)
    return pl.BlockSpec(shape, lambda i: (0,) * nd)


def _hbm_spec():
    return pl.BlockSpec(memory_space=pl.ANY)


def _load_late_weight(wa_hbm, dst_ref, name):
    r0, nr = next((r0, nr) for n, r0, nr in ROWS_A if n == name)
    for s in range(N_CHIPS):
        dst = dst_ref.at[s] if len(dst_ref.shape) == 3 else dst_ref.at[pl.ds(nr * s, nr), :]
        pltpu.sync_copy(wa_hbm.at[s, pl.ds(r0, nr), :], dst)


def _inproj_bwd(x, dx1, dz, dxbc, du5, dgt, ddt, g, wp, rider=None):
    T = x.shape[0]
    tm = TOKEN_TILE

    def body(x_ref, dx1_ref, dz_ref, dxbc_ref, du5_ref, dgt_ref, ddt_ref, g_ref, w_hbm, dx_ref, dg_ref, w_ref):
        @pl.when(pl.program_id(0) == 0)
        def _():
            pltpu.sync_copy(w_hbm, w_ref)
            dg_ref[...] = jnp.zeros_like(dg_ref)

        xn, r = _rms(x_ref[...])
        gv = g_ref[...]
        dh = _dot_nt(dz_ref[...].astype(BF16), w_ref[:, P_Z:P_XBC])
        dh += _dot_nt(dxbc_ref[...].astype(BF16), w_ref[:, P_XBC:P_U5])
        dh += _dot_nt(du5_ref[...].astype(BF16), w_ref[:, P_U5:P_G])
        dh += _dot_nt(dgt_ref[...].astype(BF16), w_ref[:, P_G:P_DT])
        dh += _dot_nt(ddt_ref[...].astype(BF16), w_ref[:, P_DT:P_END])
        dg_ref[...] += jnp.sum(dh * xn, axis=0, keepdims=True)
        dx_ref[...] = dx1_ref[...] + _rms_bwd(xn, r, dh * gv)

    return _call(
        body, rider, name="inproj_bwd", grid=(T // tm,),
        in_specs=[_row_spec(tm, 1024), _row_spec(tm, 1024), _row_spec(tm, 1024), _row_spec(tm, 2048),
                  _row_spec(tm, 512), _row_spec(tm, 2048), _row_spec(tm, DT_PAD), _const_spec((1, 1024)), _hbm_spec()],
        out_specs=[_row_spec(tm, 1024), _const_spec((1, 1024))],
        out_shape=[jax.ShapeDtypeStruct((T, 1024), F32), jax.ShapeDtypeStruct((1, 1024), F32)],
        scratch_shapes=[pltpu.VMEM((D_MODEL, P_END), BF16)],
        compiler_params=_cparams(("arbitrary",)),
    )(x, dx1, dz, dxbc, du5, dgt, ddt, g, wp)


def _conv_fwd(xbc_raw, dt_raw, conv_w, conv_b, dt_bias):
    T = xbc_raw.shape[0]
    tm = TOKEN_TILE

    def body(u_ref, dtr_ref, w_ref, b_ref, db_ref, act_ref, dt_ref, ext_ref):
        @pl.when(pl.program_id(0) == 0)
        def _():
            ext_ref[0:HALO, :] = jnp.zeros((HALO, CONV_DIM), F32)

        ext_ref[HALO:, :] = u_ref[...]
        for c0 in range(0, CONV_DIM, CONV_COLS):
            cols = slice(c0, c0 + CONV_COLS)
            taps = [w_ref[k:k + 1, cols] for k in range(CONV_K)]
            bias = b_ref[:, cols]
            for r0 in range(0, tm, CONV_ROWS_BLK):
                y = bias + taps[0] * ext_ref[pl.ds(HALO - (CONV_K - 1) + r0, CONV_ROWS_BLK), cols]
                for k in range(1, CONV_K):
                    y += taps[k] * ext_ref[pl.ds(HALO - (CONV_K - 1) + k + r0, CONV_ROWS_BLK), cols]
                act_ref[r0:r0 + CONV_ROWS_BLK, cols] = y * _sigmoid(y)
        ext_ref[0:HALO, :] = u_ref[tm - HALO:tm, :]
        dt_ref[...] = _softplus(dtr_ref[...] + db_ref[...])

    return _pc(
        body, name="conv_fwd", grid=(T // tm,),
        in_specs=[_row_spec(tm, CONV_DIM), _row_spec(tm, DT_PAD), _const_spec((CONV_K, CONV_DIM)),
                  _const_spec((1, CONV_DIM)), _const_spec((1, DT_PAD))],
        out_specs=[_row_spec(tm, CONV_DIM), _row_spec(tm, DT_PAD)],
        out_shape=[jax.ShapeDtypeStruct((T, CONV_DIM), F32), jax.ShapeDtypeStruct((T, DT_PAD), F32)],
        scratch_shapes=[pltpu.VMEM((tm + HALO, CONV_DIM), F32)],
        compiler_params=_cparams(("arbitrary",)),
    )(xbc_raw, dt_raw, conv_w, conv_b, dt_bias)


def _conv_bwd(xbc_raw, dt_raw, dys, dvec, dxs_b, dB, dC, ddt, conv_w, conv_b, dt_bias):
    T = xbc_raw.shape[0]
    tm = TOKEN_TILE
    n = T // tm
    hb = tm // HALO

    def rev(width):
        return pl.BlockSpec((tm, width), lambda i: (n - 1 - i, 0))

    def body(u_ref, up_ref, dtr_ref, dys_ref, dv_ref, dxb_ref, dB_ref, dC_ref, ddt_ref, w_ref, b_ref, db_ref,
             du_ref, ddtr_ref, dw_ref, dcb_ref, ddb_ref, ext_ref, dye_ref):
        i = pl.program_id(0)

        @pl.when(i == 0)
        def _():
            dye_ref[tm:, :] = jnp.zeros((HALO, CONV_DIM), F32)
            dw_ref[...] = jnp.zeros_like(dw_ref)
            dcb_ref[...] = jnp.zeros_like(dcb_ref)
            ddb_ref[...] = jnp.zeros_like(ddb_ref)

        first = (i == n - 1).astype(F32)
        ext_ref[0:HALO, :] = up_ref[...] * (1.0 - first)
        ext_ref[HALO:, :] = u_ref[...]
        for c0 in range(0, CONV_DIM, CONV_COLS):
            cols = slice(c0, c0 + CONV_COLS)
            taps = [w_ref[k:k + 1, cols] for k in range(CONV_K)]
            bias = b_ref[:, cols]
            acc_b = jnp.zeros((HALO, CONV_COLS), F32)
            acc_w = [jnp.zeros((HALO, CONV_COLS), F32) for _ in range(CONV_K)]
            for r0 in range(0, tm, CONV_ROWS_BLK):
                rows = slice(r0, r0 + CONV_ROWS_BLK)
                us = [ext_ref[pl.ds(HALO - (CONV_K - 1) + k + r0, CONV_ROWS_BLK), cols] for k in range(CONV_K)]
                y = bias + taps[0] * us[0]
                for k in range(1, CONV_K):
                    y += taps[k] * us[k]
                s = _sigmoid(y)
                if c0 < SSD_INNER:
                    dact = dys_ref[rows, cols] * dv_ref[:, cols] + dxb_ref[rows, cols]
                elif c0 < SSD_INNER + 512:
                    dact = dB_ref[rows, c0 - SSD_INNER:c0 - SSD_INNER + CONV_COLS]
                else:
                    dact = dC_ref[rows, c0 - SSD_INNER - 512:c0 - SSD_INNER - 512 + CONV_COLS]
                dy = dact * (s * (1.0 + y * (1.0 - s)))
                dye_ref[rows, cols] = dy
                acc_b += jnp.sum(dy.reshape(CONV_ROWS_BLK // HALO, HALO, CONV_COLS), axis=0)
                for k in range(CONV_K):
                    acc_w[k] += jnp.sum((dy * us[k]).reshape(CONV_ROWS_BLK // HALO, HALO, CONV_COLS), axis=0)
            dcb_ref[:, cols] += jnp.sum(acc_b, axis=0, keepdims=True)
            for k in range(CONV_K):
                dw_ref[k:k + 1, cols] += jnp.sum(acc_w[k], axis=0, keepdims=True)
        for c0 in range(0, CONV_DIM, CONV_COLS):
            cols = slice(c0, c0 + CONV_COLS)
            taps = [w_ref[k:k + 1, cols] for k in range(CONV_K)]
            for r0 in range(0, tm, CONV_ROWS_BLK):
                du = taps[0] * dye_ref[pl.ds(CONV_K - 1 + r0, CONV_ROWS_BLK), cols]
                for k in range(1, CONV_K):
                    du += taps[k] * dye_ref[pl.ds(CONV_K - 1 - k + r0, CONV_ROWS_BLK), cols]
                du_ref[r0:r0 + CONV_ROWS_BLK, cols] = du.astype(BF16)
        dye_ref[tm:, :] = dye_ref[0:HALO, :]
        sg = _sigmoid(dtr_ref[...] + db_ref[...])
        ddtr = ddt_ref[...] * sg
        ddtr_ref[...] = ddtr.astype(BF16)
        ddb_ref[...] += jnp.sum(ddtr, axis=0, keepdims=True)

    prev_spec = pl.BlockSpec((HALO, CONV_DIM), lambda i: (jnp.maximum((n - 1 - i) * hb - 1, 0), 0))
    return _pc(
        body, name="conv_bwd", grid=(n,),
        in_specs=[rev(CONV_DIM), prev_spec, rev(DT_PAD), rev(1024), _const_spec((1, SSD_INNER)), rev(1024), rev(512), rev(512),
                  rev(DT_PAD), _const_spec((CONV_K, CONV_DIM)), _const_spec((1, CONV_DIM)), _const_spec((1, DT_PAD))],
        out_specs=[rev(CONV_DIM), rev(DT_PAD), _const_spec((HALO, CONV_DIM)), _const_spec((1, CONV_DIM)),
                   _const_spec((1, DT_PAD))],
        out_shape=[jax.ShapeDtypeStruct((T, CONV_DIM), BF16), jax.ShapeDtypeStruct((T, DT_PAD), BF16),
                   jax.ShapeDtypeStruct((HALO, CONV_DIM), F32), jax.ShapeDtypeStruct((1, CONV_DIM), F32),
                   jax.ShapeDtypeStruct((1, DT_PAD), F32)],
        scratch_shapes=[pltpu.VMEM((tm + HALO, CONV_DIM), F32), pltpu.VMEM((tm + HALO, CONV_DIM), F32)],
        compiler_params=_cparams(("arbitrary",)),
    )(xbc_raw, xbc_raw, dt_raw, dys, dvec, dxs_b, dB, dC, ddt, conv_w, conv_b, dt_bias)


GROUP_LANES = SSD_HPG * SSD_HEADDIM


def _ssd_expanders():
    head = jnp.arange(DT_PAD)[:, None]
    to_wide = (jnp.arange(SSD_INNER)[None, :] // SSD_HEADDIM == head).astype(BF16)
    return to_wide, to_wide.T


def _ssd_prep(dt_ref, alog_ref, wide_ref):
    q = SSD_CHUNK
    a = -jnp.exp(alog_ref[...])
    dtv = dt_ref[...]
    la = dtv * a
    row = lax.broadcasted_iota(jnp.int32, (q, q), 0)
    col = lax.broadcasted_iota(jnp.int32, (q, q), 1)
    tri = (col <= row).astype(BF16)
    cum = _dot_split_rhs(tri, la, 3)
    cum_t = _dot_split(la, tri, 3, (((0,), (1,)), ((), ())))
    dtw = _dot_split(dtv, wide_ref[...], 2)
    cumw = _dot_split(cum, wide_ref[...], 3)
    return a, dtv, row, col, tri, cum_t, dtw, cumw, cum


def _decay(cum, cum_t, h, keep):
    return jnp.where(keep, jnp.exp(jnp.minimum(cum[:, h:h + 1] - cum_t[h:h + 1, :], 0.0)), 0.0)


def _decay_t(cum, cum_t, h, keep_t):
    return jnp.where(keep_t, jnp.exp(jnp.minimum(cum_t[h:h + 1, :] - cum[:, h:h + 1], 0.0)), 0.0)


def _ssd_fwd(xbc_act, dt, alog):
    T = xbc_act.shape[0]
    q = SSD_CHUNK
    nc = T // q
    to_wide, _ = _ssd_expanders()

    def body(xbc_ref, dt_ref, alog_ref, wide_ref, y_ref, sp_ref, st_ref, xd_ref, xde_ref):
        @pl.when(pl.program_id(0) == 0)
        def _():
            st_ref[...] = jnp.zeros_like(st_ref)

        a, dtv, row, col, tri, cum_t, dtw, cumw, segcol = _ssd_prep(dt_ref, alog_ref, wide_ref)
        clw = cumw[q - 1:q, :]
        ecw = jnp.exp(cumw)
        xd = xbc_ref[:, 0:SSD_INNER] * dtw
        xd_ref[...] = xd.astype(BF16)
        xde_ref[...] = (xd * jnp.exp(clw - cumw)).astype(BF16)
        cdw = jnp.exp(clw)
        keep = col <= row
        sp_ref[0] = st_ref[...]
        for g in range(SSD_GROUPS):
            gl = slice(GROUP_LANES * g, GROUP_LANES * (g + 1))
            bb = xbc_ref[:, 1024 + 128 * g:1152 + 128 * g].astype(BF16)
            cb = xbc_ref[:, 1536 + 128 * g:1664 + 128 * g].astype(BF16)
            gm = _dot_nt(cb, bb)
            stp = st_ref[g]
            yoff = _dot(cb, stp.astype(BF16)) * ecw[:, gl]
            for r in range(SSD_HPG):
                h = SSD_HPG * g + r
                m = (gm * _decay(segcol, cum_t, h, keep)).astype(BF16)
                y_ref[:, 64 * h:64 * h + 64] = _dot(m, xd_ref[:, 64 * h:64 * h + 64]) + yoff[:, 64 * r:64 * r + 64]
            st_ref[g] = stp * cdw[:, gl] + _dot_tn(bb, xde_ref[:, gl])

    return _pc(
        body, name="ssd_fwd", grid=(nc,),
        in_specs=[_row_spec(q, CONV_DIM), _row_spec(q, DT_PAD), _const_spec((1, DT_PAD)),
                  _const_spec(to_wide.shape)],
        out_specs=[_row_spec(q, SSD_INNER),
                   pl.BlockSpec((1, SSD_GROUPS, SSD_STATE, GROUP_LANES), lambda i: (i, 0, 0, 0))],
        out_shape=[jax.ShapeDtypeStruct((T, SSD_INNER), F32),
                   jax.ShapeDtypeStruct((nc, SSD_GROUPS, SSD_STATE, GROUP_LANES), F32)],
        scratch_shapes=[pltpu.VMEM((SSD_GROUPS, SSD_STATE, GROUP_LANES), F32), pltpu.VMEM((q, SSD_INNER), BF16),
                        pltpu.VMEM((q, SSD_INNER), BF16)],
        compiler_params=_cparams(("arbitrary",)),
    )(xbc_act, dt, alog, to_wide)


def _ssd_bwd(xbc_act, dt, alog, sprev, dy):
    T = xbc_act.shape[0]
    q = SSD_CHUNK
    nc = T // q
    to_wide, to_heads = _ssd_expanders()

    def rev(width):
        return pl.BlockSpec((q, width), lambda i: (nc - 1 - i, 0))

    def body(xbc_ref, dt_ref, alog_ref, sp_ref, dy_ref, wide_ref, heads_ref,
             dxs_ref, dB_ref, dC_ref, ddt_ref, dalog_ref, ds_ref, xd_ref, dxd_ref):
        i = pl.program_id(0)

        @pl.when(i == 0)
        def _():
            ds_ref[...] = jnp.zeros_like(ds_ref)
            dalog_ref[...] = jnp.zeros_like(dalog_ref)

        a, dtv, row, col, tri, cum_t, dtw, cumw, segcol = _ssd_prep(dt_ref, alog_ref, wide_ref)
        clw = cumw[q - 1:q, :]
        ecw = jnp.exp(cumw)
        dew = jnp.exp(clw - cumw)
        cdw = jnp.exp(clw)
        xs = xbc_ref[:, 0:SSD_INNER]
        xd = xs * dtw
        xd_ref[...] = xd.astype(BF16)
        dyv = dy_ref[...]
        dye = (dyv * ecw).astype(BF16)
        xde = (xd * dew).astype(BF16)
        keep = col <= row
        keep_t = col >= row
        rows_k = lax.broadcasted_iota(jnp.int32, (SSD_HPG * q, DT_PAD), 0) // q
        lanes_k = lax.broadcasted_iota(jnp.int32, (SSD_HPG * q, DT_PAD), 1)
        dcw_parts = []
        dcum = jnp.zeros((q, DT_PAD), F32)
        for g in range(SSD_GROUPS):
            gl = slice(GROUP_LANES * g, GROUP_LANES * (g + 1))
            bb = xbc_ref[:, 1024 + 128 * g:1152 + 128 * g].astype(BF16)
            cb = xbc_ref[:, 1536 + 128 * g:1664 + 128 * g].astype(BF16)
            gm = _dot_nt(cb, bb)
            gmt = _dot_nt(bb, cb)
            stp = sp_ref[0, g]
            dst = ds_ref[g]
            stpb = stp.astype(BF16)
            dstb = dst.astype(BF16)
            yoff = _dot(cb, stpb) * ecw[:, gl]
            dcg = _dot_nt(dye[:, gl], stpb)
            ds_ref[g] = dst * cdw[:, gl] + _dot_tn(cb, dye[:, gl])
            dlast = jnp.sum(dst * stp, axis=0, keepdims=True) * cdw[:, gl]
            dbg = _dot_nt(xde[:, gl], dstb)
            w = _dot(bb, dstb) * dew[:, gl]
            wx = w * xd[:, gl]
            dlast = dlast + jnp.sum(wx, axis=0, keepdims=True)
            dcw_parts.append(dyv[:, gl] * yoff - wx
                             + jnp.where(lax.broadcasted_iota(jnp.int32, (q, 1), 0) == q - 1, dlast, 0.0))
            dgm = jnp.zeros((q, q), F32)
            diag = []
            for r in range(SSD_HPG):
                h = SSD_HPG * g + r
                hl = slice(64 * h, 64 * h + 64)
                dyb = dy_ref[:, hl].astype(BF16)
                xdh = xd_ref[:, hl]
                dm = _dot_nt(dyb, xdh)
                dmt = _dot_nt(xdh, dyb)
                dec = _decay(segcol, cum_t, h, keep)
                mt = gmt * _decay_t(segcol, cum_t, h, keep_t)
                dgm += dm * dec
                diag.append(dm * (gm * dec) - dmt * mt)
                dxd_ref[:, hl] = _dot(mt.astype(BF16), dyb) + w[:, 64 * r:64 * r + 64]
            onehots = (lanes_k == SSD_HPG * g + rows_k).astype(BF16)
            dcum += _dot_split(jnp.concatenate(diag, axis=1), onehots, 2)
            dgb = dgm.astype(BF16)
            dC_ref[:, 128 * g:128 * g + 128] = dcg + _dot(dgb, bb)
            dB_ref[:, 128 * g:128 * g + 128] = dbg + _dot_tn(dgb, cb)
        dxd = dxd_ref[...]
        dxs_ref[...] = dxd * dtw
        dcum += _dot_split(jnp.concatenate(dcw_parts, axis=1), heads_ref[...], 2)
        dla = _dot_split_rhs(tri, dcum, 3, (((0,), (0,)), ((), ())))
        ddt_ref[...] = _dot_split(xs * dxd, heads_ref[...], 2) + dla * a
        dalog_ref[...] += jnp.sum(dla * dtv, axis=0, keepdims=True)

        @pl.when(i == nc - 1)
        def _():
            dalog_ref[...] = dalog_ref[...] * a

    st_spec = pl.BlockSpec((1, SSD_GROUPS, SSD_STATE, GROUP_LANES), lambda i: (nc - 1 - i, 0, 0, 0))
    return _pc(
        body, name="ssd_bwd", grid=(nc,),
        in_specs=[rev(CONV_DIM), rev(DT_PAD), _const_spec((1, DT_PAD)), st_spec, rev(SSD_INNER),
                  _const_spec(to_wide.shape), _const_spec(to_heads.shape)],
        out_specs=[rev(SSD_INNER), rev(512), rev(512), rev(DT_PAD), _const_spec((1, DT_PAD))],
        out_shape=[jax.ShapeDtypeStruct((T, SSD_INNER), F32), jax.ShapeDtypeStruct((T, 512), F32),
                   jax.ShapeDtypeStruct((T, 512), F32), jax.ShapeDtypeStruct((T, DT_PAD), F32),
                   jax.ShapeDtypeStruct((1, DT_PAD), F32)],
        scratch_shapes=[pltpu.VMEM((SSD_GROUPS, SSD_STATE, GROUP_LANES), F32), pltpu.VMEM((q, SSD_INNER), BF16),
                        pltpu.VMEM((q, SSD_INNER), F32)],
        compiler_params=_cparams(("arbitrary",)),
    )(xbc_act, dt, alog, sprev, dy, to_wide, to_heads)


def _s5_disc_vals(a_re, a_im, log_dt, b_re, b_im):
    dt = jnp.exp(log_dt)
    mag = jnp.exp(a_re * dt)
    ab_re = mag * jnp.cos(a_im * dt)
    ab_im = mag * jnp.sin(a_im * dt)
    den = a_re * a_re + a_im * a_im
    nr = ab_re - 1.0
    ni = ab_im
    coef_re = (nr * a_re + ni * a_im) / den
    coef_im = (ni * a_re - nr * a_im) / den
    bb_re = coef_re * b_re - coef_im * b_im
    bb_im = coef_re * b_im + coef_im * b_re
    return ab_re, ab_im, bb_re, bb_im


def _s5_disc(a_re, a_im, log_dt, b_re, b_im):
    def body(ar, ai, ld, br, bi, o1, o2, o3, o4):
        o1[...], o2[...], o3[...], o4[...] = _s5_disc_vals(ar[...], ai[...], ld[...], br[...], bi[...])

    return _pc(
        body, name="s5_disc",
        out_shape=[jax.ShapeDtypeStruct((1, S5_STATES), F32), jax.ShapeDtypeStruct((1, S5_STATES), F32),
                   jax.ShapeDtypeStruct((16, S5_STATES), F32), jax.ShapeDtypeStruct((16, S5_STATES), F32)],
    )(a_re, a_im, log_dt, b_re, b_im)


def _s5_disc_bwd(a_re, a_im, log_dt, b_re, b_im, d_ab_re, d_ab_im, d_bb_re, d_bb_im):
    def body(ar, ai, ld, br, bi, g1, g2, g3, g4, o1, o2, o3, o4, o5):
        _, vjp = jax.vjp(_s5_disc_vals, ar[...], ai[...], ld[...], br[...], bi[...])
        d1, d2, d3, d4, d5 = vjp((g1[...], g2[...], g3[...], g4[...]))
        o1[...] = d1
        o2[...] = d2
        st = lax.broadcasted_iota(jnp.int32, (S5_STATES, DT_PAD), 0)
        grp = lax.broadcasted_iota(jnp.int32, (S5_STATES, DT_PAD), 1)
        sel = (st // 64 == grp).astype(F32)
        o3[...] = _dot_hi(d3, sel)
        o4[...] = d4
        o5[...] = d5

    return _pc(
        body, name="s5_disc_bwd",
        out_shape=[jax.ShapeDtypeStruct((1, S5_STATES), F32), jax.ShapeDtypeStruct((1, S5_STATES), F32),
                   jax.ShapeDtypeStruct((1, DT_PAD), F32),
                   jax.ShapeDtypeStruct((16, S5_STATES), F32), jax.ShapeDtypeStruct((16, S5_STATES), F32)],
    )(a_re, a_im, log_dt, b_re, b_im, d_ab_re, d_ab_im, d_bb_re, d_bb_im)


def _cmul_add(xr, xi, pr, pi, yr, yi):
    return xr + pr * yr - pi * yi, xi + pr * yi + pi * yr


def _powers(ar, ai, n):
    out = [(ar, ai)]
    for _ in range(n - 1):
        pr, pi = out[-1]
        out.append((pr * pr - pi * pi, 2.0 * pr * pi))
    return out


_BW = S5_STATES // S5_BLOCKS
_BI = S5_WIDTH // S5_BLOCKS
SUB = 8
S5_ROWS = S5_CHUNK // SUB


S5_TAB_ROWS = 8 * SUB


def _scan8(br, bi, tab_ref, reverse):
    for level, k in enumerate((1, 2, 4)):
        r0 = 2 * SUB * (level + 1)
        shift = SUB - k if reverse else k
        br, bi = _cmul_add(br, bi, tab_ref[r0:r0 + SUB, :], tab_ref[r0 + SUB:r0 + 2 * SUB, :],
                           pltpu.roll(br, shift, 0), pltpu.roll(bi, shift, 0))
    return br, bi


def _s5_tables(ab_ref, tab_ref, reverse):
    rowin = lax.broadcasted_iota(jnp.int32, (SUB, 1), 0)
    ar = ab_ref[0:1, :]
    ai = -ab_ref[1:2, :] if reverse else ab_ref[1:2, :]
    zero = jnp.zeros((SUB, S5_STATES), F32)
    for level, (pr, pi) in enumerate(_powers(ar, ai, 3)):
        k = 2 ** level
        keep = (rowin < SUB - k) if reverse else (rowin >= k)
        r0 = 2 * SUB * (level + 1)
        tab_ref[r0:r0 + SUB, :] = jnp.where(keep, pr, 0.0) + zero
        tab_ref[r0 + SUB:r0 + 2 * SUB, :] = jnp.where(keep, pi, 0.0) + zero
    hit = rowin == (SUB - 1 if reverse else 0)
    pr, pi = _scan8(jnp.where(hit, ar, 0.0) + zero, jnp.where(hit, ai, 0.0) + zero, tab_ref, reverse)
    tab_ref[0:SUB, :] = pr
    tab_ref[SUB:2 * SUB, :] = pi


def _s5_fwd_chunk(u_ref, y_ref, r0, wb_ref, wc_ref, d_ref, carry_ref, tab_ref, sr_ref, si_ref, between):
    q = S5_CHUNK
    rows = slice(r0, r0 + q)
    for j in range(S5_BLOCKS):
        bu = _dot(u_ref[rows, _BI * j:_BI * (j + 1)].astype(BF16), wb_ref[j])
        sr_ref[:, :, _BW * j:_BW * (j + 1)] = bu[:, :_BW].reshape(S5_ROWS, SUB, _BW)
        si_ref[:, :, _BW * j:_BW * (j + 1)] = bu[:, _BW:].reshape(S5_ROWS, SUB, _BW)
    tr, ti = tab_ref[0:SUB, :], tab_ref[SUB:2 * SUB, :]
    cr, ci = carry_ref[0:1, :], carry_ref[1:2, :]
    for k in range(S5_ROWS):
        sr, si = _scan8(sr_ref[k], si_ref[k], tab_ref, False)
        sr, si = _cmul_add(sr, si, tr, ti, cr, ci)
        sr_ref[k] = sr
        si_ref[k] = si
        cr, ci = sr[SUB - 1:SUB, :], si[SUB - 1:SUB, :]
        between()
    carry_ref[0:1, :] = cr
    carry_ref[1:2, :] = ci
    for j in range(S5_BLOCKS):
        sl = slice(_BW * j, _BW * (j + 1))
        ul = slice(_BI * j, _BI * (j + 1))
        s = jnp.concatenate([sr_ref[:, :, sl].reshape(q, _BW), si_ref[:, :, sl].reshape(q, _BW)], axis=1).astype(BF16)
        y_ref[rows, ul] = _dot(s, wc_ref[j]) + d_ref[:, ul] * u_ref[rows, ul]


def _inproj_s5_fwd(x, g, wp, wb4, wc4, ab, dvec, rider=None):
    T = x.shape[0]
    tm = TOKEN_TILE
    per_tile = tm // S5_CHUNK
    nc = T // S5_CHUNK

    def body(x_ref, g_ref, w_hbm, wb_ref, wc_ref, ab_ref, d_ref, z_ref, xbc_ref, u5_ref, gt_ref, dt_ref, h_ref, y_ref, sp_ref,
             w_ref, carry_ref, tab_ref, sr_ref, si_ref):
        @pl.when(pl.program_id(0) == 0)
        def _():
            pltpu.sync_copy(w_hbm, w_ref)
            carry_ref[...] = jnp.zeros_like(carry_ref)
            _s5_tables(ab_ref, tab_ref, False)

        xn, _ = _rms(x_ref[...])
        h = (xn * g_ref[...]).astype(BF16)
        h_ref[...] = h
        u5_ref[...] = _dot(h, w_ref[:, P_U5:P_G])
        pieces = [(z_ref, P_Z, c0) for c0 in range(0, P_XBC - P_Z, INPROJ_PIECE)]
        pieces += [(xbc_ref, P_XBC, c0) for c0 in range(0, P_U5 - P_XBC, INPROJ_PIECE)]
        pieces += [(gt_ref, P_G, c0) for c0 in range(0, P_DT - P_G, INPROJ_PIECE)]
        todo = iter(pieces)
        slabs, calls = per_tile * S5_ROWS, [0]

        def between():
            calls[0] += 1
            if (calls[0] * len(pieces)) // slabs > ((calls[0] - 1) * len(pieces)) // slabs:
                o_ref, base, c0 = next(todo)
                o_ref[:, c0:c0 + INPROJ_PIECE] = _dot(h, w_ref[:, base + c0:base + c0 + INPROJ_PIECE])

        dt_ref[...] = _dot(h, w_ref[:, P_DT:P_END])
        for c in range(per_tile):
            sp_ref[c] = carry_ref[...]
            _s5_fwd_chunk(u5_ref, y_ref, S5_CHUNK * c, wb_ref, wc_ref, d_ref, carry_ref, tab_ref, sr_ref, si_ref, between)
        assert next(todo, None) is None

    widths = (1024, 2048, 512, 2048, DT_PAD)
    return _call(
        body, rider, name="inproj_s5_fwd", grid=(T // tm,),
        in_specs=[_row_spec(tm, D_MODEL), _const_spec((1, D_MODEL)), _hbm_spec(),
                  _const_spec((S5_BLOCKS, _BI, 2 * _BW)), _const_spec((S5_BLOCKS, 2 * _BW, _BI)),
                  _const_spec((8, S5_STATES)), _const_spec((1, S5_WIDTH))],
        out_specs=[_row_spec(tm, w) for w in widths] + [_row_spec(tm, D_MODEL), _row_spec(tm, S5_WIDTH),
                                                         pl.BlockSpec((per_tile, 8, S5_STATES), lambda i: (i, 0, 0))],
        out_shape=[jax.ShapeDtypeStruct((T, w), F32) for w in widths] + [
            jax.ShapeDtypeStruct((T, D_MODEL), BF16), jax.ShapeDtypeStruct((T, S5_WIDTH), F32),
            jax.ShapeDtypeStruct((nc, 8, S5_STATES), F32)],
        scratch_shapes=[pltpu.VMEM((D_MODEL, P_END), BF16), pltpu.VMEM((8, S5_STATES), F32),
                        pltpu.VMEM((S5_TAB_ROWS, S5_STATES), F32), pltpu.VMEM((S5_ROWS, SUB, S5_STATES), F32),
                        pltpu.VMEM((S5_ROWS, SUB, S5_STATES), F32)],
        compiler_params=_cparams(("arbitrary",)),
    )(x, g, wp, wb4, wc4, ab, dvec)


def _s5_bwd(u5, dy5, wb4, wc4, ab, dvec, sprev, rider=None):
    T = u5.shape[0]
    q = S5_CHUNK
    nc = T // q

    def rev(width):
        return pl.BlockSpec((q, width), lambda i: (nc - 1 - i, 0))

    def body(u_ref, dy_ref, wb_ref, wc_ref, ab_ref, d_ref, sp_ref, du_ref, dwb_ref, dwc_ref, dab_ref, dd_ref,
             carry_ref, tab_ref, rtab_ref, sr_ref, si_ref, lr_ref, li_ref):
        i = pl.program_id(0)
        rowin = lax.broadcasted_iota(jnp.int32, (SUB, 1), 0)

        @pl.when(i == 0)
        def _():
            carry_ref[...] = jnp.zeros_like(carry_ref)
            dwb_ref[...] = jnp.zeros_like(dwb_ref)
            dwc_ref[...] = jnp.zeros_like(dwc_ref)
            dab_ref[...] = jnp.zeros_like(dab_ref)
            dd_ref[...] = jnp.zeros_like(dd_ref)
            _s5_tables(ab_ref, tab_ref, False)
            _s5_tables(ab_ref, rtab_ref, True)

        for j in range(S5_BLOCKS):
            sl = slice(_BW * j, _BW * (j + 1))
            ul = slice(_BI * j, _BI * (j + 1))
            bu = _dot(u_ref[:, ul].astype(BF16), wb_ref[j])
            sr_ref[:, :, sl] = bu[:, :_BW].reshape(S5_ROWS, SUB, _BW)
            si_ref[:, :, sl] = bu[:, _BW:].reshape(S5_ROWS, SUB, _BW)
            ds = _dot_nt(dy_ref[:, ul].astype(BF16), wc_ref[j])
            lr_ref[:, :, sl] = ds[:, :_BW].reshape(S5_ROWS, SUB, _BW)
            li_ref[:, :, sl] = ds[:, _BW:].reshape(S5_ROWS, SUB, _BW)
        ar, ai = ab_ref[0:1, :], ab_ref[1:2, :]
        tr, ti = tab_ref[0:SUB, :], tab_ref[SUB:2 * SUB, :]
        cr, ci = sp_ref[0, 0:1, :], sp_ref[0, 1:2, :]
        for k in range(S5_ROWS):
            sr, si = _scan8(sr_ref[k], si_ref[k], tab_ref, False)
            sr, si = _cmul_add(sr, si, tr, ti, cr, ci)
            sr_ref[k] = sr
            si_ref[k] = si
            cr, ci = sr[SUB - 1:SUB, :], si[SUB - 1:SUB, :]
        tr, ti = rtab_ref[0:SUB, :], rtab_ref[SUB:2 * SUB, :]
        cr, ci = carry_ref[0:1, :], carry_ref[1:2, :]
        acc_r = jnp.zeros((SUB, S5_STATES), F32)
        acc_i = jnp.zeros((SUB, S5_STATES), F32)
        for k in reversed(range(S5_ROWS)):
            lr, li = _scan8(lr_ref[k], li_ref[k], rtab_ref, True)
            lr, li = _cmul_add(lr, li, tr, ti, cr, ci)
            lr_ref[k] = lr
            li_ref[k] = li
            cr, ci = lr[0:1, :], li[0:1, :]
            if k > 0:
                before_r, before_i = sr_ref[k - 1, SUB - 1:SUB, :], si_ref[k - 1, SUB - 1:SUB, :]
            else:
                before_r, before_i = sp_ref[0, 0:1, :], sp_ref[0, 1:2, :]
            keep = rowin >= 1
            pr = jnp.where(keep, pltpu.roll(sr_ref[k], 1, 0), before_r)
            pi = jnp.where(keep, pltpu.roll(si_ref[k], 1, 0), before_i)
            acc_r += lr * pr + li * pi
            acc_i += li * pr - lr * pi
        carry_ref[0:1, :] = cr
        carry_ref[1:2, :] = ci
        dab_ref[0:1, :] += jnp.sum(acc_r, axis=0, keepdims=True)
        dab_ref[1:2, :] += jnp.sum(acc_i, axis=0, keepdims=True)
        for j in range(S5_BLOCKS):
            sl = slice(_BW * j, _BW * (j + 1))
            ul = slice(_BI * j, _BI * (j + 1))
            u = u_ref[:, ul]
            dy = dy_ref[:, ul]
            dyb = dy.astype(BF16)
            lam = jnp.concatenate([lr_ref[:, :, sl].reshape(q, _BW), li_ref[:, :, sl].reshape(q, _BW)], axis=1).astype(BF16)
            s = jnp.concatenate([sr_ref[:, :, sl].reshape(q, _BW), si_ref[:, :, sl].reshape(q, _BW)], axis=1).astype(BF16)
            du_ref[:, ul] = (_dot_nt(lam, wb_ref[j]) + d_ref[:, ul] * dy).astype(BF16)
            dwb_ref[j] += _dot_tn(u.astype(BF16), lam)
            dwc_ref[j] += _dot_tn(s, dyb)
            dd_ref[:, ul] += jnp.sum(dy * u, axis=0, keepdims=True)

    big = pltpu.VMEM((S5_ROWS, SUB, S5_STATES), F32)
    return _call(
        body, rider, name="s5_bwd", grid=(nc,),
        in_specs=[rev(S5_WIDTH), rev(S5_WIDTH), _const_spec((S5_BLOCKS, _BI, 2 * _BW)), _const_spec((S5_BLOCKS, 2 * _BW, _BI)),
                  _const_spec((8, S5_STATES)), _const_spec((1, S5_WIDTH)),
                  pl.BlockSpec((1, 8, S5_STATES), lambda i: (nc - 1 - i, 0, 0))],
        out_specs=[rev(S5_WIDTH), _const_spec((S5_BLOCKS, _BI, 2 * _BW)), _const_spec((S5_BLOCKS, 2 * _BW, _BI)),
                   _const_spec((8, S5_STATES)), _const_spec((1, S5_WIDTH))],
        out_shape=[jax.ShapeDtypeStruct((T, S5_WIDTH), BF16), jax.ShapeDtypeStruct((S5_BLOCKS, _BI, 2 * _BW), F32),
                   jax.ShapeDtypeStruct((S5_BLOCKS, 2 * _BW, _BI), F32), jax.ShapeDtypeStruct((8, S5_STATES), F32),
                   jax.ShapeDtypeStruct((1, S5_WIDTH), F32)],
        scratch_shapes=[pltpu.VMEM((8, S5_STATES), F32), pltpu.VMEM((S5_TAB_ROWS, S5_STATES), F32),
                        pltpu.VMEM((S5_TAB_ROWS, S5_STATES), F32), big, big, big, big],
        compiler_params=_cparams(("arbitrary",)),
    )(u5, dy5, wb4, wc4, ab, dvec, sprev)


def _merge_vals(ys, xs, z, y5, gates, dvec, gssd, glu_w, glu_b, wbr):
    sz = _sigmoid(z)
    qv = ys + dvec * xs
    pre = qv * (z * sz)
    yn, rs = [], []
    for gi in range(SSD_GROUPS):
        p, r = _rms(pre[:, 256 * gi:256 * (gi + 1)])
        yn.append(p)
        rs.append(r)
    yn = jnp.concatenate(yn, axis=1)
    ya = yn * gssd
    pa = _dot(ya.astype(BF16), wbr[0:SSD_INNER, :])
    gel = _gelu(y5)
    pre_g = _dot(gel.astype(BF16), glu_w)
    s0 = _sigmoid(gates[:, :D_MODEL])
    s1 = _sigmoid(gates[:, D_MODEL:])
    sg = _sigmoid(pre_g + glu_b)
    yb = gel * sg
    pb = _dot(yb.astype(BF16), wbr[SSD_INNER:, :])
    merged = s0 * pa + s1 * pb
    return dict(sz=sz, qv=qv, yn=yn, rs=rs, ya=ya, gel=gel, sg=sg, yb=yb, pa=pa, pb=pb, s0=s0, s1=s1, merged=merged)


def _merge_specs(tm):
    acts = [_row_spec(tm, 1024), _row_spec(tm, 1024, 0), _row_spec(tm, 1024), _row_spec(tm, 512), _row_spec(tm, 2048),
            _row_spec(tm, 1024)]
    params = [_const_spec((1, 1024)), _const_spec((1, 1024)), _const_spec((512, 512)), _const_spec((1, 512)), _hbm_spec()]
    return acts, params


def _merge_fwd(ys, xbc_act, z, y5, gates, x, dvec, gssd, glu_w, glu_b, wa):
    T = x.shape[0]
    tm = TOKEN_TILE
    acts, params = _merge_specs(tm)

    def body(ys_ref, xs_ref, z_ref, y5_ref, gt_ref, x_ref, dv_ref, gs_ref, gw_ref, gb_ref, wa_hbm, x1_ref,
             wbr_ref, wout_ref):
        @pl.when(pl.program_id(0) == 0)
        def _():
            _load_late_weight(wa_hbm, wbr_ref, "w_branch")
            _load_late_weight(wa_hbm, wout_ref, "w_out")

        v = _merge_vals(ys_ref[...], xs_ref[...], z_ref[...], y5_ref[...], gt_ref[...], dv_ref[...], gs_ref[...],
                        gw_ref[...], gb_ref[...], wbr_ref)
        x1_ref[...] = x_ref[...] + _dot(v["merged"].astype(BF16), wout_ref[...])

    return _pc(
        body, name="merge_fwd", grid=(T // tm,),
        in_specs=acts + params, out_specs=_row_spec(tm, 1024),
        out_shape=jax.ShapeDtypeStruct((T, 1024), F32),
        scratch_shapes=[pltpu.VMEM((1536, 1024), BF16), pltpu.VMEM((1024, 1024), BF16)],
        compiler_params=_cparams(("arbitrary",)),
    )(ys, xbc_act, z, y5, gates, x, dvec, gssd, glu_w, glu_b, wa)


def _merge_bwd(ys, xbc_act, z, y5, gates, dx1, dvec, gssd, glu_w, glu_b, wa, head_sel, rider=None):
    T = dx1.shape[0]
    tm = TOKEN_TILE
    acts, params = _merge_specs(tm)

    def body(ys_ref, xs_ref, z_ref, y5_ref, gt_ref, dx1_ref, dv_ref, gs_ref, gw_ref, gb_ref, wa_hbm, hs_ref,
             dys_ref, dz_ref, dy5_ref, dgt_ref, mg_ref, ya_ref, yb_ref, dpa_ref, dpb_ref, gel_ref, dpre_ref,
             ddv_ref, dgs_ref, dgb_ref, wbr_ref, wout_ref, ddacc_ref):
        i = pl.program_id(0)

        @pl.when(i == 0)
        def _():
            _load_late_weight(wa_hbm, wbr_ref, "w_branch")
            _load_late_weight(wa_hbm, wout_ref, "w_out")
            ddacc_ref[...] = jnp.zeros_like(ddacc_ref)
            dgs_ref[...] = jnp.zeros_like(dgs_ref)
            dgb_ref[...] = jnp.zeros_like(dgb_ref)

        ys, xs, z, y5, gates = ys_ref[...], xs_ref[...], z_ref[...], y5_ref[...], gt_ref[...]
        dvv, gsv, gw = dv_ref[...], gs_ref[...], gw_ref[...]
        v = _merge_vals(ys, xs, z, y5, gates, dvv, gsv, gw, gb_ref[...], wbr_ref)
        dmg = _dot_nt(dx1_ref[...].astype(BF16), wout_ref[...])
        s0, s1, pa, pb = v["s0"], v["s1"], v["pa"], v["pb"]
        dgt_ref[:, :D_MODEL] = (dmg * pa * s0 * (1.0 - s0)).astype(BF16)
        dgt_ref[:, D_MODEL:] = (dmg * pb * s1 * (1.0 - s1)).astype(BF16)
        dpa = (dmg * s0).astype(BF16)
        dpb = (dmg * s1).astype(BF16)
        dya = _dot_nt(dpa, wbr_ref[0:SSD_INNER, :])
        dyb = _dot_nt(dpb, wbr_ref[SSD_INNER:, :])
        gel, sg = v["gel"], v["sg"]
        dpre = (dyb * gel * sg * (1.0 - sg))
        dgb_ref[...] += jnp.sum(dpre, axis=0, keepdims=True)
        dpre_b = dpre.astype(BF16)
        dgel = dyb * sg + _dot_nt(dpre_b, gw)
        dy5_ref[...] = dgel * _gelu_grad(y5)
        yn = v["yn"]
        dgs_ref[...] += jnp.sum(dya * yn, axis=0, keepdims=True)
        dyn = dya * gsv
        dpre_a = jnp.concatenate(
            [_rms_bwd(yn[:, 256 * gi:256 * (gi + 1)], v["rs"][gi], dyn[:, 256 * gi:256 * (gi + 1)])
             for gi in range(SSD_GROUPS)], axis=1)
        sz, qv = v["sz"], v["qv"]
        dq = dpre_a * (z * sz)
        dz_ref[...] = (dpre_a * qv * (sz * (1.0 + z * (1.0 - sz)))).astype(BF16)
        dys_ref[...] = dq
        ddacc_ref[...] += jnp.sum(dq * xs, axis=0, keepdims=True)
        mg_ref[...] = v["merged"].astype(BF16)
        ya_ref[...] = v["ya"].astype(BF16)
        yb_ref[...] = v["yb"].astype(BF16)
        dpa_ref[...] = dpa
        dpb_ref[...] = dpb
        gel_ref[...] = gel.astype(BF16)
        dpre_ref[...] = dpre_b

        @pl.when(i == pl.num_programs(0) - 1)
        def _():
            ddv_ref[...] = _dot_hi(ddacc_ref[...], hs_ref[...])

    outs = [(1024, F32), (1024, BF16), (512, F32), (2048, BF16),
            (1024, BF16), (1024, BF16), (512, BF16), (1024, BF16), (1024, BF16), (512, BF16), (512, BF16)]
    return _call(
        body, rider, name="merge_bwd", grid=(T // tm,),
        in_specs=acts + params + [_const_spec((1024, DT_PAD))],
        out_specs=[_row_spec(tm, w) for w, _ in outs] + [_const_spec((1, DT_PAD)), _const_spec((1, 1024)), _const_spec((1, 512))],
        out_shape=[jax.ShapeDtypeStruct((T, w), d) for w, d in outs] + [
            jax.ShapeDtypeStruct((1, DT_PAD), F32), jax.ShapeDtypeStruct((1, 1024), F32), jax.ShapeDtypeStruct((1, 512), F32)],
        scratch_shapes=[pltpu.VMEM((1536, 1024), BF16), pltpu.VMEM((1024, 1024), BF16), pltpu.VMEM((1, 1024), F32)],
        compiler_params=_cparams(("arbitrary",)),
    )(ys, xbc_act, z, y5, gates, dx1, dvec, gssd, glu_w, glu_b, wa, head_sel)


def _mlp_fwd_loss(x1, target, g, g_fin, wa):
    T = x1.shape[0]
    tm = TOKEN_TILE

    def body(x_ref, t_ref, g_ref, gf_ref, wa_hbm, dx_ref, loss_ref, dg_ref, w1_ref, w2_ref):
        @pl.when(pl.program_id(0) == 0)
        def _():
            _load_late_weight(wa_hbm, w1_ref, "w_mlp_in")
            _load_late_weight(wa_hbm, w2_ref, "w_mlp_out")
            loss_ref[...] = jnp.zeros_like(loss_ref)
            dg_ref[...] = jnp.zeros_like(dg_ref)

        xv = x_ref[...]
        xn, _ = _rms(xv)
        h = (xn * g_ref[...]).astype(BF16)
        acc = xv
        for s in range(FF_SHARDS):
            rl = jnp.maximum(_dot(h, w1_ref[s]), 0.0)
            acc += _dot((rl * rl).astype(BF16), w2_ref[FF_SHARD * s:FF_SHARD * (s + 1), :])
        yn, r = _rms(acc)
        gv = gf_ref[...]
        err = yn * gv - t_ref[...]
        loss_ref[...] += jnp.sum(err * err, axis=0, keepdims=True) * (0.5 / D_MODEL)
        dy = err * (1.0 / D_MODEL)
        dg_ref[...] += jnp.sum(dy * yn, axis=0, keepdims=True)
        dx_ref[...] = _rms_bwd(yn, r, dy * gv)

    return _pc(
        body, name="mlp_fwd_loss", grid=(T // tm,),
        in_specs=[_row_spec(tm, 1024), _row_spec(tm, 1024), _const_spec((1, 1024)), _const_spec((1, 1024)), _hbm_spec()],
        out_specs=[_row_spec(tm, 1024), _const_spec((1, 1024)), _const_spec((1, 1024))],
        out_shape=[jax.ShapeDtypeStruct((T, 1024), F32), jax.ShapeDtypeStruct((1, 1024), F32),
                   jax.ShapeDtypeStruct((1, 1024), F32)],
        scratch_shapes=[pltpu.VMEM((FF_SHARDS, D_MODEL, FF_SHARD), BF16), pltpu.VMEM((D_FF, D_MODEL), BF16)],
        compiler_params=_cparams(("arbitrary",)),
    )(x1, target, g, g_fin, wa)


def _mlp_bwd(x1, dx2, g, wa):
    T = x1.shape[0]
    tm = TOKEN_TILE

    def body(x_ref, dx2_ref, g_ref, wa_hbm, dx1_ref, h_ref, act_ref, da_ref, dg_ref, w1_ref, w2_ref):
        @pl.when(pl.program_id(0) == 0)
        def _():
            _load_late_weight(wa_hbm, w1_ref, "w_mlp_in")
            _load_late_weight(wa_hbm, w2_ref, "w_mlp_out")
            dg_ref[...] = jnp.zeros_like(dg_ref)

        xn, r = _rms(x_ref[...])
        gv = g_ref[...]
        h = (xn * gv).astype(BF16)
        h_ref[...] = h
        dx2 = dx2_ref[...]
        dx2b = dx2.astype(BF16)
        dh = jnp.zeros((tm, D_MODEL), F32)
        for s in range(FF_SHARDS):
            ff = slice(FF_SHARD * s, FF_SHARD * (s + 1))
            rl = jnp.maximum(_dot(h, w1_ref[s]), 0.0)
            act_ref[:, ff] = (rl * rl).astype(BF16)
            da = (_dot_nt(dx2b, w2_ref[ff, :]) * (2.0 * rl)).astype(BF16)
            da_ref[:, ff] = da
            dh += _dot_nt(da, w1_ref[s])
        dg_ref[...] += jnp.sum(dh * xn, axis=0, keepdims=True)
        dx1_ref[...] = dx2 + _rms_bwd(xn, r, dh * gv)

    return _pc(
        body, name="mlp_bwd", grid=(T // tm,),
        in_specs=[_row_spec(tm, 1024), _row_spec(tm, 1024), _const_spec((1, 1024)), _hbm_spec()],
        out_specs=[_row_spec(tm, 1024), _row_spec(tm, 1024), _row_spec(tm, D_FF), _row_spec(tm, D_FF), _const_spec((1, 1024))],
        out_shape=[jax.ShapeDtypeStruct((T, 1024), F32), jax.ShapeDtypeStruct((T, 1024), BF16),
                   jax.ShapeDtypeStruct((T, D_FF), BF16), jax.ShapeDtypeStruct((T, D_FF), BF16),
                   jax.ShapeDtypeStruct((1, 1024), F32)],
        scratch_shapes=[pltpu.VMEM((FF_SHARDS, D_MODEL, FF_SHARD), BF16), pltpu.VMEM((D_FF, D_MODEL), BF16)],
        compiler_params=_cparams(("arbitrary",)),
    )(x1, dx2, g, wa)


WGRAD_OUT_ELEMS = 2 * 1024 * 1024
WGRAD_TILE_BYTES = 8 * 1024 * 1024


def _wgrad(a, b, name, col_shards=None, row_shards_into=None):
    T, K = a.shape
    N = b.shape[1]
    nb = N // col_shards if col_shards else min(N, max(128, WGRAD_OUT_ELEMS // K))
    tt = min(T, WGRAD_TOKENS)
    while tt * max(K * a.dtype.itemsize, nb * b.dtype.itemsize) > WGRAD_TILE_BYTES:
        tt //= 2
    assert N % nb == 0 and T % tt == 0
    in_specs = [pl.BlockSpec((tt, K), lambda n, t: (t, 0)), pl.BlockSpec((tt, nb), lambda n, t: (t, n))]
    args, aliases = [a, b], {}
    if col_shards:
        out_spec = pl.BlockSpec((None, None, K, nb), lambda n, t: (n, 0, 0, 0))
        out_shape = jax.ShapeDtypeStruct((col_shards, 2, K, nb), F32)
    elif row_shards_into is not None:
        shards, _, rows, cols = row_shards_into.shape
        assert shards * rows == K and cols == N
        out_spec = pl.BlockSpec((shards, None, rows, nb), lambda n, t: (0, 1, 0, n))
        out_shape = jax.ShapeDtypeStruct(row_shards_into.shape, F32)
        in_specs.append(_hbm_spec())
        args.append(row_shards_into)
        aliases = {2: 0}
    else:
        out_spec = pl.BlockSpec((K, nb), lambda n, t: (0, n))
        out_shape = jax.ShapeDtypeStruct((K, N), F32)

    def body(a_ref, b_ref, *rest):
        o_ref = rest[-1]

        @pl.when(pl.program_id(1) == 0)
        def _():
            o_ref[...] = jnp.zeros_like(o_ref)

        o_ref[...] += _dot_tn(a_ref[...].astype(BF16), b_ref[...].astype(BF16)).reshape(o_ref.shape)

    return _pc(
        body, name=name, grid=(N // nb, T // tt), in_specs=in_specs, out_specs=out_spec, out_shape=out_shape,
        input_output_aliases=aliases, compiler_params=_cparams(("parallel", "arbitrary")),
    )(*args)


def _s5_block_weights(bb_re, bb_im, c_re, c_im):
    eye = jnp.eye(8, dtype=F32)
    bre = bb_re.reshape(16, S5_BLOCKS, 8, 64)
    bim = bb_im.reshape(16, S5_BLOCKS, 8, 64)
    wb_re = jnp.einsum('kjgp,gh->jhkgp', bre, eye).reshape(S5_BLOCKS, _BI, _BW)
    wb_im = jnp.einsum('kjgp,gh->jhkgp', bim, eye).reshape(S5_BLOCKS, _BI, _BW)
    wb4 = jnp.concatenate([wb_re, wb_im], axis=2).astype(BF16)
    cre = c_re.reshape(S5_BLOCKS, 8, 16, 64)
    cim = c_im.reshape(S5_BLOCKS, 8, 16, 64)
    wc_re = jnp.einsum('jgkp,gh->jgphk', cre, eye).reshape(S5_BLOCKS, _BW, _BI)
    wc_im = jnp.einsum('jgkp,gh->jgphk', -cim, eye).reshape(S5_BLOCKS, _BW, _BI)
    wc4 = jnp.concatenate([wc_re, wc_im], axis=1).astype(BF16)
    return wb4, wc4


def _s5_block_grads(dwb4, dwc4):
    eye = jnp.eye(8, dtype=F32)
    dwb = dwb4.reshape(S5_BLOCKS, 8, 16, 2, 8, 64)
    dbb = jnp.einsum('jhkrgp,gh->rkjgp', dwb, eye).reshape(2, 16, S5_STATES)
    dwc = dwc4.reshape(S5_BLOCKS, 2, 8, 64, 8, 16)
    dc = jnp.einsum('jrgphk,gh->rjgkp', dwc, eye).reshape(2, 32, 16, 64)
    return dbb[0], dbb[1], dc[0], -dc[1]


def _row(v, width=None):
    v = v.reshape(1, -1)
    if width is not None and v.shape[1] < width:
        v = jnp.concatenate([v, jnp.zeros((1, width - v.shape[1]), v.dtype)], axis=1)
    return v


def _local_step(x, target, p, comm=None):
    g_mix, g_mlp, g_fin = _row(p["norm_mix_g"]), _row(p["norm_mlp_g"]), _row(p["norm_final_g"])
    conv_b = _row(p["conv_b"])
    dt_bias = _row(p["dt_bias"], DT_PAD)
    alog = _row(p["a_log"], DT_PAD)
    dvec = _row(jnp.repeat(p["d_ssd"], SSD_HEADDIM))
    gssd = _row(p["ssd_norm_g"])
    s5d = _row(p["s5_d"])
    glu_b = _row(p["s5_glu_b"])
    head_sel = (jnp.arange(SSD_INNER)[:, None] // SSD_HEADDIM == jnp.arange(DT_PAD)[None, :]).astype(F32)

    a_re = p["s5_a_re"].reshape(1, S5_STATES)
    a_im = p["s5_a_im"].reshape(1, S5_STATES)
    log_dt = jnp.repeat(p["s5_log_dt"], 64).reshape(1, S5_STATES)
    b_re = p["s5_b_re"].reshape(S5_STATES, 16).T
    b_im = p["s5_b_im"].reshape(S5_STATES, 16).T
    ab_re, ab_im, bb_re, bb_im = _s5_disc(a_re, a_im, log_dt, b_re, b_im)
    wb4, wc4 = _s5_block_weights(bb_re, bb_im, p["s5_c_re"], p["s5_c_im"])
    ab = jnp.concatenate([ab_re, ab_im, jnp.zeros((6, S5_STATES), F32)], axis=0)

    wp = p["w_in_perm"]

    first_args = (x, g_mix, wp, wb4, wc4, ab, s5d)
    if comm is None:
        z, xbc_raw, u5, gates, dt_raw, h, y5, s5_states = _inproj_s5_fwd(*first_args)
    else:
        first_out, late = _inproj_s5_fwd(*first_args, rider=_Gather(comm["late_srcs"], comm["late_ks"]))
        z, xbc_raw, u5, gates, dt_raw, h, y5, s5_states = first_out
        p = {**p, **comm["late_unpack"](late)}
    xbc_act, dt = _conv_fwd(xbc_raw, dt_raw, p["conv_w"], conv_b, dt_bias)
    ys, ssd_states = _ssd_fwd(xbc_act, dt, alog)
    wa, glu_w = p["late_weights"], p["s5_glu_w"]
    x1 = _merge_fwd(ys, xbc_act, z, y5, gates, x, dvec, gssd, glu_w, glu_b, wa)
    dx2, loss_lanes, d_gfin = _mlp_fwd_loss(x1, target, g_mlp, g_fin, wa)

    dx1, h2, act, da1, d_gmlp = _mlp_bwd(x1, dx2, g_mlp, wa)
    g_mlp4 = _wgrad(h2, da1, "wgrad_mlp_in", col_shards=FF_SHARDS)
    g_mlp4 = _wgrad(act, dx2, "wgrad_mlp_out", row_shards_into=g_mlp4)
    d_w_mlp_in, d_w_mlp_out = g_mlp4[:, 0], g_mlp4[:, 1].reshape(D_FF, D_MODEL)
    merge_args = (ys, xbc_act, z, y5, gates, dx1, dvec, gssd, glu_w, glu_b, wa, head_sel)
    if comm is None:
        merge_out = _merge_bwd(*merge_args)
    else:
        g_mlp = g_mlp4.reshape(N_CHIPS, 2 * FF_SHARD, D_MODEL)
        merge_out, (sib_mlp,) = _merge_bwd(*merge_args, rider=_Pair([g_mlp]))
        pf_mlp, pb_mlp = _pair_sum(comm["place"], g_mlp, sib_mlp, "pair_sum_mlp")
    (dys, dz, dy5, dgates, mg, ya, yb, dpa, dpb, gel, dpre, d_dssd, d_gssd, d_glu_b) = merge_out
    d_w_out = _wgrad(mg, dx1, "wgrad_out")
    d_w_branch = jnp.concatenate([_wgrad(ya, dpa, "wgrad_branch_a"), _wgrad(yb, dpb, "wgrad_branch_b")], axis=0)
    d_glu_w = _wgrad(gel, dpre, "wgrad_glu")
    s5_args = (u5, dy5, wb4, wc4, ab, s5d, s5_states)
    if comm is None:
        du5, dwb4, dwc4, dab, d_s5d = _s5_bwd(*s5_args)
        mlp_total = None
    else:
        (du5, dwb4, dwc4, dab, d_s5d), (got_mlp,) = _s5_bwd(*s5_args, rider=_Chip([pb_mlp]))
        mlp_total = _chip_sum(comm["place"], pf_mlp, got_mlp, "chip_sum_mlp")
    dbb_re, dbb_im, d_c_re, d_c_im = _s5_block_grads(dwb4, dwc4)
    d_a_re, d_a_im, d_log_dt, d_b_re, d_b_im = _s5_disc_bwd(
        a_re, a_im, log_dt, b_re, b_im, dab[0:1], dab[1:2], dbb_re, dbb_im)
    dxs_s, dB, dC, ddt, d_alog = _ssd_bwd(xbc_act, dt, alog, ssd_states, dys)
    dxbc_raw, ddt_raw, d_conv_w, d_conv_b, d_dt_bias = _conv_bwd(
        xbc_raw, dt_raw, dys, dvec, dxs_s, dB, dC, ddt, p["conv_w"], conv_b, dt_bias)
    d_w_in = dict(z=_wgrad(h, dz, "wgrad_in_z"), xbc=_wgrad(h, dxbc_raw, "wgrad_in_xbc"),
                  dt=_wgrad(h, ddt_raw, "wgrad_in_dt")[:, :16], u5=_wgrad(h, du5, "wgrad_in_u5"),
                  gates=_wgrad(h, dgates, "wgrad_in_gates"))
    w_in_pieces = [(c0, d_w_in[n]) for n, c0, _ in W_IN_PIECES]
    inproj_args = (x, dx1, dz, dxbc_raw, du5, dgates, ddt_raw, g_mix, wp)
    if comm is None:
        dx, d_gmix = _inproj_bwd(*inproj_args)
        late_totals = None
    else:
        g_b, g_in = _late_buffers(d_w_out, d_w_branch, d_glu_w, d_conv_w[:CONV_K], w_in_pieces)
        sib_b, sib_in = _exchange(_Pair([g_b, g_in]), "pair_exchange")
        pf_b, pb_b = _pair_sum(comm["place"], g_b, sib_b, "pair_sum_b")
        pf_in, pb_in = _pair_sum(comm["place"], g_in, sib_in, "pair_sum_in")
        (dx, d_gmix), (got_b, got_in) = _inproj_bwd(*inproj_args, rider=_Chip([pb_b, pb_in]))
        late_totals = (_chip_sum(comm["place"], pf_b, got_b, "chip_sum_b"),
                       _chip_sum(comm["place"], pf_in, got_in, "chip_sum_in"))

    grads = dict(
        norm_mix_g=d_gmix.reshape(-1), w_in_pieces=w_in_pieces, late_totals=late_totals,
        conv_w=d_conv_w[:CONV_K], conv_b=d_conv_b.reshape(-1),
        dt_bias=d_dt_bias[0, :16], a_log=d_alog[0, :16], d_ssd=d_dssd[0, :16], ssd_norm_g=d_gssd.reshape(-1),
        s5_a_re=d_a_re.reshape(32, 64), s5_a_im=d_a_im.reshape(32, 64), s5_log_dt=d_log_dt[0, :32],
        s5_b_re=d_b_re.T.reshape(32, 64, 16), s5_b_im=d_b_im.T.reshape(32, 64, 16), s5_c_re=d_c_re, s5_c_im=d_c_im,
        s5_d=d_s5d.reshape(-1), s5_glu_w=d_glu_w, s5_glu_b=d_glu_b.reshape(-1), w_branch=d_w_branch, w_out=d_w_out,
        norm_mlp_g=d_gmlp.reshape(-1), w_mlp_in=d_w_mlp_in, w_mlp_out=d_w_mlp_out, norm_final_g=d_gfin.reshape(-1),
        mlp_total=mlp_total)
    return jnp.sum(loss_lanes), dx, grads


MESH = pl.DeviceIdType.MESH
N_CHIPS = 4


def _place():
    x, y, c = lax.axis_index("x"), lax.axis_index("y"), lax.axis_index("c")
    chips = [(1 - x, y), (x, 1 - y), (1 - x, 1 - y)]
    return x, y, c, chips


def _remote(src, dst, send_sems, recv_sems, k, to):
    return pltpu.make_async_remote_copy(src_ref=src, dst_ref=dst, send_sem=send_sems.at[k], recv_sem=recv_sems.at[k],
                                        device_id=to, device_id_type=MESH)


def _row_chunks(rows, k, align):
    step = rows // k
    assert rows % k == 0 and step % align == 0, (rows, k, align)
    return [(i * step, step) for i in range(k)]


ICI_CHUNKS = 4
D2D_CHUNKS = 24


class _Gather:
    def __init__(self, srcs, ks):
        self.inputs = list(srcs)
        self.out_shapes = [jax.ShapeDtypeStruct((N_CHIPS,) + a.shape, a.dtype) for a in srcs]
        self.halves = [a.shape[0] // 2 for a in srcs]
        self.pieces = [_row_chunks(h, k, 32 // a.dtype.itemsize) for a, h, k in zip(srcs, self.halves, ks)]
        self.n_ici = 3 * sum(ks)
        self.n_sems = 2 * self.n_ici + len(srcs)

    def _plan(self, src_refs, out_refs, send_sems, recv_sems):
        x, y, c, chips = _place()
        own = 2 * x + y
        sib = (x, y, 1 - c)
        first, fwd_plan, k = [], [], 0
        for a, (src_ref, out_ref) in enumerate(zip(src_refs, out_refs)):
            h = self.halves[a]
            for r0, nr in self.pieces[a]:
                for cx, cy in chips:
                    first.append(_remote(src_ref.at[pl.ds(c * h + r0, nr), :], out_ref.at[own, pl.ds(c * h + r0, nr), :],
                                         send_sems, recv_sems, k, (cx, cy, c)))
                    fwd_plan.append((out_ref, 2 * cx + cy, h, r0, nr, k, (cx, cy, c)))
                    k += 1
        for a, (src_ref, out_ref) in enumerate(zip(src_refs, out_refs)):
            first.append(_remote(src_ref, out_ref.at[own], send_sems, recv_sems, 2 * self.n_ici + a, sib))
        return first, fwd_plan, c, sib

    def issue(self, src_refs, out_refs, send_sems, recv_sems):
        for cp in self._plan(src_refs, out_refs, send_sems, recv_sems)[0]:
            cp.start()

    def complete(self, src_refs, out_refs, send_sems, recv_sems):
        first, fwd_plan, c, sib = self._plan(src_refs, out_refs, send_sems, recv_sems)
        passed = []
        for out_ref, s, h, r0, nr, k, frm in fwd_plan:
            got = out_ref.at[s, pl.ds(c * h + r0, nr), :]
            _remote(got, got, send_sems, recv_sems, k, frm).wait_recv()
            fw = _remote(got, got, send_sems, recv_sems, self.n_ici + k, sib)
            fw.start()
            passed.append(fw)
        for out_ref, s, h, r0, nr, k, frm in fwd_plan:
            got = out_ref.at[s, pl.ds((1 - c) * h + r0, nr), :]
            _remote(got, got, send_sems, recv_sems, self.n_ici + k, sib).wait_recv()
        own_copies = first[self.n_ici:]
        for cp in own_copies:
            cp.wait_recv()
        for cp in first + passed:
            cp.wait_send()


def _exchange(rider, name):
    ri, ro = len(rider.inputs), len(rider.out_shapes)

    def body(*refs):
        rider.issue(refs[:ri], refs[ri:ri + ro], *refs[ri + ro:])
        rider.complete(refs[:ri], refs[ri:ri + ro], *refs[ri + ro:])

    return _pc(
        body, name=name, in_specs=[_hbm_spec()] * ri, out_specs=[_hbm_spec()] * ro, out_shape=list(rider.out_shapes),
        scratch_shapes=[pltpu.SemaphoreType.DMA((rider.n_sems,))] * 2,
    )(*rider.inputs)


def _call(body, rider=None, **kw):
    if rider is None:
        return _pc(body, **kw)
    single = not isinstance(kw["out_shape"], (list, tuple))
    out_specs = [kw["out_specs"]] if single else list(kw["out_specs"])
    out_shape = [kw["out_shape"]] if single else list(kw["out_shape"])
    scratch = list(kw.get("scratch_shapes", ()))
    n_in, n_out, n_scr = len(kw["in_specs"]), len(out_specs), len(scratch)
    ri, ro = len(rider.inputs), len(rider.out_shapes)
    steps = kw["grid"][0]

    def wrapped(*refs):
        o0 = n_in + ri
        s0 = o0 + n_out + ro
        r_in, r_out, sems = refs[n_in:o0], refs[o0 + n_out:s0], refs[s0 + n_scr:]

        @pl.when(pl.program_id(0) == 0)
        def _():
            rider.issue(r_in, r_out, *sems)

        body(*refs[:n_in], *refs[o0:o0 + n_out], *refs[s0:s0 + n_scr])

        @pl.when(pl.program_id(0) == steps - 1)
        def _():
            rider.complete(r_in, r_out, *sems)

    f = _pc(wrapped, name=kw["name"], grid=kw["grid"], in_specs=list(kw["in_specs"]) + [_hbm_spec()] * ri,
            out_specs=out_specs + [_hbm_spec()] * ro, out_shape=out_shape + list(rider.out_shapes),
            scratch_shapes=scratch + [pltpu.SemaphoreType.DMA((rider.n_sems,))] * 2, compiler_params=kw["compiler_params"])

    def run(*args):
        res = f(*args, *rider.inputs)
        return (res[0] if single else res[:n_out]), res[n_out:]

    return run


def _d2d_pieces(rows):
    k = next(k for k in range(24, 0, -1) if rows % k == 0 and (rows // k) % 8 == 0)
    return _row_chunks(rows, k, 8)


class _Pair:
    def __init__(self, gs, small=None):
        self.n = len(gs)
        self.halves = [g.shape[1] // 2 for g in gs]
        self.inputs = list(gs) + ([small] if small is not None else [])
        self.out_shapes = [jax.ShapeDtypeStruct((N_CHIPS, h, g.shape[2]), F32) for g, h in zip(gs, self.halves)]
        if small is not None:
            self.out_shapes.append(jax.ShapeDtypeStruct(small.shape, F32))
        self.n_sems = len(self.inputs)

    def issue(self, in_refs, out_refs, send_sems, recv_sems):
        x, y, c, _ = _place()
        sib = (x, y, 1 - c)
        for a in range(self.n):
            h = self.halves[a]
            for s in range(N_CHIPS):
                for r0, nr in _d2d_pieces(h):
                    _remote(in_refs[a].at[s, pl.ds((1 - c) * h + r0, nr), :], out_refs[a].at[s, pl.ds(r0, nr), :],
                            send_sems, recv_sems, a, sib).start()
        for a in range(self.n, len(self.inputs)):
            _remote(in_refs[a], out_refs[a], send_sems, recv_sems, a, sib).start()

    def complete(self, in_refs, out_refs, send_sems, recv_sems):
        x, y, c, _ = _place()
        for a in range(len(self.inputs)):
            _remote(out_refs[a], out_refs[a], send_sems, recv_sems, a, (x, y, 1 - c)).wait()


SUM_BLOCKS = 4


def _pair_sum(place, g, sib, name):
    n, R, C = g.shape
    H = R // 2
    rb = H // SUM_BLOCKS
    assert H % SUM_BLOCKS == 0 and rb % 16 == 0

    def body(place_ref, a_ref, b_ref, pf_ref, pb_ref):
        p = a_ref[...] + b_ref[...]
        pf_ref[...] = p
        pb_ref[...] = p.astype(BF16)

    blk = pl.BlockSpec((1, rb, C), lambda s, i, pr: (s, i, 0))
    mine = pl.BlockSpec((1, rb, C), lambda s, i, pr: (s, pr[1] * SUM_BLOCKS + i, 0))
    return _pc(
        body, name=name, out_shape=[jax.ShapeDtypeStruct((n, H, C), F32), jax.ShapeDtypeStruct((n, H, C), BF16)],
        grid_spec=pltpu.PrefetchScalarGridSpec(num_scalar_prefetch=1, grid=(n, SUM_BLOCKS), in_specs=[mine, blk],
                                               out_specs=[blk, blk]),
        compiler_params=_cparams(("arbitrary", "arbitrary")),
    )(place, g, sib)


class _Chip:
    def __init__(self, pbs, psmall=None):
        self.n = len(pbs)
        self.rows = [pb.shape[1] for pb in pbs]
        self.inputs = list(pbs) + ([psmall] if psmall is not None else [])
        self.out_shapes = [jax.ShapeDtypeStruct((3,) + pb.shape[1:], BF16) for pb in pbs]
        if psmall is not None:
            self.out_shapes.append(jax.ShapeDtypeStruct((N_CHIPS,) + psmall.shape, F32))
        self.n_sems = 3 * len(self.inputs)

    def issue(self, in_refs, out_refs, send_sems, recv_sems):
        x, y, c, chips = _place()
        own = 2 * x + y
        for j, (cx, cy) in enumerate(chips):
            for a in range(self.n):
                for r0, nr in _row_chunks(self.rows[a], ICI_CHUNKS, 16):
                    _remote(in_refs[a].at[2 * cx + cy, pl.ds(r0, nr), :], out_refs[a].at[j, pl.ds(r0, nr), :],
                            send_sems, recv_sems, 3 * a + j, (cx, cy, c)).start()
            for a in range(self.n, len(self.inputs)):
                _remote(in_refs[a], out_refs[a].at[own], send_sems, recv_sems, 3 * a + j, (cx, cy, c)).start()

    def complete(self, in_refs, out_refs, send_sems, recv_sems):
        x, y, c, chips = _place()
        own = 2 * x + y
        for j, (cx, cy) in enumerate(chips):
            for a in range(self.n):
                _remote(in_refs[a].at[own], out_refs[a].at[j], send_sems, recv_sems, 3 * a + j, (cx, cy, c)).wait()
            for a in range(self.n, len(self.inputs)):
                _remote(in_refs[a], out_refs[a].at[2 * cx + cy], send_sems, recv_sems, 3 * a + j, (cx, cy, c)).wait()


def _chip_sum(place, pf, got, name):
    _, H, C = pf.shape
    rb = H // SUM_BLOCKS

    def body(place_ref, o_ref, g_ref, tot_ref):
        tot_ref[...] = ((o_ref[0] + g_ref[0].astype(F32)) + g_ref[1].astype(F32)) + g_ref[2].astype(F32)

    ins = [pl.BlockSpec((1, rb, C), lambda i, pr: (pr[0], i, 0)), pl.BlockSpec((3, rb, C), lambda i, pr: (0, i, 0))]
    out = pl.BlockSpec((rb, C), lambda i, pr: (pr[1] * SUM_BLOCKS + i, 0))
    return _pc(
        body, name=name, out_shape=jax.ShapeDtypeStruct((2 * H, C), F32),
        grid_spec=pltpu.PrefetchScalarGridSpec(num_scalar_prefetch=1, grid=(SUM_BLOCKS,), in_specs=ins, out_specs=out),
        compiler_params=_cparams(("arbitrary",)),
    )(place, pf, got)


def _half_exchange(fulls):
    n = len(fulls)

    def body(*refs):
        in_refs, out_refs = refs[:n], refs[n:2 * n]
        send_sems, recv_sems = refs[2 * n:]
        x, y, c, _ = _place()
        sib = (x, y, 1 - c)
        for a in range(n):
            h = fulls[a].shape[0] // 2
            for r0, nr in _d2d_pieces(h):
                rows = pl.ds(c * h + r0, nr)
                _remote(in_refs[a].at[rows, :], out_refs[a].at[rows, :], send_sems, recv_sems, a, sib).start()
        for a in range(n):
            h = fulls[a].shape[0] // 2
            _remote(in_refs[a].at[pl.ds(c * h, h), :], out_refs[a].at[pl.ds((1 - c) * h, h), :], send_sems, recv_sems, a,
                    sib).wait()

    return _pc(
        body, name="half_exchange", in_specs=[_hbm_spec()] * n, out_specs=[_hbm_spec()] * n,
        out_shape=[jax.ShapeDtypeStruct(f.shape, F32) for f in fulls],
        input_output_aliases={a: a for a in range(n)},
        scratch_shapes=[pltpu.SemaphoreType.DMA((n,)), pltpu.SemaphoreType.DMA((n,))],
    )(*fulls)


def _small_allreduce(pack):
    R, C = pack.shape

    def body(p_ref, o_ref, sib_ref, pair_ref, slots_ref, send_sems, recv_sems):
        x, y, c, chips = _place()
        own = 2 * x + y
        cp = _remote(p_ref, sib_ref, send_sems, recv_sems, 0, (x, y, 1 - c))
        cp.start()
        cp.wait()
        pair_ref[...] = p_ref[...] + sib_ref[...]
        slots_ref[own] = pair_ref[...]
        out = [_remote(pair_ref, slots_ref.at[own], send_sems, recv_sems, 1 + j, (cx, cy, c)) for j, (cx, cy) in enumerate(chips)]
        for cp in out:
            cp.start()
        for j, (cx, cy) in enumerate(chips):
            _remote(pair_ref, slots_ref.at[2 * cx + cy], send_sems, recv_sems, 1 + j, (cx, cy, c)).wait()
        o_ref[...] = ((slots_ref[0] + slots_ref[1]) + slots_ref[2]) + slots_ref[3]

    vmem = pl.BlockSpec(memory_space=pltpu.VMEM)
    return _pc(
        body, name="small_allreduce", in_specs=[vmem], out_specs=vmem, out_shape=jax.ShapeDtypeStruct((R, C), F32),
        scratch_shapes=[pltpu.VMEM((R, C), F32), pltpu.VMEM((R, C), F32), pltpu.VMEM((N_CHIPS, R, C), F32),
                        pltpu.SemaphoreType.DMA((4,)), pltpu.SemaphoreType.DMA((4,))],
    )(pack)


def _adamw(w, g, m, v, name, g_row0=0, with_grad=False, col_block=None):
    R, C = w.shape
    rb = 256 if R % 256 == 0 else (128 if R % 128 == 0 else R)
    if col_block:
        rb = R
    assert g_row0 % rb == 0

    def body(w_ref, g_ref, m_ref, v_ref, d_ref, nm_ref, nv_ref, *g_out):
        gv = g_ref[...]
        m2 = ADAM_B1 * m_ref[...] + (1.0 - ADAM_B1) * gv
        v2 = ADAM_B2 * v_ref[...] + (1.0 - ADAM_B2) * (gv * gv)
        m_hat = m2 * (1.0 / (1.0 - ADAM_B1 ** ADAM_STEP))
        v_hat = v2 * (1.0 / (1.0 - ADAM_B2 ** ADAM_STEP))
        d_ref[...] = -ADAM_LR * (m_hat / (jnp.sqrt(v_hat) + ADAM_EPS) + ADAM_WD * w_ref[...])
        nm_ref[...] = m2
        nv_ref[...] = v2
        if with_grad:
            g_out[0][...] = gv

    if col_block:
        spec = g_spec = pl.BlockSpec((R, col_block), lambda i: (0, i))
        steps = C // col_block
    else:
        spec = pl.BlockSpec((rb, C), lambda i: (i, 0))
        g_spec = pl.BlockSpec((rb, C), lambda i: (g_row0 // rb + i, 0))
        steps = R // rb
    n_out = 4 if with_grad else 3
    return _pc(
        body, name=name, grid=(steps,), in_specs=[spec, g_spec, spec, spec], out_specs=[spec] * n_out,
        out_shape=[jax.ShapeDtypeStruct((R, C), F32)] * n_out, compiler_params=_cparams(("parallel",)),
    )(w, g, m, v)


PACK_COLS = 1024
ROWS_A = (("w_mlp_in", 0, 1024), ("w_mlp_out", 1024, 1024), ("w_out", 2048, 256), ("w_branch", 2304, 384))
ROWS_A_TOTAL = 2688
ROWS_B = (("w_out", 0, 256), ("w_branch", 256, 384))
ROW_B_GLU, ROW_B_CONV, ROWS_B_TOTAL = 640, 704, 768
W_IN_SHARD = 1412
CONV_PAD_ROWS = 16
SMALL = (("norm_mix_g", (1024,)), ("conv_b", (2048,)), ("dt_bias", (16,)), ("a_log", (16,)), ("d_ssd", (16,)),
         ("ssd_norm_g", (1024,)), ("s5_a_re", (32, 64)), ("s5_a_im", (32, 64)), ("s5_log_dt", (32,)),
         ("s5_b_re", (32, 64, 16)), ("s5_b_im", (32, 64, 16)), ("s5_c_re", (32, 16, 64)), ("s5_c_im", (32, 16, 64)),
         ("s5_d", (512,)), ("s5_glu_b", (512,)), ("norm_mlp_g", (1024,)), ("norm_final_g", (1024,)))
SMALL_ROWS = 144
SMALL_COUNT = sum(math.prod(shp) for _, shp in SMALL)
GLU_ROWS = S5_WIDTH * S5_WIDTH // PACK_COLS
CONV_ROWS = CONV_K * CONV_DIM // PACK_COLS
W_IN_PIECES = (("z", 0, 1024), ("xbc", 1024, 2048), ("dt", OFF_DT, 16), ("u5", OFF_U, 512), ("gates", 3600, 2048))


def _pack_small(parts):
    flat = jnp.concatenate([a.astype(F32).reshape(-1) for a in parts])
    return jnp.concatenate([flat, jnp.zeros((SMALL_ROWS * PACK_COLS - flat.shape[0],), F32)]).reshape(SMALL_ROWS, PACK_COLS)


def _unpack_small(pack):
    flat, out, r = pack.reshape(-1), {}, 0
    for name, shp in SMALL:
        n = math.prod(shp)
        out[name] = flat[r:r + n].reshape(shp)
        r += n
    return out


def _late_buffers(d_w_out, d_w_branch, d_glu_w, d_conv_w, w_in_pieces):
    conv4 = d_conv_w.reshape(CONV_K, N_CHIPS, 512).transpose(1, 0, 2).reshape(N_CHIPS, CONV_ROWS // N_CHIPS, PACK_COLS)
    g_b = jnp.concatenate(
        [d_w_out.reshape(N_CHIPS, -1, PACK_COLS), d_w_branch.reshape(N_CHIPS, -1, PACK_COLS),
         d_glu_w.reshape(N_CHIPS, GLU_ROWS // N_CHIPS, PACK_COLS),
         jnp.pad(conv4, ((0, 0), (0, ROWS_B_TOTAL - ROW_B_CONV - CONV_ROWS // N_CHIPS), (0, 0)))], axis=1)
    g_in = jnp.stack([jnp.concatenate(_column_range(w_in_pieces, W_IN_SHARD * s, W_IN_SHARD * (s + 1)), axis=1)
                      for s in range(N_CHIPS)])
    return g_b, g_in


def _column_range(pieces, lo, hi):
    out = []
    for c0, a in pieces:
        a0, a1 = max(lo, c0), min(hi, c0 + a.shape[-1])
        if a0 < a1:
            out.append(a[..., a0 - c0:a1 - c0])
    return out


def kernel(x, norm_mix_g, w_in, conv_w, conv_b, dt_bias, a_log, d_ssd, ssd_norm_g, s5_a_re, s5_a_im, s5_log_dt, s5_b_re, s5_b_im, s5_c_re, s5_c_im, s5_d, s5_glu_w, s5_glu_b, w_branch, w_out, norm_mlp_g, w_mlp_in, w_mlp_out, norm_final_g, loss_target, m_norm_mix_g, m_w_in, m_conv_w, m_conv_b, m_dt_bias, m_a_log, m_d_ssd, m_ssd_norm_g, m_s5_a_re, m_s5_a_im, m_s5_log_dt, m_s5_b_re, m_s5_b_im, m_s5_c_re, m_s5_c_im, m_s5_d, m_s5_glu_w, m_s5_glu_b, m_w_branch, m_w_out, m_norm_mlp_g, m_w_mlp_in, m_w_mlp_out, m_norm_final_g, v_norm_mix_g, v_w_in, v_conv_w, v_conv_b, v_dt_bias, v_a_log, v_d_ssd, v_ssd_norm_g, v_s5_a_re, v_s5_a_im, v_s5_log_dt, v_s5_b_re, v_s5_b_im, v_s5_c_re, v_s5_c_im, v_s5_d, v_s5_glu_w, v_s5_glu_b, v_w_branch, v_w_out, v_norm_mlp_g, v_w_mlp_in, v_w_mlp_out, v_norm_final_g):
    names = ("norm_mix_g", "w_in", "conv_w", "conv_b", "dt_bias", "a_log", "d_ssd", "ssd_norm_g", "s5_a_re", "s5_a_im",
             "s5_log_dt", "s5_b_re", "s5_b_im", "s5_c_re", "s5_c_im", "s5_d", "s5_glu_w", "s5_glu_b", "w_branch", "w_out",
             "norm_mlp_g", "w_mlp_in", "w_mlp_out", "norm_final_g")
    w = dict(zip(names, (norm_mix_g, w_in, conv_w, conv_b, dt_bias, a_log, d_ssd, ssd_norm_g, s5_a_re, s5_a_im, s5_log_dt,
                         s5_b_re, s5_b_im, s5_c_re, s5_c_im, s5_d, s5_glu_w, s5_glu_b, w_branch, w_out, norm_mlp_g,
                         w_mlp_in, w_mlp_out, norm_final_g)))
    m = dict(zip(names, (m_norm_mix_g, m_w_in, m_conv_w, m_conv_b, m_dt_bias, m_a_log, m_d_ssd, m_ssd_norm_g, m_s5_a_re,
                         m_s5_a_im, m_s5_log_dt, m_s5_b_re, m_s5_b_im, m_s5_c_re, m_s5_c_im, m_s5_d, m_s5_glu_w,
                         m_s5_glu_b, m_w_branch, m_w_out, m_norm_mlp_g, m_w_mlp_in, m_w_mlp_out, m_norm_final_g)))
    v = dict(zip(names, (v_norm_mix_g, v_w_in, v_conv_w, v_conv_b, v_dt_bias, v_a_log, v_d_ssd, v_ssd_norm_g, v_s5_a_re,
                         v_s5_a_im, v_s5_log_dt, v_s5_b_re, v_s5_b_im, v_s5_c_re, v_s5_c_im, v_s5_d, v_s5_glu_w,
                         v_s5_glu_b, v_w_branch, v_w_out, v_norm_mlp_g, v_w_mlp_in, v_w_mlp_out, v_norm_final_g)))

    cx, cy, cc = lax.axis_index("x"), lax.axis_index("y"), lax.axis_index("c")
    own = 2 * cx + cy
    place = jnp.stack([own, cc]).astype(jnp.int32)

    src_conv = jnp.concatenate([conv_w, jnp.zeros((CONV_PAD_ROWS - CONV_K, 512), F32)], axis=0)
    all_in, all_conv = _exchange(_Gather([w_in.astype(BF16), src_conv], [ICI_CHUNKS, 1]), "gather_first")
    p = {n: w[n] for n, _ in SMALL}
    p["conv_w"] = jnp.concatenate([all_conv[s, :CONV_K] for s in range(N_CHIPS)], axis=1)
    shards = [(W_IN_SHARD * s, all_in[s]) for s in range(N_CHIPS)]
    p["w_in_perm"] = jnp.concatenate(
        _column_range(shards, 0, OFF_DT) + _column_range(shards, OFF_U, D_IN_PROJ) + _column_range(shards, OFF_DT, OFF_U)
        + [jnp.zeros((D_MODEL, DT_PAD - 16), BF16)], axis=1)

    def late_unpack(gathered):
        all_a, all_glu = gathered
        return {"late_weights": all_a, "s5_glu_w": all_glu.reshape(S5_WIDTH, S5_WIDTH)}

    comm = dict(place=place, late_ks=[ICI_CHUNKS, 1], late_unpack=late_unpack,
                late_srcs=[jnp.concatenate([w[n].astype(BF16) for n, _, _ in ROWS_A], axis=0), s5_glu_w.astype(BF16)])
    loss_part, grad_x, g = _local_step(x[0], loss_target[0], p, comm)

    red_mlp, red_b, red_in = _half_exchange([g["mlp_total"], *g["late_totals"]])
    small_tot = _small_allreduce(_pack_small([g[n] for n, _ in SMALL] + [loss_part.reshape(1)]))
    loss = small_tot.reshape(-1)[SMALL_COUNT]

    grads = _unpack_small(small_tot)
    delta, new_m, new_v = {}, {}, {}
    for n, r0, _ in ROWS_A[:2]:
        delta[n], new_m[n], new_v[n], grads[n] = _adamw(w[n], red_mlp, m[n], v[n], "adamw_" + n, g_row0=r0, with_grad=True)
    for n, r0, _ in ROWS_B:
        delta[n], new_m[n], new_v[n], grads[n] = _adamw(w[n], red_b, m[n], v[n], "adamw_" + n, g_row0=r0, with_grad=True)
    d_t, m_t, v_t, g_t = _adamw(w_in.T, red_in.T, m_w_in.T, v_w_in.T, "adamw_w_in", with_grad=True, col_block=128)
    delta["w_in"], new_m["w_in"], new_v["w_in"], grads["w_in"] = d_t.T, m_t.T, v_t.T, g_t.T
    grads["s5_glu_w"] = red_b[ROW_B_GLU:ROW_B_GLU + GLU_ROWS // N_CHIPS].reshape(S5_WIDTH // N_CHIPS, S5_WIDTH)
    grads["conv_w"] = red_b[ROW_B_CONV:ROW_B_CONV + CONV_ROWS // N_CHIPS].reshape(CONV_K, CONV_DIM // N_CHIPS)
    for n in ("s5_glu_w", "conv_w"):
        delta[n], new_m[n], new_v[n] = _adamw(w[n], grads[n], m[n], v[n], "adamw_" + n)
    ds, ms, vs = _adamw(_pack_small([w[n] for n, _ in SMALL]), small_tot, _pack_small([m[n] for n, _ in SMALL]),
                        _pack_small([v[n] for n, _ in SMALL]), "adamw_small")
    delta.update(_unpack_small(ds))
    new_m.update(_unpack_small(ms))
    new_v.update(_unpack_small(vs))

    return (loss, grad_x[None], *[grads[n] for n in names], *[delta[n] for n in names],
            *[new_m[n] for n in names], *[new_v[n] for n in names])
```

```python
import functools
import math

import jax
import jax.numpy as jnp
from jax import lax
from jax.experimental import pallas as pl
from jax.experimental.pallas import tpu as pltpu

F32 = jnp.float32
BF16 = jnp.bfloat16

D_MODEL = 1024
SSD_INNER = 1024
SSD_HEADS = 16
SSD_HEADDIM = 64
SSD_GROUPS = 4
SSD_HPG = 4
SSD_STATE = 128
SSD_CHUNK = 128
CONV_K = 4
CONV_DIM = 2048
S5_WIDTH = 512
S5_STATES = 2048
S5_BLOCKS = 4
S5_CHUNK = 128
D_FF = 4096
FF_SHARDS = 4
FF_SHARD = D_FF // FF_SHARDS
EPS = 1e-6
P_Z, P_XBC, P_U5, P_G, P_DT, P_END = 0, 1024, 3072, 3584, 5632, 5760
DT_PAD = 128
OFF_DT, OFF_U = 3072, 3088
D_IN_PROJ = 5648

ADAM_LR, ADAM_B1, ADAM_B2, ADAM_EPS, ADAM_WD, ADAM_STEP = 0.001, 0.9, 0.999, 1e-08, 0.01, 10

TOKEN_TILE = 256
VMEM_LIMIT = 56 * 1024 * 1024
HALO = 8
INPROJ_PIECE = 256
CONV_COLS = 256
CONV_ROWS_BLK = 64
WGRAD_TOKENS = 2048


def _pc(body, **kw):
    return pl.pallas_call(body, **kw)


def _cparams(sem=None):
    return pltpu.CompilerParams(dimension_semantics=sem, vmem_limit_bytes=VMEM_LIMIT)


def _dot(a, b):
    return jnp.dot(a, b, preferred_element_type=F32)


def _dot_nt(a, b):
    return lax.dot_general(a, b, (((1,), (1,)), ((), ())), preferred_element_type=F32)


def _dot_tn(a, b):
    return lax.dot_general(a, b, (((0,), (0,)), ((), ())), preferred_element_type=F32)


def _dot_hi(a, b, dims=(((1,), (0,)), ((), ()))):
    return lax.dot_general(a, b, dims, preferred_element_type=F32, precision=lax.Precision.HIGHEST)


def _split_bf16(x, terms):
    out = []
    for _ in range(terms - 1):
        t = x.astype(BF16)
        out.append(t)
        x = x - t.astype(F32)
    out.append(x.astype(BF16))
    return out


def _dot_split(x, onehots, terms, dims=(((1,), (0,)), ((), ()))):
    acc = None
    for t in _split_bf16(x, terms):
        p = lax.dot_general(t, onehots, dims, preferred_element_type=F32)
        acc = p if acc is None else acc + p
    return acc


def _dot_split_rhs(onehots, x, terms, dims=(((1,), (0,)), ((), ()))):
    acc = None
    for t in _split_bf16(x, terms):
        p = lax.dot_general(onehots, t, dims, preferred_element_type=F32)
        acc = p if acc is None else acc + p
    return acc


def _sigmoid(x):
    return 0.5 * jnp.tanh(0.5 * x) + 0.5


def _softplus(x):
    return jnp.maximum(x, 0.0) + jnp.log(1.0 + jnp.exp(-jnp.abs(x)))


_GELU_C = math.sqrt(2.0 / math.pi)


def _gelu(x):
    return 0.5 * x * (1.0 + jnp.tanh(_GELU_C * (x + 0.044715 * x * x * x)))


def _gelu_grad(x):
    t = jnp.tanh(_GELU_C * (x + 0.044715 * x * x * x))
    return 0.5 * (1.0 + t) + 0.5 * x * (1.0 - t * t) * _GELU_C * (1.0 + 3.0 * 0.044715 * x * x)


def _rms(x):
    r = lax.rsqrt(jnp.mean(x * x, axis=-1, keepdims=True) + EPS)
    return x * r, r


def _rms_bwd(xn, r, dxn):
    return r * (dxn - xn * jnp.mean(dxn * xn, axis=-1, keepdims=True))


def _row_spec(tm, width, col=0):
    return pl.BlockSpec((tm, width), lambda i: (i, col))


def _const_spec(shape):
    nd = len(shape)
    return pl.BlockSpec(shape, lambda i: (0,) * nd)


def _hbm_spec():
    return pl.BlockSpec(memory_space=pl.ANY)


def _load_late_weight(wa_hbm, dst_ref, name):
    r0, nr = next((r0, nr) for n, r0, nr in ROWS_A if n == name)
    for s in range(N_CHIPS):
        dst = dst_ref.at[s] if len(dst_ref.shape) == 3 else dst_ref.at[pl.ds(nr * s, nr), :]
        pltpu.sync_copy(wa_hbm.at[s, pl.ds(r0, nr), :], dst)


def _inproj_bwd(x, dx1, dz, dxbc, du5, dgt, ddt, g, wp, rider=None):
    T = x.shape[0]
    tm = TOKEN_TILE

    def body(x_ref, dx1_ref, dz_ref, dxbc_ref, du5_ref, dgt_ref, ddt_ref, g_ref, w_hbm, dx_ref, dg_ref, w_ref):
        @pl.when(pl.program_id(0) == 0)
        def _():
            pltpu.sync_copy(w_hbm, w_ref)
            dg_ref[...] = jnp.zeros_like(dg_ref)

        xn, r = _rms(x_ref[...])
        gv = g_ref[...]
        dh = _dot_nt(dz_ref[...].astype(BF16), w_ref[:, P_Z:P_XBC])
        dh += _dot_nt(dxbc_ref[...].astype(BF16), w_ref[:, P_XBC:P_U5])
        dh += _dot_nt(du5_ref[...].astype(BF16), w_ref[:, P_U5:P_G])
        dh += _dot_nt(dgt_ref[...].astype(BF16), w_ref[:, P_G:P_DT])
        dh += _dot_nt(ddt_ref[...].astype(BF16), w_ref[:, P_DT:P_END])
        dg_ref[...] += jnp.sum(dh * xn, axis=0, keepdims=True)
        dx_ref[...] = dx1_ref[...] + _rms_bwd(xn, r, dh * gv)

    return _call(
        body, rider, name="inproj_bwd", grid=(T // tm,),
        in_specs=[_row_spec(tm, 1024), _row_spec(tm, 1024), _row_spec(tm, 1024), _row_spec(tm, 2048),
                  _row_spec(tm, 512), _row_spec(tm, 2048), _row_spec(tm, DT_PAD), _const_spec((1, 1024)), _hbm_spec()],
        out_specs=[_row_spec(tm, 1024), _const_spec((1, 1024))],
        out_shape=[jax.ShapeDtypeStruct((T, 1024), F32), jax.ShapeDtypeStruct((1, 1024), F32)],
        scratch_shapes=[pltpu.VMEM((D_MODEL, P_END), BF16)],
        compiler_params=_cparams(("arbitrary",)),
    )(x, dx1, dz, dxbc, du5, dgt, ddt, g, wp)


def _conv_fwd(xbc_raw, dt_raw, conv_w, conv_b, dt_bias):
    T = xbc_raw.shape[0]
    tm = TOKEN_TILE

    def body(u_ref, dtr_ref, w_ref, b_ref, db_ref, xs_ref, bc_ref, dt_ref, ext_ref):
        @pl.when(pl.program_id(0) == 0)
        def _():
            ext_ref[0:HALO, :] = jnp.zeros((HALO, CONV_DIM), F32)

        ext_ref[HALO:, :] = u_ref[...]
        for c0 in range(0, CONV_DIM, CONV_COLS):
            cols = slice(c0, c0 + CONV_COLS)
            taps = [w_ref[k:k + 1, cols] for k in range(CONV_K)]
            bias = b_ref[:, cols]
            for r0 in range(0, tm, CONV_ROWS_BLK):
                y = bias + taps[0] * ext_ref[pl.ds(HALO - (CONV_K - 1) + r0, CONV_ROWS_BLK), cols]
                for k in range(1, CONV_K):
                    y += taps[k] * ext_ref[pl.ds(HALO - (CONV_K - 1) + k + r0, CONV_ROWS_BLK), cols]
                act = y * _sigmoid(y)
                if c0 < SSD_INNER:
                    xs_ref[r0:r0 + CONV_ROWS_BLK, cols] = act
                else:
                    bc_ref[r0:r0 + CONV_ROWS_BLK, c0 - SSD_INNER:c0 - SSD_INNER + CONV_COLS] = act.astype(BF16)
        ext_ref[0:HALO, :] = u_ref[tm - HALO:tm, :]
        dt_ref[...] = _softplus(dtr_ref[...] + db_ref[...])

    return _pc(
        body, name="conv_fwd", grid=(T // tm,),
        in_specs=[_row_spec(tm, CONV_DIM), _row_spec(tm, DT_PAD), _const_spec((CONV_K, CONV_DIM)),
                  _const_spec((1, CONV_DIM)), _const_spec((1, DT_PAD))],
        out_specs=[_row_spec(tm, SSD_INNER), _row_spec(tm, CONV_DIM - SSD_INNER), _row_spec(tm, DT_PAD)],
        out_shape=[jax.ShapeDtypeStruct((T, SSD_INNER), F32), jax.ShapeDtypeStruct((T, CONV_DIM - SSD_INNER), BF16),
                   jax.ShapeDtypeStruct((T, DT_PAD), F32)],
        scratch_shapes=[pltpu.VMEM((tm + HALO, CONV_DIM), F32)],
        compiler_params=_cparams(("arbitrary",)),
    )(xbc_raw, dt_raw, conv_w, conv_b, dt_bias)


def _conv_bwd(xbc_raw, dt_raw, dys, dvec, dxs_b, dB, dC, ddt, conv_w, conv_b, dt_bias):
    T = xbc_raw.shape[0]
    tm = TOKEN_TILE
    n = T // tm
    hb = tm // HALO

    def rev(width):
        return pl.BlockSpec((tm, width), lambda i: (n - 1 - i, 0))

    def body(u_ref, up_ref, dtr_ref, dys_ref, dv_ref, dxb_ref, dB_ref, dC_ref, ddt_ref, w_ref, b_ref, db_ref,
             du_ref, ddtr_ref, dw_ref, dcb_ref, ddb_ref, ext_ref, dye_ref):
        i = pl.program_id(0)

        @pl.when(i == 0)
        def _():
            dye_ref[tm:, :] = jnp.zeros((HALO, CONV_DIM), F32)
            dw_ref[...] = jnp.zeros_like(dw_ref)
            dcb_ref[...] = jnp.zeros_like(dcb_ref)
            ddb_ref[...] = jnp.zeros_like(ddb_ref)

        first = (i == n - 1).astype(F32)
        ext_ref[0:HALO, :] = up_ref[...] * (1.0 - first)
        ext_ref[HALO:, :] = u_ref[...]
        for c0 in range(0, CONV_DIM, CONV_COLS):
            cols = slice(c0, c0 + CONV_COLS)
            taps = [w_ref[k:k + 1, cols] for k in range(CONV_K)]
            bias = b_ref[:, cols]
            acc_b = jnp.zeros((HALO, CONV_COLS), F32)
            acc_w = [jnp.zeros((HALO, CONV_COLS), F32) for _ in range(CONV_K)]
            for r0 in range(0, tm, CONV_ROWS_BLK):
                rows = slice(r0, r0 + CONV_ROWS_BLK)
                us = [ext_ref[pl.ds(HALO - (CONV_K - 1) + k + r0, CONV_ROWS_BLK), cols] for k in range(CONV_K)]
                y = bias + taps[0] * us[0]
                for k in range(1, CONV_K):
                    y += taps[k] * us[k]
                s = _sigmoid(y)
                if c0 < SSD_INNER:
                    dact = dys_ref[rows, cols] * dv_ref[:, cols] + dxb_ref[rows, cols]
                elif c0 < SSD_INNER + 512:
                    dact = dB_ref[rows, c0 - SSD_INNER:c0 - SSD_INNER + CONV_COLS]
                else:
                    dact = dC_ref[rows, c0 - SSD_INNER - 512:c0 - SSD_INNER - 512 + CONV_COLS]
                dy = dact * (s * (1.0 + y * (1.0 - s)))
                dye_ref[rows, cols] = dy
                acc_b += jnp.sum(dy.reshape(CONV_ROWS_BLK // HALO, HALO, CONV_COLS), axis=0)
                for k in range(CONV_K):
                    acc_w[k] += jnp.sum((dy * us[k]).reshape(CONV_ROWS_BLK // HALO, HALO, CONV_COLS), axis=0)
            dcb_ref[:, cols] += jnp.sum(acc_b, axis=0, keepdims=True)
            for k in range(CONV_K):
                dw_ref[k:k + 1, cols] += jnp.sum(acc_w[k], axis=0, keepdims=True)
        for c0 in range(0, CONV_DIM, CONV_COLS):
            cols = slice(c0, c0 + CONV_COLS)
            taps = [w_ref[k:k + 1, cols] for k in range(CONV_K)]
            for r0 in range(0, tm, CONV_ROWS_BLK):
                du = taps[0] * dye_ref[pl.ds(CONV_K - 1 + r0, CONV_ROWS_BLK), cols]
                for k in range(1, CONV_K):
                    du += taps[k] * dye_ref[pl.ds(CONV_K - 1 - k + r0, CONV_ROWS_BLK), cols]
                du_ref[r0:r0 + CONV_ROWS_BLK, cols] = du.astype(BF16)
        dye_ref[tm:, :] = dye_ref[0:HALO, :]
        sg = _sigmoid(dtr_ref[...] + db_ref[...])
        ddtr = ddt_ref[...] * sg
        ddtr_ref[...] = ddtr.astype(BF16)
        ddb_ref[...] += jnp.sum(ddtr, axis=0, keepdims=True)

    prev_spec = pl.BlockSpec((HALO, CONV_DIM), lambda i: (jnp.maximum((n - 1 - i) * hb - 1, 0), 0))
    return _pc(
        body, name="conv_bwd", grid=(n,),
        in_specs=[rev(CONV_DIM), prev_spec, rev(DT_PAD), rev(1024), _const_spec((1, SSD_INNER)), rev(1024), rev(512), rev(512),
                  rev(DT_PAD), _const_spec((CONV_K, CONV_DIM)), _const_spec((1, CONV_DIM)), _const_spec((1, DT_PAD))],
        out_specs=[rev(CONV_DIM), rev(DT_PAD), _const_spec((HALO, CONV_DIM)), _const_spec((1, CONV_DIM)),
                   _const_spec((1, DT_PAD))],
        out_shape=[jax.ShapeDtypeStruct((T, CONV_DIM), BF16), jax.ShapeDtypeStruct((T, DT_PAD), BF16),
                   jax.ShapeDtypeStruct((HALO, CONV_DIM), F32), jax.ShapeDtypeStruct((1, CONV_DIM), F32),
                   jax.ShapeDtypeStruct((1, DT_PAD), F32)],
        scratch_shapes=[pltpu.VMEM((tm + HALO, CONV_DIM), F32), pltpu.VMEM((tm + HALO, CONV_DIM), F32)],
        compiler_params=_cparams(("arbitrary",)),
    )(xbc_raw, xbc_raw, dt_raw, dys, dvec, dxs_b, dB, dC, ddt, conv_w, conv_b, dt_bias)


GROUP_LANES = SSD_HPG * SSD_HEADDIM


def _ssd_expanders():
    head = jnp.arange(DT_PAD)[:, None]
    to_wide = (jnp.arange(SSD_INNER)[None, :] // SSD_HEADDIM == head).astype(BF16)
    return to_wide, to_wide.T


def _ssd_prep(dt_ref, alog_ref, wide_ref):
    q = SSD_CHUNK
    a = -jnp.exp(alog_ref[...])
    dtv = dt_ref[...]
    la = dtv * a
    row = lax.broadcasted_iota(jnp.int32, (q, q), 0)
    col = lax.broadcasted_iota(jnp.int32, (q, q), 1)
    tri = (col <= row).astype(BF16)
    cum = _dot_split_rhs(tri, la, 3)
    cum_t = _dot_split(la, tri, 3, (((0,), (1,)), ((), ())))
    dtw = _dot_split(dtv, wide_ref[...], 2)
    cumw = _dot_split(cum, wide_ref[...], 3)
    return a, dtv, row, col, tri, cum_t, dtw, cumw, cum


def _decay(cum, cum_t, h, keep):
    return jnp.where(keep, jnp.exp(jnp.minimum(cum[:, h:h + 1] - cum_t[h:h + 1, :], 0.0)), 0.0)


def _decay_t(cum, cum_t, h, keep_t):
    return jnp.where(keep_t, jnp.exp(jnp.minimum(cum_t[h:h + 1, :] - cum[:, h:h + 1], 0.0)), 0.0)


def _ssd_fwd(xs_act, bc_act, dt, alog):
    T = xs_act.shape[0]
    q = SSD_CHUNK
    nc = T // q
    to_wide, _ = _ssd_expanders()

    def body(xs_ref, bc_ref, dt_ref, alog_ref, wide_ref, y_ref, sp_ref, st_ref, xd_ref, xde_ref):
        @pl.when(pl.program_id(0) == 0)
        def _():
            st_ref[...] = jnp.zeros_like(st_ref)

        a, dtv, row, col, tri, cum_t, dtw, cumw, segcol = _ssd_prep(dt_ref, alog_ref, wide_ref)
        clw = cumw[q - 1:q, :]
        ecw = jnp.exp(cumw)
        xd = xs_ref[...] * dtw
        xd_ref[...] = xd.astype(BF16)
        xde_ref[...] = (xd * jnp.exp(clw - cumw)).astype(BF16)
        cdw = jnp.exp(clw)
        keep = col <= row
        sp_ref[0] = st_ref[...]
        for g in range(SSD_GROUPS):
            gl = slice(GROUP_LANES * g, GROUP_LANES * (g + 1))
            bb = bc_ref[:, 128 * g:128 * g + 128]
            cb = bc_ref[:, 512 + 128 * g:640 + 128 * g]
            gm = _dot_nt(cb, bb)
            stp = st_ref[g]
            yoff = _dot(cb, stp.astype(BF16)) * ecw[:, gl]
            for r in range(SSD_HPG):
                h = SSD_HPG * g + r
                m = (gm * _decay(segcol, cum_t, h, keep)).astype(BF16)
                y_ref[:, 64 * h:64 * h + 64] = _dot(m, xd_ref[:, 64 * h:64 * h + 64]) + yoff[:, 64 * r:64 * r + 64]
            st_ref[g] = stp * cdw[:, gl] + _dot_tn(bb, xde_ref[:, gl])

    return _pc(
        body, name="ssd_fwd", grid=(nc,),
        in_specs=[_row_spec(q, SSD_INNER), _row_spec(q, CONV_DIM - SSD_INNER), _row_spec(q, DT_PAD),
                  _const_spec((1, DT_PAD)), _const_spec(to_wide.shape)],
        out_specs=[_row_spec(q, SSD_INNER),
                   pl.BlockSpec((1, SSD_GROUPS, SSD_STATE, GROUP_LANES), lambda i: (i, 0, 0, 0))],
        out_shape=[jax.ShapeDtypeStruct((T, SSD_INNER), F32),
                   jax.ShapeDtypeStruct((nc, SSD_GROUPS, SSD_STATE, GROUP_LANES), F32)],
        scratch_shapes=[pltpu.VMEM((SSD_GROUPS, SSD_STATE, GROUP_LANES), F32), pltpu.VMEM((q, SSD_INNER), BF16),
                        pltpu.VMEM((q, SSD_INNER), BF16)],
        compiler_params=_cparams(("arbitrary",)),
    )(xs_act, bc_act, dt, alog, to_wide)


def _ssd_bwd(xs_act, bc_act, dt, alog, sprev, dy):
    T = xs_act.shape[0]
    q = SSD_CHUNK
    nc = T // q
    to_wide, to_heads = _ssd_expanders()

    def rev(width):
        return pl.BlockSpec((q, width), lambda i: (nc - 1 - i, 0))

    def body(xs_ref, bc_ref, dt_ref, alog_ref, sp_ref, dy_ref, wide_ref, heads_ref,
             dxs_ref, dB_ref, dC_ref, ddt_ref, dalog_ref, ds_ref, xd_ref, dxd_ref):
        i = pl.program_id(0)

        @pl.when(i == 0)
        def _():
            ds_ref[...] = jnp.zeros_like(ds_ref)
            dalog_ref[...] = jnp.zeros_like(dalog_ref)

        a, dtv, row, col, tri, cum_t, dtw, cumw, segcol = _ssd_prep(dt_ref, alog_ref, wide_ref)
        clw = cumw[q - 1:q, :]
        ecw = jnp.exp(cumw)
        dew = jnp.exp(clw - cumw)
        cdw = jnp.exp(clw)
        xs = xs_ref[...]
        xd = xs * dtw
        xd_ref[...] = xd.astype(BF16)
        dyv = dy_ref[...]
        dye = (dyv * ecw).astype(BF16)
        xde = (xd * dew).astype(BF16)
        keep = col <= row
        keep_t = col >= row
        rows_k = lax.broadcasted_iota(jnp.int32, (SSD_HPG * q, DT_PAD), 0) // q
        lanes_k = lax.broadcasted_iota(jnp.int32, (SSD_HPG * q, DT_PAD), 1)
        dcw_parts = []
        dcum = jnp.zeros((q, DT_PAD), F32)
        for g in range(SSD_GROUPS):
            gl = slice(GROUP_LANES * g, GROUP_LANES * (g + 1))
            bb = bc_ref[:, 128 * g:128 * g + 128]
            cb = bc_ref[:, 512 + 128 * g:640 + 128 * g]
            gm = _dot_nt(cb, bb)
            gmt = _dot_nt(bb, cb)
            stp = sp_ref[0, g]
            dst = ds_ref[g]
            stpb = stp.astype(BF16)
            dstb = dst.astype(BF16)
            yoff = _dot(cb, stpb) * ecw[:, gl]
            dcg = _dot_nt(dye[:, gl], stpb)
            ds_ref[g] = dst * cdw[:, gl] + _dot_tn(cb, dye[:, gl])
            dlast = jnp.sum(dst * stp, axis=0, keepdims=True) * cdw[:, gl]
            dbg = _dot_nt(xde[:, gl], dstb)
            w = _dot(bb, dstb) * dew[:, gl]
            wx = w * xd[:, gl]
            dlast = dlast + jnp.sum(wx, axis=0, keepdims=True)
            dcw_parts.append(dyv[:, gl] * yoff - wx
                             + jnp.where(lax.broadcasted_iota(jnp.int32, (q, 1), 0) == q - 1, dlast, 0.0))
            dgm = jnp.zeros((q, q), F32)
            diag = []
            for r in range(SSD_HPG):
                h = SSD_HPG * g + r
                hl = slice(64 * h, 64 * h + 64)
                dyb = dy_ref[:, hl].astype(BF16)
                xdh = xd_ref[:, hl]
                dm = _dot_nt(dyb, xdh)
                dmt = _dot_nt(xdh, dyb)
                dec = _decay(segcol, cum_t, h, keep)
                mt = gmt * _decay_t(segcol, cum_t, h, keep_t)
                dgm += dm * dec
                diag.append(dm * (gm * dec) - dmt * mt)
                dxd_ref[:, hl] = _dot(mt.astype(BF16), dyb) + w[:, 64 * r:64 * r + 64]
            onehots = (lanes_k == SSD_HPG * g + rows_k).astype(BF16)
            dcum += _dot_split(jnp.concatenate(diag, axis=1), onehots, 2)
            dgb = dgm.astype(BF16)
            dC_ref[:, 128 * g:128 * g + 128] = dcg + _dot(dgb, bb)
            dB_ref[:, 128 * g:128 * g + 128] = dbg + _dot_tn(dgb, cb)
        dxd = dxd_ref[...]
        dxs_ref[...] = dxd * dtw
        dcum += _dot_split(jnp.concatenate(dcw_parts, axis=1), heads_ref[...], 2)
        dla = _dot_split_rhs(tri, dcum, 3, (((0,), (0,)), ((), ())))
        ddt_ref[...] = _dot_split(xs * dxd, heads_ref[...], 2) + dla * a
        dalog_ref[...] += jnp.sum(dla * dtv, axis=0, keepdims=True)

        @pl.when(i == nc - 1)
        def _():
            dalog_ref[...] = dalog_ref[...] * a

    st_spec = pl.BlockSpec((1, SSD_GROUPS, SSD_STATE, GROUP_LANES), lambda i: (nc - 1 - i, 0, 0, 0))
    return _pc(
        body, name="ssd_bwd", grid=(nc,),
        in_specs=[rev(SSD_INNER), rev(CONV_DIM - SSD_INNER), rev(DT_PAD), _const_spec((1, DT_PAD)), st_spec, rev(SSD_INNER),
                  _const_spec(to_wide.shape), _const_spec(to_heads.shape)],
        out_specs=[rev(SSD_INNER), rev(512), rev(512), rev(DT_PAD), _const_spec((1, DT_PAD))],
        out_shape=[jax.ShapeDtypeStruct((T, SSD_INNER), F32), jax.ShapeDtypeStruct((T, 512), F32),
                   jax.ShapeDtypeStruct((T, 512), F32), jax.ShapeDtypeStruct((T, DT_PAD), F32),
                   jax.ShapeDtypeStruct((1, DT_PAD), F32)],
        scratch_shapes=[pltpu.VMEM((SSD_GROUPS, SSD_STATE, GROUP_LANES), F32), pltpu.VMEM((q, SSD_INNER), BF16),
                        pltpu.VMEM((q, SSD_INNER), F32)],
        compiler_params=_cparams(("arbitrary",)),
    )(xs_act, bc_act, dt, alog, sprev, dy, to_wide, to_heads)


def _s5_disc_vals(a_re, a_im, log_dt, b_re, b_im):
    dt = jnp.exp(log_dt)
    mag = jnp.exp(a_re * dt)
    ab_re = mag * jnp.cos(a_im * dt)
    ab_im = mag * jnp.sin(a_im * dt)
    den = a_re * a_re + a_im * a_im
    nr = ab_re - 1.0
    ni = ab_im
    coef_re = (nr * a_re + ni * a_im) / den
    coef_im = (ni * a_re - nr * a_im) / den
    bb_re = coef_re * b_re - coef_im * b_im
    bb_im = coef_re * b_im + coef_im * b_re
    return ab_re, ab_im, bb_re, bb_im


def _s5_disc(a_re, a_im, log_dt, b_re, b_im):
    def body(ar, ai, ld, br, bi, o1, o2, o3, o4):
        o1[...], o2[...], o3[...], o4[...] = _s5_disc_vals(ar[...], ai[...], ld[...], br[...], bi[...])

    return _pc(
        body, name="s5_disc",
        out_shape=[jax.ShapeDtypeStruct((1, S5_STATES), F32), jax.ShapeDtypeStruct((1, S5_STATES), F32),
                   jax.ShapeDtypeStruct((16, S5_STATES), F32), jax.ShapeDtypeStruct((16, S5_STATES), F32)],
    )(a_re, a_im, log_dt, b_re, b_im)


def _s5_disc_bwd(a_re, a_im, log_dt, b_re, b_im, d_ab_re, d_ab_im, d_bb_re, d_bb_im):
    def body(ar, ai, ld, br, bi, g1, g2, g3, g4, o1, o2, o3, o4, o5):
        _, vjp = jax.vjp(_s5_disc_vals, ar[...], ai[...], ld[...], br[...], bi[...])
        d1, d2, d3, d4, d5 = vjp((g1[...], g2[...], g3[...], g4[...]))
        o1[...] = d1
        o2[...] = d2
        st = lax.broadcasted_iota(jnp.int32, (S5_STATES, DT_PAD), 0)
        grp = lax.broadcasted_iota(jnp.int32, (S5_STATES, DT_PAD), 1)
        sel = (st // 64 == grp).astype(F32)
        o3[...] = _dot_hi(d3, sel)
        o4[...] = d4
        o5[...] = d5

    return _pc(
        body, name="s5_disc_bwd",
        out_shape=[jax.ShapeDtypeStruct((1, S5_STATES), F32), jax.ShapeDtypeStruct((1, S5_STATES), F32),
                   jax.ShapeDtypeStruct((1, DT_PAD), F32),
                   jax.ShapeDtypeStruct((16, S5_STATES), F32), jax.ShapeDtypeStruct((16, S5_STATES), F32)],
    )(a_re, a_im, log_dt, b_re, b_im, d_ab_re, d_ab_im, d_bb_re, d_bb_im)


def _cmul_add(xr, xi, pr, pi, yr, yi):
    return xr + pr * yr - pi * yi, xi + pr * yi + pi * yr


def _powers(ar, ai, n):
    out = [(ar, ai)]
    for _ in range(n - 1):
        pr, pi = out[-1]
        out.append((pr * pr - pi * pi, 2.0 * pr * pi))
    return out


_BW = S5_STATES // S5_BLOCKS
_BI = S5_WIDTH // S5_BLOCKS
SUB = 8
S5_ROWS = S5_CHUNK // SUB


S5_TAB_ROWS = 8 * SUB


def _scan8(br, bi, tab_ref, reverse):
    for level, k in enumerate((1, 2, 4)):
        r0 = 2 * SUB * (level + 1)
        shift = SUB - k if reverse else k
        br, bi = _cmul_add(br, bi, tab_ref[r0:r0 + SUB, :], tab_ref[r0 + SUB:r0 + 2 * SUB, :],
                           pltpu.roll(br, shift, 0), pltpu.roll(bi, shift, 0))
    return br, bi


def _s5_tables(ab_ref, tab_ref, reverse):
    rowin = lax.broadcasted_iota(jnp.int32, (SUB, 1), 0)
    ar = ab_ref[0:1, :]
    ai = -ab_ref[1:2, :] if reverse else ab_ref[1:2, :]
    zero = jnp.zeros((SUB, S5_STATES), F32)
    for level, (pr, pi) in enumerate(_powers(ar, ai, 3)):
        k = 2 ** level
        keep = (rowin < SUB - k) if reverse else (rowin >= k)
        r0 = 2 * SUB * (level + 1)
        tab_ref[r0:r0 + SUB, :] = jnp.where(keep, pr, 0.0) + zero
        tab_ref[r0 + SUB:r0 + 2 * SUB, :] = jnp.where(keep, pi, 0.0) + zero
    hit = rowin == (SUB - 1 if reverse else 0)
    pr, pi = _scan8(jnp.where(hit, ar, 0.0) + zero, jnp.where(hit, ai, 0.0) + zero, tab_ref, reverse)
    tab_ref[0:SUB, :] = pr
    tab_ref[SUB:2 * SUB, :] = pi


def _s5_fwd_chunk(u_ref, y_ref, r0, wb_ref, wc_ref, d_ref, carry_ref, tab_ref, sr_ref, si_ref, between):
    q = S5_CHUNK
    rows = slice(r0, r0 + q)
    for j in range(S5_BLOCKS):
        bu = _dot(u_ref[rows, _BI * j:_BI * (j + 1)].astype(BF16), wb_ref[j])
        sr_ref[:, :, _BW * j:_BW * (j + 1)] = bu[:, :_BW].reshape(S5_ROWS, SUB, _BW)
        si_ref[:, :, _BW * j:_BW * (j + 1)] = bu[:, _BW:].reshape(S5_ROWS, SUB, _BW)
    tr, ti = tab_ref[0:SUB, :], tab_ref[SUB:2 * SUB, :]
    cr, ci = carry_ref[0:1, :], carry_ref[1:2, :]
    for k in range(S5_ROWS):
        sr, si = _scan8(sr_ref[k], si_ref[k], tab_ref, False)
        sr, si = _cmul_add(sr, si, tr, ti, cr, ci)
        sr_ref[k] = sr
        si_ref[k] = si
        cr, ci = sr[SUB - 1:SUB, :], si[SUB - 1:SUB, :]
        between()
    carry_ref[0:1, :] = cr
    carry_ref[1:2, :] = ci
    for j in range(S5_BLOCKS):
        sl = slice(_BW * j, _BW * (j + 1))
        ul = slice(_BI * j, _BI * (j + 1))
        s = jnp.concatenate([sr_ref[:, :, sl].reshape(q, _BW), si_ref[:, :, sl].reshape(q, _BW)], axis=1).astype(BF16)
        y_ref[rows, ul] = _dot(s, wc_ref[j]) + d_ref[:, ul] * u_ref[rows, ul]


def _inproj_s5_fwd(x, g, wp, wb4, wc4, ab, dvec, rider=None):
    T = x.shape[0]
    tm = TOKEN_TILE
    per_tile = tm // S5_CHUNK
    nc = T // S5_CHUNK

    def body(x_ref, g_ref, w_hbm, wb_ref, wc_ref, ab_ref, d_ref, z_ref, xbc_ref, u5_ref, gt_ref, dt_ref, h_ref, y_ref, sp_ref,
             w_ref, carry_ref, tab_ref, sr_ref, si_ref):
        @pl.when(pl.program_id(0) == 0)
        def _():
            pltpu.sync_copy(w_hbm, w_ref)
            carry_ref[...] = jnp.zeros_like(carry_ref)
            _s5_tables(ab_ref, tab_ref, False)

        xn, _ = _rms(x_ref[...])
        h = (xn * g_ref[...]).astype(BF16)
        h_ref[...] = h
        u5_ref[...] = _dot(h, w_ref[:, P_U5:P_G])
        pieces = [(z_ref, P_Z, c0) for c0 in range(0, P_XBC - P_Z, INPROJ_PIECE)]
        pieces += [(xbc_ref, P_XBC, c0) for c0 in range(0, P_U5 - P_XBC, INPROJ_PIECE)]
        pieces += [(gt_ref, P_G, c0) for c0 in range(0, P_DT - P_G, INPROJ_PIECE)]
        todo = iter(pieces)
        slabs, calls = per_tile * S5_ROWS, [0]

        def between():
            calls[0] += 1
            if (calls[0] * len(pieces)) // slabs > ((calls[0] - 1) * len(pieces)) // slabs:
                o_ref, base, c0 = next(todo)
                o_ref[:, c0:c0 + INPROJ_PIECE] = _dot(h, w_ref[:, base + c0:base + c0 + INPROJ_PIECE])

        dt_ref[...] = _dot(h, w_ref[:, P_DT:P_END])
        for c in range(per_tile):
            sp_ref[c] = carry_ref[...]
            _s5_fwd_chunk(u5_ref, y_ref, S5_CHUNK * c, wb_ref, wc_ref, d_ref, carry_ref, tab_ref, sr_ref, si_ref, between)
        assert next(todo, None) is None

    widths = (1024, 2048, 512, 2048, DT_PAD)
    return _call(
        body, rider, name="inproj_s5_fwd", grid=(T // tm,),
        in_specs=[_row_spec(tm, D_MODEL), _const_spec((1, D_MODEL)), _hbm_spec(),
                  _const_spec((S5_BLOCKS, _BI, 2 * _BW)), _const_spec((S5_BLOCKS, 2 * _BW, _BI)),
                  _const_spec((8, S5_STATES)), _const_spec((1, S5_WIDTH))],
        out_specs=[_row_spec(tm, w) for w in widths] + [_row_spec(tm, D_MODEL), _row_spec(tm, S5_WIDTH),
                                                         pl.BlockSpec((per_tile, 8, S5_STATES), lambda i: (i, 0, 0))],
        out_shape=[jax.ShapeDtypeStruct((T, w), F32) for w in widths] + [
            jax.ShapeDtypeStruct((T, D_MODEL), BF16), jax.ShapeDtypeStruct((T, S5_WIDTH), F32),
            jax.ShapeDtypeStruct((nc, 8, S5_STATES), F32)],
        scratch_shapes=[pltpu.VMEM((D_MODEL, P_END), BF16), pltpu.VMEM((8, S5_STATES), F32),
                        pltpu.VMEM((S5_TAB_ROWS, S5_STATES), F32), pltpu.VMEM((S5_ROWS, SUB, S5_STATES), F32),
                        pltpu.VMEM((S5_ROWS, SUB, S5_STATES), F32)],
        compiler_params=_cparams(("arbitrary",)),
    )(x, g, wp, wb4, wc4, ab, dvec)


def _s5_bwd(u5, dy5, wb4, wc4, ab, dvec, sprev, rider=None):
    T = u5.shape[0]
    q = S5_CHUNK
    nc = T // q

    def rev(width):
        return pl.BlockSpec((q, width), lambda i: (nc - 1 - i, 0))

    def body(u_ref, dy_ref, wb_ref, wc_ref, ab_ref, d_ref, sp_ref, du_ref, dwb_ref, dwc_ref, dab_ref, dd_ref,
             carry_ref, tab_ref, rtab_ref, sr_ref, si_ref, lr_ref, li_ref):
        i = pl.program_id(0)
        rowin = lax.broadcasted_iota(jnp.int32, (SUB, 1), 0)

        @pl.when(i == 0)
        def _():
            carry_ref[...] = jnp.zeros_like(carry_ref)
            dwb_ref[...] = jnp.zeros_like(dwb_ref)
            dwc_ref[...] = jnp.zeros_like(dwc_ref)
            dab_ref[...] = jnp.zeros_like(dab_ref)
            dd_ref[...] = jnp.zeros_like(dd_ref)
            _s5_tables(ab_ref, tab_ref, False)
            _s5_tables(ab_ref, rtab_ref, True)

        for j in range(S5_BLOCKS):
            sl = slice(_BW * j, _BW * (j + 1))
            ul = slice(_BI * j, _BI * (j + 1))
            bu = _dot(u_ref[:, ul].astype(BF16), wb_ref[j])
            sr_ref[:, :, sl] = bu[:, :_BW].reshape(S5_ROWS, SUB, _BW)
            si_ref[:, :, sl] = bu[:, _BW:].reshape(S5_ROWS, SUB, _BW)
            ds = _dot_nt(dy_ref[:, ul].astype(BF16), wc_ref[j])
            lr_ref[:, :, sl] = ds[:, :_BW].reshape(S5_ROWS, SUB, _BW)
            li_ref[:, :, sl] = ds[:, _BW:].reshape(S5_ROWS, SUB, _BW)
        ar, ai = ab_ref[0:1, :], ab_ref[1:2, :]
        tr, ti = tab_ref[0:SUB, :], tab_ref[SUB:2 * SUB, :]
        cr, ci = sp_ref[0, 0:1, :], sp_ref[0, 1:2, :]
        for k in range(S5_ROWS):
            sr, si = _scan8(sr_ref[k], si_ref[k], tab_ref, False)
            sr, si = _cmul_add(sr, si, tr, ti, cr, ci)
            sr_ref[k] = sr
            si_ref[k] = si
            cr, ci = sr[SUB - 1:SUB, :], si[SUB - 1:SUB, :]
        tr, ti = rtab_ref[0:SUB, :], rtab_ref[SUB:2 * SUB, :]
        cr, ci = carry_ref[0:1, :], carry_ref[1:2, :]
        acc_r = jnp.zeros((SUB, S5_STATES), F32)
        acc_i = jnp.zeros((SUB, S5_STATES), F32)
        for k in reversed(range(S5_ROWS)):
            lr, li = _scan8(lr_ref[k], li_ref[k], rtab_ref, True)
            lr, li = _cmul_add(lr, li, tr, ti, cr, ci)
            lr_ref[k] = lr
            li_ref[k] = li
            cr, ci = lr[0:1, :], li[0:1, :]
            if k > 0:
                before_r, before_i = sr_ref[k - 1, SUB - 1:SUB, :], si_ref[k - 1, SUB - 1:SUB, :]
            else:
                before_r, before_i = sp_ref[0, 0:1, :], sp_ref[0, 1:2, :]
            keep = rowin >= 1
            pr = jnp.where(keep, pltpu.roll(sr_ref[k], 1, 0), before_r)
            pi = jnp.where(keep, pltpu.roll(si_ref[k], 1, 0), before_i)
            acc_r += lr * pr + li * pi
            acc_i += li * pr - lr * pi
        carry_ref[0:1, :] = cr
        carry_ref[1:2, :] = ci
        dab_ref[0:1, :] += jnp.sum(acc_r, axis=0, keepdims=True)
        dab_ref[1:2, :] += jnp.sum(acc_i, axis=0, keepdims=True)
        for j in range(S5_BLOCKS):
            sl = slice(_BW * j, _BW * (j + 1))
            ul = slice(_BI * j, _BI * (j + 1))
            u = u_ref[:, ul]
            dy = dy_ref[:, ul]
            dyb = dy.astype(BF16)
            lam = jnp.concatenate([lr_ref[:, :, sl].reshape(q, _BW), li_ref[:, :, sl].reshape(q, _BW)], axis=1).astype(BF16)
            s = jnp.concatenate([sr_ref[:, :, sl].reshape(q, _BW), si_ref[:, :, sl].reshape(q, _BW)], axis=1).astype(BF16)
            du_ref[:, ul] = (_dot_nt(lam, wb_ref[j]) + d_ref[:, ul] * dy).astype(BF16)
            dwb_ref[j] += _dot_tn(u.astype(BF16), lam)
            dwc_ref[j] += _dot_tn(s, dyb)
            dd_ref[:, ul] += jnp.sum(dy * u, axis=0, keepdims=True)

    big = pltpu.VMEM((S5_ROWS, SUB, S5_STATES), F32)
    return _call(
        body, rider, name="s5_bwd", grid=(nc,),
        in_specs=[rev(S5_WIDTH), rev(S5_WIDTH), _const_spec((S5_BLOCKS, _BI, 2 * _BW)), _const_spec((S5_BLOCKS, 2 * _BW, _BI)),
                  _const_spec((8, S5_STATES)), _const_spec((1, S5_WIDTH)),
                  pl.BlockSpec((1, 8, S5_STATES), lambda i: (nc - 1 - i, 0, 0))],
        out_specs=[rev(S5_WIDTH), _const_spec((S5_BLOCKS, _BI, 2 * _BW)), _const_spec((S5_BLOCKS, 2 * _BW, _BI)),
                   _const_spec((8, S5_STATES)), _const_spec((1, S5_WIDTH))],
        out_shape=[jax.ShapeDtypeStruct((T, S5_WIDTH), BF16), jax.ShapeDtypeStruct((S5_BLOCKS, _BI, 2 * _BW), F32),
                   jax.ShapeDtypeStruct((S5_BLOCKS, 2 * _BW, _BI), F32), jax.ShapeDtypeStruct((8, S5_STATES), F32),
                   jax.ShapeDtypeStruct((1, S5_WIDTH), F32)],
        scratch_shapes=[pltpu.VMEM((8, S5_STATES), F32), pltpu.VMEM((S5_TAB_ROWS, S5_STATES), F32),
                        pltpu.VMEM((S5_TAB_ROWS, S5_STATES), F32), big, big, big, big],
        compiler_params=_cparams(("arbitrary",)),
    )(u5, dy5, wb4, wc4, ab, dvec, sprev)


def _merge_vals(ys, xs, z, y5, gates, dvec, gssd, glu_w, glu_b, wbr):
    sz = _sigmoid(z)
    qv = ys + dvec * xs
    pre = qv * (z * sz)
    yn, rs = [], []
    for gi in range(SSD_GROUPS):
        p, r = _rms(pre[:, 256 * gi:256 * (gi + 1)])
        yn.append(p)
        rs.append(r)
    yn = jnp.concatenate(yn, axis=1)
    ya = yn * gssd
    pa = _dot(ya.astype(BF16), wbr[0:SSD_INNER, :])
    gel = _gelu(y5)
    pre_g = _dot(gel.astype(BF16), glu_w)
    s0 = _sigmoid(gates[:, :D_MODEL])
    s1 = _sigmoid(gates[:, D_MODEL:])
    sg = _sigmoid(pre_g + glu_b)
    yb = gel * sg
    pb = _dot(yb.astype(BF16), wbr[SSD_INNER:, :])
    merged = s0 * pa + s1 * pb
    return dict(sz=sz, qv=qv, yn=yn, rs=rs, ya=ya, gel=gel, sg=sg, yb=yb, pa=pa, pb=pb, s0=s0, s1=s1, merged=merged)


def _merge_specs(tm):
    acts = [_row_spec(tm, 1024), _row_spec(tm, 1024, 0), _row_spec(tm, 1024), _row_spec(tm, 512), _row_spec(tm, 2048),
            _row_spec(tm, 1024)]
    params = [_const_spec((1, 1024)), _const_spec((1, 1024)), _const_spec((512, 512)), _const_spec((1, 512)), _hbm_spec()]
    return acts, params


def _merge_fwd(ys, xs_act, z, y5, gates, x, dvec, gssd, glu_w, glu_b, wa):
    T = x.shape[0]
    tm = TOKEN_TILE
    acts, params = _merge_specs(tm)

    def body(ys_ref, xs_ref, z_ref, y5_ref, gt_ref, x_ref, dv_ref, gs_ref, gw_ref, gb_ref, wa_hbm, x1_ref,
             wbr_ref, wout_ref):
        @pl.when(pl.program_id(0) == 0)
        def _():
            _load_late_weight(wa_hbm, wbr_ref, "w_branch")
            _load_late_weight(wa_hbm, wout_ref, "w_out")

        v = _merge_vals(ys_ref[...], xs_ref[...], z_ref[...], y5_ref[...], gt_ref[...], dv_ref[...], gs_ref[...],
                        gw_ref[...], gb_ref[...], wbr_ref)
        x1_ref[...] = x_ref[...] + _dot(v["merged"].astype(BF16), wout_ref[...])

    return _pc(
        body, name="merge_fwd", grid=(T // tm,),
        in_specs=acts + params, out_specs=_row_spec(tm, 1024),
        out_shape=jax.ShapeDtypeStruct((T, 1024), F32),
        scratch_shapes=[pltpu.VMEM((1536, 1024), BF16), pltpu.VMEM((1024, 1024), BF16)],
        compiler_params=_cparams(("arbitrary",)),
    )(ys, xs_act, z, y5, gates, x, dvec, gssd, glu_w, glu_b, wa)


def _merge_bwd(ys, xs_act, z, y5, gates, dx1, dvec, gssd, glu_w, glu_b, wa, head_sel, rider=None):
    T = dx1.shape[0]
    tm = TOKEN_TILE
    acts, params = _merge_specs(tm)

    def body(ys_ref, xs_ref, z_ref, y5_ref, gt_ref, dx1_ref, dv_ref, gs_ref, gw_ref, gb_ref, wa_hbm, hs_ref,
             dys_ref, dz_ref, dy5_ref, dgt_ref, mg_ref, ya_ref, yb_ref, dpa_ref, dpb_ref, gel_ref, dpre_ref,
             ddv_ref, dgs_ref, dgb_ref, wbr_ref, wout_ref, ddacc_ref):
        i = pl.program_id(0)

        @pl.when(i == 0)
        def _():
            _load_late_weight(wa_hbm, wbr_ref, "w_branch")
            _load_late_weight(wa_hbm, wout_ref, "w_out")
            ddacc_ref[...] = jnp.zeros_like(ddacc_ref)
            dgs_ref[...] = jnp.zeros_like(dgs_ref)
            dgb_ref[...] = jnp.zeros_like(dgb_ref)

        ys, xs, z, y5, gates = ys_ref[...], xs_ref[...], z_ref[...], y5_ref[...], gt_ref[...]
        dvv, gsv, gw = dv_ref[...], gs_ref[...], gw_ref[...]
        v = _merge_vals(ys, xs, z, y5, gates, dvv, gsv, gw, gb_ref[...], wbr_ref)
        dmg = _dot_nt(dx1_ref[...].astype(BF16), wout_ref[...])
        s0, s1, pa, pb = v["s0"], v["s1"], v["pa"], v["pb"]
        dgt_ref[:, :D_MODEL] = (dmg * pa * s0 * (1.0 - s0)).astype(BF16)
        dgt_ref[:, D_MODEL:] = (dmg * pb * s1 * (1.0 - s1)).astype(BF16)
        dpa = (dmg * s0).astype(BF16)
        dpb = (dmg * s1).astype(BF16)
        dya = _dot_nt(dpa, wbr_ref[0:SSD_INNER, :])
        dyb = _dot_nt(dpb, wbr_ref[SSD_INNER:, :])
        gel, sg = v["gel"], v["sg"]
        dpre = (dyb * gel * sg * (1.0 - sg))
        dgb_ref[...] += jnp.sum(dpre, axis=0, keepdims=True)
        dpre_b = dpre.astype(BF16)
        dgel = dyb * sg + _dot_nt(dpre_b, gw)
        dy5_ref[...] = dgel * _gelu_grad(y5)
        yn = v["yn"]
        dgs_ref[...] += jnp.sum(dya * yn, axis=0, keepdims=True)
        dyn = dya * gsv
        dpre_a = jnp.concatenate(
            [_rms_bwd(yn[:, 256 * gi:256 * (gi + 1)], v["rs"][gi], dyn[:, 256 * gi:256 * (gi + 1)])
             for gi in range(SSD_GROUPS)], axis=1)
        sz, qv = v["sz"], v["qv"]
        dq = dpre_a * (z * sz)
        dz_ref[...] = (dpre_a * qv * (sz * (1.0 + z * (1.0 - sz)))).astype(BF16)
        dys_ref[...] = dq
        ddacc_ref[...] += jnp.sum(dq * xs, axis=0, keepdims=True)
        mg_ref[...] = v["merged"].astype(BF16)
        ya_ref[...] = v["ya"].astype(BF16)
        yb_ref[...] = v["yb"].astype(BF16)
        dpa_ref[...] = dpa
        dpb_ref[...] = dpb
        gel_ref[...] = gel.astype(BF16)
        dpre_ref[...] = dpre_b

        @pl.when(i == pl.num_programs(0) - 1)
        def _():
            ddv_ref[...] = _dot_hi(ddacc_ref[...], hs_ref[...])

    outs = [(1024, F32), (1024, BF16), (512, F32), (2048, BF16),
            (1024, BF16), (1024, BF16), (512, BF16), (1024, BF16), (1024, BF16), (512, BF16), (512, BF16)]
    return _call(
        body, rider, name="merge_bwd", grid=(T // tm,),
        in_specs=acts + params + [_const_spec((1024, DT_PAD))],
        out_specs=[_row_spec(tm, w) for w, _ in outs] + [_const_spec((1, DT_PAD)), _const_spec((1, 1024)), _const_spec((1, 512))],
        out_shape=[jax.ShapeDtypeStruct((T, w), d) for w, d in outs] + [
            jax.ShapeDtypeStruct((1, DT_PAD), F32), jax.ShapeDtypeStruct((1, 1024), F32), jax.ShapeDtypeStruct((1, 512), F32)],
        scratch_shapes=[pltpu.VMEM((1536, 1024), BF16), pltpu.VMEM((1024, 1024), BF16), pltpu.VMEM((1, 1024), F32)],
        compiler_params=_cparams(("arbitrary",)),
    )(ys, xs_act, z, y5, gates, dx1, dvec, gssd, glu_w, glu_b, wa, head_sel)


def _mlp_fwd_loss(x1, target, g, g_fin, wa):
    T = x1.shape[0]
    tm = TOKEN_TILE

    def body(x_ref, t_ref, g_ref, gf_ref, wa_hbm, dx_ref, loss_ref, dg_ref, w1_ref, w2_ref):
        @pl.when(pl.program_id(0) == 0)
        def _():
            _load_late_weight(wa_hbm, w1_ref, "w_mlp_in")
            _load_late_weight(wa_hbm, w2_ref, "w_mlp_out")
            loss_ref[...] = jnp.zeros_like(loss_ref)
            dg_ref[...] = jnp.zeros_like(dg_ref)

        xv = x_ref[...]
        xn, _ = _rms(xv)
        h = (xn * g_ref[...]).astype(BF16)
        acc = xv
        for s in range(FF_SHARDS):
            rl = jnp.maximum(_dot(h, w1_ref[s]), 0.0)
            acc += _dot((rl * rl).astype(BF16), w2_ref[FF_SHARD * s:FF_SHARD * (s + 1), :])
        yn, r = _rms(acc)
        gv = gf_ref[...]
        err = yn * gv - t_ref[...]
        loss_ref[...] += jnp.sum(err * err, axis=0, keepdims=True) * (0.5 / D_MODEL)
        dy = err * (1.0 / D_MODEL)
        dg_ref[...] += jnp.sum(dy * yn, axis=0, keepdims=True)
        dx_ref[...] = _rms_bwd(yn, r, dy * gv)

    return _pc(
        body, name="mlp_fwd_loss", grid=(T // tm,),
        in_specs=[_row_spec(tm, 1024), _row_spec(tm, 1024), _const_spec((1, 1024)), _const_spec((1, 1024)), _hbm_spec()],
        out_specs=[_row_spec(tm, 1024), _const_spec((1, 1024)), _const_spec((1, 1024))],
        out_shape=[jax.ShapeDtypeStruct((T, 1024), F32), jax.ShapeDtypeStruct((1, 1024), F32),
                   jax.ShapeDtypeStruct((1, 1024), F32)],
        scratch_shapes=[pltpu.VMEM((FF_SHARDS, D_MODEL, FF_SHARD), BF16), pltpu.VMEM((D_FF, D_MODEL), BF16)],
        compiler_params=_cparams(("arbitrary",)),
    )(x1, target, g, g_fin, wa)


def _mlp_bwd(x1, dx2, g, wa):
    T = x1.shape[0]
    tm = TOKEN_TILE

    def body(x_ref, dx2_ref, g_ref, wa_hbm, dx1_ref, h_ref, act_ref, da_ref, dg_ref, w1_ref, w2_ref):
        @pl.when(pl.program_id(0) == 0)
        def _():
            _load_late_weight(wa_hbm, w1_ref, "w_mlp_in")
            _load_late_weight(wa_hbm, w2_ref, "w_mlp_out")
            dg_ref[...] = jnp.zeros_like(dg_ref)

        xn, r = _rms(x_ref[...])
        gv = g_ref[...]
        h = (xn * gv).astype(BF16)
        h_ref[...] = h
        dx2 = dx2_ref[...]
        dx2b = dx2.astype(BF16)
        dh = jnp.zeros((tm, D_MODEL), F32)
        for s in range(FF_SHARDS):
            ff = slice(FF_SHARD * s, FF_SHARD * (s + 1))
            rl = jnp.maximum(_dot(h, w1_ref[s]), 0.0)
            act_ref[:, ff] = (rl * rl).astype(BF16)
            da = (_dot_nt(dx2b, w2_ref[ff, :]) * (2.0 * rl)).astype(BF16)
            da_ref[:, ff] = da
            dh += _dot_nt(da, w1_ref[s])
        dg_ref[...] += jnp.sum(dh * xn, axis=0, keepdims=True)
        dx1_ref[...] = dx2 + _rms_bwd(xn, r, dh * gv)

    return _pc(
        body, name="mlp_bwd", grid=(T // tm,),
        in_specs=[_row_spec(tm, 1024), _row_spec(tm, 1024), _const_spec((1, 1024)), _hbm_spec()],
        out_specs=[_row_spec(tm, 1024), _row_spec(tm, 1024), _row_spec(tm, D_FF), _row_spec(tm, D_FF), _const_spec((1, 1024))],
        out_shape=[jax.ShapeDtypeStruct((T, 1024), F32), jax.ShapeDtypeStruct((T, 1024), BF16),
                   jax.ShapeDtypeStruct((T, D_FF), BF16), jax.ShapeDtypeStruct((T, D_FF), BF16),
                   jax.ShapeDtypeStruct((1, 1024), F32)],
        scratch_shapes=[pltpu.VMEM((FF_SHARDS, D_MODEL, FF_SHARD), BF16), pltpu.VMEM((D_FF, D_MODEL), BF16)],
        compiler_params=_cparams(("arbitrary",)),
    )(x1, dx2, g, wa)


WGRAD_OUT_ELEMS = 2 * 1024 * 1024
WGRAD_TILE_BYTES = 8 * 1024 * 1024


def _wgrad(a, b, name, col_shards=None, row_shards_into=None):
    T, K = a.shape
    N = b.shape[1]
    nb = N // col_shards if col_shards else min(N, 1024, max(128, WGRAD_OUT_ELEMS // K))
    tt = min(T, WGRAD_TOKENS)
    while tt * max(K * a.dtype.itemsize, nb * b.dtype.itemsize) > WGRAD_TILE_BYTES:
        tt //= 2
    assert N % nb == 0 and T % tt == 0
    in_specs = [pl.BlockSpec((tt, K), lambda n, t: (t, 0)), pl.BlockSpec((tt, nb), lambda n, t: (t, n))]
    args, aliases = [a, b], {}
    if col_shards:
        out_spec = pl.BlockSpec((None, None, K, nb), lambda n, t: (n, 0, 0, 0))
        out_shape = jax.ShapeDtypeStruct((col_shards, 2, K, nb), F32)
    elif row_shards_into is not None:
        shards, _, rows, cols = row_shards_into.shape
        assert shards * rows == K and cols == N
        out_spec = pl.BlockSpec((shards, None, rows, nb), lambda n, t: (0, 1, 0, n))
        out_shape = jax.ShapeDtypeStruct(row_shards_into.shape, F32)
        in_specs.append(_hbm_spec())
        args.append(row_shards_into)
        aliases = {2: 0}
    else:
        out_spec = pl.BlockSpec((K, nb), lambda n, t: (0, n))
        out_shape = jax.ShapeDtypeStruct((K, N), F32)

    def body(a_ref, b_ref, *rest):
        o_ref = rest[-1]

        @pl.when(pl.program_id(1) == 0)
        def _():
            o_ref[...] = jnp.zeros_like(o_ref)

        o_ref[...] += _dot_tn(a_ref[...].astype(BF16), b_ref[...].astype(BF16)).reshape(o_ref.shape)

    return _pc(
        body, name=name, grid=(N // nb, T // tt), in_specs=in_specs, out_specs=out_spec, out_shape=out_shape,
        input_output_aliases=aliases, compiler_params=_cparams(("parallel", "arbitrary")),
    )(*args)


def _s5_block_weights(bb_re, bb_im, c_re, c_im):
    eye = jnp.eye(8, dtype=F32)
    bre = bb_re.reshape(16, S5_BLOCKS, 8, 64)
    bim = bb_im.reshape(16, S5_BLOCKS, 8, 64)
    wb_re = jnp.einsum('kjgp,gh->jhkgp', bre, eye).reshape(S5_BLOCKS, _BI, _BW)
    wb_im = jnp.einsum('kjgp,gh->jhkgp', bim, eye).reshape(S5_BLOCKS, _BI, _BW)
    wb4 = jnp.concatenate([wb_re, wb_im], axis=2).astype(BF16)
    cre = c_re.reshape(S5_BLOCKS, 8, 16, 64)
    cim = c_im.reshape(S5_BLOCKS, 8, 16, 64)
    wc_re = jnp.einsum('jgkp,gh->jgphk', cre, eye).reshape(S5_BLOCKS, _BW, _BI)
    wc_im = jnp.einsum('jgkp,gh->jgphk', -cim, eye).reshape(S5_BLOCKS, _BW, _BI)
    wc4 = jnp.concatenate([wc_re, wc_im], axis=1).astype(BF16)
    return wb4, wc4


def _s5_block_grads(dwb4, dwc4):
    eye = jnp.eye(8, dtype=F32)
    dwb = dwb4.reshape(S5_BLOCKS, 8, 16, 2, 8, 64)
    dbb = jnp.einsum('jhkrgp,gh->rkjgp', dwb, eye).reshape(2, 16, S5_STATES)
    dwc = dwc4.reshape(S5_BLOCKS, 2, 8, 64, 8, 16)
    dc = jnp.einsum('jrgphk,gh->rjgkp', dwc, eye).reshape(2, 32, 16, 64)
    return dbb[0], dbb[1], dc[0], -dc[1]


def _row(v, width=None):
    v = v.reshape(1, -1)
    if width is not None and v.shape[1] < width:
        v = jnp.concatenate([v, jnp.zeros((1, width - v.shape[1]), v.dtype)], axis=1)
    return v


def _local_step(x, target, p, comm=None):
    g_mix, g_mlp, g_fin = _row(p["norm_mix_g"]), _row(p["norm_mlp_g"]), _row(p["norm_final_g"])
    conv_b = _row(p["conv_b"])
    dt_bias = _row(p["dt_bias"], DT_PAD)
    alog = _row(p["a_log"], DT_PAD)
    dvec = _row(jnp.repeat(p["d_ssd"], SSD_HEADDIM))
    gssd = _row(p["ssd_norm_g"])
    s5d = _row(p["s5_d"])
    glu_b = _row(p["s5_glu_b"])
    head_sel = (jnp.arange(SSD_INNER)[:, None] // SSD_HEADDIM == jnp.arange(DT_PAD)[None, :]).astype(F32)

    a_re = p["s5_a_re"].reshape(1, S5_STATES)
    a_im = p["s5_a_im"].reshape(1, S5_STATES)
    log_dt = jnp.repeat(p["s5_log_dt"], 64).reshape(1, S5_STATES)
    b_re = p["s5_b_re"].reshape(S5_STATES, 16).T
    b_im = p["s5_b_im"].reshape(S5_STATES, 16).T
    ab_re, ab_im, bb_re, bb_im = _s5_disc(a_re, a_im, log_dt, b_re, b_im)
    wb4, wc4 = _s5_block_weights(bb_re, bb_im, p["s5_c_re"], p["s5_c_im"])
    ab = jnp.concatenate([ab_re, ab_im, jnp.zeros((6, S5_STATES), F32)], axis=0)

    wp = p["w_in_perm"]

    first_args = (x, g_mix, wp, wb4, wc4, ab, s5d)
    if comm is None:
        z, xbc_raw, u5, gates, dt_raw, h, y5, s5_states = _inproj_s5_fwd(*first_args)
    else:
        first_out, late = _inproj_s5_fwd(*first_args, rider=_Gather(comm["late_srcs"], comm["late_ks"]))
        z, xbc_raw, u5, gates, dt_raw, h, y5, s5_states = first_out
        p = {**p, **comm["late_unpack"](late)}
    xs_act, bc_act, dt = _conv_fwd(xbc_raw, dt_raw, p["conv_w"], conv_b, dt_bias)
    ys, ssd_states = _ssd_fwd(xs_act, bc_act, dt, alog)
    wa, glu_w = p["late_weights"], p["s5_glu_w"]
    x1 = _merge_fwd(ys, xs_act, z, y5, gates, x, dvec, gssd, glu_w, glu_b, wa)
    dx2, loss_lanes, d_gfin = _mlp_fwd_loss(x1, target, g_mlp, g_fin, wa)

    dx1, h2, act, da1, d_gmlp = _mlp_bwd(x1, dx2, g_mlp, wa)
    g_mlp4 = _wgrad(h2, da1, "wgrad_mlp_in", col_shards=FF_SHARDS)
    g_mlp4 = _wgrad(act, dx2, "wgrad_mlp_out", row_shards_into=g_mlp4)
    d_w_mlp_in, d_w_mlp_out = g_mlp4[:, 0], g_mlp4[:, 1].reshape(D_FF, D_MODEL)
    merge_args = (ys, xs_act, z, y5, gates, dx1, dvec, gssd, glu_w, glu_b, wa, head_sel)
    if comm is None:
        merge_out = _merge_bwd(*merge_args)
    else:
        g_mlp = g_mlp4.reshape(N_CHIPS, 2 * FF_SHARD, D_MODEL)
        merge_out, (sib_mlp,) = _merge_bwd(*merge_args, rider=_Pair([g_mlp]))
        pf_mlp, pb_mlp = _pair_sum(comm["place"], g_mlp, sib_mlp, "pair_sum_mlp")
    (dys, dz, dy5, dgates, mg, ya, yb, dpa, dpb, gel, dpre, d_dssd, d_gssd, d_glu_b) = merge_out
    d_w_out = _wgrad(mg, dx1, "wgrad_out")
    d_w_branch = jnp.concatenate([_wgrad(ya, dpa, "wgrad_branch_a"), _wgrad(yb, dpb, "wgrad_branch_b")], axis=0)
    d_glu_w = _wgrad(gel, dpre, "wgrad_glu")
    s5_args = (u5, dy5, wb4, wc4, ab, s5d, s5_states)
    if comm is None:
        du5, dwb4, dwc4, dab, d_s5d = _s5_bwd(*s5_args)
        mlp_total = None
    else:
        (du5, dwb4, dwc4, dab, d_s5d), (got_mlp,) = _s5_bwd(*s5_args, rider=_Chip([pb_mlp]))
        mlp_total = _chip_sum(comm["place"], pf_mlp, got_mlp, "chip_sum_mlp")
    dbb_re, dbb_im, d_c_re, d_c_im = _s5_block_grads(dwb4, dwc4)
    d_a_re, d_a_im, d_log_dt, d_b_re, d_b_im = _s5_disc_bwd(
        a_re, a_im, log_dt, b_re, b_im, dab[0:1], dab[1:2], dbb_re, dbb_im)
    dxs_s, dB, dC, ddt, d_alog = _ssd_bwd(xs_act, bc_act, dt, alog, ssd_states, dys)
    dxbc_raw, ddt_raw, d_conv_w, d_conv_b, d_dt_bias = _conv_bwd(
        xbc_raw, dt_raw, dys, dvec, dxs_s, dB, dC, ddt, p["conv_w"], conv_b, dt_bias)
    d_w_in = dict(z=_wgrad(h, dz, "wgrad_in_z"), xbc=_wgrad(h, dxbc_raw, "wgrad_in_xbc"),
                  dt=_wgrad(h, ddt_raw, "wgrad_in_dt")[:, :16], u5=_wgrad(h, du5, "wgrad_in_u5"),
                  gates=_wgrad(h, dgates, "wgrad_in_gates"))
    w_in_pieces = [(c0, d_w_in[n]) for n, c0, _ in W_IN_PIECES]
    inproj_args = (x, dx1, dz, dxbc_raw, du5, dgates, ddt_raw, g_mix, wp)
    if comm is None:
        dx, d_gmix = _inproj_bwd(*inproj_args)
        late_totals = None
    else:
        g_b, g_in = _late_buffers(d_w_out, d_w_branch, d_glu_w, d_conv_w[:CONV_K], w_in_pieces)
        sib_b, sib_in = _exchange(_Pair([g_b, g_in]), "pair_exchange")
        pf_b, pb_b = _pair_sum(comm["place"], g_b, sib_b, "pair_sum_b")
        pf_in, pb_in = _pair_sum(comm["place"], g_in, sib_in, "pair_sum_in")
        (dx, d_gmix), (got_b, got_in) = _inproj_bwd(*inproj_args, rider=_Chip([pb_b, pb_in]))
        late_totals = (_chip_sum(comm["place"], pf_b, got_b, "chip_sum_b"),
                       _chip_sum(comm["place"], pf_in, got_in, "chip_sum_in"))

    grads = dict(
        norm_mix_g=d_gmix.reshape(-1), w_in_pieces=w_in_pieces, late_totals=late_totals,
        conv_w=d_conv_w[:CONV_K], conv_b=d_conv_b.reshape(-1),
        dt_bias=d_dt_bias[0, :16], a_log=d_alog[0, :16], d_ssd=d_dssd[0, :16], ssd_norm_g=d_gssd.reshape(-1),
        s5_a_re=d_a_re.reshape(32, 64), s5_a_im=d_a_im.reshape(32, 64), s5_log_dt=d_log_dt[0, :32],
        s5_b_re=d_b_re.T.reshape(32, 64, 16), s5_b_im=d_b_im.T.reshape(32, 64, 16), s5_c_re=d_c_re, s5_c_im=d_c_im,
        s5_d=d_s5d.reshape(-1), s5_glu_w=d_glu_w, s5_glu_b=d_glu_b.reshape(-1), w_branch=d_w_branch, w_out=d_w_out,
        norm_mlp_g=d_gmlp.reshape(-1), w_mlp_in=d_w_mlp_in, w_mlp_out=d_w_mlp_out, norm_final_g=d_gfin.reshape(-1),
        mlp_total=mlp_total)
    return jnp.sum(loss_lanes), dx, grads


MESH = pl.DeviceIdType.MESH
N_CHIPS = 4


def _place():
    x, y, c = lax.axis_index("x"), lax.axis_index("y"), lax.axis_index("c")
    chips = [(1 - x, y), (x, 1 - y), (1 - x, 1 - y)]
    return x, y, c, chips


def _remote(src, dst, send_sems, recv_sems, k, to):
    return pltpu.make_async_remote_copy(src_ref=src, dst_ref=dst, send_sem=send_sems.at[k], recv_sem=recv_sems.at[k],
                                        device_id=to, device_id_type=MESH)


def _row_chunks(rows, k, align):
    step = rows // k
    assert rows % k == 0 and step % align == 0, (rows, k, align)
    return [(i * step, step) for i in range(k)]


ICI_CHUNKS = 4
D2D_CHUNKS = 24


class _Gather:
    def __init__(self, srcs, ks):
        self.inputs = list(srcs)
        self.out_shapes = [jax.ShapeDtypeStruct((N_CHIPS,) + a.shape, a.dtype) for a in srcs]
        self.halves = [a.shape[0] // 2 for a in srcs]
        self.pieces = [_row_chunks(h, k, 32 // a.dtype.itemsize) for a, h, k in zip(srcs, self.halves, ks)]
        self.n_ici = 3 * sum(ks)
        self.n_sems = 2 * self.n_ici + len(srcs)

    def _plan(self, src_refs, out_refs, send_sems, recv_sems):
        x, y, c, chips = _place()
        own = 2 * x + y
        sib = (x, y, 1 - c)
        first, fwd_plan, k = [], [], 0
        for a, (src_ref, out_ref) in enumerate(zip(src_refs, out_refs)):
            h = self.halves[a]
            for r0, nr in self.pieces[a]:
                for cx, cy in chips:
                    first.append(_remote(src_ref.at[pl.ds(c * h + r0, nr), :], out_ref.at[own, pl.ds(c * h + r0, nr), :],
                                         send_sems, recv_sems, k, (cx, cy, c)))
                    fwd_plan.append((out_ref, 2 * cx + cy, h, r0, nr, k, (cx, cy, c)))
                    k += 1
        for a, (src_ref, out_ref) in enumerate(zip(src_refs, out_refs)):
            first.append(_remote(src_ref, out_ref.at[own], send_sems, recv_sems, 2 * self.n_ici + a, sib))
        return first, fwd_plan, c, sib

    def issue(self, src_refs, out_refs, send_sems, recv_sems):
        for cp in self._plan(src_refs, out_refs, send_sems, recv_sems)[0]:
            cp.start()

    def complete(self, src_refs, out_refs, send_sems, recv_sems):
        first, fwd_plan, c, sib = self._plan(src_refs, out_refs, send_sems, recv_sems)
        passed = []
        for out_ref, s, h, r0, nr, k, frm in fwd_plan:
            got = out_ref.at[s, pl.ds(c * h + r0, nr), :]
            _remote(got, got, send_sems, recv_sems, k, frm).wait_recv()
            fw = _remote(got, got, send_sems, recv_sems, self.n_ici + k, sib)
            fw.start()
            passed.append(fw)
        for out_ref, s, h, r0, nr, k, frm in fwd_plan:
            got = out_ref.at[s, pl.ds((1 - c) * h + r0, nr), :]
            _remote(got, got, send_sems, recv_sems, self.n_ici + k, sib).wait_recv()
        own_copies = first[self.n_ici:]
        for cp in own_copies:
            cp.wait_recv()
        for cp in first + passed:
            cp.wait_send()


def _exchange(rider, name):
    ri, ro = len(rider.inputs), len(rider.out_shapes)

    def body(*refs):
        rider.issue(refs[:ri], refs[ri:ri + ro], *refs[ri + ro:])
        rider.complete(refs[:ri], refs[ri:ri + ro], *refs[ri + ro:])

    return _pc(
        body, name=name, in_specs=[_hbm_spec()] * ri, out_specs=[_hbm_spec()] * ro, out_shape=list(rider.out_shapes),
        scratch_shapes=[pltpu.SemaphoreType.DMA((rider.n_sems,))] * 2,
    )(*rider.inputs)


def _call(body, rider=None, **kw):
    if rider is None:
        return _pc(body, **kw)
    single = not isinstance(kw["out_shape"], (list, tuple))
    out_specs = [kw["out_specs"]] if single else list(kw["out_specs"])
    out_shape = [kw["out_shape"]] if single else list(kw["out_shape"])
    scratch = list(kw.get("scratch_shapes", ()))
    n_in, n_out, n_scr = len(kw["in_specs"]), len(out_specs), len(scratch)
    ri, ro = len(rider.inputs), len(rider.out_shapes)
    steps = kw["grid"][0]

    def wrapped(*refs):
        o0 = n_in + ri
        s0 = o0 + n_out + ro
        r_in, r_out, sems = refs[n_in:o0], refs[o0 + n_out:s0], refs[s0 + n_scr:]

        @pl.when(pl.program_id(0) == 0)
        def _():
            rider.issue(r_in, r_out, *sems)

        body(*refs[:n_in], *refs[o0:o0 + n_out], *refs[s0:s0 + n_scr])

        @pl.when(pl.program_id(0) == steps - 1)
        def _():
            rider.complete(r_in, r_out, *sems)

    f = _pc(wrapped, name=kw["name"], grid=kw["grid"], in_specs=list(kw["in_specs"]) + [_hbm_spec()] * ri,
            out_specs=out_specs + [_hbm_spec()] * ro, out_shape=out_shape + list(rider.out_shapes),
            scratch_shapes=scratch + [pltpu.SemaphoreType.DMA((rider.n_sems,))] * 2, compiler_params=kw["compiler_params"])

    def run(*args):
        res = f(*args, *rider.inputs)
        return (res[0] if single else res[:n_out]), res[n_out:]

    return run


def _d2d_pieces(rows):
    k = next(k for k in range(24, 0, -1) if rows % k == 0 and (rows // k) % 8 == 0)
    return _row_chunks(rows, k, 8)


class _Pair:
    def __init__(self, gs, small=None):
        self.n = len(gs)
        self.halves = [g.shape[1] // 2 for g in gs]
        self.inputs = list(gs) + ([small] if small is not None else [])
        self.out_shapes = [jax.ShapeDtypeStruct((N_CHIPS, h, g.shape[2]), F32) for g, h in zip(gs, self.halves)]
        if small is not None:
            self.out_shapes.append(jax.ShapeDtypeStruct(small.shape, F32))
        self.n_sems = len(self.inputs)

    def issue(self, in_refs, out_refs, send_sems, recv_sems):
        x, y, c, _ = _place()
        sib = (x, y, 1 - c)
        for a in range(self.n):
            h = self.halves[a]
            for s in range(N_CHIPS):
                for r0, nr in _d2d_pieces(h):
                    _remote(in_refs[a].at[s, pl.ds((1 - c) * h + r0, nr), :], out_refs[a].at[s, pl.ds(r0, nr), :],
                            send_sems, recv_sems, a, sib).start()
        for a in range(self.n, len(self.inputs)):
            _remote(in_refs[a], out_refs[a], send_sems, recv_sems, a, sib).start()

    def complete(self, in_refs, out_refs, send_sems, recv_sems):
        x, y, c, _ = _place()
        for a in range(len(self.inputs)):
            _remote(out_refs[a], out_refs[a], send_sems, recv_sems, a, (x, y, 1 - c)).wait()


SUM_BLOCKS = 4


def _pair_sum(place, g, sib, name):
    n, R, C = g.shape
    H = R // 2
    rb = H // SUM_BLOCKS
    assert H % SUM_BLOCKS == 0 and rb % 16 == 0

    def body(place_ref, a_ref, b_ref, pf_ref, pb_ref):
        p = a_ref[...] + b_ref[...]
        pf_ref[...] = p
        pb_ref[...] = p.astype(BF16)

    blk = pl.BlockSpec((1, rb, C), lambda s, i, pr: (s, i, 0))
    mine = pl.BlockSpec((1, rb, C), lambda s, i, pr: (s, pr[1] * SUM_BLOCKS + i, 0))
    return _pc(
        body, name=name, out_shape=[jax.ShapeDtypeStruct((n, H, C), F32), jax.ShapeDtypeStruct((n, H, C), BF16)],
        grid_spec=pltpu.PrefetchScalarGridSpec(num_scalar_prefetch=1, grid=(n, SUM_BLOCKS), in_specs=[mine, blk],
                                               out_specs=[blk, blk]),
        compiler_params=_cparams(("arbitrary", "arbitrary")),
    )(place, g, sib)


class _Chip:
    def __init__(self, pbs, psmall=None):
        self.n = len(pbs)
        self.rows = [pb.shape[1] for pb in pbs]
        self.inputs = list(pbs) + ([psmall] if psmall is not None else [])
        self.out_shapes = [jax.ShapeDtypeStruct((3,) + pb.shape[1:], BF16) for pb in pbs]
        if psmall is not None:
            self.out_shapes.append(jax.ShapeDtypeStruct((N_CHIPS,) + psmall.shape, F32))
        self.n_sems = 3 * len(self.inputs)

    def issue(self, in_refs, out_refs, send_sems, recv_sems):
        x, y, c, chips = _place()
        own = 2 * x + y
        for j, (cx, cy) in enumerate(chips):
            for a in range(self.n):
                for r0, nr in _row_chunks(self.rows[a], ICI_CHUNKS, 16):
                    _remote(in_refs[a].at[2 * cx + cy, pl.ds(r0, nr), :], out_refs[a].at[j, pl.ds(r0, nr), :],
                            send_sems, recv_sems, 3 * a + j, (cx, cy, c)).start()
            for a in range(self.n, len(self.inputs)):
                _remote(in_refs[a], out_refs[a].at[own], send_sems, recv_sems, 3 * a + j, (cx, cy, c)).start()

    def complete(self, in_refs, out_refs, send_sems, recv_sems):
        x, y, c, chips = _place()
        own = 2 * x + y
        for j, (cx, cy) in enumerate(chips):
            for a in range(self.n):
                _remote(in_refs[a].at[own], out_refs[a].at[j], send_sems, recv_sems, 3 * a + j, (cx, cy, c)).wait()
            for a in range(self.n, len(self.inputs)):
                _remote(in_refs[a], out_refs[a].at[2 * cx + cy], send_sems, recv_sems, 3 * a + j, (cx, cy, c)).wait()


def _chip_sum(place, pf, got, name):
    _, H, C = pf.shape
    rb = H // SUM_BLOCKS

    def body(place_ref, o_ref, g_ref, tot_ref):
        tot_ref[...] = ((o_ref[0] + g_ref[0].astype(F32)) + g_ref[1].astype(F32)) + g_ref[2].astype(F32)

    ins = [pl.BlockSpec((1, rb, C), lambda i, pr: (pr[0], i, 0)), pl.BlockSpec((3, rb, C), lambda i, pr: (0, i, 0))]
    out = pl.BlockSpec((rb, C), lambda i, pr: (pr[1] * SUM_BLOCKS + i, 0))
    return _pc(
        body, name=name, out_shape=jax.ShapeDtypeStruct((2 * H, C), F32),
        grid_spec=pltpu.PrefetchScalarGridSpec(num_scalar_prefetch=1, grid=(SUM_BLOCKS,), in_specs=ins, out_specs=out),
        compiler_params=_cparams(("arbitrary",)),
    )(place, pf, got)


def _half_exchange(fulls):
    n = len(fulls)

    def body(*refs):
        in_refs, out_refs = refs[:n], refs[n:2 * n]
        send_sems, recv_sems = refs[2 * n:]
        x, y, c, _ = _place()
        sib = (x, y, 1 - c)
        for a in range(n):
            h = fulls[a].shape[0] // 2
            for r0, nr in _d2d_pieces(h):
                rows = pl.ds(c * h + r0, nr)
                _remote(in_refs[a].at[rows, :], out_refs[a].at[rows, :], send_sems, recv_sems, a, sib).start()
        for a in range(n):
            h = fulls[a].shape[0] // 2
            _remote(in_refs[a].at[pl.ds(c * h, h), :], out_refs[a].at[pl.ds((1 - c) * h, h), :], send_sems, recv_sems, a,
                    sib).wait()

    return _pc(
        body, name="half_exchange", in_specs=[_hbm_spec()] * n, out_specs=[_hbm_spec()] * n,
        out_shape=[jax.ShapeDtypeStruct(f.shape, F32) for f in fulls],
        input_output_aliases={a: a for a in range(n)},
        scratch_shapes=[pltpu.SemaphoreType.DMA((n,)), pltpu.SemaphoreType.DMA((n,))],
    )(*fulls)


def _small_allreduce(pack):
    R, C = pack.shape

    def body(p_ref, o_ref, sib_ref, pair_ref, slots_ref, send_sems, recv_sems):
        x, y, c, chips = _place()
        own = 2 * x + y
        cp = _remote(p_ref, sib_ref, send_sems, recv_sems, 0, (x, y, 1 - c))
        cp.start()
        cp.wait()
        pair_ref[...] = p_ref[...] + sib_ref[...]
        slots_ref[own] = pair_ref[...]
        out = [_remote(pair_ref, slots_ref.at[own], send_sems, recv_sems, 1 + j, (cx, cy, c)) for j, (cx, cy) in enumerate(chips)]
        for cp in out:
            cp.start()
        for j, (cx, cy) in enumerate(chips):
            _remote(pair_ref, slots_ref.at[2 * cx + cy], send_sems, recv_sems, 1 + j, (cx, cy, c)).wait()
        o_ref[...] = ((slots_ref[0] + slots_ref[1]) + slots_ref[2]) + slots_ref[3]

    vmem = pl.BlockSpec(memory_space=pltpu.VMEM)
    return _pc(
        body, name="small_allreduce", in_specs=[vmem], out_specs=vmem, out_shape=jax.ShapeDtypeStruct((R, C), F32),
        scratch_shapes=[pltpu.VMEM((R, C), F32), pltpu.VMEM((R, C), F32), pltpu.VMEM((N_CHIPS, R, C), F32),
                        pltpu.SemaphoreType.DMA((4,)), pltpu.SemaphoreType.DMA((4,))],
    )(pack)


def _adamw(w, g, m, v, name, g_row0=0, with_grad=False, col_block=None):
    R, C = w.shape
    rb = 256 if R % 256 == 0 else (128 if R % 128 == 0 else R)
    if col_block:
        rb = R
    assert g_row0 % rb == 0

    def body(w_ref, g_ref, m_ref, v_ref, d_ref, nm_ref, nv_ref, *g_out):
        gv = g_ref[...]
        m2 = ADAM_B1 * m_ref[...] + (1.0 - ADAM_B1) * gv
        v2 = ADAM_B2 * v_ref[...] + (1.0 - ADAM_B2) * (gv * gv)
        m_hat = m2 * (1.0 / (1.0 - ADAM_B1 ** ADAM_STEP))
        v_hat = v2 * (1.0 / (1.0 - ADAM_B2 ** ADAM_STEP))
        d_ref[...] = -ADAM_LR * (m_hat / (jnp.sqrt(v_hat) + ADAM_EPS) + ADAM_WD * w_ref[...])
        nm_ref[...] = m2
        nv_ref[...] = v2
        if with_grad:
            g_out[0][...] = gv

    if col_block:
        spec = g_spec = pl.BlockSpec((R, col_block), lambda i: (0, i))
        steps = C // col_block
    else:
        spec = pl.BlockSpec((rb, C), lambda i: (i, 0))
        g_spec = pl.BlockSpec((rb, C), lambda i: (g_row0 // rb + i, 0))
        steps = R // rb
    n_out = 4 if with_grad else 3
    return _pc(
        body, name=name, grid=(steps,), in_specs=[spec, g_spec, spec, spec], out_specs=[spec] * n_out,
        out_shape=[jax.ShapeDtypeStruct((R, C), F32)] * n_out, compiler_params=_cparams(("parallel",)),
    )(w, g, m, v)


PACK_COLS = 1024
ROWS_A = (("w_mlp_in", 0, 1024), ("w_mlp_out", 1024, 1024), ("w_out", 2048, 256), ("w_branch", 2304, 384))
ROWS_A_TOTAL = 2688
ROWS_B = (("w_out", 0, 256), ("w_branch", 256, 384))
ROW_B_GLU, ROW_B_CONV, ROWS_B_TOTAL = 640, 704, 768
W_IN_SHARD = 1412
CONV_PAD_ROWS = 16
SMALL = (("norm_mix_g", (1024,)), ("conv_b", (2048,)), ("dt_bias", (16,)), ("a_log", (16,)), ("d_ssd", (16,)),
         ("ssd_norm_g", (1024,)), ("s5_a_re", (32, 64)), ("s5_a_im", (32, 64)), ("s5_log_dt", (32,)),
         ("s5_b_re", (32, 64, 16)), ("s5_b_im", (32, 64, 16)), ("s5_c_re", (32, 16, 64)), ("s5_c_im", (32, 16, 64)),
         ("s5_d", (512,)), ("s5_glu_b", (512,)), ("norm_mlp_g", (1024,)), ("norm_final_g", (1024,)))
SMALL_ROWS = 144
SMALL_COUNT = sum(math.prod(shp) for _, shp in SMALL)
GLU_ROWS = S5_WIDTH * S5_WIDTH // PACK_COLS
CONV_ROWS = CONV_K * CONV_DIM // PACK_COLS
W_IN_PIECES = (("z", 0, 1024), ("xbc", 1024, 2048), ("dt", OFF_DT, 16), ("u5", OFF_U, 512), ("gates", 3600, 2048))


def _pack_small(parts):
    flat = jnp.concatenate([a.astype(F32).reshape(-1) for a in parts])
    return jnp.concatenate([flat, jnp.zeros((SMALL_ROWS * PACK_COLS - flat.shape[0],), F32)]).reshape(SMALL_ROWS, PACK_COLS)


def _unpack_small(pack):
    flat, out, r = pack.reshape(-1), {}, 0
    for name, shp in SMALL:
        n = math.prod(shp)
        out[name] = flat[r:r + n].reshape(shp)
        r += n
    return out


def _late_buffers(d_w_out, d_w_branch, d_glu_w, d_conv_w, w_in_pieces):
    conv4 = d_conv_w.reshape(CONV_K, N_CHIPS, 512).transpose(1, 0, 2).reshape(N_CHIPS, CONV_ROWS // N_CHIPS, PACK_COLS)
    g_b = jnp.concatenate(
        [d_w_out.reshape(N_CHIPS, -1, PACK_COLS), d_w_branch.reshape(N_CHIPS, -1, PACK_COLS),
         d_glu_w.reshape(N_CHIPS, GLU_ROWS // N_CHIPS, PACK_COLS),
         jnp.pad(conv4, ((0, 0), (0, ROWS_B_TOTAL - ROW_B_CONV - CONV_ROWS // N_CHIPS), (0, 0)))], axis=1)
    g_in = jnp.stack([jnp.concatenate(_column_range(w_in_pieces, W_IN_SHARD * s, W_IN_SHARD * (s + 1)), axis=1)
                      for s in range(N_CHIPS)])
    return g_b, g_in


def _column_range(pieces, lo, hi):
    out = []
    for c0, a in pieces:
        a0, a1 = max(lo, c0), min(hi, c0 + a.shape[-1])
        if a0 < a1:
            out.append(a[..., a0 - c0:a1 - c0])
    return out


def kernel(x, norm_mix_g, w_in, conv_w, conv_b, dt_bias, a_log, d_ssd, ssd_norm_g, s5_a_re, s5_a_im, s5_log_dt, s5_b_re, s5_b_im, s5_c_re, s5_c_im, s5_d, s5_glu_w, s5_glu_b, w_branch, w_out, norm_mlp_g, w_mlp_in, w_mlp_out, norm_final_g, loss_target, m_norm_mix_g, m_w_in, m_conv_w, m_conv_b, m_dt_bias, m_a_log, m_d_ssd, m_ssd_norm_g, m_s5_a_re, m_s5_a_im, m_s5_log_dt, m_s5_b_re, m_s5_b_im, m_s5_c_re, m_s5_c_im, m_s5_d, m_s5_glu_w, m_s5_glu_b, m_w_branch, m_w_out, m_norm_mlp_g, m_w_mlp_in, m_w_mlp_out, m_norm_final_g, v_norm_mix_g, v_w_in, v_conv_w, v_conv_b, v_dt_bias, v_a_log, v_d_ssd, v_ssd_norm_g, v_s5_a_re, v_s5_a_im, v_s5_log_dt, v_s5_b_re, v_s5_b_im, v_s5_c_re, v_s5_c_im, v_s5_d, v_s5_glu_w, v_s5_glu_b, v_w_branch, v_w_out, v_norm_mlp_g, v_w_mlp_in, v_w_mlp_out, v_norm_final_g):
    names = ("norm_mix_g", "w_in", "conv_w", "conv_b", "dt_bias", "a_log", "d_ssd", "ssd_norm_g", "s5_a_re", "s5_a_im",
             "s5_log_dt", "s5_b_re", "s5_b_im", "s5_c_re", "s5_c_im", "s5_d", "s5_glu_w", "s5_glu_b", "w_branch", "w_out",
             "norm_mlp_g", "w_mlp_in", "w_mlp_out", "norm_final_g")
    w = dict(zip(names, (norm_mix_g, w_in, conv_w, conv_b, dt_bias, a_log, d_ssd, ssd_norm_g, s5_a_re, s5_a_im, s5_log_dt,
                         s5_b_re, s5_b_im, s5_c_re, s5_c_im, s5_d, s5_glu_w, s5_glu_b, w_branch, w_out, norm_mlp_g,
                         w_mlp_in, w_mlp_out, norm_final_g)))
    m = dict(zip(names, (m_norm_mix_g, m_w_in, m_conv_w, m_conv_b, m_dt_bias, m_a_log, m_d_ssd, m_ssd_norm_g, m_s5_a_re,
                         m_s5_a_im, m_s5_log_dt, m_s5_b_re, m_s5_b_im, m_s5_c_re, m_s5_c_im, m_s5_d, m_s5_glu_w,
                         m_s5_glu_b, m_w_branch, m_w_out, m_norm_mlp_g, m_w_mlp_in, m_w_mlp_out, m_norm_final_g)))
    v = dict(zip(names, (v_norm_mix_g, v_w_in, v_conv_w, v_conv_b, v_dt_bias, v_a_log, v_d_ssd, v_ssd_norm_g, v_s5_a_re,
                         v_s5_a_im, v_s5_log_dt, v_s5_b_re, v_s5_b_im, v_s5_c_re, v_s5_c_im, v_s5_d, v_s5_glu_w,
                         v_s5_glu_b, v_w_branch, v_w_out, v_norm_mlp_g, v_w_mlp_in, v_w_mlp_out, v_norm_final_g)))

    cx, cy, cc = lax.axis_index("x"), lax.axis_index("y"), lax.axis_index("c")
    own = 2 * cx + cy
    place = jnp.stack([own, cc]).astype(jnp.int32)

    src_conv = jnp.concatenate([conv_w, jnp.zeros((CONV_PAD_ROWS - CONV_K, 512), F32)], axis=0)
    all_in, all_conv = _exchange(_Gather([w_in.astype(BF16), src_conv], [ICI_CHUNKS, 1]), "gather_first")
    p = {n: w[n] for n, _ in SMALL}
    p["conv_w"] = jnp.concatenate([all_conv[s, :CONV_K] for s in range(N_CHIPS)], axis=1)
    shards = [(W_IN_SHARD * s, all_in[s]) for s in range(N_CHIPS)]
    p["w_in_perm"] = jnp.concatenate(
        _column_range(shards, 0, OFF_DT) + _column_range(shards, OFF_U, D_IN_PROJ) + _column_range(shards, OFF_DT, OFF_U)
        + [jnp.zeros((D_MODEL, DT_PAD - 16), BF16)], axis=1)

    def late_unpack(gathered):
        all_a, all_glu = gathered
        return {"late_weights": all_a, "s5_glu_w": all_glu.reshape(S5_WIDTH, S5_WIDTH)}

    comm = dict(place=place, late_ks=[ICI_CHUNKS, 1], late_unpack=late_unpack,
                late_srcs=[jnp.concatenate([w[n].astype(BF16) for n, _, _ in ROWS_A], axis=0), s5_glu_w.astype(BF16)])
    loss_part, grad_x, g = _local_step(x[0], loss_target[0], p, comm)

    red_mlp, red_b, red_in = _half_exchange([g["mlp_total"], *g["late_totals"]])
    small_tot = _small_allreduce(_pack_small([g[n] for n, _ in SMALL] + [loss_part.reshape(1)]))
    loss = small_tot.reshape(-1)[SMALL_COUNT]

    grads = _unpack_small(small_tot)
    delta, new_m, new_v = {}, {}, {}
    for n, r0, _ in ROWS_A[:2]:
        delta[n], new_m[n], new_v[n], grads[n] = _adamw(w[n], red_mlp, m[n], v[n], "adamw_" + n, g_row0=r0, with_grad=True)
    for n, r0, _ in ROWS_B:
        delta[n], new_m[n], new_v[n], grads[n] = _adamw(w[n], red_b, m[n], v[n], "adamw_" + n, g_row0=r0, with_grad=True)
    d_t, m_t, v_t, g_t = _adamw(w_in.T, red_in.T, m_w_in.T, v_w_in.T, "adamw_w_in", with_grad=True, col_block=128)
    delta["w_in"], new_m["w_in"], new_v["w_in"], grads["w_in"] = d_t.T, m_t.T, v_t.T, g_t.T
    grads["s5_glu_w"] = red_b[ROW_B_GLU:ROW_B_GLU + GLU_ROWS // N_CHIPS].reshape(S5_WIDTH // N_CHIPS, S5_WIDTH)
    grads["conv_w"] = red_b[ROW_B_CONV:ROW_B_CONV + CONV_ROWS // N_CHIPS].reshape(CONV_K, CONV_DIM // N_CHIPS)
    for n in ("s5_glu_w", "conv_w"):
        delta[n], new_m[n], new_v[n] = _adamw(w[n], grads[n], m[n], v[n], "adamw_" + n)
    ds, ms, vs = _adamw(_pack_small([w[n] for n, _ in SMALL]), small_tot, _pack_small([m[n] for n, _ in SMALL]),
                        _pack_small([v[n] for n, _ in SMALL]), "adamw_small")
    delta.update(_unpack_small(ds))
    new_m.update(_unpack_small(ms))
    new_v.update(_unpack_small(vs))

    return (loss, grad_x[None], *[grads[n] for n in names], *[delta[n] for n in names],
            *[new_m[n] for n in names], *[new_v[n] for n in names])
```
